```python
import jax, jax.numpy as jnp
from jax import lax
import numpy as np

D_MODEL = 1024
BATCH = 8
SEQ = 8192
DEPTH = 1

HEAD_DIM = 64
N_HEADS = D_MODEL // HEAD_DIM
N_HEADS_A = N_HEADS // 4
N_HEADS_B = N_HEADS - N_HEADS_A
WIDTH_A = N_HEADS_A * HEAD_DIM
WIDTH_B = N_HEADS_B * HEAD_DIM
CHUNK = 128
BLOCK = 128
DILATED_BRANCHES = ((128, 1), (512, 4), (2048, 16))
ROPE_THETA = 10000.0
D_FF = -(-8 * D_MODEL // (3 * 256)) * 256
PLE_DIM = 256
IN_COLS = 2 * WIDTH_A + 3 * WIDTH_B
EPS = 1e-6

kernel_name = "hybrid_sgu_dilated_attn_block"


def rmsnorm(x, g):
    xf = x.astype(jnp.float32)
    y = xf * lax.rsqrt(jnp.mean(xf * xf, axis=-1, keepdims=True) + EPS)
    return (y * g.astype(jnp.float32)).astype(x.dtype)


def rope(t, pos):
    half = t.shape[-1] // 2
    inv = ROPE_THETA ** (-jnp.arange(half, dtype=jnp.float32) / half)
    ang = pos[:, None] * inv[None, :]
    cos = jnp.cos(ang)[None, :, None, :]
    sin = jnp.sin(ang)[None, :, None, :]
    t = t.astype(jnp.float32)
    t1, t2 = t[..., :half], t[..., half:]
    return jnp.concatenate([t1 * cos - t2 * sin, t1 * sin + t2 * cos], axis=-1)


def chunked_sgu(u, v, w_s, b_s, norm_g):
    b, s, _ = u.shape
    u = jax.nn.gelu(u.astype(jnp.float32))
    vf = jax.nn.gelu(v.astype(jnp.float32))
    mu = jnp.mean(vf, axis=-1, keepdims=True)
    var = jnp.mean(jnp.square(vf - mu), axis=-1, keepdims=True)
    vf = (vf - mu) * lax.rsqrt(var + EPS) * norm_g.astype(jnp.float32)
    vf = vf.reshape(b, s // CHUNK, CHUNK, N_HEADS_A, HEAD_DIM)
    causal = jnp.tril(jnp.ones((CHUNK, CHUNK), jnp.float32))
    w = w_s.astype(jnp.float32) * causal[None]
    mixed = jnp.einsum('hij,bnjhd->bnihd', w, vf) + b_s.astype(jnp.float32).T[None, None, :, :, None]
    return u * mixed.reshape(b, s, WIDTH_A)


def dilated_branch(q, k, v, window, dilation):
    b, h, s, dh = q.shape
    n_back = window // dilation
    span = dilation * BLOCK
    s_pad = -(-s // span) * span
    sub_len = s_pad // dilation
    nb = sub_len // BLOCK

    def to_sub(t):
        t = jnp.pad(t, ((0, 0), (0, 0), (0, s_pad - s), (0, 0)))
        t = t.reshape(b, h, sub_len, dilation, dh)
        t = jnp.swapaxes(t, 2, 3)
        return t.reshape(b, h, dilation, nb, BLOCK, dh)

    qb, kb, vb = to_sub(q), to_sub(k), to_sub(v)
    shift = ((0, 0), (0, 0), (0, 0), (1, 0), (0, 0), (0, 0))
    kw = jnp.concatenate([jnp.pad(kb[:, :, :, :-1], shift), kb], axis=4)
    vw = jnp.concatenate([jnp.pad(vb[:, :, :, :-1], shift), vb], axis=4)
    scores = jnp.einsum('bhrnqd,bhrnkd->bhrnqk', qb, kw) * (dh ** -0.5)
    qi = jnp.arange(BLOCK)[:, None]
    kc = jnp.arange(2 * BLOCK)[None, :]
    dist = BLOCK + qi - kc
    band = (dist >= 0) & (dist <= n_back)
    blk = jnp.arange(nb)[:, None, None]
    valid = band[None] & ((blk > 0) | (kc[None] >= BLOCK))
    scores = jnp.where(valid, scores, -jnp.inf)
    m = jnp.max(scores, axis=-1, keepdims=True)
    pr = jnp.exp(scores - m)
    l = jnp.sum(pr, axis=-1, keepdims=True)
    o = jnp.einsum('bhrnqk,bhrnkd->bhrnqd', pr, vw) / l
    lse = (m + jnp.log(l))[..., 0]
    o = jnp.swapaxes(o.reshape(b, h, dilation, sub_len, dh), 2, 3).reshape(b, h, s_pad, dh)[:, :, :s]
    lse = jnp.swapaxes(lse.reshape(b, h, dilation, sub_len), 2, 3).reshape(b, h, s_pad)[:, :, :s]
    return o, lse


def dilated_mixture_attention(q, k, v):
    outs, lses = [], []
    for window, dilation in DILATED_BRANCHES:
        o, lse = dilated_branch(q, k, v, window, dilation)
        outs.append(o)
        lses.append(lse)
    o = jnp.stack(outs, axis=0)
    wts = jax.nn.softmax(jnp.stack(lses, axis=0), axis=0)
    return jnp.sum(wts[..., None] * o, axis=0)


def _fwd_setup_inputs(seed: int = 0) -> dict:
    key = jax.random.key(seed)
    ks = jax.random.split(key, 20)
    f32 = jnp.float32

    def nrm(k, shape, fan_in):
        return jax.random.normal(k, shape, f32) * (fan_in ** -0.5)

    def gain(k, shape):
        return 1.0 + 0.05 * jax.random.normal(k, shape, f32)

    return {
        "x": jax.random.normal(ks[0], (BATCH, SEQ, D_MODEL), f32),
        "p": jax.random.normal(ks[1], (DEPTH, BATCH, SEQ, PLE_DIM), f32),
        "mix_norm_g": gain(ks[2], (DEPTH, D_MODEL)),
        "w_in": nrm(ks[3], (DEPTH, D_MODEL, IN_COLS), D_MODEL),
        "sgu_w": nrm(ks[4], (DEPTH, N_HEADS_A, CHUNK, CHUNK), CHUNK),
        "sgu_b": 1.0 + 0.1 * jax.random.normal(ks[5], (DEPTH, N_HEADS_A, CHUNK), f32),
        "sgu_norm_g": gain(ks[6], (DEPTH, WIDTH_A)),
        "out_norm_a": gain(ks[7], (DEPTH, WIDTH_A)),
        "out_norm_b": gain(ks[8], (DEPTH, WIDTH_B)),
        "w_out": nrm(ks[9], (DEPTH, D_MODEL, D_MODEL), D_MODEL),
        "ffn_norm_g": gain(ks[10], (DEPTH, D_MODEL)),
        "w_gate": nrm(ks[11], (DEPTH, D_MODEL, D_FF), D_MODEL),
        "w_up": nrm(ks[12], (DEPTH, D_MODEL, D_FF), D_MODEL),
        "w_down": nrm(ks[13], (DEPTH, D_FF, D_MODEL), D_FF),
        "ple_norm_g": gain(ks[14], (DEPTH, D_MODEL)),
        "w_ple_gate": nrm(ks[15], (DEPTH, D_MODEL, D_MODEL), D_MODEL),
        "w_ple_proj": nrm(ks[16], (DEPTH, PLE_DIM, D_MODEL), PLE_DIM),
        "final_norm_g": gain(ks[17], (D_MODEL,)),
    }


def _fwd_reference(x, p, mix_norm_g, w_in, sgu_w, sgu_b, sgu_norm_g, out_norm_a, out_norm_b,
              w_out, ffn_norm_g, w_gate, w_up, w_down, ple_norm_g, w_ple_gate,
              w_ple_proj, final_norm_g):
    b, s, _ = x.shape
    pos = jnp.arange(s, dtype=jnp.float32)
    h = x
    for i in range(DEPTH):
        hn = rmsnorm(h, mix_norm_g[i])
        proj = hn @ w_in[i]
        u_a = proj[..., :WIDTH_A]
        v_a = proj[..., WIDTH_A:2 * WIDTH_A]
        qkv = proj[..., 2 * WIDTH_A:].reshape(b, s, 3, N_HEADS_B, HEAD_DIM)
        y_a = chunked_sgu(u_a, v_a, sgu_w[i], sgu_b[i], sgu_norm_g[i])
        q = jnp.transpose(rope(qkv[:, :, 0], pos), (0, 2, 1, 3))
        k = jnp.transpose(rope(qkv[:, :, 1], pos), (0, 2, 1, 3))
        v = jnp.transpose(qkv[:, :, 2].astype(jnp.float32), (0, 2, 1, 3))
        y_b = dilated_mixture_attention(q, k, v)
        y_b = jnp.transpose(y_b, (0, 2, 1, 3)).reshape(b, s, WIDTH_B)
        y = jnp.concatenate([rmsnorm(y_a, out_norm_a[i]), rmsnorm(y_b, out_norm_b[i])], axis=-1)
        h = h + (y.astype(h.dtype) @ w_out[i])
        hn = rmsnorm(h, ffn_norm_g[i])
        h = h + (jax.nn.silu(hn @ w_gate[i]) * (hn @ w_up[i])) @ w_down[i]
        gate = jax.nn.sigmoid(rmsnorm(h, ple_norm_g[i]) @ w_ple_gate[i])
        h = h + gate * (p[i] @ w_ple_proj[i])
    return rmsnorm(h, final_norm_g)


import jax as _jax
import jax.numpy as _jnp

TWIN_FORMAT = 'train_step'
FWD_PARAMS = ['x', 'p', 'mix_norm_g', 'w_in', 'sgu_w', 'sgu_b', 'sgu_norm_g', 'out_norm_a', 'out_norm_b', 'w_out', 'ffn_norm_g', 'w_gate', 'w_up', 'w_down', 'ple_norm_g', 'w_ple_gate', 'w_ple_proj', 'final_norm_g']
TWIN_WEIGHTS = ['mix_norm_g', 'w_in', 'sgu_w', 'sgu_b', 'sgu_norm_g', 'out_norm_a', 'out_norm_b', 'w_out', 'ffn_norm_g', 'w_gate', 'w_up', 'w_down', 'ple_norm_g', 'w_ple_gate', 'w_ple_proj', 'final_norm_g']
TWIN_DIFF_INPUT = 'x'
TWIN_INPUTS = ['x', 'p', 'mix_norm_g', 'w_in', 'sgu_w', 'sgu_b', 'sgu_norm_g', 'out_norm_a', 'out_norm_b', 'w_out', 'ffn_norm_g', 'w_gate', 'w_up', 'w_down', 'ple_norm_g', 'w_ple_gate', 'w_ple_proj', 'final_norm_g', 'loss_target', 'm_mix_norm_g', 'm_w_in', 'm_sgu_w', 'm_sgu_b', 'm_sgu_norm_g', 'm_out_norm_a', 'm_out_norm_b', 'm_w_out', 'm_ffn_norm_g', 'm_w_gate', 'm_w_up', 'm_w_down', 'm_ple_norm_g', 'm_w_ple_gate', 'm_w_ple_proj', 'm_final_norm_g', 'v_mix_norm_g', 'v_w_in', 'v_sgu_w', 'v_sgu_b', 'v_sgu_norm_g', 'v_out_norm_a', 'v_out_norm_b', 'v_w_out', 'v_ffn_norm_g', 'v_w_gate', 'v_w_up', 'v_w_down', 'v_ple_norm_g', 'v_w_ple_gate', 'v_w_ple_proj', 'v_final_norm_g']
TWIN_OUTPUTS = ['loss', 'grad_x', 'grad_mix_norm_g', 'grad_w_in', 'grad_sgu_w', 'grad_sgu_b', 'grad_sgu_norm_g', 'grad_out_norm_a', 'grad_out_norm_b', 'grad_w_out', 'grad_ffn_norm_g', 'grad_w_gate', 'grad_w_up', 'grad_w_down', 'grad_ple_norm_g', 'grad_w_ple_gate', 'grad_w_ple_proj', 'grad_final_norm_g', 'delta_mix_norm_g', 'delta_w_in', 'delta_sgu_w', 'delta_sgu_b', 'delta_sgu_norm_g', 'delta_out_norm_a', 'delta_out_norm_b', 'delta_w_out', 'delta_ffn_norm_g', 'delta_w_gate', 'delta_w_up', 'delta_w_down', 'delta_ple_norm_g', 'delta_w_ple_gate', 'delta_w_ple_proj', 'delta_final_norm_g', 'new_m_mix_norm_g', 'new_m_w_in', 'new_m_sgu_w', 'new_m_sgu_b', 'new_m_sgu_norm_g', 'new_m_out_norm_a', 'new_m_out_norm_b', 'new_m_w_out', 'new_m_ffn_norm_g', 'new_m_w_gate', 'new_m_w_up', 'new_m_w_down', 'new_m_ple_norm_g', 'new_m_w_ple_gate', 'new_m_w_ple_proj', 'new_m_final_norm_g', 'new_v_mix_norm_g', 'new_v_w_in', 'new_v_sgu_w', 'new_v_sgu_b', 'new_v_sgu_norm_g', 'new_v_out_norm_a', 'new_v_out_norm_b', 'new_v_w_out', 'new_v_ffn_norm_g', 'new_v_w_gate', 'new_v_w_up', 'new_v_w_down', 'new_v_ple_norm_g', 'new_v_w_ple_gate', 'new_v_w_ple_proj', 'new_v_final_norm_g']
TWIN_LEAF_KINDS = {'loss': 'loss', 'grad_x': 'grad_x', 'grad_mix_norm_g': 'grad_w', 'grad_w_in': 'grad_w', 'grad_sgu_w': 'grad_w', 'grad_sgu_b': 'grad_w', 'grad_sgu_norm_g': 'grad_w', 'grad_out_norm_a': 'grad_w', 'grad_out_norm_b': 'grad_w', 'grad_w_out': 'grad_w', 'grad_ffn_norm_g': 'grad_w', 'grad_w_gate': 'grad_w', 'grad_w_up': 'grad_w', 'grad_w_down': 'grad_w', 'grad_ple_norm_g': 'grad_w', 'grad_w_ple_gate': 'grad_w', 'grad_w_ple_proj': 'grad_w', 'grad_final_norm_g': 'grad_w', 'delta_mix_norm_g': 'delta_w', 'delta_w_in': 'delta_w', 'delta_sgu_w': 'delta_w', 'delta_sgu_b': 'delta_w', 'delta_sgu_norm_g': 'delta_w', 'delta_out_norm_a': 'delta_w', 'delta_out_norm_b': 'delta_w', 'delta_w_out': 'delta_w', 'delta_ffn_norm_g': 'delta_w', 'delta_w_gate': 'delta_w', 'delta_w_up': 'delta_w', 'delta_w_down': 'delta_w', 'delta_ple_norm_g': 'delta_w', 'delta_w_ple_gate': 'delta_w', 'delta_w_ple_proj': 'delta_w', 'delta_final_norm_g': 'delta_w', 'new_m_mix_norm_g': 'new_m', 'new_m_w_in': 'new_m', 'new_m_sgu_w': 'new_m', 'new_m_sgu_b': 'new_m', 'new_m_sgu_norm_g': 'new_m', 'new_m_out_norm_a': 'new_m', 'new_m_out_norm_b': 'new_m', 'new_m_w_out': 'new_m', 'new_m_ffn_norm_g': 'new_m', 'new_m_w_gate': 'new_m', 'new_m_w_up': 'new_m', 'new_m_w_down': 'new_m', 'new_m_ple_norm_g': 'new_m', 'new_m_w_ple_gate': 'new_m', 'new_m_w_ple_proj': 'new_m', 'new_m_final_norm_g': 'new_m', 'new_v_mix_norm_g': 'new_v', 'new_v_w_in': 'new_v', 'new_v_sgu_w': 'new_v', 'new_v_sgu_b': 'new_v', 'new_v_sgu_norm_g': 'new_v', 'new_v_out_norm_a': 'new_v', 'new_v_out_norm_b': 'new_v', 'new_v_w_out': 'new_v', 'new_v_ffn_norm_g': 'new_v', 'new_v_w_gate': 'new_v', 'new_v_w_up': 'new_v', 'new_v_w_down': 'new_v', 'new_v_ple_norm_g': 'new_v', 'new_v_w_ple_gate': 'new_v', 'new_v_w_ple_proj': 'new_v', 'new_v_final_norm_g': 'new_v'}


def _forward(args):
    return _fwd_reference(*[args[k] for k in FWD_PARAMS])


def _output_shape():
    def fwd():
        inp = _fwd_setup_inputs(0)
        return _fwd_reference(*[inp[k] for k in FWD_PARAMS])
    out = _jax.eval_shape(fwd)
    return out.shape, out.dtype

N_MICROBATCH = 1
ADAM_LR = 0.001
ADAM_B1 = 0.9
ADAM_B2 = 0.999
ADAM_EPS = 1e-08
ADAM_WD = 0.01
ADAM_STEP = 10
PER_EXAMPLE_BATCH_AXIS = {'x': 0, 'p': 1, 'loss_target': 0}
SHARED_INPUTS = []
_WEIGHT_DTYPES = {'mix_norm_g': _jnp.float32, 'w_in': _jnp.float32, 'sgu_w': _jnp.float32, 'sgu_b': _jnp.float32, 'sgu_norm_g': _jnp.float32, 'out_norm_a': _jnp.float32, 'out_norm_b': _jnp.float32, 'w_out': _jnp.float32, 'ffn_norm_g': _jnp.float32, 'w_gate': _jnp.float32, 'w_up': _jnp.float32, 'w_down': _jnp.float32, 'ple_norm_g': _jnp.float32, 'w_ple_gate': _jnp.float32, 'w_ple_proj': _jnp.float32, 'final_norm_g': _jnp.float32}
MOMENT_SCALE = {'mix_norm_g': 2.964415e-01, 'w_in': 1.816387e-01, 'sgu_w': 7.225616e-02, 'sgu_b': 9.549952e-02, 'sgu_norm_g': 1.023157e-01, 'out_norm_a': 1.800932e-01, 'out_norm_b': 1.999476e-01, 'w_out': 1.966042e-01, 'ffn_norm_g': 1.406700e-01, 'w_gate': 5.862776e-02, 'w_up': 5.774267e-02, 'w_down': 9.613099e-02, 'ple_norm_g': 3.807559e-02, 'w_ple_gate': 3.665349e-02, 'w_ple_proj': 8.678875e-02, 'final_norm_g': 6.417889e+01}


def _to_microbatches(a, axis):
    t = _jnp.moveaxis(a, axis, 0)
    t = t.reshape((N_MICROBATCH, t.shape[0] // N_MICROBATCH) + t.shape[1:])
    return _jnp.moveaxis(t, 1, axis + 1)


def setup_inputs(seed: int = 0) -> dict:
    inp = _fwd_setup_inputs(seed)
    key = _jax.random.fold_in(_jax.random.key(seed), 7919)
    shape, _ = _output_shape()
    out = dict(inp)
    out["loss_target"] = _jax.random.normal(_jax.random.fold_in(key, 0), shape, _jnp.float32)
    for i, name in enumerate(TWIN_WEIGHTS):
        w = inp[name].astype(_jnp.float32)
        if MOMENT_SCALE is None:
            s = _jnp.sqrt(_jnp.mean(_jnp.square(w)) + 1e-30)
        else:
            s = MOMENT_SCALE[name]
        km, kv = _jax.random.split(_jax.random.fold_in(key, i + 1))
        out[name] = w
        out["m_" + name] = s * _jax.random.normal(km, w.shape, _jnp.float32)
        out["v_" + name] = (s * s) * _jax.random.uniform(kv, w.shape, _jnp.float32, 0.5, 1.5)
    if N_MICROBATCH > 1:
        for name, axis in PER_EXAMPLE_BATCH_AXIS.items():
            out[name] = _to_microbatches(out[name], axis)
    return {'x': out['x'], 'p': out['p'], 'mix_norm_g': out['mix_norm_g'], 'w_in': out['w_in'], 'sgu_w': out['sgu_w'], 'sgu_b': out['sgu_b'], 'sgu_norm_g': out['sgu_norm_g'], 'out_norm_a': out['out_norm_a'], 'out_norm_b': out['out_norm_b'], 'w_out': out['w_out'], 'ffn_norm_g': out['ffn_norm_g'], 'w_gate': out['w_gate'], 'w_up': out['w_up'], 'w_down': out['w_down'], 'ple_norm_g': out['ple_norm_g'], 'w_ple_gate': out['w_ple_gate'], 'w_ple_proj': out['w_ple_proj'], 'final_norm_g': out['final_norm_g'], 'loss_target': out['loss_target'], 'm_mix_norm_g': out['m_mix_norm_g'], 'm_w_in': out['m_w_in'], 'm_sgu_w': out['m_sgu_w'], 'm_sgu_b': out['m_sgu_b'], 'm_sgu_norm_g': out['m_sgu_norm_g'], 'm_out_norm_a': out['m_out_norm_a'], 'm_out_norm_b': out['m_out_norm_b'], 'm_w_out': out['m_w_out'], 'm_ffn_norm_g': out['m_ffn_norm_g'], 'm_w_gate': out['m_w_gate'], 'm_w_up': out['m_w_up'], 'm_w_down': out['m_w_down'], 'm_ple_norm_g': out['m_ple_norm_g'], 'm_w_ple_gate': out['m_w_ple_gate'], 'm_w_ple_proj': out['m_w_ple_proj'], 'm_final_norm_g': out['m_final_norm_g'], 'v_mix_norm_g': out['v_mix_norm_g'], 'v_w_in': out['v_w_in'], 'v_sgu_w': out['v_sgu_w'], 'v_sgu_b': out['v_sgu_b'], 'v_sgu_norm_g': out['v_sgu_norm_g'], 'v_out_norm_a': out['v_out_norm_a'], 'v_out_norm_b': out['v_out_norm_b'], 'v_w_out': out['v_w_out'], 'v_ffn_norm_g': out['v_ffn_norm_g'], 'v_w_gate': out['v_w_gate'], 'v_w_up': out['v_w_up'], 'v_w_down': out['v_w_down'], 'v_ple_norm_g': out['v_ple_norm_g'], 'v_w_ple_gate': out['v_w_ple_gate'], 'v_w_ple_proj': out['v_w_ple_proj'], 'v_final_norm_g': out['v_final_norm_g']}


def _loss(weights, diff, rest, loss_target):
    with _jax.named_scope("forward"):
        args = {**rest, TWIN_DIFF_INPUT: diff, **{k: w.astype(_WEIGHT_DTYPES[k]) for k, w in weights.items()}}
        y = _forward(args)
    with _jax.named_scope("loss_head"):
        err = _jnp.square(y.astype(_jnp.float32) - loss_target)
        return 0.5 * _jnp.sum(_jnp.mean(err, axis=-1)) if err.ndim else 0.5 * err


def _adamw(w, g, m, v):
    m = ADAM_B1 * m + (1.0 - ADAM_B1) * g
    v = ADAM_B2 * v + (1.0 - ADAM_B2) * _jnp.square(g)
    m_hat = m / (1.0 - ADAM_B1 ** ADAM_STEP)
    v_hat = v / (1.0 - ADAM_B2 ** ADAM_STEP)
    delta = -ADAM_LR * (m_hat / (_jnp.sqrt(v_hat) + ADAM_EPS) + ADAM_WD * w)
    return delta, m, v


def reference(x, p, mix_norm_g, w_in, sgu_w, sgu_b, sgu_norm_g, out_norm_a, out_norm_b, w_out, ffn_norm_g, w_gate, w_up, w_down, ple_norm_g, w_ple_gate, w_ple_proj, final_norm_g, loss_target, m_mix_norm_g, m_w_in, m_sgu_w, m_sgu_b, m_sgu_norm_g, m_out_norm_a, m_out_norm_b, m_w_out, m_ffn_norm_g, m_w_gate, m_w_up, m_w_down, m_ple_norm_g, m_w_ple_gate, m_w_ple_proj, m_final_norm_g, v_mix_norm_g, v_w_in, v_sgu_w, v_sgu_b, v_sgu_norm_g, v_out_norm_a, v_out_norm_b, v_w_out, v_ffn_norm_g, v_w_gate, v_w_up, v_w_down, v_ple_norm_g, v_w_ple_gate, v_w_ple_proj, v_final_norm_g):
    given = dict(x=x, p=p, mix_norm_g=mix_norm_g, w_in=w_in, sgu_w=sgu_w, sgu_b=sgu_b, sgu_norm_g=sgu_norm_g, out_norm_a=out_norm_a, out_norm_b=out_norm_b, w_out=w_out, ffn_norm_g=ffn_norm_g, w_gate=w_gate, w_up=w_up, w_down=w_down, ple_norm_g=ple_norm_g, w_ple_gate=w_ple_gate, w_ple_proj=w_ple_proj, final_norm_g=final_norm_g, loss_target=loss_target, m_mix_norm_g=m_mix_norm_g, m_w_in=m_w_in, m_sgu_w=m_sgu_w, m_sgu_b=m_sgu_b, m_sgu_norm_g=m_sgu_norm_g, m_out_norm_a=m_out_norm_a, m_out_norm_b=m_out_norm_b, m_w_out=m_w_out, m_ffn_norm_g=m_ffn_norm_g, m_w_gate=m_w_gate, m_w_up=m_w_up, m_w_down=m_w_down, m_ple_norm_g=m_ple_norm_g, m_w_ple_gate=m_w_ple_gate, m_w_ple_proj=m_w_ple_proj, m_final_norm_g=m_final_norm_g, v_mix_norm_g=v_mix_norm_g, v_w_in=v_w_in, v_sgu_w=v_sgu_w, v_sgu_b=v_sgu_b, v_sgu_norm_g=v_sgu_norm_g, v_out_norm_a=v_out_norm_a, v_out_norm_b=v_out_norm_b, v_w_out=v_w_out, v_ffn_norm_g=v_ffn_norm_g, v_w_gate=v_w_gate, v_w_up=v_w_up, v_w_down=v_w_down, v_ple_norm_g=v_ple_norm_g, v_w_ple_gate=v_w_ple_gate, v_w_ple_proj=v_w_ple_proj, v_final_norm_g=v_final_norm_g)
    weights = {n: given[n] for n in TWIN_WEIGHTS}
    shared = {n: given[n] for n in SHARED_INPUTS}
    per_example = {n: given[n] for n in ['x', 'p']}
    grad_fn = _jax.value_and_grad(_loss, argnums=(0, 1))

    def one_microbatch(ex, loss_target):
        ex = dict(ex)
        diff = ex.pop(TWIN_DIFF_INPUT)
        return grad_fn(weights, diff, {**shared, **ex}, loss_target)

    if N_MICROBATCH == 1:
        loss, (grad_w, grad_x) = one_microbatch(per_example, given["loss_target"])
    else:
        def body(carry, xs):
            loss_sum, grad_sum = carry
            l_k, (gw_k, gx_k) = one_microbatch(xs[0], xs[1])
            with _jax.named_scope("update"):
                return (loss_sum + l_k, _jax.tree.map(_jnp.add, grad_sum, gw_k)), gx_k

        init = (_jnp.zeros((), _jnp.float32), _jax.tree.map(_jnp.zeros_like, weights))
        (loss, grad_w), grad_x = _jax.lax.scan(body, init, (per_example, given["loss_target"]))
    with _jax.named_scope("update"):
        delta_w, new_m, new_v = {}, {}, {}
        for n in TWIN_WEIGHTS:
            delta_w[n], new_m[n], new_v[n] = _adamw(weights[n], grad_w[n], given["m_" + n], given["v_" + n])
    return (loss, grad_x, *[grad_w[n] for n in TWIN_WEIGHTS], *[delta_w[n] for n in TWIN_WEIGHTS],
            *[new_m[n] for n in TWIN_WEIGHTS], *[new_v[n] for n in TWIN_WEIGHTS])
```

```python
import functools
import math

import jax
import jax.numpy as jnp
from jax import lax
from jax.experimental import pallas as pl
from jax.experimental.pallas import tpu as pltpu

F32 = jnp.float32
MXU_DTYPE = jnp.bfloat16

D_MODEL = 1024
HEAD_DIM = 64
HEADS_A = 4
HEADS_B = 12
WIDTH_A = HEADS_A * HEAD_DIM
WIDTH_B = HEADS_B * HEAD_DIM
CHUNK = 128
BLOCK = 128
DILATIONS = (1, 4, 16)
ROPE_THETA = 10000.0
D_FF = 2816
FF_HALF = D_FF // 2
PLE_DIM = 256
IN_COLS = 2 * WIDTH_A + 3 * WIDTH_B
EPS = 1e-6
LANES = 128
N_CHIPS = 4
N_DEV = 8

ADAM_LR = 0.001
ADAM_B1 = 0.9
ADAM_B2 = 0.999
ADAM_EPS = 1e-08
ADAM_WD = 0.01
ADAM_STEP = 10

VMEM_LIMIT = 56 * 1024 * 1024

WEIGHT_NAMES = ['mix_norm_g', 'w_in', 'sgu_w', 'sgu_b', 'sgu_norm_g', 'out_norm_a', 'out_norm_b', 'w_out',
                'ffn_norm_g', 'w_gate', 'w_up', 'w_down', 'ple_norm_g', 'w_ple_gate', 'w_ple_proj', 'final_norm_g']
SHARDED = [('w_in', 1), ('w_out', 0), ('w_gate', 1), ('w_up', 1), ('w_down', 0), ('w_ple_gate', 0), ('w_ple_proj', 1)]
FULL_SHAPES = {'w_in': (D_MODEL, IN_COLS), 'w_out': (D_MODEL, D_MODEL), 'w_gate': (D_MODEL, D_FF),
               'w_up': (D_MODEL, D_FF), 'w_down': (D_FF, D_MODEL), 'w_ple_gate': (D_MODEL, D_MODEL),
               'w_ple_proj': (PLE_DIM, D_MODEL)}
SMALL = ['mix_norm_g', 'sgu_w', 'sgu_b', 'sgu_norm_g', 'out_norm_a', 'out_norm_b', 'ffn_norm_g', 'ple_norm_g',
         'final_norm_g']
SMALL_SIZES = {'mix_norm_g': 1024, 'sgu_w': 65536, 'sgu_b': 512, 'sgu_norm_g': 256, 'out_norm_a': 256,
               'out_norm_b': 768, 'ffn_norm_g': 1024, 'ple_norm_g': 1024, 'final_norm_g': 1024}
SMALL_ROWS = 72


def _params(semantics=None):
    return pltpu.CompilerParams(dimension_semantics=semantics, vmem_limit_bytes=VMEM_LIMIT)


def _full(shape):
    nd = len(shape)
    return pl.BlockSpec(shape, lambda i: (0,) * nd)


def _rows(tm, width):
    return pl.BlockSpec((tm, width), lambda i: (i, 0))


def _rms_stats(x):
    r = lax.rsqrt(jnp.mean(x * x, axis=-1, keepdims=True) + EPS)
    return x * r, r


def _rms_bwd(dn, n, r):
    return r * (dn - n * jnp.mean(dn * n, axis=-1, keepdims=True))


def _dot(a, b):
    return jnp.dot(a, b, preferred_element_type=F32)


def _dot_nt(a, b):
    return lax.dot_general(a, b, (((1,), (1,)), ((), ())), preferred_element_type=F32)


def _dot_tn(a, b):
    return lax.dot_general(a, b, (((0,), (0,)), ((), ())), preferred_element_type=F32)


def _gelu_parts(x):
    c = math.sqrt(2.0 / math.pi)
    t = jnp.tanh(c * (x + 0.044715 * x * x * x))
    return 0.5 * x * (1.0 + t), t


def _gelu_grad(x, t):
    c = math.sqrt(2.0 / math.pi)
    return 0.5 * (1.0 + t) + 0.5 * x * (1.0 - t * t) * c * (1.0 + 3.0 * 0.044715 * x * x)


def _half_masks(dtype):
    lane = lax.broadcasted_iota(jnp.int32, (BLOCK, LANES), 1)
    lo = (lane < HEAD_DIM).astype(F32)
    return lo.astype(dtype), (1.0 - lo).astype(dtype)


def _rope_partner(t):
    lane = lax.broadcasted_iota(jnp.int32, t.shape, 1)
    first_half = (lane % HEAD_DIM) < (HEAD_DIM // 2)
    return jnp.where(first_half, pltpu.roll(t, LANES - HEAD_DIM // 2, 1), pltpu.roll(t, HEAD_DIM // 2, 1))


def _in_fwd(x, g_mix, w_in, cos_t, sin_t):
    s = x.shape[0]
    tm = 512

    def body(x_ref, g_ref, w_ref, cos_ref, sin_ref, uv_ref, q_ref, k_ref, v_ref, hn_ref):
        n, _ = _rms_stats(x_ref[...])
        hn = (n * g_ref[...]).astype(MXU_DTYPE)
        hn_ref[...] = hn
        proj = _dot(hn, w_ref[...])
        uv_ref[...] = proj[:, :2 * WIDTH_A]
        cos = cos_ref[...]
        sin = sin_ref[...]
        for i in range(WIDTH_B // LANES):
            lo = 2 * WIDTH_A + i * LANES
            tq = proj[:, lo:lo + LANES]
            tk = proj[:, lo + WIDTH_B:lo + WIDTH_B + LANES]
            cols = slice(i * LANES, (i + 1) * LANES)
            q_ref[:, cols] = ((tq * cos + _rope_partner(tq) * sin) * (HEAD_DIM ** -0.5)).astype(MXU_DTYPE)
            k_ref[:, cols] = (tk * cos + _rope_partner(tk) * sin).astype(MXU_DTYPE)
        v_ref[...] = proj[:, 2 * WIDTH_A + 2 * WIDTH_B:].astype(MXU_DTYPE)

    return pl.pallas_call(
        body, name="in_fwd", grid=(s // tm,),
        in_specs=[_rows(tm, D_MODEL), _full((1, D_MODEL)), _full((D_MODEL, IN_COLS)), _rows(tm, LANES),
                  _rows(tm, LANES)],
        out_specs=[_rows(tm, 2 * WIDTH_A), _rows(tm, WIDTH_B), _rows(tm, WIDTH_B), _rows(tm, WIDTH_B),
                   _rows(tm, D_MODEL)],
        out_shape=[jax.ShapeDtypeStruct((s, 2 * WIDTH_A), F32), jax.ShapeDtypeStruct((s, WIDTH_B), MXU_DTYPE),
                   jax.ShapeDtypeStruct((s, WIDTH_B), MXU_DTYPE), jax.ShapeDtypeStruct((s, WIDTH_B), MXU_DTYPE),
                   jax.ShapeDtypeStruct((s, D_MODEL), MXU_DTYPE)],
        compiler_params=_params(("arbitrary",)),
    )(x, g_mix, w_in, cos_t, sin_t)


def _band_mask(n):
    qi = lax.broadcasted_iota(jnp.int32, (BLOCK, 2 * BLOCK), 0)
    kc = lax.broadcasted_iota(jnp.int32, (BLOCK, 2 * BLOCK), 1)
    dist = BLOCK + qi - kc
    return (dist >= 0) & (dist <= BLOCK) & ((n > 0) | (kc >= BLOCK))


def _attn_fwd_branch(q, k, v, state, dil, last):
    s = q.shape[0]
    rows = s // dil
    nb = rows // BLOCK
    first = state is None
    view = lambda a, w: a.reshape(rows, dil * w)

    def body(*refs):
        q_ref, kc_ref, kp_ref, vc_ref, vp_ref = refs[:5]
        if first:
            outs = refs[5:]
        else:
            acc_ref, m_ref, l_ref = refs[5:8]
            outs = refs[8:]
        o_ref, m_out = outs[0], outs[1]
        valid = _band_mask(pl.program_id(1))
        lane = lax.broadcasted_iota(jnp.int32, (BLOCK, LANES), 1)
        lo = lane < HEAD_DIM
        masks = _half_masks(MXU_DTYPE)
        m_blk = jnp.zeros((BLOCK, LANES), F32)
        l_blk = jnp.zeros((BLOCK, LANES), F32)
        for hp in range(HEADS_B // 2):
            cols = slice(hp * LANES, (hp + 1) * LANES)
            qp = q_ref[:, cols]
            kcat = jnp.concatenate([kp_ref[:, cols], kc_ref[:, cols]], axis=0)
            vcat = jnp.concatenate([vp_ref[:, cols], vc_ref[:, cols]], axis=0)
            pair = []
            for j in range(2):
                h = 2 * hp + j
                sc = jnp.where(valid, _dot_nt(qp * masks[j], kcat), -jnp.inf)
                m_new = jnp.max(sc, axis=1, keepdims=True)
                if not first:
                    m_prev = m_ref[:, h:h + 1]
                    m_new = jnp.maximum(m_prev, m_new)
                p = jnp.exp(sc - m_new)
                l_new = jnp.sum(p, axis=1, keepdims=True)
                acc = _dot(p.astype(MXU_DTYPE), vcat)
                if not first:
                    alpha = jnp.exp(m_prev - m_new)
                    l_new = alpha * l_ref[:, h:h + 1] + l_new
                    acc = alpha * acc_ref[:, cols] + acc
                if last:
                    acc = acc / l_new
                    m_blk = jnp.where(lane == h, m_new + jnp.log(l_new), m_blk)
                else:
                    m_blk = jnp.where(lane == h, m_new, m_blk)
                    l_blk = jnp.where(lane == h, l_new, l_blk)
                pair.append(acc)
            o_ref[:, cols] = jnp.where(lo, pair[0], pair[1])
        m_out[...] = m_blk
        if not last:
            outs[2][...] = l_blk

    cur = lambda w: pl.BlockSpec((BLOCK, w), lambda r, n: (n, r))
    prev = lambda w: pl.BlockSpec((BLOCK, w), lambda r, n: (jnp.maximum(n - 1, 0), r))
    in_specs = [cur(WIDTH_B), cur(WIDTH_B), prev(WIDTH_B), cur(WIDTH_B), prev(WIDTH_B)]
    args = [view(q, WIDTH_B), view(k, WIDTH_B), view(k, WIDTH_B), view(v, WIDTH_B), view(v, WIDTH_B)]
    if not first:
        in_specs += [cur(WIDTH_B), cur(LANES), cur(LANES)]
        args += [view(state[0], WIDTH_B), view(state[1], LANES), view(state[2], LANES)]
    out_specs = [cur(WIDTH_B), cur(LANES)]
    out_shape = [jax.ShapeDtypeStruct((rows, dil * WIDTH_B), F32), jax.ShapeDtypeStruct((rows, dil * LANES), F32)]
    if not last:
        out_specs.append(cur(LANES))
        out_shape.append(jax.ShapeDtypeStruct((rows, dil * LANES), F32))
    res = pl.pallas_call(
        body, name="attn_fwd_d%d" % dil, grid=(dil, nb), in_specs=in_specs, out_specs=out_specs,
        out_shape=out_shape, compiler_params=_params(("arbitrary", "arbitrary")),
    )(*args)
    widths = (WIDTH_B, LANES, LANES)
    return tuple(a.reshape(s, w) for a, w in zip(res, widths))


def _sgu_forward_tile(uv, w_ref, bias, g_sgu):
    tm = uv.shape[0]
    u = uv[:, :WIDTH_A]
    v = uv[:, WIDTH_A:]
    ug, tu = _gelu_parts(u)
    vg, tv = _gelu_parts(v)
    mu = jnp.mean(vg, axis=-1, keepdims=True)
    vc = vg - mu
    rs = lax.rsqrt(jnp.mean(vc * vc, axis=-1, keepdims=True) + EPS)
    vhat = vc * rs
    vn = (vhat * g_sgu).astype(MXU_DTYPE)
    masks = _half_masks(MXU_DTYPE)
    chunks = []
    for c in range(tm // CHUNK):
        rows = slice(c * CHUNK, (c + 1) * CHUNK)
        groups = []
        for gp in range(2):
            vn_g = vn[rows, gp * LANES:(gp + 1) * LANES]
            groups.append(_dot(w_ref[2 * gp], vn_g * masks[0]) + _dot(w_ref[2 * gp + 1], vn_g * masks[1]))
        chunks.append(jnp.concatenate(groups, axis=1) + bias)
    mixed = jnp.concatenate(chunks, axis=0)
    return dict(u=u, v=v, ug=ug, tu=tu, tv=tv, rs=rs, vhat=vhat, vn=vn, mixed=mixed, ya=ug * mixed)


def _sgu_fwd(uv, w_tril, bias, g_sgu, g_a):
    s = uv.shape[0]
    tm = 512

    def body(uv_ref, w_ref, b_ref, gs_ref, ga_ref, o_ref):
        t = _sgu_forward_tile(uv_ref[...], w_ref, b_ref[...], gs_ref[...])
        n, _ = _rms_stats(t['ya'])
        o_ref[...] = (n * ga_ref[...]).astype(MXU_DTYPE)

    return pl.pallas_call(
        body, name="sgu_fwd", grid=(s // tm,),
        in_specs=[_rows(tm, 2 * WIDTH_A), _full((HEADS_A, CHUNK, CHUNK)), _full((CHUNK, WIDTH_A)),
                  _full((1, WIDTH_A)), _full((1, WIDTH_A))],
        out_specs=_rows(tm, WIDTH_A), out_shape=jax.ShapeDtypeStruct((s, WIDTH_A), MXU_DTYPE),
        compiler_params=_params(("arbitrary",)),
    )(uv, w_tril, bias, g_sgu, g_a)


def _out_fwd(ya_n, y_b, g_b, w_out, x):
    s = x.shape[0]
    tm = 512

    def body(ya_ref, yb_ref, g_ref, w_ref, x_ref, h_ref, ybn_ref):
        n, _ = _rms_stats(yb_ref[...])
        ybn = (n * g_ref[...]).astype(MXU_DTYPE)
        ybn_ref[...] = ybn
        h_ref[...] = x_ref[...] + _dot(ya_ref[...], w_ref[:WIDTH_A, :]) + _dot(ybn, w_ref[WIDTH_A:, :])

    return pl.pallas_call(
        body, name="out_fwd", grid=(s // tm,),
        in_specs=[_rows(tm, WIDTH_A), _rows(tm, WIDTH_B), _full((1, WIDTH_B)), _full((D_MODEL, D_MODEL)),
                  _rows(tm, D_MODEL)],
        out_specs=[_rows(tm, D_MODEL), _rows(tm, WIDTH_B)],
        out_shape=[jax.ShapeDtypeStruct((s, D_MODEL), F32), jax.ShapeDtypeStruct((s, WIDTH_B), MXU_DTYPE)],
        compiler_params=_params(("arbitrary",)),
    )(ya_n, y_b, g_b, w_out, x)


def _ffn_fwd(h1, g_ffn, w_gate, w_up, w_down):
    s = h1.shape[0]
    tm = 256

    def body(h_ref, g_ref, wg_ref, wu_ref, wd_ref, o_ref, gate_ref, up_ref, hn_ref):
        h = h_ref[...]
        n, _ = _rms_stats(h)
        hn = (n * g_ref[...]).astype(MXU_DTYPE)
        hn_ref[...] = hn
        out = h
        for c in range(2):
            cols = slice(c * FF_HALF, (c + 1) * FF_HALF)
            gate = _dot(hn, wg_ref[:, cols])
            up = _dot(hn, wu_ref[:, cols])
            gate_ref[:, cols] = gate.astype(MXU_DTYPE)
            up_ref[:, cols] = up.astype(MXU_DTYPE)
            act = (gate * jax.nn.sigmoid(gate) * up).astype(MXU_DTYPE)
            out = out + _dot(act, wd_ref[cols, :])
        o_ref[...] = out

    return pl.pallas_call(
        body, name="ffn_fwd", grid=(s // tm,),
        in_specs=[_rows(tm, D_MODEL), _full((1, D_MODEL)), _full((D_MODEL, D_FF)), _full((D_MODEL, D_FF)),
                  _full((D_FF, D_MODEL))],
        out_specs=[_rows(tm, D_MODEL), _rows(tm, D_FF), _rows(tm, D_FF), _rows(tm, D_MODEL)],
        out_shape=[jax.ShapeDtypeStruct((s, D_MODEL), F32), jax.ShapeDtypeStruct((s, D_FF), MXU_DTYPE),
                   jax.ShapeDtypeStruct((s, D_FF), MXU_DTYPE), jax.ShapeDtypeStruct((s, D_MODEL), MXU_DTYPE)],
        compiler_params=_params(("arbitrary",)),
    )(h1, g_ffn, w_gate, w_up, w_down)


def _ple_loss(h2, p, target, g_ple, w_pg, w_pg_t, w_pp, g_final):
    s = h2.shape[0]
    tm = 256

    def body(h_ref, p_ref, t_ref, gp_ref, wg_ref, wgt_ref, wp_ref, gf_ref,
             loss_ref, dh_ref, dz_ref, dpp_ref, hn_ref, dgp_ref, dgf_ref):
        @pl.when(pl.program_id(0) == 0)
        def _():
            loss_ref[...] = jnp.zeros_like(loss_ref)
            dgp_ref[...] = jnp.zeros_like(dgp_ref)
            dgf_ref[...] = jnp.zeros_like(dgf_ref)

        h2t = h_ref[...]
        n2, r2 = _rms_stats(h2t)
        hn = (n2 * gp_ref[...]).astype(MXU_DTYPE)
        hn_ref[...] = hn
        gate = jax.nn.sigmoid(_dot(hn, wg_ref[...]))
        pp = _dot(p_ref[...].astype(MXU_DTYPE), wp_ref[...])
        h3 = h2t + gate * pp
        n3, r3 = _rms_stats(h3)
        diff = n3 * gf_ref[...] - t_ref[...]
        loss_ref[...] += jnp.full(loss_ref.shape, 0.5 * jnp.sum(diff * diff) / D_MODEL, F32)
        dy = diff * (1.0 / D_MODEL)
        dgf_ref[...] += jnp.sum(dy * n3, axis=0, keepdims=True)
        dh3 = _rms_bwd(dy * gf_ref[...], n3, r3)
        dpp_ref[...] = (dh3 * gate).astype(MXU_DTYPE)
        dz = (dh3 * pp * gate * (1.0 - gate)).astype(MXU_DTYPE)
        dz_ref[...] = dz
        dhn = _dot(dz, wgt_ref[...])
        dgp_ref[...] += jnp.sum(dhn * n2, axis=0, keepdims=True)
        dh_ref[...] = dh3 + _rms_bwd(dhn * gp_ref[...], n2, r2)

    return pl.pallas_call(
        body, name="ple_loss", grid=(s // tm,),
        in_specs=[_rows(tm, D_MODEL), _rows(tm, PLE_DIM), _rows(tm, D_MODEL), _full((1, D_MODEL)),
                  _full((D_MODEL, D_MODEL)), _full((D_MODEL, D_MODEL)), _full((PLE_DIM, D_MODEL)),
                  _full((1, D_MODEL))],
        out_specs=[_full((1, LANES)), _rows(tm, D_MODEL), _rows(tm, D_MODEL), _rows(tm, D_MODEL),
                   _rows(tm, D_MODEL), _full((1, D_MODEL)), _full((1, D_MODEL))],
        out_shape=[jax.ShapeDtypeStruct((1, LANES), F32), jax.ShapeDtypeStruct((s, D_MODEL), F32),
                   jax.ShapeDtypeStruct((s, D_MODEL), MXU_DTYPE), jax.ShapeDtypeStruct((s, D_MODEL), MXU_DTYPE),
                   jax.ShapeDtypeStruct((s, D_MODEL), MXU_DTYPE), jax.ShapeDtypeStruct((1, D_MODEL), F32),
                   jax.ShapeDtypeStruct((1, D_MODEL), F32)],
        compiler_params=_params(("arbitrary",)),
    )(h2, p, target, g_ple, w_pg, w_pg_t, w_pp, g_final)


def _ffn_bwd(dh2, h1, gate, up, g_ffn, w_down_t, w_gate_t, w_up_t):
    s = h1.shape[0]
    tm = 256

    def body(dh_ref, h_ref, gate_ref, up_ref, g_ref, wdt_ref, wgt_ref, wut_ref,
             o_ref, act_ref, dg_ref, du_ref, dgn_ref):
        @pl.when(pl.program_id(0) == 0)
        def _():
            dgn_ref[...] = jnp.zeros_like(dgn_ref)

        dh = dh_ref[...]
        dhb = dh.astype(MXU_DTYPE)
        dhn = jnp.zeros((tm, D_MODEL), F32)
        for c in range(2):
            cols = slice(c * FF_HALF, (c + 1) * FF_HALF)
            dact = _dot(dhb, wdt_ref[:, cols])
            g = gate_ref[:, cols].astype(F32)
            u = up_ref[:, cols].astype(F32)
            sg = jax.nn.sigmoid(g)
            silu = g * sg
            act_ref[:, cols] = (silu * u).astype(MXU_DTYPE)
            du = (dact * silu).astype(MXU_DTYPE)
            dg = (dact * u * sg * (1.0 + g * (1.0 - sg))).astype(MXU_DTYPE)
            du_ref[:, cols] = du
            dg_ref[:, cols] = dg
            dhn = dhn + _dot(dg, wgt_ref[cols, :]) + _dot(du, wut_ref[cols, :])
        n, r = _rms_stats(h_ref[...])
        dgn_ref[...] += jnp.sum(dhn * n, axis=0, keepdims=True)
        o_ref[...] = dh + _rms_bwd(dhn * g_ref[...], n, r)

    return pl.pallas_call(
        body, name="ffn_bwd", grid=(s // tm,),
        in_specs=[_rows(tm, D_MODEL), _rows(tm, D_MODEL), _rows(tm, D_FF), _rows(tm, D_FF), _full((1, D_MODEL)),
                  _full((D_MODEL, D_FF)), _full((D_FF, D_MODEL)), _full((D_FF, D_MODEL))],
        out_specs=[_rows(tm, D_MODEL), _rows(tm, D_FF), _rows(tm, D_FF), _rows(tm, D_FF), _full((1, D_MODEL))],
        out_shape=[jax.ShapeDtypeStruct((s, D_MODEL), F32), jax.ShapeDtypeStruct((s, D_FF), MXU_DTYPE),
                   jax.ShapeDtypeStruct((s, D_FF), MXU_DTYPE), jax.ShapeDtypeStruct((s, D_FF), MXU_DTYPE),
                   jax.ShapeDtypeStruct((1, D_MODEL), F32)],
        compiler_params=_params(("arbitrary",)),
    )(dh2, h1, gate, up, g_ffn, w_down_t, w_gate_t, w_up_t)


def _out_bwd(dh1, y_b, g_b, w_out_t):
    s = dh1.shape[0]
    tm = 512

    def body(dh_ref, yb_ref, g_ref, wt_ref, dya_ref, dyb_ref, dg_ref):
        @pl.when(pl.program_id(0) == 0)
        def _():
            dg_ref[...] = jnp.zeros_like(dg_ref)

        dy = _dot(dh_ref[...].astype(MXU_DTYPE), wt_ref[...])
        dya_ref[...] = dy[:, :WIDTH_A]
        dyb = dy[:, WIDTH_A:]
        n, r = _rms_stats(yb_ref[...])
        dg_ref[...] += jnp.sum(dyb * n, axis=0, keepdims=True)
        dyb_ref[...] = _rms_bwd(dyb * g_ref[...], n, r)

    return pl.pallas_call(
        body, name="out_bwd", grid=(s // tm,),
        in_specs=[_rows(tm, D_MODEL), _rows(tm, WIDTH_B), _full((1, WIDTH_B)), _full((D_MODEL, D_MODEL))],
        out_specs=[_rows(tm, WIDTH_A), _rows(tm, WIDTH_B), _full((1, WIDTH_B))],
        out_shape=[jax.ShapeDtypeStruct((s, WIDTH_A), F32), jax.ShapeDtypeStruct((s, WIDTH_B), F32),
                   jax.ShapeDtypeStruct((1, WIDTH_B), F32)],
        compiler_params=_params(("arbitrary",)),
    )(dh1, y_b, g_b, w_out_t)


def _attn_bwd_branch(q, k, v, do, o, lse, grads, dil):
    s = q.shape[0]
    rows = s // dil
    nb = rows // BLOCK
    first = grads is None
    view = lambda a, w: a.reshape(rows, dil * w)

    def body(*refs):
        q_ref, kc_ref, kp_ref, vc_ref, vp_ref, do_ref, o_ref, lse_ref = refs[:8]
        if first:
            rest = refs[8:]
        else:
            dq_in, dk_in, dv_in = refs[8:11]
            rest = refs[11:]
        dq_ref, dk_ref, dv_ref, dk_carry, dv_carry = rest
        n = pl.program_id(1)

        @pl.when(n == 0)
        def _():
            dk_carry[...] = jnp.zeros_like(dk_carry)
            dv_carry[...] = jnp.zeros_like(dv_carry)

        @pl.when(n < nb)
        def _():
            valid = _band_mask(n)
            lane = lax.broadcasted_iota(jnp.int32, (BLOCK, LANES), 1)
            lo = lane < HEAD_DIM
            masks = _half_masks(MXU_DTYPE)
            masks_f = _half_masks(F32)
            for hp in range(HEADS_B // 2):
                cols = slice(hp * LANES, (hp + 1) * LANES)
                qp = q_ref[:, cols]
                kcat = jnp.concatenate([kp_ref[:, cols], kc_ref[:, cols]], axis=0)
                vcat = jnp.concatenate([vp_ref[:, cols], vc_ref[:, cols]], axis=0)
                dop = do_ref[:, cols]
                prod = dop * o_ref[:, cols]
                dkc = jnp.zeros((2 * BLOCK, LANES), F32)
                dvc = jnp.zeros((2 * BLOCK, LANES), F32)
                pair = []
                for j in range(2):
                    h = 2 * hp + j
                    delta = jnp.sum(prod * masks_f[j], axis=1, keepdims=True)
                    qj = qp * masks[j]
                    doj = (dop * masks_f[j]).astype(MXU_DTYPE)
                    sc = _dot_nt(qj, kcat)
                    p = jnp.where(valid, jnp.exp(sc - lse_ref[:, h:h + 1]), 0.0)
                    ds = (p * (_dot_nt(doj, vcat) - delta)).astype(MXU_DTYPE)
                    dvc = dvc + _dot_tn(p.astype(MXU_DTYPE), doj)
                    dkc = dkc + _dot_tn(ds, qj)
                    pair.append(_dot(ds, kcat))
                dq = jnp.where(lo, pair[0], pair[1])
                dk_prev = dk_carry[:, cols] + dkc[:BLOCK]
                dv_prev = dv_carry[:, cols] + dvc[:BLOCK]
                if not first:
                    dq = dq + dq_in[:, cols]
                    dk_prev = dk_prev + dk_in[:, cols]
                    dv_prev = dv_prev + dv_in[:, cols]
                dq_ref[:, cols] = dq
                dk_ref[:, cols] = dk_prev
                dv_ref[:, cols] = dv_prev
                dk_carry[:, cols] = dkc[BLOCK:]
                dv_carry[:, cols] = dvc[BLOCK:]

        @pl.when(n == nb)
        def _():
            if first:
                dk_ref[...] = dk_carry[...]
                dv_ref[...] = dv_carry[...]
            else:
                dk_ref[...] = dk_carry[...] + dk_in[...]
                dv_ref[...] = dv_carry[...] + dv_in[...]

    cur = lambda w: pl.BlockSpec((BLOCK, w), lambda r, n: (jnp.minimum(n, nb - 1), r))
    prev = lambda w: pl.BlockSpec((BLOCK, w), lambda r, n: (jnp.maximum(jnp.minimum(n, nb - 1) - 1, 0), r))
    late = lambda w: pl.BlockSpec((BLOCK, w), lambda r, n: (jnp.maximum(n - 1, 0), r))
    in_specs = [cur(WIDTH_B), cur(WIDTH_B), prev(WIDTH_B), cur(WIDTH_B), prev(WIDTH_B), cur(WIDTH_B),
                cur(WIDTH_B), cur(LANES)]
    args = [view(q, WIDTH_B), view(k, WIDTH_B), view(k, WIDTH_B), view(v, WIDTH_B), view(v, WIDTH_B),
            view(do, WIDTH_B), view(o, WIDTH_B), view(lse, LANES)]
    if not first:
        in_specs += [cur(WIDTH_B), late(WIDTH_B), late(WIDTH_B)]
        args += [view(g, WIDTH_B) for g in grads]
    res = pl.pallas_call(
        body, name="attn_bwd_d%d" % dil, grid=(dil, nb + 1), in_specs=in_specs,
        out_specs=[cur(WIDTH_B), late(WIDTH_B), late(WIDTH_B)],
        out_shape=[jax.ShapeDtypeStruct((rows, dil * WIDTH_B), F32)] * 3,
        scratch_shapes=[pltpu.VMEM((BLOCK, WIDTH_B), F32), pltpu.VMEM((BLOCK, WIDTH_B), F32)],
        compiler_params=_params(("arbitrary", "arbitrary")),
    )(*args)
    return tuple(a.reshape(s, WIDTH_B) for a in res)


def _sgu_bwd(uv, dya_n, w_tril, w_tril_t, bias, g_sgu, g_a):
    s = uv.shape[0]
    tm = 512

    def body(uv_ref, dy_ref, w_ref, wt_ref, b_ref, gs_ref, ga_ref, duv_ref, dw_ref, db_ref, dgs_ref, dga_ref,
             db_acc):
        i = pl.program_id(0)

        @pl.when(i == 0)
        def _():
            dw_ref[...] = jnp.zeros_like(dw_ref)
            dgs_ref[...] = jnp.zeros_like(dgs_ref)
            dga_ref[...] = jnp.zeros_like(dga_ref)
            db_acc[...] = jnp.zeros_like(db_acc)

        t = _sgu_forward_tile(uv_ref[...], w_ref, b_ref[...], gs_ref[...])
        na, ra = _rms_stats(t['ya'])
        dyn = dy_ref[...]
        dga_ref[...] += jnp.sum(dyn * na, axis=0, keepdims=True)
        dya = _rms_bwd(dyn * ga_ref[...], na, ra)
        dug = dya * t['mixed']
        dmixed = dya * t['ug']
        dmb = dmixed.astype(MXU_DTYPE)
        masks = _half_masks(MXU_DTYPE)
        chunks = []
        db = jnp.zeros((CHUNK, WIDTH_A), F32)
        for c in range(tm // CHUNK):
            rows = slice(c * CHUNK, (c + 1) * CHUNK)
            db = db + dmixed[rows]
            groups = []
            for gp in range(2):
                cols = slice(gp * LANES, (gp + 1) * LANES)
                dm_g = dmb[rows, cols]
                vn_g = t['vn'][rows, cols]
                dvn_g = jnp.zeros((CHUNK, LANES), F32)
                for j in range(2):
                    dm_h = dm_g * masks[j]
                    dvn_g = dvn_g + _dot(wt_ref[2 * gp + j], dm_h)
                    dw_ref[2 * gp + j] += _dot_nt(dm_h, vn_g)
                groups.append(dvn_g)
            chunks.append(jnp.concatenate(groups, axis=1))
        db_acc[...] += db
        dvn = jnp.concatenate(chunks, axis=0)
        vhat = t['vhat']
        dgs_ref[...] += jnp.sum(dvn * vhat, axis=0, keepdims=True)
        dvh = dvn * gs_ref[...]
        dvg = t['rs'] * (dvh - jnp.mean(dvh, axis=-1, keepdims=True)
                         - vhat * jnp.mean(dvh * vhat, axis=-1, keepdims=True))
        duv_ref[:, :WIDTH_A] = (dug * _gelu_grad(t['u'], t['tu'])).astype(MXU_DTYPE)
        duv_ref[:, WIDTH_A:] = (dvg * _gelu_grad(t['v'], t['tv'])).astype(MXU_DTYPE)

        @pl.when(i == pl.num_programs(0) - 1)
        def _():
            lane_a = lax.broadcasted_iota(jnp.int32, (CHUNK, WIDTH_A), 1)
            lane = lax.broadcasted_iota(jnp.int32, (CHUNK, LANES), 1)
            acc = db_acc[...]
            out = jnp.zeros((CHUNK, LANES), F32)
            for h in range(HEADS_A):
                col = jnp.sum(jnp.where(lane_a // HEAD_DIM == h, acc, 0.0), axis=1, keepdims=True)
                out = jnp.where(lane == h, col, out)
            db_ref[...] = out
            causal = (lax.broadcasted_iota(jnp.int32, (CHUNK, CHUNK), 0)
                      >= lax.broadcasted_iota(jnp.int32, (CHUNK, CHUNK), 1))
            for h in range(HEADS_A):
                dw_ref[h] = jnp.where(causal, dw_ref[h], 0.0)

    return pl.pallas_call(
        body, name="sgu_bwd", grid=(s // tm,),
        in_specs=[_rows(tm, 2 * WIDTH_A), _rows(tm, WIDTH_A), _full((HEADS_A, CHUNK, CHUNK)),
                  _full((HEADS_A, CHUNK, CHUNK)), _full((CHUNK, WIDTH_A)), _full((1, WIDTH_A)),
                  _full((1, WIDTH_A))],
        out_specs=[_rows(tm, 2 * WIDTH_A), _full((HEADS_A, CHUNK, CHUNK)), _full((CHUNK, LANES)),
                   _full((1, WIDTH_A)), _full((1, WIDTH_A))],
        out_shape=[jax.ShapeDtypeStruct((s, 2 * WIDTH_A), MXU_DTYPE),
                   jax.ShapeDtypeStruct((HEADS_A, CHUNK, CHUNK), F32), jax.ShapeDtypeStruct((CHUNK, LANES), F32),
                   jax.ShapeDtypeStruct((1, WIDTH_A), F32), jax.ShapeDtypeStruct((1, WIDTH_A), F32)],
        scratch_shapes=[pltpu.VMEM((CHUNK, WIDTH_A), F32)],
        compiler_params=_params(("arbitrary",)),
    )(uv, dya_n, w_tril, w_tril_t, bias, g_sgu, g_a)


def _in_bwd(duv, dq, dk, dv, cos_t, sin_t, w_in_t, x, g_mix, dh1):
    s = x.shape[0]
    tm = 512

    def body(duv_ref, dq_ref, dk_ref, dv_ref, cos_ref, sin_ref, wt_ref, x_ref, g_ref, dh_ref,
             gx_ref, dp_ref, dg_ref):
        @pl.when(pl.program_id(0) == 0)
        def _():
            dg_ref[...] = jnp.zeros_like(dg_ref)

        cos = cos_ref[...]
        sin = sin_ref[...]
        dp_ref[:, :2 * WIDTH_A] = duv_ref[...]
        for i in range(WIDTH_B // LANES):
            cols = slice(i * LANES, (i + 1) * LANES)
            lo = 2 * WIDTH_A + i * LANES
            tq = dq_ref[:, cols] * (HEAD_DIM ** -0.5)
            tk = dk_ref[:, cols]
            dp_ref[:, lo:lo + LANES] = (tq * cos + _rope_partner(tq * sin)).astype(MXU_DTYPE)
            dp_ref[:, lo + WIDTH_B:lo + WIDTH_B + LANES] = (tk * cos + _rope_partner(tk * sin)).astype(MXU_DTYPE)
        dp_ref[:, 2 * WIDTH_A + 2 * WIDTH_B:] = dv_ref[...].astype(MXU_DTYPE)
        dhn = _dot(dp_ref[...], wt_ref[...])
        n, r = _rms_stats(x_ref[...])
        dg_ref[...] += jnp.sum(dhn * n, axis=0, keepdims=True)
        gx_ref[...] = dh_ref[...] + _rms_bwd(dhn * g_ref[...], n, r)

    return pl.pallas_call(
        body, name="in_bwd", grid=(s // tm,),
        in_specs=[_rows(tm, 2 * WIDTH_A), _rows(tm, WIDTH_B), _rows(tm, WIDTH_B), _rows(tm, WIDTH_B),
                  _rows(tm, LANES), _rows(tm, LANES), _full((IN_COLS, D_MODEL)), _rows(tm, D_MODEL),
                  _full((1, D_MODEL)), _rows(tm, D_MODEL)],
        out_specs=[_rows(tm, D_MODEL), _rows(tm, IN_COLS), _full((1, D_MODEL))],
        out_shape=[jax.ShapeDtypeStruct((s, D_MODEL), F32), jax.ShapeDtypeStruct((s, IN_COLS), MXU_DTYPE),
                   jax.ShapeDtypeStruct((1, D_MODEL), F32)],
        compiler_params=_params(("arbitrary",)),
    )(duv, dq, dk, dv, cos_t, sin_t, w_in_t, x, g_mix, dh1)


def _wgrad(a, bs, name):
    s, m = a.shape
    bm = 512 if m % 512 == 0 else (FF_HALF if m == D_FF else m)
    ts = 512
    nsteps = s // ts

    def body(a_ref, *refs):
        b_refs, o_refs = refs[:len(bs)], refs[len(bs):]
        kk = pl.program_id(1)
        at = a_ref[...].astype(MXU_DTYPE)
        for b_ref, o_ref in zip(b_refs, o_refs):
            c = _dot_tn(at, b_ref[...].astype(MXU_DTYPE))

            @pl.when(kk == 0)
            def _():
                o_ref[...] = c

            @pl.when(kk > 0)
            def _():
                o_ref[...] += c

    return pl.pallas_call(
        body, name=name, grid=(m // bm, nsteps),
        in_specs=[pl.BlockSpec((ts, bm), lambda i, kk: (kk, i))]
        + [pl.BlockSpec((ts, b.shape[1]), lambda i, kk: (kk, 0)) for b in bs],
        out_specs=[pl.BlockSpec((bm, b.shape[1]), lambda i, kk: (i, 0)) for b in bs],
        out_shape=[jax.ShapeDtypeStruct((m, b.shape[1]), F32) for b in bs],
        compiler_params=_params(("arbitrary", "arbitrary")),
    )(a, *bs)


def _rope_tables(s):
    half = HEAD_DIM // 2
    inv = ROPE_THETA ** (-jnp.arange(half, dtype=F32) / half)
    ang = jnp.arange(s, dtype=F32)[:, None] * inv[None, :]
    cos = jnp.cos(ang)
    sin = jnp.sin(ang)
    cos_t = jnp.concatenate([cos, cos, cos, cos], axis=1)
    sin_t = jnp.concatenate([-sin, sin, -sin, sin], axis=1)
    return cos_t, sin_t


def _local_step(x, p, target, w, small):
    s = x.shape[0]
    cos_t, sin_t = _rope_tables(s)
    tril = jnp.tril(jnp.ones((CHUNK, CHUNK), F32))
    w_tril = (small['sgu_w'].reshape(HEADS_A, CHUNK, CHUNK) * tril).astype(MXU_DTYPE)
    w_tril_t = jnp.swapaxes(w_tril, 1, 2)
    bias = jnp.repeat(small['sgu_b'].reshape(HEADS_A, CHUNK).T, HEAD_DIM, axis=1)
    g = {n: small[n].reshape(1, -1) for n in SMALL if n not in ('sgu_w', 'sgu_b')}

    uv, q, k, v, hn1 = _in_fwd(x, g['mix_norm_g'], w['w_in'], cos_t, sin_t)
    ya_n = _sgu_fwd(uv, w_tril, bias, g['sgu_norm_g'], g['out_norm_a'])
    state = None
    for i, dil in enumerate(DILATIONS):
        state = _attn_fwd_branch(q, k, v, state, dil, last=(i == len(DILATIONS) - 1))
    y_b, lse = state
    h1, yb_n = _out_fwd(ya_n, y_b, g['out_norm_b'], w['w_out'], x)
    h2, gate, up, hn2 = _ffn_fwd(h1, g['ffn_norm_g'], w['w_gate'], w['w_up'], w['w_down'])
    loss, dh2, dz, dpp, hn3, d_ple_g, d_final_g = _ple_loss(
        h2, p, target, g['ple_norm_g'], w['w_ple_gate'], w['w_ple_gate'].T, w['w_ple_proj'], g['final_norm_g'])

    dh1, act, dgate, dup, d_ffn_g = _ffn_bwd(dh2, h1, gate, up, g['ffn_norm_g'], w['w_down'].T, w['w_gate'].T,
                                             w['w_up'].T)
    dya_n, dyb, d_out_b = _out_bwd(dh1, y_b, g['out_norm_b'], w['w_out'].T)
    grads = None
    for dil in DILATIONS:
        grads = _attn_bwd_branch(q, k, v, dyb, y_b, lse, grads, dil)
    duv, d_sgu_w, d_sgu_b, d_sgu_g, d_out_a = _sgu_bwd(uv, dya_n, w_tril, w_tril_t, bias, g['sgu_norm_g'],
                                                       g['out_norm_a'])
    grad_x, dproj, d_mix_g = _in_bwd(duv, grads[0], grads[1], grads[2], cos_t, sin_t, w['w_in'].T, x,
                                     g['mix_norm_g'], dh1)

    gw = {}
    gw['w_in'], = _wgrad(hn1, [dproj], "wgrad_in")
    gw_out_a, = _wgrad(ya_n, [dh1], "wgrad_out_a")
    gw_out_b, = _wgrad(yb_n, [dh1], "wgrad_out_b")
    gw['w_out'] = jnp.concatenate([gw_out_a, gw_out_b], axis=0)
    gw['w_gate'], gw['w_up'] = _wgrad(hn2, [dgate, dup], "wgrad_gate_up")
    gw['w_down'], = _wgrad(act, [dh2], "wgrad_down")
    gw['w_ple_gate'], = _wgrad(hn3, [dz], "wgrad_ple_gate")
    gw['w_ple_proj'], = _wgrad(p, [dpp], "wgrad_ple_proj")

    gs = {
        'mix_norm_g': d_mix_g, 'sgu_w': d_sgu_w, 'sgu_b': d_sgu_b[:, :HEADS_A].T, 'sgu_norm_g': d_sgu_g,
        'out_norm_a': d_out_a, 'out_norm_b': d_out_b, 'ffn_norm_g': d_ffn_g, 'ple_norm_g': d_ple_g,
        'final_norm_g': d_final_g,
    }
    return loss, grad_x, gw, gs


MESH = pl.DeviceIdType.MESH
ANY = pl.BlockSpec(memory_space=pl.ANY)
PACK_ROWS = sum(FULL_SHAPES[n][0] * FULL_SHAPES[n][1] for n, _ in SHARDED) // N_CHIPS // D_MODEL
HALF_ROWS = PACK_ROWS // 2


def _place():
    x, y, c = lax.axis_index("x"), lax.axis_index("y"), lax.axis_index("c")
    other_chips = [(1 - x, y), (x, 1 - y), (1 - x, 1 - y)]
    return x, y, c, other_chips


def _all_gather_weights(packed):
    def body(x_ref, out_ref, send_sems, recv_sems, local_sem):
        x, y, c, chips = _place()
        sibling = (x, y, 1 - c)

        def half(chip, hc):
            return out_ref.at[2 * chip[0] + chip[1], pl.ds(hc * HALF_ROWS, HALF_ROWS), :]

        def copy(k, src, dst, to):
            return pltpu.make_async_remote_copy(src_ref=src, dst_ref=dst, send_sem=send_sems.at[k],
                                                recv_sem=recv_sems.at[k], device_id=to, device_id_type=MESH)

        mine = pltpu.make_async_copy(x_ref, out_ref.at[2 * x + y], local_sem)
        mine.start()
        my_half = x_ref.at[pl.ds(c * HALF_ROWS, HALF_ROWS), :]
        first = [copy(j, my_half, half((x, y), c), (*chip, c)) for j, chip in enumerate(chips)]
        for cp in first:
            cp.start()
        passed = [copy(3 + j, half(chip, c), half(chip, c), sibling) for j, chip in enumerate(chips)]
        for j, chip in enumerate(chips):
            copy(j, my_half, half(chip, c), (*chip, c)).wait_recv()
            passed[j].start()
        for j, chip in enumerate(chips):
            copy(3 + j, my_half, half(chip, 1 - c), sibling).wait_recv()
        for cp in first + passed:
            cp.wait_send()
        mine.wait()

    return pl.pallas_call(
        body, name="all_gather_weights",
        out_shape=jax.ShapeDtypeStruct((N_CHIPS,) + packed.shape, packed.dtype),
        in_specs=[ANY], out_specs=ANY,
        scratch_shapes=[pltpu.SemaphoreType.DMA((6,)), pltpu.SemaphoreType.DMA((6,)), pltpu.SemaphoreType.DMA],
    )(packed)


def _rs_to_sibling(g):
    def body(g_ref, r_ref, send_sem, recv_sem):
        x, y, c, _ = _place()
        cp = pltpu.make_async_remote_copy(
            src_ref=g_ref.at[:, pl.ds((1 - c) * HALF_ROWS, HALF_ROWS), :], dst_ref=r_ref, send_sem=send_sem,
            recv_sem=recv_sem, device_id=(x, y, 1 - c), device_id_type=MESH)
        cp.start()
        cp.wait()

    return pl.pallas_call(
        body, name="rs_to_sibling", out_shape=jax.ShapeDtypeStruct((N_CHIPS, HALF_ROWS, D_MODEL), g.dtype),
        in_specs=[ANY], out_specs=ANY,
        scratch_shapes=[pltpu.SemaphoreType.DMA, pltpu.SemaphoreType.DMA],
    )(g)


def _rs_add(g, r, c):
    tr = HALF_ROWS // 2
    spec = pltpu.PrefetchScalarGridSpec(
        num_scalar_prefetch=1, grid=(N_CHIPS, 2),
        in_specs=[pl.BlockSpec((1, tr, D_MODEL), lambda j, i, c_ref: (j, 2 * c_ref[0] + i, 0)),
                  pl.BlockSpec((1, tr, D_MODEL), lambda j, i, c_ref: (j, i, 0))],
        out_specs=pl.BlockSpec((1, tr, D_MODEL), lambda j, i, c_ref: (j, i, 0)))

    def body(c_ref, g_ref, r_ref, o_ref):
        o_ref[...] = (g_ref[...] + r_ref[...]).astype(o_ref.dtype)

    return pl.pallas_call(
        body, name="rs_add", grid_spec=spec,
        out_shape=jax.ShapeDtypeStruct((N_CHIPS, HALF_ROWS, D_MODEL), jnp.bfloat16),
        compiler_params=_params(("arbitrary", "arbitrary")),
    )(c.reshape(1).astype(jnp.int32), g, r)


def _rs_between_chips(part):
    def body(p_ref, out_ref, send_sems, recv_sems, local_sem):
        x, y, c, chips = _place()
        me = 2 * x + y
        mine = pltpu.make_async_copy(p_ref.at[me], out_ref.at[me], local_sem)
        mine.start()
        sends = []
        for j, chip in enumerate(chips):
            cp = pltpu.make_async_remote_copy(
                src_ref=p_ref.at[2 * chip[0] + chip[1]], dst_ref=out_ref.at[me], send_sem=send_sems.at[j],
                recv_sem=recv_sems.at[j], device_id=(*chip, c), device_id_type=MESH)
            cp.start()
            sends.append(cp)
        for j, chip in enumerate(chips):
            pltpu.make_async_remote_copy(
                src_ref=p_ref.at[me], dst_ref=out_ref.at[2 * chip[0] + chip[1]], send_sem=send_sems.at[j],
                recv_sem=recv_sems.at[j], device_id=(*chip, c), device_id_type=MESH).wait_recv()
        for cp in sends:
            cp.wait_send()
        mine.wait()

    return pl.pallas_call(
        body, name="rs_between_chips", out_shape=jax.ShapeDtypeStruct(part.shape, part.dtype),
        in_specs=[ANY], out_specs=ANY,
        scratch_shapes=[pltpu.SemaphoreType.DMA((3,)), pltpu.SemaphoreType.DMA((3,)), pltpu.SemaphoreType.DMA],
    )(part)


def _rs_sum(parts):
    tr = HALF_ROWS // 2

    def body(p_ref, o_ref):
        acc = p_ref[0].astype(F32)
        for i in range(1, N_CHIPS):
            acc = acc + p_ref[i].astype(F32)
        o_ref[...] = acc

    return pl.pallas_call(
        body, name="rs_sum", grid=(2,),
        in_specs=[pl.BlockSpec((N_CHIPS, tr, D_MODEL), lambda i: (0, i, 0))],
        out_specs=pl.BlockSpec((tr, D_MODEL), lambda i: (i, 0)),
        out_shape=jax.ShapeDtypeStruct((HALF_ROWS, D_MODEL), F32),
        compiler_params=_params(("arbitrary",)),
    )(parts)


def _rs_swap(red):
    def body(r_ref, out_ref, send_sem, recv_sem, local_sem):
        x, y, c, _ = _place()
        mine_rows = out_ref.at[pl.ds(c * HALF_ROWS, HALF_ROWS), :]
        mine = pltpu.make_async_copy(r_ref, mine_rows, local_sem)
        mine.start()
        cp = pltpu.make_async_remote_copy(src_ref=r_ref, dst_ref=mine_rows, send_sem=send_sem, recv_sem=recv_sem,
                                          device_id=(x, y, 1 - c), device_id_type=MESH)
        cp.start()
        cp.wait_send()
        pltpu.make_async_remote_copy(
            src_ref=r_ref, dst_ref=out_ref.at[pl.ds((1 - c) * HALF_ROWS, HALF_ROWS), :], send_sem=send_sem,
            recv_sem=recv_sem, device_id=(x, y, 1 - c), device_id_type=MESH).wait_recv()
        mine.wait()

    return pl.pallas_call(
        body, name="rs_swap", out_shape=jax.ShapeDtypeStruct((PACK_ROWS, D_MODEL), red.dtype),
        in_specs=[ANY], out_specs=ANY,
        scratch_shapes=[pltpu.SemaphoreType.DMA, pltpu.SemaphoreType.DMA, pltpu.SemaphoreType.DMA],
    )(red)


def _small_all_reduce(block):
    rows = block.shape[0]

    def body(x_ref, all_ref, sum_ref, send_sems, recv_sems, local_sem):
        x, y, c, chips = _place()
        me, sibling = (x, y, c), (x, y, 1 - c)

        def blk(px, py, pc):
            return all_ref.at[pl.ds((4 * px + 2 * py + pc) * rows, rows), :]

        def copy(k, who, to, src=None):
            return pltpu.make_async_remote_copy(
                src_ref=blk(*who) if src is None else src, dst_ref=blk(*who), send_sem=send_sems.at[k],
                recv_sem=recv_sems.at[k], device_id=to, device_id_type=MESH)

        mine = pltpu.make_async_copy(x_ref, blk(*me), local_sem)
        mine.start()
        first = [copy(0, me, sibling, src=x_ref)]
        first += [copy(1 + j, me, (*chip, c), src=x_ref) for j, chip in enumerate(chips)]
        for cp in first:
            cp.start()
        passed = [copy(4 + j, (*chip, c), sibling) for j, chip in enumerate(chips)]
        for j, chip in enumerate(chips):
            copy(1 + j, (*chip, c), me).wait_recv()
            passed[j].start()
        copy(0, sibling, me).wait_recv()
        for j, chip in enumerate(chips):
            copy(4 + j, (*chip, 1 - c), me).wait_recv()
        for cp in first + passed:
            cp.wait_send()
        mine.wait()
        acc = all_ref[pl.ds(0, rows), :]
        for dev in range(1, N_DEV):
            acc = acc + all_ref[pl.ds(dev * rows, rows), :]
        sum_ref[...] = acc

    vmem = pl.BlockSpec(memory_space=pltpu.VMEM)
    return pl.pallas_call(
        body, name="small_all_reduce",
        out_shape=[jax.ShapeDtypeStruct((N_DEV * rows, D_MODEL), F32), jax.ShapeDtypeStruct((rows, D_MODEL), F32)],
        in_specs=[vmem], out_specs=[vmem, vmem],
        scratch_shapes=[pltpu.SemaphoreType.DMA((7,)), pltpu.SemaphoreType.DMA((7,)), pltpu.SemaphoreType.DMA],
    )(block)[1]


def _adamw(w, g, m, v, name):
    rows, cols = w.shape
    tm = rows
    if rows > 512:
        tm = next(t for t in range(512, 7, -8) if rows % t == 0)

    def body(w_ref, g_ref, m_ref, v_ref, d_ref, nm_ref, nv_ref):
        g_ = g_ref[...]
        m_ = ADAM_B1 * m_ref[...] + (1.0 - ADAM_B1) * g_
        v_ = ADAM_B2 * v_ref[...] + (1.0 - ADAM_B2) * (g_ * g_)
        m_hat = m_ / (1.0 - ADAM_B1 ** ADAM_STEP)
        v_hat = v_ / (1.0 - ADAM_B2 ** ADAM_STEP)
        d_ref[...] = -ADAM_LR * (m_hat / (jnp.sqrt(v_hat) + ADAM_EPS) + ADAM_WD * w_ref[...])
        nm_ref[...] = m_
        nv_ref[...] = v_

    spec = pl.BlockSpec((tm, cols), lambda i: (i, 0))
    return pl.pallas_call(
        body, name=name, grid=(rows // tm,), in_specs=[spec] * 4, out_specs=[spec] * 3,
        out_shape=[jax.ShapeDtypeStruct(w.shape, F32)] * 3, compiler_params=_params(("arbitrary",)),
    )(w, g, m, v)


def _pack_small(values):
    flat = jnp.concatenate([values[n].reshape(-1).astype(F32) for n in SMALL])
    return jnp.pad(flat, (0, SMALL_ROWS * D_MODEL - flat.shape[0])).reshape(SMALL_ROWS, D_MODEL)


def _unpack_small(block, shapes):
    flat = block.reshape(-1)
    out, lo = {}, 0
    for n in SMALL:
        out[n] = flat[lo:lo + SMALL_SIZES[n]].reshape(shapes[n])
        lo += SMALL_SIZES[n]
    return out


def _shard_rows(name):
    shape = FULL_SHAPES[name]
    return shape[0] * shape[1] // N_CHIPS // D_MODEL


def _unpack_gathered(gathered):
    out, lo = {}, 0
    for name, axis in SHARDED:
        rows, cols = FULL_SHAPES[name]
        part = gathered[:, lo:lo + _shard_rows(name)]
        lo += _shard_rows(name)
        if axis == 0:
            out[name] = part.reshape(rows, cols)
        else:
            out[name] = part.reshape(N_CHIPS, rows, cols // N_CHIPS).transpose(1, 0, 2).reshape(rows, cols)
    return out


def _pack_full_grads(gw):
    parts = []
    for name, axis in SHARDED:
        rows, cols = FULL_SHAPES[name]
        g = gw[name]
        if axis == 1:
            g = g.reshape(rows, N_CHIPS, cols // N_CHIPS).transpose(1, 0, 2)
        parts.append(g.reshape(N_CHIPS, _shard_rows(name), D_MODEL))
    return jnp.concatenate(parts, axis=1)


def kernel(x, p, mix_norm_g, w_in, sgu_w, sgu_b, sgu_norm_g, out_norm_a, out_norm_b, w_out, ffn_norm_g, w_gate, w_up, w_down, ple_norm_g, w_ple_gate, w_ple_proj, final_norm_g, loss_target, m_mix_norm_g, m_w_in, m_sgu_w, m_sgu_b, m_sgu_norm_g, m_out_norm_a, m_out_norm_b, m_w_out, m_ffn_norm_g, m_w_gate, m_w_up, m_w_down, m_ple_norm_g, m_w_ple_gate, m_w_ple_proj, m_final_norm_g, v_mix_norm_g, v_w_in, v_sgu_w, v_sgu_b, v_sgu_norm_g, v_out_norm_a, v_out_norm_b, v_w_out, v_ffn_norm_g, v_w_gate, v_w_up, v_w_down, v_ple_norm_g, v_w_ple_gate, v_w_ple_proj, v_final_norm_g):
    given = dict(locals())
    c = lax.axis_index("c")
    sharded_names = [n for n, _ in SHARDED]

    packed = jnp.concatenate([given[n][0].astype(MXU_DTYPE).reshape(-1, D_MODEL) for n in sharded_names], axis=0)
    w_full = _unpack_gathered(_all_gather_weights(packed))

    small = {n: given[n] for n in SMALL}
    loss, grad_x, gw, gs = _local_step(x[0], p[0, 0], loss_target[0], w_full, small)

    gs_block = _pack_small(gs)
    gs_block = gs_block.at[SMALL_ROWS - 1, 0].set(loss[0, 0])
    small_sum = _small_all_reduce(gs_block)
    loss_out = small_sum[SMALL_ROWS - 1, 0]
    small_shapes = {n: given[n].shape for n in SMALL}
    g_small = _unpack_small(small_sum, small_shapes)

    g_all = _pack_full_grads(gw)
    from_sibling = _rs_to_sibling(g_all)
    chip_part = _rs_add(g_all, from_sibling, c)
    summed_half = _rs_sum(_rs_between_chips(chip_part))
    g_pack = _rs_swap(summed_half)

    grads, deltas, new_m, new_v = {}, {}, {}, {}
    lo = 0
    for n in sharded_names:
        shard_shape = given[n].shape[1:]
        g = g_pack[lo:lo + _shard_rows(n)].reshape(shard_shape)
        lo += _shard_rows(n)
        d, nm, nv = _adamw(given[n][0], g, given["m_" + n][0], given["v_" + n][0], "adamw_" + n)
        grads[n], deltas[n], new_m[n], new_v[n] = g[None], d[None], nm[None], nv[None]

    d, nm, nv = _adamw(_pack_small(small), small_sum, _pack_small({n: given["m_" + n] for n in SMALL}),
                       _pack_small({n: given["v_" + n] for n in SMALL}), "adamw_small")
    for res, blk in ((grads, small_sum), (deltas, d), (new_m, nm), (new_v, nv)):
        res.update(_unpack_small(blk, small_shapes))

    outs = [loss_out, grad_x[None]]
    for res in (grads, deltas, new_m, new_v):
        outs += [res[n] for n in WEIGHT_NAMES]
    return tuple(outs)
```

```python
import functools
import math

import jax
import jax.numpy as jnp
from jax import lax
from jax.experimental import pallas as pl
from jax.experimental.pallas import tpu as pltpu

F32 = jnp.float32
MXU_DTYPE = jnp.bfloat16

D_MODEL = 1024
HEAD_DIM = 64
HEADS_A = 4
HEADS_B = 12
WIDTH_A = HEADS_A * HEAD_DIM
WIDTH_B = HEADS_B * HEAD_DIM
CHUNK = 128
BLOCK = 128
DILATIONS = (1, 4, 16)
ROPE_THETA = 10000.0
D_FF = 2816
FF_HALF = D_FF // 2
PLE_DIM = 256
IN_COLS = 2 * WIDTH_A + 3 * WIDTH_B
EPS = 1e-6
LANES = 128
N_CHIPS = 4
N_DEV = 8

ADAM_LR = 0.001
ADAM_B1 = 0.9
ADAM_B2 = 0.999
ADAM_EPS = 1e-08
ADAM_WD = 0.01
ADAM_STEP = 10

VMEM_LIMIT = 56 * 1024 * 1024

WEIGHT_NAMES = ['mix_norm_g', 'w_in', 'sgu_w', 'sgu_b', 'sgu_norm_g', 'out_norm_a', 'out_norm_b', 'w_out',
                'ffn_norm_g', 'w_gate', 'w_up', 'w_down', 'ple_norm_g', 'w_ple_gate', 'w_ple_proj', 'final_norm_g']
SHARDED = [('w_in', 1), ('w_out', 0), ('w_gate', 1), ('w_up', 1), ('w_down', 0), ('w_ple_gate', 0), ('w_ple_proj', 1)]
FULL_SHAPES = {'w_in': (D_MODEL, IN_COLS), 'w_out': (D_MODEL, D_MODEL), 'w_gate': (D_MODEL, D_FF),
               'w_up': (D_MODEL, D_FF), 'w_down': (D_FF, D_MODEL), 'w_ple_gate': (D_MODEL, D_MODEL),
               'w_ple_proj': (PLE_DIM, D_MODEL)}
SMALL = ['mix_norm_g', 'sgu_w', 'sgu_b', 'sgu_norm_g', 'out_norm_a', 'out_norm_b', 'ffn_norm_g', 'ple_norm_g',
         'final_norm_g']
SMALL_SIZES = {'mix_norm_g': 1024, 'sgu_w': 65536, 'sgu_b': 512, 'sgu_norm_g': 256, 'out_norm_a': 256,
               'out_norm_b': 768, 'ffn_norm_g': 1024, 'ple_norm_g': 1024, 'final_norm_g': 1024}
SMALL_ROWS = 72


def _params(semantics=None):
    return pltpu.CompilerParams(dimension_semantics=semantics, vmem_limit_bytes=VMEM_LIMIT)


def _full(shape):
    nd = len(shape)
    return pl.BlockSpec(shape, lambda i: (0,) * nd)


def _rows(tm, width):
    return pl.BlockSpec((tm, width), lambda i: (i, 0))


def _rms_stats(x):
    r = lax.rsqrt(jnp.mean(x * x, axis=-1, keepdims=True) + EPS)
    return x * r, r


def _rms_bwd(dn, n, r):
    return r * (dn - n * jnp.mean(dn * n, axis=-1, keepdims=True))


def _dot(a, b):
    return jnp.dot(a, b, preferred_element_type=F32)


def _dot_nt(a, b):
    return lax.dot_general(a, b, (((1,), (1,)), ((), ())), preferred_element_type=F32)


def _dot_tn(a, b):
    return lax.dot_general(a, b, (((0,), (0,)), ((), ())), preferred_element_type=F32)


def _gelu_parts(x):
    c = math.sqrt(2.0 / math.pi)
    t = jnp.tanh(c * (x + 0.044715 * x * x * x))
    return 0.5 * x * (1.0 + t), t


def _gelu_grad(x, t):
    c = math.sqrt(2.0 / math.pi)
    return 0.5 * (1.0 + t) + 0.5 * x * (1.0 - t * t) * c * (1.0 + 3.0 * 0.044715 * x * x)


def _half_masks(dtype):
    lane = lax.broadcasted_iota(jnp.int32, (BLOCK, LANES), 1)
    lo = (lane < HEAD_DIM).astype(F32)
    return lo.astype(dtype), (1.0 - lo).astype(dtype)


def _rope_partner(t):
    lane = lax.broadcasted_iota(jnp.int32, t.shape, 1)
    first_half = (lane % HEAD_DIM) < (HEAD_DIM // 2)
    return jnp.where(first_half, pltpu.roll(t, LANES - HEAD_DIM // 2, 1), pltpu.roll(t, HEAD_DIM // 2, 1))


def _in_fwd(x, g_mix, w_in, cos_t, sin_t):
    s = x.shape[0]
    tm = 512

    def body(x_ref, g_ref, w_ref, cos_ref, sin_ref, uv_ref, q_ref, k_ref, v_ref, hn_ref):
        n, _ = _rms_stats(x_ref[...])
        hn = (n * g_ref[...]).astype(MXU_DTYPE)
        hn_ref[...] = hn
        proj = _dot(hn, w_ref[...])
        uv_ref[...] = proj[:, :2 * WIDTH_A]
        cos = cos_ref[...]
        sin = sin_ref[...]
        for i in range(WIDTH_B // LANES):
            lo = 2 * WIDTH_A + i * LANES
            tq = proj[:, lo:lo + LANES]
            tk = proj[:, lo + WIDTH_B:lo + WIDTH_B + LANES]
            cols = slice(i * LANES, (i + 1) * LANES)
            q_ref[:, cols] = ((tq * cos + _rope_partner(tq) * sin) * (HEAD_DIM ** -0.5)).astype(MXU_DTYPE)
            k_ref[:, cols] = (tk * cos + _rope_partner(tk) * sin).astype(MXU_DTYPE)
        v_ref[...] = proj[:, 2 * WIDTH_A + 2 * WIDTH_B:].astype(MXU_DTYPE)

    return pl.pallas_call(
        body, name="in_fwd", grid=(s // tm,),
        in_specs=[_rows(tm, D_MODEL), _full((1, D_MODEL)), _full((D_MODEL, IN_COLS)), _rows(tm, LANES),
                  _rows(tm, LANES)],
        out_specs=[_rows(tm, 2 * WIDTH_A), _rows(tm, WIDTH_B), _rows(tm, WIDTH_B), _rows(tm, WIDTH_B),
                   _rows(tm, D_MODEL)],
        out_shape=[jax.ShapeDtypeStruct((s, 2 * WIDTH_A), F32), jax.ShapeDtypeStruct((s, WIDTH_B), MXU_DTYPE),
                   jax.ShapeDtypeStruct((s, WIDTH_B), MXU_DTYPE), jax.ShapeDtypeStruct((s, WIDTH_B), MXU_DTYPE),
                   jax.ShapeDtypeStruct((s, D_MODEL), MXU_DTYPE)],
        compiler_params=_params(("arbitrary",)),
    )(x, g_mix, w_in, cos_t, sin_t)


def _band_mask(n):
    qi = lax.broadcasted_iota(jnp.int32, (BLOCK, 2 * BLOCK), 0)
    kc = lax.broadcasted_iota(jnp.int32, (BLOCK, 2 * BLOCK), 1)
    dist = BLOCK + qi - kc
    return (dist >= 0) & (dist <= BLOCK) & ((n > 0) | (kc >= BLOCK))


def _attn_fwd_branch(q, k, v, state, dil, last):
    s = q.shape[0]
    rows = s // dil
    nb = rows // BLOCK
    first = state is None
    view = lambda a, w: a.reshape(rows, dil * w)

    def body(*refs):
        q_ref, kc_ref, kp_ref, vc_ref, vp_ref = refs[:5]
        if first:
            outs = refs[5:]
        else:
            acc_ref, m_ref, l_ref = refs[5:8]
            outs = refs[8:]
        o_ref, m_out = outs[0], outs[1]
        valid = _band_mask(pl.program_id(1))
        lane = lax.broadcasted_iota(jnp.int32, (BLOCK, LANES), 1)
        lo = lane < HEAD_DIM
        masks = _half_masks(MXU_DTYPE)
        m_blk = jnp.zeros((BLOCK, LANES), F32)
        l_blk = jnp.zeros((BLOCK, LANES), F32)
        for hp in range(HEADS_B // 2):
            cols = slice(hp * LANES, (hp + 1) * LANES)
            qp = q_ref[:, cols]
            kcat = jnp.concatenate([kp_ref[:, cols], kc_ref[:, cols]], axis=0)
            vcat = jnp.concatenate([vp_ref[:, cols], vc_ref[:, cols]], axis=0)
            pair = []
            for j in range(2):
                h = 2 * hp + j
                sc = jnp.where(valid, _dot_nt(qp * masks[j], kcat), -jnp.inf)
                m_new = jnp.max(sc, axis=1, keepdims=True)
                if not first:
                    m_prev = m_ref[:, h:h + 1]
                    m_new = jnp.maximum(m_prev, m_new)
                p = jnp.exp(sc - m_new)
                l_new = jnp.sum(p, axis=1, keepdims=True)
                acc = _dot(p.astype(MXU_DTYPE), vcat)
                if not first:
                    alpha = jnp.exp(m_prev - m_new)
                    l_new = alpha * l_ref[:, h:h + 1] + l_new
                    acc = alpha * acc_ref[:, cols] + acc
                if last:
                    acc = acc / l_new
                    m_blk = jnp.where(lane == h, m_new + jnp.log(l_new), m_blk)
                else:
                    m_blk = jnp.where(lane == h, m_new, m_blk)
                    l_blk = jnp.where(lane == h, l_new, l_blk)
                pair.append(acc)
            o_ref[:, cols] = jnp.where(lo, pair[0], pair[1])
        m_out[...] = m_blk
        if not last:
            outs[2][...] = l_blk

    cur = lambda w: pl.BlockSpec((BLOCK, w), lambda r, n: (n, r))
    prev = lambda w: pl.BlockSpec((BLOCK, w), lambda r, n: (jnp.maximum(n - 1, 0), r))
    in_specs = [cur(WIDTH_B), cur(WIDTH_B), prev(WIDTH_B), cur(WIDTH_B), prev(WIDTH_B)]
    args = [view(q, WIDTH_B), view(k, WIDTH_B), view(k, WIDTH_B), view(v, WIDTH_B), view(v, WIDTH_B)]
    if not first:
        in_specs += [cur(WIDTH_B), cur(LANES), cur(LANES)]
        args += [view(state[0], WIDTH_B), view(state[1], LANES), view(state[2], LANES)]
    out_specs = [cur(WIDTH_B), cur(LANES)]
    out_shape = [jax.ShapeDtypeStruct((rows, dil * WIDTH_B), F32), jax.ShapeDtypeStruct((rows, dil * LANES), F32)]
    if not last:
        out_specs.append(cur(LANES))
        out_shape.append(jax.ShapeDtypeStruct((rows, dil * LANES), F32))
    res = pl.pallas_call(
        body, name="attn_fwd_d%d" % dil, grid=(dil, nb), in_specs=in_specs, out_specs=out_specs,
        out_shape=out_shape, compiler_params=_params(("arbitrary", "arbitrary")),
    )(*args)
    widths = (WIDTH_B, LANES, LANES)
    return tuple(a.reshape(s, w) for a, w in zip(res, widths))


def _sgu_forward_tile(uv, w_ref, bias, g_sgu):
    tm = uv.shape[0]
    u = uv[:, :WIDTH_A]
    v = uv[:, WIDTH_A:]
    ug, tu = _gelu_parts(u)
    vg, tv = _gelu_parts(v)
    mu = jnp.mean(vg, axis=-1, keepdims=True)
    vc = vg - mu
    rs = lax.rsqrt(jnp.mean(vc * vc, axis=-1, keepdims=True) + EPS)
    vhat = vc * rs
    vn = (vhat * g_sgu).astype(MXU_DTYPE)
    masks = _half_masks(MXU_DTYPE)
    chunks = []
    for c in range(tm // CHUNK):
        rows = slice(c * CHUNK, (c + 1) * CHUNK)
        groups = []
        for gp in range(2):
            vn_g = vn[rows, gp * LANES:(gp + 1) * LANES]
            groups.append(_dot(w_ref[2 * gp], vn_g * masks[0]) + _dot(w_ref[2 * gp + 1], vn_g * masks[1]))
        chunks.append(jnp.concatenate(groups, axis=1) + bias)
    mixed = jnp.concatenate(chunks, axis=0)
    return dict(u=u, v=v, ug=ug, tu=tu, tv=tv, rs=rs, vhat=vhat, vn=vn, mixed=mixed, ya=ug * mixed)


def _sgu_fwd(uv, w_tril, bias, g_sgu, g_a):
    s = uv.shape[0]
    tm = 512

    def body(uv_ref, w_ref, b_ref, gs_ref, ga_ref, o_ref):
        t = _sgu_forward_tile(uv_ref[...], w_ref, b_ref[...], gs_ref[...])
        n, _ = _rms_stats(t['ya'])
        o_ref[...] = (n * ga_ref[...]).astype(MXU_DTYPE)

    return pl.pallas_call(
        body, name="sgu_fwd", grid=(s // tm,),
        in_specs=[_rows(tm, 2 * WIDTH_A), _full((HEADS_A, CHUNK, CHUNK)), _full((CHUNK, WIDTH_A)),
                  _full((1, WIDTH_A)), _full((1, WIDTH_A))],
        out_specs=_rows(tm, WIDTH_A), out_shape=jax.ShapeDtypeStruct((s, WIDTH_A), MXU_DTYPE),
        compiler_params=_params(("arbitrary",)),
    )(uv, w_tril, bias, g_sgu, g_a)


def _out_fwd(ya_n, y_b, g_b, w_out, x):
    s = x.shape[0]
    tm = 512

    def body(ya_ref, yb_ref, g_ref, w_ref, x_ref, h_ref, ybn_ref):
        n, _ = _rms_stats(yb_ref[...])
        ybn = (n * g_ref[...]).astype(MXU_DTYPE)
        ybn_ref[...] = ybn
        h_ref[...] = x_ref[...] + _dot(ya_ref[...], w_ref[:WIDTH_A, :]) + _dot(ybn, w_ref[WIDTH_A:, :])

    return pl.pallas_call(
        body, name="out_fwd", grid=(s // tm,),
        in_specs=[_rows(tm, WIDTH_A), _rows(tm, WIDTH_B), _full((1, WIDTH_B)), _full((D_MODEL, D_MODEL)),
                  _rows(tm, D_MODEL)],
        out_specs=[_rows(tm, D_MODEL), _rows(tm, WIDTH_B)],
        out_shape=[jax.ShapeDtypeStruct((s, D_MODEL), F32), jax.ShapeDtypeStruct((s, WIDTH_B), MXU_DTYPE)],
        compiler_params=_params(("arbitrary",)),
    )(ya_n, y_b, g_b, w_out, x)


def _ffn_fwd(h1, g_ffn, w_gate, w_up, w_down):
    s = h1.shape[0]
    tm = 256

    def body(h_ref, g_ref, wg_ref, wu_ref, wd_ref, o_ref, gate_ref, up_ref, hn_ref):
        h = h_ref[...]
        n, _ = _rms_stats(h)
        hn = (n * g_ref[...]).astype(MXU_DTYPE)
        hn_ref[...] = hn
        out = h
        for c in range(2):
            cols = slice(c * FF_HALF, (c + 1) * FF_HALF)
            gate = _dot(hn, wg_ref[:, cols])
            up = _dot(hn, wu_ref[:, cols])
            gate_ref[:, cols] = gate.astype(MXU_DTYPE)
            up_ref[:, cols] = up.astype(MXU_DTYPE)
            act = (gate * jax.nn.sigmoid(gate) * up).astype(MXU_DTYPE)
            out = out + _dot(act, wd_ref[cols, :])
        o_ref[...] = out

    return pl.pallas_call(
        body, name="ffn_fwd", grid=(s // tm,),
        in_specs=[_rows(tm, D_MODEL), _full((1, D_MODEL)), _full((D_MODEL, D_FF)), _full((D_MODEL, D_FF)),
                  _full((D_FF, D_MODEL))],
        out_specs=[_rows(tm, D_MODEL), _rows(tm, D_FF), _rows(tm, D_FF), _rows(tm, D_MODEL)],
        out_shape=[jax.ShapeDtypeStruct((s, D_MODEL), F32), jax.ShapeDtypeStruct((s, D_FF), MXU_DTYPE),
                   jax.ShapeDtypeStruct((s, D_FF), MXU_DTYPE), jax.ShapeDtypeStruct((s, D_MODEL), MXU_DTYPE)],
        compiler_params=_params(("arbitrary",)),
    )(h1, g_ffn, w_gate, w_up, w_down)


def _ple_loss(h2, p, target, g_ple, w_pg, w_pg_t, w_pp, g_final):
    s = h2.shape[0]
    tm = 256

    def body(h_ref, p_ref, t_ref, gp_ref, wg_ref, wgt_ref, wp_ref, gf_ref,
             loss_ref, dh_ref, dz_ref, dpp_ref, hn_ref, dgp_ref, dgf_ref):
        @pl.when(pl.program_id(0) == 0)
        def _():
            loss_ref[...] = jnp.zeros_like(loss_ref)
            dgp_ref[...] = jnp.zeros_like(dgp_ref)
            dgf_ref[...] = jnp.zeros_like(dgf_ref)

        h2t = h_ref[...]
        n2, r2 = _rms_stats(h2t)
        hn = (n2 * gp_ref[...]).astype(MXU_DTYPE)
        hn_ref[...] = hn
        gate = jax.nn.sigmoid(_dot(hn, wg_ref[...]))
        pp = _dot(p_ref[...].astype(MXU_DTYPE), wp_ref[...])
        h3 = h2t + gate * pp
        n3, r3 = _rms_stats(h3)
        diff = n3 * gf_ref[...] - t_ref[...]
        loss_ref[...] += jnp.full(loss_ref.shape, 0.5 * jnp.sum(diff * diff) / D_MODEL, F32)
        dy = diff * (1.0 / D_MODEL)
        dgf_ref[...] += jnp.sum(dy * n3, axis=0, keepdims=True)
        dh3 = _rms_bwd(dy * gf_ref[...], n3, r3)
        dpp_ref[...] = (dh3 * gate).astype(MXU_DTYPE)
        dz = (dh3 * pp * gate * (1.0 - gate)).astype(MXU_DTYPE)
        dz_ref[...] = dz
        dhn = _dot(dz, wgt_ref[...])
        dgp_ref[...] += jnp.sum(dhn * n2, axis=0, keepdims=True)
        dh_ref[...] = dh3 + _rms_bwd(dhn * gp_ref[...], n2, r2)

    return pl.pallas_call(
        body, name="ple_loss", grid=(s // tm,),
        in_specs=[_rows(tm, D_MODEL), _rows(tm, PLE_DIM), _rows(tm, D_MODEL), _full((1, D_MODEL)),
                  _full((D_MODEL, D_MODEL)), _full((D_MODEL, D_MODEL)), _full((PLE_DIM, D_MODEL)),
                  _full((1, D_MODEL))],
        out_specs=[_full((1, LANES)), _rows(tm, D_MODEL), _rows(tm, D_MODEL), _rows(tm, D_MODEL),
                   _rows(tm, D_MODEL), _full((1, D_MODEL)), _full((1, D_MODEL))],
        out_shape=[jax.ShapeDtypeStruct((1, LANES), F32), jax.ShapeDtypeStruct((s, D_MODEL), F32),
                   jax.ShapeDtypeStruct((s, D_MODEL), MXU_DTYPE), jax.ShapeDtypeStruct((s, D_MODEL), MXU_DTYPE),
                   jax.ShapeDtypeStruct((s, D_MODEL), MXU_DTYPE), jax.ShapeDtypeStruct((1, D_MODEL), F32),
                   jax.ShapeDtypeStruct((1, D_MODEL), F32)],
        compiler_params=_params(("arbitrary",)),
    )(h2, p, target, g_ple, w_pg, w_pg_t, w_pp, g_final)


def _ffn_bwd(dh2, h1, gate, up, g_ffn, w_down_t, w_gate_t, w_up_t):
    s = h1.shape[0]
    tm = 256

    def body(dh_ref, h_ref, gate_ref, up_ref, g_ref, wdt_ref, wgt_ref, wut_ref,
             o_ref, act_ref, dg_ref, du_ref, dgn_ref):
        @pl.when(pl.program_id(0) == 0)
        def _():
            dgn_ref[...] = jnp.zeros_like(dgn_ref)

        dh = dh_ref[...]
        dhb = dh.astype(MXU_DTYPE)
        dhn = jnp.zeros((tm, D_MODEL), F32)
        for c in range(2):
            cols = slice(c * FF_HALF, (c + 1) * FF_HALF)
            dact = _dot(dhb, wdt_ref[:, cols])
            g = gate_ref[:, cols].astype(F32)
            u = up_ref[:, cols].astype(F32)
            sg = jax.nn.sigmoid(g)
            silu = g * sg
            act_ref[:, cols] = (silu * u).astype(MXU_DTYPE)
            du = (dact * silu).astype(MXU_DTYPE)
            dg = (dact * u * sg * (1.0 + g * (1.0 - sg))).astype(MXU_DTYPE)
            du_ref[:, cols] = du
            dg_ref[:, cols] = dg
            dhn = dhn + _dot(dg, wgt_ref[cols, :]) + _dot(du, wut_ref[cols, :])
        n, r = _rms_stats(h_ref[...])
        dgn_ref[...] += jnp.sum(dhn * n, axis=0, keepdims=True)
        o_ref[...] = dh + _rms_bwd(dhn * g_ref[...], n, r)

    return pl.pallas_call(
        body, name="ffn_bwd", grid=(s // tm,),
        in_specs=[_rows(tm, D_MODEL), _rows(tm, D_MODEL), _rows(tm, D_FF), _rows(tm, D_FF), _full((1, D_MODEL)),
                  _full((D_MODEL, D_FF)), _full((D_FF, D_MODEL)), _full((D_FF, D_MODEL))],
        out_specs=[_rows(tm, D_MODEL), _rows(tm, D_FF), _rows(tm, D_FF), _rows(tm, D_FF), _full((1, D_MODEL))],
        out_shape=[jax.ShapeDtypeStruct((s, D_MODEL), F32), jax.ShapeDtypeStruct((s, D_FF), MXU_DTYPE),
                   jax.ShapeDtypeStruct((s, D_FF), MXU_DTYPE), jax.ShapeDtypeStruct((s, D_FF), MXU_DTYPE),
                   jax.ShapeDtypeStruct((1, D_MODEL), F32)],
        compiler_params=_params(("arbitrary",)),
    )(dh2, h1, gate, up, g_ffn, w_down_t, w_gate_t, w_up_t)


def _out_bwd(dh1, y_b, g_b, w_out_t):
    s = dh1.shape[0]
    tm = 512

    def body(dh_ref, yb_ref, g_ref, wt_ref, dya_ref, dyb_ref, dg_ref):
        @pl.when(pl.program_id(0) == 0)
        def _():
            dg_ref[...] = jnp.zeros_like(dg_ref)

        dy = _dot(dh_ref[...].astype(MXU_DTYPE), wt_ref[...])
        dya_ref[...] = dy[:, :WIDTH_A]
        dyb = dy[:, WIDTH_A:]
        n, r = _rms_stats(yb_ref[...])
        dg_ref[...] += jnp.sum(dyb * n, axis=0, keepdims=True)
        dyb_ref[...] = _rms_bwd(dyb * g_ref[...], n, r)

    return pl.pallas_call(
        body, name="out_bwd", grid=(s // tm,),
        in_specs=[_rows(tm, D_MODEL), _rows(tm, WIDTH_B), _full((1, WIDTH_B)), _full((D_MODEL, D_MODEL))],
        out_specs=[_rows(tm, WIDTH_A), _rows(tm, WIDTH_B), _full((1, WIDTH_B))],
        out_shape=[jax.ShapeDtypeStruct((s, WIDTH_A), F32), jax.ShapeDtypeStruct((s, WIDTH_B), F32),
                   jax.ShapeDtypeStruct((1, WIDTH_B), F32)],
        compiler_params=_params(("arbitrary",)),
    )(dh1, y_b, g_b, w_out_t)


def _attn_bwd_branch(q, k, v, do, o, lse, grads, dil):
    s = q.shape[0]
    rows = s // dil
    nb = rows // BLOCK
    first = grads is None
    view = lambda a, w: a.reshape(rows, dil * w)

    def body(*refs):
        q_ref, kc_ref, kp_ref, vc_ref, vp_ref, do_ref, o_ref, lse_ref = refs[:8]
        if first:
            rest = refs[8:]
        else:
            dq_in, dk_in, dv_in = refs[8:11]
            rest = refs[11:]
        dq_ref, dk_ref, dv_ref, dk_carry, dv_carry = rest
        n = pl.program_id(1)

        @pl.when(n == 0)
        def _():
            dk_carry[...] = jnp.zeros_like(dk_carry)
            dv_carry[...] = jnp.zeros_like(dv_carry)

        @pl.when(n < nb)
        def _():
            valid = _band_mask(n)
            lane = lax.broadcasted_iota(jnp.int32, (BLOCK, LANES), 1)
            lo = lane < HEAD_DIM
            masks = _half_masks(MXU_DTYPE)
            masks_f = _half_masks(F32)
            for hp in range(HEADS_B // 2):
                cols = slice(hp * LANES, (hp + 1) * LANES)
                qp = q_ref[:, cols]
                kcat = jnp.concatenate([kp_ref[:, cols], kc_ref[:, cols]], axis=0)
                vcat = jnp.concatenate([vp_ref[:, cols], vc_ref[:, cols]], axis=0)
                dop = do_ref[:, cols]
                prod = dop * o_ref[:, cols]
                dkc = jnp.zeros((2 * BLOCK, LANES), F32)
                dvc = jnp.zeros((2 * BLOCK, LANES), F32)
                pair = []
                for j in range(2):
                    h = 2 * hp + j
                    delta = jnp.sum(prod * masks_f[j], axis=1, keepdims=True)
                    qj = qp * masks[j]
                    doj = (dop * masks_f[j]).astype(MXU_DTYPE)
                    sc = _dot_nt(qj, kcat)
                    p = jnp.where(valid, jnp.exp(sc - lse_ref[:, h:h + 1]), 0.0)
                    ds = (p * (_dot_nt(doj, vcat) - delta)).astype(MXU_DTYPE)
                    dvc = dvc + _dot_tn(p.astype(MXU_DTYPE), doj)
                    dkc = dkc + _dot_tn(ds, qj)
                    pair.append(_dot(ds, kcat))
                dq = jnp.where(lo, pair[0], pair[1])
                dk_prev = dk_carry[:, cols] + dkc[:BLOCK]
                dv_prev = dv_carry[:, cols] + dvc[:BLOCK]
                if not first:
                    dq = dq + dq_in[:, cols]
                    dk_prev = dk_prev + dk_in[:, cols]
                    dv_prev = dv_prev + dv_in[:, cols]
                dq_ref[:, cols] = dq
                dk_ref[:, cols] = dk_prev
                dv_ref[:, cols] = dv_prev
                dk_carry[:, cols] = dkc[BLOCK:]
                dv_carry[:, cols] = dvc[BLOCK:]

        @pl.when(n == nb)
        def _():
            if first:
                dk_ref[...] = dk_carry[...]
                dv_ref[...] = dv_carry[...]
            else:
                dk_ref[...] = dk_carry[...] + dk_in[...]
                dv_ref[...] = dv_carry[...] + dv_in[...]

    cur = lambda w: pl.BlockSpec((BLOCK, w), lambda r, n: (jnp.minimum(n, nb - 1), r))
    prev = lambda w: pl.BlockSpec((BLOCK, w), lambda r, n: (jnp.maximum(jnp.minimum(n, nb - 1) - 1, 0), r))
    late = lambda w: pl.BlockSpec((BLOCK, w), lambda r, n: (jnp.maximum(n - 1, 0), r))
    in_specs = [cur(WIDTH_B), cur(WIDTH_B), prev(WIDTH_B), cur(WIDTH_B), prev(WIDTH_B), cur(WIDTH_B),
                cur(WIDTH_B), cur(LANES)]
    args = [view(q, WIDTH_B), view(k, WIDTH_B), view(k, WIDTH_B), view(v, WIDTH_B), view(v, WIDTH_B),
            view(do, WIDTH_B), view(o, WIDTH_B), view(lse, LANES)]
    if not first:
        in_specs += [cur(WIDTH_B), late(WIDTH_B), late(WIDTH_B)]
        args += [view(g, WIDTH_B) for g in grads]
    res = pl.pallas_call(
        body, name="attn_bwd_d%d" % dil, grid=(dil, nb + 1), in_specs=in_specs,
        out_specs=[cur(WIDTH_B), late(WIDTH_B), late(WIDTH_B)],
        out_shape=[jax.ShapeDtypeStruct((rows, dil * WIDTH_B), F32)] * 3,
        scratch_shapes=[pltpu.VMEM((BLOCK, WIDTH_B), F32), pltpu.VMEM((BLOCK, WIDTH_B), F32)],
        compiler_params=_params(("arbitrary", "arbitrary")),
    )(*args)
    return tuple(a.reshape(s, WIDTH_B) for a in res)


def _sgu_bwd(uv, dya_n, w_tril, w_tril_t, bias, g_sgu, g_a):
    s = uv.shape[0]
    tm = 512

    def body(uv_ref, dy_ref, w_ref, wt_ref, b_ref, gs_ref, ga_ref, duv_ref, dw_ref, db_ref, dgs_ref, dga_ref,
             db_acc):
        i = pl.program_id(0)

        @pl.when(i == 0)
        def _():
            dw_ref[...] = jnp.zeros_like(dw_ref)
            dgs_ref[...] = jnp.zeros_like(dgs_ref)
            dga_ref[...] = jnp.zeros_like(dga_ref)
            db_acc[...] = jnp.zeros_like(db_acc)

        t = _sgu_forward_tile(uv_ref[...], w_ref, b_ref[...], gs_ref[...])
        na, ra = _rms_stats(t['ya'])
        dyn = dy_ref[...]
        dga_ref[...] += jnp.sum(dyn * na, axis=0, keepdims=True)
        dya = _rms_bwd(dyn * ga_ref[...], na, ra)
        dug = dya * t['mixed']
        dmixed = dya * t['ug']
        dmb = dmixed.astype(MXU_DTYPE)
        masks = _half_masks(MXU_DTYPE)
        chunks = []
        db = jnp.zeros((CHUNK, WIDTH_A), F32)
        for c in range(tm // CHUNK):
            rows = slice(c * CHUNK, (c + 1) * CHUNK)
            db = db + dmixed[rows]
            groups = []
            for gp in range(2):
                cols = slice(gp * LANES, (gp + 1) * LANES)
                dm_g = dmb[rows, cols]
                vn_g = t['vn'][rows, cols]
                dvn_g = jnp.zeros((CHUNK, LANES), F32)
                for j in range(2):
                    dm_h = dm_g * masks[j]
                    dvn_g = dvn_g + _dot(wt_ref[2 * gp + j], dm_h)
                    dw_ref[2 * gp + j] += _dot_nt(dm_h, vn_g)
                groups.append(dvn_g)
            chunks.append(jnp.concatenate(groups, axis=1))
        db_acc[...] += db
        dvn = jnp.concatenate(chunks, axis=0)
        vhat = t['vhat']
        dgs_ref[...] += jnp.sum(dvn * vhat, axis=0, keepdims=True)
        dvh = dvn * gs_ref[...]
        dvg = t['rs'] * (dvh - jnp.mean(dvh, axis=-1, keepdims=True)
                         - vhat * jnp.mean(dvh * vhat, axis=-1, keepdims=True))
        duv_ref[:, :WIDTH_A] = (dug * _gelu_grad(t['u'], t['tu'])).astype(MXU_DTYPE)
        duv_ref[:, WIDTH_A:] = (dvg * _gelu_grad(t['v'], t['tv'])).astype(MXU_DTYPE)

        @pl.when(i == pl.num_programs(0) - 1)
        def _():
            lane_a = lax.broadcasted_iota(jnp.int32, (CHUNK, WIDTH_A), 1)
            lane = lax.broadcasted_iota(jnp.int32, (CHUNK, LANES), 1)
            acc = db_acc[...]
            out = jnp.zeros((CHUNK, LANES), F32)
            for h in range(HEADS_A):
                col = jnp.sum(jnp.where(lane_a // HEAD_DIM == h, acc, 0.0), axis=1, keepdims=True)
                out = jnp.where(lane == h, col, out)
            db_ref[...] = out
            causal = (lax.broadcasted_iota(jnp.int32, (CHUNK, CHUNK), 0)
                      >= lax.broadcasted_iota(jnp.int32, (CHUNK, CHUNK), 1))
            for h in range(HEADS_A):
                dw_ref[h] = jnp.where(causal, dw_ref[h], 0.0)

    return pl.pallas_call(
        body, name="sgu_bwd", grid=(s // tm,),
        in_specs=[_rows(tm, 2 * WIDTH_A), _rows(tm, WIDTH_A), _full((HEADS_A, CHUNK, CHUNK)),
                  _full((HEADS_A, CHUNK, CHUNK)), _full((CHUNK, WIDTH_A)), _full((1, WIDTH_A)),
                  _full((1, WIDTH_A))],
        out_specs=[_rows(tm, 2 * WIDTH_A), _full((HEADS_A, CHUNK, CHUNK)), _full((CHUNK, LANES)),
                   _full((1, WIDTH_A)), _full((1, WIDTH_A))],
        out_shape=[jax.ShapeDtypeStruct((s, 2 * WIDTH_A), MXU_DTYPE),
                   jax.ShapeDtypeStruct((HEADS_A, CHUNK, CHUNK), F32), jax.ShapeDtypeStruct((CHUNK, LANES), F32),
                   jax.ShapeDtypeStruct((1, WIDTH_A), F32), jax.ShapeDtypeStruct((1, WIDTH_A), F32)],
        scratch_shapes=[pltpu.VMEM((CHUNK, WIDTH_A), F32)],
        compiler_params=_params(("arbitrary",)),
    )(uv, dya_n, w_tril, w_tril_t, bias, g_sgu, g_a)


def _in_bwd(duv, dq, dk, dv, cos_t, sin_t, w_in_t, x, g_mix, dh1):
    s = x.shape[0]
    tm = 512

    def body(duv_ref, dq_ref, dk_ref, dv_ref, cos_ref, sin_ref, wt_ref, x_ref, g_ref, dh_ref,
             gx_ref, dp_ref, dg_ref):
        @pl.when(pl.program_id(0) == 0)
        def _():
            dg_ref[...] = jnp.zeros_like(dg_ref)

        cos = cos_ref[...]
        sin = sin_ref[...]
        dp_ref[:, :2 * WIDTH_A] = duv_ref[...]
        for i in range(WIDTH_B // LANES):
            cols = slice(i * LANES, (i + 1) * LANES)
            lo = 2 * WIDTH_A + i * LANES
            tq = dq_ref[:, cols] * (HEAD_DIM ** -0.5)
            tk = dk_ref[:, cols]
            dp_ref[:, lo:lo + LANES] = (tq * cos + _rope_partner(tq * sin)).astype(MXU_DTYPE)
            dp_ref[:, lo + WIDTH_B:lo + WIDTH_B + LANES] = (tk * cos + _rope_partner(tk * sin)).astype(MXU_DTYPE)
        dp_ref[:, 2 * WIDTH_A + 2 * WIDTH_B:] = dv_ref[...].astype(MXU_DTYPE)
        dhn = _dot(dp_ref[...], wt_ref[...])
        n, r = _rms_stats(x_ref[...])
        dg_ref[...] += jnp.sum(dhn * n, axis=0, keepdims=True)
        gx_ref[...] = dh_ref[...] + _rms_bwd(dhn * g_ref[...], n, r)

    return pl.pallas_call(
        body, name="in_bwd", grid=(s // tm,),
        in_specs=[_rows(tm, 2 * WIDTH_A), _rows(tm, WIDTH_B), _rows(tm, WIDTH_B), _rows(tm, WIDTH_B),
                  _rows(tm, LANES), _rows(tm, LANES), _full((IN_COLS, D_MODEL)), _rows(tm, D_MODEL),
                  _full((1, D_MODEL)), _rows(tm, D_MODEL)],
        out_specs=[_rows(tm, D_MODEL), _rows(tm, IN_COLS), _full((1, D_MODEL))],
        out_shape=[jax.ShapeDtypeStruct((s, D_MODEL), F32), jax.ShapeDtypeStruct((s, IN_COLS), MXU_DTYPE),
                   jax.ShapeDtypeStruct((1, D_MODEL), F32)],
        compiler_params=_params(("arbitrary",)),
    )(duv, dq, dk, dv, cos_t, sin_t, w_in_t, x, g_mix, dh1)


def _wgrad(a, bs, name):
    s, m = a.shape
    bm = 512 if m % 512 == 0 else (FF_HALF if m == D_FF else m)
    ts = 512
    nsteps = s // ts

    def body(a_ref, *refs):
        b_refs, o_refs = refs[:len(bs)], refs[len(bs):]
        kk = pl.program_id(1)
        at = a_ref[...].astype(MXU_DTYPE)
        for b_ref, o_ref in zip(b_refs, o_refs):
            c = _dot_tn(at, b_ref[...].astype(MXU_DTYPE))

            @pl.when(kk == 0)
            def _():
                o_ref[...] = c

            @pl.when(kk > 0)
            def _():
                o_ref[...] += c

    return pl.pallas_call(
        body, name=name, grid=(m // bm, nsteps),
        in_specs=[pl.BlockSpec((ts, bm), lambda i, kk: (kk, i))]
        + [pl.BlockSpec((ts, b.shape[1]), lambda i, kk: (kk, 0)) for b in bs],
        out_specs=[pl.BlockSpec((bm, b.shape[1]), lambda i, kk: (i, 0)) for b in bs],
        out_shape=[jax.ShapeDtypeStruct((m, b.shape[1]), F32) for b in bs],
        compiler_params=_params(("arbitrary", "arbitrary")),
    )(a, *bs)


def _rope_tables(s):
    half = HEAD_DIM // 2
    inv = ROPE_THETA ** (-jnp.arange(half, dtype=F32) / half)
    ang = jnp.arange(s, dtype=F32)[:, None] * inv[None, :]
    cos = jnp.cos(ang)
    sin = jnp.sin(ang)
    cos_t = jnp.concatenate([cos, cos, cos, cos], axis=1)
    sin_t = jnp.concatenate([-sin, sin, -sin, sin], axis=1)
    return cos_t, sin_t


def _local_step(x, p, target, w, small):
    s = x.shape[0]
    cos_t, sin_t = _rope_tables(s)
    tril = jnp.tril(jnp.ones((CHUNK, CHUNK), F32))
    w_tril = (small['sgu_w'].reshape(HEADS_A, CHUNK, CHUNK) * tril).astype(MXU_DTYPE)
    w_tril_t = jnp.swapaxes(w_tril, 1, 2)
    bias = jnp.repeat(small['sgu_b'].reshape(HEADS_A, CHUNK).T, HEAD_DIM, axis=1)
    g = {n: small[n].reshape(1, -1) for n in SMALL if n not in ('sgu_w', 'sgu_b')}

    uv, q, k, v, hn1 = _in_fwd(x, g['mix_norm_g'], w['w_in'], cos_t, sin_t)
    ya_n = _sgu_fwd(uv, w_tril, bias, g['sgu_norm_g'], g['out_norm_a'])
    state = None
    for i, dil in enumerate(DILATIONS):
        state = _attn_fwd_branch(q, k, v, state, dil, last=(i == len(DILATIONS) - 1))
    y_b, lse = state
    h1, yb_n = _out_fwd(ya_n, y_b, g['out_norm_b'], w['w_out'], x)
    h2, gate, up, hn2 = _ffn_fwd(h1, g['ffn_norm_g'], w['w_gate'], w['w_up'], w['w_down'])
    loss, dh2, dz, dpp, hn3, d_ple_g, d_final_g = _ple_loss(
        h2, p, target, g['ple_norm_g'], w['w_ple_gate'], w['w_ple_gate'].T, w['w_ple_proj'], g['final_norm_g'])

    dh1, act, dgate, dup, d_ffn_g = _ffn_bwd(dh2, h1, gate, up, g['ffn_norm_g'], w['w_down'].T, w['w_gate'].T,
                                             w['w_up'].T)
    dya_n, dyb, d_out_b = _out_bwd(dh1, y_b, g['out_norm_b'], w['w_out'].T)
    grads = None
    for dil in DILATIONS:
        grads = _attn_bwd_branch(q, k, v, dyb, y_b, lse, grads, dil)
    duv, d_sgu_w, d_sgu_b, d_sgu_g, d_out_a = _sgu_bwd(uv, dya_n, w_tril, w_tril_t, bias, g['sgu_norm_g'],
                                                       g['out_norm_a'])
    grad_x, dproj, d_mix_g = _in_bwd(duv, grads[0], grads[1], grads[2], cos_t, sin_t, w['w_in'].T, x,
                                     g['mix_norm_g'], dh1)

    gw = {}
    gw['w_in'], = _wgrad(hn1, [dproj], "wgrad_in")
    gw_out_a, = _wgrad(ya_n, [dh1], "wgrad_out_a")
    gw_out_b, = _wgrad(yb_n, [dh1], "wgrad_out_b")
    gw['w_out'] = jnp.concatenate([gw_out_a, gw_out_b], axis=0)
    gw['w_gate'], gw['w_up'] = _wgrad(hn2, [dgate, dup], "wgrad_gate_up")
    gw['w_down'], = _wgrad(act, [dh2], "wgrad_down")
    gw['w_ple_gate'], = _wgrad(hn3, [dz], "wgrad_ple_gate")
    gw['w_ple_proj'], = _wgrad(p, [dpp], "wgrad_ple_proj")

    gs = {
        'mix_norm_g': d_mix_g, 'sgu_w': d_sgu_w, 'sgu_b': d_sgu_b[:, :HEADS_A].T, 'sgu_norm_g': d_sgu_g,
        'out_norm_a': d_out_a, 'out_norm_b': d_out_b, 'ffn_norm_g': d_ffn_g, 'ple_norm_g': d_ple_g,
        'final_norm_g': d_final_g,
    }
    return loss, grad_x, gw, gs


MESH = pl.DeviceIdType.MESH
ANY = pl.BlockSpec(memory_space=pl.ANY)
PACK_ROWS = sum(FULL_SHAPES[n][0] * FULL_SHAPES[n][1] for n, _ in SHARDED) // N_CHIPS // D_MODEL
HALF_ROWS = PACK_ROWS // 2


def _place():
    x, y, c = lax.axis_index("x"), lax.axis_index("y"), lax.axis_index("c")
    other_chips = [(1 - x, y), (x, 1 - y), (1 - x, 1 - y)]
    return x, y, c, other_chips


def _all_gather_weights(packed):
    def body(x_ref, out_ref, send_sems, recv_sems):
        x, y, c, chips = _place()
        sibling = (x, y, 1 - c)

        def half(chip, hc):
            return out_ref.at[2 * chip[0] + chip[1], pl.ds(hc * HALF_ROWS, HALF_ROWS), :]

        def copy(k, src, dst, to):
            return pltpu.make_async_remote_copy(src_ref=src, dst_ref=dst, send_sem=send_sems.at[k],
                                                recv_sem=recv_sems.at[k], device_id=to, device_id_type=MESH)

        my_half = x_ref.at[pl.ds(c * HALF_ROWS, HALF_ROWS), :]
        first = [copy(j, my_half, half((x, y), c), (*chip, c)) for j, chip in enumerate(chips)]
        for cp in first:
            cp.start()
        passed = [copy(3 + j, half(chip, c), half(chip, c), sibling) for j, chip in enumerate(chips)]
        for j, chip in enumerate(chips):
            copy(j, my_half, half(chip, c), (*chip, c)).wait_recv()
            passed[j].start()
        for j, chip in enumerate(chips):
            copy(3 + j, my_half, half(chip, 1 - c), sibling).wait_recv()
        for cp in first + passed:
            cp.wait_send()

    gathered = pl.pallas_call(
        body, name="all_gather_weights",
        out_shape=jax.ShapeDtypeStruct((N_CHIPS,) + packed.shape, packed.dtype),
        in_specs=[ANY], out_specs=ANY,
        scratch_shapes=[pltpu.SemaphoreType.DMA((6,)), pltpu.SemaphoreType.DMA((6,))],
    )(packed)
    me = 2 * lax.axis_index("x") + lax.axis_index("y")
    return lax.dynamic_update_slice(gathered, packed[None], (me, 0, 0))


def _rs_to_sibling(g):
    def body(g_ref, r_ref, send_sem, recv_sem):
        x, y, c, _ = _place()
        cp = pltpu.make_async_remote_copy(
            src_ref=g_ref.at[:, pl.ds((1 - c) * HALF_ROWS, HALF_ROWS), :], dst_ref=r_ref, send_sem=send_sem,
            recv_sem=recv_sem, device_id=(x, y, 1 - c), device_id_type=MESH)
        cp.start()
        cp.wait()

    return pl.pallas_call(
        body, name="rs_to_sibling", out_shape=jax.ShapeDtypeStruct((N_CHIPS, HALF_ROWS, D_MODEL), g.dtype),
        in_specs=[ANY], out_specs=ANY,
        scratch_shapes=[pltpu.SemaphoreType.DMA, pltpu.SemaphoreType.DMA],
    )(g)


def _rs_add(g, r, c):
    tr = HALF_ROWS // 2
    spec = pltpu.PrefetchScalarGridSpec(
        num_scalar_prefetch=1, grid=(N_CHIPS, 2),
        in_specs=[pl.BlockSpec((1, tr, D_MODEL), lambda j, i, c_ref: (j, 2 * c_ref[0] + i, 0)),
                  pl.BlockSpec((1, tr, D_MODEL), lambda j, i, c_ref: (j, i, 0))],
        out_specs=pl.BlockSpec((1, tr, D_MODEL), lambda j, i, c_ref: (j, i, 0)))

    def body(c_ref, g_ref, r_ref, o_ref):
        o_ref[...] = (g_ref[...] + r_ref[...]).astype(o_ref.dtype)

    return pl.pallas_call(
        body, name="rs_add", grid_spec=spec,
        out_shape=jax.ShapeDtypeStruct((N_CHIPS, HALF_ROWS, D_MODEL), jnp.bfloat16),
        compiler_params=_params(("arbitrary", "arbitrary")),
    )(c.reshape(1).astype(jnp.int32), g, r)


def _rs_between_chips(part):
    def body(p_ref, out_ref, send_sems, recv_sems):
        x, y, c, chips = _place()
        me = 2 * x + y
        sends = []
        for j, chip in enumerate(chips):
            cp = pltpu.make_async_remote_copy(
                src_ref=p_ref.at[2 * chip[0] + chip[1]], dst_ref=out_ref.at[me], send_sem=send_sems.at[j],
                recv_sem=recv_sems.at[j], device_id=(*chip, c), device_id_type=MESH)
            cp.start()
            sends.append(cp)
        for j, chip in enumerate(chips):
            pltpu.make_async_remote_copy(
                src_ref=p_ref.at[me], dst_ref=out_ref.at[2 * chip[0] + chip[1]], send_sem=send_sems.at[j],
                recv_sem=recv_sems.at[j], device_id=(*chip, c), device_id_type=MESH).wait_recv()
        for cp in sends:
            cp.wait_send()

    got = pl.pallas_call(
        body, name="rs_between_chips", out_shape=jax.ShapeDtypeStruct(part.shape, part.dtype),
        in_specs=[ANY], out_specs=ANY,
        scratch_shapes=[pltpu.SemaphoreType.DMA((3,)), pltpu.SemaphoreType.DMA((3,))],
    )(part)
    me = 2 * lax.axis_index("x") + lax.axis_index("y")
    own = lax.dynamic_slice_in_dim(part, me, 1, axis=0)
    return lax.dynamic_update_slice(got, own, (me, 0, 0))


def _rs_sum(parts):
    tr = HALF_ROWS // 2

    def body(p_ref, o_ref):
        acc = p_ref[0].astype(F32)
        for i in range(1, N_CHIPS):
            acc = acc + p_ref[i].astype(F32)
        o_ref[...] = acc

    return pl.pallas_call(
        body, name="rs_sum", grid=(2,),
        in_specs=[pl.BlockSpec((N_CHIPS, tr, D_MODEL), lambda i: (0, i, 0))],
        out_specs=pl.BlockSpec((tr, D_MODEL), lambda i: (i, 0)),
        out_shape=jax.ShapeDtypeStruct((HALF_ROWS, D_MODEL), F32),
        compiler_params=_params(("arbitrary",)),
    )(parts)


def _rs_swap(red):
    def body(r_ref, out_ref, send_sem, recv_sem):
        x, y, c, _ = _place()
        cp = pltpu.make_async_remote_copy(src_ref=r_ref, dst_ref=out_ref, send_sem=send_sem, recv_sem=recv_sem,
                                          device_id=(x, y, 1 - c), device_id_type=MESH)
        cp.start()
        cp.wait()

    other = pl.pallas_call(
        body, name="rs_swap", out_shape=jax.ShapeDtypeStruct(red.shape, red.dtype),
        in_specs=[ANY], out_specs=ANY,
        scratch_shapes=[pltpu.SemaphoreType.DMA, pltpu.SemaphoreType.DMA],
    )(red)
    south = lax.axis_index("c") == 0
    return jnp.concatenate([jnp.where(south, red, other), jnp.where(south, other, red)], axis=0)


def _small_all_reduce(block):
    rows = block.shape[0]

    def body(x_ref, all_ref, sum_ref, send_sems, recv_sems, local_sem):
        x, y, c, chips = _place()
        me, sibling = (x, y, c), (x, y, 1 - c)

        def blk(px, py, pc):
            return all_ref.at[pl.ds((4 * px + 2 * py + pc) * rows, rows), :]

        def copy(k, who, to, src=None):
            return pltpu.make_async_remote_copy(
                src_ref=blk(*who) if src is None else src, dst_ref=blk(*who), send_sem=send_sems.at[k],
                recv_sem=recv_sems.at[k], device_id=to, device_id_type=MESH)

        mine = pltpu.make_async_copy(x_ref, blk(*me), local_sem)
        mine.start()
        first = [copy(0, me, sibling, src=x_ref)]
        first += [copy(1 + j, me, (*chip, c), src=x_ref) for j, chip in enumerate(chips)]
        for cp in first:
            cp.start()
        passed = [copy(4 + j, (*chip, c), sibling) for j, chip in enumerate(chips)]
        for j, chip in enumerate(chips):
            copy(1 + j, (*chip, c), me).wait_recv()
            passed[j].start()
        copy(0, sibling, me).wait_recv()
        for j, chip in enumerate(chips):
            copy(4 + j, (*chip, 1 - c), me).wait_recv()
        for cp in first + passed:
            cp.wait_send()
        mine.wait()
        acc = all_ref[pl.ds(0, rows), :]
        for dev in range(1, N_DEV):
            acc = acc + all_ref[pl.ds(dev * rows, rows), :]
        sum_ref[...] = acc

    vmem = pl.BlockSpec(memory_space=pltpu.VMEM)
    return pl.pallas_call(
        body, name="small_all_reduce",
        out_shape=[jax.ShapeDtypeStruct((N_DEV * rows, D_MODEL), F32), jax.ShapeDtypeStruct((rows, D_MODEL), F32)],
        in_specs=[vmem], out_specs=[vmem, vmem],
        scratch_shapes=[pltpu.SemaphoreType.DMA((7,)), pltpu.SemaphoreType.DMA((7,)), pltpu.SemaphoreType.DMA],
    )(block)[1]


def _adamw(w, g, m, v, name):
    rows, cols = w.shape
    tm = rows
    if rows > 512:
        tm = next(t for t in range(512, 7, -8) if rows % t == 0)

    def body(w_ref, g_ref, m_ref, v_ref, d_ref, nm_ref, nv_ref):
        g_ = g_ref[...]
        m_ = ADAM_B1 * m_ref[...] + (1.0 - ADAM_B1) * g_
        v_ = ADAM_B2 * v_ref[...] + (1.0 - ADAM_B2) * (g_ * g_)
        m_hat = m_ / (1.0 - ADAM_B1 ** ADAM_STEP)
        v_hat = v_ / (1.0 - ADAM_B2 ** ADAM_STEP)
        d_ref[...] = -ADAM_LR * (m_hat / (jnp.sqrt(v_hat) + ADAM_EPS) + ADAM_WD * w_ref[...])
        nm_ref[...] = m_
        nv_ref[...] = v_

    spec = pl.BlockSpec((tm, cols), lambda i: (i, 0))
    return pl.pallas_call(
        body, name=name, grid=(rows // tm,), in_specs=[spec] * 4, out_specs=[spec] * 3,
        out_shape=[jax.ShapeDtypeStruct(w.shape, F32)] * 3, compiler_params=_params(("arbitrary",)),
    )(w, g, m, v)


def _pack_small(values):
    flat = jnp.concatenate([values[n].reshape(-1).astype(F32) for n in SMALL])
    return jnp.pad(flat, (0, SMALL_ROWS * D_MODEL - flat.shape[0])).reshape(SMALL_ROWS, D_MODEL)


def _unpack_small(block, shapes):
    flat = block.reshape(-1)
    out, lo = {}, 0
    for n in SMALL:
        out[n] = flat[lo:lo + SMALL_SIZES[n]].reshape(shapes[n])
        lo += SMALL_SIZES[n]
    return out


def _shard_rows(name):
    shape = FULL_SHAPES[name]
    return shape[0] * shape[1] // N_CHIPS // D_MODEL


def _unpack_gathered(gathered):
    out, lo = {}, 0
    for name, axis in SHARDED:
        rows, cols = FULL_SHAPES[name]
        part = gathered[:, lo:lo + _shard_rows(name)]
        lo += _shard_rows(name)
        if axis == 0:
            out[name] = part.reshape(rows, cols)
        else:
            out[name] = part.reshape(N_CHIPS, rows, cols // N_CHIPS).transpose(1, 0, 2).reshape(rows, cols)
    return out


def _pack_full_grads(gw):
    parts = []
    for name, axis in SHARDED:
        rows, cols = FULL_SHAPES[name]
        g = gw[name]
        if axis == 1:
            g = g.reshape(rows, N_CHIPS, cols // N_CHIPS).transpose(1, 0, 2)
        parts.append(g.reshape(N_CHIPS, _shard_rows(name), D_MODEL))
    return jnp.concatenate(parts, axis=1)


def kernel(x, p, mix_norm_g, w_in, sgu_w, sgu_b, sgu_norm_g, out_norm_a, out_norm_b, w_out, ffn_norm_g, w_gate, w_up, w_down, ple_norm_g, w_ple_gate, w_ple_proj, final_norm_g, loss_target, m_mix_norm_g, m_w_in, m_sgu_w, m_sgu_b, m_sgu_norm_g, m_out_norm_a, m_out_norm_b, m_w_out, m_ffn_norm_g, m_w_gate, m_w_up, m_w_down, m_ple_norm_g, m_w_ple_gate, m_w_ple_proj, m_final_norm_g, v_mix_norm_g, v_w_in, v_sgu_w, v_sgu_b, v_sgu_norm_g, v_out_norm_a, v_out_norm_b, v_w_out, v_ffn_norm_g, v_w_gate, v_w_up, v_w_down, v_ple_norm_g, v_w_ple_gate, v_w_ple_proj, v_final_norm_g):
    given = dict(locals())
    c = lax.axis_index("c")
    sharded_names = [n for n, _ in SHARDED]

    packed = jnp.concatenate([given[n][0].astype(MXU_DTYPE).reshape(-1, D_MODEL) for n in sharded_names], axis=0)
    w_full = _unpack_gathered(_all_gather_weights(packed))

    small = {n: given[n] for n in SMALL}
    loss, grad_x, gw, gs = _local_step(x[0], p[0, 0], loss_target[0], w_full, small)

    gs_block = _pack_small(gs)
    gs_block = gs_block.at[SMALL_ROWS - 1, 0].set(loss[0, 0])
    small_sum = _small_all_reduce(gs_block)
    loss_out = small_sum[SMALL_ROWS - 1, 0]
    small_shapes = {n: given[n].shape for n in SMALL}
    g_small = _unpack_small(small_sum, small_shapes)

    g_all = _pack_full_grads(gw)
    from_sibling = _rs_to_sibling(g_all)
    chip_part = _rs_add(g_all, from_sibling, c)
    summed_half = _rs_sum(_rs_between_chips(chip_part))
    g_pack = _rs_swap(summed_half)

    grads, deltas, new_m, new_v = {}, {}, {}, {}
    lo = 0
    for n in sharded_names:
        shard_shape = given[n].shape[1:]
        g = g_pack[lo:lo + _shard_rows(n)].reshape(shard_shape)
        lo += _shard_rows(n)
        d, nm, nv = _adamw(given[n][0], g, given["m_" + n][0], given["v_" + n][0], "adamw_" + n)
        grads[n], deltas[n], new_m[n], new_v[n] = g[None], d[None], nm[None], nv[None]

    d, nm, nv = _adamw(_pack_small(small), small_sum, _pack_small({n: given["m_" + n] for n in SMALL}),
                       _pack_small({n: given["v_" + n] for n in SMALL}), "adamw_small")
    for res, blk in ((grads, small_sum), (deltas, d), (new_m, nm), (new_v, nv)):
        res.update(_unpack_small(blk, small_shapes))

    outs = [loss_out, grad_x[None]]
    for res in (grads, deltas, new_m, new_v):
        outs += [res[n] for n in WEIGHT_NAMES]
    return tuple(outs)
```

```python
import functools
import math

import jax
import jax.numpy as jnp
import numpy as np
from jax import lax
from jax.experimental import pallas as pl
from jax.experimental.pallas import tpu as pltpu

F32 = jnp.float32
MXU_DTYPE = jnp.bfloat16

D_MODEL = 1024
HEAD_DIM = 64
HEADS_A = 4
HEADS_B = 12
WIDTH_A = HEADS_A * HEAD_DIM
WIDTH_B = HEADS_B * HEAD_DIM
CHUNK = 128
BLOCK = 128
DILATIONS = (1, 4, 16)
ROPE_THETA = 10000.0
D_FF = 2816
FF_HALF = D_FF // 2
PLE_DIM = 256
IN_COLS = 2 * WIDTH_A + 3 * WIDTH_B
EPS = 1e-6
LANES = 128
N_CHIPS = 4
N_DEV = 8

ADAM_LR = 0.001
ADAM_B1 = 0.9
ADAM_B2 = 0.999
ADAM_EPS = 1e-08
ADAM_WD = 0.01
ADAM_STEP = 10

VMEM_LIMIT = 56 * 1024 * 1024

WEIGHT_NAMES = ['mix_norm_g', 'w_in', 'sgu_w', 'sgu_b', 'sgu_norm_g', 'out_norm_a', 'out_norm_b', 'w_out',
                'ffn_norm_g', 'w_gate', 'w_up', 'w_down', 'ple_norm_g', 'w_ple_gate', 'w_ple_proj', 'final_norm_g']
SHARDED = [('w_in', 1), ('w_out', 0), ('w_gate', 1), ('w_up', 1), ('w_down', 0), ('w_ple_gate', 0), ('w_ple_proj', 1)]
FULL_SHAPES = {'w_in': (D_MODEL, IN_COLS), 'w_out': (D_MODEL, D_MODEL), 'w_gate': (D_MODEL, D_FF),
               'w_up': (D_MODEL, D_FF), 'w_down': (D_FF, D_MODEL), 'w_ple_gate': (D_MODEL, D_MODEL),
               'w_ple_proj': (PLE_DIM, D_MODEL)}
SMALL = ['mix_norm_g', 'sgu_w', 'sgu_b', 'sgu_norm_g', 'out_norm_a', 'out_norm_b', 'ffn_norm_g', 'ple_norm_g',
         'final_norm_g']
SMALL_SIZES = {'mix_norm_g': 1024, 'sgu_w': 65536, 'sgu_b': 512, 'sgu_norm_g': 256, 'out_norm_a': 256,
               'out_norm_b': 768, 'ffn_norm_g': 1024, 'ple_norm_g': 1024, 'final_norm_g': 1024}
SMALL_ROWS = 72


def _params(semantics=None):
    return pltpu.CompilerParams(dimension_semantics=semantics, vmem_limit_bytes=VMEM_LIMIT)


def _full(shape):
    nd = len(shape)
    return pl.BlockSpec(shape, lambda i: (0,) * nd)


def _rows(tm, width):
    return pl.BlockSpec((tm, width), lambda i: (i, 0))


def _rms_stats(x):
    r = lax.rsqrt(jnp.mean(x * x, axis=-1, keepdims=True) + EPS)
    return x * r, r


def _rms_bwd(dn, n, r):
    return r * (dn - n * jnp.mean(dn * n, axis=-1, keepdims=True))


def _dot(a, b):
    return jnp.dot(a, b, preferred_element_type=F32)


def _dot_nt(a, b):
    return lax.dot_general(a, b, (((1,), (1,)), ((), ())), preferred_element_type=F32)


def _dot_tn(a, b):
    return lax.dot_general(a, b, (((0,), (0,)), ((), ())), preferred_element_type=F32)


def _gelu_parts(x):
    c = math.sqrt(2.0 / math.pi)
    t = jnp.tanh(c * (x + 0.044715 * x * x * x))
    return 0.5 * x * (1.0 + t), t


def _gelu_grad(x, t):
    c = math.sqrt(2.0 / math.pi)
    return 0.5 * (1.0 + t) + 0.5 * x * (1.0 - t * t) * c * (1.0 + 3.0 * 0.044715 * x * x)


def _half_masks(dtype):
    lane = lax.broadcasted_iota(jnp.int32, (BLOCK, LANES), 1)
    lo = (lane < HEAD_DIM).astype(F32)
    return lo.astype(dtype), (1.0 - lo).astype(dtype)


def _rope_partner(t):
    lane = lax.broadcasted_iota(jnp.int32, t.shape, 1)
    first_half = (lane % HEAD_DIM) < (HEAD_DIM // 2)
    return jnp.where(first_half, pltpu.roll(t, LANES - HEAD_DIM // 2, 1), pltpu.roll(t, HEAD_DIM // 2, 1))


L_BLOCK = 256
L_GROUP = 16


def _store_l256(scr, out_ref, cols, value):
    tm = value.shape[0]
    scr[...] = value
    for blk in range(tm // L_BLOCK):
        for r in range(L_GROUP):
            lo = blk * L_BLOCK + r * L_GROUP
            piece = scr[pl.ds(blk * L_BLOCK + r, L_GROUP, stride=L_GROUP), :]
            out_ref[lo:lo + L_GROUP, cols] = piece.astype(out_ref.dtype)


def _load_l256(col_refs, tm):
    cols = []
    for ref in col_refs:
        pieces = [ref[pl.ds(blk * L_BLOCK + i, L_GROUP, stride=L_GROUP), :]
                  for blk in range(tm // L_BLOCK) for i in range(L_GROUP)]
        cols.append(jnp.concatenate(pieces, axis=0))
    return jnp.concatenate(cols, axis=1)


def _col_specs(tm, width):
    return [pl.BlockSpec((tm, LANES), lambda i, j=j: (i, j)) for j in range(width // LANES)]


def _in_fwd(x, g_mix, w_in, cos_t, sin_t):
    s = x.shape[0]
    tm = 512

    def body(x_ref, g_ref, w_ref, cos_ref, sin_ref, uv_ref, q_ref, k_ref, v_ref, hn_ref, scr):
        n, _ = _rms_stats(x_ref[...])
        hn = (n * g_ref[...]).astype(MXU_DTYPE)
        hn_ref[...] = hn
        proj = _dot(hn, w_ref[...])
        uv_ref[...] = proj[:, :2 * WIDTH_A]
        cos = cos_ref[...]
        sin = sin_ref[...]
        for i in range(WIDTH_B // LANES):
            lo = 2 * WIDTH_A + i * LANES
            tq = proj[:, lo:lo + LANES]
            tk = proj[:, lo + WIDTH_B:lo + WIDTH_B + LANES]
            tv = proj[:, lo + 2 * WIDTH_B:lo + 2 * WIDTH_B + LANES]
            cols = slice(i * LANES, (i + 1) * LANES)
            _store_l256(scr, q_ref, cols, (tq * cos + _rope_partner(tq) * sin) * (HEAD_DIM ** -0.5))
            _store_l256(scr, k_ref, cols, tk * cos + _rope_partner(tk) * sin)
            _store_l256(scr, v_ref, cols, tv)

    return pl.pallas_call(
        body, name="in_fwd", grid=(s // tm,), scratch_shapes=[pltpu.VMEM((tm, LANES), F32)],
        in_specs=[_rows(tm, D_MODEL), _full((1, D_MODEL)), _full((D_MODEL, IN_COLS)), _rows(tm, LANES),
                  _rows(tm, LANES)],
        out_specs=[_rows(tm, 2 * WIDTH_A), _rows(tm, WIDTH_B), _rows(tm, WIDTH_B), _rows(tm, WIDTH_B),
                   _rows(tm, D_MODEL)],
        out_shape=[jax.ShapeDtypeStruct((s, 2 * WIDTH_A), F32), jax.ShapeDtypeStruct((s, WIDTH_B), MXU_DTYPE),
                   jax.ShapeDtypeStruct((s, WIDTH_B), MXU_DTYPE), jax.ShapeDtypeStruct((s, WIDTH_B), MXU_DTYPE),
                   jax.ShapeDtypeStruct((s, D_MODEL), MXU_DTYPE)],
        compiler_params=_params(("arbitrary",)),
    )(x, g_mix, w_in, cos_t, sin_t)


class _Branch:
    def __init__(self, dil, s):
        self.dil = dil
        i = np.arange(L_GROUP)
        if dil == 16:
            self.grid = (16, s // 2048)
            self.shape = (8, 1, 1, L_GROUP)
            self.index = lambda r, n: (n, r // 4, r % 4, 0, 0)
            pos = (np.arange(8)[:, None] * 16 + i[None, :]).reshape(-1)
        elif dil == 4:
            self.grid = (4, s // 512)
            self.shape = (2, 4, 1, L_GROUP)
            self.index = lambda r, n: (n, 0, r, 0, 0)
            pos = (np.arange(2)[:, None, None] * 64 + np.arange(4)[None, :, None] + 4 * i[None, None, :]).reshape(-1)
        else:
            self.grid = (1, s // L_BLOCK)
            self.shape = (1, 4, 4, L_GROUP)
            self.index = lambda r, n: (n, 0, 0, 0, 0)
            pos = (np.arange(16)[:, None] + 16 * i[None, :]).reshape(-1)
        self.qn = pos.shape[0]
        self.nb = self.grid[1]
        dist = pos[:, None] - np.concatenate([pos - self.qn, pos])[None, :]
        band = (dist >= 0) & (dist <= BLOCK)
        start = band & (np.arange(2 * self.qn)[None, :] >= self.qn)
        self.bias = np.where(np.stack([band, start]), 0.0, -np.inf).astype(np.float32)

    def view(self, a):
        return a.reshape(a.shape[0] // L_BLOCK, 4, 4, L_GROUP, a.shape[1])

    def spec(self, w, step=lambda n: n):
        return pl.BlockSpec(self.shape + (w,), lambda r, n: self.index(r, step(n)))

    def bias_spec(self, step=lambda n: n):
        return pl.BlockSpec((1, self.qn, 2 * self.qn), lambda r, n: (jnp.where(step(n) == 0, 1, 0), 0, 0))

    def load(self, ref, cols=slice(None)):
        x = ref[:, :, :, :, cols]
        return x.reshape(self.qn, x.shape[-1])

    def store(self, ref, cols, value):
        ref[:, :, :, :, cols] = value.reshape(self.shape + (value.shape[-1],))


def _attn_fwd_branch(q, k, v, state, dil, last):
    s = q.shape[0]
    br = _Branch(dil, s)
    qn = br.qn
    first = state is None

    def body(*refs):
        bias_ref, q_ref, kc_ref, kp_ref, vc_ref, vp_ref = refs[:6]
        if first:
            outs = refs[6:]
        else:
            acc_ref, m_ref, l_ref = refs[6:9]
            outs = refs[9:]
            m_in = br.load(m_ref)
            l_in = br.load(l_ref)
        o_ref, m_out = outs[0], outs[1]
        bias2 = jnp.concatenate([bias_ref[0], bias_ref[0]], axis=0)
        lane = lax.broadcasted_iota(jnp.int32, (qn, LANES), 1)
        lo = lane < HEAD_DIM
        mask_lo = lo.astype(F32).astype(MXU_DTYPE)
        masks = (mask_lo, 1 - mask_lo)
        m_blk = jnp.zeros((qn, LANES), F32)
        l_blk = jnp.zeros((qn, LANES), F32)
        for hp in range(HEADS_B // 2):
            cols = slice(hp * LANES, (hp + 1) * LANES)
            qp = br.load(q_ref, cols)
            kcat = jnp.concatenate([br.load(kp_ref, cols), br.load(kc_ref, cols)], axis=0)
            vcat = jnp.concatenate([br.load(vp_ref, cols), br.load(vc_ref, cols)], axis=0)
            h0, h1 = 2 * hp, 2 * hp + 1
            sc = _dot_nt(jnp.concatenate([qp * masks[0], qp * masks[1]], axis=0), kcat) + bias2
            m_new = jnp.max(sc, axis=1, keepdims=True)
            if not first:
                m_prev = jnp.concatenate([m_in[:, h0:h0 + 1], m_in[:, h1:h1 + 1]], axis=0)
                m_new = jnp.maximum(m_prev, m_new)
            p = jnp.exp(sc - m_new)
            l_new = jnp.sum(p, axis=1, keepdims=True)
            acc = _dot(p.astype(MXU_DTYPE), vcat)
            if not first:
                alpha = jnp.exp(m_prev - m_new)
                l_new = alpha * jnp.concatenate([l_in[:, h0:h0 + 1], l_in[:, h1:h1 + 1]], axis=0) + l_new
                acc_in = br.load(acc_ref, cols)
                acc = alpha * jnp.concatenate([acc_in, acc_in], axis=0) + acc
            if last:
                acc = acc / l_new
                m_new = m_new + jnp.log(l_new)
            m_blk = jnp.where(lane == h0, m_new[:qn], jnp.where(lane == h1, m_new[qn:], m_blk))
            l_blk = jnp.where(lane == h0, l_new[:qn], jnp.where(lane == h1, l_new[qn:], l_blk))
            br.store(o_ref, cols, jnp.where(lo, acc[:qn], acc[qn:]))
        br.store(m_out, slice(None), m_blk)
        if not last:
            br.store(outs[2], slice(None), l_blk)

    before = lambda n: jnp.maximum(n - 1, 0)
    in_specs = [br.bias_spec(), br.spec(WIDTH_B), br.spec(WIDTH_B), br.spec(WIDTH_B, before), br.spec(WIDTH_B),
                br.spec(WIDTH_B, before)]
    args = [jnp.asarray(br.bias), br.view(q), br.view(k), br.view(k), br.view(v), br.view(v)]
    if not first:
        in_specs += [br.spec(WIDTH_B), br.spec(LANES), br.spec(LANES)]
        args += [br.view(a) for a in state]
    widths = (WIDTH_B, LANES) if last else (WIDTH_B, LANES, LANES)
    res = pl.pallas_call(
        body, name="attn_fwd_d%d" % dil, grid=br.grid, in_specs=in_specs,
        out_specs=[br.spec(w) for w in widths],
        out_shape=[jax.ShapeDtypeStruct((s // L_BLOCK, 4, 4, L_GROUP, w), F32) for w in widths],
        compiler_params=_params(("arbitrary", "arbitrary")),
    )(*args)
    return tuple(a.reshape(s, w) for a, w in zip(res, widths))


def _sgu_forward_tile(uv, w_ref, bias, g_sgu):
    tm = uv.shape[0]
    u = uv[:, :WIDTH_A]
    v = uv[:, WIDTH_A:]
    ug, tu = _gelu_parts(u)
    vg, tv = _gelu_parts(v)
    mu = jnp.mean(vg, axis=-1, keepdims=True)
    vc = vg - mu
    rs = lax.rsqrt(jnp.mean(vc * vc, axis=-1, keepdims=True) + EPS)
    vhat = vc * rs
    vn = (vhat * g_sgu).astype(MXU_DTYPE)
    masks = _half_masks(MXU_DTYPE)
    chunks = []
    for c in range(tm // CHUNK):
        rows = slice(c * CHUNK, (c + 1) * CHUNK)
        groups = []
        for gp in range(2):
            vn_g = vn[rows, gp * LANES:(gp + 1) * LANES]
            groups.append(_dot(w_ref[2 * gp], vn_g * masks[0]) + _dot(w_ref[2 * gp + 1], vn_g * masks[1]))
        chunks.append(jnp.concatenate(groups, axis=1) + bias)
    mixed = jnp.concatenate(chunks, axis=0)
    return dict(u=u, v=v, ug=ug, tu=tu, tv=tv, rs=rs, vhat=vhat, vn=vn, mixed=mixed, ya=ug * mixed)


def _sgu_fwd(uv, w_tril, bias, g_sgu, g_a):
    s = uv.shape[0]
    tm = 512

    def body(uv_ref, w_ref, b_ref, gs_ref, ga_ref, o_ref):
        t = _sgu_forward_tile(uv_ref[...], w_ref, b_ref[...], gs_ref[...])
        n, _ = _rms_stats(t['ya'])
        o_ref[...] = (n * ga_ref[...]).astype(MXU_DTYPE)

    return pl.pallas_call(
        body, name="sgu_fwd", grid=(s // tm,),
        in_specs=[_rows(tm, 2 * WIDTH_A), _full((HEADS_A, CHUNK, CHUNK)), _full((CHUNK, WIDTH_A)),
                  _full((1, WIDTH_A)), _full((1, WIDTH_A))],
        out_specs=_rows(tm, WIDTH_A), out_shape=jax.ShapeDtypeStruct((s, WIDTH_A), MXU_DTYPE),
        compiler_params=_params(("arbitrary",)),
    )(uv, w_tril, bias, g_sgu, g_a)


def _out_fwd(ya_n, y_b, g_b, w_out, x):
    s = x.shape[0]
    tm = 512

    nc = WIDTH_B // LANES

    def body(ya_ref, *refs):
        yb_refs = refs[:nc]
        g_ref, w_ref, x_ref, h_ref, ybn_ref = refs[nc:]
        n, _ = _rms_stats(_load_l256(yb_refs, tm))
        ybn = (n * g_ref[...]).astype(MXU_DTYPE)
        ybn_ref[...] = ybn
        h_ref[...] = x_ref[...] + _dot(ya_ref[...], w_ref[:WIDTH_A, :]) + _dot(ybn, w_ref[WIDTH_A:, :])

    return pl.pallas_call(
        body, name="out_fwd", grid=(s // tm,),
        in_specs=[_rows(tm, WIDTH_A)] + _col_specs(tm, WIDTH_B) + [_full((1, WIDTH_B)), _full((D_MODEL, D_MODEL)),
                                                                 _rows(tm, D_MODEL)],
        out_specs=[_rows(tm, D_MODEL), _rows(tm, WIDTH_B)],
        out_shape=[jax.ShapeDtypeStruct((s, D_MODEL), F32), jax.ShapeDtypeStruct((s, WIDTH_B), MXU_DTYPE)],
        compiler_params=_params(("arbitrary",)),
    )(ya_n, *([y_b] * nc), g_b, w_out, x)


def _ffn_fwd(h1, g_ffn, w_gate, w_up, w_down):
    s = h1.shape[0]
    tm = 256

    def body(h_ref, g_ref, wg_ref, wu_ref, wd_ref, o_ref, gate_ref, up_ref, hn_ref):
        h = h_ref[...]
        n, _ = _rms_stats(h)
        hn = (n * g_ref[...]).astype(MXU_DTYPE)
        hn_ref[...] = hn
        out = h
        for c in range(2):
            cols = slice(c * FF_HALF, (c + 1) * FF_HALF)
            gate = _dot(hn, wg_ref[:, cols])
            up = _dot(hn, wu_ref[:, cols])
            gate_ref[:, cols] = gate.astype(MXU_DTYPE)
            up_ref[:, cols] = up.astype(MXU_DTYPE)
            act = (gate * jax.nn.sigmoid(gate) * up).astype(MXU_DTYPE)
            out = out + _dot(act, wd_ref[cols, :])
        o_ref[...] = out

    return pl.pallas_call(
        body, name="ffn_fwd", grid=(s // tm,),
        in_specs=[_rows(tm, D_MODEL), _full((1, D_MODEL)), _full((D_MODEL, D_FF)), _full((D_MODEL, D_FF)),
                  _full((D_FF, D_MODEL))],
        out_specs=[_rows(tm, D_MODEL), _rows(tm, D_FF), _rows(tm, D_FF), _rows(tm, D_MODEL)],
        out_shape=[jax.ShapeDtypeStruct((s, D_MODEL), F32), jax.ShapeDtypeStruct((s, D_FF), MXU_DTYPE),
                   jax.ShapeDtypeStruct((s, D_FF), MXU_DTYPE), jax.ShapeDtypeStruct((s, D_MODEL), MXU_DTYPE)],
        compiler_params=_params(("arbitrary",)),
    )(h1, g_ffn, w_gate, w_up, w_down)


def _ple_loss(h2, p, target, g_ple, w_pg, w_pg_t, w_pp, g_final):
    s = h2.shape[0]
    tm = 256

    def body(h_ref, p_ref, t_ref, gp_ref, wg_ref, wgt_ref, wp_ref, gf_ref,
             loss_ref, dh_ref, dz_ref, dpp_ref, hn_ref, dgp_ref, dgf_ref):
        @pl.when(pl.program_id(0) == 0)
        def _():
            loss_ref[...] = jnp.zeros_like(loss_ref)
            dgp_ref[...] = jnp.zeros_like(dgp_ref)
            dgf_ref[...] = jnp.zeros_like(dgf_ref)

        h2t = h_ref[...]
        n2, r2 = _rms_stats(h2t)
        hn = (n2 * gp_ref[...]).astype(MXU_DTYPE)
        hn_ref[...] = hn
        gate = jax.nn.sigmoid(_dot(hn, wg_ref[...]))
        pp = _dot(p_ref[...].astype(MXU_DTYPE), wp_ref[...])
        h3 = h2t + gate * pp
        n3, r3 = _rms_stats(h3)
        diff = n3 * gf_ref[...] - t_ref[...]
        loss_ref[...] += jnp.full(loss_ref.shape, 0.5 * jnp.sum(diff * diff) / D_MODEL, F32)
        dy = diff * (1.0 / D_MODEL)
        dgf_ref[...] += jnp.sum(dy * n3, axis=0, keepdims=True)
        dh3 = _rms_bwd(dy * gf_ref[...], n3, r3)
        dpp_ref[...] = (dh3 * gate).astype(MXU_DTYPE)
        dz = (dh3 * pp * gate * (1.0 - gate)).astype(MXU_DTYPE)
        dz_ref[...] = dz
        dhn = _dot(dz, wgt_ref[...])
        dgp_ref[...] += jnp.sum(dhn * n2, axis=0, keepdims=True)
        dh_ref[...] = dh3 + _rms_bwd(dhn * gp_ref[...], n2, r2)

    return pl.pallas_call(
        body, name="ple_loss", grid=(s // tm,),
        in_specs=[_rows(tm, D_MODEL), _rows(tm, PLE_DIM), _rows(tm, D_MODEL), _full((1, D_MODEL)),
                  _full((D_MODEL, D_MODEL)), _full((D_MODEL, D_MODEL)), _full((PLE_DIM, D_MODEL)),
                  _full((1, D_MODEL))],
        out_specs=[_full((1, LANES)), _rows(tm, D_MODEL), _rows(tm, D_MODEL), _rows(tm, D_MODEL),
                   _rows(tm, D_MODEL), _full((1, D_MODEL)), _full((1, D_MODEL))],
        out_shape=[jax.ShapeDtypeStruct((1, LANES), F32), jax.ShapeDtypeStruct((s, D_MODEL), F32),
                   jax.ShapeDtypeStruct((s, D_MODEL), MXU_DTYPE), jax.ShapeDtypeStruct((s, D_MODEL), MXU_DTYPE),
                   jax.ShapeDtypeStruct((s, D_MODEL), MXU_DTYPE), jax.ShapeDtypeStruct((1, D_MODEL), F32),
                   jax.ShapeDtypeStruct((1, D_MODEL), F32)],
        compiler_params=_params(("arbitrary",)),
    )(h2, p, target, g_ple, w_pg, w_pg_t, w_pp, g_final)


def _ffn_bwd(dh2, h1, gate, up, g_ffn, w_down_t, w_gate_t, w_up_t):
    s = h1.shape[0]
    tm = 256

    def body(dh_ref, h_ref, gate_ref, up_ref, g_ref, wdt_ref, wgt_ref, wut_ref,
             o_ref, act_ref, dg_ref, du_ref, dgn_ref):
        @pl.when(pl.program_id(0) == 0)
        def _():
            dgn_ref[...] = jnp.zeros_like(dgn_ref)

        dh = dh_ref[...]
        dhb = dh.astype(MXU_DTYPE)
        dhn = jnp.zeros((tm, D_MODEL), F32)
        for c in range(2):
            cols = slice(c * FF_HALF, (c + 1) * FF_HALF)
            dact = _dot(dhb, wdt_ref[:, cols])
            g = gate_ref[:, cols].astype(F32)
            u = up_ref[:, cols].astype(F32)
            sg = jax.nn.sigmoid(g)
            silu = g * sg
            act_ref[:, cols] = (silu * u).astype(MXU_DTYPE)
            du = (dact * silu).astype(MXU_DTYPE)
            dg = (dact * u * sg * (1.0 + g * (1.0 - sg))).astype(MXU_DTYPE)
            du_ref[:, cols] = du
            dg_ref[:, cols] = dg
            dhn = dhn + _dot(dg, wgt_ref[cols, :]) + _dot(du, wut_ref[cols, :])
        n, r = _rms_stats(h_ref[...])
        dgn_ref[...] += jnp.sum(dhn * n, axis=0, keepdims=True)
        o_ref[...] = dh + _rms_bwd(dhn * g_ref[...], n, r)

    return pl.pallas_call(
        body, name="ffn_bwd", grid=(s // tm,),
        in_specs=[_rows(tm, D_MODEL), _rows(tm, D_MODEL), _rows(tm, D_FF), _rows(tm, D_FF), _full((1, D_MODEL)),
                  _full((D_MODEL, D_FF)), _full((D_FF, D_MODEL)), _full((D_FF, D_MODEL))],
        out_specs=[_rows(tm, D_MODEL), _rows(tm, D_FF), _rows(tm, D_FF), _rows(tm, D_FF), _full((1, D_MODEL))],
        out_shape=[jax.ShapeDtypeStruct((s, D_MODEL), F32), jax.ShapeDtypeStruct((s, D_FF), MXU_DTYPE),
                   jax.ShapeDtypeStruct((s, D_FF), MXU_DTYPE), jax.ShapeDtypeStruct((s, D_FF), MXU_DTYPE),
                   jax.ShapeDtypeStruct((1, D_MODEL), F32)],
        compiler_params=_params(("arbitrary",)),
    )(dh2, h1, gate, up, g_ffn, w_down_t, w_gate_t, w_up_t)


def _out_bwd(dh1, y_b, g_b, w_out_t):
    s = dh1.shape[0]
    tm = 512

    nc = WIDTH_B // LANES

    def body(dh_ref, *refs):
        yb_refs = refs[:nc]
        g_ref, wt_ref, dya_ref, dyb_ref, dg_ref, scr = refs[nc:]

        @pl.when(pl.program_id(0) == 0)
        def _():
            dg_ref[...] = jnp.zeros_like(dg_ref)

        dy = _dot(dh_ref[...].astype(MXU_DTYPE), wt_ref[...])
        dya_ref[...] = dy[:, :WIDTH_A]
        dyb = dy[:, WIDTH_A:]
        n, r = _rms_stats(_load_l256(yb_refs, tm))
        dg_ref[...] += jnp.sum(dyb * n, axis=0, keepdims=True)
        dyb_in = _rms_bwd(dyb * g_ref[...], n, r)
        for j in range(nc):
            cols = slice(j * LANES, (j + 1) * LANES)
            _store_l256(scr, dyb_ref, cols, dyb_in[:, cols])

    return pl.pallas_call(
        body, name="out_bwd", grid=(s // tm,), scratch_shapes=[pltpu.VMEM((tm, LANES), F32)],
        in_specs=[_rows(tm, D_MODEL)] + _col_specs(tm, WIDTH_B) + [_full((1, WIDTH_B)), _full((D_MODEL, D_MODEL))],
        out_specs=[_rows(tm, WIDTH_A), _rows(tm, WIDTH_B), _full((1, WIDTH_B))],
        out_shape=[jax.ShapeDtypeStruct((s, WIDTH_A), F32), jax.ShapeDtypeStruct((s, WIDTH_B), F32),
                   jax.ShapeDtypeStruct((1, WIDTH_B), F32)],
        compiler_params=_params(("arbitrary",)),
    )(dh1, *([y_b] * nc), g_b, w_out_t)


def _attn_bwd_branch(q, k, v, do, o, lse, grads, dil):
    s = q.shape[0]
    br = _Branch(dil, s)
    qn, nb = br.qn, br.nb
    first = grads is None

    def body(*refs):
        bias_ref, q_ref, kc_ref, kp_ref, vc_ref, vp_ref, do_ref, o_ref, lse_ref = refs[:9]
        if first:
            rest = refs[9:]
        else:
            dq_in, dk_in, dv_in = refs[9:12]
            rest = refs[12:]
        dq_ref, dk_ref, dv_ref, dk_carry, dv_carry = rest
        n = pl.program_id(1)

        @pl.when(n == 0)
        def _():
            dk_carry[...] = jnp.zeros_like(dk_carry)
            dv_carry[...] = jnp.zeros_like(dv_carry)

        @pl.when(n < nb)
        def _():
            bias2 = jnp.concatenate([bias_ref[0], bias_ref[0]], axis=0)
            lane = lax.broadcasted_iota(jnp.int32, (qn, LANES), 1)
            lo = lane < HEAD_DIM
            mask_f = lo.astype(F32)
            mask_lo = mask_f.astype(MXU_DTYPE)
            lse = br.load(lse_ref)
            for hp in range(HEADS_B // 2):
                cols = slice(hp * LANES, (hp + 1) * LANES)
                h0, h1 = 2 * hp, 2 * hp + 1
                qp = br.load(q_ref, cols)
                kcat = jnp.concatenate([br.load(kp_ref, cols), br.load(kc_ref, cols)], axis=0)
                vcat = jnp.concatenate([br.load(vp_ref, cols), br.load(vc_ref, cols)], axis=0)
                dop = br.load(do_ref, cols)
                prod = dop * br.load(o_ref, cols)
                prod_lo = prod * mask_f
                delta = jnp.concatenate([jnp.sum(prod_lo, axis=1, keepdims=True),
                                         jnp.sum(prod - prod_lo, axis=1, keepdims=True)], axis=0)
                qs = jnp.concatenate([qp * mask_lo, qp * (1 - mask_lo)], axis=0)
                dos = jnp.concatenate([dop * mask_f, dop * (1.0 - mask_f)], axis=0).astype(MXU_DTYPE)
                lse2 = jnp.concatenate([lse[:, h0:h0 + 1], lse[:, h1:h1 + 1]], axis=0)
                p = jnp.exp(_dot_nt(qs, kcat) + bias2 - lse2)
                ds = (p * (_dot_nt(dos, vcat) - delta)).astype(MXU_DTYPE)
                dvc = _dot_tn(p.astype(MXU_DTYPE), dos)
                dkc = _dot_tn(ds, qs)
                dq2 = _dot(ds, kcat)
                dq = jnp.where(lo, dq2[:qn], dq2[qn:])
                dk_prev = dk_carry[:, cols] + dkc[:qn]
                dv_prev = dv_carry[:, cols] + dvc[:qn]
                if not first:
                    dq = dq + br.load(dq_in, cols)
                    dk_prev = dk_prev + br.load(dk_in, cols)
                    dv_prev = dv_prev + br.load(dv_in, cols)
                br.store(dq_ref, cols, dq)
                br.store(dk_ref, cols, dk_prev)
                br.store(dv_ref, cols, dv_prev)
                dk_carry[:, cols] = dkc[qn:]
                dv_carry[:, cols] = dvc[qn:]

        @pl.when(n == nb)
        def _():
            dk_last = dk_carry[...]
            dv_last = dv_carry[...]
            if not first:
                dk_last = dk_last + br.load(dk_in)
                dv_last = dv_last + br.load(dv_in)
            br.store(dk_ref, slice(None), dk_last)
            br.store(dv_ref, slice(None), dv_last)

    cur = lambda n: jnp.minimum(n, nb - 1)
    before = lambda n: jnp.maximum(cur(n) - 1, 0)
    late = lambda n: jnp.maximum(n - 1, 0)
    in_specs = [br.bias_spec(cur), br.spec(WIDTH_B, cur), br.spec(WIDTH_B, cur), br.spec(WIDTH_B, before),
                br.spec(WIDTH_B, cur), br.spec(WIDTH_B, before), br.spec(WIDTH_B, cur), br.spec(WIDTH_B, cur),
                br.spec(LANES, cur)]
    args = [jnp.asarray(br.bias)] + [br.view(a) for a in (q, k, k, v, v, do, o, lse)]
    if not first:
        in_specs += [br.spec(WIDTH_B, cur), br.spec(WIDTH_B, late), br.spec(WIDTH_B, late)]
        args += [br.view(g) for g in grads]
    res = pl.pallas_call(
        body, name="attn_bwd_d%d" % dil, grid=(br.grid[0], nb + 1), in_specs=in_specs,
        out_specs=[br.spec(WIDTH_B, cur), br.spec(WIDTH_B, late), br.spec(WIDTH_B, late)],
        out_shape=[jax.ShapeDtypeStruct((s // L_BLOCK, 4, 4, L_GROUP, WIDTH_B), F32)] * 3,
        scratch_shapes=[pltpu.VMEM((qn, WIDTH_B), F32), pltpu.VMEM((qn, WIDTH_B), F32)],
        compiler_params=_params(("arbitrary", "arbitrary")),
    )(*args)
    return tuple(a.reshape(s, WIDTH_B) for a in res)


def _sgu_bwd(uv, dya_n, w_tril, w_tril_t, bias, g_sgu, g_a):
    s = uv.shape[0]
    tm = 512

    def body(uv_ref, dy_ref, w_ref, wt_ref, b_ref, gs_ref, ga_ref, duv_ref, dw_ref, db_ref, dgs_ref, dga_ref,
             db_acc):
        i = pl.program_id(0)

        @pl.when(i == 0)
        def _():
            dw_ref[...] = jnp.zeros_like(dw_ref)
            dgs_ref[...] = jnp.zeros_like(dgs_ref)
            dga_ref[...] = jnp.zeros_like(dga_ref)
            db_acc[...] = jnp.zeros_like(db_acc)

        t = _sgu_forward_tile(uv_ref[...], w_ref, b_ref[...], gs_ref[...])
        na, ra = _rms_stats(t['ya'])
        dyn = dy_ref[...]
        dga_ref[...] += jnp.sum(dyn * na, axis=0, keepdims=True)
        dya = _rms_bwd(dyn * ga_ref[...], na, ra)
        dug = dya * t['mixed']
        dmixed = dya * t['ug']
        dmb = dmixed.astype(MXU_DTYPE)
        masks = _half_masks(MXU_DTYPE)
        chunks = []
        db = jnp.zeros((CHUNK, WIDTH_A), F32)
        for c in range(tm // CHUNK):
            rows = slice(c * CHUNK, (c + 1) * CHUNK)
            db = db + dmixed[rows]
            groups = []
            for gp in range(2):
                cols = slice(gp * LANES, (gp + 1) * LANES)
                dm_g = dmb[rows, cols]
                vn_g = t['vn'][rows, cols]
                dvn_g = jnp.zeros((CHUNK, LANES), F32)
                for j in range(2):
                    dm_h = dm_g * masks[j]
                    dvn_g = dvn_g + _dot(wt_ref[2 * gp + j], dm_h)
                    dw_ref[2 * gp + j] += _dot_nt(dm_h, vn_g)
                groups.append(dvn_g)
            chunks.append(jnp.concatenate(groups, axis=1))
        db_acc[...] += db
        dvn = jnp.concatenate(chunks, axis=0)
        vhat = t['vhat']
        dgs_ref[...] += jnp.sum(dvn * vhat, axis=0, keepdims=True)
        dvh = dvn * gs_ref[...]
        dvg = t['rs'] * (dvh - jnp.mean(dvh, axis=-1, keepdims=True)
                         - vhat * jnp.mean(dvh * vhat, axis=-1, keepdims=True))
        duv_ref[:, :WIDTH_A] = (dug * _gelu_grad(t['u'], t['tu'])).astype(MXU_DTYPE)
        duv_ref[:, WIDTH_A:] = (dvg * _gelu_grad(t['v'], t['tv'])).astype(MXU_DTYPE)

        @pl.when(i == pl.num_programs(0) - 1)
        def _():
            lane_a = lax.broadcasted_iota(jnp.int32, (CHUNK, WIDTH_A), 1)
            lane = lax.broadcasted_iota(jnp.int32, (CHUNK, LANES), 1)
            acc = db_acc[...]
            out = jnp.zeros((CHUNK, LANES), F32)
            for h in range(HEADS_A):
                col = jnp.sum(jnp.where(lane_a // HEAD_DIM == h, acc, 0.0), axis=1, keepdims=True)
                out = jnp.where(lane == h, col, out)
            db_ref[...] = out
            causal = (lax.broadcasted_iota(jnp.int32, (CHUNK, CHUNK), 0)
                      >= lax.broadcasted_iota(jnp.int32, (CHUNK, CHUNK), 1))
            for h in range(HEADS_A):
                dw_ref[h] = jnp.where(causal, dw_ref[h], 0.0)

    return pl.pallas_call(
        body, name="sgu_bwd", grid=(s // tm,),
        in_specs=[_rows(tm, 2 * WIDTH_A), _rows(tm, WIDTH_A), _full((HEADS_A, CHUNK, CHUNK)),
                  _full((HEADS_A, CHUNK, CHUNK)), _full((CHUNK, WIDTH_A)), _full((1, WIDTH_A)),
                  _full((1, WIDTH_A))],
        out_specs=[_rows(tm, 2 * WIDTH_A), _full((HEADS_A, CHUNK, CHUNK)), _full((CHUNK, LANES)),
                   _full((1, WIDTH_A)), _full((1, WIDTH_A))],
        out_shape=[jax.ShapeDtypeStruct((s, 2 * WIDTH_A), MXU_DTYPE),
                   jax.ShapeDtypeStruct((HEADS_A, CHUNK, CHUNK), F32), jax.ShapeDtypeStruct((CHUNK, LANES), F32),
                   jax.ShapeDtypeStruct((1, WIDTH_A), F32), jax.ShapeDtypeStruct((1, WIDTH_A), F32)],
        scratch_shapes=[pltpu.VMEM((CHUNK, WIDTH_A), F32)],
        compiler_params=_params(("arbitrary",)),
    )(uv, dya_n, w_tril, w_tril_t, bias, g_sgu, g_a)


def _in_bwd(duv, dq, dk, dv, cos_t, sin_t, w_in_t, x, g_mix, dh1):
    s = x.shape[0]
    tm = 512

    nc = WIDTH_B // LANES

    def body(duv_ref, *refs):
        dq_refs, dk_refs, dv_refs = refs[:nc], refs[nc:2 * nc], refs[2 * nc:3 * nc]
        cos_ref, sin_ref, wt_ref, x_ref, g_ref, dh_ref, gx_ref, dp_ref, dg_ref = refs[3 * nc:]

        @pl.when(pl.program_id(0) == 0)
        def _():
            dg_ref[...] = jnp.zeros_like(dg_ref)

        cos = cos_ref[...]
        sin = sin_ref[...]
        dp_ref[:, :2 * WIDTH_A] = duv_ref[...]
        for i in range(nc):
            lo = 2 * WIDTH_A + i * LANES
            tq = _load_l256(dq_refs[i:i + 1], tm) * (HEAD_DIM ** -0.5)
            tk = _load_l256(dk_refs[i:i + 1], tm)
            dp_ref[:, lo:lo + LANES] = (tq * cos + _rope_partner(tq * sin)).astype(MXU_DTYPE)
            dp_ref[:, lo + WIDTH_B:lo + WIDTH_B + LANES] = (tk * cos + _rope_partner(tk * sin)).astype(MXU_DTYPE)
            dp_ref[:, lo + 2 * WIDTH_B:lo + 2 * WIDTH_B + LANES] = _load_l256(dv_refs[i:i + 1], tm).astype(MXU_DTYPE)
        dhn = _dot(dp_ref[...], wt_ref[...])
        n, r = _rms_stats(x_ref[...])
        dg_ref[...] += jnp.sum(dhn * n, axis=0, keepdims=True)
        gx_ref[...] = dh_ref[...] + _rms_bwd(dhn * g_ref[...], n, r)

    return pl.pallas_call(
        body, name="in_bwd", grid=(s // tm,),
        in_specs=[_rows(tm, 2 * WIDTH_A)] + 3 * _col_specs(tm, WIDTH_B)
        + [_rows(tm, LANES), _rows(tm, LANES), _full((IN_COLS, D_MODEL)), _rows(tm, D_MODEL),
           _full((1, D_MODEL)), _rows(tm, D_MODEL)],
        out_specs=[_rows(tm, D_MODEL), _rows(tm, IN_COLS), _full((1, D_MODEL))],
        out_shape=[jax.ShapeDtypeStruct((s, D_MODEL), F32), jax.ShapeDtypeStruct((s, IN_COLS), MXU_DTYPE),
                   jax.ShapeDtypeStruct((1, D_MODEL), F32)],
        compiler_params=_params(("arbitrary",)),
    )(duv, *([dq] * nc), *([dk] * nc), *([dv] * nc), cos_t, sin_t, w_in_t, x, g_mix, dh1)


def _wgrad(a, bs, name):
    s, m = a.shape
    bm = 512 if m % 512 == 0 else (FF_HALF if m == D_FF else m)
    ts = 512
    nsteps = s // ts

    def body(a_ref, *refs):
        b_refs, o_refs = refs[:len(bs)], refs[len(bs):]
        kk = pl.program_id(1)
        at = a_ref[...].astype(MXU_DTYPE)
        for b_ref, o_ref in zip(b_refs, o_refs):
            c = _dot_tn(at, b_ref[...].astype(MXU_DTYPE))

            @pl.when(kk == 0)
            def _():
                o_ref[...] = c

            @pl.when(kk > 0)
            def _():
                o_ref[...] += c

    return pl.pallas_call(
        body, name=name, grid=(m // bm, nsteps),
        in_specs=[pl.BlockSpec((ts, bm), lambda i, kk: (kk, i))]
        + [pl.BlockSpec((ts, b.shape[1]), lambda i, kk: (kk, 0)) for b in bs],
        out_specs=[pl.BlockSpec((bm, b.shape[1]), lambda i, kk: (i, 0)) for b in bs],
        out_shape=[jax.ShapeDtypeStruct((m, b.shape[1]), F32) for b in bs],
        compiler_params=_params(("arbitrary", "arbitrary")),
    )(a, *bs)


def _rope_tables(s):
    half = HEAD_DIM // 2
    inv = ROPE_THETA ** (-jnp.arange(half, dtype=F32) / half)
    ang = jnp.arange(s, dtype=F32)[:, None] * inv[None, :]
    cos = jnp.cos(ang)
    sin = jnp.sin(ang)
    cos_t = jnp.concatenate([cos, cos, cos, cos], axis=1)
    sin_t = jnp.concatenate([-sin, sin, -sin, sin], axis=1)
    return cos_t, sin_t


def _local_step(x, p, target, w, small):
    s = x.shape[0]
    cos_t, sin_t = _rope_tables(s)
    tril = jnp.tril(jnp.ones((CHUNK, CHUNK), F32))
    w_tril = (small['sgu_w'].reshape(HEADS_A, CHUNK, CHUNK) * tril).astype(MXU_DTYPE)
    w_tril_t = jnp.swapaxes(w_tril, 1, 2)
    bias = jnp.repeat(small['sgu_b'].reshape(HEADS_A, CHUNK).T, HEAD_DIM, axis=1)
    g = {n: small[n].reshape(1, -1) for n in SMALL if n not in ('sgu_w', 'sgu_b')}

    uv, q, k, v, hn1 = _in_fwd(x, g['mix_norm_g'], w['w_in'], cos_t, sin_t)
    ya_n = _sgu_fwd(uv, w_tril, bias, g['sgu_norm_g'], g['out_norm_a'])
    state = None
    for i, dil in enumerate(DILATIONS):
        state = _attn_fwd_branch(q, k, v, state, dil, last=(i == len(DILATIONS) - 1))
    y_b, lse = state
    h1, yb_n = _out_fwd(ya_n, y_b, g['out_norm_b'], w['w_out'], x)
    h2, gate, up, hn2 = _ffn_fwd(h1, g['ffn_norm_g'], w['w_gate'], w['w_up'], w['w_down'])
    loss, dh2, dz, dpp, hn3, d_ple_g, d_final_g = _ple_loss(
        h2, p, target, g['ple_norm_g'], w['w_ple_gate'], w['w_ple_gate'].T, w['w_ple_proj'], g['final_norm_g'])

    dh1, act, dgate, dup, d_ffn_g = _ffn_bwd(dh2, h1, gate, up, g['ffn_norm_g'], w['w_down'].T, w['w_gate'].T,
                                             w['w_up'].T)
    dya_n, dyb, d_out_b = _out_bwd(dh1, y_b, g['out_norm_b'], w['w_out'].T)
    grads = None
    for dil in DILATIONS:
        grads = _attn_bwd_branch(q, k, v, dyb, y_b, lse, grads, dil)
    duv, d_sgu_w, d_sgu_b, d_sgu_g, d_out_a = _sgu_bwd(uv, dya_n, w_tril, w_tril_t, bias, g['sgu_norm_g'],
                                                       g['out_norm_a'])
    grad_x, dproj, d_mix_g = _in_bwd(duv, grads[0], grads[1], grads[2], cos_t, sin_t, w['w_in'].T, x,
                                     g['mix_norm_g'], dh1)

    gw = {}
    gw['w_in'], = _wgrad(hn1, [dproj], "wgrad_in")
    gw_out_a, = _wgrad(ya_n, [dh1], "wgrad_out_a")
    gw_out_b, = _wgrad(yb_n, [dh1], "wgrad_out_b")
    gw['w_out'] = jnp.concatenate([gw_out_a, gw_out_b], axis=0)
    gw['w_gate'], gw['w_up'] = _wgrad(hn2, [dgate, dup], "wgrad_gate_up")
    gw['w_down'], = _wgrad(act, [dh2], "wgrad_down")
    gw['w_ple_gate'], = _wgrad(hn3, [dz], "wgrad_ple_gate")
    gw['w_ple_proj'], = _wgrad(p, [dpp], "wgrad_ple_proj")

    gs = {
        'mix_norm_g': d_mix_g, 'sgu_w': d_sgu_w, 'sgu_b': d_sgu_b[:, :HEADS_A].T, 'sgu_norm_g': d_sgu_g,
        'out_norm_a': d_out_a, 'out_norm_b': d_out_b, 'ffn_norm_g': d_ffn_g, 'ple_norm_g': d_ple_g,
        'final_norm_g': d_final_g,
    }
    return loss, grad_x, gw, gs


MESH = pl.DeviceIdType.MESH
ANY = pl.BlockSpec(memory_space=pl.ANY)
PACK_ROWS = sum(FULL_SHAPES[n][0] * FULL_SHAPES[n][1] for n, _ in SHARDED) // N_CHIPS // D_MODEL
HALF_ROWS = PACK_ROWS // 2


def _place():
    x, y, c = lax.axis_index("x"), lax.axis_index("y"), lax.axis_index("c")
    other_chips = [(1 - x, y), (x, 1 - y), (1 - x, 1 - y)]
    return x, y, c, other_chips


def _all_gather_weights(packed):
    def body(x_ref, out_ref, send_sems, recv_sems):
        x, y, c, chips = _place()
        sibling = (x, y, 1 - c)

        def half(chip, hc):
            return out_ref.at[2 * chip[0] + chip[1], pl.ds(hc * HALF_ROWS, HALF_ROWS), :]

        def copy(k, src, dst, to):
            return pltpu.make_async_remote_copy(src_ref=src, dst_ref=dst, send_sem=send_sems.at[k],
                                                recv_sem=recv_sems.at[k], device_id=to, device_id_type=MESH)

        my_half = x_ref.at[pl.ds(c * HALF_ROWS, HALF_ROWS), :]
        first = [copy(j, my_half, half((x, y), c), (*chip, c)) for j, chip in enumerate(chips)]
        for cp in first:
            cp.start()
        passed = [copy(3 + j, half(chip, c), half(chip, c), sibling) for j, chip in enumerate(chips)]
        for j, chip in enumerate(chips):
            copy(j, my_half, half(chip, c), (*chip, c)).wait_recv()
            passed[j].start()
        for j, chip in enumerate(chips):
            copy(3 + j, my_half, half(chip, 1 - c), sibling).wait_recv()
        for cp in first + passed:
            cp.wait_send()

    gathered = pl.pallas_call(
        body, name="all_gather_weights",
        out_shape=jax.ShapeDtypeStruct((N_CHIPS,) + packed.shape, packed.dtype),
        in_specs=[ANY], out_specs=ANY,
        scratch_shapes=[pltpu.SemaphoreType.DMA((6,)), pltpu.SemaphoreType.DMA((6,))],
    )(packed)
    me = 2 * lax.axis_index("x") + lax.axis_index("y")
    return lax.dynamic_update_slice(gathered, packed[None], (me, 0, 0))


def _rs_to_sibling(g):
    def body(g_ref, r_ref, send_sem, recv_sem):
        x, y, c, _ = _place()
        cp = pltpu.make_async_remote_copy(
            src_ref=g_ref.at[:, pl.ds((1 - c) * HALF_ROWS, HALF_ROWS), :], dst_ref=r_ref, send_sem=send_sem,
            recv_sem=recv_sem, device_id=(x, y, 1 - c), device_id_type=MESH)
        cp.start()
        cp.wait()

    return pl.pallas_call(
        body, name="rs_to_sibling", out_shape=jax.ShapeDtypeStruct((N_CHIPS, HALF_ROWS, D_MODEL), g.dtype),
        in_specs=[ANY], out_specs=ANY,
        scratch_shapes=[pltpu.SemaphoreType.DMA, pltpu.SemaphoreType.DMA],
    )(g)


def _rs_add(g, r, c):
    tr = HALF_ROWS // 2
    spec = pltpu.PrefetchScalarGridSpec(
        num_scalar_prefetch=1, grid=(N_CHIPS, 2),
        in_specs=[pl.BlockSpec((1, tr, D_MODEL), lambda j, i, c_ref: (j, 2 * c_ref[0] + i, 0)),
                  pl.BlockSpec((1, tr, D_MODEL), lambda j, i, c_ref: (j, i, 0))],
        out_specs=pl.BlockSpec((1, tr, D_MODEL), lambda j, i, c_ref: (j, i, 0)))

    def body(c_ref, g_ref, r_ref, o_ref):
        o_ref[...] = (g_ref[...] + r_ref[...]).astype(o_ref.dtype)

    return pl.pallas_call(
        body, name="rs_add", grid_spec=spec,
        out_shape=jax.ShapeDtypeStruct((N_CHIPS, HALF_ROWS, D_MODEL), jnp.bfloat16),
        compiler_params=_params(("arbitrary", "arbitrary")),
    )(c.reshape(1).astype(jnp.int32), g, r)


def _rs_between_chips(part):
    def body(p_ref, out_ref, send_sems, recv_sems):
        x, y, c, chips = _place()
        me = 2 * x + y
        sends = []
        for j, chip in enumerate(chips):
            cp = pltpu.make_async_remote_copy(
                src_ref=p_ref.at[2 * chip[0] + chip[1]], dst_ref=out_ref.at[me], send_sem=send_sems.at[j],
                recv_sem=recv_sems.at[j], device_id=(*chip, c), device_id_type=MESH)
            cp.start()
            sends.append(cp)
        for j, chip in enumerate(chips):
            pltpu.make_async_remote_copy(
                src_ref=p_ref.at[me], dst_ref=out_ref.at[2 * chip[0] + chip[1]], send_sem=send_sems.at[j],
                recv_sem=recv_sems.at[j], device_id=(*chip, c), device_id_type=MESH).wait_recv()
        for cp in sends:
            cp.wait_send()

    got = pl.pallas_call(
        body, name="rs_between_chips", out_shape=jax.ShapeDtypeStruct(part.shape, part.dtype),
        in_specs=[ANY], out_specs=ANY,
        scratch_shapes=[pltpu.SemaphoreType.DMA((3,)), pltpu.SemaphoreType.DMA((3,))],
    )(part)
    me = 2 * lax.axis_index("x") + lax.axis_index("y")
    own = lax.dynamic_slice_in_dim(part, me, 1, axis=0)
    return lax.dynamic_update_slice(got, own, (me, 0, 0))


def _rs_sum(parts):
    tr = HALF_ROWS // 2

    def body(p_ref, o_ref):
        acc = p_ref[0].astype(F32)
        for i in range(1, N_CHIPS):
            acc = acc + p_ref[i].astype(F32)
        o_ref[...] = acc

    return pl.pallas_call(
        body, name="rs_sum", grid=(2,),
        in_specs=[pl.BlockSpec((N_CHIPS, tr, D_MODEL), lambda i: (0, i, 0))],
        out_specs=pl.BlockSpec((tr, D_MODEL), lambda i: (i, 0)),
        out_shape=jax.ShapeDtypeStruct((HALF_ROWS, D_MODEL), F32),
        compiler_params=_params(("arbitrary",)),
    )(parts)


def _rs_swap(red):
    def body(r_ref, out_ref, send_sem, recv_sem):
        x, y, c, _ = _place()
        cp = pltpu.make_async_remote_copy(src_ref=r_ref, dst_ref=out_ref, send_sem=send_sem, recv_sem=recv_sem,
                                          device_id=(x, y, 1 - c), device_id_type=MESH)
        cp.start()
        cp.wait()

    other = pl.pallas_call(
        body, name="rs_swap", out_shape=jax.ShapeDtypeStruct(red.shape, red.dtype),
        in_specs=[ANY], out_specs=ANY,
        scratch_shapes=[pltpu.SemaphoreType.DMA, pltpu.SemaphoreType.DMA],
    )(red)
    south = lax.axis_index("c") == 0
    return jnp.concatenate([jnp.where(south, red, other), jnp.where(south, other, red)], axis=0)


def _small_all_reduce(block):
    rows = block.shape[0]

    def body(x_ref, all_ref, sum_ref, send_sems, recv_sems, local_sem):
        x, y, c, chips = _place()
        me, sibling = (x, y, c), (x, y, 1 - c)

        def blk(px, py, pc):
            return all_ref.at[pl.ds((4 * px + 2 * py + pc) * rows, rows), :]

        def copy(k, who, to, src=None):
            return pltpu.make_async_remote_copy(
                src_ref=blk(*who) if src is None else src, dst_ref=blk(*who), send_sem=send_sems.at[k],
                recv_sem=recv_sems.at[k], device_id=to, device_id_type=MESH)

        mine = pltpu.make_async_copy(x_ref, blk(*me), local_sem)
        mine.start()
        first = [copy(0, me, sibling, src=x_ref)]
        first += [copy(1 + j, me, (*chip, c), src=x_ref) for j, chip in enumerate(chips)]
        for cp in first:
            cp.start()
        passed = [copy(4 + j, (*chip, c), sibling) for j, chip in enumerate(chips)]
        for j, chip in enumerate(chips):
            copy(1 + j, (*chip, c), me).wait_recv()
            passed[j].start()
        copy(0, sibling, me).wait_recv()
        for j, chip in enumerate(chips):
            copy(4 + j, (*chip, 1 - c), me).wait_recv()
        for cp in first + passed:
            cp.wait_send()
        mine.wait()
        acc = all_ref[pl.ds(0, rows), :]
        for dev in range(1, N_DEV):
            acc = acc + all_ref[pl.ds(dev * rows, rows), :]
        sum_ref[...] = acc

    vmem = pl.BlockSpec(memory_space=pltpu.VMEM)
    return pl.pallas_call(
        body, name="small_all_reduce",
        out_shape=[jax.ShapeDtypeStruct((N_DEV * rows, D_MODEL), F32), jax.ShapeDtypeStruct((rows, D_MODEL), F32)],
        in_specs=[vmem], out_specs=[vmem, vmem],
        scratch_shapes=[pltpu.SemaphoreType.DMA((7,)), pltpu.SemaphoreType.DMA((7,)), pltpu.SemaphoreType.DMA],
    )(block)[1]


def _adamw(w, g, m, v, name):
    rows, cols = w.shape
    tm = rows
    if rows > 512:
        tm = next(t for t in range(512, 7, -8) if rows % t == 0)

    def body(w_ref, g_ref, m_ref, v_ref, d_ref, nm_ref, nv_ref):
        g_ = g_ref[...]
        m_ = ADAM_B1 * m_ref[...] + (1.0 - ADAM_B1) * g_
        v_ = ADAM_B2 * v_ref[...] + (1.0 - ADAM_B2) * (g_ * g_)
        m_hat = m_ / (1.0 - ADAM_B1 ** ADAM_STEP)
        v_hat = v_ / (1.0 - ADAM_B2 ** ADAM_STEP)
        d_ref[...] = -ADAM_LR * (m_hat / (jnp.sqrt(v_hat) + ADAM_EPS) + ADAM_WD * w_ref[...])
        nm_ref[...] = m_
        nv_ref[...] = v_

    spec = pl.BlockSpec((tm, cols), lambda i: (i, 0))
    return pl.pallas_call(
        body, name=name, grid=(rows // tm,), in_specs=[spec] * 4, out_specs=[spec] * 3,
        out_shape=[jax.ShapeDtypeStruct(w.shape, F32)] * 3, compiler_params=_params(("arbitrary",)),
    )(w, g, m, v)


def _pack_small(values):
    flat = jnp.concatenate([values[n].reshape(-1).astype(F32) for n in SMALL])
    return jnp.pad(flat, (0, SMALL_ROWS * D_MODEL - flat.shape[0])).reshape(SMALL_ROWS, D_MODEL)


def _unpack_small(block, shapes):
    flat = block.reshape(-1)
    out, lo = {}, 0
    for n in SMALL:
        out[n] = flat[lo:lo + SMALL_SIZES[n]].reshape(shapes[n])
        lo += SMALL_SIZES[n]
    return out


def _shard_rows(name):
    shape = FULL_SHAPES[name]
    return shape[0] * shape[1] // N_CHIPS // D_MODEL


def _unpack_gathered(gathered):
    out, lo = {}, 0
    for name, axis in SHARDED:
        rows, cols = FULL_SHAPES[name]
        part = gathered[:, lo:lo + _shard_rows(name)]
        lo += _shard_rows(name)
        if axis == 0:
            out[name] = part.reshape(rows, cols)
        else:
            out[name] = part.reshape(N_CHIPS, rows, cols // N_CHIPS).transpose(1, 0, 2).reshape(rows, cols)
    return out


def _pack_full_grads(gw):
    parts = []
    for name, axis in SHARDED:
        rows, cols = FULL_SHAPES[name]
        g = gw[name]
        if axis == 1:
            g = g.reshape(rows, N_CHIPS, cols // N_CHIPS).transpose(1, 0, 2)
        parts.append(g.reshape(N_CHIPS, _shard_rows(name), D_MODEL))
    return jnp.concatenate(parts, axis=1)


def kernel(x, p, mix_norm_g, w_in, sgu_w, sgu_b, sgu_norm_g, out_norm_a, out_norm_b, w_out, ffn_norm_g, w_gate, w_up, w_down, ple_norm_g, w_ple_gate, w_ple_proj, final_norm_g, loss_target, m_mix_norm_g, m_w_in, m_sgu_w, m_sgu_b, m_sgu_norm_g, m_out_norm_a, m_out_norm_b, m_w_out, m_ffn_norm_g, m_w_gate, m_w_up, m_w_down, m_ple_norm_g, m_w_ple_gate, m_w_ple_proj, m_final_norm_g, v_mix_norm_g, v_w_in, v_sgu_w, v_sgu_b, v_sgu_norm_g, v_out_norm_a, v_out_norm_b, v_w_out, v_ffn_norm_g, v_w_gate, v_w_up, v_w_down, v_ple_norm_g, v_w_ple_gate, v_w_ple_proj, v_final_norm_g):
    given = dict(locals())
    c = lax.axis_index("c")
    sharded_names = [n for n, _ in SHARDED]

    packed = jnp.concatenate([given[n][0].astype(MXU_DTYPE).reshape(-1, D_MODEL) for n in sharded_names], axis=0)
    w_full = _unpack_gathered(_all_gather_weights(packed))

    small = {n: given[n] for n in SMALL}
    loss, grad_x, gw, gs = _local_step(x[0], p[0, 0], loss_target[0], w_full, small)

    gs_block = _pack_small(gs)
    gs_block = gs_block.at[SMALL_ROWS - 1, 0].set(loss[0, 0])
    small_sum = _small_all_reduce(gs_block)
    loss_out = small_sum[SMALL_ROWS - 1, 0]
    small_shapes = {n: given[n].shape for n in SMALL}
    g_small = _unpack_small(small_sum, small_shapes)

    g_all = _pack_full_grads(gw)
    from_sibling = _rs_to_sibling(g_all)
    chip_part = _rs_add(g_all, from_sibling, c)
    summed_half = _rs_sum(_rs_between_chips(chip_part))
    g_pack = _rs_swap(summed_half)

    grads, deltas, new_m, new_v = {}, {}, {}, {}
    lo = 0
    for n in sharded_names:
        shard_shape = given[n].shape[1:]
        g = g_pack[lo:lo + _shard_rows(n)].reshape(shard_shape)
        lo += _shard_rows(n)
        d, nm, nv = _adamw(given[n][0], g, given["m_" + n][0], given["v_" + n][0], "adamw_" + n)
        grads[n], deltas[n], new_m[n], new_v[n] = g[None], d[None], nm[None], nv[None]

    d, nm, nv = _adamw(_pack_small(small), small_sum, _pack_small({n: given["m_" + n] for n in SMALL}),
                       _pack_small({n: given["v_" + n] for n in SMALL}), "adamw_small")
    for res, blk in ((grads, small_sum), (deltas, d), (new_m, nm), (new_v, nv)):
        res.update(_unpack_small(blk, small_shapes))

    outs = [loss_out, grad_x[None]]
    for res in (grads, deltas, new_m, new_v):
        outs += [res[n] for n in WEIGHT_NAMES]
    return tuple(outs)
```

```python
import math

import jax
import jax.numpy as jnp
import numpy as np
from jax import lax
from jax.experimental import pallas as pl
from jax.experimental.pallas import tpu as pltpu

F32 = jnp.float32
MXU_DTYPE = jnp.bfloat16

D_MODEL = 1024
HEAD_DIM = 64
HEADS_A = 4
HEADS_B = 12
WIDTH_A = HEADS_A * HEAD_DIM
WIDTH_B = HEADS_B * HEAD_DIM
CHUNK = 128
BLOCK = 128
DILATIONS = (1, 4, 16)
ROPE_THETA = 10000.0
D_FF = 2816
FF_HALF = D_FF // 2
PLE_DIM = 256
IN_COLS = 2 * WIDTH_A + 3 * WIDTH_B
EPS = 1e-6
LANES = 128
N_CHIPS = 4
N_DEV = 8

ADAM_LR = 0.001
ADAM_B1 = 0.9
ADAM_B2 = 0.999
ADAM_EPS = 1e-08
ADAM_WD = 0.01
ADAM_STEP = 10

VMEM_LIMIT = 56 * 1024 * 1024

WEIGHT_NAMES = ['mix_norm_g', 'w_in', 'sgu_w', 'sgu_b', 'sgu_norm_g', 'out_norm_a', 'out_norm_b', 'w_out',
                'ffn_norm_g', 'w_gate', 'w_up', 'w_down', 'ple_norm_g', 'w_ple_gate', 'w_ple_proj', 'final_norm_g']
SHARDED = ['w_in', 'w_out', 'w_gate', 'w_up', 'w_down', 'w_ple_gate', 'w_ple_proj']
SMALL = ['mix_norm_g', 'sgu_w', 'sgu_b', 'sgu_norm_g', 'out_norm_a', 'out_norm_b', 'ffn_norm_g', 'ple_norm_g',
         'final_norm_g']
SMALL_SIZES = {'mix_norm_g': 1024, 'sgu_w': 65536, 'sgu_b': 512, 'sgu_norm_g': 256, 'out_norm_a': 256,
               'out_norm_b': 768, 'ffn_norm_g': 1024, 'ple_norm_g': 1024, 'final_norm_g': 1024}
SMALL_ROWS = 72


def _params(semantics=None):
    return pltpu.CompilerParams(dimension_semantics=semantics, vmem_limit_bytes=VMEM_LIMIT)


def _full(shape):
    nd = len(shape)
    return pl.BlockSpec(shape, lambda i: (0,) * nd)


def _rows(tm, width):
    return pl.BlockSpec((tm, width), lambda i: (i, 0))


def _rms_stats(x):
    r = lax.rsqrt(jnp.mean(x * x, axis=-1, keepdims=True) + EPS)
    return x * r, r


def _rms_bwd(dn, n, r):
    return r * (dn - n * jnp.mean(dn * n, axis=-1, keepdims=True))


def _dot(a, b):
    return jnp.dot(a, b, preferred_element_type=F32)


def _dot_nt(a, b):
    return lax.dot_general(a, b, (((1,), (1,)), ((), ())), preferred_element_type=F32)


def _dot_tn(a, b):
    return lax.dot_general(a, b, (((0,), (0,)), ((), ())), preferred_element_type=F32)


def _gelu_parts(x):
    c = math.sqrt(2.0 / math.pi)
    t = jnp.tanh(c * (x + 0.044715 * x * x * x))
    return 0.5 * x * (1.0 + t), t


def _gelu_grad(x, t):
    c = math.sqrt(2.0 / math.pi)
    return 0.5 * (1.0 + t) + 0.5 * x * (1.0 - t * t) * c * (1.0 + 3.0 * 0.044715 * x * x)


def _half_masks(dtype):
    lane = lax.broadcasted_iota(jnp.int32, (BLOCK, LANES), 1)
    lo = (lane < HEAD_DIM).astype(F32)
    return lo.astype(dtype), (1.0 - lo).astype(dtype)


def _rope_partner(t):
    lane = lax.broadcasted_iota(jnp.int32, t.shape, 1)
    first_half = (lane % HEAD_DIM) < (HEAD_DIM // 2)
    return jnp.where(first_half, pltpu.roll(t, LANES - HEAD_DIM // 2, 1), pltpu.roll(t, HEAD_DIM // 2, 1))


L_BLOCK = 256
L_GROUP = 16


def _store_l256(scr, out_ref, cols, value):
    tm = value.shape[0]
    scr[...] = value
    for blk in range(tm // L_BLOCK):
        for r in range(L_GROUP):
            lo = blk * L_BLOCK + r * L_GROUP
            piece = scr[pl.ds(blk * L_BLOCK + r, L_GROUP, stride=L_GROUP), :]
            out_ref[lo:lo + L_GROUP, cols] = piece.astype(out_ref.dtype)


def _load_l256(col_refs, tm):
    cols = []
    for ref in col_refs:
        pieces = [ref[pl.ds(blk * L_BLOCK + i, L_GROUP, stride=L_GROUP), :]
                  for blk in range(tm // L_BLOCK) for i in range(L_GROUP)]
        cols.append(jnp.concatenate(pieces, axis=0))
    return jnp.concatenate(cols, axis=1)


def _col_specs(tm, width):
    return [pl.BlockSpec((tm, LANES), lambda i, j=j: (i, j)) for j in range(width // LANES)]


def _in_fwd(x, g_mix, w_in, cos_t, sin_t):
    s = x.shape[0]
    tm = 512

    def body(x_ref, g_ref, w_ref, cos_ref, sin_ref, uv_ref, q_ref, k_ref, v_ref, hn_ref, scr):
        n, _ = _rms_stats(x_ref[...])
        hn = (n * g_ref[...]).astype(MXU_DTYPE)
        hn_ref[...] = hn
        proj = _dot(hn, w_ref[...])
        uv_ref[...] = proj[:, :2 * WIDTH_A]
        cos = cos_ref[...]
        sin = sin_ref[...]
        for i in range(WIDTH_B // LANES):
            lo = 2 * WIDTH_A + i * LANES
            tq = proj[:, lo:lo + LANES]
            tk = proj[:, lo + WIDTH_B:lo + WIDTH_B + LANES]
            tv = proj[:, lo + 2 * WIDTH_B:lo + 2 * WIDTH_B + LANES]
            cols = slice(i * LANES, (i + 1) * LANES)
            _store_l256(scr, q_ref, cols, (tq * cos + _rope_partner(tq) * sin) * (HEAD_DIM ** -0.5))
            _store_l256(scr, k_ref, cols, tk * cos + _rope_partner(tk) * sin)
            _store_l256(scr, v_ref, cols, tv)

    return pl.pallas_call(
        body, name="in_fwd", grid=(s // tm,), scratch_shapes=[pltpu.VMEM((tm, LANES), F32)],
        in_specs=[_rows(tm, D_MODEL), _full((1, D_MODEL)), _full((D_MODEL, IN_COLS)), _rows(tm, LANES),
                  _rows(tm, LANES)],
        out_specs=[_rows(tm, 2 * WIDTH_A), _rows(tm, WIDTH_B), _rows(tm, WIDTH_B), _rows(tm, WIDTH_B),
                   _rows(tm, D_MODEL)],
        out_shape=[jax.ShapeDtypeStruct((s, 2 * WIDTH_A), F32), jax.ShapeDtypeStruct((s, WIDTH_B), MXU_DTYPE),
                   jax.ShapeDtypeStruct((s, WIDTH_B), MXU_DTYPE), jax.ShapeDtypeStruct((s, WIDTH_B), MXU_DTYPE),
                   jax.ShapeDtypeStruct((s, D_MODEL), MXU_DTYPE)],
        compiler_params=_params(("arbitrary",)),
    )(x, g_mix, w_in, cos_t, sin_t)


class _Branch:
    def __init__(self, dil, s):
        self.dil = dil
        i = np.arange(L_GROUP)
        if dil == 16:
            self.grid = (16, s // 2048)
            self.shape = (8, 1, 1, L_GROUP)
            self.index = lambda r, n: (n, r // 4, r % 4, 0, 0)
            pos = (np.arange(8)[:, None] * 16 + i[None, :]).reshape(-1)
        elif dil == 4:
            self.grid = (4, s // 512)
            self.shape = (2, 4, 1, L_GROUP)
            self.index = lambda r, n: (n, 0, r, 0, 0)
            pos = (np.arange(2)[:, None, None] * 64 + np.arange(4)[None, :, None] + 4 * i[None, None, :]).reshape(-1)
        else:
            self.grid = (1, s // L_BLOCK)
            self.shape = (1, 4, 4, L_GROUP)
            self.index = lambda r, n: (n, 0, 0, 0, 0)
            pos = (np.arange(16)[:, None] + 16 * i[None, :]).reshape(-1)
        self.qn = pos.shape[0]
        self.nb = self.grid[1]
        dist = pos[:, None] - np.concatenate([pos - self.qn, pos])[None, :]
        band = (dist >= 0) & (dist <= BLOCK)
        start = band & (np.arange(2 * self.qn)[None, :] >= self.qn)
        self.bias = np.where(np.stack([band, start]), 0.0, -np.inf).astype(np.float32)

    def view(self, a):
        return a.reshape(a.shape[0] // L_BLOCK, 4, 4, L_GROUP, a.shape[1])

    def spec(self, w, step=lambda n: n):
        return pl.BlockSpec(self.shape + (w,), lambda r, n: self.index(r, step(n)))

    def bias_spec(self, step=lambda n: n):
        return pl.BlockSpec((1, self.qn, 2 * self.qn), lambda r, n: (jnp.where(step(n) == 0, 1, 0), 0, 0))

    def load(self, ref, cols=slice(None)):
        x = ref[:, :, :, :, cols]
        return x.reshape(self.qn, x.shape[-1])

    def store(self, ref, cols, value):
        ref[:, :, :, :, cols] = value.reshape(self.shape + (value.shape[-1],))


def _attn_fwd_branch(q, k, v, state, dil, last):
    s = q.shape[0]
    br = _Branch(dil, s)
    qn = br.qn
    first = state is None

    def body(*refs):
        bias_ref, q_ref, kc_ref, kp_ref, vc_ref, vp_ref = refs[:6]
        if first:
            outs = refs[6:]
        else:
            acc_ref, m_ref, l_ref = refs[6:9]
            outs = refs[9:]
            m_in = br.load(m_ref)
            l_in = br.load(l_ref)
        o_ref, m_out = outs[0], outs[1]
        bias2 = jnp.concatenate([bias_ref[0], bias_ref[0]], axis=0)
        lane = lax.broadcasted_iota(jnp.int32, (qn, LANES), 1)
        lo = lane < HEAD_DIM
        mask_lo = lo.astype(F32).astype(MXU_DTYPE)
        masks = (mask_lo, 1 - mask_lo)
        m_blk = jnp.zeros((qn, LANES), F32)
        l_blk = jnp.zeros((qn, LANES), F32)
        for hp in range(HEADS_B // 2):
            cols = slice(hp * LANES, (hp + 1) * LANES)
            qp = br.load(q_ref, cols)
            kcat = jnp.concatenate([br.load(kp_ref, cols), br.load(kc_ref, cols)], axis=0)
            vcat = jnp.concatenate([br.load(vp_ref, cols), br.load(vc_ref, cols)], axis=0)
            h0, h1 = 2 * hp, 2 * hp + 1
            sc = _dot_nt(jnp.concatenate([qp * masks[0], qp * masks[1]], axis=0), kcat) + bias2
            m_new = jnp.max(sc, axis=1, keepdims=True)
            if not first:
                m_prev = jnp.concatenate([m_in[:, h0:h0 + 1], m_in[:, h1:h1 + 1]], axis=0)
                m_new = jnp.maximum(m_prev, m_new)
            p = jnp.exp(sc - m_new)
            l_new = jnp.sum(p, axis=1, keepdims=True)
            acc = _dot(p.astype(MXU_DTYPE), vcat)
            if not first:
                alpha = jnp.exp(m_prev - m_new)
                l_new = alpha * jnp.concatenate([l_in[:, h0:h0 + 1], l_in[:, h1:h1 + 1]], axis=0) + l_new
                acc_in = br.load(acc_ref, cols)
                acc = alpha * jnp.concatenate([acc_in, acc_in], axis=0) + acc
            if last:
                acc = acc / l_new
                m_new = m_new + jnp.log(l_new)
            m_blk = jnp.where(lane == h0, m_new[:qn], jnp.where(lane == h1, m_new[qn:], m_blk))
            l_blk = jnp.where(lane == h0, l_new[:qn], jnp.where(lane == h1, l_new[qn:], l_blk))
            br.store(o_ref, cols, jnp.where(lo, acc[:qn], acc[qn:]))
        br.store(m_out, slice(None), m_blk)
        if not last:
            br.store(outs[2], slice(None), l_blk)

    before = lambda n: jnp.maximum(n - 1, 0)
    in_specs = [br.bias_spec(), br.spec(WIDTH_B), br.spec(WIDTH_B), br.spec(WIDTH_B, before), br.spec(WIDTH_B),
                br.spec(WIDTH_B, before)]
    args = [jnp.asarray(br.bias), br.view(q), br.view(k), br.view(k), br.view(v), br.view(v)]
    if not first:
        in_specs += [br.spec(WIDTH_B), br.spec(LANES), br.spec(LANES)]
        args += [br.view(a) for a in state]
    widths = (WIDTH_B, LANES) if last else (WIDTH_B, LANES, LANES)
    res = pl.pallas_call(
        body, name="attn_fwd_d%d" % dil, grid=br.grid, in_specs=in_specs,
        out_specs=[br.spec(w) for w in widths],
        out_shape=[jax.ShapeDtypeStruct((s // L_BLOCK, 4, 4, L_GROUP, w), F32) for w in widths],
        compiler_params=_params(("arbitrary", "arbitrary")),
    )(*args)
    return tuple(a.reshape(s, w) for a, w in zip(res, widths))


def _sgu_forward_tile(uv, w_ref, bias, g_sgu):
    tm = uv.shape[0]
    u = uv[:, :WIDTH_A]
    v = uv[:, WIDTH_A:]
    ug, tu = _gelu_parts(u)
    vg, tv = _gelu_parts(v)
    mu = jnp.mean(vg, axis=-1, keepdims=True)
    vc = vg - mu
    rs = lax.rsqrt(jnp.mean(vc * vc, axis=-1, keepdims=True) + EPS)
    vhat = vc * rs
    vn = (vhat * g_sgu).astype(MXU_DTYPE)
    masks = _half_masks(MXU_DTYPE)
    chunks = []
    for c in range(tm // CHUNK):
        rows = slice(c * CHUNK, (c + 1) * CHUNK)
        groups = []
        for gp in range(2):
            vn_g = vn[rows, gp * LANES:(gp + 1) * LANES]
            groups.append(_dot(w_ref[2 * gp], vn_g * masks[0]) + _dot(w_ref[2 * gp + 1], vn_g * masks[1]))
        chunks.append(jnp.concatenate(groups, axis=1) + bias)
    mixed = jnp.concatenate(chunks, axis=0)
    return dict(u=u, v=v, ug=ug, tu=tu, tv=tv, rs=rs, vhat=vhat, vn=vn, mixed=mixed, ya=ug * mixed)


def _sgu_fwd(uv, w_tril, bias, g_sgu, g_a):
    s = uv.shape[0]
    tm = 512

    def body(uv_ref, w_ref, b_ref, gs_ref, ga_ref, o_ref):
        t = _sgu_forward_tile(uv_ref[...], w_ref, b_ref[...], gs_ref[...])
        n, _ = _rms_stats(t['ya'])
        o_ref[...] = (n * ga_ref[...]).astype(MXU_DTYPE)

    return pl.pallas_call(
        body, name="sgu_fwd", grid=(s // tm,),
        in_specs=[_rows(tm, 2 * WIDTH_A), _full((HEADS_A, CHUNK, CHUNK)), _full((CHUNK, WIDTH_A)),
                  _full((1, WIDTH_A)), _full((1, WIDTH_A))],
        out_specs=_rows(tm, WIDTH_A), out_shape=jax.ShapeDtypeStruct((s, WIDTH_A), MXU_DTYPE),
        compiler_params=_params(("arbitrary",)),
    )(uv, w_tril, bias, g_sgu, g_a)


def _out_fwd(ya_n, y_b, g_b, w_out, x):
    s = x.shape[0]
    tm = 512
    nc = WIDTH_B // LANES

    def body(ya_ref, *refs):
        yb_refs = refs[:nc]
        g_ref, w_ref, x_ref, h_ref, yn_ref = refs[nc:]
        n, _ = _rms_stats(_load_l256(yb_refs, tm))
        yn = jnp.concatenate([ya_ref[...], (n * g_ref[...]).astype(MXU_DTYPE)], axis=1)
        yn_ref[...] = yn
        h_ref[...] = x_ref[...] + _dot(yn, w_ref[...])

    return pl.pallas_call(
        body, name="out_fwd", grid=(s // tm,),
        in_specs=[_rows(tm, WIDTH_A)] + _col_specs(tm, WIDTH_B) + [_full((1, WIDTH_B)), _full((D_MODEL, D_MODEL)),
                                                                 _rows(tm, D_MODEL)],
        out_specs=[_rows(tm, D_MODEL), _rows(tm, D_MODEL)],
        out_shape=[jax.ShapeDtypeStruct((s, D_MODEL), F32), jax.ShapeDtypeStruct((s, D_MODEL), MXU_DTYPE)],
        compiler_params=_params(("arbitrary",)),
    )(ya_n, *([y_b] * nc), g_b, w_out, x)


def _ffn_fwd(h1, g_ffn, w_gate, w_up, w_down):
    s = h1.shape[0]
    tm = 256

    def body(h_ref, g_ref, wg_ref, wu_ref, wd_ref, o_ref, gate_ref, up_ref, hn_ref):
        h = h_ref[...]
        n, _ = _rms_stats(h)
        hn = (n * g_ref[...]).astype(MXU_DTYPE)
        hn_ref[...] = hn
        out = h
        for c in range(2):
            cols = slice(c * FF_HALF, (c + 1) * FF_HALF)
            gate = _dot(hn, wg_ref[:, cols])
            up = _dot(hn, wu_ref[:, cols])
            gate_ref[:, cols] = gate.astype(MXU_DTYPE)
            up_ref[:, cols] = up.astype(MXU_DTYPE)
            act = (gate * jax.nn.sigmoid(gate) * up).astype(MXU_DTYPE)
            out = out + _dot(act, wd_ref[cols, :])
        o_ref[...] = out

    return pl.pallas_call(
        body, name="ffn_fwd", grid=(s // tm,),
        in_specs=[_rows(tm, D_MODEL), _full((1, D_MODEL)), _full((D_MODEL, D_FF)), _full((D_MODEL, D_FF)),
                  _full((D_FF, D_MODEL))],
        out_specs=[_rows(tm, D_MODEL), _rows(tm, D_FF), _rows(tm, D_FF), _rows(tm, D_MODEL)],
        out_shape=[jax.ShapeDtypeStruct((s, D_MODEL), F32), jax.ShapeDtypeStruct((s, D_FF), MXU_DTYPE),
                   jax.ShapeDtypeStruct((s, D_FF), MXU_DTYPE), jax.ShapeDtypeStruct((s, D_MODEL), MXU_DTYPE)],
        compiler_params=_params(("arbitrary",)),
    )(h1, g_ffn, w_gate, w_up, w_down)


def _ple_loss(h2, p, target, g_ple, w_pg, w_pg_t, w_pp, g_final):
    s = h2.shape[0]
    tm = 256

    def body(h_ref, p_ref, t_ref, gp_ref, wg_ref, wgt_ref, wp_ref, gf_ref,
             loss_ref, dh_ref, dz_ref, dpp_ref, hn_ref, dgp_ref, dgf_ref):
        @pl.when(pl.program_id(0) == 0)
        def _():
            loss_ref[...] = jnp.zeros_like(loss_ref)
            dgp_ref[...] = jnp.zeros_like(dgp_ref)
            dgf_ref[...] = jnp.zeros_like(dgf_ref)

        h2t = h_ref[...]
        n2, r2 = _rms_stats(h2t)
        hn = (n2 * gp_ref[...]).astype(MXU_DTYPE)
        hn_ref[...] = hn
        gate = jax.nn.sigmoid(_dot(hn, wg_ref[...]))
        pp = _dot(p_ref[...].astype(MXU_DTYPE), wp_ref[...])
        h3 = h2t + gate * pp
        n3, r3 = _rms_stats(h3)
        diff = n3 * gf_ref[...] - t_ref[...]
        loss_ref[...] += jnp.full(loss_ref.shape, 0.5 * jnp.sum(diff * diff) / D_MODEL, F32)
        dy = diff * (1.0 / D_MODEL)
        dgf_ref[...] += jnp.sum(dy * n3, axis=0, keepdims=True)
        dh3 = _rms_bwd(dy * gf_ref[...], n3, r3)
        dpp_ref[...] = (dh3 * gate).astype(MXU_DTYPE)
        dz = (dh3 * pp * gate * (1.0 - gate)).astype(MXU_DTYPE)
        dz_ref[...] = dz
        dhn = _dot(dz, wgt_ref[...])
        dgp_ref[...] += jnp.sum(dhn * n2, axis=0, keepdims=True)
        dh_ref[...] = dh3 + _rms_bwd(dhn * gp_ref[...], n2, r2)

    return pl.pallas_call(
        body, name="ple_loss", grid=(s // tm,),
        in_specs=[_rows(tm, D_MODEL), _rows(tm, PLE_DIM), _rows(tm, D_MODEL), _full((1, D_MODEL)),
                  _full((D_MODEL, D_MODEL)), _full((D_MODEL, D_MODEL)), _full((PLE_DIM, D_MODEL)),
                  _full((1, D_MODEL))],
        out_specs=[_full((1, LANES)), _rows(tm, D_MODEL), _rows(tm, D_MODEL), _rows(tm, D_MODEL),
                   _rows(tm, D_MODEL), _full((1, D_MODEL)), _full((1, D_MODEL))],
        out_shape=[jax.ShapeDtypeStruct((1, LANES), F32), jax.ShapeDtypeStruct((s, D_MODEL), F32),
                   jax.ShapeDtypeStruct((s, D_MODEL), MXU_DTYPE), jax.ShapeDtypeStruct((s, D_MODEL), MXU_DTYPE),
                   jax.ShapeDtypeStruct((s, D_MODEL), MXU_DTYPE), jax.ShapeDtypeStruct((1, D_MODEL), F32),
                   jax.ShapeDtypeStruct((1, D_MODEL), F32)],
        compiler_params=_params(("arbitrary",)),
    )(h2, p, target, g_ple, w_pg, w_pg_t, w_pp, g_final)


def _ffn_bwd(dh2, h1, gate, up, g_ffn, w_down_t, w_gate_t, w_up_t):
    s = h1.shape[0]
    tm = 256

    def body(dh_ref, h_ref, gate_ref, up_ref, g_ref, wdt_ref, wgt_ref, wut_ref,
             o_ref, act_ref, dg_ref, du_ref, dgn_ref):
        @pl.when(pl.program_id(0) == 0)
        def _():
            dgn_ref[...] = jnp.zeros_like(dgn_ref)

        dh = dh_ref[...]
        dhb = dh.astype(MXU_DTYPE)
        dhn = jnp.zeros((tm, D_MODEL), F32)
        for c in range(2):
            cols = slice(c * FF_HALF, (c + 1) * FF_HALF)
            dact = _dot(dhb, wdt_ref[:, cols])
            g = gate_ref[:, cols].astype(F32)
            u = up_ref[:, cols].astype(F32)
            sg = jax.nn.sigmoid(g)
            silu = g * sg
            act_ref[:, cols] = (silu * u).astype(MXU_DTYPE)
            du = (dact * silu).astype(MXU_DTYPE)
            dg = (dact * u * sg * (1.0 + g * (1.0 - sg))).astype(MXU_DTYPE)
            du_ref[:, cols] = du
            dg_ref[:, cols] = dg
            dhn = dhn + _dot(dg, wgt_ref[cols, :]) + _dot(du, wut_ref[cols, :])
        n, r = _rms_stats(h_ref[...])
        dgn_ref[...] += jnp.sum(dhn * n, axis=0, keepdims=True)
        o_ref[...] = dh + _rms_bwd(dhn * g_ref[...], n, r)

    return pl.pallas_call(
        body, name="ffn_bwd", grid=(s // tm,),
        in_specs=[_rows(tm, D_MODEL), _rows(tm, D_MODEL), _rows(tm, D_FF), _rows(tm, D_FF), _full((1, D_MODEL)),
                  _full((D_MODEL, D_FF)), _full((D_FF, D_MODEL)), _full((D_FF, D_MODEL))],
        out_specs=[_rows(tm, D_MODEL), _rows(tm, D_FF), _rows(tm, D_FF), _rows(tm, D_FF), _full((1, D_MODEL))],
        out_shape=[jax.ShapeDtypeStruct((s, D_MODEL), F32), jax.ShapeDtypeStruct((s, D_FF), MXU_DTYPE),
                   jax.ShapeDtypeStruct((s, D_FF), MXU_DTYPE), jax.ShapeDtypeStruct((s, D_FF), MXU_DTYPE),
                   jax.ShapeDtypeStruct((1, D_MODEL), F32)],
        compiler_params=_params(("arbitrary",)),
    )(dh2, h1, gate, up, g_ffn, w_down_t, w_gate_t, w_up_t)


def _out_bwd(dh1, y_b, g_b, w_out_t):
    s = dh1.shape[0]
    tm = 512
    nc = WIDTH_B // LANES

    def body(dh_ref, *refs):
        yb_refs = refs[:nc]
        g_ref, wt_ref, dya_ref, dyb_ref, dg_ref, scr = refs[nc:]

        @pl.when(pl.program_id(0) == 0)
        def _():
            dg_ref[...] = jnp.zeros_like(dg_ref)

        dy = _dot(dh_ref[...].astype(MXU_DTYPE), wt_ref[...])
        dya_ref[...] = dy[:, :WIDTH_A]
        dyb = dy[:, WIDTH_A:]
        n, r = _rms_stats(_load_l256(yb_refs, tm))
        dg_ref[...] += jnp.sum(dyb * n, axis=0, keepdims=True)
        dyb_in = _rms_bwd(dyb * g_ref[...], n, r)
        for j in range(nc):
            cols = slice(j * LANES, (j + 1) * LANES)
            _store_l256(scr, dyb_ref, cols, dyb_in[:, cols])

    return pl.pallas_call(
        body, name="out_bwd", grid=(s // tm,), scratch_shapes=[pltpu.VMEM((tm, LANES), F32)],
        in_specs=[_rows(tm, D_MODEL)] + _col_specs(tm, WIDTH_B) + [_full((1, WIDTH_B)), _full((D_MODEL, D_MODEL))],
        out_specs=[_rows(tm, WIDTH_A), _rows(tm, WIDTH_B), _full((1, WIDTH_B))],
        out_shape=[jax.ShapeDtypeStruct((s, WIDTH_A), F32), jax.ShapeDtypeStruct((s, WIDTH_B), F32),
                   jax.ShapeDtypeStruct((1, WIDTH_B), F32)],
        compiler_params=_params(("arbitrary",)),
    )(dh1, *([y_b] * nc), g_b, w_out_t)


def _attn_bwd_branch(q, k, v, do, o, lse, grads, dil):
    s = q.shape[0]
    br = _Branch(dil, s)
    qn, nb = br.qn, br.nb
    first = grads is None

    def body(*refs):
        bias_ref, q_ref, kc_ref, kp_ref, vc_ref, vp_ref, do_ref, o_ref, lse_ref = refs[:9]
        if first:
            rest = refs[9:]
        else:
            dq_in, dk_in, dv_in = refs[9:12]
            rest = refs[12:]
        dq_ref, dk_ref, dv_ref, dk_carry, dv_carry = rest
        n = pl.program_id(1)

        @pl.when(n == 0)
        def _():
            dk_carry[...] = jnp.zeros_like(dk_carry)
            dv_carry[...] = jnp.zeros_like(dv_carry)

        @pl.when(n < nb)
        def _():
            bias2 = jnp.concatenate([bias_ref[0], bias_ref[0]], axis=0)
            lane = lax.broadcasted_iota(jnp.int32, (qn, LANES), 1)
            lo = lane < HEAD_DIM
            mask_f = lo.astype(F32)
            mask_lo = mask_f.astype(MXU_DTYPE)
            lse = br.load(lse_ref)
            for hp in range(HEADS_B // 2):
                cols = slice(hp * LANES, (hp + 1) * LANES)
                h0, h1 = 2 * hp, 2 * hp + 1
                qp = br.load(q_ref, cols)
                kcat = jnp.concatenate([br.load(kp_ref, cols), br.load(kc_ref, cols)], axis=0)
                vcat = jnp.concatenate([br.load(vp_ref, cols), br.load(vc_ref, cols)], axis=0)
                dop = br.load(do_ref, cols)
                prod = dop * br.load(o_ref, cols)
                prod_lo = prod * mask_f
                delta = jnp.concatenate([jnp.sum(prod_lo, axis=1, keepdims=True),
                                         jnp.sum(prod - prod_lo, axis=1, keepdims=True)], axis=0)
                qs = jnp.concatenate([qp * mask_lo, qp * (1 - mask_lo)], axis=0)
                dos = jnp.concatenate([dop * mask_f, dop * (1.0 - mask_f)], axis=0).astype(MXU_DTYPE)
                lse2 = jnp.concatenate([lse[:, h0:h0 + 1], lse[:, h1:h1 + 1]], axis=0)
                p = jnp.exp(_dot_nt(qs, kcat) + bias2 - lse2)
                ds = (p * (_dot_nt(dos, vcat) - delta)).astype(MXU_DTYPE)
                dvc = _dot_tn(p.astype(MXU_DTYPE), dos)
                dkc = _dot_tn(ds, qs)
                dq2 = _dot(ds, kcat)
                dq = jnp.where(lo, dq2[:qn], dq2[qn:])
                dk_prev = dk_carry[:, cols] + dkc[:qn]
                dv_prev = dv_carry[:, cols] + dvc[:qn]
                if not first:
                    dq = dq + br.load(dq_in, cols)
                    dk_prev = dk_prev + br.load(dk_in, cols)
                    dv_prev = dv_prev + br.load(dv_in, cols)
                br.store(dq_ref, cols, dq)
                br.store(dk_ref, cols, dk_prev)
                br.store(dv_ref, cols, dv_prev)
                dk_carry[:, cols] = dkc[qn:]
                dv_carry[:, cols] = dvc[qn:]

        @pl.when(n == nb)
        def _():
            dk_last = dk_carry[...]
            dv_last = dv_carry[...]
            if not first:
                dk_last = dk_last + br.load(dk_in)
                dv_last = dv_last + br.load(dv_in)
            br.store(dk_ref, slice(None), dk_last)
            br.store(dv_ref, slice(None), dv_last)

    cur = lambda n: jnp.minimum(n, nb - 1)
    before = lambda n: jnp.maximum(cur(n) - 1, 0)
    late = lambda n: jnp.maximum(n - 1, 0)
    in_specs = [br.bias_spec(cur), br.spec(WIDTH_B, cur), br.spec(WIDTH_B, cur), br.spec(WIDTH_B, before),
                br.spec(WIDTH_B, cur), br.spec(WIDTH_B, before), br.spec(WIDTH_B, cur), br.spec(WIDTH_B, cur),
                br.spec(LANES, cur)]
    args = [jnp.asarray(br.bias)] + [br.view(a) for a in (q, k, k, v, v, do, o, lse)]
    if not first:
        in_specs += [br.spec(WIDTH_B, cur), br.spec(WIDTH_B, late), br.spec(WIDTH_B, late)]
        args += [br.view(g) for g in grads]
    res = pl.pallas_call(
        body, name="attn_bwd_d%d" % dil, grid=(br.grid[0], nb + 1), in_specs=in_specs,
        out_specs=[br.spec(WIDTH_B, cur), br.spec(WIDTH_B, late), br.spec(WIDTH_B, late)],
        out_shape=[jax.ShapeDtypeStruct((s // L_BLOCK, 4, 4, L_GROUP, WIDTH_B), F32)] * 3,
        scratch_shapes=[pltpu.VMEM((qn, WIDTH_B), F32), pltpu.VMEM((qn, WIDTH_B), F32)],
        compiler_params=_params(("arbitrary", "arbitrary")),
    )(*args)
    return tuple(a.reshape(s, WIDTH_B) for a in res)


def _sgu_bwd(uv, dya_n, w_tril, w_tril_t, bias, g_sgu, g_a):
    s = uv.shape[0]
    tm = 512

    def body(uv_ref, dy_ref, w_ref, wt_ref, b_ref, gs_ref, ga_ref, duv_ref, dw_ref, db_ref, dgs_ref, dga_ref,
             db_acc):
        i = pl.program_id(0)

        @pl.when(i == 0)
        def _():
            dw_ref[...] = jnp.zeros_like(dw_ref)
            dgs_ref[...] = jnp.zeros_like(dgs_ref)
            dga_ref[...] = jnp.zeros_like(dga_ref)
            db_acc[...] = jnp.zeros_like(db_acc)

        t = _sgu_forward_tile(uv_ref[...], w_ref, b_ref[...], gs_ref[...])
        na, ra = _rms_stats(t['ya'])
        dyn = dy_ref[...]
        dga_ref[...] += jnp.sum(dyn * na, axis=0, keepdims=True)
        dya = _rms_bwd(dyn * ga_ref[...], na, ra)
        dug = dya * t['mixed']
        dmixed = dya * t['ug']
        dmb = dmixed.astype(MXU_DTYPE)
        masks = _half_masks(MXU_DTYPE)
        chunks = []
        db = jnp.zeros((CHUNK, WIDTH_A), F32)
        for c in range(tm // CHUNK):
            rows = slice(c * CHUNK, (c + 1) * CHUNK)
            db = db + dmixed[rows]
            groups = []
            for gp in range(2):
                cols = slice(gp * LANES, (gp + 1) * LANES)
                dm_g = dmb[rows, cols]
                vn_g = t['vn'][rows, cols]
                dvn_g = jnp.zeros((CHUNK, LANES), F32)
                for j in range(2):
                    dm_h = dm_g * masks[j]
                    dvn_g = dvn_g + _dot(wt_ref[2 * gp + j], dm_h)
                    dw_ref[2 * gp + j] += _dot_nt(dm_h, vn_g)
                groups.append(dvn_g)
            chunks.append(jnp.concatenate(groups, axis=1))
        db_acc[...] += db
        dvn = jnp.concatenate(chunks, axis=0)
        vhat = t['vhat']
        dgs_ref[...] += jnp.sum(dvn * vhat, axis=0, keepdims=True)
        dvh = dvn * gs_ref[...]
        dvg = t['rs'] * (dvh - jnp.mean(dvh, axis=-1, keepdims=True)
                         - vhat * jnp.mean(dvh * vhat, axis=-1, keepdims=True))
        duv_ref[:, :WIDTH_A] = (dug * _gelu_grad(t['u'], t['tu'])).astype(MXU_DTYPE)
        duv_ref[:, WIDTH_A:] = (dvg * _gelu_grad(t['v'], t['tv'])).astype(MXU_DTYPE)

        @pl.when(i == pl.num_programs(0) - 1)
        def _():
            lane_a = lax.broadcasted_iota(jnp.int32, (CHUNK, WIDTH_A), 1)
            lane = lax.broadcasted_iota(jnp.int32, (CHUNK, LANES), 1)
            acc = db_acc[...]
            out = jnp.zeros((CHUNK, LANES), F32)
            for h in range(HEADS_A):
                col = jnp.sum(jnp.where(lane_a // HEAD_DIM == h, acc, 0.0), axis=1, keepdims=True)
                out = jnp.where(lane == h, col, out)
            db_ref[...] = out
            causal = (lax.broadcasted_iota(jnp.int32, (CHUNK, CHUNK), 0)
                      >= lax.broadcasted_iota(jnp.int32, (CHUNK, CHUNK), 1))
            for h in range(HEADS_A):
                dw_ref[h] = jnp.where(causal, dw_ref[h], 0.0)

    return pl.pallas_call(
        body, name="sgu_bwd", grid=(s // tm,),
        in_specs=[_rows(tm, 2 * WIDTH_A), _rows(tm, WIDTH_A), _full((HEADS_A, CHUNK, CHUNK)),
                  _full((HEADS_A, CHUNK, CHUNK)), _full((CHUNK, WIDTH_A)), _full((1, WIDTH_A)),
                  _full((1, WIDTH_A))],
        out_specs=[_rows(tm, 2 * WIDTH_A), _full((HEADS_A, CHUNK, CHUNK)), _full((CHUNK, LANES)),
                   _full((1, WIDTH_A)), _full((1, WIDTH_A))],
        out_shape=[jax.ShapeDtypeStruct((s, 2 * WIDTH_A), MXU_DTYPE),
                   jax.ShapeDtypeStruct((HEADS_A, CHUNK, CHUNK), F32), jax.ShapeDtypeStruct((CHUNK, LANES), F32),
                   jax.ShapeDtypeStruct((1, WIDTH_A), F32), jax.ShapeDtypeStruct((1, WIDTH_A), F32)],
        scratch_shapes=[pltpu.VMEM((CHUNK, WIDTH_A), F32)],
        compiler_params=_params(("arbitrary",)),
    )(uv, dya_n, w_tril, w_tril_t, bias, g_sgu, g_a)


def _in_bwd_proj(duv, dq, dk, dv, cos_t, sin_t):
    s = duv.shape[0]
    tm = 512
    nc = WIDTH_B // LANES

    def body(duv_ref, *refs):
        dq_refs, dk_refs, dv_refs = refs[:nc], refs[nc:2 * nc], refs[2 * nc:3 * nc]
        cos_ref, sin_ref, dp_ref = refs[3 * nc:]
        cos = cos_ref[...]
        sin = sin_ref[...]
        dp_ref[:, :2 * WIDTH_A] = duv_ref[...]
        for i in range(nc):
            lo = 2 * WIDTH_A + i * LANES
            tq = _load_l256(dq_refs[i:i + 1], tm) * (HEAD_DIM ** -0.5)
            tk = _load_l256(dk_refs[i:i + 1], tm)
            dp_ref[:, lo:lo + LANES] = (tq * cos + _rope_partner(tq * sin)).astype(MXU_DTYPE)
            dp_ref[:, lo + WIDTH_B:lo + WIDTH_B + LANES] = (tk * cos + _rope_partner(tk * sin)).astype(MXU_DTYPE)
            dp_ref[:, lo + 2 * WIDTH_B:lo + 2 * WIDTH_B + LANES] = _load_l256(dv_refs[i:i + 1], tm).astype(MXU_DTYPE)

    return pl.pallas_call(
        body, name="in_bwd_proj", grid=(s // tm,),
        in_specs=[_rows(tm, 2 * WIDTH_A)] + 3 * _col_specs(tm, WIDTH_B) + [_rows(tm, LANES), _rows(tm, LANES)],
        out_specs=_rows(tm, IN_COLS), out_shape=jax.ShapeDtypeStruct((s, IN_COLS), MXU_DTYPE),
        compiler_params=_params(("arbitrary",)),
    )(duv, *([dq] * nc), *([dk] * nc), *([dv] * nc), cos_t, sin_t)


def _in_bwd_x(dproj, w_in_t, x, g_mix, dh1):
    s = x.shape[0]
    tm = 512

    def body(dp_ref, wt_ref, x_ref, g_ref, dh_ref, gx_ref, dg_ref):
        @pl.when(pl.program_id(0) == 0)
        def _():
            dg_ref[...] = jnp.zeros_like(dg_ref)

        dhn = _dot(dp_ref[...], wt_ref[...])
        n, r = _rms_stats(x_ref[...])
        dg_ref[...] += jnp.sum(dhn * n, axis=0, keepdims=True)
        gx_ref[...] = dh_ref[...] + _rms_bwd(dhn * g_ref[...], n, r)

    return pl.pallas_call(
        body, name="in_bwd_x", grid=(s // tm,),
        in_specs=[_rows(tm, IN_COLS), _full((IN_COLS, D_MODEL)), _rows(tm, D_MODEL), _full((1, D_MODEL)),
                  _rows(tm, D_MODEL)],
        out_specs=[_rows(tm, D_MODEL), _full((1, D_MODEL))],
        out_shape=[jax.ShapeDtypeStruct((s, D_MODEL), F32), jax.ShapeDtypeStruct((1, D_MODEL), F32)],
        compiler_params=_params(("arbitrary",)),
    )(dproj, w_in_t, x, g_mix, dh1)


def _wgrad(a, bs, name, transposed=False):
    s, m = a.shape
    bm = 512 if m % 512 == 0 else (FF_HALF if m == D_FF else m)
    ts = 512
    nsteps = s // ts
    nb = len(bs)

    def body(a_ref, *refs):
        b_refs, o_refs, accs = refs[:nb], refs[nb:2 * nb], refs[2 * nb:]
        kk = pl.program_id(1)
        at = a_ref[...].astype(MXU_DTYPE)
        for b_ref, o_ref, acc in zip(b_refs, o_refs, accs):
            c = _dot_tn(at, b_ref[...].astype(MXU_DTYPE))

            @pl.when(kk == 0)
            def _():
                acc[...] = c

            @pl.when(kk > 0)
            def _():
                acc[...] += c

            @pl.when(kk == nsteps - 1)
            def _():
                total = acc[...]
                o_ref[...] = (total.T if transposed else total).astype(o_ref.dtype)

    if transposed:
        out_specs = [pl.BlockSpec((b.shape[1], bm), lambda i, kk: (0, i)) for b in bs]
        out_shape = [jax.ShapeDtypeStruct((b.shape[1], m), jnp.bfloat16) for b in bs]
    else:
        out_specs = [pl.BlockSpec((bm, b.shape[1]), lambda i, kk: (i, 0)) for b in bs]
        out_shape = [jax.ShapeDtypeStruct((m, b.shape[1]), jnp.bfloat16) for b in bs]
    return pl.pallas_call(
        body, name=name, grid=(m // bm, nsteps),
        in_specs=[pl.BlockSpec((ts, bm), lambda i, kk: (kk, i))]
        + [pl.BlockSpec((ts, b.shape[1]), lambda i, kk: (kk, 0)) for b in bs],
        out_specs=out_specs, out_shape=out_shape,
        scratch_shapes=[pltpu.VMEM((bm, b.shape[1]), F32) for b in bs],
        compiler_params=_params(("arbitrary", "arbitrary")),
    )(a, *bs)


def _rope_tables(s):
    half = HEAD_DIM // 2
    inv = ROPE_THETA ** (-jnp.arange(half, dtype=F32) / half)
    ang = jnp.arange(s, dtype=F32)[:, None] * inv[None, :]
    cos = jnp.cos(ang)
    sin = jnp.sin(ang)
    cos_t = jnp.concatenate([cos, cos, cos, cos], axis=1)
    sin_t = jnp.concatenate([-sin, sin, -sin, sin], axis=1)
    return cos_t, sin_t


MESH = pl.DeviceIdType.MESH
ANY = pl.BlockSpec(memory_space=pl.ANY)
SEM = pl.BlockSpec(memory_space=pltpu.SEMAPHORE)
SPLIT_COPY = pltpu.CompilerParams(has_side_effects=pltpu.SideEffectType.DATAFLOW_SIDE_EFFECTING)
SLAB_IS_TRANSPOSED = {'w_in': True, 'w_out': False, 'w_gate': True, 'w_up': True, 'w_down': False,
                      'w_ple_gate': False, 'w_ple_proj': True}


def _place():
    x, y, c = lax.axis_index("x"), lax.axis_index("y"), lax.axis_index("c")
    other_chips = [(1 - x, y), (x, 1 - y), (1 - x, 1 - y)]
    return x, y, c, other_chips


def _chip_of(chip):
    return 2 * chip[0] + chip[1]


def _half(ref, lead, hc):
    hr = ref.shape[1] // 2
    return ref.at[lead, pl.ds(hc * hr, hr), :]


def _put_own(stack, own, index):
    return lax.dynamic_update_slice(stack, own[None], (index,) + (0,) * own.ndim)


def _all_gather_now(slab):
    rows, cols = slab.shape

    def body(x_ref, out_ref, send_sems, recv_sems):
        x, y, c, chips = _place()
        sibling = (x, y, 1 - c)
        hr = rows // 2

        def copy(k, src, dst, to):
            return pltpu.make_async_remote_copy(src_ref=src, dst_ref=dst, send_sem=send_sems.at[k],
                                                recv_sem=recv_sems.at[k], device_id=to, device_id_type=MESH)

        my_half = x_ref.at[pl.ds(c * hr, hr), :]
        first = [copy(j, my_half, _half(out_ref, 2 * x + y, c), (*chip, c)) for j, chip in enumerate(chips)]
        for cp in first:
            cp.start()
        passed = [copy(3 + j, _half(out_ref, _chip_of(chip), c), _half(out_ref, _chip_of(chip), c), sibling)
                  for j, chip in enumerate(chips)]
        for j, chip in enumerate(chips):
            copy(j, my_half, _half(out_ref, _chip_of(chip), c), (*chip, c)).wait_recv()
            passed[j].start()
        for j, chip in enumerate(chips):
            copy(3 + j, my_half, _half(out_ref, _chip_of(chip), 1 - c), sibling).wait_recv()
        for cp in first + passed:
            cp.wait_send()

    gathered = pl.pallas_call(
        body, name="all_gather_now", out_shape=jax.ShapeDtypeStruct((N_CHIPS, rows, cols), slab.dtype),
        in_specs=[ANY], out_specs=ANY,
        scratch_shapes=[pltpu.SemaphoreType.DMA((6,)), pltpu.SemaphoreType.DMA((6,))],
    )(slab)
    me = 2 * lax.axis_index("x") + lax.axis_index("y")
    return _put_own(gathered, slab, me).reshape(N_CHIPS * rows, cols)


def _gather_copies(slab_refs, land_refs, send_sems, recv_sems):
    x, y, c, chips = _place()
    sends, recvs = [], []
    for k, (src, land) in enumerate(zip(slab_refs, land_refs)):
        hr = src.shape[0] // 2
        for j, chip in enumerate(chips):
            for t in range(2):
                sends.append(pltpu.make_async_remote_copy(
                    src_ref=src.at[pl.ds(c * hr, hr), :], dst_ref=_half(land, 2 * x + y, c),
                    send_sem=send_sems.at[6 * k + 2 * j + t], recv_sem=recv_sems.at[6 * k + 2 * j + c],
                    device_id=(*chip, t), device_id_type=MESH))
                recvs.append(pltpu.make_async_remote_copy(
                    src_ref=src.at[pl.ds(t * hr, hr), :], dst_ref=_half(land, _chip_of(chip), t),
                    send_sem=send_sems.at[6 * k + 2 * j + t], recv_sem=recv_sems.at[6 * k + 2 * j + t],
                    device_id=(*chip, t), device_id_type=MESH))
    return sends, recvs


def _all_gather_start(slabs):
    n = len(slabs)

    def body(*refs):
        slab_refs, land_refs = refs[:n], refs[n:2 * n]
        send_sems, recv_sems = refs[2 * n:2 * n + 2]
        token = refs[-1]
        sends, _ = _gather_copies(slab_refs, land_refs, send_sems, recv_sems)
        for cp in sends:
            cp.start()
        token[...] = jnp.zeros_like(token)

    lands = [lax.empty((N_CHIPS,) + s.shape, s.dtype) for s in slabs]
    hbm = lambda a: pltpu.HBM(a.shape, a.dtype)
    res = pl.pallas_call(
        body, name="all_gather_start",
        out_shape=(pltpu.SemaphoreType.DMA((6 * n,)), pltpu.SemaphoreType.DMA((6 * n,)), *map(hbm, slabs),
                   *map(hbm, lands), jax.ShapeDtypeStruct((8, LANES), F32)),
        in_specs=[ANY] * (2 * n), out_specs=(SEM, SEM, *([ANY] * (2 * n)), pl.BlockSpec(memory_space=pltpu.VMEM)),
        input_output_aliases={i: 2 + i for i in range(2 * n)}, compiler_params=SPLIT_COPY,
    )(*[pltpu.with_memory_space_constraint(a, pltpu.HBM) for a in list(slabs) + lands])
    return res[:-1], res[-1]


def _all_gather_wait(handle, after):
    send_sems, recv_sems = handle[:2]
    n = (len(handle) - 2) // 2
    slabs, lands = handle[2:2 + n], handle[2 + n:]

    def body(*refs):
        slab_refs, land_refs = refs[:n], refs[n:2 * n]
        send_sems, recv_sems = refs[2 * n:2 * n + 2]
        sends, recvs = _gather_copies(slab_refs, land_refs, send_sems, recv_sems)
        for cp in sends:
            cp.wait_send()
        for cp in recvs:
            cp.wait_recv()

    hbm = lambda a: pltpu.HBM(a.shape, a.dtype)
    res = pl.pallas_call(
        body, name="all_gather_wait", out_shape=tuple(map(hbm, list(slabs) + list(lands))),
        in_specs=[ANY] * (2 * n) + [SEM, SEM, ANY], out_specs=tuple([ANY] * (2 * n)),
        input_output_aliases={i: i for i in range(2 * n)}, compiler_params=SPLIT_COPY,
    )(*slabs, *lands, send_sems, recv_sems, after)
    me = 2 * lax.axis_index("x") + lax.axis_index("y")
    return [_put_own(land, slab, me).reshape(N_CHIPS * slab.shape[0], slab.shape[1])
            for slab, land in zip(res[:n], res[n:])]


def _scatter_copies(part_refs, land_refs, send_sems, recv_sems):
    x, y, c, chips = _place()
    me = 4 * x + 2 * y + c
    sends, recvs = [], []
    for k, (part, land) in enumerate(zip(part_refs, land_refs)):
        for j, chip in enumerate(chips):
            for h in range(2):
                sends.append(pltpu.make_async_remote_copy(
                    src_ref=_half(part, _chip_of(chip), h), dst_ref=land.at[me],
                    send_sem=send_sems.at[7 * k + 2 * j + h], recv_sem=recv_sems.at[7 * k + 2 * j + c],
                    device_id=(*chip, h), device_id_type=MESH))
                recvs.append(pltpu.make_async_remote_copy(
                    src_ref=_half(part, _chip_of(chip), h), dst_ref=land.at[2 * _chip_of(chip) + h],
                    send_sem=send_sems.at[7 * k + 2 * j + h], recv_sem=recv_sems.at[7 * k + 2 * j + h],
                    device_id=(*chip, h), device_id_type=MESH))
        sends.append(pltpu.make_async_remote_copy(
            src_ref=_half(part, 2 * x + y, 1 - c), dst_ref=land.at[me], send_sem=send_sems.at[7 * k + 6],
            recv_sem=recv_sems.at[7 * k + 6], device_id=(x, y, 1 - c), device_id_type=MESH))
        recvs.append(pltpu.make_async_remote_copy(
            src_ref=_half(part, 2 * x + y, 1 - c), dst_ref=land.at[4 * x + 2 * y + 1 - c],
            send_sem=send_sems.at[7 * k + 6], recv_sem=recv_sems.at[7 * k + 6], device_id=(x, y, 1 - c),
            device_id_type=MESH))
    return sends, recvs


def _reduce_scatter_start(parts, name):
    n = len(parts)
    parts = [p.reshape(N_CHIPS, p.shape[0] // N_CHIPS, p.shape[1]) for p in parts]

    def body(*refs):
        part_refs, land_refs = refs[:n], refs[n:2 * n]
        send_sems, recv_sems = refs[2 * n:2 * n + 2]
        token = refs[-1]
        sends, _ = _scatter_copies(part_refs, land_refs, send_sems, recv_sems)
        for cp in sends:
            cp.start()
        token[...] = jnp.zeros_like(token)

    lands = [lax.empty((N_DEV, p.shape[1] // 2, p.shape[2]), p.dtype) for p in parts]
    hbm = lambda a: pltpu.HBM(a.shape, a.dtype)
    res = pl.pallas_call(
        body, name=name,
        out_shape=(pltpu.SemaphoreType.DMA((7 * n,)), pltpu.SemaphoreType.DMA((7 * n,)), *map(hbm, parts),
                   *map(hbm, lands), jax.ShapeDtypeStruct((8, LANES), F32)),
        in_specs=[ANY] * (2 * n), out_specs=(SEM, SEM, *([ANY] * (2 * n)), pl.BlockSpec(memory_space=pltpu.VMEM)),
        input_output_aliases={i: 2 + i for i in range(2 * n)}, compiler_params=SPLIT_COPY,
    )(*[pltpu.with_memory_space_constraint(a, pltpu.HBM) for a in parts + lands])
    return res[:-1], res[-1]


def _reduce_scatter_wait(handle, after, name):
    send_sems, recv_sems = handle[:2]
    n = (len(handle) - 2) // 2
    parts, lands = handle[2:2 + n], handle[2 + n:]

    def body(*refs):
        part_refs, land_refs = refs[:n], refs[n:2 * n]
        send_sems, recv_sems = refs[2 * n:2 * n + 2]
        sends, recvs = _scatter_copies(part_refs, land_refs, send_sems, recv_sems)
        for cp in sends:
            cp.wait_send()
        for cp in recvs:
            cp.wait_recv()

    hbm = lambda a: pltpu.HBM(a.shape, a.dtype)
    res = pl.pallas_call(
        body, name=name, out_shape=tuple(map(hbm, list(parts) + list(lands))),
        in_specs=[ANY] * (2 * n) + [SEM, SEM, ANY], out_specs=tuple([ANY] * (2 * n)),
        input_output_aliases={i: i for i in range(2 * n)}, compiler_params=SPLIT_COPY,
    )(*parts, *lands, send_sems, recv_sems, after)
    x, y, c = lax.axis_index("x"), lax.axis_index("y"), lax.axis_index("c")
    out = []
    for part, land in zip(res[:n], res[n:]):
        hr = land.shape[1]
        own = lax.dynamic_slice(part, (2 * x + y, c * hr, 0), (1, hr, part.shape[2]))[0]
        out.append(_put_own(land, own, 4 * x + 2 * y + c))
    return out


def _sum_shares(land, name):
    n, rows, cols = land.shape

    def body(l_ref, o_ref):
        acc = l_ref[0].astype(F32)
        for i in range(1, n):
            acc = acc + l_ref[i].astype(F32)
        o_ref[...] = acc

    return pl.pallas_call(
        body, name=name, grid=(1,), in_specs=[pl.BlockSpec((n, rows, cols), lambda i: (0, 0, 0))],
        out_specs=pl.BlockSpec((rows, cols), lambda i: (0, 0)), out_shape=jax.ShapeDtypeStruct((rows, cols), F32),
        compiler_params=_params(("arbitrary",)),
    )(land)


def _swap_halves(halves):
    n = len(halves)

    def body(*refs):
        in_refs, out_refs, send_sems, recv_sems = refs[:n], refs[n:2 * n], refs[2 * n], refs[2 * n + 1]
        x, y, c, _ = _place()
        copies = [pltpu.make_async_remote_copy(src_ref=i_ref, dst_ref=o_ref, send_sem=send_sems.at[k],
                                               recv_sem=recv_sems.at[k], device_id=(x, y, 1 - c), device_id_type=MESH)
                  for k, (i_ref, o_ref) in enumerate(zip(in_refs, out_refs))]
        for cp in copies:
            cp.start()
        for cp in copies:
            cp.wait()

    others = pl.pallas_call(
        body, name="swap_halves", out_shape=[jax.ShapeDtypeStruct(h.shape, h.dtype) for h in halves],
        in_specs=[ANY] * n, out_specs=[ANY] * n,
        scratch_shapes=[pltpu.SemaphoreType.DMA((n,)), pltpu.SemaphoreType.DMA((n,))],
    )(*halves)
    south = lax.axis_index("c") == 0
    return [jnp.concatenate([jnp.where(south, h, o), jnp.where(south, o, h)], axis=0) for h, o in zip(halves, others)]


def _small_all_reduce(block):
    rows = block.shape[0]

    def body(x_ref, all_ref, sum_ref, send_sems, recv_sems, local_sem):
        x, y, c, chips = _place()
        me, sibling = (x, y, c), (x, y, 1 - c)

        def blk(px, py, pc):
            return all_ref.at[pl.ds((4 * px + 2 * py + pc) * rows, rows), :]

        def copy(k, who, to, src=None):
            return pltpu.make_async_remote_copy(
                src_ref=blk(*who) if src is None else src, dst_ref=blk(*who), send_sem=send_sems.at[k],
                recv_sem=recv_sems.at[k], device_id=to, device_id_type=MESH)

        mine = pltpu.make_async_copy(x_ref, blk(*me), local_sem)
        mine.start()
        first = [copy(0, me, sibling, src=x_ref)]
        first += [copy(1 + j, me, (*chip, c), src=x_ref) for j, chip in enumerate(chips)]
        for cp in first:
            cp.start()
        passed = [copy(4 + j, (*chip, c), sibling) for j, chip in enumerate(chips)]
        for j, chip in enumerate(chips):
            copy(1 + j, (*chip, c), me).wait_recv()
            passed[j].start()
        copy(0, sibling, me).wait_recv()
        for j, chip in enumerate(chips):
            copy(4 + j, (*chip, 1 - c), me).wait_recv()
        for cp in first + passed:
            cp.wait_send()
        mine.wait()
        acc = all_ref[pl.ds(0, rows), :]
        for dev in range(1, N_DEV):
            acc = acc + all_ref[pl.ds(dev * rows, rows), :]
        sum_ref[...] = acc

    vmem = pl.BlockSpec(memory_space=pltpu.VMEM)
    return pl.pallas_call(
        body, name="small_all_reduce",
        out_shape=[jax.ShapeDtypeStruct((N_DEV * rows, D_MODEL), F32), jax.ShapeDtypeStruct((rows, D_MODEL), F32)],
        in_specs=[vmem], out_specs=[vmem, vmem],
        scratch_shapes=[pltpu.SemaphoreType.DMA((7,)), pltpu.SemaphoreType.DMA((7,)), pltpu.SemaphoreType.DMA],
    )(block)[1]


def _adamw(w, g, m, v, name):
    rows, cols = w.shape
    tm = rows
    if rows > 512:
        tm = next(t for t in range(512, 7, -8) if rows % t == 0)

    def body(w_ref, g_ref, m_ref, v_ref, d_ref, nm_ref, nv_ref):
        g_ = g_ref[...]
        m_ = ADAM_B1 * m_ref[...] + (1.0 - ADAM_B1) * g_
        v_ = ADAM_B2 * v_ref[...] + (1.0 - ADAM_B2) * (g_ * g_)
        m_hat = m_ / (1.0 - ADAM_B1 ** ADAM_STEP)
        v_hat = v_ / (1.0 - ADAM_B2 ** ADAM_STEP)
        d_ref[...] = -ADAM_LR * (m_hat / (jnp.sqrt(v_hat) + ADAM_EPS) + ADAM_WD * w_ref[...])
        nm_ref[...] = m_
        nv_ref[...] = v_

    spec = pl.BlockSpec((tm, cols), lambda i: (i, 0))
    return pl.pallas_call(
        body, name=name, grid=(rows // tm,), in_specs=[spec] * 4, out_specs=[spec] * 3,
        out_shape=[jax.ShapeDtypeStruct(w.shape, F32)] * 3, compiler_params=_params(("arbitrary",)),
    )(w, g, m, v)


def _pack_small(values):
    flat = jnp.concatenate([values[n].reshape(-1).astype(F32) for n in SMALL])
    return jnp.pad(flat, (0, SMALL_ROWS * D_MODEL - flat.shape[0])).reshape(SMALL_ROWS, D_MODEL)


def _unpack_small(block, shapes):
    flat = block.reshape(-1)
    out, lo = {}, 0
    for n in SMALL:
        out[n] = flat[lo:lo + SMALL_SIZES[n]].reshape(shapes[n])
        lo += SMALL_SIZES[n]
    return out


def _after(token, a):
    return lax.optimization_barrier((token, a))[1]


def kernel(x, p, mix_norm_g, w_in, sgu_w, sgu_b, sgu_norm_g, out_norm_a, out_norm_b, w_out, ffn_norm_g, w_gate, w_up, w_down, ple_norm_g, w_ple_gate, w_ple_proj, final_norm_g, loss_target, m_mix_norm_g, m_w_in, m_sgu_w, m_sgu_b, m_sgu_norm_g, m_out_norm_a, m_out_norm_b, m_w_out, m_ffn_norm_g, m_w_gate, m_w_up, m_w_down, m_ple_norm_g, m_w_ple_gate, m_w_ple_proj, m_final_norm_g, v_mix_norm_g, v_w_in, v_sgu_w, v_sgu_b, v_sgu_norm_g, v_out_norm_a, v_out_norm_b, v_w_out, v_ffn_norm_g, v_w_gate, v_w_up, v_w_down, v_ple_norm_g, v_w_ple_gate, v_w_ple_proj, v_final_norm_g):
    given = dict(locals())
    xs, ps, target = x[0], p[0, 0], loss_target[0]
    s = xs.shape[0]

    def slab_of(name):
        shard = given[name][0].astype(MXU_DTYPE)
        return shard.T if SLAB_IS_TRANSPOSED[name] else shard

    w_in_t = _all_gather_now(slab_of('w_in'))
    later = ['w_out', 'w_gate', 'w_up', 'w_down', 'w_ple_gate', 'w_ple_proj']
    gather, token = _all_gather_start(_after(w_in_t, [slab_of(n) for n in later]))

    cos_t, sin_t = _rope_tables(s)
    tril = jnp.tril(jnp.ones((CHUNK, CHUNK), F32))
    w_tril = (sgu_w.reshape(HEADS_A, CHUNK, CHUNK) * tril).astype(MXU_DTYPE)
    w_tril_t = jnp.swapaxes(w_tril, 1, 2)
    bias = jnp.repeat(sgu_b.reshape(HEADS_A, CHUNK).T, HEAD_DIM, axis=1)
    g = {n: given[n].reshape(1, -1) for n in SMALL if n not in ('sgu_w', 'sgu_b')}

    uv, q, k, v, hn1 = _in_fwd(xs, _after(token, g['mix_norm_g']), w_in_t.T, cos_t, sin_t)
    ya_n = _sgu_fwd(uv, w_tril, bias, g['sgu_norm_g'], g['out_norm_a'])
    state = None
    for i, dil in enumerate(DILATIONS):
        state = _attn_fwd_branch(q, k, v, state, dil, last=(i == len(DILATIONS) - 1))
    y_b, lse = state
    stacks = dict(zip(later, _all_gather_wait(gather, lse)))
    w_gate_t, w_up_t, w_pp_t = stacks['w_gate'], stacks['w_up'], stacks['w_ple_proj']
    h1, y_n = _out_fwd(ya_n, y_b, g['out_norm_b'], stacks['w_out'], xs)
    h2, gate, up, hn2 = _ffn_fwd(h1, g['ffn_norm_g'], w_gate_t.T, w_up_t.T, stacks['w_down'])
    loss, dh2, dz, dpp, hn3, d_ple_g, d_final_g = _ple_loss(
        h2, ps, target, g['ple_norm_g'], stacks['w_ple_gate'], stacks['w_ple_gate'].T, w_pp_t.T, g['final_norm_g'])

    share = {}
    share['w_ple_gate'], = _wgrad(hn3, [dz], "wgrad_ple_gate")
    share['w_ple_proj'], = _wgrad(ps, [dpp], "wgrad_ple_proj", transposed=True)
    scatter_1, token = _reduce_scatter_start([share['w_ple_gate'], share['w_ple_proj']], "reduce_scatter_start_1")
    dh1, act, dgate, dup, d_ffn_g = _ffn_bwd(dh2, h1, gate, up, _after(token, g['ffn_norm_g']), stacks['w_down'].T,
                                             w_gate_t, w_up_t)
    share['w_down'], = _wgrad(act, [dh2], "wgrad_down")
    share['w_gate'], share['w_up'] = _wgrad(hn2, [dgate, dup], "wgrad_gate_up", transposed=True)
    scatter_2, token = _reduce_scatter_start([share['w_down'], share['w_gate'], share['w_up']],
                                             "reduce_scatter_start_2")
    dya_n, dyb, d_out_b = _out_bwd(dh1, y_b, _after(token, g['out_norm_b']), stacks['w_out'].T)
    share['w_out'], = _wgrad(y_n, [dh1], "wgrad_out")
    scatter_3, token = _reduce_scatter_start([share['w_out']], "reduce_scatter_start_3")
    grads = None
    for dil in DILATIONS:
        grads = _attn_bwd_branch(q, k, v, dyb, y_b, lse, grads, dil)
    duv, d_sgu_w, d_sgu_b, d_sgu_g, d_out_a = _sgu_bwd(uv, dya_n, w_tril, w_tril_t, bias,
                                                       _after(token, g['sgu_norm_g']), g['out_norm_a'])
    dproj = _in_bwd_proj(duv, grads[0], grads[1], grads[2], cos_t, sin_t)
    share['w_in'], = _wgrad(hn1, [dproj], "wgrad_in", transposed=True)
    scatter_4, token = _reduce_scatter_start([share['w_in']], "reduce_scatter_start_4")
    grad_x, d_mix_g = _in_bwd_x(dproj, w_in_t, xs, _after(token, g['mix_norm_g']), dh1)

    gs = {'mix_norm_g': d_mix_g, 'sgu_w': d_sgu_w, 'sgu_b': d_sgu_b[:, :HEADS_A].T, 'sgu_norm_g': d_sgu_g,
          'out_norm_a': d_out_a, 'out_norm_b': d_out_b, 'ffn_norm_g': d_ffn_g, 'ple_norm_g': d_ple_g,
          'final_norm_g': d_final_g}
    gs_block = _pack_small(gs).at[SMALL_ROWS - 1, 0].set(loss[0, 0])
    small_sum = _small_all_reduce(gs_block)
    loss_out = small_sum[SMALL_ROWS - 1, 0]
    small_shapes = {n: given[n].shape for n in SMALL}

    landed = {}
    for names, handle, tag in ((['w_ple_gate', 'w_ple_proj'], scatter_1, "1"), (['w_down', 'w_gate', 'w_up'], scatter_2, "2"),
                               (['w_out'], scatter_3, "3"), (['w_in'], scatter_4, "4")):
        landed.update(zip(names, _reduce_scatter_wait(handle, small_sum, "reduce_scatter_wait_" + tag)))
    slabs = dict(zip(SHARDED, _swap_halves([_sum_shares(landed[n], "sum_shares_" + n) for n in SHARDED])))

    grads, deltas, new_m, new_v = {}, {}, {}, {}
    for n in SHARDED:
        g_shard = slabs[n].T if SLAB_IS_TRANSPOSED[n] else slabs[n]
        d, nm, nv = _adamw(given[n][0], g_shard, given["m_" + n][0], given["v_" + n][0], "adamw_" + n)
        grads[n], deltas[n], new_m[n], new_v[n] = g_shard[None], d[None], nm[None], nv[None]

    small = {n: given[n] for n in SMALL}
    d, nm, nv = _adamw(_pack_small(small), small_sum, _pack_small({n: given["m_" + n] for n in SMALL}),
                       _pack_small({n: given["v_" + n] for n in SMALL}), "adamw_small")
    for res, blk in ((grads, small_sum), (deltas, d), (new_m, nm), (new_v, nv)):
        res.update(_unpack_small(blk, small_shapes))

    outs = [loss_out, grad_x[None]]
    for res in (grads, deltas, new_m, new_v):
        outs += [res[n] for n in WEIGHT_NAMES]
    return tuple(outs)
```

```python
import math

import jax
import jax.numpy as jnp
import numpy as np
from jax import lax
from jax.experimental import pallas as pl
from jax.experimental.pallas import tpu as pltpu

F32 = jnp.float32
MXU_DTYPE = jnp.bfloat16

D_MODEL = 1024
HEAD_DIM = 64
HEADS_A = 4
HEADS_B = 12
WIDTH_A = HEADS_A * HEAD_DIM
WIDTH_B = HEADS_B * HEAD_DIM
CHUNK = 128
BLOCK = 128
DILATIONS = (1, 4, 16)
ROPE_THETA = 10000.0
D_FF = 2816
FF_HALF = D_FF // 2
PLE_DIM = 256
IN_COLS = 2 * WIDTH_A + 3 * WIDTH_B
EPS = 1e-6
LANES = 128
N_CHIPS = 4
N_DEV = 8

ADAM_LR = 0.001
ADAM_B1 = 0.9
ADAM_B2 = 0.999
ADAM_EPS = 1e-08
ADAM_WD = 0.01
ADAM_STEP = 10

VMEM_LIMIT = 56 * 1024 * 1024

WEIGHT_NAMES = ['mix_norm_g', 'w_in', 'sgu_w', 'sgu_b', 'sgu_norm_g', 'out_norm_a', 'out_norm_b', 'w_out',
                'ffn_norm_g', 'w_gate', 'w_up', 'w_down', 'ple_norm_g', 'w_ple_gate', 'w_ple_proj', 'final_norm_g']
SHARDED = ['w_in', 'w_out', 'w_gate', 'w_up', 'w_down', 'w_ple_gate', 'w_ple_proj']
SMALL = ['mix_norm_g', 'sgu_w', 'sgu_b', 'sgu_norm_g', 'out_norm_a', 'out_norm_b', 'ffn_norm_g', 'ple_norm_g',
         'final_norm_g']
SMALL_SIZES = {'mix_norm_g': 1024, 'sgu_w': 65536, 'sgu_b': 512, 'sgu_norm_g': 256, 'out_norm_a': 256,
               'out_norm_b': 768, 'ffn_norm_g': 1024, 'ple_norm_g': 1024, 'final_norm_g': 1024}
SMALL_ROWS = 72


def _params(semantics=None):
    return pltpu.CompilerParams(dimension_semantics=semantics, vmem_limit_bytes=VMEM_LIMIT)


def _full(shape):
    nd = len(shape)
    return pl.BlockSpec(shape, lambda i: (0,) * nd)


def _rows(tm, width):
    return pl.BlockSpec((tm, width), lambda i: (i, 0))


def _rms_stats(x):
    r = lax.rsqrt(jnp.mean(x * x, axis=-1, keepdims=True) + EPS)
    return x * r, r


def _rms_bwd(dn, n, r):
    return r * (dn - n * jnp.mean(dn * n, axis=-1, keepdims=True))


def _dot(a, b):
    return jnp.dot(a, b, preferred_element_type=F32)


def _dot_nt(a, b):
    return lax.dot_general(a, b, (((1,), (1,)), ((), ())), preferred_element_type=F32)


def _dot_tn(a, b):
    return lax.dot_general(a, b, (((0,), (0,)), ((), ())), preferred_element_type=F32)


def _gelu_parts(x):
    c = math.sqrt(2.0 / math.pi)
    t = jnp.tanh(c * (x + 0.044715 * x * x * x))
    return 0.5 * x * (1.0 + t), t


def _gelu_grad(x, t):
    c = math.sqrt(2.0 / math.pi)
    return 0.5 * (1.0 + t) + 0.5 * x * (1.0 - t * t) * c * (1.0 + 3.0 * 0.044715 * x * x)


def _half_masks(dtype):
    lane = lax.broadcasted_iota(jnp.int32, (BLOCK, LANES), 1)
    lo = (lane < HEAD_DIM).astype(F32)
    return lo.astype(dtype), (1.0 - lo).astype(dtype)


def _rope_partner(t):
    lane = lax.broadcasted_iota(jnp.int32, t.shape, 1)
    first_half = (lane % HEAD_DIM) < (HEAD_DIM // 2)
    return jnp.where(first_half, pltpu.roll(t, LANES - HEAD_DIM // 2, 1), pltpu.roll(t, HEAD_DIM // 2, 1))


L_BLOCK = 256
L_GROUP = 16


def _store_l256(scr, out_ref, cols, value):
    tm = value.shape[0]
    scr[...] = value
    for blk in range(tm // L_BLOCK):
        for r in range(L_GROUP):
            lo = blk * L_BLOCK + r * L_GROUP
            piece = scr[pl.ds(blk * L_BLOCK + r, L_GROUP, stride=L_GROUP), :]
            out_ref[lo:lo + L_GROUP, cols] = piece.astype(out_ref.dtype)


def _load_l256(col_refs, tm):
    cols = []
    for ref in col_refs:
        pieces = [ref[pl.ds(blk * L_BLOCK + i, L_GROUP, stride=L_GROUP), :]
                  for blk in range(tm // L_BLOCK) for i in range(L_GROUP)]
        cols.append(jnp.concatenate(pieces, axis=0))
    return jnp.concatenate(cols, axis=1)


def _col_specs(tm, width):
    return [pl.BlockSpec((tm, LANES), lambda i, j=j: (i, j)) for j in range(width // LANES)]


def _in_fwd(x, g_mix, w_in, cos_t, sin_t):
    s = x.shape[0]
    tm = 512

    def body(x_ref, g_ref, w_ref, cos_ref, sin_ref, uv_ref, q_ref, k_ref, v_ref, hn_ref, scr):
        n, _ = _rms_stats(x_ref[...])
        hn = (n * g_ref[...]).astype(MXU_DTYPE)
        hn_ref[...] = hn
        proj = _dot(hn, w_ref[...])
        uv_ref[...] = proj[:, :2 * WIDTH_A]
        cos = cos_ref[...]
        sin = sin_ref[...]
        for i in range(WIDTH_B // LANES):
            lo = 2 * WIDTH_A + i * LANES
            tq = proj[:, lo:lo + LANES]
            tk = proj[:, lo + WIDTH_B:lo + WIDTH_B + LANES]
            tv = proj[:, lo + 2 * WIDTH_B:lo + 2 * WIDTH_B + LANES]
            cols = slice(i * LANES, (i + 1) * LANES)
            _store_l256(scr, q_ref, cols, (tq * cos + _rope_partner(tq) * sin) * (HEAD_DIM ** -0.5))
            _store_l256(scr, k_ref, cols, tk * cos + _rope_partner(tk) * sin)
            _store_l256(scr, v_ref, cols, tv)

    return pl.pallas_call(
        body, name="in_fwd", grid=(s // tm,), scratch_shapes=[pltpu.VMEM((tm, LANES), F32)],
        in_specs=[_rows(tm, D_MODEL), _full((1, D_MODEL)), _full((D_MODEL, IN_COLS)), _rows(tm, LANES),
                  _rows(tm, LANES)],
        out_specs=[_rows(tm, 2 * WIDTH_A), _rows(tm, WIDTH_B), _rows(tm, WIDTH_B), _rows(tm, WIDTH_B),
                   _rows(tm, D_MODEL)],
        out_shape=[jax.ShapeDtypeStruct((s, 2 * WIDTH_A), F32), jax.ShapeDtypeStruct((s, WIDTH_B), MXU_DTYPE),
                   jax.ShapeDtypeStruct((s, WIDTH_B), MXU_DTYPE), jax.ShapeDtypeStruct((s, WIDTH_B), MXU_DTYPE),
                   jax.ShapeDtypeStruct((s, D_MODEL), MXU_DTYPE)],
        compiler_params=_params(("arbitrary",)),
    )(x, g_mix, w_in, cos_t, sin_t)


class _Branch:
    def __init__(self, dil, s):
        self.dil = dil
        i = np.arange(L_GROUP)
        if dil == 16:
            self.grid = (16, s // 2048)
            self.shape = (8, 1, 1, L_GROUP)
            self.index = lambda r, n: (n, r // 4, r % 4, 0, 0)
            pos = (np.arange(8)[:, None] * 16 + i[None, :]).reshape(-1)
        elif dil == 4:
            self.grid = (4, s // 512)
            self.shape = (2, 4, 1, L_GROUP)
            self.index = lambda r, n: (n, 0, r, 0, 0)
            pos = (np.arange(2)[:, None, None] * 64 + np.arange(4)[None, :, None] + 4 * i[None, None, :]).reshape(-1)
        else:
            self.grid = (1, s // L_BLOCK)
            self.shape = (1, 4, 4, L_GROUP)
            self.index = lambda r, n: (n, 0, 0, 0, 0)
            pos = (np.arange(16)[:, None] + 16 * i[None, :]).reshape(-1)
        self.qn = pos.shape[0]
        self.nb = self.grid[1]
        dist = pos[:, None] - np.concatenate([pos - self.qn, pos])[None, :]
        band = (dist >= 0) & (dist <= BLOCK)
        start = band & (np.arange(2 * self.qn)[None, :] >= self.qn)
        self.bias = np.where(np.stack([band, start]), 0.0, -np.inf).astype(np.float32)

    def view(self, a):
        return a.reshape(a.shape[0] // L_BLOCK, 4, 4, L_GROUP, a.shape[1])

    def spec(self, w, step=lambda n: n):
        return pl.BlockSpec(self.shape + (w,), lambda r, n: self.index(r, step(n)))

    def bias_spec(self, step=lambda n: n):
        return pl.BlockSpec((1, self.qn, 2 * self.qn), lambda r, n: (jnp.where(step(n) == 0, 1, 0), 0, 0))

    def load(self, ref, cols=slice(None)):
        x = ref[:, :, :, :, cols]
        return x.reshape(self.qn, x.shape[-1])

    def store(self, ref, cols, value):
        ref[:, :, :, :, cols] = value.reshape(self.shape + (value.shape[-1],))


def _attn_fwd_branch(q, k, v, state, dil, last):
    s = q.shape[0]
    br = _Branch(dil, s)
    qn = br.qn
    first = state is None

    def body(*refs):
        bias_ref, q_ref, kc_ref, kp_ref, vc_ref, vp_ref = refs[:6]
        if first:
            outs = refs[6:]
        else:
            acc_ref, m_ref, l_ref = refs[6:9]
            outs = refs[9:]
            m_in = br.load(m_ref)
            l_in = br.load(l_ref)
        o_ref, m_out = outs[0], outs[1]
        bias2 = jnp.concatenate([bias_ref[0], bias_ref[0]], axis=0)
        lane = lax.broadcasted_iota(jnp.int32, (qn, LANES), 1)
        lo = lane < HEAD_DIM
        mask_lo = lo.astype(F32).astype(MXU_DTYPE)
        masks = (mask_lo, 1 - mask_lo)
        m_blk = jnp.zeros((qn, LANES), F32)
        l_blk = jnp.zeros((qn, LANES), F32)
        for hp in range(HEADS_B // 2):
            cols = slice(hp * LANES, (hp + 1) * LANES)
            qp = br.load(q_ref, cols)
            kcat = jnp.concatenate([br.load(kp_ref, cols), br.load(kc_ref, cols)], axis=0)
            vcat = jnp.concatenate([br.load(vp_ref, cols), br.load(vc_ref, cols)], axis=0)
            h0, h1 = 2 * hp, 2 * hp + 1
            sc = _dot_nt(jnp.concatenate([qp * masks[0], qp * masks[1]], axis=0), kcat) + bias2
            m_new = jnp.max(sc, axis=1, keepdims=True)
            if not first:
                m_prev = jnp.concatenate([m_in[:, h0:h0 + 1], m_in[:, h1:h1 + 1]], axis=0)
                m_new = jnp.maximum(m_prev, m_new)
            p = jnp.exp(sc - m_new)
            l_new = jnp.sum(p, axis=1, keepdims=True)
            acc = _dot(p.astype(MXU_DTYPE), vcat)
            if not first:
                alpha = jnp.exp(m_prev - m_new)
                l_new = alpha * jnp.concatenate([l_in[:, h0:h0 + 1], l_in[:, h1:h1 + 1]], axis=0) + l_new
                acc_in = br.load(acc_ref, cols)
                acc = alpha * jnp.concatenate([acc_in, acc_in], axis=0) + acc
            if last:
                acc = acc / l_new
                m_new = m_new + jnp.log(l_new)
            m_blk = jnp.where(lane == h0, m_new[:qn], jnp.where(lane == h1, m_new[qn:], m_blk))
            l_blk = jnp.where(lane == h0, l_new[:qn], jnp.where(lane == h1, l_new[qn:], l_blk))
            br.store(o_ref, cols, jnp.where(lo, acc[:qn], acc[qn:]))
        br.store(m_out, slice(None), m_blk)
        if not last:
            br.store(outs[2], slice(None), l_blk)

    before = lambda n: jnp.maximum(n - 1, 0)
    in_specs = [br.bias_spec(), br.spec(WIDTH_B), br.spec(WIDTH_B), br.spec(WIDTH_B, before), br.spec(WIDTH_B),
                br.spec(WIDTH_B, before)]
    args = [jnp.asarray(br.bias), br.view(q), br.view(k), br.view(k), br.view(v), br.view(v)]
    if not first:
        in_specs += [br.spec(WIDTH_B), br.spec(LANES), br.spec(LANES)]
        args += [br.view(a) for a in state]
    widths = (WIDTH_B, LANES) if last else (WIDTH_B, LANES, LANES)
    res = pl.pallas_call(
        body, name="attn_fwd_d%d" % dil, grid=br.grid, in_specs=in_specs,
        out_specs=[br.spec(w) for w in widths],
        out_shape=[jax.ShapeDtypeStruct((s // L_BLOCK, 4, 4, L_GROUP, w), F32) for w in widths],
        compiler_params=_params(("arbitrary", "arbitrary")),
    )(*args)
    return tuple(a.reshape(s, w) for a, w in zip(res, widths))


def _sgu_forward_tile(uv, w_ref, bias, g_sgu):
    tm = uv.shape[0]
    u = uv[:, :WIDTH_A]
    v = uv[:, WIDTH_A:]
    ug, tu = _gelu_parts(u)
    vg, tv = _gelu_parts(v)
    mu = jnp.mean(vg, axis=-1, keepdims=True)
    vc = vg - mu
    rs = lax.rsqrt(jnp.mean(vc * vc, axis=-1, keepdims=True) + EPS)
    vhat = vc * rs
    vn = (vhat * g_sgu).astype(MXU_DTYPE)
    masks = _half_masks(MXU_DTYPE)
    chunks = []
    for c in range(tm // CHUNK):
        rows = slice(c * CHUNK, (c + 1) * CHUNK)
        groups = []
        for gp in range(2):
            vn_g = vn[rows, gp * LANES:(gp + 1) * LANES]
            groups.append(_dot(w_ref[2 * gp], vn_g * masks[0]) + _dot(w_ref[2 * gp + 1], vn_g * masks[1]))
        chunks.append(jnp.concatenate(groups, axis=1) + bias)
    mixed = jnp.concatenate(chunks, axis=0)
    return dict(u=u, v=v, ug=ug, tu=tu, tv=tv, rs=rs, vhat=vhat, vn=vn, mixed=mixed, ya=ug * mixed)


def _sgu_fwd(uv, w_tril, bias, g_sgu, g_a):
    s = uv.shape[0]
    tm = 512

    def body(uv_ref, w_ref, b_ref, gs_ref, ga_ref, o_ref):
        t = _sgu_forward_tile(uv_ref[...], w_ref, b_ref[...], gs_ref[...])
        n, _ = _rms_stats(t['ya'])
        o_ref[...] = (n * ga_ref[...]).astype(MXU_DTYPE)

    return pl.pallas_call(
        body, name="sgu_fwd", grid=(s // tm,),
        in_specs=[_rows(tm, 2 * WIDTH_A), _full((HEADS_A, CHUNK, CHUNK)), _full((CHUNK, WIDTH_A)),
                  _full((1, WIDTH_A)), _full((1, WIDTH_A))],
        out_specs=_rows(tm, WIDTH_A), out_shape=jax.ShapeDtypeStruct((s, WIDTH_A), MXU_DTYPE),
        compiler_params=_params(("arbitrary",)),
    )(uv, w_tril, bias, g_sgu, g_a)


def _out_fwd(ya_n, y_b, g_b, w_out, x):
    s = x.shape[0]
    tm = 512
    nc = WIDTH_B // LANES

    def body(ya_ref, *refs):
        yb_refs = refs[:nc]
        g_ref, w_ref, x_ref, h_ref, yn_ref = refs[nc:]
        n, _ = _rms_stats(_load_l256(yb_refs, tm))
        yn = jnp.concatenate([ya_ref[...], (n * g_ref[...]).astype(MXU_DTYPE)], axis=1)
        yn_ref[...] = yn
        h_ref[...] = x_ref[...] + _dot(yn, w_ref[...])

    return pl.pallas_call(
        body, name="out_fwd", grid=(s // tm,),
        in_specs=[_rows(tm, WIDTH_A)] + _col_specs(tm, WIDTH_B) + [_full((1, WIDTH_B)), _full((D_MODEL, D_MODEL)),
                                                                 _rows(tm, D_MODEL)],
        out_specs=[_rows(tm, D_MODEL), _rows(tm, D_MODEL)],
        out_shape=[jax.ShapeDtypeStruct((s, D_MODEL), F32), jax.ShapeDtypeStruct((s, D_MODEL), MXU_DTYPE)],
        compiler_params=_params(("arbitrary",)),
    )(ya_n, *([y_b] * nc), g_b, w_out, x)


def _ffn_fwd(h1, g_ffn, w_gate, w_up, w_down):
    s = h1.shape[0]
    tm = 256

    def body(h_ref, g_ref, wg_ref, wu_ref, wd_ref, o_ref, gate_ref, up_ref, hn_ref):
        h = h_ref[...]
        n, _ = _rms_stats(h)
        hn = (n * g_ref[...]).astype(MXU_DTYPE)
        hn_ref[...] = hn
        out = h
        for c in range(2):
            cols = slice(c * FF_HALF, (c + 1) * FF_HALF)
            gate = _dot(hn, wg_ref[:, cols])
            up = _dot(hn, wu_ref[:, cols])
            gate_ref[:, cols] = gate.astype(MXU_DTYPE)
            up_ref[:, cols] = up.astype(MXU_DTYPE)
            act = (gate * jax.nn.sigmoid(gate) * up).astype(MXU_DTYPE)
            out = out + _dot(act, wd_ref[cols, :])
        o_ref[...] = out

    return pl.pallas_call(
        body, name="ffn_fwd", grid=(s // tm,),
        in_specs=[_rows(tm, D_MODEL), _full((1, D_MODEL)), _full((D_MODEL, D_FF)), _full((D_MODEL, D_FF)),
                  _full((D_FF, D_MODEL))],
        out_specs=[_rows(tm, D_MODEL), _rows(tm, D_FF), _rows(tm, D_FF), _rows(tm, D_MODEL)],
        out_shape=[jax.ShapeDtypeStruct((s, D_MODEL), F32), jax.ShapeDtypeStruct((s, D_FF), MXU_DTYPE),
                   jax.ShapeDtypeStruct((s, D_FF), MXU_DTYPE), jax.ShapeDtypeStruct((s, D_MODEL), MXU_DTYPE)],
        compiler_params=_params(("arbitrary",)),
    )(h1, g_ffn, w_gate, w_up, w_down)


def _ple_loss(h2, p, target, g_ple, w_pg, w_pg_t, w_pp, g_final):
    s = h2.shape[0]
    tm = 256

    def body(h_ref, p_ref, t_ref, gp_ref, wg_ref, wgt_ref, wp_ref, gf_ref,
             loss_ref, dh_ref, dz_ref, dpp_ref, hn_ref, dgp_ref, dgf_ref):
        @pl.when(pl.program_id(0) == 0)
        def _():
            loss_ref[...] = jnp.zeros_like(loss_ref)
            dgp_ref[...] = jnp.zeros_like(dgp_ref)
            dgf_ref[...] = jnp.zeros_like(dgf_ref)

        h2t = h_ref[...]
        n2, r2 = _rms_stats(h2t)
        hn = (n2 * gp_ref[...]).astype(MXU_DTYPE)
        hn_ref[...] = hn
        gate = jax.nn.sigmoid(_dot(hn, wg_ref[...]))
        pp = _dot(p_ref[...].astype(MXU_DTYPE), wp_ref[...])
        h3 = h2t + gate * pp
        n3, r3 = _rms_stats(h3)
        diff = n3 * gf_ref[...] - t_ref[...]
        loss_ref[...] += jnp.full(loss_ref.shape, 0.5 * jnp.sum(diff * diff) / D_MODEL, F32)
        dy = diff * (1.0 / D_MODEL)
        dgf_ref[...] += jnp.sum(dy * n3, axis=0, keepdims=True)
        dh3 = _rms_bwd(dy * gf_ref[...], n3, r3)
        dpp_ref[...] = (dh3 * gate).astype(MXU_DTYPE)
        dz = (dh3 * pp * gate * (1.0 - gate)).astype(MXU_DTYPE)
        dz_ref[...] = dz
        dhn = _dot(dz, wgt_ref[...])
        dgp_ref[...] += jnp.sum(dhn * n2, axis=0, keepdims=True)
        dh_ref[...] = dh3 + _rms_bwd(dhn * gp_ref[...], n2, r2)

    return pl.pallas_call(
        body, name="ple_loss", grid=(s // tm,),
        in_specs=[_rows(tm, D_MODEL), _rows(tm, PLE_DIM), _rows(tm, D_MODEL), _full((1, D_MODEL)),
                  _full((D_MODEL, D_MODEL)), _full((D_MODEL, D_MODEL)), _full((PLE_DIM, D_MODEL)),
                  _full((1, D_MODEL))],
        out_specs=[_full((1, LANES)), _rows(tm, D_MODEL), _rows(tm, D_MODEL), _rows(tm, D_MODEL),
                   _rows(tm, D_MODEL), _full((1, D_MODEL)), _full((1, D_MODEL))],
        out_shape=[jax.ShapeDtypeStruct((1, LANES), F32), jax.ShapeDtypeStruct((s, D_MODEL), F32),
                   jax.ShapeDtypeStruct((s, D_MODEL), MXU_DTYPE), jax.ShapeDtypeStruct((s, D_MODEL), MXU_DTYPE),
                   jax.ShapeDtypeStruct((s, D_MODEL), MXU_DTYPE), jax.ShapeDtypeStruct((1, D_MODEL), F32),
                   jax.ShapeDtypeStruct((1, D_MODEL), F32)],
        compiler_params=_params(("arbitrary",)),
    )(h2, p, target, g_ple, w_pg, w_pg_t, w_pp, g_final)


def _ffn_bwd(dh2, h1, gate, up, g_ffn, w_down_t, w_gate_t, w_up_t):
    s = h1.shape[0]
    tm = 256

    def body(dh_ref, h_ref, gate_ref, up_ref, g_ref, wdt_ref, wgt_ref, wut_ref,
             o_ref, act_ref, dg_ref, du_ref, dgn_ref):
        @pl.when(pl.program_id(0) == 0)
        def _():
            dgn_ref[...] = jnp.zeros_like(dgn_ref)

        dh = dh_ref[...]
        dhb = dh.astype(MXU_DTYPE)
        dhn = jnp.zeros((tm, D_MODEL), F32)
        for c in range(2):
            cols = slice(c * FF_HALF, (c + 1) * FF_HALF)
            dact = _dot(dhb, wdt_ref[:, cols])
            g = gate_ref[:, cols].astype(F32)
            u = up_ref[:, cols].astype(F32)
            sg = jax.nn.sigmoid(g)
            silu = g * sg
            act_ref[:, cols] = (silu * u).astype(MXU_DTYPE)
            du = (dact * silu).astype(MXU_DTYPE)
            dg = (dact * u * sg * (1.0 + g * (1.0 - sg))).astype(MXU_DTYPE)
            du_ref[:, cols] = du
            dg_ref[:, cols] = dg
            dhn = dhn + _dot(dg, wgt_ref[cols, :]) + _dot(du, wut_ref[cols, :])
        n, r = _rms_stats(h_ref[...])
        dgn_ref[...] += jnp.sum(dhn * n, axis=0, keepdims=True)
        o_ref[...] = dh + _rms_bwd(dhn * g_ref[...], n, r)

    return pl.pallas_call(
        body, name="ffn_bwd", grid=(s // tm,),
        in_specs=[_rows(tm, D_MODEL), _rows(tm, D_MODEL), _rows(tm, D_FF), _rows(tm, D_FF), _full((1, D_MODEL)),
                  _full((D_MODEL, D_FF)), _full((D_FF, D_MODEL)), _full((D_FF, D_MODEL))],
        out_specs=[_rows(tm, D_MODEL), _rows(tm, D_FF), _rows(tm, D_FF), _rows(tm, D_FF), _full((1, D_MODEL))],
        out_shape=[jax.ShapeDtypeStruct((s, D_MODEL), F32), jax.ShapeDtypeStruct((s, D_FF), MXU_DTYPE),
                   jax.ShapeDtypeStruct((s, D_FF), MXU_DTYPE), jax.ShapeDtypeStruct((s, D_FF), MXU_DTYPE),
                   jax.ShapeDtypeStruct((1, D_MODEL), F32)],
        compiler_params=_params(("arbitrary",)),
    )(dh2, h1, gate, up, g_ffn, w_down_t, w_gate_t, w_up_t)


def _out_bwd(dh1, y_b, g_b, w_out_t):
    s = dh1.shape[0]
    tm = 512
    nc = WIDTH_B // LANES

    def body(dh_ref, *refs):
        yb_refs = refs[:nc]
        g_ref, wt_ref, dya_ref, dyb_ref, dg_ref, scr = refs[nc:]

        @pl.when(pl.program_id(0) == 0)
        def _():
            dg_ref[...] = jnp.zeros_like(dg_ref)

        dy = _dot(dh_ref[...].astype(MXU_DTYPE), wt_ref[...])
        dya_ref[...] = dy[:, :WIDTH_A]
        dyb = dy[:, WIDTH_A:]
        n, r = _rms_stats(_load_l256(yb_refs, tm))
        dg_ref[...] += jnp.sum(dyb * n, axis=0, keepdims=True)
        dyb_in = _rms_bwd(dyb * g_ref[...], n, r)
        for j in range(nc):
            cols = slice(j * LANES, (j + 1) * LANES)
            _store_l256(scr, dyb_ref, cols, dyb_in[:, cols])

    return pl.pallas_call(
        body, name="out_bwd", grid=(s // tm,), scratch_shapes=[pltpu.VMEM((tm, LANES), F32)],
        in_specs=[_rows(tm, D_MODEL)] + _col_specs(tm, WIDTH_B) + [_full((1, WIDTH_B)), _full((D_MODEL, D_MODEL))],
        out_specs=[_rows(tm, WIDTH_A), _rows(tm, WIDTH_B), _full((1, WIDTH_B))],
        out_shape=[jax.ShapeDtypeStruct((s, WIDTH_A), F32), jax.ShapeDtypeStruct((s, WIDTH_B), F32),
                   jax.ShapeDtypeStruct((1, WIDTH_B), F32)],
        compiler_params=_params(("arbitrary",)),
    )(dh1, *([y_b] * nc), g_b, w_out_t)


def _attn_bwd_branch(q, k, v, do, o, lse, grads, dil):
    s = q.shape[0]
    br = _Branch(dil, s)
    qn, nb = br.qn, br.nb
    first = grads is None

    def body(*refs):
        bias_ref, q_ref, kc_ref, kp_ref, vc_ref, vp_ref, do_ref, o_ref, lse_ref = refs[:9]
        if first:
            rest = refs[9:]
        else:
            dq_in, dk_in, dv_in = refs[9:12]
            rest = refs[12:]
        dq_ref, dk_ref, dv_ref, dk_carry, dv_carry = rest
        n = pl.program_id(1)

        @pl.when(n == 0)
        def _():
            dk_carry[...] = jnp.zeros_like(dk_carry)
            dv_carry[...] = jnp.zeros_like(dv_carry)

        @pl.when(n < nb)
        def _():
            bias2 = jnp.concatenate([bias_ref[0], bias_ref[0]], axis=0)
            lane = lax.broadcasted_iota(jnp.int32, (qn, LANES), 1)
            lo = lane < HEAD_DIM
            mask_f = lo.astype(F32)
            mask_lo = mask_f.astype(MXU_DTYPE)
            lse = br.load(lse_ref)
            for hp in range(HEADS_B // 2):
                cols = slice(hp * LANES, (hp + 1) * LANES)
                h0, h1 = 2 * hp, 2 * hp + 1
                qp = br.load(q_ref, cols)
                kcat = jnp.concatenate([br.load(kp_ref, cols), br.load(kc_ref, cols)], axis=0)
                vcat = jnp.concatenate([br.load(vp_ref, cols), br.load(vc_ref, cols)], axis=0)
                dop = br.load(do_ref, cols)
                prod = dop * br.load(o_ref, cols)
                prod_lo = prod * mask_f
                delta = jnp.concatenate([jnp.sum(prod_lo, axis=1, keepdims=True),
                                         jnp.sum(prod - prod_lo, axis=1, keepdims=True)], axis=0)
                qs = jnp.concatenate([qp * mask_lo, qp * (1 - mask_lo)], axis=0)
                dos = jnp.concatenate([dop * mask_f, dop * (1.0 - mask_f)], axis=0).astype(MXU_DTYPE)
                lse2 = jnp.concatenate([lse[:, h0:h0 + 1], lse[:, h1:h1 + 1]], axis=0)
                p = jnp.exp(_dot_nt(qs, kcat) + bias2 - lse2)
                ds = (p * (_dot_nt(dos, vcat) - delta)).astype(MXU_DTYPE)
                dvc = _dot_tn(p.astype(MXU_DTYPE), dos)
                dkc = _dot_tn(ds, qs)
                dq2 = _dot(ds, kcat)
                dq = jnp.where(lo, dq2[:qn], dq2[qn:])
                dk_prev = dk_carry[:, cols] + dkc[:qn]
                dv_prev = dv_carry[:, cols] + dvc[:qn]
                if not first:
                    dq = dq + br.load(dq_in, cols)
                    dk_prev = dk_prev + br.load(dk_in, cols)
                    dv_prev = dv_prev + br.load(dv_in, cols)
                br.store(dq_ref, cols, dq)
                br.store(dk_ref, cols, dk_prev)
                br.store(dv_ref, cols, dv_prev)
                dk_carry[:, cols] = dkc[qn:]
                dv_carry[:, cols] = dvc[qn:]

        @pl.when(n == nb)
        def _():
            dk_last = dk_carry[...]
            dv_last = dv_carry[...]
            if not first:
                dk_last = dk_last + br.load(dk_in)
                dv_last = dv_last + br.load(dv_in)
            br.store(dk_ref, slice(None), dk_last)
            br.store(dv_ref, slice(None), dv_last)

    cur = lambda n: jnp.minimum(n, nb - 1)
    before = lambda n: jnp.maximum(cur(n) - 1, 0)
    late = lambda n: jnp.maximum(n - 1, 0)
    in_specs = [br.bias_spec(cur), br.spec(WIDTH_B, cur), br.spec(WIDTH_B, cur), br.spec(WIDTH_B, before),
                br.spec(WIDTH_B, cur), br.spec(WIDTH_B, before), br.spec(WIDTH_B, cur), br.spec(WIDTH_B, cur),
                br.spec(LANES, cur)]
    args = [jnp.asarray(br.bias)] + [br.view(a) for a in (q, k, k, v, v, do, o, lse)]
    if not first:
        in_specs += [br.spec(WIDTH_B, cur), br.spec(WIDTH_B, late), br.spec(WIDTH_B, late)]
        args += [br.view(g) for g in grads]
    res = pl.pallas_call(
        body, name="attn_bwd_d%d" % dil, grid=(br.grid[0], nb + 1), in_specs=in_specs,
        out_specs=[br.spec(WIDTH_B, cur), br.spec(WIDTH_B, late), br.spec(WIDTH_B, late)],
        out_shape=[jax.ShapeDtypeStruct((s // L_BLOCK, 4, 4, L_GROUP, WIDTH_B), F32)] * 3,
        scratch_shapes=[pltpu.VMEM((qn, WIDTH_B), F32), pltpu.VMEM((qn, WIDTH_B), F32)],
        compiler_params=_params(("arbitrary", "arbitrary")),
    )(*args)
    return tuple(a.reshape(s, WIDTH_B) for a in res)


def _sgu_bwd(uv, dya_n, w_tril, w_tril_t, bias, g_sgu, g_a):
    s = uv.shape[0]
    tm = 512

    def body(uv_ref, dy_ref, w_ref, wt_ref, b_ref, gs_ref, ga_ref, duv_ref, dw_ref, db_ref, dgs_ref, dga_ref,
             db_acc):
        i = pl.program_id(0)

        @pl.when(i == 0)
        def _():
            dw_ref[...] = jnp.zeros_like(dw_ref)
            dgs_ref[...] = jnp.zeros_like(dgs_ref)
            dga_ref[...] = jnp.zeros_like(dga_ref)
            db_acc[...] = jnp.zeros_like(db_acc)

        t = _sgu_forward_tile(uv_ref[...], w_ref, b_ref[...], gs_ref[...])
        na, ra = _rms_stats(t['ya'])
        dyn = dy_ref[...]
        dga_ref[...] += jnp.sum(dyn * na, axis=0, keepdims=True)
        dya = _rms_bwd(dyn * ga_ref[...], na, ra)
        dug = dya * t['mixed']
        dmixed = dya * t['ug']
        dmb = dmixed.astype(MXU_DTYPE)
        masks = _half_masks(MXU_DTYPE)
        chunks = []
        db = jnp.zeros((CHUNK, WIDTH_A), F32)
        for c in range(tm // CHUNK):
            rows = slice(c * CHUNK, (c + 1) * CHUNK)
            db = db + dmixed[rows]
            groups = []
            for gp in range(2):
                cols = slice(gp * LANES, (gp + 1) * LANES)
                dm_g = dmb[rows, cols]
                vn_g = t['vn'][rows, cols]
                dvn_g = jnp.zeros((CHUNK, LANES), F32)
                for j in range(2):
                    dm_h = dm_g * masks[j]
                    dvn_g = dvn_g + _dot(wt_ref[2 * gp + j], dm_h)
                    dw_ref[2 * gp + j] += _dot_nt(dm_h, vn_g)
                groups.append(dvn_g)
            chunks.append(jnp.concatenate(groups, axis=1))
        db_acc[...] += db
        dvn = jnp.concatenate(chunks, axis=0)
        vhat = t['vhat']
        dgs_ref[...] += jnp.sum(dvn * vhat, axis=0, keepdims=True)
        dvh = dvn * gs_ref[...]
        dvg = t['rs'] * (dvh - jnp.mean(dvh, axis=-1, keepdims=True)
                         - vhat * jnp.mean(dvh * vhat, axis=-1, keepdims=True))
        duv_ref[:, :WIDTH_A] = (dug * _gelu_grad(t['u'], t['tu'])).astype(MXU_DTYPE)
        duv_ref[:, WIDTH_A:] = (dvg * _gelu_grad(t['v'], t['tv'])).astype(MXU_DTYPE)

        @pl.when(i == pl.num_programs(0) - 1)
        def _():
            lane_a = lax.broadcasted_iota(jnp.int32, (CHUNK, WIDTH_A), 1)
            lane = lax.broadcasted_iota(jnp.int32, (CHUNK, LANES), 1)
            acc = db_acc[...]
            out = jnp.zeros((CHUNK, LANES), F32)
            for h in range(HEADS_A):
                col = jnp.sum(jnp.where(lane_a // HEAD_DIM == h, acc, 0.0), axis=1, keepdims=True)
                out = jnp.where(lane == h, col, out)
            db_ref[...] = out
            causal = (lax.broadcasted_iota(jnp.int32, (CHUNK, CHUNK), 0)
                      >= lax.broadcasted_iota(jnp.int32, (CHUNK, CHUNK), 1))
            for h in range(HEADS_A):
                dw_ref[h] = jnp.where(causal, dw_ref[h], 0.0)

    return pl.pallas_call(
        body, name="sgu_bwd", grid=(s // tm,),
        in_specs=[_rows(tm, 2 * WIDTH_A), _rows(tm, WIDTH_A), _full((HEADS_A, CHUNK, CHUNK)),
                  _full((HEADS_A, CHUNK, CHUNK)), _full((CHUNK, WIDTH_A)), _full((1, WIDTH_A)),
                  _full((1, WIDTH_A))],
        out_specs=[_rows(tm, 2 * WIDTH_A), _full((HEADS_A, CHUNK, CHUNK)), _full((CHUNK, LANES)),
                   _full((1, WIDTH_A)), _full((1, WIDTH_A))],
        out_shape=[jax.ShapeDtypeStruct((s, 2 * WIDTH_A), MXU_DTYPE),
                   jax.ShapeDtypeStruct((HEADS_A, CHUNK, CHUNK), F32), jax.ShapeDtypeStruct((CHUNK, LANES), F32),
                   jax.ShapeDtypeStruct((1, WIDTH_A), F32), jax.ShapeDtypeStruct((1, WIDTH_A), F32)],
        scratch_shapes=[pltpu.VMEM((CHUNK, WIDTH_A), F32)],
        compiler_params=_params(("arbitrary",)),
    )(uv, dya_n, w_tril, w_tril_t, bias, g_sgu, g_a)


def _in_bwd_proj(duv, dq, dk, dv, cos_t, sin_t):
    s = duv.shape[0]
    tm = 512
    nc = WIDTH_B // LANES

    def body(duv_ref, *refs):
        dq_refs, dk_refs, dv_refs = refs[:nc], refs[nc:2 * nc], refs[2 * nc:3 * nc]
        cos_ref, sin_ref, dp_ref = refs[3 * nc:]
        cos = cos_ref[...]
        sin = sin_ref[...]
        dp_ref[:, :2 * WIDTH_A] = duv_ref[...]
        for i in range(nc):
            lo = 2 * WIDTH_A + i * LANES
            tq = _load_l256(dq_refs[i:i + 1], tm) * (HEAD_DIM ** -0.5)
            tk = _load_l256(dk_refs[i:i + 1], tm)
            dp_ref[:, lo:lo + LANES] = (tq * cos + _rope_partner(tq * sin)).astype(MXU_DTYPE)
            dp_ref[:, lo + WIDTH_B:lo + WIDTH_B + LANES] = (tk * cos + _rope_partner(tk * sin)).astype(MXU_DTYPE)
            dp_ref[:, lo + 2 * WIDTH_B:lo + 2 * WIDTH_B + LANES] = _load_l256(dv_refs[i:i + 1], tm).astype(MXU_DTYPE)

    return pl.pallas_call(
        body, name="in_bwd_proj", grid=(s // tm,),
        in_specs=[_rows(tm, 2 * WIDTH_A)] + 3 * _col_specs(tm, WIDTH_B) + [_rows(tm, LANES), _rows(tm, LANES)],
        out_specs=_rows(tm, IN_COLS), out_shape=jax.ShapeDtypeStruct((s, IN_COLS), MXU_DTYPE),
        compiler_params=_params(("arbitrary",)),
    )(duv, *([dq] * nc), *([dk] * nc), *([dv] * nc), cos_t, sin_t)


def _in_bwd_x(dproj, w_in_t, x, g_mix, dh1):
    s = x.shape[0]
    tm = 512

    def body(dp_ref, wt_ref, x_ref, g_ref, dh_ref, gx_ref, dg_ref):
        @pl.when(pl.program_id(0) == 0)
        def _():
            dg_ref[...] = jnp.zeros_like(dg_ref)

        dhn = _dot(dp_ref[...], wt_ref[...])
        n, r = _rms_stats(x_ref[...])
        dg_ref[...] += jnp.sum(dhn * n, axis=0, keepdims=True)
        gx_ref[...] = dh_ref[...] + _rms_bwd(dhn * g_ref[...], n, r)

    return pl.pallas_call(
        body, name="in_bwd_x", grid=(s // tm,),
        in_specs=[_rows(tm, IN_COLS), _full((IN_COLS, D_MODEL)), _rows(tm, D_MODEL), _full((1, D_MODEL)),
                  _rows(tm, D_MODEL)],
        out_specs=[_rows(tm, D_MODEL), _full((1, D_MODEL))],
        out_shape=[jax.ShapeDtypeStruct((s, D_MODEL), F32), jax.ShapeDtypeStruct((1, D_MODEL), F32)],
        compiler_params=_params(("arbitrary",)),
    )(dproj, w_in_t, x, g_mix, dh1)


def _wgrad(a, bs, name, transposed=False):
    s, m = a.shape
    bm = 512 if m % 512 == 0 else (FF_HALF if m == D_FF else m)
    ts = 512
    nsteps = s // ts
    nb = len(bs)

    def body(a_ref, *refs):
        b_refs, o_refs, accs = refs[:nb], refs[nb:2 * nb], refs[2 * nb:]
        kk = pl.program_id(1)
        at = a_ref[...].astype(MXU_DTYPE)
        for b_ref, o_ref, acc in zip(b_refs, o_refs, accs):
            c = _dot_tn(at, b_ref[...].astype(MXU_DTYPE))

            @pl.when(kk == 0)
            def _():
                acc[...] = c

            @pl.when(kk > 0)
            def _():
                acc[...] += c

            @pl.when(kk == nsteps - 1)
            def _():
                total = acc[...]
                o_ref[...] = (total.T if transposed else total).astype(o_ref.dtype)

    if transposed:
        out_specs = [pl.BlockSpec((b.shape[1], bm), lambda i, kk: (0, i)) for b in bs]
        out_shape = [jax.ShapeDtypeStruct((b.shape[1], m), jnp.bfloat16) for b in bs]
    else:
        out_specs = [pl.BlockSpec((bm, b.shape[1]), lambda i, kk: (i, 0)) for b in bs]
        out_shape = [jax.ShapeDtypeStruct((m, b.shape[1]), jnp.bfloat16) for b in bs]
    return pl.pallas_call(
        body, name=name, grid=(m // bm, nsteps),
        in_specs=[pl.BlockSpec((ts, bm), lambda i, kk: (kk, i))]
        + [pl.BlockSpec((ts, b.shape[1]), lambda i, kk: (kk, 0)) for b in bs],
        out_specs=out_specs, out_shape=out_shape,
        scratch_shapes=[pltpu.VMEM((bm, b.shape[1]), F32) for b in bs],
        compiler_params=_params(("arbitrary", "arbitrary")),
    )(a, *bs)


def _rope_tables(s):
    half = HEAD_DIM // 2
    inv = ROPE_THETA ** (-jnp.arange(half, dtype=F32) / half)
    ang = jnp.arange(s, dtype=F32)[:, None] * inv[None, :]
    cos = jnp.cos(ang)
    sin = jnp.sin(ang)
    cos_t = jnp.concatenate([cos, cos, cos, cos], axis=1)
    sin_t = jnp.concatenate([-sin, sin, -sin, sin], axis=1)
    return cos_t, sin_t


MESH = pl.DeviceIdType.MESH
ANY = pl.BlockSpec(memory_space=pl.ANY)
SEM = pl.BlockSpec(memory_space=pltpu.SEMAPHORE)
SPLIT_COPY = pltpu.CompilerParams(has_side_effects=pltpu.SideEffectType.DATAFLOW_SIDE_EFFECTING)
SLAB_IS_TRANSPOSED = {'w_in': True, 'w_out': False, 'w_gate': True, 'w_up': True, 'w_down': False,
                      'w_ple_gate': False, 'w_ple_proj': True}


def _place():
    x, y, c = lax.axis_index("x"), lax.axis_index("y"), lax.axis_index("c")
    other_chips = [(1 - x, y), (x, 1 - y), (1 - x, 1 - y)]
    return x, y, c, other_chips


def _chip_of(chip):
    return 2 * chip[0] + chip[1]


def _half(ref, lead, hc):
    hr = ref.shape[1] // 2
    return ref.at[lead, pl.ds(hc * hr, hr), :]


def _put_own(stack, own, index):
    return lax.dynamic_update_slice(stack, own[None], (index,) + (0,) * own.ndim)


def _all_gather_now(slab):
    rows, cols = slab.shape

    def body(x_ref, out_ref, send_sems, recv_sems):
        x, y, c, chips = _place()
        sibling = (x, y, 1 - c)
        hr = rows // 2

        def copy(k, src, dst, to):
            return pltpu.make_async_remote_copy(src_ref=src, dst_ref=dst, send_sem=send_sems.at[k],
                                                recv_sem=recv_sems.at[k], device_id=to, device_id_type=MESH)

        my_half = x_ref.at[pl.ds(c * hr, hr), :]
        first = [copy(j, my_half, _half(out_ref, 2 * x + y, c), (*chip, c)) for j, chip in enumerate(chips)]
        for cp in first:
            cp.start()
        passed = [copy(3 + j, _half(out_ref, _chip_of(chip), c), _half(out_ref, _chip_of(chip), c), sibling)
                  for j, chip in enumerate(chips)]
        for j, chip in enumerate(chips):
            copy(j, my_half, _half(out_ref, _chip_of(chip), c), (*chip, c)).wait_recv()
            passed[j].start()
        for j, chip in enumerate(chips):
            copy(3 + j, my_half, _half(out_ref, _chip_of(chip), 1 - c), sibling).wait_recv()
        for cp in first + passed:
            cp.wait_send()

    gathered = pl.pallas_call(
        body, name="all_gather_now", out_shape=jax.ShapeDtypeStruct((N_CHIPS, rows, cols), slab.dtype),
        in_specs=[ANY], out_specs=ANY,
        scratch_shapes=[pltpu.SemaphoreType.DMA((6,)), pltpu.SemaphoreType.DMA((6,))],
    )(slab)
    me = 2 * lax.axis_index("x") + lax.axis_index("y")
    return _put_own(gathered, slab, me).reshape(N_CHIPS * rows, cols)


def _gather_copies(slab_refs, land_refs, send_sems, recv_sems):
    x, y, c, chips = _place()
    sends, recvs = [], []
    for k, (src, land) in enumerate(zip(slab_refs, land_refs)):
        hr = src.shape[0] // 2
        for j, chip in enumerate(chips):
            for t in range(2):
                sends.append(pltpu.make_async_remote_copy(
                    src_ref=src.at[pl.ds(c * hr, hr), :], dst_ref=_half(land, 2 * x + y, c),
                    send_sem=send_sems.at[6 * k + 2 * j + t], recv_sem=recv_sems.at[6 * k + 2 * j + c],
                    device_id=(*chip, t), device_id_type=MESH))
                recvs.append(pltpu.make_async_remote_copy(
                    src_ref=src.at[pl.ds(t * hr, hr), :], dst_ref=_half(land, _chip_of(chip), t),
                    send_sem=send_sems.at[6 * k + 2 * j + t], recv_sem=recv_sems.at[6 * k + 2 * j + t],
                    device_id=(*chip, t), device_id_type=MESH))
    return sends, recvs


def _all_gather_start(slabs, after):
    n = len(slabs)

    def body(*refs):
        slab_refs, land_refs = refs[:n], refs[n:2 * n]
        send_sems, recv_sems = refs[2 * n + 1:2 * n + 3]
        token = refs[-1]
        sends, _ = _gather_copies(slab_refs, land_refs, send_sems, recv_sems)
        for cp in sends:
            cp.start()
        token[...] = jnp.zeros_like(token)

    lands = [lax.empty((N_CHIPS,) + s.shape, s.dtype) for s in slabs]
    hbm = lambda a: pltpu.HBM(a.shape, a.dtype)
    res = pl.pallas_call(
        body, name="all_gather_start",
        out_shape=(pltpu.SemaphoreType.DMA((6 * n,)), pltpu.SemaphoreType.DMA((6 * n,)), *map(hbm, slabs),
                   *map(hbm, lands), jax.ShapeDtypeStruct((8, LANES), F32)),
        in_specs=[ANY] * (2 * n + 1),
        out_specs=(SEM, SEM, *([ANY] * (2 * n)), pl.BlockSpec(memory_space=pltpu.VMEM)),
        input_output_aliases={i: 2 + i for i in range(2 * n)}, compiler_params=SPLIT_COPY,
    )(*[pltpu.with_memory_space_constraint(a, pltpu.HBM) for a in list(slabs) + lands], after)
    return res[:-1], res[-1]


def _all_gather_wait(handle, after):
    send_sems, recv_sems = handle[:2]
    n = (len(handle) - 2) // 2
    slabs, lands = handle[2:2 + n], handle[2 + n:]

    def body(*refs):
        slab_refs, land_refs = refs[:n], refs[n:2 * n]
        send_sems, recv_sems = refs[2 * n:2 * n + 2]
        sends, recvs = _gather_copies(slab_refs, land_refs, send_sems, recv_sems)
        for cp in sends:
            cp.wait_send()
        for cp in recvs:
            cp.wait_recv()

    hbm = lambda a: pltpu.HBM(a.shape, a.dtype)
    res = pl.pallas_call(
        body, name="all_gather_wait", out_shape=tuple(map(hbm, list(slabs) + list(lands))),
        in_specs=[ANY] * (2 * n) + [SEM, SEM, ANY], out_specs=tuple([ANY] * (2 * n)),
        input_output_aliases={i: i for i in range(2 * n)}, compiler_params=SPLIT_COPY,
    )(*slabs, *lands, send_sems, recv_sems, after)
    me = 2 * lax.axis_index("x") + lax.axis_index("y")
    return [_put_own(land, slab, me).reshape(N_CHIPS * slab.shape[0], slab.shape[1])
            for slab, land in zip(res[:n], res[n:])]


def _scatter_copies(part_refs, land_refs, send_sems, recv_sems):
    x, y, c, chips = _place()
    me = 4 * x + 2 * y + c
    sends, recvs = [], []
    for k, (part, land) in enumerate(zip(part_refs, land_refs)):
        for j, chip in enumerate(chips):
            for h in range(2):
                sends.append(pltpu.make_async_remote_copy(
                    src_ref=_half(part, _chip_of(chip), h), dst_ref=land.at[me],
                    send_sem=send_sems.at[7 * k + 2 * j + h], recv_sem=recv_sems.at[7 * k + 2 * j + c],
                    device_id=(*chip, h), device_id_type=MESH))
                recvs.append(pltpu.make_async_remote_copy(
                    src_ref=_half(part, _chip_of(chip), h), dst_ref=land.at[2 * _chip_of(chip) + h],
                    send_sem=send_sems.at[7 * k + 2 * j + h], recv_sem=recv_sems.at[7 * k + 2 * j + h],
                    device_id=(*chip, h), device_id_type=MESH))
        sends.append(pltpu.make_async_remote_copy(
            src_ref=_half(part, 2 * x + y, 1 - c), dst_ref=land.at[me], send_sem=send_sems.at[7 * k + 6],
            recv_sem=recv_sems.at[7 * k + 6], device_id=(x, y, 1 - c), device_id_type=MESH))
        recvs.append(pltpu.make_async_remote_copy(
            src_ref=_half(part, 2 * x + y, 1 - c), dst_ref=land.at[4 * x + 2 * y + 1 - c],
            send_sem=send_sems.at[7 * k + 6], recv_sem=recv_sems.at[7 * k + 6], device_id=(x, y, 1 - c),
            device_id_type=MESH))
    return sends, recvs


def _reduce_scatter_start(parts, name):
    n = len(parts)
    parts = [p.reshape(N_CHIPS, p.shape[0] // N_CHIPS, p.shape[1]) for p in parts]

    def body(*refs):
        part_refs, land_refs = refs[:n], refs[n:2 * n]
        send_sems, recv_sems = refs[2 * n:2 * n + 2]
        token = refs[-1]
        sends, _ = _scatter_copies(part_refs, land_refs, send_sems, recv_sems)
        for cp in sends:
            cp.start()
        token[...] = jnp.zeros_like(token)

    lands = [lax.empty((N_DEV, p.shape[1] // 2, p.shape[2]), p.dtype) for p in parts]
    hbm = lambda a: pltpu.HBM(a.shape, a.dtype)
    res = pl.pallas_call(
        body, name=name,
        out_shape=(pltpu.SemaphoreType.DMA((7 * n,)), pltpu.SemaphoreType.DMA((7 * n,)), *map(hbm, parts),
                   *map(hbm, lands), jax.ShapeDtypeStruct((8, LANES), F32)),
        in_specs=[ANY] * (2 * n), out_specs=(SEM, SEM, *([ANY] * (2 * n)), pl.BlockSpec(memory_space=pltpu.VMEM)),
        input_output_aliases={i: 2 + i for i in range(2 * n)}, compiler_params=SPLIT_COPY,
    )(*[pltpu.with_memory_space_constraint(a, pltpu.HBM) for a in parts + lands])
    return res[:-1], res[-1]


def _reduce_scatter_wait(handle, after, name):
    send_sems, recv_sems = handle[:2]
    n = (len(handle) - 2) // 2
    parts, lands = handle[2:2 + n], handle[2 + n:]

    def body(*refs):
        part_refs, land_refs = refs[:n], refs[n:2 * n]
        send_sems, recv_sems = refs[2 * n:2 * n + 2]
        sends, recvs = _scatter_copies(part_refs, land_refs, send_sems, recv_sems)
        for cp in sends:
            cp.wait_send()
        for cp in recvs:
            cp.wait_recv()

    hbm = lambda a: pltpu.HBM(a.shape, a.dtype)
    res = pl.pallas_call(
        body, name=name, out_shape=tuple(map(hbm, list(parts) + list(lands))),
        in_specs=[ANY] * (2 * n) + [SEM, SEM, ANY], out_specs=tuple([ANY] * (2 * n)),
        input_output_aliases={i: i for i in range(2 * n)}, compiler_params=SPLIT_COPY,
    )(*parts, *lands, send_sems, recv_sems, after)
    x, y, c = lax.axis_index("x"), lax.axis_index("y"), lax.axis_index("c")
    out = []
    for part, land in zip(res[:n], res[n:]):
        hr = land.shape[1]
        own = lax.dynamic_slice(part, (2 * x + y, c * hr, 0), (1, hr, part.shape[2]))[0]
        out.append(_put_own(land, own, 4 * x + 2 * y + c))
    return out


def _sum_shares(land, name):
    n, rows, cols = land.shape

    def body(l_ref, o_ref):
        acc = l_ref[0].astype(F32)
        for i in range(1, n):
            acc = acc + l_ref[i].astype(F32)
        o_ref[...] = acc

    return pl.pallas_call(
        body, name=name, grid=(1,), in_specs=[pl.BlockSpec((n, rows, cols), lambda i: (0, 0, 0))],
        out_specs=pl.BlockSpec((rows, cols), lambda i: (0, 0)), out_shape=jax.ShapeDtypeStruct((rows, cols), F32),
        compiler_params=_params(("arbitrary",)),
    )(land)


def _swap_halves(halves):
    n = len(halves)

    def body(*refs):
        in_refs, out_refs, send_sems, recv_sems = refs[:n], refs[n:2 * n], refs[2 * n], refs[2 * n + 1]
        x, y, c, _ = _place()
        copies = [pltpu.make_async_remote_copy(src_ref=i_ref, dst_ref=o_ref, send_sem=send_sems.at[k],
                                               recv_sem=recv_sems.at[k], device_id=(x, y, 1 - c), device_id_type=MESH)
                  for k, (i_ref, o_ref) in enumerate(zip(in_refs, out_refs))]
        for cp in copies:
            cp.start()
        for cp in copies:
            cp.wait()

    others = pl.pallas_call(
        body, name="swap_halves", out_shape=[jax.ShapeDtypeStruct(h.shape, h.dtype) for h in halves],
        in_specs=[ANY] * n, out_specs=[ANY] * n,
        scratch_shapes=[pltpu.SemaphoreType.DMA((n,)), pltpu.SemaphoreType.DMA((n,))],
    )(*halves)
    south = lax.axis_index("c") == 0
    return [jnp.concatenate([jnp.where(south, h, o), jnp.where(south, o, h)], axis=0) for h, o in zip(halves, others)]


def _small_all_reduce(block):
    rows = block.shape[0]

    def body(x_ref, all_ref, sum_ref, send_sems, recv_sems, local_sem):
        x, y, c, chips = _place()
        me, sibling = (x, y, c), (x, y, 1 - c)

        def blk(px, py, pc):
            return all_ref.at[pl.ds((4 * px + 2 * py + pc) * rows, rows), :]

        def copy(k, who, to, src=None):
            return pltpu.make_async_remote_copy(
                src_ref=blk(*who) if src is None else src, dst_ref=blk(*who), send_sem=send_sems.at[k],
                recv_sem=recv_sems.at[k], device_id=to, device_id_type=MESH)

        mine = pltpu.make_async_copy(x_ref, blk(*me), local_sem)
        mine.start()
        first = [copy(0, me, sibling, src=x_ref)]
        first += [copy(1 + j, me, (*chip, c), src=x_ref) for j, chip in enumerate(chips)]
        for cp in first:
            cp.start()
        passed = [copy(4 + j, (*chip, c), sibling) for j, chip in enumerate(chips)]
        for j, chip in enumerate(chips):
            copy(1 + j, (*chip, c), me).wait_recv()
            passed[j].start()
        copy(0, sibling, me).wait_recv()
        for j, chip in enumerate(chips):
            copy(4 + j, (*chip, 1 - c), me).wait_recv()
        for cp in first + passed:
            cp.wait_send()
        mine.wait()
        acc = all_ref[pl.ds(0, rows), :]
        for dev in range(1, N_DEV):
            acc = acc + all_ref[pl.ds(dev * rows, rows), :]
        sum_ref[...] = acc

    vmem = pl.BlockSpec(memory_space=pltpu.VMEM)
    return pl.pallas_call(
        body, name="small_all_reduce",
        out_shape=[jax.ShapeDtypeStruct((N_DEV * rows, D_MODEL), F32), jax.ShapeDtypeStruct((rows, D_MODEL), F32)],
        in_specs=[vmem], out_specs=[vmem, vmem],
        scratch_shapes=[pltpu.SemaphoreType.DMA((7,)), pltpu.SemaphoreType.DMA((7,)), pltpu.SemaphoreType.DMA],
    )(block)[1]


def _adamw(w, g, m, v, name):
    rows, cols = w.shape
    tm = rows
    if rows > 512:
        tm = next(t for t in range(512, 7, -8) if rows % t == 0)

    def body(w_ref, g_ref, m_ref, v_ref, d_ref, nm_ref, nv_ref):
        g_ = g_ref[...]
        m_ = ADAM_B1 * m_ref[...] + (1.0 - ADAM_B1) * g_
        v_ = ADAM_B2 * v_ref[...] + (1.0 - ADAM_B2) * (g_ * g_)
        m_hat = m_ / (1.0 - ADAM_B1 ** ADAM_STEP)
        v_hat = v_ / (1.0 - ADAM_B2 ** ADAM_STEP)
        d_ref[...] = -ADAM_LR * (m_hat / (jnp.sqrt(v_hat) + ADAM_EPS) + ADAM_WD * w_ref[...])
        nm_ref[...] = m_
        nv_ref[...] = v_

    spec = pl.BlockSpec((tm, cols), lambda i: (i, 0))
    return pl.pallas_call(
        body, name=name, grid=(rows // tm,), in_specs=[spec] * 4, out_specs=[spec] * 3,
        out_shape=[jax.ShapeDtypeStruct(w.shape, F32)] * 3, compiler_params=_params(("arbitrary",)),
    )(w, g, m, v)


def _pack_small(values):
    flat = jnp.concatenate([values[n].reshape(-1).astype(F32) for n in SMALL])
    return jnp.pad(flat, (0, SMALL_ROWS * D_MODEL - flat.shape[0])).reshape(SMALL_ROWS, D_MODEL)


def _unpack_small(block, shapes):
    flat = block.reshape(-1)
    out, lo = {}, 0
    for n in SMALL:
        out[n] = flat[lo:lo + SMALL_SIZES[n]].reshape(shapes[n])
        lo += SMALL_SIZES[n]
    return out


def _after(token, a):
    return a + token[:1, :1].astype(a.dtype)


def kernel(x, p, mix_norm_g, w_in, sgu_w, sgu_b, sgu_norm_g, out_norm_a, out_norm_b, w_out, ffn_norm_g, w_gate, w_up, w_down, ple_norm_g, w_ple_gate, w_ple_proj, final_norm_g, loss_target, m_mix_norm_g, m_w_in, m_sgu_w, m_sgu_b, m_sgu_norm_g, m_out_norm_a, m_out_norm_b, m_w_out, m_ffn_norm_g, m_w_gate, m_w_up, m_w_down, m_ple_norm_g, m_w_ple_gate, m_w_ple_proj, m_final_norm_g, v_mix_norm_g, v_w_in, v_sgu_w, v_sgu_b, v_sgu_norm_g, v_out_norm_a, v_out_norm_b, v_w_out, v_ffn_norm_g, v_w_gate, v_w_up, v_w_down, v_ple_norm_g, v_w_ple_gate, v_w_ple_proj, v_final_norm_g):
    given = dict(locals())
    xs, ps, target = x[0], p[0, 0], loss_target[0]
    s = xs.shape[0]

    def slab_of(name):
        shard = given[name][0].astype(MXU_DTYPE)
        return shard.T if SLAB_IS_TRANSPOSED[name] else shard

    w_in_t = _all_gather_now(slab_of('w_in'))
    later = ['w_out', 'w_gate', 'w_up', 'w_down', 'w_ple_gate', 'w_ple_proj']
    gather, token = _all_gather_start([slab_of(n) for n in later], w_in_t)

    cos_t, sin_t = _rope_tables(s)
    tril = jnp.tril(jnp.ones((CHUNK, CHUNK), F32))
    w_tril = (sgu_w.reshape(HEADS_A, CHUNK, CHUNK) * tril).astype(MXU_DTYPE)
    w_tril_t = jnp.swapaxes(w_tril, 1, 2)
    bias = jnp.repeat(sgu_b.reshape(HEADS_A, CHUNK).T, HEAD_DIM, axis=1)
    g = {n: given[n].reshape(1, -1) for n in SMALL if n not in ('sgu_w', 'sgu_b')}

    uv, q, k, v, hn1 = _in_fwd(xs, _after(token, g['mix_norm_g']), w_in_t.T, cos_t, sin_t)
    ya_n = _sgu_fwd(uv, w_tril, bias, g['sgu_norm_g'], g['out_norm_a'])
    state = None
    for i, dil in enumerate(DILATIONS):
        state = _attn_fwd_branch(q, k, v, state, dil, last=(i == len(DILATIONS) - 1))
    y_b, lse = state
    stacks = dict(zip(later, _all_gather_wait(gather, lse)))
    w_gate_t, w_up_t, w_pp_t = stacks['w_gate'], stacks['w_up'], stacks['w_ple_proj']
    h1, y_n = _out_fwd(ya_n, y_b, g['out_norm_b'], stacks['w_out'], xs)
    h2, gate, up, hn2 = _ffn_fwd(h1, g['ffn_norm_g'], w_gate_t.T, w_up_t.T, stacks['w_down'])
    loss, dh2, dz, dpp, hn3, d_ple_g, d_final_g = _ple_loss(
        h2, ps, target, g['ple_norm_g'], stacks['w_ple_gate'], stacks['w_ple_gate'].T, w_pp_t.T, g['final_norm_g'])

    share = {}
    share['w_ple_gate'], = _wgrad(hn3, [dz], "wgrad_ple_gate")
    share['w_ple_proj'], = _wgrad(ps, [dpp], "wgrad_ple_proj", transposed=True)
    scatter_1, token = _reduce_scatter_start([share['w_ple_gate'], share['w_ple_proj']], "reduce_scatter_start_1")
    dh1, act, dgate, dup, d_ffn_g = _ffn_bwd(dh2, h1, gate, up, _after(token, g['ffn_norm_g']), stacks['w_down'].T,
                                             w_gate_t, w_up_t)
    share['w_down'], = _wgrad(act, [dh2], "wgrad_down")
    share['w_gate'], share['w_up'] = _wgrad(hn2, [dgate, dup], "wgrad_gate_up", transposed=True)
    scatter_2, token = _reduce_scatter_start([share['w_down'], share['w_gate'], share['w_up']],
                                             "reduce_scatter_start_2")
    dya_n, dyb, d_out_b = _out_bwd(dh1, y_b, _after(token, g['out_norm_b']), stacks['w_out'].T)
    share['w_out'], = _wgrad(y_n, [dh1], "wgrad_out")
    scatter_3, token = _reduce_scatter_start([share['w_out']], "reduce_scatter_start_3")
    grads = None
    for dil in DILATIONS:
        grads = _attn_bwd_branch(q, k, v, dyb, y_b, lse, grads, dil)
    duv, d_sgu_w, d_sgu_b, d_sgu_g, d_out_a = _sgu_bwd(uv, dya_n, w_tril, w_tril_t, bias,
                                                       _after(token, g['sgu_norm_g']), g['out_norm_a'])
    dproj = _in_bwd_proj(duv, grads[0], grads[1], grads[2], cos_t, sin_t)
    share['w_in'], = _wgrad(hn1, [dproj], "wgrad_in", transposed=True)
    scatter_4, token = _reduce_scatter_start([share['w_in']], "reduce_scatter_start_4")
    grad_x, d_mix_g = _in_bwd_x(dproj, w_in_t, xs, _after(token, g['mix_norm_g']), dh1)

    gs = {'mix_norm_g': d_mix_g, 'sgu_w': d_sgu_w, 'sgu_b': d_sgu_b[:, :HEADS_A].T, 'sgu_norm_g': d_sgu_g,
          'out_norm_a': d_out_a, 'out_norm_b': d_out_b, 'ffn_norm_g': d_ffn_g, 'ple_norm_g': d_ple_g,
          'final_norm_g': d_final_g}
    gs_block = _pack_small(gs).at[SMALL_ROWS - 1, 0].set(loss[0, 0])
    small_sum = _small_all_reduce(gs_block)
    loss_out = small_sum[SMALL_ROWS - 1, 0]
    small_shapes = {n: given[n].shape for n in SMALL}

    landed = {}
    for names, handle, tag in ((['w_ple_gate', 'w_ple_proj'], scatter_1, "1"), (['w_down', 'w_gate', 'w_up'], scatter_2, "2"),
                               (['w_out'], scatter_3, "3"), (['w_in'], scatter_4, "4")):
        landed.update(zip(names, _reduce_scatter_wait(handle, small_sum, "reduce_scatter_wait_" + tag)))
    slabs = dict(zip(SHARDED, _swap_halves([_sum_shares(landed[n], "sum_shares_" + n) for n in SHARDED])))

    grads, deltas, new_m, new_v = {}, {}, {}, {}
    for n in SHARDED:
        g_shard = slabs[n].T if SLAB_IS_TRANSPOSED[n] else slabs[n]
        d, nm, nv = _adamw(given[n][0], g_shard, given["m_" + n][0], given["v_" + n][0], "adamw_" + n)
        grads[n], deltas[n], new_m[n], new_v[n] = g_shard[None], d[None], nm[None], nv[None]

    small = {n: given[n] for n in SMALL}
    d, nm, nv = _adamw(_pack_small(small), small_sum, _pack_small({n: given["m_" + n] for n in SMALL}),
                       _pack_small({n: given["v_" + n] for n in SMALL}), "adamw_small")
    for res, blk in ((grads, small_sum), (deltas, d), (new_m, nm), (new_v, nv)):
        res.update(_unpack_small(blk, small_shapes))

    outs = [loss_out, grad_x[None]]
    for res in (grads, deltas, new_m, new_v):
        outs += [res[n] for n in WEIGHT_NAMES]
    return tuple(outs)
```

```python
import math

import jax
import jax.numpy as jnp
import numpy as np
from jax import lax
from jax.experimental import pallas as pl
from jax.experimental.pallas import tpu as pltpu

F32 = jnp.float32
MXU_DTYPE = jnp.bfloat16

D_MODEL = 1024
HEAD_DIM = 64
HEADS_A = 4
HEADS_B = 12
WIDTH_A = HEADS_A * HEAD_DIM
WIDTH_B = HEADS_B * HEAD_DIM
CHUNK = 128
BLOCK = 128
DILATIONS = (1, 4, 16)
ROPE_THETA = 10000.0
D_FF = 2816
FF_HALF = D_FF // 2
PLE_DIM = 256
IN_COLS = 2 * WIDTH_A + 3 * WIDTH_B
EPS = 1e-6
LANES = 128
N_CHIPS = 4
N_DEV = 8

ADAM_LR = 0.001
ADAM_B1 = 0.9
ADAM_B2 = 0.999
ADAM_EPS = 1e-08
ADAM_WD = 0.01
ADAM_STEP = 10

VMEM_LIMIT = 56 * 1024 * 1024

WEIGHT_NAMES = ['mix_norm_g', 'w_in', 'sgu_w', 'sgu_b', 'sgu_norm_g', 'out_norm_a', 'out_norm_b', 'w_out',
                'ffn_norm_g', 'w_gate', 'w_up', 'w_down', 'ple_norm_g', 'w_ple_gate', 'w_ple_proj', 'final_norm_g']
SHARDED = ['w_in', 'w_out', 'w_gate', 'w_up', 'w_down', 'w_ple_gate', 'w_ple_proj']
SMALL = ['mix_norm_g', 'sgu_w', 'sgu_b', 'sgu_norm_g', 'out_norm_a', 'out_norm_b', 'ffn_norm_g', 'ple_norm_g',
         'final_norm_g']
SMALL_SIZES = {'mix_norm_g': 1024, 'sgu_w': 65536, 'sgu_b': 512, 'sgu_norm_g': 256, 'out_norm_a': 256,
               'out_norm_b': 768, 'ffn_norm_g': 1024, 'ple_norm_g': 1024, 'final_norm_g': 1024}
SMALL_ROWS = 72


def _params(semantics=None):
    return pltpu.CompilerParams(dimension_semantics=semantics, vmem_limit_bytes=VMEM_LIMIT)


def _full(shape):
    nd = len(shape)
    return pl.BlockSpec(shape, lambda i: (0,) * nd)


def _rows(tm, width):
    return pl.BlockSpec((tm, width), lambda i: (i, 0))


def _rms_stats(x):
    r = lax.rsqrt(jnp.mean(x * x, axis=-1, keepdims=True) + EPS)
    return x * r, r


def _rms_bwd(dn, n, r):
    return r * (dn - n * jnp.mean(dn * n, axis=-1, keepdims=True))


def _dot(a, b):
    return jnp.dot(a, b, preferred_element_type=F32)


def _dot_nt(a, b):
    return lax.dot_general(a, b, (((1,), (1,)), ((), ())), preferred_element_type=F32)


def _dot_tn(a, b):
    return lax.dot_general(a, b, (((0,), (0,)), ((), ())), preferred_element_type=F32)


def _gelu_parts(x):
    c = math.sqrt(2.0 / math.pi)
    t = jnp.tanh(c * (x + 0.044715 * x * x * x))
    return 0.5 * x * (1.0 + t), t


def _gelu_grad(x, t):
    c = math.sqrt(2.0 / math.pi)
    return 0.5 * (1.0 + t) + 0.5 * x * (1.0 - t * t) * c * (1.0 + 3.0 * 0.044715 * x * x)


def _half_masks(dtype):
    lane = lax.broadcasted_iota(jnp.int32, (BLOCK, LANES), 1)
    lo = (lane < HEAD_DIM).astype(F32)
    return lo.astype(dtype), (1.0 - lo).astype(dtype)


def _rope_partner(t):
    lane = lax.broadcasted_iota(jnp.int32, t.shape, 1)
    first_half = (lane % HEAD_DIM) < (HEAD_DIM // 2)
    return jnp.where(first_half, pltpu.roll(t, LANES - HEAD_DIM // 2, 1), pltpu.roll(t, HEAD_DIM // 2, 1))


L_BLOCK = 256
L_GROUP = 16


def _store_l256(scr, out_ref, cols, value):
    tm = value.shape[0]
    scr[...] = value
    for blk in range(tm // L_BLOCK):
        for r in range(L_GROUP):
            lo = blk * L_BLOCK + r * L_GROUP
            piece = scr[pl.ds(blk * L_BLOCK + r, L_GROUP, stride=L_GROUP), :]
            out_ref[lo:lo + L_GROUP, cols] = piece.astype(out_ref.dtype)


def _load_l256(col_refs, tm):
    cols = []
    for ref in col_refs:
        pieces = [ref[pl.ds(blk * L_BLOCK + i, L_GROUP, stride=L_GROUP), :]
                  for blk in range(tm // L_BLOCK) for i in range(L_GROUP)]
        cols.append(jnp.concatenate(pieces, axis=0))
    return jnp.concatenate(cols, axis=1)


def _col_specs(tm, width):
    return [pl.BlockSpec((tm, LANES), lambda i, j=j: (i, j)) for j in range(width // LANES)]


def _in_fwd(x, g_mix, w_in, cos_t, sin_t):
    s = x.shape[0]
    tm = 512

    def body(x_ref, g_ref, w_ref, cos_ref, sin_ref, uv_ref, q_ref, k_ref, v_ref, hn_ref, scr):
        n, _ = _rms_stats(x_ref[...])
        hn = (n * g_ref[...]).astype(MXU_DTYPE)
        hn_ref[...] = hn
        proj = _dot(hn, w_ref[...])
        uv_ref[...] = proj[:, :2 * WIDTH_A]
        cos = cos_ref[...]
        sin = sin_ref[...]
        for i in range(WIDTH_B // LANES):
            lo = 2 * WIDTH_A + i * LANES
            tq = proj[:, lo:lo + LANES]
            tk = proj[:, lo + WIDTH_B:lo + WIDTH_B + LANES]
            tv = proj[:, lo + 2 * WIDTH_B:lo + 2 * WIDTH_B + LANES]
            cols = slice(i * LANES, (i + 1) * LANES)
            _store_l256(scr, q_ref, cols, (tq * cos + _rope_partner(tq) * sin) * (HEAD_DIM ** -0.5))
            _store_l256(scr, k_ref, cols, tk * cos + _rope_partner(tk) * sin)
            _store_l256(scr, v_ref, cols, tv)

    return pl.pallas_call(
        body, name="in_fwd", grid=(s // tm,), scratch_shapes=[pltpu.VMEM((tm, LANES), F32)],
        in_specs=[_rows(tm, D_MODEL), _full((1, D_MODEL)), _full((D_MODEL, IN_COLS)), _rows(tm, LANES),
                  _rows(tm, LANES)],
        out_specs=[_rows(tm, 2 * WIDTH_A), _rows(tm, WIDTH_B), _rows(tm, WIDTH_B), _rows(tm, WIDTH_B),
                   _rows(tm, D_MODEL)],
        out_shape=[jax.ShapeDtypeStruct((s, 2 * WIDTH_A), F32), jax.ShapeDtypeStruct((s, WIDTH_B), MXU_DTYPE),
                   jax.ShapeDtypeStruct((s, WIDTH_B), MXU_DTYPE), jax.ShapeDtypeStruct((s, WIDTH_B), MXU_DTYPE),
                   jax.ShapeDtypeStruct((s, D_MODEL), MXU_DTYPE)],
        compiler_params=_params(("arbitrary",)),
    )(x, g_mix, w_in, cos_t, sin_t)


class _Branch:
    def __init__(self, dil, s):
        self.dil = dil
        i = np.arange(L_GROUP)
        if dil == 16:
            self.grid = (16, s // 2048)
            self.shape = (8, 1, 1, L_GROUP)
            self.index = lambda r, n: (n, r // 4, r % 4, 0, 0)
            pos = (np.arange(8)[:, None] * 16 + i[None, :]).reshape(-1)
        elif dil == 4:
            self.grid = (4, s // 512)
            self.shape = (2, 4, 1, L_GROUP)
            self.index = lambda r, n: (n, 0, r, 0, 0)
            pos = (np.arange(2)[:, None, None] * 64 + np.arange(4)[None, :, None] + 4 * i[None, None, :]).reshape(-1)
        else:
            self.grid = (1, s // L_BLOCK)
            self.shape = (1, 4, 4, L_GROUP)
            self.index = lambda r, n: (n, 0, 0, 0, 0)
            pos = (np.arange(16)[:, None] + 16 * i[None, :]).reshape(-1)
        self.qn = pos.shape[0]
        self.nb = self.grid[1]
        dist = pos[:, None] - np.concatenate([pos - self.qn, pos])[None, :]
        band = (dist >= 0) & (dist <= BLOCK)
        start = band & (np.arange(2 * self.qn)[None, :] >= self.qn)
        self.bias = np.where(np.stack([band, start]), 0.0, -np.inf).astype(np.float32)

    def view(self, a):
        return a.reshape(a.shape[0] // L_BLOCK, 4, 4, L_GROUP, a.shape[1])

    def spec(self, w, step=lambda n: n):
        return pl.BlockSpec(self.shape + (w,), lambda r, n: self.index(r, step(n)))

    def bias_spec(self, step=lambda n: n):
        return pl.BlockSpec((1, self.qn, 2 * self.qn), lambda r, n: (jnp.where(step(n) == 0, 1, 0), 0, 0))

    def load(self, ref, cols=slice(None)):
        x = ref[:, :, :, :, cols]
        return x.reshape(self.qn, x.shape[-1])

    def store(self, ref, cols, value):
        ref[:, :, :, :, cols] = value.reshape(self.shape + (value.shape[-1],))


def _attn_fwd_branch(q, k, v, state, dil, last):
    s = q.shape[0]
    br = _Branch(dil, s)
    qn = br.qn
    first = state is None

    def body(*refs):
        bias_ref, q_ref, kc_ref, kp_ref, vc_ref, vp_ref = refs[:6]
        if first:
            outs = refs[6:]
        else:
            acc_ref, m_ref, l_ref = refs[6:9]
            outs = refs[9:]
            m_in = br.load(m_ref)
            l_in = br.load(l_ref)
        o_ref, m_out = outs[0], outs[1]
        bias2 = jnp.concatenate([bias_ref[0], bias_ref[0]], axis=0)
        lane = lax.broadcasted_iota(jnp.int32, (qn, LANES), 1)
        lo = lane < HEAD_DIM
        mask_lo = lo.astype(F32).astype(MXU_DTYPE)
        masks = (mask_lo, 1 - mask_lo)
        m_blk = jnp.zeros((qn, LANES), F32)
        l_blk = jnp.zeros((qn, LANES), F32)
        for hp in range(HEADS_B // 2):
            cols = slice(hp * LANES, (hp + 1) * LANES)
            qp = br.load(q_ref, cols)
            kcat = jnp.concatenate([br.load(kp_ref, cols), br.load(kc_ref, cols)], axis=0)
            vcat = jnp.concatenate([br.load(vp_ref, cols), br.load(vc_ref, cols)], axis=0)
            h0, h1 = 2 * hp, 2 * hp + 1
            sc = _dot_nt(jnp.concatenate([qp * masks[0], qp * masks[1]], axis=0), kcat) + bias2
            m_new = jnp.max(sc, axis=1, keepdims=True)
            if not first:
                m_prev = jnp.concatenate([m_in[:, h0:h0 + 1], m_in[:, h1:h1 + 1]], axis=0)
                m_new = jnp.maximum(m_prev, m_new)
            p = jnp.exp(sc - m_new)
            l_new = jnp.sum(p, axis=1, keepdims=True)
            acc = _dot(p.astype(MXU_DTYPE), vcat)
            if not first:
                alpha = jnp.exp(m_prev - m_new)
                l_new = alpha * jnp.concatenate([l_in[:, h0:h0 + 1], l_in[:, h1:h1 + 1]], axis=0) + l_new
                acc_in = br.load(acc_ref, cols)
                acc = alpha * jnp.concatenate([acc_in, acc_in], axis=0) + acc
            if last:
                acc = acc / l_new
                m_new = m_new + jnp.log(l_new)
            m_blk = jnp.where(lane == h0, m_new[:qn], jnp.where(lane == h1, m_new[qn:], m_blk))
            l_blk = jnp.where(lane == h0, l_new[:qn], jnp.where(lane == h1, l_new[qn:], l_blk))
            br.store(o_ref, cols, jnp.where(lo, acc[:qn], acc[qn:]))
        br.store(m_out, slice(None), m_blk)
        if not last:
            br.store(outs[2], slice(None), l_blk)

    before = lambda n: jnp.maximum(n - 1, 0)
    in_specs = [br.bias_spec(), br.spec(WIDTH_B), br.spec(WIDTH_B), br.spec(WIDTH_B, before), br.spec(WIDTH_B),
                br.spec(WIDTH_B, before)]
    args = [jnp.asarray(br.bias), br.view(q), br.view(k), br.view(k), br.view(v), br.view(v)]
    if not first:
        in_specs += [br.spec(WIDTH_B), br.spec(LANES), br.spec(LANES)]
        args += [br.view(a) for a in state]
    widths = (WIDTH_B, LANES) if last else (WIDTH_B, LANES, LANES)
    res = pl.pallas_call(
        body, name="attn_fwd_d%d" % dil, grid=br.grid, in_specs=in_specs,
        out_specs=[br.spec(w) for w in widths],
        out_shape=[jax.ShapeDtypeStruct((s // L_BLOCK, 4, 4, L_GROUP, w), F32) for w in widths],
        compiler_params=_params(("arbitrary", "arbitrary")),
    )(*args)
    return tuple(a.reshape(s, w) for a, w in zip(res, widths))


def _sgu_forward_tile(uv, w_ref, bias, g_sgu):
    tm = uv.shape[0]
    u = uv[:, :WIDTH_A]
    v = uv[:, WIDTH_A:]
    ug, tu = _gelu_parts(u)
    vg, tv = _gelu_parts(v)
    mu = jnp.mean(vg, axis=-1, keepdims=True)
    vc = vg - mu
    rs = lax.rsqrt(jnp.mean(vc * vc, axis=-1, keepdims=True) + EPS)
    vhat = vc * rs
    vn = (vhat * g_sgu).astype(MXU_DTYPE)
    masks = _half_masks(MXU_DTYPE)
    chunks = []
    for c in range(tm // CHUNK):
        rows = slice(c * CHUNK, (c + 1) * CHUNK)
        groups = []
        for gp in range(2):
            vn_g = vn[rows, gp * LANES:(gp + 1) * LANES]
            groups.append(_dot(w_ref[2 * gp], vn_g * masks[0]) + _dot(w_ref[2 * gp + 1], vn_g * masks[1]))
        chunks.append(jnp.concatenate(groups, axis=1) + bias)
    mixed = jnp.concatenate(chunks, axis=0)
    return dict(u=u, v=v, ug=ug, tu=tu, tv=tv, rs=rs, vhat=vhat, vn=vn, mixed=mixed, ya=ug * mixed)


def _sgu_fwd(uv, w_tril, bias, g_sgu, g_a):
    s = uv.shape[0]
    tm = 512

    def body(uv_ref, w_ref, b_ref, gs_ref, ga_ref, o_ref):
        t = _sgu_forward_tile(uv_ref[...], w_ref, b_ref[...], gs_ref[...])
        n, _ = _rms_stats(t['ya'])
        o_ref[...] = (n * ga_ref[...]).astype(MXU_DTYPE)

    return pl.pallas_call(
        body, name="sgu_fwd", grid=(s // tm,),
        in_specs=[_rows(tm, 2 * WIDTH_A), _full((HEADS_A, CHUNK, CHUNK)), _full((CHUNK, WIDTH_A)),
                  _full((1, WIDTH_A)), _full((1, WIDTH_A))],
        out_specs=_rows(tm, WIDTH_A), out_shape=jax.ShapeDtypeStruct((s, WIDTH_A), MXU_DTYPE),
        compiler_params=_params(("arbitrary",)),
    )(uv, w_tril, bias, g_sgu, g_a)


def _out_fwd(ya_n, y_b, g_b, w_out, x):
    s = x.shape[0]
    tm = 512
    nc = WIDTH_B // LANES

    def body(ya_ref, *refs):
        yb_refs = refs[:nc]
        g_ref, w_ref, x_ref, h_ref, yn_ref = refs[nc:]
        n, _ = _rms_stats(_load_l256(yb_refs, tm))
        yn = jnp.concatenate([ya_ref[...], (n * g_ref[...]).astype(MXU_DTYPE)], axis=1)
        yn_ref[...] = yn
        h_ref[...] = x_ref[...] + _dot(yn, w_ref[...])

    return pl.pallas_call(
        body, name="out_fwd", grid=(s // tm,),
        in_specs=[_rows(tm, WIDTH_A)] + _col_specs(tm, WIDTH_B) + [_full((1, WIDTH_B)), _full((D_MODEL, D_MODEL)),
                                                                 _rows(tm, D_MODEL)],
        out_specs=[_rows(tm, D_MODEL), _rows(tm, D_MODEL)],
        out_shape=[jax.ShapeDtypeStruct((s, D_MODEL), F32), jax.ShapeDtypeStruct((s, D_MODEL), MXU_DTYPE)],
        compiler_params=_params(("arbitrary",)),
    )(ya_n, *([y_b] * nc), g_b, w_out, x)


def _ffn_fwd(h1, g_ffn, w_gate, w_up, w_down):
    s = h1.shape[0]
    tm = 256

    def body(h_ref, g_ref, wg_ref, wu_ref, wd_ref, o_ref, gate_ref, up_ref, hn_ref):
        h = h_ref[...]
        n, _ = _rms_stats(h)
        hn = (n * g_ref[...]).astype(MXU_DTYPE)
        hn_ref[...] = hn
        out = h
        for c in range(2):
            cols = slice(c * FF_HALF, (c + 1) * FF_HALF)
            gate = _dot(hn, wg_ref[:, cols])
            up = _dot(hn, wu_ref[:, cols])
            gate_ref[:, cols] = gate.astype(MXU_DTYPE)
            up_ref[:, cols] = up.astype(MXU_DTYPE)
            act = (gate * jax.nn.sigmoid(gate) * up).astype(MXU_DTYPE)
            out = out + _dot(act, wd_ref[cols, :])
        o_ref[...] = out

    return pl.pallas_call(
        body, name="ffn_fwd", grid=(s // tm,),
        in_specs=[_rows(tm, D_MODEL), _full((1, D_MODEL)), _full((D_MODEL, D_FF)), _full((D_MODEL, D_FF)),
                  _full((D_FF, D_MODEL))],
        out_specs=[_rows(tm, D_MODEL), _rows(tm, D_FF), _rows(tm, D_FF), _rows(tm, D_MODEL)],
        out_shape=[jax.ShapeDtypeStruct((s, D_MODEL), F32), jax.ShapeDtypeStruct((s, D_FF), MXU_DTYPE),
                   jax.ShapeDtypeStruct((s, D_FF), MXU_DTYPE), jax.ShapeDtypeStruct((s, D_MODEL), MXU_DTYPE)],
        compiler_params=_params(("arbitrary",)),
    )(h1, g_ffn, w_gate, w_up, w_down)


def _ple_loss(h2, p, target, g_ple, w_pg, w_pg_t, w_pp, g_final):
    s = h2.shape[0]
    tm = 256

    def body(h_ref, p_ref, t_ref, gp_ref, wg_ref, wgt_ref, wp_ref, gf_ref,
             loss_ref, dh_ref, dz_ref, dpp_ref, hn_ref, dgp_ref, dgf_ref):
        @pl.when(pl.program_id(0) == 0)
        def _():
            loss_ref[...] = jnp.zeros_like(loss_ref)
            dgp_ref[...] = jnp.zeros_like(dgp_ref)
            dgf_ref[...] = jnp.zeros_like(dgf_ref)

        h2t = h_ref[...]
        n2, r2 = _rms_stats(h2t)
        hn = (n2 * gp_ref[...]).astype(MXU_DTYPE)
        hn_ref[...] = hn
        gate = jax.nn.sigmoid(_dot(hn, wg_ref[...]))
        pp = _dot(p_ref[...].astype(MXU_DTYPE), wp_ref[...])
        h3 = h2t + gate * pp
        n3, r3 = _rms_stats(h3)
        diff = n3 * gf_ref[...] - t_ref[...]
        loss_ref[...] += jnp.full(loss_ref.shape, 0.5 * jnp.sum(diff * diff) / D_MODEL, F32)
        dy = diff * (1.0 / D_MODEL)
        dgf_ref[...] += jnp.sum(dy * n3, axis=0, keepdims=True)
        dh3 = _rms_bwd(dy * gf_ref[...], n3, r3)
        dpp_ref[...] = (dh3 * gate).astype(MXU_DTYPE)
        dz = (dh3 * pp * gate * (1.0 - gate)).astype(MXU_DTYPE)
        dz_ref[...] = dz
        dhn = _dot(dz, wgt_ref[...])
        dgp_ref[...] += jnp.sum(dhn * n2, axis=0, keepdims=True)
        dh_ref[...] = dh3 + _rms_bwd(dhn * gp_ref[...], n2, r2)

    return pl.pallas_call(
        body, name="ple_loss", grid=(s // tm,),
        in_specs=[_rows(tm, D_MODEL), _rows(tm, PLE_DIM), _rows(tm, D_MODEL), _full((1, D_MODEL)),
                  _full((D_MODEL, D_MODEL)), _full((D_MODEL, D_MODEL)), _full((PLE_DIM, D_MODEL)),
                  _full((1, D_MODEL))],
        out_specs=[_full((1, LANES)), _rows(tm, D_MODEL), _rows(tm, D_MODEL), _rows(tm, D_MODEL),
                   _rows(tm, D_MODEL), _full((1, D_MODEL)), _full((1, D_MODEL))],
        out_shape=[jax.ShapeDtypeStruct((1, LANES), F32), jax.ShapeDtypeStruct((s, D_MODEL), F32),
                   jax.ShapeDtypeStruct((s, D_MODEL), MXU_DTYPE), jax.ShapeDtypeStruct((s, D_MODEL), MXU_DTYPE),
                   jax.ShapeDtypeStruct((s, D_MODEL), MXU_DTYPE), jax.ShapeDtypeStruct((1, D_MODEL), F32),
                   jax.ShapeDtypeStruct((1, D_MODEL), F32)],
        compiler_params=_params(("arbitrary",)),
    )(h2, p, target, g_ple, w_pg, w_pg_t, w_pp, g_final)


def _ffn_bwd(dh2, h1, gate, up, g_ffn, w_down_t, w_gate_t, w_up_t):
    s = h1.shape[0]
    tm = 256

    def body(dh_ref, h_ref, gate_ref, up_ref, g_ref, wdt_ref, wgt_ref, wut_ref,
             o_ref, act_ref, dg_ref, du_ref, dgn_ref):
        @pl.when(pl.program_id(0) == 0)
        def _():
            dgn_ref[...] = jnp.zeros_like(dgn_ref)

        dh = dh_ref[...]
        dhb = dh.astype(MXU_DTYPE)
        dhn = jnp.zeros((tm, D_MODEL), F32)
        for c in range(2):
            cols = slice(c * FF_HALF, (c + 1) * FF_HALF)
            dact = _dot(dhb, wdt_ref[:, cols])
            g = gate_ref[:, cols].astype(F32)
            u = up_ref[:, cols].astype(F32)
            sg = jax.nn.sigmoid(g)
            silu = g * sg
            act_ref[:, cols] = (silu * u).astype(MXU_DTYPE)
            du = (dact * silu).astype(MXU_DTYPE)
            dg = (dact * u * sg * (1.0 + g * (1.0 - sg))).astype(MXU_DTYPE)
            du_ref[:, cols] = du
            dg_ref[:, cols] = dg
            dhn = dhn + _dot(dg, wgt_ref[cols, :]) + _dot(du, wut_ref[cols, :])
        n, r = _rms_stats(h_ref[...])
        dgn_ref[...] += jnp.sum(dhn * n, axis=0, keepdims=True)
        o_ref[...] = dh + _rms_bwd(dhn * g_ref[...], n, r)

    return pl.pallas_call(
        body, name="ffn_bwd", grid=(s // tm,),
        in_specs=[_rows(tm, D_MODEL), _rows(tm, D_MODEL), _rows(tm, D_FF), _rows(tm, D_FF), _full((1, D_MODEL)),
                  _full((D_MODEL, D_FF)), _full((D_FF, D_MODEL)), _full((D_FF, D_MODEL))],
        out_specs=[_rows(tm, D_MODEL), _rows(tm, D_FF), _rows(tm, D_FF), _rows(tm, D_FF), _full((1, D_MODEL))],
        out_shape=[jax.ShapeDtypeStruct((s, D_MODEL), F32), jax.ShapeDtypeStruct((s, D_FF), MXU_DTYPE),
                   jax.ShapeDtypeStruct((s, D_FF), MXU_DTYPE), jax.ShapeDtypeStruct((s, D_FF), MXU_DTYPE),
                   jax.ShapeDtypeStruct((1, D_MODEL), F32)],
        compiler_params=_params(("arbitrary",)),
    )(dh2, h1, gate, up, g_ffn, w_down_t, w_gate_t, w_up_t)


def _out_bwd(dh1, y_b, g_b, w_out_t):
    s = dh1.shape[0]
    tm = 512
    nc = WIDTH_B // LANES

    def body(dh_ref, *refs):
        yb_refs = refs[:nc]
        g_ref, wt_ref, dya_ref, dyb_ref, dg_ref, scr = refs[nc:]

        @pl.when(pl.program_id(0) == 0)
        def _():
            dg_ref[...] = jnp.zeros_like(dg_ref)

        dy = _dot(dh_ref[...].astype(MXU_DTYPE), wt_ref[...])
        dya_ref[...] = dy[:, :WIDTH_A]
        dyb = dy[:, WIDTH_A:]
        n, r = _rms_stats(_load_l256(yb_refs, tm))
        dg_ref[...] += jnp.sum(dyb * n, axis=0, keepdims=True)
        dyb_in = _rms_bwd(dyb * g_ref[...], n, r)
        for j in range(nc):
            cols = slice(j * LANES, (j + 1) * LANES)
            _store_l256(scr, dyb_ref, cols, dyb_in[:, cols])

    return pl.pallas_call(
        body, name="out_bwd", grid=(s // tm,), scratch_shapes=[pltpu.VMEM((tm, LANES), F32)],
        in_specs=[_rows(tm, D_MODEL)] + _col_specs(tm, WIDTH_B) + [_full((1, WIDTH_B)), _full((D_MODEL, D_MODEL))],
        out_specs=[_rows(tm, WIDTH_A), _rows(tm, WIDTH_B), _full((1, WIDTH_B))],
        out_shape=[jax.ShapeDtypeStruct((s, WIDTH_A), F32), jax.ShapeDtypeStruct((s, WIDTH_B), F32),
                   jax.ShapeDtypeStruct((1, WIDTH_B), F32)],
        compiler_params=_params(("arbitrary",)),
    )(dh1, *([y_b] * nc), g_b, w_out_t)


def _attn_bwd_branch(q, k, v, do, o, lse, grads, dil):
    s = q.shape[0]
    br = _Branch(dil, s)
    qn, nb = br.qn, br.nb
    first = grads is None

    def body(*refs):
        bias_ref, q_ref, kc_ref, kp_ref, vc_ref, vp_ref, do_ref, o_ref, lse_ref = refs[:9]
        if first:
            rest = refs[9:]
        else:
            dq_in, dk_in, dv_in = refs[9:12]
            rest = refs[12:]
        dq_ref, dk_ref, dv_ref, dk_carry, dv_carry = rest
        n = pl.program_id(1)

        @pl.when(n == 0)
        def _():
            dk_carry[...] = jnp.zeros_like(dk_carry)
            dv_carry[...] = jnp.zeros_like(dv_carry)

        @pl.when(n < nb)
        def _():
            bias2 = jnp.concatenate([bias_ref[0], bias_ref[0]], axis=0)
            lane = lax.broadcasted_iota(jnp.int32, (qn, LANES), 1)
            lo = lane < HEAD_DIM
            mask_f = lo.astype(F32)
            mask_lo = mask_f.astype(MXU_DTYPE)
            lse = br.load(lse_ref)
            for hp in range(HEADS_B // 2):
                cols = slice(hp * LANES, (hp + 1) * LANES)
                h0, h1 = 2 * hp, 2 * hp + 1
                qp = br.load(q_ref, cols)
                kcat = jnp.concatenate([br.load(kp_ref, cols), br.load(kc_ref, cols)], axis=0)
                vcat = jnp.concatenate([br.load(vp_ref, cols), br.load(vc_ref, cols)], axis=0)
                dop = br.load(do_ref, cols)
                prod = dop * br.load(o_ref, cols)
                prod_lo = prod * mask_f
                delta = jnp.concatenate([jnp.sum(prod_lo, axis=1, keepdims=True),
                                         jnp.sum(prod - prod_lo, axis=1, keepdims=True)], axis=0)
                qs = jnp.concatenate([qp * mask_lo, qp * (1 - mask_lo)], axis=0)
                dos = jnp.concatenate([dop * mask_f, dop * (1.0 - mask_f)], axis=0).astype(MXU_DTYPE)
                lse2 = jnp.concatenate([lse[:, h0:h0 + 1], lse[:, h1:h1 + 1]], axis=0)
                p = jnp.exp(_dot_nt(qs, kcat) + bias2 - lse2)
                ds = (p * (_dot_nt(dos, vcat) - delta)).astype(MXU_DTYPE)
                dvc = _dot_tn(p.astype(MXU_DTYPE), dos)
                dkc = _dot_tn(ds, qs)
                dq2 = _dot(ds, kcat)
                dq = jnp.where(lo, dq2[:qn], dq2[qn:])
                dk_prev = dk_carry[:, cols] + dkc[:qn]
                dv_prev = dv_carry[:, cols] + dvc[:qn]
                if not first:
                    dq = dq + br.load(dq_in, cols)
                    dk_prev = dk_prev + br.load(dk_in, cols)
                    dv_prev = dv_prev + br.load(dv_in, cols)
                br.store(dq_ref, cols, dq)
                br.store(dk_ref, cols, dk_prev)
                br.store(dv_ref, cols, dv_prev)
                dk_carry[:, cols] = dkc[qn:]
                dv_carry[:, cols] = dvc[qn:]

        @pl.when(n == nb)
        def _():
            dk_last = dk_carry[...]
            dv_last = dv_carry[...]
            if not first:
                dk_last = dk_last + br.load(dk_in)
                dv_last = dv_last + br.load(dv_in)
            br.store(dk_ref, slice(None), dk_last)
            br.store(dv_ref, slice(None), dv_last)

    cur = lambda n: jnp.minimum(n, nb - 1)
    before = lambda n: jnp.maximum(cur(n) - 1, 0)
    late = lambda n: jnp.maximum(n - 1, 0)
    in_specs = [br.bias_spec(cur), br.spec(WIDTH_B, cur), br.spec(WIDTH_B, cur), br.spec(WIDTH_B, before),
                br.spec(WIDTH_B, cur), br.spec(WIDTH_B, before), br.spec(WIDTH_B, cur), br.spec(WIDTH_B, cur),
                br.spec(LANES, cur)]
    args = [jnp.asarray(br.bias)] + [br.view(a) for a in (q, k, k, v, v, do, o, lse)]
    if not first:
        in_specs += [br.spec(WIDTH_B, cur), br.spec(WIDTH_B, late), br.spec(WIDTH_B, late)]
        args += [br.view(g) for g in grads]
    res = pl.pallas_call(
        body, name="attn_bwd_d%d" % dil, grid=(br.grid[0], nb + 1), in_specs=in_specs,
        out_specs=[br.spec(WIDTH_B, cur), br.spec(WIDTH_B, late), br.spec(WIDTH_B, late)],
        out_shape=[jax.ShapeDtypeStruct((s // L_BLOCK, 4, 4, L_GROUP, WIDTH_B), F32)] * 3,
        scratch_shapes=[pltpu.VMEM((qn, WIDTH_B), F32), pltpu.VMEM((qn, WIDTH_B), F32)],
        compiler_params=_params(("arbitrary", "arbitrary")),
    )(*args)
    return tuple(a.reshape(s, WIDTH_B) for a in res)


def _sgu_bwd(uv, dya_n, w_tril, w_tril_t, bias, g_sgu, g_a):
    s = uv.shape[0]
    tm = 512

    def body(uv_ref, dy_ref, w_ref, wt_ref, b_ref, gs_ref, ga_ref, duv_ref, dw_ref, db_ref, dgs_ref, dga_ref,
             db_acc):
        i = pl.program_id(0)

        @pl.when(i == 0)
        def _():
            dw_ref[...] = jnp.zeros_like(dw_ref)
            dgs_ref[...] = jnp.zeros_like(dgs_ref)
            dga_ref[...] = jnp.zeros_like(dga_ref)
            db_acc[...] = jnp.zeros_like(db_acc)

        t = _sgu_forward_tile(uv_ref[...], w_ref, b_ref[...], gs_ref[...])
        na, ra = _rms_stats(t['ya'])
        dyn = dy_ref[...]
        dga_ref[...] += jnp.sum(dyn * na, axis=0, keepdims=True)
        dya = _rms_bwd(dyn * ga_ref[...], na, ra)
        dug = dya * t['mixed']
        dmixed = dya * t['ug']
        dmb = dmixed.astype(MXU_DTYPE)
        masks = _half_masks(MXU_DTYPE)
        chunks = []
        db = jnp.zeros((CHUNK, WIDTH_A), F32)
        for c in range(tm // CHUNK):
            rows = slice(c * CHUNK, (c + 1) * CHUNK)
            db = db + dmixed[rows]
            groups = []
            for gp in range(2):
                cols = slice(gp * LANES, (gp + 1) * LANES)
                dm_g = dmb[rows, cols]
                vn_g = t['vn'][rows, cols]
                dvn_g = jnp.zeros((CHUNK, LANES), F32)
                for j in range(2):
                    dm_h = dm_g * masks[j]
                    dvn_g = dvn_g + _dot(wt_ref[2 * gp + j], dm_h)
                    dw_ref[2 * gp + j] += _dot_nt(dm_h, vn_g)
                groups.append(dvn_g)
            chunks.append(jnp.concatenate(groups, axis=1))
        db_acc[...] += db
        dvn = jnp.concatenate(chunks, axis=0)
        vhat = t['vhat']
        dgs_ref[...] += jnp.sum(dvn * vhat, axis=0, keepdims=True)
        dvh = dvn * gs_ref[...]
        dvg = t['rs'] * (dvh - jnp.mean(dvh, axis=-1, keepdims=True)
                         - vhat * jnp.mean(dvh * vhat, axis=-1, keepdims=True))
        duv_ref[:, :WIDTH_A] = (dug * _gelu_grad(t['u'], t['tu'])).astype(MXU_DTYPE)
        duv_ref[:, WIDTH_A:] = (dvg * _gelu_grad(t['v'], t['tv'])).astype(MXU_DTYPE)

        @pl.when(i == pl.num_programs(0) - 1)
        def _():
            lane_a = lax.broadcasted_iota(jnp.int32, (CHUNK, WIDTH_A), 1)
            lane = lax.broadcasted_iota(jnp.int32, (CHUNK, LANES), 1)
            acc = db_acc[...]
            out = jnp.zeros((CHUNK, LANES), F32)
            for h in range(HEADS_A):
                col = jnp.sum(jnp.where(lane_a // HEAD_DIM == h, acc, 0.0), axis=1, keepdims=True)
                out = jnp.where(lane == h, col, out)
            db_ref[...] = out
            causal = (lax.broadcasted_iota(jnp.int32, (CHUNK, CHUNK), 0)
                      >= lax.broadcasted_iota(jnp.int32, (CHUNK, CHUNK), 1))
            for h in range(HEADS_A):
                dw_ref[h] = jnp.where(causal, dw_ref[h], 0.0)

    return pl.pallas_call(
        body, name="sgu_bwd", grid=(s // tm,),
        in_specs=[_rows(tm, 2 * WIDTH_A), _rows(tm, WIDTH_A), _full((HEADS_A, CHUNK, CHUNK)),
                  _full((HEADS_A, CHUNK, CHUNK)), _full((CHUNK, WIDTH_A)), _full((1, WIDTH_A)),
                  _full((1, WIDTH_A))],
        out_specs=[_rows(tm, 2 * WIDTH_A), _full((HEADS_A, CHUNK, CHUNK)), _full((CHUNK, LANES)),
                   _full((1, WIDTH_A)), _full((1, WIDTH_A))],
        out_shape=[jax.ShapeDtypeStruct((s, 2 * WIDTH_A), MXU_DTYPE),
                   jax.ShapeDtypeStruct((HEADS_A, CHUNK, CHUNK), F32), jax.ShapeDtypeStruct((CHUNK, LANES), F32),
                   jax.ShapeDtypeStruct((1, WIDTH_A), F32), jax.ShapeDtypeStruct((1, WIDTH_A), F32)],
        scratch_shapes=[pltpu.VMEM((CHUNK, WIDTH_A), F32)],
        compiler_params=_params(("arbitrary",)),
    )(uv, dya_n, w_tril, w_tril_t, bias, g_sgu, g_a)


def _in_bwd_proj(duv, dq, dk, dv, cos_t, sin_t):
    s = duv.shape[0]
    tm = 512
    nc = WIDTH_B // LANES

    def body(duv_ref, *refs):
        dq_refs, dk_refs, dv_refs = refs[:nc], refs[nc:2 * nc], refs[2 * nc:3 * nc]
        cos_ref, sin_ref, dp_ref = refs[3 * nc:]
        cos = cos_ref[...]
        sin = sin_ref[...]
        dp_ref[:, :2 * WIDTH_A] = duv_ref[...]
        for i in range(nc):
            lo = 2 * WIDTH_A + i * LANES
            tq = _load_l256(dq_refs[i:i + 1], tm) * (HEAD_DIM ** -0.5)
            tk = _load_l256(dk_refs[i:i + 1], tm)
            dp_ref[:, lo:lo + LANES] = (tq * cos + _rope_partner(tq * sin)).astype(MXU_DTYPE)
            dp_ref[:, lo + WIDTH_B:lo + WIDTH_B + LANES] = (tk * cos + _rope_partner(tk * sin)).astype(MXU_DTYPE)
            dp_ref[:, lo + 2 * WIDTH_B:lo + 2 * WIDTH_B + LANES] = _load_l256(dv_refs[i:i + 1], tm).astype(MXU_DTYPE)

    return pl.pallas_call(
        body, name="in_bwd_proj", grid=(s // tm,),
        in_specs=[_rows(tm, 2 * WIDTH_A)] + 3 * _col_specs(tm, WIDTH_B) + [_rows(tm, LANES), _rows(tm, LANES)],
        out_specs=_rows(tm, IN_COLS), out_shape=jax.ShapeDtypeStruct((s, IN_COLS), MXU_DTYPE),
        compiler_params=_params(("arbitrary",)),
    )(duv, *([dq] * nc), *([dk] * nc), *([dv] * nc), cos_t, sin_t)


def _in_bwd_x(dproj, w_in_t, x, g_mix, dh1):
    s = x.shape[0]
    tm = 512

    def body(dp_ref, wt_ref, x_ref, g_ref, dh_ref, gx_ref, dg_ref):
        @pl.when(pl.program_id(0) == 0)
        def _():
            dg_ref[...] = jnp.zeros_like(dg_ref)

        dhn = _dot(dp_ref[...], wt_ref[...])
        n, r = _rms_stats(x_ref[...])
        dg_ref[...] += jnp.sum(dhn * n, axis=0, keepdims=True)
        gx_ref[...] = dh_ref[...] + _rms_bwd(dhn * g_ref[...], n, r)

    return pl.pallas_call(
        body, name="in_bwd_x", grid=(s // tm,),
        in_specs=[_rows(tm, IN_COLS), _full((IN_COLS, D_MODEL)), _rows(tm, D_MODEL), _full((1, D_MODEL)),
                  _rows(tm, D_MODEL)],
        out_specs=[_rows(tm, D_MODEL), _full((1, D_MODEL))],
        out_shape=[jax.ShapeDtypeStruct((s, D_MODEL), F32), jax.ShapeDtypeStruct((1, D_MODEL), F32)],
        compiler_params=_params(("arbitrary",)),
    )(dproj, w_in_t, x, g_mix, dh1)


def _wgrad(a, bs, name, transposed=False):
    s, m = a.shape
    bm = 512 if m % 512 == 0 else (FF_HALF if m == D_FF else m)
    ts = 512
    nsteps = s // ts
    nb = len(bs)

    def body(a_ref, *refs):
        b_refs, o_refs, accs = refs[:nb], refs[nb:2 * nb], refs[2 * nb:]
        kk = pl.program_id(1)
        at = a_ref[...].astype(MXU_DTYPE)
        for b_ref, o_ref, acc in zip(b_refs, o_refs, accs):
            c = _dot_tn(at, b_ref[...].astype(MXU_DTYPE))

            @pl.when(kk == 0)
            def _():
                acc[...] = c

            @pl.when(kk > 0)
            def _():
                acc[...] += c

            @pl.when(kk == nsteps - 1)
            def _():
                total = acc[...]
                o_ref[...] = (total.T if transposed else total).astype(o_ref.dtype)

    if transposed:
        out_specs = [pl.BlockSpec((b.shape[1], bm), lambda i, kk: (0, i)) for b in bs]
        out_shape = [jax.ShapeDtypeStruct((b.shape[1], m), jnp.bfloat16) for b in bs]
    else:
        out_specs = [pl.BlockSpec((bm, b.shape[1]), lambda i, kk: (i, 0)) for b in bs]
        out_shape = [jax.ShapeDtypeStruct((m, b.shape[1]), jnp.bfloat16) for b in bs]
    return pl.pallas_call(
        body, name=name, grid=(m // bm, nsteps),
        in_specs=[pl.BlockSpec((ts, bm), lambda i, kk: (kk, i))]
        + [pl.BlockSpec((ts, b.shape[1]), lambda i, kk: (kk, 0)) for b in bs],
        out_specs=out_specs, out_shape=out_shape,
        scratch_shapes=[pltpu.VMEM((bm, b.shape[1]), F32) for b in bs],
        compiler_params=_params(("arbitrary", "arbitrary")),
    )(a, *bs)


def _rope_tables(s):
    half = HEAD_DIM // 2
    inv = ROPE_THETA ** (-jnp.arange(half, dtype=F32) / half)
    ang = jnp.arange(s, dtype=F32)[:, None] * inv[None, :]
    cos = jnp.cos(ang)
    sin = jnp.sin(ang)
    cos_t = jnp.concatenate([cos, cos, cos, cos], axis=1)
    sin_t = jnp.concatenate([-sin, sin, -sin, sin], axis=1)
    return cos_t, sin_t


MESH = pl.DeviceIdType.MESH
ANY = pl.BlockSpec(memory_space=pl.ANY)
SEM = pl.BlockSpec(memory_space=pltpu.SEMAPHORE)
SPLIT_COPY = pltpu.CompilerParams(has_side_effects=pltpu.SideEffectType.DATAFLOW_SIDE_EFFECTING)
SLAB_IS_TRANSPOSED = {'w_in': True, 'w_out': False, 'w_gate': True, 'w_up': True, 'w_down': False,
                      'w_ple_gate': False, 'w_ple_proj': True}


def _place():
    x, y, c = lax.axis_index("x"), lax.axis_index("y"), lax.axis_index("c")
    other_chips = [(1 - x, y), (x, 1 - y), (1 - x, 1 - y)]
    return x, y, c, other_chips


def _chip_of(chip):
    return 2 * chip[0] + chip[1]


def _half(ref, lead, hc):
    hr = ref.shape[1] // 2
    return ref.at[lead, pl.ds(hc * hr, hr), :]


def _put_own(stack, own, index):
    return lax.dynamic_update_slice(stack, own[None], (index,) + (0,) * own.ndim)


def _all_gather_now(slab):
    rows, cols = slab.shape

    def body(x_ref, out_ref, send_sems, recv_sems):
        x, y, c, chips = _place()
        sibling = (x, y, 1 - c)
        hr = rows // 2

        def copy(k, src, dst, to):
            return pltpu.make_async_remote_copy(src_ref=src, dst_ref=dst, send_sem=send_sems.at[k],
                                                recv_sem=recv_sems.at[k], device_id=to, device_id_type=MESH)

        my_half = x_ref.at[pl.ds(c * hr, hr), :]
        first = [copy(j, my_half, _half(out_ref, 2 * x + y, c), (*chip, c)) for j, chip in enumerate(chips)]
        for cp in first:
            cp.start()
        passed = [copy(3 + j, _half(out_ref, _chip_of(chip), c), _half(out_ref, _chip_of(chip), c), sibling)
                  for j, chip in enumerate(chips)]
        for j, chip in enumerate(chips):
            copy(j, my_half, _half(out_ref, _chip_of(chip), c), (*chip, c)).wait_recv()
            passed[j].start()
        for j, chip in enumerate(chips):
            copy(3 + j, my_half, _half(out_ref, _chip_of(chip), 1 - c), sibling).wait_recv()
        for cp in first + passed:
            cp.wait_send()

    gathered = pl.pallas_call(
        body, name="all_gather_now", out_shape=jax.ShapeDtypeStruct((N_CHIPS, rows, cols), slab.dtype),
        in_specs=[ANY], out_specs=ANY,
        scratch_shapes=[pltpu.SemaphoreType.DMA((6,)), pltpu.SemaphoreType.DMA((6,))],
    )(slab)
    me = 2 * lax.axis_index("x") + lax.axis_index("y")
    return _put_own(gathered, slab, me).reshape(N_CHIPS * rows, cols)


def _gather_copies(slab_refs, land_refs, send_sems, recv_sems):
    x, y, c, chips = _place()
    sends, recvs = [], []
    for k, (src, land) in enumerate(zip(slab_refs, land_refs)):
        hr = src.shape[0] // 2
        for j, chip in enumerate(chips):
            for t in range(2):
                sends.append(pltpu.make_async_remote_copy(
                    src_ref=src.at[pl.ds(c * hr, hr), :], dst_ref=_half(land, 2 * x + y, c),
                    send_sem=send_sems.at[6 * k + 2 * j + t], recv_sem=recv_sems.at[6 * k + 2 * j + c],
                    device_id=(*chip, t), device_id_type=MESH))
                recvs.append(pltpu.make_async_remote_copy(
                    src_ref=src.at[pl.ds(t * hr, hr), :], dst_ref=_half(land, _chip_of(chip), t),
                    send_sem=send_sems.at[6 * k + 2 * j + t], recv_sem=recv_sems.at[6 * k + 2 * j + t],
                    device_id=(*chip, t), device_id_type=MESH))
    return sends, recvs


def _all_gather_start(slabs, after):
    n = len(slabs)

    def body(*refs):
        slab_refs, land_refs = refs[:n], refs[n:2 * n]
        send_sems, recv_sems = refs[2 * n + 1:2 * n + 3]
        token = refs[-1]
        sends, _ = _gather_copies(slab_refs, land_refs, send_sems, recv_sems)
        for cp in sends:
            cp.start()
        token[...] = jnp.zeros_like(token)

    lands = [lax.empty((N_CHIPS,) + s.shape, s.dtype) for s in slabs]
    hbm = lambda a: pltpu.HBM(a.shape, a.dtype)
    res = pl.pallas_call(
        body, name="all_gather_start",
        out_shape=(pltpu.SemaphoreType.DMA((6 * n,)), pltpu.SemaphoreType.DMA((6 * n,)), *map(hbm, slabs),
                   *map(hbm, lands), jax.ShapeDtypeStruct((8, LANES), F32)),
        in_specs=[ANY] * (2 * n + 1),
        out_specs=(SEM, SEM, *([ANY] * (2 * n)), pl.BlockSpec(memory_space=pltpu.VMEM)),
        input_output_aliases={i: 2 + i for i in range(2 * n)}, compiler_params=SPLIT_COPY,
    )(*[pltpu.with_memory_space_constraint(a, pltpu.HBM) for a in list(slabs) + lands], after)
    return res[:-1], res[-1]


def _all_gather_wait(handle, after):
    send_sems, recv_sems = handle[:2]
    n = (len(handle) - 2) // 2
    slabs, lands = handle[2:2 + n], handle[2 + n:]

    def body(*refs):
        slab_refs, land_refs = refs[:n], refs[n:2 * n]
        send_sems, recv_sems = refs[2 * n:2 * n + 2]
        sends, recvs = _gather_copies(slab_refs, land_refs, send_sems, recv_sems)
        for cp in sends:
            cp.wait_send()
        for cp in recvs:
            cp.wait_recv()

    hbm = lambda a: pltpu.HBM(a.shape, a.dtype)
    res = pl.pallas_call(
        body, name="all_gather_wait", out_shape=tuple(map(hbm, list(slabs) + list(lands))),
        in_specs=[ANY] * (2 * n) + [SEM, SEM, ANY], out_specs=tuple([ANY] * (2 * n)),
        input_output_aliases={i: i for i in range(2 * n)}, compiler_params=SPLIT_COPY,
    )(*slabs, *lands, send_sems, recv_sems, after)
    me = 2 * lax.axis_index("x") + lax.axis_index("y")
    return [_put_own(land, slab, me).reshape(N_CHIPS * slab.shape[0], slab.shape[1])
            for slab, land in zip(res[:n], res[n:])]


def _scatter_copies(part_refs, land_refs, send_sems, recv_sems):
    x, y, c, chips = _place()
    me = 4 * x + 2 * y + c
    sends, recvs = [], []
    for k, (part, land) in enumerate(zip(part_refs, land_refs)):
        for j, chip in enumerate(chips):
            for h in range(2):
                sends.append(pltpu.make_async_remote_copy(
                    src_ref=_half(part, _chip_of(chip), h), dst_ref=land.at[me],
                    send_sem=send_sems.at[7 * k + 2 * j + h], recv_sem=recv_sems.at[7 * k + 2 * j + c],
                    device_id=(*chip, h), device_id_type=MESH))
                recvs.append(pltpu.make_async_remote_copy(
                    src_ref=_half(part, _chip_of(chip), h), dst_ref=land.at[2 * _chip_of(chip) + h],
                    send_sem=send_sems.at[7 * k + 2 * j + h], recv_sem=recv_sems.at[7 * k + 2 * j + h],
                    device_id=(*chip, h), device_id_type=MESH))
        sends.append(pltpu.make_async_remote_copy(
            src_ref=_half(part, 2 * x + y, 1 - c), dst_ref=land.at[me], send_sem=send_sems.at[7 * k + 6],
            recv_sem=recv_sems.at[7 * k + 6], device_id=(x, y, 1 - c), device_id_type=MESH))
        recvs.append(pltpu.make_async_remote_copy(
            src_ref=_half(part, 2 * x + y, 1 - c), dst_ref=land.at[4 * x + 2 * y + 1 - c],
            send_sem=send_sems.at[7 * k + 6], recv_sem=recv_sems.at[7 * k + 6], device_id=(x, y, 1 - c),
            device_id_type=MESH))
    return sends, recvs


def _reduce_scatter_start(parts, name):
    n = len(parts)
    parts = [p.reshape(N_CHIPS, p.shape[0] // N_CHIPS, p.shape[1]) for p in parts]

    def body(*refs):
        part_refs, land_refs = refs[:n], refs[n:2 * n]
        send_sems, recv_sems = refs[2 * n:2 * n + 2]
        token = refs[-1]
        sends, _ = _scatter_copies(part_refs, land_refs, send_sems, recv_sems)
        for cp in sends:
            cp.start()
        token[...] = jnp.zeros_like(token)

    lands = [lax.empty((N_DEV, p.shape[1] // 2, p.shape[2]), p.dtype) for p in parts]
    hbm = lambda a: pltpu.HBM(a.shape, a.dtype)
    res = pl.pallas_call(
        body, name=name,
        out_shape=(pltpu.SemaphoreType.DMA((7 * n,)), pltpu.SemaphoreType.DMA((7 * n,)), *map(hbm, parts),
                   *map(hbm, lands), jax.ShapeDtypeStruct((8, LANES), F32)),
        in_specs=[ANY] * (2 * n), out_specs=(SEM, SEM, *([ANY] * (2 * n)), pl.BlockSpec(memory_space=pltpu.VMEM)),
        input_output_aliases={i: 2 + i for i in range(2 * n)}, compiler_params=SPLIT_COPY,
    )(*[pltpu.with_memory_space_constraint(a, pltpu.HBM) for a in parts + lands])
    return res[:-1], res[-1]


def _reduce_scatter_wait(handle, after, name):
    send_sems, recv_sems = handle[:2]
    n = (len(handle) - 2) // 2
    parts, lands = handle[2:2 + n], handle[2 + n:]

    def body(*refs):
        part_refs, land_refs = refs[:n], refs[n:2 * n]
        send_sems, recv_sems = refs[2 * n:2 * n + 2]
        sends, recvs = _scatter_copies(part_refs, land_refs, send_sems, recv_sems)
        for cp in sends:
            cp.wait_send()
        for cp in recvs:
            cp.wait_recv()

    hbm = lambda a: pltpu.HBM(a.shape, a.dtype)
    res = pl.pallas_call(
        body, name=name, out_shape=tuple(map(hbm, list(parts) + list(lands))),
        in_specs=[ANY] * (2 * n) + [SEM, SEM, ANY], out_specs=tuple([ANY] * (2 * n)),
        input_output_aliases={i: i for i in range(2 * n)}, compiler_params=SPLIT_COPY,
    )(*parts, *lands, send_sems, recv_sems, after)
    x, y, c = lax.axis_index("x"), lax.axis_index("y"), lax.axis_index("c")
    out = []
    for part, land in zip(res[:n], res[n:]):
        hr = land.shape[1]
        own = lax.dynamic_slice(part, (2 * x + y, c * hr, 0), (1, hr, part.shape[2]))[0]
        out.append(_put_own(land, own, 4 * x + 2 * y + c))
    return out


def _sum_shares(land, name):
    n, rows, cols = land.shape

    def body(l_ref, o_ref):
        acc = l_ref[0].astype(F32)
        for i in range(1, n):
            acc = acc + l_ref[i].astype(F32)
        o_ref[...] = acc

    return pl.pallas_call(
        body, name=name, grid=(1,), in_specs=[pl.BlockSpec((n, rows, cols), lambda i: (0, 0, 0))],
        out_specs=pl.BlockSpec((rows, cols), lambda i: (0, 0)), out_shape=jax.ShapeDtypeStruct((rows, cols), F32),
        compiler_params=_params(("arbitrary",)),
    )(land)


def _swap_halves(halves, name):
    n = len(halves)

    def body(*refs):
        in_refs, out_refs, send_sems, recv_sems = refs[:n], refs[n:2 * n], refs[2 * n], refs[2 * n + 1]
        x, y, c, _ = _place()
        copies = [pltpu.make_async_remote_copy(src_ref=i_ref, dst_ref=o_ref, send_sem=send_sems.at[k],
                                               recv_sem=recv_sems.at[k], device_id=(x, y, 1 - c), device_id_type=MESH)
                  for k, (i_ref, o_ref) in enumerate(zip(in_refs, out_refs))]
        for cp in copies:
            cp.start()
        for cp in copies:
            cp.wait()

    others = pl.pallas_call(
        body, name=name, out_shape=[jax.ShapeDtypeStruct(h.shape, h.dtype) for h in halves],
        in_specs=[ANY] * n, out_specs=[ANY] * n,
        scratch_shapes=[pltpu.SemaphoreType.DMA((n,)), pltpu.SemaphoreType.DMA((n,))],
    )(*halves)
    south = lax.axis_index("c") == 0
    return [jnp.concatenate([jnp.where(south, h, o), jnp.where(south, o, h)], axis=0) for h, o in zip(halves, others)]


def _small_all_reduce(block):
    rows = block.shape[0]

    def body(x_ref, all_ref, sum_ref, send_sems, recv_sems, local_sem):
        x, y, c, chips = _place()
        me, sibling = (x, y, c), (x, y, 1 - c)

        def blk(px, py, pc):
            return all_ref.at[pl.ds((4 * px + 2 * py + pc) * rows, rows), :]

        def copy(k, who, to, src=None):
            return pltpu.make_async_remote_copy(
                src_ref=blk(*who) if src is None else src, dst_ref=blk(*who), send_sem=send_sems.at[k],
                recv_sem=recv_sems.at[k], device_id=to, device_id_type=MESH)

        mine = pltpu.make_async_copy(x_ref, blk(*me), local_sem)
        mine.start()
        first = [copy(0, me, sibling, src=x_ref)]
        first += [copy(1 + j, me, (*chip, c), src=x_ref) for j, chip in enumerate(chips)]
        for cp in first:
            cp.start()
        passed = [copy(4 + j, (*chip, c), sibling) for j, chip in enumerate(chips)]
        for j, chip in enumerate(chips):
            copy(1 + j, (*chip, c), me).wait_recv()
            passed[j].start()
        copy(0, sibling, me).wait_recv()
        for j, chip in enumerate(chips):
            copy(4 + j, (*chip, 1 - c), me).wait_recv()
        for cp in first + passed:
            cp.wait_send()
        mine.wait()
        acc = all_ref[pl.ds(0, rows), :]
        for dev in range(1, N_DEV):
            acc = acc + all_ref[pl.ds(dev * rows, rows), :]
        sum_ref[...] = acc

    vmem = pl.BlockSpec(memory_space=pltpu.VMEM)
    return pl.pallas_call(
        body, name="small_all_reduce",
        out_shape=[jax.ShapeDtypeStruct((N_DEV * rows, D_MODEL), F32), jax.ShapeDtypeStruct((rows, D_MODEL), F32)],
        in_specs=[vmem], out_specs=[vmem, vmem],
        scratch_shapes=[pltpu.SemaphoreType.DMA((7,)), pltpu.SemaphoreType.DMA((7,)), pltpu.SemaphoreType.DMA],
    )(block)[1]


def _adamw(w, g, m, v, name):
    rows, cols = w.shape
    tm = rows
    if rows > 512:
        tm = next(t for t in range(512, 7, -8) if rows % t == 0)

    def body(w_ref, g_ref, m_ref, v_ref, d_ref, nm_ref, nv_ref):
        g_ = g_ref[...]
        m_ = ADAM_B1 * m_ref[...] + (1.0 - ADAM_B1) * g_
        v_ = ADAM_B2 * v_ref[...] + (1.0 - ADAM_B2) * (g_ * g_)
        m_hat = m_ / (1.0 - ADAM_B1 ** ADAM_STEP)
        v_hat = v_ / (1.0 - ADAM_B2 ** ADAM_STEP)
        d_ref[...] = -ADAM_LR * (m_hat / (jnp.sqrt(v_hat) + ADAM_EPS) + ADAM_WD * w_ref[...])
        nm_ref[...] = m_
        nv_ref[...] = v_

    spec = pl.BlockSpec((tm, cols), lambda i: (i, 0))
    return pl.pallas_call(
        body, name=name, grid=(rows // tm,), in_specs=[spec] * 4, out_specs=[spec] * 3,
        out_shape=[jax.ShapeDtypeStruct(w.shape, F32)] * 3, compiler_params=_params(("arbitrary",)),
    )(w, g, m, v)


def _pack_small(values):
    flat = jnp.concatenate([values[n].reshape(-1).astype(F32) for n in SMALL])
    return jnp.pad(flat, (0, SMALL_ROWS * D_MODEL - flat.shape[0])).reshape(SMALL_ROWS, D_MODEL)


def _unpack_small(block, shapes):
    flat = block.reshape(-1)
    out, lo = {}, 0
    for n in SMALL:
        out[n] = flat[lo:lo + SMALL_SIZES[n]].reshape(shapes[n])
        lo += SMALL_SIZES[n]
    return out


def _after(token, a):
    return a + token[:1, :1].astype(a.dtype)


def kernel(x, p, mix_norm_g, w_in, sgu_w, sgu_b, sgu_norm_g, out_norm_a, out_norm_b, w_out, ffn_norm_g, w_gate, w_up, w_down, ple_norm_g, w_ple_gate, w_ple_proj, final_norm_g, loss_target, m_mix_norm_g, m_w_in, m_sgu_w, m_sgu_b, m_sgu_norm_g, m_out_norm_a, m_out_norm_b, m_w_out, m_ffn_norm_g, m_w_gate, m_w_up, m_w_down, m_ple_norm_g, m_w_ple_gate, m_w_ple_proj, m_final_norm_g, v_mix_norm_g, v_w_in, v_sgu_w, v_sgu_b, v_sgu_norm_g, v_out_norm_a, v_out_norm_b, v_w_out, v_ffn_norm_g, v_w_gate, v_w_up, v_w_down, v_ple_norm_g, v_w_ple_gate, v_w_ple_proj, v_final_norm_g):
    given = dict(locals())
    drop_lead = lambda a, lead: a.reshape(a.shape[lead:])
    xs, ps, target = drop_lead(x, 1), drop_lead(p, 2), drop_lead(loss_target, 1)
    s = xs.shape[0]
    shard = lambda name: drop_lead(given[name], 1)

    def slab_of(name):
        local = shard(name).astype(MXU_DTYPE)
        return local.T if SLAB_IS_TRANSPOSED[name] else local

    w_in_t = _all_gather_now(slab_of('w_in'))
    later = ['w_out', 'w_gate', 'w_up', 'w_down', 'w_ple_gate', 'w_ple_proj']
    gather, token = _all_gather_start([slab_of(n) for n in later], w_in_t)

    cos_t, sin_t = _rope_tables(s)
    tril = jnp.tril(jnp.ones((CHUNK, CHUNK), F32))
    w_tril = (sgu_w.reshape(HEADS_A, CHUNK, CHUNK) * tril).astype(MXU_DTYPE)
    w_tril_t = jnp.swapaxes(w_tril, 1, 2)
    bias = jnp.repeat(sgu_b.reshape(HEADS_A, CHUNK).T, HEAD_DIM, axis=1)
    g = {n: given[n].reshape(1, -1) for n in SMALL if n not in ('sgu_w', 'sgu_b')}

    uv, q, k, v, hn1 = _in_fwd(xs, _after(token, g['mix_norm_g']), w_in_t.T, cos_t, sin_t)
    ya_n = _sgu_fwd(uv, w_tril, bias, g['sgu_norm_g'], g['out_norm_a'])
    state = None
    for i, dil in enumerate(DILATIONS):
        state = _attn_fwd_branch(q, k, v, state, dil, last=(i == len(DILATIONS) - 1))
    y_b, lse = state
    stacks = dict(zip(later, _all_gather_wait(gather, lse)))
    w_gate_t, w_up_t, w_pp_t = stacks['w_gate'], stacks['w_up'], stacks['w_ple_proj']
    h1, y_n = _out_fwd(ya_n, y_b, g['out_norm_b'], stacks['w_out'], xs)
    h2, gate, up, hn2 = _ffn_fwd(h1, g['ffn_norm_g'], w_gate_t.T, w_up_t.T, stacks['w_down'])
    loss, dh2, dz, dpp, hn3, d_ple_g, d_final_g = _ple_loss(
        h2, ps, target, g['ple_norm_g'], stacks['w_ple_gate'], stacks['w_ple_gate'].T, w_pp_t.T, g['final_norm_g'])

    share = {}
    share['w_ple_gate'], = _wgrad(hn3, [dz], "wgrad_ple_gate")
    share['w_ple_proj'], = _wgrad(ps, [dpp], "wgrad_ple_proj", transposed=True)
    scatter_1, token = _reduce_scatter_start([share['w_ple_gate'], share['w_ple_proj']], "reduce_scatter_start_1")
    dh1, act, dgate, dup, d_ffn_g = _ffn_bwd(dh2, h1, gate, up, _after(token, g['ffn_norm_g']), stacks['w_down'].T,
                                             w_gate_t, w_up_t)
    share['w_down'], = _wgrad(act, [dh2], "wgrad_down")
    share['w_gate'], share['w_up'] = _wgrad(hn2, [dgate, dup], "wgrad_gate_up", transposed=True)
    scatter_2, token = _reduce_scatter_start([share['w_down'], share['w_gate'], share['w_up']],
                                             "reduce_scatter_start_2")
    dya_n, dyb, d_out_b = _out_bwd(dh1, y_b, _after(token, g['out_norm_b']), stacks['w_out'].T)
    share['w_out'], = _wgrad(y_n, [dh1], "wgrad_out")
    scatter_3, token = _reduce_scatter_start([share['w_out']], "reduce_scatter_start_3")
    grads = None
    for dil in DILATIONS:
        grads = _attn_bwd_branch(q, k, v, dyb, y_b, lse, grads, dil)
    duv, d_sgu_w, d_sgu_b, d_sgu_g, d_out_a = _sgu_bwd(uv, dya_n, w_tril, w_tril_t, bias,
                                                       _after(token, g['sgu_norm_g']), g['out_norm_a'])
    dproj = _in_bwd_proj(duv, grads[0], grads[1], grads[2], cos_t, sin_t)
    share['w_in'], = _wgrad(hn1, [dproj], "wgrad_in", transposed=True)
    scatter_4, token = _reduce_scatter_start([share['w_in']], "reduce_scatter_start_4")
    grad_x, d_mix_g = _in_bwd_x(dproj, w_in_t, xs, _after(token, g['mix_norm_g']), dh1)

    grads, deltas, new_m, new_v = {}, {}, {}, {}
    add_lead = lambda a: a.reshape((1,) + a.shape)

    def finish(names, handles, after, tag):
        landed = []
        for i, handle in enumerate(handles):
            landed += _reduce_scatter_wait(handle, after, "reduce_scatter_wait_%s%d" % (tag, i))
        halves = [_sum_shares(land, "sum_shares_" + n) for n, land in zip(names, landed)]
        for n, slab in zip(names, _swap_halves(halves, "swap_halves_" + tag)):
            g_shard = slab.T if SLAB_IS_TRANSPOSED[n] else slab
            d, nm, nv = _adamw(shard(n), g_shard, shard("m_" + n), shard("v_" + n), "adamw_" + n)
            grads[n], deltas[n], new_m[n], new_v[n] = map(add_lead, (g_shard, d, nm, nv))

    finish(['w_ple_gate', 'w_ple_proj', 'w_down', 'w_gate', 'w_up', 'w_out'], [scatter_1, scatter_2, scatter_3], token,
           "early")

    gs = {'mix_norm_g': d_mix_g, 'sgu_w': d_sgu_w, 'sgu_b': d_sgu_b[:, :HEADS_A].T, 'sgu_norm_g': d_sgu_g,
          'out_norm_a': d_out_a, 'out_norm_b': d_out_b, 'ffn_norm_g': d_ffn_g, 'ple_norm_g': d_ple_g,
          'final_norm_g': d_final_g}
    gs_block = _pack_small(gs).at[SMALL_ROWS - 1, 0].set(loss[0, 0])
    small_sum = _small_all_reduce(gs_block)
    loss_out = small_sum[SMALL_ROWS - 1, 0]
    small_shapes = {n: given[n].shape for n in SMALL}
    finish(['w_in'], [scatter_4], small_sum, "last")

    small = {n: given[n] for n in SMALL}
    d, nm, nv = _adamw(_pack_small(small), small_sum, _pack_small({n: given["m_" + n] for n in SMALL}),
                       _pack_small({n: given["v_" + n] for n in SMALL}), "adamw_small")
    for res, blk in ((grads, small_sum), (deltas, d), (new_m, nm), (new_v, nv)):
        res.update(_unpack_small(blk, small_shapes))

    outs = [loss_out, add_lead(grad_x)]
    for res in (grads, deltas, new_m, new_v):
        outs += [res[n] for n in WEIGHT_NAMES]
    return tuple(outs)
```

```python
import math

import jax
import jax.numpy as jnp
import numpy as np
from jax import lax
from jax.experimental import pallas as pl
from jax.experimental.pallas import tpu as pltpu

F32 = jnp.float32
MXU_DTYPE = jnp.bfloat16

D_MODEL = 1024
HEAD_DIM = 64
HEADS_A = 4
HEADS_B = 12
WIDTH_A = HEADS_A * HEAD_DIM
WIDTH_B = HEADS_B * HEAD_DIM
CHUNK = 128
BLOCK = 128
DILATIONS = (1, 4, 16)
ROPE_THETA = 10000.0
D_FF = 2816
FF_HALF = D_FF // 2
PLE_DIM = 256
IN_COLS = 2 * WIDTH_A + 3 * WIDTH_B
EPS = 1e-6
LANES = 128
N_CHIPS = 4
N_DEV = 8

ADAM_LR = 0.001
ADAM_B1 = 0.9
ADAM_B2 = 0.999
ADAM_EPS = 1e-08
ADAM_WD = 0.01
ADAM_STEP = 10

VMEM_LIMIT = 56 * 1024 * 1024

WEIGHT_NAMES = ['mix_norm_g', 'w_in', 'sgu_w', 'sgu_b', 'sgu_norm_g', 'out_norm_a', 'out_norm_b', 'w_out',
                'ffn_norm_g', 'w_gate', 'w_up', 'w_down', 'ple_norm_g', 'w_ple_gate', 'w_ple_proj', 'final_norm_g']
SHARDED = ['w_in', 'w_out', 'w_gate', 'w_up', 'w_down', 'w_ple_gate', 'w_ple_proj']
SMALL = ['mix_norm_g', 'sgu_w', 'sgu_b', 'sgu_norm_g', 'out_norm_a', 'out_norm_b', 'ffn_norm_g', 'ple_norm_g',
         'final_norm_g']
SMALL_SIZES = {'mix_norm_g': 1024, 'sgu_w': 65536, 'sgu_b': 512, 'sgu_norm_g': 256, 'out_norm_a': 256,
               'out_norm_b': 768, 'ffn_norm_g': 1024, 'ple_norm_g': 1024, 'final_norm_g': 1024}
SMALL_ROWS = 72


def _params(semantics=None):
    return pltpu.CompilerParams(dimension_semantics=semantics, vmem_limit_bytes=VMEM_LIMIT)


def _full(shape):
    nd = len(shape)
    return pl.BlockSpec(shape, lambda i: (0,) * nd)


def _rows(tm, width):
    return pl.BlockSpec((tm, width), lambda i: (i, 0))


def _rms_stats(x):
    r = lax.rsqrt(jnp.mean(x * x, axis=-1, keepdims=True) + EPS)
    return x * r, r


def _rms_bwd(dn, n, r):
    return r * (dn - n * jnp.mean(dn * n, axis=-1, keepdims=True))


def _dot(a, b):
    return jnp.dot(a, b, preferred_element_type=F32)


def _dot_nt(a, b):
    return lax.dot_general(a, b, (((1,), (1,)), ((), ())), preferred_element_type=F32)


def _dot_tn(a, b):
    return lax.dot_general(a, b, (((0,), (0,)), ((), ())), preferred_element_type=F32)


def _gelu_parts(x):
    c = math.sqrt(2.0 / math.pi)
    t = jnp.tanh(c * (x + 0.044715 * x * x * x))
    return 0.5 * x * (1.0 + t), t


def _gelu_grad(x, t):
    c = math.sqrt(2.0 / math.pi)
    return 0.5 * (1.0 + t) + 0.5 * x * (1.0 - t * t) * c * (1.0 + 3.0 * 0.044715 * x * x)


def _half_masks(dtype):
    lane = lax.broadcasted_iota(jnp.int32, (BLOCK, LANES), 1)
    lo = (lane < HEAD_DIM).astype(F32)
    return lo.astype(dtype), (1.0 - lo).astype(dtype)


def _rope_partner(t):
    lane = lax.broadcasted_iota(jnp.int32, t.shape, 1)
    first_half = (lane % HEAD_DIM) < (HEAD_DIM // 2)
    return jnp.where(first_half, pltpu.roll(t, LANES - HEAD_DIM // 2, 1), pltpu.roll(t, HEAD_DIM // 2, 1))


L_BLOCK = 256
L_GROUP = 16


def _store_l256(scr, out_ref, cols, value):
    tm = value.shape[0]
    scr[...] = value
    for blk in range(tm // L_BLOCK):
        for r in range(L_GROUP):
            lo = blk * L_BLOCK + r * L_GROUP
            piece = scr[pl.ds(blk * L_BLOCK + r, L_GROUP, stride=L_GROUP), :]
            out_ref[lo:lo + L_GROUP, cols] = piece.astype(out_ref.dtype)


def _load_l256(col_refs, tm):
    cols = []
    for ref in col_refs:
        pieces = [ref[pl.ds(blk * L_BLOCK + i, L_GROUP, stride=L_GROUP), :]
                  for blk in range(tm // L_BLOCK) for i in range(L_GROUP)]
        cols.append(jnp.concatenate(pieces, axis=0))
    return jnp.concatenate(cols, axis=1)


def _col_specs(tm, width):
    return [pl.BlockSpec((tm, LANES), lambda i, j=j: (i, j)) for j in range(width // LANES)]


def _in_fwd(x, g_mix, w_in, cos_t, sin_t):
    s = x.shape[0]
    tm = 512

    def body(x_ref, g_ref, w_ref, cos_ref, sin_ref, uv_ref, q_ref, k_ref, v_ref, hn_ref, scr):
        n, _ = _rms_stats(x_ref[...])
        hn = (n * g_ref[...]).astype(MXU_DTYPE)
        hn_ref[...] = hn
        proj = _dot(hn, w_ref[...])
        uv_ref[...] = proj[:, :2 * WIDTH_A]
        cos = cos_ref[...]
        sin = sin_ref[...]
        for i in range(WIDTH_B // LANES):
            lo = 2 * WIDTH_A + i * LANES
            tq = proj[:, lo:lo + LANES]
            tk = proj[:, lo + WIDTH_B:lo + WIDTH_B + LANES]
            tv = proj[:, lo + 2 * WIDTH_B:lo + 2 * WIDTH_B + LANES]
            cols = slice(i * LANES, (i + 1) * LANES)
            _store_l256(scr, q_ref, cols, (tq * cos + _rope_partner(tq) * sin) * (HEAD_DIM ** -0.5))
            _store_l256(scr, k_ref, cols, tk * cos + _rope_partner(tk) * sin)
            _store_l256(scr, v_ref, cols, tv)

    return pl.pallas_call(
        body, name="in_fwd", grid=(s // tm,), scratch_shapes=[pltpu.VMEM((tm, LANES), F32)],
        in_specs=[_rows(tm, D_MODEL), _full((1, D_MODEL)), _full((D_MODEL, IN_COLS)), _rows(tm, LANES),
                  _rows(tm, LANES)],
        out_specs=[_rows(tm, 2 * WIDTH_A), _rows(tm, WIDTH_B), _rows(tm, WIDTH_B), _rows(tm, WIDTH_B),
                   _rows(tm, D_MODEL)],
        out_shape=[jax.ShapeDtypeStruct((s, 2 * WIDTH_A), F32), jax.ShapeDtypeStruct((s, WIDTH_B), MXU_DTYPE),
                   jax.ShapeDtypeStruct((s, WIDTH_B), MXU_DTYPE), jax.ShapeDtypeStruct((s, WIDTH_B), MXU_DTYPE),
                   jax.ShapeDtypeStruct((s, D_MODEL), MXU_DTYPE)],
        compiler_params=_params(("arbitrary",)),
    )(x, g_mix, w_in, cos_t, sin_t)


class _Branch:
    def __init__(self, dil, s):
        self.dil = dil
        i = np.arange(L_GROUP)
        if dil == 16:
            self.grid = (16, s // 2048)
            self.shape = (8, 1, 1, L_GROUP)
            self.index = lambda r, n: (n, r // 4, r % 4, 0, 0)
            pos = (np.arange(8)[:, None] * 16 + i[None, :]).reshape(-1)
        elif dil == 4:
            self.grid = (4, s // 512)
            self.shape = (2, 4, 1, L_GROUP)
            self.index = lambda r, n: (n, 0, r, 0, 0)
            pos = (np.arange(2)[:, None, None] * 64 + np.arange(4)[None, :, None] + 4 * i[None, None, :]).reshape(-1)
        else:
            self.grid = (1, s // L_BLOCK)
            self.shape = (1, 4, 4, L_GROUP)
            self.index = lambda r, n: (n, 0, 0, 0, 0)
            pos = (np.arange(16)[:, None] + 16 * i[None, :]).reshape(-1)
        self.qn = pos.shape[0]
        self.nb = self.grid[1]
        dist = pos[:, None] - np.concatenate([pos - self.qn, pos])[None, :]
        band = (dist >= 0) & (dist <= BLOCK)
        start = band & (np.arange(2 * self.qn)[None, :] >= self.qn)
        self.bias = np.where(np.stack([band, start]), 0.0, -np.inf).astype(np.float32)

    def view(self, a):
        return a.reshape(a.shape[0] // L_BLOCK, 4, 4, L_GROUP, a.shape[1])

    def spec(self, w, step=lambda n: n):
        return pl.BlockSpec(self.shape + (w,), lambda r, n: self.index(r, step(n)))

    def bias_spec(self, step=lambda n: n):
        return pl.BlockSpec((1, self.qn, 2 * self.qn), lambda r, n: (jnp.where(step(n) == 0, 1, 0), 0, 0))

    def load(self, ref, cols=slice(None)):
        x = ref[:, :, :, :, cols]
        return x.reshape(self.qn, x.shape[-1])

    def store(self, ref, cols, value):
        ref[:, :, :, :, cols] = value.reshape(self.shape + (value.shape[-1],))


def _attn_fwd_branch(q, k, v, state, dil, last):
    s = q.shape[0]
    br = _Branch(dil, s)
    qn = br.qn
    first = state is None

    def body(*refs):
        bias_ref, q_ref, kc_ref, kp_ref, vc_ref, vp_ref = refs[:6]
        if first:
            outs = refs[6:]
        else:
            acc_ref, m_ref, l_ref = refs[6:9]
            outs = refs[9:]
            m_in = br.load(m_ref)
            l_in = br.load(l_ref)
        o_ref, m_out = outs[0], outs[1]
        bias2 = jnp.concatenate([bias_ref[0], bias_ref[0]], axis=0)
        lane = lax.broadcasted_iota(jnp.int32, (qn, LANES), 1)
        lo = lane < HEAD_DIM
        mask_lo = lo.astype(F32).astype(MXU_DTYPE)
        masks = (mask_lo, 1 - mask_lo)
        m_blk = jnp.zeros((qn, LANES), F32)
        l_blk = jnp.zeros((qn, LANES), F32)
        for hp in range(HEADS_B // 2):
            cols = slice(hp * LANES, (hp + 1) * LANES)
            qp = br.load(q_ref, cols)
            kcat = jnp.concatenate([br.load(kp_ref, cols), br.load(kc_ref, cols)], axis=0)
            vcat = jnp.concatenate([br.load(vp_ref, cols), br.load(vc_ref, cols)], axis=0)
            h0, h1 = 2 * hp, 2 * hp + 1
            sc = _dot_nt(jnp.concatenate([qp * masks[0], qp * masks[1]], axis=0), kcat) + bias2
            m_new = jnp.max(sc, axis=1, keepdims=True)
            if not first:
                m_prev = jnp.concatenate([m_in[:, h0:h0 + 1], m_in[:, h1:h1 + 1]], axis=0)
                m_new = jnp.maximum(m_prev, m_new)
            p = jnp.exp(sc - m_new)
            l_new = jnp.sum(p, axis=1, keepdims=True)
            acc = _dot(p.astype(MXU_DTYPE), vcat)
            if not first:
                alpha = jnp.exp(m_prev - m_new)
                l_new = alpha * jnp.concatenate([l_in[:, h0:h0 + 1], l_in[:, h1:h1 + 1]], axis=0) + l_new
                acc_in = br.load(acc_ref, cols)
                acc = alpha * jnp.concatenate([acc_in, acc_in], axis=0) + acc
            if last:
                acc = acc / l_new
                m_new = m_new + jnp.log(l_new)
            m_blk = jnp.where(lane == h0, m_new[:qn], jnp.where(lane == h1, m_new[qn:], m_blk))
            l_blk = jnp.where(lane == h0, l_new[:qn], jnp.where(lane == h1, l_new[qn:], l_blk))
            br.store(o_ref, cols, jnp.where(lo, acc[:qn], acc[qn:]))
        br.store(m_out, slice(None), m_blk)
        if not last:
            br.store(outs[2], slice(None), l_blk)

    before = lambda n: jnp.maximum(n - 1, 0)
    in_specs = [br.bias_spec(), br.spec(WIDTH_B), br.spec(WIDTH_B), br.spec(WIDTH_B, before), br.spec(WIDTH_B),
                br.spec(WIDTH_B, before)]
    args = [jnp.asarray(br.bias), br.view(q), br.view(k), br.view(k), br.view(v), br.view(v)]
    if not first:
        in_specs += [br.spec(WIDTH_B), br.spec(LANES), br.spec(LANES)]
        args += [br.view(a) for a in state]
    widths = (WIDTH_B, LANES) if last else (WIDTH_B, LANES, LANES)
    res = pl.pallas_call(
        body, name="attn_fwd_d%d" % dil, grid=br.grid, in_specs=in_specs,
        out_specs=[br.spec(w) for w in widths],
        out_shape=[jax.ShapeDtypeStruct((s // L_BLOCK, 4, 4, L_GROUP, w), F32) for w in widths],
        compiler_params=_params(("arbitrary", "arbitrary")),
    )(*args)
    return tuple(a.reshape(s, w) for a, w in zip(res, widths))


def _sgu_forward_tile(uv, w_ref, bias, g_sgu):
    tm = uv.shape[0]
    u = uv[:, :WIDTH_A]
    v = uv[:, WIDTH_A:]
    ug, tu = _gelu_parts(u)
    vg, tv = _gelu_parts(v)
    mu = jnp.mean(vg, axis=-1, keepdims=True)
    vc = vg - mu
    rs = lax.rsqrt(jnp.mean(vc * vc, axis=-1, keepdims=True) + EPS)
    vhat = vc * rs
    vn = (vhat * g_sgu).astype(MXU_DTYPE)
    masks = _half_masks(MXU_DTYPE)
    chunks = []
    for c in range(tm // CHUNK):
        rows = slice(c * CHUNK, (c + 1) * CHUNK)
        groups = []
        for gp in range(2):
            vn_g = vn[rows, gp * LANES:(gp + 1) * LANES]
            groups.append(_dot(w_ref[2 * gp], vn_g * masks[0]) + _dot(w_ref[2 * gp + 1], vn_g * masks[1]))
        chunks.append(jnp.concatenate(groups, axis=1) + bias)
    mixed = jnp.concatenate(chunks, axis=0)
    return dict(u=u, v=v, ug=ug, tu=tu, tv=tv, rs=rs, vhat=vhat, vn=vn, mixed=mixed, ya=ug * mixed)


def _sgu_fwd(uv, w_tril, bias, g_sgu, g_a):
    s = uv.shape[0]
    tm = 512

    def body(uv_ref, w_ref, b_ref, gs_ref, ga_ref, o_ref):
        t = _sgu_forward_tile(uv_ref[...], w_ref, b_ref[...], gs_ref[...])
        n, _ = _rms_stats(t['ya'])
        o_ref[...] = (n * ga_ref[...]).astype(MXU_DTYPE)

    return pl.pallas_call(
        body, name="sgu_fwd", grid=(s // tm,),
        in_specs=[_rows(tm, 2 * WIDTH_A), _full((HEADS_A, CHUNK, CHUNK)), _full((CHUNK, WIDTH_A)),
                  _full((1, WIDTH_A)), _full((1, WIDTH_A))],
        out_specs=_rows(tm, WIDTH_A), out_shape=jax.ShapeDtypeStruct((s, WIDTH_A), MXU_DTYPE),
        compiler_params=_params(("arbitrary",)),
    )(uv, w_tril, bias, g_sgu, g_a)


def _out_fwd(ya_n, y_b, g_b, w_out, x):
    s = x.shape[0]
    tm = 512
    nc = WIDTH_B // LANES

    def body(ya_ref, *refs):
        yb_refs = refs[:nc]
        g_ref, w_ref, x_ref, h_ref, yn_ref = refs[nc:]
        n, _ = _rms_stats(_load_l256(yb_refs, tm))
        yn = jnp.concatenate([ya_ref[...], (n * g_ref[...]).astype(MXU_DTYPE)], axis=1)
        yn_ref[...] = yn
        h_ref[...] = x_ref[...] + _dot(yn, w_ref[...])

    return pl.pallas_call(
        body, name="out_fwd", grid=(s // tm,),
        in_specs=[_rows(tm, WIDTH_A)] + _col_specs(tm, WIDTH_B) + [_full((1, WIDTH_B)), _full((D_MODEL, D_MODEL)),
                                                                 _rows(tm, D_MODEL)],
        out_specs=[_rows(tm, D_MODEL), _rows(tm, D_MODEL)],
        out_shape=[jax.ShapeDtypeStruct((s, D_MODEL), F32), jax.ShapeDtypeStruct((s, D_MODEL), MXU_DTYPE)],
        compiler_params=_params(("arbitrary",)),
    )(ya_n, *([y_b] * nc), g_b, w_out, x)


def _ffn_fwd(h1, g_ffn, w_gate, w_up, w_down):
    s = h1.shape[0]
    tm = 256

    def body(h_ref, g_ref, wg_ref, wu_ref, wd_ref, o_ref, gate_ref, up_ref, hn_ref):
        h = h_ref[...]
        n, _ = _rms_stats(h)
        hn = (n * g_ref[...]).astype(MXU_DTYPE)
        hn_ref[...] = hn
        out = h
        for c in range(2):
            cols = slice(c * FF_HALF, (c + 1) * FF_HALF)
            gate = _dot(hn, wg_ref[:, cols])
            up = _dot(hn, wu_ref[:, cols])
            gate_ref[:, cols] = gate.astype(MXU_DTYPE)
            up_ref[:, cols] = up.astype(MXU_DTYPE)
            act = (gate * jax.nn.sigmoid(gate) * up).astype(MXU_DTYPE)
            out = out + _dot(act, wd_ref[cols, :])
        o_ref[...] = out

    return pl.pallas_call(
        body, name="ffn_fwd", grid=(s // tm,),
        in_specs=[_rows(tm, D_MODEL), _full((1, D_MODEL)), _full((D_MODEL, D_FF)), _full((D_MODEL, D_FF)),
                  _full((D_FF, D_MODEL))],
        out_specs=[_rows(tm, D_MODEL), _rows(tm, D_FF), _rows(tm, D_FF), _rows(tm, D_MODEL)],
        out_shape=[jax.ShapeDtypeStruct((s, D_MODEL), F32), jax.ShapeDtypeStruct((s, D_FF), MXU_DTYPE),
                   jax.ShapeDtypeStruct((s, D_FF), MXU_DTYPE), jax.ShapeDtypeStruct((s, D_MODEL), MXU_DTYPE)],
        compiler_params=_params(("arbitrary",)),
    )(h1, g_ffn, w_gate, w_up, w_down)


def _ple_loss(h2, p, target, g_ple, w_pg, w_pg_t, w_pp, g_final):
    s = h2.shape[0]
    tm = 256

    def body(h_ref, p_ref, t_ref, gp_ref, wg_ref, wgt_ref, wp_ref, gf_ref,
             loss_ref, dh_ref, dz_ref, dpp_ref, hn_ref, dgp_ref, dgf_ref):
        @pl.when(pl.program_id(0) == 0)
        def _():
            loss_ref[...] = jnp.zeros_like(loss_ref)
            dgp_ref[...] = jnp.zeros_like(dgp_ref)
            dgf_ref[...] = jnp.zeros_like(dgf_ref)

        h2t = h_ref[...]
        n2, r2 = _rms_stats(h2t)
        hn = (n2 * gp_ref[...]).astype(MXU_DTYPE)
        hn_ref[...] = hn
        gate = jax.nn.sigmoid(_dot(hn, wg_ref[...]))
        pp = _dot(p_ref[...].astype(MXU_DTYPE), wp_ref[...])
        h3 = h2t + gate * pp
        n3, r3 = _rms_stats(h3)
        diff = n3 * gf_ref[...] - t_ref[...]
        loss_ref[...] += jnp.full(loss_ref.shape, 0.5 * jnp.sum(diff * diff) / D_MODEL, F32)
        dy = diff * (1.0 / D_MODEL)
        dgf_ref[...] += jnp.sum(dy * n3, axis=0, keepdims=True)
        dh3 = _rms_bwd(dy * gf_ref[...], n3, r3)
        dpp_ref[...] = (dh3 * gate).astype(MXU_DTYPE)
        dz = (dh3 * pp * gate * (1.0 - gate)).astype(MXU_DTYPE)
        dz_ref[...] = dz
        dhn = _dot(dz, wgt_ref[...])
        dgp_ref[...] += jnp.sum(dhn * n2, axis=0, keepdims=True)
        dh_ref[...] = dh3 + _rms_bwd(dhn * gp_ref[...], n2, r2)

    return pl.pallas_call(
        body, name="ple_loss", grid=(s // tm,),
        in_specs=[_rows(tm, D_MODEL), _rows(tm, PLE_DIM), _rows(tm, D_MODEL), _full((1, D_MODEL)),
                  _full((D_MODEL, D_MODEL)), _full((D_MODEL, D_MODEL)), _full((PLE_DIM, D_MODEL)),
                  _full((1, D_MODEL))],
        out_specs=[_full((1, LANES)), _rows(tm, D_MODEL), _rows(tm, D_MODEL), _rows(tm, D_MODEL),
                   _rows(tm, D_MODEL), _full((1, D_MODEL)), _full((1, D_MODEL))],
        out_shape=[jax.ShapeDtypeStruct((1, LANES), F32), jax.ShapeDtypeStruct((s, D_MODEL), F32),
                   jax.ShapeDtypeStruct((s, D_MODEL), MXU_DTYPE), jax.ShapeDtypeStruct((s, D_MODEL), MXU_DTYPE),
                   jax.ShapeDtypeStruct((s, D_MODEL), MXU_DTYPE), jax.ShapeDtypeStruct((1, D_MODEL), F32),
                   jax.ShapeDtypeStruct((1, D_MODEL), F32)],
        compiler_params=_params(("arbitrary",)),
    )(h2, p, target, g_ple, w_pg, w_pg_t, w_pp, g_final)


def _ffn_bwd(dh2, h1, gate, up, g_ffn, w_down_t, w_gate_t, w_up_t):
    s = h1.shape[0]
    tm = 256

    def body(dh_ref, h_ref, gate_ref, up_ref, g_ref, wdt_ref, wgt_ref, wut_ref,
             o_ref, act_ref, dg_ref, du_ref, dgn_ref):
        @pl.when(pl.program_id(0) == 0)
        def _():
            dgn_ref[...] = jnp.zeros_like(dgn_ref)

        dh = dh_ref[...]
        dhb = dh.astype(MXU_DTYPE)
        dhn = jnp.zeros((tm, D_MODEL), F32)
        for c in range(2):
            cols = slice(c * FF_HALF, (c + 1) * FF_HALF)
            dact = _dot(dhb, wdt_ref[:, cols])
            g = gate_ref[:, cols].astype(F32)
            u = up_ref[:, cols].astype(F32)
            sg = jax.nn.sigmoid(g)
            silu = g * sg
            act_ref[:, cols] = (silu * u).astype(MXU_DTYPE)
            du = (dact * silu).astype(MXU_DTYPE)
            dg = (dact * u * sg * (1.0 + g * (1.0 - sg))).astype(MXU_DTYPE)
            du_ref[:, cols] = du
            dg_ref[:, cols] = dg
            dhn = dhn + _dot(dg, wgt_ref[cols, :]) + _dot(du, wut_ref[cols, :])
        n, r = _rms_stats(h_ref[...])
        dgn_ref[...] += jnp.sum(dhn * n, axis=0, keepdims=True)
        o_ref[...] = dh + _rms_bwd(dhn * g_ref[...], n, r)

    return pl.pallas_call(
        body, name="ffn_bwd", grid=(s // tm,),
        in_specs=[_rows(tm, D_MODEL), _rows(tm, D_MODEL), _rows(tm, D_FF), _rows(tm, D_FF), _full((1, D_MODEL)),
                  _full((D_MODEL, D_FF)), _full((D_FF, D_MODEL)), _full((D_FF, D_MODEL))],
        out_specs=[_rows(tm, D_MODEL), _rows(tm, D_FF), _rows(tm, D_FF), _rows(tm, D_FF), _full((1, D_MODEL))],
        out_shape=[jax.ShapeDtypeStruct((s, D_MODEL), F32), jax.ShapeDtypeStruct((s, D_FF), MXU_DTYPE),
                   jax.ShapeDtypeStruct((s, D_FF), MXU_DTYPE), jax.ShapeDtypeStruct((s, D_FF), MXU_DTYPE),
                   jax.ShapeDtypeStruct((1, D_MODEL), F32)],
        compiler_params=_params(("arbitrary",)),
    )(dh2, h1, gate, up, g_ffn, w_down_t, w_gate_t, w_up_t)


def _out_bwd(dh1, y_b, g_b, w_out_t):
    s = dh1.shape[0]
    tm = 512
    nc = WIDTH_B // LANES

    def body(dh_ref, *refs):
        yb_refs = refs[:nc]
        g_ref, wt_ref, dya_ref, dyb_ref, dg_ref, scr = refs[nc:]

        @pl.when(pl.program_id(0) == 0)
        def _():
            dg_ref[...] = jnp.zeros_like(dg_ref)

        dy = _dot(dh_ref[...].astype(MXU_DTYPE), wt_ref[...])
        dya_ref[...] = dy[:, :WIDTH_A]
        dyb = dy[:, WIDTH_A:]
        n, r = _rms_stats(_load_l256(yb_refs, tm))
        dg_ref[...] += jnp.sum(dyb * n, axis=0, keepdims=True)
        dyb_in = _rms_bwd(dyb * g_ref[...], n, r)
        for j in range(nc):
            cols = slice(j * LANES, (j + 1) * LANES)
            _store_l256(scr, dyb_ref, cols, dyb_in[:, cols])

    return pl.pallas_call(
        body, name="out_bwd", grid=(s // tm,), scratch_shapes=[pltpu.VMEM((tm, LANES), F32)],
        in_specs=[_rows(tm, D_MODEL)] + _col_specs(tm, WIDTH_B) + [_full((1, WIDTH_B)), _full((D_MODEL, D_MODEL))],
        out_specs=[_rows(tm, WIDTH_A), _rows(tm, WIDTH_B), _full((1, WIDTH_B))],
        out_shape=[jax.ShapeDtypeStruct((s, WIDTH_A), F32), jax.ShapeDtypeStruct((s, WIDTH_B), F32),
                   jax.ShapeDtypeStruct((1, WIDTH_B), F32)],
        compiler_params=_params(("arbitrary",)),
    )(dh1, *([y_b] * nc), g_b, w_out_t)


def _attn_bwd_branch(q, k, v, do, o, lse, grads, dil):
    s = q.shape[0]
    br = _Branch(dil, s)
    qn, nb = br.qn, br.nb
    first = grads is None

    def body(*refs):
        bias_ref, q_ref, kc_ref, kp_ref, vc_ref, vp_ref, do_ref, o_ref, lse_ref = refs[:9]
        if first:
            rest = refs[9:]
        else:
            dq_in, dk_in, dv_in = refs[9:12]
            rest = refs[12:]
        dq_ref, dk_ref, dv_ref, dk_carry, dv_carry = rest
        n = pl.program_id(1)

        @pl.when(n == 0)
        def _():
            dk_carry[...] = jnp.zeros_like(dk_carry)
            dv_carry[...] = jnp.zeros_like(dv_carry)

        @pl.when(n < nb)
        def _():
            bias2 = jnp.concatenate([bias_ref[0], bias_ref[0]], axis=0)
            lane = lax.broadcasted_iota(jnp.int32, (qn, LANES), 1)
            lo = lane < HEAD_DIM
            mask_f = lo.astype(F32)
            mask_lo = mask_f.astype(MXU_DTYPE)
            lse = br.load(lse_ref)
            for hp in range(HEADS_B // 2):
                cols = slice(hp * LANES, (hp + 1) * LANES)
                h0, h1 = 2 * hp, 2 * hp + 1
                qp = br.load(q_ref, cols)
                kcat = jnp.concatenate([br.load(kp_ref, cols), br.load(kc_ref, cols)], axis=0)
                vcat = jnp.concatenate([br.load(vp_ref, cols), br.load(vc_ref, cols)], axis=0)
                dop = br.load(do_ref, cols)
                prod = dop * br.load(o_ref, cols)
                prod_lo = prod * mask_f
                delta = jnp.concatenate([jnp.sum(prod_lo, axis=1, keepdims=True),
                                         jnp.sum(prod - prod_lo, axis=1, keepdims=True)], axis=0)
                qs = jnp.concatenate([qp * mask_lo, qp * (1 - mask_lo)], axis=0)
                dos = jnp.concatenate([dop * mask_f, dop * (1.0 - mask_f)], axis=0).astype(MXU_DTYPE)
                lse2 = jnp.concatenate([lse[:, h0:h0 + 1], lse[:, h1:h1 + 1]], axis=0)
                p = jnp.exp(_dot_nt(qs, kcat) + bias2 - lse2)
                ds = (p * (_dot_nt(dos, vcat) - delta)).astype(MXU_DTYPE)
                dvc = _dot_tn(p.astype(MXU_DTYPE), dos)
                dkc = _dot_tn(ds, qs)
                dq2 = _dot(ds, kcat)
                dq = jnp.where(lo, dq2[:qn], dq2[qn:])
                dk_prev = dk_carry[:, cols] + dkc[:qn]
                dv_prev = dv_carry[:, cols] + dvc[:qn]
                if not first:
                    dq = dq + br.load(dq_in, cols)
                    dk_prev = dk_prev + br.load(dk_in, cols)
                    dv_prev = dv_prev + br.load(dv_in, cols)
                br.store(dq_ref, cols, dq)
                br.store(dk_ref, cols, dk_prev)
                br.store(dv_ref, cols, dv_prev)
                dk_carry[:, cols] = dkc[qn:]
                dv_carry[:, cols] = dvc[qn:]

        @pl.when(n == nb)
        def _():
            dk_last = dk_carry[...]
            dv_last = dv_carry[...]
            if not first:
                dk_last = dk_last + br.load(dk_in)
                dv_last = dv_last + br.load(dv_in)
            br.store(dk_ref, slice(None), dk_last)
            br.store(dv_ref, slice(None), dv_last)

    cur = lambda n: jnp.minimum(n, nb - 1)
    before = lambda n: jnp.maximum(cur(n) - 1, 0)
    late = lambda n: jnp.maximum(n - 1, 0)
    in_specs = [br.bias_spec(cur), br.spec(WIDTH_B, cur), br.spec(WIDTH_B, cur), br.spec(WIDTH_B, before),
                br.spec(WIDTH_B, cur), br.spec(WIDTH_B, before), br.spec(WIDTH_B, cur), br.spec(WIDTH_B, cur),
                br.spec(LANES, cur)]
    args = [jnp.asarray(br.bias)] + [br.view(a) for a in (q, k, k, v, v, do, o, lse)]
    if not first:
        in_specs += [br.spec(WIDTH_B, cur), br.spec(WIDTH_B, late), br.spec(WIDTH_B, late)]
        args += [br.view(g) for g in grads]
    res = pl.pallas_call(
        body, name="attn_bwd_d%d" % dil, grid=(br.grid[0], nb + 1), in_specs=in_specs,
        out_specs=[br.spec(WIDTH_B, cur), br.spec(WIDTH_B, late), br.spec(WIDTH_B, late)],
        out_shape=[jax.ShapeDtypeStruct((s // L_BLOCK, 4, 4, L_GROUP, WIDTH_B), F32)] * 3,
        scratch_shapes=[pltpu.VMEM((qn, WIDTH_B), F32), pltpu.VMEM((qn, WIDTH_B), F32)],
        compiler_params=_params(("arbitrary", "arbitrary")),
    )(*args)
    return tuple(a.reshape(s, WIDTH_B) for a in res)


def _sgu_bwd(uv, dya_n, w_tril, w_tril_t, bias, g_sgu, g_a):
    s = uv.shape[0]
    tm = 512

    def body(uv_ref, dy_ref, w_ref, wt_ref, b_ref, gs_ref, ga_ref, duv_ref, dw_ref, db_ref, dgs_ref, dga_ref,
             db_acc):
        i = pl.program_id(0)

        @pl.when(i == 0)
        def _():
            dw_ref[...] = jnp.zeros_like(dw_ref)
            dgs_ref[...] = jnp.zeros_like(dgs_ref)
            dga_ref[...] = jnp.zeros_like(dga_ref)
            db_acc[...] = jnp.zeros_like(db_acc)

        t = _sgu_forward_tile(uv_ref[...], w_ref, b_ref[...], gs_ref[...])
        na, ra = _rms_stats(t['ya'])
        dyn = dy_ref[...]
        dga_ref[...] += jnp.sum(dyn * na, axis=0, keepdims=True)
        dya = _rms_bwd(dyn * ga_ref[...], na, ra)
        dug = dya * t['mixed']
        dmixed = dya * t['ug']
        dmb = dmixed.astype(MXU_DTYPE)
        masks = _half_masks(MXU_DTYPE)
        chunks = []
        db = jnp.zeros((CHUNK, WIDTH_A), F32)
        for c in range(tm // CHUNK):
            rows = slice(c * CHUNK, (c + 1) * CHUNK)
            db = db + dmixed[rows]
            groups = []
            for gp in range(2):
                cols = slice(gp * LANES, (gp + 1) * LANES)
                dm_g = dmb[rows, cols]
                vn_g = t['vn'][rows, cols]
                dvn_g = jnp.zeros((CHUNK, LANES), F32)
                for j in range(2):
                    dm_h = dm_g * masks[j]
                    dvn_g = dvn_g + _dot(wt_ref[2 * gp + j], dm_h)
                    dw_ref[2 * gp + j] += _dot_nt(dm_h, vn_g)
                groups.append(dvn_g)
            chunks.append(jnp.concatenate(groups, axis=1))
        db_acc[...] += db
        dvn = jnp.concatenate(chunks, axis=0)
        vhat = t['vhat']
        dgs_ref[...] += jnp.sum(dvn * vhat, axis=0, keepdims=True)
        dvh = dvn * gs_ref[...]
        dvg = t['rs'] * (dvh - jnp.mean(dvh, axis=-1, keepdims=True)
                         - vhat * jnp.mean(dvh * vhat, axis=-1, keepdims=True))
        duv_ref[:, :WIDTH_A] = (dug * _gelu_grad(t['u'], t['tu'])).astype(MXU_DTYPE)
        duv_ref[:, WIDTH_A:] = (dvg * _gelu_grad(t['v'], t['tv'])).astype(MXU_DTYPE)

        @pl.when(i == pl.num_programs(0) - 1)
        def _():
            lane_a = lax.broadcasted_iota(jnp.int32, (CHUNK, WIDTH_A), 1)
            lane = lax.broadcasted_iota(jnp.int32, (CHUNK, LANES), 1)
            acc = db_acc[...]
            out = jnp.zeros((CHUNK, LANES), F32)
            for h in range(HEADS_A):
                col = jnp.sum(jnp.where(lane_a // HEAD_DIM == h, acc, 0.0), axis=1, keepdims=True)
                out = jnp.where(lane == h, col, out)
            db_ref[...] = out
            causal = (lax.broadcasted_iota(jnp.int32, (CHUNK, CHUNK), 0)
                      >= lax.broadcasted_iota(jnp.int32, (CHUNK, CHUNK), 1))
            for h in range(HEADS_A):
                dw_ref[h] = jnp.where(causal, dw_ref[h], 0.0)

    return pl.pallas_call(
        body, name="sgu_bwd", grid=(s // tm,),
        in_specs=[_rows(tm, 2 * WIDTH_A), _rows(tm, WIDTH_A), _full((HEADS_A, CHUNK, CHUNK)),
                  _full((HEADS_A, CHUNK, CHUNK)), _full((CHUNK, WIDTH_A)), _full((1, WIDTH_A)),
                  _full((1, WIDTH_A))],
        out_specs=[_rows(tm, 2 * WIDTH_A), _full((HEADS_A, CHUNK, CHUNK)), _full((CHUNK, LANES)),
                   _full((1, WIDTH_A)), _full((1, WIDTH_A))],
        out_shape=[jax.ShapeDtypeStruct((s, 2 * WIDTH_A), MXU_DTYPE),
                   jax.ShapeDtypeStruct((HEADS_A, CHUNK, CHUNK), F32), jax.ShapeDtypeStruct((CHUNK, LANES), F32),
                   jax.ShapeDtypeStruct((1, WIDTH_A), F32), jax.ShapeDtypeStruct((1, WIDTH_A), F32)],
        scratch_shapes=[pltpu.VMEM((CHUNK, WIDTH_A), F32)],
        compiler_params=_params(("arbitrary",)),
    )(uv, dya_n, w_tril, w_tril_t, bias, g_sgu, g_a)


def _in_bwd_proj(duv, dq, dk, dv, cos_t, sin_t):
    s = duv.shape[0]
    tm = 512
    nc = WIDTH_B // LANES

    def body(duv_ref, *refs):
        dq_refs, dk_refs, dv_refs = refs[:nc], refs[nc:2 * nc], refs[2 * nc:3 * nc]
        cos_ref, sin_ref, dp_ref = refs[3 * nc:]
        cos = cos_ref[...]
        sin = sin_ref[...]
        dp_ref[:, :2 * WIDTH_A] = duv_ref[...]
        for i in range(nc):
            lo = 2 * WIDTH_A + i * LANES
            tq = _load_l256(dq_refs[i:i + 1], tm) * (HEAD_DIM ** -0.5)
            tk = _load_l256(dk_refs[i:i + 1], tm)
            dp_ref[:, lo:lo + LANES] = (tq * cos + _rope_partner(tq * sin)).astype(MXU_DTYPE)
            dp_ref[:, lo + WIDTH_B:lo + WIDTH_B + LANES] = (tk * cos + _rope_partner(tk * sin)).astype(MXU_DTYPE)
            dp_ref[:, lo + 2 * WIDTH_B:lo + 2 * WIDTH_B + LANES] = _load_l256(dv_refs[i:i + 1], tm).astype(MXU_DTYPE)

    return pl.pallas_call(
        body, name="in_bwd_proj", grid=(s // tm,),
        in_specs=[_rows(tm, 2 * WIDTH_A)] + 3 * _col_specs(tm, WIDTH_B) + [_rows(tm, LANES), _rows(tm, LANES)],
        out_specs=_rows(tm, IN_COLS), out_shape=jax.ShapeDtypeStruct((s, IN_COLS), MXU_DTYPE),
        compiler_params=_params(("arbitrary",)),
    )(duv, *([dq] * nc), *([dk] * nc), *([dv] * nc), cos_t, sin_t)


def _in_bwd_x(dproj, w_in_t, x, g_mix, dh1):
    s = x.shape[0]
    tm = 512

    def body(dp_ref, wt_ref, x_ref, g_ref, dh_ref, gx_ref, dg_ref):
        @pl.when(pl.program_id(0) == 0)
        def _():
            dg_ref[...] = jnp.zeros_like(dg_ref)

        dhn = _dot(dp_ref[...], wt_ref[...])
        n, r = _rms_stats(x_ref[...])
        dg_ref[...] += jnp.sum(dhn * n, axis=0, keepdims=True)
        gx_ref[...] = dh_ref[...] + _rms_bwd(dhn * g_ref[...], n, r)

    return pl.pallas_call(
        body, name="in_bwd_x", grid=(s // tm,),
        in_specs=[_rows(tm, IN_COLS), _full((IN_COLS, D_MODEL)), _rows(tm, D_MODEL), _full((1, D_MODEL)),
                  _rows(tm, D_MODEL)],
        out_specs=[_rows(tm, D_MODEL), _full((1, D_MODEL))],
        out_shape=[jax.ShapeDtypeStruct((s, D_MODEL), F32), jax.ShapeDtypeStruct((1, D_MODEL), F32)],
        compiler_params=_params(("arbitrary",)),
    )(dproj, w_in_t, x, g_mix, dh1)


def _wgrad(a, bs, name, transposed=False):
    s, m = a.shape
    bm = 512 if m % 512 == 0 else (FF_HALF if m == D_FF else m)
    ts = 512
    nsteps = s // ts
    nb = len(bs)

    def body(a_ref, *refs):
        b_refs, o_refs, accs = refs[:nb], refs[nb:2 * nb], refs[2 * nb:]
        kk = pl.program_id(1)
        at = a_ref[...].astype(MXU_DTYPE)
        for b_ref, o_ref, acc in zip(b_refs, o_refs, accs):
            c = _dot_tn(at, b_ref[...].astype(MXU_DTYPE))

            @pl.when(kk == 0)
            def _():
                acc[...] = c

            @pl.when(kk > 0)
            def _():
                acc[...] += c

            @pl.when(kk == nsteps - 1)
            def _():
                total = acc[...]
                o_ref[...] = (total.T if transposed else total).astype(o_ref.dtype)

    if transposed:
        out_specs = [pl.BlockSpec((b.shape[1], bm), lambda i, kk: (0, i)) for b in bs]
        out_shape = [jax.ShapeDtypeStruct((b.shape[1], m), jnp.bfloat16) for b in bs]
    else:
        out_specs = [pl.BlockSpec((bm, b.shape[1]), lambda i, kk: (i, 0)) for b in bs]
        out_shape = [jax.ShapeDtypeStruct((m, b.shape[1]), jnp.bfloat16) for b in bs]
    return pl.pallas_call(
        body, name=name, grid=(m // bm, nsteps),
        in_specs=[pl.BlockSpec((ts, bm), lambda i, kk: (kk, i))]
        + [pl.BlockSpec((ts, b.shape[1]), lambda i, kk: (kk, 0)) for b in bs],
        out_specs=out_specs, out_shape=out_shape,
        scratch_shapes=[pltpu.VMEM((bm, b.shape[1]), F32) for b in bs],
        compiler_params=_params(("arbitrary", "arbitrary")),
    )(a, *bs)


def _rope_tables(s):
    half = HEAD_DIM // 2
    inv = ROPE_THETA ** (-jnp.arange(half, dtype=F32) / half)
    ang = jnp.arange(s, dtype=F32)[:, None] * inv[None, :]
    cos = jnp.cos(ang)
    sin = jnp.sin(ang)
    cos_t = jnp.concatenate([cos, cos, cos, cos], axis=1)
    sin_t = jnp.concatenate([-sin, sin, -sin, sin], axis=1)
    return cos_t, sin_t


MESH = pl.DeviceIdType.MESH
ANY = pl.BlockSpec(memory_space=pl.ANY)
SEM = pl.BlockSpec(memory_space=pltpu.SEMAPHORE)
SPLIT_COPY = pltpu.CompilerParams(has_side_effects=pltpu.SideEffectType.DATAFLOW_SIDE_EFFECTING)
SLAB_IS_TRANSPOSED = {'w_in': True, 'w_out': False, 'w_gate': True, 'w_up': True, 'w_down': False,
                      'w_ple_gate': False, 'w_ple_proj': True}


def _place():
    x, y, c = lax.axis_index("x"), lax.axis_index("y"), lax.axis_index("c")
    other_chips = [(1 - x, y), (x, 1 - y), (1 - x, 1 - y)]
    return x, y, c, other_chips


def _chip_of(chip):
    return 2 * chip[0] + chip[1]


def _half(ref, lead, hc):
    hr = ref.shape[1] // 2
    return ref.at[lead, pl.ds(hc * hr, hr), :]


def _put_own(stack, own, index):
    return lax.dynamic_update_slice(stack, own[None], (index,) + (0,) * own.ndim)


def _all_gather_now(slab):
    rows, cols = slab.shape

    def body(x_ref, out_ref, send_sems, recv_sems):
        x, y, c, chips = _place()
        sibling = (x, y, 1 - c)
        hr = rows // 2

        def copy(k, src, dst, to):
            return pltpu.make_async_remote_copy(src_ref=src, dst_ref=dst, send_sem=send_sems.at[k],
                                                recv_sem=recv_sems.at[k], device_id=to, device_id_type=MESH)

        my_half = x_ref.at[pl.ds(c * hr, hr), :]
        first = [copy(j, my_half, _half(out_ref, 2 * x + y, c), (*chip, c)) for j, chip in enumerate(chips)]
        for cp in first:
            cp.start()
        passed = [copy(3 + j, _half(out_ref, _chip_of(chip), c), _half(out_ref, _chip_of(chip), c), sibling)
                  for j, chip in enumerate(chips)]
        for j, chip in enumerate(chips):
            copy(j, my_half, _half(out_ref, _chip_of(chip), c), (*chip, c)).wait_recv()
            passed[j].start()
        for j, chip in enumerate(chips):
            copy(3 + j, my_half, _half(out_ref, _chip_of(chip), 1 - c), sibling).wait_recv()
        for cp in first + passed:
            cp.wait_send()

    gathered = pl.pallas_call(
        body, name="all_gather_now", out_shape=jax.ShapeDtypeStruct((N_CHIPS, rows, cols), slab.dtype),
        in_specs=[ANY], out_specs=ANY,
        scratch_shapes=[pltpu.SemaphoreType.DMA((6,)), pltpu.SemaphoreType.DMA((6,))],
    )(slab)
    me = 2 * lax.axis_index("x") + lax.axis_index("y")
    return _put_own(gathered, slab, me).reshape(N_CHIPS * rows, cols)


def _gather_copies(slab_refs, land_refs, send_sems, recv_sems):
    x, y, c, chips = _place()
    sends, recvs = [], []
    for k, (src, land) in enumerate(zip(slab_refs, land_refs)):
        hr = src.shape[0] // 2
        for j, chip in enumerate(chips):
            for t in range(2):
                sends.append(pltpu.make_async_remote_copy(
                    src_ref=src.at[pl.ds(c * hr, hr), :], dst_ref=_half(land, 2 * x + y, c),
                    send_sem=send_sems.at[6 * k + 2 * j + t], recv_sem=recv_sems.at[6 * k + 2 * j + c],
                    device_id=(*chip, t), device_id_type=MESH))
                recvs.append(pltpu.make_async_remote_copy(
                    src_ref=src.at[pl.ds(t * hr, hr), :], dst_ref=_half(land, _chip_of(chip), t),
                    send_sem=send_sems.at[6 * k + 2 * j + t], recv_sem=recv_sems.at[6 * k + 2 * j + t],
                    device_id=(*chip, t), device_id_type=MESH))
    return sends, recvs


def _all_gather_start(slabs, after):
    n = len(slabs)

    def body(*refs):
        slab_refs, land_refs = refs[:n], refs[n:2 * n]
        send_sems, recv_sems = refs[2 * n + 1:2 * n + 3]
        token = refs[-1]
        sends, _ = _gather_copies(slab_refs, land_refs, send_sems, recv_sems)
        for cp in sends:
            cp.start()
        token[...] = jnp.zeros_like(token)

    lands = [lax.empty((N_CHIPS,) + s.shape, s.dtype) for s in slabs]
    hbm = lambda a: pltpu.HBM(a.shape, a.dtype)
    res = pl.pallas_call(
        body, name="all_gather_start",
        out_shape=(pltpu.SemaphoreType.DMA((6 * n,)), pltpu.SemaphoreType.DMA((6 * n,)), *map(hbm, slabs),
                   *map(hbm, lands), jax.ShapeDtypeStruct((8, LANES), F32)),
        in_specs=[ANY] * (2 * n + 1),
        out_specs=(SEM, SEM, *([ANY] * (2 * n)), pl.BlockSpec(memory_space=pltpu.VMEM)),
        input_output_aliases={i: 2 + i for i in range(2 * n)}, compiler_params=SPLIT_COPY,
    )(*[pltpu.with_memory_space_constraint(a, pltpu.HBM) for a in list(slabs) + lands], after)
    return res[:-1], res[-1]


def _all_gather_wait(handle, after):
    send_sems, recv_sems = handle[:2]
    n = (len(handle) - 2) // 2
    slabs, lands = handle[2:2 + n], handle[2 + n:]

    def body(*refs):
        slab_refs, land_refs = refs[:n], refs[n:2 * n]
        send_sems, recv_sems = refs[2 * n:2 * n + 2]
        sends, recvs = _gather_copies(slab_refs, land_refs, send_sems, recv_sems)
        for cp in sends:
            cp.wait_send()
        for cp in recvs:
            cp.wait_recv()

    hbm = lambda a: pltpu.HBM(a.shape, a.dtype)
    res = pl.pallas_call(
        body, name="all_gather_wait", out_shape=tuple(map(hbm, list(slabs) + list(lands))),
        in_specs=[ANY] * (2 * n) + [SEM, SEM, ANY], out_specs=tuple([ANY] * (2 * n)),
        input_output_aliases={i: i for i in range(2 * n)}, compiler_params=SPLIT_COPY,
    )(*slabs, *lands, send_sems, recv_sems, after)
    me = 2 * lax.axis_index("x") + lax.axis_index("y")
    return [_put_own(land, slab, me).reshape(N_CHIPS * slab.shape[0], slab.shape[1])
            for slab, land in zip(res[:n], res[n:])]


def _scatter_copies(part_refs, land_refs, send_sems, recv_sems):
    x, y, c, chips = _place()
    me = 4 * x + 2 * y + c
    sends, recvs = [], []
    for k, (part, land) in enumerate(zip(part_refs, land_refs)):
        for j, chip in enumerate(chips):
            for h in range(2):
                sends.append(pltpu.make_async_remote_copy(
                    src_ref=_half(part, _chip_of(chip), h), dst_ref=land.at[me],
                    send_sem=send_sems.at[7 * k + 2 * j + h], recv_sem=recv_sems.at[7 * k + 2 * j + c],
                    device_id=(*chip, h), device_id_type=MESH))
                recvs.append(pltpu.make_async_remote_copy(
                    src_ref=_half(part, _chip_of(chip), h), dst_ref=land.at[2 * _chip_of(chip) + h],
                    send_sem=send_sems.at[7 * k + 2 * j + h], recv_sem=recv_sems.at[7 * k + 2 * j + h],
                    device_id=(*chip, h), device_id_type=MESH))
        sends.append(pltpu.make_async_remote_copy(
            src_ref=_half(part, 2 * x + y, 1 - c), dst_ref=land.at[me], send_sem=send_sems.at[7 * k + 6],
            recv_sem=recv_sems.at[7 * k + 6], device_id=(x, y, 1 - c), device_id_type=MESH))
        recvs.append(pltpu.make_async_remote_copy(
            src_ref=_half(part, 2 * x + y, 1 - c), dst_ref=land.at[4 * x + 2 * y + 1 - c],
            send_sem=send_sems.at[7 * k + 6], recv_sem=recv_sems.at[7 * k + 6], device_id=(x, y, 1 - c),
            device_id_type=MESH))
    return sends, recvs


def _reduce_scatter_start(parts, name):
    n = len(parts)
    parts = [p.reshape(N_CHIPS, p.shape[0] // N_CHIPS, p.shape[1]) for p in parts]

    def body(*refs):
        part_refs, land_refs = refs[:n], refs[n:2 * n]
        send_sems, recv_sems = refs[2 * n:2 * n + 2]
        token = refs[-1]
        sends, _ = _scatter_copies(part_refs, land_refs, send_sems, recv_sems)
        for cp in sends:
            cp.start()
        token[...] = jnp.zeros_like(token)

    lands = [lax.empty((N_DEV, p.shape[1] // 2, p.shape[2]), p.dtype) for p in parts]
    hbm = lambda a: pltpu.HBM(a.shape, a.dtype)
    res = pl.pallas_call(
        body, name=name,
        out_shape=(pltpu.SemaphoreType.DMA((7 * n,)), pltpu.SemaphoreType.DMA((7 * n,)), *map(hbm, parts),
                   *map(hbm, lands), jax.ShapeDtypeStruct((8, LANES), F32)),
        in_specs=[ANY] * (2 * n), out_specs=(SEM, SEM, *([ANY] * (2 * n)), pl.BlockSpec(memory_space=pltpu.VMEM)),
        input_output_aliases={i: 2 + i for i in range(2 * n)}, compiler_params=SPLIT_COPY,
    )(*[pltpu.with_memory_space_constraint(a, pltpu.HBM) for a in parts + lands])
    return res[:-1], res[-1]


def _reduce_scatter_wait(handle, after, name):
    send_sems, recv_sems = handle[:2]
    n = (len(handle) - 2) // 2
    parts, lands = handle[2:2 + n], handle[2 + n:]

    def body(*refs):
        part_refs, land_refs = refs[:n], refs[n:2 * n]
        send_sems, recv_sems = refs[2 * n:2 * n + 2]
        sends, recvs = _scatter_copies(part_refs, land_refs, send_sems, recv_sems)
        for cp in sends:
            cp.wait_send()
        for cp in recvs:
            cp.wait_recv()

    hbm = lambda a: pltpu.HBM(a.shape, a.dtype)
    res = pl.pallas_call(
        body, name=name, out_shape=tuple(map(hbm, list(parts) + list(lands))),
        in_specs=[ANY] * (2 * n) + [SEM, SEM, ANY], out_specs=tuple([ANY] * (2 * n)),
        input_output_aliases={i: i for i in range(2 * n)}, compiler_params=SPLIT_COPY,
    )(*parts, *lands, send_sems, recv_sems, after)
    x, y, c = lax.axis_index("x"), lax.axis_index("y"), lax.axis_index("c")
    out = []
    for part, land in zip(res[:n], res[n:]):
        hr = land.shape[1]
        own = lax.dynamic_slice(part, (2 * x + y, c * hr, 0), (1, hr, part.shape[2]))[0]
        out.append(_put_own(land, own, 4 * x + 2 * y + c))
    return out


def _sum_shares(land, name):
    n, rows, cols = land.shape

    def body(l_ref, o_ref):
        acc = l_ref[0].astype(F32)
        for i in range(1, n):
            acc = acc + l_ref[i].astype(F32)
        o_ref[...] = acc

    return pl.pallas_call(
        body, name=name, grid=(1,), in_specs=[pl.BlockSpec((n, rows, cols), lambda i: (0, 0, 0))],
        out_specs=pl.BlockSpec((rows, cols), lambda i: (0, 0)), out_shape=jax.ShapeDtypeStruct((rows, cols), F32),
        compiler_params=_params(("arbitrary",)),
    )(land)


def _swap_halves(halves, name):
    n = len(halves)

    def body(*refs):
        in_refs, out_refs, send_sems, recv_sems = refs[:n], refs[n:2 * n], refs[2 * n], refs[2 * n + 1]
        x, y, c, _ = _place()
        copies = [pltpu.make_async_remote_copy(src_ref=i_ref, dst_ref=o_ref, send_sem=send_sems.at[k],
                                               recv_sem=recv_sems.at[k], device_id=(x, y, 1 - c), device_id_type=MESH)
                  for k, (i_ref, o_ref) in enumerate(zip(in_refs, out_refs))]
        for cp in copies:
            cp.start()
        for cp in copies:
            cp.wait()

    others = pl.pallas_call(
        body, name=name, out_shape=[jax.ShapeDtypeStruct(h.shape, h.dtype) for h in halves],
        in_specs=[ANY] * n, out_specs=[ANY] * n,
        scratch_shapes=[pltpu.SemaphoreType.DMA((n,)), pltpu.SemaphoreType.DMA((n,))],
    )(*halves)
    south = lax.axis_index("c") == 0
    return [jnp.concatenate([jnp.where(south, h, o), jnp.where(south, o, h)], axis=0) for h, o in zip(halves, others)]


def _small_all_reduce(block):
    rows = block.shape[0]

    def body(x_ref, all_ref, sum_ref, send_sems, recv_sems, local_sem):
        x, y, c, chips = _place()
        me, sibling = (x, y, c), (x, y, 1 - c)

        def blk(px, py, pc):
            return all_ref.at[pl.ds((4 * px + 2 * py + pc) * rows, rows), :]

        def copy(k, who, to, src=None):
            return pltpu.make_async_remote_copy(
                src_ref=blk(*who) if src is None else src, dst_ref=blk(*who), send_sem=send_sems.at[k],
                recv_sem=recv_sems.at[k], device_id=to, device_id_type=MESH)

        mine = pltpu.make_async_copy(x_ref, blk(*me), local_sem)
        mine.start()
        first = [copy(0, me, sibling, src=x_ref)]
        first += [copy(1 + j, me, (*chip, c), src=x_ref) for j, chip in enumerate(chips)]
        for cp in first:
            cp.start()
        passed = [copy(4 + j, (*chip, c), sibling) for j, chip in enumerate(chips)]
        for j, chip in enumerate(chips):
            copy(1 + j, (*chip, c), me).wait_recv()
            passed[j].start()
        copy(0, sibling, me).wait_recv()
        for j, chip in enumerate(chips):
            copy(4 + j, (*chip, 1 - c), me).wait_recv()
        for cp in first + passed:
            cp.wait_send()
        mine.wait()
        acc = all_ref[pl.ds(0, rows), :]
        for dev in range(1, N_DEV):
            acc = acc + all_ref[pl.ds(dev * rows, rows), :]
        sum_ref[...] = acc

    vmem = pl.BlockSpec(memory_space=pltpu.VMEM)
    return pl.pallas_call(
        body, name="small_all_reduce",
        out_shape=[jax.ShapeDtypeStruct((N_DEV * rows, D_MODEL), F32), jax.ShapeDtypeStruct((rows, D_MODEL), F32)],
        in_specs=[vmem], out_specs=[vmem, vmem],
        scratch_shapes=[pltpu.SemaphoreType.DMA((7,)), pltpu.SemaphoreType.DMA((7,)), pltpu.SemaphoreType.DMA],
    )(block)[1]


def _adamw(w, g, m, v, name):
    rows, cols = w.shape
    tm = rows
    if rows > 512:
        tm = next(t for t in range(512, 7, -8) if rows % t == 0)

    def body(w_ref, g_ref, m_ref, v_ref, d_ref, nm_ref, nv_ref):
        g_ = g_ref[...]
        m_ = ADAM_B1 * m_ref[...] + (1.0 - ADAM_B1) * g_
        v_ = ADAM_B2 * v_ref[...] + (1.0 - ADAM_B2) * (g_ * g_)
        m_hat = m_ / (1.0 - ADAM_B1 ** ADAM_STEP)
        v_hat = v_ / (1.0 - ADAM_B2 ** ADAM_STEP)
        d_ref[...] = -ADAM_LR * (m_hat / (jnp.sqrt(v_hat) + ADAM_EPS) + ADAM_WD * w_ref[...])
        nm_ref[...] = m_
        nv_ref[...] = v_

    spec = pl.BlockSpec((tm, cols), lambda i: (i, 0))
    return pl.pallas_call(
        body, name=name, grid=(rows // tm,), in_specs=[spec] * 4, out_specs=[spec] * 3,
        out_shape=[jax.ShapeDtypeStruct(w.shape, F32)] * 3, compiler_params=_params(("arbitrary",)),
    )(w, g, m, v)


def _pack_small(values):
    flat = jnp.concatenate([values[n].reshape(-1).astype(F32) for n in SMALL])
    return jnp.pad(flat, (0, SMALL_ROWS * D_MODEL - flat.shape[0])).reshape(SMALL_ROWS, D_MODEL)


def _unpack_small(block, shapes):
    flat = block.reshape(-1)
    out, lo = {}, 0
    for n in SMALL:
        out[n] = flat[lo:lo + SMALL_SIZES[n]].reshape(shapes[n])
        lo += SMALL_SIZES[n]
    return out


def _after(token, a):
    return a + token[:1, :1].astype(a.dtype)


def kernel(x, p, mix_norm_g, w_in, sgu_w, sgu_b, sgu_norm_g, out_norm_a, out_norm_b, w_out, ffn_norm_g, w_gate, w_up, w_down, ple_norm_g, w_ple_gate, w_ple_proj, final_norm_g, loss_target, m_mix_norm_g, m_w_in, m_sgu_w, m_sgu_b, m_sgu_norm_g, m_out_norm_a, m_out_norm_b, m_w_out, m_ffn_norm_g, m_w_gate, m_w_up, m_w_down, m_ple_norm_g, m_w_ple_gate, m_w_ple_proj, m_final_norm_g, v_mix_norm_g, v_w_in, v_sgu_w, v_sgu_b, v_sgu_norm_g, v_out_norm_a, v_out_norm_b, v_w_out, v_ffn_norm_g, v_w_gate, v_w_up, v_w_down, v_ple_norm_g, v_w_ple_gate, v_w_ple_proj, v_final_norm_g):
    given = dict(locals())
    drop_lead = lambda a, lead: a.reshape(a.shape[lead:])
    xs, ps, target = drop_lead(x, 1), drop_lead(p, 2), drop_lead(loss_target, 1)
    s = xs.shape[0]
    shard = lambda name: drop_lead(given[name], 1)

    def slab_of(name):
        local = shard(name).astype(MXU_DTYPE)
        return local.T if SLAB_IS_TRANSPOSED[name] else local

    w_in_t = _all_gather_now(slab_of('w_in'))
    later = ['w_out', 'w_gate', 'w_up', 'w_down', 'w_ple_gate', 'w_ple_proj']
    gather, token = _all_gather_start([slab_of(n) for n in later], w_in_t)

    cos_t, sin_t = _rope_tables(s)
    tril = jnp.tril(jnp.ones((CHUNK, CHUNK), F32))
    w_tril = (sgu_w.reshape(HEADS_A, CHUNK, CHUNK) * tril).astype(MXU_DTYPE)
    w_tril_t = jnp.swapaxes(w_tril, 1, 2)
    bias = jnp.repeat(sgu_b.reshape(HEADS_A, CHUNK).T, HEAD_DIM, axis=1)
    g = {n: given[n].reshape(1, -1) for n in SMALL if n not in ('sgu_w', 'sgu_b')}

    uv, q, k, v, hn1 = _in_fwd(xs, _after(token, g['mix_norm_g']), w_in_t.T, cos_t, sin_t)
    ya_n = _sgu_fwd(uv, w_tril, bias, g['sgu_norm_g'], g['out_norm_a'])
    state = None
    for i, dil in enumerate(DILATIONS):
        state = _attn_fwd_branch(q, k, v, state, dil, last=(i == len(DILATIONS) - 1))
    y_b, lse = state
    stacks = dict(zip(later, _all_gather_wait(gather, lse)))
    w_gate_t, w_up_t, w_pp_t = stacks['w_gate'], stacks['w_up'], stacks['w_ple_proj']
    h1, y_n = _out_fwd(ya_n, y_b, g['out_norm_b'], stacks['w_out'], xs)
    h2, gate, up, hn2 = _ffn_fwd(h1, g['ffn_norm_g'], w_gate_t.T, w_up_t.T, stacks['w_down'])
    loss, dh2, dz, dpp, hn3, d_ple_g, d_final_g = _ple_loss(
        h2, ps, target, g['ple_norm_g'], stacks['w_ple_gate'], stacks['w_ple_gate'].T, w_pp_t.T, g['final_norm_g'])

    share = {}
    share['w_ple_gate'], = _wgrad(hn3, [dz], "wgrad_ple_gate")
    share['w_ple_proj'], = _wgrad(ps, [dpp], "wgrad_ple_proj", transposed=True)
    scatter_1, token = _reduce_scatter_start([share['w_ple_gate'], share['w_ple_proj']], "reduce_scatter_start_1")
    dh1, act, dgate, dup, d_ffn_g = _ffn_bwd(dh2, h1, gate, up, _after(token, g['ffn_norm_g']), stacks['w_down'].T,
                                             w_gate_t, w_up_t)
    share['w_down'], = _wgrad(act, [dh2], "wgrad_down")
    share['w_gate'], share['w_up'] = _wgrad(hn2, [dgate, dup], "wgrad_gate_up", transposed=True)
    scatter_2, token = _reduce_scatter_start([share['w_down'], share['w_gate'], share['w_up']],
                                             "reduce_scatter_start_2")
    dya_n, dyb, d_out_b = _out_bwd(dh1, y_b, _after(token, g['out_norm_b']), stacks['w_out'].T)
    share['w_out'], = _wgrad(y_n, [dh1], "wgrad_out")
    scatter_3, token = _reduce_scatter_start([share['w_out']], "reduce_scatter_start_3")
    grads = None
    for dil in DILATIONS:
        grads = _attn_bwd_branch(q, k, v, dyb, y_b, lse, grads, dil)
    duv, d_sgu_w, d_sgu_b, d_sgu_g, d_out_a = _sgu_bwd(uv, dya_n, w_tril, w_tril_t, bias,
                                                       _after(token, g['sgu_norm_g']), g['out_norm_a'])
    dproj = _in_bwd_proj(duv, grads[0], grads[1], grads[2], cos_t, sin_t)
    share['w_in'], = _wgrad(hn1, [dproj], "wgrad_in", transposed=True)
    scatter_4, token = _reduce_scatter_start([share['w_in']], "reduce_scatter_start_4")
    grad_x, d_mix_g = _in_bwd_x(dproj, w_in_t, xs, _after(token, g['mix_norm_g']), dh1)

    grads, deltas, new_m, new_v = {}, {}, {}, {}
    add_lead = lambda a: a.reshape((1,) + a.shape)

    def finish(names, handles, after, tag):
        landed = []
        for i, handle in enumerate(handles):
            landed += _reduce_scatter_wait(handle, after, "reduce_scatter_wait_%s%d" % (tag, i))
        halves = [_sum_shares(land, "sum_shares_" + n) for n, land in zip(names, landed)]
        for n, slab in zip(names, _swap_halves(halves, "swap_halves_" + tag)):
            turn = (lambda a: a.T) if SLAB_IS_TRANSPOSED[n] else (lambda a: a)
            d, nm, nv = _adamw(turn(shard(n)), slab, turn(shard("m_" + n)), turn(shard("v_" + n)), "adamw_" + n)
            grads[n], deltas[n], new_m[n], new_v[n] = (add_lead(turn(a)) for a in (slab, d, nm, nv))

    finish(['w_ple_gate', 'w_ple_proj', 'w_down', 'w_gate', 'w_up', 'w_out'], [scatter_1, scatter_2, scatter_3], token,
           "early")

    gs = {'mix_norm_g': d_mix_g, 'sgu_w': d_sgu_w, 'sgu_b': d_sgu_b[:, :HEADS_A].T, 'sgu_norm_g': d_sgu_g,
          'out_norm_a': d_out_a, 'out_norm_b': d_out_b, 'ffn_norm_g': d_ffn_g, 'ple_norm_g': d_ple_g,
          'final_norm_g': d_final_g}
    gs_block = _pack_small(gs).at[SMALL_ROWS - 1, 0].set(loss[0, 0])
    small_sum = _small_all_reduce(gs_block)
    loss_out = small_sum[SMALL_ROWS - 1, 0]
    small_shapes = {n: given[n].shape for n in SMALL}
    finish(['w_in'], [scatter_4], small_sum, "last")

    small = {n: given[n] for n in SMALL}
    d, nm, nv = _adamw(_pack_small(small), small_sum, _pack_small({n: given["m_" + n] for n in SMALL}),
                       _pack_small({n: given["v_" + n] for n in SMALL}), "adamw_small")
    for res, blk in ((grads, small_sum), (deltas, d), (new_m, nm), (new_v, nv)):
        res.update(_unpack_small(blk, small_shapes))

    outs = [loss_out, add_lead(grad_x)]
    for res in (grads, deltas, new_m, new_v):
        outs += [res[n] for n in WEIGHT_NAMES]
    return tuple(outs)
```

```python
import functools
import math

import jax
import jax.numpy as jnp
import numpy as np
from jax import lax
from jax.experimental import pallas as pl
from jax.experimental.pallas import tpu as pltpu

F32 = jnp.float32
MXU_DTYPE = jnp.bfloat16

D_MODEL = 1024
HEAD_DIM = 64
HEADS_A = 4
HEADS_B = 12
WIDTH_A = HEADS_A * HEAD_DIM
WIDTH_B = HEADS_B * HEAD_DIM
CHUNK = 128
BLOCK = 128
DILATIONS = (1, 4, 16)
ROPE_THETA = 10000.0
D_FF = 2816
FF_HALF = D_FF // 2
PLE_DIM = 256
IN_COLS = 2 * WIDTH_A + 3 * WIDTH_B
EPS = 1e-6
LANES = 128
N_CHIPS = 4
N_DEV = 8

ADAM_LR = 0.001
ADAM_B1 = 0.9
ADAM_B2 = 0.999
ADAM_EPS = 1e-08
ADAM_WD = 0.01
ADAM_STEP = 10

VMEM_LIMIT = 56 * 1024 * 1024

WEIGHT_NAMES = ['mix_norm_g', 'w_in', 'sgu_w', 'sgu_b', 'sgu_norm_g', 'out_norm_a', 'out_norm_b', 'w_out',
                'ffn_norm_g', 'w_gate', 'w_up', 'w_down', 'ple_norm_g', 'w_ple_gate', 'w_ple_proj', 'final_norm_g']
SHARDED = ['w_in', 'w_out', 'w_gate', 'w_up', 'w_down', 'w_ple_gate', 'w_ple_proj']
SMALL = ['mix_norm_g', 'sgu_w', 'sgu_b', 'sgu_norm_g', 'out_norm_a', 'out_norm_b', 'ffn_norm_g', 'ple_norm_g',
         'final_norm_g']
SMALL_SIZES = {'mix_norm_g': 1024, 'sgu_w': 65536, 'sgu_b': 512, 'sgu_norm_g': 256, 'out_norm_a': 256,
               'out_norm_b': 768, 'ffn_norm_g': 1024, 'ple_norm_g': 1024, 'final_norm_g': 1024}
SMALL_ROWS = 72


def _params(semantics=None):
    return pltpu.CompilerParams(dimension_semantics=semantics, vmem_limit_bytes=VMEM_LIMIT)


def _full(shape):
    nd = len(shape)
    return pl.BlockSpec(shape, lambda i: (0,) * nd)


def _rows(tm, width):
    return pl.BlockSpec((tm, width), lambda i: (i, 0))


def _rms_stats(x):
    r = lax.rsqrt(jnp.mean(x * x, axis=-1, keepdims=True) + EPS)
    return x * r, r


def _rms_bwd(dn, n, r):
    return r * (dn - n * jnp.mean(dn * n, axis=-1, keepdims=True))


def _dot(a, b):
    return jnp.dot(a, b, preferred_element_type=F32)


def _dot_nt(a, b):
    return lax.dot_general(a, b, (((1,), (1,)), ((), ())), preferred_element_type=F32)


def _dot_tn(a, b):
    return lax.dot_general(a, b, (((0,), (0,)), ((), ())), preferred_element_type=F32)


def _gelu_parts(x):
    c = math.sqrt(2.0 / math.pi)
    t = jnp.tanh(c * (x + 0.044715 * x * x * x))
    return 0.5 * x * (1.0 + t), t


def _gelu_grad(x, t):
    c = math.sqrt(2.0 / math.pi)
    return 0.5 * (1.0 + t) + 0.5 * x * (1.0 - t * t) * c * (1.0 + 3.0 * 0.044715 * x * x)


def _half_masks(dtype):
    lane = lax.broadcasted_iota(jnp.int32, (BLOCK, LANES), 1)
    lo = (lane < HEAD_DIM).astype(F32)
    return lo.astype(dtype), (1.0 - lo).astype(dtype)


def _rope_partner(t):
    lane = lax.broadcasted_iota(jnp.int32, t.shape, 1)
    first_half = (lane % HEAD_DIM) < (HEAD_DIM // 2)
    return jnp.where(first_half, pltpu.roll(t, LANES - HEAD_DIM // 2, 1), pltpu.roll(t, HEAD_DIM // 2, 1))


L_BLOCK = 256
L_GROUP = 16


def _store_l256(scr, out_ref, cols, value):
    tm = value.shape[0]
    scr[...] = value
    for blk in range(tm // L_BLOCK):
        for r in range(L_GROUP):
            lo = blk * L_BLOCK + r * L_GROUP
            piece = scr[pl.ds(blk * L_BLOCK + r, L_GROUP, stride=L_GROUP), :]
            out_ref[lo:lo + L_GROUP, cols] = piece.astype(out_ref.dtype)


def _load_l256(col_refs, tm):
    cols = []
    for ref in col_refs:
        pieces = [ref[pl.ds(blk * L_BLOCK + i, L_GROUP, stride=L_GROUP), :]
                  for blk in range(tm // L_BLOCK) for i in range(L_GROUP)]
        cols.append(jnp.concatenate(pieces, axis=0))
    return jnp.concatenate(cols, axis=1)


def _col_specs(tm, width):
    return [pl.BlockSpec((tm, LANES), lambda i, j=j: (i, j)) for j in range(width // LANES)]


def _in_fwd(x, g_mix, w_in, cos_t, sin_t):
    s = x.shape[0]
    tm = 512

    def body(x_ref, g_ref, w_ref, cos_ref, sin_ref, uv_ref, q_ref, k_ref, v_ref, hn_ref, scr):
        n, _ = _rms_stats(x_ref[...])
        hn = (n * g_ref[...]).astype(MXU_DTYPE)
        hn_ref[...] = hn
        proj = _dot(hn, w_ref[...])
        uv_ref[...] = proj[:, :2 * WIDTH_A]
        cos = cos_ref[...]
        sin = sin_ref[...]
        for i in range(WIDTH_B // LANES):
            lo = 2 * WIDTH_A + i * LANES
            tq = proj[:, lo:lo + LANES]
            tk = proj[:, lo + WIDTH_B:lo + WIDTH_B + LANES]
            tv = proj[:, lo + 2 * WIDTH_B:lo + 2 * WIDTH_B + LANES]
            cols = slice(i * LANES, (i + 1) * LANES)
            _store_l256(scr, q_ref, cols, (tq * cos + _rope_partner(tq) * sin) * (HEAD_DIM ** -0.5))
            _store_l256(scr, k_ref, cols, tk * cos + _rope_partner(tk) * sin)
            _store_l256(scr, v_ref, cols, tv)

    return pl.pallas_call(
        body, name="in_fwd", grid=(s // tm,), scratch_shapes=[pltpu.VMEM((tm, LANES), F32)],
        in_specs=[_rows(tm, D_MODEL), _full((1, D_MODEL)), _full((D_MODEL, IN_COLS)), _rows(tm, LANES),
                  _rows(tm, LANES)],
        out_specs=[_rows(tm, 2 * WIDTH_A), _rows(tm, WIDTH_B), _rows(tm, WIDTH_B), _rows(tm, WIDTH_B),
                   _rows(tm, D_MODEL)],
        out_shape=[jax.ShapeDtypeStruct((s, 2 * WIDTH_A), F32), jax.ShapeDtypeStruct((s, WIDTH_B), MXU_DTYPE),
                   jax.ShapeDtypeStruct((s, WIDTH_B), MXU_DTYPE), jax.ShapeDtypeStruct((s, WIDTH_B), MXU_DTYPE),
                   jax.ShapeDtypeStruct((s, D_MODEL), MXU_DTYPE)],
        compiler_params=_params(("arbitrary",)),
    )(x, g_mix, w_in, cos_t, sin_t)


class _Branch:
    def __init__(self, dil, s, qn=BLOCK):
        self.dil = dil
        i = np.arange(L_GROUP)
        if dil == 16:
            nblk = qn // 16
            self.grid = (16, s // (L_BLOCK * nblk))
            self.shape = (nblk, 1, 1, L_GROUP)
            self.index = lambda r, n: (n, r // 4, r % 4, 0, 0)
            pos = (np.arange(nblk)[:, None] * 16 + i[None, :]).reshape(-1)
        elif dil == 4:
            nblk = qn // 64
            self.grid = (4, s // (L_BLOCK * nblk))
            self.shape = (nblk, 4, 1, L_GROUP)
            self.index = lambda r, n: (n, 0, r, 0, 0)
            pos = (np.arange(nblk)[:, None, None] * 64 + np.arange(4)[None, :, None]
                   + 4 * i[None, None, :]).reshape(-1)
        else:
            self.grid = (1, s // L_BLOCK)
            self.shape = (1, 4, 4, L_GROUP)
            self.index = lambda r, n: (n, 0, 0, 0, 0)
            pos = (np.arange(16)[:, None] + 16 * i[None, :]).reshape(-1)
        self.qn = pos.shape[0]
        self.nb = self.grid[1]
        dist = pos[:, None] - np.concatenate([pos - self.qn, pos])[None, :]
        band = (dist >= 0) & (dist <= BLOCK)
        start = band & (np.arange(2 * self.qn)[None, :] >= self.qn)
        self.bias = np.where(np.stack([band, start]), 0.0, -np.inf).astype(np.float32)

    def view(self, a):
        return a.reshape(a.shape[0] // L_BLOCK, 4, 4, L_GROUP, a.shape[1])

    def spec(self, w, step=lambda n: n):
        return pl.BlockSpec(self.shape + (w,), lambda r, n: self.index(r, step(n)))

    def bias_spec(self, step=lambda n: n):
        return pl.BlockSpec((1, self.qn, 2 * self.qn), lambda r, n: (jnp.where(step(n) == 0, 1, 0), 0, 0))

    def load(self, ref, cols=slice(None)):
        x = ref[:, :, :, :, cols]
        return x.reshape(self.qn, x.shape[-1])

    def store(self, ref, cols, value):
        ref[:, :, :, :, cols] = value.reshape(self.shape + (value.shape[-1],))


def _attn_fwd_branch(q, k, v, dil):
    s = q.shape[0]
    br = _Branch(dil, s)
    qn = br.qn

    def body(bias_ref, q_ref, kc_ref, kp_ref, vc_ref, vp_ref, o_ref, lse_ref):
        bias2 = jnp.concatenate([bias_ref[0], bias_ref[0]], axis=0)
        lo = lax.broadcasted_iota(jnp.int32, (qn, LANES), 1) < HEAD_DIM
        mask_lo = lo.astype(F32).astype(MXU_DTYPE)
        for hp in range(HEADS_B // 2):
            cols = slice(hp * LANES, (hp + 1) * LANES)
            qp = br.load(q_ref, cols)
            kcat = jnp.concatenate([br.load(kp_ref, cols), br.load(kc_ref, cols)], axis=0)
            vcat = jnp.concatenate([br.load(vp_ref, cols), br.load(vc_ref, cols)], axis=0)
            sc = _dot_nt(jnp.concatenate([qp * mask_lo, qp * (1 - mask_lo)], axis=0), kcat) + bias2
            m = jnp.max(sc, axis=1, keepdims=True)
            p = jnp.exp(sc - m)
            l = jnp.sum(p, axis=1, keepdims=True)
            out = _dot(p.astype(MXU_DTYPE), vcat) / l
            lse = m + jnp.log(l)
            br.store(o_ref, cols, jnp.where(lo, out[:qn], out[qn:]))
            br.store(lse_ref, cols, jnp.where(lo, lse[:qn], lse[qn:]))

    before = lambda n: jnp.maximum(n - 1, 0)
    res = pl.pallas_call(
        body, name="attn_fwd_d%d" % dil, grid=br.grid,
        in_specs=[br.bias_spec(), br.spec(WIDTH_B), br.spec(WIDTH_B), br.spec(WIDTH_B, before), br.spec(WIDTH_B),
                  br.spec(WIDTH_B, before)],
        out_specs=[br.spec(WIDTH_B), br.spec(WIDTH_B)],
        out_shape=[jax.ShapeDtypeStruct((s // L_BLOCK, 4, 4, L_GROUP, WIDTH_B), F32)] * 2,
        compiler_params=_params(("arbitrary", "arbitrary")),
    )(jnp.asarray(br.bias), br.view(q), br.view(k), br.view(k), br.view(v), br.view(v))
    return tuple(a.reshape(s, WIDTH_B) for a in res)


def _attn_merge(outs, lses):
    s = outs[0].shape[0]
    tm = 512
    nbr = len(outs)

    def body(*refs):
        o_refs, l_refs, y_ref, lse_ref = refs[:nbr], refs[nbr:2 * nbr], refs[2 * nbr], refs[2 * nbr + 1]
        ls = [r[...] for r in l_refs]
        top = functools.reduce(jnp.maximum, ls)
        ws = [jnp.exp(l - top) for l in ls]
        den = functools.reduce(jnp.add, ws)
        num = functools.reduce(jnp.add, [w * r[...] for w, r in zip(ws, o_refs)])
        y_ref[...] = num / den
        lse_ref[...] = top + jnp.log(den)

    return pl.pallas_call(
        body, name="attn_merge", grid=(s // tm,), in_specs=[_rows(tm, WIDTH_B)] * (2 * nbr),
        out_specs=[_rows(tm, WIDTH_B)] * 2, out_shape=[jax.ShapeDtypeStruct((s, WIDTH_B), F32)] * 2,
        compiler_params=_params(("arbitrary",)),
    )(*outs, *lses)


def _sgu_forward_tile(uv, w_ref, bias, g_sgu):
    tm = uv.shape[0]
    u = uv[:, :WIDTH_A]
    v = uv[:, WIDTH_A:]
    ug, tu = _gelu_parts(u)
    vg, tv = _gelu_parts(v)
    mu = jnp.mean(vg, axis=-1, keepdims=True)
    vc = vg - mu
    rs = lax.rsqrt(jnp.mean(vc * vc, axis=-1, keepdims=True) + EPS)
    vhat = vc * rs
    vn = (vhat * g_sgu).astype(MXU_DTYPE)
    masks = _half_masks(MXU_DTYPE)
    chunks = []
    for c in range(tm // CHUNK):
        rows = slice(c * CHUNK, (c + 1) * CHUNK)
        groups = []
        for gp in range(2):
            vn_g = vn[rows, gp * LANES:(gp + 1) * LANES]
            groups.append(_dot(w_ref[2 * gp], vn_g * masks[0]) + _dot(w_ref[2 * gp + 1], vn_g * masks[1]))
        chunks.append(jnp.concatenate(groups, axis=1) + bias)
    mixed = jnp.concatenate(chunks, axis=0)
    return dict(u=u, v=v, ug=ug, tu=tu, tv=tv, rs=rs, vhat=vhat, vn=vn, mixed=mixed, ya=ug * mixed)


def _sgu_fwd(uv, w_tril, bias, g_sgu, g_a):
    s = uv.shape[0]
    tm = 512

    def body(uv_ref, w_ref, b_ref, gs_ref, ga_ref, o_ref):
        t = _sgu_forward_tile(uv_ref[...], w_ref, b_ref[...], gs_ref[...])
        n, _ = _rms_stats(t['ya'])
        o_ref[...] = (n * ga_ref[...]).astype(MXU_DTYPE)

    return pl.pallas_call(
        body, name="sgu_fwd", grid=(s // tm,),
        in_specs=[_rows(tm, 2 * WIDTH_A), _full((HEADS_A, CHUNK, CHUNK)), _full((CHUNK, WIDTH_A)),
                  _full((1, WIDTH_A)), _full((1, WIDTH_A))],
        out_specs=_rows(tm, WIDTH_A), out_shape=jax.ShapeDtypeStruct((s, WIDTH_A), MXU_DTYPE),
        compiler_params=_params(("arbitrary",)),
    )(uv, w_tril, bias, g_sgu, g_a)


def _out_fwd(ya_n, y_b, g_b, w_out, x):
    s = x.shape[0]
    tm = 512
    nc = WIDTH_B // LANES

    def body(ya_ref, *refs):
        yb_refs = refs[:nc]
        g_ref, w_ref, x_ref, h_ref, yn_ref = refs[nc:]
        n, _ = _rms_stats(_load_l256(yb_refs, tm))
        yn = jnp.concatenate([ya_ref[...], (n * g_ref[...]).astype(MXU_DTYPE)], axis=1)
        yn_ref[...] = yn
        h_ref[...] = x_ref[...] + _dot(yn, w_ref[...])

    return pl.pallas_call(
        body, name="out_fwd", grid=(s // tm,),
        in_specs=[_rows(tm, WIDTH_A)] + _col_specs(tm, WIDTH_B) + [_full((1, WIDTH_B)), _full((D_MODEL, D_MODEL)),
                                                                 _rows(tm, D_MODEL)],
        out_specs=[_rows(tm, D_MODEL), _rows(tm, D_MODEL)],
        out_shape=[jax.ShapeDtypeStruct((s, D_MODEL), F32), jax.ShapeDtypeStruct((s, D_MODEL), MXU_DTYPE)],
        compiler_params=_params(("arbitrary",)),
    )(ya_n, *([y_b] * nc), g_b, w_out, x)


def _ffn_fwd(h1, g_ffn, w_gate, w_up, w_down):
    s = h1.shape[0]
    tm = 256

    def body(h_ref, g_ref, wg_ref, wu_ref, wd_ref, o_ref, gate_ref, up_ref, hn_ref):
        h = h_ref[...]
        n, _ = _rms_stats(h)
        hn = (n * g_ref[...]).astype(MXU_DTYPE)
        hn_ref[...] = hn
        out = h
        for c in range(2):
            cols = slice(c * FF_HALF, (c + 1) * FF_HALF)
            gate = _dot(hn, wg_ref[:, cols])
            up = _dot(hn, wu_ref[:, cols])
            gate_ref[:, cols] = gate.astype(MXU_DTYPE)
            up_ref[:, cols] = up.astype(MXU_DTYPE)
            act = (gate * jax.nn.sigmoid(gate) * up).astype(MXU_DTYPE)
            out = out + _dot(act, wd_ref[cols, :])
        o_ref[...] = out

    return pl.pallas_call(
        body, name="ffn_fwd", grid=(s // tm,),
        in_specs=[_rows(tm, D_MODEL), _full((1, D_MODEL)), _full((D_MODEL, D_FF)), _full((D_MODEL, D_FF)),
                  _full((D_FF, D_MODEL))],
        out_specs=[_rows(tm, D_MODEL), _rows(tm, D_FF), _rows(tm, D_FF), _rows(tm, D_MODEL)],
        out_shape=[jax.ShapeDtypeStruct((s, D_MODEL), F32), jax.ShapeDtypeStruct((s, D_FF), MXU_DTYPE),
                   jax.ShapeDtypeStruct((s, D_FF), MXU_DTYPE), jax.ShapeDtypeStruct((s, D_MODEL), MXU_DTYPE)],
        compiler_params=_params(("arbitrary",)),
    )(h1, g_ffn, w_gate, w_up, w_down)


def _ple_loss(h2, p, target, g_ple, w_pg, w_pg_t, w_pp, g_final):
    s = h2.shape[0]
    tm = 256

    def body(h_ref, p_ref, t_ref, gp_ref, wg_ref, wgt_ref, wp_ref, gf_ref,
             loss_ref, dh_ref, dz_ref, dpp_ref, hn_ref, dgp_ref, dgf_ref):
        @pl.when(pl.program_id(0) == 0)
        def _():
            loss_ref[...] = jnp.zeros_like(loss_ref)
            dgp_ref[...] = jnp.zeros_like(dgp_ref)
            dgf_ref[...] = jnp.zeros_like(dgf_ref)

        h2t = h_ref[...]
        n2, r2 = _rms_stats(h2t)
        hn = (n2 * gp_ref[...]).astype(MXU_DTYPE)
        hn_ref[...] = hn
        gate = jax.nn.sigmoid(_dot(hn, wg_ref[...]))
        pp = _dot(p_ref[...].astype(MXU_DTYPE), wp_ref[...])
        h3 = h2t + gate * pp
        n3, r3 = _rms_stats(h3)
        diff = n3 * gf_ref[...] - t_ref[...]
        loss_ref[...] += jnp.full(loss_ref.shape, 0.5 * jnp.sum(diff * diff) / D_MODEL, F32)
        dy = diff * (1.0 / D_MODEL)
        dgf_ref[...] += jnp.sum(dy * n3, axis=0, keepdims=True)
        dh3 = _rms_bwd(dy * gf_ref[...], n3, r3)
        dpp_ref[...] = (dh3 * gate).astype(MXU_DTYPE)
        dz = (dh3 * pp * gate * (1.0 - gate)).astype(MXU_DTYPE)
        dz_ref[...] = dz
        dhn = _dot(dz, wgt_ref[...])
        dgp_ref[...] += jnp.sum(dhn * n2, axis=0, keepdims=True)
        dh_ref[...] = dh3 + _rms_bwd(dhn * gp_ref[...], n2, r2)

    return pl.pallas_call(
        body, name="ple_loss", grid=(s // tm,),
        in_specs=[_rows(tm, D_MODEL), _rows(tm, PLE_DIM), _rows(tm, D_MODEL), _full((1, D_MODEL)),
                  _full((D_MODEL, D_MODEL)), _full((D_MODEL, D_MODEL)), _full((PLE_DIM, D_MODEL)),
                  _full((1, D_MODEL))],
        out_specs=[_full((1, LANES)), _rows(tm, D_MODEL), _rows(tm, D_MODEL), _rows(tm, D_MODEL),
                   _rows(tm, D_MODEL), _full((1, D_MODEL)), _full((1, D_MODEL))],
        out_shape=[jax.ShapeDtypeStruct((1, LANES), F32), jax.ShapeDtypeStruct((s, D_MODEL), F32),
                   jax.ShapeDtypeStruct((s, D_MODEL), MXU_DTYPE), jax.ShapeDtypeStruct((s, D_MODEL), MXU_DTYPE),
                   jax.ShapeDtypeStruct((s, D_MODEL), MXU_DTYPE), jax.ShapeDtypeStruct((1, D_MODEL), F32),
                   jax.ShapeDtypeStruct((1, D_MODEL), F32)],
        compiler_params=_params(("arbitrary",)),
    )(h2, p, target, g_ple, w_pg, w_pg_t, w_pp, g_final)


def _ffn_bwd(dh2, h1, gate, up, g_ffn, w_down_t, w_gate_t, w_up_t):
    s = h1.shape[0]
    tm = 256

    def body(dh_ref, h_ref, gate_ref, up_ref, g_ref, wdt_ref, wgt_ref, wut_ref,
             o_ref, act_ref, dg_ref, du_ref, dgn_ref):
        @pl.when(pl.program_id(0) == 0)
        def _():
            dgn_ref[...] = jnp.zeros_like(dgn_ref)

        dh = dh_ref[...]
        dhb = dh.astype(MXU_DTYPE)
        dhn = jnp.zeros((tm, D_MODEL), F32)
        for c in range(2):
            cols = slice(c * FF_HALF, (c + 1) * FF_HALF)
            dact = _dot(dhb, wdt_ref[:, cols])
            g = gate_ref[:, cols].astype(F32)
            u = up_ref[:, cols].astype(F32)
            sg = jax.nn.sigmoid(g)
            silu = g * sg
            act_ref[:, cols] = (silu * u).astype(MXU_DTYPE)
            du = (dact * silu).astype(MXU_DTYPE)
            dg = (dact * u * sg * (1.0 + g * (1.0 - sg))).astype(MXU_DTYPE)
            du_ref[:, cols] = du
            dg_ref[:, cols] = dg
            dhn = dhn + _dot(dg, wgt_ref[cols, :]) + _dot(du, wut_ref[cols, :])
        n, r = _rms_stats(h_ref[...])
        dgn_ref[...] += jnp.sum(dhn * n, axis=0, keepdims=True)
        o_ref[...] = dh + _rms_bwd(dhn * g_ref[...], n, r)

    return pl.pallas_call(
        body, name="ffn_bwd", grid=(s // tm,),
        in_specs=[_rows(tm, D_MODEL), _rows(tm, D_MODEL), _rows(tm, D_FF), _rows(tm, D_FF), _full((1, D_MODEL)),
                  _full((D_MODEL, D_FF)), _full((D_FF, D_MODEL)), _full((D_FF, D_MODEL))],
        out_specs=[_rows(tm, D_MODEL), _rows(tm, D_FF), _rows(tm, D_FF), _rows(tm, D_FF), _full((1, D_MODEL))],
        out_shape=[jax.ShapeDtypeStruct((s, D_MODEL), F32), jax.ShapeDtypeStruct((s, D_FF), MXU_DTYPE),
                   jax.ShapeDtypeStruct((s, D_FF), MXU_DTYPE), jax.ShapeDtypeStruct((s, D_FF), MXU_DTYPE),
                   jax.ShapeDtypeStruct((1, D_MODEL), F32)],
        compiler_params=_params(("arbitrary",)),
    )(dh2, h1, gate, up, g_ffn, w_down_t, w_gate_t, w_up_t)


def _out_bwd(dh1, y_b, g_b, w_out_t):
    s = dh1.shape[0]
    tm = 512
    nc = WIDTH_B // LANES

    def body(dh_ref, *refs):
        yb_refs = refs[:nc]
        g_ref, wt_ref, dya_ref, dyb_ref, dg_ref, scr = refs[nc:]

        @pl.when(pl.program_id(0) == 0)
        def _():
            dg_ref[...] = jnp.zeros_like(dg_ref)

        dy = _dot(dh_ref[...].astype(MXU_DTYPE), wt_ref[...])
        dya_ref[...] = dy[:, :WIDTH_A]
        dyb = dy[:, WIDTH_A:]
        n, r = _rms_stats(_load_l256(yb_refs, tm))
        dg_ref[...] += jnp.sum(dyb * n, axis=0, keepdims=True)
        dyb_in = _rms_bwd(dyb * g_ref[...], n, r)
        for j in range(nc):
            cols = slice(j * LANES, (j + 1) * LANES)
            _store_l256(scr, dyb_ref, cols, dyb_in[:, cols])

    return pl.pallas_call(
        body, name="out_bwd", grid=(s // tm,), scratch_shapes=[pltpu.VMEM((tm, LANES), F32)],
        in_specs=[_rows(tm, D_MODEL)] + _col_specs(tm, WIDTH_B) + [_full((1, WIDTH_B)), _full((D_MODEL, D_MODEL))],
        out_specs=[_rows(tm, WIDTH_A), _rows(tm, WIDTH_B), _full((1, WIDTH_B))],
        out_shape=[jax.ShapeDtypeStruct((s, WIDTH_A), F32), jax.ShapeDtypeStruct((s, WIDTH_B), F32),
                   jax.ShapeDtypeStruct((1, WIDTH_B), F32)],
        compiler_params=_params(("arbitrary",)),
    )(dh1, *([y_b] * nc), g_b, w_out_t)


def _attn_bwd_branch(q, k, v, do, o, lse, grads, dil):
    s = q.shape[0]
    br = _Branch(dil, s)
    qn, nb = br.qn, br.nb
    first = grads is None

    def body(*refs):
        bias_ref, q_ref, kc_ref, kp_ref, vc_ref, vp_ref, do_ref, o_ref, lse_ref = refs[:9]
        if first:
            rest = refs[9:]
        else:
            dq_in, dk_in, dv_in = refs[9:12]
            rest = refs[12:]
        dq_ref, dk_ref, dv_ref, dk_carry, dv_carry = rest
        n = pl.program_id(1)

        @pl.when(n == 0)
        def _():
            dk_carry[...] = jnp.zeros_like(dk_carry)
            dv_carry[...] = jnp.zeros_like(dv_carry)

        @pl.when(n < nb)
        def _():
            bias2 = jnp.concatenate([bias_ref[0], bias_ref[0]], axis=0)
            lane = lax.broadcasted_iota(jnp.int32, (qn, LANES), 1)
            lo = lane < HEAD_DIM
            mask_f = lo.astype(F32)
            mask_lo = mask_f.astype(MXU_DTYPE)
            for hp in range(HEADS_B // 2):
                cols = slice(hp * LANES, (hp + 1) * LANES)
                h0, h1 = 2 * hp, 2 * hp + 1
                qp = br.load(q_ref, cols)
                kcat = jnp.concatenate([br.load(kp_ref, cols), br.load(kc_ref, cols)], axis=0)
                vcat = jnp.concatenate([br.load(vp_ref, cols), br.load(vc_ref, cols)], axis=0)
                dop = br.load(do_ref, cols)
                prod = dop * br.load(o_ref, cols)
                prod_lo = prod * mask_f
                delta = jnp.concatenate([jnp.sum(prod_lo, axis=1, keepdims=True),
                                         jnp.sum(prod - prod_lo, axis=1, keepdims=True)], axis=0)
                qs = jnp.concatenate([qp * mask_lo, qp * (1 - mask_lo)], axis=0)
                dos = jnp.concatenate([dop * mask_f, dop * (1.0 - mask_f)], axis=0).astype(MXU_DTYPE)
                lse = br.load(lse_ref, cols)
                lse2 = jnp.concatenate([lse[:, :1], lse[:, HEAD_DIM:HEAD_DIM + 1]], axis=0)
                p = jnp.exp(_dot_nt(qs, kcat) + bias2 - lse2)
                ds = (p * (_dot_nt(dos, vcat) - delta)).astype(MXU_DTYPE)
                dvc = _dot_tn(p.astype(MXU_DTYPE), dos)
                dkc = _dot_tn(ds, qs)
                dq2 = _dot(ds, kcat)
                dq = jnp.where(lo, dq2[:qn], dq2[qn:])
                dk_prev = dk_carry[:, cols] + dkc[:qn]
                dv_prev = dv_carry[:, cols] + dvc[:qn]
                if not first:
                    dq = dq + br.load(dq_in, cols)
                    dk_prev = dk_prev + br.load(dk_in, cols)
                    dv_prev = dv_prev + br.load(dv_in, cols)
                br.store(dq_ref, cols, dq)
                br.store(dk_ref, cols, dk_prev)
                br.store(dv_ref, cols, dv_prev)
                dk_carry[:, cols] = dkc[qn:]
                dv_carry[:, cols] = dvc[qn:]

        @pl.when(n == nb)
        def _():
            dk_last = dk_carry[...]
            dv_last = dv_carry[...]
            if not first:
                dk_last = dk_last + br.load(dk_in)
                dv_last = dv_last + br.load(dv_in)
            br.store(dk_ref, slice(None), dk_last)
            br.store(dv_ref, slice(None), dv_last)

    cur = lambda n: jnp.minimum(n, nb - 1)
    before = lambda n: jnp.maximum(cur(n) - 1, 0)
    late = lambda n: jnp.maximum(n - 1, 0)
    in_specs = [br.bias_spec(cur), br.spec(WIDTH_B, cur), br.spec(WIDTH_B, cur), br.spec(WIDTH_B, before),
                br.spec(WIDTH_B, cur), br.spec(WIDTH_B, before), br.spec(WIDTH_B, cur), br.spec(WIDTH_B, cur),
                br.spec(WIDTH_B, cur)]
    args = [jnp.asarray(br.bias)] + [br.view(a) for a in (q, k, k, v, v, do, o, lse)]
    if not first:
        in_specs += [br.spec(WIDTH_B, cur), br.spec(WIDTH_B, late), br.spec(WIDTH_B, late)]
        args += [br.view(g) for g in grads]
    res = pl.pallas_call(
        body, name="attn_bwd_d%d" % dil, grid=(br.grid[0], nb + 1), in_specs=in_specs,
        out_specs=[br.spec(WIDTH_B, cur), br.spec(WIDTH_B, late), br.spec(WIDTH_B, late)],
        out_shape=[jax.ShapeDtypeStruct((s // L_BLOCK, 4, 4, L_GROUP, WIDTH_B), F32)] * 3,
        scratch_shapes=[pltpu.VMEM((qn, WIDTH_B), F32), pltpu.VMEM((qn, WIDTH_B), F32)],
        compiler_params=_params(("arbitrary", "arbitrary")),
    )(*args)
    return tuple(a.reshape(s, WIDTH_B) for a in res)


def _sgu_bwd(uv, dya_n, w_tril, w_tril_t, bias, g_sgu, g_a):
    s = uv.shape[0]
    tm = 512

    def body(uv_ref, dy_ref, w_ref, wt_ref, b_ref, gs_ref, ga_ref, duv_ref, dw_ref, db_ref, dgs_ref, dga_ref,
             db_acc):
        i = pl.program_id(0)

        @pl.when(i == 0)
        def _():
            dw_ref[...] = jnp.zeros_like(dw_ref)
            dgs_ref[...] = jnp.zeros_like(dgs_ref)
            dga_ref[...] = jnp.zeros_like(dga_ref)
            db_acc[...] = jnp.zeros_like(db_acc)

        t = _sgu_forward_tile(uv_ref[...], w_ref, b_ref[...], gs_ref[...])
        na, ra = _rms_stats(t['ya'])
        dyn = dy_ref[...]
        dga_ref[...] += jnp.sum(dyn * na, axis=0, keepdims=True)
        dya = _rms_bwd(dyn * ga_ref[...], na, ra)
        dug = dya * t['mixed']
        dmixed = dya * t['ug']
        dmb = dmixed.astype(MXU_DTYPE)
        masks = _half_masks(MXU_DTYPE)
        chunks = []
        db = jnp.zeros((CHUNK, WIDTH_A), F32)
        for c in range(tm // CHUNK):
            rows = slice(c * CHUNK, (c + 1) * CHUNK)
            db = db + dmixed[rows]
            groups = []
            for gp in range(2):
                cols = slice(gp * LANES, (gp + 1) * LANES)
                dm_g = dmb[rows, cols]
                vn_g = t['vn'][rows, cols]
                dvn_g = jnp.zeros((CHUNK, LANES), F32)
                for j in range(2):
                    dm_h = dm_g * masks[j]
                    dvn_g = dvn_g + _dot(wt_ref[2 * gp + j], dm_h)
                    dw_ref[2 * gp + j] += _dot_nt(dm_h, vn_g)
                groups.append(dvn_g)
            chunks.append(jnp.concatenate(groups, axis=1))
        db_acc[...] += db
        dvn = jnp.concatenate(chunks, axis=0)
        vhat = t['vhat']
        dgs_ref[...] += jnp.sum(dvn * vhat, axis=0, keepdims=True)
        dvh = dvn * gs_ref[...]
        dvg = t['rs'] * (dvh - jnp.mean(dvh, axis=-1, keepdims=True)
                         - vhat * jnp.mean(dvh * vhat, axis=-1, keepdims=True))
        duv_ref[:, :WIDTH_A] = (dug * _gelu_grad(t['u'], t['tu'])).astype(MXU_DTYPE)
        duv_ref[:, WIDTH_A:] = (dvg * _gelu_grad(t['v'], t['tv'])).astype(MXU_DTYPE)

        @pl.when(i == pl.num_programs(0) - 1)
        def _():
            lane_a = lax.broadcasted_iota(jnp.int32, (CHUNK, WIDTH_A), 1)
            lane = lax.broadcasted_iota(jnp.int32, (CHUNK, LANES), 1)
            acc = db_acc[...]
            out = jnp.zeros((CHUNK, LANES), F32)
            for h in range(HEADS_A):
                col = jnp.sum(jnp.where(lane_a // HEAD_DIM == h, acc, 0.0), axis=1, keepdims=True)
                out = jnp.where(lane == h, col, out)
            db_ref[...] = out
            causal = (lax.broadcasted_iota(jnp.int32, (CHUNK, CHUNK), 0)
                      >= lax.broadcasted_iota(jnp.int32, (CHUNK, CHUNK), 1))
            for h in range(HEADS_A):
                dw_ref[h] = jnp.where(causal, dw_ref[h], 0.0)

    return pl.pallas_call(
        body, name="sgu_bwd", grid=(s // tm,),
        in_specs=[_rows(tm, 2 * WIDTH_A), _rows(tm, WIDTH_A), _full((HEADS_A, CHUNK, CHUNK)),
                  _full((HEADS_A, CHUNK, CHUNK)), _full((CHUNK, WIDTH_A)), _full((1, WIDTH_A)),
                  _full((1, WIDTH_A))],
        out_specs=[_rows(tm, 2 * WIDTH_A), _full((HEADS_A, CHUNK, CHUNK)), _full((CHUNK, LANES)),
                   _full((1, WIDTH_A)), _full((1, WIDTH_A))],
        out_shape=[jax.ShapeDtypeStruct((s, 2 * WIDTH_A), MXU_DTYPE),
                   jax.ShapeDtypeStruct((HEADS_A, CHUNK, CHUNK), F32), jax.ShapeDtypeStruct((CHUNK, LANES), F32),
                   jax.ShapeDtypeStruct((1, WIDTH_A), F32), jax.ShapeDtypeStruct((1, WIDTH_A), F32)],
        scratch_shapes=[pltpu.VMEM((CHUNK, WIDTH_A), F32)],
        compiler_params=_params(("arbitrary",)),
    )(uv, dya_n, w_tril, w_tril_t, bias, g_sgu, g_a)


def _in_bwd_proj(duv, dq, dk, dv, cos_t, sin_t):
    s = duv.shape[0]
    tm = 512
    nc = WIDTH_B // LANES

    def body(duv_ref, *refs):
        dq_refs, dk_refs, dv_refs = refs[:nc], refs[nc:2 * nc], refs[2 * nc:3 * nc]
        cos_ref, sin_ref, dp_ref = refs[3 * nc:]
        cos = cos_ref[...]
        sin = sin_ref[...]
        dp_ref[:, :2 * WIDTH_A] = duv_ref[...]
        for i in range(nc):
            lo = 2 * WIDTH_A + i * LANES
            tq = _load_l256(dq_refs[i:i + 1], tm) * (HEAD_DIM ** -0.5)
            tk = _load_l256(dk_refs[i:i + 1], tm)
            dp_ref[:, lo:lo + LANES] = (tq * cos + _rope_partner(tq * sin)).astype(MXU_DTYPE)
            dp_ref[:, lo + WIDTH_B:lo + WIDTH_B + LANES] = (tk * cos + _rope_partner(tk * sin)).astype(MXU_DTYPE)
            dp_ref[:, lo + 2 * WIDTH_B:lo + 2 * WIDTH_B + LANES] = _load_l256(dv_refs[i:i + 1], tm).astype(MXU_DTYPE)

    return pl.pallas_call(
        body, name="in_bwd_proj", grid=(s // tm,),
        in_specs=[_rows(tm, 2 * WIDTH_A)] + 3 * _col_specs(tm, WIDTH_B) + [_rows(tm, LANES), _rows(tm, LANES)],
        out_specs=_rows(tm, IN_COLS), out_shape=jax.ShapeDtypeStruct((s, IN_COLS), MXU_DTYPE),
        compiler_params=_params(("arbitrary",)),
    )(duv, *([dq] * nc), *([dk] * nc), *([dv] * nc), cos_t, sin_t)


def _in_bwd_x(dproj, w_in_t, x, g_mix, dh1):
    s = x.shape[0]
    tm = 512

    def body(dp_ref, wt_ref, x_ref, g_ref, dh_ref, gx_ref, dg_ref):
        @pl.when(pl.program_id(0) == 0)
        def _():
            dg_ref[...] = jnp.zeros_like(dg_ref)

        dhn = _dot(dp_ref[...], wt_ref[...])
        n, r = _rms_stats(x_ref[...])
        dg_ref[...] += jnp.sum(dhn * n, axis=0, keepdims=True)
        gx_ref[...] = dh_ref[...] + _rms_bwd(dhn * g_ref[...], n, r)

    return pl.pallas_call(
        body, name="in_bwd_x", grid=(s // tm,),
        in_specs=[_rows(tm, IN_COLS), _full((IN_COLS, D_MODEL)), _rows(tm, D_MODEL), _full((1, D_MODEL)),
                  _rows(tm, D_MODEL)],
        out_specs=[_rows(tm, D_MODEL), _full((1, D_MODEL))],
        out_shape=[jax.ShapeDtypeStruct((s, D_MODEL), F32), jax.ShapeDtypeStruct((1, D_MODEL), F32)],
        compiler_params=_params(("arbitrary",)),
    )(dproj, w_in_t, x, g_mix, dh1)


def _wgrad(a, bs, name, transposed=False):
    s, m = a.shape
    bm = 512 if m % 512 == 0 else (FF_HALF if m == D_FF else m)
    ts = 512
    nsteps = s // ts
    nb = len(bs)

    def body(a_ref, *refs):
        b_refs, o_refs, accs = refs[:nb], refs[nb:2 * nb], refs[2 * nb:]
        kk = pl.program_id(1)
        at = a_ref[...].astype(MXU_DTYPE)
        for b_ref, o_ref, acc in zip(b_refs, o_refs, accs):
            c = _dot_tn(at, b_ref[...].astype(MXU_DTYPE))

            @pl.when(kk == 0)
            def _():
                acc[...] = c

            @pl.when(kk > 0)
            def _():
                acc[...] += c

            @pl.when(kk == nsteps - 1)
            def _():
                total = acc[...]
                o_ref[...] = (total.T if transposed else total).astype(o_ref.dtype)

    if transposed:
        out_specs = [pl.BlockSpec((b.shape[1], bm), lambda i, kk: (0, i)) for b in bs]
        out_shape = [jax.ShapeDtypeStruct((b.shape[1], m), jnp.bfloat16) for b in bs]
    else:
        out_specs = [pl.BlockSpec((bm, b.shape[1]), lambda i, kk: (i, 0)) for b in bs]
        out_shape = [jax.ShapeDtypeStruct((m, b.shape[1]), jnp.bfloat16) for b in bs]
    return pl.pallas_call(
        body, name=name, grid=(m // bm, nsteps),
        in_specs=[pl.BlockSpec((ts, bm), lambda i, kk: (kk, i))]
        + [pl.BlockSpec((ts, b.shape[1]), lambda i, kk: (kk, 0)) for b in bs],
        out_specs=out_specs, out_shape=out_shape,
        scratch_shapes=[pltpu.VMEM((bm, b.shape[1]), F32) for b in bs],
        compiler_params=_params(("arbitrary", "arbitrary")),
    )(a, *bs)


def _rope_tables(s):
    half = HEAD_DIM // 2
    inv = ROPE_THETA ** (-jnp.arange(half, dtype=F32) / half)
    ang = jnp.arange(s, dtype=F32)[:, None] * inv[None, :]
    cos = jnp.cos(ang)
    sin = jnp.sin(ang)
    cos_t = jnp.concatenate([cos, cos, cos, cos], axis=1)
    sin_t = jnp.concatenate([-sin, sin, -sin, sin], axis=1)
    return cos_t, sin_t


MESH = pl.DeviceIdType.MESH
ANY = pl.BlockSpec(memory_space=pl.ANY)
SEM = pl.BlockSpec(memory_space=pltpu.SEMAPHORE)
SPLIT_COPY = pltpu.CompilerParams(has_side_effects=pltpu.SideEffectType.DATAFLOW_SIDE_EFFECTING)
SLAB_IS_TRANSPOSED = {'w_in': True, 'w_out': False, 'w_gate': True, 'w_up': True, 'w_down': False,
                      'w_ple_gate': False, 'w_ple_proj': True}


def _place():
    x, y, c = lax.axis_index("x"), lax.axis_index("y"), lax.axis_index("c")
    other_chips = [(1 - x, y), (x, 1 - y), (1 - x, 1 - y)]
    return x, y, c, other_chips


def _chip_of(chip):
    return 2 * chip[0] + chip[1]


def _half(ref, lead, hc):
    hr = ref.shape[1] // 2
    return ref.at[lead, pl.ds(hc * hr, hr), :]


def _put_own(stack, own, index):
    return lax.dynamic_update_slice(stack, own[None], (index,) + (0,) * own.ndim)


def _all_gather_now(slab):
    rows, cols = slab.shape

    def body(x_ref, out_ref, send_sems, recv_sems):
        x, y, c, chips = _place()
        sibling = (x, y, 1 - c)
        hr = rows // 2

        def copy(k, src, dst, to):
            return pltpu.make_async_remote_copy(src_ref=src, dst_ref=dst, send_sem=send_sems.at[k],
                                                recv_sem=recv_sems.at[k], device_id=to, device_id_type=MESH)

        my_half = x_ref.at[pl.ds(c * hr, hr), :]
        first = [copy(j, my_half, _half(out_ref, 2 * x + y, c), (*chip, c)) for j, chip in enumerate(chips)]
        for cp in first:
            cp.start()
        passed = [copy(3 + j, _half(out_ref, _chip_of(chip), c), _half(out_ref, _chip_of(chip), c), sibling)
                  for j, chip in enumerate(chips)]
        for j, chip in enumerate(chips):
            copy(j, my_half, _half(out_ref, _chip_of(chip), c), (*chip, c)).wait_recv()
            passed[j].start()
        for j, chip in enumerate(chips):
            copy(3 + j, my_half, _half(out_ref, _chip_of(chip), 1 - c), sibling).wait_recv()
        for cp in first + passed:
            cp.wait_send()

    gathered = pl.pallas_call(
        body, name="all_gather_now", out_shape=jax.ShapeDtypeStruct((N_CHIPS, rows, cols), slab.dtype),
        in_specs=[ANY], out_specs=ANY,
        scratch_shapes=[pltpu.SemaphoreType.DMA((6,)), pltpu.SemaphoreType.DMA((6,))],
    )(slab)
    me = 2 * lax.axis_index("x") + lax.axis_index("y")
    return _put_own(gathered, slab, me).reshape(N_CHIPS * rows, cols)


def _gather_copies(slab_refs, land_refs, send_sems, recv_sems):
    x, y, c, chips = _place()
    sends, recvs = [], []
    for k, (src, land) in enumerate(zip(slab_refs, land_refs)):
        hr = src.shape[0] // 2
        for j, chip in enumerate(chips):
            for t in range(2):
                sends.append(pltpu.make_async_remote_copy(
                    src_ref=src.at[pl.ds(c * hr, hr), :], dst_ref=_half(land, 2 * x + y, c),
                    send_sem=send_sems.at[6 * k + 2 * j + t], recv_sem=recv_sems.at[6 * k + 2 * j + c],
                    device_id=(*chip, t), device_id_type=MESH))
                recvs.append(pltpu.make_async_remote_copy(
                    src_ref=src.at[pl.ds(t * hr, hr), :], dst_ref=_half(land, _chip_of(chip), t),
                    send_sem=send_sems.at[6 * k + 2 * j + t], recv_sem=recv_sems.at[6 * k + 2 * j + t],
                    device_id=(*chip, t), device_id_type=MESH))
    return sends, recvs


def _all_gather_start(slabs, after):
    n = len(slabs)

    def body(*refs):
        slab_refs, land_refs = refs[:n], refs[n:2 * n]
        send_sems, recv_sems = refs[2 * n + 1:2 * n + 3]
        token = refs[-1]
        sends, _ = _gather_copies(slab_refs, land_refs, send_sems, recv_sems)
        for cp in sends:
            cp.start()
        token[...] = jnp.zeros_like(token)

    lands = [lax.empty((N_CHIPS,) + s.shape, s.dtype) for s in slabs]
    hbm = lambda a: pltpu.HBM(a.shape, a.dtype)
    res = pl.pallas_call(
        body, name="all_gather_start",
        out_shape=(pltpu.SemaphoreType.DMA((6 * n,)), pltpu.SemaphoreType.DMA((6 * n,)), *map(hbm, slabs),
                   *map(hbm, lands), jax.ShapeDtypeStruct((8, LANES), F32)),
        in_specs=[ANY] * (2 * n + 1),
        out_specs=(SEM, SEM, *([ANY] * (2 * n)), pl.BlockSpec(memory_space=pltpu.VMEM)),
        input_output_aliases={i: 2 + i for i in range(2 * n)}, compiler_params=SPLIT_COPY,
    )(*[pltpu.with_memory_space_constraint(a, pltpu.HBM) for a in list(slabs) + lands], after)
    return res[:-1], res[-1]


def _all_gather_wait(handle, after):
    send_sems, recv_sems = handle[:2]
    n = (len(handle) - 2) // 2
    slabs, lands = handle[2:2 + n], handle[2 + n:]

    def body(*refs):
        slab_refs, land_refs = refs[:n], refs[n:2 * n]
        send_sems, recv_sems = refs[2 * n:2 * n + 2]
        sends, recvs = _gather_copies(slab_refs, land_refs, send_sems, recv_sems)
        for cp in sends:
            cp.wait_send()
        for cp in recvs:
            cp.wait_recv()

    hbm = lambda a: pltpu.HBM(a.shape, a.dtype)
    res = pl.pallas_call(
        body, name="all_gather_wait", out_shape=tuple(map(hbm, list(slabs) + list(lands))),
        in_specs=[ANY] * (2 * n) + [SEM, SEM, ANY], out_specs=tuple([ANY] * (2 * n)),
        input_output_aliases={i: i for i in range(2 * n)}, compiler_params=SPLIT_COPY,
    )(*slabs, *lands, send_sems, recv_sems, after)
    me = 2 * lax.axis_index("x") + lax.axis_index("y")
    return [_put_own(land, slab, me).reshape(N_CHIPS * slab.shape[0], slab.shape[1])
            for slab, land in zip(res[:n], res[n:])]


def _scatter_copies(part_refs, land_refs, send_sems, recv_sems):
    x, y, c, chips = _place()
    me = 4 * x + 2 * y + c
    sends, recvs = [], []
    for k, (part, land) in enumerate(zip(part_refs, land_refs)):
        for j, chip in enumerate(chips):
            for h in range(2):
                sends.append(pltpu.make_async_remote_copy(
                    src_ref=_half(part, _chip_of(chip), h), dst_ref=land.at[me],
                    send_sem=send_sems.at[7 * k + 2 * j + h], recv_sem=recv_sems.at[7 * k + 2 * j + c],
                    device_id=(*chip, h), device_id_type=MESH))
                recvs.append(pltpu.make_async_remote_copy(
                    src_ref=_half(part, _chip_of(chip), h), dst_ref=land.at[2 * _chip_of(chip) + h],
                    send_sem=send_sems.at[7 * k + 2 * j + h], recv_sem=recv_sems.at[7 * k + 2 * j + h],
                    device_id=(*chip, h), device_id_type=MESH))
        sends.append(pltpu.make_async_remote_copy(
            src_ref=_half(part, 2 * x + y, 1 - c), dst_ref=land.at[me], send_sem=send_sems.at[7 * k + 6],
            recv_sem=recv_sems.at[7 * k + 6], device_id=(x, y, 1 - c), device_id_type=MESH))
        recvs.append(pltpu.make_async_remote_copy(
            src_ref=_half(part, 2 * x + y, 1 - c), dst_ref=land.at[4 * x + 2 * y + 1 - c],
            send_sem=send_sems.at[7 * k + 6], recv_sem=recv_sems.at[7 * k + 6], device_id=(x, y, 1 - c),
            device_id_type=MESH))
    return sends, recvs


def _reduce_scatter_start(parts, name):
    n = len(parts)
    parts = [p.reshape(N_CHIPS, p.shape[0] // N_CHIPS, p.shape[1]) for p in parts]

    def body(*refs):
        part_refs, land_refs = refs[:n], refs[n:2 * n]
        send_sems, recv_sems = refs[2 * n:2 * n + 2]
        token = refs[-1]
        sends, _ = _scatter_copies(part_refs, land_refs, send_sems, recv_sems)
        for cp in sends:
            cp.start()
        token[...] = jnp.zeros_like(token)

    lands = [lax.empty((N_DEV, p.shape[1] // 2, p.shape[2]), p.dtype) for p in parts]
    hbm = lambda a: pltpu.HBM(a.shape, a.dtype)
    res = pl.pallas_call(
        body, name=name,
        out_shape=(pltpu.SemaphoreType.DMA((7 * n,)), pltpu.SemaphoreType.DMA((7 * n,)), *map(hbm, parts),
                   *map(hbm, lands), jax.ShapeDtypeStruct((8, LANES), F32)),
        in_specs=[ANY] * (2 * n), out_specs=(SEM, SEM, *([ANY] * (2 * n)), pl.BlockSpec(memory_space=pltpu.VMEM)),
        input_output_aliases={i: 2 + i for i in range(2 * n)}, compiler_params=SPLIT_COPY,
    )(*[pltpu.with_memory_space_constraint(a, pltpu.HBM) for a in parts + lands])
    return res[:-1], res[-1]


def _reduce_scatter_wait(handle, after, name):
    send_sems, recv_sems = handle[:2]
    n = (len(handle) - 2) // 2
    parts, lands = handle[2:2 + n], handle[2 + n:]

    def body(*refs):
        part_refs, land_refs = refs[:n], refs[n:2 * n]
        send_sems, recv_sems = refs[2 * n:2 * n + 2]
        sends, recvs = _scatter_copies(part_refs, land_refs, send_sems, recv_sems)
        for cp in sends:
            cp.wait_send()
        for cp in recvs:
            cp.wait_recv()

    hbm = lambda a: pltpu.HBM(a.shape, a.dtype)
    res = pl.pallas_call(
        body, name=name, out_shape=tuple(map(hbm, list(parts) + list(lands))),
        in_specs=[ANY] * (2 * n) + [SEM, SEM, ANY], out_specs=tuple([ANY] * (2 * n)),
        input_output_aliases={i: i for i in range(2 * n)}, compiler_params=SPLIT_COPY,
    )(*parts, *lands, send_sems, recv_sems, after)
    x, y, c = lax.axis_index("x"), lax.axis_index("y"), lax.axis_index("c")
    out = []
    for part, land in zip(res[:n], res[n:]):
        hr = land.shape[1]
        own = lax.dynamic_slice(part, (2 * x + y, c * hr, 0), (1, hr, part.shape[2]))[0]
        out.append(_put_own(land, own, 4 * x + 2 * y + c))
    return out


def _sum_shares(land, name):
    n, rows, cols = land.shape

    def body(l_ref, o_ref):
        acc = l_ref[0].astype(F32)
        for i in range(1, n):
            acc = acc + l_ref[i].astype(F32)
        o_ref[...] = acc

    return pl.pallas_call(
        body, name=name, grid=(1,), in_specs=[pl.BlockSpec((n, rows, cols), lambda i: (0, 0, 0))],
        out_specs=pl.BlockSpec((rows, cols), lambda i: (0, 0)), out_shape=jax.ShapeDtypeStruct((rows, cols), F32),
        compiler_params=_params(("arbitrary",)),
    )(land)


def _swap_halves(halves, name):
    n = len(halves)

    def body(*refs):
        in_refs, out_refs, send_sems, recv_sems = refs[:n], refs[n:2 * n], refs[2 * n], refs[2 * n + 1]
        x, y, c, _ = _place()
        copies = [pltpu.make_async_remote_copy(src_ref=i_ref, dst_ref=o_ref, send_sem=send_sems.at[k],
                                               recv_sem=recv_sems.at[k], device_id=(x, y, 1 - c), device_id_type=MESH)
                  for k, (i_ref, o_ref) in enumerate(zip(in_refs, out_refs))]
        for cp in copies:
            cp.start()
        for cp in copies:
            cp.wait()

    others = pl.pallas_call(
        body, name=name, out_shape=[jax.ShapeDtypeStruct(h.shape, h.dtype) for h in halves],
        in_specs=[ANY] * n, out_specs=[ANY] * n,
        scratch_shapes=[pltpu.SemaphoreType.DMA((n,)), pltpu.SemaphoreType.DMA((n,))],
    )(*halves)
    south = lax.axis_index("c") == 0
    return [jnp.concatenate([jnp.where(south, h, o), jnp.where(south, o, h)], axis=0) for h, o in zip(halves, others)]


def _small_all_reduce(block):
    rows = block.shape[0]

    def body(x_ref, all_ref, sum_ref, send_sems, recv_sems, local_sem):
        x, y, c, chips = _place()
        me, sibling = (x, y, c), (x, y, 1 - c)

        def blk(px, py, pc):
            return all_ref.at[pl.ds((4 * px + 2 * py + pc) * rows, rows), :]

        def copy(k, who, to, src=None):
            return pltpu.make_async_remote_copy(
                src_ref=blk(*who) if src is None else src, dst_ref=blk(*who), send_sem=send_sems.at[k],
                recv_sem=recv_sems.at[k], device_id=to, device_id_type=MESH)

        mine = pltpu.make_async_copy(x_ref, blk(*me), local_sem)
        mine.start()
        first = [copy(0, me, sibling, src=x_ref)]
        first += [copy(1 + j, me, (*chip, c), src=x_ref) for j, chip in enumerate(chips)]
        for cp in first:
            cp.start()
        passed = [copy(4 + j, (*chip, c), sibling) for j, chip in enumerate(chips)]
        for j, chip in enumerate(chips):
            copy(1 + j, (*chip, c), me).wait_recv()
            passed[j].start()
        copy(0, sibling, me).wait_recv()
        for j, chip in enumerate(chips):
            copy(4 + j, (*chip, 1 - c), me).wait_recv()
        for cp in first + passed:
            cp.wait_send()
        mine.wait()
        acc = all_ref[pl.ds(0, rows), :]
        for dev in range(1, N_DEV):
            acc = acc + all_ref[pl.ds(dev * rows, rows), :]
        sum_ref[...] = acc

    vmem = pl.BlockSpec(memory_space=pltpu.VMEM)
    return pl.pallas_call(
        body, name="small_all_reduce",
        out_shape=[jax.ShapeDtypeStruct((N_DEV * rows, D_MODEL), F32), jax.ShapeDtypeStruct((rows, D_MODEL), F32)],
        in_specs=[vmem], out_specs=[vmem, vmem],
        scratch_shapes=[pltpu.SemaphoreType.DMA((7,)), pltpu.SemaphoreType.DMA((7,)), pltpu.SemaphoreType.DMA],
    )(block)[1]


def _adamw(w, g, m, v, name):
    rows, cols = w.shape
    tm = rows
    if rows > 512:
        tm = next(t for t in range(512, 7, -8) if rows % t == 0)

    def body(w_ref, g_ref, m_ref, v_ref, d_ref, nm_ref, nv_ref):
        g_ = g_ref[...]
        m_ = ADAM_B1 * m_ref[...] + (1.0 - ADAM_B1) * g_
        v_ = ADAM_B2 * v_ref[...] + (1.0 - ADAM_B2) * (g_ * g_)
        m_hat = m_ / (1.0 - ADAM_B1 ** ADAM_STEP)
        v_hat = v_ / (1.0 - ADAM_B2 ** ADAM_STEP)
        d_ref[...] = -ADAM_LR * (m_hat / (jnp.sqrt(v_hat) + ADAM_EPS) + ADAM_WD * w_ref[...])
        nm_ref[...] = m_
        nv_ref[...] = v_

    spec = pl.BlockSpec((tm, cols), lambda i: (i, 0))
    return pl.pallas_call(
        body, name=name, grid=(rows // tm,), in_specs=[spec] * 4, out_specs=[spec] * 3,
        out_shape=[jax.ShapeDtypeStruct(w.shape, F32)] * 3, compiler_params=_params(("arbitrary",)),
    )(w, g, m, v)


def _pack_small(values):
    flat = jnp.concatenate([values[n].reshape(-1).astype(F32) for n in SMALL])
    return jnp.pad(flat, (0, SMALL_ROWS * D_MODEL - flat.shape[0])).reshape(SMALL_ROWS, D_MODEL)


def _unpack_small(block, shapes):
    flat = block.reshape(-1)
    out, lo = {}, 0
    for n in SMALL:
        out[n] = flat[lo:lo + SMALL_SIZES[n]].reshape(shapes[n])
        lo += SMALL_SIZES[n]
    return out


def _after(token, a):
    return a + token[:1, :1].astype(a.dtype)


def kernel(x, p, mix_norm_g, w_in, sgu_w, sgu_b, sgu_norm_g, out_norm_a, out_norm_b, w_out, ffn_norm_g, w_gate, w_up, w_down, ple_norm_g, w_ple_gate, w_ple_proj, final_norm_g, loss_target, m_mix_norm_g, m_w_in, m_sgu_w, m_sgu_b, m_sgu_norm_g, m_out_norm_a, m_out_norm_b, m_w_out, m_ffn_norm_g, m_w_gate, m_w_up, m_w_down, m_ple_norm_g, m_w_ple_gate, m_w_ple_proj, m_final_norm_g, v_mix_norm_g, v_w_in, v_sgu_w, v_sgu_b, v_sgu_norm_g, v_out_norm_a, v_out_norm_b, v_w_out, v_ffn_norm_g, v_w_gate, v_w_up, v_w_down, v_ple_norm_g, v_w_ple_gate, v_w_ple_proj, v_final_norm_g):
    given = dict(locals())
    drop_lead = lambda a, lead: a.reshape(a.shape[lead:])
    xs, ps, target = drop_lead(x, 1), drop_lead(p, 2), drop_lead(loss_target, 1)
    s = xs.shape[0]
    shard = lambda name: drop_lead(given[name], 1)

    def slab_of(name):
        local = shard(name).astype(MXU_DTYPE)
        return local.T if SLAB_IS_TRANSPOSED[name] else local

    w_in_t = _all_gather_now(slab_of('w_in'))
    later = ['w_out', 'w_gate', 'w_up', 'w_down', 'w_ple_gate', 'w_ple_proj']
    gather, token = _all_gather_start([slab_of(n) for n in later], w_in_t)

    cos_t, sin_t = _rope_tables(s)
    tril = jnp.tril(jnp.ones((CHUNK, CHUNK), F32))
    w_tril = (sgu_w.reshape(HEADS_A, CHUNK, CHUNK) * tril).astype(MXU_DTYPE)
    w_tril_t = jnp.swapaxes(w_tril, 1, 2)
    bias = jnp.repeat(sgu_b.reshape(HEADS_A, CHUNK).T, HEAD_DIM, axis=1)
    g = {n: given[n].reshape(1, -1) for n in SMALL if n not in ('sgu_w', 'sgu_b')}

    uv, q, k, v, hn1 = _in_fwd(xs, _after(token, g['mix_norm_g']), w_in_t.T, cos_t, sin_t)
    ya_n = _sgu_fwd(uv, w_tril, bias, g['sgu_norm_g'], g['out_norm_a'])
    branches = [_attn_fwd_branch(q, k, v, dil) for dil in DILATIONS]
    y_b, lse = _attn_merge([o for o, _ in branches], [l for _, l in branches])
    stacks = dict(zip(later, _all_gather_wait(gather, lse)))
    w_gate_t, w_up_t, w_pp_t = stacks['w_gate'], stacks['w_up'], stacks['w_ple_proj']
    h1, y_n = _out_fwd(ya_n, y_b, g['out_norm_b'], stacks['w_out'], xs)
    h2, gate, up, hn2 = _ffn_fwd(h1, g['ffn_norm_g'], w_gate_t.T, w_up_t.T, stacks['w_down'])
    loss, dh2, dz, dpp, hn3, d_ple_g, d_final_g = _ple_loss(
        h2, ps, target, g['ple_norm_g'], stacks['w_ple_gate'], stacks['w_ple_gate'].T, w_pp_t.T, g['final_norm_g'])

    share = {}
    share['w_ple_gate'], = _wgrad(hn3, [dz], "wgrad_ple_gate")
    share['w_ple_proj'], = _wgrad(ps, [dpp], "wgrad_ple_proj", transposed=True)
    scatter_1, token = _reduce_scatter_start([share['w_ple_gate'], share['w_ple_proj']], "reduce_scatter_start_1")
    dh1, act, dgate, dup, d_ffn_g = _ffn_bwd(dh2, h1, gate, up, _after(token, g['ffn_norm_g']), stacks['w_down'].T,
                                             w_gate_t, w_up_t)
    share['w_down'], = _wgrad(act, [dh2], "wgrad_down")
    share['w_gate'], share['w_up'] = _wgrad(hn2, [dgate, dup], "wgrad_gate_up", transposed=True)
    scatter_2, token = _reduce_scatter_start([share['w_down'], share['w_gate'], share['w_up']],
                                             "reduce_scatter_start_2")
    dya_n, dyb, d_out_b = _out_bwd(dh1, y_b, _after(token, g['out_norm_b']), stacks['w_out'].T)
    share['w_out'], = _wgrad(y_n, [dh1], "wgrad_out")
    scatter_3, token = _reduce_scatter_start([share['w_out']], "reduce_scatter_start_3")
    grads = None
    for dil in DILATIONS:
        grads = _attn_bwd_branch(q, k, v, dyb, y_b, lse, grads, dil)
    duv, d_sgu_w, d_sgu_b, d_sgu_g, d_out_a = _sgu_bwd(uv, dya_n, w_tril, w_tril_t, bias,
                                                       _after(token, g['sgu_norm_g']), g['out_norm_a'])
    dproj = _in_bwd_proj(duv, grads[0], grads[1], grads[2], cos_t, sin_t)
    share['w_in'], = _wgrad(hn1, [dproj], "wgrad_in", transposed=True)
    scatter_4, token = _reduce_scatter_start([share['w_in']], "reduce_scatter_start_4")
    grad_x, d_mix_g = _in_bwd_x(dproj, w_in_t, xs, _after(token, g['mix_norm_g']), dh1)

    grads, deltas, new_m, new_v = {}, {}, {}, {}
    add_lead = lambda a: a.reshape((1,) + a.shape)

    def finish(names, handles, after, tag):
        landed = []
        for i, handle in enumerate(handles):
            landed += _reduce_scatter_wait(handle, after, "reduce_scatter_wait_%s%d" % (tag, i))
        halves = [_sum_shares(land, "sum_shares_" + n) for n, land in zip(names, landed)]
        for n, slab in zip(names, _swap_halves(halves, "swap_halves_" + tag)):
            turn = (lambda a: a.T) if SLAB_IS_TRANSPOSED[n] else (lambda a: a)
            d, nm, nv = _adamw(turn(shard(n)), slab, turn(shard("m_" + n)), turn(shard("v_" + n)), "adamw_" + n)
            grads[n], deltas[n], new_m[n], new_v[n] = (add_lead(turn(a)) for a in (slab, d, nm, nv))

    finish(['w_ple_gate', 'w_ple_proj', 'w_down', 'w_gate', 'w_up', 'w_out'], [scatter_1, scatter_2, scatter_3], token,
           "early")

    gs = {'mix_norm_g': d_mix_g, 'sgu_w': d_sgu_w, 'sgu_b': d_sgu_b[:, :HEADS_A].T, 'sgu_norm_g': d_sgu_g,
          'out_norm_a': d_out_a, 'out_norm_b': d_out_b, 'ffn_norm_g': d_ffn_g, 'ple_norm_g': d_ple_g,
          'final_norm_g': d_final_g}
    gs_block = _pack_small(gs).at[SMALL_ROWS - 1, 0].set(loss[0, 0])
    small_sum = _small_all_reduce(gs_block)
    loss_out = small_sum[SMALL_ROWS - 1, 0]
    small_shapes = {n: given[n].shape for n in SMALL}
    finish(['w_in'], [scatter_4], small_sum, "last")

    small = {n: given[n] for n in SMALL}
    d, nm, nv = _adamw(_pack_small(small), small_sum, _pack_small({n: given["m_" + n] for n in SMALL}),
                       _pack_small({n: given["v_" + n] for n in SMALL}), "adamw_small")
    for res, blk in ((grads, small_sum), (deltas, d), (new_m, nm), (new_v, nv)):
        res.update(_unpack_small(blk, small_shapes))

    outs = [loss_out, add_lead(grad_x)]
    for res in (grads, deltas, new_m, new_v):
        outs += [res[n] for n in WEIGHT_NAMES]
    return tuple(outs)
```

```python
import functools
import math

import jax
import jax.numpy as jnp
import numpy as np
from jax import lax
from jax.experimental import pallas as pl
from jax.experimental.pallas import tpu as pltpu

F32 = jnp.float32
MXU_DTYPE = jnp.bfloat16

D_MODEL = 1024
HEAD_DIM = 64
HEADS_A = 4
HEADS_B = 12
WIDTH_A = HEADS_A * HEAD_DIM
WIDTH_B = HEADS_B * HEAD_DIM
CHUNK = 128
BLOCK = 128
DILATIONS = (1, 4, 16)
ROPE_THETA = 10000.0
D_FF = 2816
FF_HALF = D_FF // 2
PLE_DIM = 256
IN_COLS = 2 * WIDTH_A + 3 * WIDTH_B
EPS = 1e-6
LANES = 128
N_CHIPS = 4
N_DEV = 8

ADAM_LR = 0.001
ADAM_B1 = 0.9
ADAM_B2 = 0.999
ADAM_EPS = 1e-08
ADAM_WD = 0.01
ADAM_STEP = 10

VMEM_LIMIT = 56 * 1024 * 1024

WEIGHT_NAMES = ['mix_norm_g', 'w_in', 'sgu_w', 'sgu_b', 'sgu_norm_g', 'out_norm_a', 'out_norm_b', 'w_out',
                'ffn_norm_g', 'w_gate', 'w_up', 'w_down', 'ple_norm_g', 'w_ple_gate', 'w_ple_proj', 'final_norm_g']
SHARDED = ['w_in', 'w_out', 'w_gate', 'w_up', 'w_down', 'w_ple_gate', 'w_ple_proj']
SMALL = ['mix_norm_g', 'sgu_w', 'sgu_b', 'sgu_norm_g', 'out_norm_a', 'out_norm_b', 'ffn_norm_g', 'ple_norm_g',
         'final_norm_g']
SMALL_SIZES = {'mix_norm_g': 1024, 'sgu_w': 65536, 'sgu_b': 512, 'sgu_norm_g': 256, 'out_norm_a': 256,
               'out_norm_b': 768, 'ffn_norm_g': 1024, 'ple_norm_g': 1024, 'final_norm_g': 1024}
SMALL_ROWS = 72


def _params(semantics=None):
    return pltpu.CompilerParams(dimension_semantics=semantics, vmem_limit_bytes=VMEM_LIMIT)


def _full(shape):
    nd = len(shape)
    return pl.BlockSpec(shape, lambda i: (0,) * nd)


def _rows(tm, width):
    return pl.BlockSpec((tm, width), lambda i: (i, 0))


def _rms_stats(x):
    r = lax.rsqrt(jnp.mean(x * x, axis=-1, keepdims=True) + EPS)
    return x * r, r


def _rms_bwd(dn, n, r):
    return r * (dn - n * jnp.mean(dn * n, axis=-1, keepdims=True))


def _dot(a, b):
    return jnp.dot(a, b, preferred_element_type=F32)


def _dot_nt(a, b):
    return lax.dot_general(a, b, (((1,), (1,)), ((), ())), preferred_element_type=F32)


def _dot_tn(a, b):
    return lax.dot_general(a, b, (((0,), (0,)), ((), ())), preferred_element_type=F32)


def _gelu_parts(x):
    c = math.sqrt(2.0 / math.pi)
    t = jnp.tanh(c * (x + 0.044715 * x * x * x))
    return 0.5 * x * (1.0 + t), t


def _gelu_grad(x, t):
    c = math.sqrt(2.0 / math.pi)
    return 0.5 * (1.0 + t) + 0.5 * x * (1.0 - t * t) * c * (1.0 + 3.0 * 0.044715 * x * x)


def _half_masks(dtype):
    lane = lax.broadcasted_iota(jnp.int32, (BLOCK, LANES), 1)
    lo = (lane < HEAD_DIM).astype(F32)
    return lo.astype(dtype), (1.0 - lo).astype(dtype)


def _rope_partner(t):
    lane = lax.broadcasted_iota(jnp.int32, t.shape, 1)
    first_half = (lane % HEAD_DIM) < (HEAD_DIM // 2)
    return jnp.where(first_half, pltpu.roll(t, LANES - HEAD_DIM // 2, 1), pltpu.roll(t, HEAD_DIM // 2, 1))


L_BLOCK = 256
L_GROUP = 16


def _store_l256(scr, out_ref, cols, value):
    tm = value.shape[0]
    scr[...] = value
    for blk in range(tm // L_BLOCK):
        for r in range(L_GROUP):
            lo = blk * L_BLOCK + r * L_GROUP
            piece = scr[pl.ds(blk * L_BLOCK + r, L_GROUP, stride=L_GROUP), :]
            out_ref[lo:lo + L_GROUP, cols] = piece.astype(out_ref.dtype)


def _load_l256(col_refs, tm):
    cols = []
    for ref in col_refs:
        pieces = [ref[pl.ds(blk * L_BLOCK + i, L_GROUP, stride=L_GROUP), :]
                  for blk in range(tm // L_BLOCK) for i in range(L_GROUP)]
        cols.append(jnp.concatenate(pieces, axis=0))
    return jnp.concatenate(cols, axis=1)


def _col_specs(tm, width):
    return [pl.BlockSpec((tm, LANES), lambda i, j=j: (i, j)) for j in range(width // LANES)]


def _in_fwd(x, g_mix, w_in, cos_t, sin_t):
    s = x.shape[0]
    tm = 512

    def body(x_ref, g_ref, w_ref, cos_ref, sin_ref, uv_ref, q_ref, k_ref, v_ref, hn_ref, scr):
        n, _ = _rms_stats(x_ref[...])
        hn = (n * g_ref[...]).astype(MXU_DTYPE)
        hn_ref[...] = hn
        proj = _dot(hn, w_ref[...])
        uv_ref[...] = proj[:, :2 * WIDTH_A]
        cos = cos_ref[...]
        sin = sin_ref[...]
        for i in range(WIDTH_B // LANES):
            lo = 2 * WIDTH_A + i * LANES
            tq = proj[:, lo:lo + LANES]
            tk = proj[:, lo + WIDTH_B:lo + WIDTH_B + LANES]
            tv = proj[:, lo + 2 * WIDTH_B:lo + 2 * WIDTH_B + LANES]
            cols = slice(i * LANES, (i + 1) * LANES)
            _store_l256(scr, q_ref, cols, (tq * cos + _rope_partner(tq) * sin) * (HEAD_DIM ** -0.5))
            _store_l256(scr, k_ref, cols, tk * cos + _rope_partner(tk) * sin)
            _store_l256(scr, v_ref, cols, tv)

    return pl.pallas_call(
        body, name="in_fwd", grid=(s // tm,), scratch_shapes=[pltpu.VMEM((tm, LANES), F32)],
        in_specs=[_rows(tm, D_MODEL), _full((1, D_MODEL)), _full((D_MODEL, IN_COLS)), _rows(tm, LANES),
                  _rows(tm, LANES)],
        out_specs=[_rows(tm, 2 * WIDTH_A), _rows(tm, WIDTH_B), _rows(tm, WIDTH_B), _rows(tm, WIDTH_B),
                   _rows(tm, D_MODEL)],
        out_shape=[jax.ShapeDtypeStruct((s, 2 * WIDTH_A), F32), jax.ShapeDtypeStruct((s, WIDTH_B), MXU_DTYPE),
                   jax.ShapeDtypeStruct((s, WIDTH_B), MXU_DTYPE), jax.ShapeDtypeStruct((s, WIDTH_B), MXU_DTYPE),
                   jax.ShapeDtypeStruct((s, D_MODEL), MXU_DTYPE)],
        compiler_params=_params(("arbitrary",)),
    )(x, g_mix, w_in, cos_t, sin_t)


class _Branch:
    def __init__(self, dil, s, qn=BLOCK):
        self.dil = dil
        i = np.arange(L_GROUP)
        if dil == 16:
            nblk = qn // 16
            self.grid = (16, s // (L_BLOCK * nblk))
            self.shape = (nblk, 1, 1, L_GROUP)
            self.index = lambda r, n: (n, r // 4, r % 4, 0, 0)
            pos = (np.arange(nblk)[:, None] * 16 + i[None, :]).reshape(-1)
        elif dil == 4:
            nblk = qn // 64
            self.grid = (4, s // (L_BLOCK * nblk))
            self.shape = (nblk, 4, 1, L_GROUP)
            self.index = lambda r, n: (n, 0, r, 0, 0)
            pos = (np.arange(nblk)[:, None, None] * 64 + np.arange(4)[None, :, None]
                   + 4 * i[None, None, :]).reshape(-1)
        else:
            self.grid = (1, s // L_BLOCK)
            self.shape = (1, 4, 4, L_GROUP)
            self.index = lambda r, n: (n, 0, 0, 0, 0)
            pos = (np.arange(16)[:, None] + 16 * i[None, :]).reshape(-1)
        self.qn = pos.shape[0]
        self.nb = self.grid[1]
        dist = pos[:, None] - np.concatenate([pos - self.qn, pos])[None, :]
        band = (dist >= 0) & (dist <= BLOCK)
        start = band & (np.arange(2 * self.qn)[None, :] >= self.qn)
        self.bias = np.where(np.stack([band, start]), 0.0, -np.inf).astype(np.float32)

    def view(self, a):
        return a.reshape(a.shape[0] // L_BLOCK, 4, 4, L_GROUP, a.shape[1])

    def spec(self, w, step=lambda n: n):
        return pl.BlockSpec(self.shape + (w,), lambda r, n: self.index(r, step(n)))

    def bias_spec(self, step=lambda n: n):
        return pl.BlockSpec((1, self.qn, 2 * self.qn), lambda r, n: (jnp.where(step(n) == 0, 1, 0), 0, 0))

    def load(self, ref, cols=slice(None)):
        x = ref[:, :, :, :, cols]
        return x.reshape(self.qn, x.shape[-1])

    def store(self, ref, cols, value):
        ref[:, :, :, :, cols] = value.reshape(self.shape + (value.shape[-1],))


def _attn_fwd_branch(q, k, v, dil):
    s = q.shape[0]
    br = _Branch(dil, s)
    qn = br.qn

    def body(bias_ref, q_ref, kc_ref, kp_ref, vc_ref, vp_ref, o_ref, lse_ref):
        bias2 = jnp.concatenate([bias_ref[0], bias_ref[0]], axis=0)
        lo = lax.broadcasted_iota(jnp.int32, (qn, LANES), 1) < HEAD_DIM
        mask_lo = lo.astype(F32).astype(MXU_DTYPE)
        for hp in range(HEADS_B // 2):
            cols = slice(hp * LANES, (hp + 1) * LANES)
            qp = br.load(q_ref, cols)
            kcat = jnp.concatenate([br.load(kp_ref, cols), br.load(kc_ref, cols)], axis=0)
            vcat = jnp.concatenate([br.load(vp_ref, cols), br.load(vc_ref, cols)], axis=0)
            sc = _dot_nt(jnp.concatenate([qp * mask_lo, qp * (1 - mask_lo)], axis=0), kcat) + bias2
            m = jnp.max(sc, axis=1, keepdims=True)
            p = jnp.exp(sc - m)
            l = jnp.sum(p, axis=1, keepdims=True)
            out = _dot(p.astype(MXU_DTYPE), vcat) / l
            lse = m + jnp.log(l)
            br.store(o_ref, cols, jnp.where(lo, out[:qn], out[qn:]))
            br.store(lse_ref, cols, jnp.where(lo, lse[:qn], lse[qn:]))

    before = lambda n: jnp.maximum(n - 1, 0)
    res = pl.pallas_call(
        body, name="attn_fwd_d%d" % dil, grid=br.grid,
        in_specs=[br.bias_spec(), br.spec(WIDTH_B), br.spec(WIDTH_B), br.spec(WIDTH_B, before), br.spec(WIDTH_B),
                  br.spec(WIDTH_B, before)],
        out_specs=[br.spec(WIDTH_B), br.spec(WIDTH_B)],
        out_shape=[jax.ShapeDtypeStruct((s // L_BLOCK, 4, 4, L_GROUP, WIDTH_B), F32)] * 2,
        compiler_params=_params(("arbitrary", "arbitrary")),
    )(jnp.asarray(br.bias), br.view(q), br.view(k), br.view(k), br.view(v), br.view(v))
    return tuple(a.reshape(s, WIDTH_B) for a in res)


def _attn_merge(outs, lses):
    s = outs[0].shape[0]
    tm = 512
    nbr = len(outs)

    def body(*refs):
        o_refs, l_refs, y_ref, lse_ref = refs[:nbr], refs[nbr:2 * nbr], refs[2 * nbr], refs[2 * nbr + 1]
        ls = [r[...] for r in l_refs]
        top = functools.reduce(jnp.maximum, ls)
        ws = [jnp.exp(l - top) for l in ls]
        den = functools.reduce(jnp.add, ws)
        num = functools.reduce(jnp.add, [w * r[...] for w, r in zip(ws, o_refs)])
        y_ref[...] = num / den
        lse_ref[...] = top + jnp.log(den)

    return pl.pallas_call(
        body, name="attn_merge", grid=(s // tm,), in_specs=[_rows(tm, WIDTH_B)] * (2 * nbr),
        out_specs=[_rows(tm, WIDTH_B)] * 2, out_shape=[jax.ShapeDtypeStruct((s, WIDTH_B), F32)] * 2,
        compiler_params=_params(("arbitrary",)),
    )(*outs, *lses)


def _sgu_forward_tile(uv, w_ref, bias, g_sgu):
    tm = uv.shape[0]
    u = uv[:, :WIDTH_A]
    v = uv[:, WIDTH_A:]
    ug, tu = _gelu_parts(u)
    vg, tv = _gelu_parts(v)
    mu = jnp.mean(vg, axis=-1, keepdims=True)
    vc = vg - mu
    rs = lax.rsqrt(jnp.mean(vc * vc, axis=-1, keepdims=True) + EPS)
    vhat = vc * rs
    vn = (vhat * g_sgu).astype(MXU_DTYPE)
    masks = _half_masks(MXU_DTYPE)
    chunks = []
    for c in range(tm // CHUNK):
        rows = slice(c * CHUNK, (c + 1) * CHUNK)
        groups = []
        for gp in range(2):
            vn_g = vn[rows, gp * LANES:(gp + 1) * LANES]
            groups.append(_dot(w_ref[2 * gp], vn_g * masks[0]) + _dot(w_ref[2 * gp + 1], vn_g * masks[1]))
        chunks.append(jnp.concatenate(groups, axis=1) + bias)
    mixed = jnp.concatenate(chunks, axis=0)
    return dict(u=u, v=v, ug=ug, tu=tu, tv=tv, rs=rs, vhat=vhat, vn=vn, mixed=mixed, ya=ug * mixed)


def _sgu_fwd(uv, w_tril, bias, g_sgu, g_a):
    s = uv.shape[0]
    tm = 512

    def body(uv_ref, w_ref, b_ref, gs_ref, ga_ref, o_ref):
        t = _sgu_forward_tile(uv_ref[...], w_ref, b_ref[...], gs_ref[...])
        n, _ = _rms_stats(t['ya'])
        o_ref[...] = (n * ga_ref[...]).astype(MXU_DTYPE)

    return pl.pallas_call(
        body, name="sgu_fwd", grid=(s // tm,),
        in_specs=[_rows(tm, 2 * WIDTH_A), _full((HEADS_A, CHUNK, CHUNK)), _full((CHUNK, WIDTH_A)),
                  _full((1, WIDTH_A)), _full((1, WIDTH_A))],
        out_specs=_rows(tm, WIDTH_A), out_shape=jax.ShapeDtypeStruct((s, WIDTH_A), MXU_DTYPE),
        compiler_params=_params(("arbitrary",)),
    )(uv, w_tril, bias, g_sgu, g_a)


def _out_fwd(ya_n, y_b, g_b, w_out, x):
    s = x.shape[0]
    tm = 512
    nc = WIDTH_B // LANES

    def body(ya_ref, *refs):
        yb_refs = refs[:nc]
        g_ref, w_ref, x_ref, h_ref, yn_ref = refs[nc:]
        n, _ = _rms_stats(_load_l256(yb_refs, tm))
        yn = jnp.concatenate([ya_ref[...], (n * g_ref[...]).astype(MXU_DTYPE)], axis=1)
        yn_ref[...] = yn
        h_ref[...] = x_ref[...] + _dot(yn, w_ref[...])

    return pl.pallas_call(
        body, name="out_fwd", grid=(s // tm,),
        in_specs=[_rows(tm, WIDTH_A)] + _col_specs(tm, WIDTH_B) + [_full((1, WIDTH_B)), _full((D_MODEL, D_MODEL)),
                                                                 _rows(tm, D_MODEL)],
        out_specs=[_rows(tm, D_MODEL), _rows(tm, D_MODEL)],
        out_shape=[jax.ShapeDtypeStruct((s, D_MODEL), F32), jax.ShapeDtypeStruct((s, D_MODEL), MXU_DTYPE)],
        compiler_params=_params(("arbitrary",)),
    )(ya_n, *([y_b] * nc), g_b, w_out, x)


def _ffn_fwd(h1, g_ffn, w_gate, w_up, w_down):
    s = h1.shape[0]
    tm = 256

    def body(h_ref, g_ref, wg_ref, wu_ref, wd_ref, o_ref, gate_ref, up_ref, hn_ref):
        h = h_ref[...]
        n, _ = _rms_stats(h)
        hn = (n * g_ref[...]).astype(MXU_DTYPE)
        hn_ref[...] = hn
        out = h
        for c in range(2):
            cols = slice(c * FF_HALF, (c + 1) * FF_HALF)
            gate = _dot(hn, wg_ref[:, cols])
            up = _dot(hn, wu_ref[:, cols])
            gate_ref[:, cols] = gate.astype(MXU_DTYPE)
            up_ref[:, cols] = up.astype(MXU_DTYPE)
            act = (gate * jax.nn.sigmoid(gate) * up).astype(MXU_DTYPE)
            out = out + _dot(act, wd_ref[cols, :])
        o_ref[...] = out

    return pl.pallas_call(
        body, name="ffn_fwd", grid=(s // tm,),
        in_specs=[_rows(tm, D_MODEL), _full((1, D_MODEL)), _full((D_MODEL, D_FF)), _full((D_MODEL, D_FF)),
                  _full((D_FF, D_MODEL))],
        out_specs=[_rows(tm, D_MODEL), _rows(tm, D_FF), _rows(tm, D_FF), _rows(tm, D_MODEL)],
        out_shape=[jax.ShapeDtypeStruct((s, D_MODEL), F32), jax.ShapeDtypeStruct((s, D_FF), MXU_DTYPE),
                   jax.ShapeDtypeStruct((s, D_FF), MXU_DTYPE), jax.ShapeDtypeStruct((s, D_MODEL), MXU_DTYPE)],
        compiler_params=_params(("arbitrary",)),
    )(h1, g_ffn, w_gate, w_up, w_down)


def _ple_loss(h2, p, target, g_ple, w_pg, w_pg_t, w_pp, g_final):
    s = h2.shape[0]
    tm = 256

    def body(h_ref, p_ref, t_ref, gp_ref, wg_ref, wgt_ref, wp_ref, gf_ref,
             loss_ref, dh_ref, dz_ref, dpp_ref, hn_ref, dgp_ref, dgf_ref):
        @pl.when(pl.program_id(0) == 0)
        def _():
            loss_ref[...] = jnp.zeros_like(loss_ref)
            dgp_ref[...] = jnp.zeros_like(dgp_ref)
            dgf_ref[...] = jnp.zeros_like(dgf_ref)

        h2t = h_ref[...]
        n2, r2 = _rms_stats(h2t)
        hn = (n2 * gp_ref[...]).astype(MXU_DTYPE)
        hn_ref[...] = hn
        gate = jax.nn.sigmoid(_dot(hn, wg_ref[...]))
        pp = _dot(p_ref[...].astype(MXU_DTYPE), wp_ref[...])
        h3 = h2t + gate * pp
        n3, r3 = _rms_stats(h3)
        diff = n3 * gf_ref[...] - t_ref[...]
        loss_ref[...] += jnp.full(loss_ref.shape, 0.5 * jnp.sum(diff * diff) / D_MODEL, F32)
        dy = diff * (1.0 / D_MODEL)
        dgf_ref[...] += jnp.sum(dy * n3, axis=0, keepdims=True)
        dh3 = _rms_bwd(dy * gf_ref[...], n3, r3)
        dpp_ref[...] = (dh3 * gate).astype(MXU_DTYPE)
        dz = (dh3 * pp * gate * (1.0 - gate)).astype(MXU_DTYPE)
        dz_ref[...] = dz
        dhn = _dot(dz, wgt_ref[...])
        dgp_ref[...] += jnp.sum(dhn * n2, axis=0, keepdims=True)
        dh_ref[...] = dh3 + _rms_bwd(dhn * gp_ref[...], n2, r2)

    return pl.pallas_call(
        body, name="ple_loss", grid=(s // tm,),
        in_specs=[_rows(tm, D_MODEL), _rows(tm, PLE_DIM), _rows(tm, D_MODEL), _full((1, D_MODEL)),
                  _full((D_MODEL, D_MODEL)), _full((D_MODEL, D_MODEL)), _full((PLE_DIM, D_MODEL)),
                  _full((1, D_MODEL))],
        out_specs=[_full((1, LANES)), _rows(tm, D_MODEL), _rows(tm, D_MODEL), _rows(tm, D_MODEL),
                   _rows(tm, D_MODEL), _full((1, D_MODEL)), _full((1, D_MODEL))],
        out_shape=[jax.ShapeDtypeStruct((1, LANES), F32), jax.ShapeDtypeStruct((s, D_MODEL), F32),
                   jax.ShapeDtypeStruct((s, D_MODEL), MXU_DTYPE), jax.ShapeDtypeStruct((s, D_MODEL), MXU_DTYPE),
                   jax.ShapeDtypeStruct((s, D_MODEL), MXU_DTYPE), jax.ShapeDtypeStruct((1, D_MODEL), F32),
                   jax.ShapeDtypeStruct((1, D_MODEL), F32)],
        compiler_params=_params(("arbitrary",)),
    )(h2, p, target, g_ple, w_pg, w_pg_t, w_pp, g_final)


def _ffn_bwd(dh2, h1, gate, up, g_ffn, w_down_t, w_gate_t, w_up_t):
    s = h1.shape[0]
    tm = 256

    def body(dh_ref, h_ref, gate_ref, up_ref, g_ref, wdt_ref, wgt_ref, wut_ref,
             o_ref, act_ref, dg_ref, du_ref, dgn_ref):
        @pl.when(pl.program_id(0) == 0)
        def _():
            dgn_ref[...] = jnp.zeros_like(dgn_ref)

        dh = dh_ref[...]
        dhb = dh.astype(MXU_DTYPE)
        dhn = jnp.zeros((tm, D_MODEL), F32)
        for c in range(2):
            cols = slice(c * FF_HALF, (c + 1) * FF_HALF)
            dact = _dot(dhb, wdt_ref[:, cols])
            g = gate_ref[:, cols].astype(F32)
            u = up_ref[:, cols].astype(F32)
            sg = jax.nn.sigmoid(g)
            silu = g * sg
            act_ref[:, cols] = (silu * u).astype(MXU_DTYPE)
            du = (dact * silu).astype(MXU_DTYPE)
            dg = (dact * u * sg * (1.0 + g * (1.0 - sg))).astype(MXU_DTYPE)
            du_ref[:, cols] = du
            dg_ref[:, cols] = dg
            dhn = dhn + _dot(dg, wgt_ref[cols, :]) + _dot(du, wut_ref[cols, :])
        n, r = _rms_stats(h_ref[...])
        dgn_ref[...] += jnp.sum(dhn * n, axis=0, keepdims=True)
        o_ref[...] = dh + _rms_bwd(dhn * g_ref[...], n, r)

    return pl.pallas_call(
        body, name="ffn_bwd", grid=(s // tm,),
        in_specs=[_rows(tm, D_MODEL), _rows(tm, D_MODEL), _rows(tm, D_FF), _rows(tm, D_FF), _full((1, D_MODEL)),
                  _full((D_MODEL, D_FF)), _full((D_FF, D_MODEL)), _full((D_FF, D_MODEL))],
        out_specs=[_rows(tm, D_MODEL), _rows(tm, D_FF), _rows(tm, D_FF), _rows(tm, D_FF), _full((1, D_MODEL))],
        out_shape=[jax.ShapeDtypeStruct((s, D_MODEL), F32), jax.ShapeDtypeStruct((s, D_FF), MXU_DTYPE),
                   jax.ShapeDtypeStruct((s, D_FF), MXU_DTYPE), jax.ShapeDtypeStruct((s, D_FF), MXU_DTYPE),
                   jax.ShapeDtypeStruct((1, D_MODEL), F32)],
        compiler_params=_params(("arbitrary",)),
    )(dh2, h1, gate, up, g_ffn, w_down_t, w_gate_t, w_up_t)


def _out_bwd(dh1, y_b, g_b, w_out_t):
    s = dh1.shape[0]
    tm = 512
    nc = WIDTH_B // LANES

    def body(dh_ref, *refs):
        yb_refs = refs[:nc]
        g_ref, wt_ref, dya_ref, dyb_ref, dg_ref, scr = refs[nc:]

        @pl.when(pl.program_id(0) == 0)
        def _():
            dg_ref[...] = jnp.zeros_like(dg_ref)

        dy = _dot(dh_ref[...].astype(MXU_DTYPE), wt_ref[...])
        dya_ref[...] = dy[:, :WIDTH_A]
        dyb = dy[:, WIDTH_A:]
        n, r = _rms_stats(_load_l256(yb_refs, tm))
        dg_ref[...] += jnp.sum(dyb * n, axis=0, keepdims=True)
        dyb_in = _rms_bwd(dyb * g_ref[...], n, r)
        for j in range(nc):
            cols = slice(j * LANES, (j + 1) * LANES)
            _store_l256(scr, dyb_ref, cols, dyb_in[:, cols])

    return pl.pallas_call(
        body, name="out_bwd", grid=(s // tm,), scratch_shapes=[pltpu.VMEM((tm, LANES), F32)],
        in_specs=[_rows(tm, D_MODEL)] + _col_specs(tm, WIDTH_B) + [_full((1, WIDTH_B)), _full((D_MODEL, D_MODEL))],
        out_specs=[_rows(tm, WIDTH_A), _rows(tm, WIDTH_B), _full((1, WIDTH_B))],
        out_shape=[jax.ShapeDtypeStruct((s, WIDTH_A), F32), jax.ShapeDtypeStruct((s, WIDTH_B), F32),
                   jax.ShapeDtypeStruct((1, WIDTH_B), F32)],
        compiler_params=_params(("arbitrary",)),
    )(dh1, *([y_b] * nc), g_b, w_out_t)


def _attn_bwd_branch(q, k, v, do, o, lse, grads, dil):
    s = q.shape[0]
    br = _Branch(dil, s)
    qn, nb = br.qn, br.nb
    first = grads is None

    def body(*refs):
        bias_ref, q_ref, kc_ref, kp_ref, vc_ref, vp_ref, do_ref, o_ref, lse_ref = refs[:9]
        if first:
            rest = refs[9:]
        else:
            dq_in, dk_in, dv_in = refs[9:12]
            rest = refs[12:]
        dq_ref, dk_ref, dv_ref, dk_carry, dv_carry = rest
        n = pl.program_id(1)

        @pl.when(n == 0)
        def _():
            dk_carry[...] = jnp.zeros_like(dk_carry)
            dv_carry[...] = jnp.zeros_like(dv_carry)

        @pl.when(n < nb)
        def _():
            bias2 = jnp.concatenate([bias_ref[0], bias_ref[0]], axis=0)
            lane = lax.broadcasted_iota(jnp.int32, (qn, LANES), 1)
            lo = lane < HEAD_DIM
            mask_f = lo.astype(F32)
            mask_lo = mask_f.astype(MXU_DTYPE)
            for hp in range(HEADS_B // 2):
                cols = slice(hp * LANES, (hp + 1) * LANES)
                h0, h1 = 2 * hp, 2 * hp + 1
                qp = br.load(q_ref, cols)
                kcat = jnp.concatenate([br.load(kp_ref, cols), br.load(kc_ref, cols)], axis=0)
                vcat = jnp.concatenate([br.load(vp_ref, cols), br.load(vc_ref, cols)], axis=0)
                dop = br.load(do_ref, cols)
                prod = dop * br.load(o_ref, cols)
                prod_lo = prod * mask_f
                delta = jnp.concatenate([jnp.sum(prod_lo, axis=1, keepdims=True),
                                         jnp.sum(prod - prod_lo, axis=1, keepdims=True)], axis=0)
                qs = jnp.concatenate([qp * mask_lo, qp * (1 - mask_lo)], axis=0)
                dos = jnp.concatenate([dop * mask_f, dop * (1.0 - mask_f)], axis=0).astype(MXU_DTYPE)
                lse = br.load(lse_ref, cols)
                lse2 = jnp.concatenate([lse[:, :1], lse[:, HEAD_DIM:HEAD_DIM + 1]], axis=0)
                p = jnp.exp(_dot_nt(qs, kcat) + bias2 - lse2)
                ds = (p * (_dot_nt(dos, vcat) - delta)).astype(MXU_DTYPE)
                dvc = _dot_tn(p.astype(MXU_DTYPE), dos)
                dkc = _dot_tn(ds, qs)
                dq2 = _dot(ds, kcat)
                dq = jnp.where(lo, dq2[:qn], dq2[qn:])
                dk_prev = dk_carry[:, cols] + dkc[:qn]
                dv_prev = dv_carry[:, cols] + dvc[:qn]
                if not first:
                    dq = dq + br.load(dq_in, cols)
                    dk_prev = dk_prev + br.load(dk_in, cols)
                    dv_prev = dv_prev + br.load(dv_in, cols)
                br.store(dq_ref, cols, dq)
                br.store(dk_ref, cols, dk_prev)
                br.store(dv_ref, cols, dv_prev)
                dk_carry[:, cols] = dkc[qn:]
                dv_carry[:, cols] = dvc[qn:]

        @pl.when(n == nb)
        def _():
            dk_last = dk_carry[...]
            dv_last = dv_carry[...]
            if not first:
                dk_last = dk_last + br.load(dk_in)
                dv_last = dv_last + br.load(dv_in)
            br.store(dk_ref, slice(None), dk_last)
            br.store(dv_ref, slice(None), dv_last)

    cur = lambda n: jnp.minimum(n, nb - 1)
    before = lambda n: jnp.maximum(cur(n) - 1, 0)
    late = lambda n: jnp.maximum(n - 1, 0)
    in_specs = [br.bias_spec(cur), br.spec(WIDTH_B, cur), br.spec(WIDTH_B, cur), br.spec(WIDTH_B, before),
                br.spec(WIDTH_B, cur), br.spec(WIDTH_B, before), br.spec(WIDTH_B, cur), br.spec(WIDTH_B, cur),
                br.spec(WIDTH_B, cur)]
    args = [jnp.asarray(br.bias)] + [br.view(a) for a in (q, k, k, v, v, do, o, lse)]
    if not first:
        in_specs += [br.spec(WIDTH_B, cur), br.spec(WIDTH_B, late), br.spec(WIDTH_B, late)]
        args += [br.view(g) for g in grads]
    res = pl.pallas_call(
        body, name="attn_bwd_d%d" % dil, grid=(br.grid[0], nb + 1), in_specs=in_specs,
        out_specs=[br.spec(WIDTH_B, cur), br.spec(WIDTH_B, late), br.spec(WIDTH_B, late)],
        out_shape=[jax.ShapeDtypeStruct((s // L_BLOCK, 4, 4, L_GROUP, WIDTH_B), F32)] * 3,
        scratch_shapes=[pltpu.VMEM((qn, WIDTH_B), F32), pltpu.VMEM((qn, WIDTH_B), F32)],
        compiler_params=_params(("arbitrary", "arbitrary")),
    )(*args)
    return tuple(a.reshape(s, WIDTH_B) for a in res)


def _sgu_bwd(uv, dya_n, w_tril, w_tril_t, bias, g_sgu, g_a):
    s = uv.shape[0]
    tm = 512

    def body(uv_ref, dy_ref, w_ref, wt_ref, b_ref, gs_ref, ga_ref, duv_ref, dw_ref, db_ref, dgs_ref, dga_ref,
             db_acc):
        i = pl.program_id(0)

        @pl.when(i == 0)
        def _():
            dw_ref[...] = jnp.zeros_like(dw_ref)
            dgs_ref[...] = jnp.zeros_like(dgs_ref)
            dga_ref[...] = jnp.zeros_like(dga_ref)
            db_acc[...] = jnp.zeros_like(db_acc)

        t = _sgu_forward_tile(uv_ref[...], w_ref, b_ref[...], gs_ref[...])
        na, ra = _rms_stats(t['ya'])
        dyn = dy_ref[...]
        dga_ref[...] += jnp.sum(dyn * na, axis=0, keepdims=True)
        dya = _rms_bwd(dyn * ga_ref[...], na, ra)
        dug = dya * t['mixed']
        dmixed = dya * t['ug']
        dmb = dmixed.astype(MXU_DTYPE)
        masks = _half_masks(MXU_DTYPE)
        chunks = []
        db = jnp.zeros((CHUNK, WIDTH_A), F32)
        for c in range(tm // CHUNK):
            rows = slice(c * CHUNK, (c + 1) * CHUNK)
            db = db + dmixed[rows]
            groups = []
            for gp in range(2):
                cols = slice(gp * LANES, (gp + 1) * LANES)
                dm_g = dmb[rows, cols]
                vn_g = t['vn'][rows, cols]
                dvn_g = jnp.zeros((CHUNK, LANES), F32)
                for j in range(2):
                    dm_h = dm_g * masks[j]
                    dvn_g = dvn_g + _dot(wt_ref[2 * gp + j], dm_h)
                    dw_ref[2 * gp + j] += _dot_nt(dm_h, vn_g)
                groups.append(dvn_g)
            chunks.append(jnp.concatenate(groups, axis=1))
        db_acc[...] += db
        dvn = jnp.concatenate(chunks, axis=0)
        vhat = t['vhat']
        dgs_ref[...] += jnp.sum(dvn * vhat, axis=0, keepdims=True)
        dvh = dvn * gs_ref[...]
        dvg = t['rs'] * (dvh - jnp.mean(dvh, axis=-1, keepdims=True)
                         - vhat * jnp.mean(dvh * vhat, axis=-1, keepdims=True))
        duv_ref[:, :WIDTH_A] = (dug * _gelu_grad(t['u'], t['tu'])).astype(MXU_DTYPE)
        duv_ref[:, WIDTH_A:] = (dvg * _gelu_grad(t['v'], t['tv'])).astype(MXU_DTYPE)

        @pl.when(i == pl.num_programs(0) - 1)
        def _():
            lane_a = lax.broadcasted_iota(jnp.int32, (CHUNK, WIDTH_A), 1)
            lane = lax.broadcasted_iota(jnp.int32, (CHUNK, LANES), 1)
            acc = db_acc[...]
            out = jnp.zeros((CHUNK, LANES), F32)
            for h in range(HEADS_A):
                col = jnp.sum(jnp.where(lane_a // HEAD_DIM == h, acc, 0.0), axis=1, keepdims=True)
                out = jnp.where(lane == h, col, out)
            db_ref[...] = out
            causal = (lax.broadcasted_iota(jnp.int32, (CHUNK, CHUNK), 0)
                      >= lax.broadcasted_iota(jnp.int32, (CHUNK, CHUNK), 1))
            for h in range(HEADS_A):
                dw_ref[h] = jnp.where(causal, dw_ref[h], 0.0)

    return pl.pallas_call(
        body, name="sgu_bwd", grid=(s // tm,),
        in_specs=[_rows(tm, 2 * WIDTH_A), _rows(tm, WIDTH_A), _full((HEADS_A, CHUNK, CHUNK)),
                  _full((HEADS_A, CHUNK, CHUNK)), _full((CHUNK, WIDTH_A)), _full((1, WIDTH_A)),
                  _full((1, WIDTH_A))],
        out_specs=[_rows(tm, 2 * WIDTH_A), _full((HEADS_A, CHUNK, CHUNK)), _full((CHUNK, LANES)),
                   _full((1, WIDTH_A)), _full((1, WIDTH_A))],
        out_shape=[jax.ShapeDtypeStruct((s, 2 * WIDTH_A), MXU_DTYPE),
                   jax.ShapeDtypeStruct((HEADS_A, CHUNK, CHUNK), F32), jax.ShapeDtypeStruct((CHUNK, LANES), F32),
                   jax.ShapeDtypeStruct((1, WIDTH_A), F32), jax.ShapeDtypeStruct((1, WIDTH_A), F32)],
        scratch_shapes=[pltpu.VMEM((CHUNK, WIDTH_A), F32)],
        compiler_params=_params(("arbitrary",)),
    )(uv, dya_n, w_tril, w_tril_t, bias, g_sgu, g_a)


def _in_bwd_proj(duv, dq, dk, dv, cos_t, sin_t):
    s = duv.shape[0]
    tm = 512
    nc = WIDTH_B // LANES

    def body(duv_ref, *refs):
        dq_refs, dk_refs, dv_refs = refs[:nc], refs[nc:2 * nc], refs[2 * nc:3 * nc]
        cos_ref, sin_ref, dp_ref = refs[3 * nc:]
        cos = cos_ref[...]
        sin = sin_ref[...]
        dp_ref[:, :2 * WIDTH_A] = duv_ref[...]
        for i in range(nc):
            lo = 2 * WIDTH_A + i * LANES
            tq = _load_l256(dq_refs[i:i + 1], tm) * (HEAD_DIM ** -0.5)
            tk = _load_l256(dk_refs[i:i + 1], tm)
            dp_ref[:, lo:lo + LANES] = (tq * cos + _rope_partner(tq * sin)).astype(MXU_DTYPE)
            dp_ref[:, lo + WIDTH_B:lo + WIDTH_B + LANES] = (tk * cos + _rope_partner(tk * sin)).astype(MXU_DTYPE)
            dp_ref[:, lo + 2 * WIDTH_B:lo + 2 * WIDTH_B + LANES] = _load_l256(dv_refs[i:i + 1], tm).astype(MXU_DTYPE)

    return pl.pallas_call(
        body, name="in_bwd_proj", grid=(s // tm,),
        in_specs=[_rows(tm, 2 * WIDTH_A)] + 3 * _col_specs(tm, WIDTH_B) + [_rows(tm, LANES), _rows(tm, LANES)],
        out_specs=_rows(tm, IN_COLS), out_shape=jax.ShapeDtypeStruct((s, IN_COLS), MXU_DTYPE),
        compiler_params=_params(("arbitrary",)),
    )(duv, *([dq] * nc), *([dk] * nc), *([dv] * nc), cos_t, sin_t)


def _in_bwd_x(dproj, w_in_t, x, g_mix, dh1, after):
    s = x.shape[0]
    tm = 512

    def body(dp_ref, wt_ref, x_ref, g_ref, dh_ref, *rest):
        gx_ref, dg_ref = rest[len(after):]
        @pl.when(pl.program_id(0) == 0)
        def _():
            dg_ref[...] = jnp.zeros_like(dg_ref)

        dhn = _dot(dp_ref[...], wt_ref[...])
        n, r = _rms_stats(x_ref[...])
        dg_ref[...] += jnp.sum(dhn * n, axis=0, keepdims=True)
        gx_ref[...] = dh_ref[...] + _rms_bwd(dhn * g_ref[...], n, r)

    return pl.pallas_call(
        body, name="in_bwd_x", grid=(s // tm,),
        in_specs=[_rows(tm, IN_COLS), _full((IN_COLS, D_MODEL)), _rows(tm, D_MODEL), _full((1, D_MODEL)),
                  _rows(tm, D_MODEL)] + [pl.BlockSpec(memory_space=pl.ANY)] * len(after),
        out_specs=[_rows(tm, D_MODEL), _full((1, D_MODEL))],
        out_shape=[jax.ShapeDtypeStruct((s, D_MODEL), F32), jax.ShapeDtypeStruct((1, D_MODEL), F32)],
        compiler_params=_params(("arbitrary",)),
    )(dproj, w_in_t, x, g_mix, dh1, *after)


def _wgrad(a, bs, name, transposed=False):
    s, m = a.shape
    bm = 512 if m % 512 == 0 else (FF_HALF if m == D_FF else m)
    ts = 512
    nsteps = s // ts
    nb = len(bs)

    def body(a_ref, *refs):
        b_refs, o_refs, accs = refs[:nb], refs[nb:2 * nb], refs[2 * nb:]
        kk = pl.program_id(1)
        at = a_ref[...].astype(MXU_DTYPE)
        for b_ref, o_ref, acc in zip(b_refs, o_refs, accs):
            c = _dot_tn(at, b_ref[...].astype(MXU_DTYPE))

            @pl.when(kk == 0)
            def _():
                acc[...] = c

            @pl.when(kk > 0)
            def _():
                acc[...] += c

            @pl.when(kk == nsteps - 1)
            def _():
                total = acc[...]
                o_ref[...] = (total.T if transposed else total).astype(o_ref.dtype)

    if transposed:
        out_specs = [pl.BlockSpec((b.shape[1], bm), lambda i, kk: (0, i)) for b in bs]
        out_shape = [jax.ShapeDtypeStruct((b.shape[1], m), jnp.bfloat16) for b in bs]
    else:
        out_specs = [pl.BlockSpec((bm, b.shape[1]), lambda i, kk: (i, 0)) for b in bs]
        out_shape = [jax.ShapeDtypeStruct((m, b.shape[1]), jnp.bfloat16) for b in bs]
    return pl.pallas_call(
        body, name=name, grid=(m // bm, nsteps),
        in_specs=[pl.BlockSpec((ts, bm), lambda i, kk: (kk, i))]
        + [pl.BlockSpec((ts, b.shape[1]), lambda i, kk: (kk, 0)) for b in bs],
        out_specs=out_specs, out_shape=out_shape,
        scratch_shapes=[pltpu.VMEM((bm, b.shape[1]), F32) for b in bs],
        compiler_params=_params(("arbitrary", "arbitrary")),
    )(a, *bs)


def _rope_tables(s):
    half = HEAD_DIM // 2
    inv = ROPE_THETA ** (-jnp.arange(half, dtype=F32) / half)
    ang = jnp.arange(s, dtype=F32)[:, None] * inv[None, :]
    cos = jnp.cos(ang)
    sin = jnp.sin(ang)
    cos_t = jnp.concatenate([cos, cos, cos, cos], axis=1)
    sin_t = jnp.concatenate([-sin, sin, -sin, sin], axis=1)
    return cos_t, sin_t


MESH = pl.DeviceIdType.MESH
ANY = pl.BlockSpec(memory_space=pl.ANY)
SEM = pl.BlockSpec(memory_space=pltpu.SEMAPHORE)
SPLIT_COPY = pltpu.CompilerParams(has_side_effects=pltpu.SideEffectType.DATAFLOW_SIDE_EFFECTING)
SLAB_IS_TRANSPOSED = {'w_in': True, 'w_out': False, 'w_gate': True, 'w_up': True, 'w_down': False,
                      'w_ple_gate': False, 'w_ple_proj': True}


def _place():
    x, y, c = lax.axis_index("x"), lax.axis_index("y"), lax.axis_index("c")
    other_chips = [(1 - x, y), (x, 1 - y), (1 - x, 1 - y)]
    return x, y, c, other_chips


def _chip_of(chip):
    return 2 * chip[0] + chip[1]


def _half(ref, lead, hc):
    hr = ref.shape[1] // 2
    return ref.at[lead, pl.ds(hc * hr, hr), :]


def _put_own(stack, own, index):
    return lax.dynamic_update_slice(stack, own[None], (index,) + (0,) * own.ndim)


def _all_gather_now(slab):
    rows, cols = slab.shape

    def body(x_ref, out_ref, send_sems, recv_sems):
        x, y, c, chips = _place()
        sibling = (x, y, 1 - c)
        hr = rows // 2

        def copy(k, src, dst, to):
            return pltpu.make_async_remote_copy(src_ref=src, dst_ref=dst, send_sem=send_sems.at[k],
                                                recv_sem=recv_sems.at[k], device_id=to, device_id_type=MESH)

        my_half = x_ref.at[pl.ds(c * hr, hr), :]
        first = [copy(j, my_half, _half(out_ref, 2 * x + y, c), (*chip, c)) for j, chip in enumerate(chips)]
        for cp in first:
            cp.start()
        passed = [copy(3 + j, _half(out_ref, _chip_of(chip), c), _half(out_ref, _chip_of(chip), c), sibling)
                  for j, chip in enumerate(chips)]
        for j, chip in enumerate(chips):
            copy(j, my_half, _half(out_ref, _chip_of(chip), c), (*chip, c)).wait_recv()
            passed[j].start()
        for j, chip in enumerate(chips):
            copy(3 + j, my_half, _half(out_ref, _chip_of(chip), 1 - c), sibling).wait_recv()
        for cp in first + passed:
            cp.wait_send()

    gathered = pl.pallas_call(
        body, name="all_gather_now", out_shape=jax.ShapeDtypeStruct((N_CHIPS, rows, cols), slab.dtype),
        in_specs=[ANY], out_specs=ANY,
        scratch_shapes=[pltpu.SemaphoreType.DMA((6,)), pltpu.SemaphoreType.DMA((6,))],
    )(slab)
    me = 2 * lax.axis_index("x") + lax.axis_index("y")
    return _put_own(gathered, slab, me).reshape(N_CHIPS * rows, cols)


def _gather_copies(slab_refs, land_refs, send_sems, recv_sems):
    x, y, c, chips = _place()
    sends, recvs = [], []
    for k, (src, land) in enumerate(zip(slab_refs, land_refs)):
        hr = src.shape[0] // 2
        for j, chip in enumerate(chips):
            for t in range(2):
                sends.append(pltpu.make_async_remote_copy(
                    src_ref=src.at[pl.ds(c * hr, hr), :], dst_ref=_half(land, 2 * x + y, c),
                    send_sem=send_sems.at[6 * k + 2 * j + t], recv_sem=recv_sems.at[6 * k + 2 * j + c],
                    device_id=(*chip, t), device_id_type=MESH))
                recvs.append(pltpu.make_async_remote_copy(
                    src_ref=src.at[pl.ds(t * hr, hr), :], dst_ref=_half(land, _chip_of(chip), t),
                    send_sem=send_sems.at[6 * k + 2 * j + t], recv_sem=recv_sems.at[6 * k + 2 * j + t],
                    device_id=(*chip, t), device_id_type=MESH))
    return sends, recvs


def _all_gather_start(slabs, after):
    n = len(slabs)

    def body(*refs):
        slab_refs, land_refs = refs[:n], refs[n:2 * n]
        send_sems, recv_sems = refs[2 * n + 1:2 * n + 3]
        token = refs[-1]
        sends, _ = _gather_copies(slab_refs, land_refs, send_sems, recv_sems)
        for cp in sends:
            cp.start()
        token[...] = jnp.zeros_like(token)

    lands = [lax.empty((N_CHIPS,) + s.shape, s.dtype) for s in slabs]
    hbm = lambda a: pltpu.HBM(a.shape, a.dtype)
    res = pl.pallas_call(
        body, name="all_gather_start",
        out_shape=(pltpu.SemaphoreType.DMA((6 * n,)), pltpu.SemaphoreType.DMA((6 * n,)), *map(hbm, slabs),
                   *map(hbm, lands), jax.ShapeDtypeStruct((8, LANES), F32)),
        in_specs=[ANY] * (2 * n + 1),
        out_specs=(SEM, SEM, *([ANY] * (2 * n)), pl.BlockSpec(memory_space=pltpu.VMEM)),
        input_output_aliases={i: 2 + i for i in range(2 * n)}, compiler_params=SPLIT_COPY,
    )(*[pltpu.with_memory_space_constraint(a, pltpu.HBM) for a in list(slabs) + lands], after)
    return res[:-1], res[-1]


def _all_gather_wait(handle, after):
    send_sems, recv_sems = handle[:2]
    n = (len(handle) - 2) // 2
    slabs, lands = handle[2:2 + n], handle[2 + n:]

    def body(*refs):
        slab_refs, land_refs = refs[:n], refs[n:2 * n]
        send_sems, recv_sems = refs[2 * n:2 * n + 2]
        sends, recvs = _gather_copies(slab_refs, land_refs, send_sems, recv_sems)
        for cp in sends:
            cp.wait_send()
        for cp in recvs:
            cp.wait_recv()

    hbm = lambda a: pltpu.HBM(a.shape, a.dtype)
    res = pl.pallas_call(
        body, name="all_gather_wait", out_shape=tuple(map(hbm, list(slabs) + list(lands))),
        in_specs=[ANY] * (2 * n) + [SEM, SEM, ANY], out_specs=tuple([ANY] * (2 * n)),
        input_output_aliases={i: i for i in range(2 * n)}, compiler_params=SPLIT_COPY,
    )(*slabs, *lands, send_sems, recv_sems, after)
    me = 2 * lax.axis_index("x") + lax.axis_index("y")
    return [_put_own(land, slab, me).reshape(N_CHIPS * slab.shape[0], slab.shape[1])
            for slab, land in zip(res[:n], res[n:])]


def _scatter_copies(part_refs, land_refs, send_sems, recv_sems):
    x, y, c, chips = _place()
    me = 4 * x + 2 * y + c
    sends, recvs = [], []
    for k, (part, land) in enumerate(zip(part_refs, land_refs)):
        for j, chip in enumerate(chips):
            for t in range(2):
                sends.append(pltpu.make_async_remote_copy(
                    src_ref=part.at[_chip_of(chip)], dst_ref=land.at[me],
                    send_sem=send_sems.at[7 * k + 2 * j + t], recv_sem=recv_sems.at[7 * k + 2 * j + c],
                    device_id=(*chip, t), device_id_type=MESH))
                recvs.append(pltpu.make_async_remote_copy(
                    src_ref=part.at[_chip_of(chip)], dst_ref=land.at[2 * _chip_of(chip) + t],
                    send_sem=send_sems.at[7 * k + 2 * j + t], recv_sem=recv_sems.at[7 * k + 2 * j + t],
                    device_id=(*chip, t), device_id_type=MESH))
        sends.append(pltpu.make_async_remote_copy(
            src_ref=part.at[2 * x + y], dst_ref=land.at[me], send_sem=send_sems.at[7 * k + 6],
            recv_sem=recv_sems.at[7 * k + 6], device_id=(x, y, 1 - c), device_id_type=MESH))
        recvs.append(pltpu.make_async_remote_copy(
            src_ref=part.at[2 * x + y], dst_ref=land.at[4 * x + 2 * y + 1 - c],
            send_sem=send_sems.at[7 * k + 6], recv_sem=recv_sems.at[7 * k + 6], device_id=(x, y, 1 - c),
            device_id_type=MESH))
    return sends, recvs


def _reduce_scatter_start(parts, name):
    n = len(parts)
    parts = [p.reshape(N_CHIPS, p.shape[0] // N_CHIPS, p.shape[1]) for p in parts]

    def body(*refs):
        part_refs, land_refs = refs[:n], refs[n:2 * n]
        send_sems, recv_sems = refs[2 * n:2 * n + 2]
        token = refs[-1]
        sends, _ = _scatter_copies(part_refs, land_refs, send_sems, recv_sems)
        for cp in sends:
            cp.start()
        token[...] = jnp.zeros_like(token)

    lands = [lax.empty((N_DEV, p.shape[1], p.shape[2]), p.dtype) for p in parts]
    hbm = lambda a: pltpu.HBM(a.shape, a.dtype)
    res = pl.pallas_call(
        body, name=name,
        out_shape=(pltpu.SemaphoreType.DMA((7 * n,)), pltpu.SemaphoreType.DMA((7 * n,)), *map(hbm, parts),
                   *map(hbm, lands), jax.ShapeDtypeStruct((8, LANES), F32)),
        in_specs=[ANY] * (2 * n), out_specs=(SEM, SEM, *([ANY] * (2 * n)), pl.BlockSpec(memory_space=pltpu.VMEM)),
        input_output_aliases={i: 2 + i for i in range(2 * n)}, compiler_params=SPLIT_COPY,
    )(*[pltpu.with_memory_space_constraint(a, pltpu.HBM) for a in parts + lands])
    return res[:-1], res[-1]


def _reduce_scatter_wait(handle, after, name):
    send_sems, recv_sems = handle[:2]
    n = (len(handle) - 2) // 2
    parts, lands = handle[2:2 + n], handle[2 + n:]

    def body(*refs):
        part_refs, land_refs = refs[:n], refs[n:2 * n]
        send_sems, recv_sems = refs[2 * n:2 * n + 2]
        sends, recvs = _scatter_copies(part_refs, land_refs, send_sems, recv_sems)
        for cp in sends:
            cp.wait_send()
        for cp in recvs:
            cp.wait_recv()

    hbm = lambda a: pltpu.HBM(a.shape, a.dtype)
    res = pl.pallas_call(
        body, name=name, out_shape=tuple(map(hbm, list(parts) + list(lands))),
        in_specs=[ANY] * (2 * n) + [SEM, SEM, ANY], out_specs=tuple([ANY] * (2 * n)),
        input_output_aliases={i: i for i in range(2 * n)}, compiler_params=SPLIT_COPY,
    )(*parts, *lands, send_sems, recv_sems, after)
    return list(zip(res[:n], res[n:]))


def _small_all_reduce(block):
    rows = block.shape[0]

    def body(x_ref, all_ref, sum_ref, send_sems, recv_sems, local_sem):
        x, y, c, chips = _place()
        me, sibling = (x, y, c), (x, y, 1 - c)

        def blk(px, py, pc):
            return all_ref.at[pl.ds((4 * px + 2 * py + pc) * rows, rows), :]

        def copy(k, who, to, src=None):
            return pltpu.make_async_remote_copy(
                src_ref=blk(*who) if src is None else src, dst_ref=blk(*who), send_sem=send_sems.at[k],
                recv_sem=recv_sems.at[k], device_id=to, device_id_type=MESH)

        mine = pltpu.make_async_copy(x_ref, blk(*me), local_sem)
        mine.start()
        first = [copy(0, me, sibling, src=x_ref)]
        first += [copy(1 + j, me, (*chip, c), src=x_ref) for j, chip in enumerate(chips)]
        for cp in first:
            cp.start()
        passed = [copy(4 + j, (*chip, c), sibling) for j, chip in enumerate(chips)]
        for j, chip in enumerate(chips):
            copy(1 + j, (*chip, c), me).wait_recv()
            passed[j].start()
        copy(0, sibling, me).wait_recv()
        for j, chip in enumerate(chips):
            copy(4 + j, (*chip, 1 - c), me).wait_recv()
        for cp in first + passed:
            cp.wait_send()
        mine.wait()
        acc = all_ref[pl.ds(0, rows), :]
        for dev in range(1, N_DEV):
            acc = acc + all_ref[pl.ds(dev * rows, rows), :]
        sum_ref[...] = acc

    vmem = pl.BlockSpec(memory_space=pltpu.VMEM)
    return pl.pallas_call(
        body, name="small_all_reduce",
        out_shape=[jax.ShapeDtypeStruct((N_DEV * rows, D_MODEL), F32), jax.ShapeDtypeStruct((rows, D_MODEL), F32)],
        in_specs=[vmem], out_specs=[vmem, vmem],
        scratch_shapes=[pltpu.SemaphoreType.DMA((7,)), pltpu.SemaphoreType.DMA((7,)), pltpu.SemaphoreType.DMA],
    )(block)[1]


def _adamw(w, g, m, v, name):
    rows, cols = w.shape
    tm = rows
    if rows > 512:
        tm = next(t for t in range(512, 7, -8) if rows % t == 0)

    def body(w_ref, g_ref, m_ref, v_ref, d_ref, nm_ref, nv_ref):
        g_ = g_ref[...]
        m_ = ADAM_B1 * m_ref[...] + (1.0 - ADAM_B1) * g_
        v_ = ADAM_B2 * v_ref[...] + (1.0 - ADAM_B2) * (g_ * g_)
        m_hat = m_ / (1.0 - ADAM_B1 ** ADAM_STEP)
        v_hat = v_ / (1.0 - ADAM_B2 ** ADAM_STEP)
        d_ref[...] = -ADAM_LR * (m_hat / (jnp.sqrt(v_hat) + ADAM_EPS) + ADAM_WD * w_ref[...])
        nm_ref[...] = m_
        nv_ref[...] = v_

    spec = pl.BlockSpec((tm, cols), lambda i: (i, 0))
    return pl.pallas_call(
        body, name=name, grid=(rows // tm,), in_specs=[spec] * 4, out_specs=[spec] * 3,
        out_shape=[jax.ShapeDtypeStruct(w.shape, F32)] * 3, compiler_params=_params(("arbitrary",)),
    )(w, g, m, v)


def _adamw_of_shares(w, own, land, m, v, name):
    rows, cols = w.shape
    tm = rows // 4
    x, y, c = lax.axis_index("x"), lax.axis_index("y"), lax.axis_index("c")
    where = jnp.stack([2 * x + y, 4 * x + 2 * y + c]).astype(jnp.int32)

    def body(where_ref, w_ref, own_ref, land_ref, m_ref, v_ref, g_ref, d_ref, nm_ref, nv_ref):
        me = where_ref[1]
        g_ = jnp.zeros((tm, cols), F32)
        for dev in range(N_DEV):
            g_ = g_ + jnp.where(me == dev, own_ref[0], land_ref[dev]).astype(F32)
        m_ = ADAM_B1 * m_ref[...] + (1.0 - ADAM_B1) * g_
        v_ = ADAM_B2 * v_ref[...] + (1.0 - ADAM_B2) * (g_ * g_)
        m_hat = m_ / (1.0 - ADAM_B1 ** ADAM_STEP)
        v_hat = v_ / (1.0 - ADAM_B2 ** ADAM_STEP)
        g_ref[...] = g_
        d_ref[...] = -ADAM_LR * (m_hat / (jnp.sqrt(v_hat) + ADAM_EPS) + ADAM_WD * w_ref[...])
        nm_ref[...] = m_
        nv_ref[...] = v_

    tile = pl.BlockSpec((tm, cols), lambda i, where_ref: (i, 0))
    spec = pltpu.PrefetchScalarGridSpec(
        num_scalar_prefetch=1, grid=(rows // tm,),
        in_specs=[tile, pl.BlockSpec((1, tm, cols), lambda i, where_ref: (where_ref[0], i, 0)),
                  pl.BlockSpec((N_DEV, tm, cols), lambda i, where_ref: (0, i, 0)), tile, tile],
        out_specs=[tile] * 4)
    return pl.pallas_call(
        body, name=name, grid_spec=spec, out_shape=[jax.ShapeDtypeStruct(w.shape, F32)] * 4,
        compiler_params=_params(("arbitrary",)),
    )(where, w, own, land, m, v)


def _pack_small(values):
    flat = jnp.concatenate([values[n].reshape(-1).astype(F32) for n in SMALL])
    return jnp.pad(flat, (0, SMALL_ROWS * D_MODEL - flat.shape[0])).reshape(SMALL_ROWS, D_MODEL)


def _unpack_small(block, shapes):
    flat = block.reshape(-1)
    out, lo = {}, 0
    for n in SMALL:
        out[n] = flat[lo:lo + SMALL_SIZES[n]].reshape(shapes[n])
        lo += SMALL_SIZES[n]
    return out


def _after(token, a):
    return a + token[:1, :1].astype(a.dtype)


def kernel(x, p, mix_norm_g, w_in, sgu_w, sgu_b, sgu_norm_g, out_norm_a, out_norm_b, w_out, ffn_norm_g, w_gate, w_up, w_down, ple_norm_g, w_ple_gate, w_ple_proj, final_norm_g, loss_target, m_mix_norm_g, m_w_in, m_sgu_w, m_sgu_b, m_sgu_norm_g, m_out_norm_a, m_out_norm_b, m_w_out, m_ffn_norm_g, m_w_gate, m_w_up, m_w_down, m_ple_norm_g, m_w_ple_gate, m_w_ple_proj, m_final_norm_g, v_mix_norm_g, v_w_in, v_sgu_w, v_sgu_b, v_sgu_norm_g, v_out_norm_a, v_out_norm_b, v_w_out, v_ffn_norm_g, v_w_gate, v_w_up, v_w_down, v_ple_norm_g, v_w_ple_gate, v_w_ple_proj, v_final_norm_g):
    given = dict(locals())
    drop_lead = lambda a, lead: a.reshape(a.shape[lead:])
    xs, ps, target = drop_lead(x, 1), drop_lead(p, 2), drop_lead(loss_target, 1)
    s = xs.shape[0]
    shard = lambda name: drop_lead(given[name], 1)

    def slab_of(name):
        local = shard(name).astype(MXU_DTYPE)
        return local.T if SLAB_IS_TRANSPOSED[name] else local

    w_in_t = _all_gather_now(slab_of('w_in'))
    later = ['w_out', 'w_gate', 'w_up', 'w_down', 'w_ple_gate', 'w_ple_proj']
    gather, token = _all_gather_start([slab_of(n) for n in later], w_in_t)

    cos_t, sin_t = _rope_tables(s)
    tril = jnp.tril(jnp.ones((CHUNK, CHUNK), F32))
    w_tril = (sgu_w.reshape(HEADS_A, CHUNK, CHUNK) * tril).astype(MXU_DTYPE)
    w_tril_t = jnp.swapaxes(w_tril, 1, 2)
    bias = jnp.repeat(sgu_b.reshape(HEADS_A, CHUNK).T, HEAD_DIM, axis=1)
    g = {n: given[n].reshape(1, -1) for n in SMALL if n not in ('sgu_w', 'sgu_b')}

    uv, q, k, v, hn1 = _in_fwd(xs, _after(token, g['mix_norm_g']), w_in_t.T, cos_t, sin_t)
    ya_n = _sgu_fwd(uv, w_tril, bias, g['sgu_norm_g'], g['out_norm_a'])
    branches = [_attn_fwd_branch(q, k, v, dil) for dil in DILATIONS]
    y_b, lse = _attn_merge([o for o, _ in branches], [l for _, l in branches])
    stacks = dict(zip(later, _all_gather_wait(gather, lse)))
    w_gate_t, w_up_t, w_pp_t = stacks['w_gate'], stacks['w_up'], stacks['w_ple_proj']
    h1, y_n = _out_fwd(ya_n, y_b, g['out_norm_b'], stacks['w_out'], xs)
    h2, gate, up, hn2 = _ffn_fwd(h1, g['ffn_norm_g'], w_gate_t.T, w_up_t.T, stacks['w_down'])
    loss, dh2, dz, dpp, hn3, d_ple_g, d_final_g = _ple_loss(
        h2, ps, target, g['ple_norm_g'], stacks['w_ple_gate'], stacks['w_ple_gate'].T, w_pp_t.T, g['final_norm_g'])

    share = {}
    share['w_ple_gate'], = _wgrad(hn3, [dz], "wgrad_ple_gate")
    share['w_ple_proj'], = _wgrad(ps, [dpp], "wgrad_ple_proj", transposed=True)
    scatter_1, token = _reduce_scatter_start([share['w_ple_gate'], share['w_ple_proj']], "reduce_scatter_start_1")
    dh1, act, dgate, dup, d_ffn_g = _ffn_bwd(dh2, h1, gate, up, _after(token, g['ffn_norm_g']), stacks['w_down'].T,
                                             w_gate_t, w_up_t)
    share['w_down'], = _wgrad(act, [dh2], "wgrad_down")
    share['w_gate'], share['w_up'] = _wgrad(hn2, [dgate, dup], "wgrad_gate_up", transposed=True)
    scatter_2, token = _reduce_scatter_start([share['w_down'], share['w_gate'], share['w_up']],
                                             "reduce_scatter_start_2")
    dya_n, dyb, d_out_b = _out_bwd(dh1, y_b, _after(token, g['out_norm_b']), stacks['w_out'].T)
    share['w_out'], = _wgrad(y_n, [dh1], "wgrad_out")
    scatter_3, token = _reduce_scatter_start([share['w_out']], "reduce_scatter_start_3")
    grads = None
    for dil in DILATIONS:
        grads = _attn_bwd_branch(q, k, v, dyb, y_b, lse, grads, dil)
    duv, d_sgu_w, d_sgu_b, d_sgu_g, d_out_a = _sgu_bwd(uv, dya_n, w_tril, w_tril_t, bias,
                                                       _after(token, g['sgu_norm_g']), g['out_norm_a'])
    dproj = _in_bwd_proj(duv, grads[0], grads[1], grads[2], cos_t, sin_t)
    share['w_in'], = _wgrad(hn1, [dproj], "wgrad_in", transposed=True)
    scatter_4, token = _reduce_scatter_start([share['w_in']], "reduce_scatter_start_4")

    grads, deltas, new_m, new_v = {}, {}, {}, {}
    add_lead = lambda a: a.reshape((1,) + a.shape)

    def finish(names, handles, after, tag):
        landed = []
        for i, handle in enumerate(handles):
            landed += _reduce_scatter_wait(handle, after, "reduce_scatter_wait_%s%d" % (tag, i))
        for n, (own, land) in zip(names, landed):
            turn = (lambda a: a.T) if SLAB_IS_TRANSPOSED[n] else (lambda a: a)
            res = _adamw_of_shares(turn(shard(n)), own, land, turn(shard("m_" + n)), turn(shard("v_" + n)),
                                   "adamw_" + n)
            grads[n], deltas[n], new_m[n], new_v[n] = (add_lead(turn(a)) for a in res)

    finish(['w_ple_gate', 'w_ple_proj', 'w_down', 'w_gate', 'w_up', 'w_out'], [scatter_1, scatter_2, scatter_3], token,
           "early")
    grad_x, d_mix_g = _in_bwd_x(dproj, w_in_t, xs, g['mix_norm_g'], dh1,
                                after=[new_v[n] for n in ('w_down', 'w_gate', 'w_up', 'w_out')])

    gs = {'mix_norm_g': d_mix_g, 'sgu_w': d_sgu_w, 'sgu_b': d_sgu_b[:, :HEADS_A].T, 'sgu_norm_g': d_sgu_g,
          'out_norm_a': d_out_a, 'out_norm_b': d_out_b, 'ffn_norm_g': d_ffn_g, 'ple_norm_g': d_ple_g,
          'final_norm_g': d_final_g}
    gs_block = _pack_small(gs).at[SMALL_ROWS - 1, 0].set(loss[0, 0])
    small_sum = _small_all_reduce(gs_block)
    loss_out = small_sum[SMALL_ROWS - 1, 0]
    small_shapes = {n: given[n].shape for n in SMALL}
    finish(['w_in'], [scatter_4], small_sum, "last")

    small = {n: given[n] for n in SMALL}
    d, nm, nv = _adamw(_pack_small(small), small_sum, _pack_small({n: given["m_" + n] for n in SMALL}),
                       _pack_small({n: given["v_" + n] for n in SMALL}), "adamw_small")
    for res, blk in ((grads, small_sum), (deltas, d), (new_m, nm), (new_v, nv)):
        res.update(_unpack_small(blk, small_shapes))

    outs = [loss_out, add_lead(grad_x)]
    for res in (grads, deltas, new_m, new_v):
        outs += [res[n] for n in WEIGHT_NAMES]
    return tuple(outs)
```

```python
import functools
import math

import jax
import jax.numpy as jnp
import numpy as np
from jax import lax
from jax.experimental import pallas as pl
from jax.experimental.pallas import tpu as pltpu

F32 = jnp.float32
MXU_DTYPE = jnp.bfloat16

D_MODEL = 1024
HEAD_DIM = 64
HEADS_A = 4
HEADS_B = 12
WIDTH_A = HEADS_A * HEAD_DIM
WIDTH_B = HEADS_B * HEAD_DIM
CHUNK = 128
BLOCK = 128
DILATIONS = (1, 4, 16)
ROPE_THETA = 10000.0
D_FF = 2816
FF_HALF = D_FF // 2
PLE_DIM = 256
IN_COLS = 2 * WIDTH_A + 3 * WIDTH_B
EPS = 1e-6
LANES = 128
N_CHIPS = 4
N_DEV = 8

ADAM_LR = 0.001
ADAM_B1 = 0.9
ADAM_B2 = 0.999
ADAM_EPS = 1e-08
ADAM_WD = 0.01
ADAM_STEP = 10

VMEM_LIMIT = 56 * 1024 * 1024

WEIGHT_NAMES = ['mix_norm_g', 'w_in', 'sgu_w', 'sgu_b', 'sgu_norm_g', 'out_norm_a', 'out_norm_b', 'w_out',
                'ffn_norm_g', 'w_gate', 'w_up', 'w_down', 'ple_norm_g', 'w_ple_gate', 'w_ple_proj', 'final_norm_g']
SHARDED = ['w_in', 'w_out', 'w_gate', 'w_up', 'w_down', 'w_ple_gate', 'w_ple_proj']
SMALL = ['mix_norm_g', 'sgu_w', 'sgu_b', 'sgu_norm_g', 'out_norm_a', 'out_norm_b', 'ffn_norm_g', 'ple_norm_g',
         'final_norm_g']
SMALL_SIZES = {'mix_norm_g': 1024, 'sgu_w': 65536, 'sgu_b': 512, 'sgu_norm_g': 256, 'out_norm_a': 256,
               'out_norm_b': 768, 'ffn_norm_g': 1024, 'ple_norm_g': 1024, 'final_norm_g': 1024}
SMALL_ROWS = 72


def _params(semantics=None):
    return pltpu.CompilerParams(dimension_semantics=semantics, vmem_limit_bytes=VMEM_LIMIT)


def _full(shape):
    nd = len(shape)
    return pl.BlockSpec(shape, lambda i: (0,) * nd)


def _rows(tm, width):
    return pl.BlockSpec((tm, width), lambda i: (i, 0))


def _rms_stats(x):
    r = lax.rsqrt(jnp.mean(x * x, axis=-1, keepdims=True) + EPS)
    return x * r, r


def _rms_bwd(dn, n, r):
    return r * (dn - n * jnp.mean(dn * n, axis=-1, keepdims=True))


def _dot(a, b):
    return jnp.dot(a, b, preferred_element_type=F32)


def _dot_nt(a, b):
    return lax.dot_general(a, b, (((1,), (1,)), ((), ())), preferred_element_type=F32)


def _dot_tn(a, b):
    return lax.dot_general(a, b, (((0,), (0,)), ((), ())), preferred_element_type=F32)


def _gelu_parts(x):
    c = math.sqrt(2.0 / math.pi)
    t = jnp.tanh(c * (x + 0.044715 * x * x * x))
    return 0.5 * x * (1.0 + t), t


def _gelu_grad(x, t):
    c = math.sqrt(2.0 / math.pi)
    return 0.5 * (1.0 + t) + 0.5 * x * (1.0 - t * t) * c * (1.0 + 3.0 * 0.044715 * x * x)


def _half_masks(dtype):
    lane = lax.broadcasted_iota(jnp.int32, (BLOCK, LANES), 1)
    lo = (lane < HEAD_DIM).astype(F32)
    return lo.astype(dtype), (1.0 - lo).astype(dtype)


def _rope_partner(t):
    lane = lax.broadcasted_iota(jnp.int32, t.shape, 1)
    first_half = (lane % HEAD_DIM) < (HEAD_DIM // 2)
    return jnp.where(first_half, pltpu.roll(t, LANES - HEAD_DIM // 2, 1), pltpu.roll(t, HEAD_DIM // 2, 1))


L_BLOCK = 256
L_GROUP = 16


def _store_l256(scr, out_ref, cols, value):
    tm = value.shape[0]
    scr[...] = value
    for blk in range(tm // L_BLOCK):
        for r in range(L_GROUP):
            lo = blk * L_BLOCK + r * L_GROUP
            piece = scr[pl.ds(blk * L_BLOCK + r, L_GROUP, stride=L_GROUP), :]
            out_ref[lo:lo + L_GROUP, cols] = piece.astype(out_ref.dtype)


def _load_l256(col_refs, tm):
    cols = []
    for ref in col_refs:
        pieces = [ref[pl.ds(blk * L_BLOCK + i, L_GROUP, stride=L_GROUP), :]
                  for blk in range(tm // L_BLOCK) for i in range(L_GROUP)]
        cols.append(jnp.concatenate(pieces, axis=0))
    return jnp.concatenate(cols, axis=1)


def _col_specs(tm, width):
    return [pl.BlockSpec((tm, LANES), lambda i, j=j: (i, j)) for j in range(width // LANES)]


def _in_fwd(x, g_mix, w_in, cos_t, sin_t):
    s = x.shape[0]
    tm = 512

    def body(x_ref, g_ref, w_ref, cos_ref, sin_ref, uv_ref, q_ref, k_ref, v_ref, hn_ref, scr):
        n, _ = _rms_stats(x_ref[...])
        hn = (n * g_ref[...]).astype(MXU_DTYPE)
        hn_ref[...] = hn
        proj = _dot(hn, w_ref[...])
        uv_ref[...] = proj[:, :2 * WIDTH_A]
        cos = cos_ref[...]
        sin = sin_ref[...]
        for i in range(WIDTH_B // LANES):
            lo = 2 * WIDTH_A + i * LANES
            tq = proj[:, lo:lo + LANES]
            tk = proj[:, lo + WIDTH_B:lo + WIDTH_B + LANES]
            tv = proj[:, lo + 2 * WIDTH_B:lo + 2 * WIDTH_B + LANES]
            cols = slice(i * LANES, (i + 1) * LANES)
            _store_l256(scr, q_ref, cols, (tq * cos + _rope_partner(tq) * sin) * (HEAD_DIM ** -0.5))
            _store_l256(scr, k_ref, cols, tk * cos + _rope_partner(tk) * sin)
            _store_l256(scr, v_ref, cols, tv)

    return pl.pallas_call(
        body, name="in_fwd", grid=(s // tm,), scratch_shapes=[pltpu.VMEM((tm, LANES), F32)],
        in_specs=[_rows(tm, D_MODEL), _full((1, D_MODEL)), _full((D_MODEL, IN_COLS)), _rows(tm, LANES),
                  _rows(tm, LANES)],
        out_specs=[_rows(tm, 2 * WIDTH_A), _rows(tm, WIDTH_B), _rows(tm, WIDTH_B), _rows(tm, WIDTH_B),
                   _rows(tm, D_MODEL)],
        out_shape=[jax.ShapeDtypeStruct((s, 2 * WIDTH_A), F32), jax.ShapeDtypeStruct((s, WIDTH_B), MXU_DTYPE),
                   jax.ShapeDtypeStruct((s, WIDTH_B), MXU_DTYPE), jax.ShapeDtypeStruct((s, WIDTH_B), MXU_DTYPE),
                   jax.ShapeDtypeStruct((s, D_MODEL), MXU_DTYPE)],
        compiler_params=_params(("arbitrary",)),
    )(x, g_mix, w_in, cos_t, sin_t)


class _Branch:
    def __init__(self, dil, s, qn=BLOCK):
        self.dil = dil
        i = np.arange(L_GROUP)
        if dil == 16:
            nblk = qn // 16
            self.grid = (16, s // (L_BLOCK * nblk))
            self.shape = (nblk, 1, 1, L_GROUP)
            self.index = lambda r, n: (n, r // 4, r % 4, 0, 0)
            pos = (np.arange(nblk)[:, None] * 16 + i[None, :]).reshape(-1)
        elif dil == 4:
            nblk = qn // 64
            self.grid = (4, s // (L_BLOCK * nblk))
            self.shape = (nblk, 4, 1, L_GROUP)
            self.index = lambda r, n: (n, 0, r, 0, 0)
            pos = (np.arange(nblk)[:, None, None] * 64 + np.arange(4)[None, :, None]
                   + 4 * i[None, None, :]).reshape(-1)
        else:
            self.grid = (1, s // L_BLOCK)
            self.shape = (1, 4, 4, L_GROUP)
            self.index = lambda r, n: (n, 0, 0, 0, 0)
            pos = (np.arange(16)[:, None] + 16 * i[None, :]).reshape(-1)
        self.qn = pos.shape[0]
        self.nb = self.grid[1]
        dist = pos[:, None] - np.concatenate([pos - self.qn, pos])[None, :]
        band = (dist >= 0) & (dist <= BLOCK)
        start = band & (np.arange(2 * self.qn)[None, :] >= self.qn)
        self.bias = np.where(np.stack([band, start]), 0.0, -np.inf).astype(np.float32)

    def view(self, a):
        return a.reshape(a.shape[0] // L_BLOCK, 4, 4, L_GROUP, a.shape[1])

    def spec(self, w, step=lambda n: n):
        return pl.BlockSpec(self.shape + (w,), lambda r, n: self.index(r, step(n)))

    def bias_spec(self, step=lambda n: n):
        return pl.BlockSpec((1, self.qn, 2 * self.qn), lambda r, n: (jnp.where(step(n) == 0, 1, 0), 0, 0))

    def load(self, ref, cols=slice(None)):
        x = ref[:, :, :, :, cols]
        return x.reshape(self.qn, x.shape[-1])

    def store(self, ref, cols, value):
        ref[:, :, :, :, cols] = value.reshape(self.shape + (value.shape[-1],))


def _attn_fwd_branch(q, k, v, dil):
    s = q.shape[0]
    br = _Branch(dil, s)
    qn = br.qn

    def body(bias_ref, q_ref, kc_ref, kp_ref, vc_ref, vp_ref, o_ref, lse_ref):
        bias2 = jnp.concatenate([bias_ref[0], bias_ref[0]], axis=0)
        lo = lax.broadcasted_iota(jnp.int32, (qn, LANES), 1) < HEAD_DIM
        mask_lo = lo.astype(F32).astype(MXU_DTYPE)
        for hp in range(HEADS_B // 2):
            cols = slice(hp * LANES, (hp + 1) * LANES)
            qp = br.load(q_ref, cols)
            kcat = jnp.concatenate([br.load(kp_ref, cols), br.load(kc_ref, cols)], axis=0)
            vcat = jnp.concatenate([br.load(vp_ref, cols), br.load(vc_ref, cols)], axis=0)
            sc = _dot_nt(jnp.concatenate([qp * mask_lo, qp * (1 - mask_lo)], axis=0), kcat) + bias2
            m = jnp.max(sc, axis=1, keepdims=True)
            p = jnp.exp(sc - m)
            l = jnp.sum(p, axis=1, keepdims=True)
            out = _dot(p.astype(MXU_DTYPE), vcat) / l
            lse = m + jnp.log(l)
            br.store(o_ref, cols, jnp.where(lo, out[:qn], out[qn:]))
            br.store(lse_ref, cols, jnp.where(lo, lse[:qn], lse[qn:]))

    before = lambda n: jnp.maximum(n - 1, 0)
    res = pl.pallas_call(
        body, name="attn_fwd_d%d" % dil, grid=br.grid,
        in_specs=[br.bias_spec(), br.spec(WIDTH_B), br.spec(WIDTH_B), br.spec(WIDTH_B, before), br.spec(WIDTH_B),
                  br.spec(WIDTH_B, before)],
        out_specs=[br.spec(WIDTH_B), br.spec(WIDTH_B)],
        out_shape=[jax.ShapeDtypeStruct((s // L_BLOCK, 4, 4, L_GROUP, WIDTH_B), F32)] * 2,
        compiler_params=_params(("arbitrary", "arbitrary")),
    )(jnp.asarray(br.bias), br.view(q), br.view(k), br.view(k), br.view(v), br.view(v))
    return tuple(a.reshape(s, WIDTH_B) for a in res)


def _attn_merge(outs, lses):
    s = outs[0].shape[0]
    tm = 512
    nbr = len(outs)

    def body(*refs):
        o_refs, l_refs, y_ref, lse_ref = refs[:nbr], refs[nbr:2 * nbr], refs[2 * nbr], refs[2 * nbr + 1]
        ls = [r[...] for r in l_refs]
        top = functools.reduce(jnp.maximum, ls)
        ws = [jnp.exp(l - top) for l in ls]
        den = functools.reduce(jnp.add, ws)
        num = functools.reduce(jnp.add, [w * r[...] for w, r in zip(ws, o_refs)])
        y_ref[...] = num / den
        lse_ref[...] = top + jnp.log(den)

    return pl.pallas_call(
        body, name="attn_merge", grid=(s // tm,), in_specs=[_rows(tm, WIDTH_B)] * (2 * nbr),
        out_specs=[_rows(tm, WIDTH_B)] * 2, out_shape=[jax.ShapeDtypeStruct((s, WIDTH_B), F32)] * 2,
        compiler_params=_params(("arbitrary",)),
    )(*outs, *lses)


def _sgu_forward_tile(uv, w_ref, bias, g_sgu):
    tm = uv.shape[0]
    u = uv[:, :WIDTH_A]
    v = uv[:, WIDTH_A:]
    ug, tu = _gelu_parts(u)
    vg, tv = _gelu_parts(v)
    mu = jnp.mean(vg, axis=-1, keepdims=True)
    vc = vg - mu
    rs = lax.rsqrt(jnp.mean(vc * vc, axis=-1, keepdims=True) + EPS)
    vhat = vc * rs
    vn = (vhat * g_sgu).astype(MXU_DTYPE)
    masks = _half_masks(MXU_DTYPE)
    chunks = []
    for c in range(tm // CHUNK):
        rows = slice(c * CHUNK, (c + 1) * CHUNK)
        groups = []
        for gp in range(2):
            vn_g = vn[rows, gp * LANES:(gp + 1) * LANES]
            groups.append(_dot(w_ref[2 * gp], vn_g * masks[0]) + _dot(w_ref[2 * gp + 1], vn_g * masks[1]))
        chunks.append(jnp.concatenate(groups, axis=1) + bias)
    mixed = jnp.concatenate(chunks, axis=0)
    return dict(u=u, v=v, ug=ug, tu=tu, tv=tv, rs=rs, vhat=vhat, vn=vn, mixed=mixed, ya=ug * mixed)


def _sgu_fwd(uv, w_tril, bias, g_sgu, g_a):
    s = uv.shape[0]
    tm = 512

    def body(uv_ref, w_ref, b_ref, gs_ref, ga_ref, o_ref):
        t = _sgu_forward_tile(uv_ref[...], w_ref, b_ref[...], gs_ref[...])
        n, _ = _rms_stats(t['ya'])
        o_ref[...] = (n * ga_ref[...]).astype(MXU_DTYPE)

    return pl.pallas_call(
        body, name="sgu_fwd", grid=(s // tm,),
        in_specs=[_rows(tm, 2 * WIDTH_A), _full((HEADS_A, CHUNK, CHUNK)), _full((CHUNK, WIDTH_A)),
                  _full((1, WIDTH_A)), _full((1, WIDTH_A))],
        out_specs=_rows(tm, WIDTH_A), out_shape=jax.ShapeDtypeStruct((s, WIDTH_A), MXU_DTYPE),
        compiler_params=_params(("arbitrary",)),
    )(uv, w_tril, bias, g_sgu, g_a)


def _out_fwd(ya_n, y_b, g_b, w_out, x):
    s = x.shape[0]
    tm = 512
    nc = WIDTH_B // LANES

    def body(ya_ref, *refs):
        yb_refs = refs[:nc]
        g_ref, w_ref, x_ref, h_ref, yn_ref = refs[nc:]
        n, _ = _rms_stats(_load_l256(yb_refs, tm))
        yn = jnp.concatenate([ya_ref[...], (n * g_ref[...]).astype(MXU_DTYPE)], axis=1)
        yn_ref[...] = yn
        h_ref[...] = x_ref[...] + _dot(yn, w_ref[...])

    return pl.pallas_call(
        body, name="out_fwd", grid=(s // tm,),
        in_specs=[_rows(tm, WIDTH_A)] + _col_specs(tm, WIDTH_B) + [_full((1, WIDTH_B)), _full((D_MODEL, D_MODEL)),
                                                                 _rows(tm, D_MODEL)],
        out_specs=[_rows(tm, D_MODEL), _rows(tm, D_MODEL)],
        out_shape=[jax.ShapeDtypeStruct((s, D_MODEL), F32), jax.ShapeDtypeStruct((s, D_MODEL), MXU_DTYPE)],
        compiler_params=_params(("arbitrary",)),
    )(ya_n, *([y_b] * nc), g_b, w_out, x)


def _ffn_fwd(h1, g_ffn, w_gate, w_up, w_down):
    s = h1.shape[0]
    tm = 256

    def body(h_ref, g_ref, wg_ref, wu_ref, wd_ref, o_ref, gate_ref, up_ref, hn_ref):
        h = h_ref[...]
        n, _ = _rms_stats(h)
        hn = (n * g_ref[...]).astype(MXU_DTYPE)
        hn_ref[...] = hn
        out = h
        for c in range(2):
            cols = slice(c * FF_HALF, (c + 1) * FF_HALF)
            gate = _dot(hn, wg_ref[:, cols])
            up = _dot(hn, wu_ref[:, cols])
            gate_ref[:, cols] = gate.astype(MXU_DTYPE)
            up_ref[:, cols] = up.astype(MXU_DTYPE)
            act = (gate * jax.nn.sigmoid(gate) * up).astype(MXU_DTYPE)
            out = out + _dot(act, wd_ref[cols, :])
        o_ref[...] = out

    return pl.pallas_call(
        body, name="ffn_fwd", grid=(s // tm,),
        in_specs=[_rows(tm, D_MODEL), _full((1, D_MODEL)), _full((D_MODEL, D_FF)), _full((D_MODEL, D_FF)),
                  _full((D_FF, D_MODEL))],
        out_specs=[_rows(tm, D_MODEL), _rows(tm, D_FF), _rows(tm, D_FF), _rows(tm, D_MODEL)],
        out_shape=[jax.ShapeDtypeStruct((s, D_MODEL), F32), jax.ShapeDtypeStruct((s, D_FF), MXU_DTYPE),
                   jax.ShapeDtypeStruct((s, D_FF), MXU_DTYPE), jax.ShapeDtypeStruct((s, D_MODEL), MXU_DTYPE)],
        compiler_params=_params(("arbitrary",)),
    )(h1, g_ffn, w_gate, w_up, w_down)


def _ple_loss(h2, p, target, g_ple, w_pg, w_pg_t, w_pp, g_final):
    s = h2.shape[0]
    tm = 256

    def body(h_ref, p_ref, t_ref, gp_ref, wg_ref, wgt_ref, wp_ref, gf_ref,
             loss_ref, dh_ref, dz_ref, dpp_ref, hn_ref, dgp_ref, dgf_ref):
        @pl.when(pl.program_id(0) == 0)
        def _():
            loss_ref[...] = jnp.zeros_like(loss_ref)
            dgp_ref[...] = jnp.zeros_like(dgp_ref)
            dgf_ref[...] = jnp.zeros_like(dgf_ref)

        h2t = h_ref[...]
        n2, r2 = _rms_stats(h2t)
        hn = (n2 * gp_ref[...]).astype(MXU_DTYPE)
        hn_ref[...] = hn
        gate = jax.nn.sigmoid(_dot(hn, wg_ref[...]))
        pp = _dot(p_ref[...].astype(MXU_DTYPE), wp_ref[...])
        h3 = h2t + gate * pp
        n3, r3 = _rms_stats(h3)
        diff = n3 * gf_ref[...] - t_ref[...]
        loss_ref[...] += jnp.full(loss_ref.shape, 0.5 * jnp.sum(diff * diff) / D_MODEL, F32)
        dy = diff * (1.0 / D_MODEL)
        dgf_ref[...] += jnp.sum(dy * n3, axis=0, keepdims=True)
        dh3 = _rms_bwd(dy * gf_ref[...], n3, r3)
        dpp_ref[...] = (dh3 * gate).astype(MXU_DTYPE)
        dz = (dh3 * pp * gate * (1.0 - gate)).astype(MXU_DTYPE)
        dz_ref[...] = dz
        dhn = _dot(dz, wgt_ref[...])
        dgp_ref[...] += jnp.sum(dhn * n2, axis=0, keepdims=True)
        dh_ref[...] = dh3 + _rms_bwd(dhn * gp_ref[...], n2, r2)

    return pl.pallas_call(
        body, name="ple_loss", grid=(s // tm,),
        in_specs=[_rows(tm, D_MODEL), _rows(tm, PLE_DIM), _rows(tm, D_MODEL), _full((1, D_MODEL)),
                  _full((D_MODEL, D_MODEL)), _full((D_MODEL, D_MODEL)), _full((PLE_DIM, D_MODEL)),
                  _full((1, D_MODEL))],
        out_specs=[_full((1, LANES)), _rows(tm, D_MODEL), _rows(tm, D_MODEL), _rows(tm, D_MODEL),
                   _rows(tm, D_MODEL), _full((1, D_MODEL)), _full((1, D_MODEL))],
        out_shape=[jax.ShapeDtypeStruct((1, LANES), F32), jax.ShapeDtypeStruct((s, D_MODEL), F32),
                   jax.ShapeDtypeStruct((s, D_MODEL), MXU_DTYPE), jax.ShapeDtypeStruct((s, D_MODEL), MXU_DTYPE),
                   jax.ShapeDtypeStruct((s, D_MODEL), MXU_DTYPE), jax.ShapeDtypeStruct((1, D_MODEL), F32),
                   jax.ShapeDtypeStruct((1, D_MODEL), F32)],
        compiler_params=_params(("arbitrary",)),
    )(h2, p, target, g_ple, w_pg, w_pg_t, w_pp, g_final)


def _ffn_bwd(dh2, h1, gate, up, g_ffn, w_down_t, w_gate_t, w_up_t):
    s = h1.shape[0]
    tm = 256

    def body(dh_ref, h_ref, gate_ref, up_ref, g_ref, wdt_ref, wgt_ref, wut_ref,
             o_ref, act_ref, dg_ref, du_ref, dgn_ref):
        @pl.when(pl.program_id(0) == 0)
        def _():
            dgn_ref[...] = jnp.zeros_like(dgn_ref)

        dh = dh_ref[...]
        dhb = dh.astype(MXU_DTYPE)
        dhn = jnp.zeros((tm, D_MODEL), F32)
        for c in range(2):
            cols = slice(c * FF_HALF, (c + 1) * FF_HALF)
            dact = _dot(dhb, wdt_ref[:, cols])
            g = gate_ref[:, cols].astype(F32)
            u = up_ref[:, cols].astype(F32)
            sg = jax.nn.sigmoid(g)
            silu = g * sg
            act_ref[:, cols] = (silu * u).astype(MXU_DTYPE)
            du = (dact * silu).astype(MXU_DTYPE)
            dg = (dact * u * sg * (1.0 + g * (1.0 - sg))).astype(MXU_DTYPE)
            du_ref[:, cols] = du
            dg_ref[:, cols] = dg
            dhn = dhn + _dot(dg, wgt_ref[cols, :]) + _dot(du, wut_ref[cols, :])
        n, r = _rms_stats(h_ref[...])
        dgn_ref[...] += jnp.sum(dhn * n, axis=0, keepdims=True)
        o_ref[...] = dh + _rms_bwd(dhn * g_ref[...], n, r)

    return pl.pallas_call(
        body, name="ffn_bwd", grid=(s // tm,),
        in_specs=[_rows(tm, D_MODEL), _rows(tm, D_MODEL), _rows(tm, D_FF), _rows(tm, D_FF), _full((1, D_MODEL)),
                  _full((D_MODEL, D_FF)), _full((D_FF, D_MODEL)), _full((D_FF, D_MODEL))],
        out_specs=[_rows(tm, D_MODEL), _rows(tm, D_FF), _rows(tm, D_FF), _rows(tm, D_FF), _full((1, D_MODEL))],
        out_shape=[jax.ShapeDtypeStruct((s, D_MODEL), F32), jax.ShapeDtypeStruct((s, D_FF), MXU_DTYPE),
                   jax.ShapeDtypeStruct((s, D_FF), MXU_DTYPE), jax.ShapeDtypeStruct((s, D_FF), MXU_DTYPE),
                   jax.ShapeDtypeStruct((1, D_MODEL), F32)],
        compiler_params=_params(("arbitrary",)),
    )(dh2, h1, gate, up, g_ffn, w_down_t, w_gate_t, w_up_t)


def _out_bwd(dh1, y_b, g_b, w_out_t):
    s = dh1.shape[0]
    tm = 512
    nc = WIDTH_B // LANES

    def body(dh_ref, *refs):
        yb_refs = refs[:nc]
        g_ref, wt_ref, dya_ref, dyb_ref, dg_ref, scr = refs[nc:]

        @pl.when(pl.program_id(0) == 0)
        def _():
            dg_ref[...] = jnp.zeros_like(dg_ref)

        dy = _dot(dh_ref[...].astype(MXU_DTYPE), wt_ref[...])
        dya_ref[...] = dy[:, :WIDTH_A]
        dyb = dy[:, WIDTH_A:]
        n, r = _rms_stats(_load_l256(yb_refs, tm))
        dg_ref[...] += jnp.sum(dyb * n, axis=0, keepdims=True)
        dyb_in = _rms_bwd(dyb * g_ref[...], n, r)
        for j in range(nc):
            cols = slice(j * LANES, (j + 1) * LANES)
            _store_l256(scr, dyb_ref, cols, dyb_in[:, cols])

    return pl.pallas_call(
        body, name="out_bwd", grid=(s // tm,), scratch_shapes=[pltpu.VMEM((tm, LANES), F32)],
        in_specs=[_rows(tm, D_MODEL)] + _col_specs(tm, WIDTH_B) + [_full((1, WIDTH_B)), _full((D_MODEL, D_MODEL))],
        out_specs=[_rows(tm, WIDTH_A), _rows(tm, WIDTH_B), _full((1, WIDTH_B))],
        out_shape=[jax.ShapeDtypeStruct((s, WIDTH_A), F32), jax.ShapeDtypeStruct((s, WIDTH_B), F32),
                   jax.ShapeDtypeStruct((1, WIDTH_B), F32)],
        compiler_params=_params(("arbitrary",)),
    )(dh1, *([y_b] * nc), g_b, w_out_t)


def _attn_bwd_branch(q, k, v, do, o, lse, grads, dil):
    s = q.shape[0]
    br = _Branch(dil, s)
    qn, nb = br.qn, br.nb
    first = grads is None

    def body(*refs):
        bias_ref, q_ref, kc_ref, kp_ref, vc_ref, vp_ref, do_ref, o_ref, lse_ref = refs[:9]
        if first:
            rest = refs[9:]
        else:
            dq_in, dk_in, dv_in = refs[9:12]
            rest = refs[12:]
        dq_ref, dk_ref, dv_ref, dk_carry, dv_carry = rest
        n = pl.program_id(1)

        @pl.when(n == 0)
        def _():
            dk_carry[...] = jnp.zeros_like(dk_carry)
            dv_carry[...] = jnp.zeros_like(dv_carry)

        @pl.when(n < nb)
        def _():
            bias2 = jnp.concatenate([bias_ref[0], bias_ref[0]], axis=0)
            lane = lax.broadcasted_iota(jnp.int32, (qn, LANES), 1)
            lo = lane < HEAD_DIM
            mask_f = lo.astype(F32)
            mask_lo = mask_f.astype(MXU_DTYPE)
            for hp in range(HEADS_B // 2):
                cols = slice(hp * LANES, (hp + 1) * LANES)
                h0, h1 = 2 * hp, 2 * hp + 1
                qp = br.load(q_ref, cols)
                kcat = jnp.concatenate([br.load(kp_ref, cols), br.load(kc_ref, cols)], axis=0)
                vcat = jnp.concatenate([br.load(vp_ref, cols), br.load(vc_ref, cols)], axis=0)
                dop = br.load(do_ref, cols)
                prod = dop * br.load(o_ref, cols)
                prod_lo = prod * mask_f
                delta = jnp.concatenate([jnp.sum(prod_lo, axis=1, keepdims=True),
                                         jnp.sum(prod - prod_lo, axis=1, keepdims=True)], axis=0)
                qs = jnp.concatenate([qp * mask_lo, qp * (1 - mask_lo)], axis=0)
                dos = jnp.concatenate([dop * mask_f, dop * (1.0 - mask_f)], axis=0).astype(MXU_DTYPE)
                lse = br.load(lse_ref, cols)
                lse2 = jnp.concatenate([lse[:, :1], lse[:, HEAD_DIM:HEAD_DIM + 1]], axis=0)
                p = jnp.exp(_dot_nt(qs, kcat) + bias2 - lse2)
                ds = (p * (_dot_nt(dos, vcat) - delta)).astype(MXU_DTYPE)
                dvc = _dot_tn(p.astype(MXU_DTYPE), dos)
                dkc = _dot_tn(ds, qs)
                dq2 = _dot(ds, kcat)
                dq = jnp.where(lo, dq2[:qn], dq2[qn:])
                dk_prev = dk_carry[:, cols] + dkc[:qn]
                dv_prev = dv_carry[:, cols] + dvc[:qn]
                if not first:
                    dq = dq + br.load(dq_in, cols)
                    dk_prev = dk_prev + br.load(dk_in, cols)
                    dv_prev = dv_prev + br.load(dv_in, cols)
                br.store(dq_ref, cols, dq)
                br.store(dk_ref, cols, dk_prev)
                br.store(dv_ref, cols, dv_prev)
                dk_carry[:, cols] = dkc[qn:]
                dv_carry[:, cols] = dvc[qn:]

        @pl.when(n == nb)
        def _():
            dk_last = dk_carry[...]
            dv_last = dv_carry[...]
            if not first:
                dk_last = dk_last + br.load(dk_in)
                dv_last = dv_last + br.load(dv_in)
            br.store(dk_ref, slice(None), dk_last)
            br.store(dv_ref, slice(None), dv_last)

    cur = lambda n: jnp.minimum(n, nb - 1)
    before = lambda n: jnp.maximum(cur(n) - 1, 0)
    late = lambda n: jnp.maximum(n - 1, 0)
    in_specs = [br.bias_spec(cur), br.spec(WIDTH_B, cur), br.spec(WIDTH_B, cur), br.spec(WIDTH_B, before),
                br.spec(WIDTH_B, cur), br.spec(WIDTH_B, before), br.spec(WIDTH_B, cur), br.spec(WIDTH_B, cur),
                br.spec(WIDTH_B, cur)]
    args = [jnp.asarray(br.bias)] + [br.view(a) for a in (q, k, k, v, v, do, o, lse)]
    if not first:
        in_specs += [br.spec(WIDTH_B, cur), br.spec(WIDTH_B, late), br.spec(WIDTH_B, late)]
        args += [br.view(g) for g in grads]
    res = pl.pallas_call(
        body, name="attn_bwd_d%d" % dil, grid=(br.grid[0], nb + 1), in_specs=in_specs,
        out_specs=[br.spec(WIDTH_B, cur), br.spec(WIDTH_B, late), br.spec(WIDTH_B, late)],
        out_shape=[jax.ShapeDtypeStruct((s // L_BLOCK, 4, 4, L_GROUP, WIDTH_B), F32)] * 3,
        scratch_shapes=[pltpu.VMEM((qn, WIDTH_B), F32), pltpu.VMEM((qn, WIDTH_B), F32)],
        compiler_params=_params(("arbitrary", "arbitrary")),
    )(*args)
    return tuple(a.reshape(s, WIDTH_B) for a in res)


def _sgu_bwd(uv, dya_n, w_tril, w_tril_t, bias, g_sgu, g_a):
    s = uv.shape[0]
    tm = 512

    def body(uv_ref, dy_ref, w_ref, wt_ref, b_ref, gs_ref, ga_ref, duv_ref, dw_ref, db_ref, dgs_ref, dga_ref,
             db_acc):
        i = pl.program_id(0)

        @pl.when(i == 0)
        def _():
            dw_ref[...] = jnp.zeros_like(dw_ref)
            dgs_ref[...] = jnp.zeros_like(dgs_ref)
            dga_ref[...] = jnp.zeros_like(dga_ref)
            db_acc[...] = jnp.zeros_like(db_acc)

        t = _sgu_forward_tile(uv_ref[...], w_ref, b_ref[...], gs_ref[...])
        na, ra = _rms_stats(t['ya'])
        dyn = dy_ref[...]
        dga_ref[...] += jnp.sum(dyn * na, axis=0, keepdims=True)
        dya = _rms_bwd(dyn * ga_ref[...], na, ra)
        dug = dya * t['mixed']
        dmixed = dya * t['ug']
        dmb = dmixed.astype(MXU_DTYPE)
        masks = _half_masks(MXU_DTYPE)
        chunks = []
        db = jnp.zeros((CHUNK, WIDTH_A), F32)
        for c in range(tm // CHUNK):
            rows = slice(c * CHUNK, (c + 1) * CHUNK)
            db = db + dmixed[rows]
            groups = []
            for gp in range(2):
                cols = slice(gp * LANES, (gp + 1) * LANES)
                dm_g = dmb[rows, cols]
                vn_g = t['vn'][rows, cols]
                dvn_g = jnp.zeros((CHUNK, LANES), F32)
                for j in range(2):
                    dm_h = dm_g * masks[j]
                    dvn_g = dvn_g + _dot(wt_ref[2 * gp + j], dm_h)
                    dw_ref[2 * gp + j] += _dot_nt(dm_h, vn_g)
                groups.append(dvn_g)
            chunks.append(jnp.concatenate(groups, axis=1))
        db_acc[...] += db
        dvn = jnp.concatenate(chunks, axis=0)
        vhat = t['vhat']
        dgs_ref[...] += jnp.sum(dvn * vhat, axis=0, keepdims=True)
        dvh = dvn * gs_ref[...]
        dvg = t['rs'] * (dvh - jnp.mean(dvh, axis=-1, keepdims=True)
                         - vhat * jnp.mean(dvh * vhat, axis=-1, keepdims=True))
        duv_ref[:, :WIDTH_A] = (dug * _gelu_grad(t['u'], t['tu'])).astype(MXU_DTYPE)
        duv_ref[:, WIDTH_A:] = (dvg * _gelu_grad(t['v'], t['tv'])).astype(MXU_DTYPE)

        @pl.when(i == pl.num_programs(0) - 1)
        def _():
            lane_a = lax.broadcasted_iota(jnp.int32, (CHUNK, WIDTH_A), 1)
            lane = lax.broadcasted_iota(jnp.int32, (CHUNK, LANES), 1)
            acc = db_acc[...]
            out = jnp.zeros((CHUNK, LANES), F32)
            for h in range(HEADS_A):
                col = jnp.sum(jnp.where(lane_a // HEAD_DIM == h, acc, 0.0), axis=1, keepdims=True)
                out = jnp.where(lane == h, col, out)
            db_ref[...] = out
            causal = (lax.broadcasted_iota(jnp.int32, (CHUNK, CHUNK), 0)
                      >= lax.broadcasted_iota(jnp.int32, (CHUNK, CHUNK), 1))
            for h in range(HEADS_A):
                dw_ref[h] = jnp.where(causal, dw_ref[h], 0.0)

    return pl.pallas_call(
        body, name="sgu_bwd", grid=(s // tm,),
        in_specs=[_rows(tm, 2 * WIDTH_A), _rows(tm, WIDTH_A), _full((HEADS_A, CHUNK, CHUNK)),
                  _full((HEADS_A, CHUNK, CHUNK)), _full((CHUNK, WIDTH_A)), _full((1, WIDTH_A)),
                  _full((1, WIDTH_A))],
        out_specs=[_rows(tm, 2 * WIDTH_A), _full((HEADS_A, CHUNK, CHUNK)), _full((CHUNK, LANES)),
                   _full((1, WIDTH_A)), _full((1, WIDTH_A))],
        out_shape=[jax.ShapeDtypeStruct((s, 2 * WIDTH_A), MXU_DTYPE),
                   jax.ShapeDtypeStruct((HEADS_A, CHUNK, CHUNK), F32), jax.ShapeDtypeStruct((CHUNK, LANES), F32),
                   jax.ShapeDtypeStruct((1, WIDTH_A), F32), jax.ShapeDtypeStruct((1, WIDTH_A), F32)],
        scratch_shapes=[pltpu.VMEM((CHUNK, WIDTH_A), F32)],
        compiler_params=_params(("arbitrary",)),
    )(uv, dya_n, w_tril, w_tril_t, bias, g_sgu, g_a)


def _in_bwd_proj(duv, dq, dk, dv, cos_t, sin_t):
    s = duv.shape[0]
    tm = 512
    nc = WIDTH_B // LANES

    def body(duv_ref, *refs):
        dq_refs, dk_refs, dv_refs = refs[:nc], refs[nc:2 * nc], refs[2 * nc:3 * nc]
        cos_ref, sin_ref, dp_ref = refs[3 * nc:]
        cos = cos_ref[...]
        sin = sin_ref[...]
        dp_ref[:, :2 * WIDTH_A] = duv_ref[...]
        for i in range(nc):
            lo = 2 * WIDTH_A + i * LANES
            tq = _load_l256(dq_refs[i:i + 1], tm) * (HEAD_DIM ** -0.5)
            tk = _load_l256(dk_refs[i:i + 1], tm)
            dp_ref[:, lo:lo + LANES] = (tq * cos + _rope_partner(tq * sin)).astype(MXU_DTYPE)
            dp_ref[:, lo + WIDTH_B:lo + WIDTH_B + LANES] = (tk * cos + _rope_partner(tk * sin)).astype(MXU_DTYPE)
            dp_ref[:, lo + 2 * WIDTH_B:lo + 2 * WIDTH_B + LANES] = _load_l256(dv_refs[i:i + 1], tm).astype(MXU_DTYPE)

    return pl.pallas_call(
        body, name="in_bwd_proj", grid=(s // tm,),
        in_specs=[_rows(tm, 2 * WIDTH_A)] + 3 * _col_specs(tm, WIDTH_B) + [_rows(tm, LANES), _rows(tm, LANES)],
        out_specs=_rows(tm, IN_COLS), out_shape=jax.ShapeDtypeStruct((s, IN_COLS), MXU_DTYPE),
        compiler_params=_params(("arbitrary",)),
    )(duv, *([dq] * nc), *([dk] * nc), *([dv] * nc), cos_t, sin_t)


def _in_bwd_x(dproj, w_in_t, x, g_mix, dh1, after):
    s = x.shape[0]
    tm = 512

    def body(dp_ref, wt_ref, x_ref, g_ref, dh_ref, *rest):
        gx_ref, dg_ref = rest[len(after):]
        @pl.when(pl.program_id(0) == 0)
        def _():
            dg_ref[...] = jnp.zeros_like(dg_ref)

        dhn = _dot(dp_ref[...], wt_ref[...])
        n, r = _rms_stats(x_ref[...])
        dg_ref[...] += jnp.sum(dhn * n, axis=0, keepdims=True)
        gx_ref[...] = dh_ref[...] + _rms_bwd(dhn * g_ref[...], n, r)

    return pl.pallas_call(
        body, name="in_bwd_x", grid=(s // tm,),
        in_specs=[_rows(tm, IN_COLS), _full((IN_COLS, D_MODEL)), _rows(tm, D_MODEL), _full((1, D_MODEL)),
                  _rows(tm, D_MODEL)] + [pl.BlockSpec(memory_space=pl.ANY)] * len(after),
        out_specs=[_rows(tm, D_MODEL), _full((1, D_MODEL))],
        out_shape=[jax.ShapeDtypeStruct((s, D_MODEL), F32), jax.ShapeDtypeStruct((1, D_MODEL), F32)],
        compiler_params=_params(("arbitrary",)),
    )(dproj, w_in_t, x, g_mix, dh1, *after)


def _wgrad(a, b, name):
    s, m = a.shape
    n = b.shape[1]
    bm = 512 if m % 512 == 0 else FF_HALF
    ts = 1024
    nsteps = s // ts

    def body(a_ref, b_ref, o_ref, acc):
        kk = pl.program_id(1)

        @pl.when(kk == 0)
        def _():
            acc[...] = jnp.zeros_like(acc)

        acc[...] += _dot_tn(a_ref[...].astype(MXU_DTYPE), b_ref[...].astype(MXU_DTYPE))

        @pl.when(kk == nsteps - 1)
        def _():
            o_ref[...] = acc[...].astype(o_ref.dtype)

    return pl.pallas_call(
        body, name=name, grid=(m // bm, nsteps),
        in_specs=[pl.BlockSpec((ts, bm), lambda i, kk: (kk, i)), pl.BlockSpec((ts, n), lambda i, kk: (kk, 0))],
        out_specs=pl.BlockSpec((bm, n), lambda i, kk: (i, 0)), out_shape=jax.ShapeDtypeStruct((m, n), jnp.bfloat16),
        scratch_shapes=[pltpu.VMEM((bm, n), F32)],
        compiler_params=_params(("arbitrary", "arbitrary")),
    )(a, b)


def _rope_tables(s):
    half = HEAD_DIM // 2
    inv = ROPE_THETA ** (-jnp.arange(half, dtype=F32) / half)
    ang = jnp.arange(s, dtype=F32)[:, None] * inv[None, :]
    cos = jnp.cos(ang)
    sin = jnp.sin(ang)
    cos_t = jnp.concatenate([cos, cos, cos, cos], axis=1)
    sin_t = jnp.concatenate([-sin, sin, -sin, sin], axis=1)
    return cos_t, sin_t


MESH = pl.DeviceIdType.MESH
ANY = pl.BlockSpec(memory_space=pl.ANY)
SEM = pl.BlockSpec(memory_space=pltpu.SEMAPHORE)
SPLIT_COPY = pltpu.CompilerParams(has_side_effects=pltpu.SideEffectType.DATAFLOW_SIDE_EFFECTING)
SLAB_IS_TRANSPOSED = {'w_in': True, 'w_out': False, 'w_gate': True, 'w_up': True, 'w_down': False,
                      'w_ple_gate': False, 'w_ple_proj': True}


def _place():
    x, y, c = lax.axis_index("x"), lax.axis_index("y"), lax.axis_index("c")
    other_chips = [(1 - x, y), (x, 1 - y), (1 - x, 1 - y)]
    return x, y, c, other_chips


def _chip_of(chip):
    return 2 * chip[0] + chip[1]


def _half(ref, lead, hc):
    hr = ref.shape[1] // 2
    return ref.at[lead, pl.ds(hc * hr, hr), :]


def _put_own(stack, own, index):
    return lax.dynamic_update_slice(stack, own[None], (index,) + (0,) * own.ndim)


def _all_gather_now(slab):
    rows, cols = slab.shape

    def body(x_ref, out_ref, send_sems, recv_sems):
        x, y, c, chips = _place()
        sibling = (x, y, 1 - c)
        hr = rows // 2

        def copy(k, src, dst, to):
            return pltpu.make_async_remote_copy(src_ref=src, dst_ref=dst, send_sem=send_sems.at[k],
                                                recv_sem=recv_sems.at[k], device_id=to, device_id_type=MESH)

        my_half = x_ref.at[pl.ds(c * hr, hr), :]
        first = [copy(j, my_half, _half(out_ref, 2 * x + y, c), (*chip, c)) for j, chip in enumerate(chips)]
        for cp in first:
            cp.start()
        passed = [copy(3 + j, _half(out_ref, _chip_of(chip), c), _half(out_ref, _chip_of(chip), c), sibling)
                  for j, chip in enumerate(chips)]
        for j, chip in enumerate(chips):
            copy(j, my_half, _half(out_ref, _chip_of(chip), c), (*chip, c)).wait_recv()
            passed[j].start()
        for j, chip in enumerate(chips):
            copy(3 + j, my_half, _half(out_ref, _chip_of(chip), 1 - c), sibling).wait_recv()
        for cp in first + passed:
            cp.wait_send()

    gathered = pl.pallas_call(
        body, name="all_gather_now", out_shape=jax.ShapeDtypeStruct((N_CHIPS, rows, cols), slab.dtype),
        in_specs=[ANY], out_specs=ANY,
        scratch_shapes=[pltpu.SemaphoreType.DMA((6,)), pltpu.SemaphoreType.DMA((6,))],
    )(slab)
    me = 2 * lax.axis_index("x") + lax.axis_index("y")
    return _put_own(gathered, slab, me).reshape(N_CHIPS * rows, cols)


def _gather_copies(slab_refs, land_refs, send_sems, recv_sems):
    x, y, c, chips = _place()
    sends, recvs = [], []
    for k, (src, land) in enumerate(zip(slab_refs, land_refs)):
        hr = src.shape[0] // 2
        for j, chip in enumerate(chips):
            for t in range(2):
                sends.append(pltpu.make_async_remote_copy(
                    src_ref=src.at[pl.ds(c * hr, hr), :], dst_ref=_half(land, 2 * x + y, c),
                    send_sem=send_sems.at[6 * k + 2 * j + t], recv_sem=recv_sems.at[6 * k + 2 * j + c],
                    device_id=(*chip, t), device_id_type=MESH))
                recvs.append(pltpu.make_async_remote_copy(
                    src_ref=src.at[pl.ds(t * hr, hr), :], dst_ref=_half(land, _chip_of(chip), t),
                    send_sem=send_sems.at[6 * k + 2 * j + t], recv_sem=recv_sems.at[6 * k + 2 * j + t],
                    device_id=(*chip, t), device_id_type=MESH))
    return sends, recvs


def _all_gather_start(slabs, after):
    n = len(slabs)

    def body(*refs):
        slab_refs, land_refs = refs[:n], refs[n:2 * n]
        send_sems, recv_sems = refs[2 * n + 1:2 * n + 3]
        token = refs[-1]
        sends, _ = _gather_copies(slab_refs, land_refs, send_sems, recv_sems)
        for cp in sends:
            cp.start()
        token[...] = jnp.zeros_like(token)

    lands = [lax.empty((N_CHIPS,) + s.shape, s.dtype) for s in slabs]
    hbm = lambda a: pltpu.HBM(a.shape, a.dtype)
    res = pl.pallas_call(
        body, name="all_gather_start",
        out_shape=(pltpu.SemaphoreType.DMA((6 * n,)), pltpu.SemaphoreType.DMA((6 * n,)), *map(hbm, slabs),
                   *map(hbm, lands), jax.ShapeDtypeStruct((8, LANES), F32)),
        in_specs=[ANY] * (2 * n + 1),
        out_specs=(SEM, SEM, *([ANY] * (2 * n)), pl.BlockSpec(memory_space=pltpu.VMEM)),
        input_output_aliases={i: 2 + i for i in range(2 * n)}, compiler_params=SPLIT_COPY,
    )(*[pltpu.with_memory_space_constraint(a, pltpu.HBM) for a in list(slabs) + lands], after)
    return res[:-1], res[-1]


def _all_gather_wait(handle, after):
    send_sems, recv_sems = handle[:2]
    n = (len(handle) - 2) // 2
    slabs, lands = handle[2:2 + n], handle[2 + n:]

    def body(*refs):
        slab_refs, land_refs = refs[:n], refs[n:2 * n]
        send_sems, recv_sems = refs[2 * n:2 * n + 2]
        sends, recvs = _gather_copies(slab_refs, land_refs, send_sems, recv_sems)
        for cp in sends:
            cp.wait_send()
        for cp in recvs:
            cp.wait_recv()

    hbm = lambda a: pltpu.HBM(a.shape, a.dtype)
    res = pl.pallas_call(
        body, name="all_gather_wait", out_shape=tuple(map(hbm, list(slabs) + list(lands))),
        in_specs=[ANY] * (2 * n) + [SEM, SEM, ANY], out_specs=tuple([ANY] * (2 * n)),
        input_output_aliases={i: i for i in range(2 * n)}, compiler_params=SPLIT_COPY,
    )(*slabs, *lands, send_sems, recv_sems, after)
    me = 2 * lax.axis_index("x") + lax.axis_index("y")
    return [_put_own(land, slab, me).reshape(N_CHIPS * slab.shape[0], slab.shape[1])
            for slab, land in zip(res[:n], res[n:])]


def _scatter_copies(part_refs, land_refs, send_sems, recv_sems):
    x, y, c, chips = _place()
    me = 4 * x + 2 * y + c
    sends, recvs = [], []
    for k, (part, land) in enumerate(zip(part_refs, land_refs)):
        for j, chip in enumerate(chips):
            for t in range(2):
                sends.append(pltpu.make_async_remote_copy(
                    src_ref=part.at[_chip_of(chip)], dst_ref=land.at[me],
                    send_sem=send_sems.at[7 * k + 2 * j + t], recv_sem=recv_sems.at[7 * k + 2 * j + c],
                    device_id=(*chip, t), device_id_type=MESH))
                recvs.append(pltpu.make_async_remote_copy(
                    src_ref=part.at[_chip_of(chip)], dst_ref=land.at[2 * _chip_of(chip) + t],
                    send_sem=send_sems.at[7 * k + 2 * j + t], recv_sem=recv_sems.at[7 * k + 2 * j + t],
                    device_id=(*chip, t), device_id_type=MESH))
        sends.append(pltpu.make_async_remote_copy(
            src_ref=part.at[2 * x + y], dst_ref=land.at[me], send_sem=send_sems.at[7 * k + 6],
            recv_sem=recv_sems.at[7 * k + 6], device_id=(x, y, 1 - c), device_id_type=MESH))
        recvs.append(pltpu.make_async_remote_copy(
            src_ref=part.at[2 * x + y], dst_ref=land.at[4 * x + 2 * y + 1 - c],
            send_sem=send_sems.at[7 * k + 6], recv_sem=recv_sems.at[7 * k + 6], device_id=(x, y, 1 - c),
            device_id_type=MESH))
    return sends, recvs


def _reduce_scatter_start(parts, name):
    n = len(parts)
    parts = [p.reshape(N_CHIPS, p.shape[0] // N_CHIPS, p.shape[1]) for p in parts]

    def body(*refs):
        part_refs, land_refs = refs[:n], refs[n:2 * n]
        send_sems, recv_sems = refs[2 * n:2 * n + 2]
        token = refs[-1]
        sends, _ = _scatter_copies(part_refs, land_refs, send_sems, recv_sems)
        for cp in sends:
            cp.start()
        token[...] = jnp.zeros_like(token)

    lands = [lax.empty((N_DEV, p.shape[1], p.shape[2]), p.dtype) for p in parts]
    hbm = lambda a: pltpu.HBM(a.shape, a.dtype)
    res = pl.pallas_call(
        body, name=name,
        out_shape=(pltpu.SemaphoreType.DMA((7 * n,)), pltpu.SemaphoreType.DMA((7 * n,)), *map(hbm, parts),
                   *map(hbm, lands), jax.ShapeDtypeStruct((8, LANES), F32)),
        in_specs=[ANY] * (2 * n), out_specs=(SEM, SEM, *([ANY] * (2 * n)), pl.BlockSpec(memory_space=pltpu.VMEM)),
        input_output_aliases={i: 2 + i for i in range(2 * n)}, compiler_params=SPLIT_COPY,
    )(*[pltpu.with_memory_space_constraint(a, pltpu.HBM) for a in parts + lands])
    return res[:-1], res[-1]


def _reduce_scatter_wait(handle, after, name):
    send_sems, recv_sems = handle[:2]
    n = (len(handle) - 2) // 2
    parts, lands = handle[2:2 + n], handle[2 + n:]

    def body(*refs):
        part_refs, land_refs = refs[:n], refs[n:2 * n]
        send_sems, recv_sems = refs[2 * n:2 * n + 2]
        sends, recvs = _scatter_copies(part_refs, land_refs, send_sems, recv_sems)
        for cp in sends:
            cp.wait_send()
        for cp in recvs:
            cp.wait_recv()

    hbm = lambda a: pltpu.HBM(a.shape, a.dtype)
    res = pl.pallas_call(
        body, name=name, out_shape=tuple(map(hbm, list(parts) + list(lands))),
        in_specs=[ANY] * (2 * n) + [SEM, SEM, ANY], out_specs=tuple([ANY] * (2 * n)),
        input_output_aliases={i: i for i in range(2 * n)}, compiler_params=SPLIT_COPY,
    )(*parts, *lands, send_sems, recv_sems, after)
    return list(zip(res[:n], res[n:]))


def _small_all_reduce(block):
    rows = block.shape[0]

    def body(x_ref, all_ref, sum_ref, send_sems, recv_sems, local_sem):
        x, y, c, chips = _place()
        me, sibling = (x, y, c), (x, y, 1 - c)

        def blk(px, py, pc):
            return all_ref.at[pl.ds((4 * px + 2 * py + pc) * rows, rows), :]

        def copy(k, who, to, src=None):
            return pltpu.make_async_remote_copy(
                src_ref=blk(*who) if src is None else src, dst_ref=blk(*who), send_sem=send_sems.at[k],
                recv_sem=recv_sems.at[k], device_id=to, device_id_type=MESH)

        mine = pltpu.make_async_copy(x_ref, blk(*me), local_sem)
        mine.start()
        first = [copy(0, me, sibling, src=x_ref)]
        first += [copy(1 + j, me, (*chip, c), src=x_ref) for j, chip in enumerate(chips)]
        for cp in first:
            cp.start()
        passed = [copy(4 + j, (*chip, c), sibling) for j, chip in enumerate(chips)]
        for j, chip in enumerate(chips):
            copy(1 + j, (*chip, c), me).wait_recv()
            passed[j].start()
        copy(0, sibling, me).wait_recv()
        for j, chip in enumerate(chips):
            copy(4 + j, (*chip, 1 - c), me).wait_recv()
        for cp in first + passed:
            cp.wait_send()
        mine.wait()
        acc = all_ref[pl.ds(0, rows), :]
        for dev in range(1, N_DEV):
            acc = acc + all_ref[pl.ds(dev * rows, rows), :]
        sum_ref[...] = acc

    vmem = pl.BlockSpec(memory_space=pltpu.VMEM)
    return pl.pallas_call(
        body, name="small_all_reduce",
        out_shape=[jax.ShapeDtypeStruct((N_DEV * rows, D_MODEL), F32), jax.ShapeDtypeStruct((rows, D_MODEL), F32)],
        in_specs=[vmem], out_specs=[vmem, vmem],
        scratch_shapes=[pltpu.SemaphoreType.DMA((7,)), pltpu.SemaphoreType.DMA((7,)), pltpu.SemaphoreType.DMA],
    )(block)[1]


def _adamw(w, g, m, v, name):
    rows, cols = w.shape
    tm = rows
    if rows > 512:
        tm = next(t for t in range(512, 7, -8) if rows % t == 0)

    def body(w_ref, g_ref, m_ref, v_ref, d_ref, nm_ref, nv_ref):
        g_ = g_ref[...]
        m_ = ADAM_B1 * m_ref[...] + (1.0 - ADAM_B1) * g_
        v_ = ADAM_B2 * v_ref[...] + (1.0 - ADAM_B2) * (g_ * g_)
        m_hat = m_ / (1.0 - ADAM_B1 ** ADAM_STEP)
        v_hat = v_ / (1.0 - ADAM_B2 ** ADAM_STEP)
        d_ref[...] = -ADAM_LR * (m_hat / (jnp.sqrt(v_hat) + ADAM_EPS) + ADAM_WD * w_ref[...])
        nm_ref[...] = m_
        nv_ref[...] = v_

    spec = pl.BlockSpec((tm, cols), lambda i: (i, 0))
    return pl.pallas_call(
        body, name=name, grid=(rows // tm,), in_specs=[spec] * 4, out_specs=[spec] * 3,
        out_shape=[jax.ShapeDtypeStruct(w.shape, F32)] * 3, compiler_params=_params(("arbitrary",)),
    )(w, g, m, v)


def _adamw_of_shares(w, own, land, m, v, name):
    rows, cols = w.shape
    tm = rows // 4
    x, y, c = lax.axis_index("x"), lax.axis_index("y"), lax.axis_index("c")
    where = jnp.stack([2 * x + y, 4 * x + 2 * y + c]).astype(jnp.int32)

    def body(where_ref, w_ref, own_ref, land_ref, m_ref, v_ref, g_ref, d_ref, nm_ref, nv_ref):
        me = where_ref[1]
        g_ = jnp.zeros((tm, cols), F32)
        for dev in range(N_DEV):
            g_ = g_ + jnp.where(me == dev, own_ref[0], land_ref[dev]).astype(F32)
        m_ = ADAM_B1 * m_ref[...] + (1.0 - ADAM_B1) * g_
        v_ = ADAM_B2 * v_ref[...] + (1.0 - ADAM_B2) * (g_ * g_)
        m_hat = m_ / (1.0 - ADAM_B1 ** ADAM_STEP)
        v_hat = v_ / (1.0 - ADAM_B2 ** ADAM_STEP)
        g_ref[...] = g_
        d_ref[...] = -ADAM_LR * (m_hat / (jnp.sqrt(v_hat) + ADAM_EPS) + ADAM_WD * w_ref[...])
        nm_ref[...] = m_
        nv_ref[...] = v_

    tile = pl.BlockSpec((tm, cols), lambda i, where_ref: (i, 0))
    spec = pltpu.PrefetchScalarGridSpec(
        num_scalar_prefetch=1, grid=(rows // tm,),
        in_specs=[tile, pl.BlockSpec((1, tm, cols), lambda i, where_ref: (where_ref[0], i, 0)),
                  pl.BlockSpec((N_DEV, tm, cols), lambda i, where_ref: (0, i, 0)), tile, tile],
        out_specs=[tile] * 4)
    return pl.pallas_call(
        body, name=name, grid_spec=spec, out_shape=[jax.ShapeDtypeStruct(w.shape, F32)] * 4,
        compiler_params=_params(("arbitrary",)),
    )(where, w, own, land, m, v)


def _pack_small(values):
    flat = jnp.concatenate([values[n].reshape(-1).astype(F32) for n in SMALL])
    return jnp.pad(flat, (0, SMALL_ROWS * D_MODEL - flat.shape[0])).reshape(SMALL_ROWS, D_MODEL)


def _unpack_small(block, shapes):
    flat = block.reshape(-1)
    out, lo = {}, 0
    for n in SMALL:
        out[n] = flat[lo:lo + SMALL_SIZES[n]].reshape(shapes[n])
        lo += SMALL_SIZES[n]
    return out


def _after(token, a):
    return a + token[:1, :1].astype(a.dtype)


def kernel(x, p, mix_norm_g, w_in, sgu_w, sgu_b, sgu_norm_g, out_norm_a, out_norm_b, w_out, ffn_norm_g, w_gate, w_up, w_down, ple_norm_g, w_ple_gate, w_ple_proj, final_norm_g, loss_target, m_mix_norm_g, m_w_in, m_sgu_w, m_sgu_b, m_sgu_norm_g, m_out_norm_a, m_out_norm_b, m_w_out, m_ffn_norm_g, m_w_gate, m_w_up, m_w_down, m_ple_norm_g, m_w_ple_gate, m_w_ple_proj, m_final_norm_g, v_mix_norm_g, v_w_in, v_sgu_w, v_sgu_b, v_sgu_norm_g, v_out_norm_a, v_out_norm_b, v_w_out, v_ffn_norm_g, v_w_gate, v_w_up, v_w_down, v_ple_norm_g, v_w_ple_gate, v_w_ple_proj, v_final_norm_g):
    given = dict(locals())
    drop_lead = lambda a, lead: a.reshape(a.shape[lead:])
    xs, ps, target = drop_lead(x, 1), drop_lead(p, 2), drop_lead(loss_target, 1)
    s = xs.shape[0]
    shard = lambda name: drop_lead(given[name], 1)

    def slab_of(name):
        local = shard(name).astype(MXU_DTYPE)
        return local.T if SLAB_IS_TRANSPOSED[name] else local

    w_in_t = _all_gather_now(slab_of('w_in'))
    later = ['w_out', 'w_gate', 'w_up', 'w_down', 'w_ple_gate', 'w_ple_proj']
    gather, token = _all_gather_start([slab_of(n) for n in later], w_in_t)

    cos_t, sin_t = _rope_tables(s)
    tril = jnp.tril(jnp.ones((CHUNK, CHUNK), F32))
    w_tril = (sgu_w.reshape(HEADS_A, CHUNK, CHUNK) * tril).astype(MXU_DTYPE)
    w_tril_t = jnp.swapaxes(w_tril, 1, 2)
    bias = jnp.repeat(sgu_b.reshape(HEADS_A, CHUNK).T, HEAD_DIM, axis=1)
    g = {n: given[n].reshape(1, -1) for n in SMALL if n not in ('sgu_w', 'sgu_b')}

    uv, q, k, v, hn1 = _in_fwd(xs, _after(token, g['mix_norm_g']), w_in_t.T, cos_t, sin_t)
    ya_n = _sgu_fwd(uv, w_tril, bias, g['sgu_norm_g'], g['out_norm_a'])
    branches = [_attn_fwd_branch(q, k, v, dil) for dil in DILATIONS]
    y_b, lse = _attn_merge([o for o, _ in branches], [l for _, l in branches])
    stacks = dict(zip(later, _all_gather_wait(gather, lse)))
    w_gate_t, w_up_t, w_pp_t = stacks['w_gate'], stacks['w_up'], stacks['w_ple_proj']
    h1, y_n = _out_fwd(ya_n, y_b, g['out_norm_b'], stacks['w_out'], xs)
    h2, gate, up, hn2 = _ffn_fwd(h1, g['ffn_norm_g'], w_gate_t.T, w_up_t.T, stacks['w_down'])
    loss, dh2, dz, dpp, hn3, d_ple_g, d_final_g = _ple_loss(
        h2, ps, target, g['ple_norm_g'], stacks['w_ple_gate'], stacks['w_ple_gate'].T, w_pp_t.T, g['final_norm_g'])

    share = {}
    share['w_ple_gate'] = _wgrad(hn3, dz, "wgrad_ple_gate")
    share['w_ple_proj'] = _wgrad(dpp, ps, "wgrad_ple_proj")
    scatter_1, token = _reduce_scatter_start([share['w_ple_gate'], share['w_ple_proj']], "reduce_scatter_start_1")
    dh1, act, dgate, dup, d_ffn_g = _ffn_bwd(dh2, h1, gate, up, _after(token, g['ffn_norm_g']), stacks['w_down'].T,
                                             w_gate_t, w_up_t)
    share['w_down'] = _wgrad(act, dh2, "wgrad_down")
    share['w_gate'] = _wgrad(dgate, hn2, "wgrad_gate")
    share['w_up'] = _wgrad(dup, hn2, "wgrad_up")
    scatter_2, token = _reduce_scatter_start([share['w_down'], share['w_gate'], share['w_up']],
                                             "reduce_scatter_start_2")
    dya_n, dyb, d_out_b = _out_bwd(dh1, y_b, _after(token, g['out_norm_b']), stacks['w_out'].T)
    share['w_out'] = _wgrad(y_n, dh1, "wgrad_out")
    scatter_3, token = _reduce_scatter_start([share['w_out']], "reduce_scatter_start_3")
    grads = None
    for dil in DILATIONS:
        grads = _attn_bwd_branch(q, k, v, dyb, y_b, lse, grads, dil)
    duv, d_sgu_w, d_sgu_b, d_sgu_g, d_out_a = _sgu_bwd(uv, dya_n, w_tril, w_tril_t, bias,
                                                       _after(token, g['sgu_norm_g']), g['out_norm_a'])
    dproj = _in_bwd_proj(duv, grads[0], grads[1], grads[2], cos_t, sin_t)
    share['w_in'] = _wgrad(dproj, hn1, "wgrad_in")
    scatter_4, token = _reduce_scatter_start([share['w_in']], "reduce_scatter_start_4")

    grads, deltas, new_m, new_v = {}, {}, {}, {}
    add_lead = lambda a: a.reshape((1,) + a.shape)

    def finish(names, handles, after, tag):
        landed = []
        for i, handle in enumerate(handles):
            landed += _reduce_scatter_wait(handle, after, "reduce_scatter_wait_%s%d" % (tag, i))
        for n, (own, land) in zip(names, landed):
            turn = (lambda a: a.T) if SLAB_IS_TRANSPOSED[n] else (lambda a: a)
            res = _adamw_of_shares(turn(shard(n)), own, land, turn(shard("m_" + n)), turn(shard("v_" + n)),
                                   "adamw_" + n)
            grads[n], deltas[n], new_m[n], new_v[n] = (add_lead(turn(a)) for a in res)

    finish(['w_ple_gate', 'w_ple_proj', 'w_down', 'w_gate', 'w_up', 'w_out'], [scatter_1, scatter_2, scatter_3], token,
           "early")
    grad_x, d_mix_g = _in_bwd_x(dproj, w_in_t, xs, g['mix_norm_g'], dh1,
                                after=[new_v[n] for n in ('w_down', 'w_gate', 'w_up', 'w_out')])

    gs = {'mix_norm_g': d_mix_g, 'sgu_w': d_sgu_w, 'sgu_b': d_sgu_b[:, :HEADS_A].T, 'sgu_norm_g': d_sgu_g,
          'out_norm_a': d_out_a, 'out_norm_b': d_out_b, 'ffn_norm_g': d_ffn_g, 'ple_norm_g': d_ple_g,
          'final_norm_g': d_final_g}
    gs_block = _pack_small(gs).at[SMALL_ROWS - 1, 0].set(loss[0, 0])
    small_sum = _small_all_reduce(gs_block)
    loss_out = small_sum[SMALL_ROWS - 1, 0]
    small_shapes = {n: given[n].shape for n in SMALL}
    finish(['w_in'], [scatter_4], small_sum, "last")

    small = {n: given[n] for n in SMALL}
    d, nm, nv = _adamw(_pack_small(small), small_sum, _pack_small({n: given["m_" + n] for n in SMALL}),
                       _pack_small({n: given["v_" + n] for n in SMALL}), "adamw_small")
    for res, blk in ((grads, small_sum), (deltas, d), (new_m, nm), (new_v, nv)):
        res.update(_unpack_small(blk, small_shapes))

    outs = [loss_out, add_lead(grad_x)]
    for res in (grads, deltas, new_m, new_v):
        outs += [res[n] for n in WEIGHT_NAMES]
    return tuple(outs)
```

```python
import functools
import math

import jax
import jax.numpy as jnp
import numpy as np
from jax import lax
from jax.experimental import pallas as pl
from jax.experimental.pallas import tpu as pltpu

F32 = jnp.float32
MXU_DTYPE = jnp.bfloat16

D_MODEL = 1024
HEAD_DIM = 64
HEADS_A = 4
HEADS_B = 12
WIDTH_A = HEADS_A * HEAD_DIM
WIDTH_B = HEADS_B * HEAD_DIM
CHUNK = 128
BLOCK = 128
DILATIONS = (1, 4, 16)
ROPE_THETA = 10000.0
D_FF = 2816
FF_HALF = D_FF // 2
FF_STRIP = FF_HALF
PLE_DIM = 256
IN_COLS = 2 * WIDTH_A + 3 * WIDTH_B
EPS = 1e-6
LANES = 128
N_CHIPS = 4
N_DEV = 8

ADAM_LR = 0.001
ADAM_B1 = 0.9
ADAM_B2 = 0.999
ADAM_EPS = 1e-08
ADAM_WD = 0.01
ADAM_STEP = 10

VMEM_LIMIT = 56 * 1024 * 1024

WEIGHT_NAMES = ['mix_norm_g', 'w_in', 'sgu_w', 'sgu_b', 'sgu_norm_g', 'out_norm_a', 'out_norm_b', 'w_out',
                'ffn_norm_g', 'w_gate', 'w_up', 'w_down', 'ple_norm_g', 'w_ple_gate', 'w_ple_proj', 'final_norm_g']
SHARDED = ['w_in', 'w_out', 'w_gate', 'w_up', 'w_down', 'w_ple_gate', 'w_ple_proj']
SMALL = ['mix_norm_g', 'sgu_w', 'sgu_b', 'sgu_norm_g', 'out_norm_a', 'out_norm_b', 'ffn_norm_g', 'ple_norm_g',
         'final_norm_g']
SMALL_SIZES = {'mix_norm_g': 1024, 'sgu_w': 65536, 'sgu_b': 512, 'sgu_norm_g': 256, 'out_norm_a': 256,
               'out_norm_b': 768, 'ffn_norm_g': 1024, 'ple_norm_g': 1024, 'final_norm_g': 1024}
SMALL_ROWS = 72


def _params(semantics=None):
    return pltpu.CompilerParams(dimension_semantics=semantics, vmem_limit_bytes=VMEM_LIMIT)


def _full(shape):
    nd = len(shape)
    return pl.BlockSpec(shape, lambda i: (0,) * nd, pipeline_mode=pl.Buffered(1))


def _rows(tm, width):
    return pl.BlockSpec((tm, width), lambda i: (i, 0))


def _rms_stats(x):
    r = lax.rsqrt(jnp.mean(x * x, axis=-1, keepdims=True) + EPS)
    return x * r, r


def _rms_bwd(dn, n, r):
    return r * (dn - n * jnp.mean(dn * n, axis=-1, keepdims=True))


def _dot(a, b):
    return jnp.dot(a, b, preferred_element_type=F32)


def _dot_nt(a, b):
    return lax.dot_general(a, b, (((1,), (1,)), ((), ())), preferred_element_type=F32)


def _dot_tn(a, b):
    return lax.dot_general(a, b, (((0,), (0,)), ((), ())), preferred_element_type=F32)


def _gelu_parts(x):
    c = math.sqrt(2.0 / math.pi)
    t = jnp.tanh(c * (x + 0.044715 * x * x * x))
    return 0.5 * x * (1.0 + t), t


def _gelu_grad(x, t):
    c = math.sqrt(2.0 / math.pi)
    return 0.5 * (1.0 + t) + 0.5 * x * (1.0 - t * t) * c * (1.0 + 3.0 * 0.044715 * x * x)


def _half_masks(dtype):
    lane = lax.broadcasted_iota(jnp.int32, (BLOCK, LANES), 1)
    lo = (lane < HEAD_DIM).astype(F32)
    return lo.astype(dtype), (1.0 - lo).astype(dtype)


def _rope_partner(t):
    lane = lax.broadcasted_iota(jnp.int32, t.shape, 1)
    first_half = (lane % HEAD_DIM) < (HEAD_DIM // 2)
    return jnp.where(first_half, pltpu.roll(t, LANES - HEAD_DIM // 2, 1), pltpu.roll(t, HEAD_DIM // 2, 1))


L_BLOCK = 256
L_GROUP = 16


def _store_l256(scr, out_ref, cols, value):
    tm = value.shape[0]
    scr[...] = value
    for blk in range(tm // L_BLOCK):
        for r in range(L_GROUP):
            lo = blk * L_BLOCK + r * L_GROUP
            piece = scr[pl.ds(blk * L_BLOCK + r, L_GROUP, stride=L_GROUP), :]
            out_ref[lo:lo + L_GROUP, cols] = piece.astype(out_ref.dtype)


def _load_l256(col_refs, tm):
    cols = []
    for ref in col_refs:
        pieces = [ref[pl.ds(blk * L_BLOCK + i, L_GROUP, stride=L_GROUP), :]
                  for blk in range(tm // L_BLOCK) for i in range(L_GROUP)]
        cols.append(jnp.concatenate(pieces, axis=0))
    return jnp.concatenate(cols, axis=1)


def _col_specs(tm, width):
    return [pl.BlockSpec((tm, LANES), lambda i, j=j: (i, j)) for j in range(width // LANES)]


def _in_fwd(x, g_mix, w_in, cos_t, sin_t):
    s = x.shape[0]
    tm = 512

    def body(x_ref, g_ref, w_ref, cos_ref, sin_ref, uv_ref, q_ref, k_ref, v_ref, hn_ref, *scrs):
        n, _ = _rms_stats(x_ref[...])
        hn = (n * g_ref[...]).astype(MXU_DTYPE)
        hn_ref[...] = hn
        cos = cos_ref[...]
        sin = sin_ref[...]
        strip = 2 * LANES
        for j in range(IN_COLS // strip):
            proj = _dot(hn, w_ref[:, j * strip:(j + 1) * strip])
            lo = j * strip - 2 * WIDTH_A
            if lo < 0:
                uv_ref[:, j * strip:(j + 1) * strip] = proj
                continue
            which, lo = divmod(lo, WIDTH_B)
            for i in range(strip // LANES):
                t = proj[:, i * LANES:(i + 1) * LANES]
                cols = slice(lo + i * LANES, lo + (i + 1) * LANES)
                scr = scrs[i]
                if which == 0:
                    _store_l256(scr, q_ref, cols, (t * cos + _rope_partner(t) * sin) * (HEAD_DIM ** -0.5))
                elif which == 1:
                    _store_l256(scr, k_ref, cols, t * cos + _rope_partner(t) * sin)
                else:
                    _store_l256(scr, v_ref, cols, t)

    return pl.pallas_call(
        body, name="in_fwd", grid=(s // tm,), scratch_shapes=[pltpu.VMEM((tm, LANES), F32)] * 2,
        in_specs=[_rows(tm, D_MODEL), _full((1, D_MODEL)), _full((D_MODEL, IN_COLS)), _rows(tm, LANES),
                  _rows(tm, LANES)],
        out_specs=[_rows(tm, 2 * WIDTH_A), _rows(tm, WIDTH_B), _rows(tm, WIDTH_B), _rows(tm, WIDTH_B),
                   _rows(tm, D_MODEL)],
        out_shape=[jax.ShapeDtypeStruct((s, 2 * WIDTH_A), F32), jax.ShapeDtypeStruct((s, WIDTH_B), MXU_DTYPE),
                   jax.ShapeDtypeStruct((s, WIDTH_B), MXU_DTYPE), jax.ShapeDtypeStruct((s, WIDTH_B), MXU_DTYPE),
                   jax.ShapeDtypeStruct((s, D_MODEL), MXU_DTYPE)],
        compiler_params=_params(("arbitrary",)),
    )(x, g_mix, w_in, cos_t, sin_t)


class _Branch:
    def __init__(self, dil, s, qn=BLOCK):
        self.dil = dil
        i = np.arange(L_GROUP)
        if dil == 16:
            nblk = qn // 16
            self.grid = (16, s // (L_BLOCK * nblk))
            self.shape = (nblk, 1, 1, L_GROUP)
            self.index = lambda r, n: (n, r // 4, r % 4, 0, 0)
            pos = (np.arange(nblk)[:, None] * 16 + i[None, :]).reshape(-1)
        elif dil == 4:
            nblk = qn // 64
            self.grid = (4, s // (L_BLOCK * nblk))
            self.shape = (nblk, 4, 1, L_GROUP)
            self.index = lambda r, n: (n, 0, r, 0, 0)
            pos = (np.arange(nblk)[:, None, None] * 64 + np.arange(4)[None, :, None]
                   + 4 * i[None, None, :]).reshape(-1)
        else:
            self.grid = (1, s // L_BLOCK)
            self.shape = (1, 4, 4, L_GROUP)
            self.index = lambda r, n: (n, 0, 0, 0, 0)
            pos = (np.arange(16)[:, None] + 16 * i[None, :]).reshape(-1)
        self.qn = pos.shape[0]
        self.nb = self.grid[1]
        dist = pos[:, None] - np.concatenate([pos - self.qn, pos])[None, :]
        band = (dist >= 0) & (dist <= BLOCK)
        start = band & (np.arange(2 * self.qn)[None, :] >= self.qn)
        self.bias = np.where(np.stack([band, start]), 0.0, -np.inf).astype(np.float32)

    def view(self, a):
        return a.reshape(a.shape[0] // L_BLOCK, 4, 4, L_GROUP, a.shape[1])

    def spec(self, w, step=lambda n: n):
        return pl.BlockSpec(self.shape + (w,), lambda r, n: self.index(r, step(n)))

    def bias_spec(self, step=lambda n: n):
        return pl.BlockSpec((1, self.qn, 2 * self.qn), lambda r, n: (jnp.where(step(n) == 0, 1, 0), 0, 0))

    def load(self, ref, cols=slice(None)):
        x = ref[:, :, :, :, cols]
        return x.reshape(self.qn, x.shape[-1])

    def store(self, ref, cols, value):
        ref[:, :, :, :, cols] = value.reshape(self.shape + (value.shape[-1],))


def _attn_fwd_branch(q, k, v, dil):
    s = q.shape[0]
    br = _Branch(dil, s)
    qn = br.qn

    def body(bias_ref, q_ref, kc_ref, kp_ref, vc_ref, vp_ref, o_ref, lse_ref):
        bias2 = jnp.concatenate([bias_ref[0], bias_ref[0]], axis=0)
        lo = lax.broadcasted_iota(jnp.int32, (qn, LANES), 1) < HEAD_DIM
        mask_lo = lo.astype(F32).astype(MXU_DTYPE)
        for hp in range(HEADS_B // 2):
            cols = slice(hp * LANES, (hp + 1) * LANES)
            qp = br.load(q_ref, cols)
            kcat = jnp.concatenate([br.load(kp_ref, cols), br.load(kc_ref, cols)], axis=0)
            vcat = jnp.concatenate([br.load(vp_ref, cols), br.load(vc_ref, cols)], axis=0)
            sc = _dot_nt(jnp.concatenate([qp * mask_lo, qp * (1 - mask_lo)], axis=0), kcat) + bias2
            m = jnp.max(sc, axis=1, keepdims=True)
            p = jnp.exp(sc - m)
            l = jnp.sum(p, axis=1, keepdims=True)
            out = _dot(p.astype(MXU_DTYPE), vcat) / l
            lse = m + jnp.log(l)
            br.store(o_ref, cols, jnp.where(lo, out[:qn], out[qn:]))
            br.store(lse_ref, cols, jnp.where(lo, lse[:qn], lse[qn:]))

    before = lambda n: jnp.maximum(n - 1, 0)
    res = pl.pallas_call(
        body, name="attn_fwd_d%d" % dil, grid=br.grid,
        in_specs=[br.bias_spec(), br.spec(WIDTH_B), br.spec(WIDTH_B), br.spec(WIDTH_B, before), br.spec(WIDTH_B),
                  br.spec(WIDTH_B, before)],
        out_specs=[br.spec(WIDTH_B), br.spec(WIDTH_B)],
        out_shape=[jax.ShapeDtypeStruct((s // L_BLOCK, 4, 4, L_GROUP, WIDTH_B), F32)] * 2,
        compiler_params=_params(("arbitrary", "arbitrary")),
    )(jnp.asarray(br.bias), br.view(q), br.view(k), br.view(k), br.view(v), br.view(v))
    return tuple(a.reshape(s, WIDTH_B) for a in res)


def _attn_merge(outs, lses):
    s = outs[0].shape[0]
    tm = 512
    nbr = len(outs)

    def body(*refs):
        o_refs, l_refs, y_ref, lse_ref = refs[:nbr], refs[nbr:2 * nbr], refs[2 * nbr], refs[2 * nbr + 1]
        ls = [r[...] for r in l_refs]
        top = functools.reduce(jnp.maximum, ls)
        ws = [jnp.exp(l - top) for l in ls]
        den = functools.reduce(jnp.add, ws)
        num = functools.reduce(jnp.add, [w * r[...] for w, r in zip(ws, o_refs)])
        y_ref[...] = num / den
        lse_ref[...] = top + jnp.log(den)

    return pl.pallas_call(
        body, name="attn_merge", grid=(s // tm,), in_specs=[_rows(tm, WIDTH_B)] * (2 * nbr),
        out_specs=[_rows(tm, WIDTH_B)] * 2, out_shape=[jax.ShapeDtypeStruct((s, WIDTH_B), F32)] * 2,
        compiler_params=_params(("arbitrary",)),
    )(*outs, *lses)


def _sgu_forward_tile(uv, w_ref, bias, g_sgu):
    tm = uv.shape[0]
    u = uv[:, :WIDTH_A]
    v = uv[:, WIDTH_A:]
    ug, tu = _gelu_parts(u)
    vg, tv = _gelu_parts(v)
    mu = jnp.mean(vg, axis=-1, keepdims=True)
    vc = vg - mu
    rs = lax.rsqrt(jnp.mean(vc * vc, axis=-1, keepdims=True) + EPS)
    vhat = vc * rs
    vn = (vhat * g_sgu).astype(MXU_DTYPE)
    masks = _half_masks(MXU_DTYPE)
    chunks = []
    for c in range(tm // CHUNK):
        rows = slice(c * CHUNK, (c + 1) * CHUNK)
        groups = []
        for gp in range(2):
            vn_g = vn[rows, gp * LANES:(gp + 1) * LANES]
            groups.append(_dot(w_ref[2 * gp], vn_g * masks[0]) + _dot(w_ref[2 * gp + 1], vn_g * masks[1]))
        chunks.append(jnp.concatenate(groups, axis=1) + bias)
    mixed = jnp.concatenate(chunks, axis=0)
    return dict(u=u, v=v, ug=ug, tu=tu, tv=tv, rs=rs, vhat=vhat, vn=vn, mixed=mixed, ya=ug * mixed)


def _sgu_fwd(uv, w_tril, bias, g_sgu, g_a):
    s = uv.shape[0]
    tm = 512

    def body(uv_ref, w_ref, b_ref, gs_ref, ga_ref, o_ref):
        t = _sgu_forward_tile(uv_ref[...], w_ref, b_ref[...], gs_ref[...])
        n, _ = _rms_stats(t['ya'])
        o_ref[...] = (n * ga_ref[...]).astype(MXU_DTYPE)

    return pl.pallas_call(
        body, name="sgu_fwd", grid=(s // tm,),
        in_specs=[_rows(tm, 2 * WIDTH_A), _full((HEADS_A, CHUNK, CHUNK)), _full((CHUNK, WIDTH_A)),
                  _full((1, WIDTH_A)), _full((1, WIDTH_A))],
        out_specs=_rows(tm, WIDTH_A), out_shape=jax.ShapeDtypeStruct((s, WIDTH_A), MXU_DTYPE),
        compiler_params=_params(("arbitrary",)),
    )(uv, w_tril, bias, g_sgu, g_a)


def _out_fwd(ya_n, y_b, g_b, w_out, x):
    s = x.shape[0]
    tm = 512
    nc = WIDTH_B // LANES

    def body(ya_ref, *refs):
        yb_refs = refs[:nc]
        g_ref, w_ref, x_ref, h_ref, yn_ref = refs[nc:]
        n, _ = _rms_stats(_load_l256(yb_refs, tm))
        yn = jnp.concatenate([ya_ref[...], (n * g_ref[...]).astype(MXU_DTYPE)], axis=1)
        yn_ref[...] = yn
        h_ref[...] = x_ref[...] + _dot(yn, w_ref[...])

    return pl.pallas_call(
        body, name="out_fwd", grid=(s // tm,),
        in_specs=[_rows(tm, WIDTH_A)] + _col_specs(tm, WIDTH_B) + [_full((1, WIDTH_B)), _full((D_MODEL, D_MODEL)),
                                                                 _rows(tm, D_MODEL)],
        out_specs=[_rows(tm, D_MODEL), _rows(tm, D_MODEL)],
        out_shape=[jax.ShapeDtypeStruct((s, D_MODEL), F32), jax.ShapeDtypeStruct((s, D_MODEL), MXU_DTYPE)],
        compiler_params=_params(("arbitrary",)),
    )(ya_n, *([y_b] * nc), g_b, w_out, x)


def _ffn_fwd(h1, g_ffn, w_gate, w_up, w_down):
    s = h1.shape[0]
    tm = 512

    def body(h_ref, g_ref, wg_ref, wu_ref, wd_ref, o_ref, gate_ref, up_ref, hn_ref):
        h = h_ref[...]
        n, _ = _rms_stats(h)
        hn = (n * g_ref[...]).astype(MXU_DTYPE)
        hn_ref[...] = hn
        out = h
        for c in range(D_FF // FF_STRIP):
            cols = slice(c * FF_STRIP, (c + 1) * FF_STRIP)
            gate = _dot(hn, wg_ref[:, cols])
            up = _dot(hn, wu_ref[:, cols])
            gate_ref[:, cols] = gate.astype(MXU_DTYPE)
            up_ref[:, cols] = up.astype(MXU_DTYPE)
            act = (gate * jax.nn.sigmoid(gate) * up).astype(MXU_DTYPE)
            out = out + _dot(act, wd_ref[cols, :])
        o_ref[...] = out

    return pl.pallas_call(
        body, name="ffn_fwd", grid=(s // tm,),
        in_specs=[_rows(tm, D_MODEL), _full((1, D_MODEL)), _full((D_MODEL, D_FF)), _full((D_MODEL, D_FF)),
                  _full((D_FF, D_MODEL))],
        out_specs=[_rows(tm, D_MODEL), _rows(tm, D_FF), _rows(tm, D_FF), _rows(tm, D_MODEL)],
        out_shape=[jax.ShapeDtypeStruct((s, D_MODEL), F32), jax.ShapeDtypeStruct((s, D_FF), MXU_DTYPE),
                   jax.ShapeDtypeStruct((s, D_FF), MXU_DTYPE), jax.ShapeDtypeStruct((s, D_MODEL), MXU_DTYPE)],
        compiler_params=_params(("arbitrary",)),
    )(h1, g_ffn, w_gate, w_up, w_down)


def _ple_loss(h2, p, target, g_ple, w_pg, w_pg_t, w_pp, g_final):
    s = h2.shape[0]
    tm = 256

    def body(h_ref, p_ref, t_ref, gp_ref, wg_ref, wgt_ref, wp_ref, gf_ref,
             loss_ref, dh_ref, dz_ref, dpp_ref, hn_ref, dgp_ref, dgf_ref):
        @pl.when(pl.program_id(0) == 0)
        def _():
            loss_ref[...] = jnp.zeros_like(loss_ref)
            dgp_ref[...] = jnp.zeros_like(dgp_ref)
            dgf_ref[...] = jnp.zeros_like(dgf_ref)

        h2t = h_ref[...]
        n2, r2 = _rms_stats(h2t)
        hn = (n2 * gp_ref[...]).astype(MXU_DTYPE)
        hn_ref[...] = hn
        gate = jax.nn.sigmoid(_dot(hn, wg_ref[...]))
        pp = _dot(p_ref[...].astype(MXU_DTYPE), wp_ref[...])
        h3 = h2t + gate * pp
        n3, r3 = _rms_stats(h3)
        diff = n3 * gf_ref[...] - t_ref[...]
        loss_ref[...] += jnp.full(loss_ref.shape, 0.5 * jnp.sum(diff * diff) / D_MODEL, F32)
        dy = diff * (1.0 / D_MODEL)
        dgf_ref[...] += jnp.sum(dy * n3, axis=0, keepdims=True)
        dh3 = _rms_bwd(dy * gf_ref[...], n3, r3)
        dpp_ref[...] = (dh3 * gate).astype(MXU_DTYPE)
        dz = (dh3 * pp * gate * (1.0 - gate)).astype(MXU_DTYPE)
        dz_ref[...] = dz
        dhn = _dot(dz, wgt_ref[...])
        dgp_ref[...] += jnp.sum(dhn * n2, axis=0, keepdims=True)
        dh_ref[...] = dh3 + _rms_bwd(dhn * gp_ref[...], n2, r2)

    return pl.pallas_call(
        body, name="ple_loss", grid=(s // tm,),
        in_specs=[_rows(tm, D_MODEL), _rows(tm, PLE_DIM), _rows(tm, D_MODEL), _full((1, D_MODEL)),
                  _full((D_MODEL, D_MODEL)), _full((D_MODEL, D_MODEL)), _full((PLE_DIM, D_MODEL)),
                  _full((1, D_MODEL))],
        out_specs=[_full((1, LANES)), _rows(tm, D_MODEL), _rows(tm, D_MODEL), _rows(tm, D_MODEL),
                   _rows(tm, D_MODEL), _full((1, D_MODEL)), _full((1, D_MODEL))],
        out_shape=[jax.ShapeDtypeStruct((1, LANES), F32), jax.ShapeDtypeStruct((s, D_MODEL), F32),
                   jax.ShapeDtypeStruct((s, D_MODEL), MXU_DTYPE), jax.ShapeDtypeStruct((s, D_MODEL), MXU_DTYPE),
                   jax.ShapeDtypeStruct((s, D_MODEL), MXU_DTYPE), jax.ShapeDtypeStruct((1, D_MODEL), F32),
                   jax.ShapeDtypeStruct((1, D_MODEL), F32)],
        compiler_params=_params(("arbitrary",)),
    )(h2, p, target, g_ple, w_pg, w_pg_t, w_pp, g_final)


def _ffn_bwd(dh2, h1, gate, up, g_ffn, w_down_t, w_gate_t, w_up_t):
    s = h1.shape[0]
    tm = 256

    def body(dh_ref, h_ref, gate_ref, up_ref, g_ref, wdt_ref, wgt_ref, wut_ref,
             o_ref, act_ref, dg_ref, du_ref, dgn_ref):
        @pl.when(pl.program_id(0) == 0)
        def _():
            dgn_ref[...] = jnp.zeros_like(dgn_ref)

        dh = dh_ref[...]
        dhb = dh.astype(MXU_DTYPE)
        dhn = jnp.zeros((tm, D_MODEL), F32)
        for c in range(D_FF // FF_STRIP):
            cols = slice(c * FF_STRIP, (c + 1) * FF_STRIP)
            dact = _dot(dhb, wdt_ref[:, cols])
            g = gate_ref[:, cols].astype(F32)
            u = up_ref[:, cols].astype(F32)
            sg = jax.nn.sigmoid(g)
            silu = g * sg
            act_ref[:, cols] = (silu * u).astype(MXU_DTYPE)
            du = (dact * silu).astype(MXU_DTYPE)
            dg = (dact * u * sg * (1.0 + g * (1.0 - sg))).astype(MXU_DTYPE)
            du_ref[:, cols] = du
            dg_ref[:, cols] = dg
            dhn = dhn + _dot(dg, wgt_ref[cols, :]) + _dot(du, wut_ref[cols, :])
        n, r = _rms_stats(h_ref[...])
        dgn_ref[...] += jnp.sum(dhn * n, axis=0, keepdims=True)
        o_ref[...] = dh + _rms_bwd(dhn * g_ref[...], n, r)

    return pl.pallas_call(
        body, name="ffn_bwd", grid=(s // tm,),
        in_specs=[_rows(tm, D_MODEL), _rows(tm, D_MODEL), _rows(tm, D_FF), _rows(tm, D_FF), _full((1, D_MODEL)),
                  _full((D_MODEL, D_FF)), _full((D_FF, D_MODEL)), _full((D_FF, D_MODEL))],
        out_specs=[_rows(tm, D_MODEL), _rows(tm, D_FF), _rows(tm, D_FF), _rows(tm, D_FF), _full((1, D_MODEL))],
        out_shape=[jax.ShapeDtypeStruct((s, D_MODEL), F32), jax.ShapeDtypeStruct((s, D_FF), MXU_DTYPE),
                   jax.ShapeDtypeStruct((s, D_FF), MXU_DTYPE), jax.ShapeDtypeStruct((s, D_FF), MXU_DTYPE),
                   jax.ShapeDtypeStruct((1, D_MODEL), F32)],
        compiler_params=_params(("arbitrary",)),
    )(dh2, h1, gate, up, g_ffn, w_down_t, w_gate_t, w_up_t)


def _out_bwd(dh1, y_b, g_b, w_out_t):
    s = dh1.shape[0]
    tm = 512
    nc = WIDTH_B // LANES

    def body(dh_ref, *refs):
        yb_refs = refs[:nc]
        g_ref, wt_ref, dya_ref, dyb_ref, dg_ref, scr = refs[nc:]

        @pl.when(pl.program_id(0) == 0)
        def _():
            dg_ref[...] = jnp.zeros_like(dg_ref)

        dy = _dot(dh_ref[...].astype(MXU_DTYPE), wt_ref[...])
        dya_ref[...] = dy[:, :WIDTH_A]
        dyb = dy[:, WIDTH_A:]
        n, r = _rms_stats(_load_l256(yb_refs, tm))
        dg_ref[...] += jnp.sum(dyb * n, axis=0, keepdims=True)
        dyb_in = _rms_bwd(dyb * g_ref[...], n, r)
        for j in range(nc):
            cols = slice(j * LANES, (j + 1) * LANES)
            _store_l256(scr, dyb_ref, cols, dyb_in[:, cols])

    return pl.pallas_call(
        body, name="out_bwd", grid=(s // tm,), scratch_shapes=[pltpu.VMEM((tm, LANES), F32)],
        in_specs=[_rows(tm, D_MODEL)] + _col_specs(tm, WIDTH_B) + [_full((1, WIDTH_B)), _full((D_MODEL, D_MODEL))],
        out_specs=[_rows(tm, WIDTH_A), _rows(tm, WIDTH_B), _full((1, WIDTH_B))],
        out_shape=[jax.ShapeDtypeStruct((s, WIDTH_A), F32), jax.ShapeDtypeStruct((s, WIDTH_B), F32),
                   jax.ShapeDtypeStruct((1, WIDTH_B), F32)],
        compiler_params=_params(("arbitrary",)),
    )(dh1, *([y_b] * nc), g_b, w_out_t)


def _attn_bwd_branch(q, k, v, do, o, lse, grads, dil):
    s = q.shape[0]
    br = _Branch(dil, s)
    qn, nb = br.qn, br.nb
    first = grads is None

    def body(*refs):
        bias_ref, q_ref, kc_ref, kp_ref, vc_ref, vp_ref, do_ref, o_ref, lse_ref = refs[:9]
        if first:
            rest = refs[9:]
        else:
            dq_in, dk_in, dv_in = refs[9:12]
            rest = refs[12:]
        dq_ref, dk_ref, dv_ref, dk_carry, dv_carry = rest
        n = pl.program_id(1)

        @pl.when(n == 0)
        def _():
            dk_carry[...] = jnp.zeros_like(dk_carry)
            dv_carry[...] = jnp.zeros_like(dv_carry)

        @pl.when(n < nb)
        def _():
            bias2 = jnp.concatenate([bias_ref[0], bias_ref[0]], axis=0)
            lane = lax.broadcasted_iota(jnp.int32, (qn, LANES), 1)
            lo = lane < HEAD_DIM
            mask_f = lo.astype(F32)
            mask_lo = mask_f.astype(MXU_DTYPE)
            for hp in range(HEADS_B // 2):
                cols = slice(hp * LANES, (hp + 1) * LANES)
                h0, h1 = 2 * hp, 2 * hp + 1
                qp = br.load(q_ref, cols)
                kcat = jnp.concatenate([br.load(kp_ref, cols), br.load(kc_ref, cols)], axis=0)
                vcat = jnp.concatenate([br.load(vp_ref, cols), br.load(vc_ref, cols)], axis=0)
                dop = br.load(do_ref, cols)
                prod = dop * br.load(o_ref, cols)
                prod_lo = prod * mask_f
                delta = jnp.concatenate([jnp.sum(prod_lo, axis=1, keepdims=True),
                                         jnp.sum(prod - prod_lo, axis=1, keepdims=True)], axis=0)
                qs = jnp.concatenate([qp * mask_lo, qp * (1 - mask_lo)], axis=0)
                dos = jnp.concatenate([dop * mask_f, dop * (1.0 - mask_f)], axis=0).astype(MXU_DTYPE)
                lse = br.load(lse_ref, cols)
                lse2 = jnp.concatenate([lse[:, :1], lse[:, HEAD_DIM:HEAD_DIM + 1]], axis=0)
                p = jnp.exp(_dot_nt(qs, kcat) + bias2 - lse2)
                ds = (p * (_dot_nt(dos, vcat) - delta)).astype(MXU_DTYPE)
                dvc = _dot_tn(p.astype(MXU_DTYPE), dos)
                dkc = _dot_tn(ds, qs)
                dq2 = _dot(ds, kcat)
                dq = jnp.where(lo, dq2[:qn], dq2[qn:])
                dk_prev = dk_carry[:, cols] + dkc[:qn]
                dv_prev = dv_carry[:, cols] + dvc[:qn]
                if not first:
                    dq = dq + br.load(dq_in, cols)
                    dk_prev = dk_prev + br.load(dk_in, cols)
                    dv_prev = dv_prev + br.load(dv_in, cols)
                br.store(dq_ref, cols, dq)
                br.store(dk_ref, cols, dk_prev)
                br.store(dv_ref, cols, dv_prev)
                dk_carry[:, cols] = dkc[qn:]
                dv_carry[:, cols] = dvc[qn:]

        @pl.when(n == nb)
        def _():
            dk_last = dk_carry[...]
            dv_last = dv_carry[...]
            if not first:
                dk_last = dk_last + br.load(dk_in)
                dv_last = dv_last + br.load(dv_in)
            br.store(dk_ref, slice(None), dk_last)
            br.store(dv_ref, slice(None), dv_last)

    cur = lambda n: jnp.minimum(n, nb - 1)
    before = lambda n: jnp.maximum(cur(n) - 1, 0)
    late = lambda n: jnp.maximum(n - 1, 0)
    in_specs = [br.bias_spec(cur), br.spec(WIDTH_B, cur), br.spec(WIDTH_B, cur), br.spec(WIDTH_B, before),
                br.spec(WIDTH_B, cur), br.spec(WIDTH_B, before), br.spec(WIDTH_B, cur), br.spec(WIDTH_B, cur),
                br.spec(WIDTH_B, cur)]
    args = [jnp.asarray(br.bias)] + [br.view(a) for a in (q, k, k, v, v, do, o, lse)]
    if not first:
        in_specs += [br.spec(WIDTH_B, cur), br.spec(WIDTH_B, late), br.spec(WIDTH_B, late)]
        args += [br.view(g) for g in grads]
    res = pl.pallas_call(
        body, name="attn_bwd_d%d" % dil, grid=(br.grid[0], nb + 1), in_specs=in_specs,
        out_specs=[br.spec(WIDTH_B, cur), br.spec(WIDTH_B, late), br.spec(WIDTH_B, late)],
        out_shape=[jax.ShapeDtypeStruct((s // L_BLOCK, 4, 4, L_GROUP, WIDTH_B), F32)] * 3,
        scratch_shapes=[pltpu.VMEM((qn, WIDTH_B), F32), pltpu.VMEM((qn, WIDTH_B), F32)],
        compiler_params=_params(("arbitrary", "arbitrary")),
    )(*args)
    return tuple(a.reshape(s, WIDTH_B) for a in res)


def _sgu_bwd(uv, dya_n, w_tril, w_tril_t, bias, g_sgu, g_a):
    s = uv.shape[0]
    tm = 512

    def body(uv_ref, dy_ref, w_ref, wt_ref, b_ref, gs_ref, ga_ref, duv_ref, dw_ref, db_ref, dgs_ref, dga_ref,
             db_acc):
        i = pl.program_id(0)

        @pl.when(i == 0)
        def _():
            dw_ref[...] = jnp.zeros_like(dw_ref)
            dgs_ref[...] = jnp.zeros_like(dgs_ref)
            dga_ref[...] = jnp.zeros_like(dga_ref)
            db_acc[...] = jnp.zeros_like(db_acc)

        t = _sgu_forward_tile(uv_ref[...], w_ref, b_ref[...], gs_ref[...])
        na, ra = _rms_stats(t['ya'])
        dyn = dy_ref[...]
        dga_ref[...] += jnp.sum(dyn * na, axis=0, keepdims=True)
        dya = _rms_bwd(dyn * ga_ref[...], na, ra)
        dug = dya * t['mixed']
        dmixed = dya * t['ug']
        dmb = dmixed.astype(MXU_DTYPE)
        masks = _half_masks(MXU_DTYPE)
        chunks = []
        db = jnp.zeros((CHUNK, WIDTH_A), F32)
        for c in range(tm // CHUNK):
            rows = slice(c * CHUNK, (c + 1) * CHUNK)
            db = db + dmixed[rows]
            groups = []
            for gp in range(2):
                cols = slice(gp * LANES, (gp + 1) * LANES)
                dm_g = dmb[rows, cols]
                vn_g = t['vn'][rows, cols]
                dvn_g = jnp.zeros((CHUNK, LANES), F32)
                for j in range(2):
                    dm_h = dm_g * masks[j]
                    dvn_g = dvn_g + _dot(wt_ref[2 * gp + j], dm_h)
                    dw_ref[2 * gp + j] += _dot_nt(dm_h, vn_g)
                groups.append(dvn_g)
            chunks.append(jnp.concatenate(groups, axis=1))
        db_acc[...] += db
        dvn = jnp.concatenate(chunks, axis=0)
        vhat = t['vhat']
        dgs_ref[...] += jnp.sum(dvn * vhat, axis=0, keepdims=True)
        dvh = dvn * gs_ref[...]
        dvg = t['rs'] * (dvh - jnp.mean(dvh, axis=-1, keepdims=True)
                         - vhat * jnp.mean(dvh * vhat, axis=-1, keepdims=True))
        duv_ref[:, :WIDTH_A] = (dug * _gelu_grad(t['u'], t['tu'])).astype(MXU_DTYPE)
        duv_ref[:, WIDTH_A:] = (dvg * _gelu_grad(t['v'], t['tv'])).astype(MXU_DTYPE)

        @pl.when(i == pl.num_programs(0) - 1)
        def _():
            lane_a = lax.broadcasted_iota(jnp.int32, (CHUNK, WIDTH_A), 1)
            lane = lax.broadcasted_iota(jnp.int32, (CHUNK, LANES), 1)
            acc = db_acc[...]
            out = jnp.zeros((CHUNK, LANES), F32)
            for h in range(HEADS_A):
                col = jnp.sum(jnp.where(lane_a // HEAD_DIM == h, acc, 0.0), axis=1, keepdims=True)
                out = jnp.where(lane == h, col, out)
            db_ref[...] = out
            causal = (lax.broadcasted_iota(jnp.int32, (CHUNK, CHUNK), 0)
                      >= lax.broadcasted_iota(jnp.int32, (CHUNK, CHUNK), 1))
            for h in range(HEADS_A):
                dw_ref[h] = jnp.where(causal, dw_ref[h], 0.0)

    return pl.pallas_call(
        body, name="sgu_bwd", grid=(s // tm,),
        in_specs=[_rows(tm, 2 * WIDTH_A), _rows(tm, WIDTH_A), _full((HEADS_A, CHUNK, CHUNK)),
                  _full((HEADS_A, CHUNK, CHUNK)), _full((CHUNK, WIDTH_A)), _full((1, WIDTH_A)),
                  _full((1, WIDTH_A))],
        out_specs=[_rows(tm, 2 * WIDTH_A), _full((HEADS_A, CHUNK, CHUNK)), _full((CHUNK, LANES)),
                   _full((1, WIDTH_A)), _full((1, WIDTH_A))],
        out_shape=[jax.ShapeDtypeStruct((s, 2 * WIDTH_A), MXU_DTYPE),
                   jax.ShapeDtypeStruct((HEADS_A, CHUNK, CHUNK), F32), jax.ShapeDtypeStruct((CHUNK, LANES), F32),
                   jax.ShapeDtypeStruct((1, WIDTH_A), F32), jax.ShapeDtypeStruct((1, WIDTH_A), F32)],
        scratch_shapes=[pltpu.VMEM((CHUNK, WIDTH_A), F32)],
        compiler_params=_params(("arbitrary",)),
    )(uv, dya_n, w_tril, w_tril_t, bias, g_sgu, g_a)


def _in_bwd_proj(duv, dq, dk, dv, cos_t, sin_t):
    s = duv.shape[0]
    tm = 512
    nc = WIDTH_B // LANES

    def body(duv_ref, *refs):
        dq_refs, dk_refs, dv_refs = refs[:nc], refs[nc:2 * nc], refs[2 * nc:3 * nc]
        cos_ref, sin_ref, dp_ref = refs[3 * nc:]
        cos = cos_ref[...]
        sin = sin_ref[...]
        dp_ref[:, :2 * WIDTH_A] = duv_ref[...]
        for i in range(nc):
            lo = 2 * WIDTH_A + i * LANES
            tq = _load_l256(dq_refs[i:i + 1], tm) * (HEAD_DIM ** -0.5)
            tk = _load_l256(dk_refs[i:i + 1], tm)
            dp_ref[:, lo:lo + LANES] = (tq * cos + _rope_partner(tq * sin)).astype(MXU_DTYPE)
            dp_ref[:, lo + WIDTH_B:lo + WIDTH_B + LANES] = (tk * cos + _rope_partner(tk * sin)).astype(MXU_DTYPE)
            dp_ref[:, lo + 2 * WIDTH_B:lo + 2 * WIDTH_B + LANES] = _load_l256(dv_refs[i:i + 1], tm).astype(MXU_DTYPE)

    return pl.pallas_call(
        body, name="in_bwd_proj", grid=(s // tm,),
        in_specs=[_rows(tm, 2 * WIDTH_A)] + 3 * _col_specs(tm, WIDTH_B) + [_rows(tm, LANES), _rows(tm, LANES)],
        out_specs=_rows(tm, IN_COLS), out_shape=jax.ShapeDtypeStruct((s, IN_COLS), MXU_DTYPE),
        compiler_params=_params(("arbitrary",)),
    )(duv, *([dq] * nc), *([dk] * nc), *([dv] * nc), cos_t, sin_t)


def _in_bwd_x(dproj, w_in_t, x, g_mix, dh1, after):
    s = x.shape[0]
    tm = 512

    def body(dp_ref, wt_ref, x_ref, g_ref, dh_ref, *rest):
        gx_ref, dg_ref = rest[len(after):]
        @pl.when(pl.program_id(0) == 0)
        def _():
            dg_ref[...] = jnp.zeros_like(dg_ref)

        dhn = _dot(dp_ref[...], wt_ref[...])
        n, r = _rms_stats(x_ref[...])
        dg_ref[...] += jnp.sum(dhn * n, axis=0, keepdims=True)
        gx_ref[...] = dh_ref[...] + _rms_bwd(dhn * g_ref[...], n, r)

    return pl.pallas_call(
        body, name="in_bwd_x", grid=(s // tm,),
        in_specs=[_rows(tm, IN_COLS), _full((IN_COLS, D_MODEL)), _rows(tm, D_MODEL), _full((1, D_MODEL)),
                  _rows(tm, D_MODEL)] + [pl.BlockSpec(memory_space=pl.ANY)] * len(after),
        out_specs=[_rows(tm, D_MODEL), _full((1, D_MODEL))],
        out_shape=[jax.ShapeDtypeStruct((s, D_MODEL), F32), jax.ShapeDtypeStruct((1, D_MODEL), F32)],
        compiler_params=_params(("arbitrary",)),
    )(dproj, w_in_t, x, g_mix, dh1, *after)


def _wgrad(a, b, name):
    s, m = a.shape
    n = b.shape[1]
    bm = 512 if m % 512 == 0 else FF_HALF
    ts = 1024
    nsteps = s // ts

    def body(a_ref, b_ref, o_ref, acc):
        kk = pl.program_id(1)

        @pl.when(kk == 0)
        def _():
            acc[...] = jnp.zeros_like(acc)

        acc[...] += _dot_tn(a_ref[...].astype(MXU_DTYPE), b_ref[...].astype(MXU_DTYPE))

        @pl.when(kk == nsteps - 1)
        def _():
            o_ref[...] = acc[...].astype(o_ref.dtype)

    return pl.pallas_call(
        body, name=name, grid=(m // bm, nsteps),
        in_specs=[pl.BlockSpec((ts, bm), lambda i, kk: (kk, i)), pl.BlockSpec((ts, n), lambda i, kk: (kk, 0))],
        out_specs=pl.BlockSpec((bm, n), lambda i, kk: (i, 0)), out_shape=jax.ShapeDtypeStruct((m, n), jnp.bfloat16),
        scratch_shapes=[pltpu.VMEM((bm, n), F32)],
        compiler_params=_params(("arbitrary", "arbitrary")),
    )(a, b)


def _rope_tables(s):
    half = HEAD_DIM // 2
    inv = ROPE_THETA ** (-jnp.arange(half, dtype=F32) / half)
    ang = jnp.arange(s, dtype=F32)[:, None] * inv[None, :]
    cos = jnp.cos(ang)
    sin = jnp.sin(ang)
    cos_t = jnp.concatenate([cos, cos, cos, cos], axis=1)
    sin_t = jnp.concatenate([-sin, sin, -sin, sin], axis=1)
    return cos_t, sin_t


MESH = pl.DeviceIdType.MESH
ANY = pl.BlockSpec(memory_space=pl.ANY)
SEM = pl.BlockSpec(memory_space=pltpu.SEMAPHORE)
SPLIT_COPY = pltpu.CompilerParams(has_side_effects=pltpu.SideEffectType.DATAFLOW_SIDE_EFFECTING)
SLAB_IS_TRANSPOSED = {'w_in': True, 'w_out': False, 'w_gate': True, 'w_up': True, 'w_down': False,
                      'w_ple_gate': False, 'w_ple_proj': True}


def _place():
    x, y, c = lax.axis_index("x"), lax.axis_index("y"), lax.axis_index("c")
    other_chips = [(1 - x, y), (x, 1 - y), (1 - x, 1 - y)]
    return x, y, c, other_chips


def _chip_of(chip):
    return 2 * chip[0] + chip[1]


def _half(ref, lead, hc):
    hr = ref.shape[1] // 2
    return ref.at[lead, pl.ds(hc * hr, hr), :]


def _put_own(stack, own, index):
    return lax.dynamic_update_slice(stack, own[None], (index,) + (0,) * own.ndim)


def _all_gather_now(slab):
    rows, cols = slab.shape

    def body(x_ref, out_ref, send_sems, recv_sems):
        x, y, c, chips = _place()
        sibling = (x, y, 1 - c)
        hr = rows // 2

        def copy(k, src, dst, to):
            return pltpu.make_async_remote_copy(src_ref=src, dst_ref=dst, send_sem=send_sems.at[k],
                                                recv_sem=recv_sems.at[k], device_id=to, device_id_type=MESH)

        my_half = x_ref.at[pl.ds(c * hr, hr), :]
        first = [copy(j, my_half, _half(out_ref, 2 * x + y, c), (*chip, c)) for j, chip in enumerate(chips)]
        for cp in first:
            cp.start()
        passed = [copy(3 + j, _half(out_ref, _chip_of(chip), c), _half(out_ref, _chip_of(chip), c), sibling)
                  for j, chip in enumerate(chips)]
        for j, chip in enumerate(chips):
            copy(j, my_half, _half(out_ref, _chip_of(chip), c), (*chip, c)).wait_recv()
            passed[j].start()
        for j, chip in enumerate(chips):
            copy(3 + j, my_half, _half(out_ref, _chip_of(chip), 1 - c), sibling).wait_recv()
        for cp in first + passed:
            cp.wait_send()

    gathered = pl.pallas_call(
        body, name="all_gather_now", out_shape=jax.ShapeDtypeStruct((N_CHIPS, rows, cols), slab.dtype),
        in_specs=[ANY], out_specs=ANY,
        scratch_shapes=[pltpu.SemaphoreType.DMA((6,)), pltpu.SemaphoreType.DMA((6,))],
    )(slab)
    me = 2 * lax.axis_index("x") + lax.axis_index("y")
    return _put_own(gathered, slab, me).reshape(N_CHIPS * rows, cols)


def _gather_copies(slab_refs, land_refs, send_sems, recv_sems):
    x, y, c, chips = _place()
    sends, recvs = [], []
    for k, (src, land) in enumerate(zip(slab_refs, land_refs)):
        hr = src.shape[0] // 2
        for j, chip in enumerate(chips):
            for t in range(2):
                sends.append(pltpu.make_async_remote_copy(
                    src_ref=src.at[pl.ds(c * hr, hr), :], dst_ref=_half(land, 2 * x + y, c),
                    send_sem=send_sems.at[6 * k + 2 * j + t], recv_sem=recv_sems.at[6 * k + 2 * j + c],
                    device_id=(*chip, t), device_id_type=MESH))
                recvs.append(pltpu.make_async_remote_copy(
                    src_ref=src.at[pl.ds(t * hr, hr), :], dst_ref=_half(land, _chip_of(chip), t),
                    send_sem=send_sems.at[6 * k + 2 * j + t], recv_sem=recv_sems.at[6 * k + 2 * j + t],
                    device_id=(*chip, t), device_id_type=MESH))
    return sends, recvs


def _all_gather_start(slabs, after):
    n = len(slabs)

    def body(*refs):
        slab_refs, land_refs = refs[:n], refs[n:2 * n]
        send_sems, recv_sems = refs[2 * n + 1:2 * n + 3]
        token = refs[-1]
        sends, _ = _gather_copies(slab_refs, land_refs, send_sems, recv_sems)
        for cp in sends:
            cp.start()
        token[...] = jnp.zeros_like(token)

    lands = [lax.empty((N_CHIPS,) + s.shape, s.dtype) for s in slabs]
    hbm = lambda a: pltpu.HBM(a.shape, a.dtype)
    res = pl.pallas_call(
        body, name="all_gather_start",
        out_shape=(pltpu.SemaphoreType.DMA((6 * n,)), pltpu.SemaphoreType.DMA((6 * n,)), *map(hbm, slabs),
                   *map(hbm, lands), jax.ShapeDtypeStruct((8, LANES), F32)),
        in_specs=[ANY] * (2 * n + 1),
        out_specs=(SEM, SEM, *([ANY] * (2 * n)), pl.BlockSpec(memory_space=pltpu.VMEM)),
        input_output_aliases={i: 2 + i for i in range(2 * n)}, compiler_params=SPLIT_COPY,
    )(*[pltpu.with_memory_space_constraint(a, pltpu.HBM) for a in list(slabs) + lands], after)
    return res[:-1], res[-1]


def _all_gather_wait(handle, after):
    send_sems, recv_sems = handle[:2]
    n = (len(handle) - 2) // 2
    slabs, lands = handle[2:2 + n], handle[2 + n:]

    def body(*refs):
        slab_refs, land_refs = refs[:n], refs[n:2 * n]
        send_sems, recv_sems = refs[2 * n:2 * n + 2]
        sends, recvs = _gather_copies(slab_refs, land_refs, send_sems, recv_sems)
        for cp in sends:
            cp.wait_send()
        for cp in recvs:
            cp.wait_recv()

    hbm = lambda a: pltpu.HBM(a.shape, a.dtype)
    res = pl.pallas_call(
        body, name="all_gather_wait", out_shape=tuple(map(hbm, list(slabs) + list(lands))),
        in_specs=[ANY] * (2 * n) + [SEM, SEM, ANY], out_specs=tuple([ANY] * (2 * n)),
        input_output_aliases={i: i for i in range(2 * n)}, compiler_params=SPLIT_COPY,
    )(*slabs, *lands, send_sems, recv_sems, after)
    me = 2 * lax.axis_index("x") + lax.axis_index("y")
    return [_put_own(land, slab, me).reshape(N_CHIPS * slab.shape[0], slab.shape[1])
            for slab, land in zip(res[:n], res[n:])]


def _scatter_copies(part_refs, land_refs, send_sems, recv_sems):
    x, y, c, chips = _place()
    me = 4 * x + 2 * y + c
    sends, recvs = [], []
    for k, (part, land) in enumerate(zip(part_refs, land_refs)):
        for j, chip in enumerate(chips):
            for t in range(2):
                sends.append(pltpu.make_async_remote_copy(
                    src_ref=part.at[_chip_of(chip)], dst_ref=land.at[me],
                    send_sem=send_sems.at[7 * k + 2 * j + t], recv_sem=recv_sems.at[7 * k + 2 * j + c],
                    device_id=(*chip, t), device_id_type=MESH))
                recvs.append(pltpu.make_async_remote_copy(
                    src_ref=part.at[_chip_of(chip)], dst_ref=land.at[2 * _chip_of(chip) + t],
                    send_sem=send_sems.at[7 * k + 2 * j + t], recv_sem=recv_sems.at[7 * k + 2 * j + t],
                    device_id=(*chip, t), device_id_type=MESH))
        sends.append(pltpu.make_async_remote_copy(
            src_ref=part.at[2 * x + y], dst_ref=land.at[me], send_sem=send_sems.at[7 * k + 6],
            recv_sem=recv_sems.at[7 * k + 6], device_id=(x, y, 1 - c), device_id_type=MESH))
        recvs.append(pltpu.make_async_remote_copy(
            src_ref=part.at[2 * x + y], dst_ref=land.at[4 * x + 2 * y + 1 - c],
            send_sem=send_sems.at[7 * k + 6], recv_sem=recv_sems.at[7 * k + 6], device_id=(x, y, 1 - c),
            device_id_type=MESH))
    return sends, recvs


def _reduce_scatter_start(parts, name):
    n = len(parts)
    parts = [p.reshape(N_CHIPS, p.shape[0] // N_CHIPS, p.shape[1]) for p in parts]

    def body(*refs):
        part_refs, land_refs = refs[:n], refs[n:2 * n]
        send_sems, recv_sems = refs[2 * n:2 * n + 2]
        token = refs[-1]
        sends, _ = _scatter_copies(part_refs, land_refs, send_sems, recv_sems)
        for cp in sends:
            cp.start()
        token[...] = jnp.zeros_like(token)

    lands = [lax.empty((N_DEV, p.shape[1], p.shape[2]), p.dtype) for p in parts]
    hbm = lambda a: pltpu.HBM(a.shape, a.dtype)
    res = pl.pallas_call(
        body, name=name,
        out_shape=(pltpu.SemaphoreType.DMA((7 * n,)), pltpu.SemaphoreType.DMA((7 * n,)), *map(hbm, parts),
                   *map(hbm, lands), jax.ShapeDtypeStruct((8, LANES), F32)),
        in_specs=[ANY] * (2 * n), out_specs=(SEM, SEM, *([ANY] * (2 * n)), pl.BlockSpec(memory_space=pltpu.VMEM)),
        input_output_aliases={i: 2 + i for i in range(2 * n)}, compiler_params=SPLIT_COPY,
    )(*[pltpu.with_memory_space_constraint(a, pltpu.HBM) for a in parts + lands])
    return res[:-1], res[-1]


def _reduce_scatter_wait(handle, after, name):
    send_sems, recv_sems = handle[:2]
    n = (len(handle) - 2) // 2
    parts, lands = handle[2:2 + n], handle[2 + n:]

    def body(*refs):
        part_refs, land_refs = refs[:n], refs[n:2 * n]
        send_sems, recv_sems = refs[2 * n:2 * n + 2]
        sends, recvs = _scatter_copies(part_refs, land_refs, send_sems, recv_sems)
        for cp in sends:
            cp.wait_send()
        for cp in recvs:
            cp.wait_recv()

    hbm = lambda a: pltpu.HBM(a.shape, a.dtype)
    res = pl.pallas_call(
        body, name=name, out_shape=tuple(map(hbm, list(parts) + list(lands))),
        in_specs=[ANY] * (2 * n) + [SEM, SEM, ANY], out_specs=tuple([ANY] * (2 * n)),
        input_output_aliases={i: i for i in range(2 * n)}, compiler_params=SPLIT_COPY,
    )(*parts, *lands, send_sems, recv_sems, after)
    return list(zip(res[:n], res[n:]))


def _small_all_reduce(block):
    rows = block.shape[0]

    def body(x_ref, all_ref, sum_ref, send_sems, recv_sems, local_sem):
        x, y, c, chips = _place()
        me, sibling = (x, y, c), (x, y, 1 - c)

        def blk(px, py, pc):
            return all_ref.at[pl.ds((4 * px + 2 * py + pc) * rows, rows), :]

        def copy(k, who, to, src=None):
            return pltpu.make_async_remote_copy(
                src_ref=blk(*who) if src is None else src, dst_ref=blk(*who), send_sem=send_sems.at[k],
                recv_sem=recv_sems.at[k], device_id=to, device_id_type=MESH)

        mine = pltpu.make_async_copy(x_ref, blk(*me), local_sem)
        mine.start()
        first = [copy(0, me, sibling, src=x_ref)]
        first += [copy(1 + j, me, (*chip, c), src=x_ref) for j, chip in enumerate(chips)]
        for cp in first:
            cp.start()
        passed = [copy(4 + j, (*chip, c), sibling) for j, chip in enumerate(chips)]
        for j, chip in enumerate(chips):
            copy(1 + j, (*chip, c), me).wait_recv()
            passed[j].start()
        copy(0, sibling, me).wait_recv()
        for j, chip in enumerate(chips):
            copy(4 + j, (*chip, 1 - c), me).wait_recv()
        for cp in first + passed:
            cp.wait_send()
        mine.wait()
        acc = all_ref[pl.ds(0, rows), :]
        for dev in range(1, N_DEV):
            acc = acc + all_ref[pl.ds(dev * rows, rows), :]
        sum_ref[...] = acc

    vmem = pl.BlockSpec(memory_space=pltpu.VMEM)
    return pl.pallas_call(
        body, name="small_all_reduce",
        out_shape=[jax.ShapeDtypeStruct((N_DEV * rows, D_MODEL), F32), jax.ShapeDtypeStruct((rows, D_MODEL), F32)],
        in_specs=[vmem], out_specs=[vmem, vmem],
        scratch_shapes=[pltpu.SemaphoreType.DMA((7,)), pltpu.SemaphoreType.DMA((7,)), pltpu.SemaphoreType.DMA],
    )(block)[1]


def _adamw(w, g, m, v, name):
    rows, cols = w.shape
    tm = rows
    if rows > 512:
        tm = next(t for t in range(512, 7, -8) if rows % t == 0)

    def body(w_ref, g_ref, m_ref, v_ref, d_ref, nm_ref, nv_ref):
        g_ = g_ref[...]
        m_ = ADAM_B1 * m_ref[...] + (1.0 - ADAM_B1) * g_
        v_ = ADAM_B2 * v_ref[...] + (1.0 - ADAM_B2) * (g_ * g_)
        m_hat = m_ / (1.0 - ADAM_B1 ** ADAM_STEP)
        v_hat = v_ / (1.0 - ADAM_B2 ** ADAM_STEP)
        d_ref[...] = -ADAM_LR * (m_hat / (jnp.sqrt(v_hat) + ADAM_EPS) + ADAM_WD * w_ref[...])
        nm_ref[...] = m_
        nv_ref[...] = v_

    spec = pl.BlockSpec((tm, cols), lambda i: (i, 0))
    return pl.pallas_call(
        body, name=name, grid=(rows // tm,), in_specs=[spec] * 4, out_specs=[spec] * 3,
        out_shape=[jax.ShapeDtypeStruct(w.shape, F32)] * 3, compiler_params=_params(("arbitrary",)),
    )(w, g, m, v)


def _adamw_of_shares(w, own, land, m, v, name):
    rows, cols = w.shape
    tm = rows // 4
    x, y, c = lax.axis_index("x"), lax.axis_index("y"), lax.axis_index("c")
    where = jnp.stack([2 * x + y, 4 * x + 2 * y + c]).astype(jnp.int32)

    def body(where_ref, w_ref, own_ref, land_ref, m_ref, v_ref, g_ref, d_ref, nm_ref, nv_ref):
        me = where_ref[1]
        g_ = jnp.zeros((tm, cols), F32)
        for dev in range(N_DEV):
            g_ = g_ + jnp.where(me == dev, own_ref[0], land_ref[dev]).astype(F32)
        m_ = ADAM_B1 * m_ref[...] + (1.0 - ADAM_B1) * g_
        v_ = ADAM_B2 * v_ref[...] + (1.0 - ADAM_B2) * (g_ * g_)
        m_hat = m_ / (1.0 - ADAM_B1 ** ADAM_STEP)
        v_hat = v_ / (1.0 - ADAM_B2 ** ADAM_STEP)
        g_ref[...] = g_
        d_ref[...] = -ADAM_LR * (m_hat / (jnp.sqrt(v_hat) + ADAM_EPS) + ADAM_WD * w_ref[...])
        nm_ref[...] = m_
        nv_ref[...] = v_

    tile = pl.BlockSpec((tm, cols), lambda i, where_ref: (i, 0))
    spec = pltpu.PrefetchScalarGridSpec(
        num_scalar_prefetch=1, grid=(rows // tm,),
        in_specs=[tile, pl.BlockSpec((1, tm, cols), lambda i, where_ref: (where_ref[0], i, 0)),
                  pl.BlockSpec((N_DEV, tm, cols), lambda i, where_ref: (0, i, 0)), tile, tile],
        out_specs=[tile] * 4)
    return pl.pallas_call(
        body, name=name, grid_spec=spec, out_shape=[jax.ShapeDtypeStruct(w.shape, F32)] * 4,
        compiler_params=_params(("arbitrary",)),
    )(where, w, own, land, m, v)


def _pack_small(values):
    flat = jnp.concatenate([values[n].reshape(-1).astype(F32) for n in SMALL])
    return jnp.pad(flat, (0, SMALL_ROWS * D_MODEL - flat.shape[0])).reshape(SMALL_ROWS, D_MODEL)


def _unpack_small(block, shapes):
    flat = block.reshape(-1)
    out, lo = {}, 0
    for n in SMALL:
        out[n] = flat[lo:lo + SMALL_SIZES[n]].reshape(shapes[n])
        lo += SMALL_SIZES[n]
    return out


def _after(token, a):
    return a + token[:1, :1].astype(a.dtype)


def kernel(x, p, mix_norm_g, w_in, sgu_w, sgu_b, sgu_norm_g, out_norm_a, out_norm_b, w_out, ffn_norm_g, w_gate, w_up, w_down, ple_norm_g, w_ple_gate, w_ple_proj, final_norm_g, loss_target, m_mix_norm_g, m_w_in, m_sgu_w, m_sgu_b, m_sgu_norm_g, m_out_norm_a, m_out_norm_b, m_w_out, m_ffn_norm_g, m_w_gate, m_w_up, m_w_down, m_ple_norm_g, m_w_ple_gate, m_w_ple_proj, m_final_norm_g, v_mix_norm_g, v_w_in, v_sgu_w, v_sgu_b, v_sgu_norm_g, v_out_norm_a, v_out_norm_b, v_w_out, v_ffn_norm_g, v_w_gate, v_w_up, v_w_down, v_ple_norm_g, v_w_ple_gate, v_w_ple_proj, v_final_norm_g):
    given = dict(locals())
    drop_lead = lambda a, lead: a.reshape(a.shape[lead:])
    xs, ps, target = drop_lead(x, 1), drop_lead(p, 2), drop_lead(loss_target, 1)
    s = xs.shape[0]
    shard = lambda name: drop_lead(given[name], 1)

    def slab_of(name):
        local = shard(name).astype(MXU_DTYPE)
        return local.T if SLAB_IS_TRANSPOSED[name] else local

    w_in_t = _all_gather_now(slab_of('w_in'))
    later = ['w_out', 'w_gate', 'w_up', 'w_down', 'w_ple_gate', 'w_ple_proj']
    gather, token = _all_gather_start([slab_of(n) for n in later], w_in_t)

    cos_t, sin_t = _rope_tables(s)
    tril = jnp.tril(jnp.ones((CHUNK, CHUNK), F32))
    w_tril = (sgu_w.reshape(HEADS_A, CHUNK, CHUNK) * tril).astype(MXU_DTYPE)
    w_tril_t = jnp.swapaxes(w_tril, 1, 2)
    bias = jnp.repeat(sgu_b.reshape(HEADS_A, CHUNK).T, HEAD_DIM, axis=1)
    g = {n: given[n].reshape(1, -1) for n in SMALL if n not in ('sgu_w', 'sgu_b')}

    uv, q, k, v, hn1 = _in_fwd(xs, _after(token, g['mix_norm_g']), w_in_t.T, cos_t, sin_t)
    ya_n = _sgu_fwd(uv, w_tril, bias, g['sgu_norm_g'], g['out_norm_a'])
    branches = [_attn_fwd_branch(q, k, v, dil) for dil in DILATIONS]
    y_b, lse = _attn_merge([o for o, _ in branches], [l for _, l in branches])
    stacks = dict(zip(later, _all_gather_wait(gather, lse)))
    w_gate_t, w_up_t, w_pp_t = stacks['w_gate'], stacks['w_up'], stacks['w_ple_proj']
    h1, y_n = _out_fwd(ya_n, y_b, g['out_norm_b'], stacks['w_out'], xs)
    h2, gate, up, hn2 = _ffn_fwd(h1, g['ffn_norm_g'], w_gate_t.T, w_up_t.T, stacks['w_down'])
    loss, dh2, dz, dpp, hn3, d_ple_g, d_final_g = _ple_loss(
        h2, ps, target, g['ple_norm_g'], stacks['w_ple_gate'], stacks['w_ple_gate'].T, w_pp_t.T, g['final_norm_g'])

    share = {}
    share['w_ple_gate'] = _wgrad(hn3, dz, "wgrad_ple_gate")
    share['w_ple_proj'] = _wgrad(dpp, ps, "wgrad_ple_proj")
    scatter_1, token = _reduce_scatter_start([share['w_ple_gate'], share['w_ple_proj']], "reduce_scatter_start_1")
    dh1, act, dgate, dup, d_ffn_g = _ffn_bwd(dh2, h1, gate, up, _after(token, g['ffn_norm_g']), stacks['w_down'].T,
                                             w_gate_t, w_up_t)
    share['w_down'] = _wgrad(act, dh2, "wgrad_down")
    share['w_gate'] = _wgrad(dgate, hn2, "wgrad_gate")
    share['w_up'] = _wgrad(dup, hn2, "wgrad_up")
    scatter_2, token = _reduce_scatter_start([share['w_down'], share['w_gate'], share['w_up']],
                                             "reduce_scatter_start_2")
    dya_n, dyb, d_out_b = _out_bwd(dh1, y_b, _after(token, g['out_norm_b']), stacks['w_out'].T)
    share['w_out'] = _wgrad(y_n, dh1, "wgrad_out")
    scatter_3, token = _reduce_scatter_start([share['w_out']], "reduce_scatter_start_3")
    grads = None
    for dil in DILATIONS:
        grads = _attn_bwd_branch(q, k, v, dyb, y_b, lse, grads, dil)
    duv, d_sgu_w, d_sgu_b, d_sgu_g, d_out_a = _sgu_bwd(uv, dya_n, w_tril, w_tril_t, bias,
                                                       _after(token, g['sgu_norm_g']), g['out_norm_a'])
    dproj = _in_bwd_proj(duv, grads[0], grads[1], grads[2], cos_t, sin_t)
    share['w_in'] = _wgrad(dproj, hn1, "wgrad_in")
    scatter_4, token = _reduce_scatter_start([share['w_in']], "reduce_scatter_start_4")

    grads, deltas, new_m, new_v = {}, {}, {}, {}
    add_lead = lambda a: a.reshape((1,) + a.shape)

    def finish(names, handles, after, tag):
        landed = []
        for i, handle in enumerate(handles):
            landed += _reduce_scatter_wait(handle, after, "reduce_scatter_wait_%s%d" % (tag, i))
        for n, (own, land) in zip(names, landed):
            turn = (lambda a: a.T) if SLAB_IS_TRANSPOSED[n] else (lambda a: a)
            res = _adamw_of_shares(turn(shard(n)), own, land, turn(shard("m_" + n)), turn(shard("v_" + n)),
                                   "adamw_" + n)
            grads[n], deltas[n], new_m[n], new_v[n] = (add_lead(turn(a)) for a in res)

    finish(['w_ple_gate', 'w_ple_proj', 'w_down', 'w_gate', 'w_up', 'w_out'], [scatter_1, scatter_2, scatter_3], token,
           "early")
    grad_x, d_mix_g = _in_bwd_x(dproj, w_in_t, xs, g['mix_norm_g'], dh1,
                                after=[new_v[n] for n in ('w_down', 'w_gate', 'w_up', 'w_out')])

    gs = {'mix_norm_g': d_mix_g, 'sgu_w': d_sgu_w, 'sgu_b': d_sgu_b[:, :HEADS_A].T, 'sgu_norm_g': d_sgu_g,
          'out_norm_a': d_out_a, 'out_norm_b': d_out_b, 'ffn_norm_g': d_ffn_g, 'ple_norm_g': d_ple_g,
          'final_norm_g': d_final_g}
    gs_block = _pack_small(gs).at[SMALL_ROWS - 1, 0].set(loss[0, 0])
    small_sum = _small_all_reduce(gs_block)
    loss_out = small_sum[SMALL_ROWS - 1, 0]
    small_shapes = {n: given[n].shape for n in SMALL}
    finish(['w_in'], [scatter_4], small_sum, "last")

    small = {n: given[n] for n in SMALL}
    d, nm, nv = _adamw(_pack_small(small), small_sum, _pack_small({n: given["m_" + n] for n in SMALL}),
                       _pack_small({n: given["v_" + n] for n in SMALL}), "adamw_small")
    for res, blk in ((grads, small_sum), (deltas, d), (new_m, nm), (new_v, nv)):
        res.update(_unpack_small(blk, small_shapes))

    outs = [loss_out, add_lead(grad_x)]
    for res in (grads, deltas, new_m, new_v):
        outs += [res[n] for n in WEIGHT_NAMES]
    return tuple(outs)
```

```python
import functools
import math

import jax
import jax.numpy as jnp
import numpy as np
from jax import lax
from jax.experimental import pallas as pl
from jax.experimental.pallas import tpu as pltpu

F32 = jnp.float32
MXU_DTYPE = jnp.bfloat16

D_MODEL = 1024
HEAD_DIM = 64
HEADS_A = 4
HEADS_B = 12
WIDTH_A = HEADS_A * HEAD_DIM
WIDTH_B = HEADS_B * HEAD_DIM
CHUNK = 128
BLOCK = 128
DILATIONS = (1, 4, 16)
ROPE_THETA = 10000.0
D_FF = 2816
FF_HALF = D_FF // 2
FF_STRIP = FF_HALF
PLE_DIM = 256
IN_COLS = 2 * WIDTH_A + 3 * WIDTH_B
EPS = 1e-6
LANES = 128
N_CHIPS = 4
N_DEV = 8

ADAM_LR = 0.001
ADAM_B1 = 0.9
ADAM_B2 = 0.999
ADAM_EPS = 1e-08
ADAM_WD = 0.01
ADAM_STEP = 10

VMEM_LIMIT = 56 * 1024 * 1024

WEIGHT_NAMES = ['mix_norm_g', 'w_in', 'sgu_w', 'sgu_b', 'sgu_norm_g', 'out_norm_a', 'out_norm_b', 'w_out',
                'ffn_norm_g', 'w_gate', 'w_up', 'w_down', 'ple_norm_g', 'w_ple_gate', 'w_ple_proj', 'final_norm_g']
SHARDED = ['w_in', 'w_out', 'w_gate', 'w_up', 'w_down', 'w_ple_gate', 'w_ple_proj']
SMALL = ['mix_norm_g', 'sgu_w', 'sgu_b', 'sgu_norm_g', 'out_norm_a', 'out_norm_b', 'ffn_norm_g', 'ple_norm_g',
         'final_norm_g']
SMALL_SIZES = {'mix_norm_g': 1024, 'sgu_w': 65536, 'sgu_b': 512, 'sgu_norm_g': 256, 'out_norm_a': 256,
               'out_norm_b': 768, 'ffn_norm_g': 1024, 'ple_norm_g': 1024, 'final_norm_g': 1024}
SMALL_ROWS = 72


def _params(semantics=None):
    return pltpu.CompilerParams(dimension_semantics=semantics, vmem_limit_bytes=VMEM_LIMIT)


def _full(shape):
    nd = len(shape)
    return pl.BlockSpec(shape, lambda i: (0,) * nd, pipeline_mode=pl.Buffered(1))


def _rows(tm, width):
    return pl.BlockSpec((tm, width), lambda i: (i, 0))


def _rms_stats(x):
    r = lax.rsqrt(jnp.mean(x * x, axis=-1, keepdims=True) + EPS)
    return x * r, r


def _rms_bwd(dn, n, r):
    return r * (dn - n * jnp.mean(dn * n, axis=-1, keepdims=True))


def _dot(a, b):
    return jnp.dot(a, b, preferred_element_type=F32)


def _dot_nt(a, b):
    return lax.dot_general(a, b, (((1,), (1,)), ((), ())), preferred_element_type=F32)


def _dot_tn(a, b):
    return lax.dot_general(a, b, (((0,), (0,)), ((), ())), preferred_element_type=F32)


def _gelu_parts(x):
    c = math.sqrt(2.0 / math.pi)
    t = jnp.tanh(c * (x + 0.044715 * x * x * x))
    return 0.5 * x * (1.0 + t), t


def _gelu_grad(x, t):
    c = math.sqrt(2.0 / math.pi)
    return 0.5 * (1.0 + t) + 0.5 * x * (1.0 - t * t) * c * (1.0 + 3.0 * 0.044715 * x * x)


def _half_masks(dtype):
    lane = lax.broadcasted_iota(jnp.int32, (BLOCK, LANES), 1)
    lo = (lane < HEAD_DIM).astype(F32)
    return lo.astype(dtype), (1.0 - lo).astype(dtype)


def _rope_partner(t):
    lane = lax.broadcasted_iota(jnp.int32, t.shape, 1)
    first_half = (lane % HEAD_DIM) < (HEAD_DIM // 2)
    return jnp.where(first_half, pltpu.roll(t, LANES - HEAD_DIM // 2, 1), pltpu.roll(t, HEAD_DIM // 2, 1))


PAIRS_ABREAST = 2
L_BLOCK = 256
L_GROUP = 16


def _store_l256(scr, out_ref, cols, value):
    tm = value.shape[0]
    scr[...] = value
    for blk in range(tm // L_BLOCK):
        for r in range(L_GROUP):
            lo = blk * L_BLOCK + r * L_GROUP
            piece = scr[pl.ds(blk * L_BLOCK + r, L_GROUP, stride=L_GROUP), :]
            out_ref[lo:lo + L_GROUP, cols] = piece.astype(out_ref.dtype)


def _load_l256(col_refs, tm):
    cols = []
    for ref in col_refs:
        pieces = [ref[pl.ds(blk * L_BLOCK + i, L_GROUP, stride=L_GROUP), :]
                  for blk in range(tm // L_BLOCK) for i in range(L_GROUP)]
        cols.append(jnp.concatenate(pieces, axis=0))
    return jnp.concatenate(cols, axis=1)


def _col_specs(tm, width):
    return [pl.BlockSpec((tm, LANES), lambda i, j=j: (i, j)) for j in range(width // LANES)]


def _in_fwd(x, g_mix, w_in, cos_t, sin_t):
    s = x.shape[0]
    tm = 512

    def body(x_ref, g_ref, w_ref, cos_ref, sin_ref, uv_ref, q_ref, k_ref, v_ref, hn_ref, *scrs):
        n, _ = _rms_stats(x_ref[...])
        hn = (n * g_ref[...]).astype(MXU_DTYPE)
        hn_ref[...] = hn
        cos = cos_ref[...]
        sin = sin_ref[...]
        strip = 2 * LANES
        for j in range(IN_COLS // strip):
            proj = _dot(hn, w_ref[:, j * strip:(j + 1) * strip])
            lo = j * strip - 2 * WIDTH_A
            if lo < 0:
                uv_ref[:, j * strip:(j + 1) * strip] = proj
                continue
            which, lo = divmod(lo, WIDTH_B)
            for i in range(strip // LANES):
                t = proj[:, i * LANES:(i + 1) * LANES]
                cols = slice(lo + i * LANES, lo + (i + 1) * LANES)
                scr = scrs[i]
                if which == 0:
                    _store_l256(scr, q_ref, cols, (t * cos + _rope_partner(t) * sin) * (HEAD_DIM ** -0.5))
                elif which == 1:
                    _store_l256(scr, k_ref, cols, t * cos + _rope_partner(t) * sin)
                else:
                    _store_l256(scr, v_ref, cols, t)

    return pl.pallas_call(
        body, name="in_fwd", grid=(s // tm,), scratch_shapes=[pltpu.VMEM((tm, LANES), F32)] * 2,
        in_specs=[_rows(tm, D_MODEL), _full((1, D_MODEL)), _full((D_MODEL, IN_COLS)), _rows(tm, LANES),
                  _rows(tm, LANES)],
        out_specs=[_rows(tm, 2 * WIDTH_A), _rows(tm, WIDTH_B), _rows(tm, WIDTH_B), _rows(tm, WIDTH_B),
                   _rows(tm, D_MODEL)],
        out_shape=[jax.ShapeDtypeStruct((s, 2 * WIDTH_A), F32), jax.ShapeDtypeStruct((s, WIDTH_B), MXU_DTYPE),
                   jax.ShapeDtypeStruct((s, WIDTH_B), MXU_DTYPE), jax.ShapeDtypeStruct((s, WIDTH_B), MXU_DTYPE),
                   jax.ShapeDtypeStruct((s, D_MODEL), MXU_DTYPE)],
        compiler_params=_params(("arbitrary",)),
    )(x, g_mix, w_in, cos_t, sin_t)


class _Branch:
    def __init__(self, dil, s, qn=BLOCK):
        self.dil = dil
        i = np.arange(L_GROUP)
        if dil == 16:
            nblk = qn // 16
            self.grid = (16, s // (L_BLOCK * nblk))
            self.shape = (nblk, 1, 1, L_GROUP)
            self.index = lambda r, n: (n, r // 4, r % 4, 0, 0)
            pos = (np.arange(nblk)[:, None] * 16 + i[None, :]).reshape(-1)
        elif dil == 4:
            nblk = qn // 64
            self.grid = (4, s // (L_BLOCK * nblk))
            self.shape = (nblk, 4, 1, L_GROUP)
            self.index = lambda r, n: (n, 0, r, 0, 0)
            pos = (np.arange(nblk)[:, None, None] * 64 + np.arange(4)[None, :, None]
                   + 4 * i[None, None, :]).reshape(-1)
        else:
            self.grid = (1, s // L_BLOCK)
            self.shape = (1, 4, 4, L_GROUP)
            self.index = lambda r, n: (n, 0, 0, 0, 0)
            pos = (np.arange(16)[:, None] + 16 * i[None, :]).reshape(-1)
        self.qn = pos.shape[0]
        self.nb = self.grid[1]
        dist = pos[:, None] - np.concatenate([pos - self.qn, pos])[None, :]
        band = (dist >= 0) & (dist <= BLOCK)
        start = band & (np.arange(2 * self.qn)[None, :] >= self.qn)
        self.bias = np.where(np.stack([band, start]), 0.0, -np.inf).astype(np.float32)

    def view(self, a):
        return a.reshape(a.shape[0] // L_BLOCK, 4, 4, L_GROUP, a.shape[1])

    def spec(self, w, step=lambda n: n):
        return pl.BlockSpec(self.shape + (w,), lambda r, n: self.index(r, step(n)))

    def bias_spec(self, step=lambda n: n):
        return pl.BlockSpec((1, self.qn, 2 * self.qn), lambda r, n: (jnp.where(step(n) == 0, 1, 0), 0, 0))

    def load(self, ref, cols=slice(None)):
        x = ref[:, :, :, :, cols]
        return x.reshape(self.qn, x.shape[-1])

    def store(self, ref, cols, value):
        ref[:, :, :, :, cols] = value.reshape(self.shape + (value.shape[-1],))


def _attn_fwd_branch(q, k, v, dil):
    s = q.shape[0]
    br = _Branch(dil, s)
    qn = br.qn

    def body(bias_ref, q_ref, kc_ref, kp_ref, vc_ref, vp_ref, o_ref, lse_ref):
        bias2 = jnp.concatenate([bias_ref[0], bias_ref[0]], axis=0)
        lo = lax.broadcasted_iota(jnp.int32, (qn, LANES), 1) < HEAD_DIM
        mask_lo = lo.astype(F32).astype(MXU_DTYPE)
        for hp in range(HEADS_B // 2):
            cols = slice(hp * LANES, (hp + 1) * LANES)
            qp = br.load(q_ref, cols)
            kcat = jnp.concatenate([br.load(kp_ref, cols), br.load(kc_ref, cols)], axis=0)
            vcat = jnp.concatenate([br.load(vp_ref, cols), br.load(vc_ref, cols)], axis=0)
            sc = _dot_nt(jnp.concatenate([qp * mask_lo, qp * (1 - mask_lo)], axis=0), kcat) + bias2
            m = jnp.max(sc, axis=1, keepdims=True)
            p = jnp.exp(sc - m)
            l = jnp.sum(p, axis=1, keepdims=True)
            out = _dot(p.astype(MXU_DTYPE), vcat) / l
            lse = m + jnp.log(l)
            br.store(o_ref, cols, jnp.where(lo, out[:qn], out[qn:]))
            br.store(lse_ref, cols, jnp.where(lo, lse[:qn], lse[qn:]))

    before = lambda n: jnp.maximum(n - 1, 0)
    res = pl.pallas_call(
        body, name="attn_fwd_d%d" % dil, grid=br.grid,
        in_specs=[br.bias_spec(), br.spec(WIDTH_B), br.spec(WIDTH_B), br.spec(WIDTH_B, before), br.spec(WIDTH_B),
                  br.spec(WIDTH_B, before)],
        out_specs=[br.spec(WIDTH_B), br.spec(WIDTH_B)],
        out_shape=[jax.ShapeDtypeStruct((s // L_BLOCK, 4, 4, L_GROUP, WIDTH_B), F32)] * 2,
        compiler_params=_params(("arbitrary", "arbitrary")),
    )(jnp.asarray(br.bias), br.view(q), br.view(k), br.view(k), br.view(v), br.view(v))
    return tuple(a.reshape(s, WIDTH_B) for a in res)


def _attn_merge(outs, lses):
    s = outs[0].shape[0]
    tm = 512
    nbr = len(outs)

    def body(*refs):
        o_refs, l_refs, y_ref, lse_ref = refs[:nbr], refs[nbr:2 * nbr], refs[2 * nbr], refs[2 * nbr + 1]
        ls = [r[...] for r in l_refs]
        top = functools.reduce(jnp.maximum, ls)
        ws = [jnp.exp(l - top) for l in ls]
        den = functools.reduce(jnp.add, ws)
        num = functools.reduce(jnp.add, [w * r[...] for w, r in zip(ws, o_refs)])
        y_ref[...] = num / den
        lse_ref[...] = top + jnp.log(den)

    return pl.pallas_call(
        body, name="attn_merge", grid=(s // tm,), in_specs=[_rows(tm, WIDTH_B)] * (2 * nbr),
        out_specs=[_rows(tm, WIDTH_B)] * 2, out_shape=[jax.ShapeDtypeStruct((s, WIDTH_B), F32)] * 2,
        compiler_params=_params(("arbitrary",)),
    )(*outs, *lses)


def _sgu_forward_tile(uv, w_ref, bias, g_sgu):
    tm = uv.shape[0]
    u = uv[:, :WIDTH_A]
    v = uv[:, WIDTH_A:]
    ug, tu = _gelu_parts(u)
    vg, tv = _gelu_parts(v)
    mu = jnp.mean(vg, axis=-1, keepdims=True)
    vc = vg - mu
    rs = lax.rsqrt(jnp.mean(vc * vc, axis=-1, keepdims=True) + EPS)
    vhat = vc * rs
    vn = (vhat * g_sgu).astype(MXU_DTYPE)
    masks = _half_masks(MXU_DTYPE)
    chunks = []
    for c in range(tm // CHUNK):
        rows = slice(c * CHUNK, (c + 1) * CHUNK)
        groups = []
        for gp in range(2):
            vn_g = vn[rows, gp * LANES:(gp + 1) * LANES]
            groups.append(_dot(w_ref[2 * gp], vn_g * masks[0]) + _dot(w_ref[2 * gp + 1], vn_g * masks[1]))
        chunks.append(jnp.concatenate(groups, axis=1) + bias)
    mixed = jnp.concatenate(chunks, axis=0)
    return dict(u=u, v=v, ug=ug, tu=tu, tv=tv, rs=rs, vhat=vhat, vn=vn, mixed=mixed, ya=ug * mixed)


def _sgu_fwd(uv, w_tril, bias, g_sgu, g_a):
    s = uv.shape[0]
    tm = 512

    def body(uv_ref, w_ref, b_ref, gs_ref, ga_ref, o_ref):
        t = _sgu_forward_tile(uv_ref[...], w_ref, b_ref[...], gs_ref[...])
        n, _ = _rms_stats(t['ya'])
        o_ref[...] = (n * ga_ref[...]).astype(MXU_DTYPE)

    return pl.pallas_call(
        body, name="sgu_fwd", grid=(s // tm,),
        in_specs=[_rows(tm, 2 * WIDTH_A), _full((HEADS_A, CHUNK, CHUNK)), _full((CHUNK, WIDTH_A)),
                  _full((1, WIDTH_A)), _full((1, WIDTH_A))],
        out_specs=_rows(tm, WIDTH_A), out_shape=jax.ShapeDtypeStruct((s, WIDTH_A), MXU_DTYPE),
        compiler_params=_params(("arbitrary",)),
    )(uv, w_tril, bias, g_sgu, g_a)


def _out_fwd(ya_n, y_b, g_b, w_out, x):
    s = x.shape[0]
    tm = 512
    nc = WIDTH_B // LANES

    def body(ya_ref, *refs):
        yb_refs = refs[:nc]
        g_ref, w_ref, x_ref, h_ref, yn_ref = refs[nc:]
        n, _ = _rms_stats(_load_l256(yb_refs, tm))
        yn = jnp.concatenate([ya_ref[...], (n * g_ref[...]).astype(MXU_DTYPE)], axis=1)
        yn_ref[...] = yn
        h_ref[...] = x_ref[...] + _dot(yn, w_ref[...])

    return pl.pallas_call(
        body, name="out_fwd", grid=(s // tm,),
        in_specs=[_rows(tm, WIDTH_A)] + _col_specs(tm, WIDTH_B) + [_full((1, WIDTH_B)), _full((D_MODEL, D_MODEL)),
                                                                 _rows(tm, D_MODEL)],
        out_specs=[_rows(tm, D_MODEL), _rows(tm, D_MODEL)],
        out_shape=[jax.ShapeDtypeStruct((s, D_MODEL), F32), jax.ShapeDtypeStruct((s, D_MODEL), MXU_DTYPE)],
        compiler_params=_params(("arbitrary",)),
    )(ya_n, *([y_b] * nc), g_b, w_out, x)


def _ffn_fwd(h1, g_ffn, w_gate, w_up, w_down):
    s = h1.shape[0]
    tm = 512

    def body(h_ref, g_ref, wg_ref, wu_ref, wd_ref, o_ref, gate_ref, up_ref, hn_ref):
        h = h_ref[...]
        n, _ = _rms_stats(h)
        hn = (n * g_ref[...]).astype(MXU_DTYPE)
        hn_ref[...] = hn
        out = h
        for c in range(D_FF // FF_STRIP):
            cols = slice(c * FF_STRIP, (c + 1) * FF_STRIP)
            gate = _dot(hn, wg_ref[:, cols])
            up = _dot(hn, wu_ref[:, cols])
            gate_ref[:, cols] = gate.astype(MXU_DTYPE)
            up_ref[:, cols] = up.astype(MXU_DTYPE)
            act = (gate * jax.nn.sigmoid(gate) * up).astype(MXU_DTYPE)
            out = out + _dot(act, wd_ref[cols, :])
        o_ref[...] = out

    return pl.pallas_call(
        body, name="ffn_fwd", grid=(s // tm,),
        in_specs=[_rows(tm, D_MODEL), _full((1, D_MODEL)), _full((D_MODEL, D_FF)), _full((D_MODEL, D_FF)),
                  _full((D_FF, D_MODEL))],
        out_specs=[_rows(tm, D_MODEL), _rows(tm, D_FF), _rows(tm, D_FF), _rows(tm, D_MODEL)],
        out_shape=[jax.ShapeDtypeStruct((s, D_MODEL), F32), jax.ShapeDtypeStruct((s, D_FF), MXU_DTYPE),
                   jax.ShapeDtypeStruct((s, D_FF), MXU_DTYPE), jax.ShapeDtypeStruct((s, D_MODEL), MXU_DTYPE)],
        compiler_params=_params(("arbitrary",)),
    )(h1, g_ffn, w_gate, w_up, w_down)


def _ple_loss(h2, p, target, g_ple, w_pg, w_pg_t, w_pp, g_final):
    s = h2.shape[0]
    tm = 256

    def body(h_ref, p_ref, t_ref, gp_ref, wg_ref, wgt_ref, wp_ref, gf_ref,
             loss_ref, dh_ref, dz_ref, dpp_ref, hn_ref, dgp_ref, dgf_ref):
        @pl.when(pl.program_id(0) == 0)
        def _():
            loss_ref[...] = jnp.zeros_like(loss_ref)
            dgp_ref[...] = jnp.zeros_like(dgp_ref)
            dgf_ref[...] = jnp.zeros_like(dgf_ref)

        h2t = h_ref[...]
        n2, r2 = _rms_stats(h2t)
        hn = (n2 * gp_ref[...]).astype(MXU_DTYPE)
        hn_ref[...] = hn
        gate = jax.nn.sigmoid(_dot(hn, wg_ref[...]))
        pp = _dot(p_ref[...].astype(MXU_DTYPE), wp_ref[...])
        h3 = h2t + gate * pp
        n3, r3 = _rms_stats(h3)
        diff = n3 * gf_ref[...] - t_ref[...]
        loss_ref[...] += jnp.full(loss_ref.shape, 0.5 * jnp.sum(diff * diff) / D_MODEL, F32)
        dy = diff * (1.0 / D_MODEL)
        dgf_ref[...] += jnp.sum(dy * n3, axis=0, keepdims=True)
        dh3 = _rms_bwd(dy * gf_ref[...], n3, r3)
        dpp_ref[...] = (dh3 * gate).astype(MXU_DTYPE)
        dz = (dh3 * pp * gate * (1.0 - gate)).astype(MXU_DTYPE)
        dz_ref[...] = dz
        dhn = _dot(dz, wgt_ref[...])
        dgp_ref[...] += jnp.sum(dhn * n2, axis=0, keepdims=True)
        dh_ref[...] = dh3 + _rms_bwd(dhn * gp_ref[...], n2, r2)

    return pl.pallas_call(
        body, name="ple_loss", grid=(s // tm,),
        in_specs=[_rows(tm, D_MODEL), _rows(tm, PLE_DIM), _rows(tm, D_MODEL), _full((1, D_MODEL)),
                  _full((D_MODEL, D_MODEL)), _full((D_MODEL, D_MODEL)), _full((PLE_DIM, D_MODEL)),
                  _full((1, D_MODEL))],
        out_specs=[_full((1, LANES)), _rows(tm, D_MODEL), _rows(tm, D_MODEL), _rows(tm, D_MODEL),
                   _rows(tm, D_MODEL), _full((1, D_MODEL)), _full((1, D_MODEL))],
        out_shape=[jax.ShapeDtypeStruct((1, LANES), F32), jax.ShapeDtypeStruct((s, D_MODEL), F32),
                   jax.ShapeDtypeStruct((s, D_MODEL), MXU_DTYPE), jax.ShapeDtypeStruct((s, D_MODEL), MXU_DTYPE),
                   jax.ShapeDtypeStruct((s, D_MODEL), MXU_DTYPE), jax.ShapeDtypeStruct((1, D_MODEL), F32),
                   jax.ShapeDtypeStruct((1, D_MODEL), F32)],
        compiler_params=_params(("arbitrary",)),
    )(h2, p, target, g_ple, w_pg, w_pg_t, w_pp, g_final)


def _ffn_bwd(dh2, h1, gate, up, g_ffn, w_down_t, w_gate_t, w_up_t):
    s = h1.shape[0]
    tm = 256

    def body(dh_ref, h_ref, gate_ref, up_ref, g_ref, wdt_ref, wgt_ref, wut_ref,
             o_ref, act_ref, dg_ref, du_ref, dgn_ref):
        @pl.when(pl.program_id(0) == 0)
        def _():
            dgn_ref[...] = jnp.zeros_like(dgn_ref)

        dh = dh_ref[...]
        dhb = dh.astype(MXU_DTYPE)
        dhn = jnp.zeros((tm, D_MODEL), F32)
        for c in range(D_FF // FF_STRIP):
            cols = slice(c * FF_STRIP, (c + 1) * FF_STRIP)
            dact = _dot(dhb, wdt_ref[:, cols])
            g = gate_ref[:, cols].astype(F32)
            u = up_ref[:, cols].astype(F32)
            sg = jax.nn.sigmoid(g)
            silu = g * sg
            act_ref[:, cols] = (silu * u).astype(MXU_DTYPE)
            du = (dact * silu).astype(MXU_DTYPE)
            dg = (dact * u * sg * (1.0 + g * (1.0 - sg))).astype(MXU_DTYPE)
            du_ref[:, cols] = du
            dg_ref[:, cols] = dg
            dhn = dhn + _dot(dg, wgt_ref[cols, :]) + _dot(du, wut_ref[cols, :])
        n, r = _rms_stats(h_ref[...])
        dgn_ref[...] += jnp.sum(dhn * n, axis=0, keepdims=True)
        o_ref[...] = dh + _rms_bwd(dhn * g_ref[...], n, r)

    return pl.pallas_call(
        body, name="ffn_bwd", grid=(s // tm,),
        in_specs=[_rows(tm, D_MODEL), _rows(tm, D_MODEL), _rows(tm, D_FF), _rows(tm, D_FF), _full((1, D_MODEL)),
                  _full((D_MODEL, D_FF)), _full((D_FF, D_MODEL)), _full((D_FF, D_MODEL))],
        out_specs=[_rows(tm, D_MODEL), _rows(tm, D_FF), _rows(tm, D_FF), _rows(tm, D_FF), _full((1, D_MODEL))],
        out_shape=[jax.ShapeDtypeStruct((s, D_MODEL), F32), jax.ShapeDtypeStruct((s, D_FF), MXU_DTYPE),
                   jax.ShapeDtypeStruct((s, D_FF), MXU_DTYPE), jax.ShapeDtypeStruct((s, D_FF), MXU_DTYPE),
                   jax.ShapeDtypeStruct((1, D_MODEL), F32)],
        compiler_params=_params(("arbitrary",)),
    )(dh2, h1, gate, up, g_ffn, w_down_t, w_gate_t, w_up_t)


def _out_bwd(dh1, y_b, g_b, w_out_t):
    s = dh1.shape[0]
    tm = 512
    nc = WIDTH_B // LANES

    def body(dh_ref, *refs):
        yb_refs = refs[:nc]
        g_ref, wt_ref, dya_ref, dyb_ref, dg_ref, scr = refs[nc:]

        @pl.when(pl.program_id(0) == 0)
        def _():
            dg_ref[...] = jnp.zeros_like(dg_ref)

        dy = _dot(dh_ref[...].astype(MXU_DTYPE), wt_ref[...])
        dya_ref[...] = dy[:, :WIDTH_A]
        dyb = dy[:, WIDTH_A:]
        n, r = _rms_stats(_load_l256(yb_refs, tm))
        dg_ref[...] += jnp.sum(dyb * n, axis=0, keepdims=True)
        dyb_in = _rms_bwd(dyb * g_ref[...], n, r)
        for j in range(nc):
            cols = slice(j * LANES, (j + 1) * LANES)
            _store_l256(scr, dyb_ref, cols, dyb_in[:, cols])

    return pl.pallas_call(
        body, name="out_bwd", grid=(s // tm,), scratch_shapes=[pltpu.VMEM((tm, LANES), F32)],
        in_specs=[_rows(tm, D_MODEL)] + _col_specs(tm, WIDTH_B) + [_full((1, WIDTH_B)), _full((D_MODEL, D_MODEL))],
        out_specs=[_rows(tm, WIDTH_A), _rows(tm, WIDTH_B), _full((1, WIDTH_B))],
        out_shape=[jax.ShapeDtypeStruct((s, WIDTH_A), F32), jax.ShapeDtypeStruct((s, WIDTH_B), F32),
                   jax.ShapeDtypeStruct((1, WIDTH_B), F32)],
        compiler_params=_params(("arbitrary",)),
    )(dh1, *([y_b] * nc), g_b, w_out_t)


def _attn_bwd_branch(q, k, v, do, o, lse, grads, dil):
    s = q.shape[0]
    br = _Branch(dil, s)
    qn, nb = br.qn, br.nb
    first = grads is None

    def body(*refs):
        bias_ref, q_ref, kc_ref, kp_ref, vc_ref, vp_ref, do_ref, o_ref, lse_ref = refs[:9]
        if first:
            rest = refs[9:]
        else:
            dq_in, dk_in, dv_in = refs[9:12]
            rest = refs[12:]
        dq_ref, dk_ref, dv_ref, dk_carry, dv_carry = rest
        n = pl.program_id(1)

        @pl.when(n == 0)
        def _():
            dk_carry[...] = jnp.zeros_like(dk_carry)
            dv_carry[...] = jnp.zeros_like(dv_carry)

        @pl.when(n < nb)
        def _():
            bias2 = jnp.concatenate([bias_ref[0], bias_ref[0]], axis=0)
            lane = lax.broadcasted_iota(jnp.int32, (qn, LANES), 1)
            lo = lane < HEAD_DIM
            mask_f = lo.astype(F32)
            mask_lo = mask_f.astype(MXU_DTYPE)
            def prepare(hp):
                cols = slice(hp * LANES, (hp + 1) * LANES)
                qp = br.load(q_ref, cols)
                dop = br.load(do_ref, cols)
                prod = dop * br.load(o_ref, cols)
                prod_lo = prod * mask_f
                lse = br.load(lse_ref, cols)
                return dict(
                    cols=cols,
                    kcat=jnp.concatenate([br.load(kp_ref, cols), br.load(kc_ref, cols)], axis=0),
                    vcat=jnp.concatenate([br.load(vp_ref, cols), br.load(vc_ref, cols)], axis=0),
                    qs=jnp.concatenate([qp * mask_lo, qp * (1 - mask_lo)], axis=0),
                    dos=jnp.concatenate([dop * mask_f, dop * (1.0 - mask_f)], axis=0).astype(MXU_DTYPE),
                    delta=jnp.concatenate([jnp.sum(prod_lo, axis=1, keepdims=True),
                                           jnp.sum(prod - prod_lo, axis=1, keepdims=True)], axis=0),
                    lse2=jnp.concatenate([lse[:, :1], lse[:, HEAD_DIM:HEAD_DIM + 1]], axis=0))

            def scores(t):
                t['sc'] = _dot_nt(t['qs'], t['kcat'])
                t['dp'] = _dot_nt(t['dos'], t['vcat'])

            def softmax(t):
                p = jnp.exp(t['sc'] + bias2 - t['lse2'])
                t['ds'] = (p * (t['dp'] - t['delta'])).astype(MXU_DTYPE)
                t['p'] = p.astype(MXU_DTYPE)

            def gradients(t):
                t['dvc'] = _dot_tn(t['p'], t['dos'])
                t['dkc'] = _dot_tn(t['ds'], t['qs'])
                t['dq2'] = _dot(t['ds'], t['kcat'])

            def store(t):
                cols, dkc, dvc = t['cols'], t['dkc'], t['dvc']
                dq = jnp.where(lo, t['dq2'][:qn], t['dq2'][qn:])
                dk_prev = dk_carry[:, cols] + dkc[:qn]
                dv_prev = dv_carry[:, cols] + dvc[:qn]
                if not first:
                    dq = dq + br.load(dq_in, cols)
                    dk_prev = dk_prev + br.load(dk_in, cols)
                    dv_prev = dv_prev + br.load(dv_in, cols)
                br.store(dq_ref, cols, dq)
                br.store(dk_ref, cols, dk_prev)
                br.store(dv_ref, cols, dv_prev)
                dk_carry[:, cols] = dkc[qn:]
                dv_carry[:, cols] = dvc[qn:]

            for first_pair in range(0, HEADS_B // 2, PAIRS_ABREAST):
                group = [prepare(hp) for hp in range(first_pair, first_pair + PAIRS_ABREAST)]
                for stage in (scores, softmax, gradients, store):
                    for t in group:
                        stage(t)

        @pl.when(n == nb)
        def _():
            dk_last = dk_carry[...]
            dv_last = dv_carry[...]
            if not first:
                dk_last = dk_last + br.load(dk_in)
                dv_last = dv_last + br.load(dv_in)
            br.store(dk_ref, slice(None), dk_last)
            br.store(dv_ref, slice(None), dv_last)

    cur = lambda n: jnp.minimum(n, nb - 1)
    before = lambda n: jnp.maximum(cur(n) - 1, 0)
    late = lambda n: jnp.maximum(n - 1, 0)
    in_specs = [br.bias_spec(cur), br.spec(WIDTH_B, cur), br.spec(WIDTH_B, cur), br.spec(WIDTH_B, before),
                br.spec(WIDTH_B, cur), br.spec(WIDTH_B, before), br.spec(WIDTH_B, cur), br.spec(WIDTH_B, cur),
                br.spec(WIDTH_B, cur)]
    args = [jnp.asarray(br.bias)] + [br.view(a) for a in (q, k, k, v, v, do, o, lse)]
    if not first:
        in_specs += [br.spec(WIDTH_B, cur), br.spec(WIDTH_B, late), br.spec(WIDTH_B, late)]
        args += [br.view(g) for g in grads]
    res = pl.pallas_call(
        body, name="attn_bwd_d%d" % dil, grid=(br.grid[0], nb + 1), in_specs=in_specs,
        out_specs=[br.spec(WIDTH_B, cur), br.spec(WIDTH_B, late), br.spec(WIDTH_B, late)],
        out_shape=[jax.ShapeDtypeStruct((s // L_BLOCK, 4, 4, L_GROUP, WIDTH_B), F32)] * 3,
        scratch_shapes=[pltpu.VMEM((qn, WIDTH_B), F32), pltpu.VMEM((qn, WIDTH_B), F32)],
        compiler_params=_params(("arbitrary", "arbitrary")),
    )(*args)
    return tuple(a.reshape(s, WIDTH_B) for a in res)


def _sgu_bwd(uv, dya_n, w_tril, w_tril_t, bias, g_sgu, g_a):
    s = uv.shape[0]
    tm = 512

    def body(uv_ref, dy_ref, w_ref, wt_ref, b_ref, gs_ref, ga_ref, duv_ref, dw_ref, db_ref, dgs_ref, dga_ref,
             db_acc):
        i = pl.program_id(0)

        @pl.when(i == 0)
        def _():
            dw_ref[...] = jnp.zeros_like(dw_ref)
            dgs_ref[...] = jnp.zeros_like(dgs_ref)
            dga_ref[...] = jnp.zeros_like(dga_ref)
            db_acc[...] = jnp.zeros_like(db_acc)

        t = _sgu_forward_tile(uv_ref[...], w_ref, b_ref[...], gs_ref[...])
        na, ra = _rms_stats(t['ya'])
        dyn = dy_ref[...]
        dga_ref[...] += jnp.sum(dyn * na, axis=0, keepdims=True)
        dya = _rms_bwd(dyn * ga_ref[...], na, ra)
        dug = dya * t['mixed']
        dmixed = dya * t['ug']
        dmb = dmixed.astype(MXU_DTYPE)
        masks = _half_masks(MXU_DTYPE)
        chunks = []
        db = jnp.zeros((CHUNK, WIDTH_A), F32)
        for c in range(tm // CHUNK):
            rows = slice(c * CHUNK, (c + 1) * CHUNK)
            db = db + dmixed[rows]
            groups = []
            for gp in range(2):
                cols = slice(gp * LANES, (gp + 1) * LANES)
                dm_g = dmb[rows, cols]
                vn_g = t['vn'][rows, cols]
                dvn_g = jnp.zeros((CHUNK, LANES), F32)
                for j in range(2):
                    dm_h = dm_g * masks[j]
                    dvn_g = dvn_g + _dot(wt_ref[2 * gp + j], dm_h)
                    dw_ref[2 * gp + j] += _dot_nt(dm_h, vn_g)
                groups.append(dvn_g)
            chunks.append(jnp.concatenate(groups, axis=1))
        db_acc[...] += db
        dvn = jnp.concatenate(chunks, axis=0)
        vhat = t['vhat']
        dgs_ref[...] += jnp.sum(dvn * vhat, axis=0, keepdims=True)
        dvh = dvn * gs_ref[...]
        dvg = t['rs'] * (dvh - jnp.mean(dvh, axis=-1, keepdims=True)
                         - vhat * jnp.mean(dvh * vhat, axis=-1, keepdims=True))
        duv_ref[:, :WIDTH_A] = (dug * _gelu_grad(t['u'], t['tu'])).astype(MXU_DTYPE)
        duv_ref[:, WIDTH_A:] = (dvg * _gelu_grad(t['v'], t['tv'])).astype(MXU_DTYPE)

        @pl.when(i == pl.num_programs(0) - 1)
        def _():
            lane_a = lax.broadcasted_iota(jnp.int32, (CHUNK, WIDTH_A), 1)
            lane = lax.broadcasted_iota(jnp.int32, (CHUNK, LANES), 1)
            acc = db_acc[...]
            out = jnp.zeros((CHUNK, LANES), F32)
            for h in range(HEADS_A):
                col = jnp.sum(jnp.where(lane_a // HEAD_DIM == h, acc, 0.0), axis=1, keepdims=True)
                out = jnp.where(lane == h, col, out)
            db_ref[...] = out
            causal = (lax.broadcasted_iota(jnp.int32, (CHUNK, CHUNK), 0)
                      >= lax.broadcasted_iota(jnp.int32, (CHUNK, CHUNK), 1))
            for h in range(HEADS_A):
                dw_ref[h] = jnp.where(causal, dw_ref[h], 0.0)

    return pl.pallas_call(
        body, name="sgu_bwd", grid=(s // tm,),
        in_specs=[_rows(tm, 2 * WIDTH_A), _rows(tm, WIDTH_A), _full((HEADS_A, CHUNK, CHUNK)),
                  _full((HEADS_A, CHUNK, CHUNK)), _full((CHUNK, WIDTH_A)), _full((1, WIDTH_A)),
                  _full((1, WIDTH_A))],
        out_specs=[_rows(tm, 2 * WIDTH_A), _full((HEADS_A, CHUNK, CHUNK)), _full((CHUNK, LANES)),
                   _full((1, WIDTH_A)), _full((1, WIDTH_A))],
        out_shape=[jax.ShapeDtypeStruct((s, 2 * WIDTH_A), MXU_DTYPE),
                   jax.ShapeDtypeStruct((HEADS_A, CHUNK, CHUNK), F32), jax.ShapeDtypeStruct((CHUNK, LANES), F32),
                   jax.ShapeDtypeStruct((1, WIDTH_A), F32), jax.ShapeDtypeStruct((1, WIDTH_A), F32)],
        scratch_shapes=[pltpu.VMEM((CHUNK, WIDTH_A), F32)],
        compiler_params=_params(("arbitrary",)),
    )(uv, dya_n, w_tril, w_tril_t, bias, g_sgu, g_a)


def _in_bwd_proj(duv, dq, dk, dv, cos_t, sin_t):
    s = duv.shape[0]
    tm = 512
    nc = WIDTH_B // LANES

    def body(duv_ref, *refs):
        dq_refs, dk_refs, dv_refs = refs[:nc], refs[nc:2 * nc], refs[2 * nc:3 * nc]
        cos_ref, sin_ref, dp_ref = refs[3 * nc:]
        cos = cos_ref[...]
        sin = sin_ref[...]
        dp_ref[:, :2 * WIDTH_A] = duv_ref[...]
        for i in range(nc):
            lo = 2 * WIDTH_A + i * LANES
            tq = _load_l256(dq_refs[i:i + 1], tm) * (HEAD_DIM ** -0.5)
            tk = _load_l256(dk_refs[i:i + 1], tm)
            dp_ref[:, lo:lo + LANES] = (tq * cos + _rope_partner(tq * sin)).astype(MXU_DTYPE)
            dp_ref[:, lo + WIDTH_B:lo + WIDTH_B + LANES] = (tk * cos + _rope_partner(tk * sin)).astype(MXU_DTYPE)
            dp_ref[:, lo + 2 * WIDTH_B:lo + 2 * WIDTH_B + LANES] = _load_l256(dv_refs[i:i + 1], tm).astype(MXU_DTYPE)

    return pl.pallas_call(
        body, name="in_bwd_proj", grid=(s // tm,),
        in_specs=[_rows(tm, 2 * WIDTH_A)] + 3 * _col_specs(tm, WIDTH_B) + [_rows(tm, LANES), _rows(tm, LANES)],
        out_specs=_rows(tm, IN_COLS), out_shape=jax.ShapeDtypeStruct((s, IN_COLS), MXU_DTYPE),
        compiler_params=_params(("arbitrary",)),
    )(duv, *([dq] * nc), *([dk] * nc), *([dv] * nc), cos_t, sin_t)


def _in_bwd_x(dproj, w_in_t, x, g_mix, dh1, after):
    s = x.shape[0]
    tm = 512

    def body(dp_ref, wt_ref, x_ref, g_ref, dh_ref, *rest):
        gx_ref, dg_ref = rest[len(after):]
        @pl.when(pl.program_id(0) == 0)
        def _():
            dg_ref[...] = jnp.zeros_like(dg_ref)

        dhn = _dot(dp_ref[...], wt_ref[...])
        n, r = _rms_stats(x_ref[...])
        dg_ref[...] += jnp.sum(dhn * n, axis=0, keepdims=True)
        gx_ref[...] = dh_ref[...] + _rms_bwd(dhn * g_ref[...], n, r)

    return pl.pallas_call(
        body, name="in_bwd_x", grid=(s // tm,),
        in_specs=[_rows(tm, IN_COLS), _full((IN_COLS, D_MODEL)), _rows(tm, D_MODEL), _full((1, D_MODEL)),
                  _rows(tm, D_MODEL)] + [pl.BlockSpec(memory_space=pl.ANY)] * len(after),
        out_specs=[_rows(tm, D_MODEL), _full((1, D_MODEL))],
        out_shape=[jax.ShapeDtypeStruct((s, D_MODEL), F32), jax.ShapeDtypeStruct((1, D_MODEL), F32)],
        compiler_params=_params(("arbitrary",)),
    )(dproj, w_in_t, x, g_mix, dh1, *after)


def _wgrad(a, b, name):
    s, m = a.shape
    n = b.shape[1]
    bm = 512 if m % 512 == 0 else FF_HALF
    ts = 1024
    nsteps = s // ts

    def body(a_ref, b_ref, o_ref, acc):
        kk = pl.program_id(1)

        @pl.when(kk == 0)
        def _():
            acc[...] = jnp.zeros_like(acc)

        acc[...] += _dot_tn(a_ref[...].astype(MXU_DTYPE), b_ref[...].astype(MXU_DTYPE))

        @pl.when(kk == nsteps - 1)
        def _():
            o_ref[...] = acc[...].astype(o_ref.dtype)

    return pl.pallas_call(
        body, name=name, grid=(m // bm, nsteps),
        in_specs=[pl.BlockSpec((ts, bm), lambda i, kk: (kk, i)), pl.BlockSpec((ts, n), lambda i, kk: (kk, 0))],
        out_specs=pl.BlockSpec((bm, n), lambda i, kk: (i, 0)), out_shape=jax.ShapeDtypeStruct((m, n), jnp.bfloat16),
        scratch_shapes=[pltpu.VMEM((bm, n), F32)],
        compiler_params=_params(("arbitrary", "arbitrary")),
    )(a, b)


def _rope_tables(s):
    half = HEAD_DIM // 2
    inv = ROPE_THETA ** (-jnp.arange(half, dtype=F32) / half)
    ang = jnp.arange(s, dtype=F32)[:, None] * inv[None, :]
    cos = jnp.cos(ang)
    sin = jnp.sin(ang)
    cos_t = jnp.concatenate([cos, cos, cos, cos], axis=1)
    sin_t = jnp.concatenate([-sin, sin, -sin, sin], axis=1)
    return cos_t, sin_t


MESH = pl.DeviceIdType.MESH
ANY = pl.BlockSpec(memory_space=pl.ANY)
SEM = pl.BlockSpec(memory_space=pltpu.SEMAPHORE)
SPLIT_COPY = pltpu.CompilerParams(has_side_effects=pltpu.SideEffectType.DATAFLOW_SIDE_EFFECTING)
SLAB_IS_TRANSPOSED = {'w_in': True, 'w_out': False, 'w_gate': True, 'w_up': True, 'w_down': False,
                      'w_ple_gate': False, 'w_ple_proj': True}


def _place():
    x, y, c = lax.axis_index("x"), lax.axis_index("y"), lax.axis_index("c")
    other_chips = [(1 - x, y), (x, 1 - y), (1 - x, 1 - y)]
    return x, y, c, other_chips


def _chip_of(chip):
    return 2 * chip[0] + chip[1]


def _half(ref, lead, hc):
    hr = ref.shape[1] // 2
    return ref.at[lead, pl.ds(hc * hr, hr), :]


def _put_own(stack, own, index):
    return lax.dynamic_update_slice(stack, own[None], (index,) + (0,) * own.ndim)


def _all_gather_now(slab):
    rows, cols = slab.shape

    def body(x_ref, out_ref, send_sems, recv_sems):
        x, y, c, chips = _place()
        sibling = (x, y, 1 - c)
        hr = rows // 2

        def copy(k, src, dst, to):
            return pltpu.make_async_remote_copy(src_ref=src, dst_ref=dst, send_sem=send_sems.at[k],
                                                recv_sem=recv_sems.at[k], device_id=to, device_id_type=MESH)

        my_half = x_ref.at[pl.ds(c * hr, hr), :]
        first = [copy(j, my_half, _half(out_ref, 2 * x + y, c), (*chip, c)) for j, chip in enumerate(chips)]
        for cp in first:
            cp.start()
        passed = [copy(3 + j, _half(out_ref, _chip_of(chip), c), _half(out_ref, _chip_of(chip), c), sibling)
                  for j, chip in enumerate(chips)]
        for j, chip in enumerate(chips):
            copy(j, my_half, _half(out_ref, _chip_of(chip), c), (*chip, c)).wait_recv()
            passed[j].start()
        for j, chip in enumerate(chips):
            copy(3 + j, my_half, _half(out_ref, _chip_of(chip), 1 - c), sibling).wait_recv()
        for cp in first + passed:
            cp.wait_send()

    gathered = pl.pallas_call(
        body, name="all_gather_now", out_shape=jax.ShapeDtypeStruct((N_CHIPS, rows, cols), slab.dtype),
        in_specs=[ANY], out_specs=ANY,
        scratch_shapes=[pltpu.SemaphoreType.DMA((6,)), pltpu.SemaphoreType.DMA((6,))],
    )(slab)
    me = 2 * lax.axis_index("x") + lax.axis_index("y")
    return _put_own(gathered, slab, me).reshape(N_CHIPS * rows, cols)


def _gather_copies(slab_refs, land_refs, send_sems, recv_sems):
    x, y, c, chips = _place()
    sends, recvs = [], []
    for k, (src, land) in enumerate(zip(slab_refs, land_refs)):
        hr = src.shape[0] // 2
        for j, chip in enumerate(chips):
            for t in range(2):
                sends.append(pltpu.make_async_remote_copy(
                    src_ref=src.at[pl.ds(c * hr, hr), :], dst_ref=_half(land, 2 * x + y, c),
                    send_sem=send_sems.at[6 * k + 2 * j + t], recv_sem=recv_sems.at[6 * k + 2 * j + c],
                    device_id=(*chip, t), device_id_type=MESH))
                recvs.append(pltpu.make_async_remote_copy(
                    src_ref=src.at[pl.ds(t * hr, hr), :], dst_ref=_half(land, _chip_of(chip), t),
                    send_sem=send_sems.at[6 * k + 2 * j + t], recv_sem=recv_sems.at[6 * k + 2 * j + t],
                    device_id=(*chip, t), device_id_type=MESH))
    return sends, recvs


def _all_gather_start(slabs, after):
    n = len(slabs)

    def body(*refs):
        slab_refs, land_refs = refs[:n], refs[n:2 * n]
        send_sems, recv_sems = refs[2 * n + 1:2 * n + 3]
        token = refs[-1]
        sends, _ = _gather_copies(slab_refs, land_refs, send_sems, recv_sems)
        for cp in sends:
            cp.start()
        token[...] = jnp.zeros_like(token)

    lands = [lax.empty((N_CHIPS,) + s.shape, s.dtype) for s in slabs]
    hbm = lambda a: pltpu.HBM(a.shape, a.dtype)
    res = pl.pallas_call(
        body, name="all_gather_start",
        out_shape=(pltpu.SemaphoreType.DMA((6 * n,)), pltpu.SemaphoreType.DMA((6 * n,)), *map(hbm, slabs),
                   *map(hbm, lands), jax.ShapeDtypeStruct((8, LANES), F32)),
        in_specs=[ANY] * (2 * n + 1),
        out_specs=(SEM, SEM, *([ANY] * (2 * n)), pl.BlockSpec(memory_space=pltpu.VMEM)),
        input_output_aliases={i: 2 + i for i in range(2 * n)}, compiler_params=SPLIT_COPY,
    )(*[pltpu.with_memory_space_constraint(a, pltpu.HBM) for a in list(slabs) + lands], after)
    return res[:-1], res[-1]


def _all_gather_wait(handle, after):
    send_sems, recv_sems = handle[:2]
    n = (len(handle) - 2) // 2
    slabs, lands = handle[2:2 + n], handle[2 + n:]

    def body(*refs):
        slab_refs, land_refs = refs[:n], refs[n:2 * n]
        send_sems, recv_sems = refs[2 * n:2 * n + 2]
        sends, recvs = _gather_copies(slab_refs, land_refs, send_sems, recv_sems)
        for cp in sends:
            cp.wait_send()
        for cp in recvs:
            cp.wait_recv()

    hbm = lambda a: pltpu.HBM(a.shape, a.dtype)
    res = pl.pallas_call(
        body, name="all_gather_wait", out_shape=tuple(map(hbm, list(slabs) + list(lands))),
        in_specs=[ANY] * (2 * n) + [SEM, SEM, ANY], out_specs=tuple([ANY] * (2 * n)),
        input_output_aliases={i: i for i in range(2 * n)}, compiler_params=SPLIT_COPY,
    )(*slabs, *lands, send_sems, recv_sems, after)
    me = 2 * lax.axis_index("x") + lax.axis_index("y")
    return [_put_own(land, slab, me).reshape(N_CHIPS * slab.shape[0], slab.shape[1])
            for slab, land in zip(res[:n], res[n:])]


def _scatter_copies(part_refs, land_refs, send_sems, recv_sems):
    x, y, c, chips = _place()
    me = 4 * x + 2 * y + c
    sends, recvs = [], []
    for k, (part, land) in enumerate(zip(part_refs, land_refs)):
        for j, chip in enumerate(chips):
            for t in range(2):
                sends.append(pltpu.make_async_remote_copy(
                    src_ref=part.at[_chip_of(chip)], dst_ref=land.at[me],
                    send_sem=send_sems.at[7 * k + 2 * j + t], recv_sem=recv_sems.at[7 * k + 2 * j + c],
                    device_id=(*chip, t), device_id_type=MESH))
                recvs.append(pltpu.make_async_remote_copy(
                    src_ref=part.at[_chip_of(chip)], dst_ref=land.at[2 * _chip_of(chip) + t],
                    send_sem=send_sems.at[7 * k + 2 * j + t], recv_sem=recv_sems.at[7 * k + 2 * j + t],
                    device_id=(*chip, t), device_id_type=MESH))
        sends.append(pltpu.make_async_remote_copy(
            src_ref=part.at[2 * x + y], dst_ref=land.at[me], send_sem=send_sems.at[7 * k + 6],
            recv_sem=recv_sems.at[7 * k + 6], device_id=(x, y, 1 - c), device_id_type=MESH))
        recvs.append(pltpu.make_async_remote_copy(
            src_ref=part.at[2 * x + y], dst_ref=land.at[4 * x + 2 * y + 1 - c],
            send_sem=send_sems.at[7 * k + 6], recv_sem=recv_sems.at[7 * k + 6], device_id=(x, y, 1 - c),
            device_id_type=MESH))
    return sends, recvs


def _reduce_scatter_start(parts, name):
    n = len(parts)
    parts = [p.reshape(N_CHIPS, p.shape[0] // N_CHIPS, p.shape[1]) for p in parts]

    def body(*refs):
        part_refs, land_refs = refs[:n], refs[n:2 * n]
        send_sems, recv_sems = refs[2 * n:2 * n + 2]
        token = refs[-1]
        sends, _ = _scatter_copies(part_refs, land_refs, send_sems, recv_sems)
        for cp in sends:
            cp.start()
        token[...] = jnp.zeros_like(token)

    lands = [lax.empty((N_DEV, p.shape[1], p.shape[2]), p.dtype) for p in parts]
    hbm = lambda a: pltpu.HBM(a.shape, a.dtype)
    res = pl.pallas_call(
        body, name=name,
        out_shape=(pltpu.SemaphoreType.DMA((7 * n,)), pltpu.SemaphoreType.DMA((7 * n,)), *map(hbm, parts),
                   *map(hbm, lands), jax.ShapeDtypeStruct((8, LANES), F32)),
        in_specs=[ANY] * (2 * n), out_specs=(SEM, SEM, *([ANY] * (2 * n)), pl.BlockSpec(memory_space=pltpu.VMEM)),
        input_output_aliases={i: 2 + i for i in range(2 * n)}, compiler_params=SPLIT_COPY,
    )(*[pltpu.with_memory_space_constraint(a, pltpu.HBM) for a in parts + lands])
    return res[:-1], res[-1]


def _reduce_scatter_wait(handle, after, name):
    send_sems, recv_sems = handle[:2]
    n = (len(handle) - 2) // 2
    parts, lands = handle[2:2 + n], handle[2 + n:]

    def body(*refs):
        part_refs, land_refs = refs[:n], refs[n:2 * n]
        send_sems, recv_sems = refs[2 * n:2 * n + 2]
        sends, recvs = _scatter_copies(part_refs, land_refs, send_sems, recv_sems)
        for cp in sends:
            cp.wait_send()
        for cp in recvs:
            cp.wait_recv()

    hbm = lambda a: pltpu.HBM(a.shape, a.dtype)
    res = pl.pallas_call(
        body, name=name, out_shape=tuple(map(hbm, list(parts) + list(lands))),
        in_specs=[ANY] * (2 * n) + [SEM, SEM, ANY], out_specs=tuple([ANY] * (2 * n)),
        input_output_aliases={i: i for i in range(2 * n)}, compiler_params=SPLIT_COPY,
    )(*parts, *lands, send_sems, recv_sems, after)
    return list(zip(res[:n], res[n:]))


def _small_all_reduce(block):
    rows = block.shape[0]

    def body(x_ref, all_ref, sum_ref, send_sems, recv_sems, local_sem):
        x, y, c, chips = _place()
        me, sibling = (x, y, c), (x, y, 1 - c)

        def blk(px, py, pc):
            return all_ref.at[pl.ds((4 * px + 2 * py + pc) * rows, rows), :]

        def copy(k, who, to, src=None):
            return pltpu.make_async_remote_copy(
                src_ref=blk(*who) if src is None else src, dst_ref=blk(*who), send_sem=send_sems.at[k],
                recv_sem=recv_sems.at[k], device_id=to, device_id_type=MESH)

        mine = pltpu.make_async_copy(x_ref, blk(*me), local_sem)
        mine.start()
        first = [copy(0, me, sibling, src=x_ref)]
        first += [copy(1 + j, me, (*chip, c), src=x_ref) for j, chip in enumerate(chips)]
        for cp in first:
            cp.start()
        passed = [copy(4 + j, (*chip, c), sibling) for j, chip in enumerate(chips)]
        for j, chip in enumerate(chips):
            copy(1 + j, (*chip, c), me).wait_recv()
            passed[j].start()
        copy(0, sibling, me).wait_recv()
        for j, chip in enumerate(chips):
            copy(4 + j, (*chip, 1 - c), me).wait_recv()
        for cp in first + passed:
            cp.wait_send()
        mine.wait()
        acc = all_ref[pl.ds(0, rows), :]
        for dev in range(1, N_DEV):
            acc = acc + all_ref[pl.ds(dev * rows, rows), :]
        sum_ref[...] = acc

    vmem = pl.BlockSpec(memory_space=pltpu.VMEM)
    return pl.pallas_call(
        body, name="small_all_reduce",
        out_shape=[jax.ShapeDtypeStruct((N_DEV * rows, D_MODEL), F32), jax.ShapeDtypeStruct((rows, D_MODEL), F32)],
        in_specs=[vmem], out_specs=[vmem, vmem],
        scratch_shapes=[pltpu.SemaphoreType.DMA((7,)), pltpu.SemaphoreType.DMA((7,)), pltpu.SemaphoreType.DMA],
    )(block)[1]


def _adamw(w, g, m, v, name):
    rows, cols = w.shape
    tm = rows
    if rows > 512:
        tm = next(t for t in range(512, 7, -8) if rows % t == 0)

    def body(w_ref, g_ref, m_ref, v_ref, d_ref, nm_ref, nv_ref):
        g_ = g_ref[...]
        m_ = ADAM_B1 * m_ref[...] + (1.0 - ADAM_B1) * g_
        v_ = ADAM_B2 * v_ref[...] + (1.0 - ADAM_B2) * (g_ * g_)
        m_hat = m_ / (1.0 - ADAM_B1 ** ADAM_STEP)
        v_hat = v_ / (1.0 - ADAM_B2 ** ADAM_STEP)
        d_ref[...] = -ADAM_LR * (m_hat / (jnp.sqrt(v_hat) + ADAM_EPS) + ADAM_WD * w_ref[...])
        nm_ref[...] = m_
        nv_ref[...] = v_

    spec = pl.BlockSpec((tm, cols), lambda i: (i, 0))
    return pl.pallas_call(
        body, name=name, grid=(rows // tm,), in_specs=[spec] * 4, out_specs=[spec] * 3,
        out_shape=[jax.ShapeDtypeStruct(w.shape, F32)] * 3, compiler_params=_params(("arbitrary",)),
    )(w, g, m, v)


def _adamw_of_shares(w, own, land, m, v, name):
    rows, cols = w.shape
    tm = rows // 4
    x, y, c = lax.axis_index("x"), lax.axis_index("y"), lax.axis_index("c")
    where = jnp.stack([2 * x + y, 4 * x + 2 * y + c]).astype(jnp.int32)

    def body(where_ref, w_ref, own_ref, land_ref, m_ref, v_ref, g_ref, d_ref, nm_ref, nv_ref):
        me = where_ref[1]
        g_ = jnp.zeros((tm, cols), F32)
        for dev in range(N_DEV):
            g_ = g_ + jnp.where(me == dev, own_ref[0], land_ref[dev]).astype(F32)
        m_ = ADAM_B1 * m_ref[...] + (1.0 - ADAM_B1) * g_
        v_ = ADAM_B2 * v_ref[...] + (1.0 - ADAM_B2) * (g_ * g_)
        m_hat = m_ / (1.0 - ADAM_B1 ** ADAM_STEP)
        v_hat = v_ / (1.0 - ADAM_B2 ** ADAM_STEP)
        g_ref[...] = g_
        d_ref[...] = -ADAM_LR * (m_hat / (jnp.sqrt(v_hat) + ADAM_EPS) + ADAM_WD * w_ref[...])
        nm_ref[...] = m_
        nv_ref[...] = v_

    tile = pl.BlockSpec((tm, cols), lambda i, where_ref: (i, 0))
    spec = pltpu.PrefetchScalarGridSpec(
        num_scalar_prefetch=1, grid=(rows // tm,),
        in_specs=[tile, pl.BlockSpec((1, tm, cols), lambda i, where_ref: (where_ref[0], i, 0)),
                  pl.BlockSpec((N_DEV, tm, cols), lambda i, where_ref: (0, i, 0)), tile, tile],
        out_specs=[tile] * 4)
    return pl.pallas_call(
        body, name=name, grid_spec=spec, out_shape=[jax.ShapeDtypeStruct(w.shape, F32)] * 4,
        compiler_params=_params(("arbitrary",)),
    )(where, w, own, land, m, v)


def _pack_small(values):
    flat = jnp.concatenate([values[n].reshape(-1).astype(F32) for n in SMALL])
    return jnp.pad(flat, (0, SMALL_ROWS * D_MODEL - flat.shape[0])).reshape(SMALL_ROWS, D_MODEL)


def _unpack_small(block, shapes):
    flat = block.reshape(-1)
    out, lo = {}, 0
    for n in SMALL:
        out[n] = flat[lo:lo + SMALL_SIZES[n]].reshape(shapes[n])
        lo += SMALL_SIZES[n]
    return out


def _after(token, a):
    return a + token[:1, :1].astype(a.dtype)


def kernel(x, p, mix_norm_g, w_in, sgu_w, sgu_b, sgu_norm_g, out_norm_a, out_norm_b, w_out, ffn_norm_g, w_gate, w_up, w_down, ple_norm_g, w_ple_gate, w_ple_proj, final_norm_g, loss_target, m_mix_norm_g, m_w_in, m_sgu_w, m_sgu_b, m_sgu_norm_g, m_out_norm_a, m_out_norm_b, m_w_out, m_ffn_norm_g, m_w_gate, m_w_up, m_w_down, m_ple_norm_g, m_w_ple_gate, m_w_ple_proj, m_final_norm_g, v_mix_norm_g, v_w_in, v_sgu_w, v_sgu_b, v_sgu_norm_g, v_out_norm_a, v_out_norm_b, v_w_out, v_ffn_norm_g, v_w_gate, v_w_up, v_w_down, v_ple_norm_g, v_w_ple_gate, v_w_ple_proj, v_final_norm_g):
    given = dict(locals())
    drop_lead = lambda a, lead: a.reshape(a.shape[lead:])
    xs, ps, target = drop_lead(x, 1), drop_lead(p, 2), drop_lead(loss_target, 1)
    s = xs.shape[0]
    shard = lambda name: drop_lead(given[name], 1)

    def slab_of(name):
        local = shard(name).astype(MXU_DTYPE)
        return local.T if SLAB_IS_TRANSPOSED[name] else local

    w_in_t = _all_gather_now(slab_of('w_in'))
    later = ['w_out', 'w_gate', 'w_up', 'w_down', 'w_ple_gate', 'w_ple_proj']
    gather, token = _all_gather_start([slab_of(n) for n in later], w_in_t)

    cos_t, sin_t = _rope_tables(s)
    tril = jnp.tril(jnp.ones((CHUNK, CHUNK), F32))
    w_tril = (sgu_w.reshape(HEADS_A, CHUNK, CHUNK) * tril).astype(MXU_DTYPE)
    w_tril_t = jnp.swapaxes(w_tril, 1, 2)
    bias = jnp.repeat(sgu_b.reshape(HEADS_A, CHUNK).T, HEAD_DIM, axis=1)
    g = {n: given[n].reshape(1, -1) for n in SMALL if n not in ('sgu_w', 'sgu_b')}

    uv, q, k, v, hn1 = _in_fwd(xs, _after(token, g['mix_norm_g']), w_in_t.T, cos_t, sin_t)
    ya_n = _sgu_fwd(uv, w_tril, bias, g['sgu_norm_g'], g['out_norm_a'])
    branches = [_attn_fwd_branch(q, k, v, dil) for dil in DILATIONS]
    y_b, lse = _attn_merge([o for o, _ in branches], [l for _, l in branches])
    stacks = dict(zip(later, _all_gather_wait(gather, lse)))
    w_gate_t, w_up_t, w_pp_t = stacks['w_gate'], stacks['w_up'], stacks['w_ple_proj']
    h1, y_n = _out_fwd(ya_n, y_b, g['out_norm_b'], stacks['w_out'], xs)
    h2, gate, up, hn2 = _ffn_fwd(h1, g['ffn_norm_g'], w_gate_t.T, w_up_t.T, stacks['w_down'])
    loss, dh2, dz, dpp, hn3, d_ple_g, d_final_g = _ple_loss(
        h2, ps, target, g['ple_norm_g'], stacks['w_ple_gate'], stacks['w_ple_gate'].T, w_pp_t.T, g['final_norm_g'])

    share = {}
    share['w_ple_gate'] = _wgrad(hn3, dz, "wgrad_ple_gate")
    share['w_ple_proj'] = _wgrad(dpp, ps, "wgrad_ple_proj")
    scatter_1, token = _reduce_scatter_start([share['w_ple_gate'], share['w_ple_proj']], "reduce_scatter_start_1")
    dh1, act, dgate, dup, d_ffn_g = _ffn_bwd(dh2, h1, gate, up, _after(token, g['ffn_norm_g']), stacks['w_down'].T,
                                             w_gate_t, w_up_t)
    share['w_down'] = _wgrad(act, dh2, "wgrad_down")
    share['w_gate'] = _wgrad(dgate, hn2, "wgrad_gate")
    share['w_up'] = _wgrad(dup, hn2, "wgrad_up")
    scatter_2, token = _reduce_scatter_start([share['w_down'], share['w_gate'], share['w_up']],
                                             "reduce_scatter_start_2")
    dya_n, dyb, d_out_b = _out_bwd(dh1, y_b, _after(token, g['out_norm_b']), stacks['w_out'].T)
    share['w_out'] = _wgrad(y_n, dh1, "wgrad_out")
    scatter_3, token = _reduce_scatter_start([share['w_out']], "reduce_scatter_start_3")
    grads = None
    for dil in DILATIONS:
        grads = _attn_bwd_branch(q, k, v, dyb, y_b, lse, grads, dil)
    duv, d_sgu_w, d_sgu_b, d_sgu_g, d_out_a = _sgu_bwd(uv, dya_n, w_tril, w_tril_t, bias,
                                                       _after(token, g['sgu_norm_g']), g['out_norm_a'])
    dproj = _in_bwd_proj(duv, grads[0], grads[1], grads[2], cos_t, sin_t)
    share['w_in'] = _wgrad(dproj, hn1, "wgrad_in")
    scatter_4, token = _reduce_scatter_start([share['w_in']], "reduce_scatter_start_4")

    grads, deltas, new_m, new_v = {}, {}, {}, {}
    add_lead = lambda a: a.reshape((1,) + a.shape)

    def finish(names, handles, after, tag):
        landed = []
        for i, handle in enumerate(handles):
            landed += _reduce_scatter_wait(handle, after, "reduce_scatter_wait_%s%d" % (tag, i))
        for n, (own, land) in zip(names, landed):
            turn = (lambda a: a.T) if SLAB_IS_TRANSPOSED[n] else (lambda a: a)
            res = _adamw_of_shares(turn(shard(n)), own, land, turn(shard("m_" + n)), turn(shard("v_" + n)),
                                   "adamw_" + n)
            grads[n], deltas[n], new_m[n], new_v[n] = (add_lead(turn(a)) for a in res)

    finish(['w_ple_gate', 'w_ple_proj', 'w_down', 'w_gate', 'w_up', 'w_out'], [scatter_1, scatter_2, scatter_3], token,
           "early")
    grad_x, d_mix_g = _in_bwd_x(dproj, w_in_t, xs, g['mix_norm_g'], dh1,
                                after=[new_v[n] for n in ('w_down', 'w_gate', 'w_up', 'w_out')])

    gs = {'mix_norm_g': d_mix_g, 'sgu_w': d_sgu_w, 'sgu_b': d_sgu_b[:, :HEADS_A].T, 'sgu_norm_g': d_sgu_g,
          'out_norm_a': d_out_a, 'out_norm_b': d_out_b, 'ffn_norm_g': d_ffn_g, 'ple_norm_g': d_ple_g,
          'final_norm_g': d_final_g}
    gs_block = _pack_small(gs).at[SMALL_ROWS - 1, 0].set(loss[0, 0])
    small_sum = _small_all_reduce(gs_block)
    loss_out = small_sum[SMALL_ROWS - 1, 0]
    small_shapes = {n: given[n].shape for n in SMALL}
    finish(['w_in'], [scatter_4], small_sum, "last")

    small = {n: given[n] for n in SMALL}
    d, nm, nv = _adamw(_pack_small(small), small_sum, _pack_small({n: given["m_" + n] for n in SMALL}),
                       _pack_small({n: given["v_" + n] for n in SMALL}), "adamw_small")
    for res, blk in ((grads, small_sum), (deltas, d), (new_m, nm), (new_v, nv)):
        res.update(_unpack_small(blk, small_shapes))

    outs = [loss_out, add_lead(grad_x)]
    for res in (grads, deltas, new_m, new_v):
        outs += [res[n] for n in WEIGHT_NAMES]
    return tuple(outs)
```

```python
import functools
import math

import jax
import jax.numpy as jnp
import numpy as np
from jax import lax
from jax.experimental import pallas as pl
from jax.experimental.pallas import tpu as pltpu

F32 = jnp.float32
MXU_DTYPE = jnp.bfloat16

D_MODEL = 1024
HEAD_DIM = 64
HEADS_A = 4
HEADS_B = 12
WIDTH_A = HEADS_A * HEAD_DIM
WIDTH_B = HEADS_B * HEAD_DIM
CHUNK = 128
BLOCK = 128
DILATIONS = (1, 4, 16)
ROPE_THETA = 10000.0
D_FF = 2816
FF_HALF = D_FF // 2
FF_STRIP = FF_HALF
FF_STRIPS = ((0, 1024), (1024, 2048), (2048, D_FF))
PLE_DIM = 256
IN_COLS = 2 * WIDTH_A + 3 * WIDTH_B
EPS = 1e-6
LANES = 128
N_CHIPS = 4
N_DEV = 8

ADAM_LR = 0.001
ADAM_B1 = 0.9
ADAM_B2 = 0.999
ADAM_EPS = 1e-08
ADAM_WD = 0.01
ADAM_STEP = 10

VMEM_LIMIT = 56 * 1024 * 1024

WEIGHT_NAMES = ['mix_norm_g', 'w_in', 'sgu_w', 'sgu_b', 'sgu_norm_g', 'out_norm_a', 'out_norm_b', 'w_out',
                'ffn_norm_g', 'w_gate', 'w_up', 'w_down', 'ple_norm_g', 'w_ple_gate', 'w_ple_proj', 'final_norm_g']
SHARDED = ['w_in', 'w_out', 'w_gate', 'w_up', 'w_down', 'w_ple_gate', 'w_ple_proj']
SMALL = ['mix_norm_g', 'sgu_w', 'sgu_b', 'sgu_norm_g', 'out_norm_a', 'out_norm_b', 'ffn_norm_g', 'ple_norm_g',
         'final_norm_g']
SMALL_SIZES = {'mix_norm_g': 1024, 'sgu_w': 65536, 'sgu_b': 512, 'sgu_norm_g': 256, 'out_norm_a': 256,
               'out_norm_b': 768, 'ffn_norm_g': 1024, 'ple_norm_g': 1024, 'final_norm_g': 1024}
SMALL_ROWS = 72


def _params(semantics=None):
    return pltpu.CompilerParams(dimension_semantics=semantics, vmem_limit_bytes=VMEM_LIMIT)


def _full(shape):
    nd = len(shape)
    return pl.BlockSpec(shape, lambda i: (0,) * nd, pipeline_mode=pl.Buffered(1))


def _rows(tm, width):
    return pl.BlockSpec((tm, width), lambda i: (i, 0))


def _rms_stats(x):
    r = lax.rsqrt(jnp.mean(x * x, axis=-1, keepdims=True) + EPS)
    return x * r, r


def _rms_bwd(dn, n, r):
    return r * (dn - n * jnp.mean(dn * n, axis=-1, keepdims=True))


def _dot(a, b):
    return jnp.dot(a, b, preferred_element_type=F32)


def _dot_nt(a, b):
    return lax.dot_general(a, b, (((1,), (1,)), ((), ())), preferred_element_type=F32)


def _dot_tn(a, b):
    return lax.dot_general(a, b, (((0,), (0,)), ((), ())), preferred_element_type=F32)


def _gelu_parts(x):
    c = math.sqrt(2.0 / math.pi)
    t = jnp.tanh(c * (x + 0.044715 * x * x * x))
    return 0.5 * x * (1.0 + t), t


def _gelu_grad(x, t):
    c = math.sqrt(2.0 / math.pi)
    return 0.5 * (1.0 + t) + 0.5 * x * (1.0 - t * t) * c * (1.0 + 3.0 * 0.044715 * x * x)


def _half_masks(dtype):
    lane = lax.broadcasted_iota(jnp.int32, (BLOCK, LANES), 1)
    lo = (lane < HEAD_DIM).astype(F32)
    return lo.astype(dtype), (1.0 - lo).astype(dtype)


def _rope_partner(t):
    lane = lax.broadcasted_iota(jnp.int32, t.shape, 1)
    first_half = (lane % HEAD_DIM) < (HEAD_DIM // 2)
    return jnp.where(first_half, pltpu.roll(t, LANES - HEAD_DIM // 2, 1), pltpu.roll(t, HEAD_DIM // 2, 1))


PAIRS_ABREAST = 2
L_BLOCK = 256
L_GROUP = 16


def _store_l256(scr, out_ref, cols, value):
    tm = value.shape[0]
    scr[...] = value
    for blk in range(tm // L_BLOCK):
        for r in range(L_GROUP):
            lo = blk * L_BLOCK + r * L_GROUP
            piece = scr[pl.ds(blk * L_BLOCK + r, L_GROUP, stride=L_GROUP), :]
            out_ref[lo:lo + L_GROUP, cols] = piece.astype(out_ref.dtype)


def _load_l256(col_refs, tm):
    cols = []
    for ref in col_refs:
        pieces = [ref[pl.ds(blk * L_BLOCK + i, L_GROUP, stride=L_GROUP), :]
                  for blk in range(tm // L_BLOCK) for i in range(L_GROUP)]
        cols.append(jnp.concatenate(pieces, axis=0))
    return jnp.concatenate(cols, axis=1)


def _col_specs(tm, width):
    return [pl.BlockSpec((tm, LANES), lambda i, j=j: (i, j)) for j in range(width // LANES)]


def _in_fwd(x, g_mix, w_in, cos_t, sin_t):
    s = x.shape[0]
    tm = 512

    def body(x_ref, g_ref, w_ref, cos_ref, sin_ref, uv_ref, q_ref, k_ref, v_ref, hn_ref, *scrs):
        n, _ = _rms_stats(x_ref[...])
        hn = (n * g_ref[...]).astype(MXU_DTYPE)
        hn_ref[...] = hn
        cos = cos_ref[...]
        sin = sin_ref[...]
        strip = 2 * LANES
        for j in range(IN_COLS // strip):
            proj = _dot(hn, w_ref[:, j * strip:(j + 1) * strip])
            lo = j * strip - 2 * WIDTH_A
            if lo < 0:
                uv_ref[:, j * strip:(j + 1) * strip] = proj
                continue
            which, lo = divmod(lo, WIDTH_B)
            for i in range(strip // LANES):
                t = proj[:, i * LANES:(i + 1) * LANES]
                cols = slice(lo + i * LANES, lo + (i + 1) * LANES)
                scr = scrs[i]
                if which == 0:
                    _store_l256(scr, q_ref, cols, (t * cos + _rope_partner(t) * sin) * (HEAD_DIM ** -0.5))
                elif which == 1:
                    _store_l256(scr, k_ref, cols, t * cos + _rope_partner(t) * sin)
                else:
                    _store_l256(scr, v_ref, cols, t)

    return pl.pallas_call(
        body, name="in_fwd", grid=(s // tm,), scratch_shapes=[pltpu.VMEM((tm, LANES), F32)] * 2,
        in_specs=[_rows(tm, D_MODEL), _full((1, D_MODEL)), _full((D_MODEL, IN_COLS)), _rows(tm, LANES),
                  _rows(tm, LANES)],
        out_specs=[_rows(tm, 2 * WIDTH_A), _rows(tm, WIDTH_B), _rows(tm, WIDTH_B), _rows(tm, WIDTH_B),
                   _rows(tm, D_MODEL)],
        out_shape=[jax.ShapeDtypeStruct((s, 2 * WIDTH_A), F32), jax.ShapeDtypeStruct((s, WIDTH_B), MXU_DTYPE),
                   jax.ShapeDtypeStruct((s, WIDTH_B), MXU_DTYPE), jax.ShapeDtypeStruct((s, WIDTH_B), MXU_DTYPE),
                   jax.ShapeDtypeStruct((s, D_MODEL), MXU_DTYPE)],
        compiler_params=_params(("arbitrary",)),
    )(x, g_mix, w_in, cos_t, sin_t)


class _Branch:
    def __init__(self, dil, s, qn=BLOCK):
        self.dil = dil
        i = np.arange(L_GROUP)
        if dil == 16:
            nblk = qn // 16
            self.grid = (16, s // (L_BLOCK * nblk))
            self.shape = (nblk, 1, 1, L_GROUP)
            self.index = lambda r, n: (n, r // 4, r % 4, 0, 0)
            pos = (np.arange(nblk)[:, None] * 16 + i[None, :]).reshape(-1)
        elif dil == 4:
            nblk = qn // 64
            self.grid = (4, s // (L_BLOCK * nblk))
            self.shape = (nblk, 4, 1, L_GROUP)
            self.index = lambda r, n: (n, 0, r, 0, 0)
            pos = (np.arange(nblk)[:, None, None] * 64 + np.arange(4)[None, :, None]
                   + 4 * i[None, None, :]).reshape(-1)
        else:
            self.grid = (1, s // L_BLOCK)
            self.shape = (1, 4, 4, L_GROUP)
            self.index = lambda r, n: (n, 0, 0, 0, 0)
            pos = (np.arange(16)[:, None] + 16 * i[None, :]).reshape(-1)
        self.qn = pos.shape[0]
        self.nb = self.grid[1]
        dist = pos[:, None] - np.concatenate([pos - self.qn, pos])[None, :]
        band = (dist >= 0) & (dist <= BLOCK)
        start = band & (np.arange(2 * self.qn)[None, :] >= self.qn)
        self.bias = np.where(np.stack([band, start]), 0.0, -np.inf).astype(np.float32)

    def view(self, a):
        return a.reshape(a.shape[0] // L_BLOCK, 4, 4, L_GROUP, a.shape[1])

    def spec(self, w, step=lambda n: n):
        return pl.BlockSpec(self.shape + (w,), lambda r, n: self.index(r, step(n)))

    def bias_spec(self, step=lambda n: n):
        return pl.BlockSpec((1, self.qn, 2 * self.qn), lambda r, n: (jnp.where(step(n) == 0, 1, 0), 0, 0))

    def load(self, ref, cols=slice(None)):
        x = ref[:, :, :, :, cols]
        return x.reshape(self.qn, x.shape[-1])

    def store(self, ref, cols, value):
        ref[:, :, :, :, cols] = value.reshape(self.shape + (value.shape[-1],))


def _attn_fwd_branch(q, k, v, dil):
    s = q.shape[0]
    br = _Branch(dil, s)
    qn = br.qn

    def body(bias_ref, q_ref, kc_ref, kp_ref, vc_ref, vp_ref, o_ref, lse_ref):
        bias2 = jnp.concatenate([bias_ref[0], bias_ref[0]], axis=0)
        lo = lax.broadcasted_iota(jnp.int32, (qn, LANES), 1) < HEAD_DIM
        mask_lo = lo.astype(F32).astype(MXU_DTYPE)
        for hp in range(HEADS_B // 2):
            cols = slice(hp * LANES, (hp + 1) * LANES)
            qp = br.load(q_ref, cols)
            kcat = jnp.concatenate([br.load(kp_ref, cols), br.load(kc_ref, cols)], axis=0)
            vcat = jnp.concatenate([br.load(vp_ref, cols), br.load(vc_ref, cols)], axis=0)
            sc = _dot_nt(jnp.concatenate([qp * mask_lo, qp * (1 - mask_lo)], axis=0), kcat) + bias2
            m = jnp.max(sc, axis=1, keepdims=True)
            p = jnp.exp(sc - m)
            l = jnp.sum(p, axis=1, keepdims=True)
            out = _dot(p.astype(MXU_DTYPE), vcat) / l
            lse = m + jnp.log(l)
            br.store(o_ref, cols, jnp.where(lo, out[:qn], out[qn:]))
            br.store(lse_ref, cols, jnp.where(lo, lse[:qn], lse[qn:]))

    before = lambda n: jnp.maximum(n - 1, 0)
    res = pl.pallas_call(
        body, name="attn_fwd_d%d" % dil, grid=br.grid,
        in_specs=[br.bias_spec(), br.spec(WIDTH_B), br.spec(WIDTH_B), br.spec(WIDTH_B, before), br.spec(WIDTH_B),
                  br.spec(WIDTH_B, before)],
        out_specs=[br.spec(WIDTH_B), br.spec(WIDTH_B)],
        out_shape=[jax.ShapeDtypeStruct((s // L_BLOCK, 4, 4, L_GROUP, WIDTH_B), F32)] * 2,
        compiler_params=_params(("arbitrary", "arbitrary")),
    )(jnp.asarray(br.bias), br.view(q), br.view(k), br.view(k), br.view(v), br.view(v))
    return tuple(a.reshape(s, WIDTH_B) for a in res)


def _attn_merge(outs, lses):
    s = outs[0].shape[0]
    tm = 512
    nbr = len(outs)

    def body(*refs):
        o_refs, l_refs, y_ref, lse_ref = refs[:nbr], refs[nbr:2 * nbr], refs[2 * nbr], refs[2 * nbr + 1]
        ls = [r[...] for r in l_refs]
        top = functools.reduce(jnp.maximum, ls)
        ws = [jnp.exp(l - top) for l in ls]
        den = functools.reduce(jnp.add, ws)
        num = functools.reduce(jnp.add, [w * r[...] for w, r in zip(ws, o_refs)])
        y_ref[...] = num / den
        lse_ref[...] = top + jnp.log(den)

    return pl.pallas_call(
        body, name="attn_merge", grid=(s // tm,), in_specs=[_rows(tm, WIDTH_B)] * (2 * nbr),
        out_specs=[_rows(tm, WIDTH_B)] * 2, out_shape=[jax.ShapeDtypeStruct((s, WIDTH_B), F32)] * 2,
        compiler_params=_params(("arbitrary",)),
    )(*outs, *lses)


def _sgu_forward_tile(uv, w_ref, bias, g_sgu):
    tm = uv.shape[0]
    u = uv[:, :WIDTH_A]
    v = uv[:, WIDTH_A:]
    ug, tu = _gelu_parts(u)
    vg, tv = _gelu_parts(v)
    mu = jnp.mean(vg, axis=-1, keepdims=True)
    vc = vg - mu
    rs = lax.rsqrt(jnp.mean(vc * vc, axis=-1, keepdims=True) + EPS)
    vhat = vc * rs
    vn = (vhat * g_sgu).astype(MXU_DTYPE)
    masks = _half_masks(MXU_DTYPE)
    chunks = []
    for c in range(tm // CHUNK):
        rows = slice(c * CHUNK, (c + 1) * CHUNK)
        groups = []
        for gp in range(2):
            vn_g = vn[rows, gp * LANES:(gp + 1) * LANES]
            groups.append(_dot(w_ref[2 * gp], vn_g * masks[0]) + _dot(w_ref[2 * gp + 1], vn_g * masks[1]))
        chunks.append(jnp.concatenate(groups, axis=1) + bias)
    mixed = jnp.concatenate(chunks, axis=0)
    return dict(u=u, v=v, ug=ug, tu=tu, tv=tv, rs=rs, vhat=vhat, vn=vn, mixed=mixed, ya=ug * mixed)


def _sgu_fwd(uv, w_tril, bias, g_sgu, g_a):
    s = uv.shape[0]
    tm = 512

    def body(uv_ref, w_ref, b_ref, gs_ref, ga_ref, o_ref):
        t = _sgu_forward_tile(uv_ref[...], w_ref, b_ref[...], gs_ref[...])
        n, _ = _rms_stats(t['ya'])
        o_ref[...] = (n * ga_ref[...]).astype(MXU_DTYPE)

    return pl.pallas_call(
        body, name="sgu_fwd", grid=(s // tm,),
        in_specs=[_rows(tm, 2 * WIDTH_A), _full((HEADS_A, CHUNK, CHUNK)), _full((CHUNK, WIDTH_A)),
                  _full((1, WIDTH_A)), _full((1, WIDTH_A))],
        out_specs=_rows(tm, WIDTH_A), out_shape=jax.ShapeDtypeStruct((s, WIDTH_A), MXU_DTYPE),
        compiler_params=_params(("arbitrary",)),
    )(uv, w_tril, bias, g_sgu, g_a)


def _out_fwd(ya_n, y_b, g_b, w_out, x):
    s = x.shape[0]
    tm = 512
    nc = WIDTH_B // LANES

    def body(ya_ref, *refs):
        yb_refs = refs[:nc]
        g_ref, w_ref, x_ref, h_ref, yn_ref = refs[nc:]
        n, _ = _rms_stats(_load_l256(yb_refs, tm))
        yn = jnp.concatenate([ya_ref[...], (n * g_ref[...]).astype(MXU_DTYPE)], axis=1)
        yn_ref[...] = yn
        h_ref[...] = x_ref[...] + _dot(yn, w_ref[...])

    return pl.pallas_call(
        body, name="out_fwd", grid=(s // tm,),
        in_specs=[_rows(tm, WIDTH_A)] + _col_specs(tm, WIDTH_B) + [_full((1, WIDTH_B)), _full((D_MODEL, D_MODEL)),
                                                                 _rows(tm, D_MODEL)],
        out_specs=[_rows(tm, D_MODEL), _rows(tm, D_MODEL)],
        out_shape=[jax.ShapeDtypeStruct((s, D_MODEL), F32), jax.ShapeDtypeStruct((s, D_MODEL), MXU_DTYPE)],
        compiler_params=_params(("arbitrary",)),
    )(ya_n, *([y_b] * nc), g_b, w_out, x)


def _ffn_fwd(h1, g_ffn, w_gate, w_up, w_down):
    s = h1.shape[0]
    tm = 512

    def body(h_ref, g_ref, wg_ref, wu_ref, wd_ref, o_ref, gate_ref, up_ref, hn_ref):
        h = h_ref[...]
        n, _ = _rms_stats(h)
        hn = (n * g_ref[...]).astype(MXU_DTYPE)
        hn_ref[...] = hn
        strips = [dict(cols=slice(lo, hi)) for lo, hi in FF_STRIPS]

        def project(t):
            t['gate'] = _dot(hn, wg_ref[:, t['cols']])
            t['up'] = _dot(hn, wu_ref[:, t['cols']])

        def activate(t):
            gate, up = t['gate'], t['up']
            gate_ref[:, t['cols']] = gate.astype(MXU_DTYPE)
            up_ref[:, t['cols']] = up.astype(MXU_DTYPE)
            t['act'] = (gate * jax.nn.sigmoid(gate) * up).astype(MXU_DTYPE)

        def down(t):
            return _dot(t['act'], wd_ref[t['cols'], :])

        out = h
        project(strips[0])
        for i, t in enumerate(strips):
            if i + 1 < len(strips):
                project(strips[i + 1])
            activate(t)
            out = out + down(t)
        o_ref[...] = out

    return pl.pallas_call(
        body, name="ffn_fwd", grid=(s // tm,),
        in_specs=[_rows(tm, D_MODEL), _full((1, D_MODEL)), _full((D_MODEL, D_FF)), _full((D_MODEL, D_FF)),
                  _full((D_FF, D_MODEL))],
        out_specs=[_rows(tm, D_MODEL), _rows(tm, D_FF), _rows(tm, D_FF), _rows(tm, D_MODEL)],
        out_shape=[jax.ShapeDtypeStruct((s, D_MODEL), F32), jax.ShapeDtypeStruct((s, D_FF), MXU_DTYPE),
                   jax.ShapeDtypeStruct((s, D_FF), MXU_DTYPE), jax.ShapeDtypeStruct((s, D_MODEL), MXU_DTYPE)],
        compiler_params=_params(("arbitrary",)),
    )(h1, g_ffn, w_gate, w_up, w_down)


def _ple_loss(h2, p, target, g_ple, w_pg, w_pg_t, w_pp, g_final):
    s = h2.shape[0]
    tm = 256

    def body(h_ref, p_ref, t_ref, gp_ref, wg_ref, wgt_ref, wp_ref, gf_ref,
             loss_ref, dh_ref, dz_ref, dpp_ref, hn_ref, dgp_ref, dgf_ref):
        @pl.when(pl.program_id(0) == 0)
        def _():
            loss_ref[...] = jnp.zeros_like(loss_ref)
            dgp_ref[...] = jnp.zeros_like(dgp_ref)
            dgf_ref[...] = jnp.zeros_like(dgf_ref)

        h2t = h_ref[...]
        n2, r2 = _rms_stats(h2t)
        hn = (n2 * gp_ref[...]).astype(MXU_DTYPE)
        hn_ref[...] = hn
        gate = jax.nn.sigmoid(_dot(hn, wg_ref[...]))
        pp = _dot(p_ref[...].astype(MXU_DTYPE), wp_ref[...])
        h3 = h2t + gate * pp
        n3, r3 = _rms_stats(h3)
        diff = n3 * gf_ref[...] - t_ref[...]
        loss_ref[...] += jnp.full(loss_ref.shape, 0.5 * jnp.sum(diff * diff) / D_MODEL, F32)
        dy = diff * (1.0 / D_MODEL)
        dgf_ref[...] += jnp.sum(dy * n3, axis=0, keepdims=True)
        dh3 = _rms_bwd(dy * gf_ref[...], n3, r3)
        dpp_ref[...] = (dh3 * gate).astype(MXU_DTYPE)
        dz = (dh3 * pp * gate * (1.0 - gate)).astype(MXU_DTYPE)
        dz_ref[...] = dz
        dhn = _dot(dz, wgt_ref[...])
        dgp_ref[...] += jnp.sum(dhn * n2, axis=0, keepdims=True)
        dh_ref[...] = dh3 + _rms_bwd(dhn * gp_ref[...], n2, r2)

    return pl.pallas_call(
        body, name="ple_loss", grid=(s // tm,),
        in_specs=[_rows(tm, D_MODEL), _rows(tm, PLE_DIM), _rows(tm, D_MODEL), _full((1, D_MODEL)),
                  _full((D_MODEL, D_MODEL)), _full((D_MODEL, D_MODEL)), _full((PLE_DIM, D_MODEL)),
                  _full((1, D_MODEL))],
        out_specs=[_full((1, LANES)), _rows(tm, D_MODEL), _rows(tm, D_MODEL), _rows(tm, D_MODEL),
                   _rows(tm, D_MODEL), _full((1, D_MODEL)), _full((1, D_MODEL))],
        out_shape=[jax.ShapeDtypeStruct((1, LANES), F32), jax.ShapeDtypeStruct((s, D_MODEL), F32),
                   jax.ShapeDtypeStruct((s, D_MODEL), MXU_DTYPE), jax.ShapeDtypeStruct((s, D_MODEL), MXU_DTYPE),
                   jax.ShapeDtypeStruct((s, D_MODEL), MXU_DTYPE), jax.ShapeDtypeStruct((1, D_MODEL), F32),
                   jax.ShapeDtypeStruct((1, D_MODEL), F32)],
        compiler_params=_params(("arbitrary",)),
    )(h2, p, target, g_ple, w_pg, w_pg_t, w_pp, g_final)


def _ffn_bwd(dh2, h1, gate, up, g_ffn, w_down_t, w_gate_t, w_up_t):
    s = h1.shape[0]
    tm = 256

    def body(dh_ref, h_ref, gate_ref, up_ref, g_ref, wdt_ref, wgt_ref, wut_ref,
             o_ref, act_ref, dg_ref, du_ref, dgn_ref):
        @pl.when(pl.program_id(0) == 0)
        def _():
            dgn_ref[...] = jnp.zeros_like(dgn_ref)

        dh = dh_ref[...]
        dhb = dh.astype(MXU_DTYPE)
        strips = [dict(cols=slice(lo, hi)) for lo, hi in FF_STRIPS]

        def back_down(t):
            t['dact'] = _dot(dhb, wdt_ref[:, t['cols']])

        def back_act(t):
            cols, dact = t['cols'], t['dact']
            g = gate_ref[:, cols].astype(F32)
            u = up_ref[:, cols].astype(F32)
            sg = jax.nn.sigmoid(g)
            silu = g * sg
            act_ref[:, cols] = (silu * u).astype(MXU_DTYPE)
            t['du'] = (dact * silu).astype(MXU_DTYPE)
            t['dg'] = (dact * u * sg * (1.0 + g * (1.0 - sg))).astype(MXU_DTYPE)
            du_ref[:, cols] = t['du']
            dg_ref[:, cols] = t['dg']

        def back_in(t):
            return _dot(t['dg'], wgt_ref[t['cols'], :]) + _dot(t['du'], wut_ref[t['cols'], :])

        dhn = jnp.zeros((tm, D_MODEL), F32)
        back_down(strips[0])
        for i, t in enumerate(strips):
            if i + 1 < len(strips):
                back_down(strips[i + 1])
            back_act(t)
            dhn = dhn + back_in(t)
        n, r = _rms_stats(h_ref[...])
        dgn_ref[...] += jnp.sum(dhn * n, axis=0, keepdims=True)
        o_ref[...] = dh + _rms_bwd(dhn * g_ref[...], n, r)

    return pl.pallas_call(
        body, name="ffn_bwd", grid=(s // tm,),
        in_specs=[_rows(tm, D_MODEL), _rows(tm, D_MODEL), _rows(tm, D_FF), _rows(tm, D_FF), _full((1, D_MODEL)),
                  _full((D_MODEL, D_FF)), _full((D_FF, D_MODEL)), _full((D_FF, D_MODEL))],
        out_specs=[_rows(tm, D_MODEL), _rows(tm, D_FF), _rows(tm, D_FF), _rows(tm, D_FF), _full((1, D_MODEL))],
        out_shape=[jax.ShapeDtypeStruct((s, D_MODEL), F32), jax.ShapeDtypeStruct((s, D_FF), MXU_DTYPE),
                   jax.ShapeDtypeStruct((s, D_FF), MXU_DTYPE), jax.ShapeDtypeStruct((s, D_FF), MXU_DTYPE),
                   jax.ShapeDtypeStruct((1, D_MODEL), F32)],
        compiler_params=_params(("arbitrary",)),
    )(dh2, h1, gate, up, g_ffn, w_down_t, w_gate_t, w_up_t)


def _out_bwd(dh1, y_b, g_b, w_out_t):
    s = dh1.shape[0]
    tm = 512
    nc = WIDTH_B // LANES

    def body(dh_ref, *refs):
        yb_refs = refs[:nc]
        g_ref, wt_ref, dya_ref, dyb_ref, dg_ref, scr = refs[nc:]

        @pl.when(pl.program_id(0) == 0)
        def _():
            dg_ref[...] = jnp.zeros_like(dg_ref)

        dy = _dot(dh_ref[...].astype(MXU_DTYPE), wt_ref[...])
        dya_ref[...] = dy[:, :WIDTH_A]
        dyb = dy[:, WIDTH_A:]
        n, r = _rms_stats(_load_l256(yb_refs, tm))
        dg_ref[...] += jnp.sum(dyb * n, axis=0, keepdims=True)
        dyb_in = _rms_bwd(dyb * g_ref[...], n, r)
        for j in range(nc):
            cols = slice(j * LANES, (j + 1) * LANES)
            _store_l256(scr, dyb_ref, cols, dyb_in[:, cols])

    return pl.pallas_call(
        body, name="out_bwd", grid=(s // tm,), scratch_shapes=[pltpu.VMEM((tm, LANES), F32)],
        in_specs=[_rows(tm, D_MODEL)] + _col_specs(tm, WIDTH_B) + [_full((1, WIDTH_B)), _full((D_MODEL, D_MODEL))],
        out_specs=[_rows(tm, WIDTH_A), _rows(tm, WIDTH_B), _full((1, WIDTH_B))],
        out_shape=[jax.ShapeDtypeStruct((s, WIDTH_A), F32), jax.ShapeDtypeStruct((s, WIDTH_B), F32),
                   jax.ShapeDtypeStruct((1, WIDTH_B), F32)],
        compiler_params=_params(("arbitrary",)),
    )(dh1, *([y_b] * nc), g_b, w_out_t)


def _attn_bwd_branch(q, k, v, do, o, lse, grads, dil):
    s = q.shape[0]
    br = _Branch(dil, s)
    qn, nb = br.qn, br.nb
    first = grads is None

    def body(*refs):
        bias_ref, q_ref, kc_ref, kp_ref, vc_ref, vp_ref, do_ref, o_ref, lse_ref = refs[:9]
        if first:
            rest = refs[9:]
        else:
            dq_in, dk_in, dv_in = refs[9:12]
            rest = refs[12:]
        dq_ref, dk_ref, dv_ref, dk_carry, dv_carry = rest
        n = pl.program_id(1)

        @pl.when(n == 0)
        def _():
            dk_carry[...] = jnp.zeros_like(dk_carry)
            dv_carry[...] = jnp.zeros_like(dv_carry)

        @pl.when(n < nb)
        def _():
            bias2 = jnp.concatenate([bias_ref[0], bias_ref[0]], axis=0)
            lane = lax.broadcasted_iota(jnp.int32, (qn, LANES), 1)
            lo = lane < HEAD_DIM
            mask_f = lo.astype(F32)
            mask_lo = mask_f.astype(MXU_DTYPE)
            def prepare(hp):
                cols = slice(hp * LANES, (hp + 1) * LANES)
                qp = br.load(q_ref, cols)
                dop = br.load(do_ref, cols)
                prod = dop * br.load(o_ref, cols)
                prod_lo = prod * mask_f
                lse = br.load(lse_ref, cols)
                return dict(
                    cols=cols,
                    kcat=jnp.concatenate([br.load(kp_ref, cols), br.load(kc_ref, cols)], axis=0),
                    vcat=jnp.concatenate([br.load(vp_ref, cols), br.load(vc_ref, cols)], axis=0),
                    qs=jnp.concatenate([qp * mask_lo, qp * (1 - mask_lo)], axis=0),
                    dos=jnp.concatenate([dop * mask_f, dop * (1.0 - mask_f)], axis=0).astype(MXU_DTYPE),
                    delta=jnp.concatenate([jnp.sum(prod_lo, axis=1, keepdims=True),
                                           jnp.sum(prod - prod_lo, axis=1, keepdims=True)], axis=0),
                    lse2=jnp.concatenate([lse[:, :1], lse[:, HEAD_DIM:HEAD_DIM + 1]], axis=0))

            def scores(t):
                t['sc'] = _dot_nt(t['qs'], t['kcat'])
                t['dp'] = _dot_nt(t['dos'], t['vcat'])

            def softmax(t):
                p = jnp.exp(t['sc'] + bias2 - t['lse2'])
                t['ds'] = (p * (t['dp'] - t['delta'])).astype(MXU_DTYPE)
                t['p'] = p.astype(MXU_DTYPE)

            def gradients(t):
                t['dvc'] = _dot_tn(t['p'], t['dos'])
                t['dkc'] = _dot_tn(t['ds'], t['qs'])
                t['dq2'] = _dot(t['ds'], t['kcat'])

            def store(t):
                cols, dkc, dvc = t['cols'], t['dkc'], t['dvc']
                dq = jnp.where(lo, t['dq2'][:qn], t['dq2'][qn:])
                dk_prev = dk_carry[:, cols] + dkc[:qn]
                dv_prev = dv_carry[:, cols] + dvc[:qn]
                if not first:
                    dq = dq + br.load(dq_in, cols)
                    dk_prev = dk_prev + br.load(dk_in, cols)
                    dv_prev = dv_prev + br.load(dv_in, cols)
                br.store(dq_ref, cols, dq)
                br.store(dk_ref, cols, dk_prev)
                br.store(dv_ref, cols, dv_prev)
                dk_carry[:, cols] = dkc[qn:]
                dv_carry[:, cols] = dvc[qn:]

            for first_pair in range(0, HEADS_B // 2, PAIRS_ABREAST):
                group = [prepare(hp) for hp in range(first_pair, first_pair + PAIRS_ABREAST)]
                for stage in (scores, softmax, gradients, store):
                    for t in group:
                        stage(t)

        @pl.when(n == nb)
        def _():
            dk_last = dk_carry[...]
            dv_last = dv_carry[...]
            if not first:
                dk_last = dk_last + br.load(dk_in)
                dv_last = dv_last + br.load(dv_in)
            br.store(dk_ref, slice(None), dk_last)
            br.store(dv_ref, slice(None), dv_last)

    cur = lambda n: jnp.minimum(n, nb - 1)
    before = lambda n: jnp.maximum(cur(n) - 1, 0)
    late = lambda n: jnp.maximum(n - 1, 0)
    in_specs = [br.bias_spec(cur), br.spec(WIDTH_B, cur), br.spec(WIDTH_B, cur), br.spec(WIDTH_B, before),
                br.spec(WIDTH_B, cur), br.spec(WIDTH_B, before), br.spec(WIDTH_B, cur), br.spec(WIDTH_B, cur),
                br.spec(WIDTH_B, cur)]
    args = [jnp.asarray(br.bias)] + [br.view(a) for a in (q, k, k, v, v, do, o, lse)]
    if not first:
        in_specs += [br.spec(WIDTH_B, cur), br.spec(WIDTH_B, late), br.spec(WIDTH_B, late)]
        args += [br.view(g) for g in grads]
    res = pl.pallas_call(
        body, name="attn_bwd_d%d" % dil, grid=(br.grid[0], nb + 1), in_specs=in_specs,
        out_specs=[br.spec(WIDTH_B, cur), br.spec(WIDTH_B, late), br.spec(WIDTH_B, late)],
        out_shape=[jax.ShapeDtypeStruct((s // L_BLOCK, 4, 4, L_GROUP, WIDTH_B), F32)] * 3,
        scratch_shapes=[pltpu.VMEM((qn, WIDTH_B), F32), pltpu.VMEM((qn, WIDTH_B), F32)],
        compiler_params=_params(("arbitrary", "arbitrary")),
    )(*args)
    return tuple(a.reshape(s, WIDTH_B) for a in res)


def _sgu_bwd(uv, dya_n, w_tril, w_tril_t, bias, g_sgu, g_a):
    s = uv.shape[0]
    tm = 512

    def body(uv_ref, dy_ref, w_ref, wt_ref, b_ref, gs_ref, ga_ref, duv_ref, dw_ref, db_ref, dgs_ref, dga_ref,
             db_acc):
        i = pl.program_id(0)

        @pl.when(i == 0)
        def _():
            dw_ref[...] = jnp.zeros_like(dw_ref)
            dgs_ref[...] = jnp.zeros_like(dgs_ref)
            dga_ref[...] = jnp.zeros_like(dga_ref)
            db_acc[...] = jnp.zeros_like(db_acc)

        t = _sgu_forward_tile(uv_ref[...], w_ref, b_ref[...], gs_ref[...])
        na, ra = _rms_stats(t['ya'])
        dyn = dy_ref[...]
        dga_ref[...] += jnp.sum(dyn * na, axis=0, keepdims=True)
        dya = _rms_bwd(dyn * ga_ref[...], na, ra)
        dug = dya * t['mixed']
        dmixed = dya * t['ug']
        dmb = dmixed.astype(MXU_DTYPE)
        masks = _half_masks(MXU_DTYPE)
        chunks = []
        db = jnp.zeros((CHUNK, WIDTH_A), F32)
        for c in range(tm // CHUNK):
            rows = slice(c * CHUNK, (c + 1) * CHUNK)
            db = db + dmixed[rows]
            groups = []
            for gp in range(2):
                cols = slice(gp * LANES, (gp + 1) * LANES)
                dm_g = dmb[rows, cols]
                vn_g = t['vn'][rows, cols]
                dvn_g = jnp.zeros((CHUNK, LANES), F32)
                for j in range(2):
                    dm_h = dm_g * masks[j]
                    dvn_g = dvn_g + _dot(wt_ref[2 * gp + j], dm_h)
                    dw_ref[2 * gp + j] += _dot_nt(dm_h, vn_g)
                groups.append(dvn_g)
            chunks.append(jnp.concatenate(groups, axis=1))
        db_acc[...] += db
        dvn = jnp.concatenate(chunks, axis=0)
        vhat = t['vhat']
        dgs_ref[...] += jnp.sum(dvn * vhat, axis=0, keepdims=True)
        dvh = dvn * gs_ref[...]
        dvg = t['rs'] * (dvh - jnp.mean(dvh, axis=-1, keepdims=True)
                         - vhat * jnp.mean(dvh * vhat, axis=-1, keepdims=True))
        duv_ref[:, :WIDTH_A] = (dug * _gelu_grad(t['u'], t['tu'])).astype(MXU_DTYPE)
        duv_ref[:, WIDTH_A:] = (dvg * _gelu_grad(t['v'], t['tv'])).astype(MXU_DTYPE)

        @pl.when(i == pl.num_programs(0) - 1)
        def _():
            lane_a = lax.broadcasted_iota(jnp.int32, (CHUNK, WIDTH_A), 1)
            lane = lax.broadcasted_iota(jnp.int32, (CHUNK, LANES), 1)
            acc = db_acc[...]
            out = jnp.zeros((CHUNK, LANES), F32)
            for h in range(HEADS_A):
                col = jnp.sum(jnp.where(lane_a // HEAD_DIM == h, acc, 0.0), axis=1, keepdims=True)
                out = jnp.where(lane == h, col, out)
            db_ref[...] = out
            causal = (lax.broadcasted_iota(jnp.int32, (CHUNK, CHUNK), 0)
                      >= lax.broadcasted_iota(jnp.int32, (CHUNK, CHUNK), 1))
            for h in range(HEADS_A):
                dw_ref[h] = jnp.where(causal, dw_ref[h], 0.0)

    return pl.pallas_call(
        body, name="sgu_bwd", grid=(s // tm,),
        in_specs=[_rows(tm, 2 * WIDTH_A), _rows(tm, WIDTH_A), _full((HEADS_A, CHUNK, CHUNK)),
                  _full((HEADS_A, CHUNK, CHUNK)), _full((CHUNK, WIDTH_A)), _full((1, WIDTH_A)),
                  _full((1, WIDTH_A))],
        out_specs=[_rows(tm, 2 * WIDTH_A), _full((HEADS_A, CHUNK, CHUNK)), _full((CHUNK, LANES)),
                   _full((1, WIDTH_A)), _full((1, WIDTH_A))],
        out_shape=[jax.ShapeDtypeStruct((s, 2 * WIDTH_A), MXU_DTYPE),
                   jax.ShapeDtypeStruct((HEADS_A, CHUNK, CHUNK), F32), jax.ShapeDtypeStruct((CHUNK, LANES), F32),
                   jax.ShapeDtypeStruct((1, WIDTH_A), F32), jax.ShapeDtypeStruct((1, WIDTH_A), F32)],
        scratch_shapes=[pltpu.VMEM((CHUNK, WIDTH_A), F32)],
        compiler_params=_params(("arbitrary",)),
    )(uv, dya_n, w_tril, w_tril_t, bias, g_sgu, g_a)


def _in_bwd_proj(duv, dq, dk, dv, cos_t, sin_t):
    s = duv.shape[0]
    tm = 512
    nc = WIDTH_B // LANES

    def body(duv_ref, *refs):
        dq_refs, dk_refs, dv_refs = refs[:nc], refs[nc:2 * nc], refs[2 * nc:3 * nc]
        cos_ref, sin_ref, dp_ref = refs[3 * nc:]
        cos = cos_ref[...]
        sin = sin_ref[...]
        dp_ref[:, :2 * WIDTH_A] = duv_ref[...]
        for i in range(nc):
            lo = 2 * WIDTH_A + i * LANES
            tq = _load_l256(dq_refs[i:i + 1], tm) * (HEAD_DIM ** -0.5)
            tk = _load_l256(dk_refs[i:i + 1], tm)
            dp_ref[:, lo:lo + LANES] = (tq * cos + _rope_partner(tq * sin)).astype(MXU_DTYPE)
            dp_ref[:, lo + WIDTH_B:lo + WIDTH_B + LANES] = (tk * cos + _rope_partner(tk * sin)).astype(MXU_DTYPE)
            dp_ref[:, lo + 2 * WIDTH_B:lo + 2 * WIDTH_B + LANES] = _load_l256(dv_refs[i:i + 1], tm).astype(MXU_DTYPE)

    return pl.pallas_call(
        body, name="in_bwd_proj", grid=(s // tm,),
        in_specs=[_rows(tm, 2 * WIDTH_A)] + 3 * _col_specs(tm, WIDTH_B) + [_rows(tm, LANES), _rows(tm, LANES)],
        out_specs=_rows(tm, IN_COLS), out_shape=jax.ShapeDtypeStruct((s, IN_COLS), MXU_DTYPE),
        compiler_params=_params(("arbitrary",)),
    )(duv, *([dq] * nc), *([dk] * nc), *([dv] * nc), cos_t, sin_t)


def _in_bwd_x(dproj, w_in_t, x, g_mix, dh1, after):
    s = x.shape[0]
    tm = 512

    def body(dp_ref, wt_ref, x_ref, g_ref, dh_ref, *rest):
        gx_ref, dg_ref = rest[len(after):]
        @pl.when(pl.program_id(0) == 0)
        def _():
            dg_ref[...] = jnp.zeros_like(dg_ref)

        dhn = _dot(dp_ref[...], wt_ref[...])
        n, r = _rms_stats(x_ref[...])
        dg_ref[...] += jnp.sum(dhn * n, axis=0, keepdims=True)
        gx_ref[...] = dh_ref[...] + _rms_bwd(dhn * g_ref[...], n, r)

    return pl.pallas_call(
        body, name="in_bwd_x", grid=(s // tm,),
        in_specs=[_rows(tm, IN_COLS), _full((IN_COLS, D_MODEL)), _rows(tm, D_MODEL), _full((1, D_MODEL)),
                  _rows(tm, D_MODEL)] + [pl.BlockSpec(memory_space=pl.ANY)] * len(after),
        out_specs=[_rows(tm, D_MODEL), _full((1, D_MODEL))],
        out_shape=[jax.ShapeDtypeStruct((s, D_MODEL), F32), jax.ShapeDtypeStruct((1, D_MODEL), F32)],
        compiler_params=_params(("arbitrary",)),
    )(dproj, w_in_t, x, g_mix, dh1, *after)


def _wgrad(a, b, name):
    s, m = a.shape
    n = b.shape[1]
    bm = 512 if m % 512 == 0 else FF_HALF
    ts = 1024
    nsteps = s // ts

    def body(a_ref, b_ref, o_ref, acc):
        kk = pl.program_id(1)

        @pl.when(kk == 0)
        def _():
            acc[...] = jnp.zeros_like(acc)

        acc[...] += _dot_tn(a_ref[...].astype(MXU_DTYPE), b_ref[...].astype(MXU_DTYPE))

        @pl.when(kk == nsteps - 1)
        def _():
            o_ref[...] = acc[...].astype(o_ref.dtype)

    return pl.pallas_call(
        body, name=name, grid=(m // bm, nsteps),
        in_specs=[pl.BlockSpec((ts, bm), lambda i, kk: (kk, i)), pl.BlockSpec((ts, n), lambda i, kk: (kk, 0))],
        out_specs=pl.BlockSpec((bm, n), lambda i, kk: (i, 0)), out_shape=jax.ShapeDtypeStruct((m, n), jnp.bfloat16),
        scratch_shapes=[pltpu.VMEM((bm, n), F32)],
        compiler_params=_params(("arbitrary", "arbitrary")),
    )(a, b)


def _rope_tables(s):
    half = HEAD_DIM // 2
    inv = ROPE_THETA ** (-jnp.arange(half, dtype=F32) / half)
    ang = jnp.arange(s, dtype=F32)[:, None] * inv[None, :]
    cos = jnp.cos(ang)
    sin = jnp.sin(ang)
    cos_t = jnp.concatenate([cos, cos, cos, cos], axis=1)
    sin_t = jnp.concatenate([-sin, sin, -sin, sin], axis=1)
    return cos_t, sin_t


MESH = pl.DeviceIdType.MESH
ANY = pl.BlockSpec(memory_space=pl.ANY)
SEM = pl.BlockSpec(memory_space=pltpu.SEMAPHORE)
SPLIT_COPY = pltpu.CompilerParams(has_side_effects=pltpu.SideEffectType.DATAFLOW_SIDE_EFFECTING)
SLAB_IS_TRANSPOSED = {'w_in': True, 'w_out': False, 'w_gate': True, 'w_up': True, 'w_down': False,
                      'w_ple_gate': False, 'w_ple_proj': True}


def _place():
    x, y, c = lax.axis_index("x"), lax.axis_index("y"), lax.axis_index("c")
    other_chips = [(1 - x, y), (x, 1 - y), (1 - x, 1 - y)]
    return x, y, c, other_chips


def _chip_of(chip):
    return 2 * chip[0] + chip[1]


def _half(ref, lead, hc):
    hr = ref.shape[1] // 2
    return ref.at[lead, pl.ds(hc * hr, hr), :]


def _put_own(stack, own, index):
    return lax.dynamic_update_slice(stack, own[None], (index,) + (0,) * own.ndim)


def _all_gather_now(slab):
    rows, cols = slab.shape

    def body(x_ref, out_ref, send_sems, recv_sems):
        x, y, c, chips = _place()
        sibling = (x, y, 1 - c)
        hr = rows // 2

        def copy(k, src, dst, to):
            return pltpu.make_async_remote_copy(src_ref=src, dst_ref=dst, send_sem=send_sems.at[k],
                                                recv_sem=recv_sems.at[k], device_id=to, device_id_type=MESH)

        my_half = x_ref.at[pl.ds(c * hr, hr), :]
        first = [copy(j, my_half, _half(out_ref, 2 * x + y, c), (*chip, c)) for j, chip in enumerate(chips)]
        for cp in first:
            cp.start()
        passed = [copy(3 + j, _half(out_ref, _chip_of(chip), c), _half(out_ref, _chip_of(chip), c), sibling)
                  for j, chip in enumerate(chips)]
        for j, chip in enumerate(chips):
            copy(j, my_half, _half(out_ref, _chip_of(chip), c), (*chip, c)).wait_recv()
            passed[j].start()
        for j, chip in enumerate(chips):
            copy(3 + j, my_half, _half(out_ref, _chip_of(chip), 1 - c), sibling).wait_recv()
        for cp in first + passed:
            cp.wait_send()

    gathered = pl.pallas_call(
        body, name="all_gather_now", out_shape=jax.ShapeDtypeStruct((N_CHIPS, rows, cols), slab.dtype),
        in_specs=[ANY], out_specs=ANY,
        scratch_shapes=[pltpu.SemaphoreType.DMA((6,)), pltpu.SemaphoreType.DMA((6,))],
    )(slab)
    me = 2 * lax.axis_index("x") + lax.axis_index("y")
    return _put_own(gathered, slab, me).reshape(N_CHIPS * rows, cols)


def _gather_copies(slab_refs, land_refs, send_sems, recv_sems):
    x, y, c, chips = _place()
    sends, recvs = [], []
    for k, (src, land) in enumerate(zip(slab_refs, land_refs)):
        hr = src.shape[0] // 2
        for j, chip in enumerate(chips):
            for t in range(2):
                sends.append(pltpu.make_async_remote_copy(
                    src_ref=src.at[pl.ds(c * hr, hr), :], dst_ref=_half(land, 2 * x + y, c),
                    send_sem=send_sems.at[6 * k + 2 * j + t], recv_sem=recv_sems.at[6 * k + 2 * j + c],
                    device_id=(*chip, t), device_id_type=MESH))
                recvs.append(pltpu.make_async_remote_copy(
                    src_ref=src.at[pl.ds(t * hr, hr), :], dst_ref=_half(land, _chip_of(chip), t),
                    send_sem=send_sems.at[6 * k + 2 * j + t], recv_sem=recv_sems.at[6 * k + 2 * j + t],
                    device_id=(*chip, t), device_id_type=MESH))
    return sends, recvs


def _all_gather_start(slabs, after):
    n = len(slabs)

    def body(*refs):
        slab_refs, land_refs = refs[:n], refs[n:2 * n]
        send_sems, recv_sems = refs[2 * n + 1:2 * n + 3]
        token = refs[-1]
        sends, _ = _gather_copies(slab_refs, land_refs, send_sems, recv_sems)
        for cp in sends:
            cp.start()
        token[...] = jnp.zeros_like(token)

    lands = [lax.empty((N_CHIPS,) + s.shape, s.dtype) for s in slabs]
    hbm = lambda a: pltpu.HBM(a.shape, a.dtype)
    res = pl.pallas_call(
        body, name="all_gather_start",
        out_shape=(pltpu.SemaphoreType.DMA((6 * n,)), pltpu.SemaphoreType.DMA((6 * n,)), *map(hbm, slabs),
                   *map(hbm, lands), jax.ShapeDtypeStruct((8, LANES), F32)),
        in_specs=[ANY] * (2 * n + 1),
        out_specs=(SEM, SEM, *([ANY] * (2 * n)), pl.BlockSpec(memory_space=pltpu.VMEM)),
        input_output_aliases={i: 2 + i for i in range(2 * n)}, compiler_params=SPLIT_COPY,
    )(*[pltpu.with_memory_space_constraint(a, pltpu.HBM) for a in list(slabs) + lands], after)
    return res[:-1], res[-1]


def _all_gather_wait(handle, after):
    send_sems, recv_sems = handle[:2]
    n = (len(handle) - 2) // 2
    slabs, lands = handle[2:2 + n], handle[2 + n:]

    def body(*refs):
        slab_refs, land_refs = refs[:n], refs[n:2 * n]
        send_sems, recv_sems = refs[2 * n:2 * n + 2]
        sends, recvs = _gather_copies(slab_refs, land_refs, send_sems, recv_sems)
        for cp in sends:
            cp.wait_send()
        for cp in recvs:
            cp.wait_recv()

    hbm = lambda a: pltpu.HBM(a.shape, a.dtype)
    res = pl.pallas_call(
        body, name="all_gather_wait", out_shape=tuple(map(hbm, list(slabs) + list(lands))),
        in_specs=[ANY] * (2 * n) + [SEM, SEM, ANY], out_specs=tuple([ANY] * (2 * n)),
        input_output_aliases={i: i for i in range(2 * n)}, compiler_params=SPLIT_COPY,
    )(*slabs, *lands, send_sems, recv_sems, after)
    me = 2 * lax.axis_index("x") + lax.axis_index("y")
    return [_put_own(land, slab, me).reshape(N_CHIPS * slab.shape[0], slab.shape[1])
            for slab, land in zip(res[:n], res[n:])]


def _scatter_copies(part_refs, land_refs, send_sems, recv_sems):
    x, y, c, chips = _place()
    me = 4 * x + 2 * y + c
    sends, recvs = [], []
    for k, (part, land) in enumerate(zip(part_refs, land_refs)):
        for j, chip in enumerate(chips):
            for t in range(2):
                sends.append(pltpu.make_async_remote_copy(
                    src_ref=part.at[_chip_of(chip)], dst_ref=land.at[me],
                    send_sem=send_sems.at[7 * k + 2 * j + t], recv_sem=recv_sems.at[7 * k + 2 * j + c],
                    device_id=(*chip, t), device_id_type=MESH))
                recvs.append(pltpu.make_async_remote_copy(
                    src_ref=part.at[_chip_of(chip)], dst_ref=land.at[2 * _chip_of(chip) + t],
                    send_sem=send_sems.at[7 * k + 2 * j + t], recv_sem=recv_sems.at[7 * k + 2 * j + t],
                    device_id=(*chip, t), device_id_type=MESH))
        sends.append(pltpu.make_async_remote_copy(
            src_ref=part.at[2 * x + y], dst_ref=land.at[me], send_sem=send_sems.at[7 * k + 6],
            recv_sem=recv_sems.at[7 * k + 6], device_id=(x, y, 1 - c), device_id_type=MESH))
        recvs.append(pltpu.make_async_remote_copy(
            src_ref=part.at[2 * x + y], dst_ref=land.at[4 * x + 2 * y + 1 - c],
            send_sem=send_sems.at[7 * k + 6], recv_sem=recv_sems.at[7 * k + 6], device_id=(x, y, 1 - c),
            device_id_type=MESH))
    return sends, recvs


def _reduce_scatter_start(parts, name):
    n = len(parts)
    parts = [p.reshape(N_CHIPS, p.shape[0] // N_CHIPS, p.shape[1]) for p in parts]

    def body(*refs):
        part_refs, land_refs = refs[:n], refs[n:2 * n]
        send_sems, recv_sems = refs[2 * n:2 * n + 2]
        token = refs[-1]
        sends, _ = _scatter_copies(part_refs, land_refs, send_sems, recv_sems)
        for cp in sends:
            cp.start()
        token[...] = jnp.zeros_like(token)

    lands = [lax.empty((N_DEV, p.shape[1], p.shape[2]), p.dtype) for p in parts]
    hbm = lambda a: pltpu.HBM(a.shape, a.dtype)
    res = pl.pallas_call(
        body, name=name,
        out_shape=(pltpu.SemaphoreType.DMA((7 * n,)), pltpu.SemaphoreType.DMA((7 * n,)), *map(hbm, parts),
                   *map(hbm, lands), jax.ShapeDtypeStruct((8, LANES), F32)),
        in_specs=[ANY] * (2 * n), out_specs=(SEM, SEM, *([ANY] * (2 * n)), pl.BlockSpec(memory_space=pltpu.VMEM)),
        input_output_aliases={i: 2 + i for i in range(2 * n)}, compiler_params=SPLIT_COPY,
    )(*[pltpu.with_memory_space_constraint(a, pltpu.HBM) for a in parts + lands])
    return res[:-1], res[-1]


def _reduce_scatter_wait(handle, after, name):
    send_sems, recv_sems = handle[:2]
    n = (len(handle) - 2) // 2
    parts, lands = handle[2:2 + n], handle[2 + n:]

    def body(*refs):
        part_refs, land_refs = refs[:n], refs[n:2 * n]
        send_sems, recv_sems = refs[2 * n:2 * n + 2]
        sends, recvs = _scatter_copies(part_refs, land_refs, send_sems, recv_sems)
        for cp in sends:
            cp.wait_send()
        for cp in recvs:
            cp.wait_recv()

    hbm = lambda a: pltpu.HBM(a.shape, a.dtype)
    res = pl.pallas_call(
        body, name=name, out_shape=tuple(map(hbm, list(parts) + list(lands))),
        in_specs=[ANY] * (2 * n) + [SEM, SEM, ANY], out_specs=tuple([ANY] * (2 * n)),
        input_output_aliases={i: i for i in range(2 * n)}, compiler_params=SPLIT_COPY,
    )(*parts, *lands, send_sems, recv_sems, after)
    return list(zip(res[:n], res[n:]))


def _small_all_reduce(block):
    rows = block.shape[0]

    def body(x_ref, all_ref, sum_ref, send_sems, recv_sems, local_sem):
        x, y, c, chips = _place()
        me, sibling = (x, y, c), (x, y, 1 - c)

        def blk(px, py, pc):
            return all_ref.at[pl.ds((4 * px + 2 * py + pc) * rows, rows), :]

        def copy(k, who, to, src=None):
            return pltpu.make_async_remote_copy(
                src_ref=blk(*who) if src is None else src, dst_ref=blk(*who), send_sem=send_sems.at[k],
                recv_sem=recv_sems.at[k], device_id=to, device_id_type=MESH)

        mine = pltpu.make_async_copy(x_ref, blk(*me), local_sem)
        mine.start()
        first = [copy(0, me, sibling, src=x_ref)]
        first += [copy(1 + j, me, (*chip, c), src=x_ref) for j, chip in enumerate(chips)]
        for cp in first:
            cp.start()
        passed = [copy(4 + j, (*chip, c), sibling) for j, chip in enumerate(chips)]
        for j, chip in enumerate(chips):
            copy(1 + j, (*chip, c), me).wait_recv()
            passed[j].start()
        copy(0, sibling, me).wait_recv()
        for j, chip in enumerate(chips):
            copy(4 + j, (*chip, 1 - c), me).wait_recv()
        for cp in first + passed:
            cp.wait_send()
        mine.wait()
        acc = all_ref[pl.ds(0, rows), :]
        for dev in range(1, N_DEV):
            acc = acc + all_ref[pl.ds(dev * rows, rows), :]
        sum_ref[...] = acc

    vmem = pl.BlockSpec(memory_space=pltpu.VMEM)
    return pl.pallas_call(
        body, name="small_all_reduce",
        out_shape=[jax.ShapeDtypeStruct((N_DEV * rows, D_MODEL), F32), jax.ShapeDtypeStruct((rows, D_MODEL), F32)],
        in_specs=[vmem], out_specs=[vmem, vmem],
        scratch_shapes=[pltpu.SemaphoreType.DMA((7,)), pltpu.SemaphoreType.DMA((7,)), pltpu.SemaphoreType.DMA],
    )(block)[1]


def _adamw(w, g, m, v, name):
    rows, cols = w.shape
    tm = rows
    if rows > 512:
        tm = next(t for t in range(512, 7, -8) if rows % t == 0)

    def body(w_ref, g_ref, m_ref, v_ref, d_ref, nm_ref, nv_ref):
        g_ = g_ref[...]
        m_ = ADAM_B1 * m_ref[...] + (1.0 - ADAM_B1) * g_
        v_ = ADAM_B2 * v_ref[...] + (1.0 - ADAM_B2) * (g_ * g_)
        m_hat = m_ / (1.0 - ADAM_B1 ** ADAM_STEP)
        v_hat = v_ / (1.0 - ADAM_B2 ** ADAM_STEP)
        d_ref[...] = -ADAM_LR * (m_hat / (jnp.sqrt(v_hat) + ADAM_EPS) + ADAM_WD * w_ref[...])
        nm_ref[...] = m_
        nv_ref[...] = v_

    spec = pl.BlockSpec((tm, cols), lambda i: (i, 0))
    return pl.pallas_call(
        body, name=name, grid=(rows // tm,), in_specs=[spec] * 4, out_specs=[spec] * 3,
        out_shape=[jax.ShapeDtypeStruct(w.shape, F32)] * 3, compiler_params=_params(("arbitrary",)),
    )(w, g, m, v)


def _adamw_of_shares(w, own, land, m, v, name):
    rows, cols = w.shape
    tm = rows // 4
    x, y, c = lax.axis_index("x"), lax.axis_index("y"), lax.axis_index("c")
    where = jnp.stack([2 * x + y, 4 * x + 2 * y + c]).astype(jnp.int32)

    def body(where_ref, w_ref, own_ref, land_ref, m_ref, v_ref, g_ref, d_ref, nm_ref, nv_ref):
        me = where_ref[1]
        g_ = jnp.zeros((tm, cols), F32)
        for dev in range(N_DEV):
            g_ = g_ + jnp.where(me == dev, own_ref[0], land_ref[dev]).astype(F32)
        m_ = ADAM_B1 * m_ref[...] + (1.0 - ADAM_B1) * g_
        v_ = ADAM_B2 * v_ref[...] + (1.0 - ADAM_B2) * (g_ * g_)
        m_hat = m_ / (1.0 - ADAM_B1 ** ADAM_STEP)
        v_hat = v_ / (1.0 - ADAM_B2 ** ADAM_STEP)
        g_ref[...] = g_
        d_ref[...] = -ADAM_LR * (m_hat / (jnp.sqrt(v_hat) + ADAM_EPS) + ADAM_WD * w_ref[...])
        nm_ref[...] = m_
        nv_ref[...] = v_

    tile = pl.BlockSpec((tm, cols), lambda i, where_ref: (i, 0))
    spec = pltpu.PrefetchScalarGridSpec(
        num_scalar_prefetch=1, grid=(rows // tm,),
        in_specs=[tile, pl.BlockSpec((1, tm, cols), lambda i, where_ref: (where_ref[0], i, 0)),
                  pl.BlockSpec((N_DEV, tm, cols), lambda i, where_ref: (0, i, 0)), tile, tile],
        out_specs=[tile] * 4)
    return pl.pallas_call(
        body, name=name, grid_spec=spec, out_shape=[jax.ShapeDtypeStruct(w.shape, F32)] * 4,
        compiler_params=_params(("arbitrary",)),
    )(where, w, own, land, m, v)


def _pack_small(values):
    flat = jnp.concatenate([values[n].reshape(-1).astype(F32) for n in SMALL])
    return jnp.pad(flat, (0, SMALL_ROWS * D_MODEL - flat.shape[0])).reshape(SMALL_ROWS, D_MODEL)


def _unpack_small(block, shapes):
    flat = block.reshape(-1)
    out, lo = {}, 0
    for n in SMALL:
        out[n] = flat[lo:lo + SMALL_SIZES[n]].reshape(shapes[n])
        lo += SMALL_SIZES[n]
    return out


def _after(token, a):
    return a + token[:1, :1].astype(a.dtype)


def kernel(x, p, mix_norm_g, w_in, sgu_w, sgu_b, sgu_norm_g, out_norm_a, out_norm_b, w_out, ffn_norm_g, w_gate, w_up, w_down, ple_norm_g, w_ple_gate, w_ple_proj, final_norm_g, loss_target, m_mix_norm_g, m_w_in, m_sgu_w, m_sgu_b, m_sgu_norm_g, m_out_norm_a, m_out_norm_b, m_w_out, m_ffn_norm_g, m_w_gate, m_w_up, m_w_down, m_ple_norm_g, m_w_ple_gate, m_w_ple_proj, m_final_norm_g, v_mix_norm_g, v_w_in, v_sgu_w, v_sgu_b, v_sgu_norm_g, v_out_norm_a, v_out_norm_b, v_w_out, v_ffn_norm_g, v_w_gate, v_w_up, v_w_down, v_ple_norm_g, v_w_ple_gate, v_w_ple_proj, v_final_norm_g):
    given = dict(locals())
    drop_lead = lambda a, lead: a.reshape(a.shape[lead:])
    xs, ps, target = drop_lead(x, 1), drop_lead(p, 2), drop_lead(loss_target, 1)
    s = xs.shape[0]
    shard = lambda name: drop_lead(given[name], 1)

    def slab_of(name):
        local = shard(name).astype(MXU_DTYPE)
        return local.T if SLAB_IS_TRANSPOSED[name] else local

    w_in_t = _all_gather_now(slab_of('w_in'))
    later = ['w_out', 'w_gate', 'w_up', 'w_down', 'w_ple_gate', 'w_ple_proj']
    gather, token = _all_gather_start([slab_of(n) for n in later], w_in_t)

    cos_t, sin_t = _rope_tables(s)
    tril = jnp.tril(jnp.ones((CHUNK, CHUNK), F32))
    w_tril = (sgu_w.reshape(HEADS_A, CHUNK, CHUNK) * tril).astype(MXU_DTYPE)
    w_tril_t = jnp.swapaxes(w_tril, 1, 2)
    bias = jnp.repeat(sgu_b.reshape(HEADS_A, CHUNK).T, HEAD_DIM, axis=1)
    g = {n: given[n].reshape(1, -1) for n in SMALL if n not in ('sgu_w', 'sgu_b')}

    uv, q, k, v, hn1 = _in_fwd(xs, _after(token, g['mix_norm_g']), w_in_t.T, cos_t, sin_t)
    ya_n = _sgu_fwd(uv, w_tril, bias, g['sgu_norm_g'], g['out_norm_a'])
    branches = [_attn_fwd_branch(q, k, v, dil) for dil in DILATIONS]
    y_b, lse = _attn_merge([o for o, _ in branches], [l for _, l in branches])
    stacks = dict(zip(later, _all_gather_wait(gather, lse)))
    w_gate_t, w_up_t, w_pp_t = stacks['w_gate'], stacks['w_up'], stacks['w_ple_proj']
    h1, y_n = _out_fwd(ya_n, y_b, g['out_norm_b'], stacks['w_out'], xs)
    h2, gate, up, hn2 = _ffn_fwd(h1, g['ffn_norm_g'], w_gate_t.T, w_up_t.T, stacks['w_down'])
    loss, dh2, dz, dpp, hn3, d_ple_g, d_final_g = _ple_loss(
        h2, ps, target, g['ple_norm_g'], stacks['w_ple_gate'], stacks['w_ple_gate'].T, w_pp_t.T, g['final_norm_g'])

    share = {}
    share['w_ple_gate'] = _wgrad(hn3, dz, "wgrad_ple_gate")
    share['w_ple_proj'] = _wgrad(dpp, ps, "wgrad_ple_proj")
    scatter_1, token = _reduce_scatter_start([share['w_ple_gate'], share['w_ple_proj']], "reduce_scatter_start_1")
    dh1, act, dgate, dup, d_ffn_g = _ffn_bwd(dh2, h1, gate, up, _after(token, g['ffn_norm_g']), stacks['w_down'].T,
                                             w_gate_t, w_up_t)
    share['w_down'] = _wgrad(act, dh2, "wgrad_down")
    share['w_gate'] = _wgrad(dgate, hn2, "wgrad_gate")
    share['w_up'] = _wgrad(dup, hn2, "wgrad_up")
    scatter_2, token = _reduce_scatter_start([share['w_down'], share['w_gate'], share['w_up']],
                                             "reduce_scatter_start_2")
    dya_n, dyb, d_out_b = _out_bwd(dh1, y_b, _after(token, g['out_norm_b']), stacks['w_out'].T)
    share['w_out'] = _wgrad(y_n, dh1, "wgrad_out")
    scatter_3, token = _reduce_scatter_start([share['w_out']], "reduce_scatter_start_3")
    grads = None
    for dil in DILATIONS:
        grads = _attn_bwd_branch(q, k, v, dyb, y_b, lse, grads, dil)
    duv, d_sgu_w, d_sgu_b, d_sgu_g, d_out_a = _sgu_bwd(uv, dya_n, w_tril, w_tril_t, bias,
                                                       _after(token, g['sgu_norm_g']), g['out_norm_a'])
    dproj = _in_bwd_proj(duv, grads[0], grads[1], grads[2], cos_t, sin_t)
    share['w_in'] = _wgrad(dproj, hn1, "wgrad_in")
    scatter_4, token = _reduce_scatter_start([share['w_in']], "reduce_scatter_start_4")

    grads, deltas, new_m, new_v = {}, {}, {}, {}
    add_lead = lambda a: a.reshape((1,) + a.shape)

    def finish(names, handles, after, tag):
        landed = []
        for i, handle in enumerate(handles):
            landed += _reduce_scatter_wait(handle, after, "reduce_scatter_wait_%s%d" % (tag, i))
        for n, (own, land) in zip(names, landed):
            turn = (lambda a: a.T) if SLAB_IS_TRANSPOSED[n] else (lambda a: a)
            res = _adamw_of_shares(turn(shard(n)), own, land, turn(shard("m_" + n)), turn(shard("v_" + n)),
                                   "adamw_" + n)
            grads[n], deltas[n], new_m[n], new_v[n] = (add_lead(turn(a)) for a in res)

    finish(['w_ple_gate', 'w_ple_proj', 'w_down', 'w_gate', 'w_up', 'w_out'], [scatter_1, scatter_2, scatter_3], token,
           "early")
    grad_x, d_mix_g = _in_bwd_x(dproj, w_in_t, xs, g['mix_norm_g'], dh1,
                                after=[new_v[n] for n in ('w_down', 'w_gate', 'w_up', 'w_out')])

    gs = {'mix_norm_g': d_mix_g, 'sgu_w': d_sgu_w, 'sgu_b': d_sgu_b[:, :HEADS_A].T, 'sgu_norm_g': d_sgu_g,
          'out_norm_a': d_out_a, 'out_norm_b': d_out_b, 'ffn_norm_g': d_ffn_g, 'ple_norm_g': d_ple_g,
          'final_norm_g': d_final_g}
    gs_block = _pack_small(gs).at[SMALL_ROWS - 1, 0].set(loss[0, 0])
    small_sum = _small_all_reduce(gs_block)
    loss_out = small_sum[SMALL_ROWS - 1, 0]
    small_shapes = {n: given[n].shape for n in SMALL}
    finish(['w_in'], [scatter_4], small_sum, "last")

    small = {n: given[n] for n in SMALL}
    d, nm, nv = _adamw(_pack_small(small), small_sum, _pack_small({n: given["m_" + n] for n in SMALL}),
                       _pack_small({n: given["v_" + n] for n in SMALL}), "adamw_small")
    for res, blk in ((grads, small_sum), (deltas, d), (new_m, nm), (new_v, nv)):
        res.update(_unpack_small(blk, small_shapes))

    outs = [loss_out, add_lead(grad_x)]
    for res in (grads, deltas, new_m, new_v):
        outs += [res[n] for n in WEIGHT_NAMES]
    return tuple(outs)
```

```python
import functools
import math

import jax
import jax.numpy as jnp
import numpy as np
from jax import lax
from jax.experimental import pallas as pl
from jax.experimental.pallas import tpu as pltpu

F32 = jnp.float32
MXU_DTYPE = jnp.bfloat16

D_MODEL = 1024
HEAD_DIM = 64
HEADS_A = 4
HEADS_B = 12
WIDTH_A = HEADS_A * HEAD_DIM
WIDTH_B = HEADS_B * HEAD_DIM
CHUNK = 128
BLOCK = 128
DILATIONS = (1, 4, 16)
ROPE_THETA = 10000.0
D_FF = 2816
FF_HALF = D_FF // 2
FF_STRIP = FF_HALF
FF_STRIPS = ((0, 1024), (1024, 2048), (2048, D_FF))
PLE_DIM = 256
IN_COLS = 2 * WIDTH_A + 3 * WIDTH_B
EPS = 1e-6
LANES = 128
N_CHIPS = 4
N_DEV = 8

ADAM_LR = 0.001
ADAM_B1 = 0.9
ADAM_B2 = 0.999
ADAM_EPS = 1e-08
ADAM_WD = 0.01
ADAM_STEP = 10

VMEM_LIMIT = 56 * 1024 * 1024

WEIGHT_NAMES = ['mix_norm_g', 'w_in', 'sgu_w', 'sgu_b', 'sgu_norm_g', 'out_norm_a', 'out_norm_b', 'w_out',
                'ffn_norm_g', 'w_gate', 'w_up', 'w_down', 'ple_norm_g', 'w_ple_gate', 'w_ple_proj', 'final_norm_g']
SHARDED = ['w_in', 'w_out', 'w_gate', 'w_up', 'w_down', 'w_ple_gate', 'w_ple_proj']
SMALL = ['mix_norm_g', 'sgu_w', 'sgu_b', 'sgu_norm_g', 'out_norm_a', 'out_norm_b', 'ffn_norm_g', 'ple_norm_g',
         'final_norm_g']
SMALL_SIZES = {'mix_norm_g': 1024, 'sgu_w': 65536, 'sgu_b': 512, 'sgu_norm_g': 256, 'out_norm_a': 256,
               'out_norm_b': 768, 'ffn_norm_g': 1024, 'ple_norm_g': 1024, 'final_norm_g': 1024}
SMALL_ROWS = 72


def _params(semantics=None):
    return pltpu.CompilerParams(dimension_semantics=semantics, vmem_limit_bytes=VMEM_LIMIT)


def _full(shape):
    nd = len(shape)
    return pl.BlockSpec(shape, lambda i: (0,) * nd, pipeline_mode=pl.Buffered(1))


def _rows(tm, width):
    return pl.BlockSpec((tm, width), lambda i: (i, 0))


def _rms_stats(x):
    r = lax.rsqrt(jnp.mean(x * x, axis=-1, keepdims=True) + EPS)
    return x * r, r


def _rms_bwd(dn, n, r):
    return r * (dn - n * jnp.mean(dn * n, axis=-1, keepdims=True))


def _dot(a, b):
    return jnp.dot(a, b, preferred_element_type=F32)


def _dot_nt(a, b):
    return lax.dot_general(a, b, (((1,), (1,)), ((), ())), preferred_element_type=F32)


def _dot_tn(a, b):
    return lax.dot_general(a, b, (((0,), (0,)), ((), ())), preferred_element_type=F32)


def _gelu_parts(x):
    c = math.sqrt(2.0 / math.pi)
    t = jnp.tanh(c * (x + 0.044715 * x * x * x))
    return 0.5 * x * (1.0 + t), t


def _gelu_grad(x, t):
    c = math.sqrt(2.0 / math.pi)
    return 0.5 * (1.0 + t) + 0.5 * x * (1.0 - t * t) * c * (1.0 + 3.0 * 0.044715 * x * x)


def _half_masks(dtype):
    lane = lax.broadcasted_iota(jnp.int32, (BLOCK, LANES), 1)
    lo = (lane < HEAD_DIM).astype(F32)
    return lo.astype(dtype), (1.0 - lo).astype(dtype)


def _rope_partner(t):
    lane = lax.broadcasted_iota(jnp.int32, t.shape, 1)
    first_half = (lane % HEAD_DIM) < (HEAD_DIM // 2)
    return jnp.where(first_half, pltpu.roll(t, LANES - HEAD_DIM // 2, 1), pltpu.roll(t, HEAD_DIM // 2, 1))


PAIRS_ABREAST = 2
L_BLOCK = 256
L_GROUP = 16


def _store_l256(scr, out_ref, cols, value):
    tm = value.shape[0]
    scr[...] = value
    for blk in range(tm // L_BLOCK):
        for r in range(L_GROUP):
            lo = blk * L_BLOCK + r * L_GROUP
            piece = scr[pl.ds(blk * L_BLOCK + r, L_GROUP, stride=L_GROUP), :]
            out_ref[lo:lo + L_GROUP, cols] = piece.astype(out_ref.dtype)


def _load_l256(col_refs, tm):
    cols = []
    for ref in col_refs:
        pieces = [ref[pl.ds(blk * L_BLOCK + i, L_GROUP, stride=L_GROUP), :]
                  for blk in range(tm // L_BLOCK) for i in range(L_GROUP)]
        cols.append(jnp.concatenate(pieces, axis=0))
    return jnp.concatenate(cols, axis=1)


def _col_specs(tm, width):
    return [pl.BlockSpec((tm, LANES), lambda i, j=j: (i, j)) for j in range(width // LANES)]


def _in_fwd(x, g_mix, w_in_t, cos_t, sin_t):
    s = x.shape[0]
    tm = 512

    def body(x_ref, g_ref, wt_ref, cos_ref, sin_ref, uv_ref, q_ref, k_ref, v_ref, hn_ref, *scrs):
        n, _ = _rms_stats(x_ref[...])
        hn = (n * g_ref[...]).astype(MXU_DTYPE)
        hn_ref[...] = hn
        cos = cos_ref[...]
        sin = sin_ref[...]
        strip = 2 * LANES
        for j in range(IN_COLS // strip):
            proj = _dot_nt(hn, wt_ref[j * strip:(j + 1) * strip, :])
            lo = j * strip - 2 * WIDTH_A
            if lo < 0:
                uv_ref[:, j * strip:(j + 1) * strip] = proj
                continue
            which, lo = divmod(lo, WIDTH_B)
            for i in range(strip // LANES):
                t = proj[:, i * LANES:(i + 1) * LANES]
                cols = slice(lo + i * LANES, lo + (i + 1) * LANES)
                scr = scrs[i]
                if which == 0:
                    _store_l256(scr, q_ref, cols, (t * cos + _rope_partner(t) * sin) * (HEAD_DIM ** -0.5))
                elif which == 1:
                    _store_l256(scr, k_ref, cols, t * cos + _rope_partner(t) * sin)
                else:
                    _store_l256(scr, v_ref, cols, t)

    return pl.pallas_call(
        body, name="in_fwd", grid=(s // tm,), scratch_shapes=[pltpu.VMEM((tm, LANES), F32)] * 2,
        in_specs=[_rows(tm, D_MODEL), _full((1, D_MODEL)), _full((IN_COLS, D_MODEL)), _rows(tm, LANES),
                  _rows(tm, LANES)],
        out_specs=[_rows(tm, 2 * WIDTH_A), _rows(tm, WIDTH_B), _rows(tm, WIDTH_B), _rows(tm, WIDTH_B),
                   _rows(tm, D_MODEL)],
        out_shape=[jax.ShapeDtypeStruct((s, 2 * WIDTH_A), F32), jax.ShapeDtypeStruct((s, WIDTH_B), MXU_DTYPE),
                   jax.ShapeDtypeStruct((s, WIDTH_B), MXU_DTYPE), jax.ShapeDtypeStruct((s, WIDTH_B), MXU_DTYPE),
                   jax.ShapeDtypeStruct((s, D_MODEL), MXU_DTYPE)],
        compiler_params=_params(("arbitrary",)),
    )(x, g_mix, w_in_t, cos_t, sin_t)


class _Branch:
    def __init__(self, dil, s, qn=BLOCK):
        self.dil = dil
        i = np.arange(L_GROUP)
        if dil == 16:
            nblk = qn // 16
            self.grid = (16, s // (L_BLOCK * nblk))
            self.shape = (nblk, 1, 1, L_GROUP)
            self.index = lambda r, n: (n, r // 4, r % 4, 0, 0)
            pos = (np.arange(nblk)[:, None] * 16 + i[None, :]).reshape(-1)
        elif dil == 4:
            nblk = qn // 64
            self.grid = (4, s // (L_BLOCK * nblk))
            self.shape = (nblk, 4, 1, L_GROUP)
            self.index = lambda r, n: (n, 0, r, 0, 0)
            pos = (np.arange(nblk)[:, None, None] * 64 + np.arange(4)[None, :, None]
                   + 4 * i[None, None, :]).reshape(-1)
        else:
            self.grid = (1, s // L_BLOCK)
            self.shape = (1, 4, 4, L_GROUP)
            self.index = lambda r, n: (n, 0, 0, 0, 0)
            pos = (np.arange(16)[:, None] + 16 * i[None, :]).reshape(-1)
        self.qn = pos.shape[0]
        self.nb = self.grid[1]
        dist = pos[:, None] - np.concatenate([pos - self.qn, pos])[None, :]
        band = (dist >= 0) & (dist <= BLOCK)
        start = band & (np.arange(2 * self.qn)[None, :] >= self.qn)
        self.bias = np.where(np.stack([band, start]), 0.0, -np.inf).astype(np.float32)

    def view(self, a):
        return a.reshape(a.shape[0] // L_BLOCK, 4, 4, L_GROUP, a.shape[1])

    def spec(self, w, step=lambda n: n):
        return pl.BlockSpec(self.shape + (w,), lambda r, n: self.index(r, step(n)))

    def bias_spec(self, step=lambda n: n):
        return pl.BlockSpec((1, self.qn, 2 * self.qn), lambda r, n: (jnp.where(step(n) == 0, 1, 0), 0, 0))

    def load(self, ref, cols=slice(None)):
        x = ref[:, :, :, :, cols]
        return x.reshape(self.qn, x.shape[-1])

    def store(self, ref, cols, value):
        ref[:, :, :, :, cols] = value.reshape(self.shape + (value.shape[-1],))


def _attn_fwd_branch(q, k, v, dil):
    s = q.shape[0]
    br = _Branch(dil, s)
    qn = br.qn

    def body(bias_ref, q_ref, kc_ref, kp_ref, vc_ref, vp_ref, o_ref, lse_ref):
        bias2 = jnp.concatenate([bias_ref[0], bias_ref[0]], axis=0)
        lo = lax.broadcasted_iota(jnp.int32, (qn, LANES), 1) < HEAD_DIM
        mask_lo = lo.astype(F32).astype(MXU_DTYPE)
        for hp in range(HEADS_B // 2):
            cols = slice(hp * LANES, (hp + 1) * LANES)
            qp = br.load(q_ref, cols)
            kcat = jnp.concatenate([br.load(kp_ref, cols), br.load(kc_ref, cols)], axis=0)
            vcat = jnp.concatenate([br.load(vp_ref, cols), br.load(vc_ref, cols)], axis=0)
            sc = _dot_nt(jnp.concatenate([qp * mask_lo, qp * (1 - mask_lo)], axis=0), kcat) + bias2
            m = jnp.max(sc, axis=1, keepdims=True)
            p = jnp.exp(sc - m)
            l = jnp.sum(p, axis=1, keepdims=True)
            out = _dot(p.astype(MXU_DTYPE), vcat) / l
            lse = m + jnp.log(l)
            br.store(o_ref, cols, jnp.where(lo, out[:qn], out[qn:]))
            br.store(lse_ref, cols, jnp.where(lo, lse[:qn], lse[qn:]))

    before = lambda n: jnp.maximum(n - 1, 0)
    res = pl.pallas_call(
        body, name="attn_fwd_d%d" % dil, grid=br.grid,
        in_specs=[br.bias_spec(), br.spec(WIDTH_B), br.spec(WIDTH_B), br.spec(WIDTH_B, before), br.spec(WIDTH_B),
                  br.spec(WIDTH_B, before)],
        out_specs=[br.spec(WIDTH_B), br.spec(WIDTH_B)],
        out_shape=[jax.ShapeDtypeStruct((s // L_BLOCK, 4, 4, L_GROUP, WIDTH_B), F32)] * 2,
        compiler_params=_params(("arbitrary", "arbitrary")),
    )(jnp.asarray(br.bias), br.view(q), br.view(k), br.view(k), br.view(v), br.view(v))
    return tuple(a.reshape(s, WIDTH_B) for a in res)


def _attn_merge(outs, lses):
    s = outs[0].shape[0]
    tm = 512
    nbr = len(outs)

    def body(*refs):
        o_refs, l_refs, y_ref, lse_ref = refs[:nbr], refs[nbr:2 * nbr], refs[2 * nbr], refs[2 * nbr + 1]
        ls = [r[...] for r in l_refs]
        top = functools.reduce(jnp.maximum, ls)
        ws = [jnp.exp(l - top) for l in ls]
        den = functools.reduce(jnp.add, ws)
        num = functools.reduce(jnp.add, [w * r[...] for w, r in zip(ws, o_refs)])
        y_ref[...] = num / den
        lse_ref[...] = top + jnp.log(den)

    return pl.pallas_call(
        body, name="attn_merge", grid=(s // tm,), in_specs=[_rows(tm, WIDTH_B)] * (2 * nbr),
        out_specs=[_rows(tm, WIDTH_B)] * 2, out_shape=[jax.ShapeDtypeStruct((s, WIDTH_B), F32)] * 2,
        compiler_params=_params(("arbitrary",)),
    )(*outs, *lses)


def _sgu_forward_tile(uv, w_ref, bias, g_sgu):
    tm = uv.shape[0]
    u = uv[:, :WIDTH_A]
    v = uv[:, WIDTH_A:]
    ug, tu = _gelu_parts(u)
    vg, tv = _gelu_parts(v)
    mu = jnp.mean(vg, axis=-1, keepdims=True)
    vc = vg - mu
    rs = lax.rsqrt(jnp.mean(vc * vc, axis=-1, keepdims=True) + EPS)
    vhat = vc * rs
    vn = (vhat * g_sgu).astype(MXU_DTYPE)
    masks = _half_masks(MXU_DTYPE)
    chunks = []
    for c in range(tm // CHUNK):
        rows = slice(c * CHUNK, (c + 1) * CHUNK)
        groups = []
        for gp in range(2):
            vn_g = vn[rows, gp * LANES:(gp + 1) * LANES]
            groups.append(_dot(w_ref[2 * gp], vn_g * masks[0]) + _dot(w_ref[2 * gp + 1], vn_g * masks[1]))
        chunks.append(jnp.concatenate(groups, axis=1) + bias)
    mixed = jnp.concatenate(chunks, axis=0)
    return dict(u=u, v=v, ug=ug, tu=tu, tv=tv, rs=rs, vhat=vhat, vn=vn, mixed=mixed, ya=ug * mixed)


def _sgu_fwd(uv, w_tril, bias, g_sgu, g_a):
    s = uv.shape[0]
    tm = 512

    def body(uv_ref, w_ref, b_ref, gs_ref, ga_ref, o_ref):
        t = _sgu_forward_tile(uv_ref[...], w_ref, b_ref[...], gs_ref[...])
        n, _ = _rms_stats(t['ya'])
        o_ref[...] = (n * ga_ref[...]).astype(MXU_DTYPE)

    return pl.pallas_call(
        body, name="sgu_fwd", grid=(s // tm,),
        in_specs=[_rows(tm, 2 * WIDTH_A), _full((HEADS_A, CHUNK, CHUNK)), _full((CHUNK, WIDTH_A)),
                  _full((1, WIDTH_A)), _full((1, WIDTH_A))],
        out_specs=_rows(tm, WIDTH_A), out_shape=jax.ShapeDtypeStruct((s, WIDTH_A), MXU_DTYPE),
        compiler_params=_params(("arbitrary",)),
    )(uv, w_tril, bias, g_sgu, g_a)


def _out_fwd(ya_n, y_b, g_b, w_out, x):
    s = x.shape[0]
    tm = 512
    nc = WIDTH_B // LANES

    def body(ya_ref, *refs):
        yb_refs = refs[:nc]
        g_ref, w_ref, x_ref, h_ref, yn_ref = refs[nc:]
        n, _ = _rms_stats(_load_l256(yb_refs, tm))
        yn = jnp.concatenate([ya_ref[...], (n * g_ref[...]).astype(MXU_DTYPE)], axis=1)
        yn_ref[...] = yn
        h_ref[...] = x_ref[...] + _dot(yn, w_ref[...])

    return pl.pallas_call(
        body, name="out_fwd", grid=(s // tm,),
        in_specs=[_rows(tm, WIDTH_A)] + _col_specs(tm, WIDTH_B) + [_full((1, WIDTH_B)), _full((D_MODEL, D_MODEL)),
                                                                 _rows(tm, D_MODEL)],
        out_specs=[_rows(tm, D_MODEL), _rows(tm, D_MODEL)],
        out_shape=[jax.ShapeDtypeStruct((s, D_MODEL), F32), jax.ShapeDtypeStruct((s, D_MODEL), MXU_DTYPE)],
        compiler_params=_params(("arbitrary",)),
    )(ya_n, *([y_b] * nc), g_b, w_out, x)


def _ffn_fwd(h1, g_ffn, w_gate_t, w_up_t, w_down):
    s = h1.shape[0]
    tm = 512

    def body(h_ref, g_ref, wgt_ref, wut_ref, wd_ref, o_ref, gate_ref, up_ref, hn_ref):
        h = h_ref[...]
        n, _ = _rms_stats(h)
        hn = (n * g_ref[...]).astype(MXU_DTYPE)
        hn_ref[...] = hn
        strips = [dict(cols=slice(lo, hi)) for lo, hi in FF_STRIPS]

        def project(t):
            t['gate'] = _dot_nt(hn, wgt_ref[t['cols'], :])
            t['up'] = _dot_nt(hn, wut_ref[t['cols'], :])

        def activate(t):
            gate, up = t['gate'], t['up']
            gate_ref[:, t['cols']] = gate.astype(MXU_DTYPE)
            up_ref[:, t['cols']] = up.astype(MXU_DTYPE)
            t['act'] = (gate * jax.nn.sigmoid(gate) * up).astype(MXU_DTYPE)

        def down(t):
            return _dot(t['act'], wd_ref[t['cols'], :])

        out = h
        project(strips[0])
        for i, t in enumerate(strips):
            if i + 1 < len(strips):
                project(strips[i + 1])
            activate(t)
            out = out + down(t)
        o_ref[...] = out

    return pl.pallas_call(
        body, name="ffn_fwd", grid=(s // tm,),
        in_specs=[_rows(tm, D_MODEL), _full((1, D_MODEL)), _full((D_FF, D_MODEL)), _full((D_FF, D_MODEL)),
                  _full((D_FF, D_MODEL))],
        out_specs=[_rows(tm, D_MODEL), _rows(tm, D_FF), _rows(tm, D_FF), _rows(tm, D_MODEL)],
        out_shape=[jax.ShapeDtypeStruct((s, D_MODEL), F32), jax.ShapeDtypeStruct((s, D_FF), MXU_DTYPE),
                   jax.ShapeDtypeStruct((s, D_FF), MXU_DTYPE), jax.ShapeDtypeStruct((s, D_MODEL), MXU_DTYPE)],
        compiler_params=_params(("arbitrary",)),
    )(h1, g_ffn, w_gate_t, w_up_t, w_down)


def _ple_loss(h2, p, target, g_ple, w_pg, w_pp_t, g_final):
    s = h2.shape[0]
    tm = 256

    def body(h_ref, p_ref, t_ref, gp_ref, wg_ref, wpt_ref, gf_ref,
             loss_ref, dh_ref, dz_ref, dpp_ref, hn_ref, dgp_ref, dgf_ref):
        @pl.when(pl.program_id(0) == 0)
        def _():
            loss_ref[...] = jnp.zeros_like(loss_ref)
            dgp_ref[...] = jnp.zeros_like(dgp_ref)
            dgf_ref[...] = jnp.zeros_like(dgf_ref)

        h2t = h_ref[...]
        n2, r2 = _rms_stats(h2t)
        hn = (n2 * gp_ref[...]).astype(MXU_DTYPE)
        hn_ref[...] = hn
        gate = jax.nn.sigmoid(_dot(hn, wg_ref[...]))
        pp = _dot_nt(p_ref[...].astype(MXU_DTYPE), wpt_ref[...])
        h3 = h2t + gate * pp
        n3, r3 = _rms_stats(h3)
        diff = n3 * gf_ref[...] - t_ref[...]
        loss_ref[...] += jnp.full(loss_ref.shape, 0.5 * jnp.sum(diff * diff) / D_MODEL, F32)
        dy = diff * (1.0 / D_MODEL)
        dgf_ref[...] += jnp.sum(dy * n3, axis=0, keepdims=True)
        dh3 = _rms_bwd(dy * gf_ref[...], n3, r3)
        dpp_ref[...] = (dh3 * gate).astype(MXU_DTYPE)
        dz = (dh3 * pp * gate * (1.0 - gate)).astype(MXU_DTYPE)
        dz_ref[...] = dz
        dhn = _dot_nt(dz, wg_ref[...])
        dgp_ref[...] += jnp.sum(dhn * n2, axis=0, keepdims=True)
        dh_ref[...] = dh3 + _rms_bwd(dhn * gp_ref[...], n2, r2)

    return pl.pallas_call(
        body, name="ple_loss", grid=(s // tm,),
        in_specs=[_rows(tm, D_MODEL), _rows(tm, PLE_DIM), _rows(tm, D_MODEL), _full((1, D_MODEL)),
                  _full((D_MODEL, D_MODEL)), _full((D_MODEL, PLE_DIM)), _full((1, D_MODEL))],
        out_specs=[_full((1, LANES)), _rows(tm, D_MODEL), _rows(tm, D_MODEL), _rows(tm, D_MODEL),
                   _rows(tm, D_MODEL), _full((1, D_MODEL)), _full((1, D_MODEL))],
        out_shape=[jax.ShapeDtypeStruct((1, LANES), F32), jax.ShapeDtypeStruct((s, D_MODEL), F32),
                   jax.ShapeDtypeStruct((s, D_MODEL), MXU_DTYPE), jax.ShapeDtypeStruct((s, D_MODEL), MXU_DTYPE),
                   jax.ShapeDtypeStruct((s, D_MODEL), MXU_DTYPE), jax.ShapeDtypeStruct((1, D_MODEL), F32),
                   jax.ShapeDtypeStruct((1, D_MODEL), F32)],
        compiler_params=_params(("arbitrary",)),
    )(h2, p, target, g_ple, w_pg, w_pp_t, g_final)


def _ffn_bwd(dh2, h1, gate, up, g_ffn, w_down, w_gate_t, w_up_t):
    s = h1.shape[0]
    tm = 256

    def body(dh_ref, h_ref, gate_ref, up_ref, g_ref, wd_ref, wgt_ref, wut_ref,
             o_ref, act_ref, dg_ref, du_ref, dgn_ref):
        @pl.when(pl.program_id(0) == 0)
        def _():
            dgn_ref[...] = jnp.zeros_like(dgn_ref)

        dh = dh_ref[...]
        dhb = dh.astype(MXU_DTYPE)
        strips = [dict(cols=slice(lo, hi)) for lo, hi in FF_STRIPS]

        def back_down(t):
            t['dact'] = _dot_nt(dhb, wd_ref[t['cols'], :])

        def back_act(t):
            cols, dact = t['cols'], t['dact']
            g = gate_ref[:, cols].astype(F32)
            u = up_ref[:, cols].astype(F32)
            sg = jax.nn.sigmoid(g)
            silu = g * sg
            act_ref[:, cols] = (silu * u).astype(MXU_DTYPE)
            t['du'] = (dact * silu).astype(MXU_DTYPE)
            t['dg'] = (dact * u * sg * (1.0 + g * (1.0 - sg))).astype(MXU_DTYPE)
            du_ref[:, cols] = t['du']
            dg_ref[:, cols] = t['dg']

        def back_in(t):
            return _dot(t['dg'], wgt_ref[t['cols'], :]) + _dot(t['du'], wut_ref[t['cols'], :])

        dhn = jnp.zeros((tm, D_MODEL), F32)
        back_down(strips[0])
        for i, t in enumerate(strips):
            if i + 1 < len(strips):
                back_down(strips[i + 1])
            back_act(t)
            dhn = dhn + back_in(t)
        n, r = _rms_stats(h_ref[...])
        dgn_ref[...] += jnp.sum(dhn * n, axis=0, keepdims=True)
        o_ref[...] = dh + _rms_bwd(dhn * g_ref[...], n, r)

    return pl.pallas_call(
        body, name="ffn_bwd", grid=(s // tm,),
        in_specs=[_rows(tm, D_MODEL), _rows(tm, D_MODEL), _rows(tm, D_FF), _rows(tm, D_FF), _full((1, D_MODEL)),
                  _full((D_FF, D_MODEL)), _full((D_FF, D_MODEL)), _full((D_FF, D_MODEL))],
        out_specs=[_rows(tm, D_MODEL), _rows(tm, D_FF), _rows(tm, D_FF), _rows(tm, D_FF), _full((1, D_MODEL))],
        out_shape=[jax.ShapeDtypeStruct((s, D_MODEL), F32), jax.ShapeDtypeStruct((s, D_FF), MXU_DTYPE),
                   jax.ShapeDtypeStruct((s, D_FF), MXU_DTYPE), jax.ShapeDtypeStruct((s, D_FF), MXU_DTYPE),
                   jax.ShapeDtypeStruct((1, D_MODEL), F32)],
        compiler_params=_params(("arbitrary",)),
    )(dh2, h1, gate, up, g_ffn, w_down, w_gate_t, w_up_t)


def _out_bwd(dh1, y_b, g_b, w_out):
    s = dh1.shape[0]
    tm = 512
    nc = WIDTH_B // LANES

    def body(dh_ref, *refs):
        yb_refs = refs[:nc]
        g_ref, w_ref, dya_ref, dyb_ref, dg_ref, scr = refs[nc:]

        @pl.when(pl.program_id(0) == 0)
        def _():
            dg_ref[...] = jnp.zeros_like(dg_ref)

        dy = _dot_nt(dh_ref[...].astype(MXU_DTYPE), w_ref[...])
        dya_ref[...] = dy[:, :WIDTH_A]
        dyb = dy[:, WIDTH_A:]
        n, r = _rms_stats(_load_l256(yb_refs, tm))
        dg_ref[...] += jnp.sum(dyb * n, axis=0, keepdims=True)
        dyb_in = _rms_bwd(dyb * g_ref[...], n, r)
        for j in range(nc):
            cols = slice(j * LANES, (j + 1) * LANES)
            _store_l256(scr, dyb_ref, cols, dyb_in[:, cols])

    return pl.pallas_call(
        body, name="out_bwd", grid=(s // tm,), scratch_shapes=[pltpu.VMEM((tm, LANES), F32)],
        in_specs=[_rows(tm, D_MODEL)] + _col_specs(tm, WIDTH_B) + [_full((1, WIDTH_B)), _full((D_MODEL, D_MODEL))],
        out_specs=[_rows(tm, WIDTH_A), _rows(tm, WIDTH_B), _full((1, WIDTH_B))],
        out_shape=[jax.ShapeDtypeStruct((s, WIDTH_A), F32), jax.ShapeDtypeStruct((s, WIDTH_B), F32),
                   jax.ShapeDtypeStruct((1, WIDTH_B), F32)],
        compiler_params=_params(("arbitrary",)),
    )(dh1, *([y_b] * nc), g_b, w_out)


def _attn_bwd_branch(q, k, v, do, o, lse, grads, dil):
    s = q.shape[0]
    br = _Branch(dil, s)
    qn, nb = br.qn, br.nb
    first = grads is None

    def body(*refs):
        bias_ref, q_ref, kc_ref, kp_ref, vc_ref, vp_ref, do_ref, o_ref, lse_ref = refs[:9]
        if first:
            rest = refs[9:]
        else:
            dq_in, dk_in, dv_in = refs[9:12]
            rest = refs[12:]
        dq_ref, dk_ref, dv_ref, dk_carry, dv_carry = rest
        n = pl.program_id(1)

        @pl.when(n == 0)
        def _():
            dk_carry[...] = jnp.zeros_like(dk_carry)
            dv_carry[...] = jnp.zeros_like(dv_carry)

        @pl.when(n < nb)
        def _():
            bias2 = jnp.concatenate([bias_ref[0], bias_ref[0]], axis=0)
            lane = lax.broadcasted_iota(jnp.int32, (qn, LANES), 1)
            lo = lane < HEAD_DIM
            mask_f = lo.astype(F32)
            mask_lo = mask_f.astype(MXU_DTYPE)
            def prepare(hp):
                cols = slice(hp * LANES, (hp + 1) * LANES)
                qp = br.load(q_ref, cols)
                dop = br.load(do_ref, cols)
                prod = dop * br.load(o_ref, cols)
                prod_lo = prod * mask_f
                lse = br.load(lse_ref, cols)
                return dict(
                    cols=cols,
                    kcat=jnp.concatenate([br.load(kp_ref, cols), br.load(kc_ref, cols)], axis=0),
                    vcat=jnp.concatenate([br.load(vp_ref, cols), br.load(vc_ref, cols)], axis=0),
                    qs=jnp.concatenate([qp * mask_lo, qp * (1 - mask_lo)], axis=0),
                    dos=jnp.concatenate([dop * mask_f, dop * (1.0 - mask_f)], axis=0).astype(MXU_DTYPE),
                    delta=jnp.concatenate([jnp.sum(prod_lo, axis=1, keepdims=True),
                                           jnp.sum(prod - prod_lo, axis=1, keepdims=True)], axis=0),
                    lse2=jnp.concatenate([lse[:, :1], lse[:, HEAD_DIM:HEAD_DIM + 1]], axis=0))

            def scores(t):
                t['sc'] = _dot_nt(t['qs'], t['kcat'])
                t['dp'] = _dot_nt(t['dos'], t['vcat'])

            def softmax(t):
                p = jnp.exp(t['sc'] + bias2 - t['lse2'])
                t['ds'] = (p * (t['dp'] - t['delta'])).astype(MXU_DTYPE)
                t['p'] = p.astype(MXU_DTYPE)

            def gradients(t):
                t['dvc'] = _dot_tn(t['p'], t['dos'])
                t['dkc'] = _dot_tn(t['ds'], t['qs'])
                t['dq2'] = _dot(t['ds'], t['kcat'])

            def store(t):
                cols, dkc, dvc = t['cols'], t['dkc'], t['dvc']
                dq = jnp.where(lo, t['dq2'][:qn], t['dq2'][qn:])
                dk_prev = dk_carry[:, cols] + dkc[:qn]
                dv_prev = dv_carry[:, cols] + dvc[:qn]
                if not first:
                    dq = dq + br.load(dq_in, cols)
                    dk_prev = dk_prev + br.load(dk_in, cols)
                    dv_prev = dv_prev + br.load(dv_in, cols)
                br.store(dq_ref, cols, dq)
                br.store(dk_ref, cols, dk_prev)
                br.store(dv_ref, cols, dv_prev)
                dk_carry[:, cols] = dkc[qn:]
                dv_carry[:, cols] = dvc[qn:]

            for first_pair in range(0, HEADS_B // 2, PAIRS_ABREAST):
                group = [prepare(hp) for hp in range(first_pair, first_pair + PAIRS_ABREAST)]
                for stage in (scores, softmax, gradients, store):
                    for t in group:
                        stage(t)

        @pl.when(n == nb)
        def _():
            dk_last = dk_carry[...]
            dv_last = dv_carry[...]
            if not first:
                dk_last = dk_last + br.load(dk_in)
                dv_last = dv_last + br.load(dv_in)
            br.store(dk_ref, slice(None), dk_last)
            br.store(dv_ref, slice(None), dv_last)

    cur = lambda n: jnp.minimum(n, nb - 1)
    before = lambda n: jnp.maximum(cur(n) - 1, 0)
    late = lambda n: jnp.maximum(n - 1, 0)
    in_specs = [br.bias_spec(cur), br.spec(WIDTH_B, cur), br.spec(WIDTH_B, cur), br.spec(WIDTH_B, before),
                br.spec(WIDTH_B, cur), br.spec(WIDTH_B, before), br.spec(WIDTH_B, cur), br.spec(WIDTH_B, cur),
                br.spec(WIDTH_B, cur)]
    args = [jnp.asarray(br.bias)] + [br.view(a) for a in (q, k, k, v, v, do, o, lse)]
    if not first:
        in_specs += [br.spec(WIDTH_B, cur), br.spec(WIDTH_B, late), br.spec(WIDTH_B, late)]
        args += [br.view(g) for g in grads]
    res = pl.pallas_call(
        body, name="attn_bwd_d%d" % dil, grid=(br.grid[0], nb + 1), in_specs=in_specs,
        out_specs=[br.spec(WIDTH_B, cur), br.spec(WIDTH_B, late), br.spec(WIDTH_B, late)],
        out_shape=[jax.ShapeDtypeStruct((s // L_BLOCK, 4, 4, L_GROUP, WIDTH_B), F32)] * 3,
        scratch_shapes=[pltpu.VMEM((qn, WIDTH_B), F32), pltpu.VMEM((qn, WIDTH_B), F32)],
        compiler_params=_params(("arbitrary", "arbitrary")),
    )(*args)
    return tuple(a.reshape(s, WIDTH_B) for a in res)


def _sgu_bwd(uv, dya_n, w_tril, w_tril_t, bias, g_sgu, g_a):
    s = uv.shape[0]
    tm = 512

    def body(uv_ref, dy_ref, w_ref, wt_ref, b_ref, gs_ref, ga_ref, duv_ref, dw_ref, db_ref, dgs_ref, dga_ref,
             db_acc):
        i = pl.program_id(0)

        @pl.when(i == 0)
        def _():
            dw_ref[...] = jnp.zeros_like(dw_ref)
            dgs_ref[...] = jnp.zeros_like(dgs_ref)
            dga_ref[...] = jnp.zeros_like(dga_ref)
            db_acc[...] = jnp.zeros_like(db_acc)

        t = _sgu_forward_tile(uv_ref[...], w_ref, b_ref[...], gs_ref[...])
        na, ra = _rms_stats(t['ya'])
        dyn = dy_ref[...]
        dga_ref[...] += jnp.sum(dyn * na, axis=0, keepdims=True)
        dya = _rms_bwd(dyn * ga_ref[...], na, ra)
        dug = dya * t['mixed']
        dmixed = dya * t['ug']
        dmb = dmixed.astype(MXU_DTYPE)
        masks = _half_masks(MXU_DTYPE)
        chunks = []
        db = jnp.zeros((CHUNK, WIDTH_A), F32)
        for c in range(tm // CHUNK):
            rows = slice(c * CHUNK, (c + 1) * CHUNK)
            db = db + dmixed[rows]
            groups = []
            for gp in range(2):
                cols = slice(gp * LANES, (gp + 1) * LANES)
                dm_g = dmb[rows, cols]
                vn_g = t['vn'][rows, cols]
                dvn_g = jnp.zeros((CHUNK, LANES), F32)
                for j in range(2):
                    dm_h = dm_g * masks[j]
                    dvn_g = dvn_g + _dot(wt_ref[2 * gp + j], dm_h)
                    dw_ref[2 * gp + j] += _dot_nt(dm_h, vn_g)
                groups.append(dvn_g)
            chunks.append(jnp.concatenate(groups, axis=1))
        db_acc[...] += db
        dvn = jnp.concatenate(chunks, axis=0)
        vhat = t['vhat']
        dgs_ref[...] += jnp.sum(dvn * vhat, axis=0, keepdims=True)
        dvh = dvn * gs_ref[...]
        dvg = t['rs'] * (dvh - jnp.mean(dvh, axis=-1, keepdims=True)
                         - vhat * jnp.mean(dvh * vhat, axis=-1, keepdims=True))
        duv_ref[:, :WIDTH_A] = (dug * _gelu_grad(t['u'], t['tu'])).astype(MXU_DTYPE)
        duv_ref[:, WIDTH_A:] = (dvg * _gelu_grad(t['v'], t['tv'])).astype(MXU_DTYPE)

        @pl.when(i == pl.num_programs(0) - 1)
        def _():
            lane_a = lax.broadcasted_iota(jnp.int32, (CHUNK, WIDTH_A), 1)
            lane = lax.broadcasted_iota(jnp.int32, (CHUNK, LANES), 1)
            acc = db_acc[...]
            out = jnp.zeros((CHUNK, LANES), F32)
            for h in range(HEADS_A):
                col = jnp.sum(jnp.where(lane_a // HEAD_DIM == h, acc, 0.0), axis=1, keepdims=True)
                out = jnp.where(lane == h, col, out)
            db_ref[...] = out
            causal = (lax.broadcasted_iota(jnp.int32, (CHUNK, CHUNK), 0)
                      >= lax.broadcasted_iota(jnp.int32, (CHUNK, CHUNK), 1))
            for h in range(HEADS_A):
                dw_ref[h] = jnp.where(causal, dw_ref[h], 0.0)

    return pl.pallas_call(
        body, name="sgu_bwd", grid=(s // tm,),
        in_specs=[_rows(tm, 2 * WIDTH_A), _rows(tm, WIDTH_A), _full((HEADS_A, CHUNK, CHUNK)),
                  _full((HEADS_A, CHUNK, CHUNK)), _full((CHUNK, WIDTH_A)), _full((1, WIDTH_A)),
                  _full((1, WIDTH_A))],
        out_specs=[_rows(tm, 2 * WIDTH_A), _full((HEADS_A, CHUNK, CHUNK)), _full((CHUNK, LANES)),
                   _full((1, WIDTH_A)), _full((1, WIDTH_A))],
        out_shape=[jax.ShapeDtypeStruct((s, 2 * WIDTH_A), MXU_DTYPE),
                   jax.ShapeDtypeStruct((HEADS_A, CHUNK, CHUNK), F32), jax.ShapeDtypeStruct((CHUNK, LANES), F32),
                   jax.ShapeDtypeStruct((1, WIDTH_A), F32), jax.ShapeDtypeStruct((1, WIDTH_A), F32)],
        scratch_shapes=[pltpu.VMEM((CHUNK, WIDTH_A), F32)],
        compiler_params=_params(("arbitrary",)),
    )(uv, dya_n, w_tril, w_tril_t, bias, g_sgu, g_a)


def _in_bwd_proj(duv, dq, dk, dv, cos_t, sin_t):
    s = duv.shape[0]
    tm = 512
    nc = WIDTH_B // LANES

    def body(duv_ref, *refs):
        dq_refs, dk_refs, dv_refs = refs[:nc], refs[nc:2 * nc], refs[2 * nc:3 * nc]
        cos_ref, sin_ref, dp_ref = refs[3 * nc:]
        cos = cos_ref[...]
        sin = sin_ref[...]
        dp_ref[:, :2 * WIDTH_A] = duv_ref[...]
        for i in range(nc):
            lo = 2 * WIDTH_A + i * LANES
            tq = _load_l256(dq_refs[i:i + 1], tm) * (HEAD_DIM ** -0.5)
            tk = _load_l256(dk_refs[i:i + 1], tm)
            dp_ref[:, lo:lo + LANES] = (tq * cos + _rope_partner(tq * sin)).astype(MXU_DTYPE)
            dp_ref[:, lo + WIDTH_B:lo + WIDTH_B + LANES] = (tk * cos + _rope_partner(tk * sin)).astype(MXU_DTYPE)
            dp_ref[:, lo + 2 * WIDTH_B:lo + 2 * WIDTH_B + LANES] = _load_l256(dv_refs[i:i + 1], tm).astype(MXU_DTYPE)

    return pl.pallas_call(
        body, name="in_bwd_proj", grid=(s // tm,),
        in_specs=[_rows(tm, 2 * WIDTH_A)] + 3 * _col_specs(tm, WIDTH_B) + [_rows(tm, LANES), _rows(tm, LANES)],
        out_specs=_rows(tm, IN_COLS), out_shape=jax.ShapeDtypeStruct((s, IN_COLS), MXU_DTYPE),
        compiler_params=_params(("arbitrary",)),
    )(duv, *([dq] * nc), *([dk] * nc), *([dv] * nc), cos_t, sin_t)


def _in_bwd_x(dproj, w_in_t, x, g_mix, dh1, after):
    s = x.shape[0]
    tm = 512

    def body(dp_ref, wt_ref, x_ref, g_ref, dh_ref, *rest):
        gx_ref, dg_ref = rest[len(after):]
        @pl.when(pl.program_id(0) == 0)
        def _():
            dg_ref[...] = jnp.zeros_like(dg_ref)

        dhn = _dot(dp_ref[...], wt_ref[...])
        n, r = _rms_stats(x_ref[...])
        dg_ref[...] += jnp.sum(dhn * n, axis=0, keepdims=True)
        gx_ref[...] = dh_ref[...] + _rms_bwd(dhn * g_ref[...], n, r)

    return pl.pallas_call(
        body, name="in_bwd_x", grid=(s // tm,),
        in_specs=[_rows(tm, IN_COLS), _full((IN_COLS, D_MODEL)), _rows(tm, D_MODEL), _full((1, D_MODEL)),
                  _rows(tm, D_MODEL)] + [pl.BlockSpec(memory_space=pl.ANY)] * len(after),
        out_specs=[_rows(tm, D_MODEL), _full((1, D_MODEL))],
        out_shape=[jax.ShapeDtypeStruct((s, D_MODEL), F32), jax.ShapeDtypeStruct((1, D_MODEL), F32)],
        compiler_params=_params(("arbitrary",)),
    )(dproj, w_in_t, x, g_mix, dh1, *after)


def _wgrad(a, b, name):
    s, m = a.shape
    n = b.shape[1]
    bm = 512 if m % 512 == 0 else FF_HALF
    ts = 1024
    nsteps = s // ts

    def body(a_ref, b_ref, o_ref, acc):
        kk = pl.program_id(1)

        @pl.when(kk == 0)
        def _():
            acc[...] = jnp.zeros_like(acc)

        acc[...] += _dot_tn(a_ref[...].astype(MXU_DTYPE), b_ref[...].astype(MXU_DTYPE))

        @pl.when(kk == nsteps - 1)
        def _():
            o_ref[...] = acc[...].astype(o_ref.dtype)

    return pl.pallas_call(
        body, name=name, grid=(m // bm, nsteps),
        in_specs=[pl.BlockSpec((ts, bm), lambda i, kk: (kk, i)), pl.BlockSpec((ts, n), lambda i, kk: (kk, 0))],
        out_specs=pl.BlockSpec((bm, n), lambda i, kk: (i, 0)), out_shape=jax.ShapeDtypeStruct((m, n), jnp.bfloat16),
        scratch_shapes=[pltpu.VMEM((bm, n), F32)],
        compiler_params=_params(("arbitrary", "arbitrary")),
    )(a, b)


def _rope_tables(s):
    half = HEAD_DIM // 2
    inv = ROPE_THETA ** (-jnp.arange(half, dtype=F32) / half)
    ang = jnp.arange(s, dtype=F32)[:, None] * inv[None, :]
    cos = jnp.cos(ang)
    sin = jnp.sin(ang)
    cos_t = jnp.concatenate([cos, cos, cos, cos], axis=1)
    sin_t = jnp.concatenate([-sin, sin, -sin, sin], axis=1)
    return cos_t, sin_t


MESH = pl.DeviceIdType.MESH
ANY = pl.BlockSpec(memory_space=pl.ANY)
SEM = pl.BlockSpec(memory_space=pltpu.SEMAPHORE)
SPLIT_COPY = pltpu.CompilerParams(has_side_effects=pltpu.SideEffectType.DATAFLOW_SIDE_EFFECTING)
SLAB_IS_TRANSPOSED = {'w_in': True, 'w_out': False, 'w_gate': True, 'w_up': True, 'w_down': False,
                      'w_ple_gate': False, 'w_ple_proj': True}


def _place():
    x, y, c = lax.axis_index("x"), lax.axis_index("y"), lax.axis_index("c")
    other_chips = [(1 - x, y), (x, 1 - y), (1 - x, 1 - y)]
    return x, y, c, other_chips


def _chip_of(chip):
    return 2 * chip[0] + chip[1]


def _half(ref, lead, hc):
    hr = ref.shape[1] // 2
    return ref.at[lead, pl.ds(hc * hr, hr), :]


def _put_own(stack, own, index):
    return lax.dynamic_update_slice(stack, own[None], (index,) + (0,) * own.ndim)


def _all_gather_now(slab):
    rows, cols = slab.shape

    def body(x_ref, out_ref, send_sems, recv_sems):
        x, y, c, chips = _place()
        sibling = (x, y, 1 - c)
        hr = rows // 2

        def copy(k, src, dst, to):
            return pltpu.make_async_remote_copy(src_ref=src, dst_ref=dst, send_sem=send_sems.at[k],
                                                recv_sem=recv_sems.at[k], device_id=to, device_id_type=MESH)

        my_half = x_ref.at[pl.ds(c * hr, hr), :]
        first = [copy(j, my_half, _half(out_ref, 2 * x + y, c), (*chip, c)) for j, chip in enumerate(chips)]
        for cp in first:
            cp.start()
        passed = [copy(3 + j, _half(out_ref, _chip_of(chip), c), _half(out_ref, _chip_of(chip), c), sibling)
                  for j, chip in enumerate(chips)]
        for j, chip in enumerate(chips):
            copy(j, my_half, _half(out_ref, _chip_of(chip), c), (*chip, c)).wait_recv()
            passed[j].start()
        for j, chip in enumerate(chips):
            copy(3 + j, my_half, _half(out_ref, _chip_of(chip), 1 - c), sibling).wait_recv()
        for cp in first + passed:
            cp.wait_send()

    gathered = pl.pallas_call(
        body, name="all_gather_now", out_shape=jax.ShapeDtypeStruct((N_CHIPS, rows, cols), slab.dtype),
        in_specs=[ANY], out_specs=ANY,
        scratch_shapes=[pltpu.SemaphoreType.DMA((6,)), pltpu.SemaphoreType.DMA((6,))],
    )(slab)
    me = 2 * lax.axis_index("x") + lax.axis_index("y")
    return _put_own(gathered, slab, me).reshape(N_CHIPS * rows, cols)


def _gather_copies(slab_refs, land_refs, send_sems, recv_sems):
    x, y, c, chips = _place()
    sends, recvs = [], []
    for k, (src, land) in enumerate(zip(slab_refs, land_refs)):
        hr = src.shape[0] // 2
        for j, chip in enumerate(chips):
            for t in range(2):
                sends.append(pltpu.make_async_remote_copy(
                    src_ref=src.at[pl.ds(c * hr, hr), :], dst_ref=_half(land, 2 * x + y, c),
                    send_sem=send_sems.at[6 * k + 2 * j + t], recv_sem=recv_sems.at[6 * k + 2 * j + c],
                    device_id=(*chip, t), device_id_type=MESH))
                recvs.append(pltpu.make_async_remote_copy(
                    src_ref=src.at[pl.ds(t * hr, hr), :], dst_ref=_half(land, _chip_of(chip), t),
                    send_sem=send_sems.at[6 * k + 2 * j + t], recv_sem=recv_sems.at[6 * k + 2 * j + t],
                    device_id=(*chip, t), device_id_type=MESH))
    return sends, recvs


def _all_gather_start(slabs, after):
    n = len(slabs)

    def body(*refs):
        slab_refs, land_refs = refs[:n], refs[n:2 * n]
        send_sems, recv_sems = refs[2 * n + 1:2 * n + 3]
        token = refs[-1]
        sends, _ = _gather_copies(slab_refs, land_refs, send_sems, recv_sems)
        for cp in sends:
            cp.start()
        token[...] = jnp.zeros_like(token)

    lands = [lax.empty((N_CHIPS,) + s.shape, s.dtype) for s in slabs]
    hbm = lambda a: pltpu.HBM(a.shape, a.dtype)
    res = pl.pallas_call(
        body, name="all_gather_start",
        out_shape=(pltpu.SemaphoreType.DMA((6 * n,)), pltpu.SemaphoreType.DMA((6 * n,)), *map(hbm, slabs),
                   *map(hbm, lands), jax.ShapeDtypeStruct((8, LANES), F32)),
        in_specs=[ANY] * (2 * n + 1),
        out_specs=(SEM, SEM, *([ANY] * (2 * n)), pl.BlockSpec(memory_space=pltpu.VMEM)),
        input_output_aliases={i: 2 + i for i in range(2 * n)}, compiler_params=SPLIT_COPY,
    )(*[pltpu.with_memory_space_constraint(a, pltpu.HBM) for a in list(slabs) + lands], after)
    return res[:-1], res[-1]


def _all_gather_wait(handle, after):
    send_sems, recv_sems = handle[:2]
    n = (len(handle) - 2) // 2
    slabs, lands = handle[2:2 + n], handle[2 + n:]

    def body(*refs):
        slab_refs, land_refs = refs[:n], refs[n:2 * n]
        send_sems, recv_sems = refs[2 * n:2 * n + 2]
        sends, recvs = _gather_copies(slab_refs, land_refs, send_sems, recv_sems)
        for cp in sends:
            cp.wait_send()
        for cp in recvs:
            cp.wait_recv()

    hbm = lambda a: pltpu.HBM(a.shape, a.dtype)
    res = pl.pallas_call(
        body, name="all_gather_wait", out_shape=tuple(map(hbm, list(slabs) + list(lands))),
        in_specs=[ANY] * (2 * n) + [SEM, SEM, ANY], out_specs=tuple([ANY] * (2 * n)),
        input_output_aliases={i: i for i in range(2 * n)}, compiler_params=SPLIT_COPY,
    )(*slabs, *lands, send_sems, recv_sems, after)
    me = 2 * lax.axis_index("x") + lax.axis_index("y")
    return [_put_own(land, slab, me).reshape(N_CHIPS * slab.shape[0], slab.shape[1])
            for slab, land in zip(res[:n], res[n:])]


def _scatter_copies(part_refs, land_refs, send_sems, recv_sems):
    x, y, c, chips = _place()
    me = 4 * x + 2 * y + c
    sends, recvs = [], []
    for k, (part, land) in enumerate(zip(part_refs, land_refs)):
        for j, chip in enumerate(chips):
            for t in range(2):
                sends.append(pltpu.make_async_remote_copy(
                    src_ref=part.at[_chip_of(chip)], dst_ref=land.at[me],
                    send_sem=send_sems.at[7 * k + 2 * j + t], recv_sem=recv_sems.at[7 * k + 2 * j + c],
                    device_id=(*chip, t), device_id_type=MESH))
                recvs.append(pltpu.make_async_remote_copy(
                    src_ref=part.at[_chip_of(chip)], dst_ref=land.at[2 * _chip_of(chip) + t],
                    send_sem=send_sems.at[7 * k + 2 * j + t], recv_sem=recv_sems.at[7 * k + 2 * j + t],
                    device_id=(*chip, t), device_id_type=MESH))
        sends.append(pltpu.make_async_remote_copy(
            src_ref=part.at[2 * x + y], dst_ref=land.at[me], send_sem=send_sems.at[7 * k + 6],
            recv_sem=recv_sems.at[7 * k + 6], device_id=(x, y, 1 - c), device_id_type=MESH))
        recvs.append(pltpu.make_async_remote_copy(
            src_ref=part.at[2 * x + y], dst_ref=land.at[4 * x + 2 * y + 1 - c],
            send_sem=send_sems.at[7 * k + 6], recv_sem=recv_sems.at[7 * k + 6], device_id=(x, y, 1 - c),
            device_id_type=MESH))
    return sends, recvs


def _reduce_scatter_start(parts, name):
    n = len(parts)
    parts = [p.reshape(N_CHIPS, p.shape[0] // N_CHIPS, p.shape[1]) for p in parts]

    def body(*refs):
        part_refs, land_refs = refs[:n], refs[n:2 * n]
        send_sems, recv_sems = refs[2 * n:2 * n + 2]
        token = refs[-1]
        sends, _ = _scatter_copies(part_refs, land_refs, send_sems, recv_sems)
        for cp in sends:
            cp.start()
        token[...] = jnp.zeros_like(token)

    lands = [lax.empty((N_DEV, p.shape[1], p.shape[2]), p.dtype) for p in parts]
    hbm = lambda a: pltpu.HBM(a.shape, a.dtype)
    res = pl.pallas_call(
        body, name=name,
        out_shape=(pltpu.SemaphoreType.DMA((7 * n,)), pltpu.SemaphoreType.DMA((7 * n,)), *map(hbm, parts),
                   *map(hbm, lands), jax.ShapeDtypeStruct((8, LANES), F32)),
        in_specs=[ANY] * (2 * n), out_specs=(SEM, SEM, *([ANY] * (2 * n)), pl.BlockSpec(memory_space=pltpu.VMEM)),
        input_output_aliases={i: 2 + i for i in range(2 * n)}, compiler_params=SPLIT_COPY,
    )(*[pltpu.with_memory_space_constraint(a, pltpu.HBM) for a in parts + lands])
    return res[:-1], res[-1]


def _reduce_scatter_wait(handle, after, name):
    send_sems, recv_sems = handle[:2]
    n = (len(handle) - 2) // 2
    parts, lands = handle[2:2 + n], handle[2 + n:]

    def body(*refs):
        part_refs, land_refs = refs[:n], refs[n:2 * n]
        send_sems, recv_sems = refs[2 * n:2 * n + 2]
        sends, recvs = _scatter_copies(part_refs, land_refs, send_sems, recv_sems)
        for cp in sends:
            cp.wait_send()
        for cp in recvs:
            cp.wait_recv()

    hbm = lambda a: pltpu.HBM(a.shape, a.dtype)
    res = pl.pallas_call(
        body, name=name, out_shape=tuple(map(hbm, list(parts) + list(lands))),
        in_specs=[ANY] * (2 * n) + [SEM, SEM, ANY], out_specs=tuple([ANY] * (2 * n)),
        input_output_aliases={i: i for i in range(2 * n)}, compiler_params=SPLIT_COPY,
    )(*parts, *lands, send_sems, recv_sems, after)
    return list(zip(res[:n], res[n:]))


def _small_all_reduce(block):
    rows = block.shape[0]

    def body(x_ref, all_ref, sum_ref, send_sems, recv_sems, local_sem):
        x, y, c, chips = _place()
        me, sibling = (x, y, c), (x, y, 1 - c)

        def blk(px, py, pc):
            return all_ref.at[pl.ds((4 * px + 2 * py + pc) * rows, rows), :]

        def copy(k, who, to, src=None):
            return pltpu.make_async_remote_copy(
                src_ref=blk(*who) if src is None else src, dst_ref=blk(*who), send_sem=send_sems.at[k],
                recv_sem=recv_sems.at[k], device_id=to, device_id_type=MESH)

        mine = pltpu.make_async_copy(x_ref, blk(*me), local_sem)
        mine.start()
        first = [copy(0, me, sibling, src=x_ref)]
        first += [copy(1 + j, me, (*chip, c), src=x_ref) for j, chip in enumerate(chips)]
        for cp in first:
            cp.start()
        passed = [copy(4 + j, (*chip, c), sibling) for j, chip in enumerate(chips)]
        for j, chip in enumerate(chips):
            copy(1 + j, (*chip, c), me).wait_recv()
            passed[j].start()
        copy(0, sibling, me).wait_recv()
        for j, chip in enumerate(chips):
            copy(4 + j, (*chip, 1 - c), me).wait_recv()
        for cp in first + passed:
            cp.wait_send()
        mine.wait()
        acc = all_ref[pl.ds(0, rows), :]
        for dev in range(1, N_DEV):
            acc = acc + all_ref[pl.ds(dev * rows, rows), :]
        sum_ref[...] = acc

    vmem = pl.BlockSpec(memory_space=pltpu.VMEM)
    return pl.pallas_call(
        body, name="small_all_reduce",
        out_shape=[jax.ShapeDtypeStruct((N_DEV * rows, D_MODEL), F32), jax.ShapeDtypeStruct((rows, D_MODEL), F32)],
        in_specs=[vmem], out_specs=[vmem, vmem],
        scratch_shapes=[pltpu.SemaphoreType.DMA((7,)), pltpu.SemaphoreType.DMA((7,)), pltpu.SemaphoreType.DMA],
    )(block)[1]


def _adamw(w, g, m, v, name):
    rows, cols = w.shape
    tm = rows
    if rows > 512:
        tm = next(t for t in range(512, 7, -8) if rows % t == 0)

    def body(w_ref, g_ref, m_ref, v_ref, d_ref, nm_ref, nv_ref):
        g_ = g_ref[...]
        m_ = ADAM_B1 * m_ref[...] + (1.0 - ADAM_B1) * g_
        v_ = ADAM_B2 * v_ref[...] + (1.0 - ADAM_B2) * (g_ * g_)
        m_hat = m_ / (1.0 - ADAM_B1 ** ADAM_STEP)
        v_hat = v_ / (1.0 - ADAM_B2 ** ADAM_STEP)
        d_ref[...] = -ADAM_LR * (m_hat / (jnp.sqrt(v_hat) + ADAM_EPS) + ADAM_WD * w_ref[...])
        nm_ref[...] = m_
        nv_ref[...] = v_

    spec = pl.BlockSpec((tm, cols), lambda i: (i, 0))
    return pl.pallas_call(
        body, name=name, grid=(rows // tm,), in_specs=[spec] * 4, out_specs=[spec] * 3,
        out_shape=[jax.ShapeDtypeStruct(w.shape, F32)] * 3, compiler_params=_params(("arbitrary",)),
    )(w, g, m, v)


def _adamw_of_shares(w, own, land, m, v, name):
    rows, cols = w.shape
    tm = rows // 4
    x, y, c = lax.axis_index("x"), lax.axis_index("y"), lax.axis_index("c")
    where = jnp.stack([2 * x + y, 4 * x + 2 * y + c]).astype(jnp.int32)

    def body(where_ref, w_ref, own_ref, land_ref, m_ref, v_ref, g_ref, d_ref, nm_ref, nv_ref):
        me = where_ref[1]
        g_ = jnp.zeros((tm, cols), F32)
        for dev in range(N_DEV):
            g_ = g_ + jnp.where(me == dev, own_ref[0], land_ref[dev]).astype(F32)
        m_ = ADAM_B1 * m_ref[...] + (1.0 - ADAM_B1) * g_
        v_ = ADAM_B2 * v_ref[...] + (1.0 - ADAM_B2) * (g_ * g_)
        m_hat = m_ / (1.0 - ADAM_B1 ** ADAM_STEP)
        v_hat = v_ / (1.0 - ADAM_B2 ** ADAM_STEP)
        g_ref[...] = g_
        d_ref[...] = -ADAM_LR * (m_hat / (jnp.sqrt(v_hat) + ADAM_EPS) + ADAM_WD * w_ref[...])
        nm_ref[...] = m_
        nv_ref[...] = v_

    tile = pl.BlockSpec((tm, cols), lambda i, where_ref: (i, 0))
    spec = pltpu.PrefetchScalarGridSpec(
        num_scalar_prefetch=1, grid=(rows // tm,),
        in_specs=[tile, pl.BlockSpec((1, tm, cols), lambda i, where_ref: (where_ref[0], i, 0)),
                  pl.BlockSpec((N_DEV, tm, cols), lambda i, where_ref: (0, i, 0)), tile, tile],
        out_specs=[tile] * 4)
    return pl.pallas_call(
        body, name=name, grid_spec=spec, out_shape=[jax.ShapeDtypeStruct(w.shape, F32)] * 4,
        compiler_params=_params(("arbitrary",)),
    )(where, w, own, land, m, v)


def _pack_small(values):
    flat = jnp.concatenate([values[n].reshape(-1).astype(F32) for n in SMALL])
    return jnp.pad(flat, (0, SMALL_ROWS * D_MODEL - flat.shape[0])).reshape(SMALL_ROWS, D_MODEL)


def _unpack_small(block, shapes):
    flat = block.reshape(-1)
    out, lo = {}, 0
    for n in SMALL:
        out[n] = flat[lo:lo + SMALL_SIZES[n]].reshape(shapes[n])
        lo += SMALL_SIZES[n]
    return out


def _after(token, a):
    return a + token[:1, :1].astype(a.dtype)


def kernel(x, p, mix_norm_g, w_in, sgu_w, sgu_b, sgu_norm_g, out_norm_a, out_norm_b, w_out, ffn_norm_g, w_gate, w_up, w_down, ple_norm_g, w_ple_gate, w_ple_proj, final_norm_g, loss_target, m_mix_norm_g, m_w_in, m_sgu_w, m_sgu_b, m_sgu_norm_g, m_out_norm_a, m_out_norm_b, m_w_out, m_ffn_norm_g, m_w_gate, m_w_up, m_w_down, m_ple_norm_g, m_w_ple_gate, m_w_ple_proj, m_final_norm_g, v_mix_norm_g, v_w_in, v_sgu_w, v_sgu_b, v_sgu_norm_g, v_out_norm_a, v_out_norm_b, v_w_out, v_ffn_norm_g, v_w_gate, v_w_up, v_w_down, v_ple_norm_g, v_w_ple_gate, v_w_ple_proj, v_final_norm_g):
    given = dict(locals())
    drop_lead = lambda a, lead: a.reshape(a.shape[lead:])
    xs, ps, target = drop_lead(x, 1), drop_lead(p, 2), drop_lead(loss_target, 1)
    s = xs.shape[0]
    shard = lambda name: drop_lead(given[name], 1)

    def slab_of(name):
        local = shard(name).astype(MXU_DTYPE)
        return local.T if SLAB_IS_TRANSPOSED[name] else local

    w_in_t = _all_gather_now(slab_of('w_in'))
    later = ['w_out', 'w_gate', 'w_up', 'w_down', 'w_ple_gate', 'w_ple_proj']
    gather, token = _all_gather_start([slab_of(n) for n in later], w_in_t)

    cos_t, sin_t = _rope_tables(s)
    tril = jnp.tril(jnp.ones((CHUNK, CHUNK), F32))
    w_tril = (sgu_w.reshape(HEADS_A, CHUNK, CHUNK) * tril).astype(MXU_DTYPE)
    w_tril_t = jnp.swapaxes(w_tril, 1, 2)
    bias = jnp.repeat(sgu_b.reshape(HEADS_A, CHUNK).T, HEAD_DIM, axis=1)
    g = {n: given[n].reshape(1, -1) for n in SMALL if n not in ('sgu_w', 'sgu_b')}

    uv, q, k, v, hn1 = _in_fwd(xs, _after(token, g['mix_norm_g']), w_in_t, cos_t, sin_t)
    ya_n = _sgu_fwd(uv, w_tril, bias, g['sgu_norm_g'], g['out_norm_a'])
    branches = [_attn_fwd_branch(q, k, v, dil) for dil in DILATIONS]
    y_b, lse = _attn_merge([o for o, _ in branches], [l for _, l in branches])
    stacks = dict(zip(later, _all_gather_wait(gather, lse)))
    w_gate_t, w_up_t, w_pp_t = stacks['w_gate'], stacks['w_up'], stacks['w_ple_proj']
    h1, y_n = _out_fwd(ya_n, y_b, g['out_norm_b'], stacks['w_out'], xs)
    h2, gate, up, hn2 = _ffn_fwd(h1, g['ffn_norm_g'], w_gate_t, w_up_t, stacks['w_down'])
    loss, dh2, dz, dpp, hn3, d_ple_g, d_final_g = _ple_loss(
        h2, ps, target, g['ple_norm_g'], stacks['w_ple_gate'], w_pp_t, g['final_norm_g'])

    share = {}
    share['w_ple_gate'] = _wgrad(hn3, dz, "wgrad_ple_gate")
    share['w_ple_proj'] = _wgrad(dpp, ps, "wgrad_ple_proj")
    scatter_1, token = _reduce_scatter_start([share['w_ple_gate'], share['w_ple_proj']], "reduce_scatter_start_1")
    dh1, act, dgate, dup, d_ffn_g = _ffn_bwd(dh2, h1, gate, up, _after(token, g['ffn_norm_g']), stacks['w_down'],
                                             w_gate_t, w_up_t)
    share['w_down'] = _wgrad(act, dh2, "wgrad_down")
    share['w_gate'] = _wgrad(dgate, hn2, "wgrad_gate")
    share['w_up'] = _wgrad(dup, hn2, "wgrad_up")
    scatter_2, token = _reduce_scatter_start([share['w_down'], share['w_gate'], share['w_up']],
                                             "reduce_scatter_start_2")
    dya_n, dyb, d_out_b = _out_bwd(dh1, y_b, _after(token, g['out_norm_b']), stacks['w_out'])
    share['w_out'] = _wgrad(y_n, dh1, "wgrad_out")
    scatter_3, token = _reduce_scatter_start([share['w_out']], "reduce_scatter_start_3")
    grads = None
    for dil in DILATIONS:
        grads = _attn_bwd_branch(q, k, v, dyb, y_b, lse, grads, dil)
    duv, d_sgu_w, d_sgu_b, d_sgu_g, d_out_a = _sgu_bwd(uv, dya_n, w_tril, w_tril_t, bias,
                                                       _after(token, g['sgu_norm_g']), g['out_norm_a'])
    dproj = _in_bwd_proj(duv, grads[0], grads[1], grads[2], cos_t, sin_t)
    share['w_in'] = _wgrad(dproj, hn1, "wgrad_in")
    scatter_4, token = _reduce_scatter_start([share['w_in']], "reduce_scatter_start_4")

    grads, deltas, new_m, new_v = {}, {}, {}, {}
    add_lead = lambda a: a.reshape((1,) + a.shape)

    def finish(names, handles, after, tag):
        landed = []
        for i, handle in enumerate(handles):
            landed += _reduce_scatter_wait(handle, after, "reduce_scatter_wait_%s%d" % (tag, i))
        for n, (own, land) in zip(names, landed):
            turn = (lambda a: a.T) if SLAB_IS_TRANSPOSED[n] else (lambda a: a)
            res = _adamw_of_shares(turn(shard(n)), own, land, turn(shard("m_" + n)), turn(shard("v_" + n)),
                                   "adamw_" + n)
            grads[n], deltas[n], new_m[n], new_v[n] = (add_lead(turn(a)) for a in res)

    finish(['w_ple_gate', 'w_ple_proj', 'w_down', 'w_gate', 'w_up', 'w_out'], [scatter_1, scatter_2, scatter_3], token,
           "early")
    grad_x, d_mix_g = _in_bwd_x(dproj, w_in_t, xs, g['mix_norm_g'], dh1,
                                after=[new_v[n] for n in ('w_down', 'w_gate', 'w_up', 'w_out')])

    gs = {'mix_norm_g': d_mix_g, 'sgu_w': d_sgu_w, 'sgu_b': d_sgu_b[:, :HEADS_A].T, 'sgu_norm_g': d_sgu_g,
          'out_norm_a': d_out_a, 'out_norm_b': d_out_b, 'ffn_norm_g': d_ffn_g, 'ple_norm_g': d_ple_g,
          'final_norm_g': d_final_g}
    gs_block = _pack_small(gs).at[SMALL_ROWS - 1, 0].set(loss[0, 0])
    small_sum = _small_all_reduce(gs_block)
    loss_out = small_sum[SMALL_ROWS - 1, 0]
    small_shapes = {n: given[n].shape for n in SMALL}
    finish(['w_in'], [scatter_4], small_sum, "last")

    small = {n: given[n] for n in SMALL}
    d, nm, nv = _adamw(_pack_small(small), small_sum, _pack_small({n: given["m_" + n] for n in SMALL}),
                       _pack_small({n: given["v_" + n] for n in SMALL}), "adamw_small")
    for res, blk in ((grads, small_sum), (deltas, d), (new_m, nm), (new_v, nv)):
        res.update(_unpack_small(blk, small_shapes))

    outs = [loss_out, add_lead(grad_x)]
    for res in (grads, deltas, new_m, new_v):
        outs += [res[n] for n in WEIGHT_NAMES]
    return tuple(outs)
```

```python
import functools
import math

import jax
import jax.numpy as jnp
import numpy as np
from jax import lax
from jax.experimental import pallas as pl
from jax.experimental.pallas import tpu as pltpu

F32 = jnp.float32
MXU_DTYPE = jnp.bfloat16

D_MODEL = 1024
HEAD_DIM = 64
HEADS_A = 4
HEADS_B = 12
WIDTH_A = HEADS_A * HEAD_DIM
WIDTH_B = HEADS_B * HEAD_DIM
CHUNK = 128
BLOCK = 128
DILATIONS = (1, 4, 16)
ROPE_THETA = 10000.0
D_FF = 2816
FF_HALF = D_FF // 2
FF_STRIP = FF_HALF
FF_STRIPS = ((0, 1024), (1024, 2048), (2048, D_FF))
PLE_DIM = 256
IN_COLS = 2 * WIDTH_A + 3 * WIDTH_B
EPS = 1e-6
LANES = 128
N_CHIPS = 4
N_DEV = 8

ADAM_LR = 0.001
ADAM_B1 = 0.9
ADAM_B2 = 0.999
ADAM_EPS = 1e-08
ADAM_WD = 0.01
ADAM_STEP = 10

VMEM_LIMIT = 56 * 1024 * 1024

WEIGHT_NAMES = ['mix_norm_g', 'w_in', 'sgu_w', 'sgu_b', 'sgu_norm_g', 'out_norm_a', 'out_norm_b', 'w_out',
                'ffn_norm_g', 'w_gate', 'w_up', 'w_down', 'ple_norm_g', 'w_ple_gate', 'w_ple_proj', 'final_norm_g']
SHARDED = ['w_in', 'w_out', 'w_gate', 'w_up', 'w_down', 'w_ple_gate', 'w_ple_proj']
SMALL = ['mix_norm_g', 'sgu_w', 'sgu_b', 'sgu_norm_g', 'out_norm_a', 'out_norm_b', 'ffn_norm_g', 'ple_norm_g',
         'final_norm_g']
SMALL_SIZES = {'mix_norm_g': 1024, 'sgu_w': 65536, 'sgu_b': 512, 'sgu_norm_g': 256, 'out_norm_a': 256,
               'out_norm_b': 768, 'ffn_norm_g': 1024, 'ple_norm_g': 1024, 'final_norm_g': 1024}
SMALL_ROWS = 72


def _params(semantics=None):
    return pltpu.CompilerParams(dimension_semantics=semantics, vmem_limit_bytes=VMEM_LIMIT)


def _full(shape):
    nd = len(shape)
    return pl.BlockSpec(shape, lambda i: (0,) * nd, pipeline_mode=pl.Buffered(1))


def _rows(tm, width):
    return pl.BlockSpec((tm, width), lambda i: (i, 0))


def _rms_stats(x):
    r = lax.rsqrt(jnp.mean(x * x, axis=-1, keepdims=True) + EPS)
    return x * r, r


def _rms_bwd(dn, n, r):
    return r * (dn - n * jnp.mean(dn * n, axis=-1, keepdims=True))


def _dot(a, b):
    return jnp.dot(a, b, preferred_element_type=F32)


def _dot_nt(a, b):
    return lax.dot_general(a, b, (((1,), (1,)), ((), ())), preferred_element_type=F32)


def _dot_tn(a, b):
    return lax.dot_general(a, b, (((0,), (0,)), ((), ())), preferred_element_type=F32)


def _gelu_parts(x):
    c = math.sqrt(2.0 / math.pi)
    t = jnp.tanh(c * (x + 0.044715 * x * x * x))
    return 0.5 * x * (1.0 + t), t


def _gelu_grad(x, t):
    c = math.sqrt(2.0 / math.pi)
    return 0.5 * (1.0 + t) + 0.5 * x * (1.0 - t * t) * c * (1.0 + 3.0 * 0.044715 * x * x)


def _half_masks(dtype):
    lane = lax.broadcasted_iota(jnp.int32, (BLOCK, LANES), 1)
    lo = (lane < HEAD_DIM).astype(F32)
    return lo.astype(dtype), (1.0 - lo).astype(dtype)


def _rope_partner(t):
    lane = lax.broadcasted_iota(jnp.int32, t.shape, 1)
    first_half = (lane % HEAD_DIM) < (HEAD_DIM // 2)
    return jnp.where(first_half, pltpu.roll(t, LANES - HEAD_DIM // 2, 1), pltpu.roll(t, HEAD_DIM // 2, 1))


PAIRS_ABREAST = 2
L_BLOCK = 256
L_GROUP = 16


def _store_l256(scr, out_ref, cols, value):
    tm = value.shape[0]
    scr[...] = value
    for blk in range(tm // L_BLOCK):
        for r in range(L_GROUP):
            lo = blk * L_BLOCK + r * L_GROUP
            piece = scr[pl.ds(blk * L_BLOCK + r, L_GROUP, stride=L_GROUP), :]
            out_ref[lo:lo + L_GROUP, cols] = piece.astype(out_ref.dtype)


def _load_l256(col_refs, tm):
    cols = []
    for ref in col_refs:
        pieces = [ref[pl.ds(blk * L_BLOCK + i, L_GROUP, stride=L_GROUP), :]
                  for blk in range(tm // L_BLOCK) for i in range(L_GROUP)]
        cols.append(jnp.concatenate(pieces, axis=0))
    return jnp.concatenate(cols, axis=1)


def _col_specs(tm, width):
    return [pl.BlockSpec((tm, LANES), lambda i, j=j: (i, j)) for j in range(width // LANES)]


def _in_fwd(x, g_mix, w_in_t, cos_t, sin_t):
    s = x.shape[0]
    tm = 512

    def body(x_ref, g_ref, wt_ref, cos_ref, sin_ref, uv_ref, q_ref, k_ref, v_ref, hn_ref, *scrs):
        n, _ = _rms_stats(x_ref[...])
        hn = (n * g_ref[...]).astype(MXU_DTYPE)
        hn_ref[...] = hn
        cos = cos_ref[...]
        sin = sin_ref[...]
        strip = 2 * LANES
        for j in range(IN_COLS // strip):
            proj = _dot_nt(hn, wt_ref[j * strip:(j + 1) * strip, :])
            lo = j * strip - 2 * WIDTH_A
            if lo < 0:
                uv_ref[:, j * strip:(j + 1) * strip] = proj
                continue
            which, lo = divmod(lo, WIDTH_B)
            for i in range(strip // LANES):
                t = proj[:, i * LANES:(i + 1) * LANES]
                cols = slice(lo + i * LANES, lo + (i + 1) * LANES)
                scr = scrs[i]
                if which == 0:
                    _store_l256(scr, q_ref, cols, (t * cos + _rope_partner(t) * sin) * (HEAD_DIM ** -0.5))
                elif which == 1:
                    _store_l256(scr, k_ref, cols, t * cos + _rope_partner(t) * sin)
                else:
                    _store_l256(scr, v_ref, cols, t)

    return pl.pallas_call(
        body, name="in_fwd", grid=(s // tm,), scratch_shapes=[pltpu.VMEM((tm, LANES), F32)] * 2,
        in_specs=[_rows(tm, D_MODEL), _full((1, D_MODEL)), _full((IN_COLS, D_MODEL)), _rows(tm, LANES),
                  _rows(tm, LANES)],
        out_specs=[_rows(tm, 2 * WIDTH_A), _rows(tm, WIDTH_B), _rows(tm, WIDTH_B), _rows(tm, WIDTH_B),
                   _rows(tm, D_MODEL)],
        out_shape=[jax.ShapeDtypeStruct((s, 2 * WIDTH_A), F32), jax.ShapeDtypeStruct((s, WIDTH_B), MXU_DTYPE),
                   jax.ShapeDtypeStruct((s, WIDTH_B), MXU_DTYPE), jax.ShapeDtypeStruct((s, WIDTH_B), MXU_DTYPE),
                   jax.ShapeDtypeStruct((s, D_MODEL), MXU_DTYPE)],
        compiler_params=_params(("arbitrary",)),
    )(x, g_mix, w_in_t, cos_t, sin_t)


class _Branch:
    def __init__(self, dil, s, qn=BLOCK):
        self.dil = dil
        i = np.arange(L_GROUP)
        if dil == 16:
            nblk = qn // 16
            self.grid = (16, s // (L_BLOCK * nblk))
            self.shape = (nblk, 1, 1, L_GROUP)
            self.index = lambda r, n: (n, r // 4, r % 4, 0, 0)
            pos = (np.arange(nblk)[:, None] * 16 + i[None, :]).reshape(-1)
        elif dil == 4:
            nblk = qn // 64
            self.grid = (4, s // (L_BLOCK * nblk))
            self.shape = (nblk, 4, 1, L_GROUP)
            self.index = lambda r, n: (n, 0, r, 0, 0)
            pos = (np.arange(nblk)[:, None, None] * 64 + np.arange(4)[None, :, None]
                   + 4 * i[None, None, :]).reshape(-1)
        else:
            self.grid = (1, s // L_BLOCK)
            self.shape = (1, 4, 4, L_GROUP)
            self.index = lambda r, n: (n, 0, 0, 0, 0)
            pos = (np.arange(16)[:, None] + 16 * i[None, :]).reshape(-1)
        self.qn = pos.shape[0]
        self.nb = self.grid[1]
        dist = pos[:, None] - np.concatenate([pos - self.qn, pos])[None, :]
        band = (dist >= 0) & (dist <= BLOCK)
        start = band & (np.arange(2 * self.qn)[None, :] >= self.qn)
        self.bias = np.where(np.stack([band, start]), 0.0, -np.inf).astype(np.float32)

    def view(self, a):
        return a.reshape(a.shape[0] // L_BLOCK, 4, 4, L_GROUP, a.shape[1])

    def spec(self, w, step=lambda n: n):
        return pl.BlockSpec(self.shape + (w,), lambda r, n: self.index(r, step(n)))

    def bias_spec(self, step=lambda n: n):
        return pl.BlockSpec((1, self.qn, 2 * self.qn), lambda r, n: (jnp.where(step(n) == 0, 1, 0), 0, 0))

    def load(self, ref, cols=slice(None)):
        x = ref[:, :, :, :, cols]
        return x.reshape(self.qn, x.shape[-1])

    def store(self, ref, cols, value):
        ref[:, :, :, :, cols] = value.reshape(self.shape + (value.shape[-1],))


def _attn_fwd_branch(q, k, v, dil, earlier=()):
    s = q.shape[0]
    br = _Branch(dil, s)
    qn = br.qn
    nearly = len(earlier)

    def body(bias_ref, q_ref, kc_ref, kp_ref, vc_ref, vp_ref, *refs):
        early_refs, (o_ref, lse_ref) = refs[:2 * nearly], refs[2 * nearly:]
        bias2 = jnp.concatenate([bias_ref[0], bias_ref[0]], axis=0)
        lo = lax.broadcasted_iota(jnp.int32, (qn, LANES), 1) < HEAD_DIM
        mask_lo = lo.astype(F32).astype(MXU_DTYPE)
        for hp in range(HEADS_B // 2):
            cols = slice(hp * LANES, (hp + 1) * LANES)
            qp = br.load(q_ref, cols)
            kcat = jnp.concatenate([br.load(kp_ref, cols), br.load(kc_ref, cols)], axis=0)
            vcat = jnp.concatenate([br.load(vp_ref, cols), br.load(vc_ref, cols)], axis=0)
            sc = _dot_nt(jnp.concatenate([qp * mask_lo, qp * (1 - mask_lo)], axis=0), kcat) + bias2
            m = jnp.max(sc, axis=1, keepdims=True)
            p = jnp.exp(sc - m)
            l = jnp.sum(p, axis=1, keepdims=True)
            out = _dot(p.astype(MXU_DTYPE), vcat) / l
            lse = m + jnp.log(l)
            outs = [br.load(r, cols) for r in early_refs[:nearly]] + [jnp.where(lo, out[:qn], out[qn:])]
            lses = [br.load(r, cols) for r in early_refs[nearly:]] + [jnp.where(lo, lse[:qn], lse[qn:])]
            if nearly:
                top = functools.reduce(jnp.maximum, lses)
                ws = [jnp.exp(x - top) for x in lses]
                den = functools.reduce(jnp.add, ws)
                outs = [functools.reduce(jnp.add, [w * o for w, o in zip(ws, outs)]) / den]
                lses = [top + jnp.log(den)]
            br.store(o_ref, cols, outs[0])
            br.store(lse_ref, cols, lses[0])

    before = lambda n: jnp.maximum(n - 1, 0)
    res = pl.pallas_call(
        body, name="attn_fwd_d%d" % dil, grid=br.grid,
        in_specs=[br.bias_spec(), br.spec(WIDTH_B), br.spec(WIDTH_B), br.spec(WIDTH_B, before), br.spec(WIDTH_B),
                  br.spec(WIDTH_B, before)] + [br.spec(WIDTH_B)] * (2 * nearly),
        out_specs=[br.spec(WIDTH_B), br.spec(WIDTH_B)],
        out_shape=[jax.ShapeDtypeStruct((s // L_BLOCK, 4, 4, L_GROUP, WIDTH_B), F32)] * 2,
        compiler_params=_params(("arbitrary", "arbitrary")),
    )(jnp.asarray(br.bias), br.view(q), br.view(k), br.view(k), br.view(v), br.view(v),
      *[br.view(o) for o, _ in earlier], *[br.view(x) for _, x in earlier])
    return tuple(a.reshape(s, WIDTH_B) for a in res)


def _sgu_forward_tile(uv, w_ref, bias, g_sgu):
    tm = uv.shape[0]
    u = uv[:, :WIDTH_A]
    v = uv[:, WIDTH_A:]
    ug, tu = _gelu_parts(u)
    vg, tv = _gelu_parts(v)
    mu = jnp.mean(vg, axis=-1, keepdims=True)
    vc = vg - mu
    rs = lax.rsqrt(jnp.mean(vc * vc, axis=-1, keepdims=True) + EPS)
    vhat = vc * rs
    vn = (vhat * g_sgu).astype(MXU_DTYPE)
    masks = _half_masks(MXU_DTYPE)
    chunks = []
    for c in range(tm // CHUNK):
        rows = slice(c * CHUNK, (c + 1) * CHUNK)
        groups = []
        for gp in range(2):
            vn_g = vn[rows, gp * LANES:(gp + 1) * LANES]
            groups.append(_dot(w_ref[2 * gp], vn_g * masks[0]) + _dot(w_ref[2 * gp + 1], vn_g * masks[1]))
        chunks.append(jnp.concatenate(groups, axis=1) + bias)
    mixed = jnp.concatenate(chunks, axis=0)
    return dict(u=u, v=v, ug=ug, tu=tu, tv=tv, rs=rs, vhat=vhat, vn=vn, mixed=mixed, ya=ug * mixed)


def _sgu_fwd(uv, w_tril, bias, g_sgu, g_a):
    s = uv.shape[0]
    tm = 512

    def body(uv_ref, w_ref, b_ref, gs_ref, ga_ref, o_ref):
        t = _sgu_forward_tile(uv_ref[...], w_ref, b_ref[...], gs_ref[...])
        n, _ = _rms_stats(t['ya'])
        o_ref[...] = (n * ga_ref[...]).astype(MXU_DTYPE)

    return pl.pallas_call(
        body, name="sgu_fwd", grid=(s // tm,),
        in_specs=[_rows(tm, 2 * WIDTH_A), _full((HEADS_A, CHUNK, CHUNK)), _full((CHUNK, WIDTH_A)),
                  _full((1, WIDTH_A)), _full((1, WIDTH_A))],
        out_specs=_rows(tm, WIDTH_A), out_shape=jax.ShapeDtypeStruct((s, WIDTH_A), MXU_DTYPE),
        compiler_params=_params(("arbitrary",)),
    )(uv, w_tril, bias, g_sgu, g_a)


def _out_fwd(ya_n, y_b, g_b, w_out, x):
    s = x.shape[0]
    tm = 512
    nc = WIDTH_B // LANES

    def body(ya_ref, *refs):
        yb_refs = refs[:nc]
        g_ref, w_ref, x_ref, h_ref, yn_ref = refs[nc:]
        n, _ = _rms_stats(_load_l256(yb_refs, tm))
        yn = jnp.concatenate([ya_ref[...], (n * g_ref[...]).astype(MXU_DTYPE)], axis=1)
        yn_ref[...] = yn
        h_ref[...] = x_ref[...] + _dot(yn, w_ref[...])

    return pl.pallas_call(
        body, name="out_fwd", grid=(s // tm,),
        in_specs=[_rows(tm, WIDTH_A)] + _col_specs(tm, WIDTH_B) + [_full((1, WIDTH_B)), _full((D_MODEL, D_MODEL)),
                                                                 _rows(tm, D_MODEL)],
        out_specs=[_rows(tm, D_MODEL), _rows(tm, D_MODEL)],
        out_shape=[jax.ShapeDtypeStruct((s, D_MODEL), F32), jax.ShapeDtypeStruct((s, D_MODEL), MXU_DTYPE)],
        compiler_params=_params(("arbitrary",)),
    )(ya_n, *([y_b] * nc), g_b, w_out, x)


def _ffn_fwd(h1, g_ffn, w_gate_t, w_up_t, w_down):
    s = h1.shape[0]
    tm = 512

    def body(h_ref, g_ref, wgt_ref, wut_ref, wd_ref, o_ref, gate_ref, up_ref, hn_ref):
        h = h_ref[...]
        n, _ = _rms_stats(h)
        hn = (n * g_ref[...]).astype(MXU_DTYPE)
        hn_ref[...] = hn
        strips = [dict(cols=slice(lo, hi)) for lo, hi in FF_STRIPS]

        def project(t):
            t['gate'] = _dot_nt(hn, wgt_ref[t['cols'], :])
            t['up'] = _dot_nt(hn, wut_ref[t['cols'], :])

        def activate(t):
            gate, up = t['gate'], t['up']
            gate_ref[:, t['cols']] = gate.astype(MXU_DTYPE)
            up_ref[:, t['cols']] = up.astype(MXU_DTYPE)
            t['act'] = (gate * jax.nn.sigmoid(gate) * up).astype(MXU_DTYPE)

        def down(t):
            return _dot(t['act'], wd_ref[t['cols'], :])

        out = h
        project(strips[0])
        for i, t in enumerate(strips):
            if i + 1 < len(strips):
                project(strips[i + 1])
            activate(t)
            out = out + down(t)
        o_ref[...] = out

    return pl.pallas_call(
        body, name="ffn_fwd", grid=(s // tm,),
        in_specs=[_rows(tm, D_MODEL), _full((1, D_MODEL)), _full((D_FF, D_MODEL)), _full((D_FF, D_MODEL)),
                  _full((D_FF, D_MODEL))],
        out_specs=[_rows(tm, D_MODEL), _rows(tm, D_FF), _rows(tm, D_FF), _rows(tm, D_MODEL)],
        out_shape=[jax.ShapeDtypeStruct((s, D_MODEL), F32), jax.ShapeDtypeStruct((s, D_FF), MXU_DTYPE),
                   jax.ShapeDtypeStruct((s, D_FF), MXU_DTYPE), jax.ShapeDtypeStruct((s, D_MODEL), MXU_DTYPE)],
        compiler_params=_params(("arbitrary",)),
    )(h1, g_ffn, w_gate_t, w_up_t, w_down)


def _ple_loss(h2, p, target, g_ple, w_pg, w_pp_t, g_final):
    s = h2.shape[0]
    tm = 256

    def body(h_ref, p_ref, t_ref, gp_ref, wg_ref, wpt_ref, gf_ref,
             loss_ref, dh_ref, dz_ref, dpp_ref, hn_ref, dgp_ref, dgf_ref):
        @pl.when(pl.program_id(0) == 0)
        def _():
            loss_ref[...] = jnp.zeros_like(loss_ref)
            dgp_ref[...] = jnp.zeros_like(dgp_ref)
            dgf_ref[...] = jnp.zeros_like(dgf_ref)

        h2t = h_ref[...]
        n2, r2 = _rms_stats(h2t)
        hn = (n2 * gp_ref[...]).astype(MXU_DTYPE)
        hn_ref[...] = hn
        gate = jax.nn.sigmoid(_dot(hn, wg_ref[...]))
        pp = _dot_nt(p_ref[...].astype(MXU_DTYPE), wpt_ref[...])
        h3 = h2t + gate * pp
        n3, r3 = _rms_stats(h3)
        diff = n3 * gf_ref[...] - t_ref[...]
        loss_ref[...] += jnp.full(loss_ref.shape, 0.5 * jnp.sum(diff * diff) / D_MODEL, F32)
        dy = diff * (1.0 / D_MODEL)
        dgf_ref[...] += jnp.sum(dy * n3, axis=0, keepdims=True)
        dh3 = _rms_bwd(dy * gf_ref[...], n3, r3)
        dpp_ref[...] = (dh3 * gate).astype(MXU_DTYPE)
        dz = (dh3 * pp * gate * (1.0 - gate)).astype(MXU_DTYPE)
        dz_ref[...] = dz
        dhn = _dot_nt(dz, wg_ref[...])
        dgp_ref[...] += jnp.sum(dhn * n2, axis=0, keepdims=True)
        dh_ref[...] = dh3 + _rms_bwd(dhn * gp_ref[...], n2, r2)

    return pl.pallas_call(
        body, name="ple_loss", grid=(s // tm,),
        in_specs=[_rows(tm, D_MODEL), _rows(tm, PLE_DIM), _rows(tm, D_MODEL), _full((1, D_MODEL)),
                  _full((D_MODEL, D_MODEL)), _full((D_MODEL, PLE_DIM)), _full((1, D_MODEL))],
        out_specs=[_full((1, LANES)), _rows(tm, D_MODEL), _rows(tm, D_MODEL), _rows(tm, D_MODEL),
                   _rows(tm, D_MODEL), _full((1, D_MODEL)), _full((1, D_MODEL))],
        out_shape=[jax.ShapeDtypeStruct((1, LANES), F32), jax.ShapeDtypeStruct((s, D_MODEL), F32),
                   jax.ShapeDtypeStruct((s, D_MODEL), MXU_DTYPE), jax.ShapeDtypeStruct((s, D_MODEL), MXU_DTYPE),
                   jax.ShapeDtypeStruct((s, D_MODEL), MXU_DTYPE), jax.ShapeDtypeStruct((1, D_MODEL), F32),
                   jax.ShapeDtypeStruct((1, D_MODEL), F32)],
        compiler_params=_params(("arbitrary",)),
    )(h2, p, target, g_ple, w_pg, w_pp_t, g_final)


def _ffn_bwd(dh2, h1, gate, up, g_ffn, w_down, w_gate_t, w_up_t):
    s = h1.shape[0]
    tm = 256

    def body(dh_ref, h_ref, gate_ref, up_ref, g_ref, wd_ref, wgt_ref, wut_ref,
             o_ref, act_ref, dg_ref, du_ref, dgn_ref):
        @pl.when(pl.program_id(0) == 0)
        def _():
            dgn_ref[...] = jnp.zeros_like(dgn_ref)

        dh = dh_ref[...]
        dhb = dh.astype(MXU_DTYPE)
        strips = [dict(cols=slice(lo, hi)) for lo, hi in FF_STRIPS]

        def back_down(t):
            t['dact'] = _dot_nt(dhb, wd_ref[t['cols'], :])

        def back_act(t):
            cols, dact = t['cols'], t['dact']
            g = gate_ref[:, cols].astype(F32)
            u = up_ref[:, cols].astype(F32)
            sg = jax.nn.sigmoid(g)
            silu = g * sg
            act_ref[:, cols] = (silu * u).astype(MXU_DTYPE)
            t['du'] = (dact * silu).astype(MXU_DTYPE)
            t['dg'] = (dact * u * sg * (1.0 + g * (1.0 - sg))).astype(MXU_DTYPE)
            du_ref[:, cols] = t['du']
            dg_ref[:, cols] = t['dg']

        def back_in(t):
            return _dot(t['dg'], wgt_ref[t['cols'], :]) + _dot(t['du'], wut_ref[t['cols'], :])

        dhn = jnp.zeros((tm, D_MODEL), F32)
        back_down(strips[0])
        for i, t in enumerate(strips):
            if i + 1 < len(strips):
                back_down(strips[i + 1])
            back_act(t)
            dhn = dhn + back_in(t)
        n, r = _rms_stats(h_ref[...])
        dgn_ref[...] += jnp.sum(dhn * n, axis=0, keepdims=True)
        o_ref[...] = dh + _rms_bwd(dhn * g_ref[...], n, r)

    return pl.pallas_call(
        body, name="ffn_bwd", grid=(s // tm,),
        in_specs=[_rows(tm, D_MODEL), _rows(tm, D_MODEL), _rows(tm, D_FF), _rows(tm, D_FF), _full((1, D_MODEL)),
                  _full((D_FF, D_MODEL)), _full((D_FF, D_MODEL)), _full((D_FF, D_MODEL))],
        out_specs=[_rows(tm, D_MODEL), _rows(tm, D_FF), _rows(tm, D_FF), _rows(tm, D_FF), _full((1, D_MODEL))],
        out_shape=[jax.ShapeDtypeStruct((s, D_MODEL), F32), jax.ShapeDtypeStruct((s, D_FF), MXU_DTYPE),
                   jax.ShapeDtypeStruct((s, D_FF), MXU_DTYPE), jax.ShapeDtypeStruct((s, D_FF), MXU_DTYPE),
                   jax.ShapeDtypeStruct((1, D_MODEL), F32)],
        compiler_params=_params(("arbitrary",)),
    )(dh2, h1, gate, up, g_ffn, w_down, w_gate_t, w_up_t)


def _out_bwd(dh1, y_b, g_b, w_out):
    s = dh1.shape[0]
    tm = 512
    nc = WIDTH_B // LANES

    def body(dh_ref, *refs):
        yb_refs = refs[:nc]
        g_ref, w_ref, dya_ref, dyb_ref, dg_ref, scr = refs[nc:]

        @pl.when(pl.program_id(0) == 0)
        def _():
            dg_ref[...] = jnp.zeros_like(dg_ref)

        dy = _dot_nt(dh_ref[...].astype(MXU_DTYPE), w_ref[...])
        dya_ref[...] = dy[:, :WIDTH_A]
        dyb = dy[:, WIDTH_A:]
        n, r = _rms_stats(_load_l256(yb_refs, tm))
        dg_ref[...] += jnp.sum(dyb * n, axis=0, keepdims=True)
        dyb_in = _rms_bwd(dyb * g_ref[...], n, r)
        for j in range(nc):
            cols = slice(j * LANES, (j + 1) * LANES)
            _store_l256(scr, dyb_ref, cols, dyb_in[:, cols])

    return pl.pallas_call(
        body, name="out_bwd", grid=(s // tm,), scratch_shapes=[pltpu.VMEM((tm, LANES), F32)],
        in_specs=[_rows(tm, D_MODEL)] + _col_specs(tm, WIDTH_B) + [_full((1, WIDTH_B)), _full((D_MODEL, D_MODEL))],
        out_specs=[_rows(tm, WIDTH_A), _rows(tm, WIDTH_B), _full((1, WIDTH_B))],
        out_shape=[jax.ShapeDtypeStruct((s, WIDTH_A), F32), jax.ShapeDtypeStruct((s, WIDTH_B), F32),
                   jax.ShapeDtypeStruct((1, WIDTH_B), F32)],
        compiler_params=_params(("arbitrary",)),
    )(dh1, *([y_b] * nc), g_b, w_out)


def _attn_bwd_branch(q, k, v, do, o, lse, grads, dil):
    s = q.shape[0]
    br = _Branch(dil, s)
    qn, nb = br.qn, br.nb
    first = grads is None

    def body(*refs):
        bias_ref, q_ref, kc_ref, kp_ref, vc_ref, vp_ref, do_ref, o_ref, lse_ref = refs[:9]
        if first:
            rest = refs[9:]
        else:
            dq_in, dk_in, dv_in = refs[9:12]
            rest = refs[12:]
        dq_ref, dk_ref, dv_ref, dk_carry, dv_carry = rest
        n = pl.program_id(1)

        @pl.when(n == 0)
        def _():
            dk_carry[...] = jnp.zeros_like(dk_carry)
            dv_carry[...] = jnp.zeros_like(dv_carry)

        @pl.when(n < nb)
        def _():
            bias2 = jnp.concatenate([bias_ref[0], bias_ref[0]], axis=0)
            lane = lax.broadcasted_iota(jnp.int32, (qn, LANES), 1)
            lo = lane < HEAD_DIM
            mask_f = lo.astype(F32)
            mask_lo = mask_f.astype(MXU_DTYPE)
            def prepare(hp):
                cols = slice(hp * LANES, (hp + 1) * LANES)
                qp = br.load(q_ref, cols)
                dop = br.load(do_ref, cols)
                prod = dop * br.load(o_ref, cols)
                prod_lo = prod * mask_f
                lse = br.load(lse_ref, cols)
                return dict(
                    cols=cols,
                    kcat=jnp.concatenate([br.load(kp_ref, cols), br.load(kc_ref, cols)], axis=0),
                    vcat=jnp.concatenate([br.load(vp_ref, cols), br.load(vc_ref, cols)], axis=0),
                    qs=jnp.concatenate([qp * mask_lo, qp * (1 - mask_lo)], axis=0),
                    dos=jnp.concatenate([dop * mask_f, dop * (1.0 - mask_f)], axis=0).astype(MXU_DTYPE),
                    delta=jnp.concatenate([jnp.sum(prod_lo, axis=1, keepdims=True),
                                           jnp.sum(prod - prod_lo, axis=1, keepdims=True)], axis=0),
                    lse2=jnp.concatenate([lse[:, :1], lse[:, HEAD_DIM:HEAD_DIM + 1]], axis=0))

            def scores(t):
                t['sc'] = _dot_nt(t['qs'], t['kcat'])
                t['dp'] = _dot_nt(t['dos'], t['vcat'])

            def softmax(t):
                p = jnp.exp(t['sc'] + bias2 - t['lse2'])
                t['ds'] = (p * (t['dp'] - t['delta'])).astype(MXU_DTYPE)
                t['p'] = p.astype(MXU_DTYPE)

            def gradients(t):
                t['dvc'] = _dot_tn(t['p'], t['dos'])
                t['dkc'] = _dot_tn(t['ds'], t['qs'])
                t['dq2'] = _dot(t['ds'], t['kcat'])

            def store(t):
                cols, dkc, dvc = t['cols'], t['dkc'], t['dvc']
                dq = jnp.where(lo, t['dq2'][:qn], t['dq2'][qn:])
                dk_prev = dk_carry[:, cols] + dkc[:qn]
                dv_prev = dv_carry[:, cols] + dvc[:qn]
                if not first:
                    dq = dq + br.load(dq_in, cols)
                    dk_prev = dk_prev + br.load(dk_in, cols)
                    dv_prev = dv_prev + br.load(dv_in, cols)
                br.store(dq_ref, cols, dq)
                br.store(dk_ref, cols, dk_prev)
                br.store(dv_ref, cols, dv_prev)
                dk_carry[:, cols] = dkc[qn:]
                dv_carry[:, cols] = dvc[qn:]

            for first_pair in range(0, HEADS_B // 2, PAIRS_ABREAST):
                group = [prepare(hp) for hp in range(first_pair, first_pair + PAIRS_ABREAST)]
                for stage in (scores, softmax, gradients, store):
                    for t in group:
                        stage(t)

        @pl.when(n == nb)
        def _():
            dk_last = dk_carry[...]
            dv_last = dv_carry[...]
            if not first:
                dk_last = dk_last + br.load(dk_in)
                dv_last = dv_last + br.load(dv_in)
            br.store(dk_ref, slice(None), dk_last)
            br.store(dv_ref, slice(None), dv_last)

    cur = lambda n: jnp.minimum(n, nb - 1)
    before = lambda n: jnp.maximum(cur(n) - 1, 0)
    late = lambda n: jnp.maximum(n - 1, 0)
    in_specs = [br.bias_spec(cur), br.spec(WIDTH_B, cur), br.spec(WIDTH_B, cur), br.spec(WIDTH_B, before),
                br.spec(WIDTH_B, cur), br.spec(WIDTH_B, before), br.spec(WIDTH_B, cur), br.spec(WIDTH_B, cur),
                br.spec(WIDTH_B, cur)]
    args = [jnp.asarray(br.bias)] + [br.view(a) for a in (q, k, k, v, v, do, o, lse)]
    if not first:
        in_specs += [br.spec(WIDTH_B, cur), br.spec(WIDTH_B, late), br.spec(WIDTH_B, late)]
        args += [br.view(g) for g in grads]
    res = pl.pallas_call(
        body, name="attn_bwd_d%d" % dil, grid=(br.grid[0], nb + 1), in_specs=in_specs,
        out_specs=[br.spec(WIDTH_B, cur), br.spec(WIDTH_B, late), br.spec(WIDTH_B, late)],
        out_shape=[jax.ShapeDtypeStruct((s // L_BLOCK, 4, 4, L_GROUP, WIDTH_B), F32)] * 3,
        scratch_shapes=[pltpu.VMEM((qn, WIDTH_B), F32), pltpu.VMEM((qn, WIDTH_B), F32)],
        compiler_params=_params(("arbitrary", "arbitrary")),
    )(*args)
    return tuple(a.reshape(s, WIDTH_B) for a in res)


def _sgu_bwd(uv, dya_n, w_tril, w_tril_t, bias, g_sgu, g_a):
    s = uv.shape[0]
    tm = 512

    def body(uv_ref, dy_ref, w_ref, wt_ref, b_ref, gs_ref, ga_ref, duv_ref, dw_ref, db_ref, dgs_ref, dga_ref,
             db_acc):
        i = pl.program_id(0)

        @pl.when(i == 0)
        def _():
            dw_ref[...] = jnp.zeros_like(dw_ref)
            dgs_ref[...] = jnp.zeros_like(dgs_ref)
            dga_ref[...] = jnp.zeros_like(dga_ref)
            db_acc[...] = jnp.zeros_like(db_acc)

        t = _sgu_forward_tile(uv_ref[...], w_ref, b_ref[...], gs_ref[...])
        na, ra = _rms_stats(t['ya'])
        dyn = dy_ref[...]
        dga_ref[...] += jnp.sum(dyn * na, axis=0, keepdims=True)
        dya = _rms_bwd(dyn * ga_ref[...], na, ra)
        dug = dya * t['mixed']
        dmixed = dya * t['ug']
        dmb = dmixed.astype(MXU_DTYPE)
        masks = _half_masks(MXU_DTYPE)
        chunks = []
        db = jnp.zeros((CHUNK, WIDTH_A), F32)
        for c in range(tm // CHUNK):
            rows = slice(c * CHUNK, (c + 1) * CHUNK)
            db = db + dmixed[rows]
            groups = []
            for gp in range(2):
                cols = slice(gp * LANES, (gp + 1) * LANES)
                dm_g = dmb[rows, cols]
                vn_g = t['vn'][rows, cols]
                dvn_g = jnp.zeros((CHUNK, LANES), F32)
                for j in range(2):
                    dm_h = dm_g * masks[j]
                    dvn_g = dvn_g + _dot(wt_ref[2 * gp + j], dm_h)
                    dw_ref[2 * gp + j] += _dot_nt(dm_h, vn_g)
                groups.append(dvn_g)
            chunks.append(jnp.concatenate(groups, axis=1))
        db_acc[...] += db
        dvn = jnp.concatenate(chunks, axis=0)
        vhat = t['vhat']
        dgs_ref[...] += jnp.sum(dvn * vhat, axis=0, keepdims=True)
        dvh = dvn * gs_ref[...]
        dvg = t['rs'] * (dvh - jnp.mean(dvh, axis=-1, keepdims=True)
                         - vhat * jnp.mean(dvh * vhat, axis=-1, keepdims=True))
        duv_ref[:, :WIDTH_A] = (dug * _gelu_grad(t['u'], t['tu'])).astype(MXU_DTYPE)
        duv_ref[:, WIDTH_A:] = (dvg * _gelu_grad(t['v'], t['tv'])).astype(MXU_DTYPE)

        @pl.when(i == pl.num_programs(0) - 1)
        def _():
            lane_a = lax.broadcasted_iota(jnp.int32, (CHUNK, WIDTH_A), 1)
            lane = lax.broadcasted_iota(jnp.int32, (CHUNK, LANES), 1)
            acc = db_acc[...]
            out = jnp.zeros((CHUNK, LANES), F32)
            for h in range(HEADS_A):
                col = jnp.sum(jnp.where(lane_a // HEAD_DIM == h, acc, 0.0), axis=1, keepdims=True)
                out = jnp.where(lane == h, col, out)
            db_ref[...] = out
            causal = (lax.broadcasted_iota(jnp.int32, (CHUNK, CHUNK), 0)
                      >= lax.broadcasted_iota(jnp.int32, (CHUNK, CHUNK), 1))
            for h in range(HEADS_A):
                dw_ref[h] = jnp.where(causal, dw_ref[h], 0.0)

    return pl.pallas_call(
        body, name="sgu_bwd", grid=(s // tm,),
        in_specs=[_rows(tm, 2 * WIDTH_A), _rows(tm, WIDTH_A), _full((HEADS_A, CHUNK, CHUNK)),
                  _full((HEADS_A, CHUNK, CHUNK)), _full((CHUNK, WIDTH_A)), _full((1, WIDTH_A)),
                  _full((1, WIDTH_A))],
        out_specs=[_rows(tm, 2 * WIDTH_A), _full((HEADS_A, CHUNK, CHUNK)), _full((CHUNK, LANES)),
                   _full((1, WIDTH_A)), _full((1, WIDTH_A))],
        out_shape=[jax.ShapeDtypeStruct((s, 2 * WIDTH_A), MXU_DTYPE),
                   jax.ShapeDtypeStruct((HEADS_A, CHUNK, CHUNK), F32), jax.ShapeDtypeStruct((CHUNK, LANES), F32),
                   jax.ShapeDtypeStruct((1, WIDTH_A), F32), jax.ShapeDtypeStruct((1, WIDTH_A), F32)],
        scratch_shapes=[pltpu.VMEM((CHUNK, WIDTH_A), F32)],
        compiler_params=_params(("arbitrary",)),
    )(uv, dya_n, w_tril, w_tril_t, bias, g_sgu, g_a)


def _in_bwd_proj(duv, dq, dk, dv, cos_t, sin_t):
    s = duv.shape[0]
    tm = 512
    nc = WIDTH_B // LANES

    def body(duv_ref, *refs):
        dq_refs, dk_refs, dv_refs = refs[:nc], refs[nc:2 * nc], refs[2 * nc:3 * nc]
        cos_ref, sin_ref, dp_ref = refs[3 * nc:]
        cos = cos_ref[...]
        sin = sin_ref[...]
        dp_ref[:, :2 * WIDTH_A] = duv_ref[...]
        for i in range(nc):
            lo = 2 * WIDTH_A + i * LANES
            tq = _load_l256(dq_refs[i:i + 1], tm) * (HEAD_DIM ** -0.5)
            tk = _load_l256(dk_refs[i:i + 1], tm)
            dp_ref[:, lo:lo + LANES] = (tq * cos + _rope_partner(tq * sin)).astype(MXU_DTYPE)
            dp_ref[:, lo + WIDTH_B:lo + WIDTH_B + LANES] = (tk * cos + _rope_partner(tk * sin)).astype(MXU_DTYPE)
            dp_ref[:, lo + 2 * WIDTH_B:lo + 2 * WIDTH_B + LANES] = _load_l256(dv_refs[i:i + 1], tm).astype(MXU_DTYPE)

    return pl.pallas_call(
        body, name="in_bwd_proj", grid=(s // tm,),
        in_specs=[_rows(tm, 2 * WIDTH_A)] + 3 * _col_specs(tm, WIDTH_B) + [_rows(tm, LANES), _rows(tm, LANES)],
        out_specs=_rows(tm, IN_COLS), out_shape=jax.ShapeDtypeStruct((s, IN_COLS), MXU_DTYPE),
        compiler_params=_params(("arbitrary",)),
    )(duv, *([dq] * nc), *([dk] * nc), *([dv] * nc), cos_t, sin_t)


def _in_bwd_x(dproj, w_in_t, x, g_mix, dh1, after):
    s = x.shape[0]
    tm = 512

    def body(dp_ref, wt_ref, x_ref, g_ref, dh_ref, *rest):
        gx_ref, dg_ref = rest[len(after):]
        @pl.when(pl.program_id(0) == 0)
        def _():
            dg_ref[...] = jnp.zeros_like(dg_ref)

        dhn = _dot(dp_ref[...], wt_ref[...])
        n, r = _rms_stats(x_ref[...])
        dg_ref[...] += jnp.sum(dhn * n, axis=0, keepdims=True)
        gx_ref[...] = dh_ref[...] + _rms_bwd(dhn * g_ref[...], n, r)

    return pl.pallas_call(
        body, name="in_bwd_x", grid=(s // tm,),
        in_specs=[_rows(tm, IN_COLS), _full((IN_COLS, D_MODEL)), _rows(tm, D_MODEL), _full((1, D_MODEL)),
                  _rows(tm, D_MODEL)] + [pl.BlockSpec(memory_space=pl.ANY)] * len(after),
        out_specs=[_rows(tm, D_MODEL), _full((1, D_MODEL))],
        out_shape=[jax.ShapeDtypeStruct((s, D_MODEL), F32), jax.ShapeDtypeStruct((1, D_MODEL), F32)],
        compiler_params=_params(("arbitrary",)),
    )(dproj, w_in_t, x, g_mix, dh1, *after)


def _wgrad(a, b, name):
    s, m = a.shape
    n = b.shape[1]
    bm = 512 if m % 512 == 0 else FF_HALF
    ts = 1024
    nsteps = s // ts

    def body(a_ref, b_ref, o_ref, acc):
        kk = pl.program_id(1)

        @pl.when(kk == 0)
        def _():
            acc[...] = jnp.zeros_like(acc)

        acc[...] += _dot_tn(a_ref[...].astype(MXU_DTYPE), b_ref[...].astype(MXU_DTYPE))

        @pl.when(kk == nsteps - 1)
        def _():
            o_ref[...] = acc[...].astype(o_ref.dtype)

    return pl.pallas_call(
        body, name=name, grid=(m // bm, nsteps),
        in_specs=[pl.BlockSpec((ts, bm), lambda i, kk: (kk, i)), pl.BlockSpec((ts, n), lambda i, kk: (kk, 0))],
        out_specs=pl.BlockSpec((bm, n), lambda i, kk: (i, 0)), out_shape=jax.ShapeDtypeStruct((m, n), jnp.bfloat16),
        scratch_shapes=[pltpu.VMEM((bm, n), F32)],
        compiler_params=_params(("arbitrary", "arbitrary")),
    )(a, b)


def _rope_tables(s):
    half = HEAD_DIM // 2
    inv = ROPE_THETA ** (-jnp.arange(half, dtype=F32) / half)
    ang = jnp.arange(s, dtype=F32)[:, None] * jnp.tile(inv, LANES // half)[None, :]
    sign = jnp.tile(jnp.concatenate([-jnp.ones(half, F32), jnp.ones(half, F32)]), LANES // HEAD_DIM)
    return jnp.cos(ang), jnp.sin(ang) * sign[None, :]


MESH = pl.DeviceIdType.MESH
ANY = pl.BlockSpec(memory_space=pl.ANY)
SEM = pl.BlockSpec(memory_space=pltpu.SEMAPHORE)
SPLIT_COPY = pltpu.CompilerParams(has_side_effects=pltpu.SideEffectType.DATAFLOW_SIDE_EFFECTING)
SLAB_IS_TRANSPOSED = {'w_in': True, 'w_out': False, 'w_gate': True, 'w_up': True, 'w_down': False,
                      'w_ple_gate': False, 'w_ple_proj': True}


def _place():
    x, y, c = lax.axis_index("x"), lax.axis_index("y"), lax.axis_index("c")
    other_chips = [(1 - x, y), (x, 1 - y), (1 - x, 1 - y)]
    return x, y, c, other_chips


def _chip_of(chip):
    return 2 * chip[0] + chip[1]


def _half(ref, lead, hc):
    hr = ref.shape[1] // 2
    return ref.at[lead, pl.ds(hc * hr, hr), :]


def _put_own(stack, own, index):
    return lax.dynamic_update_slice(stack, own[None], (index,) + (0,) * own.ndim)


def _all_gather_now(slab):
    rows, cols = slab.shape

    def body(x_ref, out_ref, send_sems, recv_sems):
        x, y, c, chips = _place()
        sibling = (x, y, 1 - c)
        hr = rows // 2

        def copy(k, src, dst, to):
            return pltpu.make_async_remote_copy(src_ref=src, dst_ref=dst, send_sem=send_sems.at[k],
                                                recv_sem=recv_sems.at[k], device_id=to, device_id_type=MESH)

        my_half = x_ref.at[pl.ds(c * hr, hr), :]
        first = [copy(j, my_half, _half(out_ref, 2 * x + y, c), (*chip, c)) for j, chip in enumerate(chips)]
        for cp in first:
            cp.start()
        passed = [copy(3 + j, _half(out_ref, _chip_of(chip), c), _half(out_ref, _chip_of(chip), c), sibling)
                  for j, chip in enumerate(chips)]
        for j, chip in enumerate(chips):
            copy(j, my_half, _half(out_ref, _chip_of(chip), c), (*chip, c)).wait_recv()
            passed[j].start()
        for j, chip in enumerate(chips):
            copy(3 + j, my_half, _half(out_ref, _chip_of(chip), 1 - c), sibling).wait_recv()
        for cp in first + passed:
            cp.wait_send()

    gathered = pl.pallas_call(
        body, name="all_gather_now", out_shape=jax.ShapeDtypeStruct((N_CHIPS, rows, cols), slab.dtype),
        in_specs=[ANY], out_specs=ANY,
        scratch_shapes=[pltpu.SemaphoreType.DMA((6,)), pltpu.SemaphoreType.DMA((6,))],
    )(slab)
    me = 2 * lax.axis_index("x") + lax.axis_index("y")
    return _put_own(gathered, slab, me).reshape(N_CHIPS * rows, cols)


def _gather_copies(slab_refs, land_refs, send_sems, recv_sems):
    x, y, c, chips = _place()
    sends, recvs = [], []
    for k, (src, land) in enumerate(zip(slab_refs, land_refs)):
        hr = src.shape[0] // 2
        for j, chip in enumerate(chips):
            for t in range(2):
                sends.append(pltpu.make_async_remote_copy(
                    src_ref=src.at[pl.ds(c * hr, hr), :], dst_ref=_half(land, 2 * x + y, c),
                    send_sem=send_sems.at[6 * k + 2 * j + t], recv_sem=recv_sems.at[6 * k + 2 * j + c],
                    device_id=(*chip, t), device_id_type=MESH))
                recvs.append(pltpu.make_async_remote_copy(
                    src_ref=src.at[pl.ds(t * hr, hr), :], dst_ref=_half(land, _chip_of(chip), t),
                    send_sem=send_sems.at[6 * k + 2 * j + t], recv_sem=recv_sems.at[6 * k + 2 * j + t],
                    device_id=(*chip, t), device_id_type=MESH))
    return sends, recvs


def _all_gather_start(slabs, after):
    n = len(slabs)

    def body(*refs):
        slab_refs, land_refs = refs[:n], refs[n:2 * n]
        send_sems, recv_sems = refs[2 * n + 1:2 * n + 3]
        token = refs[-1]
        sends, _ = _gather_copies(slab_refs, land_refs, send_sems, recv_sems)
        for cp in sends:
            cp.start()
        token[...] = jnp.zeros_like(token)

    lands = [lax.empty((N_CHIPS,) + s.shape, s.dtype) for s in slabs]
    hbm = lambda a: pltpu.HBM(a.shape, a.dtype)
    res = pl.pallas_call(
        body, name="all_gather_start",
        out_shape=(pltpu.SemaphoreType.DMA((6 * n,)), pltpu.SemaphoreType.DMA((6 * n,)), *map(hbm, slabs),
                   *map(hbm, lands), jax.ShapeDtypeStruct((8, LANES), F32)),
        in_specs=[ANY] * (2 * n + 1),
        out_specs=(SEM, SEM, *([ANY] * (2 * n)), pl.BlockSpec(memory_space=pltpu.VMEM)),
        input_output_aliases={i: 2 + i for i in range(2 * n)}, compiler_params=SPLIT_COPY,
    )(*[pltpu.with_memory_space_constraint(a, pltpu.HBM) for a in list(slabs) + lands], after)
    return res[:-1], res[-1]


def _all_gather_wait(handle, after):
    send_sems, recv_sems = handle[:2]
    n = (len(handle) - 2) // 2
    slabs, lands = handle[2:2 + n], handle[2 + n:]

    def body(*refs):
        slab_refs, land_refs = refs[:n], refs[n:2 * n]
        send_sems, recv_sems = refs[2 * n:2 * n + 2]
        sends, recvs = _gather_copies(slab_refs, land_refs, send_sems, recv_sems)
        for cp in sends:
            cp.wait_send()
        for cp in recvs:
            cp.wait_recv()

    hbm = lambda a: pltpu.HBM(a.shape, a.dtype)
    res = pl.pallas_call(
        body, name="all_gather_wait", out_shape=tuple(map(hbm, list(slabs) + list(lands))),
        in_specs=[ANY] * (2 * n) + [SEM, SEM, ANY], out_specs=tuple([ANY] * (2 * n)),
        input_output_aliases={i: i for i in range(2 * n)}, compiler_params=SPLIT_COPY,
    )(*slabs, *lands, send_sems, recv_sems, after)
    me = 2 * lax.axis_index("x") + lax.axis_index("y")
    return [_put_own(land, slab, me).reshape(N_CHIPS * slab.shape[0], slab.shape[1])
            for slab, land in zip(res[:n], res[n:])]


def _scatter_copies(part_refs, land_refs, send_sems, recv_sems):
    x, y, c, chips = _place()
    me = 4 * x + 2 * y + c
    sends, recvs = [], []
    for k, (part, land) in enumerate(zip(part_refs, land_refs)):
        for j, chip in enumerate(chips):
            for t in range(2):
                sends.append(pltpu.make_async_remote_copy(
                    src_ref=part.at[_chip_of(chip)], dst_ref=land.at[me],
                    send_sem=send_sems.at[7 * k + 2 * j + t], recv_sem=recv_sems.at[7 * k + 2 * j + c],
                    device_id=(*chip, t), device_id_type=MESH))
                recvs.append(pltpu.make_async_remote_copy(
                    src_ref=part.at[_chip_of(chip)], dst_ref=land.at[2 * _chip_of(chip) + t],
                    send_sem=send_sems.at[7 * k + 2 * j + t], recv_sem=recv_sems.at[7 * k + 2 * j + t],
                    device_id=(*chip, t), device_id_type=MESH))
        sends.append(pltpu.make_async_remote_copy(
            src_ref=part.at[2 * x + y], dst_ref=land.at[me], send_sem=send_sems.at[7 * k + 6],
            recv_sem=recv_sems.at[7 * k + 6], device_id=(x, y, 1 - c), device_id_type=MESH))
        recvs.append(pltpu.make_async_remote_copy(
            src_ref=part.at[2 * x + y], dst_ref=land.at[4 * x + 2 * y + 1 - c],
            send_sem=send_sems.at[7 * k + 6], recv_sem=recv_sems.at[7 * k + 6], device_id=(x, y, 1 - c),
            device_id_type=MESH))
    return sends, recvs


def _reduce_scatter_start(parts, name):
    n = len(parts)
    parts = [p.reshape(N_CHIPS, p.shape[0] // N_CHIPS, p.shape[1]) for p in parts]

    def body(*refs):
        part_refs, land_refs = refs[:n], refs[n:2 * n]
        send_sems, recv_sems = refs[2 * n:2 * n + 2]
        token = refs[-1]
        sends, _ = _scatter_copies(part_refs, land_refs, send_sems, recv_sems)
        for cp in sends:
            cp.start()
        token[...] = jnp.zeros_like(token)

    lands = [lax.empty((N_DEV, p.shape[1], p.shape[2]), p.dtype) for p in parts]
    hbm = lambda a: pltpu.HBM(a.shape, a.dtype)
    res = pl.pallas_call(
        body, name=name,
        out_shape=(pltpu.SemaphoreType.DMA((7 * n,)), pltpu.SemaphoreType.DMA((7 * n,)), *map(hbm, parts),
                   *map(hbm, lands), jax.ShapeDtypeStruct((8, LANES), F32)),
        in_specs=[ANY] * (2 * n), out_specs=(SEM, SEM, *([ANY] * (2 * n)), pl.BlockSpec(memory_space=pltpu.VMEM)),
        input_output_aliases={i: 2 + i for i in range(2 * n)}, compiler_params=SPLIT_COPY,
    )(*[pltpu.with_memory_space_constraint(a, pltpu.HBM) for a in parts + lands])
    return res[:-1], res[-1]


def _reduce_scatter_wait(handle, after, name):
    send_sems, recv_sems = handle[:2]
    n = (len(handle) - 2) // 2
    parts, lands = handle[2:2 + n], handle[2 + n:]

    def body(*refs):
        part_refs, land_refs = refs[:n], refs[n:2 * n]
        send_sems, recv_sems = refs[2 * n:2 * n + 2]
        sends, recvs = _scatter_copies(part_refs, land_refs, send_sems, recv_sems)
        for cp in sends:
            cp.wait_send()
        for cp in recvs:
            cp.wait_recv()

    hbm = lambda a: pltpu.HBM(a.shape, a.dtype)
    res = pl.pallas_call(
        body, name=name, out_shape=tuple(map(hbm, list(parts) + list(lands))),
        in_specs=[ANY] * (2 * n) + [SEM, SEM, ANY], out_specs=tuple([ANY] * (2 * n)),
        input_output_aliases={i: i for i in range(2 * n)}, compiler_params=SPLIT_COPY,
    )(*parts, *lands, send_sems, recv_sems, after)
    return list(zip(res[:n], res[n:]))


def _adamw_of_shares(w, own, land, m, v, name):
    rows, cols = w.shape
    tm = rows // 4 if rows % 32 == 0 else rows
    x, y, c = lax.axis_index("x"), lax.axis_index("y"), lax.axis_index("c")
    where = jnp.stack([2 * x + y, 4 * x + 2 * y + c]).astype(jnp.int32)

    def body(where_ref, w_ref, own_ref, land_ref, m_ref, v_ref, g_ref, d_ref, nm_ref, nv_ref):
        me = where_ref[1]
        g_ = jnp.zeros((tm, cols), F32)
        for dev in range(N_DEV):
            g_ = g_ + jnp.where(me == dev, own_ref[0], land_ref[dev]).astype(F32)
        m_ = ADAM_B1 * m_ref[...] + (1.0 - ADAM_B1) * g_
        v_ = ADAM_B2 * v_ref[...] + (1.0 - ADAM_B2) * (g_ * g_)
        m_hat = m_ / (1.0 - ADAM_B1 ** ADAM_STEP)
        v_hat = v_ / (1.0 - ADAM_B2 ** ADAM_STEP)
        g_ref[...] = g_
        d_ref[...] = -ADAM_LR * (m_hat / (jnp.sqrt(v_hat) + ADAM_EPS) + ADAM_WD * w_ref[...])
        nm_ref[...] = m_
        nv_ref[...] = v_

    tile = pl.BlockSpec((tm, cols), lambda i, where_ref: (i, 0))
    spec = pltpu.PrefetchScalarGridSpec(
        num_scalar_prefetch=1, grid=(rows // tm,),
        in_specs=[tile, pl.BlockSpec((1, tm, cols), lambda i, where_ref: (where_ref[0], i, 0)),
                  pl.BlockSpec((N_DEV, tm, cols), lambda i, where_ref: (0, i, 0)), tile, tile],
        out_specs=[tile] * 4)
    return pl.pallas_call(
        body, name=name, grid_spec=spec, out_shape=[jax.ShapeDtypeStruct(w.shape, F32)] * 4,
        compiler_params=_params(("arbitrary",)),
    )(where, w, own, land, m, v)


def _pack_small(values):
    flat = jnp.concatenate([values[n].reshape(-1).astype(F32) for n in SMALL])
    return jnp.pad(flat, (0, SMALL_ROWS * D_MODEL - flat.shape[0])).reshape(SMALL_ROWS, D_MODEL)


def _unpack_small(block, shapes):
    flat = block.reshape(-1)
    out, lo = {}, 0
    for n in SMALL:
        out[n] = flat[lo:lo + SMALL_SIZES[n]].reshape(shapes[n])
        lo += SMALL_SIZES[n]
    return out


def _after(token, a):
    return a + token[:1, :1].astype(a.dtype)


def kernel(x, p, mix_norm_g, w_in, sgu_w, sgu_b, sgu_norm_g, out_norm_a, out_norm_b, w_out, ffn_norm_g, w_gate, w_up, w_down, ple_norm_g, w_ple_gate, w_ple_proj, final_norm_g, loss_target, m_mix_norm_g, m_w_in, m_sgu_w, m_sgu_b, m_sgu_norm_g, m_out_norm_a, m_out_norm_b, m_w_out, m_ffn_norm_g, m_w_gate, m_w_up, m_w_down, m_ple_norm_g, m_w_ple_gate, m_w_ple_proj, m_final_norm_g, v_mix_norm_g, v_w_in, v_sgu_w, v_sgu_b, v_sgu_norm_g, v_out_norm_a, v_out_norm_b, v_w_out, v_ffn_norm_g, v_w_gate, v_w_up, v_w_down, v_ple_norm_g, v_w_ple_gate, v_w_ple_proj, v_final_norm_g):
    given = dict(locals())
    drop_lead = lambda a, lead: a.reshape(a.shape[lead:])
    xs, ps, target = drop_lead(x, 1), drop_lead(p, 2), drop_lead(loss_target, 1)
    s = xs.shape[0]
    shard = lambda name: drop_lead(given[name], 1)

    def slab_of(name):
        local = shard(name).astype(MXU_DTYPE)
        return local.T if SLAB_IS_TRANSPOSED[name] else local

    w_in_t = _all_gather_now(slab_of('w_in'))
    later = ['w_out', 'w_gate', 'w_up', 'w_down', 'w_ple_gate', 'w_ple_proj']
    gather, token = _all_gather_start([slab_of(n) for n in later], w_in_t)

    cos_t, sin_t = _rope_tables(s)
    tril = jnp.tril(jnp.ones((CHUNK, CHUNK), F32))
    w_tril = (sgu_w.reshape(HEADS_A, CHUNK, CHUNK) * tril).astype(MXU_DTYPE)
    w_tril_t = jnp.swapaxes(w_tril, 1, 2)
    bias = jnp.repeat(sgu_b.reshape(HEADS_A, CHUNK).T, HEAD_DIM, axis=1)
    g = {n: given[n].reshape(1, -1) for n in SMALL if n not in ('sgu_w', 'sgu_b')}

    uv, q, k, v, hn1 = _in_fwd(xs, _after(token, g['mix_norm_g']), w_in_t, cos_t, sin_t)
    ya_n = _sgu_fwd(uv, w_tril, bias, g['sgu_norm_g'], g['out_norm_a'])
    branches = [_attn_fwd_branch(q, k, v, dil) for dil in DILATIONS[:-1]]
    y_b, lse = _attn_fwd_branch(q, k, v, DILATIONS[-1], earlier=branches)
    stacks = dict(zip(later, _all_gather_wait(gather, lse)))
    w_gate_t, w_up_t, w_pp_t = stacks['w_gate'], stacks['w_up'], stacks['w_ple_proj']
    h1, y_n = _out_fwd(ya_n, y_b, g['out_norm_b'], stacks['w_out'], xs)
    h2, gate, up, hn2 = _ffn_fwd(h1, g['ffn_norm_g'], w_gate_t, w_up_t, stacks['w_down'])
    loss, dh2, dz, dpp, hn3, d_ple_g, d_final_g = _ple_loss(
        h2, ps, target, g['ple_norm_g'], stacks['w_ple_gate'], w_pp_t, g['final_norm_g'])

    share = {}
    share['w_ple_gate'] = _wgrad(hn3, dz, "wgrad_ple_gate")
    share['w_ple_proj'] = _wgrad(dpp, ps, "wgrad_ple_proj")
    scatter_1, token = _reduce_scatter_start([share['w_ple_gate'], share['w_ple_proj']], "reduce_scatter_start_1")
    dh1, act, dgate, dup, d_ffn_g = _ffn_bwd(dh2, h1, gate, up, _after(token, g['ffn_norm_g']), stacks['w_down'],
                                             w_gate_t, w_up_t)
    share['w_down'] = _wgrad(act, dh2, "wgrad_down")
    share['w_gate'] = _wgrad(dgate, hn2, "wgrad_gate")
    share['w_up'] = _wgrad(dup, hn2, "wgrad_up")
    scatter_2, token = _reduce_scatter_start([share['w_down'], share['w_gate'], share['w_up']],
                                             "reduce_scatter_start_2")
    dya_n, dyb, d_out_b = _out_bwd(dh1, y_b, _after(token, g['out_norm_b']), stacks['w_out'])
    share['w_out'] = _wgrad(y_n, dh1, "wgrad_out")
    scatter_3, token = _reduce_scatter_start([share['w_out']], "reduce_scatter_start_3")
    grads = None
    for dil in DILATIONS:
        grads = _attn_bwd_branch(q, k, v, dyb, y_b, lse, grads, dil)
    duv, d_sgu_w, d_sgu_b, d_sgu_g, d_out_a = _sgu_bwd(uv, dya_n, w_tril, w_tril_t, bias,
                                                       _after(token, g['sgu_norm_g']), g['out_norm_a'])
    dproj = _in_bwd_proj(duv, grads[0], grads[1], grads[2], cos_t, sin_t)
    share['w_in'] = _wgrad(dproj, hn1, "wgrad_in")
    scatter_4, token = _reduce_scatter_start([share['w_in']], "reduce_scatter_start_4")

    grads, deltas, new_m, new_v = {}, {}, {}, {}
    add_lead = lambda a: a.reshape((1,) + a.shape)

    def finish(names, handles, after, tag):
        landed = []
        for i, handle in enumerate(handles):
            landed += _reduce_scatter_wait(handle, after, "reduce_scatter_wait_%s%d" % (tag, i))
        for n, (own, land) in zip(names, landed):
            turn = (lambda a: a.T) if SLAB_IS_TRANSPOSED[n] else (lambda a: a)
            res = _adamw_of_shares(turn(shard(n)), own, land, turn(shard("m_" + n)), turn(shard("v_" + n)),
                                   "adamw_" + n)
            grads[n], deltas[n], new_m[n], new_v[n] = (add_lead(turn(a)) for a in res)

    finish(['w_ple_gate', 'w_ple_proj', 'w_down', 'w_gate', 'w_up', 'w_out'], [scatter_1, scatter_2, scatter_3], token,
           "early")
    grad_x, d_mix_g = _in_bwd_x(dproj, w_in_t, xs, g['mix_norm_g'], dh1,
                                after=[new_v[n] for n in ('w_down', 'w_gate', 'w_up', 'w_out')])

    gs = {'mix_norm_g': d_mix_g, 'sgu_w': d_sgu_w, 'sgu_b': d_sgu_b[:, :HEADS_A].T, 'sgu_norm_g': d_sgu_g,
          'out_norm_a': d_out_a, 'out_norm_b': d_out_b, 'ffn_norm_g': d_ffn_g, 'ple_norm_g': d_ple_g,
          'final_norm_g': d_final_g}
    gs_block = _pack_small(gs).at[SMALL_ROWS - 1, 0].set(loss[0, 0])
    to_all = jnp.broadcast_to(gs_block[None], (N_CHIPS,) + gs_block.shape).reshape(-1, D_MODEL)
    scatter_small, token = _reduce_scatter_start([to_all], "small_all_reduce_start")
    finish(['w_in'], [scatter_4], token, "last")
    (own, land), = _reduce_scatter_wait(scatter_small, new_v['w_in'], "small_all_reduce_wait")
    small = {n: given[n] for n in SMALL}
    small_res = _adamw_of_shares(_pack_small(small), own, land, _pack_small({n: given["m_" + n] for n in SMALL}),
                                 _pack_small({n: given["v_" + n] for n in SMALL}), "adamw_small")
    loss_out = small_res[0][SMALL_ROWS - 1, 0]
    small_shapes = {n: given[n].shape for n in SMALL}
    for res, blk in zip((grads, deltas, new_m, new_v), small_res):
        res.update(_unpack_small(blk, small_shapes))

    outs = [loss_out, add_lead(grad_x)]
    for res in (grads, deltas, new_m, new_v):
        outs += [res[n] for n in WEIGHT_NAMES]
    return tuple(outs)
```

```python
import functools
import itertools
import math

import jax
import jax.numpy as jnp
import numpy as np
from jax import lax
from jax.experimental import pallas as pl
from jax.experimental.pallas import tpu as pltpu

F32 = jnp.float32
MXU_DTYPE = jnp.bfloat16

D_MODEL = 1024
HEAD_DIM = 64
HEADS_A = 4
HEADS_B = 12
WIDTH_A = HEADS_A * HEAD_DIM
WIDTH_B = HEADS_B * HEAD_DIM
CHUNK = 128
BLOCK = 128
DILATIONS = (1, 4, 16)
ROPE_THETA = 10000.0
D_FF = 2816
FF_HALF = D_FF // 2
FF_STRIPS = ((0, 1024), (1024, 2048), (2048, D_FF))
PLE_DIM = 256
IN_COLS = 2 * WIDTH_A + 3 * WIDTH_B
EPS = 1e-6
LANES = 128
N_CHIPS = 4
N_DEV = 8

ADAM_LR = 0.001
ADAM_B1 = 0.9
ADAM_B2 = 0.999
ADAM_EPS = 1e-08
ADAM_WD = 0.01
ADAM_STEP = 10

VMEM_LIMIT = 56 * 1024 * 1024

WEIGHT_NAMES = ['mix_norm_g', 'w_in', 'sgu_w', 'sgu_b', 'sgu_norm_g', 'out_norm_a', 'out_norm_b', 'w_out',
                'ffn_norm_g', 'w_gate', 'w_up', 'w_down', 'ple_norm_g', 'w_ple_gate', 'w_ple_proj', 'final_norm_g']
SMALL = ['mix_norm_g', 'sgu_w', 'sgu_b', 'sgu_norm_g', 'out_norm_a', 'out_norm_b', 'ffn_norm_g', 'ple_norm_g',
         'final_norm_g']
SMALL_SIZES = {'mix_norm_g': 1024, 'sgu_w': 65536, 'sgu_b': 512, 'sgu_norm_g': 256, 'out_norm_a': 256,
               'out_norm_b': 768, 'ffn_norm_g': 1024, 'ple_norm_g': 1024, 'final_norm_g': 1024}
SMALL_ROWS = 72


def _params(semantics=None):
    return pltpu.CompilerParams(dimension_semantics=semantics, vmem_limit_bytes=VMEM_LIMIT)


def _full(shape):
    nd = len(shape)
    return pl.BlockSpec(shape, lambda i: (0,) * nd, pipeline_mode=pl.Buffered(1))


def _rows(tm, width):
    return pl.BlockSpec((tm, width), lambda i: (i, 0))


def _rms_stats(x):
    r = lax.rsqrt(jnp.mean(x * x, axis=-1, keepdims=True) + EPS)
    return x * r, r


def _rms_bwd(dn, n, r):
    return r * (dn - n * jnp.mean(dn * n, axis=-1, keepdims=True))


def _dot(a, b):
    return jnp.dot(a, b, preferred_element_type=F32)


def _dot_nt(a, b):
    return lax.dot_general(a, b, (((1,), (1,)), ((), ())), preferred_element_type=F32)


def _dot_tn(a, b):
    return lax.dot_general(a, b, (((0,), (0,)), ((), ())), preferred_element_type=F32)


def _gelu_parts(x):
    c = math.sqrt(2.0 / math.pi)
    t = jnp.tanh(c * (x + 0.044715 * x * x * x))
    return 0.5 * x * (1.0 + t), t


def _gelu_grad(x, t):
    c = math.sqrt(2.0 / math.pi)
    return 0.5 * (1.0 + t) + 0.5 * x * (1.0 - t * t) * c * (1.0 + 3.0 * 0.044715 * x * x)


def _half_masks(dtype):
    lane = lax.broadcasted_iota(jnp.int32, (BLOCK, LANES), 1)
    lo = (lane < HEAD_DIM).astype(F32)
    return lo.astype(dtype), (1.0 - lo).astype(dtype)


def _rope_partner(t):
    lane = lax.broadcasted_iota(jnp.int32, t.shape, 1)
    first_half = (lane % HEAD_DIM) < (HEAD_DIM // 2)
    return jnp.where(first_half, pltpu.roll(t, LANES - HEAD_DIM // 2, 1), pltpu.roll(t, HEAD_DIM // 2, 1))


PAIRS_ABREAST = 2
RESIDUES_PER_STEP = 4
L_BLOCK = 256
L_GROUP = 16


def _store_l256(scr, out_ref, cols, value):
    tm = value.shape[0]
    scr[...] = value
    for blk in range(tm // L_BLOCK):
        for r in range(L_GROUP):
            lo = blk * L_BLOCK + r * L_GROUP
            piece = scr[pl.ds(blk * L_BLOCK + r, L_GROUP, stride=L_GROUP), :]
            out_ref[lo:lo + L_GROUP, cols] = piece.astype(out_ref.dtype)


def _load_l256(col_refs, tm):
    cols = []
    for ref in col_refs:
        pieces = [ref[pl.ds(blk * L_BLOCK + i, L_GROUP, stride=L_GROUP), :]
                  for blk in range(tm // L_BLOCK) for i in range(L_GROUP)]
        cols.append(jnp.concatenate(pieces, axis=0))
    return jnp.concatenate(cols, axis=1)


def _col_specs(tm, width):
    return [pl.BlockSpec((tm, LANES), lambda i, j=j: (i, j)) for j in range(width // LANES)]


def _in_fwd(x, g_mix, w_in_t, cos_t, sin_t):
    s = x.shape[0]
    tm = 512

    def body(x_ref, g_ref, wt_ref, cos_ref, sin_ref, uv_ref, q_ref, k_ref, v_ref, hn_ref, *scrs):
        n, _ = _rms_stats(x_ref[...])
        hn = (n * g_ref[...]).astype(MXU_DTYPE)
        hn_ref[...] = hn
        cos = cos_ref[...]
        sin = sin_ref[...]
        strip = 2 * LANES
        for j in range(IN_COLS // strip):
            proj = _dot_nt(hn, wt_ref[j * strip:(j + 1) * strip, :])
            lo = j * strip - 2 * WIDTH_A
            if lo < 0:
                uv_ref[:, j * strip:(j + 1) * strip] = proj
                continue
            which, lo = divmod(lo, WIDTH_B)
            for i in range(strip // LANES):
                t = proj[:, i * LANES:(i + 1) * LANES]
                cols = slice(lo + i * LANES, lo + (i + 1) * LANES)
                scr = scrs[i]
                if which == 0:
                    _store_l256(scr, q_ref, cols, (t * cos + _rope_partner(t) * sin) * (HEAD_DIM ** -0.5))
                elif which == 1:
                    _store_l256(scr, k_ref, cols, t * cos + _rope_partner(t) * sin)
                else:
                    _store_l256(scr, v_ref, cols, t)

    return pl.pallas_call(
        body, name="in_fwd", grid=(s // tm,), scratch_shapes=[pltpu.VMEM((tm, LANES), F32)] * 2,
        in_specs=[_rows(tm, D_MODEL), _full((1, D_MODEL)), _full((IN_COLS, D_MODEL)), _rows(tm, LANES),
                  _rows(tm, LANES)],
        out_specs=[_rows(tm, 2 * WIDTH_A), _rows(tm, WIDTH_B), _rows(tm, WIDTH_B), _rows(tm, WIDTH_B),
                   _rows(tm, D_MODEL)],
        out_shape=[jax.ShapeDtypeStruct((s, 2 * WIDTH_A), F32), jax.ShapeDtypeStruct((s, WIDTH_B), MXU_DTYPE),
                   jax.ShapeDtypeStruct((s, WIDTH_B), MXU_DTYPE), jax.ShapeDtypeStruct((s, WIDTH_B), MXU_DTYPE),
                   jax.ShapeDtypeStruct((s, D_MODEL), MXU_DTYPE)],
        compiler_params=_params(("arbitrary",)),
    )(x, g_mix, w_in_t, cos_t, sin_t)


class _Branch:
    def __init__(self, dil, s, qn=BLOCK):
        self.dil = dil
        i = np.arange(L_GROUP)
        if dil == 16:
            nblk, self.res = qn // 16, RESIDUES_PER_STEP
            self.grid = (16 // self.res, s // (L_BLOCK * nblk))
            self.shape = (nblk, 1, self.res, L_GROUP)
            self.index = lambda r, n: (n, r // (4 // self.res), r % (4 // self.res), 0, 0)
            pos = (np.arange(nblk)[:, None] * 16 + i[None, :]).reshape(-1)
        elif dil == 4:
            nblk, self.res = qn // 64, RESIDUES_PER_STEP
            self.grid = (4 // self.res, s // (L_BLOCK * nblk))
            self.shape = (nblk, 4, self.res, L_GROUP)
            self.index = lambda r, n: (n, 0, r, 0, 0)
            pos = (np.arange(nblk)[:, None, None] * 64 + np.arange(4)[None, :, None]
                   + 4 * i[None, None, :]).reshape(-1)
        else:
            self.res = 1
            self.grid = (1, s // L_BLOCK)
            self.shape = (1, 4, 4, L_GROUP)
            self.index = lambda r, n: (n, 0, 0, 0, 0)
            pos = (np.arange(16)[:, None] + 16 * i[None, :]).reshape(-1)
        self.qn = pos.shape[0]
        self.nb = self.grid[1]
        dist = pos[:, None] - np.concatenate([pos - self.qn, pos])[None, :]
        band = (dist >= 0) & (dist <= BLOCK)
        start = band & (np.arange(2 * self.qn)[None, :] >= self.qn)
        self.bias = np.where(np.stack([band, start]), 0.0, -np.inf).astype(np.float32)

    def view(self, a):
        return a.reshape(a.shape[0] // L_BLOCK, 4, 4, L_GROUP, a.shape[1])

    def spec(self, w, step=lambda n: n):
        return pl.BlockSpec(self.shape + (w,), lambda r, n: self.index(r, step(n)))

    def bias_spec(self, step=lambda n: n):
        return pl.BlockSpec((1, self.qn, 2 * self.qn), lambda r, n: (jnp.where(step(n) == 0, 1, 0), 0, 0))

    def load(self, ref, cols=slice(None), j=0):
        x = ref[:, :, :, :, cols] if self.dil == 1 else ref[:, :, j, :, cols]
        return x.reshape(self.qn, x.shape[-1])

    def store(self, ref, cols, value, j=0):
        if self.dil == 1:
            ref[:, :, :, :, cols] = value.reshape(self.shape + (value.shape[-1],))
        else:
            ref[:, :, j, :, cols] = value.reshape(self.shape[:2] + (L_GROUP, value.shape[-1]))


def _attn_fwd_branch(q, k, v, dil, earlier=()):
    s = q.shape[0]
    br = _Branch(dil, s)
    qn = br.qn
    nearly = len(earlier)

    def body(bias_ref, q_ref, kc_ref, kp_ref, vc_ref, vp_ref, *refs):
        early_refs, (o_ref, lse_ref) = refs[:2 * nearly], refs[2 * nearly:]
        bias2 = jnp.concatenate([bias_ref[0], bias_ref[0]], axis=0)
        lo = lax.broadcasted_iota(jnp.int32, (qn, LANES), 1) < HEAD_DIM
        mask_lo = lo.astype(F32).astype(MXU_DTYPE)
        for j, hp in itertools.product(range(br.res), range(HEADS_B // 2)):
            cols = slice(hp * LANES, (hp + 1) * LANES)
            qp = br.load(q_ref, cols, j)
            kcat = jnp.concatenate([br.load(kp_ref, cols, j), br.load(kc_ref, cols, j)], axis=0)
            vcat = jnp.concatenate([br.load(vp_ref, cols, j), br.load(vc_ref, cols, j)], axis=0)
            sc = _dot_nt(jnp.concatenate([qp * mask_lo, qp * (1 - mask_lo)], axis=0), kcat) + bias2
            m = jnp.max(sc, axis=1, keepdims=True)
            p = jnp.exp(sc - m)
            l = jnp.sum(p, axis=1, keepdims=True)
            out = _dot(p.astype(MXU_DTYPE), vcat) / l
            lse = m + jnp.log(l)
            outs = [br.load(r, cols, j) for r in early_refs[:nearly]] + [jnp.where(lo, out[:qn], out[qn:])]
            lses = [br.load(r, cols, j) for r in early_refs[nearly:]] + [jnp.where(lo, lse[:qn], lse[qn:])]
            if nearly:
                top = functools.reduce(jnp.maximum, lses)
                ws = [jnp.exp(x - top) for x in lses]
                den = functools.reduce(jnp.add, ws)
                outs = [functools.reduce(jnp.add, [w * o for w, o in zip(ws, outs)]) / den]
                lses = [top + jnp.log(den)]
            br.store(o_ref, cols, outs[0], j)
            br.store(lse_ref, cols, lses[0], j)

    before = lambda n: jnp.maximum(n - 1, 0)
    res = pl.pallas_call(
        body, name="attn_fwd_d%d" % dil, grid=br.grid,
        in_specs=[br.bias_spec(), br.spec(WIDTH_B), br.spec(WIDTH_B), br.spec(WIDTH_B, before), br.spec(WIDTH_B),
                  br.spec(WIDTH_B, before)] + [br.spec(WIDTH_B)] * (2 * nearly),
        out_specs=[br.spec(WIDTH_B), br.spec(WIDTH_B)],
        out_shape=[jax.ShapeDtypeStruct((s // L_BLOCK, 4, 4, L_GROUP, WIDTH_B), F32)] * 2,
        compiler_params=_params(("arbitrary", "arbitrary")),
    )(jnp.asarray(br.bias), br.view(q), br.view(k), br.view(k), br.view(v), br.view(v),
      *[br.view(o) for o, _ in earlier], *[br.view(x) for _, x in earlier])
    return tuple(a.reshape(s, WIDTH_B) for a in res)


def _sgu_forward_tile(uv, w_ref, bias, g_sgu):
    tm = uv.shape[0]
    u = uv[:, :WIDTH_A]
    v = uv[:, WIDTH_A:]
    ug, tu = _gelu_parts(u)
    vg, tv = _gelu_parts(v)
    mu = jnp.mean(vg, axis=-1, keepdims=True)
    vc = vg - mu
    rs = lax.rsqrt(jnp.mean(vc * vc, axis=-1, keepdims=True) + EPS)
    vhat = vc * rs
    vn = (vhat * g_sgu).astype(MXU_DTYPE)
    masks = _half_masks(MXU_DTYPE)
    chunks = []
    for c in range(tm // CHUNK):
        rows = slice(c * CHUNK, (c + 1) * CHUNK)
        groups = []
        for gp in range(2):
            vn_g = vn[rows, gp * LANES:(gp + 1) * LANES]
            groups.append(_dot(w_ref[2 * gp], vn_g * masks[0]) + _dot(w_ref[2 * gp + 1], vn_g * masks[1]))
        chunks.append(jnp.concatenate(groups, axis=1) + bias)
    mixed = jnp.concatenate(chunks, axis=0)
    return dict(u=u, v=v, ug=ug, tu=tu, tv=tv, rs=rs, vhat=vhat, vn=vn, mixed=mixed, ya=ug * mixed)


def _sgu_fwd(uv, w_tril, bias, g_sgu, g_a):
    s = uv.shape[0]
    tm = 512

    def body(uv_ref, w_ref, b_ref, gs_ref, ga_ref, o_ref):
        t = _sgu_forward_tile(uv_ref[...], w_ref, b_ref[...], gs_ref[...])
        n, _ = _rms_stats(t['ya'])
        o_ref[...] = (n * ga_ref[...]).astype(MXU_DTYPE)

    return pl.pallas_call(
        body, name="sgu_fwd", grid=(s // tm,),
        in_specs=[_rows(tm, 2 * WIDTH_A), _full((HEADS_A, CHUNK, CHUNK)), _full((CHUNK, WIDTH_A)),
                  _full((1, WIDTH_A)), _full((1, WIDTH_A))],
        out_specs=_rows(tm, WIDTH_A), out_shape=jax.ShapeDtypeStruct((s, WIDTH_A), MXU_DTYPE),
        compiler_params=_params(("arbitrary",)),
    )(uv, w_tril, bias, g_sgu, g_a)


def _out_fwd(ya_n, y_b, g_b, w_out, x):
    s = x.shape[0]
    tm = 512
    nc = WIDTH_B // LANES

    def body(ya_ref, *refs):
        yb_refs = refs[:nc]
        g_ref, w_ref, x_ref, h_ref, yn_ref = refs[nc:]
        n, _ = _rms_stats(_load_l256(yb_refs, tm))
        yn = jnp.concatenate([ya_ref[...], (n * g_ref[...]).astype(MXU_DTYPE)], axis=1)
        yn_ref[...] = yn
        h_ref[...] = x_ref[...] + _dot(yn, w_ref[...])

    return pl.pallas_call(
        body, name="out_fwd", grid=(s // tm,),
        in_specs=[_rows(tm, WIDTH_A)] + _col_specs(tm, WIDTH_B) + [_full((1, WIDTH_B)), _full((D_MODEL, D_MODEL)),
                                                                 _rows(tm, D_MODEL)],
        out_specs=[_rows(tm, D_MODEL), _rows(tm, D_MODEL)],
        out_shape=[jax.ShapeDtypeStruct((s, D_MODEL), F32), jax.ShapeDtypeStruct((s, D_MODEL), MXU_DTYPE)],
        compiler_params=_params(("arbitrary",)),
    )(ya_n, *([y_b] * nc), g_b, w_out, x)


def _ffn_fwd(h1, g_ffn, w_gate_t, w_up_t, w_down):
    s = h1.shape[0]
    tm = 512

    def body(h_ref, g_ref, wgt_ref, wut_ref, wd_ref, o_ref, gate_ref, up_ref, hn_ref):
        h = h_ref[...]
        n, _ = _rms_stats(h)
        hn = (n * g_ref[...]).astype(MXU_DTYPE)
        hn_ref[...] = hn
        strips = [dict(cols=slice(lo, hi)) for lo, hi in FF_STRIPS]

        def project(t):
            t['gate'] = _dot_nt(hn, wgt_ref[t['cols'], :])
            t['up'] = _dot_nt(hn, wut_ref[t['cols'], :])

        def activate(t):
            gate, up = t['gate'], t['up']
            gate_ref[:, t['cols']] = gate.astype(MXU_DTYPE)
            up_ref[:, t['cols']] = up.astype(MXU_DTYPE)
            t['act'] = (gate * jax.nn.sigmoid(gate) * up).astype(MXU_DTYPE)

        def down(t):
            return _dot(t['act'], wd_ref[t['cols'], :])

        out = h
        project(strips[0])
        for i, t in enumerate(strips):
            if i + 1 < len(strips):
                project(strips[i + 1])
            activate(t)
            out = out + down(t)
        o_ref[...] = out

    return pl.pallas_call(
        body, name="ffn_fwd", grid=(s // tm,),
        in_specs=[_rows(tm, D_MODEL), _full((1, D_MODEL)), _full((D_FF, D_MODEL)), _full((D_FF, D_MODEL)),
                  _full((D_FF, D_MODEL))],
        out_specs=[_rows(tm, D_MODEL), _rows(tm, D_FF), _rows(tm, D_FF), _rows(tm, D_MODEL)],
        out_shape=[jax.ShapeDtypeStruct((s, D_MODEL), F32), jax.ShapeDtypeStruct((s, D_FF), MXU_DTYPE),
                   jax.ShapeDtypeStruct((s, D_FF), MXU_DTYPE), jax.ShapeDtypeStruct((s, D_MODEL), MXU_DTYPE)],
        compiler_params=_params(("arbitrary",)),
    )(h1, g_ffn, w_gate_t, w_up_t, w_down)


def _ple_loss(h2, p, target, g_ple, w_pg, w_pp_t, g_final):
    s = h2.shape[0]
    tm = 256

    def body(h_ref, p_ref, t_ref, gp_ref, wg_ref, wpt_ref, gf_ref,
             loss_ref, dh_ref, dz_ref, dpp_ref, hn_ref, dgp_ref, dgf_ref):
        @pl.when(pl.program_id(0) == 0)
        def _():
            loss_ref[...] = jnp.zeros_like(loss_ref)
            dgp_ref[...] = jnp.zeros_like(dgp_ref)
            dgf_ref[...] = jnp.zeros_like(dgf_ref)

        h2t = h_ref[...]
        n2, r2 = _rms_stats(h2t)
        hn = (n2 * gp_ref[...]).astype(MXU_DTYPE)
        hn_ref[...] = hn
        gate = jax.nn.sigmoid(_dot(hn, wg_ref[...]))
        pp = _dot_nt(p_ref[...].astype(MXU_DTYPE), wpt_ref[...])
        h3 = h2t + gate * pp
        n3, r3 = _rms_stats(h3)
        diff = n3 * gf_ref[...] - t_ref[...]
        loss_ref[...] += jnp.full(loss_ref.shape, 0.5 * jnp.sum(diff * diff) / D_MODEL, F32)
        dy = diff * (1.0 / D_MODEL)
        dgf_ref[...] += jnp.sum(dy * n3, axis=0, keepdims=True)
        dh3 = _rms_bwd(dy * gf_ref[...], n3, r3)
        dpp_ref[...] = (dh3 * gate).astype(MXU_DTYPE)
        dz = (dh3 * pp * gate * (1.0 - gate)).astype(MXU_DTYPE)
        dz_ref[...] = dz
        dhn = _dot_nt(dz, wg_ref[...])
        dgp_ref[...] += jnp.sum(dhn * n2, axis=0, keepdims=True)
        dh_ref[...] = dh3 + _rms_bwd(dhn * gp_ref[...], n2, r2)

    return pl.pallas_call(
        body, name="ple_loss", grid=(s // tm,),
        in_specs=[_rows(tm, D_MODEL), _rows(tm, PLE_DIM), _rows(tm, D_MODEL), _full((1, D_MODEL)),
                  _full((D_MODEL, D_MODEL)), _full((D_MODEL, PLE_DIM)), _full((1, D_MODEL))],
        out_specs=[_full((1, LANES)), _rows(tm, D_MODEL), _rows(tm, D_MODEL), _rows(tm, D_MODEL),
                   _rows(tm, D_MODEL), _full((1, D_MODEL)), _full((1, D_MODEL))],
        out_shape=[jax.ShapeDtypeStruct((1, LANES), F32), jax.ShapeDtypeStruct((s, D_MODEL), F32),
                   jax.ShapeDtypeStruct((s, D_MODEL), MXU_DTYPE), jax.ShapeDtypeStruct((s, D_MODEL), MXU_DTYPE),
                   jax.ShapeDtypeStruct((s, D_MODEL), MXU_DTYPE), jax.ShapeDtypeStruct((1, D_MODEL), F32),
                   jax.ShapeDtypeStruct((1, D_MODEL), F32)],
        compiler_params=_params(("arbitrary",)),
    )(h2, p, target, g_ple, w_pg, w_pp_t, g_final)


def _ffn_bwd(dh2, h1, gate, up, g_ffn, w_down, w_gate_t, w_up_t):
    s = h1.shape[0]
    tm = 256

    def body(dh_ref, h_ref, gate_ref, up_ref, g_ref, wd_ref, wgt_ref, wut_ref,
             o_ref, act_ref, dg_ref, du_ref, dgn_ref):
        @pl.when(pl.program_id(0) == 0)
        def _():
            dgn_ref[...] = jnp.zeros_like(dgn_ref)

        dh = dh_ref[...]
        dhb = dh.astype(MXU_DTYPE)
        strips = [dict(cols=slice(lo, hi)) for lo, hi in FF_STRIPS]

        def back_down(t):
            t['dact'] = _dot_nt(dhb, wd_ref[t['cols'], :])

        def back_act(t):
            cols, dact = t['cols'], t['dact']
            g = gate_ref[:, cols].astype(F32)
            u = up_ref[:, cols].astype(F32)
            sg = jax.nn.sigmoid(g)
            silu = g * sg
            act_ref[:, cols] = (silu * u).astype(MXU_DTYPE)
            t['du'] = (dact * silu).astype(MXU_DTYPE)
            t['dg'] = (dact * u * sg * (1.0 + g * (1.0 - sg))).astype(MXU_DTYPE)
            du_ref[:, cols] = t['du']
            dg_ref[:, cols] = t['dg']

        def back_in(t):
            return _dot(t['dg'], wgt_ref[t['cols'], :]) + _dot(t['du'], wut_ref[t['cols'], :])

        dhn = jnp.zeros((tm, D_MODEL), F32)
        back_down(strips[0])
        for i, t in enumerate(strips):
            if i + 1 < len(strips):
                back_down(strips[i + 1])
            back_act(t)
            dhn = dhn + back_in(t)
        n, r = _rms_stats(h_ref[...])
        dgn_ref[...] += jnp.sum(dhn * n, axis=0, keepdims=True)
        o_ref[...] = dh + _rms_bwd(dhn * g_ref[...], n, r)

    return pl.pallas_call(
        body, name="ffn_bwd", grid=(s // tm,),
        in_specs=[_rows(tm, D_MODEL), _rows(tm, D_MODEL), _rows(tm, D_FF), _rows(tm, D_FF), _full((1, D_MODEL)),
                  _full((D_FF, D_MODEL)), _full((D_FF, D_MODEL)), _full((D_FF, D_MODEL))],
        out_specs=[_rows(tm, D_MODEL), _rows(tm, D_FF), _rows(tm, D_FF), _rows(tm, D_FF), _full((1, D_MODEL))],
        out_shape=[jax.ShapeDtypeStruct((s, D_MODEL), F32), jax.ShapeDtypeStruct((s, D_FF), MXU_DTYPE),
                   jax.ShapeDtypeStruct((s, D_FF), MXU_DTYPE), jax.ShapeDtypeStruct((s, D_FF), MXU_DTYPE),
                   jax.ShapeDtypeStruct((1, D_MODEL), F32)],
        compiler_params=_params(("arbitrary",)),
    )(dh2, h1, gate, up, g_ffn, w_down, w_gate_t, w_up_t)


def _out_bwd(dh1, y_b, g_b, w_out):
    s = dh1.shape[0]
    tm = 512
    nc = WIDTH_B // LANES

    def body(dh_ref, *refs):
        yb_refs = refs[:nc]
        g_ref, w_ref, dya_ref, dyb_ref, dg_ref, scr = refs[nc:]

        @pl.when(pl.program_id(0) == 0)
        def _():
            dg_ref[...] = jnp.zeros_like(dg_ref)

        dy = _dot_nt(dh_ref[...].astype(MXU_DTYPE), w_ref[...])
        dya_ref[...] = dy[:, :WIDTH_A]
        dyb = dy[:, WIDTH_A:]
        n, r = _rms_stats(_load_l256(yb_refs, tm))
        dg_ref[...] += jnp.sum(dyb * n, axis=0, keepdims=True)
        dyb_in = _rms_bwd(dyb * g_ref[...], n, r)
        for j in range(nc):
            cols = slice(j * LANES, (j + 1) * LANES)
            _store_l256(scr, dyb_ref, cols, dyb_in[:, cols])

    return pl.pallas_call(
        body, name="out_bwd", grid=(s // tm,), scratch_shapes=[pltpu.VMEM((tm, LANES), F32)],
        in_specs=[_rows(tm, D_MODEL)] + _col_specs(tm, WIDTH_B) + [_full((1, WIDTH_B)), _full((D_MODEL, D_MODEL))],
        out_specs=[_rows(tm, WIDTH_A), _rows(tm, WIDTH_B), _full((1, WIDTH_B))],
        out_shape=[jax.ShapeDtypeStruct((s, WIDTH_A), F32), jax.ShapeDtypeStruct((s, WIDTH_B), F32),
                   jax.ShapeDtypeStruct((1, WIDTH_B), F32)],
        compiler_params=_params(("arbitrary",)),
    )(dh1, *([y_b] * nc), g_b, w_out)


def _attn_bwd_branch(q, k, v, do, o, lse, grads, dil):
    s = q.shape[0]
    br = _Branch(dil, s)
    qn, nb = br.qn, br.nb
    first = grads is None

    def body(*refs):
        bias_ref, q_ref, kc_ref, kp_ref, vc_ref, vp_ref, do_ref, o_ref, lse_ref = refs[:9]
        if first:
            rest = refs[9:]
        else:
            dq_in, dk_in, dv_in = refs[9:12]
            rest = refs[12:]
        dq_ref, dk_ref, dv_ref, dk_carry, dv_carry = rest
        n = pl.program_id(1)

        @pl.when(n == 0)
        def _():
            dk_carry[...] = jnp.zeros_like(dk_carry)
            dv_carry[...] = jnp.zeros_like(dv_carry)

        @pl.when(n < nb)
        def _():
            bias2 = jnp.concatenate([bias_ref[0], bias_ref[0]], axis=0)
            lane = lax.broadcasted_iota(jnp.int32, (qn, LANES), 1)
            lo = lane < HEAD_DIM
            mask_f = lo.astype(F32)
            mask_lo = mask_f.astype(MXU_DTYPE)
            def prepare(j, hp):
                cols = slice(hp * LANES, (hp + 1) * LANES)
                qp = br.load(q_ref, cols, j)
                dop = br.load(do_ref, cols, j)
                prod = dop * br.load(o_ref, cols, j)
                prod_lo = prod * mask_f
                lse = br.load(lse_ref, cols, j)
                return dict(
                    j=j, cols=cols,
                    kcat=jnp.concatenate([br.load(kp_ref, cols, j), br.load(kc_ref, cols, j)], axis=0),
                    vcat=jnp.concatenate([br.load(vp_ref, cols, j), br.load(vc_ref, cols, j)], axis=0),
                    qs=jnp.concatenate([qp * mask_lo, qp * (1 - mask_lo)], axis=0),
                    dos=jnp.concatenate([dop * mask_f, dop * (1.0 - mask_f)], axis=0).astype(MXU_DTYPE),
                    delta=jnp.concatenate([jnp.sum(prod_lo, axis=1, keepdims=True),
                                           jnp.sum(prod - prod_lo, axis=1, keepdims=True)], axis=0),
                    lse2=jnp.concatenate([lse[:, :1], lse[:, HEAD_DIM:HEAD_DIM + 1]], axis=0))

            def scores(t):
                t['sc'] = _dot_nt(t['qs'], t['kcat'])
                t['dp'] = _dot_nt(t['dos'], t['vcat'])

            def softmax(t):
                p = jnp.exp(t['sc'] + bias2 - t['lse2'])
                t['ds'] = (p * (t['dp'] - t['delta'])).astype(MXU_DTYPE)
                t['p'] = p.astype(MXU_DTYPE)

            def gradients(t):
                t['dvc'] = _dot_tn(t['p'], t['dos'])
                t['dkc'] = _dot_tn(t['ds'], t['qs'])
                t['dq2'] = _dot(t['ds'], t['kcat'])

            def store(t):
                j, cols, dkc, dvc = t['j'], t['cols'], t['dkc'], t['dvc']
                dq = jnp.where(lo, t['dq2'][:qn], t['dq2'][qn:])
                dk_prev = dk_carry[j, :, cols] + dkc[:qn]
                dv_prev = dv_carry[j, :, cols] + dvc[:qn]
                if not first:
                    dq = dq + br.load(dq_in, cols, j)
                    dk_prev = dk_prev + br.load(dk_in, cols, j)
                    dv_prev = dv_prev + br.load(dv_in, cols, j)
                br.store(dq_ref, cols, dq, j)
                br.store(dk_ref, cols, dk_prev, j)
                br.store(dv_ref, cols, dv_prev, j)
                dk_carry[j, :, cols] = dkc[qn:]
                dv_carry[j, :, cols] = dvc[qn:]

            for j, first_pair in itertools.product(range(br.res), range(0, HEADS_B // 2, PAIRS_ABREAST)):
                group = [prepare(j, hp) for hp in range(first_pair, first_pair + PAIRS_ABREAST)]
                for stage in (scores, softmax, gradients, store):
                    for t in group:
                        stage(t)

        @pl.when(n == nb)
        def _():
            for j in range(br.res):
                dk_last = dk_carry[j]
                dv_last = dv_carry[j]
                if not first:
                    dk_last = dk_last + br.load(dk_in, slice(None), j)
                    dv_last = dv_last + br.load(dv_in, slice(None), j)
                br.store(dk_ref, slice(None), dk_last, j)
                br.store(dv_ref, slice(None), dv_last, j)

    cur = lambda n: jnp.minimum(n, nb - 1)
    before = lambda n: jnp.maximum(cur(n) - 1, 0)
    late = lambda n: jnp.maximum(n - 1, 0)
    in_specs = [br.bias_spec(cur), br.spec(WIDTH_B, cur), br.spec(WIDTH_B, cur), br.spec(WIDTH_B, before),
                br.spec(WIDTH_B, cur), br.spec(WIDTH_B, before), br.spec(WIDTH_B, cur), br.spec(WIDTH_B, cur),
                br.spec(WIDTH_B, cur)]
    args = [jnp.asarray(br.bias)] + [br.view(a) for a in (q, k, k, v, v, do, o, lse)]
    if not first:
        in_specs += [br.spec(WIDTH_B, cur), br.spec(WIDTH_B, late), br.spec(WIDTH_B, late)]
        args += [br.view(g) for g in grads]
    res = pl.pallas_call(
        body, name="attn_bwd_d%d" % dil, grid=(br.grid[0], nb + 1), in_specs=in_specs,
        out_specs=[br.spec(WIDTH_B, cur), br.spec(WIDTH_B, late), br.spec(WIDTH_B, late)],
        out_shape=[jax.ShapeDtypeStruct((s // L_BLOCK, 4, 4, L_GROUP, WIDTH_B), F32)] * 3,
        scratch_shapes=[pltpu.VMEM((br.res, qn, WIDTH_B), F32), pltpu.VMEM((br.res, qn, WIDTH_B), F32)],
        compiler_params=_params(("arbitrary", "arbitrary")),
    )(*args)
    return tuple(a.reshape(s, WIDTH_B) for a in res)


def _sgu_bwd(uv, dya_n, w_tril, w_tril_t, bias, g_sgu, g_a):
    s = uv.shape[0]
    tm = 512

    def body(uv_ref, dy_ref, w_ref, wt_ref, b_ref, gs_ref, ga_ref, duv_ref, dw_ref, db_ref, dgs_ref, dga_ref,
             db_acc):
        i = pl.program_id(0)

        @pl.when(i == 0)
        def _():
            dw_ref[...] = jnp.zeros_like(dw_ref)
            dgs_ref[...] = jnp.zeros_like(dgs_ref)
            dga_ref[...] = jnp.zeros_like(dga_ref)
            db_acc[...] = jnp.zeros_like(db_acc)

        t = _sgu_forward_tile(uv_ref[...], w_ref, b_ref[...], gs_ref[...])
        na, ra = _rms_stats(t['ya'])
        dyn = dy_ref[...]
        dga_ref[...] += jnp.sum(dyn * na, axis=0, keepdims=True)
        dya = _rms_bwd(dyn * ga_ref[...], na, ra)
        dug = dya * t['mixed']
        dmixed = dya * t['ug']
        dmb = dmixed.astype(MXU_DTYPE)
        masks = _half_masks(MXU_DTYPE)
        chunks = []
        db = jnp.zeros((CHUNK, WIDTH_A), F32)
        for c in range(tm // CHUNK):
            rows = slice(c * CHUNK, (c + 1) * CHUNK)
            db = db + dmixed[rows]
            groups = []
            for gp in range(2):
                cols = slice(gp * LANES, (gp + 1) * LANES)
                dm_g = dmb[rows, cols]
                vn_g = t['vn'][rows, cols]
                dvn_g = jnp.zeros((CHUNK, LANES), F32)
                for j in range(2):
                    dm_h = dm_g * masks[j]
                    dvn_g = dvn_g + _dot(wt_ref[2 * gp + j], dm_h)
                    dw_ref[2 * gp + j] += _dot_nt(dm_h, vn_g)
                groups.append(dvn_g)
            chunks.append(jnp.concatenate(groups, axis=1))
        db_acc[...] += db
        dvn = jnp.concatenate(chunks, axis=0)
        vhat = t['vhat']
        dgs_ref[...] += jnp.sum(dvn * vhat, axis=0, keepdims=True)
        dvh = dvn * gs_ref[...]
        dvg = t['rs'] * (dvh - jnp.mean(dvh, axis=-1, keepdims=True)
                         - vhat * jnp.mean(dvh * vhat, axis=-1, keepdims=True))
        duv_ref[:, :WIDTH_A] = (dug * _gelu_grad(t['u'], t['tu'])).astype(MXU_DTYPE)
        duv_ref[:, WIDTH_A:] = (dvg * _gelu_grad(t['v'], t['tv'])).astype(MXU_DTYPE)

        @pl.when(i == pl.num_programs(0) - 1)
        def _():
            lane_a = lax.broadcasted_iota(jnp.int32, (CHUNK, WIDTH_A), 1)
            lane = lax.broadcasted_iota(jnp.int32, (CHUNK, LANES), 1)
            acc = db_acc[...]
            out = jnp.zeros((CHUNK, LANES), F32)
            for h in range(HEADS_A):
                col = jnp.sum(jnp.where(lane_a // HEAD_DIM == h, acc, 0.0), axis=1, keepdims=True)
                out = jnp.where(lane == h, col, out)
            db_ref[...] = out
            causal = (lax.broadcasted_iota(jnp.int32, (CHUNK, CHUNK), 0)
                      >= lax.broadcasted_iota(jnp.int32, (CHUNK, CHUNK), 1))
            for h in range(HEADS_A):
                dw_ref[h] = jnp.where(causal, dw_ref[h], 0.0)

    return pl.pallas_call(
        body, name="sgu_bwd", grid=(s // tm,),
        in_specs=[_rows(tm, 2 * WIDTH_A), _rows(tm, WIDTH_A), _full((HEADS_A, CHUNK, CHUNK)),
                  _full((HEADS_A, CHUNK, CHUNK)), _full((CHUNK, WIDTH_A)), _full((1, WIDTH_A)),
                  _full((1, WIDTH_A))],
        out_specs=[_rows(tm, 2 * WIDTH_A), _full((HEADS_A, CHUNK, CHUNK)), _full((CHUNK, LANES)),
                   _full((1, WIDTH_A)), _full((1, WIDTH_A))],
        out_shape=[jax.ShapeDtypeStruct((s, 2 * WIDTH_A), MXU_DTYPE),
                   jax.ShapeDtypeStruct((HEADS_A, CHUNK, CHUNK), F32), jax.ShapeDtypeStruct((CHUNK, LANES), F32),
                   jax.ShapeDtypeStruct((1, WIDTH_A), F32), jax.ShapeDtypeStruct((1, WIDTH_A), F32)],
        scratch_shapes=[pltpu.VMEM((CHUNK, WIDTH_A), F32)],
        compiler_params=_params(("arbitrary",)),
    )(uv, dya_n, w_tril, w_tril_t, bias, g_sgu, g_a)


def _in_bwd_proj(duv, dq, dk, dv, cos_t, sin_t):
    s = duv.shape[0]
    tm = 512
    nc = WIDTH_B // LANES

    def body(duv_ref, *refs):
        dq_refs, dk_refs, dv_refs = refs[:nc], refs[nc:2 * nc], refs[2 * nc:3 * nc]
        cos_ref, sin_ref, dp_ref = refs[3 * nc:]
        cos = cos_ref[...]
        sin = sin_ref[...]
        dp_ref[:, :2 * WIDTH_A] = duv_ref[...]
        for i in range(nc):
            lo = 2 * WIDTH_A + i * LANES
            tq = _load_l256(dq_refs[i:i + 1], tm) * (HEAD_DIM ** -0.5)
            tk = _load_l256(dk_refs[i:i + 1], tm)
            dp_ref[:, lo:lo + LANES] = (tq * cos + _rope_partner(tq * sin)).astype(MXU_DTYPE)
            dp_ref[:, lo + WIDTH_B:lo + WIDTH_B + LANES] = (tk * cos + _rope_partner(tk * sin)).astype(MXU_DTYPE)
            dp_ref[:, lo + 2 * WIDTH_B:lo + 2 * WIDTH_B + LANES] = _load_l256(dv_refs[i:i + 1], tm).astype(MXU_DTYPE)

    return pl.pallas_call(
        body, name="in_bwd_proj", grid=(s // tm,),
        in_specs=[_rows(tm, 2 * WIDTH_A)] + 3 * _col_specs(tm, WIDTH_B) + [_rows(tm, LANES), _rows(tm, LANES)],
        out_specs=_rows(tm, IN_COLS), out_shape=jax.ShapeDtypeStruct((s, IN_COLS), MXU_DTYPE),
        compiler_params=_params(("arbitrary",)),
    )(duv, *([dq] * nc), *([dk] * nc), *([dv] * nc), cos_t, sin_t)


def _in_bwd_x(dproj, w_in_t, x, g_mix, dh1, after):
    s = x.shape[0]
    tm = 512

    def body(dp_ref, wt_ref, x_ref, g_ref, dh_ref, *rest):
        gx_ref, dg_ref = rest[len(after):]
        @pl.when(pl.program_id(0) == 0)
        def _():
            dg_ref[...] = jnp.zeros_like(dg_ref)

        dhn = _dot(dp_ref[...], wt_ref[...])
        n, r = _rms_stats(x_ref[...])
        dg_ref[...] += jnp.sum(dhn * n, axis=0, keepdims=True)
        gx_ref[...] = dh_ref[...] + _rms_bwd(dhn * g_ref[...], n, r)

    return pl.pallas_call(
        body, name="in_bwd_x", grid=(s // tm,),
        in_specs=[_rows(tm, IN_COLS), _full((IN_COLS, D_MODEL)), _rows(tm, D_MODEL), _full((1, D_MODEL)),
                  _rows(tm, D_MODEL)] + [pl.BlockSpec(memory_space=pl.ANY)] * len(after),
        out_specs=[_rows(tm, D_MODEL), _full((1, D_MODEL))],
        out_shape=[jax.ShapeDtypeStruct((s, D_MODEL), F32), jax.ShapeDtypeStruct((1, D_MODEL), F32)],
        compiler_params=_params(("arbitrary",)),
    )(dproj, w_in_t, x, g_mix, dh1, *after)


def _wgrad(a, b, name):
    s, m = a.shape
    n = b.shape[1]
    bm = 512 if m % 512 == 0 else FF_HALF
    ts = 1024
    nsteps = s // ts

    def body(a_ref, b_ref, o_ref, acc):
        kk = pl.program_id(1)

        @pl.when(kk == 0)
        def _():
            acc[...] = jnp.zeros_like(acc)

        acc[...] += _dot_tn(a_ref[...].astype(MXU_DTYPE), b_ref[...].astype(MXU_DTYPE))

        @pl.when(kk == nsteps - 1)
        def _():
            o_ref[...] = acc[...].astype(o_ref.dtype)

    return pl.pallas_call(
        body, name=name, grid=(m // bm, nsteps),
        in_specs=[pl.BlockSpec((ts, bm), lambda i, kk: (kk, i)), pl.BlockSpec((ts, n), lambda i, kk: (kk, 0))],
        out_specs=pl.BlockSpec((bm, n), lambda i, kk: (i, 0)), out_shape=jax.ShapeDtypeStruct((m, n), jnp.bfloat16),
        scratch_shapes=[pltpu.VMEM((bm, n), F32)],
        compiler_params=_params(("arbitrary", "arbitrary")),
    )(a, b)


def _rope_tables(s):
    half = HEAD_DIM // 2
    inv = ROPE_THETA ** (-jnp.arange(half, dtype=F32) / half)
    ang = jnp.arange(s, dtype=F32)[:, None] * jnp.tile(inv, LANES // half)[None, :]
    sign = jnp.tile(jnp.concatenate([-jnp.ones(half, F32), jnp.ones(half, F32)]), LANES // HEAD_DIM)
    return jnp.cos(ang), jnp.sin(ang) * sign[None, :]


MESH = pl.DeviceIdType.MESH
ANY = pl.BlockSpec(memory_space=pl.ANY)
SEM = pl.BlockSpec(memory_space=pltpu.SEMAPHORE)
SPLIT_COPY = pltpu.CompilerParams(has_side_effects=pltpu.SideEffectType.DATAFLOW_SIDE_EFFECTING)
SLAB_IS_TRANSPOSED = {'w_in': True, 'w_out': False, 'w_gate': True, 'w_up': True, 'w_down': False,
                      'w_ple_gate': False, 'w_ple_proj': True}


def _place():
    x, y, c = lax.axis_index("x"), lax.axis_index("y"), lax.axis_index("c")
    other_chips = [(1 - x, y), (x, 1 - y), (1 - x, 1 - y)]
    return x, y, c, other_chips


def _chip_of(chip):
    return 2 * chip[0] + chip[1]


def _half(ref, lead, hc):
    hr = ref.shape[1] // 2
    return ref.at[lead, pl.ds(hc * hr, hr), :]


def _put_own(stack, own, index):
    return lax.dynamic_update_slice(stack, own[None], (index,) + (0,) * own.ndim)


def _all_gather_now(slab):
    rows, cols = slab.shape

    def body(x_ref, out_ref, send_sems, recv_sems):
        x, y, c, chips = _place()
        sibling = (x, y, 1 - c)
        hr = rows // 2

        def copy(k, src, dst, to):
            return pltpu.make_async_remote_copy(src_ref=src, dst_ref=dst, send_sem=send_sems.at[k],
                                                recv_sem=recv_sems.at[k], device_id=to, device_id_type=MESH)

        my_half = x_ref.at[pl.ds(c * hr, hr), :]
        first = [copy(j, my_half, _half(out_ref, 2 * x + y, c), (*chip, c)) for j, chip in enumerate(chips)]
        for cp in first:
            cp.start()
        passed = [copy(3 + j, _half(out_ref, _chip_of(chip), c), _half(out_ref, _chip_of(chip), c), sibling)
                  for j, chip in enumerate(chips)]
        for j, chip in enumerate(chips):
            copy(j, my_half, _half(out_ref, _chip_of(chip), c), (*chip, c)).wait_recv()
            passed[j].start()
        for j, chip in enumerate(chips):
            copy(3 + j, my_half, _half(out_ref, _chip_of(chip), 1 - c), sibling).wait_recv()
        for cp in first + passed:
            cp.wait_send()

    gathered = pl.pallas_call(
        body, name="all_gather_now", out_shape=jax.ShapeDtypeStruct((N_CHIPS, rows, cols), slab.dtype),
        in_specs=[ANY], out_specs=ANY,
        scratch_shapes=[pltpu.SemaphoreType.DMA((6,)), pltpu.SemaphoreType.DMA((6,))],
    )(slab)
    me = 2 * lax.axis_index("x") + lax.axis_index("y")
    return _put_own(gathered, slab, me).reshape(N_CHIPS * rows, cols)


def _gather_copies(slab_refs, land_refs, send_sems, recv_sems):
    x, y, c, chips = _place()
    sends, recvs = [], []
    for k, (src, land) in enumerate(zip(slab_refs, land_refs)):
        hr = src.shape[0] // 2
        for j, chip in enumerate(chips):
            for t in range(2):
                sends.append(pltpu.make_async_remote_copy(
                    src_ref=src.at[pl.ds(c * hr, hr), :], dst_ref=_half(land, 2 * x + y, c),
                    send_sem=send_sems.at[6 * k + 2 * j + t], recv_sem=recv_sems.at[6 * k + 2 * j + c],
                    device_id=(*chip, t), device_id_type=MESH))
                recvs.append(pltpu.make_async_remote_copy(
                    src_ref=src.at[pl.ds(t * hr, hr), :], dst_ref=_half(land, _chip_of(chip), t),
                    send_sem=send_sems.at[6 * k + 2 * j + t], recv_sem=recv_sems.at[6 * k + 2 * j + t],
                    device_id=(*chip, t), device_id_type=MESH))
    return sends, recvs


def _all_gather_start(slabs, after):
    n = len(slabs)

    def body(*refs):
        slab_refs, land_refs = refs[:n], refs[n:2 * n]
        send_sems, recv_sems = refs[2 * n + 1:2 * n + 3]
        token = refs[-1]
        sends, _ = _gather_copies(slab_refs, land_refs, send_sems, recv_sems)
        for cp in sends:
            cp.start()
        token[...] = jnp.zeros_like(token)

    lands = [lax.empty((N_CHIPS,) + s.shape, s.dtype) for s in slabs]
    hbm = lambda a: pltpu.HBM(a.shape, a.dtype)
    res = pl.pallas_call(
        body, name="all_gather_start",
        out_shape=(pltpu.SemaphoreType.DMA((6 * n,)), pltpu.SemaphoreType.DMA((6 * n,)), *map(hbm, slabs),
                   *map(hbm, lands), jax.ShapeDtypeStruct((8, LANES), F32)),
        in_specs=[ANY] * (2 * n + 1),
        out_specs=(SEM, SEM, *([ANY] * (2 * n)), pl.BlockSpec(memory_space=pltpu.VMEM)),
        input_output_aliases={i: 2 + i for i in range(2 * n)}, compiler_params=SPLIT_COPY,
    )(*[pltpu.with_memory_space_constraint(a, pltpu.HBM) for a in list(slabs) + lands], after)
    return res[:-1], res[-1]


def _all_gather_wait(handle, after):
    send_sems, recv_sems = handle[:2]
    n = (len(handle) - 2) // 2
    slabs, lands = handle[2:2 + n], handle[2 + n:]

    def body(*refs):
        slab_refs, land_refs = refs[:n], refs[n:2 * n]
        send_sems, recv_sems = refs[2 * n:2 * n + 2]
        sends, recvs = _gather_copies(slab_refs, land_refs, send_sems, recv_sems)
        for cp in sends:
            cp.wait_send()
        for cp in recvs:
            cp.wait_recv()

    hbm = lambda a: pltpu.HBM(a.shape, a.dtype)
    res = pl.pallas_call(
        body, name="all_gather_wait", out_shape=tuple(map(hbm, list(slabs) + list(lands))),
        in_specs=[ANY] * (2 * n) + [SEM, SEM, ANY], out_specs=tuple([ANY] * (2 * n)),
        input_output_aliases={i: i for i in range(2 * n)}, compiler_params=SPLIT_COPY,
    )(*slabs, *lands, send_sems, recv_sems, after)
    me = 2 * lax.axis_index("x") + lax.axis_index("y")
    return [_put_own(land, slab, me).reshape(N_CHIPS * slab.shape[0], slab.shape[1])
            for slab, land in zip(res[:n], res[n:])]


def _scatter_copies(part_refs, land_refs, send_sems, recv_sems):
    x, y, c, chips = _place()
    me = 4 * x + 2 * y + c
    sends, recvs = [], []
    for k, (part, land) in enumerate(zip(part_refs, land_refs)):
        for j, chip in enumerate(chips):
            for t in range(2):
                sends.append(pltpu.make_async_remote_copy(
                    src_ref=part.at[_chip_of(chip)], dst_ref=land.at[me],
                    send_sem=send_sems.at[7 * k + 2 * j + t], recv_sem=recv_sems.at[7 * k + 2 * j + c],
                    device_id=(*chip, t), device_id_type=MESH))
                recvs.append(pltpu.make_async_remote_copy(
                    src_ref=part.at[_chip_of(chip)], dst_ref=land.at[2 * _chip_of(chip) + t],
                    send_sem=send_sems.at[7 * k + 2 * j + t], recv_sem=recv_sems.at[7 * k + 2 * j + t],
                    device_id=(*chip, t), device_id_type=MESH))
        sends.append(pltpu.make_async_remote_copy(
            src_ref=part.at[2 * x + y], dst_ref=land.at[me], send_sem=send_sems.at[7 * k + 6],
            recv_sem=recv_sems.at[7 * k + 6], device_id=(x, y, 1 - c), device_id_type=MESH))
        recvs.append(pltpu.make_async_remote_copy(
            src_ref=part.at[2 * x + y], dst_ref=land.at[4 * x + 2 * y + 1 - c],
            send_sem=send_sems.at[7 * k + 6], recv_sem=recv_sems.at[7 * k + 6], device_id=(x, y, 1 - c),
            device_id_type=MESH))
    return sends, recvs


def _reduce_scatter_start(parts, name):
    n = len(parts)
    parts = [p.reshape(N_CHIPS, p.shape[0] // N_CHIPS, p.shape[1]) for p in parts]

    def body(*refs):
        part_refs, land_refs = refs[:n], refs[n:2 * n]
        send_sems, recv_sems = refs[2 * n:2 * n + 2]
        token = refs[-1]
        sends, _ = _scatter_copies(part_refs, land_refs, send_sems, recv_sems)
        for cp in sends:
            cp.start()
        token[...] = jnp.zeros_like(token)

    lands = [lax.empty((N_DEV, p.shape[1], p.shape[2]), p.dtype) for p in parts]
    hbm = lambda a: pltpu.HBM(a.shape, a.dtype)
    res = pl.pallas_call(
        body, name=name,
        out_shape=(pltpu.SemaphoreType.DMA((7 * n,)), pltpu.SemaphoreType.DMA((7 * n,)), *map(hbm, parts),
                   *map(hbm, lands), jax.ShapeDtypeStruct((8, LANES), F32)),
        in_specs=[ANY] * (2 * n), out_specs=(SEM, SEM, *([ANY] * (2 * n)), pl.BlockSpec(memory_space=pltpu.VMEM)),
        input_output_aliases={i: 2 + i for i in range(2 * n)}, compiler_params=SPLIT_COPY,
    )(*[pltpu.with_memory_space_constraint(a, pltpu.HBM) for a in parts + lands])
    return res[:-1], res[-1]


def _reduce_scatter_wait(handle, after, name):
    send_sems, recv_sems = handle[:2]
    n = (len(handle) - 2) // 2
    parts, lands = handle[2:2 + n], handle[2 + n:]

    def body(*refs):
        part_refs, land_refs = refs[:n], refs[n:2 * n]
        send_sems, recv_sems = refs[2 * n:2 * n + 2]
        sends, recvs = _scatter_copies(part_refs, land_refs, send_sems, recv_sems)
        for cp in sends:
            cp.wait_send()
        for cp in recvs:
            cp.wait_recv()

    hbm = lambda a: pltpu.HBM(a.shape, a.dtype)
    res = pl.pallas_call(
        body, name=name, out_shape=tuple(map(hbm, list(parts) + list(lands))),
        in_specs=[ANY] * (2 * n) + [SEM, SEM, ANY], out_specs=tuple([ANY] * (2 * n)),
        input_output_aliases={i: i for i in range(2 * n)}, compiler_params=SPLIT_COPY,
    )(*parts, *lands, send_sems, recv_sems, after)
    return list(zip(res[:n], res[n:]))


def _adamw_of_shares(w, own, land, m, v, name):
    rows, cols = w.shape
    tm = rows // 4 if rows % 32 == 0 else rows
    x, y, c = lax.axis_index("x"), lax.axis_index("y"), lax.axis_index("c")
    where = jnp.stack([2 * x + y, 4 * x + 2 * y + c]).astype(jnp.int32)

    def body(where_ref, w_ref, own_ref, land_ref, m_ref, v_ref, g_ref, d_ref, nm_ref, nv_ref):
        me = where_ref[1]
        g_ = jnp.zeros((tm, cols), F32)
        for dev in range(N_DEV):
            g_ = g_ + jnp.where(me == dev, own_ref[0], land_ref[dev]).astype(F32)
        m_ = ADAM_B1 * m_ref[...] + (1.0 - ADAM_B1) * g_
        v_ = ADAM_B2 * v_ref[...] + (1.0 - ADAM_B2) * (g_ * g_)
        m_hat = m_ / (1.0 - ADAM_B1 ** ADAM_STEP)
        v_hat = v_ / (1.0 - ADAM_B2 ** ADAM_STEP)
        g_ref[...] = g_
        d_ref[...] = -ADAM_LR * (m_hat / (jnp.sqrt(v_hat) + ADAM_EPS) + ADAM_WD * w_ref[...])
        nm_ref[...] = m_
        nv_ref[...] = v_

    tile = pl.BlockSpec((tm, cols), lambda i, where_ref: (i, 0))
    spec = pltpu.PrefetchScalarGridSpec(
        num_scalar_prefetch=1, grid=(rows // tm,),
        in_specs=[tile, pl.BlockSpec((1, tm, cols), lambda i, where_ref: (where_ref[0], i, 0)),
                  pl.BlockSpec((N_DEV, tm, cols), lambda i, where_ref: (0, i, 0)), tile, tile],
        out_specs=[tile] * 4)
    return pl.pallas_call(
        body, name=name, grid_spec=spec, out_shape=[jax.ShapeDtypeStruct(w.shape, F32)] * 4,
        compiler_params=_params(("arbitrary",)),
    )(where, w, own, land, m, v)


def _pack_small(values):
    flat = jnp.concatenate([values[n].reshape(-1).astype(F32) for n in SMALL])
    return jnp.pad(flat, (0, SMALL_ROWS * D_MODEL - flat.shape[0])).reshape(SMALL_ROWS, D_MODEL)


def _unpack_small(block, shapes):
    flat = block.reshape(-1)
    out, lo = {}, 0
    for n in SMALL:
        out[n] = flat[lo:lo + SMALL_SIZES[n]].reshape(shapes[n])
        lo += SMALL_SIZES[n]
    return out


def _after(token, a):
    return a + token[:1, :1].astype(a.dtype)


def kernel(x, p, mix_norm_g, w_in, sgu_w, sgu_b, sgu_norm_g, out_norm_a, out_norm_b, w_out, ffn_norm_g, w_gate, w_up, w_down, ple_norm_g, w_ple_gate, w_ple_proj, final_norm_g, loss_target, m_mix_norm_g, m_w_in, m_sgu_w, m_sgu_b, m_sgu_norm_g, m_out_norm_a, m_out_norm_b, m_w_out, m_ffn_norm_g, m_w_gate, m_w_up, m_w_down, m_ple_norm_g, m_w_ple_gate, m_w_ple_proj, m_final_norm_g, v_mix_norm_g, v_w_in, v_sgu_w, v_sgu_b, v_sgu_norm_g, v_out_norm_a, v_out_norm_b, v_w_out, v_ffn_norm_g, v_w_gate, v_w_up, v_w_down, v_ple_norm_g, v_w_ple_gate, v_w_ple_proj, v_final_norm_g):
    given = dict(locals())
    drop_lead = lambda a, lead: a.reshape(a.shape[lead:])
    xs, ps, target = drop_lead(x, 1), drop_lead(p, 2), drop_lead(loss_target, 1)
    s = xs.shape[0]
    shard = lambda name: drop_lead(given[name], 1)

    def slab_of(name):
        local = shard(name).astype(MXU_DTYPE)
        return local.T if SLAB_IS_TRANSPOSED[name] else local

    w_in_t = _all_gather_now(slab_of('w_in'))
    later = ['w_out', 'w_gate', 'w_up', 'w_down', 'w_ple_gate', 'w_ple_proj']
    gather, token = _all_gather_start([slab_of(n) for n in later], w_in_t)

    cos_t, sin_t = _rope_tables(s)
    tril = jnp.tril(jnp.ones((CHUNK, CHUNK), F32))
    w_tril = (sgu_w.reshape(HEADS_A, CHUNK, CHUNK) * tril).astype(MXU_DTYPE)
    w_tril_t = jnp.swapaxes(w_tril, 1, 2)
    bias = jnp.repeat(sgu_b.reshape(HEADS_A, CHUNK).T, HEAD_DIM, axis=1)
    g = {n: given[n].reshape(1, -1) for n in SMALL if n not in ('sgu_w', 'sgu_b')}

    uv, q, k, v, hn1 = _in_fwd(xs, _after(token, g['mix_norm_g']), w_in_t, cos_t, sin_t)
    ya_n = _sgu_fwd(uv, w_tril, bias, g['sgu_norm_g'], g['out_norm_a'])
    branches = [_attn_fwd_branch(q, k, v, dil) for dil in DILATIONS[:-1]]
    y_b, lse = _attn_fwd_branch(q, k, v, DILATIONS[-1], earlier=branches)
    stacks = dict(zip(later, _all_gather_wait(gather, lse)))
    w_gate_t, w_up_t, w_pp_t = stacks['w_gate'], stacks['w_up'], stacks['w_ple_proj']
    h1, y_n = _out_fwd(ya_n, y_b, g['out_norm_b'], stacks['w_out'], xs)
    h2, gate, up, hn2 = _ffn_fwd(h1, g['ffn_norm_g'], w_gate_t, w_up_t, stacks['w_down'])
    loss, dh2, dz, dpp, hn3, d_ple_g, d_final_g = _ple_loss(
        h2, ps, target, g['ple_norm_g'], stacks['w_ple_gate'], w_pp_t, g['final_norm_g'])

    share = {}
    share['w_ple_gate'] = _wgrad(hn3, dz, "wgrad_ple_gate")
    share['w_ple_proj'] = _wgrad(dpp, ps, "wgrad_ple_proj")
    scatter_1, token = _reduce_scatter_start([share['w_ple_gate'], share['w_ple_proj']], "reduce_scatter_start_1")
    dh1, act, dgate, dup, d_ffn_g = _ffn_bwd(dh2, h1, gate, up, _after(token, g['ffn_norm_g']), stacks['w_down'],
                                             w_gate_t, w_up_t)
    share['w_down'] = _wgrad(act, dh2, "wgrad_down")
    share['w_gate'] = _wgrad(dgate, hn2, "wgrad_gate")
    share['w_up'] = _wgrad(dup, hn2, "wgrad_up")
    scatter_2, token = _reduce_scatter_start([share['w_down'], share['w_gate'], share['w_up']],
                                             "reduce_scatter_start_2")
    dya_n, dyb, d_out_b = _out_bwd(dh1, y_b, _after(token, g['out_norm_b']), stacks['w_out'])
    share['w_out'] = _wgrad(y_n, dh1, "wgrad_out")
    scatter_3, token = _reduce_scatter_start([share['w_out']], "reduce_scatter_start_3")
    grads = None
    for dil in DILATIONS:
        grads = _attn_bwd_branch(q, k, v, dyb, y_b, lse, grads, dil)
    duv, d_sgu_w, d_sgu_b, d_sgu_g, d_out_a = _sgu_bwd(uv, dya_n, w_tril, w_tril_t, bias,
                                                       _after(token, g['sgu_norm_g']), g['out_norm_a'])
    dproj = _in_bwd_proj(duv, grads[0], grads[1], grads[2], cos_t, sin_t)
    share['w_in'] = _wgrad(dproj, hn1, "wgrad_in")
    scatter_4, token = _reduce_scatter_start([share['w_in']], "reduce_scatter_start_4")

    grads, deltas, new_m, new_v = {}, {}, {}, {}
    add_lead = lambda a: a.reshape((1,) + a.shape)

    def finish(names, handles, after, tag):
        landed = []
        for i, handle in enumerate(handles):
            landed += _reduce_scatter_wait(handle, after, "reduce_scatter_wait_%s%d" % (tag, i))
        for n, (own, land) in zip(names, landed):
            turn = (lambda a: a.T) if SLAB_IS_TRANSPOSED[n] else (lambda a: a)
            res = _adamw_of_shares(turn(shard(n)), own, land, turn(shard("m_" + n)), turn(shard("v_" + n)),
                                   "adamw_" + n)
            grads[n], deltas[n], new_m[n], new_v[n] = (add_lead(turn(a)) for a in res)

    finish(['w_ple_gate', 'w_ple_proj', 'w_down', 'w_gate', 'w_up', 'w_out'], [scatter_1, scatter_2, scatter_3], token,
           "early")
    grad_x, d_mix_g = _in_bwd_x(dproj, w_in_t, xs, g['mix_norm_g'], dh1,
                                after=[new_v[n] for n in ('w_down', 'w_gate', 'w_up', 'w_out')])

    gs = {'mix_norm_g': d_mix_g, 'sgu_w': d_sgu_w, 'sgu_b': d_sgu_b[:, :HEADS_A].T, 'sgu_norm_g': d_sgu_g,
          'out_norm_a': d_out_a, 'out_norm_b': d_out_b, 'ffn_norm_g': d_ffn_g, 'ple_norm_g': d_ple_g,
          'final_norm_g': d_final_g}
    gs_block = _pack_small(gs).at[SMALL_ROWS - 1, 0].set(loss[0, 0])
    to_all = jnp.broadcast_to(gs_block[None], (N_CHIPS,) + gs_block.shape).reshape(-1, D_MODEL)
    scatter_small, token = _reduce_scatter_start([to_all], "small_all_reduce_start")
    finish(['w_in'], [scatter_4], token, "last")
    (own, land), = _reduce_scatter_wait(scatter_small, new_v['w_in'], "small_all_reduce_wait")
    small = {n: given[n] for n in SMALL}
    small_res = _adamw_of_shares(_pack_small(small), own, land, _pack_small({n: given["m_" + n] for n in SMALL}),
                                 _pack_small({n: given["v_" + n] for n in SMALL}), "adamw_small")
    loss_out = small_res[0][SMALL_ROWS - 1, 0]
    small_shapes = {n: given[n].shape for n in SMALL}
    for res, blk in zip((grads, deltas, new_m, new_v), small_res):
        res.update(_unpack_small(blk, small_shapes))

    outs = [loss_out, add_lead(grad_x)]
    for res in (grads, deltas, new_m, new_v):
        outs += [res[n] for n in WEIGHT_NAMES]
    return tuple(outs)
```

```python
import functools
import itertools
import math

import jax
import jax.numpy as jnp
import numpy as np
from jax import lax
from jax.experimental import pallas as pl
from jax.experimental.pallas import tpu as pltpu

F32 = jnp.float32
MXU_DTYPE = jnp.bfloat16

D_MODEL = 1024
HEAD_DIM = 64
HEADS_A = 4
HEADS_B = 12
WIDTH_A = HEADS_A * HEAD_DIM
WIDTH_B = HEADS_B * HEAD_DIM
CHUNK = 128
BLOCK = 128
DILATIONS = (4, 16)
ROPE_THETA = 10000.0
D_FF = 2816
FF_HALF = D_FF // 2
FF_STRIPS = ((0, 1024), (1024, 2048), (2048, D_FF))
PLE_DIM = 256
IN_COLS = 2 * WIDTH_A + 3 * WIDTH_B
EPS = 1e-6
LANES = 128
N_CHIPS = 4
N_DEV = 8

ADAM_LR = 0.001
ADAM_B1 = 0.9
ADAM_B2 = 0.999
ADAM_EPS = 1e-08
ADAM_WD = 0.01
ADAM_STEP = 10

VMEM_LIMIT = 56 * 1024 * 1024

WEIGHT_NAMES = ['mix_norm_g', 'w_in', 'sgu_w', 'sgu_b', 'sgu_norm_g', 'out_norm_a', 'out_norm_b', 'w_out',
                'ffn_norm_g', 'w_gate', 'w_up', 'w_down', 'ple_norm_g', 'w_ple_gate', 'w_ple_proj', 'final_norm_g']
SMALL = ['mix_norm_g', 'sgu_w', 'sgu_b', 'sgu_norm_g', 'out_norm_a', 'out_norm_b', 'ffn_norm_g', 'ple_norm_g',
         'final_norm_g']
SMALL_SIZES = {'mix_norm_g': 1024, 'sgu_w': 65536, 'sgu_b': 512, 'sgu_norm_g': 256, 'out_norm_a': 256,
               'out_norm_b': 768, 'ffn_norm_g': 1024, 'ple_norm_g': 1024, 'final_norm_g': 1024}
SMALL_ROWS = 72


def _params(semantics=None):
    return pltpu.CompilerParams(dimension_semantics=semantics, vmem_limit_bytes=VMEM_LIMIT)


def _full(shape):
    nd = len(shape)
    return pl.BlockSpec(shape, lambda i: (0,) * nd, pipeline_mode=pl.Buffered(1))


def _rows(tm, width):
    return pl.BlockSpec((tm, width), lambda i: (i, 0))


def _rms_stats(x):
    r = lax.rsqrt(jnp.mean(x * x, axis=-1, keepdims=True) + EPS)
    return x * r, r


def _rms_bwd(dn, n, r):
    return r * (dn - n * jnp.mean(dn * n, axis=-1, keepdims=True))


def _dot(a, b):
    return jnp.dot(a, b, preferred_element_type=F32)


def _dot_nt(a, b):
    return lax.dot_general(a, b, (((1,), (1,)), ((), ())), preferred_element_type=F32)


def _dot_tn(a, b):
    return lax.dot_general(a, b, (((0,), (0,)), ((), ())), preferred_element_type=F32)


def _gelu_parts(x):
    c = math.sqrt(2.0 / math.pi)
    t = jnp.tanh(c * (x + 0.044715 * x * x * x))
    return 0.5 * x * (1.0 + t), t


def _gelu_grad(x, t):
    c = math.sqrt(2.0 / math.pi)
    return 0.5 * (1.0 + t) + 0.5 * x * (1.0 - t * t) * c * (1.0 + 3.0 * 0.044715 * x * x)


def _half_masks(dtype):
    lane = lax.broadcasted_iota(jnp.int32, (BLOCK, LANES), 1)
    lo = (lane < HEAD_DIM).astype(F32)
    return lo.astype(dtype), (1.0 - lo).astype(dtype)


def _rope_partner(t):
    lane = lax.broadcasted_iota(jnp.int32, t.shape, 1)
    first_half = (lane % HEAD_DIM) < (HEAD_DIM // 2)
    return jnp.where(first_half, pltpu.roll(t, LANES - HEAD_DIM // 2, 1), pltpu.roll(t, HEAD_DIM // 2, 1))


PAIRS_ABREAST = 2
RESIDUES_PER_STEP = 4
L_BLOCK = 256
L_GROUP = 16


def _store_l256(scr, out_ref, cols, value, chunk_ref=None):
    tm = value.shape[0]
    half = L_GROUP // 2
    scr[...] = value
    for blk in range(tm // L_BLOCK):
        pieces = [scr[pl.ds(blk * L_BLOCK + r, L_GROUP, stride=L_GROUP), :] for r in range(L_GROUP)]
        for r, piece in enumerate(pieces):
            lo = blk * L_BLOCK + r * L_GROUP
            out_ref[lo:lo + L_GROUP, cols] = piece.astype(out_ref.dtype)
        if chunk_ref is not None:
            for chunk in range(L_BLOCK // BLOCK):
                for r in range(0, L_GROUP, 2):
                    lo = blk * L_BLOCK + chunk * BLOCK + r * half
                    both = [p[chunk * half:(chunk + 1) * half] for p in pieces[r:r + 2]]
                    chunk_ref[lo:lo + L_GROUP, cols] = jnp.concatenate(both, axis=0).astype(chunk_ref.dtype)


def _load_l256(col_refs, tm):
    cols = []
    for ref in col_refs:
        pieces = [ref[pl.ds(blk * L_BLOCK + i, L_GROUP, stride=L_GROUP), :]
                  for blk in range(tm // L_BLOCK) for i in range(L_GROUP)]
        cols.append(jnp.concatenate(pieces, axis=0))
    return jnp.concatenate(cols, axis=1)


def _col_specs(tm, width):
    return [pl.BlockSpec((tm, LANES), lambda i, j=j: (i, j)) for j in range(width // LANES)]


def _in_fwd(x, g_mix, w_in_t, cos_t, sin_t):
    s = x.shape[0]
    tm = 512

    def body(x_ref, g_ref, wt_ref, cos_ref, sin_ref, uv_ref, q_ref, k_ref, v_ref, q1_ref, k1_ref, v1_ref, hn_ref,
             *scrs):
        n, _ = _rms_stats(x_ref[...])
        hn = (n * g_ref[...]).astype(MXU_DTYPE)
        hn_ref[...] = hn
        cos = cos_ref[...]
        sin = sin_ref[...]
        strip = 2 * LANES
        for j in range(IN_COLS // strip):
            proj = _dot_nt(hn, wt_ref[j * strip:(j + 1) * strip, :])
            lo = j * strip - 2 * WIDTH_A
            if lo < 0:
                uv_ref[:, j * strip:(j + 1) * strip] = proj
                continue
            which, lo = divmod(lo, WIDTH_B)
            for i in range(strip // LANES):
                t = proj[:, i * LANES:(i + 1) * LANES]
                cols = slice(lo + i * LANES, lo + (i + 1) * LANES)
                scr = scrs[i]
                if which == 0:
                    _store_l256(scr, q_ref, cols, (t * cos + _rope_partner(t) * sin) * (HEAD_DIM ** -0.5), q1_ref)
                elif which == 1:
                    _store_l256(scr, k_ref, cols, t * cos + _rope_partner(t) * sin, k1_ref)
                else:
                    _store_l256(scr, v_ref, cols, t, v1_ref)

    return pl.pallas_call(
        body, name="in_fwd", grid=(s // tm,), scratch_shapes=[pltpu.VMEM((tm, LANES), F32)] * 2,
        in_specs=[_rows(tm, D_MODEL), _full((1, D_MODEL)), _full((IN_COLS, D_MODEL)), _rows(tm, LANES),
                  _rows(tm, LANES)],
        out_specs=[_rows(tm, 2 * WIDTH_A)] + [_rows(tm, WIDTH_B)] * 6 + [_rows(tm, D_MODEL)],
        out_shape=[jax.ShapeDtypeStruct((s, 2 * WIDTH_A), F32)] + [jax.ShapeDtypeStruct((s, WIDTH_B), MXU_DTYPE)] * 6
        + [jax.ShapeDtypeStruct((s, D_MODEL), MXU_DTYPE)],
        compiler_params=_params(("arbitrary",)),
    )(x, g_mix, w_in_t, cos_t, sin_t)


class _Branch:
    def __init__(self, dil, s, qn=BLOCK):
        self.dil = dil
        i = np.arange(L_GROUP)
        if dil == 16:
            nblk, self.res = qn // 16, RESIDUES_PER_STEP
            self.grid = (16 // self.res, s // (L_BLOCK * nblk))
            self.shape = (nblk, 1, self.res, L_GROUP)
            self.index = lambda r, n: (n, r // (4 // self.res), r % (4 // self.res), 0, 0)
            pos = (np.arange(nblk)[:, None] * 16 + i[None, :]).reshape(-1)
        elif dil == 4:
            nblk, self.res = qn // 64, RESIDUES_PER_STEP
            self.grid = (4 // self.res, s // (L_BLOCK * nblk))
            self.shape = (nblk, 4, self.res, L_GROUP)
            self.index = lambda r, n: (n, 0, r, 0, 0)
            pos = (np.arange(nblk)[:, None, None] * 64 + np.arange(4)[None, :, None]
                   + 4 * i[None, None, :]).reshape(-1)
        else:
            self.res = 1
            self.grid = (1, s // L_BLOCK)
            self.shape = (1, 4, 4, L_GROUP)
            self.index = lambda r, n: (n, 0, 0, 0, 0)
            pos = (np.arange(16)[:, None] + 16 * i[None, :]).reshape(-1)
        self.qn = pos.shape[0]
        self.nb = self.grid[1]
        dist = pos[:, None] - np.concatenate([pos - self.qn, pos])[None, :]
        band = (dist >= 0) & (dist <= BLOCK)
        start = band & (np.arange(2 * self.qn)[None, :] >= self.qn)
        self.bias = np.where(np.stack([band, start]), 0.0, -np.inf).astype(np.float32)

    def view(self, a):
        return a.reshape(a.shape[0] // L_BLOCK, 4, 4, L_GROUP, a.shape[1])

    def spec(self, w, step=lambda n: n):
        return pl.BlockSpec(self.shape + (w,), lambda r, n: self.index(r, step(n)))

    def bias_spec(self, step=lambda n: n):
        return pl.BlockSpec((1, self.qn, 2 * self.qn), lambda r, n: (jnp.where(step(n) == 0, 1, 0), 0, 0))

    def load(self, ref, cols=slice(None), j=0):
        x = ref[:, :, :, :, cols] if self.dil == 1 else ref[:, :, j, :, cols]
        return x.reshape(self.qn, x.shape[-1])

    def store(self, ref, cols, value, j=0):
        if self.dil == 1:
            ref[:, :, :, :, cols] = value.reshape(self.shape + (value.shape[-1],))
        else:
            ref[:, :, j, :, cols] = value.reshape(self.shape[:2] + (L_GROUP, value.shape[-1]))


def _attn_fwd_branch(q, k, v, dil, earlier=()):
    s = q.shape[0]
    br = _Branch(dil, s)
    qn = br.qn
    nearly = len(earlier)

    def body(bias_ref, q_ref, kc_ref, kp_ref, vc_ref, vp_ref, *refs):
        early_refs, (o_ref, lse_ref) = refs[:2 * nearly], refs[2 * nearly:]
        bias2 = jnp.concatenate([bias_ref[0], bias_ref[0]], axis=0)
        lo = lax.broadcasted_iota(jnp.int32, (qn, LANES), 1) < HEAD_DIM
        mask_lo = lo.astype(F32).astype(MXU_DTYPE)
        for j, hp in itertools.product(range(br.res), range(HEADS_B // 2)):
            cols = slice(hp * LANES, (hp + 1) * LANES)
            qp = br.load(q_ref, cols, j)
            kcat = jnp.concatenate([br.load(kp_ref, cols, j), br.load(kc_ref, cols, j)], axis=0)
            vcat = jnp.concatenate([br.load(vp_ref, cols, j), br.load(vc_ref, cols, j)], axis=0)
            sc = _dot_nt(jnp.concatenate([qp * mask_lo, qp * (1 - mask_lo)], axis=0), kcat) + bias2
            m = jnp.max(sc, axis=1, keepdims=True)
            p = jnp.exp(sc - m)
            l = jnp.sum(p, axis=1, keepdims=True)
            out = _dot(p.astype(MXU_DTYPE), vcat) / l
            lse = m + jnp.log(l)
            outs = [br.load(r, cols, j) for r in early_refs[:nearly]] + [jnp.where(lo, out[:qn], out[qn:])]
            lses = [br.load(r, cols, j) for r in early_refs[nearly:]] + [jnp.where(lo, lse[:qn], lse[qn:])]
            if nearly:
                top = functools.reduce(jnp.maximum, lses)
                ws = [jnp.exp(x - top) for x in lses]
                den = functools.reduce(jnp.add, ws)
                outs = [functools.reduce(jnp.add, [w * o for w, o in zip(ws, outs)]) / den]
                lses = [top + jnp.log(den)]
            br.store(o_ref, cols, outs[0], j)
            br.store(lse_ref, cols, lses[0], j)

    before = lambda n: jnp.maximum(n - 1, 0)
    res = pl.pallas_call(
        body, name="attn_fwd_d%d" % dil, grid=br.grid,
        in_specs=[br.bias_spec(), br.spec(WIDTH_B), br.spec(WIDTH_B), br.spec(WIDTH_B, before), br.spec(WIDTH_B),
                  br.spec(WIDTH_B, before)] + [br.spec(WIDTH_B)] * (2 * nearly),
        out_specs=[br.spec(WIDTH_B), br.spec(WIDTH_B)],
        out_shape=[jax.ShapeDtypeStruct((s // L_BLOCK, 4, 4, L_GROUP, WIDTH_B), F32)] * 2,
        compiler_params=_params(("arbitrary", "arbitrary")),
    )(jnp.asarray(br.bias), br.view(q), br.view(k), br.view(k), br.view(v), br.view(v),
      *[br.view(o) for o, _ in earlier], *[br.view(x) for _, x in earlier])
    return tuple(a.reshape(s, WIDTH_B) for a in res)


LOCAL_CHUNKS = 4


def _local_bias():
    row = np.arange(BLOCK)
    pos = L_GROUP * (row % (L_GROUP // 2)) + row // (L_GROUP // 2)
    dist = pos[:, None] - np.concatenate([pos - BLOCK, pos])[None, :]
    band = (dist >= 0) & (dist <= BLOCK)
    start = band & (np.arange(2 * BLOCK)[None, :] >= BLOCK)
    return np.where(np.stack([band, start]), 0.0, -np.inf).astype(np.float32)


def _chunk_view(a):
    return a.reshape(a.shape[0] // L_BLOCK, L_GROUP, 2, L_GROUP // 2, a.shape[1])


def _chunk_of(ref, j, cols=slice(None)):
    x = ref[j // 2, :, j % 2, :, cols]
    return x.reshape(BLOCK, x.shape[-1])


def _put_chunk(ref, j, cols, value):
    ref[j // 2, :, j % 2, :, cols] = value.reshape(L_GROUP, L_GROUP // 2, value.shape[-1])


def _local_keys(cur_ref, before_ref, j, cols):
    here = slice(j * BLOCK, (j + 1) * BLOCK)
    before = before_ref[:, cols] if j == 0 else cur_ref[(j - 1) * BLOCK:j * BLOCK, cols]
    return jnp.concatenate([before, cur_ref[here, cols]], axis=0)


def _local_specs(s, step=lambda n: n):
    rows = LOCAL_CHUNKS * BLOCK
    cur = pl.BlockSpec((rows, WIDTH_B), lambda n: (step(n), 0))
    before = pl.BlockSpec((BLOCK, WIDTH_B), lambda n: (jnp.maximum(LOCAL_CHUNKS * step(n) - 1, 0), 0))
    return [cur, cur, before, cur, before]


def _attn_fwd_local(q1, k1, v1):
    s = q1.shape[0]
    rows = LOCAL_CHUNKS * BLOCK
    qn = BLOCK

    def body(bias_ref, q_ref, kc_ref, kp_ref, vc_ref, vp_ref, o_ref, lse_ref):
        first = jnp.where(pl.program_id(0) == 0, bias_ref[1], bias_ref[0])
        biases = [jnp.concatenate([b, b], axis=0) for b in (first, bias_ref[0])]
        lo = lax.broadcasted_iota(jnp.int32, (qn, LANES), 1) < HEAD_DIM
        mask_lo = lo.astype(F32).astype(MXU_DTYPE)
        for j, hp in itertools.product(range(LOCAL_CHUNKS), range(HEADS_B // 2)):
            cols = slice(hp * LANES, (hp + 1) * LANES)
            qp = q_ref[j * BLOCK:(j + 1) * BLOCK, cols]
            kcat = _local_keys(kc_ref, kp_ref, j, cols)
            vcat = _local_keys(vc_ref, vp_ref, j, cols)
            sc = _dot_nt(jnp.concatenate([qp * mask_lo, qp * (1 - mask_lo)], axis=0), kcat) + biases[min(j, 1)]
            m = jnp.max(sc, axis=1, keepdims=True)
            p = jnp.exp(sc - m)
            l = jnp.sum(p, axis=1, keepdims=True)
            out = _dot(p.astype(MXU_DTYPE), vcat) / l
            lse = m + jnp.log(l)
            _put_chunk(o_ref, j, cols, jnp.where(lo, out[:qn], out[qn:]))
            _put_chunk(lse_ref, j, cols, jnp.where(lo, lse[:qn], lse[qn:]))

    out_spec = pl.BlockSpec((LOCAL_CHUNKS // 2, L_GROUP, 2, L_GROUP // 2, WIDTH_B), lambda n: (n, 0, 0, 0, 0))
    res = pl.pallas_call(
        body, name="attn_fwd_d1", grid=(s // rows,),
        in_specs=[_full((2, BLOCK, 2 * BLOCK))] + _local_specs(s), out_specs=[out_spec] * 2,
        out_shape=[jax.ShapeDtypeStruct((s // L_BLOCK, L_GROUP, 2, L_GROUP // 2, WIDTH_B), F32)] * 2,
        compiler_params=_params(("arbitrary",)),
    )(jnp.asarray(_local_bias()), q1, k1, k1, v1, v1)
    return tuple(a.reshape(s, WIDTH_B) for a in res)


def _attn_bwd_local(q1, k1, v1, do, o, lse):
    s = q1.shape[0]
    rows = LOCAL_CHUNKS * BLOCK
    nsteps = s // rows
    qn = BLOCK

    def body(bias_ref, q_ref, kc_ref, kp_ref, vc_ref, vp_ref, do_ref, o_ref, lse_ref, dq_ref, dk_ref, dv_ref,
             dk_buf, dv_buf):
        n = pl.program_id(0)

        @pl.when(n == 0)
        def _():
            dk_buf[...] = jnp.zeros_like(dk_buf)
            dv_buf[...] = jnp.zeros_like(dv_buf)

        @pl.when(n < nsteps)
        def _():
            first = jnp.where(n == 0, bias_ref[1], bias_ref[0])
            biases = [jnp.concatenate([b, b], axis=0) for b in (first, bias_ref[0])]
            lo = lax.broadcasted_iota(jnp.int32, (qn, LANES), 1) < HEAD_DIM
            mask_f = lo.astype(F32)
            mask_lo = mask_f.astype(MXU_DTYPE)
            dk_buf[LOCAL_CHUNKS:] = jnp.zeros((LOCAL_CHUNKS, qn, WIDTH_B), F32)
            dv_buf[LOCAL_CHUNKS:] = jnp.zeros((LOCAL_CHUNKS, qn, WIDTH_B), F32)

            def prepare(j, hp):
                cols = slice(hp * LANES, (hp + 1) * LANES)
                qp = q_ref[j * BLOCK:(j + 1) * BLOCK, cols]
                dop = _chunk_of(do_ref, j, cols)
                prod = dop * _chunk_of(o_ref, j, cols)
                prod_lo = prod * mask_f
                lse = _chunk_of(lse_ref, j, cols)
                return dict(
                    j=j, cols=cols, kcat=_local_keys(kc_ref, kp_ref, j, cols), vcat=_local_keys(vc_ref, vp_ref, j, cols),
                    qs=jnp.concatenate([qp * mask_lo, qp * (1 - mask_lo)], axis=0),
                    dos=jnp.concatenate([dop * mask_f, dop * (1.0 - mask_f)], axis=0).astype(MXU_DTYPE),
                    delta=jnp.concatenate([jnp.sum(prod_lo, axis=1, keepdims=True),
                                           jnp.sum(prod - prod_lo, axis=1, keepdims=True)], axis=0),
                    lse2=jnp.concatenate([lse[:, :1], lse[:, HEAD_DIM:HEAD_DIM + 1]], axis=0))

            def scores(t):
                t['sc'] = _dot_nt(t['qs'], t['kcat'])
                t['dp'] = _dot_nt(t['dos'], t['vcat'])

            def softmax(t):
                p = jnp.exp(t['sc'] + biases[min(t['j'], 1)] - t['lse2'])
                t['ds'] = (p * (t['dp'] - t['delta'])).astype(MXU_DTYPE)
                t['p'] = p.astype(MXU_DTYPE)

            def gradients(t):
                t['dvc'] = _dot_tn(t['p'], t['dos'])
                t['dkc'] = _dot_tn(t['ds'], t['qs'])
                t['dq2'] = _dot(t['ds'], t['kcat'])

            def store(t):
                j, cols = t['j'], t['cols']
                _put_chunk(dq_ref, j, cols, jnp.where(lo, t['dq2'][:qn], t['dq2'][qn:]))
                for buf, both in ((dk_buf, t['dkc']), (dv_buf, t['dvc'])):
                    buf[LOCAL_CHUNKS + j - 1, :, cols] += both[:qn]
                    buf[LOCAL_CHUNKS + j, :, cols] += both[qn:]

            for j, first_pair in itertools.product(range(LOCAL_CHUNKS), range(0, HEADS_B // 2, PAIRS_ABREAST)):
                group = [prepare(j, hp) for hp in range(first_pair, first_pair + PAIRS_ABREAST)]
                for stage in (scores, softmax, gradients, store):
                    for t in group:
                        stage(t)

        for j in range(LOCAL_CHUNKS):
            _put_chunk(dk_ref, j, slice(None), dk_buf[j])
            _put_chunk(dv_ref, j, slice(None), dv_buf[j])
        dk_buf[:LOCAL_CHUNKS] = dk_buf[LOCAL_CHUNKS:]
        dv_buf[:LOCAL_CHUNKS] = dv_buf[LOCAL_CHUNKS:]

    cur = lambda n: jnp.minimum(n, nsteps - 1)
    late = lambda n: jnp.maximum(n - 1, 0)
    view_spec = lambda step: pl.BlockSpec((LOCAL_CHUNKS // 2, L_GROUP, 2, L_GROUP // 2, WIDTH_B),
                                          lambda n: (step(n), 0, 0, 0, 0))
    res = pl.pallas_call(
        body, name="attn_bwd_d1", grid=(nsteps + 1,),
        in_specs=[_full((2, BLOCK, 2 * BLOCK))] + _local_specs(s, cur) + [view_spec(cur)] * 3,
        out_specs=[view_spec(cur), view_spec(late), view_spec(late)],
        out_shape=[jax.ShapeDtypeStruct((s // L_BLOCK, L_GROUP, 2, L_GROUP // 2, WIDTH_B), F32)] * 3,
        scratch_shapes=[pltpu.VMEM((2 * LOCAL_CHUNKS, qn, WIDTH_B), F32)] * 2,
        compiler_params=_params(("arbitrary",)),
    )(jnp.asarray(_local_bias()), q1, k1, k1, v1, v1, _chunk_view(do), _chunk_view(o), _chunk_view(lse))
    return tuple(a.reshape(s, WIDTH_B) for a in res)


def _sgu_forward_tile(uv, w_ref, bias, g_sgu):
    tm = uv.shape[0]
    u = uv[:, :WIDTH_A]
    v = uv[:, WIDTH_A:]
    ug, tu = _gelu_parts(u)
    vg, tv = _gelu_parts(v)
    mu = jnp.mean(vg, axis=-1, keepdims=True)
    vc = vg - mu
    rs = lax.rsqrt(jnp.mean(vc * vc, axis=-1, keepdims=True) + EPS)
    vhat = vc * rs
    vn = (vhat * g_sgu).astype(MXU_DTYPE)
    masks = _half_masks(MXU_DTYPE)
    chunks = []
    for c in range(tm // CHUNK):
        rows = slice(c * CHUNK, (c + 1) * CHUNK)
        groups = []
        for gp in range(2):
            vn_g = vn[rows, gp * LANES:(gp + 1) * LANES]
            groups.append(_dot(w_ref[2 * gp], vn_g * masks[0]) + _dot(w_ref[2 * gp + 1], vn_g * masks[1]))
        chunks.append(jnp.concatenate(groups, axis=1) + bias)
    mixed = jnp.concatenate(chunks, axis=0)
    return dict(u=u, v=v, ug=ug, tu=tu, tv=tv, rs=rs, vhat=vhat, vn=vn, mixed=mixed, ya=ug * mixed)


def _sgu_fwd(uv, w_tril, bias, g_sgu, g_a):
    s = uv.shape[0]
    tm = 512

    def body(uv_ref, w_ref, b_ref, gs_ref, ga_ref, o_ref):
        t = _sgu_forward_tile(uv_ref[...], w_ref, b_ref[...], gs_ref[...])
        n, _ = _rms_stats(t['ya'])
        o_ref[...] = (n * ga_ref[...]).astype(MXU_DTYPE)

    return pl.pallas_call(
        body, name="sgu_fwd", grid=(s // tm,),
        in_specs=[_rows(tm, 2 * WIDTH_A), _full((HEADS_A, CHUNK, CHUNK)), _full((CHUNK, WIDTH_A)),
                  _full((1, WIDTH_A)), _full((1, WIDTH_A))],
        out_specs=_rows(tm, WIDTH_A), out_shape=jax.ShapeDtypeStruct((s, WIDTH_A), MXU_DTYPE),
        compiler_params=_params(("arbitrary",)),
    )(uv, w_tril, bias, g_sgu, g_a)


def _out_fwd(ya_n, y_b, g_b, w_out, x):
    s = x.shape[0]
    tm = 512
    nc = WIDTH_B // LANES

    def body(ya_ref, *refs):
        yb_refs = refs[:nc]
        g_ref, w_ref, x_ref, h_ref, yn_ref = refs[nc:]
        n, _ = _rms_stats(_load_l256(yb_refs, tm))
        yn = jnp.concatenate([ya_ref[...], (n * g_ref[...]).astype(MXU_DTYPE)], axis=1)
        yn_ref[...] = yn
        h_ref[...] = x_ref[...] + _dot(yn, w_ref[...])

    return pl.pallas_call(
        body, name="out_fwd", grid=(s // tm,),
        in_specs=[_rows(tm, WIDTH_A)] + _col_specs(tm, WIDTH_B) + [_full((1, WIDTH_B)), _full((D_MODEL, D_MODEL)),
                                                                 _rows(tm, D_MODEL)],
        out_specs=[_rows(tm, D_MODEL), _rows(tm, D_MODEL)],
        out_shape=[jax.ShapeDtypeStruct((s, D_MODEL), F32), jax.ShapeDtypeStruct((s, D_MODEL), MXU_DTYPE)],
        compiler_params=_params(("arbitrary",)),
    )(ya_n, *([y_b] * nc), g_b, w_out, x)


def _ffn_fwd(h1, g_ffn, w_gate_t, w_up_t, w_down):
    s = h1.shape[0]
    tm = 512

    def body(h_ref, g_ref, wgt_ref, wut_ref, wd_ref, o_ref, gate_ref, up_ref, hn_ref):
        h = h_ref[...]
        n, _ = _rms_stats(h)
        hn = (n * g_ref[...]).astype(MXU_DTYPE)
        hn_ref[...] = hn
        strips = [dict(cols=slice(lo, hi)) for lo, hi in FF_STRIPS]

        def project(t):
            t['gate'] = _dot_nt(hn, wgt_ref[t['cols'], :])
            t['up'] = _dot_nt(hn, wut_ref[t['cols'], :])

        def activate(t):
            gate, up = t['gate'], t['up']
            gate_ref[:, t['cols']] = gate.astype(MXU_DTYPE)
            up_ref[:, t['cols']] = up.astype(MXU_DTYPE)
            t['act'] = (gate * jax.nn.sigmoid(gate) * up).astype(MXU_DTYPE)

        def down(t):
            return _dot(t['act'], wd_ref[t['cols'], :])

        out = h
        project(strips[0])
        for i, t in enumerate(strips):
            if i + 1 < len(strips):
                project(strips[i + 1])
            activate(t)
            out = out + down(t)
        o_ref[...] = out

    return pl.pallas_call(
        body, name="ffn_fwd", grid=(s // tm,),
        in_specs=[_rows(tm, D_MODEL), _full((1, D_MODEL)), _full((D_FF, D_MODEL)), _full((D_FF, D_MODEL)),
                  _full((D_FF, D_MODEL))],
        out_specs=[_rows(tm, D_MODEL), _rows(tm, D_FF), _rows(tm, D_FF), _rows(tm, D_MODEL)],
        out_shape=[jax.ShapeDtypeStruct((s, D_MODEL), F32), jax.ShapeDtypeStruct((s, D_FF), MXU_DTYPE),
                   jax.ShapeDtypeStruct((s, D_FF), MXU_DTYPE), jax.ShapeDtypeStruct((s, D_MODEL), MXU_DTYPE)],
        compiler_params=_params(("arbitrary",)),
    )(h1, g_ffn, w_gate_t, w_up_t, w_down)


def _ple_loss(h2, p, target, g_ple, w_pg, w_pp_t, g_final):
    s = h2.shape[0]
    tm = 256

    def body(h_ref, p_ref, t_ref, gp_ref, wg_ref, wpt_ref, gf_ref,
             loss_ref, dh_ref, dz_ref, dpp_ref, hn_ref, dgp_ref, dgf_ref):
        @pl.when(pl.program_id(0) == 0)
        def _():
            loss_ref[...] = jnp.zeros_like(loss_ref)
            dgp_ref[...] = jnp.zeros_like(dgp_ref)
            dgf_ref[...] = jnp.zeros_like(dgf_ref)

        h2t = h_ref[...]
        n2, r2 = _rms_stats(h2t)
        hn = (n2 * gp_ref[...]).astype(MXU_DTYPE)
        hn_ref[...] = hn
        gate = jax.nn.sigmoid(_dot(hn, wg_ref[...]))
        pp = _dot_nt(p_ref[...].astype(MXU_DTYPE), wpt_ref[...])
        h3 = h2t + gate * pp
        n3, r3 = _rms_stats(h3)
        diff = n3 * gf_ref[...] - t_ref[...]
        loss_ref[...] += jnp.full(loss_ref.shape, 0.5 * jnp.sum(diff * diff) / D_MODEL, F32)
        dy = diff * (1.0 / D_MODEL)
        dgf_ref[...] += jnp.sum(dy * n3, axis=0, keepdims=True)
        dh3 = _rms_bwd(dy * gf_ref[...], n3, r3)
        dpp_ref[...] = (dh3 * gate).astype(MXU_DTYPE)
        dz = (dh3 * pp * gate * (1.0 - gate)).astype(MXU_DTYPE)
        dz_ref[...] = dz
        dhn = _dot_nt(dz, wg_ref[...])
        dgp_ref[...] += jnp.sum(dhn * n2, axis=0, keepdims=True)
        dh_ref[...] = dh3 + _rms_bwd(dhn * gp_ref[...], n2, r2)

    return pl.pallas_call(
        body, name="ple_loss", grid=(s // tm,),
        in_specs=[_rows(tm, D_MODEL), _rows(tm, PLE_DIM), _rows(tm, D_MODEL), _full((1, D_MODEL)),
                  _full((D_MODEL, D_MODEL)), _full((D_MODEL, PLE_DIM)), _full((1, D_MODEL))],
        out_specs=[_full((1, LANES)), _rows(tm, D_MODEL), _rows(tm, D_MODEL), _rows(tm, D_MODEL),
                   _rows(tm, D_MODEL), _full((1, D_MODEL)), _full((1, D_MODEL))],
        out_shape=[jax.ShapeDtypeStruct((1, LANES), F32), jax.ShapeDtypeStruct((s, D_MODEL), F32),
                   jax.ShapeDtypeStruct((s, D_MODEL), MXU_DTYPE), jax.ShapeDtypeStruct((s, D_MODEL), MXU_DTYPE),
                   jax.ShapeDtypeStruct((s, D_MODEL), MXU_DTYPE), jax.ShapeDtypeStruct((1, D_MODEL), F32),
                   jax.ShapeDtypeStruct((1, D_MODEL), F32)],
        compiler_params=_params(("arbitrary",)),
    )(h2, p, target, g_ple, w_pg, w_pp_t, g_final)


def _ffn_bwd(dh2, h1, gate, up, g_ffn, w_down, w_gate_t, w_up_t):
    s = h1.shape[0]
    tm = 256

    def body(dh_ref, h_ref, gate_ref, up_ref, g_ref, wd_ref, wgt_ref, wut_ref,
             o_ref, act_ref, dg_ref, du_ref, dgn_ref):
        @pl.when(pl.program_id(0) == 0)
        def _():
            dgn_ref[...] = jnp.zeros_like(dgn_ref)

        dh = dh_ref[...]
        dhb = dh.astype(MXU_DTYPE)
        strips = [dict(cols=slice(lo, hi)) for lo, hi in FF_STRIPS]

        def back_down(t):
            t['dact'] = _dot_nt(dhb, wd_ref[t['cols'], :])

        def back_act(t):
            cols, dact = t['cols'], t['dact']
            g = gate_ref[:, cols].astype(F32)
            u = up_ref[:, cols].astype(F32)
            sg = jax.nn.sigmoid(g)
            silu = g * sg
            act_ref[:, cols] = (silu * u).astype(MXU_DTYPE)
            t['du'] = (dact * silu).astype(MXU_DTYPE)
            t['dg'] = (dact * u * sg * (1.0 + g * (1.0 - sg))).astype(MXU_DTYPE)
            du_ref[:, cols] = t['du']
            dg_ref[:, cols] = t['dg']

        def back_in(t):
            return _dot(t['dg'], wgt_ref[t['cols'], :]) + _dot(t['du'], wut_ref[t['cols'], :])

        dhn = jnp.zeros((tm, D_MODEL), F32)
        back_down(strips[0])
        for i, t in enumerate(strips):
            if i + 1 < len(strips):
                back_down(strips[i + 1])
            back_act(t)
            dhn = dhn + back_in(t)
        n, r = _rms_stats(h_ref[...])
        dgn_ref[...] += jnp.sum(dhn * n, axis=0, keepdims=True)
        o_ref[...] = dh + _rms_bwd(dhn * g_ref[...], n, r)

    return pl.pallas_call(
        body, name="ffn_bwd", grid=(s // tm,),
        in_specs=[_rows(tm, D_MODEL), _rows(tm, D_MODEL), _rows(tm, D_FF), _rows(tm, D_FF), _full((1, D_MODEL)),
                  _full((D_FF, D_MODEL)), _full((D_FF, D_MODEL)), _full((D_FF, D_MODEL))],
        out_specs=[_rows(tm, D_MODEL), _rows(tm, D_FF), _rows(tm, D_FF), _rows(tm, D_FF), _full((1, D_MODEL))],
        out_shape=[jax.ShapeDtypeStruct((s, D_MODEL), F32), jax.ShapeDtypeStruct((s, D_FF), MXU_DTYPE),
                   jax.ShapeDtypeStruct((s, D_FF), MXU_DTYPE), jax.ShapeDtypeStruct((s, D_FF), MXU_DTYPE),
                   jax.ShapeDtypeStruct((1, D_MODEL), F32)],
        compiler_params=_params(("arbitrary",)),
    )(dh2, h1, gate, up, g_ffn, w_down, w_gate_t, w_up_t)


def _out_bwd(dh1, y_b, g_b, w_out):
    s = dh1.shape[0]
    tm = 512
    nc = WIDTH_B // LANES

    def body(dh_ref, *refs):
        yb_refs = refs[:nc]
        g_ref, w_ref, dya_ref, dyb_ref, dg_ref, scr = refs[nc:]

        @pl.when(pl.program_id(0) == 0)
        def _():
            dg_ref[...] = jnp.zeros_like(dg_ref)

        dy = _dot_nt(dh_ref[...].astype(MXU_DTYPE), w_ref[...])
        dya_ref[...] = dy[:, :WIDTH_A]
        dyb = dy[:, WIDTH_A:]
        n, r = _rms_stats(_load_l256(yb_refs, tm))
        dg_ref[...] += jnp.sum(dyb * n, axis=0, keepdims=True)
        dyb_in = _rms_bwd(dyb * g_ref[...], n, r)
        for j in range(nc):
            cols = slice(j * LANES, (j + 1) * LANES)
            _store_l256(scr, dyb_ref, cols, dyb_in[:, cols])

    return pl.pallas_call(
        body, name="out_bwd", grid=(s // tm,), scratch_shapes=[pltpu.VMEM((tm, LANES), F32)],
        in_specs=[_rows(tm, D_MODEL)] + _col_specs(tm, WIDTH_B) + [_full((1, WIDTH_B)), _full((D_MODEL, D_MODEL))],
        out_specs=[_rows(tm, WIDTH_A), _rows(tm, WIDTH_B), _full((1, WIDTH_B))],
        out_shape=[jax.ShapeDtypeStruct((s, WIDTH_A), F32), jax.ShapeDtypeStruct((s, WIDTH_B), F32),
                   jax.ShapeDtypeStruct((1, WIDTH_B), F32)],
        compiler_params=_params(("arbitrary",)),
    )(dh1, *([y_b] * nc), g_b, w_out)


def _attn_bwd_branch(q, k, v, do, o, lse, grads, dil):
    s = q.shape[0]
    br = _Branch(dil, s)
    qn, nb = br.qn, br.nb
    first = grads is None

    def body(*refs):
        bias_ref, q_ref, kc_ref, kp_ref, vc_ref, vp_ref, do_ref, o_ref, lse_ref = refs[:9]
        if first:
            rest = refs[9:]
        else:
            dq_in, dk_in, dv_in = refs[9:12]
            rest = refs[12:]
        dq_ref, dk_ref, dv_ref, dk_carry, dv_carry = rest
        n = pl.program_id(1)

        @pl.when(n == 0)
        def _():
            dk_carry[...] = jnp.zeros_like(dk_carry)
            dv_carry[...] = jnp.zeros_like(dv_carry)

        @pl.when(n < nb)
        def _():
            bias2 = jnp.concatenate([bias_ref[0], bias_ref[0]], axis=0)
            lane = lax.broadcasted_iota(jnp.int32, (qn, LANES), 1)
            lo = lane < HEAD_DIM
            mask_f = lo.astype(F32)
            mask_lo = mask_f.astype(MXU_DTYPE)
            def prepare(j, hp):
                cols = slice(hp * LANES, (hp + 1) * LANES)
                qp = br.load(q_ref, cols, j)
                dop = br.load(do_ref, cols, j)
                prod = dop * br.load(o_ref, cols, j)
                prod_lo = prod * mask_f
                lse = br.load(lse_ref, cols, j)
                return dict(
                    j=j, cols=cols,
                    kcat=jnp.concatenate([br.load(kp_ref, cols, j), br.load(kc_ref, cols, j)], axis=0),
                    vcat=jnp.concatenate([br.load(vp_ref, cols, j), br.load(vc_ref, cols, j)], axis=0),
                    qs=jnp.concatenate([qp * mask_lo, qp * (1 - mask_lo)], axis=0),
                    dos=jnp.concatenate([dop * mask_f, dop * (1.0 - mask_f)], axis=0).astype(MXU_DTYPE),
                    delta=jnp.concatenate([jnp.sum(prod_lo, axis=1, keepdims=True),
                                           jnp.sum(prod - prod_lo, axis=1, keepdims=True)], axis=0),
                    lse2=jnp.concatenate([lse[:, :1], lse[:, HEAD_DIM:HEAD_DIM + 1]], axis=0))

            def scores(t):
                t['sc'] = _dot_nt(t['qs'], t['kcat'])
                t['dp'] = _dot_nt(t['dos'], t['vcat'])

            def softmax(t):
                p = jnp.exp(t['sc'] + bias2 - t['lse2'])
                t['ds'] = (p * (t['dp'] - t['delta'])).astype(MXU_DTYPE)
                t['p'] = p.astype(MXU_DTYPE)

            def gradients(t):
                t['dvc'] = _dot_tn(t['p'], t['dos'])
                t['dkc'] = _dot_tn(t['ds'], t['qs'])
                t['dq2'] = _dot(t['ds'], t['kcat'])

            def store(t):
                j, cols, dkc, dvc = t['j'], t['cols'], t['dkc'], t['dvc']
                dq = jnp.where(lo, t['dq2'][:qn], t['dq2'][qn:])
                dk_prev = dk_carry[j, :, cols] + dkc[:qn]
                dv_prev = dv_carry[j, :, cols] + dvc[:qn]
                if not first:
                    dq = dq + br.load(dq_in, cols, j)
                    dk_prev = dk_prev + br.load(dk_in, cols, j)
                    dv_prev = dv_prev + br.load(dv_in, cols, j)
                br.store(dq_ref, cols, dq, j)
                br.store(dk_ref, cols, dk_prev, j)
                br.store(dv_ref, cols, dv_prev, j)
                dk_carry[j, :, cols] = dkc[qn:]
                dv_carry[j, :, cols] = dvc[qn:]

            for j, first_pair in itertools.product(range(br.res), range(0, HEADS_B // 2, PAIRS_ABREAST)):
                group = [prepare(j, hp) for hp in range(first_pair, first_pair + PAIRS_ABREAST)]
                for stage in (scores, softmax, gradients, store):
                    for t in group:
                        stage(t)

        @pl.when(n == nb)
        def _():
            for j in range(br.res):
                dk_last = dk_carry[j]
                dv_last = dv_carry[j]
                if not first:
                    dk_last = dk_last + br.load(dk_in, slice(None), j)
                    dv_last = dv_last + br.load(dv_in, slice(None), j)
                br.store(dk_ref, slice(None), dk_last, j)
                br.store(dv_ref, slice(None), dv_last, j)

    cur = lambda n: jnp.minimum(n, nb - 1)
    before = lambda n: jnp.maximum(cur(n) - 1, 0)
    late = lambda n: jnp.maximum(n - 1, 0)
    in_specs = [br.bias_spec(cur), br.spec(WIDTH_B, cur), br.spec(WIDTH_B, cur), br.spec(WIDTH_B, before),
                br.spec(WIDTH_B, cur), br.spec(WIDTH_B, before), br.spec(WIDTH_B, cur), br.spec(WIDTH_B, cur),
                br.spec(WIDTH_B, cur)]
    args = [jnp.asarray(br.bias)] + [br.view(a) for a in (q, k, k, v, v, do, o, lse)]
    if not first:
        in_specs += [br.spec(WIDTH_B, cur), br.spec(WIDTH_B, late), br.spec(WIDTH_B, late)]
        args += [br.view(g) for g in grads]
    res = pl.pallas_call(
        body, name="attn_bwd_d%d" % dil, grid=(br.grid[0], nb + 1), in_specs=in_specs,
        out_specs=[br.spec(WIDTH_B, cur), br.spec(WIDTH_B, late), br.spec(WIDTH_B, late)],
        out_shape=[jax.ShapeDtypeStruct((s // L_BLOCK, 4, 4, L_GROUP, WIDTH_B), F32)] * 3,
        scratch_shapes=[pltpu.VMEM((br.res, qn, WIDTH_B), F32), pltpu.VMEM((br.res, qn, WIDTH_B), F32)],
        compiler_params=_params(("arbitrary", "arbitrary")),
    )(*args)
    return tuple(a.reshape(s, WIDTH_B) for a in res)


def _sgu_bwd(uv, dya_n, w_tril, w_tril_t, bias, g_sgu, g_a):
    s = uv.shape[0]
    tm = 512

    def body(uv_ref, dy_ref, w_ref, wt_ref, b_ref, gs_ref, ga_ref, duv_ref, dw_ref, db_ref, dgs_ref, dga_ref,
             db_acc):
        i = pl.program_id(0)

        @pl.when(i == 0)
        def _():
            dw_ref[...] = jnp.zeros_like(dw_ref)
            dgs_ref[...] = jnp.zeros_like(dgs_ref)
            dga_ref[...] = jnp.zeros_like(dga_ref)
            db_acc[...] = jnp.zeros_like(db_acc)

        t = _sgu_forward_tile(uv_ref[...], w_ref, b_ref[...], gs_ref[...])
        na, ra = _rms_stats(t['ya'])
        dyn = dy_ref[...]
        dga_ref[...] += jnp.sum(dyn * na, axis=0, keepdims=True)
        dya = _rms_bwd(dyn * ga_ref[...], na, ra)
        dug = dya * t['mixed']
        dmixed = dya * t['ug']
        dmb = dmixed.astype(MXU_DTYPE)
        masks = _half_masks(MXU_DTYPE)
        chunks = []
        db = jnp.zeros((CHUNK, WIDTH_A), F32)
        for c in range(tm // CHUNK):
            rows = slice(c * CHUNK, (c + 1) * CHUNK)
            db = db + dmixed[rows]
            groups = []
            for gp in range(2):
                cols = slice(gp * LANES, (gp + 1) * LANES)
                dm_g = dmb[rows, cols]
                vn_g = t['vn'][rows, cols]
                dvn_g = jnp.zeros((CHUNK, LANES), F32)
                for j in range(2):
                    dm_h = dm_g * masks[j]
                    dvn_g = dvn_g + _dot(wt_ref[2 * gp + j], dm_h)
                    dw_ref[2 * gp + j] += _dot_nt(dm_h, vn_g)
                groups.append(dvn_g)
            chunks.append(jnp.concatenate(groups, axis=1))
        db_acc[...] += db
        dvn = jnp.concatenate(chunks, axis=0)
        vhat = t['vhat']
        dgs_ref[...] += jnp.sum(dvn * vhat, axis=0, keepdims=True)
        dvh = dvn * gs_ref[...]
        dvg = t['rs'] * (dvh - jnp.mean(dvh, axis=-1, keepdims=True)
                         - vhat * jnp.mean(dvh * vhat, axis=-1, keepdims=True))
        duv_ref[:, :WIDTH_A] = (dug * _gelu_grad(t['u'], t['tu'])).astype(MXU_DTYPE)
        duv_ref[:, WIDTH_A:] = (dvg * _gelu_grad(t['v'], t['tv'])).astype(MXU_DTYPE)

        @pl.when(i == pl.num_programs(0) - 1)
        def _():
            lane_a = lax.broadcasted_iota(jnp.int32, (CHUNK, WIDTH_A), 1)
            lane = lax.broadcasted_iota(jnp.int32, (CHUNK, LANES), 1)
            acc = db_acc[...]
            out = jnp.zeros((CHUNK, LANES), F32)
            for h in range(HEADS_A):
                col = jnp.sum(jnp.where(lane_a // HEAD_DIM == h, acc, 0.0), axis=1, keepdims=True)
                out = jnp.where(lane == h, col, out)
            db_ref[...] = out
            causal = (lax.broadcasted_iota(jnp.int32, (CHUNK, CHUNK), 0)
                      >= lax.broadcasted_iota(jnp.int32, (CHUNK, CHUNK), 1))
            for h in range(HEADS_A):
                dw_ref[h] = jnp.where(causal, dw_ref[h], 0.0)

    return pl.pallas_call(
        body, name="sgu_bwd", grid=(s // tm,),
        in_specs=[_rows(tm, 2 * WIDTH_A), _rows(tm, WIDTH_A), _full((HEADS_A, CHUNK, CHUNK)),
                  _full((HEADS_A, CHUNK, CHUNK)), _full((CHUNK, WIDTH_A)), _full((1, WIDTH_A)),
                  _full((1, WIDTH_A))],
        out_specs=[_rows(tm, 2 * WIDTH_A), _full((HEADS_A, CHUNK, CHUNK)), _full((CHUNK, LANES)),
                   _full((1, WIDTH_A)), _full((1, WIDTH_A))],
        out_shape=[jax.ShapeDtypeStruct((s, 2 * WIDTH_A), MXU_DTYPE),
                   jax.ShapeDtypeStruct((HEADS_A, CHUNK, CHUNK), F32), jax.ShapeDtypeStruct((CHUNK, LANES), F32),
                   jax.ShapeDtypeStruct((1, WIDTH_A), F32), jax.ShapeDtypeStruct((1, WIDTH_A), F32)],
        scratch_shapes=[pltpu.VMEM((CHUNK, WIDTH_A), F32)],
        compiler_params=_params(("arbitrary",)),
    )(uv, dya_n, w_tril, w_tril_t, bias, g_sgu, g_a)


def _in_bwd_proj(duv, dq, dk, dv, cos_t, sin_t):
    s = duv.shape[0]
    tm = 512
    nc = WIDTH_B // LANES

    def body(duv_ref, *refs):
        dq_refs, dk_refs, dv_refs = refs[:nc], refs[nc:2 * nc], refs[2 * nc:3 * nc]
        cos_ref, sin_ref, dp_ref = refs[3 * nc:]
        cos = cos_ref[...]
        sin = sin_ref[...]
        dp_ref[:, :2 * WIDTH_A] = duv_ref[...]
        for i in range(nc):
            lo = 2 * WIDTH_A + i * LANES
            tq = _load_l256(dq_refs[i:i + 1], tm) * (HEAD_DIM ** -0.5)
            tk = _load_l256(dk_refs[i:i + 1], tm)
            dp_ref[:, lo:lo + LANES] = (tq * cos + _rope_partner(tq * sin)).astype(MXU_DTYPE)
            dp_ref[:, lo + WIDTH_B:lo + WIDTH_B + LANES] = (tk * cos + _rope_partner(tk * sin)).astype(MXU_DTYPE)
            dp_ref[:, lo + 2 * WIDTH_B:lo + 2 * WIDTH_B + LANES] = _load_l256(dv_refs[i:i + 1], tm).astype(MXU_DTYPE)

    return pl.pallas_call(
        body, name="in_bwd_proj", grid=(s // tm,),
        in_specs=[_rows(tm, 2 * WIDTH_A)] + 3 * _col_specs(tm, WIDTH_B) + [_rows(tm, LANES), _rows(tm, LANES)],
        out_specs=_rows(tm, IN_COLS), out_shape=jax.ShapeDtypeStruct((s, IN_COLS), MXU_DTYPE),
        compiler_params=_params(("arbitrary",)),
    )(duv, *([dq] * nc), *([dk] * nc), *([dv] * nc), cos_t, sin_t)


def _in_bwd_x(dproj, w_in_t, x, g_mix, dh1, after):
    s = x.shape[0]
    tm = 512

    def body(dp_ref, wt_ref, x_ref, g_ref, dh_ref, *rest):
        gx_ref, dg_ref = rest[len(after):]
        @pl.when(pl.program_id(0) == 0)
        def _():
            dg_ref[...] = jnp.zeros_like(dg_ref)

        dhn = _dot(dp_ref[...], wt_ref[...])
        n, r = _rms_stats(x_ref[...])
        dg_ref[...] += jnp.sum(dhn * n, axis=0, keepdims=True)
        gx_ref[...] = dh_ref[...] + _rms_bwd(dhn * g_ref[...], n, r)

    return pl.pallas_call(
        body, name="in_bwd_x", grid=(s // tm,),
        in_specs=[_rows(tm, IN_COLS), _full((IN_COLS, D_MODEL)), _rows(tm, D_MODEL), _full((1, D_MODEL)),
                  _rows(tm, D_MODEL)] + [pl.BlockSpec(memory_space=pl.ANY)] * len(after),
        out_specs=[_rows(tm, D_MODEL), _full((1, D_MODEL))],
        out_shape=[jax.ShapeDtypeStruct((s, D_MODEL), F32), jax.ShapeDtypeStruct((1, D_MODEL), F32)],
        compiler_params=_params(("arbitrary",)),
    )(dproj, w_in_t, x, g_mix, dh1, *after)


def _wgrad(a, b, name):
    s, m = a.shape
    n = b.shape[1]
    bm = 512 if m % 512 == 0 else FF_HALF
    ts = 1024
    nsteps = s // ts

    def body(a_ref, b_ref, o_ref, acc):
        kk = pl.program_id(1)

        @pl.when(kk == 0)
        def _():
            acc[...] = jnp.zeros_like(acc)

        acc[...] += _dot_tn(a_ref[...].astype(MXU_DTYPE), b_ref[...].astype(MXU_DTYPE))

        @pl.when(kk == nsteps - 1)
        def _():
            o_ref[...] = acc[...].astype(o_ref.dtype)

    return pl.pallas_call(
        body, name=name, grid=(m // bm, nsteps),
        in_specs=[pl.BlockSpec((ts, bm), lambda i, kk: (kk, i)), pl.BlockSpec((ts, n), lambda i, kk: (kk, 0))],
        out_specs=pl.BlockSpec((bm, n), lambda i, kk: (i, 0)), out_shape=jax.ShapeDtypeStruct((m, n), jnp.bfloat16),
        scratch_shapes=[pltpu.VMEM((bm, n), F32)],
        compiler_params=_params(("arbitrary", "arbitrary")),
    )(a, b)


def _rope_tables(s):
    half = HEAD_DIM // 2
    inv = ROPE_THETA ** (-jnp.arange(half, dtype=F32) / half)
    ang = jnp.arange(s, dtype=F32)[:, None] * jnp.tile(inv, LANES // half)[None, :]
    sign = jnp.tile(jnp.concatenate([-jnp.ones(half, F32), jnp.ones(half, F32)]), LANES // HEAD_DIM)
    return jnp.cos(ang), jnp.sin(ang) * sign[None, :]


MESH = pl.DeviceIdType.MESH
ANY = pl.BlockSpec(memory_space=pl.ANY)
SEM = pl.BlockSpec(memory_space=pltpu.SEMAPHORE)
SPLIT_COPY = pltpu.CompilerParams(has_side_effects=pltpu.SideEffectType.DATAFLOW_SIDE_EFFECTING)
SLAB_IS_TRANSPOSED = {'w_in': True, 'w_out': False, 'w_gate': True, 'w_up': True, 'w_down': False,
                      'w_ple_gate': False, 'w_ple_proj': True}


def _place():
    x, y, c = lax.axis_index("x"), lax.axis_index("y"), lax.axis_index("c")
    other_chips = [(1 - x, y), (x, 1 - y), (1 - x, 1 - y)]
    return x, y, c, other_chips


def _chip_of(chip):
    return 2 * chip[0] + chip[1]


def _half(ref, lead, hc):
    hr = ref.shape[1] // 2
    return ref.at[lead, pl.ds(hc * hr, hr), :]


def _put_own(stack, own, index):
    return lax.dynamic_update_slice(stack, own[None], (index,) + (0,) * own.ndim)


def _all_gather_now(slab):
    rows, cols = slab.shape

    def body(x_ref, out_ref, send_sems, recv_sems):
        x, y, c, chips = _place()
        sibling = (x, y, 1 - c)
        hr = rows // 2

        def copy(k, src, dst, to):
            return pltpu.make_async_remote_copy(src_ref=src, dst_ref=dst, send_sem=send_sems.at[k],
                                                recv_sem=recv_sems.at[k], device_id=to, device_id_type=MESH)

        my_half = x_ref.at[pl.ds(c * hr, hr), :]
        first = [copy(j, my_half, _half(out_ref, 2 * x + y, c), (*chip, c)) for j, chip in enumerate(chips)]
        for cp in first:
            cp.start()
        passed = [copy(3 + j, _half(out_ref, _chip_of(chip), c), _half(out_ref, _chip_of(chip), c), sibling)
                  for j, chip in enumerate(chips)]
        for j, chip in enumerate(chips):
            copy(j, my_half, _half(out_ref, _chip_of(chip), c), (*chip, c)).wait_recv()
            passed[j].start()
        for j, chip in enumerate(chips):
            copy(3 + j, my_half, _half(out_ref, _chip_of(chip), 1 - c), sibling).wait_recv()
        for cp in first + passed:
            cp.wait_send()

    gathered = pl.pallas_call(
        body, name="all_gather_now", out_shape=jax.ShapeDtypeStruct((N_CHIPS, rows, cols), slab.dtype),
        in_specs=[ANY], out_specs=ANY,
        scratch_shapes=[pltpu.SemaphoreType.DMA((6,)), pltpu.SemaphoreType.DMA((6,))],
    )(slab)
    me = 2 * lax.axis_index("x") + lax.axis_index("y")
    return _put_own(gathered, slab, me).reshape(N_CHIPS * rows, cols)


def _gather_copies(slab_refs, land_refs, send_sems, recv_sems):
    x, y, c, chips = _place()
    sends, recvs = [], []
    for k, (src, land) in enumerate(zip(slab_refs, land_refs)):
        hr = src.shape[0] // 2
        for j, chip in enumerate(chips):
            for t in range(2):
                sends.append(pltpu.make_async_remote_copy(
                    src_ref=src.at[pl.ds(c * hr, hr), :], dst_ref=_half(land, 2 * x + y, c),
                    send_sem=send_sems.at[6 * k + 2 * j + t], recv_sem=recv_sems.at[6 * k + 2 * j + c],
                    device_id=(*chip, t), device_id_type=MESH))
                recvs.append(pltpu.make_async_remote_copy(
                    src_ref=src.at[pl.ds(t * hr, hr), :], dst_ref=_half(land, _chip_of(chip), t),
                    send_sem=send_sems.at[6 * k + 2 * j + t], recv_sem=recv_sems.at[6 * k + 2 * j + t],
                    device_id=(*chip, t), device_id_type=MESH))
    return sends, recvs


def _all_gather_start(slabs, after):
    n = len(slabs)

    def body(*refs):
        slab_refs, land_refs = refs[:n], refs[n:2 * n]
        send_sems, recv_sems = refs[2 * n + 1:2 * n + 3]
        token = refs[-1]
        sends, _ = _gather_copies(slab_refs, land_refs, send_sems, recv_sems)
        for cp in sends:
            cp.start()
        token[...] = jnp.zeros_like(token)

    lands = [lax.empty((N_CHIPS,) + s.shape, s.dtype) for s in slabs]
    hbm = lambda a: pltpu.HBM(a.shape, a.dtype)
    res = pl.pallas_call(
        body, name="all_gather_start",
        out_shape=(pltpu.SemaphoreType.DMA((6 * n,)), pltpu.SemaphoreType.DMA((6 * n,)), *map(hbm, slabs),
                   *map(hbm, lands), jax.ShapeDtypeStruct((8, LANES), F32)),
        in_specs=[ANY] * (2 * n + 1),
        out_specs=(SEM, SEM, *([ANY] * (2 * n)), pl.BlockSpec(memory_space=pltpu.VMEM)),
        input_output_aliases={i: 2 + i for i in range(2 * n)}, compiler_params=SPLIT_COPY,
    )(*[pltpu.with_memory_space_constraint(a, pltpu.HBM) for a in list(slabs) + lands], after)
    return res[:-1], res[-1]


def _all_gather_wait(handle, after):
    send_sems, recv_sems = handle[:2]
    n = (len(handle) - 2) // 2
    slabs, lands = handle[2:2 + n], handle[2 + n:]

    def body(*refs):
        slab_refs, land_refs = refs[:n], refs[n:2 * n]
        send_sems, recv_sems = refs[2 * n:2 * n + 2]
        sends, recvs = _gather_copies(slab_refs, land_refs, send_sems, recv_sems)
        for cp in sends:
            cp.wait_send()
        for cp in recvs:
            cp.wait_recv()

    hbm = lambda a: pltpu.HBM(a.shape, a.dtype)
    res = pl.pallas_call(
        body, name="all_gather_wait", out_shape=tuple(map(hbm, list(slabs) + list(lands))),
        in_specs=[ANY] * (2 * n) + [SEM, SEM, ANY], out_specs=tuple([ANY] * (2 * n)),
        input_output_aliases={i: i for i in range(2 * n)}, compiler_params=SPLIT_COPY,
    )(*slabs, *lands, send_sems, recv_sems, after)
    me = 2 * lax.axis_index("x") + lax.axis_index("y")
    return [_put_own(land, slab, me).reshape(N_CHIPS * slab.shape[0], slab.shape[1])
            for slab, land in zip(res[:n], res[n:])]


def _scatter_copies(part_refs, land_refs, send_sems, recv_sems):
    x, y, c, chips = _place()
    me = 4 * x + 2 * y + c
    sends, recvs = [], []
    for k, (part, land) in enumerate(zip(part_refs, land_refs)):
        for j, chip in enumerate(chips):
            for t in range(2):
                sends.append(pltpu.make_async_remote_copy(
                    src_ref=part.at[_chip_of(chip)], dst_ref=land.at[me],
                    send_sem=send_sems.at[7 * k + 2 * j + t], recv_sem=recv_sems.at[7 * k + 2 * j + c],
                    device_id=(*chip, t), device_id_type=MESH))
                recvs.append(pltpu.make_async_remote_copy(
                    src_ref=part.at[_chip_of(chip)], dst_ref=land.at[2 * _chip_of(chip) + t],
                    send_sem=send_sems.at[7 * k + 2 * j + t], recv_sem=recv_sems.at[7 * k + 2 * j + t],
                    device_id=(*chip, t), device_id_type=MESH))
        sends.append(pltpu.make_async_remote_copy(
            src_ref=part.at[2 * x + y], dst_ref=land.at[me], send_sem=send_sems.at[7 * k + 6],
            recv_sem=recv_sems.at[7 * k + 6], device_id=(x, y, 1 - c), device_id_type=MESH))
        recvs.append(pltpu.make_async_remote_copy(
            src_ref=part.at[2 * x + y], dst_ref=land.at[4 * x + 2 * y + 1 - c],
            send_sem=send_sems.at[7 * k + 6], recv_sem=recv_sems.at[7 * k + 6], device_id=(x, y, 1 - c),
            device_id_type=MESH))
    return sends, recvs


def _reduce_scatter_start(parts, name):
    n = len(parts)
    parts = [p.reshape(N_CHIPS, p.shape[0] // N_CHIPS, p.shape[1]) for p in parts]

    def body(*refs):
        part_refs, land_refs = refs[:n], refs[n:2 * n]
        send_sems, recv_sems = refs[2 * n:2 * n + 2]
        token = refs[-1]
        sends, _ = _scatter_copies(part_refs, land_refs, send_sems, recv_sems)
        for cp in sends:
            cp.start()
        token[...] = jnp.zeros_like(token)

    lands = [lax.empty((N_DEV, p.shape[1], p.shape[2]), p.dtype) for p in parts]
    hbm = lambda a: pltpu.HBM(a.shape, a.dtype)
    res = pl.pallas_call(
        body, name=name,
        out_shape=(pltpu.SemaphoreType.DMA((7 * n,)), pltpu.SemaphoreType.DMA((7 * n,)), *map(hbm, parts),
                   *map(hbm, lands), jax.ShapeDtypeStruct((8, LANES), F32)),
        in_specs=[ANY] * (2 * n), out_specs=(SEM, SEM, *([ANY] * (2 * n)), pl.BlockSpec(memory_space=pltpu.VMEM)),
        input_output_aliases={i: 2 + i for i in range(2 * n)}, compiler_params=SPLIT_COPY,
    )(*[pltpu.with_memory_space_constraint(a, pltpu.HBM) for a in parts + lands])
    return res[:-1], res[-1]


def _reduce_scatter_wait(handle, after, name):
    send_sems, recv_sems = handle[:2]
    n = (len(handle) - 2) // 2
    parts, lands = handle[2:2 + n], handle[2 + n:]

    def body(*refs):
        part_refs, land_refs = refs[:n], refs[n:2 * n]
        send_sems, recv_sems = refs[2 * n:2 * n + 2]
        sends, recvs = _scatter_copies(part_refs, land_refs, send_sems, recv_sems)
        for cp in sends:
            cp.wait_send()
        for cp in recvs:
            cp.wait_recv()

    hbm = lambda a: pltpu.HBM(a.shape, a.dtype)
    res = pl.pallas_call(
        body, name=name, out_shape=tuple(map(hbm, list(parts) + list(lands))),
        in_specs=[ANY] * (2 * n) + [SEM, SEM, ANY], out_specs=tuple([ANY] * (2 * n)),
        input_output_aliases={i: i for i in range(2 * n)}, compiler_params=SPLIT_COPY,
    )(*parts, *lands, send_sems, recv_sems, after)
    return list(zip(res[:n], res[n:]))


def _adamw_of_shares(w, own, land, m, v, name):
    rows, cols = w.shape
    tm = rows // 4 if rows % 32 == 0 else rows
    x, y, c = lax.axis_index("x"), lax.axis_index("y"), lax.axis_index("c")
    where = jnp.stack([2 * x + y, 4 * x + 2 * y + c]).astype(jnp.int32)

    def body(where_ref, w_ref, own_ref, land_ref, m_ref, v_ref, g_ref, d_ref, nm_ref, nv_ref):
        me = where_ref[1]
        g_ = jnp.zeros((tm, cols), F32)
        for dev in range(N_DEV):
            g_ = g_ + jnp.where(me == dev, own_ref[0], land_ref[dev]).astype(F32)
        m_ = ADAM_B1 * m_ref[...] + (1.0 - ADAM_B1) * g_
        v_ = ADAM_B2 * v_ref[...] + (1.0 - ADAM_B2) * (g_ * g_)
        m_hat = m_ / (1.0 - ADAM_B1 ** ADAM_STEP)
        v_hat = v_ / (1.0 - ADAM_B2 ** ADAM_STEP)
        g_ref[...] = g_
        d_ref[...] = -ADAM_LR * (m_hat / (jnp.sqrt(v_hat) + ADAM_EPS) + ADAM_WD * w_ref[...])
        nm_ref[...] = m_
        nv_ref[...] = v_

    tile = pl.BlockSpec((tm, cols), lambda i, where_ref: (i, 0))
    spec = pltpu.PrefetchScalarGridSpec(
        num_scalar_prefetch=1, grid=(rows // tm,),
        in_specs=[tile, pl.BlockSpec((1, tm, cols), lambda i, where_ref: (where_ref[0], i, 0)),
                  pl.BlockSpec((N_DEV, tm, cols), lambda i, where_ref: (0, i, 0)), tile, tile],
        out_specs=[tile] * 4)
    return pl.pallas_call(
        body, name=name, grid_spec=spec, out_shape=[jax.ShapeDtypeStruct(w.shape, F32)] * 4,
        compiler_params=_params(("arbitrary",)),
    )(where, w, own, land, m, v)


def _pack_small(values):
    flat = jnp.concatenate([values[n].reshape(-1).astype(F32) for n in SMALL])
    return jnp.pad(flat, (0, SMALL_ROWS * D_MODEL - flat.shape[0])).reshape(SMALL_ROWS, D_MODEL)


def _unpack_small(block, shapes):
    flat = block.reshape(-1)
    out, lo = {}, 0
    for n in SMALL:
        out[n] = flat[lo:lo + SMALL_SIZES[n]].reshape(shapes[n])
        lo += SMALL_SIZES[n]
    return out


def _after(token, a):
    return a + token[:1, :1].astype(a.dtype)


def kernel(x, p, mix_norm_g, w_in, sgu_w, sgu_b, sgu_norm_g, out_norm_a, out_norm_b, w_out, ffn_norm_g, w_gate, w_up, w_down, ple_norm_g, w_ple_gate, w_ple_proj, final_norm_g, loss_target, m_mix_norm_g, m_w_in, m_sgu_w, m_sgu_b, m_sgu_norm_g, m_out_norm_a, m_out_norm_b, m_w_out, m_ffn_norm_g, m_w_gate, m_w_up, m_w_down, m_ple_norm_g, m_w_ple_gate, m_w_ple_proj, m_final_norm_g, v_mix_norm_g, v_w_in, v_sgu_w, v_sgu_b, v_sgu_norm_g, v_out_norm_a, v_out_norm_b, v_w_out, v_ffn_norm_g, v_w_gate, v_w_up, v_w_down, v_ple_norm_g, v_w_ple_gate, v_w_ple_proj, v_final_norm_g):
    given = dict(locals())
    drop_lead = lambda a, lead: a.reshape(a.shape[lead:])
    xs, ps, target = drop_lead(x, 1), drop_lead(p, 2), drop_lead(loss_target, 1)
    s = xs.shape[0]
    shard = lambda name: drop_lead(given[name], 1)

    def slab_of(name):
        local = shard(name).astype(MXU_DTYPE)
        return local.T if SLAB_IS_TRANSPOSED[name] else local

    w_in_t = _all_gather_now(slab_of('w_in'))
    later = ['w_out', 'w_gate', 'w_up', 'w_down', 'w_ple_gate', 'w_ple_proj']
    gather, token = _all_gather_start([slab_of(n) for n in later], w_in_t)

    cos_t, sin_t = _rope_tables(s)
    tril = jnp.tril(jnp.ones((CHUNK, CHUNK), F32))
    w_tril = (sgu_w.reshape(HEADS_A, CHUNK, CHUNK) * tril).astype(MXU_DTYPE)
    w_tril_t = jnp.swapaxes(w_tril, 1, 2)
    bias = jnp.repeat(sgu_b.reshape(HEADS_A, CHUNK).T, HEAD_DIM, axis=1)
    g = {n: given[n].reshape(1, -1) for n in SMALL if n not in ('sgu_w', 'sgu_b')}

    uv, q, k, v, q1, k1, v1, hn1 = _in_fwd(xs, _after(token, g['mix_norm_g']), w_in_t, cos_t, sin_t)
    ya_n = _sgu_fwd(uv, w_tril, bias, g['sgu_norm_g'], g['out_norm_a'])
    branches = [_attn_fwd_local(q1, k1, v1)] + [_attn_fwd_branch(q, k, v, dil) for dil in DILATIONS[:-1]]
    y_b, lse = _attn_fwd_branch(q, k, v, DILATIONS[-1], earlier=branches)
    stacks = dict(zip(later, _all_gather_wait(gather, lse)))
    w_gate_t, w_up_t, w_pp_t = stacks['w_gate'], stacks['w_up'], stacks['w_ple_proj']
    h1, y_n = _out_fwd(ya_n, y_b, g['out_norm_b'], stacks['w_out'], xs)
    h2, gate, up, hn2 = _ffn_fwd(h1, g['ffn_norm_g'], w_gate_t, w_up_t, stacks['w_down'])
    loss, dh2, dz, dpp, hn3, d_ple_g, d_final_g = _ple_loss(
        h2, ps, target, g['ple_norm_g'], stacks['w_ple_gate'], w_pp_t, g['final_norm_g'])

    share = {}
    share['w_ple_gate'] = _wgrad(hn3, dz, "wgrad_ple_gate")
    share['w_ple_proj'] = _wgrad(dpp, ps, "wgrad_ple_proj")
    scatter_1, token = _reduce_scatter_start([share['w_ple_gate'], share['w_ple_proj']], "reduce_scatter_start_1")
    dh1, act, dgate, dup, d_ffn_g = _ffn_bwd(dh2, h1, gate, up, _after(token, g['ffn_norm_g']), stacks['w_down'],
                                             w_gate_t, w_up_t)
    share['w_down'] = _wgrad(act, dh2, "wgrad_down")
    share['w_gate'] = _wgrad(dgate, hn2, "wgrad_gate")
    share['w_up'] = _wgrad(dup, hn2, "wgrad_up")
    scatter_2, token = _reduce_scatter_start([share['w_down'], share['w_gate'], share['w_up']],
                                             "reduce_scatter_start_2")
    dya_n, dyb, d_out_b = _out_bwd(dh1, y_b, _after(token, g['out_norm_b']), stacks['w_out'])
    share['w_out'] = _wgrad(y_n, dh1, "wgrad_out")
    scatter_3, token = _reduce_scatter_start([share['w_out']], "reduce_scatter_start_3")
    grads = _attn_bwd_local(q1, k1, v1, dyb, y_b, lse)
    for dil in DILATIONS:
        grads = _attn_bwd_branch(q, k, v, dyb, y_b, lse, grads, dil)
    duv, d_sgu_w, d_sgu_b, d_sgu_g, d_out_a = _sgu_bwd(uv, dya_n, w_tril, w_tril_t, bias,
                                                       _after(token, g['sgu_norm_g']), g['out_norm_a'])
    dproj = _in_bwd_proj(duv, grads[0], grads[1], grads[2], cos_t, sin_t)
    share['w_in'] = _wgrad(dproj, hn1, "wgrad_in")
    scatter_4, token = _reduce_scatter_start([share['w_in']], "reduce_scatter_start_4")

    grads, deltas, new_m, new_v = {}, {}, {}, {}
    add_lead = lambda a: a.reshape((1,) + a.shape)

    def finish(names, handles, after, tag):
        landed = []
        for i, handle in enumerate(handles):
            landed += _reduce_scatter_wait(handle, after, "reduce_scatter_wait_%s%d" % (tag, i))
        for n, (own, land) in zip(names, landed):
            turn = (lambda a: a.T) if SLAB_IS_TRANSPOSED[n] else (lambda a: a)
            res = _adamw_of_shares(turn(shard(n)), own, land, turn(shard("m_" + n)), turn(shard("v_" + n)),
                                   "adamw_" + n)
            grads[n], deltas[n], new_m[n], new_v[n] = (add_lead(turn(a)) for a in res)

    finish(['w_ple_gate', 'w_ple_proj', 'w_down', 'w_gate', 'w_up', 'w_out'], [scatter_1, scatter_2, scatter_3], token,
           "early")
    grad_x, d_mix_g = _in_bwd_x(dproj, w_in_t, xs, g['mix_norm_g'], dh1,
                                after=[new_v[n] for n in ('w_down', 'w_gate', 'w_up', 'w_out')])

    gs = {'mix_norm_g': d_mix_g, 'sgu_w': d_sgu_w, 'sgu_b': d_sgu_b[:, :HEADS_A].T, 'sgu_norm_g': d_sgu_g,
          'out_norm_a': d_out_a, 'out_norm_b': d_out_b, 'ffn_norm_g': d_ffn_g, 'ple_norm_g': d_ple_g,
          'final_norm_g': d_final_g}
    gs_block = _pack_small(gs).at[SMALL_ROWS - 1, 0].set(loss[0, 0])
    to_all = jnp.broadcast_to(gs_block[None], (N_CHIPS,) + gs_block.shape).reshape(-1, D_MODEL)
    scatter_small, token = _reduce_scatter_start([to_all], "small_all_reduce_start")
    finish(['w_in'], [scatter_4], token, "last")
    (own, land), = _reduce_scatter_wait(scatter_small, new_v['w_in'], "small_all_reduce_wait")
    small = {n: given[n] for n in SMALL}
    small_res = _adamw_of_shares(_pack_small(small), own, land, _pack_small({n: given["m_" + n] for n in SMALL}),
                                 _pack_small({n: given["v_" + n] for n in SMALL}), "adamw_small")
    loss_out = small_res[0][SMALL_ROWS - 1, 0]
    small_shapes = {n: given[n].shape for n in SMALL}
    for res, blk in zip((grads, deltas, new_m, new_v), small_res):
        res.update(_unpack_small(blk, small_shapes))

    outs = [loss_out, add_lead(grad_x)]
    for res in (grads, deltas, new_m, new_v):
        outs += [res[n] for n in WEIGHT_NAMES]
    return tuple(outs)
```

```python
import functools
import itertools
import math

import jax
import jax.numpy as jnp
import numpy as np
from jax import lax
from jax.experimental import pallas as pl
from jax.experimental.pallas import tpu as pltpu

F32 = jnp.float32
MXU_DTYPE = jnp.bfloat16

D_MODEL = 1024
HEAD_DIM = 64
HEADS_A = 4
HEADS_B = 12
WIDTH_A = HEADS_A * HEAD_DIM
WIDTH_B = HEADS_B * HEAD_DIM
CHUNK = 128
BLOCK = 128
DILATIONS = (4, 16)
ROPE_THETA = 10000.0
D_FF = 2816
FF_HALF = D_FF // 2
FF_STRIPS = ((0, 1024), (1024, 2048), (2048, D_FF))
PLE_DIM = 256
IN_COLS = 2 * WIDTH_A + 3 * WIDTH_B
EPS = 1e-6
LANES = 128
N_CHIPS = 4
N_DEV = 8

ADAM_LR = 0.001
ADAM_B1 = 0.9
ADAM_B2 = 0.999
ADAM_EPS = 1e-08
ADAM_WD = 0.01
ADAM_STEP = 10

VMEM_LIMIT = 56 * 1024 * 1024

WEIGHT_NAMES = ['mix_norm_g', 'w_in', 'sgu_w', 'sgu_b', 'sgu_norm_g', 'out_norm_a', 'out_norm_b', 'w_out',
                'ffn_norm_g', 'w_gate', 'w_up', 'w_down', 'ple_norm_g', 'w_ple_gate', 'w_ple_proj', 'final_norm_g']
SMALL = ['mix_norm_g', 'sgu_w', 'sgu_b', 'sgu_norm_g', 'out_norm_a', 'out_norm_b', 'ffn_norm_g', 'ple_norm_g',
         'final_norm_g']
SMALL_SIZES = {'mix_norm_g': 1024, 'sgu_w': 65536, 'sgu_b': 512, 'sgu_norm_g': 256, 'out_norm_a': 256,
               'out_norm_b': 768, 'ffn_norm_g': 1024, 'ple_norm_g': 1024, 'final_norm_g': 1024}
SMALL_ROWS = 72


def _params(semantics=None):
    return pltpu.CompilerParams(dimension_semantics=semantics, vmem_limit_bytes=VMEM_LIMIT)


def _full(shape):
    nd = len(shape)
    return pl.BlockSpec(shape, lambda i: (0,) * nd, pipeline_mode=pl.Buffered(1))


def _rows(tm, width):
    return pl.BlockSpec((tm, width), lambda i: (i, 0))


def _rms_stats(x):
    r = lax.rsqrt(jnp.mean(x * x, axis=-1, keepdims=True) + EPS)
    return x * r, r


def _rms_bwd(dn, n, r):
    return r * (dn - n * jnp.mean(dn * n, axis=-1, keepdims=True))


def _dot(a, b):
    return jnp.dot(a, b, preferred_element_type=F32)


def _dot_nt(a, b):
    return lax.dot_general(a, b, (((1,), (1,)), ((), ())), preferred_element_type=F32)


def _dot_tn(a, b):
    return lax.dot_general(a, b, (((0,), (0,)), ((), ())), preferred_element_type=F32)


def _gelu_parts(x):
    c = math.sqrt(2.0 / math.pi)
    t = jnp.tanh(c * (x + 0.044715 * x * x * x))
    return 0.5 * x * (1.0 + t), t


def _gelu_grad(x, t):
    c = math.sqrt(2.0 / math.pi)
    return 0.5 * (1.0 + t) + 0.5 * x * (1.0 - t * t) * c * (1.0 + 3.0 * 0.044715 * x * x)


def _half_masks(dtype):
    lane = lax.broadcasted_iota(jnp.int32, (BLOCK, LANES), 1)
    lo = (lane < HEAD_DIM).astype(F32)
    return lo.astype(dtype), (1.0 - lo).astype(dtype)


def _rope_partner(t):
    lane = lax.broadcasted_iota(jnp.int32, t.shape, 1)
    first_half = (lane % HEAD_DIM) < (HEAD_DIM // 2)
    return jnp.where(first_half, pltpu.roll(t, LANES - HEAD_DIM // 2, 1), pltpu.roll(t, HEAD_DIM // 2, 1))


PAIRS_ABREAST = 2
RESIDUES_PER_STEP = 4
L_BLOCK = 256
L_GROUP = 16


def _store_l256(scr, out_ref, cols, value, chunk_ref=None):
    tm = value.shape[0]
    half = L_GROUP // 2
    scr[...] = value
    for blk in range(tm // L_BLOCK):
        pieces = [scr[pl.ds(blk * L_BLOCK + r, L_GROUP, stride=L_GROUP), :] for r in range(L_GROUP)]
        for r, piece in enumerate(pieces):
            lo = blk * L_BLOCK + r * L_GROUP
            out_ref[lo:lo + L_GROUP, cols] = piece.astype(out_ref.dtype)
        if chunk_ref is not None:
            for chunk in range(L_BLOCK // BLOCK):
                for r in range(0, L_GROUP, 2):
                    lo = blk * L_BLOCK + chunk * BLOCK + r * half
                    both = [p[chunk * half:(chunk + 1) * half] for p in pieces[r:r + 2]]
                    chunk_ref[lo:lo + L_GROUP, cols] = jnp.concatenate(both, axis=0).astype(chunk_ref.dtype)


def _load_l256(col_refs, tm):
    cols = []
    for ref in col_refs:
        pieces = [ref[pl.ds(blk * L_BLOCK + i, L_GROUP, stride=L_GROUP), :]
                  for blk in range(tm // L_BLOCK) for i in range(L_GROUP)]
        cols.append(jnp.concatenate(pieces, axis=0))
    return jnp.concatenate(cols, axis=1)


def _col_specs(tm, width):
    return [pl.BlockSpec((tm, LANES), lambda i, j=j: (i, j)) for j in range(width // LANES)]


def _in_fwd(x, g_mix, w_in_t, cos_t, sin_t):
    s = x.shape[0]
    tm = 512

    def body(x_ref, g_ref, wt_ref, cos_ref, sin_ref, uv_ref, q_ref, k_ref, v_ref, q1_ref, k1_ref, v1_ref, hn_ref,
             *scrs):
        n, _ = _rms_stats(x_ref[...])
        hn = (n * g_ref[...]).astype(MXU_DTYPE)
        hn_ref[...] = hn
        cos = cos_ref[...]
        sin = sin_ref[...]
        strip = 2 * LANES
        for j in range(IN_COLS // strip):
            proj = _dot_nt(hn, wt_ref[j * strip:(j + 1) * strip, :])
            lo = j * strip - 2 * WIDTH_A
            if lo < 0:
                uv_ref[:, j * strip:(j + 1) * strip] = proj
                continue
            which, lo = divmod(lo, WIDTH_B)
            for i in range(strip // LANES):
                t = proj[:, i * LANES:(i + 1) * LANES]
                cols = slice(lo + i * LANES, lo + (i + 1) * LANES)
                scr = scrs[i]
                if which == 0:
                    _store_l256(scr, q_ref, cols, (t * cos + _rope_partner(t) * sin) * (HEAD_DIM ** -0.5), q1_ref)
                elif which == 1:
                    _store_l256(scr, k_ref, cols, t * cos + _rope_partner(t) * sin, k1_ref)
                else:
                    _store_l256(scr, v_ref, cols, t, v1_ref)

    return pl.pallas_call(
        body, name="in_fwd", grid=(s // tm,), scratch_shapes=[pltpu.VMEM((tm, LANES), F32)] * 2,
        in_specs=[_rows(tm, D_MODEL), _full((1, D_MODEL)), _full((IN_COLS, D_MODEL)), _rows(tm, LANES),
                  _rows(tm, LANES)],
        out_specs=[_rows(tm, 2 * WIDTH_A)] + [_rows(tm, WIDTH_B)] * 6 + [_rows(tm, D_MODEL)],
        out_shape=[jax.ShapeDtypeStruct((s, 2 * WIDTH_A), F32)] + [jax.ShapeDtypeStruct((s, WIDTH_B), MXU_DTYPE)] * 6
        + [jax.ShapeDtypeStruct((s, D_MODEL), MXU_DTYPE)],
        compiler_params=_params(("arbitrary",)),
    )(x, g_mix, w_in_t, cos_t, sin_t)


class _Branch:
    def __init__(self, dil, s):
        i = np.arange(L_GROUP)
        self.res = RESIDUES_PER_STEP
        if dil == 16:
            nblk = BLOCK // 16
            self.grid = (16 // self.res, s // (L_BLOCK * nblk))
            self.shape = (nblk, 1, self.res, L_GROUP)
            self.index = lambda r, n: (n, r // (4 // self.res), r % (4 // self.res), 0, 0)
            pos = (np.arange(nblk)[:, None] * 16 + i[None, :]).reshape(-1)
        else:
            nblk = BLOCK // 64
            self.grid = (4 // self.res, s // (L_BLOCK * nblk))
            self.shape = (nblk, 4, self.res, L_GROUP)
            self.index = lambda r, n: (n, 0, r, 0, 0)
            pos = (np.arange(nblk)[:, None, None] * 64 + np.arange(4)[None, :, None]
                   + 4 * i[None, None, :]).reshape(-1)
        self.qn = pos.shape[0]
        self.nb = self.grid[1]
        dist = pos[:, None] - np.concatenate([pos - self.qn, pos])[None, :]
        band = (dist >= 0) & (dist <= BLOCK)
        start = band & (np.arange(2 * self.qn)[None, :] >= self.qn)
        self.bias = np.where(np.stack([band, start]), 0.0, -np.inf).astype(np.float32)

    def view(self, a):
        return a.reshape(a.shape[0] // L_BLOCK, 4, 4, L_GROUP, a.shape[1])

    def spec(self, w, step=lambda n: n):
        return pl.BlockSpec(self.shape + (w,), lambda r, n: self.index(r, step(n)))

    def bias_spec(self, step=lambda n: n):
        return pl.BlockSpec((1, self.qn, 2 * self.qn), lambda r, n: (jnp.where(step(n) == 0, 1, 0), 0, 0))

    def load(self, ref, cols=slice(None), j=0):
        x = ref[:, :, j, :, cols]
        return x.reshape(self.qn, x.shape[-1])

    def store(self, ref, cols, value, j=0):
        ref[:, :, j, :, cols] = value.reshape(self.shape[:2] + (L_GROUP, value.shape[-1]))


def _attn_fwd_branch(q, k, v, dil, earlier=()):
    s = q.shape[0]
    br = _Branch(dil, s)
    qn = br.qn
    nearly = len(earlier)

    def body(bias_ref, q_ref, kc_ref, kp_ref, vc_ref, vp_ref, *refs):
        early_refs, (o_ref, lse_ref) = refs[:2 * nearly], refs[2 * nearly:]
        bias2 = jnp.concatenate([bias_ref[0], bias_ref[0]], axis=0)
        lo = lax.broadcasted_iota(jnp.int32, (qn, LANES), 1) < HEAD_DIM
        mask_lo = lo.astype(F32).astype(MXU_DTYPE)
        for j, hp in itertools.product(range(br.res), range(HEADS_B // 2)):
            cols = slice(hp * LANES, (hp + 1) * LANES)
            qp = br.load(q_ref, cols, j)
            kcat = jnp.concatenate([br.load(kp_ref, cols, j), br.load(kc_ref, cols, j)], axis=0)
            vcat = jnp.concatenate([br.load(vp_ref, cols, j), br.load(vc_ref, cols, j)], axis=0)
            sc = _dot_nt(jnp.concatenate([qp * mask_lo, qp * (1 - mask_lo)], axis=0), kcat) + bias2
            m = jnp.max(sc, axis=1, keepdims=True)
            p = jnp.exp(sc - m)
            l = jnp.sum(p, axis=1, keepdims=True)
            out = _dot(p.astype(MXU_DTYPE), vcat) / l
            lse = m + jnp.log(l)
            outs = [br.load(r, cols, j) for r in early_refs[:nearly]] + [jnp.where(lo, out[:qn], out[qn:])]
            lses = [br.load(r, cols, j) for r in early_refs[nearly:]] + [jnp.where(lo, lse[:qn], lse[qn:])]
            if nearly:
                top = functools.reduce(jnp.maximum, lses)
                ws = [jnp.exp(x - top) for x in lses]
                den = functools.reduce(jnp.add, ws)
                outs = [functools.reduce(jnp.add, [w * o for w, o in zip(ws, outs)]) / den]
                lses = [top + jnp.log(den)]
            br.store(o_ref, cols, outs[0], j)
            br.store(lse_ref, cols, lses[0], j)

    before = lambda n: jnp.maximum(n - 1, 0)
    res = pl.pallas_call(
        body, name="attn_fwd_d%d" % dil, grid=br.grid,
        in_specs=[br.bias_spec(), br.spec(WIDTH_B), br.spec(WIDTH_B), br.spec(WIDTH_B, before), br.spec(WIDTH_B),
                  br.spec(WIDTH_B, before)] + [br.spec(WIDTH_B)] * (2 * nearly),
        out_specs=[br.spec(WIDTH_B), br.spec(WIDTH_B)],
        out_shape=[jax.ShapeDtypeStruct((s // L_BLOCK, 4, 4, L_GROUP, WIDTH_B), F32)] * 2,
        compiler_params=_params(("arbitrary", "arbitrary")),
    )(jnp.asarray(br.bias), br.view(q), br.view(k), br.view(k), br.view(v), br.view(v),
      *[br.view(o) for o, _ in earlier], *[br.view(x) for _, x in earlier])
    return tuple(a.reshape(s, WIDTH_B) for a in res)


LOCAL_CHUNKS = 4


def _local_bias():
    row = np.arange(BLOCK)
    pos = L_GROUP * (row % (L_GROUP // 2)) + row // (L_GROUP // 2)
    dist = pos[:, None] - np.concatenate([pos - BLOCK, pos])[None, :]
    band = (dist >= 0) & (dist <= BLOCK)
    start = band & (np.arange(2 * BLOCK)[None, :] >= BLOCK)
    return np.where(np.stack([band, start]), 0.0, -np.inf).astype(np.float32)


def _chunk_view(a):
    return a.reshape(a.shape[0] // L_BLOCK, L_GROUP, 2, L_GROUP // 2, a.shape[1])


def _chunk_of(ref, j, cols=slice(None)):
    x = ref[j // 2, :, j % 2, :, cols]
    return x.reshape(BLOCK, x.shape[-1])


def _put_chunk(ref, j, cols, value):
    ref[j // 2, :, j % 2, :, cols] = value.reshape(L_GROUP, L_GROUP // 2, value.shape[-1])


def _local_keys(cur_ref, before_ref, j, cols):
    here = slice(j * BLOCK, (j + 1) * BLOCK)
    before = before_ref[:, cols] if j == 0 else cur_ref[(j - 1) * BLOCK:j * BLOCK, cols]
    return jnp.concatenate([before, cur_ref[here, cols]], axis=0)


def _local_specs(s, step=lambda n: n):
    rows = LOCAL_CHUNKS * BLOCK
    cur = pl.BlockSpec((rows, WIDTH_B), lambda n: (step(n), 0))
    before = pl.BlockSpec((BLOCK, WIDTH_B), lambda n: (jnp.maximum(LOCAL_CHUNKS * step(n) - 1, 0), 0))
    return [cur, cur, before, cur, before]


def _attn_fwd_local(q1, k1, v1):
    s = q1.shape[0]
    rows = LOCAL_CHUNKS * BLOCK
    qn = BLOCK

    def body(bias_ref, q_ref, kc_ref, kp_ref, vc_ref, vp_ref, o_ref, lse_ref):
        first = jnp.where(pl.program_id(0) == 0, bias_ref[1], bias_ref[0])
        biases = [jnp.concatenate([b, b], axis=0) for b in (first, bias_ref[0])]
        lo = lax.broadcasted_iota(jnp.int32, (qn, LANES), 1) < HEAD_DIM
        mask_lo = lo.astype(F32).astype(MXU_DTYPE)
        for j, hp in itertools.product(range(LOCAL_CHUNKS), range(HEADS_B // 2)):
            cols = slice(hp * LANES, (hp + 1) * LANES)
            qp = q_ref[j * BLOCK:(j + 1) * BLOCK, cols]
            kcat = _local_keys(kc_ref, kp_ref, j, cols)
            vcat = _local_keys(vc_ref, vp_ref, j, cols)
            sc = _dot_nt(jnp.concatenate([qp * mask_lo, qp * (1 - mask_lo)], axis=0), kcat) + biases[min(j, 1)]
            m = jnp.max(sc, axis=1, keepdims=True)
            p = jnp.exp(sc - m)
            l = jnp.sum(p, axis=1, keepdims=True)
            out = _dot(p.astype(MXU_DTYPE), vcat) / l
            lse = m + jnp.log(l)
            _put_chunk(o_ref, j, cols, jnp.where(lo, out[:qn], out[qn:]))
            _put_chunk(lse_ref, j, cols, jnp.where(lo, lse[:qn], lse[qn:]))

    out_spec = pl.BlockSpec((LOCAL_CHUNKS // 2, L_GROUP, 2, L_GROUP // 2, WIDTH_B), lambda n: (n, 0, 0, 0, 0))
    res = pl.pallas_call(
        body, name="attn_fwd_d1", grid=(s // rows,),
        in_specs=[_full((2, BLOCK, 2 * BLOCK))] + _local_specs(s), out_specs=[out_spec] * 2,
        out_shape=[jax.ShapeDtypeStruct((s // L_BLOCK, L_GROUP, 2, L_GROUP // 2, WIDTH_B), F32)] * 2,
        compiler_params=_params(("arbitrary",)),
    )(jnp.asarray(_local_bias()), q1, k1, k1, v1, v1)
    return tuple(a.reshape(s, WIDTH_B) for a in res)


def _attn_bwd_local(q1, k1, v1, do, o, lse):
    s = q1.shape[0]
    rows = LOCAL_CHUNKS * BLOCK
    nsteps = s // rows
    qn = BLOCK

    def body(bias_ref, q_ref, kc_ref, kp_ref, vc_ref, vp_ref, do_ref, o_ref, lse_ref, dq_ref, dk_ref, dv_ref,
             dk_buf, dv_buf):
        n = pl.program_id(0)

        @pl.when(n == 0)
        def _():
            dk_buf[...] = jnp.zeros_like(dk_buf)
            dv_buf[...] = jnp.zeros_like(dv_buf)

        @pl.when(n < nsteps)
        def _():
            first = jnp.where(n == 0, bias_ref[1], bias_ref[0])
            biases = [jnp.concatenate([b, b], axis=0) for b in (first, bias_ref[0])]
            lo = lax.broadcasted_iota(jnp.int32, (qn, LANES), 1) < HEAD_DIM
            mask_f = lo.astype(F32)
            mask_lo = mask_f.astype(MXU_DTYPE)
            dk_buf[LOCAL_CHUNKS:] = jnp.zeros((LOCAL_CHUNKS, qn, WIDTH_B), F32)
            dv_buf[LOCAL_CHUNKS:] = jnp.zeros((LOCAL_CHUNKS, qn, WIDTH_B), F32)

            def prepare(j, hp):
                cols = slice(hp * LANES, (hp + 1) * LANES)
                qp = q_ref[j * BLOCK:(j + 1) * BLOCK, cols]
                dop = _chunk_of(do_ref, j, cols)
                prod = dop * _chunk_of(o_ref, j, cols)
                prod_lo = prod * mask_f
                lse = _chunk_of(lse_ref, j, cols)
                return dict(
                    j=j, cols=cols, kcat=_local_keys(kc_ref, kp_ref, j, cols), vcat=_local_keys(vc_ref, vp_ref, j, cols),
                    qs=jnp.concatenate([qp * mask_lo, qp * (1 - mask_lo)], axis=0),
                    dos=jnp.concatenate([dop * mask_f, dop * (1.0 - mask_f)], axis=0).astype(MXU_DTYPE),
                    delta=jnp.concatenate([jnp.sum(prod_lo, axis=1, keepdims=True),
                                           jnp.sum(prod - prod_lo, axis=1, keepdims=True)], axis=0),
                    lse2=jnp.concatenate([lse[:, :1], lse[:, HEAD_DIM:HEAD_DIM + 1]], axis=0))

            def scores(t):
                t['sc'] = _dot_nt(t['qs'], t['kcat'])
                t['dp'] = _dot_nt(t['dos'], t['vcat'])

            def softmax(t):
                p = jnp.exp(t['sc'] + biases[min(t['j'], 1)] - t['lse2'])
                t['ds'] = (p * (t['dp'] - t['delta'])).astype(MXU_DTYPE)
                t['p'] = p.astype(MXU_DTYPE)

            def gradients(t):
                t['dvc'] = _dot_tn(t['p'], t['dos'])
                t['dkc'] = _dot_tn(t['ds'], t['qs'])
                t['dq2'] = _dot(t['ds'], t['kcat'])

            def store(t):
                j, cols = t['j'], t['cols']
                _put_chunk(dq_ref, j, cols, jnp.where(lo, t['dq2'][:qn], t['dq2'][qn:]))
                for buf, both in ((dk_buf, t['dkc']), (dv_buf, t['dvc'])):
                    buf[LOCAL_CHUNKS + j - 1, :, cols] += both[:qn]
                    buf[LOCAL_CHUNKS + j, :, cols] += both[qn:]

            for j, first_pair in itertools.product(range(LOCAL_CHUNKS), range(0, HEADS_B // 2, PAIRS_ABREAST)):
                group = [prepare(j, hp) for hp in range(first_pair, first_pair + PAIRS_ABREAST)]
                for stage in (scores, softmax, gradients, store):
                    for t in group:
                        stage(t)

        for j in range(LOCAL_CHUNKS):
            _put_chunk(dk_ref, j, slice(None), dk_buf[j])
            _put_chunk(dv_ref, j, slice(None), dv_buf[j])
        dk_buf[:LOCAL_CHUNKS] = dk_buf[LOCAL_CHUNKS:]
        dv_buf[:LOCAL_CHUNKS] = dv_buf[LOCAL_CHUNKS:]

    cur = lambda n: jnp.minimum(n, nsteps - 1)
    late = lambda n: jnp.maximum(n - 1, 0)
    view_spec = lambda step: pl.BlockSpec((LOCAL_CHUNKS // 2, L_GROUP, 2, L_GROUP // 2, WIDTH_B),
                                          lambda n: (step(n), 0, 0, 0, 0))
    res = pl.pallas_call(
        body, name="attn_bwd_d1", grid=(nsteps + 1,),
        in_specs=[_full((2, BLOCK, 2 * BLOCK))] + _local_specs(s, cur) + [view_spec(cur)] * 3,
        out_specs=[view_spec(cur), view_spec(late), view_spec(late)],
        out_shape=[jax.ShapeDtypeStruct((s // L_BLOCK, L_GROUP, 2, L_GROUP // 2, WIDTH_B), F32)] * 3,
        scratch_shapes=[pltpu.VMEM((2 * LOCAL_CHUNKS, qn, WIDTH_B), F32)] * 2,
        compiler_params=_params(("arbitrary",)),
    )(jnp.asarray(_local_bias()), q1, k1, k1, v1, v1, _chunk_view(do), _chunk_view(o), _chunk_view(lse))
    return tuple(a.reshape(s, WIDTH_B) for a in res)


def _sgu_forward_tile(uv, w_ref, bias, g_sgu):
    tm = uv.shape[0]
    u = uv[:, :WIDTH_A]
    v = uv[:, WIDTH_A:]
    ug, tu = _gelu_parts(u)
    vg, tv = _gelu_parts(v)
    mu = jnp.mean(vg, axis=-1, keepdims=True)
    vc = vg - mu
    rs = lax.rsqrt(jnp.mean(vc * vc, axis=-1, keepdims=True) + EPS)
    vhat = vc * rs
    vn = (vhat * g_sgu).astype(MXU_DTYPE)
    masks = _half_masks(MXU_DTYPE)
    chunks = []
    for c in range(tm // CHUNK):
        rows = slice(c * CHUNK, (c + 1) * CHUNK)
        groups = []
        for gp in range(2):
            vn_g = vn[rows, gp * LANES:(gp + 1) * LANES]
            groups.append(_dot(w_ref[2 * gp], vn_g * masks[0]) + _dot(w_ref[2 * gp + 1], vn_g * masks[1]))
        chunks.append(jnp.concatenate(groups, axis=1) + bias)
    mixed = jnp.concatenate(chunks, axis=0)
    return dict(u=u, v=v, ug=ug, tu=tu, tv=tv, rs=rs, vhat=vhat, vn=vn, mixed=mixed, ya=ug * mixed)


def _sgu_fwd(uv, w_tril, bias, g_sgu, g_a):
    s = uv.shape[0]
    tm = 512

    def body(uv_ref, w_ref, b_ref, gs_ref, ga_ref, o_ref):
        t = _sgu_forward_tile(uv_ref[...], w_ref, b_ref[...], gs_ref[...])
        n, _ = _rms_stats(t['ya'])
        o_ref[...] = (n * ga_ref[...]).astype(MXU_DTYPE)

    return pl.pallas_call(
        body, name="sgu_fwd", grid=(s // tm,),
        in_specs=[_rows(tm, 2 * WIDTH_A), _full((HEADS_A, CHUNK, CHUNK)), _full((CHUNK, WIDTH_A)),
                  _full((1, WIDTH_A)), _full((1, WIDTH_A))],
        out_specs=_rows(tm, WIDTH_A), out_shape=jax.ShapeDtypeStruct((s, WIDTH_A), MXU_DTYPE),
        compiler_params=_params(("arbitrary",)),
    )(uv, w_tril, bias, g_sgu, g_a)


def _out_fwd(ya_n, y_b, g_b, w_out, x):
    s = x.shape[0]
    tm = 512
    nc = WIDTH_B // LANES

    def body(ya_ref, *refs):
        yb_refs = refs[:nc]
        g_ref, w_ref, x_ref, h_ref, yn_ref = refs[nc:]
        n, _ = _rms_stats(_load_l256(yb_refs, tm))
        yn = jnp.concatenate([ya_ref[...], (n * g_ref[...]).astype(MXU_DTYPE)], axis=1)
        yn_ref[...] = yn
        h_ref[...] = x_ref[...] + _dot(yn, w_ref[...])

    return pl.pallas_call(
        body, name="out_fwd", grid=(s // tm,),
        in_specs=[_rows(tm, WIDTH_A)] + _col_specs(tm, WIDTH_B) + [_full((1, WIDTH_B)), _full((D_MODEL, D_MODEL)),
                                                                 _rows(tm, D_MODEL)],
        out_specs=[_rows(tm, D_MODEL), _rows(tm, D_MODEL)],
        out_shape=[jax.ShapeDtypeStruct((s, D_MODEL), F32), jax.ShapeDtypeStruct((s, D_MODEL), MXU_DTYPE)],
        compiler_params=_params(("arbitrary",)),
    )(ya_n, *([y_b] * nc), g_b, w_out, x)


def _ffn_fwd(h1, g_ffn, w_gate_t, w_up_t, w_down):
    s = h1.shape[0]
    tm = 512

    def body(h_ref, g_ref, wgt_ref, wut_ref, wd_ref, o_ref, gate_ref, up_ref, hn_ref):
        h = h_ref[...]
        n, _ = _rms_stats(h)
        hn = (n * g_ref[...]).astype(MXU_DTYPE)
        hn_ref[...] = hn
        strips = [dict(cols=slice(lo, hi)) for lo, hi in FF_STRIPS]

        def project(t):
            t['gate'] = _dot_nt(hn, wgt_ref[t['cols'], :])
            t['up'] = _dot_nt(hn, wut_ref[t['cols'], :])

        def activate(t):
            gate, up = t['gate'], t['up']
            gate_ref[:, t['cols']] = gate.astype(MXU_DTYPE)
            up_ref[:, t['cols']] = up.astype(MXU_DTYPE)
            t['act'] = (gate * jax.nn.sigmoid(gate) * up).astype(MXU_DTYPE)

        def down(t):
            return _dot(t['act'], wd_ref[t['cols'], :])

        out = h
        project(strips[0])
        for i, t in enumerate(strips):
            if i + 1 < len(strips):
                project(strips[i + 1])
            activate(t)
            out = out + down(t)
        o_ref[...] = out

    return pl.pallas_call(
        body, name="ffn_fwd", grid=(s // tm,),
        in_specs=[_rows(tm, D_MODEL), _full((1, D_MODEL)), _full((D_FF, D_MODEL)), _full((D_FF, D_MODEL)),
                  _full((D_FF, D_MODEL))],
        out_specs=[_rows(tm, D_MODEL), _rows(tm, D_FF), _rows(tm, D_FF), _rows(tm, D_MODEL)],
        out_shape=[jax.ShapeDtypeStruct((s, D_MODEL), F32), jax.ShapeDtypeStruct((s, D_FF), MXU_DTYPE),
                   jax.ShapeDtypeStruct((s, D_FF), MXU_DTYPE), jax.ShapeDtypeStruct((s, D_MODEL), MXU_DTYPE)],
        compiler_params=_params(("arbitrary",)),
    )(h1, g_ffn, w_gate_t, w_up_t, w_down)


def _ple_loss(h2, p, target, g_ple, w_pg, w_pp_t, g_final):
    s = h2.shape[0]
    tm = 256

    def body(h_ref, p_ref, t_ref, gp_ref, wg_ref, wpt_ref, gf_ref,
             loss_ref, dh_ref, dz_ref, dpp_ref, hn_ref, dgp_ref, dgf_ref):
        @pl.when(pl.program_id(0) == 0)
        def _():
            loss_ref[...] = jnp.zeros_like(loss_ref)
            dgp_ref[...] = jnp.zeros_like(dgp_ref)
            dgf_ref[...] = jnp.zeros_like(dgf_ref)

        h2t = h_ref[...]
        n2, r2 = _rms_stats(h2t)
        hn = (n2 * gp_ref[...]).astype(MXU_DTYPE)
        hn_ref[...] = hn
        gate = jax.nn.sigmoid(_dot(hn, wg_ref[...]))
        pp = _dot_nt(p_ref[...].astype(MXU_DTYPE), wpt_ref[...])
        h3 = h2t + gate * pp
        n3, r3 = _rms_stats(h3)
        diff = n3 * gf_ref[...] - t_ref[...]
        loss_ref[...] += jnp.full(loss_ref.shape, 0.5 * jnp.sum(diff * diff) / D_MODEL, F32)
        dy = diff * (1.0 / D_MODEL)
        dgf_ref[...] += jnp.sum(dy * n3, axis=0, keepdims=True)
        dh3 = _rms_bwd(dy * gf_ref[...], n3, r3)
        dpp_ref[...] = (dh3 * gate).astype(MXU_DTYPE)
        dz = (dh3 * pp * gate * (1.0 - gate)).astype(MXU_DTYPE)
        dz_ref[...] = dz
        dhn = _dot_nt(dz, wg_ref[...])
        dgp_ref[...] += jnp.sum(dhn * n2, axis=0, keepdims=True)
        dh_ref[...] = dh3 + _rms_bwd(dhn * gp_ref[...], n2, r2)

    return pl.pallas_call(
        body, name="ple_loss", grid=(s // tm,),
        in_specs=[_rows(tm, D_MODEL), _rows(tm, PLE_DIM), _rows(tm, D_MODEL), _full((1, D_MODEL)),
                  _full((D_MODEL, D_MODEL)), _full((D_MODEL, PLE_DIM)), _full((1, D_MODEL))],
        out_specs=[_full((1, LANES)), _rows(tm, D_MODEL), _rows(tm, D_MODEL), _rows(tm, D_MODEL),
                   _rows(tm, D_MODEL), _full((1, D_MODEL)), _full((1, D_MODEL))],
        out_shape=[jax.ShapeDtypeStruct((1, LANES), F32), jax.ShapeDtypeStruct((s, D_MODEL), F32),
                   jax.ShapeDtypeStruct((s, D_MODEL), MXU_DTYPE), jax.ShapeDtypeStruct((s, D_MODEL), MXU_DTYPE),
                   jax.ShapeDtypeStruct((s, D_MODEL), MXU_DTYPE), jax.ShapeDtypeStruct((1, D_MODEL), F32),
                   jax.ShapeDtypeStruct((1, D_MODEL), F32)],
        compiler_params=_params(("arbitrary",)),
    )(h2, p, target, g_ple, w_pg, w_pp_t, g_final)


def _ffn_bwd(dh2, h1, gate, up, g_ffn, w_down, w_gate_t, w_up_t):
    s = h1.shape[0]
    tm = 256

    def body(dh_ref, h_ref, gate_ref, up_ref, g_ref, wd_ref, wgt_ref, wut_ref,
             o_ref, act_ref, dg_ref, du_ref, dgn_ref):
        @pl.when(pl.program_id(0) == 0)
        def _():
            dgn_ref[...] = jnp.zeros_like(dgn_ref)

        dh = dh_ref[...]
        dhb = dh.astype(MXU_DTYPE)
        strips = [dict(cols=slice(lo, hi)) for lo, hi in FF_STRIPS]

        def back_down(t):
            t['dact'] = _dot_nt(dhb, wd_ref[t['cols'], :])

        def back_act(t):
            cols, dact = t['cols'], t['dact']
            g = gate_ref[:, cols].astype(F32)
            u = up_ref[:, cols].astype(F32)
            sg = jax.nn.sigmoid(g)
            silu = g * sg
            act_ref[:, cols] = (silu * u).astype(MXU_DTYPE)
            t['du'] = (dact * silu).astype(MXU_DTYPE)
            t['dg'] = (dact * u * sg * (1.0 + g * (1.0 - sg))).astype(MXU_DTYPE)
            du_ref[:, cols] = t['du']
            dg_ref[:, cols] = t['dg']

        def back_in(t):
            return _dot(t['dg'], wgt_ref[t['cols'], :]) + _dot(t['du'], wut_ref[t['cols'], :])

        dhn = jnp.zeros((tm, D_MODEL), F32)
        back_down(strips[0])
        for i, t in enumerate(strips):
            if i + 1 < len(strips):
                back_down(strips[i + 1])
            back_act(t)
            dhn = dhn + back_in(t)
        n, r = _rms_stats(h_ref[...])
        dgn_ref[...] += jnp.sum(dhn * n, axis=0, keepdims=True)
        o_ref[...] = dh + _rms_bwd(dhn * g_ref[...], n, r)

    return pl.pallas_call(
        body, name="ffn_bwd", grid=(s // tm,),
        in_specs=[_rows(tm, D_MODEL), _rows(tm, D_MODEL), _rows(tm, D_FF), _rows(tm, D_FF), _full((1, D_MODEL)),
                  _full((D_FF, D_MODEL)), _full((D_FF, D_MODEL)), _full((D_FF, D_MODEL))],
        out_specs=[_rows(tm, D_MODEL), _rows(tm, D_FF), _rows(tm, D_FF), _rows(tm, D_FF), _full((1, D_MODEL))],
        out_shape=[jax.ShapeDtypeStruct((s, D_MODEL), F32), jax.ShapeDtypeStruct((s, D_FF), MXU_DTYPE),
                   jax.ShapeDtypeStruct((s, D_FF), MXU_DTYPE), jax.ShapeDtypeStruct((s, D_FF), MXU_DTYPE),
                   jax.ShapeDtypeStruct((1, D_MODEL), F32)],
        compiler_params=_params(("arbitrary",)),
    )(dh2, h1, gate, up, g_ffn, w_down, w_gate_t, w_up_t)


def _out_bwd(dh1, y_b, g_b, w_out):
    s = dh1.shape[0]
    tm = 512
    nc = WIDTH_B // LANES

    def body(dh_ref, *refs):
        yb_refs = refs[:nc]
        g_ref, w_ref, dya_ref, dyb_ref, dg_ref, scr = refs[nc:]

        @pl.when(pl.program_id(0) == 0)
        def _():
            dg_ref[...] = jnp.zeros_like(dg_ref)

        dy = _dot_nt(dh_ref[...].astype(MXU_DTYPE), w_ref[...])
        dya_ref[...] = dy[:, :WIDTH_A]
        dyb = dy[:, WIDTH_A:]
        n, r = _rms_stats(_load_l256(yb_refs, tm))
        dg_ref[...] += jnp.sum(dyb * n, axis=0, keepdims=True)
        dyb_in = _rms_bwd(dyb * g_ref[...], n, r)
        for j in range(nc):
            cols = slice(j * LANES, (j + 1) * LANES)
            _store_l256(scr, dyb_ref, cols, dyb_in[:, cols])

    return pl.pallas_call(
        body, name="out_bwd", grid=(s // tm,), scratch_shapes=[pltpu.VMEM((tm, LANES), F32)],
        in_specs=[_rows(tm, D_MODEL)] + _col_specs(tm, WIDTH_B) + [_full((1, WIDTH_B)), _full((D_MODEL, D_MODEL))],
        out_specs=[_rows(tm, WIDTH_A), _rows(tm, WIDTH_B), _full((1, WIDTH_B))],
        out_shape=[jax.ShapeDtypeStruct((s, WIDTH_A), F32), jax.ShapeDtypeStruct((s, WIDTH_B), F32),
                   jax.ShapeDtypeStruct((1, WIDTH_B), F32)],
        compiler_params=_params(("arbitrary",)),
    )(dh1, *([y_b] * nc), g_b, w_out)


def _attn_bwd_branch(q, k, v, do, o, lse, grads, dil):
    s = q.shape[0]
    br = _Branch(dil, s)
    qn, nb = br.qn, br.nb
    first = grads is None

    def body(*refs):
        bias_ref, q_ref, kc_ref, kp_ref, vc_ref, vp_ref, do_ref, o_ref, lse_ref = refs[:9]
        if first:
            rest = refs[9:]
        else:
            dq_in, dk_in, dv_in = refs[9:12]
            rest = refs[12:]
        dq_ref, dk_ref, dv_ref, dk_carry, dv_carry = rest
        n = pl.program_id(1)

        @pl.when(n == 0)
        def _():
            dk_carry[...] = jnp.zeros_like(dk_carry)
            dv_carry[...] = jnp.zeros_like(dv_carry)

        @pl.when(n < nb)
        def _():
            bias2 = jnp.concatenate([bias_ref[0], bias_ref[0]], axis=0)
            lane = lax.broadcasted_iota(jnp.int32, (qn, LANES), 1)
            lo = lane < HEAD_DIM
            mask_f = lo.astype(F32)
            mask_lo = mask_f.astype(MXU_DTYPE)
            def prepare(j, hp):
                cols = slice(hp * LANES, (hp + 1) * LANES)
                qp = br.load(q_ref, cols, j)
                dop = br.load(do_ref, cols, j)
                prod = dop * br.load(o_ref, cols, j)
                prod_lo = prod * mask_f
                lse = br.load(lse_ref, cols, j)
                return dict(
                    j=j, cols=cols,
                    kcat=jnp.concatenate([br.load(kp_ref, cols, j), br.load(kc_ref, cols, j)], axis=0),
                    vcat=jnp.concatenate([br.load(vp_ref, cols, j), br.load(vc_ref, cols, j)], axis=0),
                    qs=jnp.concatenate([qp * mask_lo, qp * (1 - mask_lo)], axis=0),
                    dos=jnp.concatenate([dop * mask_f, dop * (1.0 - mask_f)], axis=0).astype(MXU_DTYPE),
                    delta=jnp.concatenate([jnp.sum(prod_lo, axis=1, keepdims=True),
                                           jnp.sum(prod - prod_lo, axis=1, keepdims=True)], axis=0),
                    lse2=jnp.concatenate([lse[:, :1], lse[:, HEAD_DIM:HEAD_DIM + 1]], axis=0))

            def scores(t):
                t['sc'] = _dot_nt(t['qs'], t['kcat'])
                t['dp'] = _dot_nt(t['dos'], t['vcat'])

            def softmax(t):
                p = jnp.exp(t['sc'] + bias2 - t['lse2'])
                t['ds'] = (p * (t['dp'] - t['delta'])).astype(MXU_DTYPE)
                t['p'] = p.astype(MXU_DTYPE)

            def gradients(t):
                t['dvc'] = _dot_tn(t['p'], t['dos'])
                t['dkc'] = _dot_tn(t['ds'], t['qs'])
                t['dq2'] = _dot(t['ds'], t['kcat'])

            def store(t):
                j, cols, dkc, dvc = t['j'], t['cols'], t['dkc'], t['dvc']
                dq = jnp.where(lo, t['dq2'][:qn], t['dq2'][qn:])
                dk_prev = dk_carry[j, :, cols] + dkc[:qn]
                dv_prev = dv_carry[j, :, cols] + dvc[:qn]
                if not first:
                    dq = dq + br.load(dq_in, cols, j)
                    dk_prev = dk_prev + br.load(dk_in, cols, j)
                    dv_prev = dv_prev + br.load(dv_in, cols, j)
                br.store(dq_ref, cols, dq, j)
                br.store(dk_ref, cols, dk_prev, j)
                br.store(dv_ref, cols, dv_prev, j)
                dk_carry[j, :, cols] = dkc[qn:]
                dv_carry[j, :, cols] = dvc[qn:]

            for j, first_pair in itertools.product(range(br.res), range(0, HEADS_B // 2, PAIRS_ABREAST)):
                group = [prepare(j, hp) for hp in range(first_pair, first_pair + PAIRS_ABREAST)]
                for stage in (scores, softmax, gradients, store):
                    for t in group:
                        stage(t)

        @pl.when(n == nb)
        def _():
            for j in range(br.res):
                dk_last = dk_carry[j]
                dv_last = dv_carry[j]
                if not first:
                    dk_last = dk_last + br.load(dk_in, slice(None), j)
                    dv_last = dv_last + br.load(dv_in, slice(None), j)
                br.store(dk_ref, slice(None), dk_last, j)
                br.store(dv_ref, slice(None), dv_last, j)

    cur = lambda n: jnp.minimum(n, nb - 1)
    before = lambda n: jnp.maximum(cur(n) - 1, 0)
    late = lambda n: jnp.maximum(n - 1, 0)
    in_specs = [br.bias_spec(cur), br.spec(WIDTH_B, cur), br.spec(WIDTH_B, cur), br.spec(WIDTH_B, before),
                br.spec(WIDTH_B, cur), br.spec(WIDTH_B, before), br.spec(WIDTH_B, cur), br.spec(WIDTH_B, cur),
                br.spec(WIDTH_B, cur)]
    args = [jnp.asarray(br.bias)] + [br.view(a) for a in (q, k, k, v, v, do, o, lse)]
    if not first:
        in_specs += [br.spec(WIDTH_B, cur), br.spec(WIDTH_B, late), br.spec(WIDTH_B, late)]
        args += [br.view(g) for g in grads]
    res = pl.pallas_call(
        body, name="attn_bwd_d%d" % dil, grid=(br.grid[0], nb + 1), in_specs=in_specs,
        out_specs=[br.spec(WIDTH_B, cur), br.spec(WIDTH_B, late), br.spec(WIDTH_B, late)],
        out_shape=[jax.ShapeDtypeStruct((s // L_BLOCK, 4, 4, L_GROUP, WIDTH_B), F32)] * 3,
        scratch_shapes=[pltpu.VMEM((br.res, qn, WIDTH_B), F32), pltpu.VMEM((br.res, qn, WIDTH_B), F32)],
        compiler_params=_params(("arbitrary", "arbitrary")),
    )(*args)
    return tuple(a.reshape(s, WIDTH_B) for a in res)


def _sgu_bwd(uv, dya_n, w_tril, w_tril_t, bias, g_sgu, g_a):
    s = uv.shape[0]
    tm = 512

    def body(uv_ref, dy_ref, w_ref, wt_ref, b_ref, gs_ref, ga_ref, duv_ref, dw_ref, db_ref, dgs_ref, dga_ref,
             db_acc):
        i = pl.program_id(0)

        @pl.when(i == 0)
        def _():
            dw_ref[...] = jnp.zeros_like(dw_ref)
            dgs_ref[...] = jnp.zeros_like(dgs_ref)
            dga_ref[...] = jnp.zeros_like(dga_ref)
            db_acc[...] = jnp.zeros_like(db_acc)

        t = _sgu_forward_tile(uv_ref[...], w_ref, b_ref[...], gs_ref[...])
        na, ra = _rms_stats(t['ya'])
        dyn = dy_ref[...]
        dga_ref[...] += jnp.sum(dyn * na, axis=0, keepdims=True)
        dya = _rms_bwd(dyn * ga_ref[...], na, ra)
        dug = dya * t['mixed']
        dmixed = dya * t['ug']
        dmb = dmixed.astype(MXU_DTYPE)
        masks = _half_masks(MXU_DTYPE)
        chunks = []
        db = jnp.zeros((CHUNK, WIDTH_A), F32)
        for c in range(tm // CHUNK):
            rows = slice(c * CHUNK, (c + 1) * CHUNK)
            db = db + dmixed[rows]
            groups = []
            for gp in range(2):
                cols = slice(gp * LANES, (gp + 1) * LANES)
                dm_g = dmb[rows, cols]
                vn_g = t['vn'][rows, cols]
                dvn_g = jnp.zeros((CHUNK, LANES), F32)
                for j in range(2):
                    dm_h = dm_g * masks[j]
                    dvn_g = dvn_g + _dot(wt_ref[2 * gp + j], dm_h)
                    dw_ref[2 * gp + j] += _dot_nt(dm_h, vn_g)
                groups.append(dvn_g)
            chunks.append(jnp.concatenate(groups, axis=1))
        db_acc[...] += db
        dvn = jnp.concatenate(chunks, axis=0)
        vhat = t['vhat']
        dgs_ref[...] += jnp.sum(dvn * vhat, axis=0, keepdims=True)
        dvh = dvn * gs_ref[...]
        dvg = t['rs'] * (dvh - jnp.mean(dvh, axis=-1, keepdims=True)
                         - vhat * jnp.mean(dvh * vhat, axis=-1, keepdims=True))
        duv_ref[:, :WIDTH_A] = (dug * _gelu_grad(t['u'], t['tu'])).astype(MXU_DTYPE)
        duv_ref[:, WIDTH_A:] = (dvg * _gelu_grad(t['v'], t['tv'])).astype(MXU_DTYPE)

        @pl.when(i == pl.num_programs(0) - 1)
        def _():
            lane_a = lax.broadcasted_iota(jnp.int32, (CHUNK, WIDTH_A), 1)
            lane = lax.broadcasted_iota(jnp.int32, (CHUNK, LANES), 1)
            acc = db_acc[...]
            out = jnp.zeros((CHUNK, LANES), F32)
            for h in range(HEADS_A):
                col = jnp.sum(jnp.where(lane_a // HEAD_DIM == h, acc, 0.0), axis=1, keepdims=True)
                out = jnp.where(lane == h, col, out)
            db_ref[...] = out
            causal = (lax.broadcasted_iota(jnp.int32, (CHUNK, CHUNK), 0)
                      >= lax.broadcasted_iota(jnp.int32, (CHUNK, CHUNK), 1))
            for h in range(HEADS_A):
                dw_ref[h] = jnp.where(causal, dw_ref[h], 0.0)

    return pl.pallas_call(
        body, name="sgu_bwd", grid=(s // tm,),
        in_specs=[_rows(tm, 2 * WIDTH_A), _rows(tm, WIDTH_A), _full((HEADS_A, CHUNK, CHUNK)),
                  _full((HEADS_A, CHUNK, CHUNK)), _full((CHUNK, WIDTH_A)), _full((1, WIDTH_A)),
                  _full((1, WIDTH_A))],
        out_specs=[_rows(tm, 2 * WIDTH_A), _full((HEADS_A, CHUNK, CHUNK)), _full((CHUNK, LANES)),
                   _full((1, WIDTH_A)), _full((1, WIDTH_A))],
        out_shape=[jax.ShapeDtypeStruct((s, 2 * WIDTH_A), MXU_DTYPE),
                   jax.ShapeDtypeStruct((HEADS_A, CHUNK, CHUNK), F32), jax.ShapeDtypeStruct((CHUNK, LANES), F32),
                   jax.ShapeDtypeStruct((1, WIDTH_A), F32), jax.ShapeDtypeStruct((1, WIDTH_A), F32)],
        scratch_shapes=[pltpu.VMEM((CHUNK, WIDTH_A), F32)],
        compiler_params=_params(("arbitrary",)),
    )(uv, dya_n, w_tril, w_tril_t, bias, g_sgu, g_a)


def _in_bwd_proj(duv, dq, dk, dv, cos_t, sin_t):
    s = duv.shape[0]
    tm = 512
    nc = WIDTH_B // LANES

    def body(duv_ref, *refs):
        dq_refs, dk_refs, dv_refs = refs[:nc], refs[nc:2 * nc], refs[2 * nc:3 * nc]
        cos_ref, sin_ref, dp_ref = refs[3 * nc:]
        cos = cos_ref[...]
        sin = sin_ref[...]
        dp_ref[:, :2 * WIDTH_A] = duv_ref[...]
        for i in range(nc):
            lo = 2 * WIDTH_A + i * LANES
            tq = _load_l256(dq_refs[i:i + 1], tm) * (HEAD_DIM ** -0.5)
            tk = _load_l256(dk_refs[i:i + 1], tm)
            dp_ref[:, lo:lo + LANES] = (tq * cos + _rope_partner(tq * sin)).astype(MXU_DTYPE)
            dp_ref[:, lo + WIDTH_B:lo + WIDTH_B + LANES] = (tk * cos + _rope_partner(tk * sin)).astype(MXU_DTYPE)
            dp_ref[:, lo + 2 * WIDTH_B:lo + 2 * WIDTH_B + LANES] = _load_l256(dv_refs[i:i + 1], tm).astype(MXU_DTYPE)

    return pl.pallas_call(
        body, name="in_bwd_proj", grid=(s // tm,),
        in_specs=[_rows(tm, 2 * WIDTH_A)] + 3 * _col_specs(tm, WIDTH_B) + [_rows(tm, LANES), _rows(tm, LANES)],
        out_specs=_rows(tm, IN_COLS), out_shape=jax.ShapeDtypeStruct((s, IN_COLS), MXU_DTYPE),
        compiler_params=_params(("arbitrary",)),
    )(duv, *([dq] * nc), *([dk] * nc), *([dv] * nc), cos_t, sin_t)


def _in_bwd_x(dproj, w_in_t, x, g_mix, dh1, after):
    s = x.shape[0]
    tm = 512

    def body(dp_ref, wt_ref, x_ref, g_ref, dh_ref, *rest):
        gx_ref, dg_ref = rest[len(after):]
        @pl.when(pl.program_id(0) == 0)
        def _():
            dg_ref[...] = jnp.zeros_like(dg_ref)

        dhn = _dot(dp_ref[...], wt_ref[...])
        n, r = _rms_stats(x_ref[...])
        dg_ref[...] += jnp.sum(dhn * n, axis=0, keepdims=True)
        gx_ref[...] = dh_ref[...] + _rms_bwd(dhn * g_ref[...], n, r)

    return pl.pallas_call(
        body, name="in_bwd_x", grid=(s // tm,),
        in_specs=[_rows(tm, IN_COLS), _full((IN_COLS, D_MODEL)), _rows(tm, D_MODEL), _full((1, D_MODEL)),
                  _rows(tm, D_MODEL)] + [pl.BlockSpec(memory_space=pl.ANY)] * len(after),
        out_specs=[_rows(tm, D_MODEL), _full((1, D_MODEL))],
        out_shape=[jax.ShapeDtypeStruct((s, D_MODEL), F32), jax.ShapeDtypeStruct((1, D_MODEL), F32)],
        compiler_params=_params(("arbitrary",)),
    )(dproj, w_in_t, x, g_mix, dh1, *after)


def _wgrad(a, b, name):
    s, m = a.shape
    n = b.shape[1]
    bm = 512 if m % 512 == 0 else FF_HALF
    ts = 1024
    nsteps = s // ts

    def body(a_ref, b_ref, o_ref, acc):
        kk = pl.program_id(1)

        @pl.when(kk == 0)
        def _():
            acc[...] = jnp.zeros_like(acc)

        acc[...] += _dot_tn(a_ref[...].astype(MXU_DTYPE), b_ref[...].astype(MXU_DTYPE))

        @pl.when(kk == nsteps - 1)
        def _():
            o_ref[...] = acc[...].astype(o_ref.dtype)

    return pl.pallas_call(
        body, name=name, grid=(m // bm, nsteps),
        in_specs=[pl.BlockSpec((ts, bm), lambda i, kk: (kk, i)), pl.BlockSpec((ts, n), lambda i, kk: (kk, 0))],
        out_specs=pl.BlockSpec((bm, n), lambda i, kk: (i, 0)), out_shape=jax.ShapeDtypeStruct((m, n), jnp.bfloat16),
        scratch_shapes=[pltpu.VMEM((bm, n), F32)],
        compiler_params=_params(("arbitrary", "arbitrary")),
    )(a, b)


def _rope_tables(s):
    half = HEAD_DIM // 2
    inv = ROPE_THETA ** (-jnp.arange(half, dtype=F32) / half)
    ang = jnp.arange(s, dtype=F32)[:, None] * jnp.tile(inv, LANES // half)[None, :]
    sign = jnp.tile(jnp.concatenate([-jnp.ones(half, F32), jnp.ones(half, F32)]), LANES // HEAD_DIM)
    return jnp.cos(ang), jnp.sin(ang) * sign[None, :]


MESH = pl.DeviceIdType.MESH
ANY = pl.BlockSpec(memory_space=pl.ANY)
SEM = pl.BlockSpec(memory_space=pltpu.SEMAPHORE)
SPLIT_COPY = pltpu.CompilerParams(has_side_effects=pltpu.SideEffectType.DATAFLOW_SIDE_EFFECTING)
SLAB_IS_TRANSPOSED = {'w_in': True, 'w_out': False, 'w_gate': True, 'w_up': True, 'w_down': False,
                      'w_ple_gate': False, 'w_ple_proj': True}


def _place():
    x, y, c = lax.axis_index("x"), lax.axis_index("y"), lax.axis_index("c")
    other_chips = [(1 - x, y), (x, 1 - y), (1 - x, 1 - y)]
    return x, y, c, other_chips


def _chip_of(chip):
    return 2 * chip[0] + chip[1]


def _half(ref, lead, hc):
    hr = ref.shape[1] // 2
    return ref.at[lead, pl.ds(hc * hr, hr), :]


def _put_own(stack, own, index):
    return lax.dynamic_update_slice(stack, own[None], (index,) + (0,) * own.ndim)


def _all_gather_now(slab):
    rows, cols = slab.shape

    def body(x_ref, out_ref, send_sems, recv_sems):
        x, y, c, chips = _place()
        sibling = (x, y, 1 - c)
        hr = rows // 2

        def copy(k, src, dst, to):
            return pltpu.make_async_remote_copy(src_ref=src, dst_ref=dst, send_sem=send_sems.at[k],
                                                recv_sem=recv_sems.at[k], device_id=to, device_id_type=MESH)

        my_half = x_ref.at[pl.ds(c * hr, hr), :]
        first = [copy(j, my_half, _half(out_ref, 2 * x + y, c), (*chip, c)) for j, chip in enumerate(chips)]
        for cp in first:
            cp.start()
        passed = [copy(3 + j, _half(out_ref, _chip_of(chip), c), _half(out_ref, _chip_of(chip), c), sibling)
                  for j, chip in enumerate(chips)]
        for j, chip in enumerate(chips):
            copy(j, my_half, _half(out_ref, _chip_of(chip), c), (*chip, c)).wait_recv()
            passed[j].start()
        for j, chip in enumerate(chips):
            copy(3 + j, my_half, _half(out_ref, _chip_of(chip), 1 - c), sibling).wait_recv()
        for cp in first + passed:
            cp.wait_send()

    gathered = pl.pallas_call(
        body, name="all_gather_now", out_shape=jax.ShapeDtypeStruct((N_CHIPS, rows, cols), slab.dtype),
        in_specs=[ANY], out_specs=ANY,
        scratch_shapes=[pltpu.SemaphoreType.DMA((6,)), pltpu.SemaphoreType.DMA((6,))],
    )(slab)
    me = 2 * lax.axis_index("x") + lax.axis_index("y")
    return _put_own(gathered, slab, me).reshape(N_CHIPS * rows, cols)


def _gather_copies(slab_refs, land_refs, send_sems, recv_sems):
    x, y, c, chips = _place()
    sends, recvs = [], []
    for k, (src, land) in enumerate(zip(slab_refs, land_refs)):
        hr = src.shape[0] // 2
        for j, chip in enumerate(chips):
            for t in range(2):
                sends.append(pltpu.make_async_remote_copy(
                    src_ref=src.at[pl.ds(c * hr, hr), :], dst_ref=_half(land, 2 * x + y, c),
                    send_sem=send_sems.at[6 * k + 2 * j + t], recv_sem=recv_sems.at[6 * k + 2 * j + c],
                    device_id=(*chip, t), device_id_type=MESH))
                recvs.append(pltpu.make_async_remote_copy(
                    src_ref=src.at[pl.ds(t * hr, hr), :], dst_ref=_half(land, _chip_of(chip), t),
                    send_sem=send_sems.at[6 * k + 2 * j + t], recv_sem=recv_sems.at[6 * k + 2 * j + t],
                    device_id=(*chip, t), device_id_type=MESH))
    return sends, recvs


def _all_gather_start(slabs, after):
    n = len(slabs)

    def body(*refs):
        slab_refs, land_refs = refs[:n], refs[n:2 * n]
        send_sems, recv_sems = refs[2 * n + 1:2 * n + 3]
        token = refs[-1]
        sends, _ = _gather_copies(slab_refs, land_refs, send_sems, recv_sems)
        for cp in sends:
            cp.start()
        token[...] = jnp.zeros_like(token)

    lands = [lax.empty((N_CHIPS,) + s.shape, s.dtype) for s in slabs]
    hbm = lambda a: pltpu.HBM(a.shape, a.dtype)
    res = pl.pallas_call(
        body, name="all_gather_start",
        out_shape=(pltpu.SemaphoreType.DMA((6 * n,)), pltpu.SemaphoreType.DMA((6 * n,)), *map(hbm, slabs),
                   *map(hbm, lands), jax.ShapeDtypeStruct((8, LANES), F32)),
        in_specs=[ANY] * (2 * n + 1),
        out_specs=(SEM, SEM, *([ANY] * (2 * n)), pl.BlockSpec(memory_space=pltpu.VMEM)),
        input_output_aliases={i: 2 + i for i in range(2 * n)}, compiler_params=SPLIT_COPY,
    )(*[pltpu.with_memory_space_constraint(a, pltpu.HBM) for a in list(slabs) + lands], after)
    return res[:-1], res[-1]


def _all_gather_wait(handle, after):
    send_sems, recv_sems = handle[:2]
    n = (len(handle) - 2) // 2
    slabs, lands = handle[2:2 + n], handle[2 + n:]

    def body(*refs):
        slab_refs, land_refs = refs[:n], refs[n:2 * n]
        send_sems, recv_sems = refs[2 * n:2 * n + 2]
        sends, recvs = _gather_copies(slab_refs, land_refs, send_sems, recv_sems)
        for cp in sends:
            cp.wait_send()
        for cp in recvs:
            cp.wait_recv()

    hbm = lambda a: pltpu.HBM(a.shape, a.dtype)
    res = pl.pallas_call(
        body, name="all_gather_wait", out_shape=tuple(map(hbm, list(slabs) + list(lands))),
        in_specs=[ANY] * (2 * n) + [SEM, SEM, ANY], out_specs=tuple([ANY] * (2 * n)),
        input_output_aliases={i: i for i in range(2 * n)}, compiler_params=SPLIT_COPY,
    )(*slabs, *lands, send_sems, recv_sems, after)
    me = 2 * lax.axis_index("x") + lax.axis_index("y")
    return [_put_own(land, slab, me).reshape(N_CHIPS * slab.shape[0], slab.shape[1])
            for slab, land in zip(res[:n], res[n:])]


def _scatter_copies(part_refs, land_refs, send_sems, recv_sems):
    x, y, c, chips = _place()
    me = 4 * x + 2 * y + c
    sends, recvs = [], []
    for k, (part, land) in enumerate(zip(part_refs, land_refs)):
        for j, chip in enumerate(chips):
            for t in range(2):
                sends.append(pltpu.make_async_remote_copy(
                    src_ref=part.at[_chip_of(chip)], dst_ref=land.at[me],
                    send_sem=send_sems.at[7 * k + 2 * j + t], recv_sem=recv_sems.at[7 * k + 2 * j + c],
                    device_id=(*chip, t), device_id_type=MESH))
                recvs.append(pltpu.make_async_remote_copy(
                    src_ref=part.at[_chip_of(chip)], dst_ref=land.at[2 * _chip_of(chip) + t],
                    send_sem=send_sems.at[7 * k + 2 * j + t], recv_sem=recv_sems.at[7 * k + 2 * j + t],
                    device_id=(*chip, t), device_id_type=MESH))
        sends.append(pltpu.make_async_remote_copy(
            src_ref=part.at[2 * x + y], dst_ref=land.at[me], send_sem=send_sems.at[7 * k + 6],
            recv_sem=recv_sems.at[7 * k + 6], device_id=(x, y, 1 - c), device_id_type=MESH))
        recvs.append(pltpu.make_async_remote_copy(
            src_ref=part.at[2 * x + y], dst_ref=land.at[4 * x + 2 * y + 1 - c],
            send_sem=send_sems.at[7 * k + 6], recv_sem=recv_sems.at[7 * k + 6], device_id=(x, y, 1 - c),
            device_id_type=MESH))
    return sends, recvs


def _reduce_scatter_start(parts, name):
    n = len(parts)
    parts = [p.reshape(N_CHIPS, p.shape[0] // N_CHIPS, p.shape[1]) for p in parts]

    def body(*refs):
        part_refs, land_refs = refs[:n], refs[n:2 * n]
        send_sems, recv_sems = refs[2 * n:2 * n + 2]
        token = refs[-1]
        sends, _ = _scatter_copies(part_refs, land_refs, send_sems, recv_sems)
        for cp in sends:
            cp.start()
        token[...] = jnp.zeros_like(token)

    lands = [lax.empty((N_DEV, p.shape[1], p.shape[2]), p.dtype) for p in parts]
    hbm = lambda a: pltpu.HBM(a.shape, a.dtype)
    res = pl.pallas_call(
        body, name=name,
        out_shape=(pltpu.SemaphoreType.DMA((7 * n,)), pltpu.SemaphoreType.DMA((7 * n,)), *map(hbm, parts),
                   *map(hbm, lands), jax.ShapeDtypeStruct((8, LANES), F32)),
        in_specs=[ANY] * (2 * n), out_specs=(SEM, SEM, *([ANY] * (2 * n)), pl.BlockSpec(memory_space=pltpu.VMEM)),
        input_output_aliases={i: 2 + i for i in range(2 * n)}, compiler_params=SPLIT_COPY,
    )(*[pltpu.with_memory_space_constraint(a, pltpu.HBM) for a in parts + lands])
    return res[:-1], res[-1]


def _reduce_scatter_wait(handle, after, name):
    send_sems, recv_sems = handle[:2]
    n = (len(handle) - 2) // 2
    parts, lands = handle[2:2 + n], handle[2 + n:]

    def body(*refs):
        part_refs, land_refs = refs[:n], refs[n:2 * n]
        send_sems, recv_sems = refs[2 * n:2 * n + 2]
        sends, recvs = _scatter_copies(part_refs, land_refs, send_sems, recv_sems)
        for cp in sends:
            cp.wait_send()
        for cp in recvs:
            cp.wait_recv()

    hbm = lambda a: pltpu.HBM(a.shape, a.dtype)
    res = pl.pallas_call(
        body, name=name, out_shape=tuple(map(hbm, list(parts) + list(lands))),
        in_specs=[ANY] * (2 * n) + [SEM, SEM, ANY], out_specs=tuple([ANY] * (2 * n)),
        input_output_aliases={i: i for i in range(2 * n)}, compiler_params=SPLIT_COPY,
    )(*parts, *lands, send_sems, recv_sems, after)
    return list(zip(res[:n], res[n:]))


def _adamw_of_shares(w, own, land, m, v, name):
    rows, cols = w.shape
    tm = rows // 4 if rows % 32 == 0 else rows
    x, y, c = lax.axis_index("x"), lax.axis_index("y"), lax.axis_index("c")
    where = jnp.stack([2 * x + y, 4 * x + 2 * y + c]).astype(jnp.int32)

    def body(where_ref, w_ref, own_ref, land_ref, m_ref, v_ref, g_ref, d_ref, nm_ref, nv_ref):
        me = where_ref[1]
        g_ = jnp.zeros((tm, cols), F32)
        for dev in range(N_DEV):
            g_ = g_ + jnp.where(me == dev, own_ref[0], land_ref[dev]).astype(F32)
        m_ = ADAM_B1 * m_ref[...] + (1.0 - ADAM_B1) * g_
        v_ = ADAM_B2 * v_ref[...] + (1.0 - ADAM_B2) * (g_ * g_)
        m_hat = m_ / (1.0 - ADAM_B1 ** ADAM_STEP)
        v_hat = v_ / (1.0 - ADAM_B2 ** ADAM_STEP)
        g_ref[...] = g_
        d_ref[...] = -ADAM_LR * (m_hat / (jnp.sqrt(v_hat) + ADAM_EPS) + ADAM_WD * w_ref[...])
        nm_ref[...] = m_
        nv_ref[...] = v_

    tile = pl.BlockSpec((tm, cols), lambda i, where_ref: (i, 0))
    spec = pltpu.PrefetchScalarGridSpec(
        num_scalar_prefetch=1, grid=(rows // tm,),
        in_specs=[tile, pl.BlockSpec((1, tm, cols), lambda i, where_ref: (where_ref[0], i, 0)),
                  pl.BlockSpec((N_DEV, tm, cols), lambda i, where_ref: (0, i, 0)), tile, tile],
        out_specs=[tile] * 4)
    return pl.pallas_call(
        body, name=name, grid_spec=spec, out_shape=[jax.ShapeDtypeStruct(w.shape, F32)] * 4,
        compiler_params=_params(("arbitrary",)),
    )(where, w, own, land, m, v)


def _pack_small(values):
    flat = jnp.concatenate([values[n].reshape(-1).astype(F32) for n in SMALL])
    return jnp.pad(flat, (0, SMALL_ROWS * D_MODEL - flat.shape[0])).reshape(SMALL_ROWS, D_MODEL)


def _unpack_small(block, shapes):
    flat = block.reshape(-1)
    out, lo = {}, 0
    for n in SMALL:
        out[n] = flat[lo:lo + SMALL_SIZES[n]].reshape(shapes[n])
        lo += SMALL_SIZES[n]
    return out


def _after(token, a):
    return a + token[:1, :1].astype(a.dtype)


def kernel(x, p, mix_norm_g, w_in, sgu_w, sgu_b, sgu_norm_g, out_norm_a, out_norm_b, w_out, ffn_norm_g, w_gate, w_up, w_down, ple_norm_g, w_ple_gate, w_ple_proj, final_norm_g, loss_target, m_mix_norm_g, m_w_in, m_sgu_w, m_sgu_b, m_sgu_norm_g, m_out_norm_a, m_out_norm_b, m_w_out, m_ffn_norm_g, m_w_gate, m_w_up, m_w_down, m_ple_norm_g, m_w_ple_gate, m_w_ple_proj, m_final_norm_g, v_mix_norm_g, v_w_in, v_sgu_w, v_sgu_b, v_sgu_norm_g, v_out_norm_a, v_out_norm_b, v_w_out, v_ffn_norm_g, v_w_gate, v_w_up, v_w_down, v_ple_norm_g, v_w_ple_gate, v_w_ple_proj, v_final_norm_g):
    given = dict(locals())
    drop_lead = lambda a, lead: a.reshape(a.shape[lead:])
    xs, ps, target = drop_lead(x, 1), drop_lead(p, 2), drop_lead(loss_target, 1)
    s = xs.shape[0]
    shard = lambda name: drop_lead(given[name], 1)

    def slab_of(name):
        local = shard(name).astype(MXU_DTYPE)
        return local.T if SLAB_IS_TRANSPOSED[name] else local

    w_in_t = _all_gather_now(slab_of('w_in'))
    later = ['w_out', 'w_gate', 'w_up', 'w_down', 'w_ple_gate', 'w_ple_proj']
    gather, token = _all_gather_start([slab_of(n) for n in later], w_in_t)

    cos_t, sin_t = _rope_tables(s)
    tril = jnp.tril(jnp.ones((CHUNK, CHUNK), F32))
    w_tril = (sgu_w.reshape(HEADS_A, CHUNK, CHUNK) * tril).astype(MXU_DTYPE)
    w_tril_t = jnp.swapaxes(w_tril, 1, 2)
    bias = jnp.repeat(sgu_b.reshape(HEADS_A, CHUNK).T, HEAD_DIM, axis=1)
    g = {n: given[n].reshape(1, -1) for n in SMALL if n not in ('sgu_w', 'sgu_b')}

    uv, q, k, v, q1, k1, v1, hn1 = _in_fwd(xs, _after(token, g['mix_norm_g']), w_in_t, cos_t, sin_t)
    ya_n = _sgu_fwd(uv, w_tril, bias, g['sgu_norm_g'], g['out_norm_a'])
    branches = [_attn_fwd_local(q1, k1, v1)] + [_attn_fwd_branch(q, k, v, dil) for dil in DILATIONS[:-1]]
    y_b, lse = _attn_fwd_branch(q, k, v, DILATIONS[-1], earlier=branches)
    stacks = dict(zip(later, _all_gather_wait(gather, lse)))
    w_gate_t, w_up_t, w_pp_t = stacks['w_gate'], stacks['w_up'], stacks['w_ple_proj']
    h1, y_n = _out_fwd(ya_n, y_b, g['out_norm_b'], stacks['w_out'], xs)
    h2, gate, up, hn2 = _ffn_fwd(h1, g['ffn_norm_g'], w_gate_t, w_up_t, stacks['w_down'])
    loss, dh2, dz, dpp, hn3, d_ple_g, d_final_g = _ple_loss(
        h2, ps, target, g['ple_norm_g'], stacks['w_ple_gate'], w_pp_t, g['final_norm_g'])

    share = {}
    share['w_ple_gate'] = _wgrad(hn3, dz, "wgrad_ple_gate")
    share['w_ple_proj'] = _wgrad(dpp, ps, "wgrad_ple_proj")
    scatter_1, token = _reduce_scatter_start([share['w_ple_gate'], share['w_ple_proj']], "reduce_scatter_start_1")
    dh1, act, dgate, dup, d_ffn_g = _ffn_bwd(dh2, h1, gate, up, _after(token, g['ffn_norm_g']), stacks['w_down'],
                                             w_gate_t, w_up_t)
    share['w_down'] = _wgrad(act, dh2, "wgrad_down")
    share['w_gate'] = _wgrad(dgate, hn2, "wgrad_gate")
    share['w_up'] = _wgrad(dup, hn2, "wgrad_up")
    scatter_2, token = _reduce_scatter_start([share['w_down'], share['w_gate'], share['w_up']],
                                             "reduce_scatter_start_2")
    dya_n, dyb, d_out_b = _out_bwd(dh1, y_b, _after(token, g['out_norm_b']), stacks['w_out'])
    share['w_out'] = _wgrad(y_n, dh1, "wgrad_out")
    scatter_3, token = _reduce_scatter_start([share['w_out']], "reduce_scatter_start_3")
    grads = _attn_bwd_local(q1, k1, v1, dyb, y_b, lse)
    for dil in DILATIONS:
        grads = _attn_bwd_branch(q, k, v, dyb, y_b, lse, grads, dil)
    duv, d_sgu_w, d_sgu_b, d_sgu_g, d_out_a = _sgu_bwd(uv, dya_n, w_tril, w_tril_t, bias,
                                                       _after(token, g['sgu_norm_g']), g['out_norm_a'])
    dproj = _in_bwd_proj(duv, grads[0], grads[1], grads[2], cos_t, sin_t)
    share['w_in'] = _wgrad(dproj, hn1, "wgrad_in")
    scatter_4, token = _reduce_scatter_start([share['w_in']], "reduce_scatter_start_4")

    grads, deltas, new_m, new_v = {}, {}, {}, {}
    add_lead = lambda a: a.reshape((1,) + a.shape)

    done = {}

    def finish(names, handles, after, tag):
        landed = []
        for i, handle in enumerate(handles):
            landed += _reduce_scatter_wait(handle, after, "reduce_scatter_wait_%s%d" % (tag, i))
        for n, (own, land) in zip(names, landed):
            turn = (lambda a: a.T) if SLAB_IS_TRANSPOSED[n] else (lambda a: a)
            res = _adamw_of_shares(turn(shard(n)), own, land, turn(shard("m_" + n)), turn(shard("v_" + n)),
                                   "adamw_" + n)
            grads[n], deltas[n], new_m[n], new_v[n] = (add_lead(turn(a)) for a in res)
            done[n] = res[0]

    finish(['w_ple_gate', 'w_ple_proj', 'w_down', 'w_gate', 'w_up', 'w_out'], [scatter_1, scatter_2, scatter_3], token,
           "early")
    grad_x, d_mix_g = _in_bwd_x(dproj, w_in_t, xs, g['mix_norm_g'], dh1,
                                after=[done[n] for n in ('w_down', 'w_gate', 'w_up', 'w_out')])

    gs = {'mix_norm_g': d_mix_g, 'sgu_w': d_sgu_w, 'sgu_b': d_sgu_b[:, :HEADS_A].T, 'sgu_norm_g': d_sgu_g,
          'out_norm_a': d_out_a, 'out_norm_b': d_out_b, 'ffn_norm_g': d_ffn_g, 'ple_norm_g': d_ple_g,
          'final_norm_g': d_final_g}
    gs_block = _pack_small(gs).at[SMALL_ROWS - 1, 0].set(loss[0, 0])
    to_all = jnp.broadcast_to(gs_block[None], (N_CHIPS,) + gs_block.shape).reshape(-1, D_MODEL)
    scatter_small, token = _reduce_scatter_start([to_all], "small_all_reduce_start")
    finish(['w_in'], [scatter_4], token, "last")
    (own, land), = _reduce_scatter_wait(scatter_small, done['w_in'], "small_all_reduce_wait")
    small = {n: given[n] for n in SMALL}
    small_res = _adamw_of_shares(_pack_small(small), own, land, _pack_small({n: given["m_" + n] for n in SMALL}),
                                 _pack_small({n: given["v_" + n] for n in SMALL}), "adamw_small")
    loss_out = small_res[0][SMALL_ROWS - 1, 0]
    small_shapes = {n: given[n].shape for n in SMALL}
    for res, blk in zip((grads, deltas, new_m, new_v), small_res):
        res.update(_unpack_small(blk, small_shapes))

    outs = [loss_out, add_lead(grad_x)]
    for res in (grads, deltas, new_m, new_v):
        outs += [res[n] for n in WEIGHT_NAMES]
    return tuple(outs)
```

```python
import functools
import itertools
import math

import jax
import jax.numpy as jnp
import numpy as np
from jax import lax
from jax.experimental import pallas as pl
from jax.experimental.pallas import tpu as pltpu

F32 = jnp.float32
MXU_DTYPE = jnp.bfloat16

D_MODEL = 1024
HEAD_DIM = 64
HEADS_A = 4
HEADS_B = 12
WIDTH_A = HEADS_A * HEAD_DIM
WIDTH_B = HEADS_B * HEAD_DIM
CHUNK = 128
BLOCK = 128
DILATIONS = (4, 16)
ROPE_THETA = 10000.0
D_FF = 2816
FF_HALF = D_FF // 2
FF_STRIPS = ((0, 1024), (1024, 2048), (2048, D_FF))
PLE_DIM = 256
IN_COLS = 2 * WIDTH_A + 3 * WIDTH_B
EPS = 1e-6
LANES = 128
N_CHIPS = 4
N_DEV = 8

ADAM_LR = 0.001
ADAM_B1 = 0.9
ADAM_B2 = 0.999
ADAM_EPS = 1e-08
ADAM_WD = 0.01
ADAM_STEP = 10

VMEM_LIMIT = 56 * 1024 * 1024

WEIGHT_NAMES = ['mix_norm_g', 'w_in', 'sgu_w', 'sgu_b', 'sgu_norm_g', 'out_norm_a', 'out_norm_b', 'w_out',
                'ffn_norm_g', 'w_gate', 'w_up', 'w_down', 'ple_norm_g', 'w_ple_gate', 'w_ple_proj', 'final_norm_g']
SMALL = ['mix_norm_g', 'sgu_w', 'sgu_b', 'sgu_norm_g', 'out_norm_a', 'out_norm_b', 'ffn_norm_g', 'ple_norm_g',
         'final_norm_g']
SMALL_SIZES = {'mix_norm_g': 1024, 'sgu_w': 65536, 'sgu_b': 512, 'sgu_norm_g': 256, 'out_norm_a': 256,
               'out_norm_b': 768, 'ffn_norm_g': 1024, 'ple_norm_g': 1024, 'final_norm_g': 1024}
SMALL_ROWS = 72


def _params(semantics=None):
    return pltpu.CompilerParams(dimension_semantics=semantics, vmem_limit_bytes=VMEM_LIMIT)


def _full(shape):
    nd = len(shape)
    return pl.BlockSpec(shape, lambda i: (0,) * nd, pipeline_mode=pl.Buffered(1))


def _rows(tm, width):
    return pl.BlockSpec((tm, width), lambda i: (i, 0))


def _rms_stats(x):
    r = lax.rsqrt(jnp.mean(x * x, axis=-1, keepdims=True) + EPS)
    return x * r, r


def _rms_bwd(dn, n, r):
    return r * (dn - n * jnp.mean(dn * n, axis=-1, keepdims=True))


def _dot(a, b):
    return jnp.dot(a, b, preferred_element_type=F32)


def _dot_nt(a, b):
    return lax.dot_general(a, b, (((1,), (1,)), ((), ())), preferred_element_type=F32)


def _dot_tn(a, b):
    return lax.dot_general(a, b, (((0,), (0,)), ((), ())), preferred_element_type=F32)


def _gelu_parts(x):
    c = math.sqrt(2.0 / math.pi)
    t = jnp.tanh(c * (x + 0.044715 * x * x * x))
    return 0.5 * x * (1.0 + t), t


def _gelu_grad(x, t):
    c = math.sqrt(2.0 / math.pi)
    return 0.5 * (1.0 + t) + 0.5 * x * (1.0 - t * t) * c * (1.0 + 3.0 * 0.044715 * x * x)


def _half_masks(dtype):
    lane = lax.broadcasted_iota(jnp.int32, (BLOCK, LANES), 1)
    lo = (lane < HEAD_DIM).astype(F32)
    return lo.astype(dtype), (1.0 - lo).astype(dtype)


def _rope_partner(t):
    lane = lax.broadcasted_iota(jnp.int32, t.shape, 1)
    first_half = (lane % HEAD_DIM) < (HEAD_DIM // 2)
    return jnp.where(first_half, pltpu.roll(t, LANES - HEAD_DIM // 2, 1), pltpu.roll(t, HEAD_DIM // 2, 1))


PAIRS_ABREAST = 2
RESIDUES_PER_STEP = 4
L_BLOCK = 256
L_GROUP = 16


def _store_l256(scr, out_ref, cols, value, chunk_ref=None):
    tm = value.shape[0]
    half = L_GROUP // 2
    scr[...] = value
    for blk in range(tm // L_BLOCK):
        pieces = [scr[pl.ds(blk * L_BLOCK + r, L_GROUP, stride=L_GROUP), :] for r in range(L_GROUP)]
        for r, piece in enumerate(pieces):
            lo = blk * L_BLOCK + r * L_GROUP
            out_ref[lo:lo + L_GROUP, cols] = piece.astype(out_ref.dtype)
        if chunk_ref is not None:
            for chunk in range(L_BLOCK // BLOCK):
                for r in range(0, L_GROUP, 2):
                    lo = blk * L_BLOCK + chunk * BLOCK + r * half
                    both = [p[chunk * half:(chunk + 1) * half] for p in pieces[r:r + 2]]
                    chunk_ref[lo:lo + L_GROUP, cols] = jnp.concatenate(both, axis=0).astype(chunk_ref.dtype)


def _load_l256(col_refs, tm):
    cols = []
    for ref in col_refs:
        pieces = [ref[pl.ds(blk * L_BLOCK + i, L_GROUP, stride=L_GROUP), :]
                  for blk in range(tm // L_BLOCK) for i in range(L_GROUP)]
        cols.append(jnp.concatenate(pieces, axis=0))
    return jnp.concatenate(cols, axis=1)


def _col_specs(tm, width):
    return [pl.BlockSpec((tm, LANES), lambda i, j=j: (i, j)) for j in range(width // LANES)]


def _in_fwd(x, g_mix, w_in_t, cos_t, sin_t):
    s = x.shape[0]
    tm = 512

    def body(x_ref, g_ref, wt_ref, cos_ref, sin_ref, uv_ref, q_ref, k_ref, v_ref, q1_ref, k1_ref, v1_ref, hn_ref,
             *scrs):
        n, _ = _rms_stats(x_ref[...])
        hn = (n * g_ref[...]).astype(MXU_DTYPE)
        hn_ref[...] = hn
        cos = cos_ref[...]
        sin = sin_ref[...]
        strip = 2 * LANES
        for j in range(IN_COLS // strip):
            proj = _dot_nt(hn, wt_ref[j * strip:(j + 1) * strip, :])
            lo = j * strip - 2 * WIDTH_A
            if lo < 0:
                uv_ref[:, j * strip:(j + 1) * strip] = proj
                continue
            which, lo = divmod(lo, WIDTH_B)
            for i in range(strip // LANES):
                t = proj[:, i * LANES:(i + 1) * LANES]
                cols = slice(lo + i * LANES, lo + (i + 1) * LANES)
                scr = scrs[i]
                if which == 0:
                    _store_l256(scr, q_ref, cols, (t * cos + _rope_partner(t) * sin) * (HEAD_DIM ** -0.5), q1_ref)
                elif which == 1:
                    _store_l256(scr, k_ref, cols, t * cos + _rope_partner(t) * sin, k1_ref)
                else:
                    _store_l256(scr, v_ref, cols, t, v1_ref)

    return pl.pallas_call(
        body, name="in_fwd", grid=(s // tm,), scratch_shapes=[pltpu.VMEM((tm, LANES), F32)] * 2,
        in_specs=[_rows(tm, D_MODEL), _full((1, D_MODEL)), _full((IN_COLS, D_MODEL)), _rows(tm, LANES),
                  _rows(tm, LANES)],
        out_specs=[_rows(tm, 2 * WIDTH_A)] + [_rows(tm, WIDTH_B)] * 6 + [_rows(tm, D_MODEL)],
        out_shape=[jax.ShapeDtypeStruct((s, 2 * WIDTH_A), F32)] + [jax.ShapeDtypeStruct((s, WIDTH_B), MXU_DTYPE)] * 6
        + [jax.ShapeDtypeStruct((s, D_MODEL), MXU_DTYPE)],
        compiler_params=_params(("arbitrary",)),
    )(x, g_mix, w_in_t, cos_t, sin_t)


class _Branch:
    def __init__(self, dil, s):
        i = np.arange(L_GROUP)
        self.res = RESIDUES_PER_STEP
        if dil == 16:
            nblk = BLOCK // 16
            self.grid = (16 // self.res, s // (L_BLOCK * nblk))
            self.shape = (nblk, 1, self.res, L_GROUP)
            self.index = lambda r, n: (n, r // (4 // self.res), r % (4 // self.res), 0, 0)
            pos = (np.arange(nblk)[:, None] * 16 + i[None, :]).reshape(-1)
        else:
            nblk = BLOCK // 64
            self.grid = (4 // self.res, s // (L_BLOCK * nblk))
            self.shape = (nblk, 4, self.res, L_GROUP)
            self.index = lambda r, n: (n, 0, r, 0, 0)
            pos = (np.arange(nblk)[:, None, None] * 64 + np.arange(4)[None, :, None]
                   + 4 * i[None, None, :]).reshape(-1)
        self.qn = pos.shape[0]
        self.nb = self.grid[1]
        dist = pos[:, None] - np.concatenate([pos - self.qn, pos])[None, :]
        band = (dist >= 0) & (dist <= BLOCK)
        start = band & (np.arange(2 * self.qn)[None, :] >= self.qn)
        self.bias = np.where(np.stack([band, start]), 0.0, -np.inf).astype(np.float32)

    def view(self, a):
        return a.reshape(a.shape[0] // L_BLOCK, 4, 4, L_GROUP, a.shape[1])

    def spec(self, w, step=lambda n: n):
        return pl.BlockSpec(self.shape + (w,), lambda r, n: self.index(r, step(n)))

    def bias_spec(self, step=lambda n: n):
        return pl.BlockSpec((1, self.qn, 2 * self.qn), lambda r, n: (jnp.where(step(n) == 0, 1, 0), 0, 0))

    def load(self, ref, cols=slice(None), j=0):
        x = ref[:, :, j, :, cols]
        return x.reshape(self.qn, x.shape[-1])

    def store(self, ref, cols, value, j=0):
        ref[:, :, j, :, cols] = value.reshape(self.shape[:2] + (L_GROUP, value.shape[-1]))


def _attn_fwd_branch(q, k, v, dil, earlier=()):
    s = q.shape[0]
    br = _Branch(dil, s)
    qn = br.qn
    nearly = len(earlier)

    def body(bias_ref, q_ref, kc_ref, kp_ref, vc_ref, vp_ref, *refs):
        early_refs, (o_ref, lse_ref) = refs[:2 * nearly], refs[2 * nearly:]
        bias2 = jnp.concatenate([bias_ref[0], bias_ref[0]], axis=0)
        lo = lax.broadcasted_iota(jnp.int32, (qn, LANES), 1) < HEAD_DIM
        mask_lo = lo.astype(F32).astype(MXU_DTYPE)
        for j, hp in itertools.product(range(br.res), range(HEADS_B // 2)):
            cols = slice(hp * LANES, (hp + 1) * LANES)
            qp = br.load(q_ref, cols, j)
            kcat = jnp.concatenate([br.load(kp_ref, cols, j), br.load(kc_ref, cols, j)], axis=0)
            vcat = jnp.concatenate([br.load(vp_ref, cols, j), br.load(vc_ref, cols, j)], axis=0)
            sc = _dot_nt(jnp.concatenate([qp * mask_lo, qp * (1 - mask_lo)], axis=0), kcat) + bias2
            m = jnp.max(sc, axis=1, keepdims=True)
            p = jnp.exp(sc - m)
            l = jnp.sum(p, axis=1, keepdims=True)
            out = _dot(p.astype(MXU_DTYPE), vcat) / l
            lse = m + jnp.log(l)
            outs = [br.load(r, cols, j) for r in early_refs[:nearly]] + [jnp.where(lo, out[:qn], out[qn:])]
            lses = [br.load(r, cols, j) for r in early_refs[nearly:]] + [jnp.where(lo, lse[:qn], lse[qn:])]
            if nearly:
                top = functools.reduce(jnp.maximum, lses)
                ws = [jnp.exp(x - top) for x in lses]
                den = functools.reduce(jnp.add, ws)
                outs = [functools.reduce(jnp.add, [w * o for w, o in zip(ws, outs)]) / den]
                lses = [top + jnp.log(den)]
            br.store(o_ref, cols, outs[0], j)
            br.store(lse_ref, cols, lses[0], j)

    before = lambda n: jnp.maximum(n - 1, 0)
    res = pl.pallas_call(
        body, name="attn_fwd_d%d" % dil, grid=br.grid,
        in_specs=[br.bias_spec(), br.spec(WIDTH_B), br.spec(WIDTH_B), br.spec(WIDTH_B, before), br.spec(WIDTH_B),
                  br.spec(WIDTH_B, before)] + [br.spec(WIDTH_B)] * (2 * nearly),
        out_specs=[br.spec(WIDTH_B), br.spec(WIDTH_B)],
        out_shape=[jax.ShapeDtypeStruct((s // L_BLOCK, 4, 4, L_GROUP, WIDTH_B), F32)] * 2,
        compiler_params=_params(("arbitrary", "arbitrary")),
    )(jnp.asarray(br.bias), br.view(q), br.view(k), br.view(k), br.view(v), br.view(v),
      *[br.view(o) for o, _ in earlier], *[br.view(x) for _, x in earlier])
    return tuple(a.reshape(s, WIDTH_B) for a in res)


LOCAL_CHUNKS = 4


def _local_bias():
    row = np.arange(BLOCK)
    pos = L_GROUP * (row % (L_GROUP // 2)) + row // (L_GROUP // 2)
    dist = pos[:, None] - np.concatenate([pos - BLOCK, pos])[None, :]
    band = (dist >= 0) & (dist <= BLOCK)
    start = band & (np.arange(2 * BLOCK)[None, :] >= BLOCK)
    return np.where(np.stack([band, start]), 0.0, -np.inf).astype(np.float32)


def _chunk_view(a):
    return a.reshape(a.shape[0] // L_BLOCK, L_GROUP, 2, L_GROUP // 2, a.shape[1])


def _chunk_of(ref, j, cols=slice(None)):
    x = ref[j // 2, :, j % 2, :, cols]
    return x.reshape(BLOCK, x.shape[-1])


def _put_chunk(ref, j, cols, value):
    ref[j // 2, :, j % 2, :, cols] = value.reshape(L_GROUP, L_GROUP // 2, value.shape[-1])


def _local_keys(cur_ref, before_ref, j, cols):
    here = slice(j * BLOCK, (j + 1) * BLOCK)
    before = before_ref[:, cols] if j == 0 else cur_ref[(j - 1) * BLOCK:j * BLOCK, cols]
    return jnp.concatenate([before, cur_ref[here, cols]], axis=0)


def _local_specs(s, step=lambda n: n):
    rows = LOCAL_CHUNKS * BLOCK
    cur = pl.BlockSpec((rows, WIDTH_B), lambda n: (step(n), 0))
    before = pl.BlockSpec((BLOCK, WIDTH_B), lambda n: (jnp.maximum(LOCAL_CHUNKS * step(n) - 1, 0), 0))
    return [cur, cur, before, cur, before]


def _attn_fwd_local(q1, k1, v1):
    s = q1.shape[0]
    rows = LOCAL_CHUNKS * BLOCK
    qn = BLOCK

    def body(bias_ref, q_ref, kc_ref, kp_ref, vc_ref, vp_ref, o_ref, lse_ref):
        first = jnp.where(pl.program_id(0) == 0, bias_ref[1], bias_ref[0])
        biases = [jnp.concatenate([b, b], axis=0) for b in (first, bias_ref[0])]
        lo = lax.broadcasted_iota(jnp.int32, (qn, LANES), 1) < HEAD_DIM
        mask_lo = lo.astype(F32).astype(MXU_DTYPE)
        for j, hp in itertools.product(range(LOCAL_CHUNKS), range(HEADS_B // 2)):
            cols = slice(hp * LANES, (hp + 1) * LANES)
            qp = q_ref[j * BLOCK:(j + 1) * BLOCK, cols]
            kcat = _local_keys(kc_ref, kp_ref, j, cols)
            vcat = _local_keys(vc_ref, vp_ref, j, cols)
            sc = _dot_nt(jnp.concatenate([qp * mask_lo, qp * (1 - mask_lo)], axis=0), kcat) + biases[min(j, 1)]
            m = jnp.max(sc, axis=1, keepdims=True)
            p = jnp.exp(sc - m)
            l = jnp.sum(p, axis=1, keepdims=True)
            out = _dot(p.astype(MXU_DTYPE), vcat) / l
            lse = m + jnp.log(l)
            _put_chunk(o_ref, j, cols, jnp.where(lo, out[:qn], out[qn:]))
            _put_chunk(lse_ref, j, cols, jnp.where(lo, lse[:qn], lse[qn:]))

    out_spec = pl.BlockSpec((LOCAL_CHUNKS // 2, L_GROUP, 2, L_GROUP // 2, WIDTH_B), lambda n: (n, 0, 0, 0, 0))
    res = pl.pallas_call(
        body, name="attn_fwd_d1", grid=(s // rows,),
        in_specs=[_full((2, BLOCK, 2 * BLOCK))] + _local_specs(s), out_specs=[out_spec] * 2,
        out_shape=[jax.ShapeDtypeStruct((s // L_BLOCK, L_GROUP, 2, L_GROUP // 2, WIDTH_B), F32)] * 2,
        compiler_params=_params(("arbitrary",)),
    )(jnp.asarray(_local_bias()), q1, k1, k1, v1, v1)
    return tuple(a.reshape(s, WIDTH_B) for a in res)


def _attn_bwd_local(q1, k1, v1, do, o, lse):
    s = q1.shape[0]
    rows = LOCAL_CHUNKS * BLOCK
    nsteps = s // rows
    qn = BLOCK

    def body(bias_ref, q_ref, kc_ref, kp_ref, vc_ref, vp_ref, do_ref, o_ref, lse_ref, dq_ref, dk_ref, dv_ref,
             dk_buf, dv_buf):
        n = pl.program_id(0)

        @pl.when(n == 0)
        def _():
            dk_buf[...] = jnp.zeros_like(dk_buf)
            dv_buf[...] = jnp.zeros_like(dv_buf)

        @pl.when(n < nsteps)
        def _():
            first = jnp.where(n == 0, bias_ref[1], bias_ref[0])
            biases = [jnp.concatenate([b, b], axis=0) for b in (first, bias_ref[0])]
            lo = lax.broadcasted_iota(jnp.int32, (qn, LANES), 1) < HEAD_DIM
            mask_f = lo.astype(F32)
            mask_lo = mask_f.astype(MXU_DTYPE)
            dk_buf[LOCAL_CHUNKS:] = jnp.zeros((LOCAL_CHUNKS, qn, WIDTH_B), F32)
            dv_buf[LOCAL_CHUNKS:] = jnp.zeros((LOCAL_CHUNKS, qn, WIDTH_B), F32)

            def prepare(j, hp):
                cols = slice(hp * LANES, (hp + 1) * LANES)
                qp = q_ref[j * BLOCK:(j + 1) * BLOCK, cols]
                dop = _chunk_of(do_ref, j, cols)
                prod = dop * _chunk_of(o_ref, j, cols)
                prod_lo = prod * mask_f
                lse = _chunk_of(lse_ref, j, cols)
                return dict(
                    j=j, cols=cols, kcat=_local_keys(kc_ref, kp_ref, j, cols), vcat=_local_keys(vc_ref, vp_ref, j, cols),
                    qs=jnp.concatenate([qp * mask_lo, qp * (1 - mask_lo)], axis=0),
                    dos=jnp.concatenate([dop * mask_f, dop * (1.0 - mask_f)], axis=0).astype(MXU_DTYPE),
                    delta=jnp.concatenate([jnp.sum(prod_lo, axis=1, keepdims=True),
                                           jnp.sum(prod - prod_lo, axis=1, keepdims=True)], axis=0),
                    lse2=jnp.concatenate([lse[:, :1], lse[:, HEAD_DIM:HEAD_DIM + 1]], axis=0))

            def scores(t):
                t['sc'] = _dot_nt(t['qs'], t['kcat'])
                t['dp'] = _dot_nt(t['dos'], t['vcat'])

            def softmax(t):
                p = jnp.exp(t['sc'] + biases[min(t['j'], 1)] - t['lse2'])
                t['ds'] = (p * (t['dp'] - t['delta'])).astype(MXU_DTYPE)
                t['p'] = p.astype(MXU_DTYPE)

            def gradients(t):
                t['dvc'] = _dot_tn(t['p'], t['dos'])
                t['dkc'] = _dot_tn(t['ds'], t['qs'])
                t['dq2'] = _dot(t['ds'], t['kcat'])

            def store(t):
                j, cols = t['j'], t['cols']
                _put_chunk(dq_ref, j, cols, jnp.where(lo, t['dq2'][:qn], t['dq2'][qn:]))
                for buf, both in ((dk_buf, t['dkc']), (dv_buf, t['dvc'])):
                    buf[LOCAL_CHUNKS + j - 1, :, cols] += both[:qn]
                    buf[LOCAL_CHUNKS + j, :, cols] += both[qn:]

            for j, first_pair in itertools.product(range(LOCAL_CHUNKS), range(0, HEADS_B // 2, PAIRS_ABREAST)):
                group = [prepare(j, hp) for hp in range(first_pair, first_pair + PAIRS_ABREAST)]
                for stage in (scores, softmax, gradients, store):
                    for t in group:
                        stage(t)

        for j in range(LOCAL_CHUNKS):
            _put_chunk(dk_ref, j, slice(None), dk_buf[j])
            _put_chunk(dv_ref, j, slice(None), dv_buf[j])
        dk_buf[:LOCAL_CHUNKS] = dk_buf[LOCAL_CHUNKS:]
        dv_buf[:LOCAL_CHUNKS] = dv_buf[LOCAL_CHUNKS:]

    cur = lambda n: jnp.minimum(n, nsteps - 1)
    late = lambda n: jnp.maximum(n - 1, 0)
    view_spec = lambda step: pl.BlockSpec((LOCAL_CHUNKS // 2, L_GROUP, 2, L_GROUP // 2, WIDTH_B),
                                          lambda n: (step(n), 0, 0, 0, 0))
    res = pl.pallas_call(
        body, name="attn_bwd_d1", grid=(nsteps + 1,),
        in_specs=[_full((2, BLOCK, 2 * BLOCK))] + _local_specs(s, cur) + [view_spec(cur)] * 3,
        out_specs=[view_spec(cur), view_spec(late), view_spec(late)],
        out_shape=[jax.ShapeDtypeStruct((s // L_BLOCK, L_GROUP, 2, L_GROUP // 2, WIDTH_B), F32)] * 3,
        scratch_shapes=[pltpu.VMEM((2 * LOCAL_CHUNKS, qn, WIDTH_B), F32)] * 2,
        compiler_params=_params(("arbitrary",)),
    )(jnp.asarray(_local_bias()), q1, k1, k1, v1, v1, _chunk_view(do), _chunk_view(o), _chunk_view(lse))
    return tuple(a.reshape(s, WIDTH_B) for a in res)


def _sgu_forward_tile(uv, w_ref, bias, g_sgu):
    tm = uv.shape[0]
    u = uv[:, :WIDTH_A]
    v = uv[:, WIDTH_A:]
    ug, tu = _gelu_parts(u)
    vg, tv = _gelu_parts(v)
    mu = jnp.mean(vg, axis=-1, keepdims=True)
    vc = vg - mu
    rs = lax.rsqrt(jnp.mean(vc * vc, axis=-1, keepdims=True) + EPS)
    vhat = vc * rs
    vn = (vhat * g_sgu).astype(MXU_DTYPE)
    masks = _half_masks(MXU_DTYPE)
    chunks = []
    for c in range(tm // CHUNK):
        rows = slice(c * CHUNK, (c + 1) * CHUNK)
        groups = []
        for gp in range(2):
            vn_g = vn[rows, gp * LANES:(gp + 1) * LANES]
            groups.append(_dot(w_ref[2 * gp], vn_g * masks[0]) + _dot(w_ref[2 * gp + 1], vn_g * masks[1]))
        chunks.append(jnp.concatenate(groups, axis=1) + bias)
    mixed = jnp.concatenate(chunks, axis=0)
    return dict(u=u, v=v, ug=ug, tu=tu, tv=tv, rs=rs, vhat=vhat, vn=vn, mixed=mixed, ya=ug * mixed)


def _sgu_fwd(uv, w_tril, bias, g_sgu, g_a):
    s = uv.shape[0]
    tm = 512

    def body(uv_ref, w_ref, b_ref, gs_ref, ga_ref, o_ref):
        t = _sgu_forward_tile(uv_ref[...], w_ref, b_ref[...], gs_ref[...])
        n, _ = _rms_stats(t['ya'])
        o_ref[...] = (n * ga_ref[...]).astype(MXU_DTYPE)

    return pl.pallas_call(
        body, name="sgu_fwd", grid=(s // tm,),
        in_specs=[_rows(tm, 2 * WIDTH_A), _full((HEADS_A, CHUNK, CHUNK)), _full((CHUNK, WIDTH_A)),
                  _full((1, WIDTH_A)), _full((1, WIDTH_A))],
        out_specs=_rows(tm, WIDTH_A), out_shape=jax.ShapeDtypeStruct((s, WIDTH_A), MXU_DTYPE),
        compiler_params=_params(("arbitrary",)),
    )(uv, w_tril, bias, g_sgu, g_a)


def _out_fwd(ya_n, y_b, g_b, w_out, x):
    s = x.shape[0]
    tm = 512
    nc = WIDTH_B // LANES

    def body(ya_ref, *refs):
        yb_refs = refs[:nc]
        g_ref, w_ref, x_ref, h_ref, yn_ref = refs[nc:]
        n, _ = _rms_stats(_load_l256(yb_refs, tm))
        yn = jnp.concatenate([ya_ref[...], (n * g_ref[...]).astype(MXU_DTYPE)], axis=1)
        yn_ref[...] = yn
        h_ref[...] = x_ref[...] + _dot(yn, w_ref[...])

    return pl.pallas_call(
        body, name="out_fwd", grid=(s // tm,),
        in_specs=[_rows(tm, WIDTH_A)] + _col_specs(tm, WIDTH_B) + [_full((1, WIDTH_B)), _full((D_MODEL, D_MODEL)),
                                                                 _rows(tm, D_MODEL)],
        out_specs=[_rows(tm, D_MODEL), _rows(tm, D_MODEL)],
        out_shape=[jax.ShapeDtypeStruct((s, D_MODEL), F32), jax.ShapeDtypeStruct((s, D_MODEL), MXU_DTYPE)],
        compiler_params=_params(("arbitrary",)),
    )(ya_n, *([y_b] * nc), g_b, w_out, x)


def _ffn_fwd(h1, g_ffn, w_gate_t, w_up_t, w_down):
    s = h1.shape[0]
    tm = 512

    def body(h_ref, g_ref, wgt_ref, wut_ref, wd_ref, o_ref, gate_ref, up_ref, hn_ref):
        h = h_ref[...]
        n, _ = _rms_stats(h)
        hn = (n * g_ref[...]).astype(MXU_DTYPE)
        hn_ref[...] = hn
        strips = [dict(cols=slice(lo, hi)) for lo, hi in FF_STRIPS]

        def project(t):
            t['gate'] = _dot_nt(hn, wgt_ref[t['cols'], :])
            t['up'] = _dot_nt(hn, wut_ref[t['cols'], :])

        def activate(t):
            gate, up = t['gate'], t['up']
            gate_ref[:, t['cols']] = gate.astype(MXU_DTYPE)
            up_ref[:, t['cols']] = up.astype(MXU_DTYPE)
            t['act'] = (gate * jax.nn.sigmoid(gate) * up).astype(MXU_DTYPE)

        def down(t):
            return _dot(t['act'], wd_ref[t['cols'], :])

        out = h
        project(strips[0])
        for i, t in enumerate(strips):
            if i + 1 < len(strips):
                project(strips[i + 1])
            activate(t)
            out = out + down(t)
        o_ref[...] = out

    return pl.pallas_call(
        body, name="ffn_fwd", grid=(s // tm,),
        in_specs=[_rows(tm, D_MODEL), _full((1, D_MODEL)), _full((D_FF, D_MODEL)), _full((D_FF, D_MODEL)),
                  _full((D_FF, D_MODEL))],
        out_specs=[_rows(tm, D_MODEL), _rows(tm, D_FF), _rows(tm, D_FF), _rows(tm, D_MODEL)],
        out_shape=[jax.ShapeDtypeStruct((s, D_MODEL), F32), jax.ShapeDtypeStruct((s, D_FF), MXU_DTYPE),
                   jax.ShapeDtypeStruct((s, D_FF), MXU_DTYPE), jax.ShapeDtypeStruct((s, D_MODEL), MXU_DTYPE)],
        compiler_params=_params(("arbitrary",)),
    )(h1, g_ffn, w_gate_t, w_up_t, w_down)


def _ple_loss(h2, p, target, g_ple, w_pg, w_pp_t, g_final):
    s = h2.shape[0]
    tm = 256

    def body(h_ref, p_ref, t_ref, gp_ref, wg_ref, wpt_ref, gf_ref,
             loss_ref, dh_ref, dz_ref, dpp_ref, hn_ref, dgp_ref, dgf_ref):
        @pl.when(pl.program_id(0) == 0)
        def _():
            loss_ref[...] = jnp.zeros_like(loss_ref)
            dgp_ref[...] = jnp.zeros_like(dgp_ref)
            dgf_ref[...] = jnp.zeros_like(dgf_ref)

        h2t = h_ref[...]
        n2, r2 = _rms_stats(h2t)
        hn = (n2 * gp_ref[...]).astype(MXU_DTYPE)
        hn_ref[...] = hn
        gate = jax.nn.sigmoid(_dot(hn, wg_ref[...]))
        pp = _dot_nt(p_ref[...].astype(MXU_DTYPE), wpt_ref[...])
        h3 = h2t + gate * pp
        n3, r3 = _rms_stats(h3)
        diff = n3 * gf_ref[...] - t_ref[...]
        loss_ref[...] += jnp.full(loss_ref.shape, 0.5 * jnp.sum(diff * diff) / D_MODEL, F32)
        dy = diff * (1.0 / D_MODEL)
        dgf_ref[...] += jnp.sum(dy * n3, axis=0, keepdims=True)
        dh3 = _rms_bwd(dy * gf_ref[...], n3, r3)
        dpp_ref[...] = (dh3 * gate).astype(MXU_DTYPE)
        dz = (dh3 * pp * gate * (1.0 - gate)).astype(MXU_DTYPE)
        dz_ref[...] = dz
        dhn = _dot_nt(dz, wg_ref[...])
        dgp_ref[...] += jnp.sum(dhn * n2, axis=0, keepdims=True)
        dh_ref[...] = dh3 + _rms_bwd(dhn * gp_ref[...], n2, r2)

    return pl.pallas_call(
        body, name="ple_loss", grid=(s // tm,),
        in_specs=[_rows(tm, D_MODEL), _rows(tm, PLE_DIM), _rows(tm, D_MODEL), _full((1, D_MODEL)),
                  _full((D_MODEL, D_MODEL)), _full((D_MODEL, PLE_DIM)), _full((1, D_MODEL))],
        out_specs=[_full((1, LANES)), _rows(tm, D_MODEL), _rows(tm, D_MODEL), _rows(tm, D_MODEL),
                   _rows(tm, D_MODEL), _full((1, D_MODEL)), _full((1, D_MODEL))],
        out_shape=[jax.ShapeDtypeStruct((1, LANES), F32), jax.ShapeDtypeStruct((s, D_MODEL), F32),
                   jax.ShapeDtypeStruct((s, D_MODEL), MXU_DTYPE), jax.ShapeDtypeStruct((s, D_MODEL), MXU_DTYPE),
                   jax.ShapeDtypeStruct((s, D_MODEL), MXU_DTYPE), jax.ShapeDtypeStruct((1, D_MODEL), F32),
                   jax.ShapeDtypeStruct((1, D_MODEL), F32)],
        compiler_params=_params(("arbitrary",)),
    )(h2, p, target, g_ple, w_pg, w_pp_t, g_final)


def _ffn_bwd(dh2, h1, gate, up, g_ffn, w_down, w_gate_t, w_up_t):
    s = h1.shape[0]
    tm = 256

    def body(dh_ref, h_ref, gate_ref, up_ref, g_ref, wd_ref, wgt_ref, wut_ref,
             o_ref, act_ref, dg_ref, du_ref, dgn_ref):
        @pl.when(pl.program_id(0) == 0)
        def _():
            dgn_ref[...] = jnp.zeros_like(dgn_ref)

        dh = dh_ref[...]
        dhb = dh.astype(MXU_DTYPE)
        strips = [dict(cols=slice(lo, hi)) for lo, hi in FF_STRIPS]

        def back_down(t):
            t['dact'] = _dot_nt(dhb, wd_ref[t['cols'], :])

        def back_act(t):
            cols, dact = t['cols'], t['dact']
            g = gate_ref[:, cols].astype(F32)
            u = up_ref[:, cols].astype(F32)
            sg = jax.nn.sigmoid(g)
            silu = g * sg
            act_ref[:, cols] = (silu * u).astype(MXU_DTYPE)
            t['du'] = (dact * silu).astype(MXU_DTYPE)
            t['dg'] = (dact * u * sg * (1.0 + g * (1.0 - sg))).astype(MXU_DTYPE)
            du_ref[:, cols] = t['du']
            dg_ref[:, cols] = t['dg']

        def back_in(t):
            return _dot(t['dg'], wgt_ref[t['cols'], :]) + _dot(t['du'], wut_ref[t['cols'], :])

        dhn = jnp.zeros((tm, D_MODEL), F32)
        back_down(strips[0])
        for i, t in enumerate(strips):
            if i + 1 < len(strips):
                back_down(strips[i + 1])
            back_act(t)
            dhn = dhn + back_in(t)
        n, r = _rms_stats(h_ref[...])
        dgn_ref[...] += jnp.sum(dhn * n, axis=0, keepdims=True)
        o_ref[...] = dh + _rms_bwd(dhn * g_ref[...], n, r)

    return pl.pallas_call(
        body, name="ffn_bwd", grid=(s // tm,),
        in_specs=[_rows(tm, D_MODEL), _rows(tm, D_MODEL), _rows(tm, D_FF), _rows(tm, D_FF), _full((1, D_MODEL)),
                  _full((D_FF, D_MODEL)), _full((D_FF, D_MODEL)), _full((D_FF, D_MODEL))],
        out_specs=[_rows(tm, D_MODEL), _rows(tm, D_FF), _rows(tm, D_FF), _rows(tm, D_FF), _full((1, D_MODEL))],
        out_shape=[jax.ShapeDtypeStruct((s, D_MODEL), F32), jax.ShapeDtypeStruct((s, D_FF), MXU_DTYPE),
                   jax.ShapeDtypeStruct((s, D_FF), MXU_DTYPE), jax.ShapeDtypeStruct((s, D_FF), MXU_DTYPE),
                   jax.ShapeDtypeStruct((1, D_MODEL), F32)],
        compiler_params=_params(("arbitrary",)),
    )(dh2, h1, gate, up, g_ffn, w_down, w_gate_t, w_up_t)


def _out_bwd(dh1, y_b, g_b, w_out):
    s = dh1.shape[0]
    tm = 512
    nc = WIDTH_B // LANES

    def body(dh_ref, *refs):
        yb_refs = refs[:nc]
        g_ref, w_ref, dya_ref, dyb_ref, dg_ref, scr = refs[nc:]

        @pl.when(pl.program_id(0) == 0)
        def _():
            dg_ref[...] = jnp.zeros_like(dg_ref)

        dy = _dot_nt(dh_ref[...].astype(MXU_DTYPE), w_ref[...])
        dya_ref[...] = dy[:, :WIDTH_A]
        dyb = dy[:, WIDTH_A:]
        n, r = _rms_stats(_load_l256(yb_refs, tm))
        dg_ref[...] += jnp.sum(dyb * n, axis=0, keepdims=True)
        dyb_in = _rms_bwd(dyb * g_ref[...], n, r)
        for j in range(nc):
            cols = slice(j * LANES, (j + 1) * LANES)
            _store_l256(scr, dyb_ref, cols, dyb_in[:, cols])

    return pl.pallas_call(
        body, name="out_bwd", grid=(s // tm,), scratch_shapes=[pltpu.VMEM((tm, LANES), F32)],
        in_specs=[_rows(tm, D_MODEL)] + _col_specs(tm, WIDTH_B) + [_full((1, WIDTH_B)), _full((D_MODEL, D_MODEL))],
        out_specs=[_rows(tm, WIDTH_A), _rows(tm, WIDTH_B), _full((1, WIDTH_B))],
        out_shape=[jax.ShapeDtypeStruct((s, WIDTH_A), F32), jax.ShapeDtypeStruct((s, WIDTH_B), F32),
                   jax.ShapeDtypeStruct((1, WIDTH_B), F32)],
        compiler_params=_params(("arbitrary",)),
    )(dh1, *([y_b] * nc), g_b, w_out)


def _attn_bwd_branch(q, k, v, do, o, lse, grads, dil):
    s = q.shape[0]
    br = _Branch(dil, s)
    qn, nb = br.qn, br.nb
    first = grads is None

    def body(*refs):
        bias_ref, q_ref, kc_ref, kp_ref, vc_ref, vp_ref, do_ref, o_ref, lse_ref = refs[:9]
        if first:
            rest = refs[9:]
        else:
            dq_in, dk_in, dv_in = refs[9:12]
            rest = refs[12:]
        dq_ref, dk_ref, dv_ref, dk_carry, dv_carry = rest
        n = pl.program_id(1)

        @pl.when(n == 0)
        def _():
            dk_carry[...] = jnp.zeros_like(dk_carry)
            dv_carry[...] = jnp.zeros_like(dv_carry)

        @pl.when(n < nb)
        def _():
            bias2 = jnp.concatenate([bias_ref[0], bias_ref[0]], axis=0)
            lane = lax.broadcasted_iota(jnp.int32, (qn, LANES), 1)
            lo = lane < HEAD_DIM
            mask_f = lo.astype(F32)
            mask_lo = mask_f.astype(MXU_DTYPE)
            def prepare(j, hp):
                cols = slice(hp * LANES, (hp + 1) * LANES)
                qp = br.load(q_ref, cols, j)
                dop = br.load(do_ref, cols, j)
                prod = dop * br.load(o_ref, cols, j)
                prod_lo = prod * mask_f
                lse = br.load(lse_ref, cols, j)
                return dict(
                    j=j, cols=cols,
                    kcat=jnp.concatenate([br.load(kp_ref, cols, j), br.load(kc_ref, cols, j)], axis=0),
                    vcat=jnp.concatenate([br.load(vp_ref, cols, j), br.load(vc_ref, cols, j)], axis=0),
                    qs=jnp.concatenate([qp * mask_lo, qp * (1 - mask_lo)], axis=0),
                    dos=jnp.concatenate([dop * mask_f, dop * (1.0 - mask_f)], axis=0).astype(MXU_DTYPE),
                    delta=jnp.concatenate([jnp.sum(prod_lo, axis=1, keepdims=True),
                                           jnp.sum(prod - prod_lo, axis=1, keepdims=True)], axis=0),
                    lse2=jnp.concatenate([lse[:, :1], lse[:, HEAD_DIM:HEAD_DIM + 1]], axis=0))

            def scores(t):
                t['sc'] = _dot_nt(t['qs'], t['kcat'])
                t['dp'] = _dot_nt(t['dos'], t['vcat'])

            def softmax(t):
                p = jnp.exp(t['sc'] + bias2 - t['lse2'])
                t['ds'] = (p * (t['dp'] - t['delta'])).astype(MXU_DTYPE)
                t['p'] = p.astype(MXU_DTYPE)

            def gradients(t):
                t['dvc'] = _dot_tn(t['p'], t['dos'])
                t['dkc'] = _dot_tn(t['ds'], t['qs'])
                t['dq2'] = _dot(t['ds'], t['kcat'])

            def store(t):
                j, cols, dkc, dvc = t['j'], t['cols'], t['dkc'], t['dvc']
                dq = jnp.where(lo, t['dq2'][:qn], t['dq2'][qn:])
                dk_prev = dk_carry[j, :, cols] + dkc[:qn]
                dv_prev = dv_carry[j, :, cols] + dvc[:qn]
                if not first:
                    dq = dq + br.load(dq_in, cols, j)
                    dk_prev = dk_prev + br.load(dk_in, cols, j)
                    dv_prev = dv_prev + br.load(dv_in, cols, j)
                br.store(dq_ref, cols, dq, j)
                br.store(dk_ref, cols, dk_prev, j)
                br.store(dv_ref, cols, dv_prev, j)
                dk_carry[j, :, cols] = dkc[qn:]
                dv_carry[j, :, cols] = dvc[qn:]

            for j, first_pair in itertools.product(range(br.res), range(0, HEADS_B // 2, PAIRS_ABREAST)):
                group = [prepare(j, hp) for hp in range(first_pair, first_pair + PAIRS_ABREAST)]
                for stage in (scores, softmax, gradients, store):
                    for t in group:
                        stage(t)

        @pl.when(n == nb)
        def _():
            for j in range(br.res):
                dk_last = dk_carry[j]
                dv_last = dv_carry[j]
                if not first:
                    dk_last = dk_last + br.load(dk_in, slice(None), j)
                    dv_last = dv_last + br.load(dv_in, slice(None), j)
                br.store(dk_ref, slice(None), dk_last, j)
                br.store(dv_ref, slice(None), dv_last, j)

    cur = lambda n: jnp.minimum(n, nb - 1)
    before = lambda n: jnp.maximum(cur(n) - 1, 0)
    late = lambda n: jnp.maximum(n - 1, 0)
    in_specs = [br.bias_spec(cur), br.spec(WIDTH_B, cur), br.spec(WIDTH_B, cur), br.spec(WIDTH_B, before),
                br.spec(WIDTH_B, cur), br.spec(WIDTH_B, before), br.spec(WIDTH_B, cur), br.spec(WIDTH_B, cur),
                br.spec(WIDTH_B, cur)]
    args = [jnp.asarray(br.bias)] + [br.view(a) for a in (q, k, k, v, v, do, o, lse)]
    if not first:
        in_specs += [br.spec(WIDTH_B, cur), br.spec(WIDTH_B, late), br.spec(WIDTH_B, late)]
        args += [br.view(g) for g in grads]
    res = pl.pallas_call(
        body, name="attn_bwd_d%d" % dil, grid=(br.grid[0], nb + 1), in_specs=in_specs,
        out_specs=[br.spec(WIDTH_B, cur), br.spec(WIDTH_B, late), br.spec(WIDTH_B, late)],
        out_shape=[jax.ShapeDtypeStruct((s // L_BLOCK, 4, 4, L_GROUP, WIDTH_B), F32)] * 3,
        scratch_shapes=[pltpu.VMEM((br.res, qn, WIDTH_B), F32), pltpu.VMEM((br.res, qn, WIDTH_B), F32)],
        compiler_params=_params(("arbitrary", "arbitrary")),
    )(*args)
    return tuple(a.reshape(s, WIDTH_B) for a in res)


def _sgu_bwd(uv, dya_n, w_tril, w_tril_t, bias, g_sgu, g_a):
    s = uv.shape[0]
    tm = 512

    def body(uv_ref, dy_ref, w_ref, wt_ref, b_ref, gs_ref, ga_ref, duv_ref, dw_ref, db_ref, dgs_ref, dga_ref,
             db_acc):
        i = pl.program_id(0)

        @pl.when(i == 0)
        def _():
            dw_ref[...] = jnp.zeros_like(dw_ref)
            dgs_ref[...] = jnp.zeros_like(dgs_ref)
            dga_ref[...] = jnp.zeros_like(dga_ref)
            db_acc[...] = jnp.zeros_like(db_acc)

        t = _sgu_forward_tile(uv_ref[...], w_ref, b_ref[...], gs_ref[...])
        na, ra = _rms_stats(t['ya'])
        dyn = dy_ref[...]
        dga_ref[...] += jnp.sum(dyn * na, axis=0, keepdims=True)
        dya = _rms_bwd(dyn * ga_ref[...], na, ra)
        dug = dya * t['mixed']
        dmixed = dya * t['ug']
        dmb = dmixed.astype(MXU_DTYPE)
        masks = _half_masks(MXU_DTYPE)
        chunks = []
        db = jnp.zeros((CHUNK, WIDTH_A), F32)
        for c in range(tm // CHUNK):
            rows = slice(c * CHUNK, (c + 1) * CHUNK)
            db = db + dmixed[rows]
            groups = []
            for gp in range(2):
                cols = slice(gp * LANES, (gp + 1) * LANES)
                dm_g = dmb[rows, cols]
                vn_g = t['vn'][rows, cols]
                dvn_g = jnp.zeros((CHUNK, LANES), F32)
                for j in range(2):
                    dm_h = dm_g * masks[j]
                    dvn_g = dvn_g + _dot(wt_ref[2 * gp + j], dm_h)
                    dw_ref[2 * gp + j] += _dot_nt(dm_h, vn_g)
                groups.append(dvn_g)
            chunks.append(jnp.concatenate(groups, axis=1))
        db_acc[...] += db
        dvn = jnp.concatenate(chunks, axis=0)
        vhat = t['vhat']
        dgs_ref[...] += jnp.sum(dvn * vhat, axis=0, keepdims=True)
        dvh = dvn * gs_ref[...]
        dvg = t['rs'] * (dvh - jnp.mean(dvh, axis=-1, keepdims=True)
                         - vhat * jnp.mean(dvh * vhat, axis=-1, keepdims=True))
        duv_ref[:, :WIDTH_A] = (dug * _gelu_grad(t['u'], t['tu'])).astype(MXU_DTYPE)
        duv_ref[:, WIDTH_A:] = (dvg * _gelu_grad(t['v'], t['tv'])).astype(MXU_DTYPE)

        @pl.when(i == pl.num_programs(0) - 1)
        def _():
            lane_a = lax.broadcasted_iota(jnp.int32, (CHUNK, WIDTH_A), 1)
            lane = lax.broadcasted_iota(jnp.int32, (CHUNK, LANES), 1)
            acc = db_acc[...]
            out = jnp.zeros((CHUNK, LANES), F32)
            for h in range(HEADS_A):
                col = jnp.sum(jnp.where(lane_a // HEAD_DIM == h, acc, 0.0), axis=1, keepdims=True)
                out = jnp.where(lane == h, col, out)
            db_ref[...] = out
            causal = (lax.broadcasted_iota(jnp.int32, (CHUNK, CHUNK), 0)
                      >= lax.broadcasted_iota(jnp.int32, (CHUNK, CHUNK), 1))
            for h in range(HEADS_A):
                dw_ref[h] = jnp.where(causal, dw_ref[h], 0.0)

    return pl.pallas_call(
        body, name="sgu_bwd", grid=(s // tm,),
        in_specs=[_rows(tm, 2 * WIDTH_A), _rows(tm, WIDTH_A), _full((HEADS_A, CHUNK, CHUNK)),
                  _full((HEADS_A, CHUNK, CHUNK)), _full((CHUNK, WIDTH_A)), _full((1, WIDTH_A)),
                  _full((1, WIDTH_A))],
        out_specs=[_rows(tm, 2 * WIDTH_A), _full((HEADS_A, CHUNK, CHUNK)), _full((CHUNK, LANES)),
                   _full((1, WIDTH_A)), _full((1, WIDTH_A))],
        out_shape=[jax.ShapeDtypeStruct((s, 2 * WIDTH_A), MXU_DTYPE),
                   jax.ShapeDtypeStruct((HEADS_A, CHUNK, CHUNK), F32), jax.ShapeDtypeStruct((CHUNK, LANES), F32),
                   jax.ShapeDtypeStruct((1, WIDTH_A), F32), jax.ShapeDtypeStruct((1, WIDTH_A), F32)],
        scratch_shapes=[pltpu.VMEM((CHUNK, WIDTH_A), F32)],
        compiler_params=_params(("arbitrary",)),
    )(uv, dya_n, w_tril, w_tril_t, bias, g_sgu, g_a)


def _in_bwd_proj(duv, dq, dk, dv, cos_t, sin_t):
    s = duv.shape[0]
    tm = 512
    nc = WIDTH_B // LANES

    def body(duv_ref, *refs):
        dq_refs, dk_refs, dv_refs = refs[:nc], refs[nc:2 * nc], refs[2 * nc:3 * nc]
        cos_ref, sin_ref, dp_ref = refs[3 * nc:]
        cos = cos_ref[...]
        sin = sin_ref[...]
        dp_ref[:, :2 * WIDTH_A] = duv_ref[...]
        for i in range(nc):
            lo = 2 * WIDTH_A + i * LANES
            tq = _load_l256(dq_refs[i:i + 1], tm) * (HEAD_DIM ** -0.5)
            tk = _load_l256(dk_refs[i:i + 1], tm)
            dp_ref[:, lo:lo + LANES] = (tq * cos + _rope_partner(tq * sin)).astype(MXU_DTYPE)
            dp_ref[:, lo + WIDTH_B:lo + WIDTH_B + LANES] = (tk * cos + _rope_partner(tk * sin)).astype(MXU_DTYPE)
            dp_ref[:, lo + 2 * WIDTH_B:lo + 2 * WIDTH_B + LANES] = _load_l256(dv_refs[i:i + 1], tm).astype(MXU_DTYPE)

    return pl.pallas_call(
        body, name="in_bwd_proj", grid=(s // tm,),
        in_specs=[_rows(tm, 2 * WIDTH_A)] + 3 * _col_specs(tm, WIDTH_B) + [_rows(tm, LANES), _rows(tm, LANES)],
        out_specs=_rows(tm, IN_COLS), out_shape=jax.ShapeDtypeStruct((s, IN_COLS), MXU_DTYPE),
        compiler_params=_params(("arbitrary",)),
    )(duv, *([dq] * nc), *([dk] * nc), *([dv] * nc), cos_t, sin_t)


def _in_bwd_x(dproj, w_in_t, x, g_mix, dh1):
    s = x.shape[0]
    tm = 512

    def body(dp_ref, wt_ref, x_ref, g_ref, dh_ref, gx_ref, dg_ref):
        @pl.when(pl.program_id(0) == 0)
        def _():
            dg_ref[...] = jnp.zeros_like(dg_ref)

        dhn = _dot(dp_ref[...], wt_ref[...])
        n, r = _rms_stats(x_ref[...])
        dg_ref[...] += jnp.sum(dhn * n, axis=0, keepdims=True)
        gx_ref[...] = dh_ref[...] + _rms_bwd(dhn * g_ref[...], n, r)

    return pl.pallas_call(
        body, name="in_bwd_x", grid=(s // tm,),
        in_specs=[_rows(tm, IN_COLS), _full((IN_COLS, D_MODEL)), _rows(tm, D_MODEL), _full((1, D_MODEL)),
                  _rows(tm, D_MODEL)],
        out_specs=[_rows(tm, D_MODEL), _full((1, D_MODEL))],
        out_shape=[jax.ShapeDtypeStruct((s, D_MODEL), F32), jax.ShapeDtypeStruct((1, D_MODEL), F32)],
        compiler_params=_params(("arbitrary",)),
    )(dproj, w_in_t, x, g_mix, dh1)


def _wgrad(a, b, name):
    s, m = a.shape
    n = b.shape[1]
    bm = 512 if m % 512 == 0 else FF_HALF
    ts = 1024
    nsteps = s // ts

    def body(a_ref, b_ref, o_ref, acc):
        kk = pl.program_id(1)

        @pl.when(kk == 0)
        def _():
            acc[...] = jnp.zeros_like(acc)

        acc[...] += _dot_tn(a_ref[...].astype(MXU_DTYPE), b_ref[...].astype(MXU_DTYPE))

        @pl.when(kk == nsteps - 1)
        def _():
            o_ref[...] = acc[...].astype(o_ref.dtype)

    return pl.pallas_call(
        body, name=name, grid=(m // bm, nsteps),
        in_specs=[pl.BlockSpec((ts, bm), lambda i, kk: (kk, i)), pl.BlockSpec((ts, n), lambda i, kk: (kk, 0))],
        out_specs=pl.BlockSpec((bm, n), lambda i, kk: (i, 0)), out_shape=jax.ShapeDtypeStruct((m, n), jnp.bfloat16),
        scratch_shapes=[pltpu.VMEM((bm, n), F32)],
        compiler_params=_params(("arbitrary", "arbitrary")),
    )(a, b)


def _rope_tables(s):
    half = HEAD_DIM // 2
    inv = ROPE_THETA ** (-jnp.arange(half, dtype=F32) / half)
    ang = jnp.arange(s, dtype=F32)[:, None] * jnp.tile(inv, LANES // half)[None, :]
    sign = jnp.tile(jnp.concatenate([-jnp.ones(half, F32), jnp.ones(half, F32)]), LANES // HEAD_DIM)
    return jnp.cos(ang), jnp.sin(ang) * sign[None, :]


MESH = pl.DeviceIdType.MESH
ANY = pl.BlockSpec(memory_space=pl.ANY)
SEM = pl.BlockSpec(memory_space=pltpu.SEMAPHORE)
SPLIT_COPY = pltpu.CompilerParams(has_side_effects=pltpu.SideEffectType.DATAFLOW_SIDE_EFFECTING)
SLAB_IS_TRANSPOSED = {'w_in': True, 'w_out': False, 'w_gate': True, 'w_up': True, 'w_down': False,
                      'w_ple_gate': False, 'w_ple_proj': True}


def _place():
    x, y, c = lax.axis_index("x"), lax.axis_index("y"), lax.axis_index("c")
    other_chips = [(1 - x, y), (x, 1 - y), (1 - x, 1 - y)]
    return x, y, c, other_chips


def _chip_of(chip):
    return 2 * chip[0] + chip[1]


def _half(ref, lead, hc):
    hr = ref.shape[1] // 2
    return ref.at[lead, pl.ds(hc * hr, hr), :]


def _put_own(stack, own, index):
    return lax.dynamic_update_slice(stack, own[None], (index,) + (0,) * own.ndim)


def _all_gather_now(slab):
    rows, cols = slab.shape

    def body(x_ref, out_ref, send_sems, recv_sems):
        x, y, c, chips = _place()
        sibling = (x, y, 1 - c)
        hr = rows // 2

        def copy(k, src, dst, to):
            return pltpu.make_async_remote_copy(src_ref=src, dst_ref=dst, send_sem=send_sems.at[k],
                                                recv_sem=recv_sems.at[k], device_id=to, device_id_type=MESH)

        my_half = x_ref.at[pl.ds(c * hr, hr), :]
        first = [copy(j, my_half, _half(out_ref, 2 * x + y, c), (*chip, c)) for j, chip in enumerate(chips)]
        for cp in first:
            cp.start()
        passed = [copy(3 + j, _half(out_ref, _chip_of(chip), c), _half(out_ref, _chip_of(chip), c), sibling)
                  for j, chip in enumerate(chips)]
        for j, chip in enumerate(chips):
            copy(j, my_half, _half(out_ref, _chip_of(chip), c), (*chip, c)).wait_recv()
            passed[j].start()
        for j, chip in enumerate(chips):
            copy(3 + j, my_half, _half(out_ref, _chip_of(chip), 1 - c), sibling).wait_recv()
        for cp in first + passed:
            cp.wait_send()

    gathered = pl.pallas_call(
        body, name="all_gather_now", out_shape=jax.ShapeDtypeStruct((N_CHIPS, rows, cols), slab.dtype),
        in_specs=[ANY], out_specs=ANY,
        scratch_shapes=[pltpu.SemaphoreType.DMA((6,)), pltpu.SemaphoreType.DMA((6,))],
    )(slab)
    me = 2 * lax.axis_index("x") + lax.axis_index("y")
    return _put_own(gathered, slab, me).reshape(N_CHIPS * rows, cols)


def _gather_copies(slab_refs, land_refs, send_sems, recv_sems):
    x, y, c, chips = _place()
    sends, recvs = [], []
    for k, (src, land) in enumerate(zip(slab_refs, land_refs)):
        hr = src.shape[0] // 2
        for j, chip in enumerate(chips):
            for t in range(2):
                sends.append(pltpu.make_async_remote_copy(
                    src_ref=src.at[pl.ds(c * hr, hr), :], dst_ref=_half(land, 2 * x + y, c),
                    send_sem=send_sems.at[6 * k + 2 * j + t], recv_sem=recv_sems.at[6 * k + 2 * j + c],
                    device_id=(*chip, t), device_id_type=MESH))
                recvs.append(pltpu.make_async_remote_copy(
                    src_ref=src.at[pl.ds(t * hr, hr), :], dst_ref=_half(land, _chip_of(chip), t),
                    send_sem=send_sems.at[6 * k + 2 * j + t], recv_sem=recv_sems.at[6 * k + 2 * j + t],
                    device_id=(*chip, t), device_id_type=MESH))
    return sends, recvs


def _all_gather_start(slabs, after):
    n = len(slabs)

    def body(*refs):
        slab_refs, land_refs = refs[:n], refs[n:2 * n]
        send_sems, recv_sems = refs[2 * n + 1:2 * n + 3]
        token = refs[-1]
        sends, _ = _gather_copies(slab_refs, land_refs, send_sems, recv_sems)
        for cp in sends:
            cp.start()
        token[...] = jnp.zeros_like(token)

    lands = [lax.empty((N_CHIPS,) + s.shape, s.dtype) for s in slabs]
    hbm = lambda a: pltpu.HBM(a.shape, a.dtype)
    res = pl.pallas_call(
        body, name="all_gather_start",
        out_shape=(pltpu.SemaphoreType.DMA((6 * n,)), pltpu.SemaphoreType.DMA((6 * n,)), *map(hbm, slabs),
                   *map(hbm, lands), jax.ShapeDtypeStruct((8, LANES), F32)),
        in_specs=[ANY] * (2 * n + 1),
        out_specs=(SEM, SEM, *([ANY] * (2 * n)), pl.BlockSpec(memory_space=pltpu.VMEM)),
        input_output_aliases={i: 2 + i for i in range(2 * n)}, compiler_params=SPLIT_COPY,
    )(*[pltpu.with_memory_space_constraint(a, pltpu.HBM) for a in list(slabs) + lands], after)
    return res[:-1], res[-1]


def _all_gather_wait(handle, after):
    send_sems, recv_sems = handle[:2]
    n = (len(handle) - 2) // 2
    slabs, lands = handle[2:2 + n], handle[2 + n:]

    def body(*refs):
        slab_refs, land_refs = refs[:n], refs[n:2 * n]
        send_sems, recv_sems = refs[2 * n:2 * n + 2]
        sends, recvs = _gather_copies(slab_refs, land_refs, send_sems, recv_sems)
        for cp in sends:
            cp.wait_send()
        for cp in recvs:
            cp.wait_recv()

    hbm = lambda a: pltpu.HBM(a.shape, a.dtype)
    res = pl.pallas_call(
        body, name="all_gather_wait", out_shape=tuple(map(hbm, list(slabs) + list(lands))),
        in_specs=[ANY] * (2 * n) + [SEM, SEM, ANY], out_specs=tuple([ANY] * (2 * n)),
        input_output_aliases={i: i for i in range(2 * n)}, compiler_params=SPLIT_COPY,
    )(*slabs, *lands, send_sems, recv_sems, after)
    me = 2 * lax.axis_index("x") + lax.axis_index("y")
    return [_put_own(land, slab, me).reshape(N_CHIPS * slab.shape[0], slab.shape[1])
            for slab, land in zip(res[:n], res[n:])]


def _scatter_copies(part_refs, land_refs, send_sems, recv_sems):
    x, y, c, chips = _place()
    me = 4 * x + 2 * y + c
    sends, recvs = [], []
    for k, (part, land) in enumerate(zip(part_refs, land_refs)):
        for j, chip in enumerate(chips):
            for t in range(2):
                sends.append(pltpu.make_async_remote_copy(
                    src_ref=part.at[_chip_of(chip)], dst_ref=land.at[me],
                    send_sem=send_sems.at[7 * k + 2 * j + t], recv_sem=recv_sems.at[7 * k + 2 * j + c],
                    device_id=(*chip, t), device_id_type=MESH))
                recvs.append(pltpu.make_async_remote_copy(
                    src_ref=part.at[_chip_of(chip)], dst_ref=land.at[2 * _chip_of(chip) + t],
                    send_sem=send_sems.at[7 * k + 2 * j + t], recv_sem=recv_sems.at[7 * k + 2 * j + t],
                    device_id=(*chip, t), device_id_type=MESH))
        sends.append(pltpu.make_async_remote_copy(
            src_ref=part.at[2 * x + y], dst_ref=land.at[me], send_sem=send_sems.at[7 * k + 6],
            recv_sem=recv_sems.at[7 * k + 6], device_id=(x, y, 1 - c), device_id_type=MESH))
        recvs.append(pltpu.make_async_remote_copy(
            src_ref=part.at[2 * x + y], dst_ref=land.at[4 * x + 2 * y + 1 - c],
            send_sem=send_sems.at[7 * k + 6], recv_sem=recv_sems.at[7 * k + 6], device_id=(x, y, 1 - c),
            device_id_type=MESH))
    return sends, recvs


def _reduce_scatter_start(parts, name):
    n = len(parts)
    parts = [p.reshape(N_CHIPS, p.shape[0] // N_CHIPS, p.shape[1]) for p in parts]

    def body(*refs):
        part_refs, land_refs = refs[:n], refs[n:2 * n]
        send_sems, recv_sems = refs[2 * n:2 * n + 2]
        token = refs[-1]
        sends, _ = _scatter_copies(part_refs, land_refs, send_sems, recv_sems)
        for cp in sends:
            cp.start()
        token[...] = jnp.zeros_like(token)

    lands = [lax.empty((N_DEV, p.shape[1], p.shape[2]), p.dtype) for p in parts]
    hbm = lambda a: pltpu.HBM(a.shape, a.dtype)
    res = pl.pallas_call(
        body, name=name,
        out_shape=(pltpu.SemaphoreType.DMA((7 * n,)), pltpu.SemaphoreType.DMA((7 * n,)), *map(hbm, parts),
                   *map(hbm, lands), jax.ShapeDtypeStruct((8, LANES), F32)),
        in_specs=[ANY] * (2 * n), out_specs=(SEM, SEM, *([ANY] * (2 * n)), pl.BlockSpec(memory_space=pltpu.VMEM)),
        input_output_aliases={i: 2 + i for i in range(2 * n)}, compiler_params=SPLIT_COPY,
    )(*[pltpu.with_memory_space_constraint(a, pltpu.HBM) for a in parts + lands])
    return res[:-1], res[-1]


def _reduce_scatter_wait(handle, after, name):
    after = list(after) if isinstance(after, (list, tuple)) else [after]
    send_sems, recv_sems = handle[:2]
    n = (len(handle) - 2) // 2
    parts, lands = handle[2:2 + n], handle[2 + n:]

    def body(*refs):
        part_refs, land_refs = refs[:n], refs[n:2 * n]
        send_sems, recv_sems = refs[2 * n:2 * n + 2]
        sends, recvs = _scatter_copies(part_refs, land_refs, send_sems, recv_sems)
        for cp in sends:
            cp.wait_send()
        for cp in recvs:
            cp.wait_recv()

    hbm = lambda a: pltpu.HBM(a.shape, a.dtype)
    res = pl.pallas_call(
        body, name=name, out_shape=tuple(map(hbm, list(parts) + list(lands))),
        in_specs=[ANY] * (2 * n) + [SEM, SEM] + [ANY] * len(after), out_specs=tuple([ANY] * (2 * n)),
        input_output_aliases={i: i for i in range(2 * n)}, compiler_params=SPLIT_COPY,
    )(*parts, *lands, send_sems, recv_sems, *after)
    return list(zip(res[:n], res[n:]))


def _adamw_of_shares(w, own, land, m, v, name):
    rows, cols = w.shape
    tm = rows // 4 if rows % 32 == 0 else rows
    x, y, c = lax.axis_index("x"), lax.axis_index("y"), lax.axis_index("c")
    where = jnp.stack([2 * x + y, 4 * x + 2 * y + c]).astype(jnp.int32)

    def body(where_ref, w_ref, own_ref, land_ref, m_ref, v_ref, g_ref, d_ref, nm_ref, nv_ref):
        me = where_ref[1]
        g_ = jnp.zeros((tm, cols), F32)
        for dev in range(N_DEV):
            g_ = g_ + jnp.where(me == dev, own_ref[0], land_ref[dev]).astype(F32)
        m_ = ADAM_B1 * m_ref[...] + (1.0 - ADAM_B1) * g_
        v_ = ADAM_B2 * v_ref[...] + (1.0 - ADAM_B2) * (g_ * g_)
        m_hat = m_ / (1.0 - ADAM_B1 ** ADAM_STEP)
        v_hat = v_ / (1.0 - ADAM_B2 ** ADAM_STEP)
        g_ref[...] = g_
        d_ref[...] = -ADAM_LR * (m_hat / (jnp.sqrt(v_hat) + ADAM_EPS) + ADAM_WD * w_ref[...])
        nm_ref[...] = m_
        nv_ref[...] = v_

    tile = pl.BlockSpec((tm, cols), lambda i, where_ref: (i, 0))
    spec = pltpu.PrefetchScalarGridSpec(
        num_scalar_prefetch=1, grid=(rows // tm,),
        in_specs=[tile, pl.BlockSpec((1, tm, cols), lambda i, where_ref: (where_ref[0], i, 0)),
                  pl.BlockSpec((N_DEV, tm, cols), lambda i, where_ref: (0, i, 0)), tile, tile],
        out_specs=[tile] * 4)
    return pl.pallas_call(
        body, name=name, grid_spec=spec, out_shape=[jax.ShapeDtypeStruct(w.shape, F32)] * 4,
        compiler_params=_params(("arbitrary",)),
    )(where, w, own, land, m, v)


def _pack_small(values):
    flat = jnp.concatenate([values[n].reshape(-1).astype(F32) for n in SMALL])
    return jnp.pad(flat, (0, SMALL_ROWS * D_MODEL - flat.shape[0])).reshape(SMALL_ROWS, D_MODEL)


def _unpack_small(block, shapes):
    flat = block.reshape(-1)
    out, lo = {}, 0
    for n in SMALL:
        out[n] = flat[lo:lo + SMALL_SIZES[n]].reshape(shapes[n])
        lo += SMALL_SIZES[n]
    return out


def _after(token, a):
    return a + token[:1, :1].astype(a.dtype)


def kernel(x, p, mix_norm_g, w_in, sgu_w, sgu_b, sgu_norm_g, out_norm_a, out_norm_b, w_out, ffn_norm_g, w_gate, w_up, w_down, ple_norm_g, w_ple_gate, w_ple_proj, final_norm_g, loss_target, m_mix_norm_g, m_w_in, m_sgu_w, m_sgu_b, m_sgu_norm_g, m_out_norm_a, m_out_norm_b, m_w_out, m_ffn_norm_g, m_w_gate, m_w_up, m_w_down, m_ple_norm_g, m_w_ple_gate, m_w_ple_proj, m_final_norm_g, v_mix_norm_g, v_w_in, v_sgu_w, v_sgu_b, v_sgu_norm_g, v_out_norm_a, v_out_norm_b, v_w_out, v_ffn_norm_g, v_w_gate, v_w_up, v_w_down, v_ple_norm_g, v_w_ple_gate, v_w_ple_proj, v_final_norm_g):
    given = dict(locals())
    drop_lead = lambda a, lead: a.reshape(a.shape[lead:])
    xs, ps, target = drop_lead(x, 1), drop_lead(p, 2), drop_lead(loss_target, 1)
    s = xs.shape[0]
    shard = lambda name: drop_lead(given[name], 1)

    def slab_of(name):
        local = shard(name).astype(MXU_DTYPE)
        return local.T if SLAB_IS_TRANSPOSED[name] else local

    w_in_t = _all_gather_now(slab_of('w_in'))
    later = ['w_out', 'w_gate', 'w_up', 'w_down', 'w_ple_gate', 'w_ple_proj']
    gather, token = _all_gather_start([slab_of(n) for n in later], w_in_t)

    cos_t, sin_t = _rope_tables(s)
    tril = jnp.tril(jnp.ones((CHUNK, CHUNK), F32))
    w_tril = (sgu_w.reshape(HEADS_A, CHUNK, CHUNK) * tril).astype(MXU_DTYPE)
    w_tril_t = jnp.swapaxes(w_tril, 1, 2)
    bias = jnp.repeat(sgu_b.reshape(HEADS_A, CHUNK).T, HEAD_DIM, axis=1)
    g = {n: given[n].reshape(1, -1) for n in SMALL if n not in ('sgu_w', 'sgu_b')}

    uv, q, k, v, q1, k1, v1, hn1 = _in_fwd(xs, _after(token, g['mix_norm_g']), w_in_t, cos_t, sin_t)
    ya_n = _sgu_fwd(uv, w_tril, bias, g['sgu_norm_g'], g['out_norm_a'])
    branches = [_attn_fwd_local(q1, k1, v1)] + [_attn_fwd_branch(q, k, v, dil) for dil in DILATIONS[:-1]]
    y_b, lse = _attn_fwd_branch(q, k, v, DILATIONS[-1], earlier=branches)
    stacks = dict(zip(later, _all_gather_wait(gather, lse)))
    w_gate_t, w_up_t, w_pp_t = stacks['w_gate'], stacks['w_up'], stacks['w_ple_proj']
    h1, y_n = _out_fwd(ya_n, y_b, g['out_norm_b'], stacks['w_out'], xs)
    h2, gate, up, hn2 = _ffn_fwd(h1, g['ffn_norm_g'], w_gate_t, w_up_t, stacks['w_down'])
    loss, dh2, dz, dpp, hn3, d_ple_g, d_final_g = _ple_loss(
        h2, ps, target, g['ple_norm_g'], stacks['w_ple_gate'], w_pp_t, g['final_norm_g'])

    share = {}
    share['w_ple_gate'] = _wgrad(hn3, dz, "wgrad_ple_gate")
    share['w_ple_proj'] = _wgrad(dpp, ps, "wgrad_ple_proj")
    scatter_1, token = _reduce_scatter_start([share['w_ple_gate'], share['w_ple_proj']], "reduce_scatter_start_1")
    dh1, act, dgate, dup, d_ffn_g = _ffn_bwd(dh2, h1, gate, up, _after(token, g['ffn_norm_g']), stacks['w_down'],
                                             w_gate_t, w_up_t)
    share['w_down'] = _wgrad(act, dh2, "wgrad_down")
    share['w_gate'] = _wgrad(dgate, hn2, "wgrad_gate")
    share['w_up'] = _wgrad(dup, hn2, "wgrad_up")
    scatter_2, token = _reduce_scatter_start([share['w_down'], share['w_gate'], share['w_up']],
                                             "reduce_scatter_start_2")
    dya_n, dyb, d_out_b = _out_bwd(dh1, y_b, _after(token, g['out_norm_b']), stacks['w_out'])
    share['w_out'] = _wgrad(y_n, dh1, "wgrad_out")
    scatter_3, token = _reduce_scatter_start([share['w_out']], "reduce_scatter_start_3")
    grads = _attn_bwd_local(q1, k1, v1, dyb, y_b, lse)
    for dil in DILATIONS:
        grads = _attn_bwd_branch(q, k, v, dyb, y_b, lse, grads, dil)
    duv, d_sgu_w, d_sgu_b, d_sgu_g, d_out_a = _sgu_bwd(uv, dya_n, w_tril, w_tril_t, bias,
                                                       _after(token, g['sgu_norm_g']), g['out_norm_a'])
    dproj = _in_bwd_proj(duv, grads[0], grads[1], grads[2], cos_t, sin_t)
    share['w_in'] = _wgrad(dproj, hn1, "wgrad_in")
    scatter_4, token = _reduce_scatter_start([share['w_in']], "reduce_scatter_start_4")

    grads, deltas, new_m, new_v = {}, {}, {}, {}
    add_lead = lambda a: a.reshape((1,) + a.shape)

    done = {}

    def finish(names, handles, after, tag):
        landed = []
        for i, handle in enumerate(handles):
            landed += _reduce_scatter_wait(handle, after, "reduce_scatter_wait_%s%d" % (tag, i))
        for n, (own, land) in zip(names, landed):
            turn = (lambda a: a.T) if SLAB_IS_TRANSPOSED[n] else (lambda a: a)
            res = _adamw_of_shares(turn(shard(n)), own, land, turn(shard("m_" + n)), turn(shard("v_" + n)),
                                   "adamw_" + n)
            grads[n], deltas[n], new_m[n], new_v[n] = (add_lead(turn(a)) for a in res)
            done[n] = res[0]

    grad_x, d_mix_g = _in_bwd_x(dproj, w_in_t, xs, _after(token, g['mix_norm_g']), dh1)

    gs = {'mix_norm_g': d_mix_g, 'sgu_w': d_sgu_w, 'sgu_b': d_sgu_b[:, :HEADS_A].T, 'sgu_norm_g': d_sgu_g,
          'out_norm_a': d_out_a, 'out_norm_b': d_out_b, 'ffn_norm_g': d_ffn_g, 'ple_norm_g': d_ple_g,
          'final_norm_g': d_final_g}
    gs_block = _pack_small(gs).at[SMALL_ROWS - 1, 0].set(loss[0, 0])
    to_all = jnp.broadcast_to(gs_block[None], (N_CHIPS,) + gs_block.shape).reshape(-1, D_MODEL)
    scatter_small, token = _reduce_scatter_start([to_all], "small_all_reduce_start")

    finish(['w_ple_gate', 'w_ple_proj', 'w_down', 'w_gate', 'w_up', 'w_out'], [scatter_1, scatter_2, scatter_3], token,
           "early")
    finish(['w_in'], [scatter_4], [done[n] for n in ('w_down', 'w_gate', 'w_up', 'w_out')], "last")
    (own, land), = _reduce_scatter_wait(scatter_small, done['w_in'], "small_all_reduce_wait")
    small = {n: given[n] for n in SMALL}
    small_res = _adamw_of_shares(_pack_small(small), own, land, _pack_small({n: given["m_" + n] for n in SMALL}),
                                 _pack_small({n: given["v_" + n] for n in SMALL}), "adamw_small")
    loss_out = small_res[0][SMALL_ROWS - 1, 0]
    small_shapes = {n: given[n].shape for n in SMALL}
    for res, blk in zip((grads, deltas, new_m, new_v), small_res):
        res.update(_unpack_small(blk, small_shapes))

    outs = [loss_out, add_lead(grad_x)]
    for res in (grads, deltas, new_m, new_v):
        outs += [res[n] for n in WEIGHT_NAMES]
    return tuple(outs)
```

```python
import functools
import itertools
import math

import jax
import jax.numpy as jnp
import numpy as np
from jax import lax
from jax.experimental import pallas as pl
from jax.experimental.pallas import tpu as pltpu

F32 = jnp.float32
MXU_DTYPE = jnp.bfloat16

D_MODEL = 1024
HEAD_DIM = 64
HEADS_A = 4
HEADS_B = 12
WIDTH_A = HEADS_A * HEAD_DIM
WIDTH_B = HEADS_B * HEAD_DIM
CHUNK = 128
BLOCK = 128
DILATIONS = (4, 16)
ROPE_THETA = 10000.0
D_FF = 2816
FF_HALF = D_FF // 2
FF_STRIPS = ((0, 1024), (1024, 2048), (2048, D_FF))
PLE_DIM = 256
IN_COLS = 2 * WIDTH_A + 3 * WIDTH_B
EPS = 1e-6
LANES = 128
N_CHIPS = 4
N_DEV = 8

ADAM_LR = 0.001
ADAM_B1 = 0.9
ADAM_B2 = 0.999
ADAM_EPS = 1e-08
ADAM_WD = 0.01
ADAM_STEP = 10

VMEM_LIMIT = 56 * 1024 * 1024

WEIGHT_NAMES = ['mix_norm_g', 'w_in', 'sgu_w', 'sgu_b', 'sgu_norm_g', 'out_norm_a', 'out_norm_b', 'w_out',
                'ffn_norm_g', 'w_gate', 'w_up', 'w_down', 'ple_norm_g', 'w_ple_gate', 'w_ple_proj', 'final_norm_g']
SMALL = ['mix_norm_g', 'sgu_w', 'sgu_b', 'sgu_norm_g', 'out_norm_a', 'out_norm_b', 'ffn_norm_g', 'ple_norm_g',
         'final_norm_g']
SMALL_SIZES = {'mix_norm_g': 1024, 'sgu_w': 65536, 'sgu_b': 512, 'sgu_norm_g': 256, 'out_norm_a': 256,
               'out_norm_b': 768, 'ffn_norm_g': 1024, 'ple_norm_g': 1024, 'final_norm_g': 1024}
SMALL_ROWS = 72


def _params(semantics=None):
    return pltpu.CompilerParams(dimension_semantics=semantics, vmem_limit_bytes=VMEM_LIMIT)


def _full(shape):
    nd = len(shape)
    return pl.BlockSpec(shape, lambda i: (0,) * nd, pipeline_mode=pl.Buffered(1))


def _rows(tm, width):
    return pl.BlockSpec((tm, width), lambda i: (i, 0))


def _rms_stats(x):
    r = lax.rsqrt(jnp.mean(x * x, axis=-1, keepdims=True) + EPS)
    return x * r, r


def _rms_bwd(dn, n, r):
    return r * (dn - n * jnp.mean(dn * n, axis=-1, keepdims=True))


def _dot(a, b):
    return jnp.dot(a, b, preferred_element_type=F32)


def _dot_nt(a, b):
    return lax.dot_general(a, b, (((1,), (1,)), ((), ())), preferred_element_type=F32)


def _dot_tn(a, b):
    return lax.dot_general(a, b, (((0,), (0,)), ((), ())), preferred_element_type=F32)


def _gelu_parts(x):
    c = math.sqrt(2.0 / math.pi)
    t = jnp.tanh(c * (x + 0.044715 * x * x * x))
    return 0.5 * x * (1.0 + t), t


def _gelu_grad(x, t):
    c = math.sqrt(2.0 / math.pi)
    return 0.5 * (1.0 + t) + 0.5 * x * (1.0 - t * t) * c * (1.0 + 3.0 * 0.044715 * x * x)


def _half_masks(dtype):
    lane = lax.broadcasted_iota(jnp.int32, (BLOCK, LANES), 1)
    lo = (lane < HEAD_DIM).astype(F32)
    return lo.astype(dtype), (1.0 - lo).astype(dtype)


def _rope_partner(t):
    lane = lax.broadcasted_iota(jnp.int32, t.shape, 1)
    first_half = (lane % HEAD_DIM) < (HEAD_DIM // 2)
    return jnp.where(first_half, pltpu.roll(t, LANES - HEAD_DIM // 2, 1), pltpu.roll(t, HEAD_DIM // 2, 1))


PAIRS_ABREAST = 2
RESIDUES_PER_STEP = 4
L_BLOCK = 256
L_GROUP = 16


def _store_l256(scr, out_ref, cols, value, chunk_ref=None):
    tm = value.shape[0]
    half = L_GROUP // 2
    scr[...] = value
    for blk in range(tm // L_BLOCK):
        pieces = [scr[pl.ds(blk * L_BLOCK + r, L_GROUP, stride=L_GROUP), :] for r in range(L_GROUP)]
        for r, piece in enumerate(pieces):
            lo = blk * L_BLOCK + r * L_GROUP
            out_ref[lo:lo + L_GROUP, cols] = piece.astype(out_ref.dtype)
        if chunk_ref is not None:
            for chunk in range(L_BLOCK // BLOCK):
                for r in range(0, L_GROUP, 2):
                    lo = blk * L_BLOCK + chunk * BLOCK + r * half
                    both = [p[chunk * half:(chunk + 1) * half] for p in pieces[r:r + 2]]
                    chunk_ref[lo:lo + L_GROUP, cols] = jnp.concatenate(both, axis=0).astype(chunk_ref.dtype)


def _load_l256(col_refs, tm):
    cols = []
    for ref in col_refs:
        pieces = [ref[pl.ds(blk * L_BLOCK + i, L_GROUP, stride=L_GROUP), :]
                  for blk in range(tm // L_BLOCK) for i in range(L_GROUP)]
        cols.append(jnp.concatenate(pieces, axis=0))
    return jnp.concatenate(cols, axis=1)


def _col_specs(tm, width):
    return [pl.BlockSpec((tm, LANES), lambda i, j=j: (i, j)) for j in range(width // LANES)]


def _in_fwd(x, g_mix, w_in_t, cos_t, sin_t):
    s = x.shape[0]
    tm = 512

    def body(x_ref, g_ref, wt_ref, cos_ref, sin_ref, uv_ref, q_ref, k_ref, v_ref, q1_ref, k1_ref, v1_ref, hn_ref,
             *scrs):
        n, _ = _rms_stats(x_ref[...])
        hn = (n * g_ref[...]).astype(MXU_DTYPE)
        hn_ref[...] = hn
        cos = cos_ref[...]
        sin = sin_ref[...]
        strip = 2 * LANES
        for j in range(IN_COLS // strip):
            proj = _dot_nt(hn, wt_ref[j * strip:(j + 1) * strip, :])
            lo = j * strip - 2 * WIDTH_A
            if lo < 0:
                uv_ref[:, j * strip:(j + 1) * strip] = proj
                continue
            which, lo = divmod(lo, WIDTH_B)
            for i in range(strip // LANES):
                t = proj[:, i * LANES:(i + 1) * LANES]
                cols = slice(lo + i * LANES, lo + (i + 1) * LANES)
                scr = scrs[i]
                if which == 0:
                    _store_l256(scr, q_ref, cols, (t * cos + _rope_partner(t) * sin) * (HEAD_DIM ** -0.5), q1_ref)
                elif which == 1:
                    _store_l256(scr, k_ref, cols, t * cos + _rope_partner(t) * sin, k1_ref)
                else:
                    _store_l256(scr, v_ref, cols, t, v1_ref)

    return pl.pallas_call(
        body, name="in_fwd", grid=(s // tm,), scratch_shapes=[pltpu.VMEM((tm, LANES), F32)] * 2,
        in_specs=[_rows(tm, D_MODEL), _full((1, D_MODEL)), _full((IN_COLS, D_MODEL)), _rows(tm, LANES),
                  _rows(tm, LANES)],
        out_specs=[_rows(tm, 2 * WIDTH_A)] + [_rows(tm, WIDTH_B)] * 6 + [_rows(tm, D_MODEL)],
        out_shape=[jax.ShapeDtypeStruct((s, 2 * WIDTH_A), F32)] + [jax.ShapeDtypeStruct((s, WIDTH_B), MXU_DTYPE)] * 6
        + [jax.ShapeDtypeStruct((s, D_MODEL), MXU_DTYPE)],
        compiler_params=_params(("arbitrary",)),
    )(x, g_mix, w_in_t, cos_t, sin_t)


class _Branch:
    def __init__(self, dil, s):
        i = np.arange(L_GROUP)
        self.res = RESIDUES_PER_STEP
        if dil == 16:
            nblk = BLOCK // 16
            self.grid = (16 // self.res, s // (L_BLOCK * nblk))
            self.shape = (nblk, 1, self.res, L_GROUP)
            self.index = lambda r, n: (n, r // (4 // self.res), r % (4 // self.res), 0, 0)
            pos = (np.arange(nblk)[:, None] * 16 + i[None, :]).reshape(-1)
        else:
            nblk = BLOCK // 64
            self.grid = (4 // self.res, s // (L_BLOCK * nblk))
            self.shape = (nblk, 4, self.res, L_GROUP)
            self.index = lambda r, n: (n, 0, r, 0, 0)
            pos = (np.arange(nblk)[:, None, None] * 64 + np.arange(4)[None, :, None]
                   + 4 * i[None, None, :]).reshape(-1)
        self.qn = pos.shape[0]
        self.nb = self.grid[1]
        dist = pos[:, None] - np.concatenate([pos - self.qn, pos])[None, :]
        band = (dist >= 0) & (dist <= BLOCK)
        start = band & (np.arange(2 * self.qn)[None, :] >= self.qn)
        self.bias = np.where(np.stack([band, start]), 0.0, -np.inf).astype(np.float32)

    def view(self, a):
        return a.reshape(a.shape[0] // L_BLOCK, 4, 4, L_GROUP, a.shape[1])

    def spec(self, w, step=lambda n: n):
        return pl.BlockSpec(self.shape + (w,), lambda r, n: self.index(r, step(n)))

    def bias_spec(self, step=lambda n: n):
        return pl.BlockSpec((1, self.qn, 2 * self.qn), lambda r, n: (jnp.where(step(n) == 0, 1, 0), 0, 0))

    def load(self, ref, cols=slice(None), j=0):
        x = ref[:, :, j, :, cols]
        return x.reshape(self.qn, x.shape[-1])

    def store(self, ref, cols, value, j=0):
        ref[:, :, j, :, cols] = value.reshape(self.shape[:2] + (L_GROUP, value.shape[-1]))


def _attn_fwd_branch(q, k, v, dil, earlier=()):
    s = q.shape[0]
    br = _Branch(dil, s)
    qn = br.qn
    nearly = len(earlier)

    def body(bias_ref, q_ref, kc_ref, kp_ref, vc_ref, vp_ref, *refs):
        early_refs, (o_ref, lse_ref) = refs[:2 * nearly], refs[2 * nearly:]
        bias2 = jnp.concatenate([bias_ref[0], bias_ref[0]], axis=0)
        lo = lax.broadcasted_iota(jnp.int32, (qn, LANES), 1) < HEAD_DIM
        mask_lo = lo.astype(F32).astype(MXU_DTYPE)
        for j, hp in itertools.product(range(br.res), range(HEADS_B // 2)):
            cols = slice(hp * LANES, (hp + 1) * LANES)
            qp = br.load(q_ref, cols, j)
            kcat = jnp.concatenate([br.load(kp_ref, cols, j), br.load(kc_ref, cols, j)], axis=0)
            vcat = jnp.concatenate([br.load(vp_ref, cols, j), br.load(vc_ref, cols, j)], axis=0)
            sc = _dot_nt(jnp.concatenate([qp * mask_lo, qp * (1 - mask_lo)], axis=0), kcat) + bias2
            m = jnp.max(sc, axis=1, keepdims=True)
            p = jnp.exp(sc - m)
            l = jnp.sum(p, axis=1, keepdims=True)
            out = _dot(p.astype(MXU_DTYPE), vcat) / l
            lse = m + jnp.log(l)
            outs = [br.load(r, cols, j) for r in early_refs[:nearly]] + [jnp.where(lo, out[:qn], out[qn:])]
            lses = [br.load(r, cols, j) for r in early_refs[nearly:]] + [jnp.where(lo, lse[:qn], lse[qn:])]
            if nearly:
                top = functools.reduce(jnp.maximum, lses)
                ws = [jnp.exp(x - top) for x in lses]
                den = functools.reduce(jnp.add, ws)
                outs = [functools.reduce(jnp.add, [w * o for w, o in zip(ws, outs)]) / den]
                lses = [top + jnp.log(den)]
            br.store(o_ref, cols, outs[0], j)
            br.store(lse_ref, cols, lses[0], j)

    before = lambda n: jnp.maximum(n - 1, 0)
    res = pl.pallas_call(
        body, name="attn_fwd_d%d" % dil, grid=br.grid,
        in_specs=[br.bias_spec(), br.spec(WIDTH_B), br.spec(WIDTH_B), br.spec(WIDTH_B, before), br.spec(WIDTH_B),
                  br.spec(WIDTH_B, before)] + [br.spec(WIDTH_B)] * (2 * nearly),
        out_specs=[br.spec(WIDTH_B), br.spec(WIDTH_B)],
        out_shape=[jax.ShapeDtypeStruct((s // L_BLOCK, 4, 4, L_GROUP, WIDTH_B), F32)] * 2,
        compiler_params=_params(("arbitrary", "arbitrary")),
    )(jnp.asarray(br.bias), br.view(q), br.view(k), br.view(k), br.view(v), br.view(v),
      *[br.view(o) for o, _ in earlier], *[br.view(x) for _, x in earlier])
    return tuple(a.reshape(s, WIDTH_B) for a in res)


LOCAL_CHUNKS = 4


def _local_bias():
    row = np.arange(BLOCK)
    pos = L_GROUP * (row % (L_GROUP // 2)) + row // (L_GROUP // 2)
    dist = pos[:, None] - np.concatenate([pos - BLOCK, pos])[None, :]
    band = (dist >= 0) & (dist <= BLOCK)
    start = band & (np.arange(2 * BLOCK)[None, :] >= BLOCK)
    return np.where(np.stack([band, start]), 0.0, -np.inf).astype(np.float32)


def _chunk_view(a):
    return a.reshape(a.shape[0] // L_BLOCK, L_GROUP, 2, L_GROUP // 2, a.shape[1])


def _chunk_of(ref, j, cols=slice(None)):
    x = ref[j // 2, :, j % 2, :, cols]
    return x.reshape(BLOCK, x.shape[-1])


def _put_chunk(ref, j, cols, value):
    ref[j // 2, :, j % 2, :, cols] = value.reshape(L_GROUP, L_GROUP // 2, value.shape[-1])


def _local_keys(cur_ref, before_ref, j, cols):
    here = slice(j * BLOCK, (j + 1) * BLOCK)
    before = before_ref[:, cols] if j == 0 else cur_ref[(j - 1) * BLOCK:j * BLOCK, cols]
    return jnp.concatenate([before, cur_ref[here, cols]], axis=0)


def _local_specs(s, step=lambda n: n):
    rows = LOCAL_CHUNKS * BLOCK
    cur = pl.BlockSpec((rows, WIDTH_B), lambda n: (step(n), 0))
    before = pl.BlockSpec((BLOCK, WIDTH_B), lambda n: (jnp.maximum(LOCAL_CHUNKS * step(n) - 1, 0), 0))
    return [cur, cur, before, cur, before]


def _attn_fwd_local(q1, k1, v1):
    s = q1.shape[0]
    rows = LOCAL_CHUNKS * BLOCK
    qn = BLOCK

    def body(bias_ref, q_ref, kc_ref, kp_ref, vc_ref, vp_ref, o_ref, lse_ref):
        first = jnp.where(pl.program_id(0) == 0, bias_ref[1], bias_ref[0])
        biases = [jnp.concatenate([b, b], axis=0) for b in (first, bias_ref[0])]
        lo = lax.broadcasted_iota(jnp.int32, (qn, LANES), 1) < HEAD_DIM
        mask_lo = lo.astype(F32).astype(MXU_DTYPE)
        for j, hp in itertools.product(range(LOCAL_CHUNKS), range(HEADS_B // 2)):
            cols = slice(hp * LANES, (hp + 1) * LANES)
            qp = q_ref[j * BLOCK:(j + 1) * BLOCK, cols]
            kcat = _local_keys(kc_ref, kp_ref, j, cols)
            vcat = _local_keys(vc_ref, vp_ref, j, cols)
            sc = _dot_nt(jnp.concatenate([qp * mask_lo, qp * (1 - mask_lo)], axis=0), kcat) + biases[min(j, 1)]
            m = jnp.max(sc, axis=1, keepdims=True)
            p = jnp.exp(sc - m)
            l = jnp.sum(p, axis=1, keepdims=True)
            out = _dot(p.astype(MXU_DTYPE), vcat) / l
            lse = m + jnp.log(l)
            _put_chunk(o_ref, j, cols, jnp.where(lo, out[:qn], out[qn:]))
            _put_chunk(lse_ref, j, cols, jnp.where(lo, lse[:qn], lse[qn:]))

    out_spec = pl.BlockSpec((LOCAL_CHUNKS // 2, L_GROUP, 2, L_GROUP // 2, WIDTH_B), lambda n: (n, 0, 0, 0, 0))
    res = pl.pallas_call(
        body, name="attn_fwd_d1", grid=(s // rows,),
        in_specs=[_full((2, BLOCK, 2 * BLOCK))] + _local_specs(s), out_specs=[out_spec] * 2,
        out_shape=[jax.ShapeDtypeStruct((s // L_BLOCK, L_GROUP, 2, L_GROUP // 2, WIDTH_B), F32)] * 2,
        compiler_params=_params(("arbitrary",)),
    )(jnp.asarray(_local_bias()), q1, k1, k1, v1, v1)
    return tuple(a.reshape(s, WIDTH_B) for a in res)


def _attn_bwd_local(q1, k1, v1, do, o, lse):
    s = q1.shape[0]
    rows = LOCAL_CHUNKS * BLOCK
    nsteps = s // rows
    qn = BLOCK

    def body(bias_ref, q_ref, kc_ref, kp_ref, vc_ref, vp_ref, do_ref, o_ref, lse_ref, dq_ref, dk_ref, dv_ref,
             dk_buf, dv_buf):
        n = pl.program_id(0)

        @pl.when(n == 0)
        def _():
            dk_buf[...] = jnp.zeros_like(dk_buf)
            dv_buf[...] = jnp.zeros_like(dv_buf)

        @pl.when(n < nsteps)
        def _():
            first = jnp.where(n == 0, bias_ref[1], bias_ref[0])
            biases = [jnp.concatenate([b, b], axis=0) for b in (first, bias_ref[0])]
            lo = lax.broadcasted_iota(jnp.int32, (qn, LANES), 1) < HEAD_DIM
            mask_f = lo.astype(F32)
            mask_lo = mask_f.astype(MXU_DTYPE)
            dk_buf[LOCAL_CHUNKS:] = jnp.zeros((LOCAL_CHUNKS, qn, WIDTH_B), F32)
            dv_buf[LOCAL_CHUNKS:] = jnp.zeros((LOCAL_CHUNKS, qn, WIDTH_B), F32)

            def prepare(j, hp):
                cols = slice(hp * LANES, (hp + 1) * LANES)
                qp = q_ref[j * BLOCK:(j + 1) * BLOCK, cols]
                dop = _chunk_of(do_ref, j, cols)
                prod = dop * _chunk_of(o_ref, j, cols)
                prod_lo = prod * mask_f
                lse = _chunk_of(lse_ref, j, cols)
                return dict(
                    j=j, cols=cols, kcat=_local_keys(kc_ref, kp_ref, j, cols), vcat=_local_keys(vc_ref, vp_ref, j, cols),
                    qs=jnp.concatenate([qp * mask_lo, qp * (1 - mask_lo)], axis=0),
                    dos=jnp.concatenate([dop * mask_f, dop * (1.0 - mask_f)], axis=0).astype(MXU_DTYPE),
                    delta=jnp.concatenate([jnp.sum(prod_lo, axis=1, keepdims=True),
                                           jnp.sum(prod - prod_lo, axis=1, keepdims=True)], axis=0),
                    lse2=jnp.concatenate([lse[:, :1], lse[:, HEAD_DIM:HEAD_DIM + 1]], axis=0))

            def scores(t):
                t['sc'] = _dot_nt(t['qs'], t['kcat'])
                t['dp'] = _dot_nt(t['dos'], t['vcat'])

            def softmax(t):
                p = jnp.exp(t['sc'] + biases[min(t['j'], 1)] - t['lse2'])
                t['ds'] = (p * (t['dp'] - t['delta'])).astype(MXU_DTYPE)
                t['p'] = p.astype(MXU_DTYPE)

            def gradients(t):
                t['dvc'] = _dot_tn(t['p'], t['dos'])
                t['dkc'] = _dot_tn(t['ds'], t['qs'])
                t['dq2'] = _dot(t['ds'], t['kcat'])

            def store(t):
                j, cols = t['j'], t['cols']
                _put_chunk(dq_ref, j, cols, jnp.where(lo, t['dq2'][:qn], t['dq2'][qn:]))
                for buf, both in ((dk_buf, t['dkc']), (dv_buf, t['dvc'])):
                    buf[LOCAL_CHUNKS + j - 1, :, cols] += both[:qn]
                    buf[LOCAL_CHUNKS + j, :, cols] += both[qn:]

            for j, first_pair in itertools.product(range(LOCAL_CHUNKS), range(0, HEADS_B // 2, PAIRS_ABREAST)):
                group = [prepare(j, hp) for hp in range(first_pair, first_pair + PAIRS_ABREAST)]
                for stage in (scores, softmax, gradients, store):
                    for t in group:
                        stage(t)

        for j in range(LOCAL_CHUNKS):
            _put_chunk(dk_ref, j, slice(None), dk_buf[j])
            _put_chunk(dv_ref, j, slice(None), dv_buf[j])
        dk_buf[:LOCAL_CHUNKS] = dk_buf[LOCAL_CHUNKS:]
        dv_buf[:LOCAL_CHUNKS] = dv_buf[LOCAL_CHUNKS:]

    cur = lambda n: jnp.minimum(n, nsteps - 1)
    late = lambda n: jnp.maximum(n - 1, 0)
    view_spec = lambda step: pl.BlockSpec((LOCAL_CHUNKS // 2, L_GROUP, 2, L_GROUP // 2, WIDTH_B),
                                          lambda n: (step(n), 0, 0, 0, 0))
    res = pl.pallas_call(
        body, name="attn_bwd_d1", grid=(nsteps + 1,),
        in_specs=[_full((2, BLOCK, 2 * BLOCK))] + _local_specs(s, cur) + [view_spec(cur)] * 3,
        out_specs=[view_spec(cur), view_spec(late), view_spec(late)],
        out_shape=[jax.ShapeDtypeStruct((s // L_BLOCK, L_GROUP, 2, L_GROUP // 2, WIDTH_B), F32)] * 3,
        scratch_shapes=[pltpu.VMEM((2 * LOCAL_CHUNKS, qn, WIDTH_B), F32)] * 2,
        compiler_params=_params(("arbitrary",)),
    )(jnp.asarray(_local_bias()), q1, k1, k1, v1, v1, _chunk_view(do), _chunk_view(o), _chunk_view(lse))
    return tuple(a.reshape(s, WIDTH_B) for a in res)


def _sgu_forward_tile(uv, w_ref, bias, g_sgu):
    tm = uv.shape[0]
    u = uv[:, :WIDTH_A]
    v = uv[:, WIDTH_A:]
    ug, tu = _gelu_parts(u)
    vg, tv = _gelu_parts(v)
    mu = jnp.mean(vg, axis=-1, keepdims=True)
    vc = vg - mu
    rs = lax.rsqrt(jnp.mean(vc * vc, axis=-1, keepdims=True) + EPS)
    vhat = vc * rs
    vn = (vhat * g_sgu).astype(MXU_DTYPE)
    masks = _half_masks(MXU_DTYPE)
    chunks = []
    for c in range(tm // CHUNK):
        rows = slice(c * CHUNK, (c + 1) * CHUNK)
        groups = []
        for gp in range(2):
            vn_g = vn[rows, gp * LANES:(gp + 1) * LANES]
            groups.append(_dot(w_ref[2 * gp], vn_g * masks[0]) + _dot(w_ref[2 * gp + 1], vn_g * masks[1]))
        chunks.append(jnp.concatenate(groups, axis=1) + bias)
    mixed = jnp.concatenate(chunks, axis=0)
    return dict(u=u, v=v, ug=ug, tu=tu, tv=tv, rs=rs, vhat=vhat, vn=vn, mixed=mixed, ya=ug * mixed)


def _sgu_fwd(uv, w_tril, bias, g_sgu, g_a):
    s = uv.shape[0]
    tm = 512

    def body(uv_ref, w_ref, b_ref, gs_ref, ga_ref, o_ref):
        t = _sgu_forward_tile(uv_ref[...], w_ref, b_ref[...], gs_ref[...])
        n, _ = _rms_stats(t['ya'])
        o_ref[...] = (n * ga_ref[...]).astype(MXU_DTYPE)

    return pl.pallas_call(
        body, name="sgu_fwd", grid=(s // tm,),
        in_specs=[_rows(tm, 2 * WIDTH_A), _full((HEADS_A, CHUNK, CHUNK)), _full((CHUNK, WIDTH_A)),
                  _full((1, WIDTH_A)), _full((1, WIDTH_A))],
        out_specs=_rows(tm, WIDTH_A), out_shape=jax.ShapeDtypeStruct((s, WIDTH_A), MXU_DTYPE),
        compiler_params=_params(("arbitrary",)),
    )(uv, w_tril, bias, g_sgu, g_a)


def _out_fwd(ya_n, y_b, g_b, w_out, x):
    s = x.shape[0]
    tm = 512
    nc = WIDTH_B // LANES

    def body(ya_ref, *refs):
        yb_refs = refs[:nc]
        g_ref, w_ref, x_ref, h_ref, yn_ref = refs[nc:]
        n, _ = _rms_stats(_load_l256(yb_refs, tm))
        yn = jnp.concatenate([ya_ref[...], (n * g_ref[...]).astype(MXU_DTYPE)], axis=1)
        yn_ref[...] = yn
        h_ref[...] = x_ref[...] + _dot(yn, w_ref[...])

    return pl.pallas_call(
        body, name="out_fwd", grid=(s // tm,),
        in_specs=[_rows(tm, WIDTH_A)] + _col_specs(tm, WIDTH_B) + [_full((1, WIDTH_B)), _full((D_MODEL, D_MODEL)),
                                                                 _rows(tm, D_MODEL)],
        out_specs=[_rows(tm, D_MODEL), _rows(tm, D_MODEL)],
        out_shape=[jax.ShapeDtypeStruct((s, D_MODEL), F32), jax.ShapeDtypeStruct((s, D_MODEL), MXU_DTYPE)],
        compiler_params=_params(("arbitrary",)),
    )(ya_n, *([y_b] * nc), g_b, w_out, x)


def _ffn_fwd(h1, g_ffn, w_gate_t, w_up_t, w_down):
    s = h1.shape[0]
    tm = 512

    def body(h_ref, g_ref, wgt_ref, wut_ref, wd_ref, o_ref, gate_ref, up_ref, hn_ref):
        h = h_ref[...]
        n, _ = _rms_stats(h)
        hn = (n * g_ref[...]).astype(MXU_DTYPE)
        hn_ref[...] = hn
        strips = [dict(cols=slice(lo, hi)) for lo, hi in FF_STRIPS]

        def project(t):
            t['gate'] = _dot_nt(hn, wgt_ref[t['cols'], :])
            t['up'] = _dot_nt(hn, wut_ref[t['cols'], :])

        def activate(t):
            gate, up = t['gate'], t['up']
            gate_ref[:, t['cols']] = gate.astype(MXU_DTYPE)
            up_ref[:, t['cols']] = up.astype(MXU_DTYPE)
            t['act'] = (gate * jax.nn.sigmoid(gate) * up).astype(MXU_DTYPE)

        def down(t):
            return _dot(t['act'], wd_ref[t['cols'], :])

        out = h
        project(strips[0])
        for i, t in enumerate(strips):
            if i + 1 < len(strips):
                project(strips[i + 1])
            activate(t)
            out = out + down(t)
        o_ref[...] = out

    return pl.pallas_call(
        body, name="ffn_fwd", grid=(s // tm,),
        in_specs=[_rows(tm, D_MODEL), _full((1, D_MODEL)), _full((D_FF, D_MODEL)), _full((D_FF, D_MODEL)),
                  _full((D_FF, D_MODEL))],
        out_specs=[_rows(tm, D_MODEL), _rows(tm, D_FF), _rows(tm, D_FF), _rows(tm, D_MODEL)],
        out_shape=[jax.ShapeDtypeStruct((s, D_MODEL), F32), jax.ShapeDtypeStruct((s, D_FF), MXU_DTYPE),
                   jax.ShapeDtypeStruct((s, D_FF), MXU_DTYPE), jax.ShapeDtypeStruct((s, D_MODEL), MXU_DTYPE)],
        compiler_params=_params(("arbitrary",)),
    )(h1, g_ffn, w_gate_t, w_up_t, w_down)


def _ple_loss(h2, p, target, g_ple, w_pg, w_pp_t, g_final):
    s = h2.shape[0]
    tm = 512

    def body(h_ref, p_ref, t_ref, gp_ref, wg_ref, wpt_ref, gf_ref,
             loss_ref, dh_ref, dz_ref, dpp_ref, hn_ref, dgp_ref, dgf_ref):
        @pl.when(pl.program_id(0) == 0)
        def _():
            loss_ref[...] = jnp.zeros_like(loss_ref)
            dgp_ref[...] = jnp.zeros_like(dgp_ref)
            dgf_ref[...] = jnp.zeros_like(dgf_ref)

        h2t = h_ref[...]
        n2, r2 = _rms_stats(h2t)
        hn = (n2 * gp_ref[...]).astype(MXU_DTYPE)
        hn_ref[...] = hn
        gate = jax.nn.sigmoid(_dot(hn, wg_ref[...]))
        pp = _dot_nt(p_ref[...].astype(MXU_DTYPE), wpt_ref[...])
        h3 = h2t + gate * pp
        n3, r3 = _rms_stats(h3)
        diff = n3 * gf_ref[...] - t_ref[...]
        loss_ref[...] += jnp.full(loss_ref.shape, 0.5 * jnp.sum(diff * diff) / D_MODEL, F32)
        dy = diff * (1.0 / D_MODEL)
        dgf_ref[...] += jnp.sum(dy * n3, axis=0, keepdims=True)
        dh3 = _rms_bwd(dy * gf_ref[...], n3, r3)
        dpp_ref[...] = (dh3 * gate).astype(MXU_DTYPE)
        dz = (dh3 * pp * gate * (1.0 - gate)).astype(MXU_DTYPE)
        dz_ref[...] = dz
        dhn = _dot_nt(dz, wg_ref[...])
        dgp_ref[...] += jnp.sum(dhn * n2, axis=0, keepdims=True)
        dh_ref[...] = dh3 + _rms_bwd(dhn * gp_ref[...], n2, r2)

    return pl.pallas_call(
        body, name="ple_loss", grid=(s // tm,),
        in_specs=[_rows(tm, D_MODEL), _rows(tm, PLE_DIM), _rows(tm, D_MODEL), _full((1, D_MODEL)),
                  _full((D_MODEL, D_MODEL)), _full((D_MODEL, PLE_DIM)), _full((1, D_MODEL))],
        out_specs=[_full((1, LANES)), _rows(tm, D_MODEL), _rows(tm, D_MODEL), _rows(tm, D_MODEL),
                   _rows(tm, D_MODEL), _full((1, D_MODEL)), _full((1, D_MODEL))],
        out_shape=[jax.ShapeDtypeStruct((1, LANES), F32), jax.ShapeDtypeStruct((s, D_MODEL), F32),
                   jax.ShapeDtypeStruct((s, D_MODEL), MXU_DTYPE), jax.ShapeDtypeStruct((s, D_MODEL), MXU_DTYPE),
                   jax.ShapeDtypeStruct((s, D_MODEL), MXU_DTYPE), jax.ShapeDtypeStruct((1, D_MODEL), F32),
                   jax.ShapeDtypeStruct((1, D_MODEL), F32)],
        compiler_params=_params(("arbitrary",)),
    )(h2, p, target, g_ple, w_pg, w_pp_t, g_final)


def _ffn_bwd(dh2, h1, gate, up, g_ffn, w_down, w_gate_t, w_up_t):
    s = h1.shape[0]
    tm = 256

    def body(dh_ref, h_ref, gate_ref, up_ref, g_ref, wd_ref, wgt_ref, wut_ref,
             o_ref, act_ref, dg_ref, du_ref, dgn_ref):
        @pl.when(pl.program_id(0) == 0)
        def _():
            dgn_ref[...] = jnp.zeros_like(dgn_ref)

        dh = dh_ref[...]
        dhb = dh.astype(MXU_DTYPE)
        strips = [dict(cols=slice(lo, hi)) for lo, hi in FF_STRIPS]

        def back_down(t):
            t['dact'] = _dot_nt(dhb, wd_ref[t['cols'], :])

        def back_act(t):
            cols, dact = t['cols'], t['dact']
            g = gate_ref[:, cols].astype(F32)
            u = up_ref[:, cols].astype(F32)
            sg = jax.nn.sigmoid(g)
            silu = g * sg
            act_ref[:, cols] = (silu * u).astype(MXU_DTYPE)
            t['du'] = (dact * silu).astype(MXU_DTYPE)
            t['dg'] = (dact * u * sg * (1.0 + g * (1.0 - sg))).astype(MXU_DTYPE)
            du_ref[:, cols] = t['du']
            dg_ref[:, cols] = t['dg']

        def back_in(t):
            return _dot(t['dg'], wgt_ref[t['cols'], :]) + _dot(t['du'], wut_ref[t['cols'], :])

        dhn = jnp.zeros((tm, D_MODEL), F32)
        back_down(strips[0])
        for i, t in enumerate(strips):
            if i + 1 < len(strips):
                back_down(strips[i + 1])
            back_act(t)
            dhn = dhn + back_in(t)
        n, r = _rms_stats(h_ref[...])
        dgn_ref[...] += jnp.sum(dhn * n, axis=0, keepdims=True)
        o_ref[...] = dh + _rms_bwd(dhn * g_ref[...], n, r)

    return pl.pallas_call(
        body, name="ffn_bwd", grid=(s // tm,),
        in_specs=[_rows(tm, D_MODEL), _rows(tm, D_MODEL), _rows(tm, D_FF), _rows(tm, D_FF), _full((1, D_MODEL)),
                  _full((D_FF, D_MODEL)), _full((D_FF, D_MODEL)), _full((D_FF, D_MODEL))],
        out_specs=[_rows(tm, D_MODEL), _rows(tm, D_FF), _rows(tm, D_FF), _rows(tm, D_FF), _full((1, D_MODEL))],
        out_shape=[jax.ShapeDtypeStruct((s, D_MODEL), F32), jax.ShapeDtypeStruct((s, D_FF), MXU_DTYPE),
                   jax.ShapeDtypeStruct((s, D_FF), MXU_DTYPE), jax.ShapeDtypeStruct((s, D_FF), MXU_DTYPE),
                   jax.ShapeDtypeStruct((1, D_MODEL), F32)],
        compiler_params=_params(("arbitrary",)),
    )(dh2, h1, gate, up, g_ffn, w_down, w_gate_t, w_up_t)


def _out_bwd(dh1, y_b, g_b, w_out):
    s = dh1.shape[0]
    tm = 512
    nc = WIDTH_B // LANES

    def body(dh_ref, *refs):
        yb_refs = refs[:nc]
        g_ref, w_ref, dya_ref, dyb_ref, dg_ref, scr = refs[nc:]

        @pl.when(pl.program_id(0) == 0)
        def _():
            dg_ref[...] = jnp.zeros_like(dg_ref)

        dy = _dot_nt(dh_ref[...].astype(MXU_DTYPE), w_ref[...])
        dya_ref[...] = dy[:, :WIDTH_A]
        dyb = dy[:, WIDTH_A:]
        n, r = _rms_stats(_load_l256(yb_refs, tm))
        dg_ref[...] += jnp.sum(dyb * n, axis=0, keepdims=True)
        dyb_in = _rms_bwd(dyb * g_ref[...], n, r)
        for j in range(nc):
            cols = slice(j * LANES, (j + 1) * LANES)
            _store_l256(scr, dyb_ref, cols, dyb_in[:, cols])

    return pl.pallas_call(
        body, name="out_bwd", grid=(s // tm,), scratch_shapes=[pltpu.VMEM((tm, LANES), F32)],
        in_specs=[_rows(tm, D_MODEL)] + _col_specs(tm, WIDTH_B) + [_full((1, WIDTH_B)), _full((D_MODEL, D_MODEL))],
        out_specs=[_rows(tm, WIDTH_A), _rows(tm, WIDTH_B), _full((1, WIDTH_B))],
        out_shape=[jax.ShapeDtypeStruct((s, WIDTH_A), F32), jax.ShapeDtypeStruct((s, WIDTH_B), F32),
                   jax.ShapeDtypeStruct((1, WIDTH_B), F32)],
        compiler_params=_params(("arbitrary",)),
    )(dh1, *([y_b] * nc), g_b, w_out)


def _attn_bwd_branch(q, k, v, do, o, lse, grads, dil):
    s = q.shape[0]
    br = _Branch(dil, s)
    qn, nb = br.qn, br.nb
    first = grads is None

    def body(*refs):
        bias_ref, q_ref, kc_ref, kp_ref, vc_ref, vp_ref, do_ref, o_ref, lse_ref = refs[:9]
        if first:
            rest = refs[9:]
        else:
            dq_in, dk_in, dv_in = refs[9:12]
            rest = refs[12:]
        dq_ref, dk_ref, dv_ref, dk_carry, dv_carry = rest
        n = pl.program_id(1)

        @pl.when(n == 0)
        def _():
            dk_carry[...] = jnp.zeros_like(dk_carry)
            dv_carry[...] = jnp.zeros_like(dv_carry)

        @pl.when(n < nb)
        def _():
            bias2 = jnp.concatenate([bias_ref[0], bias_ref[0]], axis=0)
            lane = lax.broadcasted_iota(jnp.int32, (qn, LANES), 1)
            lo = lane < HEAD_DIM
            mask_f = lo.astype(F32)
            mask_lo = mask_f.astype(MXU_DTYPE)
            def prepare(j, hp):
                cols = slice(hp * LANES, (hp + 1) * LANES)
                qp = br.load(q_ref, cols, j)
                dop = br.load(do_ref, cols, j)
                prod = dop * br.load(o_ref, cols, j)
                prod_lo = prod * mask_f
                lse = br.load(lse_ref, cols, j)
                return dict(
                    j=j, cols=cols,
                    kcat=jnp.concatenate([br.load(kp_ref, cols, j), br.load(kc_ref, cols, j)], axis=0),
                    vcat=jnp.concatenate([br.load(vp_ref, cols, j), br.load(vc_ref, cols, j)], axis=0),
                    qs=jnp.concatenate([qp * mask_lo, qp * (1 - mask_lo)], axis=0),
                    dos=jnp.concatenate([dop * mask_f, dop * (1.0 - mask_f)], axis=0).astype(MXU_DTYPE),
                    delta=jnp.concatenate([jnp.sum(prod_lo, axis=1, keepdims=True),
                                           jnp.sum(prod - prod_lo, axis=1, keepdims=True)], axis=0),
                    lse2=jnp.concatenate([lse[:, :1], lse[:, HEAD_DIM:HEAD_DIM + 1]], axis=0))

            def scores(t):
                t['sc'] = _dot_nt(t['qs'], t['kcat'])
                t['dp'] = _dot_nt(t['dos'], t['vcat'])

            def softmax(t):
                p = jnp.exp(t['sc'] + bias2 - t['lse2'])
                t['ds'] = (p * (t['dp'] - t['delta'])).astype(MXU_DTYPE)
                t['p'] = p.astype(MXU_DTYPE)

            def gradients(t):
                t['dvc'] = _dot_tn(t['p'], t['dos'])
                t['dkc'] = _dot_tn(t['ds'], t['qs'])
                t['dq2'] = _dot(t['ds'], t['kcat'])

            def store(t):
                j, cols, dkc, dvc = t['j'], t['cols'], t['dkc'], t['dvc']
                dq = jnp.where(lo, t['dq2'][:qn], t['dq2'][qn:])
                dk_prev = dk_carry[j, :, cols] + dkc[:qn]
                dv_prev = dv_carry[j, :, cols] + dvc[:qn]
                if not first:
                    dq = dq + br.load(dq_in, cols, j)
                    dk_prev = dk_prev + br.load(dk_in, cols, j)
                    dv_prev = dv_prev + br.load(dv_in, cols, j)
                br.store(dq_ref, cols, dq, j)
                br.store(dk_ref, cols, dk_prev, j)
                br.store(dv_ref, cols, dv_prev, j)
                dk_carry[j, :, cols] = dkc[qn:]
                dv_carry[j, :, cols] = dvc[qn:]

            for j, first_pair in itertools.product(range(br.res), range(0, HEADS_B // 2, PAIRS_ABREAST)):
                group = [prepare(j, hp) for hp in range(first_pair, first_pair + PAIRS_ABREAST)]
                for stage in (scores, softmax, gradients, store):
                    for t in group:
                        stage(t)

        @pl.when(n == nb)
        def _():
            for j in range(br.res):
                dk_last = dk_carry[j]
                dv_last = dv_carry[j]
                if not first:
                    dk_last = dk_last + br.load(dk_in, slice(None), j)
                    dv_last = dv_last + br.load(dv_in, slice(None), j)
                br.store(dk_ref, slice(None), dk_last, j)
                br.store(dv_ref, slice(None), dv_last, j)

    cur = lambda n: jnp.minimum(n, nb - 1)
    before = lambda n: jnp.maximum(cur(n) - 1, 0)
    late = lambda n: jnp.maximum(n - 1, 0)
    in_specs = [br.bias_spec(cur), br.spec(WIDTH_B, cur), br.spec(WIDTH_B, cur), br.spec(WIDTH_B, before),
                br.spec(WIDTH_B, cur), br.spec(WIDTH_B, before), br.spec(WIDTH_B, cur), br.spec(WIDTH_B, cur),
                br.spec(WIDTH_B, cur)]
    args = [jnp.asarray(br.bias)] + [br.view(a) for a in (q, k, k, v, v, do, o, lse)]
    if not first:
        in_specs += [br.spec(WIDTH_B, cur), br.spec(WIDTH_B, late), br.spec(WIDTH_B, late)]
        args += [br.view(g) for g in grads]
    res = pl.pallas_call(
        body, name="attn_bwd_d%d" % dil, grid=(br.grid[0], nb + 1), in_specs=in_specs,
        out_specs=[br.spec(WIDTH_B, cur), br.spec(WIDTH_B, late), br.spec(WIDTH_B, late)],
        out_shape=[jax.ShapeDtypeStruct((s // L_BLOCK, 4, 4, L_GROUP, WIDTH_B), F32)] * 3,
        scratch_shapes=[pltpu.VMEM((br.res, qn, WIDTH_B), F32), pltpu.VMEM((br.res, qn, WIDTH_B), F32)],
        compiler_params=_params(("arbitrary", "arbitrary")),
    )(*args)
    return tuple(a.reshape(s, WIDTH_B) for a in res)


def _sgu_bwd(uv, dya_n, w_tril, w_tril_t, bias, g_sgu, g_a):
    s = uv.shape[0]
    tm = 512

    def body(uv_ref, dy_ref, w_ref, wt_ref, b_ref, gs_ref, ga_ref, duv_ref, dw_ref, db_ref, dgs_ref, dga_ref,
             db_acc):
        i = pl.program_id(0)

        @pl.when(i == 0)
        def _():
            dw_ref[...] = jnp.zeros_like(dw_ref)
            dgs_ref[...] = jnp.zeros_like(dgs_ref)
            dga_ref[...] = jnp.zeros_like(dga_ref)
            db_acc[...] = jnp.zeros_like(db_acc)

        t = _sgu_forward_tile(uv_ref[...], w_ref, b_ref[...], gs_ref[...])
        na, ra = _rms_stats(t['ya'])
        dyn = dy_ref[...]
        dga_ref[...] += jnp.sum(dyn * na, axis=0, keepdims=True)
        dya = _rms_bwd(dyn * ga_ref[...], na, ra)
        dug = dya * t['mixed']
        dmixed = dya * t['ug']
        dmb = dmixed.astype(MXU_DTYPE)
        masks = _half_masks(MXU_DTYPE)
        chunks = []
        db = jnp.zeros((CHUNK, WIDTH_A), F32)
        for c in range(tm // CHUNK):
            rows = slice(c * CHUNK, (c + 1) * CHUNK)
            db = db + dmixed[rows]
            groups = []
            for gp in range(2):
                cols = slice(gp * LANES, (gp + 1) * LANES)
                dm_g = dmb[rows, cols]
                vn_g = t['vn'][rows, cols]
                dvn_g = jnp.zeros((CHUNK, LANES), F32)
                for j in range(2):
                    dm_h = dm_g * masks[j]
                    dvn_g = dvn_g + _dot(wt_ref[2 * gp + j], dm_h)
                    dw_ref[2 * gp + j] += _dot_nt(dm_h, vn_g)
                groups.append(dvn_g)
            chunks.append(jnp.concatenate(groups, axis=1))
        db_acc[...] += db
        dvn = jnp.concatenate(chunks, axis=0)
        vhat = t['vhat']
        dgs_ref[...] += jnp.sum(dvn * vhat, axis=0, keepdims=True)
        dvh = dvn * gs_ref[...]
        dvg = t['rs'] * (dvh - jnp.mean(dvh, axis=-1, keepdims=True)
                         - vhat * jnp.mean(dvh * vhat, axis=-1, keepdims=True))
        duv_ref[:, :WIDTH_A] = (dug * _gelu_grad(t['u'], t['tu'])).astype(MXU_DTYPE)
        duv_ref[:, WIDTH_A:] = (dvg * _gelu_grad(t['v'], t['tv'])).astype(MXU_DTYPE)

        @pl.when(i == pl.num_programs(0) - 1)
        def _():
            lane_a = lax.broadcasted_iota(jnp.int32, (CHUNK, WIDTH_A), 1)
            lane = lax.broadcasted_iota(jnp.int32, (CHUNK, LANES), 1)
            acc = db_acc[...]
            out = jnp.zeros((CHUNK, LANES), F32)
            for h in range(HEADS_A):
                col = jnp.sum(jnp.where(lane_a // HEAD_DIM == h, acc, 0.0), axis=1, keepdims=True)
                out = jnp.where(lane == h, col, out)
            db_ref[...] = out
            causal = (lax.broadcasted_iota(jnp.int32, (CHUNK, CHUNK), 0)
                      >= lax.broadcasted_iota(jnp.int32, (CHUNK, CHUNK), 1))
            for h in range(HEADS_A):
                dw_ref[h] = jnp.where(causal, dw_ref[h], 0.0)

    return pl.pallas_call(
        body, name="sgu_bwd", grid=(s // tm,),
        in_specs=[_rows(tm, 2 * WIDTH_A), _rows(tm, WIDTH_A), _full((HEADS_A, CHUNK, CHUNK)),
                  _full((HEADS_A, CHUNK, CHUNK)), _full((CHUNK, WIDTH_A)), _full((1, WIDTH_A)),
                  _full((1, WIDTH_A))],
        out_specs=[_rows(tm, 2 * WIDTH_A), _full((HEADS_A, CHUNK, CHUNK)), _full((CHUNK, LANES)),
                   _full((1, WIDTH_A)), _full((1, WIDTH_A))],
        out_shape=[jax.ShapeDtypeStruct((s, 2 * WIDTH_A), MXU_DTYPE),
                   jax.ShapeDtypeStruct((HEADS_A, CHUNK, CHUNK), F32), jax.ShapeDtypeStruct((CHUNK, LANES), F32),
                   jax.ShapeDtypeStruct((1, WIDTH_A), F32), jax.ShapeDtypeStruct((1, WIDTH_A), F32)],
        scratch_shapes=[pltpu.VMEM((CHUNK, WIDTH_A), F32)],
        compiler_params=_params(("arbitrary",)),
    )(uv, dya_n, w_tril, w_tril_t, bias, g_sgu, g_a)


def _in_bwd_proj(duv, dq, dk, dv, cos_t, sin_t):
    s = duv.shape[0]
    tm = 512
    nc = WIDTH_B // LANES

    def body(duv_ref, *refs):
        dq_refs, dk_refs, dv_refs = refs[:nc], refs[nc:2 * nc], refs[2 * nc:3 * nc]
        cos_ref, sin_ref, dp_ref = refs[3 * nc:]
        cos = cos_ref[...]
        sin = sin_ref[...]
        dp_ref[:, :2 * WIDTH_A] = duv_ref[...]
        for i in range(nc):
            lo = 2 * WIDTH_A + i * LANES
            tq = _load_l256(dq_refs[i:i + 1], tm) * (HEAD_DIM ** -0.5)
            tk = _load_l256(dk_refs[i:i + 1], tm)
            dp_ref[:, lo:lo + LANES] = (tq * cos + _rope_partner(tq * sin)).astype(MXU_DTYPE)
            dp_ref[:, lo + WIDTH_B:lo + WIDTH_B + LANES] = (tk * cos + _rope_partner(tk * sin)).astype(MXU_DTYPE)
            dp_ref[:, lo + 2 * WIDTH_B:lo + 2 * WIDTH_B + LANES] = _load_l256(dv_refs[i:i + 1], tm).astype(MXU_DTYPE)

    return pl.pallas_call(
        body, name="in_bwd_proj", grid=(s // tm,),
        in_specs=[_rows(tm, 2 * WIDTH_A)] + 3 * _col_specs(tm, WIDTH_B) + [_rows(tm, LANES), _rows(tm, LANES)],
        out_specs=_rows(tm, IN_COLS), out_shape=jax.ShapeDtypeStruct((s, IN_COLS), MXU_DTYPE),
        compiler_params=_params(("arbitrary",)),
    )(duv, *([dq] * nc), *([dk] * nc), *([dv] * nc), cos_t, sin_t)


def _in_bwd_x(dproj, w_in_t, x, g_mix, dh1):
    s = x.shape[0]
    tm = 512

    def body(dp_ref, wt_ref, x_ref, g_ref, dh_ref, gx_ref, dg_ref):
        @pl.when(pl.program_id(0) == 0)
        def _():
            dg_ref[...] = jnp.zeros_like(dg_ref)

        dhn = _dot(dp_ref[...], wt_ref[...])
        n, r = _rms_stats(x_ref[...])
        dg_ref[...] += jnp.sum(dhn * n, axis=0, keepdims=True)
        gx_ref[...] = dh_ref[...] + _rms_bwd(dhn * g_ref[...], n, r)

    return pl.pallas_call(
        body, name="in_bwd_x", grid=(s // tm,),
        in_specs=[_rows(tm, IN_COLS), _full((IN_COLS, D_MODEL)), _rows(tm, D_MODEL), _full((1, D_MODEL)),
                  _rows(tm, D_MODEL)],
        out_specs=[_rows(tm, D_MODEL), _full((1, D_MODEL))],
        out_shape=[jax.ShapeDtypeStruct((s, D_MODEL), F32), jax.ShapeDtypeStruct((1, D_MODEL), F32)],
        compiler_params=_params(("arbitrary",)),
    )(dproj, w_in_t, x, g_mix, dh1)


def _wgrad(a, b, name):
    s, m = a.shape
    n = b.shape[1]
    bm = 512 if m % 512 == 0 else FF_HALF
    ts = 1024
    nsteps = s // ts

    def body(a_ref, b_ref, o_ref, acc):
        kk = pl.program_id(1)

        @pl.when(kk == 0)
        def _():
            acc[...] = jnp.zeros_like(acc)

        acc[...] += _dot_tn(a_ref[...].astype(MXU_DTYPE), b_ref[...].astype(MXU_DTYPE))

        @pl.when(kk == nsteps - 1)
        def _():
            o_ref[...] = acc[...].astype(o_ref.dtype)

    return pl.pallas_call(
        body, name=name, grid=(m // bm, nsteps),
        in_specs=[pl.BlockSpec((ts, bm), lambda i, kk: (kk, i)), pl.BlockSpec((ts, n), lambda i, kk: (kk, 0))],
        out_specs=pl.BlockSpec((bm, n), lambda i, kk: (i, 0)), out_shape=jax.ShapeDtypeStruct((m, n), jnp.bfloat16),
        scratch_shapes=[pltpu.VMEM((bm, n), F32)],
        compiler_params=_params(("arbitrary", "arbitrary")),
    )(a, b)


def _rope_tables(s):
    half = HEAD_DIM // 2
    inv = ROPE_THETA ** (-jnp.arange(half, dtype=F32) / half)
    ang = jnp.arange(s, dtype=F32)[:, None] * jnp.tile(inv, LANES // half)[None, :]
    sign = jnp.tile(jnp.concatenate([-jnp.ones(half, F32), jnp.ones(half, F32)]), LANES // HEAD_DIM)
    return jnp.cos(ang), jnp.sin(ang) * sign[None, :]


MESH = pl.DeviceIdType.MESH
ANY = pl.BlockSpec(memory_space=pl.ANY)
SEM = pl.BlockSpec(memory_space=pltpu.SEMAPHORE)
SPLIT_COPY = pltpu.CompilerParams(has_side_effects=pltpu.SideEffectType.DATAFLOW_SIDE_EFFECTING)
SLAB_IS_TRANSPOSED = {'w_in': True, 'w_out': False, 'w_gate': True, 'w_up': True, 'w_down': False,
                      'w_ple_gate': False, 'w_ple_proj': True}


def _place():
    x, y, c = lax.axis_index("x"), lax.axis_index("y"), lax.axis_index("c")
    other_chips = [(1 - x, y), (x, 1 - y), (1 - x, 1 - y)]
    return x, y, c, other_chips


def _chip_of(chip):
    return 2 * chip[0] + chip[1]


def _half(ref, lead, hc):
    hr = ref.shape[1] // 2
    return ref.at[lead, pl.ds(hc * hr, hr), :]


def _put_own(stack, own, index):
    return lax.dynamic_update_slice(stack, own[None], (index,) + (0,) * own.ndim)


def _all_gather_now(slab):
    rows, cols = slab.shape

    def body(x_ref, out_ref, send_sems, recv_sems):
        x, y, c, chips = _place()
        sibling = (x, y, 1 - c)
        hr = rows // 2

        def copy(k, src, dst, to):
            return pltpu.make_async_remote_copy(src_ref=src, dst_ref=dst, send_sem=send_sems.at[k],
                                                recv_sem=recv_sems.at[k], device_id=to, device_id_type=MESH)

        my_half = x_ref.at[pl.ds(c * hr, hr), :]
        first = [copy(j, my_half, _half(out_ref, 2 * x + y, c), (*chip, c)) for j, chip in enumerate(chips)]
        for cp in first:
            cp.start()
        passed = [copy(3 + j, _half(out_ref, _chip_of(chip), c), _half(out_ref, _chip_of(chip), c), sibling)
                  for j, chip in enumerate(chips)]
        for j, chip in enumerate(chips):
            copy(j, my_half, _half(out_ref, _chip_of(chip), c), (*chip, c)).wait_recv()
            passed[j].start()
        for j, chip in enumerate(chips):
            copy(3 + j, my_half, _half(out_ref, _chip_of(chip), 1 - c), sibling).wait_recv()
        for cp in first + passed:
            cp.wait_send()

    gathered = pl.pallas_call(
        body, name="all_gather_now", out_shape=jax.ShapeDtypeStruct((N_CHIPS, rows, cols), slab.dtype),
        in_specs=[ANY], out_specs=ANY,
        scratch_shapes=[pltpu.SemaphoreType.DMA((6,)), pltpu.SemaphoreType.DMA((6,))],
    )(slab)
    me = 2 * lax.axis_index("x") + lax.axis_index("y")
    return _put_own(gathered, slab, me).reshape(N_CHIPS * rows, cols)


def _gather_copies(slab_refs, land_refs, send_sems, recv_sems):
    x, y, c, chips = _place()
    sends, recvs = [], []
    for k, (src, land) in enumerate(zip(slab_refs, land_refs)):
        hr = src.shape[0] // 2
        for j, chip in enumerate(chips):
            for t in range(2):
                sends.append(pltpu.make_async_remote_copy(
                    src_ref=src.at[pl.ds(c * hr, hr), :], dst_ref=_half(land, 2 * x + y, c),
                    send_sem=send_sems.at[6 * k + 2 * j + t], recv_sem=recv_sems.at[6 * k + 2 * j + c],
                    device_id=(*chip, t), device_id_type=MESH))
                recvs.append(pltpu.make_async_remote_copy(
                    src_ref=src.at[pl.ds(t * hr, hr), :], dst_ref=_half(land, _chip_of(chip), t),
                    send_sem=send_sems.at[6 * k + 2 * j + t], recv_sem=recv_sems.at[6 * k + 2 * j + t],
                    device_id=(*chip, t), device_id_type=MESH))
    return sends, recvs


def _all_gather_start(slabs, after):
    n = len(slabs)

    def body(*refs):
        slab_refs, land_refs = refs[:n], refs[n:2 * n]
        send_sems, recv_sems = refs[2 * n + 1:2 * n + 3]
        token = refs[-1]
        sends, _ = _gather_copies(slab_refs, land_refs, send_sems, recv_sems)
        for cp in sends:
            cp.start()
        token[...] = jnp.zeros_like(token)

    lands = [lax.empty((N_CHIPS,) + s.shape, s.dtype) for s in slabs]
    hbm = lambda a: pltpu.HBM(a.shape, a.dtype)
    res = pl.pallas_call(
        body, name="all_gather_start",
        out_shape=(pltpu.SemaphoreType.DMA((6 * n,)), pltpu.SemaphoreType.DMA((6 * n,)), *map(hbm, slabs),
                   *map(hbm, lands), jax.ShapeDtypeStruct((8, LANES), F32)),
        in_specs=[ANY] * (2 * n + 1),
        out_specs=(SEM, SEM, *([ANY] * (2 * n)), pl.BlockSpec(memory_space=pltpu.VMEM)),
        input_output_aliases={i: 2 + i for i in range(2 * n)}, compiler_params=SPLIT_COPY,
    )(*[pltpu.with_memory_space_constraint(a, pltpu.HBM) for a in list(slabs) + lands], after)
    return res[:-1], res[-1]


def _all_gather_wait(handle, after):
    send_sems, recv_sems = handle[:2]
    n = (len(handle) - 2) // 2
    slabs, lands = handle[2:2 + n], handle[2 + n:]

    def body(*refs):
        slab_refs, land_refs = refs[:n], refs[n:2 * n]
        send_sems, recv_sems = refs[2 * n:2 * n + 2]
        sends, recvs = _gather_copies(slab_refs, land_refs, send_sems, recv_sems)
        for cp in sends:
            cp.wait_send()
        for cp in recvs:
            cp.wait_recv()

    hbm = lambda a: pltpu.HBM(a.shape, a.dtype)
    res = pl.pallas_call(
        body, name="all_gather_wait", out_shape=tuple(map(hbm, list(slabs) + list(lands))),
        in_specs=[ANY] * (2 * n) + [SEM, SEM, ANY], out_specs=tuple([ANY] * (2 * n)),
        input_output_aliases={i: i for i in range(2 * n)}, compiler_params=SPLIT_COPY,
    )(*slabs, *lands, send_sems, recv_sems, after)
    me = 2 * lax.axis_index("x") + lax.axis_index("y")
    return [_put_own(land, slab, me).reshape(N_CHIPS * slab.shape[0], slab.shape[1])
            for slab, land in zip(res[:n], res[n:])]


def _scatter_copies(part_refs, land_refs, send_sems, recv_sems):
    x, y, c, chips = _place()
    me = 4 * x + 2 * y + c
    sends, recvs = [], []
    for k, (part, land) in enumerate(zip(part_refs, land_refs)):
        for j, chip in enumerate(chips):
            for t in range(2):
                sends.append(pltpu.make_async_remote_copy(
                    src_ref=part.at[_chip_of(chip)], dst_ref=land.at[me],
                    send_sem=send_sems.at[7 * k + 2 * j + t], recv_sem=recv_sems.at[7 * k + 2 * j + c],
                    device_id=(*chip, t), device_id_type=MESH))
                recvs.append(pltpu.make_async_remote_copy(
                    src_ref=part.at[_chip_of(chip)], dst_ref=land.at[2 * _chip_of(chip) + t],
                    send_sem=send_sems.at[7 * k + 2 * j + t], recv_sem=recv_sems.at[7 * k + 2 * j + t],
                    device_id=(*chip, t), device_id_type=MESH))
        sends.append(pltpu.make_async_remote_copy(
            src_ref=part.at[2 * x + y], dst_ref=land.at[me], send_sem=send_sems.at[7 * k + 6],
            recv_sem=recv_sems.at[7 * k + 6], device_id=(x, y, 1 - c), device_id_type=MESH))
        recvs.append(pltpu.make_async_remote_copy(
            src_ref=part.at[2 * x + y], dst_ref=land.at[4 * x + 2 * y + 1 - c],
            send_sem=send_sems.at[7 * k + 6], recv_sem=recv_sems.at[7 * k + 6], device_id=(x, y, 1 - c),
            device_id_type=MESH))
    return sends, recvs


def _reduce_scatter_start(parts, name):
    n = len(parts)
    parts = [p.reshape(N_CHIPS, p.shape[0] // N_CHIPS, p.shape[1]) for p in parts]

    def body(*refs):
        part_refs, land_refs = refs[:n], refs[n:2 * n]
        send_sems, recv_sems = refs[2 * n:2 * n + 2]
        token = refs[-1]
        sends, _ = _scatter_copies(part_refs, land_refs, send_sems, recv_sems)
        for cp in sends:
            cp.start()
        token[...] = jnp.zeros_like(token)

    lands = [lax.empty((N_DEV, p.shape[1], p.shape[2]), p.dtype) for p in parts]
    hbm = lambda a: pltpu.HBM(a.shape, a.dtype)
    res = pl.pallas_call(
        body, name=name,
        out_shape=(pltpu.SemaphoreType.DMA((7 * n,)), pltpu.SemaphoreType.DMA((7 * n,)), *map(hbm, parts),
                   *map(hbm, lands), jax.ShapeDtypeStruct((8, LANES), F32)),
        in_specs=[ANY] * (2 * n), out_specs=(SEM, SEM, *([ANY] * (2 * n)), pl.BlockSpec(memory_space=pltpu.VMEM)),
        input_output_aliases={i: 2 + i for i in range(2 * n)}, compiler_params=SPLIT_COPY,
    )(*[pltpu.with_memory_space_constraint(a, pltpu.HBM) for a in parts + lands])
    return res[:-1], res[-1]


def _reduce_scatter_wait(handle, after, name):
    after = list(after) if isinstance(after, (list, tuple)) else [after]
    send_sems, recv_sems = handle[:2]
    n = (len(handle) - 2) // 2
    parts, lands = handle[2:2 + n], handle[2 + n:]

    def body(*refs):
        part_refs, land_refs = refs[:n], refs[n:2 * n]
        send_sems, recv_sems = refs[2 * n:2 * n + 2]
        sends, recvs = _scatter_copies(part_refs, land_refs, send_sems, recv_sems)
        for cp in sends:
            cp.wait_send()
        for cp in recvs:
            cp.wait_recv()

    hbm = lambda a: pltpu.HBM(a.shape, a.dtype)
    res = pl.pallas_call(
        body, name=name, out_shape=tuple(map(hbm, list(parts) + list(lands))),
        in_specs=[ANY] * (2 * n) + [SEM, SEM] + [ANY] * len(after), out_specs=tuple([ANY] * (2 * n)),
        input_output_aliases={i: i for i in range(2 * n)}, compiler_params=SPLIT_COPY,
    )(*parts, *lands, send_sems, recv_sems, *after)
    return list(zip(res[:n], res[n:]))


def _adamw_of_shares(w, own, land, m, v, name):
    rows, cols = w.shape
    tm = rows // 4 if rows % 32 == 0 else rows
    x, y, c = lax.axis_index("x"), lax.axis_index("y"), lax.axis_index("c")
    where = jnp.stack([2 * x + y, 4 * x + 2 * y + c]).astype(jnp.int32)

    def body(where_ref, w_ref, own_ref, land_ref, m_ref, v_ref, g_ref, d_ref, nm_ref, nv_ref):
        me = where_ref[1]
        g_ = jnp.zeros((tm, cols), F32)
        for dev in range(N_DEV):
            g_ = g_ + jnp.where(me == dev, own_ref[0], land_ref[dev]).astype(F32)
        m_ = ADAM_B1 * m_ref[...] + (1.0 - ADAM_B1) * g_
        v_ = ADAM_B2 * v_ref[...] + (1.0 - ADAM_B2) * (g_ * g_)
        m_hat = m_ / (1.0 - ADAM_B1 ** ADAM_STEP)
        v_hat = v_ / (1.0 - ADAM_B2 ** ADAM_STEP)
        g_ref[...] = g_
        d_ref[...] = -ADAM_LR * (m_hat / (jnp.sqrt(v_hat) + ADAM_EPS) + ADAM_WD * w_ref[...])
        nm_ref[...] = m_
        nv_ref[...] = v_

    tile = pl.BlockSpec((tm, cols), lambda i, where_ref: (i, 0))
    spec = pltpu.PrefetchScalarGridSpec(
        num_scalar_prefetch=1, grid=(rows // tm,),
        in_specs=[tile, pl.BlockSpec((1, tm, cols), lambda i, where_ref: (where_ref[0], i, 0)),
                  pl.BlockSpec((N_DEV, tm, cols), lambda i, where_ref: (0, i, 0)), tile, tile],
        out_specs=[tile] * 4)
    return pl.pallas_call(
        body, name=name, grid_spec=spec, out_shape=[jax.ShapeDtypeStruct(w.shape, F32)] * 4,
        compiler_params=_params(("arbitrary",)),
    )(where, w, own, land, m, v)


def _pack_small(values):
    flat = jnp.concatenate([values[n].reshape(-1).astype(F32) for n in SMALL])
    return jnp.pad(flat, (0, SMALL_ROWS * D_MODEL - flat.shape[0])).reshape(SMALL_ROWS, D_MODEL)


def _unpack_small(block, shapes):
    flat = block.reshape(-1)
    out, lo = {}, 0
    for n in SMALL:
        out[n] = flat[lo:lo + SMALL_SIZES[n]].reshape(shapes[n])
        lo += SMALL_SIZES[n]
    return out


def _after(token, a):
    return a + token[:1, :1].astype(a.dtype)


def kernel(x, p, mix_norm_g, w_in, sgu_w, sgu_b, sgu_norm_g, out_norm_a, out_norm_b, w_out, ffn_norm_g, w_gate, w_up, w_down, ple_norm_g, w_ple_gate, w_ple_proj, final_norm_g, loss_target, m_mix_norm_g, m_w_in, m_sgu_w, m_sgu_b, m_sgu_norm_g, m_out_norm_a, m_out_norm_b, m_w_out, m_ffn_norm_g, m_w_gate, m_w_up, m_w_down, m_ple_norm_g, m_w_ple_gate, m_w_ple_proj, m_final_norm_g, v_mix_norm_g, v_w_in, v_sgu_w, v_sgu_b, v_sgu_norm_g, v_out_norm_a, v_out_norm_b, v_w_out, v_ffn_norm_g, v_w_gate, v_w_up, v_w_down, v_ple_norm_g, v_w_ple_gate, v_w_ple_proj, v_final_norm_g):
    given = dict(locals())
    drop_lead = lambda a, lead: a.reshape(a.shape[lead:])
    xs, ps, target = drop_lead(x, 1), drop_lead(p, 2), drop_lead(loss_target, 1)
    s = xs.shape[0]
    shard = lambda name: drop_lead(given[name], 1)

    def slab_of(name):
        local = shard(name).astype(MXU_DTYPE)
        return local.T if SLAB_IS_TRANSPOSED[name] else local

    w_in_t = _all_gather_now(slab_of('w_in'))
    later = ['w_out', 'w_gate', 'w_up', 'w_down', 'w_ple_gate', 'w_ple_proj']
    gather, token = _all_gather_start([slab_of(n) for n in later], w_in_t)

    cos_t, sin_t = _rope_tables(s)
    tril = jnp.tril(jnp.ones((CHUNK, CHUNK), F32))
    w_tril = (sgu_w.reshape(HEADS_A, CHUNK, CHUNK) * tril).astype(MXU_DTYPE)
    w_tril_t = jnp.swapaxes(w_tril, 1, 2)
    bias = jnp.repeat(sgu_b.reshape(HEADS_A, CHUNK).T, HEAD_DIM, axis=1)
    g = {n: given[n].reshape(1, -1) for n in SMALL if n not in ('sgu_w', 'sgu_b')}

    uv, q, k, v, q1, k1, v1, hn1 = _in_fwd(xs, _after(token, g['mix_norm_g']), w_in_t, cos_t, sin_t)
    ya_n = _sgu_fwd(uv, w_tril, bias, g['sgu_norm_g'], g['out_norm_a'])
    branches = [_attn_fwd_local(q1, k1, v1)] + [_attn_fwd_branch(q, k, v, dil) for dil in DILATIONS[:-1]]
    y_b, lse = _attn_fwd_branch(q, k, v, DILATIONS[-1], earlier=branches)
    stacks = dict(zip(later, _all_gather_wait(gather, lse)))
    w_gate_t, w_up_t, w_pp_t = stacks['w_gate'], stacks['w_up'], stacks['w_ple_proj']
    h1, y_n = _out_fwd(ya_n, y_b, g['out_norm_b'], stacks['w_out'], xs)
    h2, gate, up, hn2 = _ffn_fwd(h1, g['ffn_norm_g'], w_gate_t, w_up_t, stacks['w_down'])
    loss, dh2, dz, dpp, hn3, d_ple_g, d_final_g = _ple_loss(
        h2, ps, target, g['ple_norm_g'], stacks['w_ple_gate'], w_pp_t, g['final_norm_g'])

    share = {}
    share['w_ple_gate'] = _wgrad(hn3, dz, "wgrad_ple_gate")
    share['w_ple_proj'] = _wgrad(dpp, ps, "wgrad_ple_proj")
    scatter_1, token = _reduce_scatter_start([share['w_ple_gate'], share['w_ple_proj']], "reduce_scatter_start_1")
    dh1, act, dgate, dup, d_ffn_g = _ffn_bwd(dh2, h1, gate, up, _after(token, g['ffn_norm_g']), stacks['w_down'],
                                             w_gate_t, w_up_t)
    share['w_down'] = _wgrad(act, dh2, "wgrad_down")
    share['w_gate'] = _wgrad(dgate, hn2, "wgrad_gate")
    share['w_up'] = _wgrad(dup, hn2, "wgrad_up")
    scatter_2, token = _reduce_scatter_start([share['w_down'], share['w_gate'], share['w_up']],
                                             "reduce_scatter_start_2")
    dya_n, dyb, d_out_b = _out_bwd(dh1, y_b, _after(token, g['out_norm_b']), stacks['w_out'])
    share['w_out'] = _wgrad(y_n, dh1, "wgrad_out")
    scatter_3, token = _reduce_scatter_start([share['w_out']], "reduce_scatter_start_3")
    grads = _attn_bwd_local(q1, k1, v1, dyb, y_b, lse)
    for dil in DILATIONS:
        grads = _attn_bwd_branch(q, k, v, dyb, y_b, lse, grads, dil)
    duv, d_sgu_w, d_sgu_b, d_sgu_g, d_out_a = _sgu_bwd(uv, dya_n, w_tril, w_tril_t, bias,
                                                       _after(token, g['sgu_norm_g']), g['out_norm_a'])
    dproj = _in_bwd_proj(duv, grads[0], grads[1], grads[2], cos_t, sin_t)
    share['w_in'] = _wgrad(dproj, hn1, "wgrad_in")
    scatter_4, token = _reduce_scatter_start([share['w_in']], "reduce_scatter_start_4")

    grads, deltas, new_m, new_v = {}, {}, {}, {}
    add_lead = lambda a: a.reshape((1,) + a.shape)

    done = {}

    def finish(names, handles, after, tag):
        landed = []
        for i, handle in enumerate(handles):
            landed += _reduce_scatter_wait(handle, after, "reduce_scatter_wait_%s%d" % (tag, i))
        for n, (own, land) in zip(names, landed):
            turn = (lambda a: a.T) if SLAB_IS_TRANSPOSED[n] else (lambda a: a)
            res = _adamw_of_shares(turn(shard(n)), own, land, turn(shard("m_" + n)), turn(shard("v_" + n)),
                                   "adamw_" + n)
            grads[n], deltas[n], new_m[n], new_v[n] = (add_lead(turn(a)) for a in res)
            done[n] = res[0]

    grad_x, d_mix_g = _in_bwd_x(dproj, w_in_t, xs, _after(token, g['mix_norm_g']), dh1)

    gs = {'mix_norm_g': d_mix_g, 'sgu_w': d_sgu_w, 'sgu_b': d_sgu_b[:, :HEADS_A].T, 'sgu_norm_g': d_sgu_g,
          'out_norm_a': d_out_a, 'out_norm_b': d_out_b, 'ffn_norm_g': d_ffn_g, 'ple_norm_g': d_ple_g,
          'final_norm_g': d_final_g}
    gs_block = _pack_small(gs).at[SMALL_ROWS - 1, 0].set(loss[0, 0])
    to_all = jnp.broadcast_to(gs_block[None], (N_CHIPS,) + gs_block.shape).reshape(-1, D_MODEL)
    scatter_small, token = _reduce_scatter_start([to_all], "small_all_reduce_start")

    finish(['w_ple_gate', 'w_ple_proj', 'w_down', 'w_gate', 'w_up', 'w_out'], [scatter_1, scatter_2, scatter_3], token,
           "early")
    finish(['w_in'], [scatter_4], [done[n] for n in ('w_down', 'w_gate', 'w_up', 'w_out')], "last")
    (own, land), = _reduce_scatter_wait(scatter_small, done['w_in'], "small_all_reduce_wait")
    small = {n: given[n] for n in SMALL}
    small_res = _adamw_of_shares(_pack_small(small), own, land, _pack_small({n: given["m_" + n] for n in SMALL}),
                                 _pack_small({n: given["v_" + n] for n in SMALL}), "adamw_small")
    loss_out = small_res[0][SMALL_ROWS - 1, 0]
    small_shapes = {n: given[n].shape for n in SMALL}
    for res, blk in zip((grads, deltas, new_m, new_v), small_res):
        res.update(_unpack_small(blk, small_shapes))

    outs = [loss_out, add_lead(grad_x)]
    for res in (grads, deltas, new_m, new_v):
        outs += [res[n] for n in WEIGHT_NAMES]
    return tuple(outs)
```

```python
import functools
import itertools
import math

import jax
import jax.numpy as jnp
import numpy as np
from jax import lax
from jax.experimental import pallas as pl
from jax.experimental.pallas import tpu as pltpu

F32 = jnp.float32
MXU_DTYPE = jnp.bfloat16

D_MODEL = 1024
HEAD_DIM = 64
HEADS_A = 4
HEADS_B = 12
WIDTH_A = HEADS_A * HEAD_DIM
WIDTH_B = HEADS_B * HEAD_DIM
CHUNK = 128
BLOCK = 128
DILATIONS = (4, 16)
ROPE_THETA = 10000.0
D_FF = 2816
FF_HALF = D_FF // 2
FF_STRIPS = ((0, 1024), (1024, 2048), (2048, D_FF))
PLE_DIM = 256
IN_COLS = 2 * WIDTH_A + 3 * WIDTH_B
EPS = 1e-6
LANES = 128
N_CHIPS = 4
N_DEV = 8

ADAM_LR = 0.001
ADAM_B1 = 0.9
ADAM_B2 = 0.999
ADAM_EPS = 1e-08
ADAM_WD = 0.01
ADAM_STEP = 10

VMEM_LIMIT = 56 * 1024 * 1024

WEIGHT_NAMES = ['mix_norm_g', 'w_in', 'sgu_w', 'sgu_b', 'sgu_norm_g', 'out_norm_a', 'out_norm_b', 'w_out',
                'ffn_norm_g', 'w_gate', 'w_up', 'w_down', 'ple_norm_g', 'w_ple_gate', 'w_ple_proj', 'final_norm_g']
SMALL = ['mix_norm_g', 'sgu_w', 'sgu_b', 'sgu_norm_g', 'out_norm_a', 'out_norm_b', 'ffn_norm_g', 'ple_norm_g',
         'final_norm_g']
SMALL_SIZES = {'mix_norm_g': 1024, 'sgu_w': 65536, 'sgu_b': 512, 'sgu_norm_g': 256, 'out_norm_a': 256,
               'out_norm_b': 768, 'ffn_norm_g': 1024, 'ple_norm_g': 1024, 'final_norm_g': 1024}
SMALL_ROWS = 72


def _params(semantics=None):
    return pltpu.CompilerParams(dimension_semantics=semantics, vmem_limit_bytes=VMEM_LIMIT)


def _full(shape):
    nd = len(shape)
    return pl.BlockSpec(shape, lambda i: (0,) * nd, pipeline_mode=pl.Buffered(1))


def _rows(tm, width):
    return pl.BlockSpec((tm, width), lambda i: (i, 0))


def _rms_stats(x):
    r = lax.rsqrt(jnp.mean(x * x, axis=-1, keepdims=True) + EPS)
    return x * r, r


def _rms_bwd(dn, n, r):
    return r * (dn - n * jnp.mean(dn * n, axis=-1, keepdims=True))


def _dot(a, b):
    return jnp.dot(a, b, preferred_element_type=F32)


def _dot_nt(a, b):
    return lax.dot_general(a, b, (((1,), (1,)), ((), ())), preferred_element_type=F32)


def _dot_tn(a, b):
    return lax.dot_general(a, b, (((0,), (0,)), ((), ())), preferred_element_type=F32)


def _gelu_parts(x):
    c = math.sqrt(2.0 / math.pi)
    t = jnp.tanh(c * (x + 0.044715 * x * x * x))
    return 0.5 * x * (1.0 + t), t


def _gelu_grad(x, t):
    c = math.sqrt(2.0 / math.pi)
    return 0.5 * (1.0 + t) + 0.5 * x * (1.0 - t * t) * c * (1.0 + 3.0 * 0.044715 * x * x)


def _half_masks(dtype):
    lane = lax.broadcasted_iota(jnp.int32, (BLOCK, LANES), 1)
    lo = (lane < HEAD_DIM).astype(F32)
    return lo.astype(dtype), (1.0 - lo).astype(dtype)


def _rope_partner(t):
    lane = lax.broadcasted_iota(jnp.int32, t.shape, 1)
    first_half = (lane % HEAD_DIM) < (HEAD_DIM // 2)
    return jnp.where(first_half, pltpu.roll(t, LANES - HEAD_DIM // 2, 1), pltpu.roll(t, HEAD_DIM // 2, 1))


PAIRS_ABREAST = 2
RESIDUES_PER_STEP = 4
L_BLOCK = 256
L_GROUP = 16


def _store_l256(scr, out_ref, cols, value, chunk_ref=None):
    tm = value.shape[0]
    half = L_GROUP // 2
    scr[...] = value
    for blk in range(tm // L_BLOCK):
        pieces = [scr[pl.ds(blk * L_BLOCK + r, L_GROUP, stride=L_GROUP), :] for r in range(L_GROUP)]
        for r, piece in enumerate(pieces):
            lo = blk * L_BLOCK + r * L_GROUP
            out_ref[lo:lo + L_GROUP, cols] = piece.astype(out_ref.dtype)
        if chunk_ref is not None:
            for chunk in range(L_BLOCK // BLOCK):
                for r in range(0, L_GROUP, 2):
                    lo = blk * L_BLOCK + chunk * BLOCK + r * half
                    both = [p[chunk * half:(chunk + 1) * half] for p in pieces[r:r + 2]]
                    chunk_ref[lo:lo + L_GROUP, cols] = jnp.concatenate(both, axis=0).astype(chunk_ref.dtype)


def _load_l256(col_refs, tm):
    cols = []
    for ref in col_refs:
        pieces = [ref[pl.ds(blk * L_BLOCK + i, L_GROUP, stride=L_GROUP), :]
                  for blk in range(tm // L_BLOCK) for i in range(L_GROUP)]
        cols.append(jnp.concatenate(pieces, axis=0))
    return jnp.concatenate(cols, axis=1)


def _col_specs(tm, width):
    return [pl.BlockSpec((tm, LANES), lambda i, j=j: (i, j)) for j in range(width // LANES)]


def _in_fwd(x, g_mix, w_in_t, cos_t, sin_t):
    s = x.shape[0]
    tm = 512

    def body(x_ref, g_ref, wt_ref, cos_ref, sin_ref, uv_ref, q_ref, k_ref, v_ref, q1_ref, k1_ref, v1_ref, hn_ref,
             *scrs):
        n, _ = _rms_stats(x_ref[...])
        hn = (n * g_ref[...]).astype(MXU_DTYPE)
        hn_ref[...] = hn
        cos = cos_ref[...]
        sin = sin_ref[...]
        strip = 2 * LANES
        for j in range(IN_COLS // strip):
            proj = _dot_nt(hn, wt_ref[j * strip:(j + 1) * strip, :])
            lo = j * strip - 2 * WIDTH_A
            if lo < 0:
                uv_ref[:, j * strip:(j + 1) * strip] = proj
                continue
            which, lo = divmod(lo, WIDTH_B)
            for i in range(strip // LANES):
                t = proj[:, i * LANES:(i + 1) * LANES]
                cols = slice(lo + i * LANES, lo + (i + 1) * LANES)
                scr = scrs[i]
                if which == 0:
                    _store_l256(scr, q_ref, cols, (t * cos + _rope_partner(t) * sin) * (HEAD_DIM ** -0.5), q1_ref)
                elif which == 1:
                    _store_l256(scr, k_ref, cols, t * cos + _rope_partner(t) * sin, k1_ref)
                else:
                    _store_l256(scr, v_ref, cols, t, v1_ref)

    return pl.pallas_call(
        body, name="in_fwd", grid=(s // tm,), scratch_shapes=[pltpu.VMEM((tm, LANES), F32)] * 2,
        in_specs=[_rows(tm, D_MODEL), _full((1, D_MODEL)), _full((IN_COLS, D_MODEL)), _rows(tm, LANES),
                  _rows(tm, LANES)],
        out_specs=[_rows(tm, 2 * WIDTH_A)] + [_rows(tm, WIDTH_B)] * 6 + [_rows(tm, D_MODEL)],
        out_shape=[jax.ShapeDtypeStruct((s, 2 * WIDTH_A), F32)] + [jax.ShapeDtypeStruct((s, WIDTH_B), MXU_DTYPE)] * 6
        + [jax.ShapeDtypeStruct((s, D_MODEL), MXU_DTYPE)],
        compiler_params=_params(("arbitrary",)),
    )(x, g_mix, w_in_t, cos_t, sin_t)


class _Branch:
    def __init__(self, dil, s):
        i = np.arange(L_GROUP)
        self.res = RESIDUES_PER_STEP
        if dil == 16:
            nblk = BLOCK // 16
            self.grid = (16 // self.res, s // (L_BLOCK * nblk))
            self.shape = (nblk, 1, self.res, L_GROUP)
            self.index = lambda r, n: (n, r // (4 // self.res), r % (4 // self.res), 0, 0)
            pos = (np.arange(nblk)[:, None] * 16 + i[None, :]).reshape(-1)
        else:
            nblk = BLOCK // 64
            self.grid = (4 // self.res, s // (L_BLOCK * nblk))
            self.shape = (nblk, 4, self.res, L_GROUP)
            self.index = lambda r, n: (n, 0, r, 0, 0)
            pos = (np.arange(nblk)[:, None, None] * 64 + np.arange(4)[None, :, None]
                   + 4 * i[None, None, :]).reshape(-1)
        self.qn = pos.shape[0]
        self.nb = self.grid[1]
        dist = pos[:, None] - np.concatenate([pos - self.qn, pos])[None, :]
        band = (dist >= 0) & (dist <= BLOCK)
        start = band & (np.arange(2 * self.qn)[None, :] >= self.qn)
        self.bias = np.where(np.stack([band, start]), 0.0, -np.inf).astype(np.float32)

    def view(self, a):
        return a.reshape(a.shape[0] // L_BLOCK, 4, 4, L_GROUP, a.shape[1])

    def spec(self, w, step=lambda n: n):
        return pl.BlockSpec(self.shape + (w,), lambda r, n: self.index(r, step(n)))

    def bias_spec(self, step=lambda n: n):
        return pl.BlockSpec((1, self.qn, 2 * self.qn), lambda r, n: (jnp.where(step(n) == 0, 1, 0), 0, 0))

    def load(self, ref, cols=slice(None), j=0):
        x = ref[:, :, j, :, cols]
        return x.reshape(self.qn, x.shape[-1])

    def store(self, ref, cols, value, j=0):
        ref[:, :, j, :, cols] = value.reshape(self.shape[:2] + (L_GROUP, value.shape[-1]))


def _attn_fwd_branch(q, k, v, dil, earlier=()):
    s = q.shape[0]
    br = _Branch(dil, s)
    qn = br.qn
    nearly = len(earlier)

    def body(bias_ref, q_ref, kc_ref, kp_ref, vc_ref, vp_ref, *refs):
        early_refs, (o_ref, lse_ref) = refs[:2 * nearly], refs[2 * nearly:]
        bias2 = jnp.concatenate([bias_ref[0], bias_ref[0]], axis=0)
        lo = lax.broadcasted_iota(jnp.int32, (qn, LANES), 1) < HEAD_DIM
        mask_lo = lo.astype(F32).astype(MXU_DTYPE)
        for j, hp in itertools.product(range(br.res), range(HEADS_B // 2)):
            cols = slice(hp * LANES, (hp + 1) * LANES)
            qp = br.load(q_ref, cols, j)
            kcat = jnp.concatenate([br.load(kp_ref, cols, j), br.load(kc_ref, cols, j)], axis=0)
            vcat = jnp.concatenate([br.load(vp_ref, cols, j), br.load(vc_ref, cols, j)], axis=0)
            sc = _dot_nt(jnp.concatenate([qp * mask_lo, qp * (1 - mask_lo)], axis=0), kcat) + bias2
            m = jnp.max(sc, axis=1, keepdims=True)
            p = jnp.exp(sc - m)
            l = jnp.sum(p, axis=1, keepdims=True)
            out = _dot(p.astype(MXU_DTYPE), vcat) / l
            lse = m + jnp.log(l)
            outs = [br.load(r, cols, j) for r in early_refs[:nearly]] + [jnp.where(lo, out[:qn], out[qn:])]
            lses = [br.load(r, cols, j) for r in early_refs[nearly:]] + [jnp.where(lo, lse[:qn], lse[qn:])]
            if nearly:
                top = functools.reduce(jnp.maximum, lses)
                ws = [jnp.exp(x - top) for x in lses]
                den = functools.reduce(jnp.add, ws)
                outs = [functools.reduce(jnp.add, [w * o for w, o in zip(ws, outs)]) / den]
                lses = [top + jnp.log(den)]
            br.store(o_ref, cols, outs[0], j)
            br.store(lse_ref, cols, lses[0], j)

    before = lambda n: jnp.maximum(n - 1, 0)
    res = pl.pallas_call(
        body, name="attn_fwd_d%d" % dil, grid=br.grid,
        in_specs=[br.bias_spec(), br.spec(WIDTH_B), br.spec(WIDTH_B), br.spec(WIDTH_B, before), br.spec(WIDTH_B),
                  br.spec(WIDTH_B, before)] + [br.spec(WIDTH_B)] * (2 * nearly),
        out_specs=[br.spec(WIDTH_B), br.spec(WIDTH_B)],
        out_shape=[jax.ShapeDtypeStruct((s // L_BLOCK, 4, 4, L_GROUP, WIDTH_B), F32)] * 2,
        compiler_params=_params(("arbitrary", "arbitrary")),
    )(jnp.asarray(br.bias), br.view(q), br.view(k), br.view(k), br.view(v), br.view(v),
      *[br.view(o) for o, _ in earlier], *[br.view(x) for _, x in earlier])
    return tuple(a.reshape(s, WIDTH_B) for a in res)


LOCAL_CHUNKS = 4


def _local_bias():
    row = np.arange(BLOCK)
    pos = L_GROUP * (row % (L_GROUP // 2)) + row // (L_GROUP // 2)
    dist = pos[:, None] - np.concatenate([pos - BLOCK, pos])[None, :]
    band = (dist >= 0) & (dist <= BLOCK)
    start = band & (np.arange(2 * BLOCK)[None, :] >= BLOCK)
    return np.where(np.stack([band, start]), 0.0, -np.inf).astype(np.float32)


def _chunk_view(a):
    return a.reshape(a.shape[0] // L_BLOCK, L_GROUP, 2, L_GROUP // 2, a.shape[1])


def _chunk_of(ref, j, cols=slice(None)):
    x = ref[j // 2, :, j % 2, :, cols]
    return x.reshape(BLOCK, x.shape[-1])


def _put_chunk(ref, j, cols, value):
    ref[j // 2, :, j % 2, :, cols] = value.reshape(L_GROUP, L_GROUP // 2, value.shape[-1])


def _local_keys(cur_ref, before_ref, j, cols):
    here = slice(j * BLOCK, (j + 1) * BLOCK)
    before = before_ref[:, cols] if j == 0 else cur_ref[(j - 1) * BLOCK:j * BLOCK, cols]
    return jnp.concatenate([before, cur_ref[here, cols]], axis=0)


def _local_specs(s, step=lambda n: n):
    rows = LOCAL_CHUNKS * BLOCK
    cur = pl.BlockSpec((rows, WIDTH_B), lambda n: (step(n), 0))
    before = pl.BlockSpec((BLOCK, WIDTH_B), lambda n: (jnp.maximum(LOCAL_CHUNKS * step(n) - 1, 0), 0))
    return [cur, cur, before, cur, before]


def _attn_fwd_local(q1, k1, v1):
    s = q1.shape[0]
    rows = LOCAL_CHUNKS * BLOCK
    qn = BLOCK

    def body(bias_ref, q_ref, kc_ref, kp_ref, vc_ref, vp_ref, o_ref, lse_ref):
        first = jnp.where(pl.program_id(0) == 0, bias_ref[1], bias_ref[0])
        biases = [jnp.concatenate([b, b], axis=0) for b in (first, bias_ref[0])]
        lo = lax.broadcasted_iota(jnp.int32, (qn, LANES), 1) < HEAD_DIM
        mask_lo = lo.astype(F32).astype(MXU_DTYPE)
        for j, hp in itertools.product(range(LOCAL_CHUNKS), range(HEADS_B // 2)):
            cols = slice(hp * LANES, (hp + 1) * LANES)
            qp = q_ref[j * BLOCK:(j + 1) * BLOCK, cols]
            kcat = _local_keys(kc_ref, kp_ref, j, cols)
            vcat = _local_keys(vc_ref, vp_ref, j, cols)
            sc = _dot_nt(jnp.concatenate([qp * mask_lo, qp * (1 - mask_lo)], axis=0), kcat) + biases[min(j, 1)]
            m = jnp.max(sc, axis=1, keepdims=True)
            p = jnp.exp(sc - m)
            l = jnp.sum(p, axis=1, keepdims=True)
            out = _dot(p.astype(MXU_DTYPE), vcat) / l
            lse = m + jnp.log(l)
            _put_chunk(o_ref, j, cols, jnp.where(lo, out[:qn], out[qn:]))
            _put_chunk(lse_ref, j, cols, jnp.where(lo, lse[:qn], lse[qn:]))

    out_spec = pl.BlockSpec((LOCAL_CHUNKS // 2, L_GROUP, 2, L_GROUP // 2, WIDTH_B), lambda n: (n, 0, 0, 0, 0))
    res = pl.pallas_call(
        body, name="attn_fwd_d1", grid=(s // rows,),
        in_specs=[_full((2, BLOCK, 2 * BLOCK))] + _local_specs(s), out_specs=[out_spec] * 2,
        out_shape=[jax.ShapeDtypeStruct((s // L_BLOCK, L_GROUP, 2, L_GROUP // 2, WIDTH_B), F32)] * 2,
        compiler_params=_params(("arbitrary",)),
    )(jnp.asarray(_local_bias()), q1, k1, k1, v1, v1)
    return tuple(a.reshape(s, WIDTH_B) for a in res)


def _attn_bwd_local(q1, k1, v1, do, o, lse):
    s = q1.shape[0]
    rows = LOCAL_CHUNKS * BLOCK
    nsteps = s // rows
    qn = BLOCK

    def body(bias_ref, q_ref, kc_ref, kp_ref, vc_ref, vp_ref, do_ref, o_ref, lse_ref, dq_ref, dk_ref, dv_ref,
             dk_buf, dv_buf):
        n = pl.program_id(0)

        @pl.when(n == 0)
        def _():
            dk_buf[...] = jnp.zeros_like(dk_buf)
            dv_buf[...] = jnp.zeros_like(dv_buf)

        @pl.when(n < nsteps)
        def _():
            first = jnp.where(n == 0, bias_ref[1], bias_ref[0])
            biases = [jnp.concatenate([b, b], axis=0) for b in (first, bias_ref[0])]
            lo = lax.broadcasted_iota(jnp.int32, (qn, LANES), 1) < HEAD_DIM
            mask_f = lo.astype(F32)
            mask_lo = mask_f.astype(MXU_DTYPE)
            dk_buf[LOCAL_CHUNKS:] = jnp.zeros((LOCAL_CHUNKS, qn, WIDTH_B), F32)
            dv_buf[LOCAL_CHUNKS:] = jnp.zeros((LOCAL_CHUNKS, qn, WIDTH_B), F32)

            def prepare(j, hp):
                cols = slice(hp * LANES, (hp + 1) * LANES)
                qp = q_ref[j * BLOCK:(j + 1) * BLOCK, cols]
                dop = _chunk_of(do_ref, j, cols)
                prod = dop * _chunk_of(o_ref, j, cols)
                prod_lo = prod * mask_f
                lse = _chunk_of(lse_ref, j, cols)
                return dict(
                    j=j, cols=cols, kcat=_local_keys(kc_ref, kp_ref, j, cols), vcat=_local_keys(vc_ref, vp_ref, j, cols),
                    qs=jnp.concatenate([qp * mask_lo, qp * (1 - mask_lo)], axis=0),
                    dos=jnp.concatenate([dop * mask_f, dop * (1.0 - mask_f)], axis=0).astype(MXU_DTYPE),
                    delta=jnp.concatenate([jnp.sum(prod_lo, axis=1, keepdims=True),
                                           jnp.sum(prod - prod_lo, axis=1, keepdims=True)], axis=0),
                    lse2=jnp.concatenate([lse[:, :1], lse[:, HEAD_DIM:HEAD_DIM + 1]], axis=0))

            def scores(t):
                t['sc'] = _dot_nt(t['qs'], t['kcat'])
                t['dp'] = _dot_nt(t['dos'], t['vcat'])

            def softmax(t):
                p = jnp.exp(t['sc'] + biases[min(t['j'], 1)] - t['lse2'])
                t['ds'] = (p * (t['dp'] - t['delta'])).astype(MXU_DTYPE)
                t['p'] = p.astype(MXU_DTYPE)

            def gradients(t):
                t['dvc'] = _dot_tn(t['p'], t['dos'])
                t['dkc'] = _dot_tn(t['ds'], t['qs'])
                t['dq2'] = _dot(t['ds'], t['kcat'])

            def store(t):
                j, cols = t['j'], t['cols']
                _put_chunk(dq_ref, j, cols, jnp.where(lo, t['dq2'][:qn], t['dq2'][qn:]))
                for buf, both in ((dk_buf, t['dkc']), (dv_buf, t['dvc'])):
                    buf[LOCAL_CHUNKS + j - 1, :, cols] += both[:qn]
                    buf[LOCAL_CHUNKS + j, :, cols] += both[qn:]

            for j, first_pair in itertools.product(range(LOCAL_CHUNKS), range(0, HEADS_B // 2, PAIRS_ABREAST)):
                group = [prepare(j, hp) for hp in range(first_pair, first_pair + PAIRS_ABREAST)]
                for stage in (scores, softmax, gradients, store):
                    for t in group:
                        stage(t)

        for j in range(LOCAL_CHUNKS):
            _put_chunk(dk_ref, j, slice(None), dk_buf[j])
            _put_chunk(dv_ref, j, slice(None), dv_buf[j])
        dk_buf[:LOCAL_CHUNKS] = dk_buf[LOCAL_CHUNKS:]
        dv_buf[:LOCAL_CHUNKS] = dv_buf[LOCAL_CHUNKS:]

    cur = lambda n: jnp.minimum(n, nsteps - 1)
    late = lambda n: jnp.maximum(n - 1, 0)
    view_spec = lambda step: pl.BlockSpec((LOCAL_CHUNKS // 2, L_GROUP, 2, L_GROUP // 2, WIDTH_B),
                                          lambda n: (step(n), 0, 0, 0, 0))
    res = pl.pallas_call(
        body, name="attn_bwd_d1", grid=(nsteps + 1,),
        in_specs=[_full((2, BLOCK, 2 * BLOCK))] + _local_specs(s, cur) + [view_spec(cur)] * 3,
        out_specs=[view_spec(cur), view_spec(late), view_spec(late)],
        out_shape=[jax.ShapeDtypeStruct((s // L_BLOCK, L_GROUP, 2, L_GROUP // 2, WIDTH_B), F32)] * 3,
        scratch_shapes=[pltpu.VMEM((2 * LOCAL_CHUNKS, qn, WIDTH_B), F32)] * 2,
        compiler_params=_params(("arbitrary",)),
    )(jnp.asarray(_local_bias()), q1, k1, k1, v1, v1, _chunk_view(do), _chunk_view(o), _chunk_view(lse))
    return tuple(a.reshape(s, WIDTH_B) for a in res)


def _sgu_forward_tile(uv, w_ref, bias, g_sgu):
    tm = uv.shape[0]
    u = uv[:, :WIDTH_A]
    v = uv[:, WIDTH_A:]
    ug, tu = _gelu_parts(u)
    vg, tv = _gelu_parts(v)
    mu = jnp.mean(vg, axis=-1, keepdims=True)
    vc = vg - mu
    rs = lax.rsqrt(jnp.mean(vc * vc, axis=-1, keepdims=True) + EPS)
    vhat = vc * rs
    vn = (vhat * g_sgu).astype(MXU_DTYPE)
    masks = _half_masks(MXU_DTYPE)
    chunks = []
    for c in range(tm // CHUNK):
        rows = slice(c * CHUNK, (c + 1) * CHUNK)
        groups = []
        for gp in range(2):
            vn_g = vn[rows, gp * LANES:(gp + 1) * LANES]
            groups.append(_dot(w_ref[2 * gp], vn_g * masks[0]) + _dot(w_ref[2 * gp + 1], vn_g * masks[1]))
        chunks.append(jnp.concatenate(groups, axis=1) + bias)
    mixed = jnp.concatenate(chunks, axis=0)
    return dict(u=u, v=v, ug=ug, tu=tu, tv=tv, rs=rs, vhat=vhat, vn=vn, mixed=mixed, ya=ug * mixed)


def _sgu_fwd(uv, w_tril, bias, g_sgu, g_a):
    s = uv.shape[0]
    tm = 512

    def body(uv_ref, w_ref, b_ref, gs_ref, ga_ref, o_ref):
        t = _sgu_forward_tile(uv_ref[...], w_ref, b_ref[...], gs_ref[...])
        n, _ = _rms_stats(t['ya'])
        o_ref[...] = (n * ga_ref[...]).astype(MXU_DTYPE)

    return pl.pallas_call(
        body, name="sgu_fwd", grid=(s // tm,),
        in_specs=[_rows(tm, 2 * WIDTH_A), _full((HEADS_A, CHUNK, CHUNK)), _full((CHUNK, WIDTH_A)),
                  _full((1, WIDTH_A)), _full((1, WIDTH_A))],
        out_specs=_rows(tm, WIDTH_A), out_shape=jax.ShapeDtypeStruct((s, WIDTH_A), MXU_DTYPE),
        compiler_params=_params(("arbitrary",)),
    )(uv, w_tril, bias, g_sgu, g_a)


def _out_fwd(ya_n, y_b, g_b, w_out, x):
    s = x.shape[0]
    tm = 512
    nc = WIDTH_B // LANES

    def body(ya_ref, *refs):
        yb_refs = refs[:nc]
        g_ref, w_ref, x_ref, h_ref, yn_ref = refs[nc:]
        n, _ = _rms_stats(_load_l256(yb_refs, tm))
        yn = jnp.concatenate([ya_ref[...], (n * g_ref[...]).astype(MXU_DTYPE)], axis=1)
        yn_ref[...] = yn
        h_ref[...] = x_ref[...] + _dot(yn, w_ref[...])

    return pl.pallas_call(
        body, name="out_fwd", grid=(s // tm,),
        in_specs=[_rows(tm, WIDTH_A)] + _col_specs(tm, WIDTH_B) + [_full((1, WIDTH_B)), _full((D_MODEL, D_MODEL)),
                                                                 _rows(tm, D_MODEL)],
        out_specs=[_rows(tm, D_MODEL), _rows(tm, D_MODEL)],
        out_shape=[jax.ShapeDtypeStruct((s, D_MODEL), F32), jax.ShapeDtypeStruct((s, D_MODEL), MXU_DTYPE)],
        compiler_params=_params(("arbitrary",)),
    )(ya_n, *([y_b] * nc), g_b, w_out, x)


def _ffn_fwd(h1, g_ffn, w_gate_t, w_up_t, w_down):
    s = h1.shape[0]
    tm = 512

    def body(h_ref, g_ref, wgt_ref, wut_ref, wd_ref, o_ref, gate_ref, up_ref, hn_ref):
        h = h_ref[...]
        n, _ = _rms_stats(h)
        hn = (n * g_ref[...]).astype(MXU_DTYPE)
        hn_ref[...] = hn
        strips = [dict(cols=slice(lo, hi)) for lo, hi in FF_STRIPS]

        def project(t):
            t['gate'] = _dot_nt(hn, wgt_ref[t['cols'], :])
            t['up'] = _dot_nt(hn, wut_ref[t['cols'], :])

        def activate(t):
            gate, up = t['gate'], t['up']
            gate_ref[:, t['cols']] = gate.astype(MXU_DTYPE)
            up_ref[:, t['cols']] = up.astype(MXU_DTYPE)
            t['act'] = (gate * jax.nn.sigmoid(gate) * up).astype(MXU_DTYPE)

        def down(t):
            return _dot(t['act'], wd_ref[t['cols'], :])

        out = h
        project(strips[0])
        for i, t in enumerate(strips):
            if i + 1 < len(strips):
                project(strips[i + 1])
            activate(t)
            out = out + down(t)
        o_ref[...] = out

    return pl.pallas_call(
        body, name="ffn_fwd", grid=(s // tm,),
        in_specs=[_rows(tm, D_MODEL), _full((1, D_MODEL)), _full((D_FF, D_MODEL)), _full((D_FF, D_MODEL)),
                  _full((D_FF, D_MODEL))],
        out_specs=[_rows(tm, D_MODEL), _rows(tm, D_FF), _rows(tm, D_FF), _rows(tm, D_MODEL)],
        out_shape=[jax.ShapeDtypeStruct((s, D_MODEL), F32), jax.ShapeDtypeStruct((s, D_FF), MXU_DTYPE),
                   jax.ShapeDtypeStruct((s, D_FF), MXU_DTYPE), jax.ShapeDtypeStruct((s, D_MODEL), MXU_DTYPE)],
        compiler_params=_params(("arbitrary",)),
    )(h1, g_ffn, w_gate_t, w_up_t, w_down)


def _ple_loss(h2, p, target, g_ple, w_pg, w_pp_t, g_final):
    s = h2.shape[0]
    tm = 512

    def body(h_ref, p_ref, t_ref, gp_ref, wg_ref, wpt_ref, gf_ref,
             loss_ref, dh_ref, dz_ref, dpp_ref, hn_ref, dgp_ref, dgf_ref):
        @pl.when(pl.program_id(0) == 0)
        def _():
            loss_ref[...] = jnp.zeros_like(loss_ref)
            dgp_ref[...] = jnp.zeros_like(dgp_ref)
            dgf_ref[...] = jnp.zeros_like(dgf_ref)

        h2t = h_ref[...]
        n2, r2 = _rms_stats(h2t)
        hn = (n2 * gp_ref[...]).astype(MXU_DTYPE)
        hn_ref[...] = hn
        gate = jax.nn.sigmoid(_dot(hn, wg_ref[...]))
        pp = _dot_nt(p_ref[...].astype(MXU_DTYPE), wpt_ref[...])
        h3 = h2t + gate * pp
        n3, r3 = _rms_stats(h3)
        diff = n3 * gf_ref[...] - t_ref[...]
        loss_ref[...] += jnp.full(loss_ref.shape, 0.5 * jnp.sum(diff * diff) / D_MODEL, F32)
        dy = diff * (1.0 / D_MODEL)
        dgf_ref[...] += jnp.sum(dy * n3, axis=0, keepdims=True)
        dh3 = _rms_bwd(dy * gf_ref[...], n3, r3)
        dpp_ref[...] = (dh3 * gate).astype(MXU_DTYPE)
        dz = (dh3 * pp * gate * (1.0 - gate)).astype(MXU_DTYPE)
        dz_ref[...] = dz
        dhn = _dot_nt(dz, wg_ref[...])
        dgp_ref[...] += jnp.sum(dhn * n2, axis=0, keepdims=True)
        dh_ref[...] = dh3 + _rms_bwd(dhn * gp_ref[...], n2, r2)

    return pl.pallas_call(
        body, name="ple_loss", grid=(s // tm,),
        in_specs=[_rows(tm, D_MODEL), _rows(tm, PLE_DIM), _rows(tm, D_MODEL), _full((1, D_MODEL)),
                  _full((D_MODEL, D_MODEL)), _full((D_MODEL, PLE_DIM)), _full((1, D_MODEL))],
        out_specs=[_full((1, LANES)), _rows(tm, D_MODEL), _rows(tm, D_MODEL), _rows(tm, D_MODEL),
                   _rows(tm, D_MODEL), _full((1, D_MODEL)), _full((1, D_MODEL))],
        out_shape=[jax.ShapeDtypeStruct((1, LANES), F32), jax.ShapeDtypeStruct((s, D_MODEL), F32),
                   jax.ShapeDtypeStruct((s, D_MODEL), MXU_DTYPE), jax.ShapeDtypeStruct((s, D_MODEL), MXU_DTYPE),
                   jax.ShapeDtypeStruct((s, D_MODEL), MXU_DTYPE), jax.ShapeDtypeStruct((1, D_MODEL), F32),
                   jax.ShapeDtypeStruct((1, D_MODEL), F32)],
        compiler_params=_params(("arbitrary",)),
    )(h2, p, target, g_ple, w_pg, w_pp_t, g_final)


def _ffn_bwd(dh2, h1, gate, up, g_ffn, w_down, w_gate_t, w_up_t):
    s = h1.shape[0]
    tm = 256

    def body(dh_ref, h_ref, gate_ref, up_ref, g_ref, wd_ref, wgt_ref, wut_ref,
             o_ref, act_ref, dg_ref, du_ref, dgn_ref):
        @pl.when(pl.program_id(0) == 0)
        def _():
            dgn_ref[...] = jnp.zeros_like(dgn_ref)

        dh = dh_ref[...]
        dhb = dh.astype(MXU_DTYPE)
        strips = [dict(cols=slice(lo, hi)) for lo, hi in FF_STRIPS]

        def back_down(t):
            t['dact'] = _dot_nt(dhb, wd_ref[t['cols'], :])

        def back_act(t):
            cols, dact = t['cols'], t['dact']
            g = gate_ref[:, cols].astype(F32)
            u = up_ref[:, cols].astype(F32)
            sg = jax.nn.sigmoid(g)
            silu = g * sg
            act_ref[:, cols] = (silu * u).astype(MXU_DTYPE)
            t['du'] = (dact * silu).astype(MXU_DTYPE)
            t['dg'] = (dact * u * sg * (1.0 + g * (1.0 - sg))).astype(MXU_DTYPE)
            du_ref[:, cols] = t['du']
            dg_ref[:, cols] = t['dg']

        def back_in(t):
            return _dot(t['dg'], wgt_ref[t['cols'], :]) + _dot(t['du'], wut_ref[t['cols'], :])

        dhn = jnp.zeros((tm, D_MODEL), F32)
        back_down(strips[0])
        for i, t in enumerate(strips):
            if i + 1 < len(strips):
                back_down(strips[i + 1])
            back_act(t)
            dhn = dhn + back_in(t)
        n, r = _rms_stats(h_ref[...])
        dgn_ref[...] += jnp.sum(dhn * n, axis=0, keepdims=True)
        o_ref[...] = dh + _rms_bwd(dhn * g_ref[...], n, r)

    return pl.pallas_call(
        body, name="ffn_bwd", grid=(s // tm,),
        in_specs=[_rows(tm, D_MODEL), _rows(tm, D_MODEL), _rows(tm, D_FF), _rows(tm, D_FF), _full((1, D_MODEL)),
                  _full((D_FF, D_MODEL)), _full((D_FF, D_MODEL)), _full((D_FF, D_MODEL))],
        out_specs=[_rows(tm, D_MODEL), _rows(tm, D_FF), _rows(tm, D_FF), _rows(tm, D_FF), _full((1, D_MODEL))],
        out_shape=[jax.ShapeDtypeStruct((s, D_MODEL), F32), jax.ShapeDtypeStruct((s, D_FF), MXU_DTYPE),
                   jax.ShapeDtypeStruct((s, D_FF), MXU_DTYPE), jax.ShapeDtypeStruct((s, D_FF), MXU_DTYPE),
                   jax.ShapeDtypeStruct((1, D_MODEL), F32)],
        compiler_params=_params(("arbitrary",)),
    )(dh2, h1, gate, up, g_ffn, w_down, w_gate_t, w_up_t)


def _out_bwd(dh1, y_b, g_b, w_out):
    s = dh1.shape[0]
    tm = 512
    nc = WIDTH_B // LANES

    def body(dh_ref, *refs):
        yb_refs = refs[:nc]
        g_ref, w_ref, dya_ref, dyb_ref, dg_ref, scr = refs[nc:]

        @pl.when(pl.program_id(0) == 0)
        def _():
            dg_ref[...] = jnp.zeros_like(dg_ref)

        dy = _dot_nt(dh_ref[...].astype(MXU_DTYPE), w_ref[...])
        dya_ref[...] = dy[:, :WIDTH_A]
        dyb = dy[:, WIDTH_A:]
        n, r = _rms_stats(_load_l256(yb_refs, tm))
        dg_ref[...] += jnp.sum(dyb * n, axis=0, keepdims=True)
        dyb_in = _rms_bwd(dyb * g_ref[...], n, r)
        for j in range(nc):
            cols = slice(j * LANES, (j + 1) * LANES)
            _store_l256(scr, dyb_ref, cols, dyb_in[:, cols])

    return pl.pallas_call(
        body, name="out_bwd", grid=(s // tm,), scratch_shapes=[pltpu.VMEM((tm, LANES), F32)],
        in_specs=[_rows(tm, D_MODEL)] + _col_specs(tm, WIDTH_B) + [_full((1, WIDTH_B)), _full((D_MODEL, D_MODEL))],
        out_specs=[_rows(tm, WIDTH_A), _rows(tm, WIDTH_B), _full((1, WIDTH_B))],
        out_shape=[jax.ShapeDtypeStruct((s, WIDTH_A), F32), jax.ShapeDtypeStruct((s, WIDTH_B), F32),
                   jax.ShapeDtypeStruct((1, WIDTH_B), F32)],
        compiler_params=_params(("arbitrary",)),
    )(dh1, *([y_b] * nc), g_b, w_out)


def _attn_bwd_branch(q, k, v, do, o, lse, grads, dil):
    s = q.shape[0]
    br = _Branch(dil, s)
    qn, nb = br.qn, br.nb
    first = grads is None

    def body(*refs):
        bias_ref, q_ref, kc_ref, kp_ref, vc_ref, vp_ref, do_ref, o_ref, lse_ref = refs[:9]
        if first:
            rest = refs[9:]
        else:
            dq_in, dk_in, dv_in = refs[9:12]
            rest = refs[12:]
        dq_ref, dk_ref, dv_ref, dk_carry, dv_carry = rest
        n = pl.program_id(1)

        @pl.when(n == 0)
        def _():
            dk_carry[...] = jnp.zeros_like(dk_carry)
            dv_carry[...] = jnp.zeros_like(dv_carry)

        @pl.when(n < nb)
        def _():
            bias2 = jnp.concatenate([bias_ref[0], bias_ref[0]], axis=0)
            lane = lax.broadcasted_iota(jnp.int32, (qn, LANES), 1)
            lo = lane < HEAD_DIM
            mask_f = lo.astype(F32)
            mask_lo = mask_f.astype(MXU_DTYPE)
            def prepare(j, hp):
                cols = slice(hp * LANES, (hp + 1) * LANES)
                qp = br.load(q_ref, cols, j)
                dop = br.load(do_ref, cols, j)
                prod = dop * br.load(o_ref, cols, j)
                prod_lo = prod * mask_f
                lse = br.load(lse_ref, cols, j)
                return dict(
                    j=j, cols=cols,
                    kcat=jnp.concatenate([br.load(kp_ref, cols, j), br.load(kc_ref, cols, j)], axis=0),
                    vcat=jnp.concatenate([br.load(vp_ref, cols, j), br.load(vc_ref, cols, j)], axis=0),
                    qs=jnp.concatenate([qp * mask_lo, qp * (1 - mask_lo)], axis=0),
                    dos=jnp.concatenate([dop * mask_f, dop * (1.0 - mask_f)], axis=0).astype(MXU_DTYPE),
                    delta=jnp.concatenate([jnp.sum(prod_lo, axis=1, keepdims=True),
                                           jnp.sum(prod - prod_lo, axis=1, keepdims=True)], axis=0),
                    lse2=jnp.concatenate([lse[:, :1], lse[:, HEAD_DIM:HEAD_DIM + 1]], axis=0))

            def scores(t):
                t['sc'] = _dot_nt(t['qs'], t['kcat'])
                t['dp'] = _dot_nt(t['dos'], t['vcat'])

            def softmax(t):
                p = jnp.exp(t['sc'] + bias2 - t['lse2'])
                t['ds'] = (p * (t['dp'] - t['delta'])).astype(MXU_DTYPE)
                t['p'] = p.astype(MXU_DTYPE)

            def gradients(t):
                t['dvc'] = _dot_tn(t['p'], t['dos'])
                t['dkc'] = _dot_tn(t['ds'], t['qs'])
                t['dq2'] = _dot(t['ds'], t['kcat'])

            def store(t):
                j, cols, dkc, dvc = t['j'], t['cols'], t['dkc'], t['dvc']
                dq = jnp.where(lo, t['dq2'][:qn], t['dq2'][qn:])
                dk_prev = dk_carry[j, :, cols] + dkc[:qn]
                dv_prev = dv_carry[j, :, cols] + dvc[:qn]
                if not first:
                    dq = dq + br.load(dq_in, cols, j)
                    dk_prev = dk_prev + br.load(dk_in, cols, j)
                    dv_prev = dv_prev + br.load(dv_in, cols, j)
                br.store(dq_ref, cols, dq, j)
                br.store(dk_ref, cols, dk_prev, j)
                br.store(dv_ref, cols, dv_prev, j)
                dk_carry[j, :, cols] = dkc[qn:]
                dv_carry[j, :, cols] = dvc[qn:]

            for j, first_pair in itertools.product(range(br.res), range(0, HEADS_B // 2, PAIRS_ABREAST)):
                group = [prepare(j, hp) for hp in range(first_pair, first_pair + PAIRS_ABREAST)]
                for stage in (scores, softmax, gradients, store):
                    for t in group:
                        stage(t)

        @pl.when(n == nb)
        def _():
            for j in range(br.res):
                dk_last = dk_carry[j]
                dv_last = dv_carry[j]
                if not first:
                    dk_last = dk_last + br.load(dk_in, slice(None), j)
                    dv_last = dv_last + br.load(dv_in, slice(None), j)
                br.store(dk_ref, slice(None), dk_last, j)
                br.store(dv_ref, slice(None), dv_last, j)

    cur = lambda n: jnp.minimum(n, nb - 1)
    before = lambda n: jnp.maximum(cur(n) - 1, 0)
    late = lambda n: jnp.maximum(n - 1, 0)
    in_specs = [br.bias_spec(cur), br.spec(WIDTH_B, cur), br.spec(WIDTH_B, cur), br.spec(WIDTH_B, before),
                br.spec(WIDTH_B, cur), br.spec(WIDTH_B, before), br.spec(WIDTH_B, cur), br.spec(WIDTH_B, cur),
                br.spec(WIDTH_B, cur)]
    args = [jnp.asarray(br.bias)] + [br.view(a) for a in (q, k, k, v, v, do, o, lse)]
    if not first:
        in_specs += [br.spec(WIDTH_B, cur), br.spec(WIDTH_B, late), br.spec(WIDTH_B, late)]
        args += [br.view(g) for g in grads]
    res = pl.pallas_call(
        body, name="attn_bwd_d%d" % dil, grid=(br.grid[0], nb + 1), in_specs=in_specs,
        out_specs=[br.spec(WIDTH_B, cur), br.spec(WIDTH_B, late), br.spec(WIDTH_B, late)],
        out_shape=[jax.ShapeDtypeStruct((s // L_BLOCK, 4, 4, L_GROUP, WIDTH_B), F32)] * 3,
        scratch_shapes=[pltpu.VMEM((br.res, qn, WIDTH_B), F32), pltpu.VMEM((br.res, qn, WIDTH_B), F32)],
        compiler_params=_params(("arbitrary", "arbitrary")),
    )(*args)
    return tuple(a.reshape(s, WIDTH_B) for a in res)


def _sgu_bwd(uv, dya_n, w_tril, w_tril_t, bias, g_sgu, g_a):
    s = uv.shape[0]
    tm = 512

    def body(uv_ref, dy_ref, w_ref, wt_ref, b_ref, gs_ref, ga_ref, duv_ref, dw_ref, db_ref, dgs_ref, dga_ref,
             db_acc):
        i = pl.program_id(0)

        @pl.when(i == 0)
        def _():
            dw_ref[...] = jnp.zeros_like(dw_ref)
            dgs_ref[...] = jnp.zeros_like(dgs_ref)
            dga_ref[...] = jnp.zeros_like(dga_ref)
            db_acc[...] = jnp.zeros_like(db_acc)

        t = _sgu_forward_tile(uv_ref[...], w_ref, b_ref[...], gs_ref[...])
        na, ra = _rms_stats(t['ya'])
        dyn = dy_ref[...]
        dga_ref[...] += jnp.sum(dyn * na, axis=0, keepdims=True)
        dya = _rms_bwd(dyn * ga_ref[...], na, ra)
        dug = dya * t['mixed']
        dmixed = dya * t['ug']
        dmb = dmixed.astype(MXU_DTYPE)
        masks = _half_masks(MXU_DTYPE)
        chunks = []
        db = jnp.zeros((CHUNK, WIDTH_A), F32)
        for c in range(tm // CHUNK):
            rows = slice(c * CHUNK, (c + 1) * CHUNK)
            db = db + dmixed[rows]
            groups = []
            for gp in range(2):
                cols = slice(gp * LANES, (gp + 1) * LANES)
                dm_g = dmb[rows, cols]
                vn_g = t['vn'][rows, cols]
                dvn_g = jnp.zeros((CHUNK, LANES), F32)
                for j in range(2):
                    dm_h = dm_g * masks[j]
                    dvn_g = dvn_g + _dot(wt_ref[2 * gp + j], dm_h)
                    dw_ref[2 * gp + j] += _dot_nt(dm_h, vn_g)
                groups.append(dvn_g)
            chunks.append(jnp.concatenate(groups, axis=1))
        db_acc[...] += db
        dvn = jnp.concatenate(chunks, axis=0)
        vhat = t['vhat']
        dgs_ref[...] += jnp.sum(dvn * vhat, axis=0, keepdims=True)
        dvh = dvn * gs_ref[...]
        dvg = t['rs'] * (dvh - jnp.mean(dvh, axis=-1, keepdims=True)
                         - vhat * jnp.mean(dvh * vhat, axis=-1, keepdims=True))
        duv_ref[:, :WIDTH_A] = (dug * _gelu_grad(t['u'], t['tu'])).astype(MXU_DTYPE)
        duv_ref[:, WIDTH_A:] = (dvg * _gelu_grad(t['v'], t['tv'])).astype(MXU_DTYPE)

        @pl.when(i == pl.num_programs(0) - 1)
        def _():
            lane_a = lax.broadcasted_iota(jnp.int32, (CHUNK, WIDTH_A), 1)
            lane = lax.broadcasted_iota(jnp.int32, (CHUNK, LANES), 1)
            acc = db_acc[...]
            out = jnp.zeros((CHUNK, LANES), F32)
            for h in range(HEADS_A):
                col = jnp.sum(jnp.where(lane_a // HEAD_DIM == h, acc, 0.0), axis=1, keepdims=True)
                out = jnp.where(lane == h, col, out)
            db_ref[...] = out
            causal = (lax.broadcasted_iota(jnp.int32, (CHUNK, CHUNK), 0)
                      >= lax.broadcasted_iota(jnp.int32, (CHUNK, CHUNK), 1))
            for h in range(HEADS_A):
                dw_ref[h] = jnp.where(causal, dw_ref[h], 0.0)

    return pl.pallas_call(
        body, name="sgu_bwd", grid=(s // tm,),
        in_specs=[_rows(tm, 2 * WIDTH_A), _rows(tm, WIDTH_A), _full((HEADS_A, CHUNK, CHUNK)),
                  _full((HEADS_A, CHUNK, CHUNK)), _full((CHUNK, WIDTH_A)), _full((1, WIDTH_A)),
                  _full((1, WIDTH_A))],
        out_specs=[_rows(tm, 2 * WIDTH_A), _full((HEADS_A, CHUNK, CHUNK)), _full((CHUNK, LANES)),
                   _full((1, WIDTH_A)), _full((1, WIDTH_A))],
        out_shape=[jax.ShapeDtypeStruct((s, 2 * WIDTH_A), MXU_DTYPE),
                   jax.ShapeDtypeStruct((HEADS_A, CHUNK, CHUNK), F32), jax.ShapeDtypeStruct((CHUNK, LANES), F32),
                   jax.ShapeDtypeStruct((1, WIDTH_A), F32), jax.ShapeDtypeStruct((1, WIDTH_A), F32)],
        scratch_shapes=[pltpu.VMEM((CHUNK, WIDTH_A), F32)],
        compiler_params=_params(("arbitrary",)),
    )(uv, dya_n, w_tril, w_tril_t, bias, g_sgu, g_a)


def _in_bwd_proj(duv, dq, dk, dv, cos_t, sin_t):
    s = duv.shape[0]
    tm = 512
    nc = WIDTH_B // LANES

    def body(duv_ref, *refs):
        dq_refs, dk_refs, dv_refs = refs[:nc], refs[nc:2 * nc], refs[2 * nc:3 * nc]
        cos_ref, sin_ref, dp_ref = refs[3 * nc:]
        cos = cos_ref[...]
        sin = sin_ref[...]
        dp_ref[:, :2 * WIDTH_A] = duv_ref[...]
        for i in range(nc):
            lo = 2 * WIDTH_A + i * LANES
            tq = _load_l256(dq_refs[i:i + 1], tm) * (HEAD_DIM ** -0.5)
            tk = _load_l256(dk_refs[i:i + 1], tm)
            dp_ref[:, lo:lo + LANES] = (tq * cos + _rope_partner(tq * sin)).astype(MXU_DTYPE)
            dp_ref[:, lo + WIDTH_B:lo + WIDTH_B + LANES] = (tk * cos + _rope_partner(tk * sin)).astype(MXU_DTYPE)
            dp_ref[:, lo + 2 * WIDTH_B:lo + 2 * WIDTH_B + LANES] = _load_l256(dv_refs[i:i + 1], tm).astype(MXU_DTYPE)

    return pl.pallas_call(
        body, name="in_bwd_proj", grid=(s // tm,),
        in_specs=[_rows(tm, 2 * WIDTH_A)] + 3 * _col_specs(tm, WIDTH_B) + [_rows(tm, LANES), _rows(tm, LANES)],
        out_specs=_rows(tm, IN_COLS), out_shape=jax.ShapeDtypeStruct((s, IN_COLS), MXU_DTYPE),
        compiler_params=_params(("arbitrary",)),
    )(duv, *([dq] * nc), *([dk] * nc), *([dv] * nc), cos_t, sin_t)


def _in_bwd_x(dproj, w_in_t, x, g_mix, dh1):
    s = x.shape[0]
    tm = 512

    def body(dp_ref, wt_ref, x_ref, g_ref, dh_ref, gx_ref, dg_ref):
        @pl.when(pl.program_id(0) == 0)
        def _():
            dg_ref[...] = jnp.zeros_like(dg_ref)

        dhn = _dot(dp_ref[...], wt_ref[...])
        n, r = _rms_stats(x_ref[...])
        dg_ref[...] += jnp.sum(dhn * n, axis=0, keepdims=True)
        gx_ref[...] = dh_ref[...] + _rms_bwd(dhn * g_ref[...], n, r)

    return pl.pallas_call(
        body, name="in_bwd_x", grid=(s // tm,),
        in_specs=[_rows(tm, IN_COLS), _full((IN_COLS, D_MODEL)), _rows(tm, D_MODEL), _full((1, D_MODEL)),
                  _rows(tm, D_MODEL)],
        out_specs=[_rows(tm, D_MODEL), _full((1, D_MODEL))],
        out_shape=[jax.ShapeDtypeStruct((s, D_MODEL), F32), jax.ShapeDtypeStruct((1, D_MODEL), F32)],
        compiler_params=_params(("arbitrary",)),
    )(dproj, w_in_t, x, g_mix, dh1)


def _wgrad(a, b, name):
    s, m = a.shape
    n = b.shape[1]
    bm = 512 if m % 512 == 0 else FF_HALF
    ts = 1024
    nsteps = s // ts

    def body(a_ref, b_ref, o_ref, acc):
        kk = pl.program_id(1)

        @pl.when(kk == 0)
        def _():
            acc[...] = jnp.zeros_like(acc)

        acc[...] += _dot_tn(a_ref[...].astype(MXU_DTYPE), b_ref[...].astype(MXU_DTYPE))

        @pl.when(kk == nsteps - 1)
        def _():
            o_ref[...] = acc[...].astype(o_ref.dtype)

    return pl.pallas_call(
        body, name=name, grid=(m // bm, nsteps),
        in_specs=[pl.BlockSpec((ts, bm), lambda i, kk: (kk, i)), pl.BlockSpec((ts, n), lambda i, kk: (kk, 0))],
        out_specs=pl.BlockSpec((bm, n), lambda i, kk: (i, 0)), out_shape=jax.ShapeDtypeStruct((m, n), jnp.bfloat16),
        scratch_shapes=[pltpu.VMEM((bm, n), F32)],
        compiler_params=_params(("arbitrary", "arbitrary")),
    )(a, b)


def _rope_tables(s):
    half = HEAD_DIM // 2
    inv = ROPE_THETA ** (-jnp.arange(half, dtype=F32) / half)
    ang = jnp.arange(s, dtype=F32)[:, None] * jnp.tile(inv, LANES // half)[None, :]
    sign = jnp.tile(jnp.concatenate([-jnp.ones(half, F32), jnp.ones(half, F32)]), LANES // HEAD_DIM)
    return jnp.cos(ang), jnp.sin(ang) * sign[None, :]


MESH = pl.DeviceIdType.MESH
ANY = pl.BlockSpec(memory_space=pl.ANY)
SEM = pl.BlockSpec(memory_space=pltpu.SEMAPHORE)
SPLIT_COPY = pltpu.CompilerParams(has_side_effects=pltpu.SideEffectType.DATAFLOW_SIDE_EFFECTING)
SLAB_IS_TRANSPOSED = {'w_in': True, 'w_out': False, 'w_gate': True, 'w_up': True, 'w_down': False,
                      'w_ple_gate': False, 'w_ple_proj': True}


def _place():
    x, y, c = lax.axis_index("x"), lax.axis_index("y"), lax.axis_index("c")
    other_chips = [(1 - x, y), (x, 1 - y), (1 - x, 1 - y)]
    return x, y, c, other_chips


def _chip_of(chip):
    return 2 * chip[0] + chip[1]


def _half(ref, lead, hc):
    hr = ref.shape[1] // 2
    return ref.at[lead, pl.ds(hc * hr, hr), :]


def _put_own(stack, own, index):
    return lax.dynamic_update_slice(stack, own[None], (index,) + (0,) * own.ndim)


def _first_hops(slab_ref, land_ref, send_sems, recv_sems):
    x, y, c, chips = _place()
    hr = slab_ref.shape[0] // 2
    mine = slab_ref.at[pl.ds(c * hr, hr), :]
    sends = [pltpu.make_async_remote_copy(src_ref=mine, dst_ref=_half(land_ref, 2 * x + y, c), send_sem=send_sems.at[j],
                                          recv_sem=recv_sems.at[j], device_id=(*chip, c), device_id_type=MESH)
             for j, chip in enumerate(chips)]
    recvs = [pltpu.make_async_remote_copy(src_ref=mine, dst_ref=_half(land_ref, _chip_of(chip), c),
                                          send_sem=send_sems.at[j], recv_sem=recv_sems.at[j], device_id=(*chip, c),
                                          device_id_type=MESH) for j, chip in enumerate(chips)]
    return sends, recvs


def _second_hops(land_ref, send_sems, recv_sems):
    x, y, c, chips = _place()
    sibling = (x, y, 1 - c)
    sends = [pltpu.make_async_remote_copy(src_ref=_half(land_ref, _chip_of(chip), c),
                                          dst_ref=_half(land_ref, _chip_of(chip), c), send_sem=send_sems.at[j],
                                          recv_sem=recv_sems.at[j], device_id=sibling, device_id_type=MESH)
             for j, chip in enumerate(chips)]
    recvs = [pltpu.make_async_remote_copy(src_ref=_half(land_ref, _chip_of(chip), c),
                                          dst_ref=_half(land_ref, _chip_of(chip), 1 - c), send_sem=send_sems.at[j],
                                          recv_sem=recv_sems.at[j], device_id=sibling, device_id_type=MESH)
             for j, chip in enumerate(chips)]
    return sends, recvs


def _two_level_start(slab):
    def body(slab_ref, land_ref, send_sems, recv_sems, slab_thru, land_thru, token):
        for cp in _first_hops(slab_ref, land_ref, send_sems, recv_sems)[0]:
            cp.start()
        token[...] = jnp.zeros_like(token)

    land = lax.empty((N_CHIPS,) + slab.shape, slab.dtype)
    hbm = lambda a: pltpu.HBM(a.shape, a.dtype)
    res = pl.pallas_call(
        body, name="two_level_start",
        out_shape=(pltpu.SemaphoreType.DMA((3,)), pltpu.SemaphoreType.DMA((3,)), hbm(slab), hbm(land),
                   jax.ShapeDtypeStruct((8, LANES), F32)),
        in_specs=[ANY, ANY], out_specs=(SEM, SEM, ANY, ANY, pl.BlockSpec(memory_space=pltpu.VMEM)),
        input_output_aliases={0: 2, 1: 3}, compiler_params=SPLIT_COPY,
    )(pltpu.with_memory_space_constraint(slab, pltpu.HBM), pltpu.with_memory_space_constraint(land, pltpu.HBM))
    return res[:-1], res[-1]


def _two_level_forward(handle, after):
    send_sems, recv_sems, slab, land = handle

    def body(slab_ref, land_ref, send_sems, recv_sems, *refs):
        send2, recv2 = refs[len(after):len(after) + 2]
        sends, recvs = _first_hops(slab_ref, land_ref, send_sems, recv_sems)
        for cp in sends:
            cp.wait_send()
        for cp in recvs:
            cp.wait_recv()
        for cp in _second_hops(land_ref, send2, recv2)[0]:
            cp.start()

    hbm = lambda a: pltpu.HBM(a.shape, a.dtype)
    return pl.pallas_call(
        body, name="two_level_forward",
        out_shape=(pltpu.SemaphoreType.DMA((3,)), pltpu.SemaphoreType.DMA((3,)), hbm(slab), hbm(land)),
        in_specs=[ANY, ANY, SEM, SEM] + [ANY] * len(after), out_specs=(SEM, SEM, ANY, ANY),
        input_output_aliases={0: 2, 1: 3}, compiler_params=SPLIT_COPY,
    )(slab, land, send_sems, recv_sems, *after)


def _two_level_wait(handle):
    send_sems, recv_sems, slab, land = handle

    def body(slab_ref, land_ref, send_sems, recv_sems, slab_thru, land_thru):
        sends, recvs = _second_hops(land_ref, send_sems, recv_sems)
        for cp in sends:
            cp.wait_send()
        for cp in recvs:
            cp.wait_recv()

    hbm = lambda a: pltpu.HBM(a.shape, a.dtype)
    slab, land = pl.pallas_call(
        body, name="two_level_wait", out_shape=(hbm(slab), hbm(land)), in_specs=[ANY, ANY, SEM, SEM],
        out_specs=(ANY, ANY), input_output_aliases={0: 0, 1: 1}, compiler_params=SPLIT_COPY,
    )(slab, land, send_sems, recv_sems)
    me = 2 * lax.axis_index("x") + lax.axis_index("y")
    return _put_own(land, slab, me).reshape(N_CHIPS * slab.shape[0], slab.shape[1])


def _gather_copies(slab_refs, land_refs, send_sems, recv_sems):
    x, y, c, chips = _place()
    sends, recvs = [], []
    for k, (src, land) in enumerate(zip(slab_refs, land_refs)):
        hr = src.shape[0] // 2
        for j, chip in enumerate(chips):
            for t in range(2):
                sends.append(pltpu.make_async_remote_copy(
                    src_ref=src.at[pl.ds(c * hr, hr), :], dst_ref=_half(land, 2 * x + y, c),
                    send_sem=send_sems.at[6 * k + 2 * j + t], recv_sem=recv_sems.at[6 * k + 2 * j + c],
                    device_id=(*chip, t), device_id_type=MESH))
                recvs.append(pltpu.make_async_remote_copy(
                    src_ref=src.at[pl.ds(t * hr, hr), :], dst_ref=_half(land, _chip_of(chip), t),
                    send_sem=send_sems.at[6 * k + 2 * j + t], recv_sem=recv_sems.at[6 * k + 2 * j + t],
                    device_id=(*chip, t), device_id_type=MESH))
    return sends, recvs


def _all_gather_start(slabs, after):
    n = len(slabs)

    def body(*refs):
        slab_refs, land_refs = refs[:n], refs[n:2 * n]
        send_sems, recv_sems = refs[2 * n + 1:2 * n + 3]
        token = refs[-1]
        sends, _ = _gather_copies(slab_refs, land_refs, send_sems, recv_sems)
        for cp in sends:
            cp.start()
        token[...] = jnp.zeros_like(token)

    lands = [lax.empty((N_CHIPS,) + s.shape, s.dtype) for s in slabs]
    hbm = lambda a: pltpu.HBM(a.shape, a.dtype)
    res = pl.pallas_call(
        body, name="all_gather_start",
        out_shape=(pltpu.SemaphoreType.DMA((6 * n,)), pltpu.SemaphoreType.DMA((6 * n,)), *map(hbm, slabs),
                   *map(hbm, lands), jax.ShapeDtypeStruct((8, LANES), F32)),
        in_specs=[ANY] * (2 * n + 1),
        out_specs=(SEM, SEM, *([ANY] * (2 * n)), pl.BlockSpec(memory_space=pltpu.VMEM)),
        input_output_aliases={i: 2 + i for i in range(2 * n)}, compiler_params=SPLIT_COPY,
    )(*[pltpu.with_memory_space_constraint(a, pltpu.HBM) for a in list(slabs) + lands], after)
    return res[:-1], res[-1]


def _all_gather_wait(handle, after):
    send_sems, recv_sems = handle[:2]
    n = (len(handle) - 2) // 2
    slabs, lands = handle[2:2 + n], handle[2 + n:]

    def body(*refs):
        slab_refs, land_refs = refs[:n], refs[n:2 * n]
        send_sems, recv_sems = refs[2 * n:2 * n + 2]
        sends, recvs = _gather_copies(slab_refs, land_refs, send_sems, recv_sems)
        for cp in sends:
            cp.wait_send()
        for cp in recvs:
            cp.wait_recv()

    hbm = lambda a: pltpu.HBM(a.shape, a.dtype)
    res = pl.pallas_call(
        body, name="all_gather_wait", out_shape=tuple(map(hbm, list(slabs) + list(lands))),
        in_specs=[ANY] * (2 * n) + [SEM, SEM, ANY], out_specs=tuple([ANY] * (2 * n)),
        input_output_aliases={i: i for i in range(2 * n)}, compiler_params=SPLIT_COPY,
    )(*slabs, *lands, send_sems, recv_sems, after)
    me = 2 * lax.axis_index("x") + lax.axis_index("y")
    return [_put_own(land, slab, me).reshape(N_CHIPS * slab.shape[0], slab.shape[1])
            for slab, land in zip(res[:n], res[n:])]


def _scatter_copies(part_refs, land_refs, send_sems, recv_sems):
    x, y, c, chips = _place()
    me = 4 * x + 2 * y + c
    sends, recvs = [], []
    for k, (part, land) in enumerate(zip(part_refs, land_refs)):
        for j, chip in enumerate(chips):
            for t in range(2):
                sends.append(pltpu.make_async_remote_copy(
                    src_ref=part.at[_chip_of(chip)], dst_ref=land.at[me],
                    send_sem=send_sems.at[7 * k + 2 * j + t], recv_sem=recv_sems.at[7 * k + 2 * j + c],
                    device_id=(*chip, t), device_id_type=MESH))
                recvs.append(pltpu.make_async_remote_copy(
                    src_ref=part.at[_chip_of(chip)], dst_ref=land.at[2 * _chip_of(chip) + t],
                    send_sem=send_sems.at[7 * k + 2 * j + t], recv_sem=recv_sems.at[7 * k + 2 * j + t],
                    device_id=(*chip, t), device_id_type=MESH))
        sends.append(pltpu.make_async_remote_copy(
            src_ref=part.at[2 * x + y], dst_ref=land.at[me], send_sem=send_sems.at[7 * k + 6],
            recv_sem=recv_sems.at[7 * k + 6], device_id=(x, y, 1 - c), device_id_type=MESH))
        recvs.append(pltpu.make_async_remote_copy(
            src_ref=part.at[2 * x + y], dst_ref=land.at[4 * x + 2 * y + 1 - c],
            send_sem=send_sems.at[7 * k + 6], recv_sem=recv_sems.at[7 * k + 6], device_id=(x, y, 1 - c),
            device_id_type=MESH))
    return sends, recvs


def _reduce_scatter_start(parts, name):
    n = len(parts)
    parts = [p.reshape(N_CHIPS, p.shape[0] // N_CHIPS, p.shape[1]) for p in parts]

    def body(*refs):
        part_refs, land_refs = refs[:n], refs[n:2 * n]
        send_sems, recv_sems = refs[2 * n:2 * n + 2]
        token = refs[-1]
        sends, _ = _scatter_copies(part_refs, land_refs, send_sems, recv_sems)
        for cp in sends:
            cp.start()
        token[...] = jnp.zeros_like(token)

    lands = [lax.empty((N_DEV, p.shape[1], p.shape[2]), p.dtype) for p in parts]
    hbm = lambda a: pltpu.HBM(a.shape, a.dtype)
    res = pl.pallas_call(
        body, name=name,
        out_shape=(pltpu.SemaphoreType.DMA((7 * n,)), pltpu.SemaphoreType.DMA((7 * n,)), *map(hbm, parts),
                   *map(hbm, lands), jax.ShapeDtypeStruct((8, LANES), F32)),
        in_specs=[ANY] * (2 * n), out_specs=(SEM, SEM, *([ANY] * (2 * n)), pl.BlockSpec(memory_space=pltpu.VMEM)),
        input_output_aliases={i: 2 + i for i in range(2 * n)}, compiler_params=SPLIT_COPY,
    )(*[pltpu.with_memory_space_constraint(a, pltpu.HBM) for a in parts + lands])
    return res[:-1], res[-1]


def _reduce_scatter_wait(handle, after, name):
    after = list(after) if isinstance(after, (list, tuple)) else [after]
    send_sems, recv_sems = handle[:2]
    n = (len(handle) - 2) // 2
    parts, lands = handle[2:2 + n], handle[2 + n:]

    def body(*refs):
        part_refs, land_refs = refs[:n], refs[n:2 * n]
        send_sems, recv_sems = refs[2 * n:2 * n + 2]
        sends, recvs = _scatter_copies(part_refs, land_refs, send_sems, recv_sems)
        for cp in sends:
            cp.wait_send()
        for cp in recvs:
            cp.wait_recv()

    hbm = lambda a: pltpu.HBM(a.shape, a.dtype)
    res = pl.pallas_call(
        body, name=name, out_shape=tuple(map(hbm, list(parts) + list(lands))),
        in_specs=[ANY] * (2 * n) + [SEM, SEM] + [ANY] * len(after), out_specs=tuple([ANY] * (2 * n)),
        input_output_aliases={i: i for i in range(2 * n)}, compiler_params=SPLIT_COPY,
    )(*parts, *lands, send_sems, recv_sems, *after)
    return list(zip(res[:n], res[n:]))


def _adamw_of_shares(w, own, land, m, v, name):
    rows, cols = w.shape
    tm = rows // 4 if rows % 32 == 0 else rows
    x, y, c = lax.axis_index("x"), lax.axis_index("y"), lax.axis_index("c")
    where = jnp.stack([2 * x + y, 4 * x + 2 * y + c]).astype(jnp.int32)

    def body(where_ref, w_ref, own_ref, land_ref, m_ref, v_ref, g_ref, d_ref, nm_ref, nv_ref):
        me = where_ref[1]
        g_ = jnp.zeros((tm, cols), F32)
        for dev in range(N_DEV):
            g_ = g_ + jnp.where(me == dev, own_ref[0], land_ref[dev]).astype(F32)
        m_ = ADAM_B1 * m_ref[...] + (1.0 - ADAM_B1) * g_
        v_ = ADAM_B2 * v_ref[...] + (1.0 - ADAM_B2) * (g_ * g_)
        m_hat = m_ / (1.0 - ADAM_B1 ** ADAM_STEP)
        v_hat = v_ / (1.0 - ADAM_B2 ** ADAM_STEP)
        g_ref[...] = g_
        d_ref[...] = -ADAM_LR * (m_hat / (jnp.sqrt(v_hat) + ADAM_EPS) + ADAM_WD * w_ref[...])
        nm_ref[...] = m_
        nv_ref[...] = v_

    tile = pl.BlockSpec((tm, cols), lambda i, where_ref: (i, 0))
    spec = pltpu.PrefetchScalarGridSpec(
        num_scalar_prefetch=1, grid=(rows // tm,),
        in_specs=[tile, pl.BlockSpec((1, tm, cols), lambda i, where_ref: (where_ref[0], i, 0)),
                  pl.BlockSpec((N_DEV, tm, cols), lambda i, where_ref: (0, i, 0)), tile, tile],
        out_specs=[tile] * 4)
    return pl.pallas_call(
        body, name=name, grid_spec=spec, out_shape=[jax.ShapeDtypeStruct(w.shape, F32)] * 4,
        compiler_params=_params(("arbitrary",)),
    )(where, w, own, land, m, v)


def _pack_small(values):
    flat = jnp.concatenate([values[n].reshape(-1).astype(F32) for n in SMALL])
    return jnp.pad(flat, (0, SMALL_ROWS * D_MODEL - flat.shape[0])).reshape(SMALL_ROWS, D_MODEL)


def _unpack_small(block, shapes):
    flat = block.reshape(-1)
    out, lo = {}, 0
    for n in SMALL:
        out[n] = flat[lo:lo + SMALL_SIZES[n]].reshape(shapes[n])
        lo += SMALL_SIZES[n]
    return out


def _after(token, a):
    return a + token[:1, :1].astype(a.dtype)


def kernel(x, p, mix_norm_g, w_in, sgu_w, sgu_b, sgu_norm_g, out_norm_a, out_norm_b, w_out, ffn_norm_g, w_gate, w_up, w_down, ple_norm_g, w_ple_gate, w_ple_proj, final_norm_g, loss_target, m_mix_norm_g, m_w_in, m_sgu_w, m_sgu_b, m_sgu_norm_g, m_out_norm_a, m_out_norm_b, m_w_out, m_ffn_norm_g, m_w_gate, m_w_up, m_w_down, m_ple_norm_g, m_w_ple_gate, m_w_ple_proj, m_final_norm_g, v_mix_norm_g, v_w_in, v_sgu_w, v_sgu_b, v_sgu_norm_g, v_out_norm_a, v_out_norm_b, v_w_out, v_ffn_norm_g, v_w_gate, v_w_up, v_w_down, v_ple_norm_g, v_w_ple_gate, v_w_ple_proj, v_final_norm_g):
    given = dict(locals())
    drop_lead = lambda a, lead: a.reshape(a.shape[lead:])
    xs, ps, target = drop_lead(x, 1), drop_lead(p, 2), drop_lead(loss_target, 1)
    s = xs.shape[0]
    shard = lambda name: drop_lead(given[name], 1)

    def slab_of(name):
        local = shard(name).astype(MXU_DTYPE)
        return local.T if SLAB_IS_TRANSPOSED[name] else local

    w_in_gather, token = _two_level_start(slab_of('w_in'))
    later = ['w_out', 'w_gate', 'w_up', 'w_down', 'w_ple_gate', 'w_ple_proj']
    later_slabs = [slab_of(n) for n in later]
    cos_t, sin_t = (_after(token, table) for table in _rope_tables(s))
    tril = jnp.tril(jnp.ones((CHUNK, CHUNK), F32))
    w_tril = (sgu_w.reshape(HEADS_A, CHUNK, CHUNK) * tril).astype(MXU_DTYPE)
    w_tril_t = jnp.swapaxes(w_tril, 1, 2)
    w_in_gather = _two_level_forward(w_in_gather, [cos_t, sin_t, w_tril, w_tril_t] + later_slabs)
    w_in_t = _two_level_wait(w_in_gather)
    gather, token = _all_gather_start(later_slabs, w_in_t)

    bias = jnp.repeat(sgu_b.reshape(HEADS_A, CHUNK).T, HEAD_DIM, axis=1)
    g = {n: given[n].reshape(1, -1) for n in SMALL if n not in ('sgu_w', 'sgu_b')}

    uv, q, k, v, q1, k1, v1, hn1 = _in_fwd(xs, _after(token, g['mix_norm_g']), w_in_t, cos_t, sin_t)
    ya_n = _sgu_fwd(uv, w_tril, bias, g['sgu_norm_g'], g['out_norm_a'])
    branches = [_attn_fwd_local(q1, k1, v1)] + [_attn_fwd_branch(q, k, v, dil) for dil in DILATIONS[:-1]]
    y_b, lse = _attn_fwd_branch(q, k, v, DILATIONS[-1], earlier=branches)
    stacks = dict(zip(later, _all_gather_wait(gather, lse)))
    w_gate_t, w_up_t, w_pp_t = stacks['w_gate'], stacks['w_up'], stacks['w_ple_proj']
    h1, y_n = _out_fwd(ya_n, y_b, g['out_norm_b'], stacks['w_out'], xs)
    h2, gate, up, hn2 = _ffn_fwd(h1, g['ffn_norm_g'], w_gate_t, w_up_t, stacks['w_down'])
    loss, dh2, dz, dpp, hn3, d_ple_g, d_final_g = _ple_loss(
        h2, ps, target, g['ple_norm_g'], stacks['w_ple_gate'], w_pp_t, g['final_norm_g'])

    share = {}
    share['w_ple_gate'] = _wgrad(hn3, dz, "wgrad_ple_gate")
    share['w_ple_proj'] = _wgrad(dpp, ps, "wgrad_ple_proj")
    scatter_1, token = _reduce_scatter_start([share['w_ple_gate'], share['w_ple_proj']], "reduce_scatter_start_1")
    dh1, act, dgate, dup, d_ffn_g = _ffn_bwd(dh2, h1, gate, up, _after(token, g['ffn_norm_g']), stacks['w_down'],
                                             w_gate_t, w_up_t)
    share['w_down'] = _wgrad(act, dh2, "wgrad_down")
    share['w_gate'] = _wgrad(dgate, hn2, "wgrad_gate")
    share['w_up'] = _wgrad(dup, hn2, "wgrad_up")
    scatter_2, token = _reduce_scatter_start([share['w_down'], share['w_gate'], share['w_up']],
                                             "reduce_scatter_start_2")
    dya_n, dyb, d_out_b = _out_bwd(dh1, y_b, _after(token, g['out_norm_b']), stacks['w_out'])
    share['w_out'] = _wgrad(y_n, dh1, "wgrad_out")
    scatter_3, token = _reduce_scatter_start([share['w_out']], "reduce_scatter_start_3")
    grads = _attn_bwd_local(q1, k1, v1, dyb, y_b, lse)
    for dil in DILATIONS:
        grads = _attn_bwd_branch(q, k, v, dyb, y_b, lse, grads, dil)
    duv, d_sgu_w, d_sgu_b, d_sgu_g, d_out_a = _sgu_bwd(uv, dya_n, w_tril, w_tril_t, bias,
                                                       _after(token, g['sgu_norm_g']), g['out_norm_a'])
    dproj = _in_bwd_proj(duv, grads[0], grads[1], grads[2], cos_t, sin_t)
    share['w_in'] = _wgrad(dproj, hn1, "wgrad_in")
    scatter_4, token = _reduce_scatter_start([share['w_in']], "reduce_scatter_start_4")

    grads, deltas, new_m, new_v = {}, {}, {}, {}
    add_lead = lambda a: a.reshape((1,) + a.shape)

    done = {}

    def finish(names, handles, after, tag):
        landed = []
        for i, handle in enumerate(handles):
            landed += _reduce_scatter_wait(handle, after, "reduce_scatter_wait_%s%d" % (tag, i))
        for n, (own, land) in zip(names, landed):
            turn = (lambda a: a.T) if SLAB_IS_TRANSPOSED[n] else (lambda a: a)
            res = _adamw_of_shares(turn(shard(n)), own, land, turn(shard("m_" + n)), turn(shard("v_" + n)),
                                   "adamw_" + n)
            grads[n], deltas[n], new_m[n], new_v[n] = (add_lead(turn(a)) for a in res)
            done[n] = res[0]

    grad_x, d_mix_g = _in_bwd_x(dproj, w_in_t, xs, _after(token, g['mix_norm_g']), dh1)

    gs = {'mix_norm_g': d_mix_g, 'sgu_w': d_sgu_w, 'sgu_b': d_sgu_b[:, :HEADS_A].T, 'sgu_norm_g': d_sgu_g,
          'out_norm_a': d_out_a, 'out_norm_b': d_out_b, 'ffn_norm_g': d_ffn_g, 'ple_norm_g': d_ple_g,
          'final_norm_g': d_final_g}
    gs_block = _pack_small(gs).at[SMALL_ROWS - 1, 0].set(loss[0, 0])
    to_all = jnp.broadcast_to(gs_block[None], (N_CHIPS,) + gs_block.shape).reshape(-1, D_MODEL)
    scatter_small, token = _reduce_scatter_start([to_all], "small_all_reduce_start")

    finish(['w_ple_gate', 'w_ple_proj', 'w_down', 'w_gate', 'w_up', 'w_out'], [scatter_1, scatter_2, scatter_3], token,
           "early")
    finish(['w_in'], [scatter_4], [done[n] for n in ('w_down', 'w_gate', 'w_up', 'w_out')], "last")
    (own, land), = _reduce_scatter_wait(scatter_small, done['w_in'], "small_all_reduce_wait")
    small = {n: given[n] for n in SMALL}
    small_res = _adamw_of_shares(_pack_small(small), own, land, _pack_small({n: given["m_" + n] for n in SMALL}),
                                 _pack_small({n: given["v_" + n] for n in SMALL}), "adamw_small")
    loss_out = small_res[0][SMALL_ROWS - 1, 0]
    small_shapes = {n: given[n].shape for n in SMALL}
    for res, blk in zip((grads, deltas, new_m, new_v), small_res):
        res.update(_unpack_small(blk, small_shapes))

    outs = [loss_out, add_lead(grad_x)]
    for res in (grads, deltas, new_m, new_v):
        outs += [res[n] for n in WEIGHT_NAMES]
    return tuple(outs)
```

```python
import functools
import itertools
import math

import jax
import jax.numpy as jnp
import numpy as np
from jax import lax
from jax.experimental import pallas as pl
from jax.experimental.pallas import tpu as pltpu

F32 = jnp.float32
MXU_DTYPE = jnp.bfloat16

D_MODEL = 1024
HEAD_DIM = 64
HEADS_A = 4
HEADS_B = 12
WIDTH_A = HEADS_A * HEAD_DIM
WIDTH_B = HEADS_B * HEAD_DIM
CHUNK = 128
BLOCK = 128
DILATIONS = (4, 16)
ROPE_THETA = 10000.0
D_FF = 2816
FF_HALF = D_FF // 2
FF_STRIPS = ((0, 1024), (1024, 2048), (2048, D_FF))
PLE_DIM = 256
IN_COLS = 2 * WIDTH_A + 3 * WIDTH_B
EPS = 1e-6
LANES = 128
N_CHIPS = 4
N_DEV = 8

ADAM_LR = 0.001
ADAM_B1 = 0.9
ADAM_B2 = 0.999
ADAM_EPS = 1e-08
ADAM_WD = 0.01
ADAM_STEP = 10

VMEM_LIMIT = 56 * 1024 * 1024

WEIGHT_NAMES = ['mix_norm_g', 'w_in', 'sgu_w', 'sgu_b', 'sgu_norm_g', 'out_norm_a', 'out_norm_b', 'w_out',
                'ffn_norm_g', 'w_gate', 'w_up', 'w_down', 'ple_norm_g', 'w_ple_gate', 'w_ple_proj', 'final_norm_g']
SMALL = ['mix_norm_g', 'sgu_w', 'sgu_b', 'sgu_norm_g', 'out_norm_a', 'out_norm_b', 'ffn_norm_g', 'ple_norm_g',
         'final_norm_g']
SMALL_SIZES = {'mix_norm_g': 1024, 'sgu_w': 65536, 'sgu_b': 512, 'sgu_norm_g': 256, 'out_norm_a': 256,
               'out_norm_b': 768, 'ffn_norm_g': 1024, 'ple_norm_g': 1024, 'final_norm_g': 1024}
SMALL_ROWS = 72


def _params(semantics=None):
    return pltpu.CompilerParams(dimension_semantics=semantics, vmem_limit_bytes=VMEM_LIMIT)


def _full(shape):
    nd = len(shape)
    return pl.BlockSpec(shape, lambda i: (0,) * nd, pipeline_mode=pl.Buffered(1))


def _rows(tm, width):
    return pl.BlockSpec((tm, width), lambda i: (i, 0))


def _rms_stats(x):
    r = lax.rsqrt(jnp.mean(x * x, axis=-1, keepdims=True) + EPS)
    return x * r, r


def _rms_bwd(dn, n, r):
    return r * (dn - n * jnp.mean(dn * n, axis=-1, keepdims=True))


def _dot(a, b):
    return jnp.dot(a, b, preferred_element_type=F32)


def _dot_nt(a, b):
    return lax.dot_general(a, b, (((1,), (1,)), ((), ())), preferred_element_type=F32)


def _dot_tn(a, b):
    return lax.dot_general(a, b, (((0,), (0,)), ((), ())), preferred_element_type=F32)


def _gelu_parts(x):
    c = math.sqrt(2.0 / math.pi)
    t = jnp.tanh(c * (x + 0.044715 * x * x * x))
    return 0.5 * x * (1.0 + t), t


def _gelu_grad(x, t):
    c = math.sqrt(2.0 / math.pi)
    return 0.5 * (1.0 + t) + 0.5 * x * (1.0 - t * t) * c * (1.0 + 3.0 * 0.044715 * x * x)


def _half_masks(dtype):
    lane = lax.broadcasted_iota(jnp.int32, (BLOCK, LANES), 1)
    lo = (lane < HEAD_DIM).astype(F32)
    return lo.astype(dtype), (1.0 - lo).astype(dtype)


def _rope_partner(t):
    lane = lax.broadcasted_iota(jnp.int32, t.shape, 1)
    first_half = (lane % HEAD_DIM) < (HEAD_DIM // 2)
    return jnp.where(first_half, pltpu.roll(t, LANES - HEAD_DIM // 2, 1), pltpu.roll(t, HEAD_DIM // 2, 1))


PAIRS_ABREAST = 2
RESIDUES_PER_STEP = 4
L_BLOCK = 256
L_GROUP = 16


def _store_l256(scr, out_ref, cols, value, chunk_ref=None):
    tm = value.shape[0]
    half = L_GROUP // 2
    scr[...] = value
    for blk in range(tm // L_BLOCK):
        pieces = [scr[pl.ds(blk * L_BLOCK + r, L_GROUP, stride=L_GROUP), :] for r in range(L_GROUP)]
        for r, piece in enumerate(pieces):
            lo = blk * L_BLOCK + r * L_GROUP
            out_ref[lo:lo + L_GROUP, cols] = piece.astype(out_ref.dtype)
        if chunk_ref is not None:
            for chunk in range(L_BLOCK // BLOCK):
                for r in range(0, L_GROUP, 2):
                    lo = blk * L_BLOCK + chunk * BLOCK + r * half
                    both = [p[chunk * half:(chunk + 1) * half] for p in pieces[r:r + 2]]
                    chunk_ref[lo:lo + L_GROUP, cols] = jnp.concatenate(both, axis=0).astype(chunk_ref.dtype)


def _load_l256(col_refs, tm):
    cols = []
    for ref in col_refs:
        pieces = [ref[pl.ds(blk * L_BLOCK + i, L_GROUP, stride=L_GROUP), :]
                  for blk in range(tm // L_BLOCK) for i in range(L_GROUP)]
        cols.append(jnp.concatenate(pieces, axis=0))
    return jnp.concatenate(cols, axis=1)


def _col_specs(tm, width):
    return [pl.BlockSpec((tm, LANES), lambda i, j=j: (i, j)) for j in range(width // LANES)]


def _in_fwd(x, g_mix, w_in_t, cos_t, sin_t):
    s = x.shape[0]
    tm = 512

    def body(x_ref, g_ref, wt_ref, cos_ref, sin_ref, uv_ref, q_ref, k_ref, v_ref, q1_ref, k1_ref, v1_ref, hn_ref,
             *scrs):
        n, _ = _rms_stats(x_ref[...])
        hn = (n * g_ref[...]).astype(MXU_DTYPE)
        hn_ref[...] = hn
        cos = cos_ref[...]
        sin = sin_ref[...]
        strip = 2 * LANES
        for j in range(IN_COLS // strip):
            proj = _dot_nt(hn, wt_ref[j * strip:(j + 1) * strip, :])
            lo = j * strip - 2 * WIDTH_A
            if lo < 0:
                uv_ref[:, j * strip:(j + 1) * strip] = proj
                continue
            which, lo = divmod(lo, WIDTH_B)
            for i in range(strip // LANES):
                t = proj[:, i * LANES:(i + 1) * LANES]
                cols = slice(lo + i * LANES, lo + (i + 1) * LANES)
                scr = scrs[i]
                if which == 0:
                    _store_l256(scr, q_ref, cols, (t * cos + _rope_partner(t) * sin) * (HEAD_DIM ** -0.5), q1_ref)
                elif which == 1:
                    _store_l256(scr, k_ref, cols, t * cos + _rope_partner(t) * sin, k1_ref)
                else:
                    _store_l256(scr, v_ref, cols, t, v1_ref)

    return pl.pallas_call(
        body, name="in_fwd", grid=(s // tm,), scratch_shapes=[pltpu.VMEM((tm, LANES), F32)] * 2,
        in_specs=[_rows(tm, D_MODEL), _full((1, D_MODEL)), _full((IN_COLS, D_MODEL)), _rows(tm, LANES),
                  _rows(tm, LANES)],
        out_specs=[_rows(tm, 2 * WIDTH_A)] + [_rows(tm, WIDTH_B)] * 6 + [_rows(tm, D_MODEL)],
        out_shape=[jax.ShapeDtypeStruct((s, 2 * WIDTH_A), F32)] + [jax.ShapeDtypeStruct((s, WIDTH_B), MXU_DTYPE)] * 6
        + [jax.ShapeDtypeStruct((s, D_MODEL), MXU_DTYPE)],
        compiler_params=_params(("arbitrary",)),
    )(x, g_mix, w_in_t, cos_t, sin_t)


class _Branch:
    def __init__(self, dil, s):
        i = np.arange(L_GROUP)
        self.res = RESIDUES_PER_STEP
        if dil == 16:
            nblk = BLOCK // 16
            self.grid = (16 // self.res, s // (L_BLOCK * nblk))
            self.shape = (nblk, 1, self.res, L_GROUP)
            self.index = lambda r, n: (n, r // (4 // self.res), r % (4 // self.res), 0, 0)
            pos = (np.arange(nblk)[:, None] * 16 + i[None, :]).reshape(-1)
        else:
            nblk = BLOCK // 64
            self.grid = (4 // self.res, s // (L_BLOCK * nblk))
            self.shape = (nblk, 4, self.res, L_GROUP)
            self.index = lambda r, n: (n, 0, r, 0, 0)
            pos = (np.arange(nblk)[:, None, None] * 64 + np.arange(4)[None, :, None]
                   + 4 * i[None, None, :]).reshape(-1)
        self.qn = pos.shape[0]
        self.nb = self.grid[1]
        dist = pos[:, None] - np.concatenate([pos - self.qn, pos])[None, :]
        band = (dist >= 0) & (dist <= BLOCK)
        start = band & (np.arange(2 * self.qn)[None, :] >= self.qn)
        self.bias = np.where(np.stack([band, start]), 0.0, -np.inf).astype(np.float32)

    def view(self, a):
        return a.reshape(a.shape[0] // L_BLOCK, 4, 4, L_GROUP, a.shape[1])

    def spec(self, w, step=lambda n: n):
        return pl.BlockSpec(self.shape + (w,), lambda r, n: self.index(r, step(n)))

    def bias_spec(self, step=lambda n: n):
        return pl.BlockSpec((1, self.qn, 2 * self.qn), lambda r, n: (jnp.where(step(n) == 0, 1, 0), 0, 0))

    def load(self, ref, cols=slice(None), j=0):
        x = ref[:, :, j, :, cols]
        return x.reshape(self.qn, x.shape[-1])

    def store(self, ref, cols, value, j=0):
        ref[:, :, j, :, cols] = value.reshape(self.shape[:2] + (L_GROUP, value.shape[-1]))


def _attn_fwd_branch(q, k, v, dil, earlier=()):
    s = q.shape[0]
    br = _Branch(dil, s)
    qn = br.qn
    nearly = len(earlier)

    def body(bias_ref, q_ref, kc_ref, kp_ref, vc_ref, vp_ref, *refs):
        early_refs, (o_ref, lse_ref) = refs[:2 * nearly], refs[2 * nearly:]
        bias2 = jnp.concatenate([bias_ref[0], bias_ref[0]], axis=0)
        lo = lax.broadcasted_iota(jnp.int32, (qn, LANES), 1) < HEAD_DIM
        mask_lo = lo.astype(F32).astype(MXU_DTYPE)
        for j, hp in itertools.product(range(br.res), range(HEADS_B // 2)):
            cols = slice(hp * LANES, (hp + 1) * LANES)
            qp = br.load(q_ref, cols, j)
            kcat = jnp.concatenate([br.load(kp_ref, cols, j), br.load(kc_ref, cols, j)], axis=0)
            vcat = jnp.concatenate([br.load(vp_ref, cols, j), br.load(vc_ref, cols, j)], axis=0)
            sc = _dot_nt(jnp.concatenate([qp * mask_lo, qp * (1 - mask_lo)], axis=0), kcat) + bias2
            m = jnp.max(sc, axis=1, keepdims=True)
            p = jnp.exp(sc - m)
            l = jnp.sum(p, axis=1, keepdims=True)
            out = _dot(p.astype(MXU_DTYPE), vcat) / l
            lse = m + jnp.log(l)
            outs = [br.load(r, cols, j) for r in early_refs[:nearly]] + [jnp.where(lo, out[:qn], out[qn:])]
            lses = [br.load(r, cols, j) for r in early_refs[nearly:]] + [jnp.where(lo, lse[:qn], lse[qn:])]
            if nearly:
                top = functools.reduce(jnp.maximum, lses)
                ws = [jnp.exp(x - top) for x in lses]
                den = functools.reduce(jnp.add, ws)
                outs = [functools.reduce(jnp.add, [w * o for w, o in zip(ws, outs)]) / den]
                lses = [top + jnp.log(den)]
            br.store(o_ref, cols, outs[0], j)
            br.store(lse_ref, cols, lses[0], j)

    before = lambda n: jnp.maximum(n - 1, 0)
    res = pl.pallas_call(
        body, name="attn_fwd_d%d" % dil, grid=br.grid,
        in_specs=[br.bias_spec(), br.spec(WIDTH_B), br.spec(WIDTH_B), br.spec(WIDTH_B, before), br.spec(WIDTH_B),
                  br.spec(WIDTH_B, before)] + [br.spec(WIDTH_B)] * (2 * nearly),
        out_specs=[br.spec(WIDTH_B), br.spec(WIDTH_B)],
        out_shape=[jax.ShapeDtypeStruct((s // L_BLOCK, 4, 4, L_GROUP, WIDTH_B), F32)] * 2,
        compiler_params=_params(("arbitrary", "arbitrary")),
    )(jnp.asarray(br.bias), br.view(q), br.view(k), br.view(k), br.view(v), br.view(v),
      *[br.view(o) for o, _ in earlier], *[br.view(x) for _, x in earlier])
    return tuple(a.reshape(s, WIDTH_B) for a in res)


LOCAL_CHUNKS = 4


def _local_bias():
    row = np.arange(BLOCK)
    pos = L_GROUP * (row % (L_GROUP // 2)) + row // (L_GROUP // 2)
    dist = pos[:, None] - np.concatenate([pos - BLOCK, pos])[None, :]
    band = (dist >= 0) & (dist <= BLOCK)
    start = band & (np.arange(2 * BLOCK)[None, :] >= BLOCK)
    return np.where(np.stack([band, start]), 0.0, -np.inf).astype(np.float32)


def _chunk_view(a):
    return a.reshape(a.shape[0] // L_BLOCK, L_GROUP, 2, L_GROUP // 2, a.shape[1])


def _chunk_of(ref, j, cols=slice(None)):
    x = ref[j // 2, :, j % 2, :, cols]
    return x.reshape(BLOCK, x.shape[-1])


def _put_chunk(ref, j, cols, value):
    ref[j // 2, :, j % 2, :, cols] = value.reshape(L_GROUP, L_GROUP // 2, value.shape[-1])


def _local_keys(cur_ref, before_ref, j, cols):
    here = slice(j * BLOCK, (j + 1) * BLOCK)
    before = before_ref[:, cols] if j == 0 else cur_ref[(j - 1) * BLOCK:j * BLOCK, cols]
    return jnp.concatenate([before, cur_ref[here, cols]], axis=0)


def _local_specs(s, step=lambda n: n):
    rows = LOCAL_CHUNKS * BLOCK
    cur = pl.BlockSpec((rows, WIDTH_B), lambda n: (step(n), 0))
    before = pl.BlockSpec((BLOCK, WIDTH_B), lambda n: (jnp.maximum(LOCAL_CHUNKS * step(n) - 1, 0), 0))
    return [cur, cur, before, cur, before]


def _attn_fwd_local(q1, k1, v1):
    s = q1.shape[0]
    rows = LOCAL_CHUNKS * BLOCK
    qn = BLOCK

    def body(bias_ref, q_ref, kc_ref, kp_ref, vc_ref, vp_ref, o_ref, lse_ref):
        first = jnp.where(pl.program_id(0) == 0, bias_ref[1], bias_ref[0])
        biases = [jnp.concatenate([b, b], axis=0) for b in (first, bias_ref[0])]
        lo = lax.broadcasted_iota(jnp.int32, (qn, LANES), 1) < HEAD_DIM
        mask_lo = lo.astype(F32).astype(MXU_DTYPE)
        for j, hp in itertools.product(range(LOCAL_CHUNKS), range(HEADS_B // 2)):
            cols = slice(hp * LANES, (hp + 1) * LANES)
            qp = q_ref[j * BLOCK:(j + 1) * BLOCK, cols]
            kcat = _local_keys(kc_ref, kp_ref, j, cols)
            vcat = _local_keys(vc_ref, vp_ref, j, cols)
            sc = _dot_nt(jnp.concatenate([qp * mask_lo, qp * (1 - mask_lo)], axis=0), kcat) + biases[min(j, 1)]
            m = jnp.max(sc, axis=1, keepdims=True)
            p = jnp.exp(sc - m)
            l = jnp.sum(p, axis=1, keepdims=True)
            out = _dot(p.astype(MXU_DTYPE), vcat) / l
            lse = m + jnp.log(l)
            _put_chunk(o_ref, j, cols, jnp.where(lo, out[:qn], out[qn:]))
            _put_chunk(lse_ref, j, cols, jnp.where(lo, lse[:qn], lse[qn:]))

    out_spec = pl.BlockSpec((LOCAL_CHUNKS // 2, L_GROUP, 2, L_GROUP // 2, WIDTH_B), lambda n: (n, 0, 0, 0, 0))
    res = pl.pallas_call(
        body, name="attn_fwd_d1", grid=(s // rows,),
        in_specs=[_full((2, BLOCK, 2 * BLOCK))] + _local_specs(s), out_specs=[out_spec] * 2,
        out_shape=[jax.ShapeDtypeStruct((s // L_BLOCK, L_GROUP, 2, L_GROUP // 2, WIDTH_B), F32)] * 2,
        compiler_params=_params(("arbitrary",)),
    )(jnp.asarray(_local_bias()), q1, k1, k1, v1, v1)
    return tuple(a.reshape(s, WIDTH_B) for a in res)


def _attn_bwd_local(q1, k1, v1, do, o, lse):
    s = q1.shape[0]
    rows = LOCAL_CHUNKS * BLOCK
    nsteps = s // rows
    qn = BLOCK

    def body(bias_ref, q_ref, kc_ref, kp_ref, vc_ref, vp_ref, do_ref, o_ref, lse_ref, dq_ref, dk_ref, dv_ref,
             dk_buf, dv_buf):
        n = pl.program_id(0)

        @pl.when(n == 0)
        def _():
            dk_buf[...] = jnp.zeros_like(dk_buf)
            dv_buf[...] = jnp.zeros_like(dv_buf)

        @pl.when(n < nsteps)
        def _():
            first = jnp.where(n == 0, bias_ref[1], bias_ref[0])
            biases = [jnp.concatenate([b, b], axis=0) for b in (first, bias_ref[0])]
            lo = lax.broadcasted_iota(jnp.int32, (qn, LANES), 1) < HEAD_DIM
            mask_f = lo.astype(F32)
            mask_lo = mask_f.astype(MXU_DTYPE)
            dk_buf[LOCAL_CHUNKS:] = jnp.zeros((LOCAL_CHUNKS, qn, WIDTH_B), F32)
            dv_buf[LOCAL_CHUNKS:] = jnp.zeros((LOCAL_CHUNKS, qn, WIDTH_B), F32)

            def prepare(j, hp):
                cols = slice(hp * LANES, (hp + 1) * LANES)
                qp = q_ref[j * BLOCK:(j + 1) * BLOCK, cols]
                dop = _chunk_of(do_ref, j, cols)
                prod = dop * _chunk_of(o_ref, j, cols)
                prod_lo = prod * mask_f
                lse = _chunk_of(lse_ref, j, cols)
                return dict(
                    j=j, cols=cols, kcat=_local_keys(kc_ref, kp_ref, j, cols), vcat=_local_keys(vc_ref, vp_ref, j, cols),
                    qs=jnp.concatenate([qp * mask_lo, qp * (1 - mask_lo)], axis=0),
                    dos=jnp.concatenate([dop * mask_f, dop * (1.0 - mask_f)], axis=0).astype(MXU_DTYPE),
                    delta=jnp.concatenate([jnp.sum(prod_lo, axis=1, keepdims=True),
                                           jnp.sum(prod - prod_lo, axis=1, keepdims=True)], axis=0),
                    lse2=jnp.concatenate([lse[:, :1], lse[:, HEAD_DIM:HEAD_DIM + 1]], axis=0))

            def scores(t):
                t['sc'] = _dot_nt(t['qs'], t['kcat'])
                t['dp'] = _dot_nt(t['dos'], t['vcat'])

            def softmax(t):
                p = jnp.exp(t['sc'] + biases[min(t['j'], 1)] - t['lse2'])
                t['ds'] = (p * (t['dp'] - t['delta'])).astype(MXU_DTYPE)
                t['p'] = p.astype(MXU_DTYPE)

            def gradients(t):
                t['dvc'] = _dot_tn(t['p'], t['dos'])
                t['dkc'] = _dot_tn(t['ds'], t['qs'])
                t['dq2'] = _dot(t['ds'], t['kcat'])

            def store(t):
                j, cols = t['j'], t['cols']
                _put_chunk(dq_ref, j, cols, jnp.where(lo, t['dq2'][:qn], t['dq2'][qn:]))
                for buf, both in ((dk_buf, t['dkc']), (dv_buf, t['dvc'])):
                    buf[LOCAL_CHUNKS + j - 1, :, cols] += both[:qn]
                    buf[LOCAL_CHUNKS + j, :, cols] += both[qn:]

            for j, first_pair in itertools.product(range(LOCAL_CHUNKS), range(0, HEADS_B // 2, PAIRS_ABREAST)):
                group = [prepare(j, hp) for hp in range(first_pair, first_pair + PAIRS_ABREAST)]
                for stage in (scores, softmax, gradients, store):
                    for t in group:
                        stage(t)

        for j in range(LOCAL_CHUNKS):
            _put_chunk(dk_ref, j, slice(None), dk_buf[j])
            _put_chunk(dv_ref, j, slice(None), dv_buf[j])
        dk_buf[:LOCAL_CHUNKS] = dk_buf[LOCAL_CHUNKS:]
        dv_buf[:LOCAL_CHUNKS] = dv_buf[LOCAL_CHUNKS:]

    cur = lambda n: jnp.minimum(n, nsteps - 1)
    late = lambda n: jnp.maximum(n - 1, 0)
    view_spec = lambda step: pl.BlockSpec((LOCAL_CHUNKS // 2, L_GROUP, 2, L_GROUP // 2, WIDTH_B),
                                          lambda n: (step(n), 0, 0, 0, 0))
    res = pl.pallas_call(
        body, name="attn_bwd_d1", grid=(nsteps + 1,),
        in_specs=[_full((2, BLOCK, 2 * BLOCK))] + _local_specs(s, cur) + [view_spec(cur)] * 3,
        out_specs=[view_spec(cur), view_spec(late), view_spec(late)],
        out_shape=[jax.ShapeDtypeStruct((s // L_BLOCK, L_GROUP, 2, L_GROUP // 2, WIDTH_B), F32)] * 3,
        scratch_shapes=[pltpu.VMEM((2 * LOCAL_CHUNKS, qn, WIDTH_B), F32)] * 2,
        compiler_params=_params(("arbitrary",)),
    )(jnp.asarray(_local_bias()), q1, k1, k1, v1, v1, _chunk_view(do), _chunk_view(o), _chunk_view(lse))
    return tuple(a.reshape(s, WIDTH_B) for a in res)


def _sgu_forward_tile(uv, w_ref, bias, g_sgu):
    tm = uv.shape[0]
    u = uv[:, :WIDTH_A]
    v = uv[:, WIDTH_A:]
    ug, tu = _gelu_parts(u)
    vg, tv = _gelu_parts(v)
    mu = jnp.mean(vg, axis=-1, keepdims=True)
    vc = vg - mu
    rs = lax.rsqrt(jnp.mean(vc * vc, axis=-1, keepdims=True) + EPS)
    vhat = vc * rs
    vn = (vhat * g_sgu).astype(MXU_DTYPE)
    masks = _half_masks(MXU_DTYPE)
    chunks = []
    for c in range(tm // CHUNK):
        rows = slice(c * CHUNK, (c + 1) * CHUNK)
        groups = []
        for gp in range(2):
            vn_g = vn[rows, gp * LANES:(gp + 1) * LANES]
            groups.append(_dot(w_ref[2 * gp], vn_g * masks[0]) + _dot(w_ref[2 * gp + 1], vn_g * masks[1]))
        chunks.append(jnp.concatenate(groups, axis=1) + bias)
    mixed = jnp.concatenate(chunks, axis=0)
    return dict(u=u, v=v, ug=ug, tu=tu, tv=tv, rs=rs, vhat=vhat, vn=vn, mixed=mixed, ya=ug * mixed)


def _sgu_fwd(uv, w_tril, bias, g_sgu, g_a):
    s = uv.shape[0]
    tm = 512

    def body(uv_ref, w_ref, b_ref, gs_ref, ga_ref, o_ref):
        t = _sgu_forward_tile(uv_ref[...], w_ref, b_ref[...], gs_ref[...])
        n, _ = _rms_stats(t['ya'])
        o_ref[...] = (n * ga_ref[...]).astype(MXU_DTYPE)

    return pl.pallas_call(
        body, name="sgu_fwd", grid=(s // tm,),
        in_specs=[_rows(tm, 2 * WIDTH_A), _full((HEADS_A, CHUNK, CHUNK)), _full((CHUNK, WIDTH_A)),
                  _full((1, WIDTH_A)), _full((1, WIDTH_A))],
        out_specs=_rows(tm, WIDTH_A), out_shape=jax.ShapeDtypeStruct((s, WIDTH_A), MXU_DTYPE),
        compiler_params=_params(("arbitrary",)),
    )(uv, w_tril, bias, g_sgu, g_a)


def _out_fwd(ya_n, y_b, g_b, w_out, x):
    s = x.shape[0]
    tm = 512
    nc = WIDTH_B // LANES

    def body(ya_ref, *refs):
        yb_refs = refs[:nc]
        g_ref, w_ref, x_ref, h_ref, yn_ref = refs[nc:]
        n, _ = _rms_stats(_load_l256(yb_refs, tm))
        yn = jnp.concatenate([ya_ref[...], (n * g_ref[...]).astype(MXU_DTYPE)], axis=1)
        yn_ref[...] = yn
        h_ref[...] = x_ref[...] + _dot(yn, w_ref[...])

    return pl.pallas_call(
        body, name="out_fwd", grid=(s // tm,),
        in_specs=[_rows(tm, WIDTH_A)] + _col_specs(tm, WIDTH_B) + [_full((1, WIDTH_B)), _full((D_MODEL, D_MODEL)),
                                                                 _rows(tm, D_MODEL)],
        out_specs=[_rows(tm, D_MODEL), _rows(tm, D_MODEL)],
        out_shape=[jax.ShapeDtypeStruct((s, D_MODEL), F32), jax.ShapeDtypeStruct((s, D_MODEL), MXU_DTYPE)],
        compiler_params=_params(("arbitrary",)),
    )(ya_n, *([y_b] * nc), g_b, w_out, x)


def _ffn_fwd(h1, g_ffn, w_gate_t, w_up_t, w_down):
    s = h1.shape[0]
    tm = 512

    def body(h_ref, g_ref, wgt_ref, wut_ref, wd_ref, o_ref, gate_ref, up_ref, hn_ref):
        h = h_ref[...]
        n, _ = _rms_stats(h)
        hn = (n * g_ref[...]).astype(MXU_DTYPE)
        hn_ref[...] = hn
        strips = [dict(cols=slice(lo, hi)) for lo, hi in FF_STRIPS]

        def project(t):
            t['gate'] = _dot_nt(hn, wgt_ref[t['cols'], :])
            t['up'] = _dot_nt(hn, wut_ref[t['cols'], :])

        def activate(t):
            gate, up = t['gate'], t['up']
            gate_ref[:, t['cols']] = gate.astype(MXU_DTYPE)
            up_ref[:, t['cols']] = up.astype(MXU_DTYPE)
            t['act'] = (gate * jax.nn.sigmoid(gate) * up).astype(MXU_DTYPE)

        def down(t):
            return _dot(t['act'], wd_ref[t['cols'], :])

        out = h
        project(strips[0])
        for i, t in enumerate(strips):
            if i + 1 < len(strips):
                project(strips[i + 1])
            activate(t)
            out = out + down(t)
        o_ref[...] = out

    return pl.pallas_call(
        body, name="ffn_fwd", grid=(s // tm,),
        in_specs=[_rows(tm, D_MODEL), _full((1, D_MODEL)), _full((D_FF, D_MODEL)), _full((D_FF, D_MODEL)),
                  _full((D_FF, D_MODEL))],
        out_specs=[_rows(tm, D_MODEL), _rows(tm, D_FF), _rows(tm, D_FF), _rows(tm, D_MODEL)],
        out_shape=[jax.ShapeDtypeStruct((s, D_MODEL), F32), jax.ShapeDtypeStruct((s, D_FF), MXU_DTYPE),
                   jax.ShapeDtypeStruct((s, D_FF), MXU_DTYPE), jax.ShapeDtypeStruct((s, D_MODEL), MXU_DTYPE)],
        compiler_params=_params(("arbitrary",)),
    )(h1, g_ffn, w_gate_t, w_up_t, w_down)


def _ple_loss(h2, p, target, g_ple, w_pg, w_pp_t, g_final):
    s = h2.shape[0]
    tm = 512

    def body(h_ref, p_ref, t_ref, gp_ref, wg_ref, wpt_ref, gf_ref,
             loss_ref, dh_ref, dz_ref, dpp_ref, hn_ref, dgp_ref, dgf_ref):
        @pl.when(pl.program_id(0) == 0)
        def _():
            loss_ref[...] = jnp.zeros_like(loss_ref)
            dgp_ref[...] = jnp.zeros_like(dgp_ref)
            dgf_ref[...] = jnp.zeros_like(dgf_ref)

        h2t = h_ref[...]
        n2, r2 = _rms_stats(h2t)
        hn = (n2 * gp_ref[...]).astype(MXU_DTYPE)
        hn_ref[...] = hn
        gate = jax.nn.sigmoid(_dot(hn, wg_ref[...]))
        pp = _dot_nt(p_ref[...].astype(MXU_DTYPE), wpt_ref[...])
        h3 = h2t + gate * pp
        n3, r3 = _rms_stats(h3)
        diff = n3 * gf_ref[...] - t_ref[...]
        loss_ref[...] += jnp.full(loss_ref.shape, 0.5 * jnp.sum(diff * diff) / D_MODEL, F32)
        dy = diff * (1.0 / D_MODEL)
        dgf_ref[...] += jnp.sum(dy * n3, axis=0, keepdims=True)
        dh3 = _rms_bwd(dy * gf_ref[...], n3, r3)
        dpp_ref[...] = (dh3 * gate).astype(MXU_DTYPE)
        dz = (dh3 * pp * gate * (1.0 - gate)).astype(MXU_DTYPE)
        dz_ref[...] = dz
        dhn = _dot_nt(dz, wg_ref[...])
        dgp_ref[...] += jnp.sum(dhn * n2, axis=0, keepdims=True)
        dh_ref[...] = dh3 + _rms_bwd(dhn * gp_ref[...], n2, r2)

    return pl.pallas_call(
        body, name="ple_loss", grid=(s // tm,),
        in_specs=[_rows(tm, D_MODEL), _rows(tm, PLE_DIM), _rows(tm, D_MODEL), _full((1, D_MODEL)),
                  _full((D_MODEL, D_MODEL)), _full((D_MODEL, PLE_DIM)), _full((1, D_MODEL))],
        out_specs=[_full((1, LANES)), _rows(tm, D_MODEL), _rows(tm, D_MODEL), _rows(tm, D_MODEL),
                   _rows(tm, D_MODEL), _full((1, D_MODEL)), _full((1, D_MODEL))],
        out_shape=[jax.ShapeDtypeStruct((1, LANES), F32), jax.ShapeDtypeStruct((s, D_MODEL), F32),
                   jax.ShapeDtypeStruct((s, D_MODEL), MXU_DTYPE), jax.ShapeDtypeStruct((s, D_MODEL), MXU_DTYPE),
                   jax.ShapeDtypeStruct((s, D_MODEL), MXU_DTYPE), jax.ShapeDtypeStruct((1, D_MODEL), F32),
                   jax.ShapeDtypeStruct((1, D_MODEL), F32)],
        compiler_params=_params(("arbitrary",)),
    )(h2, p, target, g_ple, w_pg, w_pp_t, g_final)


def _ffn_bwd(dh2, h1, gate, up, g_ffn, w_down, w_gate_t, w_up_t):
    s = h1.shape[0]
    tm = 256

    def body(dh_ref, h_ref, gate_ref, up_ref, g_ref, wd_ref, wgt_ref, wut_ref,
             o_ref, act_ref, dg_ref, du_ref, dgn_ref):
        @pl.when(pl.program_id(0) == 0)
        def _():
            dgn_ref[...] = jnp.zeros_like(dgn_ref)

        dh = dh_ref[...]
        dhb = dh.astype(MXU_DTYPE)
        strips = [dict(cols=slice(lo, hi)) for lo, hi in FF_STRIPS]

        def back_down(t):
            t['dact'] = _dot_nt(dhb, wd_ref[t['cols'], :])

        def back_act(t):
            cols, dact = t['cols'], t['dact']
            g = gate_ref[:, cols].astype(F32)
            u = up_ref[:, cols].astype(F32)
            sg = jax.nn.sigmoid(g)
            silu = g * sg
            act_ref[:, cols] = (silu * u).astype(MXU_DTYPE)
            t['du'] = (dact * silu).astype(MXU_DTYPE)
            t['dg'] = (dact * u * sg * (1.0 + g * (1.0 - sg))).astype(MXU_DTYPE)
            du_ref[:, cols] = t['du']
            dg_ref[:, cols] = t['dg']

        def back_in(t):
            return _dot(t['dg'], wgt_ref[t['cols'], :]) + _dot(t['du'], wut_ref[t['cols'], :])

        dhn = jnp.zeros((tm, D_MODEL), F32)
        back_down(strips[0])
        for i, t in enumerate(strips):
            if i + 1 < len(strips):
                back_down(strips[i + 1])
            back_act(t)
            dhn = dhn + back_in(t)
        n, r = _rms_stats(h_ref[...])
        dgn_ref[...] += jnp.sum(dhn * n, axis=0, keepdims=True)
        o_ref[...] = dh + _rms_bwd(dhn * g_ref[...], n, r)

    return pl.pallas_call(
        body, name="ffn_bwd", grid=(s // tm,),
        in_specs=[_rows(tm, D_MODEL), _rows(tm, D_MODEL), _rows(tm, D_FF), _rows(tm, D_FF), _full((1, D_MODEL)),
                  _full((D_FF, D_MODEL)), _full((D_FF, D_MODEL)), _full((D_FF, D_MODEL))],
        out_specs=[_rows(tm, D_MODEL), _rows(tm, D_FF), _rows(tm, D_FF), _rows(tm, D_FF), _full((1, D_MODEL))],
        out_shape=[jax.ShapeDtypeStruct((s, D_MODEL), F32), jax.ShapeDtypeStruct((s, D_FF), MXU_DTYPE),
                   jax.ShapeDtypeStruct((s, D_FF), MXU_DTYPE), jax.ShapeDtypeStruct((s, D_FF), MXU_DTYPE),
                   jax.ShapeDtypeStruct((1, D_MODEL), F32)],
        compiler_params=_params(("arbitrary",)),
    )(dh2, h1, gate, up, g_ffn, w_down, w_gate_t, w_up_t)


def _out_bwd(dh1, y_b, g_b, w_out):
    s = dh1.shape[0]
    tm = 512
    nc = WIDTH_B // LANES

    def body(dh_ref, *refs):
        yb_refs = refs[:nc]
        g_ref, w_ref, dya_ref, dyb_ref, dg_ref, scr = refs[nc:]

        @pl.when(pl.program_id(0) == 0)
        def _():
            dg_ref[...] = jnp.zeros_like(dg_ref)

        dy = _dot_nt(dh_ref[...].astype(MXU_DTYPE), w_ref[...])
        dya_ref[...] = dy[:, :WIDTH_A]
        dyb = dy[:, WIDTH_A:]
        n, r = _rms_stats(_load_l256(yb_refs, tm))
        dg_ref[...] += jnp.sum(dyb * n, axis=0, keepdims=True)
        dyb_in = _rms_bwd(dyb * g_ref[...], n, r)
        for j in range(nc):
            cols = slice(j * LANES, (j + 1) * LANES)
            _store_l256(scr, dyb_ref, cols, dyb_in[:, cols])

    return pl.pallas_call(
        body, name="out_bwd", grid=(s // tm,), scratch_shapes=[pltpu.VMEM((tm, LANES), F32)],
        in_specs=[_rows(tm, D_MODEL)] + _col_specs(tm, WIDTH_B) + [_full((1, WIDTH_B)), _full((D_MODEL, D_MODEL))],
        out_specs=[_rows(tm, WIDTH_A), _rows(tm, WIDTH_B), _full((1, WIDTH_B))],
        out_shape=[jax.ShapeDtypeStruct((s, WIDTH_A), F32), jax.ShapeDtypeStruct((s, WIDTH_B), F32),
                   jax.ShapeDtypeStruct((1, WIDTH_B), F32)],
        compiler_params=_params(("arbitrary",)),
    )(dh1, *([y_b] * nc), g_b, w_out)


def _attn_bwd_branch(q, k, v, do, o, lse, grads, dil):
    s = q.shape[0]
    br = _Branch(dil, s)
    qn, nb = br.qn, br.nb
    first = grads is None

    def body(*refs):
        bias_ref, q_ref, kc_ref, kp_ref, vc_ref, vp_ref, do_ref, o_ref, lse_ref = refs[:9]
        if first:
            rest = refs[9:]
        else:
            dq_in, dk_in, dv_in = refs[9:12]
            rest = refs[12:]
        dq_ref, dk_ref, dv_ref, dk_carry, dv_carry = rest
        n = pl.program_id(1)

        @pl.when(n == 0)
        def _():
            dk_carry[...] = jnp.zeros_like(dk_carry)
            dv_carry[...] = jnp.zeros_like(dv_carry)

        @pl.when(n < nb)
        def _():
            bias2 = jnp.concatenate([bias_ref[0], bias_ref[0]], axis=0)
            lane = lax.broadcasted_iota(jnp.int32, (qn, LANES), 1)
            lo = lane < HEAD_DIM
            mask_f = lo.astype(F32)
            mask_lo = mask_f.astype(MXU_DTYPE)
            def prepare(j, hp):
                cols = slice(hp * LANES, (hp + 1) * LANES)
                qp = br.load(q_ref, cols, j)
                dop = br.load(do_ref, cols, j)
                prod = dop * br.load(o_ref, cols, j)
                prod_lo = prod * mask_f
                lse = br.load(lse_ref, cols, j)
                return dict(
                    j=j, cols=cols,
                    kcat=jnp.concatenate([br.load(kp_ref, cols, j), br.load(kc_ref, cols, j)], axis=0),
                    vcat=jnp.concatenate([br.load(vp_ref, cols, j), br.load(vc_ref, cols, j)], axis=0),
                    qs=jnp.concatenate([qp * mask_lo, qp * (1 - mask_lo)], axis=0),
                    dos=jnp.concatenate([dop * mask_f, dop * (1.0 - mask_f)], axis=0).astype(MXU_DTYPE),
                    delta=jnp.concatenate([jnp.sum(prod_lo, axis=1, keepdims=True),
                                           jnp.sum(prod - prod_lo, axis=1, keepdims=True)], axis=0),
                    lse2=jnp.concatenate([lse[:, :1], lse[:, HEAD_DIM:HEAD_DIM + 1]], axis=0))

            def scores(t):
                t['sc'] = _dot_nt(t['qs'], t['kcat'])
                t['dp'] = _dot_nt(t['dos'], t['vcat'])

            def softmax(t):
                p = jnp.exp(t['sc'] + bias2 - t['lse2'])
                t['ds'] = (p * (t['dp'] - t['delta'])).astype(MXU_DTYPE)
                t['p'] = p.astype(MXU_DTYPE)

            def gradients(t):
                t['dvc'] = _dot_tn(t['p'], t['dos'])
                t['dkc'] = _dot_tn(t['ds'], t['qs'])
                t['dq2'] = _dot(t['ds'], t['kcat'])

            def store(t):
                j, cols, dkc, dvc = t['j'], t['cols'], t['dkc'], t['dvc']
                dq = jnp.where(lo, t['dq2'][:qn], t['dq2'][qn:])
                dk_prev = dk_carry[j, :, cols] + dkc[:qn]
                dv_prev = dv_carry[j, :, cols] + dvc[:qn]
                if not first:
                    dq = dq + br.load(dq_in, cols, j)
                    dk_prev = dk_prev + br.load(dk_in, cols, j)
                    dv_prev = dv_prev + br.load(dv_in, cols, j)
                br.store(dq_ref, cols, dq, j)
                br.store(dk_ref, cols, dk_prev, j)
                br.store(dv_ref, cols, dv_prev, j)
                dk_carry[j, :, cols] = dkc[qn:]
                dv_carry[j, :, cols] = dvc[qn:]

            for j, first_pair in itertools.product(range(br.res), range(0, HEADS_B // 2, PAIRS_ABREAST)):
                group = [prepare(j, hp) for hp in range(first_pair, first_pair + PAIRS_ABREAST)]
                for stage in (scores, softmax, gradients, store):
                    for t in group:
                        stage(t)

        @pl.when(n == nb)
        def _():
            for j in range(br.res):
                dk_last = dk_carry[j]
                dv_last = dv_carry[j]
                if not first:
                    dk_last = dk_last + br.load(dk_in, slice(None), j)
                    dv_last = dv_last + br.load(dv_in, slice(None), j)
                br.store(dk_ref, slice(None), dk_last, j)
                br.store(dv_ref, slice(None), dv_last, j)

    cur = lambda n: jnp.minimum(n, nb - 1)
    before = lambda n: jnp.maximum(cur(n) - 1, 0)
    late = lambda n: jnp.maximum(n - 1, 0)
    in_specs = [br.bias_spec(cur), br.spec(WIDTH_B, cur), br.spec(WIDTH_B, cur), br.spec(WIDTH_B, before),
                br.spec(WIDTH_B, cur), br.spec(WIDTH_B, before), br.spec(WIDTH_B, cur), br.spec(WIDTH_B, cur),
                br.spec(WIDTH_B, cur)]
    args = [jnp.asarray(br.bias)] + [br.view(a) for a in (q, k, k, v, v, do, o, lse)]
    if not first:
        in_specs += [br.spec(WIDTH_B, cur), br.spec(WIDTH_B, late), br.spec(WIDTH_B, late)]
        args += [br.view(g) for g in grads]
    res = pl.pallas_call(
        body, name="attn_bwd_d%d" % dil, grid=(br.grid[0], nb + 1), in_specs=in_specs,
        out_specs=[br.spec(WIDTH_B, cur), br.spec(WIDTH_B, late), br.spec(WIDTH_B, late)],
        out_shape=[jax.ShapeDtypeStruct((s // L_BLOCK, 4, 4, L_GROUP, WIDTH_B), F32)] * 3,
        scratch_shapes=[pltpu.VMEM((br.res, qn, WIDTH_B), F32), pltpu.VMEM((br.res, qn, WIDTH_B), F32)],
        compiler_params=_params(("arbitrary", "arbitrary")),
    )(*args)
    return tuple(a.reshape(s, WIDTH_B) for a in res)


def _sgu_bwd(uv, dya_n, w_tril, w_tril_t, bias, g_sgu, g_a):
    s = uv.shape[0]
    tm = 512

    def body(uv_ref, dy_ref, w_ref, wt_ref, b_ref, gs_ref, ga_ref, duv_ref, dw_ref, db_ref, dgs_ref, dga_ref,
             db_acc):
        i = pl.program_id(0)

        @pl.when(i == 0)
        def _():
            dw_ref[...] = jnp.zeros_like(dw_ref)
            dgs_ref[...] = jnp.zeros_like(dgs_ref)
            dga_ref[...] = jnp.zeros_like(dga_ref)
            db_acc[...] = jnp.zeros_like(db_acc)

        t = _sgu_forward_tile(uv_ref[...], w_ref, b_ref[...], gs_ref[...])
        na, ra = _rms_stats(t['ya'])
        dyn = dy_ref[...]
        dga_ref[...] += jnp.sum(dyn * na, axis=0, keepdims=True)
        dya = _rms_bwd(dyn * ga_ref[...], na, ra)
        dug = dya * t['mixed']
        dmixed = dya * t['ug']
        dmb = dmixed.astype(MXU_DTYPE)
        masks = _half_masks(MXU_DTYPE)
        chunks = []
        db = jnp.zeros((CHUNK, WIDTH_A), F32)
        for c in range(tm // CHUNK):
            rows = slice(c * CHUNK, (c + 1) * CHUNK)
            db = db + dmixed[rows]
            groups = []
            for gp in range(2):
                cols = slice(gp * LANES, (gp + 1) * LANES)
                dm_g = dmb[rows, cols]
                vn_g = t['vn'][rows, cols]
                dvn_g = jnp.zeros((CHUNK, LANES), F32)
                for j in range(2):
                    dm_h = dm_g * masks[j]
                    dvn_g = dvn_g + _dot(wt_ref[2 * gp + j], dm_h)
                    dw_ref[2 * gp + j] += _dot_nt(dm_h, vn_g)
                groups.append(dvn_g)
            chunks.append(jnp.concatenate(groups, axis=1))
        db_acc[...] += db
        dvn = jnp.concatenate(chunks, axis=0)
        vhat = t['vhat']
        dgs_ref[...] += jnp.sum(dvn * vhat, axis=0, keepdims=True)
        dvh = dvn * gs_ref[...]
        dvg = t['rs'] * (dvh - jnp.mean(dvh, axis=-1, keepdims=True)
                         - vhat * jnp.mean(dvh * vhat, axis=-1, keepdims=True))
        duv_ref[:, :WIDTH_A] = (dug * _gelu_grad(t['u'], t['tu'])).astype(MXU_DTYPE)
        duv_ref[:, WIDTH_A:] = (dvg * _gelu_grad(t['v'], t['tv'])).astype(MXU_DTYPE)

        @pl.when(i == pl.num_programs(0) - 1)
        def _():
            lane_a = lax.broadcasted_iota(jnp.int32, (CHUNK, WIDTH_A), 1)
            lane = lax.broadcasted_iota(jnp.int32, (CHUNK, LANES), 1)
            acc = db_acc[...]
            out = jnp.zeros((CHUNK, LANES), F32)
            for h in range(HEADS_A):
                col = jnp.sum(jnp.where(lane_a // HEAD_DIM == h, acc, 0.0), axis=1, keepdims=True)
                out = jnp.where(lane == h, col, out)
            db_ref[...] = out
            causal = (lax.broadcasted_iota(jnp.int32, (CHUNK, CHUNK), 0)
                      >= lax.broadcasted_iota(jnp.int32, (CHUNK, CHUNK), 1))
            for h in range(HEADS_A):
                dw_ref[h] = jnp.where(causal, dw_ref[h], 0.0)

    return pl.pallas_call(
        body, name="sgu_bwd", grid=(s // tm,),
        in_specs=[_rows(tm, 2 * WIDTH_A), _rows(tm, WIDTH_A), _full((HEADS_A, CHUNK, CHUNK)),
                  _full((HEADS_A, CHUNK, CHUNK)), _full((CHUNK, WIDTH_A)), _full((1, WIDTH_A)),
                  _full((1, WIDTH_A))],
        out_specs=[_rows(tm, 2 * WIDTH_A), _full((HEADS_A, CHUNK, CHUNK)), _full((CHUNK, LANES)),
                   _full((1, WIDTH_A)), _full((1, WIDTH_A))],
        out_shape=[jax.ShapeDtypeStruct((s, 2 * WIDTH_A), MXU_DTYPE),
                   jax.ShapeDtypeStruct((HEADS_A, CHUNK, CHUNK), F32), jax.ShapeDtypeStruct((CHUNK, LANES), F32),
                   jax.ShapeDtypeStruct((1, WIDTH_A), F32), jax.ShapeDtypeStruct((1, WIDTH_A), F32)],
        scratch_shapes=[pltpu.VMEM((CHUNK, WIDTH_A), F32)],
        compiler_params=_params(("arbitrary",)),
    )(uv, dya_n, w_tril, w_tril_t, bias, g_sgu, g_a)


def _in_bwd_proj(duv, dq, dk, dv, cos_t, sin_t):
    s = duv.shape[0]
    tm = 512
    nc = WIDTH_B // LANES

    def body(duv_ref, *refs):
        dq_refs, dk_refs, dv_refs = refs[:nc], refs[nc:2 * nc], refs[2 * nc:3 * nc]
        cos_ref, sin_ref, dp_ref = refs[3 * nc:]
        cos = cos_ref[...]
        sin = sin_ref[...]
        dp_ref[:, :2 * WIDTH_A] = duv_ref[...]
        for i in range(nc):
            lo = 2 * WIDTH_A + i * LANES
            tq = _load_l256(dq_refs[i:i + 1], tm) * (HEAD_DIM ** -0.5)
            tk = _load_l256(dk_refs[i:i + 1], tm)
            dp_ref[:, lo:lo + LANES] = (tq * cos + _rope_partner(tq * sin)).astype(MXU_DTYPE)
            dp_ref[:, lo + WIDTH_B:lo + WIDTH_B + LANES] = (tk * cos + _rope_partner(tk * sin)).astype(MXU_DTYPE)
            dp_ref[:, lo + 2 * WIDTH_B:lo + 2 * WIDTH_B + LANES] = _load_l256(dv_refs[i:i + 1], tm).astype(MXU_DTYPE)

    return pl.pallas_call(
        body, name="in_bwd_proj", grid=(s // tm,),
        in_specs=[_rows(tm, 2 * WIDTH_A)] + 3 * _col_specs(tm, WIDTH_B) + [_rows(tm, LANES), _rows(tm, LANES)],
        out_specs=_rows(tm, IN_COLS), out_shape=jax.ShapeDtypeStruct((s, IN_COLS), MXU_DTYPE),
        compiler_params=_params(("arbitrary",)),
    )(duv, *([dq] * nc), *([dk] * nc), *([dv] * nc), cos_t, sin_t)


def _in_bwd_x(dproj, w_in_t, x, g_mix, dh1):
    s = x.shape[0]
    tm = 512

    def body(dp_ref, wt_ref, x_ref, g_ref, dh_ref, gx_ref, dg_ref):
        @pl.when(pl.program_id(0) == 0)
        def _():
            dg_ref[...] = jnp.zeros_like(dg_ref)

        dhn = _dot(dp_ref[...], wt_ref[...])
        n, r = _rms_stats(x_ref[...])
        dg_ref[...] += jnp.sum(dhn * n, axis=0, keepdims=True)
        gx_ref[...] = dh_ref[...] + _rms_bwd(dhn * g_ref[...], n, r)

    return pl.pallas_call(
        body, name="in_bwd_x", grid=(s // tm,),
        in_specs=[_rows(tm, IN_COLS), _full((IN_COLS, D_MODEL)), _rows(tm, D_MODEL), _full((1, D_MODEL)),
                  _rows(tm, D_MODEL)],
        out_specs=[_rows(tm, D_MODEL), _full((1, D_MODEL))],
        out_shape=[jax.ShapeDtypeStruct((s, D_MODEL), F32), jax.ShapeDtypeStruct((1, D_MODEL), F32)],
        compiler_params=_params(("arbitrary",)),
    )(dproj, w_in_t, x, g_mix, dh1)


def _wgrad(a, b, name):
    s, m = a.shape
    n = b.shape[1]
    bm = min(m, FF_HALF)
    ts = 1024
    nsteps = s // ts

    def body(a_ref, b_ref, o_ref, acc):
        kk = pl.program_id(1)

        @pl.when(kk == 0)
        def _():
            acc[...] = jnp.zeros_like(acc)

        acc[...] += _dot_tn(a_ref[...].astype(MXU_DTYPE), b_ref[...].astype(MXU_DTYPE))

        @pl.when(kk == nsteps - 1)
        def _():
            o_ref[...] = acc[...].astype(o_ref.dtype)

    return pl.pallas_call(
        body, name=name, grid=(m // bm, nsteps),
        in_specs=[pl.BlockSpec((ts, bm), lambda i, kk: (kk, i)), pl.BlockSpec((ts, n), lambda i, kk: (kk, 0))],
        out_specs=pl.BlockSpec((bm, n), lambda i, kk: (i, 0)), out_shape=jax.ShapeDtypeStruct((m, n), jnp.bfloat16),
        scratch_shapes=[pltpu.VMEM((bm, n), F32)],
        compiler_params=_params(("arbitrary", "arbitrary")),
    )(a, b)


def _rope_tables(s):
    half = HEAD_DIM // 2
    inv = ROPE_THETA ** (-jnp.arange(half, dtype=F32) / half)
    ang = jnp.arange(s, dtype=F32)[:, None] * jnp.tile(inv, LANES // half)[None, :]
    sign = jnp.tile(jnp.concatenate([-jnp.ones(half, F32), jnp.ones(half, F32)]), LANES // HEAD_DIM)
    return jnp.cos(ang), jnp.sin(ang) * sign[None, :]


MESH = pl.DeviceIdType.MESH
ANY = pl.BlockSpec(memory_space=pl.ANY)
SEM = pl.BlockSpec(memory_space=pltpu.SEMAPHORE)
SPLIT_COPY = pltpu.CompilerParams(has_side_effects=pltpu.SideEffectType.DATAFLOW_SIDE_EFFECTING)
SLAB_IS_TRANSPOSED = {'w_in': True, 'w_out': False, 'w_gate': True, 'w_up': True, 'w_down': False,
                      'w_ple_gate': False, 'w_ple_proj': True}


def _place():
    x, y, c = lax.axis_index("x"), lax.axis_index("y"), lax.axis_index("c")
    other_chips = [(1 - x, y), (x, 1 - y), (1 - x, 1 - y)]
    return x, y, c, other_chips


def _chip_of(chip):
    return 2 * chip[0] + chip[1]


def _half(ref, lead, hc):
    hr = ref.shape[1] // 2
    return ref.at[lead, pl.ds(hc * hr, hr), :]


def _put_own(stack, own, index):
    return lax.dynamic_update_slice(stack, own[None], (index,) + (0,) * own.ndim)


def _first_hops(slab_ref, land_ref, send_sems, recv_sems):
    x, y, c, chips = _place()
    hr = slab_ref.shape[0] // 2
    mine = slab_ref.at[pl.ds(c * hr, hr), :]
    sends = [pltpu.make_async_remote_copy(src_ref=mine, dst_ref=_half(land_ref, 2 * x + y, c), send_sem=send_sems.at[j],
                                          recv_sem=recv_sems.at[j], device_id=(*chip, c), device_id_type=MESH)
             for j, chip in enumerate(chips)]
    recvs = [pltpu.make_async_remote_copy(src_ref=mine, dst_ref=_half(land_ref, _chip_of(chip), c),
                                          send_sem=send_sems.at[j], recv_sem=recv_sems.at[j], device_id=(*chip, c),
                                          device_id_type=MESH) for j, chip in enumerate(chips)]
    return sends, recvs


def _second_hops(land_ref, send_sems, recv_sems):
    x, y, c, chips = _place()
    sibling = (x, y, 1 - c)
    sends = [pltpu.make_async_remote_copy(src_ref=_half(land_ref, _chip_of(chip), c),
                                          dst_ref=_half(land_ref, _chip_of(chip), c), send_sem=send_sems.at[j],
                                          recv_sem=recv_sems.at[j], device_id=sibling, device_id_type=MESH)
             for j, chip in enumerate(chips)]
    recvs = [pltpu.make_async_remote_copy(src_ref=_half(land_ref, _chip_of(chip), c),
                                          dst_ref=_half(land_ref, _chip_of(chip), 1 - c), send_sem=send_sems.at[j],
                                          recv_sem=recv_sems.at[j], device_id=sibling, device_id_type=MESH)
             for j, chip in enumerate(chips)]
    return sends, recvs


def _two_level_start(slab):
    def body(slab_ref, land_ref, send_sems, recv_sems, slab_thru, land_thru, token):
        for cp in _first_hops(slab_ref, land_ref, send_sems, recv_sems)[0]:
            cp.start()
        token[...] = jnp.zeros_like(token)

    land = lax.empty((N_CHIPS,) + slab.shape, slab.dtype)
    hbm = lambda a: pltpu.HBM(a.shape, a.dtype)
    res = pl.pallas_call(
        body, name="two_level_start",
        out_shape=(pltpu.SemaphoreType.DMA((3,)), pltpu.SemaphoreType.DMA((3,)), hbm(slab), hbm(land),
                   jax.ShapeDtypeStruct((8, LANES), F32)),
        in_specs=[ANY, ANY], out_specs=(SEM, SEM, ANY, ANY, pl.BlockSpec(memory_space=pltpu.VMEM)),
        input_output_aliases={0: 2, 1: 3}, compiler_params=SPLIT_COPY,
    )(pltpu.with_memory_space_constraint(slab, pltpu.HBM), pltpu.with_memory_space_constraint(land, pltpu.HBM))
    return res[:-1], res[-1]


def _two_level_forward(handle, after):
    send_sems, recv_sems, slab, land = handle

    def body(slab_ref, land_ref, send_sems, recv_sems, *refs):
        send2, recv2 = refs[len(after):len(after) + 2]
        sends, recvs = _first_hops(slab_ref, land_ref, send_sems, recv_sems)
        for cp in sends:
            cp.wait_send()
        for cp in recvs:
            cp.wait_recv()
        for cp in _second_hops(land_ref, send2, recv2)[0]:
            cp.start()

    hbm = lambda a: pltpu.HBM(a.shape, a.dtype)
    return pl.pallas_call(
        body, name="two_level_forward",
        out_shape=(pltpu.SemaphoreType.DMA((3,)), pltpu.SemaphoreType.DMA((3,)), hbm(slab), hbm(land)),
        in_specs=[ANY, ANY, SEM, SEM] + [ANY] * len(after), out_specs=(SEM, SEM, ANY, ANY),
        input_output_aliases={0: 2, 1: 3}, compiler_params=SPLIT_COPY,
    )(slab, land, send_sems, recv_sems, *after)


def _two_level_wait(handle):
    send_sems, recv_sems, slab, land = handle

    def body(slab_ref, land_ref, send_sems, recv_sems, slab_thru, land_thru):
        sends, recvs = _second_hops(land_ref, send_sems, recv_sems)
        for cp in sends:
            cp.wait_send()
        for cp in recvs:
            cp.wait_recv()

    hbm = lambda a: pltpu.HBM(a.shape, a.dtype)
    slab, land = pl.pallas_call(
        body, name="two_level_wait", out_shape=(hbm(slab), hbm(land)), in_specs=[ANY, ANY, SEM, SEM],
        out_specs=(ANY, ANY), input_output_aliases={0: 0, 1: 1}, compiler_params=SPLIT_COPY,
    )(slab, land, send_sems, recv_sems)
    me = 2 * lax.axis_index("x") + lax.axis_index("y")
    return _put_own(land, slab, me).reshape(N_CHIPS * slab.shape[0], slab.shape[1])


def _gather_copies(slab_refs, land_refs, send_sems, recv_sems):
    x, y, c, chips = _place()
    sends, recvs = [], []
    for k, (src, land) in enumerate(zip(slab_refs, land_refs)):
        hr = src.shape[0] // 2
        for j, chip in enumerate(chips):
            for t in range(2):
                sends.append(pltpu.make_async_remote_copy(
                    src_ref=src.at[pl.ds(c * hr, hr), :], dst_ref=_half(land, 2 * x + y, c),
                    send_sem=send_sems.at[6 * k + 2 * j + t], recv_sem=recv_sems.at[6 * k + 2 * j + c],
                    device_id=(*chip, t), device_id_type=MESH))
                recvs.append(pltpu.make_async_remote_copy(
                    src_ref=src.at[pl.ds(t * hr, hr), :], dst_ref=_half(land, _chip_of(chip), t),
                    send_sem=send_sems.at[6 * k + 2 * j + t], recv_sem=recv_sems.at[6 * k + 2 * j + t],
                    device_id=(*chip, t), device_id_type=MESH))
    return sends, recvs


def _all_gather_start(slabs, after):
    n = len(slabs)

    def body(*refs):
        slab_refs, land_refs = refs[:n], refs[n:2 * n]
        send_sems, recv_sems = refs[2 * n + 1:2 * n + 3]
        token = refs[-1]
        sends, _ = _gather_copies(slab_refs, land_refs, send_sems, recv_sems)
        for cp in sends:
            cp.start()
        token[...] = jnp.zeros_like(token)

    lands = [lax.empty((N_CHIPS,) + s.shape, s.dtype) for s in slabs]
    hbm = lambda a: pltpu.HBM(a.shape, a.dtype)
    res = pl.pallas_call(
        body, name="all_gather_start",
        out_shape=(pltpu.SemaphoreType.DMA((6 * n,)), pltpu.SemaphoreType.DMA((6 * n,)), *map(hbm, slabs),
                   *map(hbm, lands), jax.ShapeDtypeStruct((8, LANES), F32)),
        in_specs=[ANY] * (2 * n + 1),
        out_specs=(SEM, SEM, *([ANY] * (2 * n)), pl.BlockSpec(memory_space=pltpu.VMEM)),
        input_output_aliases={i: 2 + i for i in range(2 * n)}, compiler_params=SPLIT_COPY,
    )(*[pltpu.with_memory_space_constraint(a, pltpu.HBM) for a in list(slabs) + lands], after)
    return res[:-1], res[-1]


def _all_gather_wait(handle, after):
    send_sems, recv_sems = handle[:2]
    n = (len(handle) - 2) // 2
    slabs, lands = handle[2:2 + n], handle[2 + n:]

    def body(*refs):
        slab_refs, land_refs = refs[:n], refs[n:2 * n]
        send_sems, recv_sems = refs[2 * n:2 * n + 2]
        sends, recvs = _gather_copies(slab_refs, land_refs, send_sems, recv_sems)
        for cp in sends:
            cp.wait_send()
        for cp in recvs:
            cp.wait_recv()

    hbm = lambda a: pltpu.HBM(a.shape, a.dtype)
    res = pl.pallas_call(
        body, name="all_gather_wait", out_shape=tuple(map(hbm, list(slabs) + list(lands))),
        in_specs=[ANY] * (2 * n) + [SEM, SEM, ANY], out_specs=tuple([ANY] * (2 * n)),
        input_output_aliases={i: i for i in range(2 * n)}, compiler_params=SPLIT_COPY,
    )(*slabs, *lands, send_sems, recv_sems, after)
    me = 2 * lax.axis_index("x") + lax.axis_index("y")
    return [_put_own(land, slab, me).reshape(N_CHIPS * slab.shape[0], slab.shape[1])
            for slab, land in zip(res[:n], res[n:])]


def _scatter_copies(part_refs, land_refs, send_sems, recv_sems):
    x, y, c, chips = _place()
    me = 4 * x + 2 * y + c
    sends, recvs = [], []
    for k, (part, land) in enumerate(zip(part_refs, land_refs)):
        for j, chip in enumerate(chips):
            for t in range(2):
                sends.append(pltpu.make_async_remote_copy(
                    src_ref=part.at[_chip_of(chip)], dst_ref=land.at[me],
                    send_sem=send_sems.at[7 * k + 2 * j + t], recv_sem=recv_sems.at[7 * k + 2 * j + c],
                    device_id=(*chip, t), device_id_type=MESH))
                recvs.append(pltpu.make_async_remote_copy(
                    src_ref=part.at[_chip_of(chip)], dst_ref=land.at[2 * _chip_of(chip) + t],
                    send_sem=send_sems.at[7 * k + 2 * j + t], recv_sem=recv_sems.at[7 * k + 2 * j + t],
                    device_id=(*chip, t), device_id_type=MESH))
        sends.append(pltpu.make_async_remote_copy(
            src_ref=part.at[2 * x + y], dst_ref=land.at[me], send_sem=send_sems.at[7 * k + 6],
            recv_sem=recv_sems.at[7 * k + 6], device_id=(x, y, 1 - c), device_id_type=MESH))
        recvs.append(pltpu.make_async_remote_copy(
            src_ref=part.at[2 * x + y], dst_ref=land.at[4 * x + 2 * y + 1 - c],
            send_sem=send_sems.at[7 * k + 6], recv_sem=recv_sems.at[7 * k + 6], device_id=(x, y, 1 - c),
            device_id_type=MESH))
    return sends, recvs


def _reduce_scatter_start(parts, name):
    n = len(parts)
    parts = [p.reshape(N_CHIPS, p.shape[0] // N_CHIPS, p.shape[1]) for p in parts]

    def body(*refs):
        part_refs, land_refs = refs[:n], refs[n:2 * n]
        send_sems, recv_sems = refs[2 * n:2 * n + 2]
        token = refs[-1]
        sends, _ = _scatter_copies(part_refs, land_refs, send_sems, recv_sems)
        for cp in sends:
            cp.start()
        token[...] = jnp.zeros_like(token)

    lands = [lax.empty((N_DEV, p.shape[1], p.shape[2]), p.dtype) for p in parts]
    hbm = lambda a: pltpu.HBM(a.shape, a.dtype)
    res = pl.pallas_call(
        body, name=name,
        out_shape=(pltpu.SemaphoreType.DMA((7 * n,)), pltpu.SemaphoreType.DMA((7 * n,)), *map(hbm, parts),
                   *map(hbm, lands), jax.ShapeDtypeStruct((8, LANES), F32)),
        in_specs=[ANY] * (2 * n), out_specs=(SEM, SEM, *([ANY] * (2 * n)), pl.BlockSpec(memory_space=pltpu.VMEM)),
        input_output_aliases={i: 2 + i for i in range(2 * n)}, compiler_params=SPLIT_COPY,
    )(*[pltpu.with_memory_space_constraint(a, pltpu.HBM) for a in parts + lands])
    return res[:-1], res[-1]


def _reduce_scatter_wait(handle, after, name):
    after = list(after) if isinstance(after, (list, tuple)) else [after]
    send_sems, recv_sems = handle[:2]
    n = (len(handle) - 2) // 2
    parts, lands = handle[2:2 + n], handle[2 + n:]

    def body(*refs):
        part_refs, land_refs = refs[:n], refs[n:2 * n]
        send_sems, recv_sems = refs[2 * n:2 * n + 2]
        sends, recvs = _scatter_copies(part_refs, land_refs, send_sems, recv_sems)
        for cp in sends:
            cp.wait_send()
        for cp in recvs:
            cp.wait_recv()

    hbm = lambda a: pltpu.HBM(a.shape, a.dtype)
    res = pl.pallas_call(
        body, name=name, out_shape=tuple(map(hbm, list(parts) + list(lands))),
        in_specs=[ANY] * (2 * n) + [SEM, SEM] + [ANY] * len(after), out_specs=tuple([ANY] * (2 * n)),
        input_output_aliases={i: i for i in range(2 * n)}, compiler_params=SPLIT_COPY,
    )(*parts, *lands, send_sems, recv_sems, *after)
    return list(zip(res[:n], res[n:]))


def _adamw_of_shares(w, own, land, m, v, name):
    rows, cols = w.shape
    tm = rows // 4 if rows % 32 == 0 else rows
    x, y, c = lax.axis_index("x"), lax.axis_index("y"), lax.axis_index("c")
    where = jnp.stack([2 * x + y, 4 * x + 2 * y + c]).astype(jnp.int32)

    def body(where_ref, w_ref, own_ref, land_ref, m_ref, v_ref, g_ref, d_ref, nm_ref, nv_ref):
        me = where_ref[1]
        g_ = jnp.zeros((tm, cols), F32)
        for dev in range(N_DEV):
            g_ = g_ + jnp.where(me == dev, own_ref[0], land_ref[dev]).astype(F32)
        m_ = ADAM_B1 * m_ref[...] + (1.0 - ADAM_B1) * g_
        v_ = ADAM_B2 * v_ref[...] + (1.0 - ADAM_B2) * (g_ * g_)
        m_hat = m_ / (1.0 - ADAM_B1 ** ADAM_STEP)
        v_hat = v_ / (1.0 - ADAM_B2 ** ADAM_STEP)
        g_ref[...] = g_
        d_ref[...] = -ADAM_LR * (m_hat / (jnp.sqrt(v_hat) + ADAM_EPS) + ADAM_WD * w_ref[...])
        nm_ref[...] = m_
        nv_ref[...] = v_

    tile = pl.BlockSpec((tm, cols), lambda i, where_ref: (i, 0))
    spec = pltpu.PrefetchScalarGridSpec(
        num_scalar_prefetch=1, grid=(rows // tm,),
        in_specs=[tile, pl.BlockSpec((1, tm, cols), lambda i, where_ref: (where_ref[0], i, 0)),
                  pl.BlockSpec((N_DEV, tm, cols), lambda i, where_ref: (0, i, 0)), tile, tile],
        out_specs=[tile] * 4)
    return pl.pallas_call(
        body, name=name, grid_spec=spec, out_shape=[jax.ShapeDtypeStruct(w.shape, F32)] * 4,
        compiler_params=_params(("arbitrary",)),
    )(where, w, own, land, m, v)


def _pack_small(values):
    flat = jnp.concatenate([values[n].reshape(-1).astype(F32) for n in SMALL])
    return jnp.pad(flat, (0, SMALL_ROWS * D_MODEL - flat.shape[0])).reshape(SMALL_ROWS, D_MODEL)


def _unpack_small(block, shapes):
    flat = block.reshape(-1)
    out, lo = {}, 0
    for n in SMALL:
        out[n] = flat[lo:lo + SMALL_SIZES[n]].reshape(shapes[n])
        lo += SMALL_SIZES[n]
    return out


def _after(token, a):
    return a + token[:1, :1].astype(a.dtype)


def kernel(x, p, mix_norm_g, w_in, sgu_w, sgu_b, sgu_norm_g, out_norm_a, out_norm_b, w_out, ffn_norm_g, w_gate, w_up, w_down, ple_norm_g, w_ple_gate, w_ple_proj, final_norm_g, loss_target, m_mix_norm_g, m_w_in, m_sgu_w, m_sgu_b, m_sgu_norm_g, m_out_norm_a, m_out_norm_b, m_w_out, m_ffn_norm_g, m_w_gate, m_w_up, m_w_down, m_ple_norm_g, m_w_ple_gate, m_w_ple_proj, m_final_norm_g, v_mix_norm_g, v_w_in, v_sgu_w, v_sgu_b, v_sgu_norm_g, v_out_norm_a, v_out_norm_b, v_w_out, v_ffn_norm_g, v_w_gate, v_w_up, v_w_down, v_ple_norm_g, v_w_ple_gate, v_w_ple_proj, v_final_norm_g):
    given = dict(locals())
    drop_lead = lambda a, lead: a.reshape(a.shape[lead:])
    xs, ps, target = drop_lead(x, 1), drop_lead(p, 2), drop_lead(loss_target, 1)
    s = xs.shape[0]
    shard = lambda name: drop_lead(given[name], 1)

    def slab_of(name):
        local = shard(name).astype(MXU_DTYPE)
        return local.T if SLAB_IS_TRANSPOSED[name] else local

    w_in_gather, token = _two_level_start(slab_of('w_in'))
    later = ['w_out', 'w_gate', 'w_up', 'w_down', 'w_ple_gate', 'w_ple_proj']
    later_slabs = [slab_of(n) for n in later]
    cos_t, sin_t = (_after(token, table) for table in _rope_tables(s))
    tril = jnp.tril(jnp.ones((CHUNK, CHUNK), F32))
    w_tril = (sgu_w.reshape(HEADS_A, CHUNK, CHUNK) * tril).astype(MXU_DTYPE)
    w_tril_t = jnp.swapaxes(w_tril, 1, 2)
    w_in_gather = _two_level_forward(w_in_gather, [cos_t, sin_t, w_tril, w_tril_t] + later_slabs)
    w_in_t = _two_level_wait(w_in_gather)
    gather, token = _all_gather_start(later_slabs, w_in_t)

    bias = jnp.repeat(sgu_b.reshape(HEADS_A, CHUNK).T, HEAD_DIM, axis=1)
    g = {n: given[n].reshape(1, -1) for n in SMALL if n not in ('sgu_w', 'sgu_b')}

    uv, q, k, v, q1, k1, v1, hn1 = _in_fwd(xs, _after(token, g['mix_norm_g']), w_in_t, cos_t, sin_t)
    ya_n = _sgu_fwd(uv, w_tril, bias, g['sgu_norm_g'], g['out_norm_a'])
    branches = [_attn_fwd_local(q1, k1, v1)] + [_attn_fwd_branch(q, k, v, dil) for dil in DILATIONS[:-1]]
    y_b, lse = _attn_fwd_branch(q, k, v, DILATIONS[-1], earlier=branches)
    stacks = dict(zip(later, _all_gather_wait(gather, lse)))
    w_gate_t, w_up_t, w_pp_t = stacks['w_gate'], stacks['w_up'], stacks['w_ple_proj']
    h1, y_n = _out_fwd(ya_n, y_b, g['out_norm_b'], stacks['w_out'], xs)
    h2, gate, up, hn2 = _ffn_fwd(h1, g['ffn_norm_g'], w_gate_t, w_up_t, stacks['w_down'])
    loss, dh2, dz, dpp, hn3, d_ple_g, d_final_g = _ple_loss(
        h2, ps, target, g['ple_norm_g'], stacks['w_ple_gate'], w_pp_t, g['final_norm_g'])

    share = {}
    share['w_ple_gate'] = _wgrad(hn3, dz, "wgrad_ple_gate")
    share['w_ple_proj'] = _wgrad(dpp, ps, "wgrad_ple_proj")
    scatter_1, token = _reduce_scatter_start([share['w_ple_gate'], share['w_ple_proj']], "reduce_scatter_start_1")
    dh1, act, dgate, dup, d_ffn_g = _ffn_bwd(dh2, h1, gate, up, _after(token, g['ffn_norm_g']), stacks['w_down'],
                                             w_gate_t, w_up_t)
    share['w_down'] = _wgrad(act, dh2, "wgrad_down")
    share['w_gate'] = _wgrad(dgate, hn2, "wgrad_gate")
    share['w_up'] = _wgrad(dup, hn2, "wgrad_up")
    scatter_2, token = _reduce_scatter_start([share['w_down'], share['w_gate'], share['w_up']],
                                             "reduce_scatter_start_2")
    dya_n, dyb, d_out_b = _out_bwd(dh1, y_b, _after(token, g['out_norm_b']), stacks['w_out'])
    share['w_out'] = _wgrad(y_n, dh1, "wgrad_out")
    scatter_3, token = _reduce_scatter_start([share['w_out']], "reduce_scatter_start_3")
    grads = _attn_bwd_local(q1, k1, v1, dyb, y_b, lse)
    for dil in DILATIONS:
        grads = _attn_bwd_branch(q, k, v, dyb, y_b, lse, grads, dil)
    duv, d_sgu_w, d_sgu_b, d_sgu_g, d_out_a = _sgu_bwd(uv, dya_n, w_tril, w_tril_t, bias,
                                                       _after(token, g['sgu_norm_g']), g['out_norm_a'])
    dproj = _in_bwd_proj(duv, grads[0], grads[1], grads[2], cos_t, sin_t)
    share['w_in'] = _wgrad(dproj, hn1, "wgrad_in")
    scatter_4, token = _reduce_scatter_start([share['w_in']], "reduce_scatter_start_4")

    grads, deltas, new_m, new_v = {}, {}, {}, {}
    add_lead = lambda a: a.reshape((1,) + a.shape)

    done = {}

    def finish(names, handles, after, tag):
        landed = []
        for i, handle in enumerate(handles):
            landed += _reduce_scatter_wait(handle, after, "reduce_scatter_wait_%s%d" % (tag, i))
        for n, (own, land) in zip(names, landed):
            turn = (lambda a: a.T) if SLAB_IS_TRANSPOSED[n] else (lambda a: a)
            res = _adamw_of_shares(turn(shard(n)), own, land, turn(shard("m_" + n)), turn(shard("v_" + n)),
                                   "adamw_" + n)
            grads[n], deltas[n], new_m[n], new_v[n] = (add_lead(turn(a)) for a in res)
            done[n] = res[0]

    grad_x, d_mix_g = _in_bwd_x(dproj, w_in_t, xs, _after(token, g['mix_norm_g']), dh1)

    gs = {'mix_norm_g': d_mix_g, 'sgu_w': d_sgu_w, 'sgu_b': d_sgu_b[:, :HEADS_A].T, 'sgu_norm_g': d_sgu_g,
          'out_norm_a': d_out_a, 'out_norm_b': d_out_b, 'ffn_norm_g': d_ffn_g, 'ple_norm_g': d_ple_g,
          'final_norm_g': d_final_g}
    gs_block = _pack_small(gs).at[SMALL_ROWS - 1, 0].set(loss[0, 0])
    to_all = jnp.broadcast_to(gs_block[None], (N_CHIPS,) + gs_block.shape).reshape(-1, D_MODEL)
    scatter_small, token = _reduce_scatter_start([to_all], "small_all_reduce_start")

    finish(['w_ple_gate', 'w_ple_proj', 'w_down', 'w_gate', 'w_up', 'w_out'], [scatter_1, scatter_2, scatter_3], token,
           "early")
    finish(['w_in'], [scatter_4], [done[n] for n in ('w_down', 'w_gate', 'w_up', 'w_out')], "last")
    (own, land), = _reduce_scatter_wait(scatter_small, done['w_in'], "small_all_reduce_wait")
    small = {n: given[n] for n in SMALL}
    small_res = _adamw_of_shares(_pack_small(small), own, land, _pack_small({n: given["m_" + n] for n in SMALL}),
                                 _pack_small({n: given["v_" + n] for n in SMALL}), "adamw_small")
    loss_out = small_res[0][SMALL_ROWS - 1, 0]
    small_shapes = {n: given[n].shape for n in SMALL}
    for res, blk in zip((grads, deltas, new_m, new_v), small_res):
        res.update(_unpack_small(blk, small_shapes))

    outs = [loss_out, add_lead(grad_x)]
    for res in (grads, deltas, new_m, new_v):
        outs += [res[n] for n in WEIGHT_NAMES]
    return tuple(outs)
```

```python
import functools
import itertools
import math

import jax
import jax.numpy as jnp
import numpy as np
from jax import lax
from jax.experimental import pallas as pl
from jax.experimental.pallas import tpu as pltpu

F32 = jnp.float32
MXU_DTYPE = jnp.bfloat16

D_MODEL = 1024
HEAD_DIM = 64
HEADS_A = 4
HEADS_B = 12
WIDTH_A = HEADS_A * HEAD_DIM
WIDTH_B = HEADS_B * HEAD_DIM
CHUNK = 128
BLOCK = 128
DILATIONS = (4, 16)
ROPE_THETA = 10000.0
D_FF = 2816
FF_HALF = D_FF // 2
FF_STRIPS = ((0, 1024), (1024, 2048), (2048, D_FF))
PLE_DIM = 256
IN_COLS = 2 * WIDTH_A + 3 * WIDTH_B
EPS = 1e-6
LANES = 128
N_CHIPS = 4
N_DEV = 8

ADAM_LR = 0.001
ADAM_B1 = 0.9
ADAM_B2 = 0.999
ADAM_EPS = 1e-08
ADAM_WD = 0.01
ADAM_STEP = 10

VMEM_LIMIT = 56 * 1024 * 1024

WEIGHT_NAMES = ['mix_norm_g', 'w_in', 'sgu_w', 'sgu_b', 'sgu_norm_g', 'out_norm_a', 'out_norm_b', 'w_out',
                'ffn_norm_g', 'w_gate', 'w_up', 'w_down', 'ple_norm_g', 'w_ple_gate', 'w_ple_proj', 'final_norm_g']
SMALL = ['mix_norm_g', 'sgu_w', 'sgu_b', 'sgu_norm_g', 'out_norm_a', 'out_norm_b', 'ffn_norm_g', 'ple_norm_g',
         'final_norm_g']
SMALL_SIZES = {'mix_norm_g': 1024, 'sgu_w': 65536, 'sgu_b': 512, 'sgu_norm_g': 256, 'out_norm_a': 256,
               'out_norm_b': 768, 'ffn_norm_g': 1024, 'ple_norm_g': 1024, 'final_norm_g': 1024}
SMALL_ROWS = 72


def _params(semantics=None):
    return pltpu.CompilerParams(dimension_semantics=semantics, vmem_limit_bytes=VMEM_LIMIT)


def _full(shape):
    nd = len(shape)
    return pl.BlockSpec(shape, lambda i: (0,) * nd, pipeline_mode=pl.Buffered(1))


def _rows(tm, width):
    return pl.BlockSpec((tm, width), lambda i: (i, 0))


def _rms_stats(x):
    r = lax.rsqrt(jnp.mean(x * x, axis=-1, keepdims=True) + EPS)
    return x * r, r


def _rms_bwd(dn, n, r):
    return r * (dn - n * jnp.mean(dn * n, axis=-1, keepdims=True))


def _dot(a, b):
    return jnp.dot(a, b, preferred_element_type=F32)


def _dot_nt(a, b):
    return lax.dot_general(a, b, (((1,), (1,)), ((), ())), preferred_element_type=F32)


def _dot_tn(a, b):
    return lax.dot_general(a, b, (((0,), (0,)), ((), ())), preferred_element_type=F32)


def _gelu_parts(x):
    c = math.sqrt(2.0 / math.pi)
    t = jnp.tanh(c * (x + 0.044715 * x * x * x))
    return 0.5 * x * (1.0 + t), t


def _gelu_grad(x, t):
    c = math.sqrt(2.0 / math.pi)
    return 0.5 * (1.0 + t) + 0.5 * x * (1.0 - t * t) * c * (1.0 + 3.0 * 0.044715 * x * x)


def _half_masks(dtype):
    lane = lax.broadcasted_iota(jnp.int32, (BLOCK, LANES), 1)
    lo = (lane < HEAD_DIM).astype(F32)
    return lo.astype(dtype), (1.0 - lo).astype(dtype)


def _rope_partner(t):
    lane = lax.broadcasted_iota(jnp.int32, t.shape, 1)
    first_half = (lane % HEAD_DIM) < (HEAD_DIM // 2)
    return jnp.where(first_half, pltpu.roll(t, LANES - HEAD_DIM // 2, 1), pltpu.roll(t, HEAD_DIM // 2, 1))


PAIRS_ABREAST = 2
RESIDUES_PER_STEP = 4
L_BLOCK = 256
L_GROUP = 16


def _store_l256(scr, out_ref, cols, value, chunk_ref=None):
    tm = value.shape[0]
    half = L_GROUP // 2
    scr[...] = value
    for blk in range(tm // L_BLOCK):
        pieces = [scr[pl.ds(blk * L_BLOCK + r, L_GROUP, stride=L_GROUP), :] for r in range(L_GROUP)]
        for r, piece in enumerate(pieces):
            lo = blk * L_BLOCK + r * L_GROUP
            out_ref[lo:lo + L_GROUP, cols] = piece.astype(out_ref.dtype)
        if chunk_ref is not None:
            for chunk in range(L_BLOCK // BLOCK):
                for r in range(0, L_GROUP, 2):
                    lo = blk * L_BLOCK + chunk * BLOCK + r * half
                    both = [p[chunk * half:(chunk + 1) * half] for p in pieces[r:r + 2]]
                    chunk_ref[lo:lo + L_GROUP, cols] = jnp.concatenate(both, axis=0).astype(chunk_ref.dtype)


def _load_l256(col_refs, tm):
    cols = []
    for ref in col_refs:
        pieces = [ref[pl.ds(blk * L_BLOCK + i, L_GROUP, stride=L_GROUP), :]
                  for blk in range(tm // L_BLOCK) for i in range(L_GROUP)]
        cols.append(jnp.concatenate(pieces, axis=0))
    return jnp.concatenate(cols, axis=1)


def _col_specs(tm, width):
    return [pl.BlockSpec((tm, LANES), lambda i, j=j: (i, j)) for j in range(width // LANES)]


def _in_fwd(x, g_mix, w_in_t, cos_t, sin_t):
    s = x.shape[0]
    tm = 512

    def body(x_ref, g_ref, wt_ref, cos_ref, sin_ref, uv_ref, q_ref, k_ref, v_ref, q1_ref, k1_ref, v1_ref, hn_ref,
             *scrs):
        n, _ = _rms_stats(x_ref[...])
        hn = (n * g_ref[...]).astype(MXU_DTYPE)
        hn_ref[...] = hn
        cos = cos_ref[...]
        sin = sin_ref[...]
        strip = 2 * LANES
        for j in range(IN_COLS // strip):
            proj = _dot_nt(hn, wt_ref[j * strip:(j + 1) * strip, :])
            lo = j * strip - 2 * WIDTH_A
            if lo < 0:
                uv_ref[:, j * strip:(j + 1) * strip] = proj
                continue
            which, lo = divmod(lo, WIDTH_B)
            for i in range(strip // LANES):
                t = proj[:, i * LANES:(i + 1) * LANES]
                cols = slice(lo + i * LANES, lo + (i + 1) * LANES)
                scr = scrs[i]
                if which == 0:
                    _store_l256(scr, q_ref, cols, (t * cos + _rope_partner(t) * sin) * (HEAD_DIM ** -0.5), q1_ref)
                elif which == 1:
                    _store_l256(scr, k_ref, cols, t * cos + _rope_partner(t) * sin, k1_ref)
                else:
                    _store_l256(scr, v_ref, cols, t, v1_ref)

    return pl.pallas_call(
        body, name="in_fwd", grid=(s // tm,), scratch_shapes=[pltpu.VMEM((tm, LANES), F32)] * 2,
        in_specs=[_rows(tm, D_MODEL), _full((1, D_MODEL)), _full((IN_COLS, D_MODEL)), _rows(tm, LANES),
                  _rows(tm, LANES)],
        out_specs=[_rows(tm, 2 * WIDTH_A)] + [_rows(tm, WIDTH_B)] * 6 + [_rows(tm, D_MODEL)],
        out_shape=[jax.ShapeDtypeStruct((s, 2 * WIDTH_A), F32)] + [jax.ShapeDtypeStruct((s, WIDTH_B), MXU_DTYPE)] * 6
        + [jax.ShapeDtypeStruct((s, D_MODEL), MXU_DTYPE)],
        compiler_params=_params(("arbitrary",)),
    )(x, g_mix, w_in_t, cos_t, sin_t)


class _Branch:
    def __init__(self, dil, s):
        i = np.arange(L_GROUP)
        self.res = RESIDUES_PER_STEP
        if dil == 16:
            nblk = BLOCK // 16
            self.grid = (16 // self.res, s // (L_BLOCK * nblk))
            self.shape = (nblk, 1, self.res, L_GROUP)
            self.index = lambda r, n: (n, r // (4 // self.res), r % (4 // self.res), 0, 0)
            pos = (np.arange(nblk)[:, None] * 16 + i[None, :]).reshape(-1)
        else:
            nblk = BLOCK // 64
            self.grid = (4 // self.res, s // (L_BLOCK * nblk))
            self.shape = (nblk, 4, self.res, L_GROUP)
            self.index = lambda r, n: (n, 0, r, 0, 0)
            pos = (np.arange(nblk)[:, None, None] * 64 + np.arange(4)[None, :, None]
                   + 4 * i[None, None, :]).reshape(-1)
        self.qn = pos.shape[0]
        self.nb = self.grid[1]
        dist = pos[:, None] - np.concatenate([pos - self.qn, pos])[None, :]
        band = (dist >= 0) & (dist <= BLOCK)
        start = band & (np.arange(2 * self.qn)[None, :] >= self.qn)
        self.bias = np.where(np.stack([band, start]), 0.0, -np.inf).astype(np.float32)

    def view(self, a):
        return a.reshape(a.shape[0] // L_BLOCK, 4, 4, L_GROUP, a.shape[1])

    def spec(self, w, step=lambda n: n):
        return pl.BlockSpec(self.shape + (w,), lambda r, n: self.index(r, step(n)))

    def bias_spec(self, step=lambda n: n):
        return pl.BlockSpec((1, self.qn, 2 * self.qn), lambda r, n: (jnp.where(step(n) == 0, 1, 0), 0, 0))

    def load(self, ref, cols=slice(None), j=0):
        x = ref[:, :, j, :, cols]
        return x.reshape(self.qn, x.shape[-1])

    def store(self, ref, cols, value, j=0):
        ref[:, :, j, :, cols] = value.reshape(self.shape[:2] + (L_GROUP, value.shape[-1]))


def _attn_fwd_branch(q, k, v, dil, earlier=()):
    s = q.shape[0]
    br = _Branch(dil, s)
    qn = br.qn
    nearly = len(earlier)

    def body(bias_ref, q_ref, kc_ref, kp_ref, vc_ref, vp_ref, *refs):
        early_refs, (o_ref, lse_ref) = refs[:2 * nearly], refs[2 * nearly:]
        bias2 = jnp.concatenate([bias_ref[0], bias_ref[0]], axis=0)
        lo = lax.broadcasted_iota(jnp.int32, (qn, LANES), 1) < HEAD_DIM
        mask_lo = lo.astype(F32).astype(MXU_DTYPE)
        for j, hp in itertools.product(range(br.res), range(HEADS_B // 2)):
            cols = slice(hp * LANES, (hp + 1) * LANES)
            qp = br.load(q_ref, cols, j)
            kcat = jnp.concatenate([br.load(kp_ref, cols, j), br.load(kc_ref, cols, j)], axis=0)
            vcat = jnp.concatenate([br.load(vp_ref, cols, j), br.load(vc_ref, cols, j)], axis=0)
            sc = _dot_nt(jnp.concatenate([qp * mask_lo, qp * (1 - mask_lo)], axis=0), kcat) + bias2
            m = jnp.max(sc, axis=1, keepdims=True)
            p = jnp.exp(sc - m)
            l = jnp.sum(p, axis=1, keepdims=True)
            out = _dot(p.astype(MXU_DTYPE), vcat) / l
            lse = m + jnp.log(l)
            outs = [br.load(r, cols, j) for r in early_refs[:nearly]] + [jnp.where(lo, out[:qn], out[qn:])]
            lses = [br.load(r, cols, j) for r in early_refs[nearly:]] + [jnp.where(lo, lse[:qn], lse[qn:])]
            if nearly:
                top = functools.reduce(jnp.maximum, lses)
                ws = [jnp.exp(x - top) for x in lses]
                den = functools.reduce(jnp.add, ws)
                outs = [functools.reduce(jnp.add, [w * o for w, o in zip(ws, outs)]) / den]
                lses = [top + jnp.log(den)]
            br.store(o_ref, cols, outs[0], j)
            br.store(lse_ref, cols, lses[0], j)

    before = lambda n: jnp.maximum(n - 1, 0)
    res = pl.pallas_call(
        body, name="attn_fwd_d%d" % dil, grid=br.grid,
        in_specs=[br.bias_spec(), br.spec(WIDTH_B), br.spec(WIDTH_B), br.spec(WIDTH_B, before), br.spec(WIDTH_B),
                  br.spec(WIDTH_B, before)] + [br.spec(WIDTH_B)] * (2 * nearly),
        out_specs=[br.spec(WIDTH_B), br.spec(WIDTH_B)],
        out_shape=[jax.ShapeDtypeStruct((s // L_BLOCK, 4, 4, L_GROUP, WIDTH_B), F32)] * 2,
        compiler_params=_params(("arbitrary", "arbitrary")),
    )(jnp.asarray(br.bias), br.view(q), br.view(k), br.view(k), br.view(v), br.view(v),
      *[br.view(o) for o, _ in earlier], *[br.view(x) for _, x in earlier])
    return tuple(a.reshape(s, WIDTH_B) for a in res)


LOCAL_CHUNKS = 4


def _local_bias():
    row = np.arange(BLOCK)
    pos = L_GROUP * (row % (L_GROUP // 2)) + row // (L_GROUP // 2)
    dist = pos[:, None] - np.concatenate([pos - BLOCK, pos])[None, :]
    band = (dist >= 0) & (dist <= BLOCK)
    start = band & (np.arange(2 * BLOCK)[None, :] >= BLOCK)
    return np.where(np.stack([band, start]), 0.0, -np.inf).astype(np.float32)


def _chunk_view(a):
    return a.reshape(a.shape[0] // L_BLOCK, L_GROUP, 2, L_GROUP // 2, a.shape[1])


def _chunk_of(ref, j, cols=slice(None)):
    x = ref[j // 2, :, j % 2, :, cols]
    return x.reshape(BLOCK, x.shape[-1])


def _put_chunk(ref, j, cols, value):
    ref[j // 2, :, j % 2, :, cols] = value.reshape(L_GROUP, L_GROUP // 2, value.shape[-1])


def _local_keys(cur_ref, before_ref, j, cols):
    here = slice(j * BLOCK, (j + 1) * BLOCK)
    before = before_ref[:, cols] if j == 0 else cur_ref[(j - 1) * BLOCK:j * BLOCK, cols]
    return jnp.concatenate([before, cur_ref[here, cols]], axis=0)


def _local_specs(s, step=lambda n: n):
    rows = LOCAL_CHUNKS * BLOCK
    cur = pl.BlockSpec((rows, WIDTH_B), lambda n: (step(n), 0))
    before = pl.BlockSpec((BLOCK, WIDTH_B), lambda n: (jnp.maximum(LOCAL_CHUNKS * step(n) - 1, 0), 0))
    return [cur, cur, before, cur, before]


def _attn_fwd_local(q1, k1, v1):
    s = q1.shape[0]
    rows = LOCAL_CHUNKS * BLOCK
    qn = BLOCK

    def body(bias_ref, q_ref, kc_ref, kp_ref, vc_ref, vp_ref, o_ref, lse_ref):
        first = jnp.where(pl.program_id(0) == 0, bias_ref[1], bias_ref[0])
        biases = [jnp.concatenate([b, b], axis=0) for b in (first, bias_ref[0])]
        lo = lax.broadcasted_iota(jnp.int32, (qn, LANES), 1) < HEAD_DIM
        mask_lo = lo.astype(F32).astype(MXU_DTYPE)
        for j, hp in itertools.product(range(LOCAL_CHUNKS), range(HEADS_B // 2)):
            cols = slice(hp * LANES, (hp + 1) * LANES)
            qp = q_ref[j * BLOCK:(j + 1) * BLOCK, cols]
            kcat = _local_keys(kc_ref, kp_ref, j, cols)
            vcat = _local_keys(vc_ref, vp_ref, j, cols)
            sc = _dot_nt(jnp.concatenate([qp * mask_lo, qp * (1 - mask_lo)], axis=0), kcat) + biases[min(j, 1)]
            m = jnp.max(sc, axis=1, keepdims=True)
            p = jnp.exp(sc - m)
            l = jnp.sum(p, axis=1, keepdims=True)
            out = _dot(p.astype(MXU_DTYPE), vcat) / l
            lse = m + jnp.log(l)
            _put_chunk(o_ref, j, cols, jnp.where(lo, out[:qn], out[qn:]))
            _put_chunk(lse_ref, j, cols, jnp.where(lo, lse[:qn], lse[qn:]))

    out_spec = pl.BlockSpec((LOCAL_CHUNKS // 2, L_GROUP, 2, L_GROUP // 2, WIDTH_B), lambda n: (n, 0, 0, 0, 0))
    res = pl.pallas_call(
        body, name="attn_fwd_d1", grid=(s // rows,),
        in_specs=[_full((2, BLOCK, 2 * BLOCK))] + _local_specs(s), out_specs=[out_spec] * 2,
        out_shape=[jax.ShapeDtypeStruct((s // L_BLOCK, L_GROUP, 2, L_GROUP // 2, WIDTH_B), F32)] * 2,
        compiler_params=_params(("arbitrary",)),
    )(jnp.asarray(_local_bias()), q1, k1, k1, v1, v1)
    return tuple(a.reshape(s, WIDTH_B) for a in res)


def _attn_bwd_local(q1, k1, v1, do, o, lse):
    s = q1.shape[0]
    rows = LOCAL_CHUNKS * BLOCK
    nsteps = s // rows
    qn = BLOCK

    def body(bias_ref, q_ref, kc_ref, kp_ref, vc_ref, vp_ref, do_ref, o_ref, lse_ref, dq_ref, dk_ref, dv_ref,
             dk_buf, dv_buf):
        n = pl.program_id(0)

        @pl.when(n == 0)
        def _():
            dk_buf[...] = jnp.zeros_like(dk_buf)
            dv_buf[...] = jnp.zeros_like(dv_buf)

        @pl.when(n < nsteps)
        def _():
            first = jnp.where(n == 0, bias_ref[1], bias_ref[0])
            biases = [jnp.concatenate([b, b], axis=0) for b in (first, bias_ref[0])]
            lo = lax.broadcasted_iota(jnp.int32, (qn, LANES), 1) < HEAD_DIM
            mask_f = lo.astype(F32)
            mask_lo = mask_f.astype(MXU_DTYPE)
            dk_buf[LOCAL_CHUNKS:] = jnp.zeros((LOCAL_CHUNKS, qn, WIDTH_B), F32)
            dv_buf[LOCAL_CHUNKS:] = jnp.zeros((LOCAL_CHUNKS, qn, WIDTH_B), F32)

            def prepare(j, hp):
                cols = slice(hp * LANES, (hp + 1) * LANES)
                qp = q_ref[j * BLOCK:(j + 1) * BLOCK, cols]
                dop = _chunk_of(do_ref, j, cols)
                prod = dop * _chunk_of(o_ref, j, cols)
                prod_lo = prod * mask_f
                lse = _chunk_of(lse_ref, j, cols)
                return dict(
                    j=j, cols=cols, kcat=_local_keys(kc_ref, kp_ref, j, cols), vcat=_local_keys(vc_ref, vp_ref, j, cols),
                    qs=jnp.concatenate([qp * mask_lo, qp * (1 - mask_lo)], axis=0),
                    dos=jnp.concatenate([dop * mask_f, dop * (1.0 - mask_f)], axis=0).astype(MXU_DTYPE),
                    delta=jnp.concatenate([jnp.sum(prod_lo, axis=1, keepdims=True),
                                           jnp.sum(prod - prod_lo, axis=1, keepdims=True)], axis=0),
                    lse2=jnp.concatenate([lse[:, :1], lse[:, HEAD_DIM:HEAD_DIM + 1]], axis=0))

            def scores(t):
                t['sc'] = _dot_nt(t['qs'], t['kcat'])
                t['dp'] = _dot_nt(t['dos'], t['vcat'])

            def softmax(t):
                p = jnp.exp(t['sc'] + biases[min(t['j'], 1)] - t['lse2'])
                t['ds'] = (p * (t['dp'] - t['delta'])).astype(MXU_DTYPE)
                t['p'] = p.astype(MXU_DTYPE)

            def gradients(t):
                t['dvc'] = _dot_tn(t['p'], t['dos'])
                t['dkc'] = _dot_tn(t['ds'], t['qs'])
                t['dq2'] = _dot(t['ds'], t['kcat'])

            def store(t):
                j, cols = t['j'], t['cols']
                _put_chunk(dq_ref, j, cols, jnp.where(lo, t['dq2'][:qn], t['dq2'][qn:]))
                for buf, both in ((dk_buf, t['dkc']), (dv_buf, t['dvc'])):
                    buf[LOCAL_CHUNKS + j - 1, :, cols] += both[:qn]
                    buf[LOCAL_CHUNKS + j, :, cols] += both[qn:]

            for j, first_pair in itertools.product(range(LOCAL_CHUNKS), range(0, HEADS_B // 2, PAIRS_ABREAST)):
                group = [prepare(j, hp) for hp in range(first_pair, first_pair + PAIRS_ABREAST)]
                for stage in (scores, softmax, gradients, store):
                    for t in group:
                        stage(t)

        for j in range(LOCAL_CHUNKS):
            _put_chunk(dk_ref, j, slice(None), dk_buf[j])
            _put_chunk(dv_ref, j, slice(None), dv_buf[j])
        dk_buf[:LOCAL_CHUNKS] = dk_buf[LOCAL_CHUNKS:]
        dv_buf[:LOCAL_CHUNKS] = dv_buf[LOCAL_CHUNKS:]

    cur = lambda n: jnp.minimum(n, nsteps - 1)
    late = lambda n: jnp.maximum(n - 1, 0)
    view_spec = lambda step: pl.BlockSpec((LOCAL_CHUNKS // 2, L_GROUP, 2, L_GROUP // 2, WIDTH_B),
                                          lambda n: (step(n), 0, 0, 0, 0))
    res = pl.pallas_call(
        body, name="attn_bwd_d1", grid=(nsteps + 1,),
        in_specs=[_full((2, BLOCK, 2 * BLOCK))] + _local_specs(s, cur) + [view_spec(cur)] * 3,
        out_specs=[view_spec(cur), view_spec(late), view_spec(late)],
        out_shape=[jax.ShapeDtypeStruct((s // L_BLOCK, L_GROUP, 2, L_GROUP // 2, WIDTH_B), F32)] * 3,
        scratch_shapes=[pltpu.VMEM((2 * LOCAL_CHUNKS, qn, WIDTH_B), F32)] * 2,
        compiler_params=_params(("arbitrary",)),
    )(jnp.asarray(_local_bias()), q1, k1, k1, v1, v1, _chunk_view(do), _chunk_view(o), _chunk_view(lse))
    return tuple(a.reshape(s, WIDTH_B) for a in res)


def _sgu_forward_tile(uv, w_ref, bias, g_sgu):
    tm = uv.shape[0]
    u = uv[:, :WIDTH_A]
    v = uv[:, WIDTH_A:]
    ug, tu = _gelu_parts(u)
    vg, tv = _gelu_parts(v)
    mu = jnp.mean(vg, axis=-1, keepdims=True)
    vc = vg - mu
    rs = lax.rsqrt(jnp.mean(vc * vc, axis=-1, keepdims=True) + EPS)
    vhat = vc * rs
    vn = (vhat * g_sgu).astype(MXU_DTYPE)
    masks = _half_masks(MXU_DTYPE)
    chunks = []
    for c in range(tm // CHUNK):
        rows = slice(c * CHUNK, (c + 1) * CHUNK)
        groups = []
        for gp in range(2):
            vn_g = vn[rows, gp * LANES:(gp + 1) * LANES]
            groups.append(_dot(w_ref[2 * gp], vn_g * masks[0]) + _dot(w_ref[2 * gp + 1], vn_g * masks[1]))
        chunks.append(jnp.concatenate(groups, axis=1) + bias)
    mixed = jnp.concatenate(chunks, axis=0)
    return dict(u=u, v=v, ug=ug, tu=tu, tv=tv, rs=rs, vhat=vhat, vn=vn, mixed=mixed, ya=ug * mixed)


def _sgu_fwd(uv, w_tril, bias, g_sgu, g_a):
    s = uv.shape[0]
    tm = 512

    def body(uv_ref, w_ref, b_ref, gs_ref, ga_ref, o_ref):
        t = _sgu_forward_tile(uv_ref[...], w_ref, b_ref[...], gs_ref[...])
        n, _ = _rms_stats(t['ya'])
        o_ref[...] = (n * ga_ref[...]).astype(MXU_DTYPE)

    return pl.pallas_call(
        body, name="sgu_fwd", grid=(s // tm,),
        in_specs=[_rows(tm, 2 * WIDTH_A), _full((HEADS_A, CHUNK, CHUNK)), _full((CHUNK, WIDTH_A)),
                  _full((1, WIDTH_A)), _full((1, WIDTH_A))],
        out_specs=_rows(tm, WIDTH_A), out_shape=jax.ShapeDtypeStruct((s, WIDTH_A), MXU_DTYPE),
        compiler_params=_params(("arbitrary",)),
    )(uv, w_tril, bias, g_sgu, g_a)


def _out_fwd(ya_n, y_b, g_b, w_out, x):
    s = x.shape[0]
    tm = 512
    nc = WIDTH_B // LANES

    def body(ya_ref, *refs):
        yb_refs = refs[:nc]
        g_ref, w_ref, x_ref, h_ref, yn_ref = refs[nc:]
        n, _ = _rms_stats(_load_l256(yb_refs, tm))
        yn = jnp.concatenate([ya_ref[...], (n * g_ref[...]).astype(MXU_DTYPE)], axis=1)
        yn_ref[...] = yn
        h_ref[...] = x_ref[...] + _dot(yn, w_ref[...])

    return pl.pallas_call(
        body, name="out_fwd", grid=(s // tm,),
        in_specs=[_rows(tm, WIDTH_A)] + _col_specs(tm, WIDTH_B) + [_full((1, WIDTH_B)), _full((D_MODEL, D_MODEL)),
                                                                 _rows(tm, D_MODEL)],
        out_specs=[_rows(tm, D_MODEL), _rows(tm, D_MODEL)],
        out_shape=[jax.ShapeDtypeStruct((s, D_MODEL), F32), jax.ShapeDtypeStruct((s, D_MODEL), MXU_DTYPE)],
        compiler_params=_params(("arbitrary",)),
    )(ya_n, *([y_b] * nc), g_b, w_out, x)


def _ffn_fwd(h1, g_ffn, w_gate_t, w_up_t, w_down):
    s = h1.shape[0]
    tm = 512

    def body(h_ref, g_ref, wgt_ref, wut_ref, wd_ref, o_ref, gate_ref, up_ref, hn_ref):
        h = h_ref[...]
        n, _ = _rms_stats(h)
        hn = (n * g_ref[...]).astype(MXU_DTYPE)
        hn_ref[...] = hn
        strips = [dict(cols=slice(lo, hi)) for lo, hi in FF_STRIPS]

        def project(t):
            t['gate'] = _dot_nt(hn, wgt_ref[t['cols'], :])
            t['up'] = _dot_nt(hn, wut_ref[t['cols'], :])

        def activate(t):
            gate, up = t['gate'], t['up']
            gate_ref[:, t['cols']] = gate.astype(MXU_DTYPE)
            up_ref[:, t['cols']] = up.astype(MXU_DTYPE)
            t['act'] = (gate * jax.nn.sigmoid(gate) * up).astype(MXU_DTYPE)

        def down(t):
            return _dot(t['act'], wd_ref[t['cols'], :])

        out = h
        project(strips[0])
        for i, t in enumerate(strips):
            if i + 1 < len(strips):
                project(strips[i + 1])
            activate(t)
            out = out + down(t)
        o_ref[...] = out

    return pl.pallas_call(
        body, name="ffn_fwd", grid=(s // tm,),
        in_specs=[_rows(tm, D_MODEL), _full((1, D_MODEL)), _full((D_FF, D_MODEL)), _full((D_FF, D_MODEL)),
                  _full((D_FF, D_MODEL))],
        out_specs=[_rows(tm, D_MODEL), _rows(tm, D_FF), _rows(tm, D_FF), _rows(tm, D_MODEL)],
        out_shape=[jax.ShapeDtypeStruct((s, D_MODEL), F32), jax.ShapeDtypeStruct((s, D_FF), MXU_DTYPE),
                   jax.ShapeDtypeStruct((s, D_FF), MXU_DTYPE), jax.ShapeDtypeStruct((s, D_MODEL), MXU_DTYPE)],
        compiler_params=_params(("arbitrary",)),
    )(h1, g_ffn, w_gate_t, w_up_t, w_down)


def _ple_loss(h2, p, target, g_ple, w_pg, w_pp_t, g_final):
    s = h2.shape[0]
    tm = 512

    def body(h_ref, p_ref, t_ref, gp_ref, wg_ref, wpt_ref, gf_ref,
             loss_ref, dh_ref, dz_ref, dpp_ref, hn_ref, dgp_ref, dgf_ref):
        @pl.when(pl.program_id(0) == 0)
        def _():
            loss_ref[...] = jnp.zeros_like(loss_ref)
            dgp_ref[...] = jnp.zeros_like(dgp_ref)
            dgf_ref[...] = jnp.zeros_like(dgf_ref)

        h2t = h_ref[...]
        n2, r2 = _rms_stats(h2t)
        hn = (n2 * gp_ref[...]).astype(MXU_DTYPE)
        hn_ref[...] = hn
        gate = jax.nn.sigmoid(_dot(hn, wg_ref[...]))
        pp = _dot_nt(p_ref[...].astype(MXU_DTYPE), wpt_ref[...])
        h3 = h2t + gate * pp
        n3, r3 = _rms_stats(h3)
        diff = n3 * gf_ref[...] - t_ref[...]
        loss_ref[...] += jnp.full(loss_ref.shape, 0.5 * jnp.sum(diff * diff) / D_MODEL, F32)
        dy = diff * (1.0 / D_MODEL)
        dgf_ref[...] += jnp.sum(dy * n3, axis=0, keepdims=True)
        dh3 = _rms_bwd(dy * gf_ref[...], n3, r3)
        dpp_ref[...] = (dh3 * gate).astype(MXU_DTYPE)
        dz = (dh3 * pp * gate * (1.0 - gate)).astype(MXU_DTYPE)
        dz_ref[...] = dz
        dhn = _dot_nt(dz, wg_ref[...])
        dgp_ref[...] += jnp.sum(dhn * n2, axis=0, keepdims=True)
        dh_ref[...] = dh3 + _rms_bwd(dhn * gp_ref[...], n2, r2)

    return pl.pallas_call(
        body, name="ple_loss", grid=(s // tm,),
        in_specs=[_rows(tm, D_MODEL), _rows(tm, PLE_DIM), _rows(tm, D_MODEL), _full((1, D_MODEL)),
                  _full((D_MODEL, D_MODEL)), _full((D_MODEL, PLE_DIM)), _full((1, D_MODEL))],
        out_specs=[_full((1, LANES)), _rows(tm, D_MODEL), _rows(tm, D_MODEL), _rows(tm, D_MODEL),
                   _rows(tm, D_MODEL), _full((1, D_MODEL)), _full((1, D_MODEL))],
        out_shape=[jax.ShapeDtypeStruct((1, LANES), F32), jax.ShapeDtypeStruct((s, D_MODEL), F32),
                   jax.ShapeDtypeStruct((s, D_MODEL), MXU_DTYPE), jax.ShapeDtypeStruct((s, D_MODEL), MXU_DTYPE),
                   jax.ShapeDtypeStruct((s, D_MODEL), MXU_DTYPE), jax.ShapeDtypeStruct((1, D_MODEL), F32),
                   jax.ShapeDtypeStruct((1, D_MODEL), F32)],
        compiler_params=_params(("arbitrary",)),
    )(h2, p, target, g_ple, w_pg, w_pp_t, g_final)


def _ffn_bwd(dh2, h1, gate, up, g_ffn, w_down, w_gate_t, w_up_t):
    s = h1.shape[0]
    tm = 256

    def body(dh_ref, h_ref, gate_ref, up_ref, g_ref, wd_ref, wgt_ref, wut_ref,
             o_ref, act_ref, dg_ref, du_ref, dgn_ref):
        @pl.when(pl.program_id(0) == 0)
        def _():
            dgn_ref[...] = jnp.zeros_like(dgn_ref)

        dh = dh_ref[...]
        dhb = dh.astype(MXU_DTYPE)
        strips = [dict(cols=slice(lo, hi)) for lo, hi in FF_STRIPS]

        def back_down(t):
            t['dact'] = _dot_nt(dhb, wd_ref[t['cols'], :])

        def back_act(t):
            cols, dact = t['cols'], t['dact']
            g = gate_ref[:, cols].astype(F32)
            u = up_ref[:, cols].astype(F32)
            sg = jax.nn.sigmoid(g)
            silu = g * sg
            act_ref[:, cols] = (silu * u).astype(MXU_DTYPE)
            t['du'] = (dact * silu).astype(MXU_DTYPE)
            t['dg'] = (dact * u * sg * (1.0 + g * (1.0 - sg))).astype(MXU_DTYPE)
            du_ref[:, cols] = t['du']
            dg_ref[:, cols] = t['dg']

        def back_in(t):
            return _dot(t['dg'], wgt_ref[t['cols'], :]) + _dot(t['du'], wut_ref[t['cols'], :])

        dhn = jnp.zeros((tm, D_MODEL), F32)
        back_down(strips[0])
        for i, t in enumerate(strips):
            if i + 1 < len(strips):
                back_down(strips[i + 1])
            back_act(t)
            dhn = dhn + back_in(t)
        n, r = _rms_stats(h_ref[...])
        dgn_ref[...] += jnp.sum(dhn * n, axis=0, keepdims=True)
        o_ref[...] = dh + _rms_bwd(dhn * g_ref[...], n, r)

    return pl.pallas_call(
        body, name="ffn_bwd", grid=(s // tm,),
        in_specs=[_rows(tm, D_MODEL), _rows(tm, D_MODEL), _rows(tm, D_FF), _rows(tm, D_FF), _full((1, D_MODEL)),
                  _full((D_FF, D_MODEL)), _full((D_FF, D_MODEL)), _full((D_FF, D_MODEL))],
        out_specs=[_rows(tm, D_MODEL), _rows(tm, D_FF), _rows(tm, D_FF), _rows(tm, D_FF), _full((1, D_MODEL))],
        out_shape=[jax.ShapeDtypeStruct((s, D_MODEL), F32), jax.ShapeDtypeStruct((s, D_FF), MXU_DTYPE),
                   jax.ShapeDtypeStruct((s, D_FF), MXU_DTYPE), jax.ShapeDtypeStruct((s, D_FF), MXU_DTYPE),
                   jax.ShapeDtypeStruct((1, D_MODEL), F32)],
        compiler_params=_params(("arbitrary",)),
    )(dh2, h1, gate, up, g_ffn, w_down, w_gate_t, w_up_t)


def _out_bwd(dh1, y_b, g_b, w_out):
    s = dh1.shape[0]
    tm = 512
    nc = WIDTH_B // LANES

    def body(dh_ref, *refs):
        yb_refs = refs[:nc]
        g_ref, w_ref, dya_ref, dyb_ref, dg_ref, scr = refs[nc:]

        @pl.when(pl.program_id(0) == 0)
        def _():
            dg_ref[...] = jnp.zeros_like(dg_ref)

        dy = _dot_nt(dh_ref[...].astype(MXU_DTYPE), w_ref[...])
        dya_ref[...] = dy[:, :WIDTH_A]
        dyb = dy[:, WIDTH_A:]
        n, r = _rms_stats(_load_l256(yb_refs, tm))
        dg_ref[...] += jnp.sum(dyb * n, axis=0, keepdims=True)
        dyb_in = _rms_bwd(dyb * g_ref[...], n, r)
        for j in range(nc):
            cols = slice(j * LANES, (j + 1) * LANES)
            _store_l256(scr, dyb_ref, cols, dyb_in[:, cols])

    return pl.pallas_call(
        body, name="out_bwd", grid=(s // tm,), scratch_shapes=[pltpu.VMEM((tm, LANES), F32)],
        in_specs=[_rows(tm, D_MODEL)] + _col_specs(tm, WIDTH_B) + [_full((1, WIDTH_B)), _full((D_MODEL, D_MODEL))],
        out_specs=[_rows(tm, WIDTH_A), _rows(tm, WIDTH_B), _full((1, WIDTH_B))],
        out_shape=[jax.ShapeDtypeStruct((s, WIDTH_A), F32), jax.ShapeDtypeStruct((s, WIDTH_B), F32),
                   jax.ShapeDtypeStruct((1, WIDTH_B), F32)],
        compiler_params=_params(("arbitrary",)),
    )(dh1, *([y_b] * nc), g_b, w_out)


def _attn_bwd_branch(q, k, v, do, o, lse, grads, dil):
    s = q.shape[0]
    br = _Branch(dil, s)
    qn, nb = br.qn, br.nb
    first = grads is None

    def body(*refs):
        bias_ref, q_ref, kc_ref, kp_ref, vc_ref, vp_ref, do_ref, o_ref, lse_ref = refs[:9]
        if first:
            rest = refs[9:]
        else:
            dq_in, dk_in, dv_in = refs[9:12]
            rest = refs[12:]
        dq_ref, dk_ref, dv_ref, dk_carry, dv_carry = rest
        n = pl.program_id(1)

        @pl.when(n == 0)
        def _():
            dk_carry[...] = jnp.zeros_like(dk_carry)
            dv_carry[...] = jnp.zeros_like(dv_carry)

        @pl.when(n < nb)
        def _():
            bias2 = jnp.concatenate([bias_ref[0], bias_ref[0]], axis=0)
            lane = lax.broadcasted_iota(jnp.int32, (qn, LANES), 1)
            lo = lane < HEAD_DIM
            mask_f = lo.astype(F32)
            mask_lo = mask_f.astype(MXU_DTYPE)
            def prepare(j, hp):
                cols = slice(hp * LANES, (hp + 1) * LANES)
                qp = br.load(q_ref, cols, j)
                dop = br.load(do_ref, cols, j)
                prod = dop * br.load(o_ref, cols, j)
                prod_lo = prod * mask_f
                lse = br.load(lse_ref, cols, j)
                return dict(
                    j=j, cols=cols,
                    kcat=jnp.concatenate([br.load(kp_ref, cols, j), br.load(kc_ref, cols, j)], axis=0),
                    vcat=jnp.concatenate([br.load(vp_ref, cols, j), br.load(vc_ref, cols, j)], axis=0),
                    qs=jnp.concatenate([qp * mask_lo, qp * (1 - mask_lo)], axis=0),
                    dos=jnp.concatenate([dop * mask_f, dop * (1.0 - mask_f)], axis=0).astype(MXU_DTYPE),
                    delta=jnp.concatenate([jnp.sum(prod_lo, axis=1, keepdims=True),
                                           jnp.sum(prod - prod_lo, axis=1, keepdims=True)], axis=0),
                    lse2=jnp.concatenate([lse[:, :1], lse[:, HEAD_DIM:HEAD_DIM + 1]], axis=0))

            def scores(t):
                t['sc'] = _dot_nt(t['qs'], t['kcat'])
                t['dp'] = _dot_nt(t['dos'], t['vcat'])

            def softmax(t):
                p = jnp.exp(t['sc'] + bias2 - t['lse2'])
                t['ds'] = (p * (t['dp'] - t['delta'])).astype(MXU_DTYPE)
                t['p'] = p.astype(MXU_DTYPE)

            def gradients(t):
                t['dvc'] = _dot_tn(t['p'], t['dos'])
                t['dkc'] = _dot_tn(t['ds'], t['qs'])
                t['dq2'] = _dot(t['ds'], t['kcat'])

            def store(t):
                j, cols, dkc, dvc = t['j'], t['cols'], t['dkc'], t['dvc']
                dq = jnp.where(lo, t['dq2'][:qn], t['dq2'][qn:])
                dk_prev = dk_carry[j, :, cols] + dkc[:qn]
                dv_prev = dv_carry[j, :, cols] + dvc[:qn]
                if not first:
                    dq = dq + br.load(dq_in, cols, j)
                    dk_prev = dk_prev + br.load(dk_in, cols, j)
                    dv_prev = dv_prev + br.load(dv_in, cols, j)
                br.store(dq_ref, cols, dq, j)
                br.store(dk_ref, cols, dk_prev, j)
                br.store(dv_ref, cols, dv_prev, j)
                dk_carry[j, :, cols] = dkc[qn:]
                dv_carry[j, :, cols] = dvc[qn:]

            for j, first_pair in itertools.product(range(br.res), range(0, HEADS_B // 2, PAIRS_ABREAST)):
                group = [prepare(j, hp) for hp in range(first_pair, first_pair + PAIRS_ABREAST)]
                for stage in (scores, softmax, gradients, store):
                    for t in group:
                        stage(t)

        @pl.when(n == nb)
        def _():
            for j in range(br.res):
                dk_last = dk_carry[j]
                dv_last = dv_carry[j]
                if not first:
                    dk_last = dk_last + br.load(dk_in, slice(None), j)
                    dv_last = dv_last + br.load(dv_in, slice(None), j)
                br.store(dk_ref, slice(None), dk_last, j)
                br.store(dv_ref, slice(None), dv_last, j)

    cur = lambda n: jnp.minimum(n, nb - 1)
    before = lambda n: jnp.maximum(cur(n) - 1, 0)
    late = lambda n: jnp.maximum(n - 1, 0)
    in_specs = [br.bias_spec(cur), br.spec(WIDTH_B, cur), br.spec(WIDTH_B, cur), br.spec(WIDTH_B, before),
                br.spec(WIDTH_B, cur), br.spec(WIDTH_B, before), br.spec(WIDTH_B, cur), br.spec(WIDTH_B, cur),
                br.spec(WIDTH_B, cur)]
    args = [jnp.asarray(br.bias)] + [br.view(a) for a in (q, k, k, v, v, do, o, lse)]
    if not first:
        in_specs += [br.spec(WIDTH_B, cur), br.spec(WIDTH_B, late), br.spec(WIDTH_B, late)]
        args += [br.view(g) for g in grads]
    res = pl.pallas_call(
        body, name="attn_bwd_d%d" % dil, grid=(br.grid[0], nb + 1), in_specs=in_specs,
        out_specs=[br.spec(WIDTH_B, cur), br.spec(WIDTH_B, late), br.spec(WIDTH_B, late)],
        out_shape=[jax.ShapeDtypeStruct((s // L_BLOCK, 4, 4, L_GROUP, WIDTH_B), F32)] * 3,
        scratch_shapes=[pltpu.VMEM((br.res, qn, WIDTH_B), F32), pltpu.VMEM((br.res, qn, WIDTH_B), F32)],
        compiler_params=_params(("arbitrary", "arbitrary")),
    )(*args)
    return tuple(a.reshape(s, WIDTH_B) for a in res)


def _sgu_bwd(uv, dya_n, w_tril, w_tril_t, bias, g_sgu, g_a):
    s = uv.shape[0]
    tm = 512

    def body(uv_ref, dy_ref, w_ref, wt_ref, b_ref, gs_ref, ga_ref, duv_ref, dw_ref, db_ref, dgs_ref, dga_ref,
             db_acc):
        i = pl.program_id(0)

        @pl.when(i == 0)
        def _():
            dw_ref[...] = jnp.zeros_like(dw_ref)
            dgs_ref[...] = jnp.zeros_like(dgs_ref)
            dga_ref[...] = jnp.zeros_like(dga_ref)
            db_acc[...] = jnp.zeros_like(db_acc)

        t = _sgu_forward_tile(uv_ref[...], w_ref, b_ref[...], gs_ref[...])
        na, ra = _rms_stats(t['ya'])
        dyn = dy_ref[...]
        dga_ref[...] += jnp.sum(dyn * na, axis=0, keepdims=True)
        dya = _rms_bwd(dyn * ga_ref[...], na, ra)
        dug = dya * t['mixed']
        dmixed = dya * t['ug']
        dmb = dmixed.astype(MXU_DTYPE)
        masks = _half_masks(MXU_DTYPE)
        chunks = []
        db = jnp.zeros((CHUNK, WIDTH_A), F32)
        for c in range(tm // CHUNK):
            rows = slice(c * CHUNK, (c + 1) * CHUNK)
            db = db + dmixed[rows]
            groups = []
            for gp in range(2):
                cols = slice(gp * LANES, (gp + 1) * LANES)
                dm_g = dmb[rows, cols]
                vn_g = t['vn'][rows, cols]
                dvn_g = jnp.zeros((CHUNK, LANES), F32)
                for j in range(2):
                    dm_h = dm_g * masks[j]
                    dvn_g = dvn_g + _dot(wt_ref[2 * gp + j], dm_h)
                    dw_ref[2 * gp + j] += _dot_nt(dm_h, vn_g)
                groups.append(dvn_g)
            chunks.append(jnp.concatenate(groups, axis=1))
        db_acc[...] += db
        dvn = jnp.concatenate(chunks, axis=0)
        vhat = t['vhat']
        dgs_ref[...] += jnp.sum(dvn * vhat, axis=0, keepdims=True)
        dvh = dvn * gs_ref[...]
        dvg = t['rs'] * (dvh - jnp.mean(dvh, axis=-1, keepdims=True)
                         - vhat * jnp.mean(dvh * vhat, axis=-1, keepdims=True))
        duv_ref[:, :WIDTH_A] = (dug * _gelu_grad(t['u'], t['tu'])).astype(MXU_DTYPE)
        duv_ref[:, WIDTH_A:] = (dvg * _gelu_grad(t['v'], t['tv'])).astype(MXU_DTYPE)

        @pl.when(i == pl.num_programs(0) - 1)
        def _():
            lane_a = lax.broadcasted_iota(jnp.int32, (CHUNK, WIDTH_A), 1)
            lane = lax.broadcasted_iota(jnp.int32, (CHUNK, LANES), 1)
            acc = db_acc[...]
            out = jnp.zeros((CHUNK, LANES), F32)
            for h in range(HEADS_A):
                col = jnp.sum(jnp.where(lane_a // HEAD_DIM == h, acc, 0.0), axis=1, keepdims=True)
                out = jnp.where(lane == h, col, out)
            db_ref[...] = out
            causal = (lax.broadcasted_iota(jnp.int32, (CHUNK, CHUNK), 0)
                      >= lax.broadcasted_iota(jnp.int32, (CHUNK, CHUNK), 1))
            for h in range(HEADS_A):
                dw_ref[h] = jnp.where(causal, dw_ref[h], 0.0)

    return pl.pallas_call(
        body, name="sgu_bwd", grid=(s // tm,),
        in_specs=[_rows(tm, 2 * WIDTH_A), _rows(tm, WIDTH_A), _full((HEADS_A, CHUNK, CHUNK)),
                  _full((HEADS_A, CHUNK, CHUNK)), _full((CHUNK, WIDTH_A)), _full((1, WIDTH_A)),
                  _full((1, WIDTH_A))],
        out_specs=[_rows(tm, 2 * WIDTH_A), _full((HEADS_A, CHUNK, CHUNK)), _full((CHUNK, LANES)),
                   _full((1, WIDTH_A)), _full((1, WIDTH_A))],
        out_shape=[jax.ShapeDtypeStruct((s, 2 * WIDTH_A), MXU_DTYPE),
                   jax.ShapeDtypeStruct((HEADS_A, CHUNK, CHUNK), F32), jax.ShapeDtypeStruct((CHUNK, LANES), F32),
                   jax.ShapeDtypeStruct((1, WIDTH_A), F32), jax.ShapeDtypeStruct((1, WIDTH_A), F32)],
        scratch_shapes=[pltpu.VMEM((CHUNK, WIDTH_A), F32)],
        compiler_params=_params(("arbitrary",)),
    )(uv, dya_n, w_tril, w_tril_t, bias, g_sgu, g_a)


def _in_bwd_proj(duv, dq, dk, dv, cos_t, sin_t):
    s = duv.shape[0]
    tm = 512
    nc = WIDTH_B // LANES

    def body(duv_ref, *refs):
        dq_refs, dk_refs, dv_refs = refs[:nc], refs[nc:2 * nc], refs[2 * nc:3 * nc]
        cos_ref, sin_ref, dp_ref = refs[3 * nc:]
        cos = cos_ref[...]
        sin = sin_ref[...]
        dp_ref[:, :2 * WIDTH_A] = duv_ref[...]
        for i in range(nc):
            lo = 2 * WIDTH_A + i * LANES
            tq = _load_l256(dq_refs[i:i + 1], tm) * (HEAD_DIM ** -0.5)
            tk = _load_l256(dk_refs[i:i + 1], tm)
            dp_ref[:, lo:lo + LANES] = (tq * cos + _rope_partner(tq * sin)).astype(MXU_DTYPE)
            dp_ref[:, lo + WIDTH_B:lo + WIDTH_B + LANES] = (tk * cos + _rope_partner(tk * sin)).astype(MXU_DTYPE)
            dp_ref[:, lo + 2 * WIDTH_B:lo + 2 * WIDTH_B + LANES] = _load_l256(dv_refs[i:i + 1], tm).astype(MXU_DTYPE)

    return pl.pallas_call(
        body, name="in_bwd_proj", grid=(s // tm,),
        in_specs=[_rows(tm, 2 * WIDTH_A)] + 3 * _col_specs(tm, WIDTH_B) + [_rows(tm, LANES), _rows(tm, LANES)],
        out_specs=_rows(tm, IN_COLS), out_shape=jax.ShapeDtypeStruct((s, IN_COLS), MXU_DTYPE),
        compiler_params=_params(("arbitrary",)),
    )(duv, *([dq] * nc), *([dk] * nc), *([dv] * nc), cos_t, sin_t)


def _in_bwd_x(dproj, w_in_t, x, g_mix, dh1):
    s = x.shape[0]
    tm = 512

    def body(dp_ref, wt_ref, x_ref, g_ref, dh_ref, gx_ref, dg_ref):
        @pl.when(pl.program_id(0) == 0)
        def _():
            dg_ref[...] = jnp.zeros_like(dg_ref)

        dhn = _dot(dp_ref[...], wt_ref[...])
        n, r = _rms_stats(x_ref[...])
        dg_ref[...] += jnp.sum(dhn * n, axis=0, keepdims=True)
        gx_ref[...] = dh_ref[...] + _rms_bwd(dhn * g_ref[...], n, r)

    return pl.pallas_call(
        body, name="in_bwd_x", grid=(s // tm,),
        in_specs=[_rows(tm, IN_COLS), _full((IN_COLS, D_MODEL)), _rows(tm, D_MODEL), _full((1, D_MODEL)),
                  _rows(tm, D_MODEL)],
        out_specs=[_rows(tm, D_MODEL), _full((1, D_MODEL))],
        out_shape=[jax.ShapeDtypeStruct((s, D_MODEL), F32), jax.ShapeDtypeStruct((1, D_MODEL), F32)],
        compiler_params=_params(("arbitrary",)),
    )(dproj, w_in_t, x, g_mix, dh1)


def _wgrad(a, b, name):
    s, m = a.shape
    n = b.shape[1]
    bm = min(m, FF_HALF)
    ts = 1024
    nsteps = s // ts

    def body(a_ref, b_ref, o_ref, acc):
        kk = pl.program_id(1)

        @pl.when(kk == 0)
        def _():
            acc[...] = jnp.zeros_like(acc)

        acc[...] += _dot_tn(a_ref[...].astype(MXU_DTYPE), b_ref[...].astype(MXU_DTYPE))

        @pl.when(kk == nsteps - 1)
        def _():
            o_ref[...] = acc[...].astype(o_ref.dtype)

    return pl.pallas_call(
        body, name=name, grid=(m // bm, nsteps),
        in_specs=[pl.BlockSpec((ts, bm), lambda i, kk: (kk, i)), pl.BlockSpec((ts, n), lambda i, kk: (kk, 0))],
        out_specs=pl.BlockSpec((bm, n), lambda i, kk: (i, 0)), out_shape=jax.ShapeDtypeStruct((m, n), jnp.bfloat16),
        scratch_shapes=[pltpu.VMEM((bm, n), F32)],
        compiler_params=_params(("arbitrary", "arbitrary")),
    )(a, b)


def _rope_tables(s):
    half = HEAD_DIM // 2
    inv = ROPE_THETA ** (-jnp.arange(half, dtype=F32) / half)
    ang = jnp.arange(s, dtype=F32)[:, None] * jnp.tile(inv, LANES // half)[None, :]
    sign = jnp.tile(jnp.concatenate([-jnp.ones(half, F32), jnp.ones(half, F32)]), LANES // HEAD_DIM)
    return jnp.cos(ang), jnp.sin(ang) * sign[None, :]


MESH = pl.DeviceIdType.MESH
ANY = pl.BlockSpec(memory_space=pl.ANY)
SEM = pl.BlockSpec(memory_space=pltpu.SEMAPHORE)
SPLIT_COPY = pltpu.CompilerParams(has_side_effects=pltpu.SideEffectType.DATAFLOW_SIDE_EFFECTING)
SLAB_IS_TRANSPOSED = {'w_in': True, 'w_out': False, 'w_gate': True, 'w_up': True, 'w_down': False,
                      'w_ple_gate': False, 'w_ple_proj': True}


def _place():
    x, y, c = lax.axis_index("x"), lax.axis_index("y"), lax.axis_index("c")
    other_chips = [(1 - x, y), (x, 1 - y), (1 - x, 1 - y)]
    return x, y, c, other_chips


def _chip_of(chip):
    return 2 * chip[0] + chip[1]


def _half(ref, lead, hc):
    hr = ref.shape[1] // 2
    return ref.at[lead, pl.ds(hc * hr, hr), :]


def _stack_with_own(slab):
    me = 2 * lax.axis_index("x") + lax.axis_index("y")
    stack = lax.empty((N_CHIPS,) + slab.shape, slab.dtype)
    return lax.dynamic_update_slice(stack, slab[None], (me, 0, 0))


def _first_hops(land_ref, send_sems, recv_sems):
    x, y, c, chips = _place()
    mine = _half(land_ref, 2 * x + y, c)
    sends = [pltpu.make_async_remote_copy(src_ref=mine, dst_ref=mine, send_sem=send_sems.at[j], recv_sem=recv_sems.at[j],
                                          device_id=(*chip, c), device_id_type=MESH) for j, chip in enumerate(chips)]
    recvs = [pltpu.make_async_remote_copy(src_ref=mine, dst_ref=_half(land_ref, _chip_of(chip), c),
                                          send_sem=send_sems.at[j], recv_sem=recv_sems.at[j], device_id=(*chip, c),
                                          device_id_type=MESH) for j, chip in enumerate(chips)]
    return sends, recvs


def _second_hops(land_ref, send_sems, recv_sems):
    x, y, c, chips = _place()
    sibling = (x, y, 1 - c)
    sends = [pltpu.make_async_remote_copy(src_ref=_half(land_ref, _chip_of(chip), c),
                                          dst_ref=_half(land_ref, _chip_of(chip), c), send_sem=send_sems.at[j],
                                          recv_sem=recv_sems.at[j], device_id=sibling, device_id_type=MESH)
             for j, chip in enumerate(chips)]
    recvs = [pltpu.make_async_remote_copy(src_ref=_half(land_ref, _chip_of(chip), c),
                                          dst_ref=_half(land_ref, _chip_of(chip), 1 - c), send_sem=send_sems.at[j],
                                          recv_sem=recv_sems.at[j], device_id=sibling, device_id_type=MESH)
             for j, chip in enumerate(chips)]
    return sends, recvs


def _two_level_start(stack):
    def body(land_ref, send_sems, recv_sems, land_thru, token):
        for cp in _first_hops(land_ref, send_sems, recv_sems)[0]:
            cp.start()
        token[...] = jnp.zeros_like(token)

    res = pl.pallas_call(
        body, name="two_level_start",
        out_shape=(pltpu.SemaphoreType.DMA((3,)), pltpu.SemaphoreType.DMA((3,)), pltpu.HBM(stack.shape, stack.dtype),
                   jax.ShapeDtypeStruct((8, LANES), F32)),
        in_specs=[ANY], out_specs=(SEM, SEM, ANY, pl.BlockSpec(memory_space=pltpu.VMEM)),
        input_output_aliases={0: 2}, compiler_params=SPLIT_COPY,
    )(pltpu.with_memory_space_constraint(stack, pltpu.HBM))
    return res[:-1], res[-1]


def _two_level_forward(handle, after):
    send_sems, recv_sems, land = handle

    def body(land_ref, send_sems, recv_sems, *refs):
        send2, recv2 = refs[len(after):len(after) + 2]
        sends, recvs = _first_hops(land_ref, send_sems, recv_sems)
        for cp in sends:
            cp.wait_send()
        for cp in recvs:
            cp.wait_recv()
        for cp in _second_hops(land_ref, send2, recv2)[0]:
            cp.start()

    return pl.pallas_call(
        body, name="two_level_forward",
        out_shape=(pltpu.SemaphoreType.DMA((3,)), pltpu.SemaphoreType.DMA((3,)), pltpu.HBM(land.shape, land.dtype)),
        in_specs=[ANY, SEM, SEM] + [ANY] * len(after), out_specs=(SEM, SEM, ANY),
        input_output_aliases={0: 2}, compiler_params=SPLIT_COPY,
    )(land, send_sems, recv_sems, *after)


def _two_level_wait(handle):
    send_sems, recv_sems, land = handle

    def body(land_ref, send_sems, recv_sems, land_thru):
        sends, recvs = _second_hops(land_ref, send_sems, recv_sems)
        for cp in sends:
            cp.wait_send()
        for cp in recvs:
            cp.wait_recv()

    land = pl.pallas_call(
        body, name="two_level_wait", out_shape=pltpu.HBM(land.shape, land.dtype), in_specs=[ANY, SEM, SEM],
        out_specs=ANY, input_output_aliases={0: 0}, compiler_params=SPLIT_COPY,
    )(land, send_sems, recv_sems)
    return land.reshape(-1, land.shape[-1])


def _gather_copies(land_refs, send_sems, recv_sems):
    x, y, c, chips = _place()
    sends, recvs = [], []
    for k, land in enumerate(land_refs):
        mine = _half(land, 2 * x + y, c)
        for j, chip in enumerate(chips):
            for t in range(2):
                sends.append(pltpu.make_async_remote_copy(
                    src_ref=mine, dst_ref=mine, send_sem=send_sems.at[6 * k + 2 * j + t],
                    recv_sem=recv_sems.at[6 * k + 2 * j + c], device_id=(*chip, t), device_id_type=MESH))
                recvs.append(pltpu.make_async_remote_copy(
                    src_ref=mine, dst_ref=_half(land, _chip_of(chip), t), send_sem=send_sems.at[6 * k + 2 * j + t],
                    recv_sem=recv_sems.at[6 * k + 2 * j + t], device_id=(*chip, t), device_id_type=MESH))
    return sends, recvs


def _all_gather_start(stacks, after):
    n = len(stacks)

    def body(*refs):
        send_sems, recv_sems = refs[n + 1:n + 3]
        token = refs[-1]
        sends, _ = _gather_copies(refs[:n], send_sems, recv_sems)
        for cp in sends:
            cp.start()
        token[...] = jnp.zeros_like(token)

    hbm = lambda a: pltpu.HBM(a.shape, a.dtype)
    res = pl.pallas_call(
        body, name="all_gather_start",
        out_shape=(pltpu.SemaphoreType.DMA((6 * n,)), pltpu.SemaphoreType.DMA((6 * n,)), *map(hbm, stacks),
                   jax.ShapeDtypeStruct((8, LANES), F32)),
        in_specs=[ANY] * (n + 1), out_specs=(SEM, SEM, *([ANY] * n), pl.BlockSpec(memory_space=pltpu.VMEM)),
        input_output_aliases={i: 2 + i for i in range(n)}, compiler_params=SPLIT_COPY,
    )(*[pltpu.with_memory_space_constraint(a, pltpu.HBM) for a in stacks], after)
    return res[:-1], res[-1]


def _all_gather_wait(handle, after):
    send_sems, recv_sems = handle[:2]
    lands = handle[2:]
    n = len(lands)

    def body(*refs):
        send_sems, recv_sems = refs[n:n + 2]
        sends, recvs = _gather_copies(refs[:n], send_sems, recv_sems)
        for cp in sends:
            cp.wait_send()
        for cp in recvs:
            cp.wait_recv()

    hbm = lambda a: pltpu.HBM(a.shape, a.dtype)
    res = pl.pallas_call(
        body, name="all_gather_wait", out_shape=tuple(map(hbm, lands)),
        in_specs=[ANY] * n + [SEM, SEM, ANY], out_specs=tuple([ANY] * n),
        input_output_aliases={i: i for i in range(n)}, compiler_params=SPLIT_COPY,
    )(*lands, send_sems, recv_sems, after)
    return [land.reshape(-1, land.shape[-1]) for land in res]


def _scatter_copies(part_refs, land_refs, send_sems, recv_sems):
    x, y, c, chips = _place()
    me = 4 * x + 2 * y + c
    sends, recvs = [], []
    for k, (part, land) in enumerate(zip(part_refs, land_refs)):
        for j, chip in enumerate(chips):
            for t in range(2):
                sends.append(pltpu.make_async_remote_copy(
                    src_ref=part.at[_chip_of(chip)], dst_ref=land.at[me],
                    send_sem=send_sems.at[7 * k + 2 * j + t], recv_sem=recv_sems.at[7 * k + 2 * j + c],
                    device_id=(*chip, t), device_id_type=MESH))
                recvs.append(pltpu.make_async_remote_copy(
                    src_ref=part.at[_chip_of(chip)], dst_ref=land.at[2 * _chip_of(chip) + t],
                    send_sem=send_sems.at[7 * k + 2 * j + t], recv_sem=recv_sems.at[7 * k + 2 * j + t],
                    device_id=(*chip, t), device_id_type=MESH))
        sends.append(pltpu.make_async_remote_copy(
            src_ref=part.at[2 * x + y], dst_ref=land.at[me], send_sem=send_sems.at[7 * k + 6],
            recv_sem=recv_sems.at[7 * k + 6], device_id=(x, y, 1 - c), device_id_type=MESH))
        recvs.append(pltpu.make_async_remote_copy(
            src_ref=part.at[2 * x + y], dst_ref=land.at[4 * x + 2 * y + 1 - c],
            send_sem=send_sems.at[7 * k + 6], recv_sem=recv_sems.at[7 * k + 6], device_id=(x, y, 1 - c),
            device_id_type=MESH))
    return sends, recvs


def _reduce_scatter_start(parts, name):
    n = len(parts)
    parts = [p.reshape(N_CHIPS, p.shape[0] // N_CHIPS, p.shape[1]) for p in parts]

    def body(*refs):
        part_refs, land_refs = refs[:n], refs[n:2 * n]
        send_sems, recv_sems = refs[2 * n:2 * n + 2]
        token = refs[-1]
        sends, _ = _scatter_copies(part_refs, land_refs, send_sems, recv_sems)
        for cp in sends:
            cp.start()
        token[...] = jnp.zeros_like(token)

    lands = [lax.empty((N_DEV, p.shape[1], p.shape[2]), p.dtype) for p in parts]
    hbm = lambda a: pltpu.HBM(a.shape, a.dtype)
    res = pl.pallas_call(
        body, name=name,
        out_shape=(pltpu.SemaphoreType.DMA((7 * n,)), pltpu.SemaphoreType.DMA((7 * n,)), *map(hbm, parts),
                   *map(hbm, lands), jax.ShapeDtypeStruct((8, LANES), F32)),
        in_specs=[ANY] * (2 * n), out_specs=(SEM, SEM, *([ANY] * (2 * n)), pl.BlockSpec(memory_space=pltpu.VMEM)),
        input_output_aliases={i: 2 + i for i in range(2 * n)}, compiler_params=SPLIT_COPY,
    )(*[pltpu.with_memory_space_constraint(a, pltpu.HBM) for a in parts + lands])
    return res[:-1], res[-1]


def _reduce_scatter_wait(handle, after, name):
    after = list(after) if isinstance(after, (list, tuple)) else [after]
    send_sems, recv_sems = handle[:2]
    n = (len(handle) - 2) // 2
    parts, lands = handle[2:2 + n], handle[2 + n:]

    def body(*refs):
        part_refs, land_refs = refs[:n], refs[n:2 * n]
        send_sems, recv_sems = refs[2 * n:2 * n + 2]
        sends, recvs = _scatter_copies(part_refs, land_refs, send_sems, recv_sems)
        for cp in sends:
            cp.wait_send()
        for cp in recvs:
            cp.wait_recv()

    hbm = lambda a: pltpu.HBM(a.shape, a.dtype)
    res = pl.pallas_call(
        body, name=name, out_shape=tuple(map(hbm, list(parts) + list(lands))),
        in_specs=[ANY] * (2 * n) + [SEM, SEM] + [ANY] * len(after), out_specs=tuple([ANY] * (2 * n)),
        input_output_aliases={i: i for i in range(2 * n)}, compiler_params=SPLIT_COPY,
    )(*parts, *lands, send_sems, recv_sems, *after)
    return list(zip(res[:n], res[n:]))


def _adamw_of_shares(w, own, land, m, v, name):
    rows, cols = w.shape
    tm = rows // 4 if rows % 32 == 0 else rows
    x, y, c = lax.axis_index("x"), lax.axis_index("y"), lax.axis_index("c")
    where = jnp.stack([2 * x + y, 4 * x + 2 * y + c]).astype(jnp.int32)

    def body(where_ref, w_ref, own_ref, land_ref, m_ref, v_ref, g_ref, d_ref, nm_ref, nv_ref):
        me = where_ref[1]
        g_ = jnp.zeros((tm, cols), F32)
        for dev in range(N_DEV):
            g_ = g_ + jnp.where(me == dev, own_ref[0], land_ref[dev]).astype(F32)
        m_ = ADAM_B1 * m_ref[...] + (1.0 - ADAM_B1) * g_
        v_ = ADAM_B2 * v_ref[...] + (1.0 - ADAM_B2) * (g_ * g_)
        m_hat = m_ / (1.0 - ADAM_B1 ** ADAM_STEP)
        v_hat = v_ / (1.0 - ADAM_B2 ** ADAM_STEP)
        g_ref[...] = g_
        d_ref[...] = -ADAM_LR * (m_hat / (jnp.sqrt(v_hat) + ADAM_EPS) + ADAM_WD * w_ref[...])
        nm_ref[...] = m_
        nv_ref[...] = v_

    tile = pl.BlockSpec((tm, cols), lambda i, where_ref: (i, 0))
    spec = pltpu.PrefetchScalarGridSpec(
        num_scalar_prefetch=1, grid=(rows // tm,),
        in_specs=[tile, pl.BlockSpec((1, tm, cols), lambda i, where_ref: (where_ref[0], i, 0)),
                  pl.BlockSpec((N_DEV, tm, cols), lambda i, where_ref: (0, i, 0)), tile, tile],
        out_specs=[tile] * 4)
    return pl.pallas_call(
        body, name=name, grid_spec=spec, out_shape=[jax.ShapeDtypeStruct(w.shape, F32)] * 4,
        compiler_params=_params(("arbitrary",)),
    )(where, w, own, land, m, v)


def _pack_small(values):
    flat = jnp.concatenate([values[n].reshape(-1).astype(F32) for n in SMALL])
    return jnp.pad(flat, (0, SMALL_ROWS * D_MODEL - flat.shape[0])).reshape(SMALL_ROWS, D_MODEL)


def _unpack_small(block, shapes):
    flat = block.reshape(-1)
    out, lo = {}, 0
    for n in SMALL:
        out[n] = flat[lo:lo + SMALL_SIZES[n]].reshape(shapes[n])
        lo += SMALL_SIZES[n]
    return out


def _after(token, a):
    return a + token[:1, :1].astype(a.dtype)


def kernel(x, p, mix_norm_g, w_in, sgu_w, sgu_b, sgu_norm_g, out_norm_a, out_norm_b, w_out, ffn_norm_g, w_gate, w_up, w_down, ple_norm_g, w_ple_gate, w_ple_proj, final_norm_g, loss_target, m_mix_norm_g, m_w_in, m_sgu_w, m_sgu_b, m_sgu_norm_g, m_out_norm_a, m_out_norm_b, m_w_out, m_ffn_norm_g, m_w_gate, m_w_up, m_w_down, m_ple_norm_g, m_w_ple_gate, m_w_ple_proj, m_final_norm_g, v_mix_norm_g, v_w_in, v_sgu_w, v_sgu_b, v_sgu_norm_g, v_out_norm_a, v_out_norm_b, v_w_out, v_ffn_norm_g, v_w_gate, v_w_up, v_w_down, v_ple_norm_g, v_w_ple_gate, v_w_ple_proj, v_final_norm_g):
    given = dict(locals())
    drop_lead = lambda a, lead: a.reshape(a.shape[lead:])
    xs, ps, target = drop_lead(x, 1), drop_lead(p, 2), drop_lead(loss_target, 1)
    s = xs.shape[0]
    shard = lambda name: drop_lead(given[name], 1)

    def slab_of(name):
        local = shard(name).astype(MXU_DTYPE)
        return local.T if SLAB_IS_TRANSPOSED[name] else local

    w_in_gather, token = _two_level_start(_stack_with_own(slab_of('w_in')))
    later = ['w_out', 'w_gate', 'w_up', 'w_down', 'w_ple_gate', 'w_ple_proj']
    later_slabs = [_stack_with_own(slab_of(n)) for n in later]
    cos_t, sin_t = (_after(token, table) for table in _rope_tables(s))
    tril = jnp.tril(jnp.ones((CHUNK, CHUNK), F32))
    w_tril = (sgu_w.reshape(HEADS_A, CHUNK, CHUNK) * tril).astype(MXU_DTYPE)
    w_tril_t = jnp.swapaxes(w_tril, 1, 2)
    w_in_gather = _two_level_forward(w_in_gather, [cos_t, sin_t, w_tril, w_tril_t] + later_slabs)
    w_in_t = _two_level_wait(w_in_gather)
    gather, token = _all_gather_start(later_slabs, w_in_t)

    bias = jnp.repeat(sgu_b.reshape(HEADS_A, CHUNK).T, HEAD_DIM, axis=1)
    g = {n: given[n].reshape(1, -1) for n in SMALL if n not in ('sgu_w', 'sgu_b')}

    uv, q, k, v, q1, k1, v1, hn1 = _in_fwd(xs, _after(token, g['mix_norm_g']), w_in_t, cos_t, sin_t)
    ya_n = _sgu_fwd(uv, w_tril, bias, g['sgu_norm_g'], g['out_norm_a'])
    branches = [_attn_fwd_local(q1, k1, v1)] + [_attn_fwd_branch(q, k, v, dil) for dil in DILATIONS[:-1]]
    y_b, lse = _attn_fwd_branch(q, k, v, DILATIONS[-1], earlier=branches)
    stacks = dict(zip(later, _all_gather_wait(gather, lse)))
    w_gate_t, w_up_t, w_pp_t = stacks['w_gate'], stacks['w_up'], stacks['w_ple_proj']
    h1, y_n = _out_fwd(ya_n, y_b, g['out_norm_b'], stacks['w_out'], xs)
    h2, gate, up, hn2 = _ffn_fwd(h1, g['ffn_norm_g'], w_gate_t, w_up_t, stacks['w_down'])
    loss, dh2, dz, dpp, hn3, d_ple_g, d_final_g = _ple_loss(
        h2, ps, target, g['ple_norm_g'], stacks['w_ple_gate'], w_pp_t, g['final_norm_g'])

    share = {}
    share['w_ple_gate'] = _wgrad(hn3, dz, "wgrad_ple_gate")
    share['w_ple_proj'] = _wgrad(dpp, ps, "wgrad_ple_proj")
    scatter_1, token = _reduce_scatter_start([share['w_ple_gate'], share['w_ple_proj']], "reduce_scatter_start_1")
    dh1, act, dgate, dup, d_ffn_g = _ffn_bwd(dh2, h1, gate, up, _after(token, g['ffn_norm_g']), stacks['w_down'],
                                             w_gate_t, w_up_t)
    share['w_down'] = _wgrad(act, dh2, "wgrad_down")
    share['w_gate'] = _wgrad(dgate, hn2, "wgrad_gate")
    share['w_up'] = _wgrad(dup, hn2, "wgrad_up")
    scatter_2, token = _reduce_scatter_start([share['w_down'], share['w_gate'], share['w_up']],
                                             "reduce_scatter_start_2")
    dya_n, dyb, d_out_b = _out_bwd(dh1, y_b, _after(token, g['out_norm_b']), stacks['w_out'])
    share['w_out'] = _wgrad(y_n, dh1, "wgrad_out")
    scatter_3, token = _reduce_scatter_start([share['w_out']], "reduce_scatter_start_3")
    grads = _attn_bwd_local(q1, k1, v1, dyb, y_b, lse)
    for dil in DILATIONS:
        grads = _attn_bwd_branch(q, k, v, dyb, y_b, lse, grads, dil)
    duv, d_sgu_w, d_sgu_b, d_sgu_g, d_out_a = _sgu_bwd(uv, dya_n, w_tril, w_tril_t, bias,
                                                       _after(token, g['sgu_norm_g']), g['out_norm_a'])
    dproj = _in_bwd_proj(duv, grads[0], grads[1], grads[2], cos_t, sin_t)
    share['w_in'] = _wgrad(dproj, hn1, "wgrad_in")
    scatter_4, token = _reduce_scatter_start([share['w_in']], "reduce_scatter_start_4")

    grads, deltas, new_m, new_v = {}, {}, {}, {}
    add_lead = lambda a: a.reshape((1,) + a.shape)

    done = {}

    def finish(names, handles, after, tag):
        landed = []
        for i, handle in enumerate(handles):
            landed += _reduce_scatter_wait(handle, after, "reduce_scatter_wait_%s%d" % (tag, i))
        for n, (own, land) in zip(names, landed):
            turn = (lambda a: a.T) if SLAB_IS_TRANSPOSED[n] else (lambda a: a)
            res = _adamw_of_shares(turn(shard(n)), own, land, turn(shard("m_" + n)), turn(shard("v_" + n)),
                                   "adamw_" + n)
            grads[n], deltas[n], new_m[n], new_v[n] = (add_lead(turn(a)) for a in res)
            done[n] = res[0]

    grad_x, d_mix_g = _in_bwd_x(dproj, w_in_t, xs, _after(token, g['mix_norm_g']), dh1)

    gs = {'mix_norm_g': d_mix_g, 'sgu_w': d_sgu_w, 'sgu_b': d_sgu_b[:, :HEADS_A].T, 'sgu_norm_g': d_sgu_g,
          'out_norm_a': d_out_a, 'out_norm_b': d_out_b, 'ffn_norm_g': d_ffn_g, 'ple_norm_g': d_ple_g,
          'final_norm_g': d_final_g}
    gs_block = _pack_small(gs).at[SMALL_ROWS - 1, 0].set(loss[0, 0])
    to_all = jnp.broadcast_to(gs_block[None], (N_CHIPS,) + gs_block.shape).reshape(-1, D_MODEL)
    scatter_small, token = _reduce_scatter_start([to_all], "small_all_reduce_start")

    finish(['w_ple_gate', 'w_ple_proj', 'w_down', 'w_gate', 'w_up', 'w_out'], [scatter_1, scatter_2, scatter_3], token,
           "early")
    finish(['w_in'], [scatter_4], [done[n] for n in ('w_down', 'w_gate', 'w_up', 'w_out')], "last")
    (own, land), = _reduce_scatter_wait(scatter_small, done['w_in'], "small_all_reduce_wait")
    small = {n: given[n] for n in SMALL}
    small_res = _adamw_of_shares(_pack_small(small), own, land, _pack_small({n: given["m_" + n] for n in SMALL}),
                                 _pack_small({n: given["v_" + n] for n in SMALL}), "adamw_small")
    loss_out = small_res[0][SMALL_ROWS - 1, 0]
    small_shapes = {n: given[n].shape for n in SMALL}
    for res, blk in zip((grads, deltas, new_m, new_v), small_res):
        res.update(_unpack_small(blk, small_shapes))

    outs = [loss_out, add_lead(grad_x)]
    for res in (grads, deltas, new_m, new_v):
        outs += [res[n] for n in WEIGHT_NAMES]
    return tuple(outs)
```

```python
import functools
import itertools
import math

import jax
import jax.numpy as jnp
import numpy as np
from jax import lax
from jax.experimental import pallas as pl
from jax.experimental.pallas import tpu as pltpu

F32 = jnp.float32
MXU_DTYPE = jnp.bfloat16

D_MODEL = 1024
HEAD_DIM = 64
HEADS_A = 4
HEADS_B = 12
WIDTH_A = HEADS_A * HEAD_DIM
WIDTH_B = HEADS_B * HEAD_DIM
CHUNK = 128
BLOCK = 128
DILATIONS = (4, 16)
ROPE_THETA = 10000.0
D_FF = 2816
FF_HALF = D_FF // 2
FF_STRIPS = ((0, 1024), (1024, 2048), (2048, D_FF))
PLE_DIM = 256
IN_COLS = 2 * WIDTH_A + 3 * WIDTH_B
EPS = 1e-6
LANES = 128
N_CHIPS = 4
N_DEV = 8

ADAM_LR = 0.001
ADAM_B1 = 0.9
ADAM_B2 = 0.999
ADAM_EPS = 1e-08
ADAM_WD = 0.01
ADAM_STEP = 10

VMEM_LIMIT = 56 * 1024 * 1024

WEIGHT_NAMES = ['mix_norm_g', 'w_in', 'sgu_w', 'sgu_b', 'sgu_norm_g', 'out_norm_a', 'out_norm_b', 'w_out',
                'ffn_norm_g', 'w_gate', 'w_up', 'w_down', 'ple_norm_g', 'w_ple_gate', 'w_ple_proj', 'final_norm_g']
SMALL = ['mix_norm_g', 'sgu_w', 'sgu_b', 'sgu_norm_g', 'out_norm_a', 'out_norm_b', 'ffn_norm_g', 'ple_norm_g',
         'final_norm_g']
SMALL_SIZES = {'mix_norm_g': 1024, 'sgu_w': 65536, 'sgu_b': 512, 'sgu_norm_g': 256, 'out_norm_a': 256,
               'out_norm_b': 768, 'ffn_norm_g': 1024, 'ple_norm_g': 1024, 'final_norm_g': 1024}
SMALL_ROWS = 72


def _params(semantics=None):
    return pltpu.CompilerParams(dimension_semantics=semantics, vmem_limit_bytes=VMEM_LIMIT)


def _full(shape):
    nd = len(shape)
    return pl.BlockSpec(shape, lambda i: (0,) * nd, pipeline_mode=pl.Buffered(1))


def _rows(tm, width):
    return pl.BlockSpec((tm, width), lambda i: (i, 0))


def _rms_stats(x):
    r = lax.rsqrt(jnp.mean(x * x, axis=-1, keepdims=True) + EPS)
    return x * r, r


def _rms_bwd(dn, n, r):
    return r * (dn - n * jnp.mean(dn * n, axis=-1, keepdims=True))


def _dot(a, b):
    return jnp.dot(a, b, preferred_element_type=F32)


def _dot_nt(a, b):
    return lax.dot_general(a, b, (((1,), (1,)), ((), ())), preferred_element_type=F32)


def _dot_tn(a, b):
    return lax.dot_general(a, b, (((0,), (0,)), ((), ())), preferred_element_type=F32)


def _gelu_parts(x):
    c = math.sqrt(2.0 / math.pi)
    t = jnp.tanh(c * (x + 0.044715 * x * x * x))
    return 0.5 * x * (1.0 + t), t


def _gelu_grad(x, t):
    c = math.sqrt(2.0 / math.pi)
    return 0.5 * (1.0 + t) + 0.5 * x * (1.0 - t * t) * c * (1.0 + 3.0 * 0.044715 * x * x)


def _half_masks(dtype):
    lane = lax.broadcasted_iota(jnp.int32, (BLOCK, LANES), 1)
    lo = (lane < HEAD_DIM).astype(F32)
    return lo.astype(dtype), (1.0 - lo).astype(dtype)


def _rope_partner(t):
    lane = lax.broadcasted_iota(jnp.int32, t.shape, 1)
    first_half = (lane % HEAD_DIM) < (HEAD_DIM // 2)
    return jnp.where(first_half, pltpu.roll(t, LANES - HEAD_DIM // 2, 1), pltpu.roll(t, HEAD_DIM // 2, 1))


PAIRS_ABREAST = 2
RESIDUES_PER_STEP = 4
L_BLOCK = 256
L_GROUP = 16


def _store_l256(scr, out_ref, cols, value, chunk_ref=None):
    tm = value.shape[0]
    half = L_GROUP // 2
    scr[...] = value
    for blk in range(tm // L_BLOCK):
        pieces = [scr[pl.ds(blk * L_BLOCK + r, L_GROUP, stride=L_GROUP), :] for r in range(L_GROUP)]
        for r, piece in enumerate(pieces):
            lo = blk * L_BLOCK + r * L_GROUP
            out_ref[lo:lo + L_GROUP, cols] = piece.astype(out_ref.dtype)
        if chunk_ref is not None:
            for chunk in range(L_BLOCK // BLOCK):
                for r in range(0, L_GROUP, 2):
                    lo = blk * L_BLOCK + chunk * BLOCK + r * half
                    both = [p[chunk * half:(chunk + 1) * half] for p in pieces[r:r + 2]]
                    chunk_ref[lo:lo + L_GROUP, cols] = jnp.concatenate(both, axis=0).astype(chunk_ref.dtype)


def _load_l256(col_refs, tm):
    cols = []
    for ref in col_refs:
        pieces = [ref[pl.ds(blk * L_BLOCK + i, L_GROUP, stride=L_GROUP), :]
                  for blk in range(tm // L_BLOCK) for i in range(L_GROUP)]
        cols.append(jnp.concatenate(pieces, axis=0))
    return jnp.concatenate(cols, axis=1)


def _col_specs(tm, width):
    return [pl.BlockSpec((tm, LANES), lambda i, j=j: (i, j)) for j in range(width // LANES)]


def _in_fwd(x, g_mix, w_in_t, cos_t, sin_t):
    s = x.shape[0]
    tm = 512

    def body(x_ref, g_ref, wt_ref, cos_ref, sin_ref, uv_ref, q_ref, k_ref, v_ref, q1_ref, k1_ref, v1_ref, hn_ref,
             *scrs):
        n, _ = _rms_stats(x_ref[...])
        hn = (n * g_ref[...]).astype(MXU_DTYPE)
        hn_ref[...] = hn
        cos = cos_ref[...]
        sin = sin_ref[...]
        strip = 2 * LANES
        for j in range(IN_COLS // strip):
            proj = _dot_nt(hn, wt_ref[j * strip:(j + 1) * strip, :])
            lo = j * strip - 2 * WIDTH_A
            if lo < 0:
                uv_ref[:, j * strip:(j + 1) * strip] = proj
                continue
            which, lo = divmod(lo, WIDTH_B)
            for i in range(strip // LANES):
                t = proj[:, i * LANES:(i + 1) * LANES]
                cols = slice(lo + i * LANES, lo + (i + 1) * LANES)
                scr = scrs[i]
                if which == 0:
                    _store_l256(scr, q_ref, cols, (t * cos + _rope_partner(t) * sin) * (HEAD_DIM ** -0.5), q1_ref)
                elif which == 1:
                    _store_l256(scr, k_ref, cols, t * cos + _rope_partner(t) * sin, k1_ref)
                else:
                    _store_l256(scr, v_ref, cols, t, v1_ref)

    return pl.pallas_call(
        body, name="in_fwd", grid=(s // tm,), scratch_shapes=[pltpu.VMEM((tm, LANES), F32)] * 2,
        in_specs=[_rows(tm, D_MODEL), _full((1, D_MODEL)), _full((IN_COLS, D_MODEL)), _rows(tm, LANES),
                  _rows(tm, LANES)],
        out_specs=[_rows(tm, 2 * WIDTH_A)] + [_rows(tm, WIDTH_B)] * 6 + [_rows(tm, D_MODEL)],
        out_shape=[jax.ShapeDtypeStruct((s, 2 * WIDTH_A), F32)] + [jax.ShapeDtypeStruct((s, WIDTH_B), MXU_DTYPE)] * 6
        + [jax.ShapeDtypeStruct((s, D_MODEL), MXU_DTYPE)],
        compiler_params=_params(("arbitrary",)),
    )(x, g_mix, w_in_t, cos_t, sin_t)


class _Branch:
    def __init__(self, dil, s):
        i = np.arange(L_GROUP)
        self.res = RESIDUES_PER_STEP
        if dil == 16:
            nblk = BLOCK // 16
            self.grid = (16 // self.res, s // (L_BLOCK * nblk))
            self.shape = (nblk, 1, self.res, L_GROUP)
            self.index = lambda r, n: (n, r // (4 // self.res), r % (4 // self.res), 0, 0)
            pos = (np.arange(nblk)[:, None] * 16 + i[None, :]).reshape(-1)
        else:
            nblk = BLOCK // 64
            self.grid = (4 // self.res, s // (L_BLOCK * nblk))
            self.shape = (nblk, 4, self.res, L_GROUP)
            self.index = lambda r, n: (n, 0, r, 0, 0)
            pos = (np.arange(nblk)[:, None, None] * 64 + np.arange(4)[None, :, None]
                   + 4 * i[None, None, :]).reshape(-1)
        self.qn = pos.shape[0]
        self.nb = self.grid[1]
        dist = pos[:, None] - np.concatenate([pos - self.qn, pos])[None, :]
        band = (dist >= 0) & (dist <= BLOCK)
        start = band & (np.arange(2 * self.qn)[None, :] >= self.qn)
        self.bias = np.where(np.stack([band, start]), 0.0, -np.inf).astype(np.float32)

    def view(self, a):
        return a.reshape(a.shape[0] // L_BLOCK, 4, 4, L_GROUP, a.shape[1])

    def spec(self, w, step=lambda n: n):
        return pl.BlockSpec(self.shape + (w,), lambda r, n: self.index(r, step(n)))

    def bias_spec(self, step=lambda n: n):
        return pl.BlockSpec((1, self.qn, 2 * self.qn), lambda r, n: (jnp.where(step(n) == 0, 1, 0), 0, 0))

    def load(self, ref, cols=slice(None), j=0):
        x = ref[:, :, j, :, cols]
        return x.reshape(self.qn, x.shape[-1])

    def store(self, ref, cols, value, j=0):
        ref[:, :, j, :, cols] = value.reshape(self.shape[:2] + (L_GROUP, value.shape[-1]))


def _attn_fwd_branch(q, k, v, dil, earlier=()):
    s = q.shape[0]
    br = _Branch(dil, s)
    qn = br.qn
    nearly = len(earlier)

    def body(bias_ref, q_ref, kc_ref, kp_ref, vc_ref, vp_ref, *refs):
        early_refs, (o_ref, lse_ref) = refs[:2 * nearly], refs[2 * nearly:]
        bias2 = jnp.concatenate([bias_ref[0], bias_ref[0]], axis=0)
        lo = lax.broadcasted_iota(jnp.int32, (qn, LANES), 1) < HEAD_DIM
        mask_lo = lo.astype(F32).astype(MXU_DTYPE)
        for j, hp in itertools.product(range(br.res), range(HEADS_B // 2)):
            cols = slice(hp * LANES, (hp + 1) * LANES)
            qp = br.load(q_ref, cols, j)
            kcat = jnp.concatenate([br.load(kp_ref, cols, j), br.load(kc_ref, cols, j)], axis=0)
            vcat = jnp.concatenate([br.load(vp_ref, cols, j), br.load(vc_ref, cols, j)], axis=0)
            sc = _dot_nt(jnp.concatenate([qp * mask_lo, qp * (1 - mask_lo)], axis=0), kcat) + bias2
            m = jnp.max(sc, axis=1, keepdims=True)
            p = jnp.exp(sc - m)
            l = jnp.sum(p, axis=1, keepdims=True)
            out = _dot(p.astype(MXU_DTYPE), vcat) / l
            lse = m + jnp.log(l)
            outs = [br.load(r, cols, j) for r in early_refs[:nearly]] + [jnp.where(lo, out[:qn], out[qn:])]
            lses = [br.load(r, cols, j) for r in early_refs[nearly:]] + [jnp.where(lo, lse[:qn], lse[qn:])]
            if nearly:
                top = functools.reduce(jnp.maximum, lses)
                ws = [jnp.exp(x - top) for x in lses]
                den = functools.reduce(jnp.add, ws)
                outs = [functools.reduce(jnp.add, [w * o for w, o in zip(ws, outs)]) / den]
                lses = [top + jnp.log(den)]
            br.store(o_ref, cols, outs[0], j)
            br.store(lse_ref, cols, lses[0], j)

    before = lambda n: jnp.maximum(n - 1, 0)
    res = pl.pallas_call(
        body, name="attn_fwd_d%d" % dil, grid=br.grid,
        in_specs=[br.bias_spec(), br.spec(WIDTH_B), br.spec(WIDTH_B), br.spec(WIDTH_B, before), br.spec(WIDTH_B),
                  br.spec(WIDTH_B, before)] + [br.spec(WIDTH_B)] * (2 * nearly),
        out_specs=[br.spec(WIDTH_B), br.spec(WIDTH_B)],
        out_shape=[jax.ShapeDtypeStruct((s // L_BLOCK, 4, 4, L_GROUP, WIDTH_B), F32)] * 2,
        compiler_params=_params(("arbitrary", "arbitrary")),
    )(jnp.asarray(br.bias), br.view(q), br.view(k), br.view(k), br.view(v), br.view(v),
      *[br.view(o) for o, _ in earlier], *[br.view(x) for _, x in earlier])
    return tuple(a.reshape(s, WIDTH_B) for a in res)


LOCAL_CHUNKS = 4


def _local_bias():
    row = np.arange(BLOCK)
    pos = L_GROUP * (row % (L_GROUP // 2)) + row // (L_GROUP // 2)
    dist = pos[:, None] - np.concatenate([pos - BLOCK, pos])[None, :]
    band = (dist >= 0) & (dist <= BLOCK)
    start = band & (np.arange(2 * BLOCK)[None, :] >= BLOCK)
    return np.where(np.stack([band, start]), 0.0, -np.inf).astype(np.float32)


def _chunk_view(a):
    return a.reshape(a.shape[0] // L_BLOCK, L_GROUP, 2, L_GROUP // 2, a.shape[1])


def _chunk_of(ref, j, cols=slice(None)):
    x = ref[j // 2, :, j % 2, :, cols]
    return x.reshape(BLOCK, x.shape[-1])


def _put_chunk(ref, j, cols, value):
    ref[j // 2, :, j % 2, :, cols] = value.reshape(L_GROUP, L_GROUP // 2, value.shape[-1])


def _local_keys(cur_ref, before_ref, j, cols):
    here = slice(j * BLOCK, (j + 1) * BLOCK)
    before = before_ref[:, cols] if j == 0 else cur_ref[(j - 1) * BLOCK:j * BLOCK, cols]
    return jnp.concatenate([before, cur_ref[here, cols]], axis=0)


def _local_specs(s, step=lambda n: n):
    rows = LOCAL_CHUNKS * BLOCK
    cur = pl.BlockSpec((rows, WIDTH_B), lambda n: (step(n), 0))
    before = pl.BlockSpec((BLOCK, WIDTH_B), lambda n: (jnp.maximum(LOCAL_CHUNKS * step(n) - 1, 0), 0))
    return [cur, cur, before, cur, before]


def _attn_fwd_local(q1, k1, v1):
    s = q1.shape[0]
    rows = LOCAL_CHUNKS * BLOCK
    qn = BLOCK

    def body(bias_ref, q_ref, kc_ref, kp_ref, vc_ref, vp_ref, o_ref, lse_ref):
        first = jnp.where(pl.program_id(0) == 0, bias_ref[1], bias_ref[0])
        biases = [jnp.concatenate([b, b], axis=0) for b in (first, bias_ref[0])]
        lo = lax.broadcasted_iota(jnp.int32, (qn, LANES), 1) < HEAD_DIM
        mask_lo = lo.astype(F32).astype(MXU_DTYPE)
        for j, hp in itertools.product(range(LOCAL_CHUNKS), range(HEADS_B // 2)):
            cols = slice(hp * LANES, (hp + 1) * LANES)
            qp = q_ref[j * BLOCK:(j + 1) * BLOCK, cols]
            kcat = _local_keys(kc_ref, kp_ref, j, cols)
            vcat = _local_keys(vc_ref, vp_ref, j, cols)
            sc = _dot_nt(jnp.concatenate([qp * mask_lo, qp * (1 - mask_lo)], axis=0), kcat) + biases[min(j, 1)]
            m = jnp.max(sc, axis=1, keepdims=True)
            p = jnp.exp(sc - m)
            l = jnp.sum(p, axis=1, keepdims=True)
            out = _dot(p.astype(MXU_DTYPE), vcat) / l
            lse = m + jnp.log(l)
            _put_chunk(o_ref, j, cols, jnp.where(lo, out[:qn], out[qn:]))
            _put_chunk(lse_ref, j, cols, jnp.where(lo, lse[:qn], lse[qn:]))

    out_spec = pl.BlockSpec((LOCAL_CHUNKS // 2, L_GROUP, 2, L_GROUP // 2, WIDTH_B), lambda n: (n, 0, 0, 0, 0))
    res = pl.pallas_call(
        body, name="attn_fwd_d1", grid=(s // rows,),
        in_specs=[_full((2, BLOCK, 2 * BLOCK))] + _local_specs(s), out_specs=[out_spec] * 2,
        out_shape=[jax.ShapeDtypeStruct((s // L_BLOCK, L_GROUP, 2, L_GROUP // 2, WIDTH_B), F32)] * 2,
        compiler_params=_params(("arbitrary",)),
    )(jnp.asarray(_local_bias()), q1, k1, k1, v1, v1)
    return tuple(a.reshape(s, WIDTH_B) for a in res)


def _attn_bwd_local(q1, k1, v1, do, o, lse):
    s = q1.shape[0]
    rows = LOCAL_CHUNKS * BLOCK
    nsteps = s // rows
    qn = BLOCK

    def body(bias_ref, q_ref, kc_ref, kp_ref, vc_ref, vp_ref, do_ref, o_ref, lse_ref, dq_ref, dk_ref, dv_ref,
             dk_buf, dv_buf):
        n = pl.program_id(0)

        @pl.when(n == 0)
        def _():
            dk_buf[...] = jnp.zeros_like(dk_buf)
            dv_buf[...] = jnp.zeros_like(dv_buf)

        @pl.when(n < nsteps)
        def _():
            first = jnp.where(n == 0, bias_ref[1], bias_ref[0])
            biases = [jnp.concatenate([b, b], axis=0) for b in (first, bias_ref[0])]
            lo = lax.broadcasted_iota(jnp.int32, (qn, LANES), 1) < HEAD_DIM
            mask_f = lo.astype(F32)
            mask_lo = mask_f.astype(MXU_DTYPE)
            dk_buf[LOCAL_CHUNKS:] = jnp.zeros((LOCAL_CHUNKS, qn, WIDTH_B), F32)
            dv_buf[LOCAL_CHUNKS:] = jnp.zeros((LOCAL_CHUNKS, qn, WIDTH_B), F32)

            def prepare(j, hp):
                cols = slice(hp * LANES, (hp + 1) * LANES)
                qp = q_ref[j * BLOCK:(j + 1) * BLOCK, cols]
                dop = _chunk_of(do_ref, j, cols)
                prod = dop * _chunk_of(o_ref, j, cols)
                prod_lo = prod * mask_f
                lse = _chunk_of(lse_ref, j, cols)
                return dict(
                    j=j, cols=cols, kcat=_local_keys(kc_ref, kp_ref, j, cols), vcat=_local_keys(vc_ref, vp_ref, j, cols),
                    qs=jnp.concatenate([qp * mask_lo, qp * (1 - mask_lo)], axis=0),
                    dos=jnp.concatenate([dop * mask_f, dop * (1.0 - mask_f)], axis=0).astype(MXU_DTYPE),
                    delta=jnp.concatenate([jnp.sum(prod_lo, axis=1, keepdims=True),
                                           jnp.sum(prod - prod_lo, axis=1, keepdims=True)], axis=0),
                    lse2=jnp.concatenate([lse[:, :1], lse[:, HEAD_DIM:HEAD_DIM + 1]], axis=0))

            def scores(t):
                t['sc'] = _dot_nt(t['qs'], t['kcat'])
                t['dp'] = _dot_nt(t['dos'], t['vcat'])

            def softmax(t):
                p = jnp.exp(t['sc'] + biases[min(t['j'], 1)] - t['lse2'])
                t['ds'] = (p * (t['dp'] - t['delta'])).astype(MXU_DTYPE)
                t['p'] = p.astype(MXU_DTYPE)

            def gradients(t):
                t['dvc'] = _dot_tn(t['p'], t['dos'])
                t['dkc'] = _dot_tn(t['ds'], t['qs'])
                t['dq2'] = _dot(t['ds'], t['kcat'])

            def store(t):
                j, cols = t['j'], t['cols']
                _put_chunk(dq_ref, j, cols, jnp.where(lo, t['dq2'][:qn], t['dq2'][qn:]))
                for buf, both in ((dk_buf, t['dkc']), (dv_buf, t['dvc'])):
                    buf[LOCAL_CHUNKS + j - 1, :, cols] += both[:qn]
                    buf[LOCAL_CHUNKS + j, :, cols] += both[qn:]

            for j, first_pair in itertools.product(range(LOCAL_CHUNKS), range(0, HEADS_B // 2, PAIRS_ABREAST)):
                group = [prepare(j, hp) for hp in range(first_pair, first_pair + PAIRS_ABREAST)]
                for stage in (scores, softmax, gradients, store):
                    for t in group:
                        stage(t)

        for j in range(LOCAL_CHUNKS):
            _put_chunk(dk_ref, j, slice(None), dk_buf[j])
            _put_chunk(dv_ref, j, slice(None), dv_buf[j])
        dk_buf[:LOCAL_CHUNKS] = dk_buf[LOCAL_CHUNKS:]
        dv_buf[:LOCAL_CHUNKS] = dv_buf[LOCAL_CHUNKS:]

    cur = lambda n: jnp.minimum(n, nsteps - 1)
    late = lambda n: jnp.maximum(n - 1, 0)
    view_spec = lambda step: pl.BlockSpec((LOCAL_CHUNKS // 2, L_GROUP, 2, L_GROUP // 2, WIDTH_B),
                                          lambda n: (step(n), 0, 0, 0, 0))
    res = pl.pallas_call(
        body, name="attn_bwd_d1", grid=(nsteps + 1,),
        in_specs=[_full((2, BLOCK, 2 * BLOCK))] + _local_specs(s, cur) + [view_spec(cur)] * 3,
        out_specs=[view_spec(cur), view_spec(late), view_spec(late)],
        out_shape=[jax.ShapeDtypeStruct((s // L_BLOCK, L_GROUP, 2, L_GROUP // 2, WIDTH_B), F32)] * 3,
        scratch_shapes=[pltpu.VMEM((2 * LOCAL_CHUNKS, qn, WIDTH_B), F32)] * 2,
        compiler_params=_params(("arbitrary",)),
    )(jnp.asarray(_local_bias()), q1, k1, k1, v1, v1, _chunk_view(do), _chunk_view(o), _chunk_view(lse))
    return tuple(a.reshape(s, WIDTH_B) for a in res)


def _sgu_forward_tile(uv, w_ref, bias, g_sgu):
    tm = uv.shape[0]
    u = uv[:, :WIDTH_A]
    v = uv[:, WIDTH_A:]
    ug, tu = _gelu_parts(u)
    vg, tv = _gelu_parts(v)
    mu = jnp.mean(vg, axis=-1, keepdims=True)
    vc = vg - mu
    rs = lax.rsqrt(jnp.mean(vc * vc, axis=-1, keepdims=True) + EPS)
    vhat = vc * rs
    vn = (vhat * g_sgu).astype(MXU_DTYPE)
    masks = _half_masks(MXU_DTYPE)
    chunks = []
    for c in range(tm // CHUNK):
        rows = slice(c * CHUNK, (c + 1) * CHUNK)
        groups = []
        for gp in range(2):
            vn_g = vn[rows, gp * LANES:(gp + 1) * LANES]
            groups.append(_dot(w_ref[2 * gp], vn_g * masks[0]) + _dot(w_ref[2 * gp + 1], vn_g * masks[1]))
        chunks.append(jnp.concatenate(groups, axis=1) + bias)
    mixed = jnp.concatenate(chunks, axis=0)
    return dict(u=u, v=v, ug=ug, tu=tu, tv=tv, rs=rs, vhat=vhat, vn=vn, mixed=mixed, ya=ug * mixed)


def _sgu_fwd(uv, w_tril, bias, g_sgu, g_a):
    s = uv.shape[0]
    tm = 512

    def body(uv_ref, w_ref, b_ref, gs_ref, ga_ref, o_ref):
        t = _sgu_forward_tile(uv_ref[...], w_ref, b_ref[...], gs_ref[...])
        n, _ = _rms_stats(t['ya'])
        o_ref[...] = (n * ga_ref[...]).astype(MXU_DTYPE)

    return pl.pallas_call(
        body, name="sgu_fwd", grid=(s // tm,),
        in_specs=[_rows(tm, 2 * WIDTH_A), _full((HEADS_A, CHUNK, CHUNK)), _full((CHUNK, WIDTH_A)),
                  _full((1, WIDTH_A)), _full((1, WIDTH_A))],
        out_specs=_rows(tm, WIDTH_A), out_shape=jax.ShapeDtypeStruct((s, WIDTH_A), MXU_DTYPE),
        compiler_params=_params(("arbitrary",)),
    )(uv, w_tril, bias, g_sgu, g_a)


def _out_fwd(ya_n, y_b, g_b, w_out, x):
    s = x.shape[0]
    tm = 512
    nc = WIDTH_B // LANES

    def body(ya_ref, *refs):
        yb_refs = refs[:nc]
        g_ref, w_ref, x_ref, h_ref, yn_ref = refs[nc:]
        n, _ = _rms_stats(_load_l256(yb_refs, tm))
        yn = jnp.concatenate([ya_ref[...], (n * g_ref[...]).astype(MXU_DTYPE)], axis=1)
        yn_ref[...] = yn
        h_ref[...] = x_ref[...] + _dot(yn, w_ref[...])

    return pl.pallas_call(
        body, name="out_fwd", grid=(s // tm,),
        in_specs=[_rows(tm, WIDTH_A)] + _col_specs(tm, WIDTH_B) + [_full((1, WIDTH_B)), _full((D_MODEL, D_MODEL)),
                                                                 _rows(tm, D_MODEL)],
        out_specs=[_rows(tm, D_MODEL), _rows(tm, D_MODEL)],
        out_shape=[jax.ShapeDtypeStruct((s, D_MODEL), F32), jax.ShapeDtypeStruct((s, D_MODEL), MXU_DTYPE)],
        compiler_params=_params(("arbitrary",)),
    )(ya_n, *([y_b] * nc), g_b, w_out, x)


def _ffn_fwd(h1, g_ffn, w_gate_t, w_up_t, w_down):
    s = h1.shape[0]
    tm = 512

    def body(h_ref, g_ref, wgt_ref, wut_ref, wd_ref, o_ref, gate_ref, up_ref, hn_ref):
        h = h_ref[...]
        n, _ = _rms_stats(h)
        hn = (n * g_ref[...]).astype(MXU_DTYPE)
        hn_ref[...] = hn
        strips = [dict(cols=slice(lo, hi)) for lo, hi in FF_STRIPS]

        def project(t):
            t['gate'] = _dot_nt(hn, wgt_ref[t['cols'], :])
            t['up'] = _dot_nt(hn, wut_ref[t['cols'], :])

        def activate(t):
            gate, up = t['gate'], t['up']
            gate_ref[:, t['cols']] = gate.astype(MXU_DTYPE)
            up_ref[:, t['cols']] = up.astype(MXU_DTYPE)
            t['act'] = (gate * jax.nn.sigmoid(gate) * up).astype(MXU_DTYPE)

        def down(t):
            return _dot(t['act'], wd_ref[t['cols'], :])

        out = h
        project(strips[0])
        for i, t in enumerate(strips):
            if i + 1 < len(strips):
                project(strips[i + 1])
            activate(t)
            out = out + down(t)
        o_ref[...] = out

    return pl.pallas_call(
        body, name="ffn_fwd", grid=(s // tm,),
        in_specs=[_rows(tm, D_MODEL), _full((1, D_MODEL)), _full((D_FF, D_MODEL)), _full((D_FF, D_MODEL)),
                  _full((D_FF, D_MODEL))],
        out_specs=[_rows(tm, D_MODEL), _rows(tm, D_FF), _rows(tm, D_FF), _rows(tm, D_MODEL)],
        out_shape=[jax.ShapeDtypeStruct((s, D_MODEL), F32), jax.ShapeDtypeStruct((s, D_FF), MXU_DTYPE),
                   jax.ShapeDtypeStruct((s, D_FF), MXU_DTYPE), jax.ShapeDtypeStruct((s, D_MODEL), MXU_DTYPE)],
        compiler_params=_params(("arbitrary",)),
    )(h1, g_ffn, w_gate_t, w_up_t, w_down)


def _ple_loss(h2, p, target, g_ple, w_pg, w_pp_t, g_final):
    s = h2.shape[0]
    tm = 512

    def body(h_ref, p_ref, t_ref, gp_ref, wg_ref, wpt_ref, gf_ref,
             loss_ref, dh_ref, dz_ref, dpp_ref, hn_ref, dgp_ref, dgf_ref):
        @pl.when(pl.program_id(0) == 0)
        def _():
            loss_ref[...] = jnp.zeros_like(loss_ref)
            dgp_ref[...] = jnp.zeros_like(dgp_ref)
            dgf_ref[...] = jnp.zeros_like(dgf_ref)

        h2t = h_ref[...]
        n2, r2 = _rms_stats(h2t)
        hn = (n2 * gp_ref[...]).astype(MXU_DTYPE)
        hn_ref[...] = hn
        gate = jax.nn.sigmoid(_dot(hn, wg_ref[...]))
        pp = _dot_nt(p_ref[...].astype(MXU_DTYPE), wpt_ref[...])
        h3 = h2t + gate * pp
        n3, r3 = _rms_stats(h3)
        diff = n3 * gf_ref[...] - t_ref[...]
        loss_ref[...] += jnp.full(loss_ref.shape, 0.5 * jnp.sum(diff * diff) / D_MODEL, F32)
        dy = diff * (1.0 / D_MODEL)
        dgf_ref[...] += jnp.sum(dy * n3, axis=0, keepdims=True)
        dh3 = _rms_bwd(dy * gf_ref[...], n3, r3)
        dpp_ref[...] = (dh3 * gate).astype(MXU_DTYPE)
        dz = (dh3 * pp * gate * (1.0 - gate)).astype(MXU_DTYPE)
        dz_ref[...] = dz
        dhn = _dot_nt(dz, wg_ref[...])
        dgp_ref[...] += jnp.sum(dhn * n2, axis=0, keepdims=True)
        dh_ref[...] = dh3 + _rms_bwd(dhn * gp_ref[...], n2, r2)

    return pl.pallas_call(
        body, name="ple_loss", grid=(s // tm,),
        in_specs=[_rows(tm, D_MODEL), _rows(tm, PLE_DIM), _rows(tm, D_MODEL), _full((1, D_MODEL)),
                  _full((D_MODEL, D_MODEL)), _full((D_MODEL, PLE_DIM)), _full((1, D_MODEL))],
        out_specs=[_full((1, LANES)), _rows(tm, D_MODEL), _rows(tm, D_MODEL), _rows(tm, D_MODEL),
                   _rows(tm, D_MODEL), _full((1, D_MODEL)), _full((1, D_MODEL))],
        out_shape=[jax.ShapeDtypeStruct((1, LANES), F32), jax.ShapeDtypeStruct((s, D_MODEL), F32),
                   jax.ShapeDtypeStruct((s, D_MODEL), MXU_DTYPE), jax.ShapeDtypeStruct((s, D_MODEL), MXU_DTYPE),
                   jax.ShapeDtypeStruct((s, D_MODEL), MXU_DTYPE), jax.ShapeDtypeStruct((1, D_MODEL), F32),
                   jax.ShapeDtypeStruct((1, D_MODEL), F32)],
        compiler_params=_params(("arbitrary",)),
    )(h2, p, target, g_ple, w_pg, w_pp_t, g_final)


def _ffn_bwd(dh2, h1, gate, up, g_ffn, w_down, w_gate_t, w_up_t):
    s = h1.shape[0]
    tm = 256

    def body(dh_ref, h_ref, gate_ref, up_ref, g_ref, wd_ref, wgt_ref, wut_ref,
             o_ref, act_ref, dg_ref, du_ref, dgn_ref):
        @pl.when(pl.program_id(0) == 0)
        def _():
            dgn_ref[...] = jnp.zeros_like(dgn_ref)

        dh = dh_ref[...]
        dhb = dh.astype(MXU_DTYPE)
        strips = [dict(cols=slice(lo, hi)) for lo, hi in FF_STRIPS]

        def back_down(t):
            t['dact'] = _dot_nt(dhb, wd_ref[t['cols'], :])

        def back_act(t):
            cols, dact = t['cols'], t['dact']
            g = gate_ref[:, cols].astype(F32)
            u = up_ref[:, cols].astype(F32)
            sg = jax.nn.sigmoid(g)
            silu = g * sg
            act_ref[:, cols] = (silu * u).astype(MXU_DTYPE)
            t['du'] = (dact * silu).astype(MXU_DTYPE)
            t['dg'] = (dact * u * sg * (1.0 + g * (1.0 - sg))).astype(MXU_DTYPE)
            du_ref[:, cols] = t['du']
            dg_ref[:, cols] = t['dg']

        def back_in(t):
            return _dot(t['dg'], wgt_ref[t['cols'], :]) + _dot(t['du'], wut_ref[t['cols'], :])

        dhn = jnp.zeros((tm, D_MODEL), F32)
        back_down(strips[0])
        for i, t in enumerate(strips):
            if i + 1 < len(strips):
                back_down(strips[i + 1])
            back_act(t)
            dhn = dhn + back_in(t)
        n, r = _rms_stats(h_ref[...])
        dgn_ref[...] += jnp.sum(dhn * n, axis=0, keepdims=True)
        o_ref[...] = dh + _rms_bwd(dhn * g_ref[...], n, r)

    return pl.pallas_call(
        body, name="ffn_bwd", grid=(s // tm,),
        in_specs=[_rows(tm, D_MODEL), _rows(tm, D_MODEL), _rows(tm, D_FF), _rows(tm, D_FF), _full((1, D_MODEL)),
                  _full((D_FF, D_MODEL)), _full((D_FF, D_MODEL)), _full((D_FF, D_MODEL))],
        out_specs=[_rows(tm, D_MODEL), _rows(tm, D_FF), _rows(tm, D_FF), _rows(tm, D_FF), _full((1, D_MODEL))],
        out_shape=[jax.ShapeDtypeStruct((s, D_MODEL), F32), jax.ShapeDtypeStruct((s, D_FF), MXU_DTYPE),
                   jax.ShapeDtypeStruct((s, D_FF), MXU_DTYPE), jax.ShapeDtypeStruct((s, D_FF), MXU_DTYPE),
                   jax.ShapeDtypeStruct((1, D_MODEL), F32)],
        compiler_params=_params(("arbitrary",)),
    )(dh2, h1, gate, up, g_ffn, w_down, w_gate_t, w_up_t)


def _out_bwd(dh1, y_b, g_b, w_out):
    s = dh1.shape[0]
    tm = 512
    nc = WIDTH_B // LANES

    def body(dh_ref, *refs):
        yb_refs = refs[:nc]
        g_ref, w_ref, dya_ref, dyb_ref, dg_ref, scr = refs[nc:]

        @pl.when(pl.program_id(0) == 0)
        def _():
            dg_ref[...] = jnp.zeros_like(dg_ref)

        dy = _dot_nt(dh_ref[...].astype(MXU_DTYPE), w_ref[...])
        dya_ref[...] = dy[:, :WIDTH_A]
        dyb = dy[:, WIDTH_A:]
        n, r = _rms_stats(_load_l256(yb_refs, tm))
        dg_ref[...] += jnp.sum(dyb * n, axis=0, keepdims=True)
        dyb_in = _rms_bwd(dyb * g_ref[...], n, r)
        for j in range(nc):
            cols = slice(j * LANES, (j + 1) * LANES)
            _store_l256(scr, dyb_ref, cols, dyb_in[:, cols])

    return pl.pallas_call(
        body, name="out_bwd", grid=(s // tm,), scratch_shapes=[pltpu.VMEM((tm, LANES), F32)],
        in_specs=[_rows(tm, D_MODEL)] + _col_specs(tm, WIDTH_B) + [_full((1, WIDTH_B)), _full((D_MODEL, D_MODEL))],
        out_specs=[_rows(tm, WIDTH_A), _rows(tm, WIDTH_B), _full((1, WIDTH_B))],
        out_shape=[jax.ShapeDtypeStruct((s, WIDTH_A), F32), jax.ShapeDtypeStruct((s, WIDTH_B), F32),
                   jax.ShapeDtypeStruct((1, WIDTH_B), F32)],
        compiler_params=_params(("arbitrary",)),
    )(dh1, *([y_b] * nc), g_b, w_out)


def _attn_bwd_branch(q, k, v, do, o, lse, grads, dil):
    s = q.shape[0]
    br = _Branch(dil, s)
    qn, nb = br.qn, br.nb
    first = grads is None

    def body(*refs):
        bias_ref, q_ref, kc_ref, kp_ref, vc_ref, vp_ref, do_ref, o_ref, lse_ref = refs[:9]
        if first:
            rest = refs[9:]
        else:
            dq_in, dk_in, dv_in = refs[9:12]
            rest = refs[12:]
        dq_ref, dk_ref, dv_ref, dk_carry, dv_carry = rest
        n = pl.program_id(1)

        @pl.when(n == 0)
        def _():
            dk_carry[...] = jnp.zeros_like(dk_carry)
            dv_carry[...] = jnp.zeros_like(dv_carry)

        @pl.when(n < nb)
        def _():
            bias2 = jnp.concatenate([bias_ref[0], bias_ref[0]], axis=0)
            lane = lax.broadcasted_iota(jnp.int32, (qn, LANES), 1)
            lo = lane < HEAD_DIM
            mask_f = lo.astype(F32)
            mask_lo = mask_f.astype(MXU_DTYPE)
            def prepare(j, hp):
                cols = slice(hp * LANES, (hp + 1) * LANES)
                qp = br.load(q_ref, cols, j)
                dop = br.load(do_ref, cols, j)
                prod = dop * br.load(o_ref, cols, j)
                prod_lo = prod * mask_f
                lse = br.load(lse_ref, cols, j)
                return dict(
                    j=j, cols=cols,
                    kcat=jnp.concatenate([br.load(kp_ref, cols, j), br.load(kc_ref, cols, j)], axis=0),
                    vcat=jnp.concatenate([br.load(vp_ref, cols, j), br.load(vc_ref, cols, j)], axis=0),
                    qs=jnp.concatenate([qp * mask_lo, qp * (1 - mask_lo)], axis=0),
                    dos=jnp.concatenate([dop * mask_f, dop * (1.0 - mask_f)], axis=0).astype(MXU_DTYPE),
                    delta=jnp.concatenate([jnp.sum(prod_lo, axis=1, keepdims=True),
                                           jnp.sum(prod - prod_lo, axis=1, keepdims=True)], axis=0),
                    lse2=jnp.concatenate([lse[:, :1], lse[:, HEAD_DIM:HEAD_DIM + 1]], axis=0))

            def scores(t):
                t['sc'] = _dot_nt(t['qs'], t['kcat'])
                t['dp'] = _dot_nt(t['dos'], t['vcat'])

            def softmax(t):
                p = jnp.exp(t['sc'] + bias2 - t['lse2'])
                t['ds'] = (p * (t['dp'] - t['delta'])).astype(MXU_DTYPE)
                t['p'] = p.astype(MXU_DTYPE)

            def gradients(t):
                t['dvc'] = _dot_tn(t['p'], t['dos'])
                t['dkc'] = _dot_tn(t['ds'], t['qs'])
                t['dq2'] = _dot(t['ds'], t['kcat'])

            def store(t):
                j, cols, dkc, dvc = t['j'], t['cols'], t['dkc'], t['dvc']
                dq = jnp.where(lo, t['dq2'][:qn], t['dq2'][qn:])
                dk_prev = dk_carry[j, :, cols] + dkc[:qn]
                dv_prev = dv_carry[j, :, cols] + dvc[:qn]
                if not first:
                    dq = dq + br.load(dq_in, cols, j)
                    dk_prev = dk_prev + br.load(dk_in, cols, j)
                    dv_prev = dv_prev + br.load(dv_in, cols, j)
                br.store(dq_ref, cols, dq, j)
                br.store(dk_ref, cols, dk_prev, j)
                br.store(dv_ref, cols, dv_prev, j)
                dk_carry[j, :, cols] = dkc[qn:]
                dv_carry[j, :, cols] = dvc[qn:]

            for j, first_pair in itertools.product(range(br.res), range(0, HEADS_B // 2, PAIRS_ABREAST)):
                group = [prepare(j, hp) for hp in range(first_pair, first_pair + PAIRS_ABREAST)]
                for stage in (scores, softmax, gradients, store):
                    for t in group:
                        stage(t)

        @pl.when(n == nb)
        def _():
            for j in range(br.res):
                dk_last = dk_carry[j]
                dv_last = dv_carry[j]
                if not first:
                    dk_last = dk_last + br.load(dk_in, slice(None), j)
                    dv_last = dv_last + br.load(dv_in, slice(None), j)
                br.store(dk_ref, slice(None), dk_last, j)
                br.store(dv_ref, slice(None), dv_last, j)

    cur = lambda n: jnp.minimum(n, nb - 1)
    before = lambda n: jnp.maximum(cur(n) - 1, 0)
    late = lambda n: jnp.maximum(n - 1, 0)
    in_specs = [br.bias_spec(cur), br.spec(WIDTH_B, cur), br.spec(WIDTH_B, cur), br.spec(WIDTH_B, before),
                br.spec(WIDTH_B, cur), br.spec(WIDTH_B, before), br.spec(WIDTH_B, cur), br.spec(WIDTH_B, cur),
                br.spec(WIDTH_B, cur)]
    args = [jnp.asarray(br.bias)] + [br.view(a) for a in (q, k, k, v, v, do, o, lse)]
    if not first:
        in_specs += [br.spec(WIDTH_B, cur), br.spec(WIDTH_B, late), br.spec(WIDTH_B, late)]
        args += [br.view(g) for g in grads]
    res = pl.pallas_call(
        body, name="attn_bwd_d%d" % dil, grid=(br.grid[0], nb + 1), in_specs=in_specs,
        out_specs=[br.spec(WIDTH_B, cur), br.spec(WIDTH_B, late), br.spec(WIDTH_B, late)],
        out_shape=[jax.ShapeDtypeStruct((s // L_BLOCK, 4, 4, L_GROUP, WIDTH_B), F32)] * 3,
        scratch_shapes=[pltpu.VMEM((br.res, qn, WIDTH_B), F32), pltpu.VMEM((br.res, qn, WIDTH_B), F32)],
        compiler_params=_params(("arbitrary", "arbitrary")),
    )(*args)
    return tuple(a.reshape(s, WIDTH_B) for a in res)


def _sgu_bwd(uv, dya_n, w_tril, w_tril_t, bias, g_sgu, g_a):
    s = uv.shape[0]
    tm = 512

    def body(uv_ref, dy_ref, w_ref, wt_ref, b_ref, gs_ref, ga_ref, duv_ref, dw_ref, db_ref, dgs_ref, dga_ref,
             db_acc):
        i = pl.program_id(0)

        @pl.when(i == 0)
        def _():
            dw_ref[...] = jnp.zeros_like(dw_ref)
            dgs_ref[...] = jnp.zeros_like(dgs_ref)
            dga_ref[...] = jnp.zeros_like(dga_ref)
            db_acc[...] = jnp.zeros_like(db_acc)

        t = _sgu_forward_tile(uv_ref[...], w_ref, b_ref[...], gs_ref[...])
        na, ra = _rms_stats(t['ya'])
        dyn = dy_ref[...]
        dga_ref[...] += jnp.sum(dyn * na, axis=0, keepdims=True)
        dya = _rms_bwd(dyn * ga_ref[...], na, ra)
        dug = dya * t['mixed']
        dmixed = dya * t['ug']
        dmb = dmixed.astype(MXU_DTYPE)
        masks = _half_masks(MXU_DTYPE)
        chunks = []
        db = jnp.zeros((CHUNK, WIDTH_A), F32)
        for c in range(tm // CHUNK):
            rows = slice(c * CHUNK, (c + 1) * CHUNK)
            db = db + dmixed[rows]
            groups = []
            for gp in range(2):
                cols = slice(gp * LANES, (gp + 1) * LANES)
                dm_g = dmb[rows, cols]
                vn_g = t['vn'][rows, cols]
                dvn_g = jnp.zeros((CHUNK, LANES), F32)
                for j in range(2):
                    dm_h = dm_g * masks[j]
                    dvn_g = dvn_g + _dot(wt_ref[2 * gp + j], dm_h)
                    dw_ref[2 * gp + j] += _dot_nt(dm_h, vn_g)
                groups.append(dvn_g)
            chunks.append(jnp.concatenate(groups, axis=1))
        db_acc[...] += db
        dvn = jnp.concatenate(chunks, axis=0)
        vhat = t['vhat']
        dgs_ref[...] += jnp.sum(dvn * vhat, axis=0, keepdims=True)
        dvh = dvn * gs_ref[...]
        dvg = t['rs'] * (dvh - jnp.mean(dvh, axis=-1, keepdims=True)
                         - vhat * jnp.mean(dvh * vhat, axis=-1, keepdims=True))
        duv_ref[:, :WIDTH_A] = (dug * _gelu_grad(t['u'], t['tu'])).astype(MXU_DTYPE)
        duv_ref[:, WIDTH_A:] = (dvg * _gelu_grad(t['v'], t['tv'])).astype(MXU_DTYPE)

        @pl.when(i == pl.num_programs(0) - 1)
        def _():
            lane_a = lax.broadcasted_iota(jnp.int32, (CHUNK, WIDTH_A), 1)
            lane = lax.broadcasted_iota(jnp.int32, (CHUNK, LANES), 1)
            acc = db_acc[...]
            out = jnp.zeros((CHUNK, LANES), F32)
            for h in range(HEADS_A):
                col = jnp.sum(jnp.where(lane_a // HEAD_DIM == h, acc, 0.0), axis=1, keepdims=True)
                out = jnp.where(lane == h, col, out)
            db_ref[...] = out
            causal = (lax.broadcasted_iota(jnp.int32, (CHUNK, CHUNK), 0)
                      >= lax.broadcasted_iota(jnp.int32, (CHUNK, CHUNK), 1))
            for h in range(HEADS_A):
                dw_ref[h] = jnp.where(causal, dw_ref[h], 0.0)

    return pl.pallas_call(
        body, name="sgu_bwd", grid=(s // tm,),
        in_specs=[_rows(tm, 2 * WIDTH_A), _rows(tm, WIDTH_A), _full((HEADS_A, CHUNK, CHUNK)),
                  _full((HEADS_A, CHUNK, CHUNK)), _full((CHUNK, WIDTH_A)), _full((1, WIDTH_A)),
                  _full((1, WIDTH_A))],
        out_specs=[_rows(tm, 2 * WIDTH_A), _full((HEADS_A, CHUNK, CHUNK)), _full((CHUNK, LANES)),
                   _full((1, WIDTH_A)), _full((1, WIDTH_A))],
        out_shape=[jax.ShapeDtypeStruct((s, 2 * WIDTH_A), MXU_DTYPE),
                   jax.ShapeDtypeStruct((HEADS_A, CHUNK, CHUNK), F32), jax.ShapeDtypeStruct((CHUNK, LANES), F32),
                   jax.ShapeDtypeStruct((1, WIDTH_A), F32), jax.ShapeDtypeStruct((1, WIDTH_A), F32)],
        scratch_shapes=[pltpu.VMEM((CHUNK, WIDTH_A), F32)],
        compiler_params=_params(("arbitrary",)),
    )(uv, dya_n, w_tril, w_tril_t, bias, g_sgu, g_a)


def _in_bwd_proj(duv, dq, dk, dv, cos_t, sin_t):
    s = duv.shape[0]
    tm = 512
    nc = WIDTH_B // LANES

    def body(duv_ref, *refs):
        dq_refs, dk_refs, dv_refs = refs[:nc], refs[nc:2 * nc], refs[2 * nc:3 * nc]
        cos_ref, sin_ref, dp_ref = refs[3 * nc:]
        cos = cos_ref[...]
        sin = sin_ref[...]
        dp_ref[:, :2 * WIDTH_A] = duv_ref[...]
        for i in range(nc):
            lo = 2 * WIDTH_A + i * LANES
            tq = _load_l256(dq_refs[i:i + 1], tm) * (HEAD_DIM ** -0.5)
            tk = _load_l256(dk_refs[i:i + 1], tm)
            dp_ref[:, lo:lo + LANES] = (tq * cos + _rope_partner(tq * sin)).astype(MXU_DTYPE)
            dp_ref[:, lo + WIDTH_B:lo + WIDTH_B + LANES] = (tk * cos + _rope_partner(tk * sin)).astype(MXU_DTYPE)
            dp_ref[:, lo + 2 * WIDTH_B:lo + 2 * WIDTH_B + LANES] = _load_l256(dv_refs[i:i + 1], tm).astype(MXU_DTYPE)

    return pl.pallas_call(
        body, name="in_bwd_proj", grid=(s // tm,),
        in_specs=[_rows(tm, 2 * WIDTH_A)] + 3 * _col_specs(tm, WIDTH_B) + [_rows(tm, LANES), _rows(tm, LANES)],
        out_specs=_rows(tm, IN_COLS), out_shape=jax.ShapeDtypeStruct((s, IN_COLS), MXU_DTYPE),
        compiler_params=_params(("arbitrary",)),
    )(duv, *([dq] * nc), *([dk] * nc), *([dv] * nc), cos_t, sin_t)


def _in_bwd_x(dproj, w_in_t, x, g_mix, dh1):
    s = x.shape[0]
    tm = 512

    def body(dp_ref, wt_ref, x_ref, g_ref, dh_ref, gx_ref, dg_ref):
        @pl.when(pl.program_id(0) == 0)
        def _():
            dg_ref[...] = jnp.zeros_like(dg_ref)

        dhn = _dot(dp_ref[...], wt_ref[...])
        n, r = _rms_stats(x_ref[...])
        dg_ref[...] += jnp.sum(dhn * n, axis=0, keepdims=True)
        gx_ref[...] = dh_ref[...] + _rms_bwd(dhn * g_ref[...], n, r)

    return pl.pallas_call(
        body, name="in_bwd_x", grid=(s // tm,),
        in_specs=[_rows(tm, IN_COLS), _full((IN_COLS, D_MODEL)), _rows(tm, D_MODEL), _full((1, D_MODEL)),
                  _rows(tm, D_MODEL)],
        out_specs=[_rows(tm, D_MODEL), _full((1, D_MODEL))],
        out_shape=[jax.ShapeDtypeStruct((s, D_MODEL), F32), jax.ShapeDtypeStruct((1, D_MODEL), F32)],
        compiler_params=_params(("arbitrary",)),
    )(dproj, w_in_t, x, g_mix, dh1)


def _wgrad(a, b, name):
    s, m = a.shape
    n = b.shape[1]
    bm = min(m, FF_HALF)
    ts = 1024
    nsteps = s // ts

    def body(a_ref, b_ref, o_ref, acc):
        kk = pl.program_id(1)

        @pl.when(kk == 0)
        def _():
            acc[...] = jnp.zeros_like(acc)

        acc[...] += _dot_tn(a_ref[...].astype(MXU_DTYPE), b_ref[...].astype(MXU_DTYPE))

        @pl.when(kk == nsteps - 1)
        def _():
            o_ref[...] = acc[...].astype(o_ref.dtype)

    return pl.pallas_call(
        body, name=name, grid=(m // bm, nsteps),
        in_specs=[pl.BlockSpec((ts, bm), lambda i, kk: (kk, i)), pl.BlockSpec((ts, n), lambda i, kk: (kk, 0))],
        out_specs=pl.BlockSpec((bm, n), lambda i, kk: (i, 0)), out_shape=jax.ShapeDtypeStruct((m, n), jnp.bfloat16),
        scratch_shapes=[pltpu.VMEM((bm, n), F32)],
        compiler_params=_params(("arbitrary", "arbitrary")),
    )(a, b)


def _rope_tables(s):
    half = HEAD_DIM // 2
    inv = ROPE_THETA ** (-jnp.arange(half, dtype=F32) / half)
    ang = jnp.arange(s, dtype=F32)[:, None] * jnp.tile(inv, LANES // half)[None, :]
    sign = jnp.tile(jnp.concatenate([-jnp.ones(half, F32), jnp.ones(half, F32)]), LANES // HEAD_DIM)
    return jnp.cos(ang), jnp.sin(ang) * sign[None, :]


MESH = pl.DeviceIdType.MESH
ANY = pl.BlockSpec(memory_space=pl.ANY)
SEM = pl.BlockSpec(memory_space=pltpu.SEMAPHORE)
SPLIT_COPY = pltpu.CompilerParams(has_side_effects=pltpu.SideEffectType.DATAFLOW_SIDE_EFFECTING)
SLAB_IS_TRANSPOSED = {'w_in': True, 'w_out': False, 'w_gate': True, 'w_up': True, 'w_down': False,
                      'w_ple_gate': False, 'w_ple_proj': True}


def _place():
    x, y, c = lax.axis_index("x"), lax.axis_index("y"), lax.axis_index("c")
    other_chips = [(1 - x, y), (x, 1 - y), (1 - x, 1 - y)]
    return x, y, c, other_chips


def _chip_of(chip):
    return 2 * chip[0] + chip[1]


def _half(ref, lead, hc):
    hr = ref.shape[1] // 2
    return ref.at[lead, pl.ds(hc * hr, hr), :]


def _stack_with_own(slab):
    me = 2 * lax.axis_index("x") + lax.axis_index("y")
    stack = lax.empty((N_CHIPS,) + slab.shape, slab.dtype)
    return lax.dynamic_update_slice(stack, slab[None], (me, 0, 0))


def _first_hops(land_ref, send_sems, recv_sems):
    x, y, c, chips = _place()
    mine = _half(land_ref, 2 * x + y, c)
    sends = [pltpu.make_async_remote_copy(src_ref=mine, dst_ref=mine, send_sem=send_sems.at[j], recv_sem=recv_sems.at[j],
                                          device_id=(*chip, c), device_id_type=MESH) for j, chip in enumerate(chips)]
    recvs = [pltpu.make_async_remote_copy(src_ref=mine, dst_ref=_half(land_ref, _chip_of(chip), c),
                                          send_sem=send_sems.at[j], recv_sem=recv_sems.at[j], device_id=(*chip, c),
                                          device_id_type=MESH) for j, chip in enumerate(chips)]
    return sends, recvs


def _second_hops(land_ref, send_sems, recv_sems):
    x, y, c, chips = _place()
    sibling = (x, y, 1 - c)
    sends = [pltpu.make_async_remote_copy(src_ref=_half(land_ref, _chip_of(chip), c),
                                          dst_ref=_half(land_ref, _chip_of(chip), c), send_sem=send_sems.at[j],
                                          recv_sem=recv_sems.at[j], device_id=sibling, device_id_type=MESH)
             for j, chip in enumerate(chips)]
    recvs = [pltpu.make_async_remote_copy(src_ref=_half(land_ref, _chip_of(chip), c),
                                          dst_ref=_half(land_ref, _chip_of(chip), 1 - c), send_sem=send_sems.at[j],
                                          recv_sem=recv_sems.at[j], device_id=sibling, device_id_type=MESH)
             for j, chip in enumerate(chips)]
    return sends, recvs


def _two_level_start(stack):
    def body(land_ref, send_sems, recv_sems, land_thru, token):
        for cp in _first_hops(land_ref, send_sems, recv_sems)[0]:
            cp.start()
        token[...] = jnp.zeros_like(token)

    res = pl.pallas_call(
        body, name="two_level_start",
        out_shape=(pltpu.SemaphoreType.DMA((3,)), pltpu.SemaphoreType.DMA((3,)), pltpu.HBM(stack.shape, stack.dtype),
                   jax.ShapeDtypeStruct((8, LANES), F32)),
        in_specs=[ANY], out_specs=(SEM, SEM, ANY, pl.BlockSpec(memory_space=pltpu.VMEM)),
        input_output_aliases={0: 2}, compiler_params=SPLIT_COPY,
    )(pltpu.with_memory_space_constraint(stack, pltpu.HBM))
    return res[:-1], res[-1]


def _two_level_forward(handle, after):
    send_sems, recv_sems, land = handle

    def body(land_ref, send_sems, recv_sems, *refs):
        send2, recv2 = refs[len(after):len(after) + 2]
        sends, recvs = _first_hops(land_ref, send_sems, recv_sems)
        for cp in sends:
            cp.wait_send()
        for cp in recvs:
            cp.wait_recv()
        for cp in _second_hops(land_ref, send2, recv2)[0]:
            cp.start()

    return pl.pallas_call(
        body, name="two_level_forward",
        out_shape=(pltpu.SemaphoreType.DMA((3,)), pltpu.SemaphoreType.DMA((3,)), pltpu.HBM(land.shape, land.dtype)),
        in_specs=[ANY, SEM, SEM] + [ANY] * len(after), out_specs=(SEM, SEM, ANY),
        input_output_aliases={0: 2}, compiler_params=SPLIT_COPY,
    )(land, send_sems, recv_sems, *after)


def _two_level_wait(handle):
    send_sems, recv_sems, land = handle

    def body(land_ref, send_sems, recv_sems, land_thru):
        sends, recvs = _second_hops(land_ref, send_sems, recv_sems)
        for cp in sends:
            cp.wait_send()
        for cp in recvs:
            cp.wait_recv()

    land = pl.pallas_call(
        body, name="two_level_wait", out_shape=pltpu.HBM(land.shape, land.dtype), in_specs=[ANY, SEM, SEM],
        out_specs=ANY, input_output_aliases={0: 0}, compiler_params=SPLIT_COPY,
    )(land, send_sems, recv_sems)
    return land.reshape(-1, land.shape[-1])


def _gather_copies(land_refs, send_sems, recv_sems):
    x, y, c, chips = _place()
    sends, recvs = [], []
    for k, land in enumerate(land_refs):
        mine = _half(land, 2 * x + y, c)
        for j, chip in enumerate(chips):
            for t in range(2):
                sends.append(pltpu.make_async_remote_copy(
                    src_ref=mine, dst_ref=mine, send_sem=send_sems.at[6 * k + 2 * j + t],
                    recv_sem=recv_sems.at[6 * k + 2 * j + c], device_id=(*chip, t), device_id_type=MESH))
                recvs.append(pltpu.make_async_remote_copy(
                    src_ref=mine, dst_ref=_half(land, _chip_of(chip), t), send_sem=send_sems.at[6 * k + 2 * j + t],
                    recv_sem=recv_sems.at[6 * k + 2 * j + t], device_id=(*chip, t), device_id_type=MESH))
    return sends, recvs


def _all_gather_start(stacks, after):
    n = len(stacks)

    def body(*refs):
        send_sems, recv_sems = refs[n + 1:n + 3]
        token = refs[-1]
        sends, _ = _gather_copies(refs[:n], send_sems, recv_sems)
        for cp in sends:
            cp.start()
        token[...] = jnp.zeros_like(token)

    hbm = lambda a: pltpu.HBM(a.shape, a.dtype)
    res = pl.pallas_call(
        body, name="all_gather_start",
        out_shape=(pltpu.SemaphoreType.DMA((6 * n,)), pltpu.SemaphoreType.DMA((6 * n,)), *map(hbm, stacks),
                   jax.ShapeDtypeStruct((8, LANES), F32)),
        in_specs=[ANY] * (n + 1), out_specs=(SEM, SEM, *([ANY] * n), pl.BlockSpec(memory_space=pltpu.VMEM)),
        input_output_aliases={i: 2 + i for i in range(n)}, compiler_params=SPLIT_COPY,
    )(*[pltpu.with_memory_space_constraint(a, pltpu.HBM) for a in stacks], after)
    return res[:-1], res[-1]


def _all_gather_wait(handle, after):
    send_sems, recv_sems = handle[:2]
    lands = handle[2:]
    n = len(lands)

    def body(*refs):
        send_sems, recv_sems = refs[n:n + 2]
        sends, recvs = _gather_copies(refs[:n], send_sems, recv_sems)
        for cp in sends:
            cp.wait_send()
        for cp in recvs:
            cp.wait_recv()

    hbm = lambda a: pltpu.HBM(a.shape, a.dtype)
    res = pl.pallas_call(
        body, name="all_gather_wait", out_shape=tuple(map(hbm, lands)),
        in_specs=[ANY] * n + [SEM, SEM, ANY], out_specs=tuple([ANY] * n),
        input_output_aliases={i: i for i in range(n)}, compiler_params=SPLIT_COPY,
    )(*lands, send_sems, recv_sems, after)
    return [land.reshape(-1, land.shape[-1]) for land in res]


def _scatter_copies(part_refs, land_refs, send_sems, recv_sems):
    x, y, c, chips = _place()
    me = 4 * x + 2 * y + c
    sends, recvs = [], []
    for k, (part, land) in enumerate(zip(part_refs, land_refs)):
        for j, chip in enumerate(chips):
            for t in range(2):
                sends.append(pltpu.make_async_remote_copy(
                    src_ref=part.at[_chip_of(chip)], dst_ref=land.at[me],
                    send_sem=send_sems.at[7 * k + 2 * j + t], recv_sem=recv_sems.at[7 * k + 2 * j + c],
                    device_id=(*chip, t), device_id_type=MESH))
                recvs.append(pltpu.make_async_remote_copy(
                    src_ref=part.at[_chip_of(chip)], dst_ref=land.at[2 * _chip_of(chip) + t],
                    send_sem=send_sems.at[7 * k + 2 * j + t], recv_sem=recv_sems.at[7 * k + 2 * j + t],
                    device_id=(*chip, t), device_id_type=MESH))
        sends.append(pltpu.make_async_remote_copy(
            src_ref=part.at[2 * x + y], dst_ref=land.at[me], send_sem=send_sems.at[7 * k + 6],
            recv_sem=recv_sems.at[7 * k + 6], device_id=(x, y, 1 - c), device_id_type=MESH))
        recvs.append(pltpu.make_async_remote_copy(
            src_ref=part.at[2 * x + y], dst_ref=land.at[4 * x + 2 * y + 1 - c],
            send_sem=send_sems.at[7 * k + 6], recv_sem=recv_sems.at[7 * k + 6], device_id=(x, y, 1 - c),
            device_id_type=MESH))
    return sends, recvs


def _reduce_scatter_start(parts, name):
    n = len(parts)
    parts = [p.reshape(N_CHIPS, p.shape[0] // N_CHIPS, p.shape[1]) for p in parts]

    def body(*refs):
        part_refs, land_refs = refs[:n], refs[n:2 * n]
        send_sems, recv_sems = refs[2 * n:2 * n + 2]
        token = refs[-1]
        sends, _ = _scatter_copies(part_refs, land_refs, send_sems, recv_sems)
        for cp in sends:
            cp.start()
        token[...] = jnp.zeros_like(token)

    lands = [lax.empty((N_DEV, p.shape[1], p.shape[2]), p.dtype) for p in parts]
    hbm = lambda a: pltpu.HBM(a.shape, a.dtype)
    res = pl.pallas_call(
        body, name=name,
        out_shape=(pltpu.SemaphoreType.DMA((7 * n,)), pltpu.SemaphoreType.DMA((7 * n,)), *map(hbm, parts),
                   *map(hbm, lands), jax.ShapeDtypeStruct((8, LANES), F32)),
        in_specs=[ANY] * (2 * n), out_specs=(SEM, SEM, *([ANY] * (2 * n)), pl.BlockSpec(memory_space=pltpu.VMEM)),
        input_output_aliases={i: 2 + i for i in range(2 * n)}, compiler_params=SPLIT_COPY,
    )(*[pltpu.with_memory_space_constraint(a, pltpu.HBM) for a in parts + lands])
    return res[:-1], res[-1]


def _reduce_scatter_wait(handle, after, name):
    after = list(after) if isinstance(after, (list, tuple)) else [after]
    send_sems, recv_sems = handle[:2]
    n = (len(handle) - 2) // 2
    parts, lands = handle[2:2 + n], handle[2 + n:]

    def body(*refs):
        part_refs, land_refs = refs[:n], refs[n:2 * n]
        send_sems, recv_sems = refs[2 * n:2 * n + 2]
        sends, recvs = _scatter_copies(part_refs, land_refs, send_sems, recv_sems)
        for cp in sends:
            cp.wait_send()
        for cp in recvs:
            cp.wait_recv()

    hbm = lambda a: pltpu.HBM(a.shape, a.dtype)
    res = pl.pallas_call(
        body, name=name, out_shape=tuple(map(hbm, list(parts) + list(lands))),
        in_specs=[ANY] * (2 * n) + [SEM, SEM] + [ANY] * len(after), out_specs=tuple([ANY] * (2 * n)),
        input_output_aliases={i: i for i in range(2 * n)}, compiler_params=SPLIT_COPY,
    )(*parts, *lands, send_sems, recv_sems, *after)
    return list(zip(res[:n], res[n:]))


def _adamw_of_shares(w, own, land, m, v, name, shares_transposed=False):
    rows, cols = w.shape
    tm = rows // 4 if rows % 32 == 0 and not shares_transposed else rows
    share_tile = (cols, rows) if shares_transposed else (tm, cols)
    x, y, c = lax.axis_index("x"), lax.axis_index("y"), lax.axis_index("c")
    where = jnp.stack([2 * x + y, 4 * x + 2 * y + c]).astype(jnp.int32)

    def body(where_ref, w_ref, own_ref, land_ref, m_ref, v_ref, g_ref, d_ref, nm_ref, nv_ref):
        me = where_ref[1]
        g_ = jnp.zeros(share_tile, F32)
        for dev in range(N_DEV):
            g_ = g_ + jnp.where(me == dev, own_ref[0], land_ref[dev]).astype(F32)
        if shares_transposed:
            g_ = g_.T
        m_ = ADAM_B1 * m_ref[...] + (1.0 - ADAM_B1) * g_
        v_ = ADAM_B2 * v_ref[...] + (1.0 - ADAM_B2) * (g_ * g_)
        m_hat = m_ / (1.0 - ADAM_B1 ** ADAM_STEP)
        v_hat = v_ / (1.0 - ADAM_B2 ** ADAM_STEP)
        g_ref[...] = g_
        d_ref[...] = -ADAM_LR * (m_hat / (jnp.sqrt(v_hat) + ADAM_EPS) + ADAM_WD * w_ref[...])
        nm_ref[...] = m_
        nv_ref[...] = v_

    tile = pl.BlockSpec((tm, cols), lambda i, where_ref: (i, 0))
    spec = pltpu.PrefetchScalarGridSpec(
        num_scalar_prefetch=1, grid=(rows // tm,),
        in_specs=[tile, pl.BlockSpec((1,) + share_tile, lambda i, where_ref: (where_ref[0], i, 0)),
                  pl.BlockSpec((N_DEV,) + share_tile, lambda i, where_ref: (0, i, 0)), tile, tile],
        out_specs=[tile] * 4)
    return pl.pallas_call(
        body, name=name, grid_spec=spec, out_shape=[jax.ShapeDtypeStruct(w.shape, F32)] * 4,
        compiler_params=_params(("arbitrary",)),
    )(where, w, own, land, m, v)


def _pack_small(values):
    flat = jnp.concatenate([values[n].reshape(-1).astype(F32) for n in SMALL])
    return jnp.pad(flat, (0, SMALL_ROWS * D_MODEL - flat.shape[0])).reshape(SMALL_ROWS, D_MODEL)


def _unpack_small(block, shapes):
    flat = block.reshape(-1)
    out, lo = {}, 0
    for n in SMALL:
        out[n] = flat[lo:lo + SMALL_SIZES[n]].reshape(shapes[n])
        lo += SMALL_SIZES[n]
    return out


def _after(token, a):
    return a + token[:1, :1].astype(a.dtype)


def kernel(x, p, mix_norm_g, w_in, sgu_w, sgu_b, sgu_norm_g, out_norm_a, out_norm_b, w_out, ffn_norm_g, w_gate, w_up, w_down, ple_norm_g, w_ple_gate, w_ple_proj, final_norm_g, loss_target, m_mix_norm_g, m_w_in, m_sgu_w, m_sgu_b, m_sgu_norm_g, m_out_norm_a, m_out_norm_b, m_w_out, m_ffn_norm_g, m_w_gate, m_w_up, m_w_down, m_ple_norm_g, m_w_ple_gate, m_w_ple_proj, m_final_norm_g, v_mix_norm_g, v_w_in, v_sgu_w, v_sgu_b, v_sgu_norm_g, v_out_norm_a, v_out_norm_b, v_w_out, v_ffn_norm_g, v_w_gate, v_w_up, v_w_down, v_ple_norm_g, v_w_ple_gate, v_w_ple_proj, v_final_norm_g):
    given = dict(locals())
    drop_lead = lambda a, lead: a.reshape(a.shape[lead:])
    xs, ps, target = drop_lead(x, 1), drop_lead(p, 2), drop_lead(loss_target, 1)
    s = xs.shape[0]
    shard = lambda name: drop_lead(given[name], 1)

    def slab_of(name):
        local = shard(name).astype(MXU_DTYPE)
        return local.T if SLAB_IS_TRANSPOSED[name] else local

    w_in_gather, token = _two_level_start(_stack_with_own(slab_of('w_in')))
    later = ['w_out', 'w_gate', 'w_up', 'w_down', 'w_ple_gate', 'w_ple_proj']
    later_slabs = [_stack_with_own(slab_of(n)) for n in later]
    cos_t, sin_t = (_after(token, table) for table in _rope_tables(s))
    tril = jnp.tril(jnp.ones((CHUNK, CHUNK), F32))
    w_tril = (sgu_w.reshape(HEADS_A, CHUNK, CHUNK) * tril).astype(MXU_DTYPE)
    w_tril_t = jnp.swapaxes(w_tril, 1, 2)
    w_in_gather = _two_level_forward(w_in_gather, [cos_t, sin_t, w_tril, w_tril_t] + later_slabs)
    w_in_t = _two_level_wait(w_in_gather)
    gather, token = _all_gather_start(later_slabs, w_in_t)

    bias = jnp.repeat(sgu_b.reshape(HEADS_A, CHUNK).T, HEAD_DIM, axis=1)
    g = {n: given[n].reshape(1, -1) for n in SMALL if n not in ('sgu_w', 'sgu_b')}

    uv, q, k, v, q1, k1, v1, hn1 = _in_fwd(xs, _after(token, g['mix_norm_g']), w_in_t, cos_t, sin_t)
    ya_n = _sgu_fwd(uv, w_tril, bias, g['sgu_norm_g'], g['out_norm_a'])
    branches = [_attn_fwd_local(q1, k1, v1)] + [_attn_fwd_branch(q, k, v, dil) for dil in DILATIONS[:-1]]
    y_b, lse = _attn_fwd_branch(q, k, v, DILATIONS[-1], earlier=branches)
    stacks = dict(zip(later, _all_gather_wait(gather, lse)))
    w_gate_t, w_up_t, w_pp_t = stacks['w_gate'], stacks['w_up'], stacks['w_ple_proj']
    h1, y_n = _out_fwd(ya_n, y_b, g['out_norm_b'], stacks['w_out'], xs)
    h2, gate, up, hn2 = _ffn_fwd(h1, g['ffn_norm_g'], w_gate_t, w_up_t, stacks['w_down'])
    loss, dh2, dz, dpp, hn3, d_ple_g, d_final_g = _ple_loss(
        h2, ps, target, g['ple_norm_g'], stacks['w_ple_gate'], w_pp_t, g['final_norm_g'])

    share = {}
    share['w_ple_gate'] = _wgrad(hn3, dz, "wgrad_ple_gate")
    share['w_ple_proj'] = _wgrad(dpp, ps, "wgrad_ple_proj")
    scatter_1, token = _reduce_scatter_start([share['w_ple_gate'], share['w_ple_proj']], "reduce_scatter_start_1")
    dh1, act, dgate, dup, d_ffn_g = _ffn_bwd(dh2, h1, gate, up, _after(token, g['ffn_norm_g']), stacks['w_down'],
                                             w_gate_t, w_up_t)
    share['w_down'] = _wgrad(act, dh2, "wgrad_down")
    share['w_gate'] = _wgrad(dgate, hn2, "wgrad_gate")
    share['w_up'] = _wgrad(dup, hn2, "wgrad_up")
    scatter_2, token = _reduce_scatter_start([share['w_down'], share['w_gate'], share['w_up']],
                                             "reduce_scatter_start_2")
    dya_n, dyb, d_out_b = _out_bwd(dh1, y_b, _after(token, g['out_norm_b']), stacks['w_out'])
    share['w_out'] = _wgrad(y_n, dh1, "wgrad_out")
    scatter_3, token = _reduce_scatter_start([share['w_out']], "reduce_scatter_start_3")
    grads = _attn_bwd_local(q1, k1, v1, dyb, y_b, lse)
    for dil in DILATIONS:
        grads = _attn_bwd_branch(q, k, v, dyb, y_b, lse, grads, dil)
    duv, d_sgu_w, d_sgu_b, d_sgu_g, d_out_a = _sgu_bwd(uv, dya_n, w_tril, w_tril_t, bias,
                                                       _after(token, g['sgu_norm_g']), g['out_norm_a'])
    dproj = _in_bwd_proj(duv, grads[0], grads[1], grads[2], cos_t, sin_t)
    share['w_in'] = _wgrad(dproj, hn1, "wgrad_in")
    scatter_4, token = _reduce_scatter_start([share['w_in']], "reduce_scatter_start_4")

    grads, deltas, new_m, new_v = {}, {}, {}, {}
    add_lead = lambda a: a.reshape((1,) + a.shape)

    done = {}

    def finish(names, handles, after, tag):
        landed = []
        for i, handle in enumerate(handles):
            landed += _reduce_scatter_wait(handle, after, "reduce_scatter_wait_%s%d" % (tag, i))
        for n, (own, land) in zip(names, landed):
            turn_shares = SLAB_IS_TRANSPOSED[n] and shard(n).shape[-1] % LANES == 0
            turn = (lambda a: a.T) if SLAB_IS_TRANSPOSED[n] and not turn_shares else (lambda a: a)
            res = _adamw_of_shares(turn(shard(n)), own, land, turn(shard("m_" + n)), turn(shard("v_" + n)),
                                   "adamw_" + n, turn_shares)
            grads[n], deltas[n], new_m[n], new_v[n] = (add_lead(turn(a)) for a in res)
            done[n] = res[0]

    grad_x, d_mix_g = _in_bwd_x(dproj, w_in_t, xs, _after(token, g['mix_norm_g']), dh1)

    gs = {'mix_norm_g': d_mix_g, 'sgu_w': d_sgu_w, 'sgu_b': d_sgu_b[:, :HEADS_A].T, 'sgu_norm_g': d_sgu_g,
          'out_norm_a': d_out_a, 'out_norm_b': d_out_b, 'ffn_norm_g': d_ffn_g, 'ple_norm_g': d_ple_g,
          'final_norm_g': d_final_g}
    gs_block = _pack_small(gs).at[SMALL_ROWS - 1, 0].set(loss[0, 0])
    to_all = jnp.broadcast_to(gs_block[None], (N_CHIPS,) + gs_block.shape).reshape(-1, D_MODEL)
    scatter_small, token = _reduce_scatter_start([to_all], "small_all_reduce_start")

    finish(['w_ple_gate', 'w_ple_proj', 'w_down', 'w_gate', 'w_up', 'w_out'], [scatter_1, scatter_2, scatter_3], token,
           "early")
    finish(['w_in'], [scatter_4], [done[n] for n in ('w_down', 'w_gate', 'w_up', 'w_out')], "last")
    (own, land), = _reduce_scatter_wait(scatter_small, done['w_in'], "small_all_reduce_wait")
    small = {n: given[n] for n in SMALL}
    small_res = _adamw_of_shares(_pack_small(small), own, land, _pack_small({n: given["m_" + n] for n in SMALL}),
                                 _pack_small({n: given["v_" + n] for n in SMALL}), "adamw_small")
    loss_out = small_res[0][SMALL_ROWS - 1, 0]
    small_shapes = {n: given[n].shape for n in SMALL}
    for res, blk in zip((grads, deltas, new_m, new_v), small_res):
        res.update(_unpack_small(blk, small_shapes))

    outs = [loss_out, add_lead(grad_x)]
    for res in (grads, deltas, new_m, new_v):
        outs += [res[n] for n in WEIGHT_NAMES]
    return tuple(outs)
```

```python
import functools
import itertools
import math

import jax
import jax.numpy as jnp
import numpy as np
from jax import lax
from jax.experimental import pallas as pl
from jax.experimental.pallas import tpu as pltpu

F32 = jnp.float32
MXU_DTYPE = jnp.bfloat16

D_MODEL = 1024
HEAD_DIM = 64
HEADS_A = 4
HEADS_B = 12
WIDTH_A = HEADS_A * HEAD_DIM
WIDTH_B = HEADS_B * HEAD_DIM
CHUNK = 128
BLOCK = 128
DILATIONS = (4, 16)
ROPE_THETA = 10000.0
D_FF = 2816
FF_HALF = D_FF // 2
FF_STRIPS = ((0, 1024), (1024, 2048), (2048, D_FF))
PLE_DIM = 256
IN_COLS = 2 * WIDTH_A + 3 * WIDTH_B
EPS = 1e-6
LANES = 128
N_CHIPS = 4
N_DEV = 8

ADAM_LR = 0.001
ADAM_B1 = 0.9
ADAM_B2 = 0.999
ADAM_EPS = 1e-08
ADAM_WD = 0.01
ADAM_STEP = 10

VMEM_LIMIT = 56 * 1024 * 1024

WEIGHT_NAMES = ['mix_norm_g', 'w_in', 'sgu_w', 'sgu_b', 'sgu_norm_g', 'out_norm_a', 'out_norm_b', 'w_out',
                'ffn_norm_g', 'w_gate', 'w_up', 'w_down', 'ple_norm_g', 'w_ple_gate', 'w_ple_proj', 'final_norm_g']
SMALL = ['mix_norm_g', 'sgu_w', 'sgu_b', 'sgu_norm_g', 'out_norm_a', 'out_norm_b', 'ffn_norm_g', 'ple_norm_g',
         'final_norm_g']
SMALL_SIZES = {'mix_norm_g': 1024, 'sgu_w': 65536, 'sgu_b': 512, 'sgu_norm_g': 256, 'out_norm_a': 256,
               'out_norm_b': 768, 'ffn_norm_g': 1024, 'ple_norm_g': 1024, 'final_norm_g': 1024}
SMALL_ROWS = 72


def _params(semantics=None):
    return pltpu.CompilerParams(dimension_semantics=semantics, vmem_limit_bytes=VMEM_LIMIT)


def _full(shape):
    nd = len(shape)
    return pl.BlockSpec(shape, lambda i: (0,) * nd, pipeline_mode=pl.Buffered(1))


def _rows(tm, width):
    return pl.BlockSpec((tm, width), lambda i: (i, 0))


def _rms_stats(x):
    r = lax.rsqrt(jnp.mean(x * x, axis=-1, keepdims=True) + EPS)
    return x * r, r


def _rms_bwd(dn, n, r):
    return r * (dn - n * jnp.mean(dn * n, axis=-1, keepdims=True))


def _dot(a, b):
    return jnp.dot(a, b, preferred_element_type=F32)


def _dot_nt(a, b):
    return lax.dot_general(a, b, (((1,), (1,)), ((), ())), preferred_element_type=F32)


def _dot_tn(a, b):
    return lax.dot_general(a, b, (((0,), (0,)), ((), ())), preferred_element_type=F32)


def _gelu_parts(x):
    c = math.sqrt(2.0 / math.pi)
    t = jnp.tanh(c * (x + 0.044715 * x * x * x))
    return 0.5 * x * (1.0 + t), t


def _gelu_grad(x, t):
    c = math.sqrt(2.0 / math.pi)
    return 0.5 * (1.0 + t) + 0.5 * x * (1.0 - t * t) * c * (1.0 + 3.0 * 0.044715 * x * x)


def _half_masks(dtype):
    lane = lax.broadcasted_iota(jnp.int32, (BLOCK, LANES), 1)
    lo = (lane < HEAD_DIM).astype(F32)
    return lo.astype(dtype), (1.0 - lo).astype(dtype)


def _rope_partner(t):
    lane = lax.broadcasted_iota(jnp.int32, t.shape, 1)
    first_half = (lane % HEAD_DIM) < (HEAD_DIM // 2)
    return jnp.where(first_half, pltpu.roll(t, LANES - HEAD_DIM // 2, 1), pltpu.roll(t, HEAD_DIM // 2, 1))


PAIRS_ABREAST = 2
RESIDUES_PER_STEP = 4
L_BLOCK = 256
L_GROUP = 16


def _store_l256(scr, out_ref, cols, value, chunk_ref=None):
    tm = value.shape[0]
    half = L_GROUP // 2
    scr[...] = value
    for blk in range(tm // L_BLOCK):
        pieces = [scr[pl.ds(blk * L_BLOCK + r, L_GROUP, stride=L_GROUP), :] for r in range(L_GROUP)]
        for r, piece in enumerate(pieces):
            lo = blk * L_BLOCK + r * L_GROUP
            out_ref[lo:lo + L_GROUP, cols] = piece.astype(out_ref.dtype)
        if chunk_ref is not None:
            for chunk in range(L_BLOCK // BLOCK):
                for r in range(0, L_GROUP, 2):
                    lo = blk * L_BLOCK + chunk * BLOCK + r * half
                    both = [p[chunk * half:(chunk + 1) * half] for p in pieces[r:r + 2]]
                    chunk_ref[lo:lo + L_GROUP, cols] = jnp.concatenate(both, axis=0).astype(chunk_ref.dtype)


def _load_l256(col_refs, tm):
    cols = []
    for ref in col_refs:
        pieces = [ref[pl.ds(blk * L_BLOCK + i, L_GROUP, stride=L_GROUP), :]
                  for blk in range(tm // L_BLOCK) for i in range(L_GROUP)]
        cols.append(jnp.concatenate(pieces, axis=0))
    return jnp.concatenate(cols, axis=1)


def _col_specs(tm, width):
    return [pl.BlockSpec((tm, LANES), lambda i, j=j: (i, j)) for j in range(width // LANES)]


def _in_fwd(x, g_mix, w_in_t, cos_t, sin_t):
    s = x.shape[0]
    tm = 512

    def body(x_ref, g_ref, wt_ref, cos_ref, sin_ref, uv_ref, q_ref, k_ref, v_ref, q1_ref, k1_ref, v1_ref, hn_ref,
             *scrs):
        n, _ = _rms_stats(x_ref[...])
        hn = (n * g_ref[...]).astype(MXU_DTYPE)
        hn_ref[...] = hn
        cos = cos_ref[...]
        sin = sin_ref[...]
        strip = 2 * LANES
        for j in range(IN_COLS // strip):
            proj = _dot_nt(hn, wt_ref[j * strip:(j + 1) * strip, :])
            lo = j * strip - 2 * WIDTH_A
            if lo < 0:
                uv_ref[:, j * strip:(j + 1) * strip] = proj
                continue
            which, lo = divmod(lo, WIDTH_B)
            for i in range(strip // LANES):
                t = proj[:, i * LANES:(i + 1) * LANES]
                cols = slice(lo + i * LANES, lo + (i + 1) * LANES)
                scr = scrs[i]
                if which == 0:
                    _store_l256(scr, q_ref, cols, (t * cos + _rope_partner(t) * sin) * (HEAD_DIM ** -0.5), q1_ref)
                elif which == 1:
                    _store_l256(scr, k_ref, cols, t * cos + _rope_partner(t) * sin, k1_ref)
                else:
                    _store_l256(scr, v_ref, cols, t, v1_ref)

    return pl.pallas_call(
        body, name="in_fwd", grid=(s // tm,), scratch_shapes=[pltpu.VMEM((tm, LANES), F32)] * 2,
        in_specs=[_rows(tm, D_MODEL), _full((1, D_MODEL)), _full((IN_COLS, D_MODEL)), _rows(tm, LANES),
                  _rows(tm, LANES)],
        out_specs=[_rows(tm, 2 * WIDTH_A)] + [_rows(tm, WIDTH_B)] * 6 + [_rows(tm, D_MODEL)],
        out_shape=[jax.ShapeDtypeStruct((s, 2 * WIDTH_A), F32)] + [jax.ShapeDtypeStruct((s, WIDTH_B), MXU_DTYPE)] * 6
        + [jax.ShapeDtypeStruct((s, D_MODEL), MXU_DTYPE)],
        compiler_params=_params(("arbitrary",)),
    )(x, g_mix, w_in_t, cos_t, sin_t)


class _Branch:
    def __init__(self, dil, s):
        i = np.arange(L_GROUP)
        self.res = RESIDUES_PER_STEP
        if dil == 16:
            nblk = BLOCK // 16
            self.grid = (16 // self.res, s // (L_BLOCK * nblk))
            self.shape = (nblk, 1, self.res, L_GROUP)
            self.index = lambda r, n: (n, r // (4 // self.res), r % (4 // self.res), 0, 0)
            pos = (np.arange(nblk)[:, None] * 16 + i[None, :]).reshape(-1)
        else:
            nblk = BLOCK // 64
            self.grid = (4 // self.res, s // (L_BLOCK * nblk))
            self.shape = (nblk, 4, self.res, L_GROUP)
            self.index = lambda r, n: (n, 0, r, 0, 0)
            pos = (np.arange(nblk)[:, None, None] * 64 + np.arange(4)[None, :, None]
                   + 4 * i[None, None, :]).reshape(-1)
        self.qn = pos.shape[0]
        self.nb = self.grid[1]
        dist = pos[:, None] - np.concatenate([pos - self.qn, pos])[None, :]
        band = (dist >= 0) & (dist <= BLOCK)
        start = band & (np.arange(2 * self.qn)[None, :] >= self.qn)
        self.bias = np.where(np.stack([band, start]), 0.0, -np.inf).astype(np.float32)

    def view(self, a):
        return a.reshape(a.shape[0] // L_BLOCK, 4, 4, L_GROUP, a.shape[1])

    def spec(self, w, step=lambda n: n):
        return pl.BlockSpec(self.shape + (w,), lambda r, n: self.index(r, step(n)))

    def bias_spec(self, step=lambda n: n):
        return pl.BlockSpec((1, self.qn, 2 * self.qn), lambda r, n: (jnp.where(step(n) == 0, 1, 0), 0, 0))

    def load(self, ref, cols=slice(None), j=0):
        x = ref[:, :, j, :, cols]
        return x.reshape(self.qn, x.shape[-1])

    def store(self, ref, cols, value, j=0):
        ref[:, :, j, :, cols] = value.reshape(self.shape[:2] + (L_GROUP, value.shape[-1]))


def _attn_fwd_branch(q, k, v, dil, earlier=()):
    s = q.shape[0]
    br = _Branch(dil, s)
    qn = br.qn
    nearly = len(earlier)

    def body(bias_ref, q_ref, kc_ref, kp_ref, vc_ref, vp_ref, *refs):
        early_refs, (o_ref, lse_ref) = refs[:2 * nearly], refs[2 * nearly:]
        bias2 = jnp.concatenate([bias_ref[0], bias_ref[0]], axis=0)
        lo = lax.broadcasted_iota(jnp.int32, (qn, LANES), 1) < HEAD_DIM
        mask_lo = lo.astype(F32).astype(MXU_DTYPE)
        for j, hp in itertools.product(range(br.res), range(HEADS_B // 2)):
            cols = slice(hp * LANES, (hp + 1) * LANES)
            qp = br.load(q_ref, cols, j)
            kcat = jnp.concatenate([br.load(kp_ref, cols, j), br.load(kc_ref, cols, j)], axis=0)
            vcat = jnp.concatenate([br.load(vp_ref, cols, j), br.load(vc_ref, cols, j)], axis=0)
            sc = _dot_nt(jnp.concatenate([qp * mask_lo, qp * (1 - mask_lo)], axis=0), kcat) + bias2
            m = jnp.max(sc, axis=1, keepdims=True)
            p = jnp.exp(sc - m)
            l = jnp.sum(p, axis=1, keepdims=True)
            out = _dot(p.astype(MXU_DTYPE), vcat) / l
            lse = m + jnp.log(l)
            outs = [br.load(r, cols, j) for r in early_refs[:nearly]] + [jnp.where(lo, out[:qn], out[qn:])]
            lses = [br.load(r, cols, j) for r in early_refs[nearly:]] + [jnp.where(lo, lse[:qn], lse[qn:])]
            if nearly:
                top = functools.reduce(jnp.maximum, lses)
                ws = [jnp.exp(x - top) for x in lses]
                den = functools.reduce(jnp.add, ws)
                outs = [functools.reduce(jnp.add, [w * o for w, o in zip(ws, outs)]) / den]
                lses = [top + jnp.log(den)]
            br.store(o_ref, cols, outs[0], j)
            br.store(lse_ref, cols, lses[0], j)

    before = lambda n: jnp.maximum(n - 1, 0)
    res = pl.pallas_call(
        body, name="attn_fwd_d%d" % dil, grid=br.grid,
        in_specs=[br.bias_spec(), br.spec(WIDTH_B), br.spec(WIDTH_B), br.spec(WIDTH_B, before), br.spec(WIDTH_B),
                  br.spec(WIDTH_B, before)] + [br.spec(WIDTH_B)] * (2 * nearly),
        out_specs=[br.spec(WIDTH_B), br.spec(WIDTH_B)],
        out_shape=[jax.ShapeDtypeStruct((s // L_BLOCK, 4, 4, L_GROUP, WIDTH_B), F32)] * 2,
        compiler_params=_params(("arbitrary", "arbitrary")),
    )(jnp.asarray(br.bias), br.view(q), br.view(k), br.view(k), br.view(v), br.view(v),
      *[br.view(o) for o, _ in earlier], *[br.view(x) for _, x in earlier])
    return tuple(a.reshape(s, WIDTH_B) for a in res)


LOCAL_CHUNKS = 4


def _local_bias():
    row = np.arange(BLOCK)
    pos = L_GROUP * (row % (L_GROUP // 2)) + row // (L_GROUP // 2)
    dist = pos[:, None] - np.concatenate([pos - BLOCK, pos])[None, :]
    band = (dist >= 0) & (dist <= BLOCK)
    start = band & (np.arange(2 * BLOCK)[None, :] >= BLOCK)
    return np.where(np.stack([band, start]), 0.0, -np.inf).astype(np.float32)


def _chunk_view(a):
    return a.reshape(a.shape[0] // L_BLOCK, L_GROUP, 2, L_GROUP // 2, a.shape[1])


def _chunk_of(ref, j, cols=slice(None)):
    x = ref[j // 2, :, j % 2, :, cols]
    return x.reshape(BLOCK, x.shape[-1])


def _put_chunk(ref, j, cols, value):
    ref[j // 2, :, j % 2, :, cols] = value.reshape(L_GROUP, L_GROUP // 2, value.shape[-1])


def _local_keys(cur_ref, before_ref, j, cols):
    here = slice(j * BLOCK, (j + 1) * BLOCK)
    before = before_ref[:, cols] if j == 0 else cur_ref[(j - 1) * BLOCK:j * BLOCK, cols]
    return jnp.concatenate([before, cur_ref[here, cols]], axis=0)


def _local_specs(s, step=lambda n: n):
    rows = LOCAL_CHUNKS * BLOCK
    cur = pl.BlockSpec((rows, WIDTH_B), lambda n: (step(n), 0))
    before = pl.BlockSpec((BLOCK, WIDTH_B), lambda n: (jnp.maximum(LOCAL_CHUNKS * step(n) - 1, 0), 0))
    return [cur, cur, before, cur, before]


def _attn_fwd_local(q1, k1, v1):
    s = q1.shape[0]
    rows = LOCAL_CHUNKS * BLOCK
    qn = BLOCK

    def body(bias_ref, q_ref, kc_ref, kp_ref, vc_ref, vp_ref, o_ref, lse_ref):
        first = jnp.where(pl.program_id(0) == 0, bias_ref[1], bias_ref[0])
        biases = [jnp.concatenate([b, b], axis=0) for b in (first, bias_ref[0])]
        lo = lax.broadcasted_iota(jnp.int32, (qn, LANES), 1) < HEAD_DIM
        mask_lo = lo.astype(F32).astype(MXU_DTYPE)
        for j, hp in itertools.product(range(LOCAL_CHUNKS), range(HEADS_B // 2)):
            cols = slice(hp * LANES, (hp + 1) * LANES)
            qp = q_ref[j * BLOCK:(j + 1) * BLOCK, cols]
            kcat = _local_keys(kc_ref, kp_ref, j, cols)
            vcat = _local_keys(vc_ref, vp_ref, j, cols)
            sc = _dot_nt(jnp.concatenate([qp * mask_lo, qp * (1 - mask_lo)], axis=0), kcat) + biases[min(j, 1)]
            m = jnp.max(sc, axis=1, keepdims=True)
            p = jnp.exp(sc - m)
            l = jnp.sum(p, axis=1, keepdims=True)
            out = _dot(p.astype(MXU_DTYPE), vcat) / l
            lse = m + jnp.log(l)
            _put_chunk(o_ref, j, cols, jnp.where(lo, out[:qn], out[qn:]))
            _put_chunk(lse_ref, j, cols, jnp.where(lo, lse[:qn], lse[qn:]))

    out_spec = pl.BlockSpec((LOCAL_CHUNKS // 2, L_GROUP, 2, L_GROUP // 2, WIDTH_B), lambda n: (n, 0, 0, 0, 0))
    res = pl.pallas_call(
        body, name="attn_fwd_d1", grid=(s // rows,),
        in_specs=[_full((2, BLOCK, 2 * BLOCK))] + _local_specs(s), out_specs=[out_spec] * 2,
        out_shape=[jax.ShapeDtypeStruct((s // L_BLOCK, L_GROUP, 2, L_GROUP // 2, WIDTH_B), F32)] * 2,
        compiler_params=_params(("arbitrary",)),
    )(jnp.asarray(_local_bias()), q1, k1, k1, v1, v1)
    return tuple(a.reshape(s, WIDTH_B) for a in res)


def _attn_bwd_local(q1, k1, v1, do, o, lse):
    s = q1.shape[0]
    rows = LOCAL_CHUNKS * BLOCK
    nsteps = s // rows
    qn = BLOCK

    def body(bias_ref, q_ref, kc_ref, kp_ref, vc_ref, vp_ref, do_ref, o_ref, lse_ref, dq_ref, dk_ref, dv_ref,
             dk_buf, dv_buf):
        n = pl.program_id(0)

        @pl.when(n == 0)
        def _():
            dk_buf[...] = jnp.zeros_like(dk_buf)
            dv_buf[...] = jnp.zeros_like(dv_buf)

        @pl.when(n < nsteps)
        def _():
            first = jnp.where(n == 0, bias_ref[1], bias_ref[0])
            biases = [jnp.concatenate([b, b], axis=0) for b in (first, bias_ref[0])]
            lo = lax.broadcasted_iota(jnp.int32, (qn, LANES), 1) < HEAD_DIM
            mask_f = lo.astype(F32)
            mask_lo = mask_f.astype(MXU_DTYPE)
            dk_buf[LOCAL_CHUNKS:] = jnp.zeros((LOCAL_CHUNKS, qn, WIDTH_B), F32)
            dv_buf[LOCAL_CHUNKS:] = jnp.zeros((LOCAL_CHUNKS, qn, WIDTH_B), F32)

            def prepare(j, hp):
                cols = slice(hp * LANES, (hp + 1) * LANES)
                qp = q_ref[j * BLOCK:(j + 1) * BLOCK, cols]
                dop = _chunk_of(do_ref, j, cols)
                prod = dop * _chunk_of(o_ref, j, cols)
                prod_lo = prod * mask_f
                lse = _chunk_of(lse_ref, j, cols)
                return dict(
                    j=j, cols=cols, kcat=_local_keys(kc_ref, kp_ref, j, cols), vcat=_local_keys(vc_ref, vp_ref, j, cols),
                    qs=jnp.concatenate([qp * mask_lo, qp * (1 - mask_lo)], axis=0),
                    dos=jnp.concatenate([dop * mask_f, dop * (1.0 - mask_f)], axis=0).astype(MXU_DTYPE),
                    delta=jnp.concatenate([jnp.sum(prod_lo, axis=1, keepdims=True),
                                           jnp.sum(prod - prod_lo, axis=1, keepdims=True)], axis=0),
                    lse2=jnp.concatenate([lse[:, :1], lse[:, HEAD_DIM:HEAD_DIM + 1]], axis=0))

            def scores(t):
                t['sc'] = _dot_nt(t['qs'], t['kcat'])
                t['dp'] = _dot_nt(t['dos'], t['vcat'])

            def softmax(t):
                p = jnp.exp(t['sc'] + biases[min(t['j'], 1)] - t['lse2'])
                t['ds'] = (p * (t['dp'] - t['delta'])).astype(MXU_DTYPE)
                t['p'] = p.astype(MXU_DTYPE)

            def gradients(t):
                t['dvc'] = _dot_tn(t['p'], t['dos'])
                t['dkc'] = _dot_tn(t['ds'], t['qs'])
                t['dq2'] = _dot(t['ds'], t['kcat'])

            def store(t):
                j, cols = t['j'], t['cols']
                _put_chunk(dq_ref, j, cols, jnp.where(lo, t['dq2'][:qn], t['dq2'][qn:]))
                for buf, both in ((dk_buf, t['dkc']), (dv_buf, t['dvc'])):
                    buf[LOCAL_CHUNKS + j - 1, :, cols] += both[:qn]
                    buf[LOCAL_CHUNKS + j, :, cols] += both[qn:]

            for j, first_pair in itertools.product(range(LOCAL_CHUNKS), range(0, HEADS_B // 2, PAIRS_ABREAST)):
                group = [prepare(j, hp) for hp in range(first_pair, first_pair + PAIRS_ABREAST)]
                for stage in (scores, softmax, gradients, store):
                    for t in group:
                        stage(t)

        for j in range(LOCAL_CHUNKS):
            _put_chunk(dk_ref, j, slice(None), dk_buf[j])
            _put_chunk(dv_ref, j, slice(None), dv_buf[j])
        dk_buf[:LOCAL_CHUNKS] = dk_buf[LOCAL_CHUNKS:]
        dv_buf[:LOCAL_CHUNKS] = dv_buf[LOCAL_CHUNKS:]

    cur = lambda n: jnp.minimum(n, nsteps - 1)
    late = lambda n: jnp.maximum(n - 1, 0)
    view_spec = lambda step: pl.BlockSpec((LOCAL_CHUNKS // 2, L_GROUP, 2, L_GROUP // 2, WIDTH_B),
                                          lambda n: (step(n), 0, 0, 0, 0))
    res = pl.pallas_call(
        body, name="attn_bwd_d1", grid=(nsteps + 1,),
        in_specs=[_full((2, BLOCK, 2 * BLOCK))] + _local_specs(s, cur) + [view_spec(cur)] * 3,
        out_specs=[view_spec(cur), view_spec(late), view_spec(late)],
        out_shape=[jax.ShapeDtypeStruct((s // L_BLOCK, L_GROUP, 2, L_GROUP // 2, WIDTH_B), F32)] * 3,
        scratch_shapes=[pltpu.VMEM((2 * LOCAL_CHUNKS, qn, WIDTH_B), F32)] * 2,
        compiler_params=_params(("arbitrary",)),
    )(jnp.asarray(_local_bias()), q1, k1, k1, v1, v1, _chunk_view(do), _chunk_view(o), _chunk_view(lse))
    return tuple(a.reshape(s, WIDTH_B) for a in res)


def _sgu_forward_tile(uv, w_ref, bias, g_sgu):
    tm = uv.shape[0]
    u = uv[:, :WIDTH_A]
    v = uv[:, WIDTH_A:]
    ug, tu = _gelu_parts(u)
    vg, tv = _gelu_parts(v)
    mu = jnp.mean(vg, axis=-1, keepdims=True)
    vc = vg - mu
    rs = lax.rsqrt(jnp.mean(vc * vc, axis=-1, keepdims=True) + EPS)
    vhat = vc * rs
    vn = (vhat * g_sgu).astype(MXU_DTYPE)
    masks = _half_masks(MXU_DTYPE)
    chunks = []
    for c in range(tm // CHUNK):
        rows = slice(c * CHUNK, (c + 1) * CHUNK)
        groups = []
        for gp in range(2):
            vn_g = vn[rows, gp * LANES:(gp + 1) * LANES]
            groups.append(_dot(w_ref[2 * gp], vn_g * masks[0]) + _dot(w_ref[2 * gp + 1], vn_g * masks[1]))
        chunks.append(jnp.concatenate(groups, axis=1) + bias)
    mixed = jnp.concatenate(chunks, axis=0)
    return dict(u=u, v=v, ug=ug, tu=tu, tv=tv, rs=rs, vhat=vhat, vn=vn, mixed=mixed, ya=ug * mixed)


def _sgu_fwd(uv, w_tril, bias, g_sgu, g_a):
    s = uv.shape[0]
    tm = 512

    def body(uv_ref, w_ref, b_ref, gs_ref, ga_ref, o_ref):
        t = _sgu_forward_tile(uv_ref[...], w_ref, b_ref[...], gs_ref[...])
        n, _ = _rms_stats(t['ya'])
        o_ref[...] = (n * ga_ref[...]).astype(MXU_DTYPE)

    return pl.pallas_call(
        body, name="sgu_fwd", grid=(s // tm,),
        in_specs=[_rows(tm, 2 * WIDTH_A), _full((HEADS_A, CHUNK, CHUNK)), _full((CHUNK, WIDTH_A)),
                  _full((1, WIDTH_A)), _full((1, WIDTH_A))],
        out_specs=_rows(tm, WIDTH_A), out_shape=jax.ShapeDtypeStruct((s, WIDTH_A), MXU_DTYPE),
        compiler_params=_params(("arbitrary",)),
    )(uv, w_tril, bias, g_sgu, g_a)


def _out_fwd(ya_n, y_b, g_b, w_out, x):
    s = x.shape[0]
    tm = 512
    nc = WIDTH_B // LANES

    def body(ya_ref, *refs):
        yb_refs = refs[:nc]
        g_ref, w_ref, x_ref, h_ref, yn_ref = refs[nc:]
        n, _ = _rms_stats(_load_l256(yb_refs, tm))
        yn = jnp.concatenate([ya_ref[...], (n * g_ref[...]).astype(MXU_DTYPE)], axis=1)
        yn_ref[...] = yn
        h_ref[...] = x_ref[...] + _dot(yn, w_ref[...])

    return pl.pallas_call(
        body, name="out_fwd", grid=(s // tm,),
        in_specs=[_rows(tm, WIDTH_A)] + _col_specs(tm, WIDTH_B) + [_full((1, WIDTH_B)), _full((D_MODEL, D_MODEL)),
                                                                 _rows(tm, D_MODEL)],
        out_specs=[_rows(tm, D_MODEL), _rows(tm, D_MODEL)],
        out_shape=[jax.ShapeDtypeStruct((s, D_MODEL), F32), jax.ShapeDtypeStruct((s, D_MODEL), MXU_DTYPE)],
        compiler_params=_params(("arbitrary",)),
    )(ya_n, *([y_b] * nc), g_b, w_out, x)


def _ffn_fwd(h1, g_ffn, w_gate_t, w_up_t, w_down):
    s = h1.shape[0]
    tm = 512

    def body(h_ref, g_ref, wgt_ref, wut_ref, wd_ref, o_ref, gate_ref, up_ref, hn_ref):
        h = h_ref[...]
        n, _ = _rms_stats(h)
        hn = (n * g_ref[...]).astype(MXU_DTYPE)
        hn_ref[...] = hn
        strips = [dict(cols=slice(lo, hi)) for lo, hi in FF_STRIPS]

        def project(t):
            t['gate'] = _dot_nt(hn, wgt_ref[t['cols'], :])
            t['up'] = _dot_nt(hn, wut_ref[t['cols'], :])

        def activate(t):
            gate, up = t['gate'], t['up']
            gate_ref[:, t['cols']] = gate.astype(MXU_DTYPE)
            up_ref[:, t['cols']] = up.astype(MXU_DTYPE)
            t['act'] = (gate * jax.nn.sigmoid(gate) * up).astype(MXU_DTYPE)

        def down(t):
            return _dot(t['act'], wd_ref[t['cols'], :])

        out = h
        project(strips[0])
        for i, t in enumerate(strips):
            if i + 1 < len(strips):
                project(strips[i + 1])
            activate(t)
            out = out + down(t)
        o_ref[...] = out

    return pl.pallas_call(
        body, name="ffn_fwd", grid=(s // tm,),
        in_specs=[_rows(tm, D_MODEL), _full((1, D_MODEL)), _full((D_FF, D_MODEL)), _full((D_FF, D_MODEL)),
                  _full((D_FF, D_MODEL))],
        out_specs=[_rows(tm, D_MODEL), _rows(tm, D_FF), _rows(tm, D_FF), _rows(tm, D_MODEL)],
        out_shape=[jax.ShapeDtypeStruct((s, D_MODEL), F32), jax.ShapeDtypeStruct((s, D_FF), MXU_DTYPE),
                   jax.ShapeDtypeStruct((s, D_FF), MXU_DTYPE), jax.ShapeDtypeStruct((s, D_MODEL), MXU_DTYPE)],
        compiler_params=_params(("arbitrary",)),
    )(h1, g_ffn, w_gate_t, w_up_t, w_down)


def _ple_loss(h2, p, target, g_ple, w_pg, w_pp_t, g_final):
    s = h2.shape[0]
    tm = 512

    def body(h_ref, p_ref, t_ref, gp_ref, wg_ref, wpt_ref, gf_ref,
             loss_ref, dh_ref, dz_ref, dpp_ref, hn_ref, dgp_ref, dgf_ref):
        @pl.when(pl.program_id(0) == 0)
        def _():
            loss_ref[...] = jnp.zeros_like(loss_ref)
            dgp_ref[...] = jnp.zeros_like(dgp_ref)
            dgf_ref[...] = jnp.zeros_like(dgf_ref)

        h2t = h_ref[...]
        n2, r2 = _rms_stats(h2t)
        hn = (n2 * gp_ref[...]).astype(MXU_DTYPE)
        hn_ref[...] = hn
        gate = jax.nn.sigmoid(_dot(hn, wg_ref[...]))
        pp = _dot_nt(p_ref[...].astype(MXU_DTYPE), wpt_ref[...])
        h3 = h2t + gate * pp
        n3, r3 = _rms_stats(h3)
        diff = n3 * gf_ref[...] - t_ref[...]
        loss_ref[...] += jnp.full(loss_ref.shape, 0.5 * jnp.sum(diff * diff) / D_MODEL, F32)
        dy = diff * (1.0 / D_MODEL)
        dgf_ref[...] += jnp.sum(dy * n3, axis=0, keepdims=True)
        dh3 = _rms_bwd(dy * gf_ref[...], n3, r3)
        dpp_ref[...] = (dh3 * gate).astype(MXU_DTYPE)
        dz = (dh3 * pp * gate * (1.0 - gate)).astype(MXU_DTYPE)
        dz_ref[...] = dz
        dhn = _dot_nt(dz, wg_ref[...])
        dgp_ref[...] += jnp.sum(dhn * n2, axis=0, keepdims=True)
        dh_ref[...] = dh3 + _rms_bwd(dhn * gp_ref[...], n2, r2)

    return pl.pallas_call(
        body, name="ple_loss", grid=(s // tm,),
        in_specs=[_rows(tm, D_MODEL), _rows(tm, PLE_DIM), _rows(tm, D_MODEL), _full((1, D_MODEL)),
                  _full((D_MODEL, D_MODEL)), _full((D_MODEL, PLE_DIM)), _full((1, D_MODEL))],
        out_specs=[_full((1, LANES)), _rows(tm, D_MODEL), _rows(tm, D_MODEL), _rows(tm, D_MODEL),
                   _rows(tm, D_MODEL), _full((1, D_MODEL)), _full((1, D_MODEL))],
        out_shape=[jax.ShapeDtypeStruct((1, LANES), F32), jax.ShapeDtypeStruct((s, D_MODEL), F32),
                   jax.ShapeDtypeStruct((s, D_MODEL), MXU_DTYPE), jax.ShapeDtypeStruct((s, D_MODEL), MXU_DTYPE),
                   jax.ShapeDtypeStruct((s, D_MODEL), MXU_DTYPE), jax.ShapeDtypeStruct((1, D_MODEL), F32),
                   jax.ShapeDtypeStruct((1, D_MODEL), F32)],
        compiler_params=_params(("arbitrary",)),
    )(h2, p, target, g_ple, w_pg, w_pp_t, g_final)


def _ffn_bwd(dh2, h1, gate, up, g_ffn, w_down, w_gate_t, w_up_t):
    s = h1.shape[0]
    tm = 256

    def body(dh_ref, h_ref, gate_ref, up_ref, g_ref, wd_ref, wgt_ref, wut_ref,
             o_ref, act_ref, dg_ref, du_ref, dgn_ref):
        @pl.when(pl.program_id(0) == 0)
        def _():
            dgn_ref[...] = jnp.zeros_like(dgn_ref)

        dh = dh_ref[...]
        dhb = dh.astype(MXU_DTYPE)
        strips = [dict(cols=slice(lo, hi)) for lo, hi in FF_STRIPS]

        def back_down(t):
            t['dact'] = _dot_nt(dhb, wd_ref[t['cols'], :])

        def back_act(t):
            cols, dact = t['cols'], t['dact']
            g = gate_ref[:, cols].astype(F32)
            u = up_ref[:, cols].astype(F32)
            sg = jax.nn.sigmoid(g)
            silu = g * sg
            act_ref[:, cols] = (silu * u).astype(MXU_DTYPE)
            t['du'] = (dact * silu).astype(MXU_DTYPE)
            t['dg'] = (dact * u * sg * (1.0 + g * (1.0 - sg))).astype(MXU_DTYPE)
            du_ref[:, cols] = t['du']
            dg_ref[:, cols] = t['dg']

        def back_in(t):
            return _dot(t['dg'], wgt_ref[t['cols'], :]) + _dot(t['du'], wut_ref[t['cols'], :])

        dhn = jnp.zeros((tm, D_MODEL), F32)
        back_down(strips[0])
        for i, t in enumerate(strips):
            if i + 1 < len(strips):
                back_down(strips[i + 1])
            back_act(t)
            dhn = dhn + back_in(t)
        n, r = _rms_stats(h_ref[...])
        dgn_ref[...] += jnp.sum(dhn * n, axis=0, keepdims=True)
        o_ref[...] = dh + _rms_bwd(dhn * g_ref[...], n, r)

    return pl.pallas_call(
        body, name="ffn_bwd", grid=(s // tm,),
        in_specs=[_rows(tm, D_MODEL), _rows(tm, D_MODEL), _rows(tm, D_FF), _rows(tm, D_FF), _full((1, D_MODEL)),
                  _full((D_FF, D_MODEL)), _full((D_FF, D_MODEL)), _full((D_FF, D_MODEL))],
        out_specs=[_rows(tm, D_MODEL), _rows(tm, D_FF), _rows(tm, D_FF), _rows(tm, D_FF), _full((1, D_MODEL))],
        out_shape=[jax.ShapeDtypeStruct((s, D_MODEL), F32), jax.ShapeDtypeStruct((s, D_FF), MXU_DTYPE),
                   jax.ShapeDtypeStruct((s, D_FF), MXU_DTYPE), jax.ShapeDtypeStruct((s, D_FF), MXU_DTYPE),
                   jax.ShapeDtypeStruct((1, D_MODEL), F32)],
        compiler_params=_params(("arbitrary",)),
    )(dh2, h1, gate, up, g_ffn, w_down, w_gate_t, w_up_t)


def _out_bwd(dh1, y_b, g_b, w_out):
    s = dh1.shape[0]
    tm = 512
    nc = WIDTH_B // LANES

    def body(dh_ref, *refs):
        yb_refs = refs[:nc]
        g_ref, w_ref, dya_ref, dyb_ref, dg_ref, scr = refs[nc:]

        @pl.when(pl.program_id(0) == 0)
        def _():
            dg_ref[...] = jnp.zeros_like(dg_ref)

        dy = _dot_nt(dh_ref[...].astype(MXU_DTYPE), w_ref[...])
        dya_ref[...] = dy[:, :WIDTH_A]
        dyb = dy[:, WIDTH_A:]
        n, r = _rms_stats(_load_l256(yb_refs, tm))
        dg_ref[...] += jnp.sum(dyb * n, axis=0, keepdims=True)
        dyb_in = _rms_bwd(dyb * g_ref[...], n, r)
        for j in range(nc):
            cols = slice(j * LANES, (j + 1) * LANES)
            _store_l256(scr, dyb_ref, cols, dyb_in[:, cols])

    return pl.pallas_call(
        body, name="out_bwd", grid=(s // tm,), scratch_shapes=[pltpu.VMEM((tm, LANES), F32)],
        in_specs=[_rows(tm, D_MODEL)] + _col_specs(tm, WIDTH_B) + [_full((1, WIDTH_B)), _full((D_MODEL, D_MODEL))],
        out_specs=[_rows(tm, WIDTH_A), _rows(tm, WIDTH_B), _full((1, WIDTH_B))],
        out_shape=[jax.ShapeDtypeStruct((s, WIDTH_A), F32), jax.ShapeDtypeStruct((s, WIDTH_B), F32),
                   jax.ShapeDtypeStruct((1, WIDTH_B), F32)],
        compiler_params=_params(("arbitrary",)),
    )(dh1, *([y_b] * nc), g_b, w_out)


def _attn_bwd_branch(q, k, v, do, o, lse, grads, dil):
    s = q.shape[0]
    br = _Branch(dil, s)
    qn, nb = br.qn, br.nb
    first = grads is None

    def body(*refs):
        bias_ref, q_ref, kc_ref, kp_ref, vc_ref, vp_ref, do_ref, o_ref, lse_ref = refs[:9]
        if first:
            rest = refs[9:]
        else:
            dq_in, dk_in, dv_in = refs[9:12]
            rest = refs[12:]
        dq_ref, dk_ref, dv_ref, dk_carry, dv_carry = rest
        n = pl.program_id(1)

        @pl.when(n == 0)
        def _():
            dk_carry[...] = jnp.zeros_like(dk_carry)
            dv_carry[...] = jnp.zeros_like(dv_carry)

        @pl.when(n < nb)
        def _():
            bias2 = jnp.concatenate([bias_ref[0], bias_ref[0]], axis=0)
            lane = lax.broadcasted_iota(jnp.int32, (qn, LANES), 1)
            lo = lane < HEAD_DIM
            mask_f = lo.astype(F32)
            mask_lo = mask_f.astype(MXU_DTYPE)
            def prepare(j, hp):
                cols = slice(hp * LANES, (hp + 1) * LANES)
                qp = br.load(q_ref, cols, j)
                dop = br.load(do_ref, cols, j)
                prod = dop * br.load(o_ref, cols, j)
                prod_lo = prod * mask_f
                lse = br.load(lse_ref, cols, j)
                return dict(
                    j=j, cols=cols,
                    kcat=jnp.concatenate([br.load(kp_ref, cols, j), br.load(kc_ref, cols, j)], axis=0),
                    vcat=jnp.concatenate([br.load(vp_ref, cols, j), br.load(vc_ref, cols, j)], axis=0),
                    qs=jnp.concatenate([qp * mask_lo, qp * (1 - mask_lo)], axis=0),
                    dos=jnp.concatenate([dop * mask_f, dop * (1.0 - mask_f)], axis=0).astype(MXU_DTYPE),
                    delta=jnp.concatenate([jnp.sum(prod_lo, axis=1, keepdims=True),
                                           jnp.sum(prod - prod_lo, axis=1, keepdims=True)], axis=0),
                    lse2=jnp.concatenate([lse[:, :1], lse[:, HEAD_DIM:HEAD_DIM + 1]], axis=0))

            def scores(t):
                t['sc'] = _dot_nt(t['qs'], t['kcat'])
                t['dp'] = _dot_nt(t['dos'], t['vcat'])

            def softmax(t):
                p = jnp.exp(t['sc'] + bias2 - t['lse2'])
                t['ds'] = (p * (t['dp'] - t['delta'])).astype(MXU_DTYPE)
                t['p'] = p.astype(MXU_DTYPE)

            def gradients(t):
                t['dvc'] = _dot_tn(t['p'], t['dos'])
                t['dkc'] = _dot_tn(t['ds'], t['qs'])
                t['dq2'] = _dot(t['ds'], t['kcat'])

            def store(t):
                j, cols, dkc, dvc = t['j'], t['cols'], t['dkc'], t['dvc']
                dq = jnp.where(lo, t['dq2'][:qn], t['dq2'][qn:])
                dk_prev = dk_carry[j, :, cols] + dkc[:qn]
                dv_prev = dv_carry[j, :, cols] + dvc[:qn]
                if not first:
                    dq = dq + br.load(dq_in, cols, j)
                    dk_prev = dk_prev + br.load(dk_in, cols, j)
                    dv_prev = dv_prev + br.load(dv_in, cols, j)
                br.store(dq_ref, cols, dq, j)
                br.store(dk_ref, cols, dk_prev, j)
                br.store(dv_ref, cols, dv_prev, j)
                dk_carry[j, :, cols] = dkc[qn:]
                dv_carry[j, :, cols] = dvc[qn:]

            for j, first_pair in itertools.product(range(br.res), range(0, HEADS_B // 2, PAIRS_ABREAST)):
                group = [prepare(j, hp) for hp in range(first_pair, first_pair + PAIRS_ABREAST)]
                for stage in (scores, softmax, gradients, store):
                    for t in group:
                        stage(t)

        @pl.when(n == nb)
        def _():
            for j in range(br.res):
                dk_last = dk_carry[j]
                dv_last = dv_carry[j]
                if not first:
                    dk_last = dk_last + br.load(dk_in, slice(None), j)
                    dv_last = dv_last + br.load(dv_in, slice(None), j)
                br.store(dk_ref, slice(None), dk_last, j)
                br.store(dv_ref, slice(None), dv_last, j)

    cur = lambda n: jnp.minimum(n, nb - 1)
    before = lambda n: jnp.maximum(cur(n) - 1, 0)
    late = lambda n: jnp.maximum(n - 1, 0)
    in_specs = [br.bias_spec(cur), br.spec(WIDTH_B, cur), br.spec(WIDTH_B, cur), br.spec(WIDTH_B, before),
                br.spec(WIDTH_B, cur), br.spec(WIDTH_B, before), br.spec(WIDTH_B, cur), br.spec(WIDTH_B, cur),
                br.spec(WIDTH_B, cur)]
    args = [jnp.asarray(br.bias)] + [br.view(a) for a in (q, k, k, v, v, do, o, lse)]
    if not first:
        in_specs += [br.spec(WIDTH_B, cur), br.spec(WIDTH_B, late), br.spec(WIDTH_B, late)]
        args += [br.view(g) for g in grads]
    res = pl.pallas_call(
        body, name="attn_bwd_d%d" % dil, grid=(br.grid[0], nb + 1), in_specs=in_specs,
        out_specs=[br.spec(WIDTH_B, cur), br.spec(WIDTH_B, late), br.spec(WIDTH_B, late)],
        out_shape=[jax.ShapeDtypeStruct((s // L_BLOCK, 4, 4, L_GROUP, WIDTH_B), F32)] * 3,
        scratch_shapes=[pltpu.VMEM((br.res, qn, WIDTH_B), F32), pltpu.VMEM((br.res, qn, WIDTH_B), F32)],
        compiler_params=_params(("arbitrary", "arbitrary")),
    )(*args)
    return tuple(a.reshape(s, WIDTH_B) for a in res)


def _sgu_bwd(uv, dya_n, w_tril, w_tril_t, bias, g_sgu, g_a):
    s = uv.shape[0]
    tm = 512

    def body(uv_ref, dy_ref, w_ref, wt_ref, b_ref, gs_ref, ga_ref, duv_ref, dw_ref, db_ref, dgs_ref, dga_ref,
             db_acc):
        i = pl.program_id(0)

        @pl.when(i == 0)
        def _():
            dw_ref[...] = jnp.zeros_like(dw_ref)
            dgs_ref[...] = jnp.zeros_like(dgs_ref)
            dga_ref[...] = jnp.zeros_like(dga_ref)
            db_acc[...] = jnp.zeros_like(db_acc)

        t = _sgu_forward_tile(uv_ref[...], w_ref, b_ref[...], gs_ref[...])
        na, ra = _rms_stats(t['ya'])
        dyn = dy_ref[...]
        dga_ref[...] += jnp.sum(dyn * na, axis=0, keepdims=True)
        dya = _rms_bwd(dyn * ga_ref[...], na, ra)
        dug = dya * t['mixed']
        dmixed = dya * t['ug']
        dmb = dmixed.astype(MXU_DTYPE)
        masks = _half_masks(MXU_DTYPE)
        chunks = []
        db = jnp.zeros((CHUNK, WIDTH_A), F32)
        for c in range(tm // CHUNK):
            rows = slice(c * CHUNK, (c + 1) * CHUNK)
            db = db + dmixed[rows]
            groups = []
            for gp in range(2):
                cols = slice(gp * LANES, (gp + 1) * LANES)
                dm_g = dmb[rows, cols]
                vn_g = t['vn'][rows, cols]
                dvn_g = jnp.zeros((CHUNK, LANES), F32)
                for j in range(2):
                    dm_h = dm_g * masks[j]
                    dvn_g = dvn_g + _dot(wt_ref[2 * gp + j], dm_h)
                    dw_ref[2 * gp + j] += _dot_nt(dm_h, vn_g)
                groups.append(dvn_g)
            chunks.append(jnp.concatenate(groups, axis=1))
        db_acc[...] += db
        dvn = jnp.concatenate(chunks, axis=0)
        vhat = t['vhat']
        dgs_ref[...] += jnp.sum(dvn * vhat, axis=0, keepdims=True)
        dvh = dvn * gs_ref[...]
        dvg = t['rs'] * (dvh - jnp.mean(dvh, axis=-1, keepdims=True)
                         - vhat * jnp.mean(dvh * vhat, axis=-1, keepdims=True))
        duv_ref[:, :WIDTH_A] = (dug * _gelu_grad(t['u'], t['tu'])).astype(MXU_DTYPE)
        duv_ref[:, WIDTH_A:] = (dvg * _gelu_grad(t['v'], t['tv'])).astype(MXU_DTYPE)

        @pl.when(i == pl.num_programs(0) - 1)
        def _():
            lane_a = lax.broadcasted_iota(jnp.int32, (CHUNK, WIDTH_A), 1)
            lane = lax.broadcasted_iota(jnp.int32, (CHUNK, LANES), 1)
            acc = db_acc[...]
            out = jnp.zeros((CHUNK, LANES), F32)
            for h in range(HEADS_A):
                col = jnp.sum(jnp.where(lane_a // HEAD_DIM == h, acc, 0.0), axis=1, keepdims=True)
                out = jnp.where(lane == h, col, out)
            db_ref[...] = out
            causal = (lax.broadcasted_iota(jnp.int32, (CHUNK, CHUNK), 0)
                      >= lax.broadcasted_iota(jnp.int32, (CHUNK, CHUNK), 1))
            for h in range(HEADS_A):
                dw_ref[h] = jnp.where(causal, dw_ref[h], 0.0)

    return pl.pallas_call(
        body, name="sgu_bwd", grid=(s // tm,),
        in_specs=[_rows(tm, 2 * WIDTH_A), _rows(tm, WIDTH_A), _full((HEADS_A, CHUNK, CHUNK)),
                  _full((HEADS_A, CHUNK, CHUNK)), _full((CHUNK, WIDTH_A)), _full((1, WIDTH_A)),
                  _full((1, WIDTH_A))],
        out_specs=[_rows(tm, 2 * WIDTH_A), _full((HEADS_A, CHUNK, CHUNK)), _full((CHUNK, LANES)),
                   _full((1, WIDTH_A)), _full((1, WIDTH_A))],
        out_shape=[jax.ShapeDtypeStruct((s, 2 * WIDTH_A), MXU_DTYPE),
                   jax.ShapeDtypeStruct((HEADS_A, CHUNK, CHUNK), F32), jax.ShapeDtypeStruct((CHUNK, LANES), F32),
                   jax.ShapeDtypeStruct((1, WIDTH_A), F32), jax.ShapeDtypeStruct((1, WIDTH_A), F32)],
        scratch_shapes=[pltpu.VMEM((CHUNK, WIDTH_A), F32)],
        compiler_params=_params(("arbitrary",)),
    )(uv, dya_n, w_tril, w_tril_t, bias, g_sgu, g_a)


def _in_bwd_proj(duv, dq, dk, dv, cos_t, sin_t):
    s = duv.shape[0]
    tm = 512
    nc = WIDTH_B // LANES

    def body(duv_ref, *refs):
        dq_refs, dk_refs, dv_refs = refs[:nc], refs[nc:2 * nc], refs[2 * nc:3 * nc]
        cos_ref, sin_ref, dp_ref = refs[3 * nc:]
        cos = cos_ref[...]
        sin = sin_ref[...]
        dp_ref[:, :2 * WIDTH_A] = duv_ref[...]
        for i in range(nc):
            lo = 2 * WIDTH_A + i * LANES
            tq = _load_l256(dq_refs[i:i + 1], tm) * (HEAD_DIM ** -0.5)
            tk = _load_l256(dk_refs[i:i + 1], tm)
            dp_ref[:, lo:lo + LANES] = (tq * cos + _rope_partner(tq * sin)).astype(MXU_DTYPE)
            dp_ref[:, lo + WIDTH_B:lo + WIDTH_B + LANES] = (tk * cos + _rope_partner(tk * sin)).astype(MXU_DTYPE)
            dp_ref[:, lo + 2 * WIDTH_B:lo + 2 * WIDTH_B + LANES] = _load_l256(dv_refs[i:i + 1], tm).astype(MXU_DTYPE)

    return pl.pallas_call(
        body, name="in_bwd_proj", grid=(s // tm,),
        in_specs=[_rows(tm, 2 * WIDTH_A)] + 3 * _col_specs(tm, WIDTH_B) + [_rows(tm, LANES), _rows(tm, LANES)],
        out_specs=_rows(tm, IN_COLS), out_shape=jax.ShapeDtypeStruct((s, IN_COLS), MXU_DTYPE),
        compiler_params=_params(("arbitrary",)),
    )(duv, *([dq] * nc), *([dk] * nc), *([dv] * nc), cos_t, sin_t)


def _in_bwd_x(dproj, w_in_t, x, g_mix, dh1):
    s = x.shape[0]
    tm = 512

    def body(dp_ref, wt_ref, x_ref, g_ref, dh_ref, gx_ref, dg_ref):
        @pl.when(pl.program_id(0) == 0)
        def _():
            dg_ref[...] = jnp.zeros_like(dg_ref)

        dhn = _dot(dp_ref[...], wt_ref[...])
        n, r = _rms_stats(x_ref[...])
        dg_ref[...] += jnp.sum(dhn * n, axis=0, keepdims=True)
        gx_ref[...] = dh_ref[...] + _rms_bwd(dhn * g_ref[...], n, r)

    return pl.pallas_call(
        body, name="in_bwd_x", grid=(s // tm,),
        in_specs=[_rows(tm, IN_COLS), _full((IN_COLS, D_MODEL)), _rows(tm, D_MODEL), _full((1, D_MODEL)),
                  _rows(tm, D_MODEL)],
        out_specs=[_rows(tm, D_MODEL), _full((1, D_MODEL))],
        out_shape=[jax.ShapeDtypeStruct((s, D_MODEL), F32), jax.ShapeDtypeStruct((1, D_MODEL), F32)],
        compiler_params=_params(("arbitrary",)),
    )(dproj, w_in_t, x, g_mix, dh1)


def _wgrad(a, b, name):
    s, m = a.shape
    n = b.shape[1]
    bm = min(m, FF_HALF)
    ts = 1024
    nsteps = s // ts

    def body(a_ref, b_ref, o_ref, acc):
        kk = pl.program_id(1)

        @pl.when(kk == 0)
        def _():
            acc[...] = jnp.zeros_like(acc)

        acc[...] += _dot_tn(a_ref[...].astype(MXU_DTYPE), b_ref[...].astype(MXU_DTYPE))

        @pl.when(kk == nsteps - 1)
        def _():
            o_ref[...] = acc[...].astype(o_ref.dtype)

    return pl.pallas_call(
        body, name=name, grid=(m // bm, nsteps),
        in_specs=[pl.BlockSpec((ts, bm), lambda i, kk: (kk, i)), pl.BlockSpec((ts, n), lambda i, kk: (kk, 0))],
        out_specs=pl.BlockSpec((bm, n), lambda i, kk: (i, 0)), out_shape=jax.ShapeDtypeStruct((m, n), jnp.bfloat16),
        scratch_shapes=[pltpu.VMEM((bm, n), F32)],
        compiler_params=_params(("arbitrary", "arbitrary")),
    )(a, b)


def _rope_tables(s):
    half = HEAD_DIM // 2
    inv = ROPE_THETA ** (-jnp.arange(half, dtype=F32) / half)
    ang = jnp.arange(s, dtype=F32)[:, None] * jnp.tile(inv, LANES // half)[None, :]
    sign = jnp.tile(jnp.concatenate([-jnp.ones(half, F32), jnp.ones(half, F32)]), LANES // HEAD_DIM)
    return jnp.cos(ang), jnp.sin(ang) * sign[None, :]


MESH = pl.DeviceIdType.MESH
ANY = pl.BlockSpec(memory_space=pl.ANY)
SEM = pl.BlockSpec(memory_space=pltpu.SEMAPHORE)
SPLIT_COPY = pltpu.CompilerParams(has_side_effects=pltpu.SideEffectType.DATAFLOW_SIDE_EFFECTING)
SLAB_IS_TRANSPOSED = {'w_in': True, 'w_out': False, 'w_gate': True, 'w_up': True, 'w_down': False,
                      'w_ple_gate': False, 'w_ple_proj': True}


def _place():
    x, y, c = lax.axis_index("x"), lax.axis_index("y"), lax.axis_index("c")
    other_chips = [(1 - x, y), (x, 1 - y), (1 - x, 1 - y)]
    return x, y, c, other_chips


def _chip_of(chip):
    return 2 * chip[0] + chip[1]


def _half(ref, lead, hc):
    hr = ref.shape[1] // 2
    return ref.at[lead, pl.ds(hc * hr, hr), :]


def _stack_with_own(slab):
    me = 2 * lax.axis_index("x") + lax.axis_index("y")
    stack = lax.empty((N_CHIPS,) + slab.shape, slab.dtype)
    return lax.dynamic_update_slice(stack, slab[None], (me, 0, 0))


def _first_hops(land_ref, send_sems, recv_sems):
    x, y, c, chips = _place()
    mine = _half(land_ref, 2 * x + y, c)
    sends = [pltpu.make_async_remote_copy(src_ref=mine, dst_ref=mine, send_sem=send_sems.at[j], recv_sem=recv_sems.at[j],
                                          device_id=(*chip, c), device_id_type=MESH) for j, chip in enumerate(chips)]
    recvs = [pltpu.make_async_remote_copy(src_ref=mine, dst_ref=_half(land_ref, _chip_of(chip), c),
                                          send_sem=send_sems.at[j], recv_sem=recv_sems.at[j], device_id=(*chip, c),
                                          device_id_type=MESH) for j, chip in enumerate(chips)]
    return sends, recvs


def _second_hops(land_ref, send_sems, recv_sems):
    x, y, c, chips = _place()
    sibling = (x, y, 1 - c)
    sends = [pltpu.make_async_remote_copy(src_ref=_half(land_ref, _chip_of(chip), c),
                                          dst_ref=_half(land_ref, _chip_of(chip), c), send_sem=send_sems.at[j],
                                          recv_sem=recv_sems.at[j], device_id=sibling, device_id_type=MESH)
             for j, chip in enumerate(chips)]
    recvs = [pltpu.make_async_remote_copy(src_ref=_half(land_ref, _chip_of(chip), c),
                                          dst_ref=_half(land_ref, _chip_of(chip), 1 - c), send_sem=send_sems.at[j],
                                          recv_sem=recv_sems.at[j], device_id=sibling, device_id_type=MESH)
             for j, chip in enumerate(chips)]
    return sends, recvs


def _two_level_start(stack):
    def body(land_ref, send_sems, recv_sems, land_thru, token):
        for cp in _first_hops(land_ref, send_sems, recv_sems)[0]:
            cp.start()
        token[...] = jnp.zeros_like(token)

    res = pl.pallas_call(
        body, name="two_level_start",
        out_shape=(pltpu.SemaphoreType.DMA((3,)), pltpu.SemaphoreType.DMA((3,)), pltpu.HBM(stack.shape, stack.dtype),
                   jax.ShapeDtypeStruct((8, LANES), F32)),
        in_specs=[ANY], out_specs=(SEM, SEM, ANY, pl.BlockSpec(memory_space=pltpu.VMEM)),
        input_output_aliases={0: 2}, compiler_params=SPLIT_COPY,
    )(pltpu.with_memory_space_constraint(stack, pltpu.HBM))
    return res[:-1], res[-1]


def _two_level_forward(handle, after):
    send_sems, recv_sems, land = handle

    def body(land_ref, send_sems, recv_sems, *refs):
        send2, recv2 = refs[len(after):len(after) + 2]
        sends, recvs = _first_hops(land_ref, send_sems, recv_sems)
        for cp in sends:
            cp.wait_send()
        for cp in recvs:
            cp.wait_recv()
        for cp in _second_hops(land_ref, send2, recv2)[0]:
            cp.start()

    return pl.pallas_call(
        body, name="two_level_forward",
        out_shape=(pltpu.SemaphoreType.DMA((3,)), pltpu.SemaphoreType.DMA((3,)), pltpu.HBM(land.shape, land.dtype)),
        in_specs=[ANY, SEM, SEM] + [ANY] * len(after), out_specs=(SEM, SEM, ANY),
        input_output_aliases={0: 2}, compiler_params=SPLIT_COPY,
    )(land, send_sems, recv_sems, *after)


def _two_level_wait(handle):
    send_sems, recv_sems, land = handle

    def body(land_ref, send_sems, recv_sems, land_thru):
        sends, recvs = _second_hops(land_ref, send_sems, recv_sems)
        for cp in sends:
            cp.wait_send()
        for cp in recvs:
            cp.wait_recv()

    land = pl.pallas_call(
        body, name="two_level_wait", out_shape=pltpu.HBM(land.shape, land.dtype), in_specs=[ANY, SEM, SEM],
        out_specs=ANY, input_output_aliases={0: 0}, compiler_params=SPLIT_COPY,
    )(land, send_sems, recv_sems)
    return land.reshape(-1, land.shape[-1])


def _gather_copies(land_refs, send_sems, recv_sems):
    x, y, c, chips = _place()
    sends, recvs = [], []
    for k, land in enumerate(land_refs):
        mine = _half(land, 2 * x + y, c)
        for j, chip in enumerate(chips):
            for t in range(2):
                sends.append(pltpu.make_async_remote_copy(
                    src_ref=mine, dst_ref=mine, send_sem=send_sems.at[6 * k + 2 * j + t],
                    recv_sem=recv_sems.at[6 * k + 2 * j + c], device_id=(*chip, t), device_id_type=MESH))
                recvs.append(pltpu.make_async_remote_copy(
                    src_ref=mine, dst_ref=_half(land, _chip_of(chip), t), send_sem=send_sems.at[6 * k + 2 * j + t],
                    recv_sem=recv_sems.at[6 * k + 2 * j + t], device_id=(*chip, t), device_id_type=MESH))
    return sends, recvs


def _all_gather_start(stacks, after, carry):
    n = len(stacks)

    def body(*refs):
        carry_ref = refs[n + 1]
        send_sems, recv_sems = refs[n + 2:n + 4]
        sends, _ = _gather_copies(refs[:n], send_sems, recv_sems)
        for cp in sends:
            cp.start()
        refs[-1][...] = carry_ref[...]

    hbm = lambda a: pltpu.HBM(a.shape, a.dtype)
    vmem = pl.BlockSpec(memory_space=pltpu.VMEM)
    res = pl.pallas_call(
        body, name="all_gather_start",
        out_shape=(pltpu.SemaphoreType.DMA((6 * n,)), pltpu.SemaphoreType.DMA((6 * n,)), *map(hbm, stacks),
                   jax.ShapeDtypeStruct(carry.shape, carry.dtype)),
        in_specs=[ANY] * (n + 1) + [vmem], out_specs=(SEM, SEM, *([ANY] * n), vmem),
        input_output_aliases={i: 2 + i for i in range(n)}, compiler_params=SPLIT_COPY,
    )(*[pltpu.with_memory_space_constraint(a, pltpu.HBM) for a in stacks], after, carry)
    return res[:-1], res[-1]


def _all_gather_wait(handle, after):
    send_sems, recv_sems = handle[:2]
    lands = handle[2:]
    n = len(lands)

    def body(*refs):
        send_sems, recv_sems = refs[n:n + 2]
        sends, recvs = _gather_copies(refs[:n], send_sems, recv_sems)
        for cp in sends:
            cp.wait_send()
        for cp in recvs:
            cp.wait_recv()

    hbm = lambda a: pltpu.HBM(a.shape, a.dtype)
    res = pl.pallas_call(
        body, name="all_gather_wait", out_shape=tuple(map(hbm, lands)),
        in_specs=[ANY] * n + [SEM, SEM, ANY], out_specs=tuple([ANY] * n),
        input_output_aliases={i: i for i in range(n)}, compiler_params=SPLIT_COPY,
    )(*lands, send_sems, recv_sems, after)
    return [land.reshape(-1, land.shape[-1]) for land in res]


def _scatter_copies(part_refs, land_refs, send_sems, recv_sems):
    x, y, c, chips = _place()
    me = 4 * x + 2 * y + c
    sends, recvs = [], []
    for k, (part, land) in enumerate(zip(part_refs, land_refs)):
        for j, chip in enumerate(chips):
            for t in range(2):
                sends.append(pltpu.make_async_remote_copy(
                    src_ref=part.at[_chip_of(chip)], dst_ref=land.at[me],
                    send_sem=send_sems.at[7 * k + 2 * j + t], recv_sem=recv_sems.at[7 * k + 2 * j + c],
                    device_id=(*chip, t), device_id_type=MESH))
                recvs.append(pltpu.make_async_remote_copy(
                    src_ref=part.at[_chip_of(chip)], dst_ref=land.at[2 * _chip_of(chip) + t],
                    send_sem=send_sems.at[7 * k + 2 * j + t], recv_sem=recv_sems.at[7 * k + 2 * j + t],
                    device_id=(*chip, t), device_id_type=MESH))
        sends.append(pltpu.make_async_remote_copy(
            src_ref=part.at[2 * x + y], dst_ref=land.at[me], send_sem=send_sems.at[7 * k + 6],
            recv_sem=recv_sems.at[7 * k + 6], device_id=(x, y, 1 - c), device_id_type=MESH))
        recvs.append(pltpu.make_async_remote_copy(
            src_ref=part.at[2 * x + y], dst_ref=land.at[4 * x + 2 * y + 1 - c],
            send_sem=send_sems.at[7 * k + 6], recv_sem=recv_sems.at[7 * k + 6], device_id=(x, y, 1 - c),
            device_id_type=MESH))
    return sends, recvs


def _reduce_scatter_start(parts, name, carry):
    n = len(parts)
    parts = [p.reshape(N_CHIPS, p.shape[0] // N_CHIPS, p.shape[1]) for p in parts]

    def body(*refs):
        part_refs, land_refs, carry_ref = refs[:n], refs[n:2 * n], refs[2 * n]
        send_sems, recv_sems = refs[2 * n + 1:2 * n + 3]
        sends, _ = _scatter_copies(part_refs, land_refs, send_sems, recv_sems)
        for cp in sends:
            cp.start()
        refs[-1][...] = carry_ref[...]

    lands = [lax.empty((N_DEV, p.shape[1], p.shape[2]), p.dtype) for p in parts]
    hbm = lambda a: pltpu.HBM(a.shape, a.dtype)
    vmem = pl.BlockSpec(memory_space=pltpu.VMEM)
    res = pl.pallas_call(
        body, name=name,
        out_shape=(pltpu.SemaphoreType.DMA((7 * n,)), pltpu.SemaphoreType.DMA((7 * n,)), *map(hbm, parts),
                   *map(hbm, lands), jax.ShapeDtypeStruct(carry.shape, carry.dtype)),
        in_specs=[ANY] * (2 * n) + [vmem], out_specs=(SEM, SEM, *([ANY] * (2 * n)), vmem),
        input_output_aliases={i: 2 + i for i in range(2 * n)}, compiler_params=SPLIT_COPY,
    )(*[pltpu.with_memory_space_constraint(a, pltpu.HBM) for a in parts + lands], carry)
    return res[:-1], res[-1]


def _reduce_scatter_wait(handle, after, name):
    after = list(after) if isinstance(after, (list, tuple)) else [after]
    send_sems, recv_sems = handle[:2]
    n = (len(handle) - 2) // 2
    parts, lands = handle[2:2 + n], handle[2 + n:]

    def body(*refs):
        part_refs, land_refs = refs[:n], refs[n:2 * n]
        send_sems, recv_sems = refs[2 * n:2 * n + 2]
        sends, recvs = _scatter_copies(part_refs, land_refs, send_sems, recv_sems)
        for cp in sends:
            cp.wait_send()
        for cp in recvs:
            cp.wait_recv()

    hbm = lambda a: pltpu.HBM(a.shape, a.dtype)
    res = pl.pallas_call(
        body, name=name, out_shape=tuple(map(hbm, list(parts) + list(lands))),
        in_specs=[ANY] * (2 * n) + [SEM, SEM] + [ANY] * len(after), out_specs=tuple([ANY] * (2 * n)),
        input_output_aliases={i: i for i in range(2 * n)}, compiler_params=SPLIT_COPY,
    )(*parts, *lands, send_sems, recv_sems, *after)
    return list(zip(res[:n], res[n:]))


def _adamw_of_shares(w, own, land, m, v, name, shares_transposed=False):
    rows, cols = w.shape
    tm = rows // 4 if rows % 32 == 0 and not shares_transposed else rows
    share_tile = (cols, rows) if shares_transposed else (tm, cols)
    x, y, c = lax.axis_index("x"), lax.axis_index("y"), lax.axis_index("c")
    where = jnp.stack([2 * x + y, 4 * x + 2 * y + c]).astype(jnp.int32)

    def body(where_ref, w_ref, own_ref, land_ref, m_ref, v_ref, g_ref, d_ref, nm_ref, nv_ref):
        me = where_ref[1]
        g_ = jnp.zeros(share_tile, F32)
        for dev in range(N_DEV):
            g_ = g_ + jnp.where(me == dev, own_ref[0], land_ref[dev]).astype(F32)
        if shares_transposed:
            g_ = g_.T
        m_ = ADAM_B1 * m_ref[...] + (1.0 - ADAM_B1) * g_
        v_ = ADAM_B2 * v_ref[...] + (1.0 - ADAM_B2) * (g_ * g_)
        m_hat = m_ / (1.0 - ADAM_B1 ** ADAM_STEP)
        v_hat = v_ / (1.0 - ADAM_B2 ** ADAM_STEP)
        g_ref[...] = g_
        d_ref[...] = -ADAM_LR * (m_hat / (jnp.sqrt(v_hat) + ADAM_EPS) + ADAM_WD * w_ref[...])
        nm_ref[...] = m_
        nv_ref[...] = v_

    tile = pl.BlockSpec((tm, cols), lambda i, where_ref: (i, 0))
    spec = pltpu.PrefetchScalarGridSpec(
        num_scalar_prefetch=1, grid=(rows // tm,),
        in_specs=[tile, pl.BlockSpec((1,) + share_tile, lambda i, where_ref: (where_ref[0], i, 0)),
                  pl.BlockSpec((N_DEV,) + share_tile, lambda i, where_ref: (0, i, 0)), tile, tile],
        out_specs=[tile] * 4)
    return pl.pallas_call(
        body, name=name, grid_spec=spec, out_shape=[jax.ShapeDtypeStruct(w.shape, F32)] * 4,
        compiler_params=_params(("arbitrary",)),
    )(where, w, own, land, m, v)


def _pack_small(values):
    flat = jnp.concatenate([values[n].reshape(-1).astype(F32) for n in SMALL])
    return jnp.pad(flat, (0, SMALL_ROWS * D_MODEL - flat.shape[0])).reshape(SMALL_ROWS, D_MODEL)


def _unpack_small(block, shapes):
    flat = block.reshape(-1)
    out, lo = {}, 0
    for n in SMALL:
        out[n] = flat[lo:lo + SMALL_SIZES[n]].reshape(shapes[n])
        lo += SMALL_SIZES[n]
    return out


def _after(token, a):
    return a + token[:1, :1].astype(a.dtype)


def kernel(x, p, mix_norm_g, w_in, sgu_w, sgu_b, sgu_norm_g, out_norm_a, out_norm_b, w_out, ffn_norm_g, w_gate, w_up, w_down, ple_norm_g, w_ple_gate, w_ple_proj, final_norm_g, loss_target, m_mix_norm_g, m_w_in, m_sgu_w, m_sgu_b, m_sgu_norm_g, m_out_norm_a, m_out_norm_b, m_w_out, m_ffn_norm_g, m_w_gate, m_w_up, m_w_down, m_ple_norm_g, m_w_ple_gate, m_w_ple_proj, m_final_norm_g, v_mix_norm_g, v_w_in, v_sgu_w, v_sgu_b, v_sgu_norm_g, v_out_norm_a, v_out_norm_b, v_w_out, v_ffn_norm_g, v_w_gate, v_w_up, v_w_down, v_ple_norm_g, v_w_ple_gate, v_w_ple_proj, v_final_norm_g):
    given = dict(locals())
    drop_lead = lambda a, lead: a.reshape(a.shape[lead:])
    xs, ps, target = drop_lead(x, 1), drop_lead(p, 2), drop_lead(loss_target, 1)
    s = xs.shape[0]
    shard = lambda name: drop_lead(given[name], 1)

    def slab_of(name):
        local = shard(name).astype(MXU_DTYPE)
        return local.T if SLAB_IS_TRANSPOSED[name] else local

    w_in_gather, token = _two_level_start(_stack_with_own(slab_of('w_in')))
    later = ['w_out', 'w_gate', 'w_up', 'w_down', 'w_ple_gate', 'w_ple_proj']
    later_stacks = [_stack_with_own(slab_of(n)) for n in later]
    cos_t, sin_t = (_after(token, table) for table in _rope_tables(s))
    tril = jnp.tril(jnp.ones((CHUNK, CHUNK), F32))
    w_tril = (sgu_w.reshape(HEADS_A, CHUNK, CHUNK) * tril).astype(MXU_DTYPE)
    w_tril_t = jnp.swapaxes(w_tril, 1, 2)
    w_in_gather = _two_level_forward(w_in_gather, [cos_t, sin_t, w_tril, w_tril_t] + later_stacks)
    w_in_t = _two_level_wait(w_in_gather)
    g = {n: given[n].reshape(1, -1) for n in SMALL if n not in ('sgu_w', 'sgu_b')}
    gather, g_mix = _all_gather_start(later_stacks, w_in_t, g['mix_norm_g'])
    bias = jnp.repeat(sgu_b.reshape(HEADS_A, CHUNK).T, HEAD_DIM, axis=1)

    uv, q, k, v, q1, k1, v1, hn1 = _in_fwd(xs, g_mix, w_in_t, cos_t, sin_t)
    ya_n = _sgu_fwd(uv, w_tril, bias, g['sgu_norm_g'], g['out_norm_a'])
    branches = [_attn_fwd_local(q1, k1, v1)] + [_attn_fwd_branch(q, k, v, dil) for dil in DILATIONS[:-1]]
    y_b, lse = _attn_fwd_branch(q, k, v, DILATIONS[-1], earlier=branches)
    stacks = dict(zip(later, _all_gather_wait(gather, lse)))
    w_gate_t, w_up_t, w_pp_t = stacks['w_gate'], stacks['w_up'], stacks['w_ple_proj']
    h1, y_n = _out_fwd(ya_n, y_b, g['out_norm_b'], stacks['w_out'], xs)
    h2, gate, up, hn2 = _ffn_fwd(h1, g['ffn_norm_g'], w_gate_t, w_up_t, stacks['w_down'])
    loss, dh2, dz, dpp, hn3, d_ple_g, d_final_g = _ple_loss(
        h2, ps, target, g['ple_norm_g'], stacks['w_ple_gate'], w_pp_t, g['final_norm_g'])

    share = {}
    share['w_ple_gate'] = _wgrad(hn3, dz, "wgrad_ple_gate")
    share['w_ple_proj'] = _wgrad(dpp, ps, "wgrad_ple_proj")
    scatter_1, g_ffn = _reduce_scatter_start([share['w_ple_gate'], share['w_ple_proj']], "reduce_scatter_start_1",
                                             g['ffn_norm_g'])
    dh1, act, dgate, dup, d_ffn_g = _ffn_bwd(dh2, h1, gate, up, g_ffn, stacks['w_down'], w_gate_t, w_up_t)
    share['w_down'] = _wgrad(act, dh2, "wgrad_down")
    share['w_gate'] = _wgrad(dgate, hn2, "wgrad_gate")
    share['w_up'] = _wgrad(dup, hn2, "wgrad_up")
    scatter_2, g_out_b = _reduce_scatter_start([share['w_down'], share['w_gate'], share['w_up']],
                                               "reduce_scatter_start_2", g['out_norm_b'])
    dya_n, dyb, d_out_b = _out_bwd(dh1, y_b, g_out_b, stacks['w_out'])
    share['w_out'] = _wgrad(y_n, dh1, "wgrad_out")
    scatter_3, g_sgu = _reduce_scatter_start([share['w_out']], "reduce_scatter_start_3", g['sgu_norm_g'])
    grads = _attn_bwd_local(q1, k1, v1, dyb, y_b, lse)
    for dil in DILATIONS:
        grads = _attn_bwd_branch(q, k, v, dyb, y_b, lse, grads, dil)
    duv, d_sgu_w, d_sgu_b, d_sgu_g, d_out_a = _sgu_bwd(uv, dya_n, w_tril, w_tril_t, bias, g_sgu, g['out_norm_a'])
    dproj = _in_bwd_proj(duv, grads[0], grads[1], grads[2], cos_t, sin_t)
    share['w_in'] = _wgrad(dproj, hn1, "wgrad_in")
    scatter_4, g_mix = _reduce_scatter_start([share['w_in']], "reduce_scatter_start_4", g['mix_norm_g'])

    grads, deltas, new_m, new_v = {}, {}, {}, {}
    add_lead = lambda a: a.reshape((1,) + a.shape)

    done = {}

    def finish(names, handles, after, tag):
        landed = []
        for i, handle in enumerate(handles):
            landed += _reduce_scatter_wait(handle, after, "reduce_scatter_wait_%s%d" % (tag, i))
        for n, (own, land) in zip(names, landed):
            turn_shares = SLAB_IS_TRANSPOSED[n] and shard(n).shape[-1] % LANES == 0
            turn = (lambda a: a.T) if SLAB_IS_TRANSPOSED[n] and not turn_shares else (lambda a: a)
            res = _adamw_of_shares(turn(shard(n)), own, land, turn(shard("m_" + n)), turn(shard("v_" + n)),
                                   "adamw_" + n, turn_shares)
            grads[n], deltas[n], new_m[n], new_v[n] = (add_lead(turn(a)) for a in res)
            done[n] = res[0]

    grad_x, d_mix_g = _in_bwd_x(dproj, w_in_t, xs, g_mix, dh1)

    gs = {'mix_norm_g': d_mix_g, 'sgu_w': d_sgu_w, 'sgu_b': d_sgu_b[:, :HEADS_A].T, 'sgu_norm_g': d_sgu_g,
          'out_norm_a': d_out_a, 'out_norm_b': d_out_b, 'ffn_norm_g': d_ffn_g, 'ple_norm_g': d_ple_g,
          'final_norm_g': d_final_g}
    gs_block = _pack_small(gs).at[SMALL_ROWS - 1, 0].set(loss[0, 0])
    to_all = jnp.broadcast_to(gs_block[None], (N_CHIPS,) + gs_block.shape).reshape(-1, D_MODEL)
    scatter_small, token = _reduce_scatter_start([to_all], "small_all_reduce_start", g['mix_norm_g'])

    finish(['w_ple_gate', 'w_ple_proj', 'w_down', 'w_gate', 'w_up', 'w_out'], [scatter_1, scatter_2, scatter_3], token,
           "early")
    finish(['w_in'], [scatter_4], [done[n] for n in ('w_down', 'w_gate', 'w_up', 'w_out')], "last")
    (own, land), = _reduce_scatter_wait(scatter_small, done['w_in'], "small_all_reduce_wait")
    small = {n: given[n] for n in SMALL}
    small_res = _adamw_of_shares(_pack_small(small), own, land, _pack_small({n: given["m_" + n] for n in SMALL}),
                                 _pack_small({n: given["v_" + n] for n in SMALL}), "adamw_small")
    loss_out = small_res[0][SMALL_ROWS - 1, 0]
    small_shapes = {n: given[n].shape for n in SMALL}
    for res, blk in zip((grads, deltas, new_m, new_v), small_res):
        res.update(_unpack_small(blk, small_shapes))

    outs = [loss_out, add_lead(grad_x)]
    for res in (grads, deltas, new_m, new_v):
        outs += [res[n] for n in WEIGHT_NAMES]
    return tuple(outs)
```

```python
import functools
import itertools
import math

import jax
import jax.numpy as jnp
import numpy as np
from jax import lax
from jax.experimental import pallas as pl
from jax.experimental.pallas import tpu as pltpu

F32 = jnp.float32
MXU_DTYPE = jnp.bfloat16

D_MODEL = 1024
HEAD_DIM = 64
HEADS_A = 4
HEADS_B = 12
WIDTH_A = HEADS_A * HEAD_DIM
WIDTH_B = HEADS_B * HEAD_DIM
CHUNK = 128
BLOCK = 128
DILATIONS = (4, 16)
ROPE_THETA = 10000.0
D_FF = 2816
FF_HALF = D_FF // 2
FF_STRIPS = ((0, 1024), (1024, 2048), (2048, D_FF))
PLE_DIM = 256
IN_COLS = 2 * WIDTH_A + 3 * WIDTH_B
EPS = 1e-6
LANES = 128
N_CHIPS = 4
N_DEV = 8

ADAM_LR = 0.001
ADAM_B1 = 0.9
ADAM_B2 = 0.999
ADAM_EPS = 1e-08
ADAM_WD = 0.01
ADAM_STEP = 10

VMEM_LIMIT = 56 * 1024 * 1024

WEIGHT_NAMES = ['mix_norm_g', 'w_in', 'sgu_w', 'sgu_b', 'sgu_norm_g', 'out_norm_a', 'out_norm_b', 'w_out',
                'ffn_norm_g', 'w_gate', 'w_up', 'w_down', 'ple_norm_g', 'w_ple_gate', 'w_ple_proj', 'final_norm_g']
SMALL = ['mix_norm_g', 'sgu_w', 'sgu_b', 'sgu_norm_g', 'out_norm_a', 'out_norm_b', 'ffn_norm_g', 'ple_norm_g',
         'final_norm_g']
SMALL_SIZES = {'mix_norm_g': 1024, 'sgu_w': 65536, 'sgu_b': 512, 'sgu_norm_g': 256, 'out_norm_a': 256,
               'out_norm_b': 768, 'ffn_norm_g': 1024, 'ple_norm_g': 1024, 'final_norm_g': 1024}
SMALL_ROWS = 72


def _params(semantics=None):
    return pltpu.CompilerParams(dimension_semantics=semantics, vmem_limit_bytes=VMEM_LIMIT)


def _full(shape):
    nd = len(shape)
    return pl.BlockSpec(shape, lambda i: (0,) * nd, pipeline_mode=pl.Buffered(1))


def _rows(tm, width):
    return pl.BlockSpec((tm, width), lambda i: (i, 0))


def _rms_stats(x):
    r = lax.rsqrt(jnp.mean(x * x, axis=-1, keepdims=True) + EPS)
    return x * r, r


def _rms_bwd(dn, n, r):
    return r * (dn - n * jnp.mean(dn * n, axis=-1, keepdims=True))


def _dot(a, b):
    return jnp.dot(a, b, preferred_element_type=F32)


def _dot_nt(a, b):
    return lax.dot_general(a, b, (((1,), (1,)), ((), ())), preferred_element_type=F32)


def _dot_tn(a, b):
    return lax.dot_general(a, b, (((0,), (0,)), ((), ())), preferred_element_type=F32)


def _gelu_parts(x):
    c = math.sqrt(2.0 / math.pi)
    t = jnp.tanh(c * (x + 0.044715 * x * x * x))
    return 0.5 * x * (1.0 + t), t


def _gelu_grad(x, t):
    c = math.sqrt(2.0 / math.pi)
    return 0.5 * (1.0 + t) + 0.5 * x * (1.0 - t * t) * c * (1.0 + 3.0 * 0.044715 * x * x)


def _half_masks(dtype):
    lane = lax.broadcasted_iota(jnp.int32, (BLOCK, LANES), 1)
    lo = (lane < HEAD_DIM).astype(F32)
    return lo.astype(dtype), (1.0 - lo).astype(dtype)


def _rope_partner(t):
    lane = lax.broadcasted_iota(jnp.int32, t.shape, 1)
    first_half = (lane % HEAD_DIM) < (HEAD_DIM // 2)
    return jnp.where(first_half, pltpu.roll(t, LANES - HEAD_DIM // 2, 1), pltpu.roll(t, HEAD_DIM // 2, 1))


PAIRS_ABREAST = 2
RESIDUES_PER_STEP = 4
L_BLOCK = 256
L_GROUP = 16


def _store_l256(scr, out_ref, cols, value, chunk_ref=None):
    tm = value.shape[0]
    half = L_GROUP // 2
    scr[...] = value
    for blk in range(tm // L_BLOCK):
        pieces = [scr[pl.ds(blk * L_BLOCK + r, L_GROUP, stride=L_GROUP), :] for r in range(L_GROUP)]
        for r, piece in enumerate(pieces):
            lo = blk * L_BLOCK + r * L_GROUP
            out_ref[lo:lo + L_GROUP, cols] = piece.astype(out_ref.dtype)
        if chunk_ref is not None:
            for chunk in range(L_BLOCK // BLOCK):
                for r in range(0, L_GROUP, 2):
                    lo = blk * L_BLOCK + chunk * BLOCK + r * half
                    both = [p[chunk * half:(chunk + 1) * half] for p in pieces[r:r + 2]]
                    chunk_ref[lo:lo + L_GROUP, cols] = jnp.concatenate(both, axis=0).astype(chunk_ref.dtype)


def _load_l256(col_refs, tm):
    cols = []
    for ref in col_refs:
        pieces = [ref[pl.ds(blk * L_BLOCK + i, L_GROUP, stride=L_GROUP), :]
                  for blk in range(tm // L_BLOCK) for i in range(L_GROUP)]
        cols.append(jnp.concatenate(pieces, axis=0))
    return jnp.concatenate(cols, axis=1)


def _col_specs(tm, width):
    return [pl.BlockSpec((tm, LANES), lambda i, j=j: (i, j)) for j in range(width // LANES)]


def _mix_norm(x, g_mix, after):
    s = x.shape[0]
    tm = 1024

    def body(x_ref, g_ref, after_ref, hn_ref):
        n, _ = _rms_stats(x_ref[...])
        hn_ref[...] = (n * g_ref[...]).astype(MXU_DTYPE)

    return pl.pallas_call(
        body, name="mix_norm", grid=(s // tm,),
        in_specs=[_rows(tm, D_MODEL), _full((1, D_MODEL)), pl.BlockSpec(memory_space=pl.ANY)],
        out_specs=_rows(tm, D_MODEL), out_shape=jax.ShapeDtypeStruct((s, D_MODEL), MXU_DTYPE),
        compiler_params=_params(("arbitrary",)),
    )(x, g_mix, after)


def _in_fwd(hn, w_in_t, cos_t, sin_t, after):
    s = hn.shape[0]
    tm = 512

    def body(hn_ref, wt_ref, cos_ref, sin_ref, after_ref, uv_ref, q_ref, k_ref, v_ref, q1_ref, k1_ref, v1_ref, *scrs):
        hn = hn_ref[...]
        cos = cos_ref[...]
        sin = sin_ref[...]
        strip = 2 * LANES
        for j in range(IN_COLS // strip):
            proj = _dot_nt(hn, wt_ref[j * strip:(j + 1) * strip, :])
            lo = j * strip - 2 * WIDTH_A
            if lo < 0:
                uv_ref[:, j * strip:(j + 1) * strip] = proj
                continue
            which, lo = divmod(lo, WIDTH_B)
            for i in range(strip // LANES):
                t = proj[:, i * LANES:(i + 1) * LANES]
                cols = slice(lo + i * LANES, lo + (i + 1) * LANES)
                scr = scrs[i]
                if which == 0:
                    _store_l256(scr, q_ref, cols, (t * cos + _rope_partner(t) * sin) * (HEAD_DIM ** -0.5), q1_ref)
                elif which == 1:
                    _store_l256(scr, k_ref, cols, t * cos + _rope_partner(t) * sin, k1_ref)
                else:
                    _store_l256(scr, v_ref, cols, t, v1_ref)

    return pl.pallas_call(
        body, name="in_fwd", grid=(s // tm,), scratch_shapes=[pltpu.VMEM((tm, LANES), F32)] * 2,
        in_specs=[_rows(tm, D_MODEL), _full((IN_COLS, D_MODEL)), _rows(tm, LANES), _rows(tm, LANES),
                  pl.BlockSpec(memory_space=pl.ANY)],
        out_specs=[_rows(tm, 2 * WIDTH_A)] + [_rows(tm, WIDTH_B)] * 6,
        out_shape=[jax.ShapeDtypeStruct((s, 2 * WIDTH_A), F32)] + [jax.ShapeDtypeStruct((s, WIDTH_B), MXU_DTYPE)] * 6,
        compiler_params=_params(("arbitrary",)),
    )(hn, w_in_t, cos_t, sin_t, after)


class _Branch:
    def __init__(self, dil, s):
        i = np.arange(L_GROUP)
        self.res = RESIDUES_PER_STEP
        if dil == 16:
            nblk = BLOCK // 16
            self.grid = (16 // self.res, s // (L_BLOCK * nblk))
            self.shape = (nblk, 1, self.res, L_GROUP)
            self.index = lambda r, n: (n, r // (4 // self.res), r % (4 // self.res), 0, 0)
            pos = (np.arange(nblk)[:, None] * 16 + i[None, :]).reshape(-1)
        else:
            nblk = BLOCK // 64
            self.grid = (4 // self.res, s // (L_BLOCK * nblk))
            self.shape = (nblk, 4, self.res, L_GROUP)
            self.index = lambda r, n: (n, 0, r, 0, 0)
            pos = (np.arange(nblk)[:, None, None] * 64 + np.arange(4)[None, :, None]
                   + 4 * i[None, None, :]).reshape(-1)
        self.qn = pos.shape[0]
        self.nb = self.grid[1]
        dist = pos[:, None] - np.concatenate([pos - self.qn, pos])[None, :]
        band = (dist >= 0) & (dist <= BLOCK)
        start = band & (np.arange(2 * self.qn)[None, :] >= self.qn)
        self.bias = np.where(np.stack([band, start]), 0.0, -np.inf).astype(np.float32)

    def view(self, a):
        return a.reshape(a.shape[0] // L_BLOCK, 4, 4, L_GROUP, a.shape[1])

    def spec(self, w, step=lambda n: n):
        return pl.BlockSpec(self.shape + (w,), lambda r, n: self.index(r, step(n)))

    def bias_spec(self, step=lambda n: n):
        return pl.BlockSpec((1, self.qn, 2 * self.qn), lambda r, n: (jnp.where(step(n) == 0, 1, 0), 0, 0))

    def load(self, ref, cols=slice(None), j=0):
        x = ref[:, :, j, :, cols]
        return x.reshape(self.qn, x.shape[-1])

    def store(self, ref, cols, value, j=0):
        ref[:, :, j, :, cols] = value.reshape(self.shape[:2] + (L_GROUP, value.shape[-1]))


def _attn_fwd_branch(q, k, v, dil, earlier=()):
    s = q.shape[0]
    br = _Branch(dil, s)
    qn = br.qn
    nearly = len(earlier)

    def body(bias_ref, q_ref, kc_ref, kp_ref, vc_ref, vp_ref, *refs):
        early_refs, (o_ref, lse_ref) = refs[:2 * nearly], refs[2 * nearly:]
        bias2 = jnp.concatenate([bias_ref[0], bias_ref[0]], axis=0)
        lo = lax.broadcasted_iota(jnp.int32, (qn, LANES), 1) < HEAD_DIM
        mask_lo = lo.astype(F32).astype(MXU_DTYPE)
        for j, hp in itertools.product(range(br.res), range(HEADS_B // 2)):
            cols = slice(hp * LANES, (hp + 1) * LANES)
            qp = br.load(q_ref, cols, j)
            kcat = jnp.concatenate([br.load(kp_ref, cols, j), br.load(kc_ref, cols, j)], axis=0)
            vcat = jnp.concatenate([br.load(vp_ref, cols, j), br.load(vc_ref, cols, j)], axis=0)
            sc = _dot_nt(jnp.concatenate([qp * mask_lo, qp * (1 - mask_lo)], axis=0), kcat) + bias2
            m = jnp.max(sc, axis=1, keepdims=True)
            p = jnp.exp(sc - m)
            l = jnp.sum(p, axis=1, keepdims=True)
            out = _dot(p.astype(MXU_DTYPE), vcat) / l
            lse = m + jnp.log(l)
            outs = [br.load(r, cols, j) for r in early_refs[:nearly]] + [jnp.where(lo, out[:qn], out[qn:])]
            lses = [br.load(r, cols, j) for r in early_refs[nearly:]] + [jnp.where(lo, lse[:qn], lse[qn:])]
            if nearly:
                top = functools.reduce(jnp.maximum, lses)
                ws = [jnp.exp(x - top) for x in lses]
                den = functools.reduce(jnp.add, ws)
                outs = [functools.reduce(jnp.add, [w * o for w, o in zip(ws, outs)]) / den]
                lses = [top + jnp.log(den)]
            br.store(o_ref, cols, outs[0], j)
            br.store(lse_ref, cols, lses[0], j)

    before = lambda n: jnp.maximum(n - 1, 0)
    res = pl.pallas_call(
        body, name="attn_fwd_d%d" % dil, grid=br.grid,
        in_specs=[br.bias_spec(), br.spec(WIDTH_B), br.spec(WIDTH_B), br.spec(WIDTH_B, before), br.spec(WIDTH_B),
                  br.spec(WIDTH_B, before)] + [br.spec(WIDTH_B)] * (2 * nearly),
        out_specs=[br.spec(WIDTH_B), br.spec(WIDTH_B)],
        out_shape=[jax.ShapeDtypeStruct((s // L_BLOCK, 4, 4, L_GROUP, WIDTH_B), F32)] * 2,
        compiler_params=_params(("arbitrary", "arbitrary")),
    )(jnp.asarray(br.bias), br.view(q), br.view(k), br.view(k), br.view(v), br.view(v),
      *[br.view(o) for o, _ in earlier], *[br.view(x) for _, x in earlier])
    return tuple(a.reshape(s, WIDTH_B) for a in res)


LOCAL_CHUNKS = 4


def _local_bias():
    row = np.arange(BLOCK)
    pos = L_GROUP * (row % (L_GROUP // 2)) + row // (L_GROUP // 2)
    dist = pos[:, None] - np.concatenate([pos - BLOCK, pos])[None, :]
    band = (dist >= 0) & (dist <= BLOCK)
    start = band & (np.arange(2 * BLOCK)[None, :] >= BLOCK)
    return np.where(np.stack([band, start]), 0.0, -np.inf).astype(np.float32)


def _chunk_view(a):
    return a.reshape(a.shape[0] // L_BLOCK, L_GROUP, 2, L_GROUP // 2, a.shape[1])


def _chunk_of(ref, j, cols=slice(None)):
    x = ref[j // 2, :, j % 2, :, cols]
    return x.reshape(BLOCK, x.shape[-1])


def _put_chunk(ref, j, cols, value):
    ref[j // 2, :, j % 2, :, cols] = value.reshape(L_GROUP, L_GROUP // 2, value.shape[-1])


def _local_keys(cur_ref, before_ref, j, cols):
    here = slice(j * BLOCK, (j + 1) * BLOCK)
    before = before_ref[:, cols] if j == 0 else cur_ref[(j - 1) * BLOCK:j * BLOCK, cols]
    return jnp.concatenate([before, cur_ref[here, cols]], axis=0)


def _local_specs(s, step=lambda n: n):
    rows = LOCAL_CHUNKS * BLOCK
    cur = pl.BlockSpec((rows, WIDTH_B), lambda n: (step(n), 0))
    before = pl.BlockSpec((BLOCK, WIDTH_B), lambda n: (jnp.maximum(LOCAL_CHUNKS * step(n) - 1, 0), 0))
    return [cur, cur, before, cur, before]


def _attn_fwd_local(q1, k1, v1):
    s = q1.shape[0]
    rows = LOCAL_CHUNKS * BLOCK
    qn = BLOCK

    def body(bias_ref, q_ref, kc_ref, kp_ref, vc_ref, vp_ref, o_ref, lse_ref):
        first = jnp.where(pl.program_id(0) == 0, bias_ref[1], bias_ref[0])
        biases = [jnp.concatenate([b, b], axis=0) for b in (first, bias_ref[0])]
        lo = lax.broadcasted_iota(jnp.int32, (qn, LANES), 1) < HEAD_DIM
        mask_lo = lo.astype(F32).astype(MXU_DTYPE)
        for j, hp in itertools.product(range(LOCAL_CHUNKS), range(HEADS_B // 2)):
            cols = slice(hp * LANES, (hp + 1) * LANES)
            qp = q_ref[j * BLOCK:(j + 1) * BLOCK, cols]
            kcat = _local_keys(kc_ref, kp_ref, j, cols)
            vcat = _local_keys(vc_ref, vp_ref, j, cols)
            sc = _dot_nt(jnp.concatenate([qp * mask_lo, qp * (1 - mask_lo)], axis=0), kcat) + biases[min(j, 1)]
            m = jnp.max(sc, axis=1, keepdims=True)
            p = jnp.exp(sc - m)
            l = jnp.sum(p, axis=1, keepdims=True)
            out = _dot(p.astype(MXU_DTYPE), vcat) / l
            lse = m + jnp.log(l)
            _put_chunk(o_ref, j, cols, jnp.where(lo, out[:qn], out[qn:]))
            _put_chunk(lse_ref, j, cols, jnp.where(lo, lse[:qn], lse[qn:]))

    out_spec = pl.BlockSpec((LOCAL_CHUNKS // 2, L_GROUP, 2, L_GROUP // 2, WIDTH_B), lambda n: (n, 0, 0, 0, 0))
    res = pl.pallas_call(
        body, name="attn_fwd_d1", grid=(s // rows,),
        in_specs=[_full((2, BLOCK, 2 * BLOCK))] + _local_specs(s), out_specs=[out_spec] * 2,
        out_shape=[jax.ShapeDtypeStruct((s // L_BLOCK, L_GROUP, 2, L_GROUP // 2, WIDTH_B), F32)] * 2,
        compiler_params=_params(("arbitrary",)),
    )(jnp.asarray(_local_bias()), q1, k1, k1, v1, v1)
    return tuple(a.reshape(s, WIDTH_B) for a in res)


def _attn_bwd_local(q1, k1, v1, do, o, lse):
    s = q1.shape[0]
    rows = LOCAL_CHUNKS * BLOCK
    nsteps = s // rows
    qn = BLOCK

    def body(bias_ref, q_ref, kc_ref, kp_ref, vc_ref, vp_ref, do_ref, o_ref, lse_ref, dq_ref, dk_ref, dv_ref,
             dk_buf, dv_buf):
        n = pl.program_id(0)

        @pl.when(n == 0)
        def _():
            dk_buf[...] = jnp.zeros_like(dk_buf)
            dv_buf[...] = jnp.zeros_like(dv_buf)

        @pl.when(n < nsteps)
        def _():
            first = jnp.where(n == 0, bias_ref[1], bias_ref[0])
            biases = [jnp.concatenate([b, b], axis=0) for b in (first, bias_ref[0])]
            lo = lax.broadcasted_iota(jnp.int32, (qn, LANES), 1) < HEAD_DIM
            mask_f = lo.astype(F32)
            mask_lo = mask_f.astype(MXU_DTYPE)
            dk_buf[LOCAL_CHUNKS:] = jnp.zeros((LOCAL_CHUNKS, qn, WIDTH_B), F32)
            dv_buf[LOCAL_CHUNKS:] = jnp.zeros((LOCAL_CHUNKS, qn, WIDTH_B), F32)

            def prepare(j, hp):
                cols = slice(hp * LANES, (hp + 1) * LANES)
                qp = q_ref[j * BLOCK:(j + 1) * BLOCK, cols]
                dop = _chunk_of(do_ref, j, cols)
                prod = dop * _chunk_of(o_ref, j, cols)
                prod_lo = prod * mask_f
                lse = _chunk_of(lse_ref, j, cols)
                return dict(
                    j=j, cols=cols, kcat=_local_keys(kc_ref, kp_ref, j, cols), vcat=_local_keys(vc_ref, vp_ref, j, cols),
                    qs=jnp.concatenate([qp * mask_lo, qp * (1 - mask_lo)], axis=0),
                    dos=jnp.concatenate([dop * mask_f, dop * (1.0 - mask_f)], axis=0).astype(MXU_DTYPE),
                    delta=jnp.concatenate([jnp.sum(prod_lo, axis=1, keepdims=True),
                                           jnp.sum(prod - prod_lo, axis=1, keepdims=True)], axis=0),
                    lse2=jnp.concatenate([lse[:, :1], lse[:, HEAD_DIM:HEAD_DIM + 1]], axis=0))

            def scores(t):
                t['sc'] = _dot_nt(t['qs'], t['kcat'])
                t['dp'] = _dot_nt(t['dos'], t['vcat'])

            def softmax(t):
                p = jnp.exp(t['sc'] + biases[min(t['j'], 1)] - t['lse2'])
                t['ds'] = (p * (t['dp'] - t['delta'])).astype(MXU_DTYPE)
                t['p'] = p.astype(MXU_DTYPE)

            def gradients(t):
                t['dvc'] = _dot_tn(t['p'], t['dos'])
                t['dkc'] = _dot_tn(t['ds'], t['qs'])
                t['dq2'] = _dot(t['ds'], t['kcat'])

            def store(t):
                j, cols = t['j'], t['cols']
                _put_chunk(dq_ref, j, cols, jnp.where(lo, t['dq2'][:qn], t['dq2'][qn:]))
                for buf, both in ((dk_buf, t['dkc']), (dv_buf, t['dvc'])):
                    buf[LOCAL_CHUNKS + j - 1, :, cols] += both[:qn]
                    buf[LOCAL_CHUNKS + j, :, cols] += both[qn:]

            for j, first_pair in itertools.product(range(LOCAL_CHUNKS), range(0, HEADS_B // 2, PAIRS_ABREAST)):
                group = [prepare(j, hp) for hp in range(first_pair, first_pair + PAIRS_ABREAST)]
                for stage in (scores, softmax, gradients, store):
                    for t in group:
                        stage(t)

        for j in range(LOCAL_CHUNKS):
            _put_chunk(dk_ref, j, slice(None), dk_buf[j])
            _put_chunk(dv_ref, j, slice(None), dv_buf[j])
        dk_buf[:LOCAL_CHUNKS] = dk_buf[LOCAL_CHUNKS:]
        dv_buf[:LOCAL_CHUNKS] = dv_buf[LOCAL_CHUNKS:]

    cur = lambda n: jnp.minimum(n, nsteps - 1)
    late = lambda n: jnp.maximum(n - 1, 0)
    view_spec = lambda step: pl.BlockSpec((LOCAL_CHUNKS // 2, L_GROUP, 2, L_GROUP // 2, WIDTH_B),
                                          lambda n: (step(n), 0, 0, 0, 0))
    res = pl.pallas_call(
        body, name="attn_bwd_d1", grid=(nsteps + 1,),
        in_specs=[_full((2, BLOCK, 2 * BLOCK))] + _local_specs(s, cur) + [view_spec(cur)] * 3,
        out_specs=[view_spec(cur), view_spec(late), view_spec(late)],
        out_shape=[jax.ShapeDtypeStruct((s // L_BLOCK, L_GROUP, 2, L_GROUP // 2, WIDTH_B), F32)] * 3,
        scratch_shapes=[pltpu.VMEM((2 * LOCAL_CHUNKS, qn, WIDTH_B), F32)] * 2,
        compiler_params=_params(("arbitrary",)),
    )(jnp.asarray(_local_bias()), q1, k1, k1, v1, v1, _chunk_view(do), _chunk_view(o), _chunk_view(lse))
    return tuple(a.reshape(s, WIDTH_B) for a in res)


def _sgu_forward_tile(uv, w_ref, bias, g_sgu):
    tm = uv.shape[0]
    u = uv[:, :WIDTH_A]
    v = uv[:, WIDTH_A:]
    ug, tu = _gelu_parts(u)
    vg, tv = _gelu_parts(v)
    mu = jnp.mean(vg, axis=-1, keepdims=True)
    vc = vg - mu
    rs = lax.rsqrt(jnp.mean(vc * vc, axis=-1, keepdims=True) + EPS)
    vhat = vc * rs
    vn = (vhat * g_sgu).astype(MXU_DTYPE)
    masks = _half_masks(MXU_DTYPE)
    chunks = []
    for c in range(tm // CHUNK):
        rows = slice(c * CHUNK, (c + 1) * CHUNK)
        groups = []
        for gp in range(2):
            vn_g = vn[rows, gp * LANES:(gp + 1) * LANES]
            groups.append(_dot(w_ref[2 * gp], vn_g * masks[0]) + _dot(w_ref[2 * gp + 1], vn_g * masks[1]))
        chunks.append(jnp.concatenate(groups, axis=1) + bias)
    mixed = jnp.concatenate(chunks, axis=0)
    return dict(u=u, v=v, ug=ug, tu=tu, tv=tv, rs=rs, vhat=vhat, vn=vn, mixed=mixed, ya=ug * mixed)


def _sgu_fwd(uv, w_tril, bias, g_sgu, g_a):
    s = uv.shape[0]
    tm = 512

    def body(uv_ref, w_ref, b_ref, gs_ref, ga_ref, o_ref):
        t = _sgu_forward_tile(uv_ref[...], w_ref, b_ref[...], gs_ref[...])
        n, _ = _rms_stats(t['ya'])
        o_ref[...] = (n * ga_ref[...]).astype(MXU_DTYPE)

    return pl.pallas_call(
        body, name="sgu_fwd", grid=(s // tm,),
        in_specs=[_rows(tm, 2 * WIDTH_A), _full((HEADS_A, CHUNK, CHUNK)), _full((CHUNK, WIDTH_A)),
                  _full((1, WIDTH_A)), _full((1, WIDTH_A))],
        out_specs=_rows(tm, WIDTH_A), out_shape=jax.ShapeDtypeStruct((s, WIDTH_A), MXU_DTYPE),
        compiler_params=_params(("arbitrary",)),
    )(uv, w_tril, bias, g_sgu, g_a)


def _out_fwd(ya_n, y_b, g_b, w_out, x):
    s = x.shape[0]
    tm = 512
    nc = WIDTH_B // LANES

    def body(ya_ref, *refs):
        yb_refs = refs[:nc]
        g_ref, w_ref, x_ref, h_ref, yn_ref = refs[nc:]
        n, _ = _rms_stats(_load_l256(yb_refs, tm))
        yn = jnp.concatenate([ya_ref[...], (n * g_ref[...]).astype(MXU_DTYPE)], axis=1)
        yn_ref[...] = yn
        h_ref[...] = x_ref[...] + _dot(yn, w_ref[...])

    return pl.pallas_call(
        body, name="out_fwd", grid=(s // tm,),
        in_specs=[_rows(tm, WIDTH_A)] + _col_specs(tm, WIDTH_B) + [_full((1, WIDTH_B)), _full((D_MODEL, D_MODEL)),
                                                                 _rows(tm, D_MODEL)],
        out_specs=[_rows(tm, D_MODEL), _rows(tm, D_MODEL)],
        out_shape=[jax.ShapeDtypeStruct((s, D_MODEL), F32), jax.ShapeDtypeStruct((s, D_MODEL), MXU_DTYPE)],
        compiler_params=_params(("arbitrary",)),
    )(ya_n, *([y_b] * nc), g_b, w_out, x)


def _ffn_fwd(h1, g_ffn, w_gate_t, w_up_t, w_down):
    s = h1.shape[0]
    tm = 512

    def body(h_ref, g_ref, wgt_ref, wut_ref, wd_ref, o_ref, gate_ref, up_ref, hn_ref):
        h = h_ref[...]
        n, _ = _rms_stats(h)
        hn = (n * g_ref[...]).astype(MXU_DTYPE)
        hn_ref[...] = hn
        strips = [dict(cols=slice(lo, hi)) for lo, hi in FF_STRIPS]

        def project(t):
            t['gate'] = _dot_nt(hn, wgt_ref[t['cols'], :])
            t['up'] = _dot_nt(hn, wut_ref[t['cols'], :])

        def activate(t):
            gate, up = t['gate'], t['up']
            gate_ref[:, t['cols']] = gate.astype(MXU_DTYPE)
            up_ref[:, t['cols']] = up.astype(MXU_DTYPE)
            t['act'] = (gate * jax.nn.sigmoid(gate) * up).astype(MXU_DTYPE)

        def down(t):
            return _dot(t['act'], wd_ref[t['cols'], :])

        out = h
        project(strips[0])
        for i, t in enumerate(strips):
            if i + 1 < len(strips):
                project(strips[i + 1])
            activate(t)
            out = out + down(t)
        o_ref[...] = out

    return pl.pallas_call(
        body, name="ffn_fwd", grid=(s // tm,),
        in_specs=[_rows(tm, D_MODEL), _full((1, D_MODEL)), _full((D_FF, D_MODEL)), _full((D_FF, D_MODEL)),
                  _full((D_FF, D_MODEL))],
        out_specs=[_rows(tm, D_MODEL), _rows(tm, D_FF), _rows(tm, D_FF), _rows(tm, D_MODEL)],
        out_shape=[jax.ShapeDtypeStruct((s, D_MODEL), F32), jax.ShapeDtypeStruct((s, D_FF), MXU_DTYPE),
                   jax.ShapeDtypeStruct((s, D_FF), MXU_DTYPE), jax.ShapeDtypeStruct((s, D_MODEL), MXU_DTYPE)],
        compiler_params=_params(("arbitrary",)),
    )(h1, g_ffn, w_gate_t, w_up_t, w_down)


def _ple_loss(h2, p, target, g_ple, w_pg, w_pp_t, g_final):
    s = h2.shape[0]
    tm = 512

    def body(h_ref, p_ref, t_ref, gp_ref, wg_ref, wpt_ref, gf_ref,
             loss_ref, dh_ref, dz_ref, dpp_ref, hn_ref, dgp_ref, dgf_ref):
        @pl.when(pl.program_id(0) == 0)
        def _():
            loss_ref[...] = jnp.zeros_like(loss_ref)
            dgp_ref[...] = jnp.zeros_like(dgp_ref)
            dgf_ref[...] = jnp.zeros_like(dgf_ref)

        h2t = h_ref[...]
        n2, r2 = _rms_stats(h2t)
        hn = (n2 * gp_ref[...]).astype(MXU_DTYPE)
        hn_ref[...] = hn
        gate = jax.nn.sigmoid(_dot(hn, wg_ref[...]))
        pp = _dot_nt(p_ref[...].astype(MXU_DTYPE), wpt_ref[...])
        h3 = h2t + gate * pp
        n3, r3 = _rms_stats(h3)
        diff = n3 * gf_ref[...] - t_ref[...]
        loss_ref[...] += jnp.full(loss_ref.shape, 0.5 * jnp.sum(diff * diff) / D_MODEL, F32)
        dy = diff * (1.0 / D_MODEL)
        dgf_ref[...] += jnp.sum(dy * n3, axis=0, keepdims=True)
        dh3 = _rms_bwd(dy * gf_ref[...], n3, r3)
        dpp_ref[...] = (dh3 * gate).astype(MXU_DTYPE)
        dz = (dh3 * pp * gate * (1.0 - gate)).astype(MXU_DTYPE)
        dz_ref[...] = dz
        dhn = _dot_nt(dz, wg_ref[...])
        dgp_ref[...] += jnp.sum(dhn * n2, axis=0, keepdims=True)
        dh_ref[...] = dh3 + _rms_bwd(dhn * gp_ref[...], n2, r2)

    return pl.pallas_call(
        body, name="ple_loss", grid=(s // tm,),
        in_specs=[_rows(tm, D_MODEL), _rows(tm, PLE_DIM), _rows(tm, D_MODEL), _full((1, D_MODEL)),
                  _full((D_MODEL, D_MODEL)), _full((D_MODEL, PLE_DIM)), _full((1, D_MODEL))],
        out_specs=[_full((1, LANES)), _rows(tm, D_MODEL), _rows(tm, D_MODEL), _rows(tm, D_MODEL),
                   _rows(tm, D_MODEL), _full((1, D_MODEL)), _full((1, D_MODEL))],
        out_shape=[jax.ShapeDtypeStruct((1, LANES), F32), jax.ShapeDtypeStruct((s, D_MODEL), F32),
                   jax.ShapeDtypeStruct((s, D_MODEL), MXU_DTYPE), jax.ShapeDtypeStruct((s, D_MODEL), MXU_DTYPE),
                   jax.ShapeDtypeStruct((s, D_MODEL), MXU_DTYPE), jax.ShapeDtypeStruct((1, D_MODEL), F32),
                   jax.ShapeDtypeStruct((1, D_MODEL), F32)],
        compiler_params=_params(("arbitrary",)),
    )(h2, p, target, g_ple, w_pg, w_pp_t, g_final)


def _ffn_bwd(dh2, h1, gate, up, g_ffn, w_down, w_gate_t, w_up_t):
    s = h1.shape[0]
    tm = 256

    def body(dh_ref, h_ref, gate_ref, up_ref, g_ref, wd_ref, wgt_ref, wut_ref,
             o_ref, act_ref, dg_ref, du_ref, dgn_ref):
        @pl.when(pl.program_id(0) == 0)
        def _():
            dgn_ref[...] = jnp.zeros_like(dgn_ref)

        dh = dh_ref[...]
        dhb = dh.astype(MXU_DTYPE)
        strips = [dict(cols=slice(lo, hi)) for lo, hi in FF_STRIPS]

        def back_down(t):
            t['dact'] = _dot_nt(dhb, wd_ref[t['cols'], :])

        def back_act(t):
            cols, dact = t['cols'], t['dact']
            g = gate_ref[:, cols].astype(F32)
            u = up_ref[:, cols].astype(F32)
            sg = jax.nn.sigmoid(g)
            silu = g * sg
            act_ref[:, cols] = (silu * u).astype(MXU_DTYPE)
            t['du'] = (dact * silu).astype(MXU_DTYPE)
            t['dg'] = (dact * u * sg * (1.0 + g * (1.0 - sg))).astype(MXU_DTYPE)
            du_ref[:, cols] = t['du']
            dg_ref[:, cols] = t['dg']

        def back_in(t):
            return _dot(t['dg'], wgt_ref[t['cols'], :]) + _dot(t['du'], wut_ref[t['cols'], :])

        dhn = jnp.zeros((tm, D_MODEL), F32)
        back_down(strips[0])
        for i, t in enumerate(strips):
            if i + 1 < len(strips):
                back_down(strips[i + 1])
            back_act(t)
            dhn = dhn + back_in(t)
        n, r = _rms_stats(h_ref[...])
        dgn_ref[...] += jnp.sum(dhn * n, axis=0, keepdims=True)
        o_ref[...] = dh + _rms_bwd(dhn * g_ref[...], n, r)

    return pl.pallas_call(
        body, name="ffn_bwd", grid=(s // tm,),
        in_specs=[_rows(tm, D_MODEL), _rows(tm, D_MODEL), _rows(tm, D_FF), _rows(tm, D_FF), _full((1, D_MODEL)),
                  _full((D_FF, D_MODEL)), _full((D_FF, D_MODEL)), _full((D_FF, D_MODEL))],
        out_specs=[_rows(tm, D_MODEL), _rows(tm, D_FF), _rows(tm, D_FF), _rows(tm, D_FF), _full((1, D_MODEL))],
        out_shape=[jax.ShapeDtypeStruct((s, D_MODEL), F32), jax.ShapeDtypeStruct((s, D_FF), MXU_DTYPE),
                   jax.ShapeDtypeStruct((s, D_FF), MXU_DTYPE), jax.ShapeDtypeStruct((s, D_FF), MXU_DTYPE),
                   jax.ShapeDtypeStruct((1, D_MODEL), F32)],
        compiler_params=_params(("arbitrary",)),
    )(dh2, h1, gate, up, g_ffn, w_down, w_gate_t, w_up_t)


def _out_bwd(dh1, y_b, g_b, w_out):
    s = dh1.shape[0]
    tm = 512
    nc = WIDTH_B // LANES

    def body(dh_ref, *refs):
        yb_refs = refs[:nc]
        g_ref, w_ref, dya_ref, dyb_ref, dg_ref, scr = refs[nc:]

        @pl.when(pl.program_id(0) == 0)
        def _():
            dg_ref[...] = jnp.zeros_like(dg_ref)

        dy = _dot_nt(dh_ref[...].astype(MXU_DTYPE), w_ref[...])
        dya_ref[...] = dy[:, :WIDTH_A]
        dyb = dy[:, WIDTH_A:]
        n, r = _rms_stats(_load_l256(yb_refs, tm))
        dg_ref[...] += jnp.sum(dyb * n, axis=0, keepdims=True)
        dyb_in = _rms_bwd(dyb * g_ref[...], n, r)
        for j in range(nc):
            cols = slice(j * LANES, (j + 1) * LANES)
            _store_l256(scr, dyb_ref, cols, dyb_in[:, cols])

    return pl.pallas_call(
        body, name="out_bwd", grid=(s // tm,), scratch_shapes=[pltpu.VMEM((tm, LANES), F32)],
        in_specs=[_rows(tm, D_MODEL)] + _col_specs(tm, WIDTH_B) + [_full((1, WIDTH_B)), _full((D_MODEL, D_MODEL))],
        out_specs=[_rows(tm, WIDTH_A), _rows(tm, WIDTH_B), _full((1, WIDTH_B))],
        out_shape=[jax.ShapeDtypeStruct((s, WIDTH_A), F32), jax.ShapeDtypeStruct((s, WIDTH_B), F32),
                   jax.ShapeDtypeStruct((1, WIDTH_B), F32)],
        compiler_params=_params(("arbitrary",)),
    )(dh1, *([y_b] * nc), g_b, w_out)


def _attn_bwd_branch(q, k, v, do, o, lse, grads, dil):
    s = q.shape[0]
    br = _Branch(dil, s)
    qn, nb = br.qn, br.nb
    first = grads is None

    def body(*refs):
        bias_ref, q_ref, kc_ref, kp_ref, vc_ref, vp_ref, do_ref, o_ref, lse_ref = refs[:9]
        if first:
            rest = refs[9:]
        else:
            dq_in, dk_in, dv_in = refs[9:12]
            rest = refs[12:]
        dq_ref, dk_ref, dv_ref, dk_carry, dv_carry = rest
        n = pl.program_id(1)

        @pl.when(n == 0)
        def _():
            dk_carry[...] = jnp.zeros_like(dk_carry)
            dv_carry[...] = jnp.zeros_like(dv_carry)

        @pl.when(n < nb)
        def _():
            bias2 = jnp.concatenate([bias_ref[0], bias_ref[0]], axis=0)
            lane = lax.broadcasted_iota(jnp.int32, (qn, LANES), 1)
            lo = lane < HEAD_DIM
            mask_f = lo.astype(F32)
            mask_lo = mask_f.astype(MXU_DTYPE)
            def prepare(j, hp):
                cols = slice(hp * LANES, (hp + 1) * LANES)
                qp = br.load(q_ref, cols, j)
                dop = br.load(do_ref, cols, j)
                prod = dop * br.load(o_ref, cols, j)
                prod_lo = prod * mask_f
                lse = br.load(lse_ref, cols, j)
                return dict(
                    j=j, cols=cols,
                    kcat=jnp.concatenate([br.load(kp_ref, cols, j), br.load(kc_ref, cols, j)], axis=0),
                    vcat=jnp.concatenate([br.load(vp_ref, cols, j), br.load(vc_ref, cols, j)], axis=0),
                    qs=jnp.concatenate([qp * mask_lo, qp * (1 - mask_lo)], axis=0),
                    dos=jnp.concatenate([dop * mask_f, dop * (1.0 - mask_f)], axis=0).astype(MXU_DTYPE),
                    delta=jnp.concatenate([jnp.sum(prod_lo, axis=1, keepdims=True),
                                           jnp.sum(prod - prod_lo, axis=1, keepdims=True)], axis=0),
                    lse2=jnp.concatenate([lse[:, :1], lse[:, HEAD_DIM:HEAD_DIM + 1]], axis=0))

            def scores(t):
                t['sc'] = _dot_nt(t['qs'], t['kcat'])
                t['dp'] = _dot_nt(t['dos'], t['vcat'])

            def softmax(t):
                p = jnp.exp(t['sc'] + bias2 - t['lse2'])
                t['ds'] = (p * (t['dp'] - t['delta'])).astype(MXU_DTYPE)
                t['p'] = p.astype(MXU_DTYPE)

            def gradients(t):
                t['dvc'] = _dot_tn(t['p'], t['dos'])
                t['dkc'] = _dot_tn(t['ds'], t['qs'])
                t['dq2'] = _dot(t['ds'], t['kcat'])

            def store(t):
                j, cols, dkc, dvc = t['j'], t['cols'], t['dkc'], t['dvc']
                dq = jnp.where(lo, t['dq2'][:qn], t['dq2'][qn:])
                dk_prev = dk_carry[j, :, cols] + dkc[:qn]
                dv_prev = dv_carry[j, :, cols] + dvc[:qn]
                if not first:
                    dq = dq + br.load(dq_in, cols, j)
                    dk_prev = dk_prev + br.load(dk_in, cols, j)
                    dv_prev = dv_prev + br.load(dv_in, cols, j)
                br.store(dq_ref, cols, dq, j)
                br.store(dk_ref, cols, dk_prev, j)
                br.store(dv_ref, cols, dv_prev, j)
                dk_carry[j, :, cols] = dkc[qn:]
                dv_carry[j, :, cols] = dvc[qn:]

            for j, first_pair in itertools.product(range(br.res), range(0, HEADS_B // 2, PAIRS_ABREAST)):
                group = [prepare(j, hp) for hp in range(first_pair, first_pair + PAIRS_ABREAST)]
                for stage in (scores, softmax, gradients, store):
                    for t in group:
                        stage(t)

        @pl.when(n == nb)
        def _():
            for j in range(br.res):
                dk_last = dk_carry[j]
                dv_last = dv_carry[j]
                if not first:
                    dk_last = dk_last + br.load(dk_in, slice(None), j)
                    dv_last = dv_last + br.load(dv_in, slice(None), j)
                br.store(dk_ref, slice(None), dk_last, j)
                br.store(dv_ref, slice(None), dv_last, j)

    cur = lambda n: jnp.minimum(n, nb - 1)
    before = lambda n: jnp.maximum(cur(n) - 1, 0)
    late = lambda n: jnp.maximum(n - 1, 0)
    in_specs = [br.bias_spec(cur), br.spec(WIDTH_B, cur), br.spec(WIDTH_B, cur), br.spec(WIDTH_B, before),
                br.spec(WIDTH_B, cur), br.spec(WIDTH_B, before), br.spec(WIDTH_B, cur), br.spec(WIDTH_B, cur),
                br.spec(WIDTH_B, cur)]
    args = [jnp.asarray(br.bias)] + [br.view(a) for a in (q, k, k, v, v, do, o, lse)]
    if not first:
        in_specs += [br.spec(WIDTH_B, cur), br.spec(WIDTH_B, late), br.spec(WIDTH_B, late)]
        args += [br.view(g) for g in grads]
    res = pl.pallas_call(
        body, name="attn_bwd_d%d" % dil, grid=(br.grid[0], nb + 1), in_specs=in_specs,
        out_specs=[br.spec(WIDTH_B, cur), br.spec(WIDTH_B, late), br.spec(WIDTH_B, late)],
        out_shape=[jax.ShapeDtypeStruct((s // L_BLOCK, 4, 4, L_GROUP, WIDTH_B), F32)] * 3,
        scratch_shapes=[pltpu.VMEM((br.res, qn, WIDTH_B), F32), pltpu.VMEM((br.res, qn, WIDTH_B), F32)],
        compiler_params=_params(("arbitrary", "arbitrary")),
    )(*args)
    return tuple(a.reshape(s, WIDTH_B) for a in res)


def _sgu_bwd(uv, dya_n, w_tril, w_tril_t, bias, g_sgu, g_a):
    s = uv.shape[0]
    tm = 512

    def body(uv_ref, dy_ref, w_ref, wt_ref, b_ref, gs_ref, ga_ref, duv_ref, dw_ref, db_ref, dgs_ref, dga_ref,
             db_acc):
        i = pl.program_id(0)

        @pl.when(i == 0)
        def _():
            dw_ref[...] = jnp.zeros_like(dw_ref)
            dgs_ref[...] = jnp.zeros_like(dgs_ref)
            dga_ref[...] = jnp.zeros_like(dga_ref)
            db_acc[...] = jnp.zeros_like(db_acc)

        t = _sgu_forward_tile(uv_ref[...], w_ref, b_ref[...], gs_ref[...])
        na, ra = _rms_stats(t['ya'])
        dyn = dy_ref[...]
        dga_ref[...] += jnp.sum(dyn * na, axis=0, keepdims=True)
        dya = _rms_bwd(dyn * ga_ref[...], na, ra)
        dug = dya * t['mixed']
        dmixed = dya * t['ug']
        dmb = dmixed.astype(MXU_DTYPE)
        masks = _half_masks(MXU_DTYPE)
        chunks = []
        db = jnp.zeros((CHUNK, WIDTH_A), F32)
        for c in range(tm // CHUNK):
            rows = slice(c * CHUNK, (c + 1) * CHUNK)
            db = db + dmixed[rows]
            groups = []
            for gp in range(2):
                cols = slice(gp * LANES, (gp + 1) * LANES)
                dm_g = dmb[rows, cols]
                vn_g = t['vn'][rows, cols]
                dvn_g = jnp.zeros((CHUNK, LANES), F32)
                for j in range(2):
                    dm_h = dm_g * masks[j]
                    dvn_g = dvn_g + _dot(wt_ref[2 * gp + j], dm_h)
                    dw_ref[2 * gp + j] += _dot_nt(dm_h, vn_g)
                groups.append(dvn_g)
            chunks.append(jnp.concatenate(groups, axis=1))
        db_acc[...] += db
        dvn = jnp.concatenate(chunks, axis=0)
        vhat = t['vhat']
        dgs_ref[...] += jnp.sum(dvn * vhat, axis=0, keepdims=True)
        dvh = dvn * gs_ref[...]
        dvg = t['rs'] * (dvh - jnp.mean(dvh, axis=-1, keepdims=True)
                         - vhat * jnp.mean(dvh * vhat, axis=-1, keepdims=True))
        duv_ref[:, :WIDTH_A] = (dug * _gelu_grad(t['u'], t['tu'])).astype(MXU_DTYPE)
        duv_ref[:, WIDTH_A:] = (dvg * _gelu_grad(t['v'], t['tv'])).astype(MXU_DTYPE)

        @pl.when(i == pl.num_programs(0) - 1)
        def _():
            lane_a = lax.broadcasted_iota(jnp.int32, (CHUNK, WIDTH_A), 1)
            lane = lax.broadcasted_iota(jnp.int32, (CHUNK, LANES), 1)
            acc = db_acc[...]
            out = jnp.zeros((CHUNK, LANES), F32)
            for h in range(HEADS_A):
                col = jnp.sum(jnp.where(lane_a // HEAD_DIM == h, acc, 0.0), axis=1, keepdims=True)
                out = jnp.where(lane == h, col, out)
            db_ref[...] = out
            causal = (lax.broadcasted_iota(jnp.int32, (CHUNK, CHUNK), 0)
                      >= lax.broadcasted_iota(jnp.int32, (CHUNK, CHUNK), 1))
            for h in range(HEADS_A):
                dw_ref[h] = jnp.where(causal, dw_ref[h], 0.0)

    return pl.pallas_call(
        body, name="sgu_bwd", grid=(s // tm,),
        in_specs=[_rows(tm, 2 * WIDTH_A), _rows(tm, WIDTH_A), _full((HEADS_A, CHUNK, CHUNK)),
                  _full((HEADS_A, CHUNK, CHUNK)), _full((CHUNK, WIDTH_A)), _full((1, WIDTH_A)),
                  _full((1, WIDTH_A))],
        out_specs=[_rows(tm, 2 * WIDTH_A), _full((HEADS_A, CHUNK, CHUNK)), _full((CHUNK, LANES)),
                   _full((1, WIDTH_A)), _full((1, WIDTH_A))],
        out_shape=[jax.ShapeDtypeStruct((s, 2 * WIDTH_A), MXU_DTYPE),
                   jax.ShapeDtypeStruct((HEADS_A, CHUNK, CHUNK), F32), jax.ShapeDtypeStruct((CHUNK, LANES), F32),
                   jax.ShapeDtypeStruct((1, WIDTH_A), F32), jax.ShapeDtypeStruct((1, WIDTH_A), F32)],
        scratch_shapes=[pltpu.VMEM((CHUNK, WIDTH_A), F32)],
        compiler_params=_params(("arbitrary",)),
    )(uv, dya_n, w_tril, w_tril_t, bias, g_sgu, g_a)


def _in_bwd_proj(duv, dq, dk, dv, cos_t, sin_t):
    s = duv.shape[0]
    tm = 512
    nc = WIDTH_B // LANES

    def body(duv_ref, *refs):
        dq_refs, dk_refs, dv_refs = refs[:nc], refs[nc:2 * nc], refs[2 * nc:3 * nc]
        cos_ref, sin_ref, dp_ref = refs[3 * nc:]
        cos = cos_ref[...]
        sin = sin_ref[...]
        dp_ref[:, :2 * WIDTH_A] = duv_ref[...]
        for i in range(nc):
            lo = 2 * WIDTH_A + i * LANES
            tq = _load_l256(dq_refs[i:i + 1], tm) * (HEAD_DIM ** -0.5)
            tk = _load_l256(dk_refs[i:i + 1], tm)
            dp_ref[:, lo:lo + LANES] = (tq * cos + _rope_partner(tq * sin)).astype(MXU_DTYPE)
            dp_ref[:, lo + WIDTH_B:lo + WIDTH_B + LANES] = (tk * cos + _rope_partner(tk * sin)).astype(MXU_DTYPE)
            dp_ref[:, lo + 2 * WIDTH_B:lo + 2 * WIDTH_B + LANES] = _load_l256(dv_refs[i:i + 1], tm).astype(MXU_DTYPE)

    return pl.pallas_call(
        body, name="in_bwd_proj", grid=(s // tm,),
        in_specs=[_rows(tm, 2 * WIDTH_A)] + 3 * _col_specs(tm, WIDTH_B) + [_rows(tm, LANES), _rows(tm, LANES)],
        out_specs=_rows(tm, IN_COLS), out_shape=jax.ShapeDtypeStruct((s, IN_COLS), MXU_DTYPE),
        compiler_params=_params(("arbitrary",)),
    )(duv, *([dq] * nc), *([dk] * nc), *([dv] * nc), cos_t, sin_t)


def _in_bwd_x(dproj, w_in_t, x, g_mix, dh1):
    s = x.shape[0]
    tm = 512

    def body(dp_ref, wt_ref, x_ref, g_ref, dh_ref, gx_ref, dg_ref):
        @pl.when(pl.program_id(0) == 0)
        def _():
            dg_ref[...] = jnp.zeros_like(dg_ref)

        dhn = _dot(dp_ref[...], wt_ref[...])
        n, r = _rms_stats(x_ref[...])
        dg_ref[...] += jnp.sum(dhn * n, axis=0, keepdims=True)
        gx_ref[...] = dh_ref[...] + _rms_bwd(dhn * g_ref[...], n, r)

    return pl.pallas_call(
        body, name="in_bwd_x", grid=(s // tm,),
        in_specs=[_rows(tm, IN_COLS), _full((IN_COLS, D_MODEL)), _rows(tm, D_MODEL), _full((1, D_MODEL)),
                  _rows(tm, D_MODEL)],
        out_specs=[_rows(tm, D_MODEL), _full((1, D_MODEL))],
        out_shape=[jax.ShapeDtypeStruct((s, D_MODEL), F32), jax.ShapeDtypeStruct((1, D_MODEL), F32)],
        compiler_params=_params(("arbitrary",)),
    )(dproj, w_in_t, x, g_mix, dh1)


def _wgrad(a, b, name):
    s, m = a.shape
    n = b.shape[1]
    bm = min(m, FF_HALF)
    ts = 1024
    nsteps = s // ts

    def body(a_ref, b_ref, o_ref, acc):
        kk = pl.program_id(1)

        @pl.when(kk == 0)
        def _():
            acc[...] = jnp.zeros_like(acc)

        acc[...] += _dot_tn(a_ref[...].astype(MXU_DTYPE), b_ref[...].astype(MXU_DTYPE))

        @pl.when(kk == nsteps - 1)
        def _():
            o_ref[...] = acc[...].astype(o_ref.dtype)

    return pl.pallas_call(
        body, name=name, grid=(m // bm, nsteps),
        in_specs=[pl.BlockSpec((ts, bm), lambda i, kk: (kk, i)), pl.BlockSpec((ts, n), lambda i, kk: (kk, 0))],
        out_specs=pl.BlockSpec((bm, n), lambda i, kk: (i, 0)), out_shape=jax.ShapeDtypeStruct((m, n), jnp.bfloat16),
        scratch_shapes=[pltpu.VMEM((bm, n), F32)],
        compiler_params=_params(("arbitrary", "arbitrary")),
    )(a, b)


def _rope_tables(s):
    half = HEAD_DIM // 2
    inv = ROPE_THETA ** (-jnp.arange(half, dtype=F32) / half)
    ang = jnp.arange(s, dtype=F32)[:, None] * jnp.tile(inv, LANES // half)[None, :]
    sign = jnp.tile(jnp.concatenate([-jnp.ones(half, F32), jnp.ones(half, F32)]), LANES // HEAD_DIM)
    return jnp.cos(ang), jnp.sin(ang) * sign[None, :]


MESH = pl.DeviceIdType.MESH
ANY = pl.BlockSpec(memory_space=pl.ANY)
SEM = pl.BlockSpec(memory_space=pltpu.SEMAPHORE)
SPLIT_COPY = pltpu.CompilerParams(has_side_effects=pltpu.SideEffectType.DATAFLOW_SIDE_EFFECTING)
SLAB_IS_TRANSPOSED = {'w_in': True, 'w_out': False, 'w_gate': True, 'w_up': True, 'w_down': False,
                      'w_ple_gate': False, 'w_ple_proj': True}


def _place():
    x, y, c = lax.axis_index("x"), lax.axis_index("y"), lax.axis_index("c")
    other_chips = [(1 - x, y), (x, 1 - y), (1 - x, 1 - y)]
    return x, y, c, other_chips


def _chip_of(chip):
    return 2 * chip[0] + chip[1]


def _half(ref, lead, hc):
    hr = ref.shape[1] // 2
    return ref.at[lead, pl.ds(hc * hr, hr), :]


def _stack_with_own(slab):
    me = 2 * lax.axis_index("x") + lax.axis_index("y")
    stack = lax.empty((N_CHIPS,) + slab.shape, slab.dtype)
    return lax.dynamic_update_slice(stack, slab[None], (me, 0, 0))


def _first_hops(land_ref, send_sems, recv_sems):
    x, y, c, chips = _place()
    mine = _half(land_ref, 2 * x + y, c)
    sends = [pltpu.make_async_remote_copy(src_ref=mine, dst_ref=mine, send_sem=send_sems.at[j], recv_sem=recv_sems.at[j],
                                          device_id=(*chip, c), device_id_type=MESH) for j, chip in enumerate(chips)]
    recvs = [pltpu.make_async_remote_copy(src_ref=mine, dst_ref=_half(land_ref, _chip_of(chip), c),
                                          send_sem=send_sems.at[j], recv_sem=recv_sems.at[j], device_id=(*chip, c),
                                          device_id_type=MESH) for j, chip in enumerate(chips)]
    return sends, recvs


def _second_hops(land_ref, send_sems, recv_sems):
    x, y, c, chips = _place()
    sibling = (x, y, 1 - c)
    sends = [pltpu.make_async_remote_copy(src_ref=_half(land_ref, _chip_of(chip), c),
                                          dst_ref=_half(land_ref, _chip_of(chip), c), send_sem=send_sems.at[j],
                                          recv_sem=recv_sems.at[j], device_id=sibling, device_id_type=MESH)
             for j, chip in enumerate(chips)]
    recvs = [pltpu.make_async_remote_copy(src_ref=_half(land_ref, _chip_of(chip), c),
                                          dst_ref=_half(land_ref, _chip_of(chip), 1 - c), send_sem=send_sems.at[j],
                                          recv_sem=recv_sems.at[j], device_id=sibling, device_id_type=MESH)
             for j, chip in enumerate(chips)]
    return sends, recvs


def _two_level_start(stack):
    def body(land_ref, send_sems, recv_sems, land_thru, token):
        for cp in _first_hops(land_ref, send_sems, recv_sems)[0]:
            cp.start()
        token[...] = jnp.zeros_like(token)

    res = pl.pallas_call(
        body, name="two_level_start",
        out_shape=(pltpu.SemaphoreType.DMA((3,)), pltpu.SemaphoreType.DMA((3,)), pltpu.HBM(stack.shape, stack.dtype),
                   jax.ShapeDtypeStruct((8, LANES), F32)),
        in_specs=[ANY], out_specs=(SEM, SEM, ANY, pl.BlockSpec(memory_space=pltpu.VMEM)),
        input_output_aliases={0: 2}, compiler_params=SPLIT_COPY,
    )(pltpu.with_memory_space_constraint(stack, pltpu.HBM))
    return res[:-1], res[-1]


def _two_level_forward(handle, after):
    send_sems, recv_sems, land = handle

    def body(land_ref, send_sems, recv_sems, *refs):
        send2, recv2 = refs[len(after):len(after) + 2]
        sends, recvs = _first_hops(land_ref, send_sems, recv_sems)
        for cp in sends:
            cp.wait_send()
        for cp in recvs:
            cp.wait_recv()
        for cp in _second_hops(land_ref, send2, recv2)[0]:
            cp.start()

    return pl.pallas_call(
        body, name="two_level_forward",
        out_shape=(pltpu.SemaphoreType.DMA((3,)), pltpu.SemaphoreType.DMA((3,)), pltpu.HBM(land.shape, land.dtype)),
        in_specs=[ANY, SEM, SEM] + [ANY] * len(after), out_specs=(SEM, SEM, ANY),
        input_output_aliases={0: 2}, compiler_params=SPLIT_COPY,
    )(land, send_sems, recv_sems, *after)


def _two_level_wait(handle):
    send_sems, recv_sems, land = handle

    def body(land_ref, send_sems, recv_sems, land_thru):
        sends, recvs = _second_hops(land_ref, send_sems, recv_sems)
        for cp in sends:
            cp.wait_send()
        for cp in recvs:
            cp.wait_recv()

    land = pl.pallas_call(
        body, name="two_level_wait", out_shape=pltpu.HBM(land.shape, land.dtype), in_specs=[ANY, SEM, SEM],
        out_specs=ANY, input_output_aliases={0: 0}, compiler_params=SPLIT_COPY,
    )(land, send_sems, recv_sems)
    return land.reshape(-1, land.shape[-1])


def _gather_copies(land_refs, send_sems, recv_sems):
    x, y, c, chips = _place()
    sends, recvs = [], []
    for k, land in enumerate(land_refs):
        mine = _half(land, 2 * x + y, c)
        for j, chip in enumerate(chips):
            for t in range(2):
                sends.append(pltpu.make_async_remote_copy(
                    src_ref=mine, dst_ref=mine, send_sem=send_sems.at[6 * k + 2 * j + t],
                    recv_sem=recv_sems.at[6 * k + 2 * j + c], device_id=(*chip, t), device_id_type=MESH))
                recvs.append(pltpu.make_async_remote_copy(
                    src_ref=mine, dst_ref=_half(land, _chip_of(chip), t), send_sem=send_sems.at[6 * k + 2 * j + t],
                    recv_sem=recv_sems.at[6 * k + 2 * j + t], device_id=(*chip, t), device_id_type=MESH))
    return sends, recvs


def _all_gather_start(stacks, after, carry):
    n = len(stacks)

    def body(*refs):
        carry_ref = refs[n + 1]
        send_sems, recv_sems = refs[n + 2:n + 4]
        sends, _ = _gather_copies(refs[:n], send_sems, recv_sems)
        for cp in sends:
            cp.start()
        refs[-1][...] = carry_ref[...]

    hbm = lambda a: pltpu.HBM(a.shape, a.dtype)
    vmem = pl.BlockSpec(memory_space=pltpu.VMEM)
    res = pl.pallas_call(
        body, name="all_gather_start",
        out_shape=(pltpu.SemaphoreType.DMA((6 * n,)), pltpu.SemaphoreType.DMA((6 * n,)), *map(hbm, stacks),
                   jax.ShapeDtypeStruct(carry.shape, carry.dtype)),
        in_specs=[ANY] * (n + 1) + [vmem], out_specs=(SEM, SEM, *([ANY] * n), vmem),
        input_output_aliases={i: 2 + i for i in range(n)}, compiler_params=SPLIT_COPY,
    )(*[pltpu.with_memory_space_constraint(a, pltpu.HBM) for a in stacks], after, carry)
    return res[:-1], res[-1]


def _all_gather_wait(handle, after):
    send_sems, recv_sems = handle[:2]
    lands = handle[2:]
    n = len(lands)

    def body(*refs):
        send_sems, recv_sems = refs[n:n + 2]
        sends, recvs = _gather_copies(refs[:n], send_sems, recv_sems)
        for cp in sends:
            cp.wait_send()
        for cp in recvs:
            cp.wait_recv()

    hbm = lambda a: pltpu.HBM(a.shape, a.dtype)
    res = pl.pallas_call(
        body, name="all_gather_wait", out_shape=tuple(map(hbm, lands)),
        in_specs=[ANY] * n + [SEM, SEM, ANY], out_specs=tuple([ANY] * n),
        input_output_aliases={i: i for i in range(n)}, compiler_params=SPLIT_COPY,
    )(*lands, send_sems, recv_sems, after)
    return [land.reshape(-1, land.shape[-1]) for land in res]


def _scatter_copies(part_refs, land_refs, send_sems, recv_sems):
    x, y, c, chips = _place()
    me = 4 * x + 2 * y + c
    sends, recvs = [], []
    for k, (part, land) in enumerate(zip(part_refs, land_refs)):
        for j, chip in enumerate(chips):
            for t in range(2):
                sends.append(pltpu.make_async_remote_copy(
                    src_ref=part.at[_chip_of(chip)], dst_ref=land.at[me],
                    send_sem=send_sems.at[7 * k + 2 * j + t], recv_sem=recv_sems.at[7 * k + 2 * j + c],
                    device_id=(*chip, t), device_id_type=MESH))
                recvs.append(pltpu.make_async_remote_copy(
                    src_ref=part.at[_chip_of(chip)], dst_ref=land.at[2 * _chip_of(chip) + t],
                    send_sem=send_sems.at[7 * k + 2 * j + t], recv_sem=recv_sems.at[7 * k + 2 * j + t],
                    device_id=(*chip, t), device_id_type=MESH))
        sends.append(pltpu.make_async_remote_copy(
            src_ref=part.at[2 * x + y], dst_ref=land.at[me], send_sem=send_sems.at[7 * k + 6],
            recv_sem=recv_sems.at[7 * k + 6], device_id=(x, y, 1 - c), device_id_type=MESH))
        recvs.append(pltpu.make_async_remote_copy(
            src_ref=part.at[2 * x + y], dst_ref=land.at[4 * x + 2 * y + 1 - c],
            send_sem=send_sems.at[7 * k + 6], recv_sem=recv_sems.at[7 * k + 6], device_id=(x, y, 1 - c),
            device_id_type=MESH))
    return sends, recvs


def _reduce_scatter_start(parts, name, carry):
    n = len(parts)
    parts = [p.reshape(N_CHIPS, p.shape[0] // N_CHIPS, p.shape[1]) for p in parts]

    def body(*refs):
        part_refs, land_refs, carry_ref = refs[:n], refs[n:2 * n], refs[2 * n]
        send_sems, recv_sems = refs[2 * n + 1:2 * n + 3]
        sends, _ = _scatter_copies(part_refs, land_refs, send_sems, recv_sems)
        for cp in sends:
            cp.start()
        refs[-1][...] = carry_ref[...]

    lands = [lax.empty((N_DEV, p.shape[1], p.shape[2]), p.dtype) for p in parts]
    hbm = lambda a: pltpu.HBM(a.shape, a.dtype)
    vmem = pl.BlockSpec(memory_space=pltpu.VMEM)
    res = pl.pallas_call(
        body, name=name,
        out_shape=(pltpu.SemaphoreType.DMA((7 * n,)), pltpu.SemaphoreType.DMA((7 * n,)), *map(hbm, parts),
                   *map(hbm, lands), jax.ShapeDtypeStruct(carry.shape, carry.dtype)),
        in_specs=[ANY] * (2 * n) + [vmem], out_specs=(SEM, SEM, *([ANY] * (2 * n)), vmem),
        input_output_aliases={i: 2 + i for i in range(2 * n)}, compiler_params=SPLIT_COPY,
    )(*[pltpu.with_memory_space_constraint(a, pltpu.HBM) for a in parts + lands], carry)
    return res[:-1], res[-1]


def _reduce_scatter_wait(handle, after, name):
    after = list(after) if isinstance(after, (list, tuple)) else [after]
    send_sems, recv_sems = handle[:2]
    n = (len(handle) - 2) // 2
    parts, lands = handle[2:2 + n], handle[2 + n:]

    def body(*refs):
        part_refs, land_refs = refs[:n], refs[n:2 * n]
        send_sems, recv_sems = refs[2 * n:2 * n + 2]
        sends, recvs = _scatter_copies(part_refs, land_refs, send_sems, recv_sems)
        for cp in sends:
            cp.wait_send()
        for cp in recvs:
            cp.wait_recv()

    hbm = lambda a: pltpu.HBM(a.shape, a.dtype)
    res = pl.pallas_call(
        body, name=name, out_shape=tuple(map(hbm, list(parts) + list(lands))),
        in_specs=[ANY] * (2 * n) + [SEM, SEM] + [ANY] * len(after), out_specs=tuple([ANY] * (2 * n)),
        input_output_aliases={i: i for i in range(2 * n)}, compiler_params=SPLIT_COPY,
    )(*parts, *lands, send_sems, recv_sems, *after)
    return list(zip(res[:n], res[n:]))


def _adamw_of_shares(w, own, land, m, v, name, shares_transposed=False):
    rows, cols = w.shape
    tm = rows // 4 if rows % 32 == 0 and not shares_transposed else rows
    share_tile = (cols, rows) if shares_transposed else (tm, cols)
    x, y, c = lax.axis_index("x"), lax.axis_index("y"), lax.axis_index("c")
    where = jnp.stack([2 * x + y, 4 * x + 2 * y + c]).astype(jnp.int32)

    def body(where_ref, w_ref, own_ref, land_ref, m_ref, v_ref, g_ref, d_ref, nm_ref, nv_ref):
        me = where_ref[1]
        g_ = jnp.zeros(share_tile, F32)
        for dev in range(N_DEV):
            g_ = g_ + jnp.where(me == dev, own_ref[0], land_ref[dev]).astype(F32)
        if shares_transposed:
            g_ = g_.T
        m_ = ADAM_B1 * m_ref[...] + (1.0 - ADAM_B1) * g_
        v_ = ADAM_B2 * v_ref[...] + (1.0 - ADAM_B2) * (g_ * g_)
        m_hat = m_ / (1.0 - ADAM_B1 ** ADAM_STEP)
        v_hat = v_ / (1.0 - ADAM_B2 ** ADAM_STEP)
        g_ref[...] = g_
        d_ref[...] = -ADAM_LR * (m_hat / (jnp.sqrt(v_hat) + ADAM_EPS) + ADAM_WD * w_ref[...])
        nm_ref[...] = m_
        nv_ref[...] = v_

    tile = pl.BlockSpec((tm, cols), lambda i, where_ref: (i, 0))
    spec = pltpu.PrefetchScalarGridSpec(
        num_scalar_prefetch=1, grid=(rows // tm,),
        in_specs=[tile, pl.BlockSpec((1,) + share_tile, lambda i, where_ref: (where_ref[0], i, 0)),
                  pl.BlockSpec((N_DEV,) + share_tile, lambda i, where_ref: (0, i, 0)), tile, tile],
        out_specs=[tile] * 4)
    return pl.pallas_call(
        body, name=name, grid_spec=spec, out_shape=[jax.ShapeDtypeStruct(w.shape, F32)] * 4,
        compiler_params=_params(("arbitrary",)),
    )(where, w, own, land, m, v)


def _pack_small(values):
    flat = jnp.concatenate([values[n].reshape(-1).astype(F32) for n in SMALL])
    return jnp.pad(flat, (0, SMALL_ROWS * D_MODEL - flat.shape[0])).reshape(SMALL_ROWS, D_MODEL)


def _unpack_small(block, shapes):
    flat = block.reshape(-1)
    out, lo = {}, 0
    for n in SMALL:
        out[n] = flat[lo:lo + SMALL_SIZES[n]].reshape(shapes[n])
        lo += SMALL_SIZES[n]
    return out


def _after(token, a):
    return a + token[:1, :1].astype(a.dtype)


def kernel(x, p, mix_norm_g, w_in, sgu_w, sgu_b, sgu_norm_g, out_norm_a, out_norm_b, w_out, ffn_norm_g, w_gate, w_up, w_down, ple_norm_g, w_ple_gate, w_ple_proj, final_norm_g, loss_target, m_mix_norm_g, m_w_in, m_sgu_w, m_sgu_b, m_sgu_norm_g, m_out_norm_a, m_out_norm_b, m_w_out, m_ffn_norm_g, m_w_gate, m_w_up, m_w_down, m_ple_norm_g, m_w_ple_gate, m_w_ple_proj, m_final_norm_g, v_mix_norm_g, v_w_in, v_sgu_w, v_sgu_b, v_sgu_norm_g, v_out_norm_a, v_out_norm_b, v_w_out, v_ffn_norm_g, v_w_gate, v_w_up, v_w_down, v_ple_norm_g, v_w_ple_gate, v_w_ple_proj, v_final_norm_g):
    given = dict(locals())
    drop_lead = lambda a, lead: a.reshape(a.shape[lead:])
    xs, ps, target = drop_lead(x, 1), drop_lead(p, 2), drop_lead(loss_target, 1)
    s = xs.shape[0]
    shard = lambda name: drop_lead(given[name], 1)

    def slab_of(name):
        local = shard(name).astype(MXU_DTYPE)
        return local.T if SLAB_IS_TRANSPOSED[name] else local

    w_in_gather, token = _two_level_start(_stack_with_own(slab_of('w_in')))
    later = ['w_out', 'w_gate', 'w_up', 'w_down', 'w_ple_gate', 'w_ple_proj']
    later_stacks = [_stack_with_own(slab_of(n)) for n in later]
    cos_t, sin_t = (_after(token, table) for table in _rope_tables(s))
    tril = jnp.tril(jnp.ones((CHUNK, CHUNK), F32))
    w_tril = (sgu_w.reshape(HEADS_A, CHUNK, CHUNK) * tril).astype(MXU_DTYPE)
    w_tril_t = jnp.swapaxes(w_tril, 1, 2)
    g = {n: given[n].reshape(1, -1) for n in SMALL if n not in ('sgu_w', 'sgu_b')}
    hn1 = _mix_norm(xs, g['mix_norm_g'], cos_t)
    w_in_gather = _two_level_forward(w_in_gather, [hn1, sin_t, w_tril, w_tril_t] + later_stacks)
    w_in_t = _two_level_wait(w_in_gather)
    gather, placed = _all_gather_start(later_stacks, w_in_t, g['mix_norm_g'])
    bias = jnp.repeat(sgu_b.reshape(HEADS_A, CHUNK).T, HEAD_DIM, axis=1)

    uv, q, k, v, q1, k1, v1 = _in_fwd(hn1, w_in_t, cos_t, sin_t, placed)
    ya_n = _sgu_fwd(uv, w_tril, bias, g['sgu_norm_g'], g['out_norm_a'])
    branches = [_attn_fwd_local(q1, k1, v1)] + [_attn_fwd_branch(q, k, v, dil) for dil in DILATIONS[:-1]]
    y_b, lse = _attn_fwd_branch(q, k, v, DILATIONS[-1], earlier=branches)
    stacks = dict(zip(later, _all_gather_wait(gather, lse)))
    w_gate_t, w_up_t, w_pp_t = stacks['w_gate'], stacks['w_up'], stacks['w_ple_proj']
    h1, y_n = _out_fwd(ya_n, y_b, g['out_norm_b'], stacks['w_out'], xs)
    h2, gate, up, hn2 = _ffn_fwd(h1, g['ffn_norm_g'], w_gate_t, w_up_t, stacks['w_down'])
    loss, dh2, dz, dpp, hn3, d_ple_g, d_final_g = _ple_loss(
        h2, ps, target, g['ple_norm_g'], stacks['w_ple_gate'], w_pp_t, g['final_norm_g'])

    share = {}
    share['w_ple_gate'] = _wgrad(hn3, dz, "wgrad_ple_gate")
    share['w_ple_proj'] = _wgrad(dpp, ps, "wgrad_ple_proj")
    scatter_1, g_ffn = _reduce_scatter_start([share['w_ple_gate'], share['w_ple_proj']], "reduce_scatter_start_1",
                                             g['ffn_norm_g'])
    dh1, act, dgate, dup, d_ffn_g = _ffn_bwd(dh2, h1, gate, up, g_ffn, stacks['w_down'], w_gate_t, w_up_t)
    share['w_down'] = _wgrad(act, dh2, "wgrad_down")
    share['w_gate'] = _wgrad(dgate, hn2, "wgrad_gate")
    share['w_up'] = _wgrad(dup, hn2, "wgrad_up")
    scatter_2, g_out_b = _reduce_scatter_start([share['w_down'], share['w_gate'], share['w_up']],
                                               "reduce_scatter_start_2", g['out_norm_b'])
    dya_n, dyb, d_out_b = _out_bwd(dh1, y_b, g_out_b, stacks['w_out'])
    share['w_out'] = _wgrad(y_n, dh1, "wgrad_out")
    scatter_3, g_sgu = _reduce_scatter_start([share['w_out']], "reduce_scatter_start_3", g['sgu_norm_g'])
    grads = _attn_bwd_local(q1, k1, v1, dyb, y_b, lse)
    for dil in DILATIONS:
        grads = _attn_bwd_branch(q, k, v, dyb, y_b, lse, grads, dil)
    duv, d_sgu_w, d_sgu_b, d_sgu_g, d_out_a = _sgu_bwd(uv, dya_n, w_tril, w_tril_t, bias, g_sgu, g['out_norm_a'])
    dproj = _in_bwd_proj(duv, grads[0], grads[1], grads[2], cos_t, sin_t)
    share['w_in'] = _wgrad(dproj, hn1, "wgrad_in")
    scatter_4, g_mix = _reduce_scatter_start([share['w_in']], "reduce_scatter_start_4", g['mix_norm_g'])

    grads, deltas, new_m, new_v = {}, {}, {}, {}
    add_lead = lambda a: a.reshape((1,) + a.shape)

    done = {}

    def finish(names, handles, after, tag):
        landed = []
        for i, handle in enumerate(handles):
            landed += _reduce_scatter_wait(handle, after, "reduce_scatter_wait_%s%d" % (tag, i))
        for n, (own, land) in zip(names, landed):
            turn_shares = SLAB_IS_TRANSPOSED[n] and shard(n).shape[-1] % LANES == 0
            turn = (lambda a: a.T) if SLAB_IS_TRANSPOSED[n] and not turn_shares else (lambda a: a)
            res = _adamw_of_shares(turn(shard(n)), own, land, turn(shard("m_" + n)), turn(shard("v_" + n)),
                                   "adamw_" + n, turn_shares)
            grads[n], deltas[n], new_m[n], new_v[n] = (add_lead(turn(a)) for a in res)
            done[n] = res[0]

    grad_x, d_mix_g = _in_bwd_x(dproj, w_in_t, xs, g_mix, dh1)

    gs = {'mix_norm_g': d_mix_g, 'sgu_w': d_sgu_w, 'sgu_b': d_sgu_b[:, :HEADS_A].T, 'sgu_norm_g': d_sgu_g,
          'out_norm_a': d_out_a, 'out_norm_b': d_out_b, 'ffn_norm_g': d_ffn_g, 'ple_norm_g': d_ple_g,
          'final_norm_g': d_final_g}
    gs_block = _pack_small(gs).at[SMALL_ROWS - 1, 0].set(loss[0, 0])
    to_all = jnp.broadcast_to(gs_block[None], (N_CHIPS,) + gs_block.shape).reshape(-1, D_MODEL)
    scatter_small, token = _reduce_scatter_start([to_all], "small_all_reduce_start", g['mix_norm_g'])

    finish(['w_ple_gate', 'w_ple_proj', 'w_down', 'w_gate', 'w_up', 'w_out'], [scatter_1, scatter_2, scatter_3], token,
           "early")
    finish(['w_in'], [scatter_4], [done[n] for n in ('w_down', 'w_gate', 'w_up', 'w_out')], "last")
    (own, land), = _reduce_scatter_wait(scatter_small, done['w_in'], "small_all_reduce_wait")
    small = {n: given[n] for n in SMALL}
    small_res = _adamw_of_shares(_pack_small(small), own, land, _pack_small({n: given["m_" + n] for n in SMALL}),
                                 _pack_small({n: given["v_" + n] for n in SMALL}), "adamw_small")
    loss_out = small_res[0][SMALL_ROWS - 1, 0]
    small_shapes = {n: given[n].shape for n in SMALL}
    for res, blk in zip((grads, deltas, new_m, new_v), small_res):
        res.update(_unpack_small(blk, small_shapes))

    outs = [loss_out, add_lead(grad_x)]
    for res in (grads, deltas, new_m, new_v):
        outs += [res[n] for n in WEIGHT_NAMES]
    return tuple(outs)
```

```python
import functools
import itertools
import math

import jax
import jax.numpy as jnp
import numpy as np
from jax import lax
from jax.experimental import pallas as pl
from jax.experimental.pallas import tpu as pltpu

F32 = jnp.float32
MXU_DTYPE = jnp.bfloat16

D_MODEL = 1024
HEAD_DIM = 64
HEADS_A = 4
HEADS_B = 12
WIDTH_A = HEADS_A * HEAD_DIM
WIDTH_B = HEADS_B * HEAD_DIM
CHUNK = 128
BLOCK = 128
DILATIONS = (4, 16)
ROPE_THETA = 10000.0
D_FF = 2816
FF_HALF = D_FF // 2
FF_STRIPS = ((0, 1024), (1024, 2048), (2048, D_FF))
PLE_DIM = 256
IN_COLS = 2 * WIDTH_A + 3 * WIDTH_B
EPS = 1e-6
LANES = 128
N_CHIPS = 4
N_DEV = 8

ADAM_LR = 0.001
ADAM_B1 = 0.9
ADAM_B2 = 0.999
ADAM_EPS = 1e-08
ADAM_WD = 0.01
ADAM_STEP = 10

VMEM_LIMIT = 56 * 1024 * 1024

WEIGHT_NAMES = ['mix_norm_g', 'w_in', 'sgu_w', 'sgu_b', 'sgu_norm_g', 'out_norm_a', 'out_norm_b', 'w_out',
                'ffn_norm_g', 'w_gate', 'w_up', 'w_down', 'ple_norm_g', 'w_ple_gate', 'w_ple_proj', 'final_norm_g']
SMALL = ['mix_norm_g', 'sgu_w', 'sgu_b', 'sgu_norm_g', 'out_norm_a', 'out_norm_b', 'ffn_norm_g', 'ple_norm_g',
         'final_norm_g']
SMALL_SIZES = {'mix_norm_g': 1024, 'sgu_w': 65536, 'sgu_b': 512, 'sgu_norm_g': 256, 'out_norm_a': 256,
               'out_norm_b': 768, 'ffn_norm_g': 1024, 'ple_norm_g': 1024, 'final_norm_g': 1024}
SMALL_ROWS = 72


def _params(semantics=None):
    return pltpu.CompilerParams(dimension_semantics=semantics, vmem_limit_bytes=VMEM_LIMIT)


def _full(shape):
    nd = len(shape)
    return pl.BlockSpec(shape, lambda i: (0,) * nd, pipeline_mode=pl.Buffered(1))


def _rows(tm, width):
    return pl.BlockSpec((tm, width), lambda i: (i, 0))


def _rms_stats(x):
    r = lax.rsqrt(jnp.mean(x * x, axis=-1, keepdims=True) + EPS)
    return x * r, r


def _rms_bwd(dn, n, r):
    return r * (dn - n * jnp.mean(dn * n, axis=-1, keepdims=True))


def _dot(a, b):
    return jnp.dot(a, b, preferred_element_type=F32)


def _dot_nt(a, b):
    return lax.dot_general(a, b, (((1,), (1,)), ((), ())), preferred_element_type=F32)


def _dot_tn(a, b):
    return lax.dot_general(a, b, (((0,), (0,)), ((), ())), preferred_element_type=F32)


def _gelu_parts(x):
    c = math.sqrt(2.0 / math.pi)
    t = jnp.tanh(c * (x + 0.044715 * x * x * x))
    return 0.5 * x * (1.0 + t), t


def _gelu_grad(x, t):
    c = math.sqrt(2.0 / math.pi)
    return 0.5 * (1.0 + t) + 0.5 * x * (1.0 - t * t) * c * (1.0 + 3.0 * 0.044715 * x * x)


def _half_masks(dtype):
    lane = lax.broadcasted_iota(jnp.int32, (BLOCK, LANES), 1)
    lo = (lane < HEAD_DIM).astype(F32)
    return lo.astype(dtype), (1.0 - lo).astype(dtype)


def _rope_partner(t):
    lane = lax.broadcasted_iota(jnp.int32, t.shape, 1)
    first_half = (lane % HEAD_DIM) < (HEAD_DIM // 2)
    return jnp.where(first_half, pltpu.roll(t, LANES - HEAD_DIM // 2, 1), pltpu.roll(t, HEAD_DIM // 2, 1))


PAIRS_ABREAST = 2
RESIDUES_PER_STEP = 4
L_BLOCK = 256
L_GROUP = 16


def _store_l256(scr, out_ref, cols, value, chunk_ref=None):
    tm = value.shape[0]
    half = L_GROUP // 2
    scr[...] = value
    for blk in range(tm // L_BLOCK):
        pieces = [scr[pl.ds(blk * L_BLOCK + r, L_GROUP, stride=L_GROUP), :] for r in range(L_GROUP)]
        for r, piece in enumerate(pieces):
            lo = blk * L_BLOCK + r * L_GROUP
            out_ref[lo:lo + L_GROUP, cols] = piece.astype(out_ref.dtype)
        if chunk_ref is not None:
            for chunk in range(L_BLOCK // BLOCK):
                for r in range(0, L_GROUP, 2):
                    lo = blk * L_BLOCK + chunk * BLOCK + r * half
                    both = [p[chunk * half:(chunk + 1) * half] for p in pieces[r:r + 2]]
                    chunk_ref[lo:lo + L_GROUP, cols] = jnp.concatenate(both, axis=0).astype(chunk_ref.dtype)


def _load_l256(col_refs, tm):
    cols = []
    for ref in col_refs:
        pieces = [ref[pl.ds(blk * L_BLOCK + i, L_GROUP, stride=L_GROUP), :]
                  for blk in range(tm // L_BLOCK) for i in range(L_GROUP)]
        cols.append(jnp.concatenate(pieces, axis=0))
    return jnp.concatenate(cols, axis=1)


def _col_specs(tm, width):
    return [pl.BlockSpec((tm, LANES), lambda i, j=j: (i, j)) for j in range(width // LANES)]


def _in_fwd(x, g_mix, w_in_t, cos_t, sin_t):
    s = x.shape[0]
    tm = 512

    def body(x_ref, g_ref, wt_ref, cos_ref, sin_ref, uv_ref, q_ref, k_ref, v_ref, q1_ref, k1_ref, v1_ref, hn_ref,
             *scrs):
        n, _ = _rms_stats(x_ref[...])
        hn = (n * g_ref[...]).astype(MXU_DTYPE)
        hn_ref[...] = hn
        cos = cos_ref[...]
        sin = sin_ref[...]
        strip = 2 * LANES
        for j in range(IN_COLS // strip):
            proj = _dot_nt(hn, wt_ref[j * strip:(j + 1) * strip, :])
            lo = j * strip - 2 * WIDTH_A
            if lo < 0:
                uv_ref[:, j * strip:(j + 1) * strip] = proj
                continue
            which, lo = divmod(lo, WIDTH_B)
            for i in range(strip // LANES):
                t = proj[:, i * LANES:(i + 1) * LANES]
                cols = slice(lo + i * LANES, lo + (i + 1) * LANES)
                scr = scrs[i]
                if which == 0:
                    _store_l256(scr, q_ref, cols, (t * cos + _rope_partner(t) * sin) * (HEAD_DIM ** -0.5), q1_ref)
                elif which == 1:
                    _store_l256(scr, k_ref, cols, t * cos + _rope_partner(t) * sin, k1_ref)
                else:
                    _store_l256(scr, v_ref, cols, t, v1_ref)

    return pl.pallas_call(
        body, name="in_fwd", grid=(s // tm,), scratch_shapes=[pltpu.VMEM((tm, LANES), F32)] * 2,
        in_specs=[_rows(tm, D_MODEL), _full((1, D_MODEL)), _full((IN_COLS, D_MODEL)), _rows(tm, LANES),
                  _rows(tm, LANES)],
        out_specs=[_rows(tm, 2 * WIDTH_A)] + [_rows(tm, WIDTH_B)] * 6 + [_rows(tm, D_MODEL)],
        out_shape=[jax.ShapeDtypeStruct((s, 2 * WIDTH_A), F32)] + [jax.ShapeDtypeStruct((s, WIDTH_B), MXU_DTYPE)] * 6
        + [jax.ShapeDtypeStruct((s, D_MODEL), MXU_DTYPE)],
        compiler_params=_params(("arbitrary",)),
    )(x, g_mix, w_in_t, cos_t, sin_t)


class _Branch:
    def __init__(self, dil, s):
        i = np.arange(L_GROUP)
        self.res = RESIDUES_PER_STEP
        if dil == 16:
            nblk = BLOCK // 16
            self.grid = (16 // self.res, s // (L_BLOCK * nblk))
            self.shape = (nblk, 1, self.res, L_GROUP)
            self.index = lambda r, n: (n, r // (4 // self.res), r % (4 // self.res), 0, 0)
            pos = (np.arange(nblk)[:, None] * 16 + i[None, :]).reshape(-1)
        else:
            nblk = BLOCK // 64
            self.grid = (4 // self.res, s // (L_BLOCK * nblk))
            self.shape = (nblk, 4, self.res, L_GROUP)
            self.index = lambda r, n: (n, 0, r, 0, 0)
            pos = (np.arange(nblk)[:, None, None] * 64 + np.arange(4)[None, :, None]
                   + 4 * i[None, None, :]).reshape(-1)
        self.qn = pos.shape[0]
        self.nb = self.grid[1]
        dist = pos[:, None] - np.concatenate([pos - self.qn, pos])[None, :]
        band = (dist >= 0) & (dist <= BLOCK)
        start = band & (np.arange(2 * self.qn)[None, :] >= self.qn)
        self.bias = np.where(np.stack([band, start]), 0.0, -np.inf).astype(np.float32)

    def view(self, a):
        return a.reshape(a.shape[0] // L_BLOCK, 4, 4, L_GROUP, a.shape[1])

    def spec(self, w, step=lambda n: n):
        return pl.BlockSpec(self.shape + (w,), lambda r, n: self.index(r, step(n)))

    def bias_spec(self, step=lambda n: n):
        return pl.BlockSpec((1, self.qn, 2 * self.qn), lambda r, n: (jnp.where(step(n) == 0, 1, 0), 0, 0))

    def load(self, ref, cols=slice(None), j=0):
        x = ref[:, :, j, :, cols]
        return x.reshape(self.qn, x.shape[-1])

    def store(self, ref, cols, value, j=0):
        ref[:, :, j, :, cols] = value.reshape(self.shape[:2] + (L_GROUP, value.shape[-1]))


def _attn_fwd_branch(q, k, v, dil, earlier=()):
    s = q.shape[0]
    br = _Branch(dil, s)
    qn = br.qn
    nearly = len(earlier)

    def body(bias_ref, q_ref, kc_ref, kp_ref, vc_ref, vp_ref, *refs):
        early_refs, (o_ref, lse_ref) = refs[:2 * nearly], refs[2 * nearly:]
        bias2 = jnp.concatenate([bias_ref[0], bias_ref[0]], axis=0)
        lo = lax.broadcasted_iota(jnp.int32, (qn, LANES), 1) < HEAD_DIM
        mask_lo = lo.astype(F32).astype(MXU_DTYPE)
        for j, hp in itertools.product(range(br.res), range(HEADS_B // 2)):
            cols = slice(hp * LANES, (hp + 1) * LANES)
            qp = br.load(q_ref, cols, j)
            kcat = jnp.concatenate([br.load(kp_ref, cols, j), br.load(kc_ref, cols, j)], axis=0)
            vcat = jnp.concatenate([br.load(vp_ref, cols, j), br.load(vc_ref, cols, j)], axis=0)
            sc = _dot_nt(jnp.concatenate([qp * mask_lo, qp * (1 - mask_lo)], axis=0), kcat) + bias2
            m = jnp.max(sc, axis=1, keepdims=True)
            p = jnp.exp(sc - m)
            l = jnp.sum(p, axis=1, keepdims=True)
            out = _dot(p.astype(MXU_DTYPE), vcat) / l
            lse = m + jnp.log(l)
            outs = [br.load(r, cols, j) for r in early_refs[:nearly]] + [jnp.where(lo, out[:qn], out[qn:])]
            lses = [br.load(r, cols, j) for r in early_refs[nearly:]] + [jnp.where(lo, lse[:qn], lse[qn:])]
            if nearly:
                top = functools.reduce(jnp.maximum, lses)
                ws = [jnp.exp(x - top) for x in lses]
                den = functools.reduce(jnp.add, ws)
                outs = [functools.reduce(jnp.add, [w * o for w, o in zip(ws, outs)]) / den]
                lses = [top + jnp.log(den)]
            br.store(o_ref, cols, outs[0], j)
            br.store(lse_ref, cols, lses[0], j)

    before = lambda n: jnp.maximum(n - 1, 0)
    res = pl.pallas_call(
        body, name="attn_fwd_d%d" % dil, grid=br.grid,
        in_specs=[br.bias_spec(), br.spec(WIDTH_B), br.spec(WIDTH_B), br.spec(WIDTH_B, before), br.spec(WIDTH_B),
                  br.spec(WIDTH_B, before)] + [br.spec(WIDTH_B)] * (2 * nearly),
        out_specs=[br.spec(WIDTH_B), br.spec(WIDTH_B)],
        out_shape=[jax.ShapeDtypeStruct((s // L_BLOCK, 4, 4, L_GROUP, WIDTH_B), F32)] * 2,
        compiler_params=_params(("arbitrary", "arbitrary")),
    )(jnp.asarray(br.bias), br.view(q), br.view(k), br.view(k), br.view(v), br.view(v),
      *[br.view(o) for o, _ in earlier], *[br.view(x) for _, x in earlier])
    return tuple(a.reshape(s, WIDTH_B) for a in res)


LOCAL_CHUNKS = 4


def _local_bias():
    row = np.arange(BLOCK)
    pos = L_GROUP * (row % (L_GROUP // 2)) + row // (L_GROUP // 2)
    dist = pos[:, None] - np.concatenate([pos - BLOCK, pos])[None, :]
    band = (dist >= 0) & (dist <= BLOCK)
    start = band & (np.arange(2 * BLOCK)[None, :] >= BLOCK)
    return np.where(np.stack([band, start]), 0.0, -np.inf).astype(np.float32)


def _chunk_view(a):
    return a.reshape(a.shape[0] // L_BLOCK, L_GROUP, 2, L_GROUP // 2, a.shape[1])


def _chunk_of(ref, j, cols=slice(None)):
    x = ref[j // 2, :, j % 2, :, cols]
    return x.reshape(BLOCK, x.shape[-1])


def _put_chunk(ref, j, cols, value):
    ref[j // 2, :, j % 2, :, cols] = value.reshape(L_GROUP, L_GROUP // 2, value.shape[-1])


def _local_keys(cur_ref, before_ref, j, cols):
    here = slice(j * BLOCK, (j + 1) * BLOCK)
    before = before_ref[:, cols] if j == 0 else cur_ref[(j - 1) * BLOCK:j * BLOCK, cols]
    return jnp.concatenate([before, cur_ref[here, cols]], axis=0)


def _local_specs(s, step=lambda n: n):
    rows = LOCAL_CHUNKS * BLOCK
    cur = pl.BlockSpec((rows, WIDTH_B), lambda n: (step(n), 0))
    before = pl.BlockSpec((BLOCK, WIDTH_B), lambda n: (jnp.maximum(LOCAL_CHUNKS * step(n) - 1, 0), 0))
    return [cur, cur, before, cur, before]


def _attn_fwd_local(q1, k1, v1):
    s = q1.shape[0]
    rows = LOCAL_CHUNKS * BLOCK
    qn = BLOCK

    def body(bias_ref, q_ref, kc_ref, kp_ref, vc_ref, vp_ref, o_ref, lse_ref):
        first = jnp.where(pl.program_id(0) == 0, bias_ref[1], bias_ref[0])
        biases = [jnp.concatenate([b, b], axis=0) for b in (first, bias_ref[0])]
        lo = lax.broadcasted_iota(jnp.int32, (qn, LANES), 1) < HEAD_DIM
        mask_lo = lo.astype(F32).astype(MXU_DTYPE)
        for j, hp in itertools.product(range(LOCAL_CHUNKS), range(HEADS_B // 2)):
            cols = slice(hp * LANES, (hp + 1) * LANES)
            qp = q_ref[j * BLOCK:(j + 1) * BLOCK, cols]
            kcat = _local_keys(kc_ref, kp_ref, j, cols)
            vcat = _local_keys(vc_ref, vp_ref, j, cols)
            sc = _dot_nt(jnp.concatenate([qp * mask_lo, qp * (1 - mask_lo)], axis=0), kcat) + biases[min(j, 1)]
            m = jnp.max(sc, axis=1, keepdims=True)
            p = jnp.exp(sc - m)
            l = jnp.sum(p, axis=1, keepdims=True)
            out = _dot(p.astype(MXU_DTYPE), vcat) / l
            lse = m + jnp.log(l)
            _put_chunk(o_ref, j, cols, jnp.where(lo, out[:qn], out[qn:]))
            _put_chunk(lse_ref, j, cols, jnp.where(lo, lse[:qn], lse[qn:]))

    out_spec = pl.BlockSpec((LOCAL_CHUNKS // 2, L_GROUP, 2, L_GROUP // 2, WIDTH_B), lambda n: (n, 0, 0, 0, 0))
    res = pl.pallas_call(
        body, name="attn_fwd_d1", grid=(s // rows,),
        in_specs=[_full((2, BLOCK, 2 * BLOCK))] + _local_specs(s), out_specs=[out_spec] * 2,
        out_shape=[jax.ShapeDtypeStruct((s // L_BLOCK, L_GROUP, 2, L_GROUP // 2, WIDTH_B), F32)] * 2,
        compiler_params=_params(("arbitrary",)),
    )(jnp.asarray(_local_bias()), q1, k1, k1, v1, v1)
    return tuple(a.reshape(s, WIDTH_B) for a in res)


def _attn_bwd_local(q1, k1, v1, do, o, lse):
    s = q1.shape[0]
    rows = LOCAL_CHUNKS * BLOCK
    nsteps = s // rows
    qn = BLOCK

    def body(bias_ref, q_ref, kc_ref, kp_ref, vc_ref, vp_ref, do_ref, o_ref, lse_ref, dq_ref, dk_ref, dv_ref,
             dk_buf, dv_buf):
        n = pl.program_id(0)

        @pl.when(n == 0)
        def _():
            dk_buf[...] = jnp.zeros_like(dk_buf)
            dv_buf[...] = jnp.zeros_like(dv_buf)

        @pl.when(n < nsteps)
        def _():
            first = jnp.where(n == 0, bias_ref[1], bias_ref[0])
            biases = [jnp.concatenate([b, b], axis=0) for b in (first, bias_ref[0])]
            lo = lax.broadcasted_iota(jnp.int32, (qn, LANES), 1) < HEAD_DIM
            mask_f = lo.astype(F32)
            mask_lo = mask_f.astype(MXU_DTYPE)
            dk_buf[LOCAL_CHUNKS:] = jnp.zeros((LOCAL_CHUNKS, qn, WIDTH_B), F32)
            dv_buf[LOCAL_CHUNKS:] = jnp.zeros((LOCAL_CHUNKS, qn, WIDTH_B), F32)

            def prepare(j, hp):
                cols = slice(hp * LANES, (hp + 1) * LANES)
                qp = q_ref[j * BLOCK:(j + 1) * BLOCK, cols]
                dop = _chunk_of(do_ref, j, cols)
                prod = dop * _chunk_of(o_ref, j, cols)
                prod_lo = prod * mask_f
                lse = _chunk_of(lse_ref, j, cols)
                return dict(
                    j=j, cols=cols, kcat=_local_keys(kc_ref, kp_ref, j, cols), vcat=_local_keys(vc_ref, vp_ref, j, cols),
                    qs=jnp.concatenate([qp * mask_lo, qp * (1 - mask_lo)], axis=0),
                    dos=jnp.concatenate([dop * mask_f, dop * (1.0 - mask_f)], axis=0).astype(MXU_DTYPE),
                    delta=jnp.concatenate([jnp.sum(prod_lo, axis=1, keepdims=True),
                                           jnp.sum(prod - prod_lo, axis=1, keepdims=True)], axis=0),
                    lse2=jnp.concatenate([lse[:, :1], lse[:, HEAD_DIM:HEAD_DIM + 1]], axis=0))

            def scores(t):
                t['sc'] = _dot_nt(t['qs'], t['kcat'])
                t['dp'] = _dot_nt(t['dos'], t['vcat'])

            def softmax(t):
                p = jnp.exp(t['sc'] + biases[min(t['j'], 1)] - t['lse2'])
                t['ds'] = (p * (t['dp'] - t['delta'])).astype(MXU_DTYPE)
                t['p'] = p.astype(MXU_DTYPE)

            def gradients(t):
                t['dvc'] = _dot_tn(t['p'], t['dos'])
                t['dkc'] = _dot_tn(t['ds'], t['qs'])
                t['dq2'] = _dot(t['ds'], t['kcat'])

            def store(t):
                j, cols = t['j'], t['cols']
                _put_chunk(dq_ref, j, cols, jnp.where(lo, t['dq2'][:qn], t['dq2'][qn:]))
                for buf, both in ((dk_buf, t['dkc']), (dv_buf, t['dvc'])):
                    buf[LOCAL_CHUNKS + j - 1, :, cols] += both[:qn]
                    buf[LOCAL_CHUNKS + j, :, cols] += both[qn:]

            for j, first_pair in itertools.product(range(LOCAL_CHUNKS), range(0, HEADS_B // 2, PAIRS_ABREAST)):
                group = [prepare(j, hp) for hp in range(first_pair, first_pair + PAIRS_ABREAST)]
                for stage in (scores, softmax, gradients, store):
                    for t in group:
                        stage(t)

        for j in range(LOCAL_CHUNKS):
            _put_chunk(dk_ref, j, slice(None), dk_buf[j])
            _put_chunk(dv_ref, j, slice(None), dv_buf[j])
        dk_buf[:LOCAL_CHUNKS] = dk_buf[LOCAL_CHUNKS:]
        dv_buf[:LOCAL_CHUNKS] = dv_buf[LOCAL_CHUNKS:]

    cur = lambda n: jnp.minimum(n, nsteps - 1)
    late = lambda n: jnp.maximum(n - 1, 0)
    view_spec = lambda step: pl.BlockSpec((LOCAL_CHUNKS // 2, L_GROUP, 2, L_GROUP // 2, WIDTH_B),
                                          lambda n: (step(n), 0, 0, 0, 0))
    res = pl.pallas_call(
        body, name="attn_bwd_d1", grid=(nsteps + 1,),
        in_specs=[_full((2, BLOCK, 2 * BLOCK))] + _local_specs(s, cur) + [view_spec(cur)] * 3,
        out_specs=[view_spec(cur), view_spec(late), view_spec(late)],
        out_shape=[jax.ShapeDtypeStruct((s // L_BLOCK, L_GROUP, 2, L_GROUP // 2, WIDTH_B), F32)] * 3,
        scratch_shapes=[pltpu.VMEM((2 * LOCAL_CHUNKS, qn, WIDTH_B), F32)] * 2,
        compiler_params=_params(("arbitrary",)),
    )(jnp.asarray(_local_bias()), q1, k1, k1, v1, v1, _chunk_view(do), _chunk_view(o), _chunk_view(lse))
    return tuple(a.reshape(s, WIDTH_B) for a in res)


def _sgu_forward_tile(uv, w_ref, bias, g_sgu):
    tm = uv.shape[0]
    u = uv[:, :WIDTH_A]
    v = uv[:, WIDTH_A:]
    ug, tu = _gelu_parts(u)
    vg, tv = _gelu_parts(v)
    mu = jnp.mean(vg, axis=-1, keepdims=True)
    vc = vg - mu
    rs = lax.rsqrt(jnp.mean(vc * vc, axis=-1, keepdims=True) + EPS)
    vhat = vc * rs
    vn = (vhat * g_sgu).astype(MXU_DTYPE)
    masks = _half_masks(MXU_DTYPE)
    chunks = []
    for c in range(tm // CHUNK):
        rows = slice(c * CHUNK, (c + 1) * CHUNK)
        groups = []
        for gp in range(2):
            vn_g = vn[rows, gp * LANES:(gp + 1) * LANES]
            groups.append(_dot(w_ref[2 * gp], vn_g * masks[0]) + _dot(w_ref[2 * gp + 1], vn_g * masks[1]))
        chunks.append(jnp.concatenate(groups, axis=1) + bias)
    mixed = jnp.concatenate(chunks, axis=0)
    return dict(u=u, v=v, ug=ug, tu=tu, tv=tv, rs=rs, vhat=vhat, vn=vn, mixed=mixed, ya=ug * mixed)


def _sgu_fwd(uv, w_tril, bias, g_sgu, g_a):
    s = uv.shape[0]
    tm = 512

    def body(uv_ref, w_ref, b_ref, gs_ref, ga_ref, o_ref):
        t = _sgu_forward_tile(uv_ref[...], w_ref, b_ref[...], gs_ref[...])
        n, _ = _rms_stats(t['ya'])
        o_ref[...] = (n * ga_ref[...]).astype(MXU_DTYPE)

    return pl.pallas_call(
        body, name="sgu_fwd", grid=(s // tm,),
        in_specs=[_rows(tm, 2 * WIDTH_A), _full((HEADS_A, CHUNK, CHUNK)), _full((CHUNK, WIDTH_A)),
                  _full((1, WIDTH_A)), _full((1, WIDTH_A))],
        out_specs=_rows(tm, WIDTH_A), out_shape=jax.ShapeDtypeStruct((s, WIDTH_A), MXU_DTYPE),
        compiler_params=_params(("arbitrary",)),
    )(uv, w_tril, bias, g_sgu, g_a)


def _out_fwd(ya_n, y_b, g_b, w_out, x):
    s = x.shape[0]
    tm = 512
    nc = WIDTH_B // LANES

    def body(ya_ref, *refs):
        yb_refs = refs[:nc]
        g_ref, w_ref, x_ref, h_ref, yn_ref = refs[nc:]
        n, _ = _rms_stats(_load_l256(yb_refs, tm))
        yn = jnp.concatenate([ya_ref[...], (n * g_ref[...]).astype(MXU_DTYPE)], axis=1)
        yn_ref[...] = yn
        h_ref[...] = x_ref[...] + _dot(yn, w_ref[...])

    return pl.pallas_call(
        body, name="out_fwd", grid=(s // tm,),
        in_specs=[_rows(tm, WIDTH_A)] + _col_specs(tm, WIDTH_B) + [_full((1, WIDTH_B)), _full((D_MODEL, D_MODEL)),
                                                                 _rows(tm, D_MODEL)],
        out_specs=[_rows(tm, D_MODEL), _rows(tm, D_MODEL)],
        out_shape=[jax.ShapeDtypeStruct((s, D_MODEL), F32), jax.ShapeDtypeStruct((s, D_MODEL), MXU_DTYPE)],
        compiler_params=_params(("arbitrary",)),
    )(ya_n, *([y_b] * nc), g_b, w_out, x)


def _ffn_fwd(h1, g_ffn, w_gate_t, w_up_t, w_down):
    s = h1.shape[0]
    tm = 512

    def body(h_ref, g_ref, wgt_ref, wut_ref, wd_ref, o_ref, gate_ref, up_ref, hn_ref):
        h = h_ref[...]
        n, _ = _rms_stats(h)
        hn = (n * g_ref[...]).astype(MXU_DTYPE)
        hn_ref[...] = hn
        strips = [dict(cols=slice(lo, hi)) for lo, hi in FF_STRIPS]

        def project(t):
            t['gate'] = _dot_nt(hn, wgt_ref[t['cols'], :])
            t['up'] = _dot_nt(hn, wut_ref[t['cols'], :])

        def activate(t):
            gate, up = t['gate'], t['up']
            gate_ref[:, t['cols']] = gate.astype(MXU_DTYPE)
            up_ref[:, t['cols']] = up.astype(MXU_DTYPE)
            t['act'] = (gate * jax.nn.sigmoid(gate) * up).astype(MXU_DTYPE)

        def down(t):
            return _dot(t['act'], wd_ref[t['cols'], :])

        out = h
        project(strips[0])
        for i, t in enumerate(strips):
            if i + 1 < len(strips):
                project(strips[i + 1])
            activate(t)
            out = out + down(t)
        o_ref[...] = out

    return pl.pallas_call(
        body, name="ffn_fwd", grid=(s // tm,),
        in_specs=[_rows(tm, D_MODEL), _full((1, D_MODEL)), _full((D_FF, D_MODEL)), _full((D_FF, D_MODEL)),
                  _full((D_FF, D_MODEL))],
        out_specs=[_rows(tm, D_MODEL), _rows(tm, D_FF), _rows(tm, D_FF), _rows(tm, D_MODEL)],
        out_shape=[jax.ShapeDtypeStruct((s, D_MODEL), F32), jax.ShapeDtypeStruct((s, D_FF), MXU_DTYPE),
                   jax.ShapeDtypeStruct((s, D_FF), MXU_DTYPE), jax.ShapeDtypeStruct((s, D_MODEL), MXU_DTYPE)],
        compiler_params=_params(("arbitrary",)),
    )(h1, g_ffn, w_gate_t, w_up_t, w_down)


def _ple_loss(h2, p, target, g_ple, w_pg, w_pp_t, g_final):
    s = h2.shape[0]
    tm = 512

    def body(h_ref, p_ref, t_ref, gp_ref, wg_ref, wpt_ref, gf_ref,
             loss_ref, dh_ref, dz_ref, dpp_ref, hn_ref, dgp_ref, dgf_ref):
        @pl.when(pl.program_id(0) == 0)
        def _():
            loss_ref[...] = jnp.zeros_like(loss_ref)
            dgp_ref[...] = jnp.zeros_like(dgp_ref)
            dgf_ref[...] = jnp.zeros_like(dgf_ref)

        h2t = h_ref[...]
        n2, r2 = _rms_stats(h2t)
        hn = (n2 * gp_ref[...]).astype(MXU_DTYPE)
        hn_ref[...] = hn
        gate = jax.nn.sigmoid(_dot(hn, wg_ref[...]))
        pp = _dot_nt(p_ref[...].astype(MXU_DTYPE), wpt_ref[...])
        h3 = h2t + gate * pp
        n3, r3 = _rms_stats(h3)
        diff = n3 * gf_ref[...] - t_ref[...]
        loss_ref[...] += jnp.full(loss_ref.shape, 0.5 * jnp.sum(diff * diff) / D_MODEL, F32)
        dy = diff * (1.0 / D_MODEL)
        dgf_ref[...] += jnp.sum(dy * n3, axis=0, keepdims=True)
        dh3 = _rms_bwd(dy * gf_ref[...], n3, r3)
        dpp_ref[...] = (dh3 * gate).astype(MXU_DTYPE)
        dz = (dh3 * pp * gate * (1.0 - gate)).astype(MXU_DTYPE)
        dz_ref[...] = dz
        dhn = _dot_nt(dz, wg_ref[...])
        dgp_ref[...] += jnp.sum(dhn * n2, axis=0, keepdims=True)
        dh_ref[...] = dh3 + _rms_bwd(dhn * gp_ref[...], n2, r2)

    return pl.pallas_call(
        body, name="ple_loss", grid=(s // tm,),
        in_specs=[_rows(tm, D_MODEL), _rows(tm, PLE_DIM), _rows(tm, D_MODEL), _full((1, D_MODEL)),
                  _full((D_MODEL, D_MODEL)), _full((D_MODEL, PLE_DIM)), _full((1, D_MODEL))],
        out_specs=[_full((1, LANES)), _rows(tm, D_MODEL), _rows(tm, D_MODEL), _rows(tm, D_MODEL),
                   _rows(tm, D_MODEL), _full((1, D_MODEL)), _full((1, D_MODEL))],
        out_shape=[jax.ShapeDtypeStruct((1, LANES), F32), jax.ShapeDtypeStruct((s, D_MODEL), F32),
                   jax.ShapeDtypeStruct((s, D_MODEL), MXU_DTYPE), jax.ShapeDtypeStruct((s, D_MODEL), MXU_DTYPE),
                   jax.ShapeDtypeStruct((s, D_MODEL), MXU_DTYPE), jax.ShapeDtypeStruct((1, D_MODEL), F32),
                   jax.ShapeDtypeStruct((1, D_MODEL), F32)],
        compiler_params=_params(("arbitrary",)),
    )(h2, p, target, g_ple, w_pg, w_pp_t, g_final)


def _ffn_bwd(dh2, h1, gate, up, g_ffn, w_down, w_gate_t, w_up_t):
    s = h1.shape[0]
    tm = 256

    def body(dh_ref, h_ref, gate_ref, up_ref, g_ref, wd_ref, wgt_ref, wut_ref,
             o_ref, act_ref, dg_ref, du_ref, dgn_ref):
        @pl.when(pl.program_id(0) == 0)
        def _():
            dgn_ref[...] = jnp.zeros_like(dgn_ref)

        dh = dh_ref[...]
        dhb = dh.astype(MXU_DTYPE)
        strips = [dict(cols=slice(lo, hi)) for lo, hi in FF_STRIPS]

        def back_down(t):
            t['dact'] = _dot_nt(dhb, wd_ref[t['cols'], :])

        def back_act(t):
            cols, dact = t['cols'], t['dact']
            g = gate_ref[:, cols].astype(F32)
            u = up_ref[:, cols].astype(F32)
            sg = jax.nn.sigmoid(g)
            silu = g * sg
            act_ref[:, cols] = (silu * u).astype(MXU_DTYPE)
            t['du'] = (dact * silu).astype(MXU_DTYPE)
            t['dg'] = (dact * u * sg * (1.0 + g * (1.0 - sg))).astype(MXU_DTYPE)
            du_ref[:, cols] = t['du']
            dg_ref[:, cols] = t['dg']

        def back_in(t):
            return _dot(t['dg'], wgt_ref[t['cols'], :]) + _dot(t['du'], wut_ref[t['cols'], :])

        dhn = jnp.zeros((tm, D_MODEL), F32)
        back_down(strips[0])
        for i, t in enumerate(strips):
            if i + 1 < len(strips):
                back_down(strips[i + 1])
            back_act(t)
            dhn = dhn + back_in(t)
        n, r = _rms_stats(h_ref[...])
        dgn_ref[...] += jnp.sum(dhn * n, axis=0, keepdims=True)
        o_ref[...] = dh + _rms_bwd(dhn * g_ref[...], n, r)

    return pl.pallas_call(
        body, name="ffn_bwd", grid=(s // tm,),
        in_specs=[_rows(tm, D_MODEL), _rows(tm, D_MODEL), _rows(tm, D_FF), _rows(tm, D_FF), _full((1, D_MODEL)),
                  _full((D_FF, D_MODEL)), _full((D_FF, D_MODEL)), _full((D_FF, D_MODEL))],
        out_specs=[_rows(tm, D_MODEL), _rows(tm, D_FF), _rows(tm, D_FF), _rows(tm, D_FF), _full((1, D_MODEL))],
        out_shape=[jax.ShapeDtypeStruct((s, D_MODEL), F32), jax.ShapeDtypeStruct((s, D_FF), MXU_DTYPE),
                   jax.ShapeDtypeStruct((s, D_FF), MXU_DTYPE), jax.ShapeDtypeStruct((s, D_FF), MXU_DTYPE),
                   jax.ShapeDtypeStruct((1, D_MODEL), F32)],
        compiler_params=_params(("arbitrary",)),
    )(dh2, h1, gate, up, g_ffn, w_down, w_gate_t, w_up_t)


def _out_bwd(dh1, y_b, g_b, w_out):
    s = dh1.shape[0]
    tm = 512
    nc = WIDTH_B // LANES

    def body(dh_ref, *refs):
        yb_refs = refs[:nc]
        g_ref, w_ref, dya_ref, dyb_ref, dg_ref, scr = refs[nc:]

        @pl.when(pl.program_id(0) == 0)
        def _():
            dg_ref[...] = jnp.zeros_like(dg_ref)

        dy = _dot_nt(dh_ref[...].astype(MXU_DTYPE), w_ref[...])
        dya_ref[...] = dy[:, :WIDTH_A]
        dyb = dy[:, WIDTH_A:]
        n, r = _rms_stats(_load_l256(yb_refs, tm))
        dg_ref[...] += jnp.sum(dyb * n, axis=0, keepdims=True)
        dyb_in = _rms_bwd(dyb * g_ref[...], n, r)
        for j in range(nc):
            cols = slice(j * LANES, (j + 1) * LANES)
            _store_l256(scr, dyb_ref, cols, dyb_in[:, cols])

    return pl.pallas_call(
        body, name="out_bwd", grid=(s // tm,), scratch_shapes=[pltpu.VMEM((tm, LANES), F32)],
        in_specs=[_rows(tm, D_MODEL)] + _col_specs(tm, WIDTH_B) + [_full((1, WIDTH_B)), _full((D_MODEL, D_MODEL))],
        out_specs=[_rows(tm, WIDTH_A), _rows(tm, WIDTH_B), _full((1, WIDTH_B))],
        out_shape=[jax.ShapeDtypeStruct((s, WIDTH_A), F32), jax.ShapeDtypeStruct((s, WIDTH_B), F32),
                   jax.ShapeDtypeStruct((1, WIDTH_B), F32)],
        compiler_params=_params(("arbitrary",)),
    )(dh1, *([y_b] * nc), g_b, w_out)


def _attn_bwd_branch(q, k, v, do, o, lse, grads, dil):
    s = q.shape[0]
    br = _Branch(dil, s)
    qn, nb = br.qn, br.nb
    first = grads is None

    def body(*refs):
        bias_ref, q_ref, kc_ref, kp_ref, vc_ref, vp_ref, do_ref, o_ref, lse_ref = refs[:9]
        if first:
            rest = refs[9:]
        else:
            dq_in, dk_in, dv_in = refs[9:12]
            rest = refs[12:]
        dq_ref, dk_ref, dv_ref, dk_carry, dv_carry = rest
        n = pl.program_id(1)

        @pl.when(n == 0)
        def _():
            dk_carry[...] = jnp.zeros_like(dk_carry)
            dv_carry[...] = jnp.zeros_like(dv_carry)

        @pl.when(n < nb)
        def _():
            bias2 = jnp.concatenate([bias_ref[0], bias_ref[0]], axis=0)
            lane = lax.broadcasted_iota(jnp.int32, (qn, LANES), 1)
            lo = lane < HEAD_DIM
            mask_f = lo.astype(F32)
            mask_lo = mask_f.astype(MXU_DTYPE)
            def prepare(j, hp):
                cols = slice(hp * LANES, (hp + 1) * LANES)
                qp = br.load(q_ref, cols, j)
                dop = br.load(do_ref, cols, j)
                prod = dop * br.load(o_ref, cols, j)
                prod_lo = prod * mask_f
                lse = br.load(lse_ref, cols, j)
                return dict(
                    j=j, cols=cols,
                    kcat=jnp.concatenate([br.load(kp_ref, cols, j), br.load(kc_ref, cols, j)], axis=0),
                    vcat=jnp.concatenate([br.load(vp_ref, cols, j), br.load(vc_ref, cols, j)], axis=0),
                    qs=jnp.concatenate([qp * mask_lo, qp * (1 - mask_lo)], axis=0),
                    dos=jnp.concatenate([dop * mask_f, dop * (1.0 - mask_f)], axis=0).astype(MXU_DTYPE),
                    delta=jnp.concatenate([jnp.sum(prod_lo, axis=1, keepdims=True),
                                           jnp.sum(prod - prod_lo, axis=1, keepdims=True)], axis=0),
                    lse2=jnp.concatenate([lse[:, :1], lse[:, HEAD_DIM:HEAD_DIM + 1]], axis=0))

            def scores(t):
                t['sc'] = _dot_nt(t['qs'], t['kcat'])
                t['dp'] = _dot_nt(t['dos'], t['vcat'])

            def softmax(t):
                p = jnp.exp(t['sc'] + bias2 - t['lse2'])
                t['ds'] = (p * (t['dp'] - t['delta'])).astype(MXU_DTYPE)
                t['p'] = p.astype(MXU_DTYPE)

            def gradients(t):
                t['dvc'] = _dot_tn(t['p'], t['dos'])
                t['dkc'] = _dot_tn(t['ds'], t['qs'])
                t['dq2'] = _dot(t['ds'], t['kcat'])

            def store(t):
                j, cols, dkc, dvc = t['j'], t['cols'], t['dkc'], t['dvc']
                dq = jnp.where(lo, t['dq2'][:qn], t['dq2'][qn:])
                dk_prev = dk_carry[j, :, cols] + dkc[:qn]
                dv_prev = dv_carry[j, :, cols] + dvc[:qn]
                if not first:
                    dq = dq + br.load(dq_in, cols, j)
                    dk_prev = dk_prev + br.load(dk_in, cols, j)
                    dv_prev = dv_prev + br.load(dv_in, cols, j)
                br.store(dq_ref, cols, dq, j)
                br.store(dk_ref, cols, dk_prev, j)
                br.store(dv_ref, cols, dv_prev, j)
                dk_carry[j, :, cols] = dkc[qn:]
                dv_carry[j, :, cols] = dvc[qn:]

            for j, first_pair in itertools.product(range(br.res), range(0, HEADS_B // 2, PAIRS_ABREAST)):
                group = [prepare(j, hp) for hp in range(first_pair, first_pair + PAIRS_ABREAST)]
                for stage in (scores, softmax, gradients, store):
                    for t in group:
                        stage(t)

        @pl.when(n == nb)
        def _():
            for j in range(br.res):
                dk_last = dk_carry[j]
                dv_last = dv_carry[j]
                if not first:
                    dk_last = dk_last + br.load(dk_in, slice(None), j)
                    dv_last = dv_last + br.load(dv_in, slice(None), j)
                br.store(dk_ref, slice(None), dk_last, j)
                br.store(dv_ref, slice(None), dv_last, j)

    cur = lambda n: jnp.minimum(n, nb - 1)
    before = lambda n: jnp.maximum(cur(n) - 1, 0)
    late = lambda n: jnp.maximum(n - 1, 0)
    in_specs = [br.bias_spec(cur), br.spec(WIDTH_B, cur), br.spec(WIDTH_B, cur), br.spec(WIDTH_B, before),
                br.spec(WIDTH_B, cur), br.spec(WIDTH_B, before), br.spec(WIDTH_B, cur), br.spec(WIDTH_B, cur),
                br.spec(WIDTH_B, cur)]
    args = [jnp.asarray(br.bias)] + [br.view(a) for a in (q, k, k, v, v, do, o, lse)]
    if not first:
        in_specs += [br.spec(WIDTH_B, cur), br.spec(WIDTH_B, late), br.spec(WIDTH_B, late)]
        args += [br.view(g) for g in grads]
    res = pl.pallas_call(
        body, name="attn_bwd_d%d" % dil, grid=(br.grid[0], nb + 1), in_specs=in_specs,
        out_specs=[br.spec(WIDTH_B, cur), br.spec(WIDTH_B, late), br.spec(WIDTH_B, late)],
        out_shape=[jax.ShapeDtypeStruct((s // L_BLOCK, 4, 4, L_GROUP, WIDTH_B), F32)] * 3,
        scratch_shapes=[pltpu.VMEM((br.res, qn, WIDTH_B), F32), pltpu.VMEM((br.res, qn, WIDTH_B), F32)],
        compiler_params=_params(("arbitrary", "arbitrary")),
    )(*args)
    return tuple(a.reshape(s, WIDTH_B) for a in res)


def _sgu_bwd(uv, dya_n, w_tril, w_tril_t, bias, g_sgu, g_a):
    s = uv.shape[0]
    tm = 512

    def body(uv_ref, dy_ref, w_ref, wt_ref, b_ref, gs_ref, ga_ref, duv_ref, dw_ref, db_ref, dgs_ref, dga_ref,
             db_acc):
        i = pl.program_id(0)

        @pl.when(i == 0)
        def _():
            dw_ref[...] = jnp.zeros_like(dw_ref)
            dgs_ref[...] = jnp.zeros_like(dgs_ref)
            dga_ref[...] = jnp.zeros_like(dga_ref)
            db_acc[...] = jnp.zeros_like(db_acc)

        t = _sgu_forward_tile(uv_ref[...], w_ref, b_ref[...], gs_ref[...])
        na, ra = _rms_stats(t['ya'])
        dyn = dy_ref[...]
        dga_ref[...] += jnp.sum(dyn * na, axis=0, keepdims=True)
        dya = _rms_bwd(dyn * ga_ref[...], na, ra)
        dug = dya * t['mixed']
        dmixed = dya * t['ug']
        dmb = dmixed.astype(MXU_DTYPE)
        masks = _half_masks(MXU_DTYPE)
        chunks = []
        db = jnp.zeros((CHUNK, WIDTH_A), F32)
        for c in range(tm // CHUNK):
            rows = slice(c * CHUNK, (c + 1) * CHUNK)
            db = db + dmixed[rows]
            groups = []
            for gp in range(2):
                cols = slice(gp * LANES, (gp + 1) * LANES)
                dm_g = dmb[rows, cols]
                vn_g = t['vn'][rows, cols]
                dvn_g = jnp.zeros((CHUNK, LANES), F32)
                for j in range(2):
                    dm_h = dm_g * masks[j]
                    dvn_g = dvn_g + _dot(wt_ref[2 * gp + j], dm_h)
                    dw_ref[2 * gp + j] += _dot_nt(dm_h, vn_g)
                groups.append(dvn_g)
            chunks.append(jnp.concatenate(groups, axis=1))
        db_acc[...] += db
        dvn = jnp.concatenate(chunks, axis=0)
        vhat = t['vhat']
        dgs_ref[...] += jnp.sum(dvn * vhat, axis=0, keepdims=True)
        dvh = dvn * gs_ref[...]
        dvg = t['rs'] * (dvh - jnp.mean(dvh, axis=-1, keepdims=True)
                         - vhat * jnp.mean(dvh * vhat, axis=-1, keepdims=True))
        duv_ref[:, :WIDTH_A] = (dug * _gelu_grad(t['u'], t['tu'])).astype(MXU_DTYPE)
        duv_ref[:, WIDTH_A:] = (dvg * _gelu_grad(t['v'], t['tv'])).astype(MXU_DTYPE)

        @pl.when(i == pl.num_programs(0) - 1)
        def _():
            lane_a = lax.broadcasted_iota(jnp.int32, (CHUNK, WIDTH_A), 1)
            lane = lax.broadcasted_iota(jnp.int32, (CHUNK, LANES), 1)
            acc = db_acc[...]
            out = jnp.zeros((CHUNK, LANES), F32)
            for h in range(HEADS_A):
                col = jnp.sum(jnp.where(lane_a // HEAD_DIM == h, acc, 0.0), axis=1, keepdims=True)
                out = jnp.where(lane == h, col, out)
            db_ref[...] = out
            causal = (lax.broadcasted_iota(jnp.int32, (CHUNK, CHUNK), 0)
                      >= lax.broadcasted_iota(jnp.int32, (CHUNK, CHUNK), 1))
            for h in range(HEADS_A):
                dw_ref[h] = jnp.where(causal, dw_ref[h], 0.0)

    return pl.pallas_call(
        body, name="sgu_bwd", grid=(s // tm,),
        in_specs=[_rows(tm, 2 * WIDTH_A), _rows(tm, WIDTH_A), _full((HEADS_A, CHUNK, CHUNK)),
                  _full((HEADS_A, CHUNK, CHUNK)), _full((CHUNK, WIDTH_A)), _full((1, WIDTH_A)),
                  _full((1, WIDTH_A))],
        out_specs=[_rows(tm, 2 * WIDTH_A), _full((HEADS_A, CHUNK, CHUNK)), _full((CHUNK, LANES)),
                   _full((1, WIDTH_A)), _full((1, WIDTH_A))],
        out_shape=[jax.ShapeDtypeStruct((s, 2 * WIDTH_A), MXU_DTYPE),
                   jax.ShapeDtypeStruct((HEADS_A, CHUNK, CHUNK), F32), jax.ShapeDtypeStruct((CHUNK, LANES), F32),
                   jax.ShapeDtypeStruct((1, WIDTH_A), F32), jax.ShapeDtypeStruct((1, WIDTH_A), F32)],
        scratch_shapes=[pltpu.VMEM((CHUNK, WIDTH_A), F32)],
        compiler_params=_params(("arbitrary",)),
    )(uv, dya_n, w_tril, w_tril_t, bias, g_sgu, g_a)


def _in_bwd_proj(duv, dq, dk, dv, cos_t, sin_t):
    s = duv.shape[0]
    tm = 512
    nc = WIDTH_B // LANES

    def body(duv_ref, *refs):
        dq_refs, dk_refs, dv_refs = refs[:nc], refs[nc:2 * nc], refs[2 * nc:3 * nc]
        cos_ref, sin_ref, dp_ref = refs[3 * nc:]
        cos = cos_ref[...]
        sin = sin_ref[...]
        dp_ref[:, :2 * WIDTH_A] = duv_ref[...]
        for i in range(nc):
            lo = 2 * WIDTH_A + i * LANES
            tq = _load_l256(dq_refs[i:i + 1], tm) * (HEAD_DIM ** -0.5)
            tk = _load_l256(dk_refs[i:i + 1], tm)
            dp_ref[:, lo:lo + LANES] = (tq * cos + _rope_partner(tq * sin)).astype(MXU_DTYPE)
            dp_ref[:, lo + WIDTH_B:lo + WIDTH_B + LANES] = (tk * cos + _rope_partner(tk * sin)).astype(MXU_DTYPE)
            dp_ref[:, lo + 2 * WIDTH_B:lo + 2 * WIDTH_B + LANES] = _load_l256(dv_refs[i:i + 1], tm).astype(MXU_DTYPE)

    return pl.pallas_call(
        body, name="in_bwd_proj", grid=(s // tm,),
        in_specs=[_rows(tm, 2 * WIDTH_A)] + 3 * _col_specs(tm, WIDTH_B) + [_rows(tm, LANES), _rows(tm, LANES)],
        out_specs=_rows(tm, IN_COLS), out_shape=jax.ShapeDtypeStruct((s, IN_COLS), MXU_DTYPE),
        compiler_params=_params(("arbitrary",)),
    )(duv, *([dq] * nc), *([dk] * nc), *([dv] * nc), cos_t, sin_t)


def _in_bwd_x(dproj, w_in_t, x, g_mix, dh1):
    s = x.shape[0]
    tm = 512

    def body(dp_ref, wt_ref, x_ref, g_ref, dh_ref, gx_ref, dg_ref):
        @pl.when(pl.program_id(0) == 0)
        def _():
            dg_ref[...] = jnp.zeros_like(dg_ref)

        dhn = _dot(dp_ref[...], wt_ref[...])
        n, r = _rms_stats(x_ref[...])
        dg_ref[...] += jnp.sum(dhn * n, axis=0, keepdims=True)
        gx_ref[...] = dh_ref[...] + _rms_bwd(dhn * g_ref[...], n, r)

    return pl.pallas_call(
        body, name="in_bwd_x", grid=(s // tm,),
        in_specs=[_rows(tm, IN_COLS), _full((IN_COLS, D_MODEL)), _rows(tm, D_MODEL), _full((1, D_MODEL)),
                  _rows(tm, D_MODEL)],
        out_specs=[_rows(tm, D_MODEL), _full((1, D_MODEL))],
        out_shape=[jax.ShapeDtypeStruct((s, D_MODEL), F32), jax.ShapeDtypeStruct((1, D_MODEL), F32)],
        compiler_params=_params(("arbitrary",)),
    )(dproj, w_in_t, x, g_mix, dh1)


def _wgrad(a, b, name):
    s, m = a.shape
    n = b.shape[1]
    bm = min(m, FF_HALF)
    ts = 1024
    nsteps = s // ts

    def body(a_ref, b_ref, o_ref, acc):
        kk = pl.program_id(1)

        @pl.when(kk == 0)
        def _():
            acc[...] = jnp.zeros_like(acc)

        acc[...] += _dot_tn(a_ref[...].astype(MXU_DTYPE), b_ref[...].astype(MXU_DTYPE))

        @pl.when(kk == nsteps - 1)
        def _():
            o_ref[...] = acc[...].astype(o_ref.dtype)

    return pl.pallas_call(
        body, name=name, grid=(m // bm, nsteps),
        in_specs=[pl.BlockSpec((ts, bm), lambda i, kk: (kk, i)), pl.BlockSpec((ts, n), lambda i, kk: (kk, 0))],
        out_specs=pl.BlockSpec((bm, n), lambda i, kk: (i, 0)), out_shape=jax.ShapeDtypeStruct((m, n), jnp.bfloat16),
        scratch_shapes=[pltpu.VMEM((bm, n), F32)],
        compiler_params=_params(("arbitrary", "arbitrary")),
    )(a, b)


def _rope_tables(s):
    half = HEAD_DIM // 2
    inv = ROPE_THETA ** (-jnp.arange(half, dtype=F32) / half)
    ang = jnp.arange(s, dtype=F32)[:, None] * jnp.tile(inv, LANES // half)[None, :]
    sign = jnp.tile(jnp.concatenate([-jnp.ones(half, F32), jnp.ones(half, F32)]), LANES // HEAD_DIM)
    return jnp.cos(ang), jnp.sin(ang) * sign[None, :]


MESH = pl.DeviceIdType.MESH
ANY = pl.BlockSpec(memory_space=pl.ANY)
SEM = pl.BlockSpec(memory_space=pltpu.SEMAPHORE)
SPLIT_COPY = pltpu.CompilerParams(has_side_effects=pltpu.SideEffectType.DATAFLOW_SIDE_EFFECTING)
SLAB_IS_TRANSPOSED = {'w_in': True, 'w_out': False, 'w_gate': True, 'w_up': True, 'w_down': False,
                      'w_ple_gate': False, 'w_ple_proj': True}


def _place():
    x, y, c = lax.axis_index("x"), lax.axis_index("y"), lax.axis_index("c")
    other_chips = [(1 - x, y), (x, 1 - y), (1 - x, 1 - y)]
    return x, y, c, other_chips


def _chip_of(chip):
    return 2 * chip[0] + chip[1]


def _half(ref, lead, hc):
    hr = ref.shape[1] // 2
    return ref.at[lead, pl.ds(hc * hr, hr), :]


def _stack_with_own(slab):
    me = 2 * lax.axis_index("x") + lax.axis_index("y")
    stack = lax.empty((N_CHIPS,) + slab.shape, slab.dtype)
    return lax.dynamic_update_slice(stack, slab[None], (me, 0, 0))


def _first_hops(land_ref, send_sems, recv_sems):
    x, y, c, chips = _place()
    mine = _half(land_ref, 2 * x + y, c)
    sends = [pltpu.make_async_remote_copy(src_ref=mine, dst_ref=mine, send_sem=send_sems.at[j], recv_sem=recv_sems.at[j],
                                          device_id=(*chip, c), device_id_type=MESH) for j, chip in enumerate(chips)]
    recvs = [pltpu.make_async_remote_copy(src_ref=mine, dst_ref=_half(land_ref, _chip_of(chip), c),
                                          send_sem=send_sems.at[j], recv_sem=recv_sems.at[j], device_id=(*chip, c),
                                          device_id_type=MESH) for j, chip in enumerate(chips)]
    return sends, recvs


def _second_hops(land_ref, send_sems, recv_sems):
    x, y, c, chips = _place()
    sibling = (x, y, 1 - c)
    sends = [pltpu.make_async_remote_copy(src_ref=_half(land_ref, _chip_of(chip), c),
                                          dst_ref=_half(land_ref, _chip_of(chip), c), send_sem=send_sems.at[j],
                                          recv_sem=recv_sems.at[j], device_id=sibling, device_id_type=MESH)
             for j, chip in enumerate(chips)]
    recvs = [pltpu.make_async_remote_copy(src_ref=_half(land_ref, _chip_of(chip), c),
                                          dst_ref=_half(land_ref, _chip_of(chip), 1 - c), send_sem=send_sems.at[j],
                                          recv_sem=recv_sems.at[j], device_id=sibling, device_id_type=MESH)
             for j, chip in enumerate(chips)]
    return sends, recvs


def _two_level_start(stack):
    def body(land_ref, send_sems, recv_sems, land_thru, token):
        for cp in _first_hops(land_ref, send_sems, recv_sems)[0]:
            cp.start()
        token[...] = jnp.zeros_like(token)

    res = pl.pallas_call(
        body, name="two_level_start",
        out_shape=(pltpu.SemaphoreType.DMA((3,)), pltpu.SemaphoreType.DMA((3,)), pltpu.HBM(stack.shape, stack.dtype),
                   jax.ShapeDtypeStruct((8, LANES), F32)),
        in_specs=[ANY], out_specs=(SEM, SEM, ANY, pl.BlockSpec(memory_space=pltpu.VMEM)),
        input_output_aliases={0: 2}, compiler_params=SPLIT_COPY,
    )(pltpu.with_memory_space_constraint(stack, pltpu.HBM))
    return res[:-1], res[-1]


def _two_level_forward(handle, after):
    send_sems, recv_sems, land = handle

    def body(land_ref, send_sems, recv_sems, *refs):
        send2, recv2 = refs[len(after):len(after) + 2]
        sends, recvs = _first_hops(land_ref, send_sems, recv_sems)
        for cp in sends:
            cp.wait_send()
        for cp in recvs:
            cp.wait_recv()
        for cp in _second_hops(land_ref, send2, recv2)[0]:
            cp.start()

    return pl.pallas_call(
        body, name="two_level_forward",
        out_shape=(pltpu.SemaphoreType.DMA((3,)), pltpu.SemaphoreType.DMA((3,)), pltpu.HBM(land.shape, land.dtype)),
        in_specs=[ANY, SEM, SEM] + [ANY] * len(after), out_specs=(SEM, SEM, ANY),
        input_output_aliases={0: 2}, compiler_params=SPLIT_COPY,
    )(land, send_sems, recv_sems, *after)


def _two_level_wait(handle):
    send_sems, recv_sems, land = handle

    def body(land_ref, send_sems, recv_sems, land_thru):
        sends, recvs = _second_hops(land_ref, send_sems, recv_sems)
        for cp in sends:
            cp.wait_send()
        for cp in recvs:
            cp.wait_recv()

    land = pl.pallas_call(
        body, name="two_level_wait", out_shape=pltpu.HBM(land.shape, land.dtype), in_specs=[ANY, SEM, SEM],
        out_specs=ANY, input_output_aliases={0: 0}, compiler_params=SPLIT_COPY,
    )(land, send_sems, recv_sems)
    return land.reshape(-1, land.shape[-1])


def _gather_copies(land_refs, send_sems, recv_sems):
    x, y, c, chips = _place()
    sends, recvs = [], []
    for k, land in enumerate(land_refs):
        mine = _half(land, 2 * x + y, c)
        for j, chip in enumerate(chips):
            for t in range(2):
                sends.append(pltpu.make_async_remote_copy(
                    src_ref=mine, dst_ref=mine, send_sem=send_sems.at[6 * k + 2 * j + t],
                    recv_sem=recv_sems.at[6 * k + 2 * j + c], device_id=(*chip, t), device_id_type=MESH))
                recvs.append(pltpu.make_async_remote_copy(
                    src_ref=mine, dst_ref=_half(land, _chip_of(chip), t), send_sem=send_sems.at[6 * k + 2 * j + t],
                    recv_sem=recv_sems.at[6 * k + 2 * j + t], device_id=(*chip, t), device_id_type=MESH))
    return sends, recvs


def _all_gather_start(stacks, after, carry):
    n = len(stacks)

    def body(*refs):
        carry_ref = refs[n + 1]
        send_sems, recv_sems = refs[n + 2:n + 4]
        sends, _ = _gather_copies(refs[:n], send_sems, recv_sems)
        for cp in sends:
            cp.start()
        refs[-1][...] = carry_ref[...]

    hbm = lambda a: pltpu.HBM(a.shape, a.dtype)
    vmem = pl.BlockSpec(memory_space=pltpu.VMEM)
    res = pl.pallas_call(
        body, name="all_gather_start",
        out_shape=(pltpu.SemaphoreType.DMA((6 * n,)), pltpu.SemaphoreType.DMA((6 * n,)), *map(hbm, stacks),
                   jax.ShapeDtypeStruct(carry.shape, carry.dtype)),
        in_specs=[ANY] * (n + 1) + [vmem], out_specs=(SEM, SEM, *([ANY] * n), vmem),
        input_output_aliases={i: 2 + i for i in range(n)}, compiler_params=SPLIT_COPY,
    )(*[pltpu.with_memory_space_constraint(a, pltpu.HBM) for a in stacks], after, carry)
    return res[:-1], res[-1]


def _all_gather_wait(handle, after):
    send_sems, recv_sems = handle[:2]
    lands = handle[2:]
    n = len(lands)

    def body(*refs):
        send_sems, recv_sems = refs[n:n + 2]
        sends, recvs = _gather_copies(refs[:n], send_sems, recv_sems)
        for cp in sends:
            cp.wait_send()
        for cp in recvs:
            cp.wait_recv()

    hbm = lambda a: pltpu.HBM(a.shape, a.dtype)
    res = pl.pallas_call(
        body, name="all_gather_wait", out_shape=tuple(map(hbm, lands)),
        in_specs=[ANY] * n + [SEM, SEM, ANY], out_specs=tuple([ANY] * n),
        input_output_aliases={i: i for i in range(n)}, compiler_params=SPLIT_COPY,
    )(*lands, send_sems, recv_sems, after)
    return [land.reshape(-1, land.shape[-1]) for land in res]


def _scatter_copies(part_refs, land_refs, send_sems, recv_sems):
    x, y, c, chips = _place()
    me = 4 * x + 2 * y + c
    sends, recvs = [], []
    for k, (part, land) in enumerate(zip(part_refs, land_refs)):
        for j, chip in enumerate(chips):
            for t in range(2):
                sends.append(pltpu.make_async_remote_copy(
                    src_ref=part.at[_chip_of(chip)], dst_ref=land.at[me],
                    send_sem=send_sems.at[7 * k + 2 * j + t], recv_sem=recv_sems.at[7 * k + 2 * j + c],
                    device_id=(*chip, t), device_id_type=MESH))
                recvs.append(pltpu.make_async_remote_copy(
                    src_ref=part.at[_chip_of(chip)], dst_ref=land.at[2 * _chip_of(chip) + t],
                    send_sem=send_sems.at[7 * k + 2 * j + t], recv_sem=recv_sems.at[7 * k + 2 * j + t],
                    device_id=(*chip, t), device_id_type=MESH))
        sends.append(pltpu.make_async_remote_copy(
            src_ref=part.at[2 * x + y], dst_ref=land.at[me], send_sem=send_sems.at[7 * k + 6],
            recv_sem=recv_sems.at[7 * k + 6], device_id=(x, y, 1 - c), device_id_type=MESH))
        recvs.append(pltpu.make_async_remote_copy(
            src_ref=part.at[2 * x + y], dst_ref=land.at[4 * x + 2 * y + 1 - c],
            send_sem=send_sems.at[7 * k + 6], recv_sem=recv_sems.at[7 * k + 6], device_id=(x, y, 1 - c),
            device_id_type=MESH))
    return sends, recvs


def _reduce_scatter_start(parts, name, carry):
    n = len(parts)
    parts = [p.reshape(N_CHIPS, p.shape[0] // N_CHIPS, p.shape[1]) for p in parts]

    def body(*refs):
        part_refs, land_refs, carry_ref = refs[:n], refs[n:2 * n], refs[2 * n]
        send_sems, recv_sems = refs[2 * n + 1:2 * n + 3]
        sends, _ = _scatter_copies(part_refs, land_refs, send_sems, recv_sems)
        for cp in sends:
            cp.start()
        refs[-1][...] = carry_ref[...]

    lands = [lax.empty((N_DEV, p.shape[1], p.shape[2]), p.dtype) for p in parts]
    hbm = lambda a: pltpu.HBM(a.shape, a.dtype)
    vmem = pl.BlockSpec(memory_space=pltpu.VMEM)
    res = pl.pallas_call(
        body, name=name,
        out_shape=(pltpu.SemaphoreType.DMA((7 * n,)), pltpu.SemaphoreType.DMA((7 * n,)), *map(hbm, parts),
                   *map(hbm, lands), jax.ShapeDtypeStruct(carry.shape, carry.dtype)),
        in_specs=[ANY] * (2 * n) + [vmem], out_specs=(SEM, SEM, *([ANY] * (2 * n)), vmem),
        input_output_aliases={i: 2 + i for i in range(2 * n)}, compiler_params=SPLIT_COPY,
    )(*[pltpu.with_memory_space_constraint(a, pltpu.HBM) for a in parts + lands], carry)
    return res[:-1], res[-1]


def _reduce_scatter_wait(handle, after, name):
    after = list(after) if isinstance(after, (list, tuple)) else [after]
    send_sems, recv_sems = handle[:2]
    n = (len(handle) - 2) // 2
    parts, lands = handle[2:2 + n], handle[2 + n:]

    def body(*refs):
        part_refs, land_refs = refs[:n], refs[n:2 * n]
        send_sems, recv_sems = refs[2 * n:2 * n + 2]
        sends, recvs = _scatter_copies(part_refs, land_refs, send_sems, recv_sems)
        for cp in sends:
            cp.wait_send()
        for cp in recvs:
            cp.wait_recv()

    hbm = lambda a: pltpu.HBM(a.shape, a.dtype)
    res = pl.pallas_call(
        body, name=name, out_shape=tuple(map(hbm, list(parts) + list(lands))),
        in_specs=[ANY] * (2 * n) + [SEM, SEM] + [ANY] * len(after), out_specs=tuple([ANY] * (2 * n)),
        input_output_aliases={i: i for i in range(2 * n)}, compiler_params=SPLIT_COPY,
    )(*parts, *lands, send_sems, recv_sems, *after)
    return list(zip(res[:n], res[n:]))


def _adamw_of_shares(w, own, land, m, v, name, shares_transposed=False):
    rows, cols = w.shape
    tm = rows // 4 if rows % 32 == 0 and not shares_transposed else rows
    share_tile = (cols, rows) if shares_transposed else (tm, cols)
    x, y, c = lax.axis_index("x"), lax.axis_index("y"), lax.axis_index("c")
    where = jnp.stack([2 * x + y, 4 * x + 2 * y + c]).astype(jnp.int32)

    def body(where_ref, w_ref, own_ref, land_ref, m_ref, v_ref, g_ref, d_ref, nm_ref, nv_ref):
        me = where_ref[1]
        g_ = jnp.zeros(share_tile, F32)
        for dev in range(N_DEV):
            g_ = g_ + jnp.where(me == dev, own_ref[0], land_ref[dev]).astype(F32)
        if shares_transposed:
            g_ = g_.T
        m_ = ADAM_B1 * m_ref[...] + (1.0 - ADAM_B1) * g_
        v_ = ADAM_B2 * v_ref[...] + (1.0 - ADAM_B2) * (g_ * g_)
        m_hat = m_ / (1.0 - ADAM_B1 ** ADAM_STEP)
        v_hat = v_ / (1.0 - ADAM_B2 ** ADAM_STEP)
        g_ref[...] = g_
        d_ref[...] = -ADAM_LR * (m_hat / (jnp.sqrt(v_hat) + ADAM_EPS) + ADAM_WD * w_ref[...])
        nm_ref[...] = m_
        nv_ref[...] = v_

    tile = pl.BlockSpec((tm, cols), lambda i, where_ref: (i, 0))
    spec = pltpu.PrefetchScalarGridSpec(
        num_scalar_prefetch=1, grid=(rows // tm,),
        in_specs=[tile, pl.BlockSpec((1,) + share_tile, lambda i, where_ref: (where_ref[0], i, 0)),
                  pl.BlockSpec((N_DEV,) + share_tile, lambda i, where_ref: (0, i, 0)), tile, tile],
        out_specs=[tile] * 4)
    return pl.pallas_call(
        body, name=name, grid_spec=spec, out_shape=[jax.ShapeDtypeStruct(w.shape, F32)] * 4,
        compiler_params=_params(("arbitrary",)),
    )(where, w, own, land, m, v)


def _pack_small(values):
    flat = jnp.concatenate([values[n].reshape(-1).astype(F32) for n in SMALL])
    return jnp.pad(flat, (0, SMALL_ROWS * D_MODEL - flat.shape[0])).reshape(SMALL_ROWS, D_MODEL)


def _unpack_small(block, shapes):
    flat = block.reshape(-1)
    out, lo = {}, 0
    for n in SMALL:
        out[n] = flat[lo:lo + SMALL_SIZES[n]].reshape(shapes[n])
        lo += SMALL_SIZES[n]
    return out


def _after(token, a):
    return a + token[:1, :1].astype(a.dtype)


def kernel(x, p, mix_norm_g, w_in, sgu_w, sgu_b, sgu_norm_g, out_norm_a, out_norm_b, w_out, ffn_norm_g, w_gate, w_up, w_down, ple_norm_g, w_ple_gate, w_ple_proj, final_norm_g, loss_target, m_mix_norm_g, m_w_in, m_sgu_w, m_sgu_b, m_sgu_norm_g, m_out_norm_a, m_out_norm_b, m_w_out, m_ffn_norm_g, m_w_gate, m_w_up, m_w_down, m_ple_norm_g, m_w_ple_gate, m_w_ple_proj, m_final_norm_g, v_mix_norm_g, v_w_in, v_sgu_w, v_sgu_b, v_sgu_norm_g, v_out_norm_a, v_out_norm_b, v_w_out, v_ffn_norm_g, v_w_gate, v_w_up, v_w_down, v_ple_norm_g, v_w_ple_gate, v_w_ple_proj, v_final_norm_g):
    given = dict(locals())
    drop_lead = lambda a, lead: a.reshape(a.shape[lead:])
    xs, ps, target = drop_lead(x, 1), drop_lead(p, 2), drop_lead(loss_target, 1)
    s = xs.shape[0]
    shard = lambda name: drop_lead(given[name], 1)

    def slab_of(name):
        local = shard(name).astype(MXU_DTYPE)
        return local.T if SLAB_IS_TRANSPOSED[name] else local

    w_in_gather, token = _two_level_start(_stack_with_own(slab_of('w_in')))
    later = ['w_out', 'w_gate', 'w_up', 'w_down', 'w_ple_gate', 'w_ple_proj']
    later_stacks = [_stack_with_own(slab_of(n)) for n in later]
    cos_t, sin_t = (_after(token, table) for table in _rope_tables(s))
    tril = jnp.tril(jnp.ones((CHUNK, CHUNK), F32))
    w_tril = (sgu_w.reshape(HEADS_A, CHUNK, CHUNK) * tril).astype(MXU_DTYPE)
    w_tril_t = jnp.swapaxes(w_tril, 1, 2)
    w_in_gather = _two_level_forward(w_in_gather, [cos_t, sin_t, w_tril, w_tril_t] + later_stacks)
    w_in_t = _two_level_wait(w_in_gather)
    g = {n: given[n].reshape(1, -1) for n in SMALL if n not in ('sgu_w', 'sgu_b')}
    gather, g_mix = _all_gather_start(later_stacks, w_in_t, g['mix_norm_g'])
    bias = jnp.repeat(sgu_b.reshape(HEADS_A, CHUNK).T, HEAD_DIM, axis=1)

    uv, q, k, v, q1, k1, v1, hn1 = _in_fwd(xs, g_mix, w_in_t, cos_t, sin_t)
    ya_n = _sgu_fwd(uv, w_tril, bias, g['sgu_norm_g'], g['out_norm_a'])
    branches = [_attn_fwd_local(q1, k1, v1)] + [_attn_fwd_branch(q, k, v, dil) for dil in DILATIONS[:-1]]
    y_b, lse = _attn_fwd_branch(q, k, v, DILATIONS[-1], earlier=branches)
    stacks = dict(zip(later, _all_gather_wait(gather, lse)))
    w_gate_t, w_up_t, w_pp_t = stacks['w_gate'], stacks['w_up'], stacks['w_ple_proj']
    h1, y_n = _out_fwd(ya_n, y_b, g['out_norm_b'], stacks['w_out'], xs)
    h2, gate, up, hn2 = _ffn_fwd(h1, g['ffn_norm_g'], w_gate_t, w_up_t, stacks['w_down'])
    loss, dh2, dz, dpp, hn3, d_ple_g, d_final_g = _ple_loss(
        h2, ps, target, g['ple_norm_g'], stacks['w_ple_gate'], w_pp_t, g['final_norm_g'])

    share = {}
    share['w_ple_gate'] = _wgrad(hn3, dz, "wgrad_ple_gate")
    share['w_ple_proj'] = _wgrad(dpp, ps, "wgrad_ple_proj")
    dh1, act, dgate, dup, d_ffn_g = _ffn_bwd(dh2, h1, gate, up, g['ffn_norm_g'], stacks['w_down'], w_gate_t, w_up_t)
    share['w_down'] = _wgrad(act, dh2, "wgrad_down")
    share['w_gate'] = _wgrad(dgate, hn2, "wgrad_gate")
    share['w_up'] = _wgrad(dup, hn2, "wgrad_up")
    scatter_2, g_out_b = _reduce_scatter_start([share[n] for n in ('w_ple_gate', 'w_ple_proj', 'w_down', 'w_gate', 'w_up')],
                                               "reduce_scatter_start_2", g['out_norm_b'])
    dya_n, dyb, d_out_b = _out_bwd(dh1, y_b, g_out_b, stacks['w_out'])
    share['w_out'] = _wgrad(y_n, dh1, "wgrad_out")
    scatter_3, g_sgu = _reduce_scatter_start([share['w_out']], "reduce_scatter_start_3", g['sgu_norm_g'])
    grads = _attn_bwd_local(q1, k1, v1, dyb, y_b, lse)
    for dil in DILATIONS:
        grads = _attn_bwd_branch(q, k, v, dyb, y_b, lse, grads, dil)
    duv, d_sgu_w, d_sgu_b, d_sgu_g, d_out_a = _sgu_bwd(uv, dya_n, w_tril, w_tril_t, bias, g_sgu, g['out_norm_a'])
    dproj = _in_bwd_proj(duv, grads[0], grads[1], grads[2], cos_t, sin_t)
    share['w_in'] = _wgrad(dproj, hn1, "wgrad_in")
    scatter_4, g_mix = _reduce_scatter_start([share['w_in']], "reduce_scatter_start_4", g['mix_norm_g'])

    grads, deltas, new_m, new_v = {}, {}, {}, {}
    add_lead = lambda a: a.reshape((1,) + a.shape)

    done = {}

    def finish(names, handles, after, tag):
        landed = []
        for i, handle in enumerate(handles):
            landed += _reduce_scatter_wait(handle, after, "reduce_scatter_wait_%s%d" % (tag, i))
        for n, (own, land) in zip(names, landed):
            turn_shares = SLAB_IS_TRANSPOSED[n] and shard(n).shape[-1] % LANES == 0
            turn = (lambda a: a.T) if SLAB_IS_TRANSPOSED[n] and not turn_shares else (lambda a: a)
            res = _adamw_of_shares(turn(shard(n)), own, land, turn(shard("m_" + n)), turn(shard("v_" + n)),
                                   "adamw_" + n, turn_shares)
            grads[n], deltas[n], new_m[n], new_v[n] = (add_lead(turn(a)) for a in res)
            done[n] = res[0]

    grad_x, d_mix_g = _in_bwd_x(dproj, w_in_t, xs, g_mix, dh1)

    gs = {'mix_norm_g': d_mix_g, 'sgu_w': d_sgu_w, 'sgu_b': d_sgu_b[:, :HEADS_A].T, 'sgu_norm_g': d_sgu_g,
          'out_norm_a': d_out_a, 'out_norm_b': d_out_b, 'ffn_norm_g': d_ffn_g, 'ple_norm_g': d_ple_g,
          'final_norm_g': d_final_g}
    gs_block = _pack_small(gs).at[SMALL_ROWS - 1, 0].set(loss[0, 0])
    to_all = jnp.broadcast_to(gs_block[None], (N_CHIPS,) + gs_block.shape).reshape(-1, D_MODEL)
    scatter_small, token = _reduce_scatter_start([to_all], "small_all_reduce_start", g['mix_norm_g'])

    finish(['w_ple_gate', 'w_ple_proj', 'w_down', 'w_gate', 'w_up', 'w_out'], [scatter_2, scatter_3], token,
           "early")
    finish(['w_in'], [scatter_4], [done[n] for n in ('w_down', 'w_gate', 'w_up', 'w_out')], "last")
    (own, land), = _reduce_scatter_wait(scatter_small, done['w_in'], "small_all_reduce_wait")
    small = {n: given[n] for n in SMALL}
    small_res = _adamw_of_shares(_pack_small(small), own, land, _pack_small({n: given["m_" + n] for n in SMALL}),
                                 _pack_small({n: given["v_" + n] for n in SMALL}), "adamw_small")
    loss_out = small_res[0][SMALL_ROWS - 1, 0]
    small_shapes = {n: given[n].shape for n in SMALL}
    for res, blk in zip((grads, deltas, new_m, new_v), small_res):
        res.update(_unpack_small(blk, small_shapes))

    outs = [loss_out, add_lead(grad_x)]
    for res in (grads, deltas, new_m, new_v):
        outs += [res[n] for n in WEIGHT_NAMES]
    return tuple(outs)
```
